```python
import jax, jax.numpy as jnp
from jax import lax
import numpy as np

D_MODEL = 1024
BATCH = 8
SEQ = 2048
DEPTH = 1

HEAD_DIM = 64
FOX_HEADS = 8
DIL_GROUPS = ((128, 1), (512, 4), (2048, 16))
DIL_HEADS_PER_GROUP = 4
DIL_HEADS = DIL_HEADS_PER_GROUP * len(DIL_GROUPS)
D_FF = 4 * D_MODEL
ROPE_THETA = 500000.0
ROPE_DIM = HEAD_DIM // 4
Q_BLOCK = 128
EPS = 1e-6
NEG_INF = -1e30
FOX_W = FOX_HEADS * HEAD_DIM
DIL_W = DIL_HEADS * HEAD_DIM
DIL_OUT_W = DIL_HEADS_PER_GROUP * HEAD_DIM
IN_SPLIT_SIZES = (FOX_W, FOX_W, FOX_W, FOX_HEADS, DIL_W, DIL_W, DIL_W, D_MODEL, D_MODEL)
D_IN = sum(IN_SPLIT_SIZES)

kernel_name = 'hybrid_fox_dilated_gated_block'


def _rms_norm(x, g):
    xf = x.astype(jnp.float32)
    y = xf * lax.rsqrt(jnp.mean(xf * xf, axis=-1, keepdims=True) + EPS)
    return (y * g.astype(jnp.float32)).astype(x.dtype)


def _partial_rope(x, positions):
    half = ROPE_DIM // 2
    inv_freq = jnp.power(jnp.float32(ROPE_THETA), -jnp.arange(half, dtype=jnp.float32) * 2.0 / ROPE_DIM)
    ang = positions.astype(jnp.float32)[:, None] * inv_freq[None, :]
    cos = jnp.cos(ang)[None, :, None, :]
    sin = jnp.sin(ang)[None, :, None, :]
    xf = x.astype(jnp.float32)
    x1 = xf[..., :half]
    x2 = xf[..., half:ROPE_DIM]
    out = jnp.concatenate([x1 * cos - x2 * sin, x2 * cos + x1 * sin, xf[..., ROPE_DIM:]], axis=-1)
    return out.astype(x.dtype)


def _fox_attention(q, k, v, log_f):
    B, S, H, D = q.shape
    nb = S // Q_BLOCK
    scale = D ** -0.5
    F = jnp.cumsum(log_f, axis=1)
    F_k = jnp.transpose(F, (0, 2, 1))[:, :, None, :]
    k_pos = jnp.arange(S)
    q_blocks = jnp.swapaxes(q.reshape(B, nb, Q_BLOCK, H, D), 0, 1)
    F_blocks = jnp.swapaxes(F.reshape(B, nb, Q_BLOCK, H), 0, 1)

    def one_block(args):
        q_blk, F_blk, i = args
        s = jnp.einsum('bqhd,bkhd->bhqk', q_blk, k, preferred_element_type=jnp.float32) * scale
        s = s + jnp.transpose(F_blk, (0, 2, 1))[..., None] - F_k
        q_pos = i * Q_BLOCK + jnp.arange(Q_BLOCK)
        mask = k_pos[None, :] <= q_pos[:, None]
        s = jnp.where(mask[None, None], s, NEG_INF)
        p = jax.nn.softmax(s, axis=-1)
        return jnp.einsum('bhqk,bkhd->bqhd', p.astype(v.dtype), v)

    out = lax.map(one_block, (q_blocks, F_blocks, jnp.arange(nb)))
    return jnp.swapaxes(out, 0, 1).reshape(B, S, H, D)


def _dilated_group(q, k, v, window, dilation):
    B, S, H, D = q.shape
    L = S // dilation
    W = window // dilation
    blk = W
    Lp = -(-L // blk) * blk
    nb = Lp // blk
    N = B * dilation
    scale = D ** -0.5

    def to_sub(t):
        t = jnp.transpose(t.reshape(B, L, dilation, H, D), (0, 2, 1, 3, 4)).reshape(N, L, H, D)
        t = jnp.pad(t, ((0, 0), (0, Lp - L), (0, 0), (0, 0)))
        return t.reshape(N, nb, blk, H, D)

    def with_prev(t):
        prev = jnp.pad(t[:, :-1], ((0, 0), (1, 0), (0, 0), (0, 0), (0, 0)))
        return jnp.concatenate([prev, t], axis=2)

    qs = to_sub(q)
    k2 = with_prev(to_sub(k))
    v2 = with_prev(to_sub(v))
    s = jnp.einsum('nbqhd,nbkhd->nbhqk', qs, k2, preferred_element_type=jnp.float32) * scale
    a = jnp.arange(blk)
    kk = jnp.arange(2 * blk)
    bidx = jnp.arange(nb)
    diff = blk + a[:, None] - kk[None, :]
    key_idx = bidx[:, None] * blk - blk + kk[None, :]
    mask = ((diff >= 0) & (diff <= W))[None, :, :] & (key_idx >= 0)[:, None, :]
    s = jnp.where(mask[None, :, None], s, NEG_INF)
    m = jnp.max(s, axis=-1, keepdims=True)
    p = jnp.exp(s - m)
    denom = jnp.sum(p, axis=-1, keepdims=True)
    lse = (m + jnp.log(denom))[..., 0]
    out = jnp.einsum('nbhqk,nbkhd->nbqhd', (p / denom).astype(v.dtype), v2)
    out = out.reshape(N, Lp, H, D)[:, :L]
    out = jnp.transpose(out.reshape(B, dilation, L, H, D), (0, 2, 1, 3, 4)).reshape(B, S, H, D)
    lse = jnp.transpose(lse, (0, 1, 3, 2)).reshape(N, Lp, H)[:, :L]
    lse = jnp.transpose(lse.reshape(B, dilation, L, H), (0, 2, 1, 3)).reshape(B, S, H)
    return out, lse


def _dilated_mixture(q, k, v):
    outs, lses = [], []
    for g, (window, dilation) in enumerate(DIL_GROUPS):
        sl = slice(g * DIL_HEADS_PER_GROUP, (g + 1) * DIL_HEADS_PER_GROUP)
        o, lse = _dilated_group(q[:, :, sl], k[:, :, sl], v[:, :, sl], window, dilation)
        outs.append(o)
        lses.append(lse)
    outs = jnp.stack(outs, axis=0)
    alpha = jax.nn.softmax(jnp.stack(lses, axis=0), axis=0)
    return jnp.sum(alpha[..., None].astype(outs.dtype) * outs, axis=0)


def _fwd_setup_inputs(seed: int = 0) -> dict:
    key = jax.random.key(seed)
    ks = jax.random.split(key, 12)
    f32 = jnp.float32

    def nrm(k, shape, fan_in):
        return jax.random.normal(k, shape, f32) * (fan_in ** -0.5)

    return {
        'x': jax.random.normal(ks[0], (BATCH, SEQ, D_MODEL), f32),
        'norm_attn_g': 1.0 + 0.02 * jax.random.normal(ks[1], (DEPTH, D_MODEL), f32),
        'w_in': nrm(ks[2], (DEPTH, D_MODEL, D_IN), D_MODEL),
        'b_forget': 2.0 + 0.5 * jax.random.normal(ks[3], (DEPTH, FOX_HEADS), f32),
        'w_branch_a': nrm(ks[4], (DEPTH, FOX_W, D_MODEL), FOX_W),
        'w_branch_b': nrm(ks[5], (DEPTH, DIL_OUT_W, D_MODEL), DIL_OUT_W),
        'w_out': nrm(ks[6], (DEPTH, D_MODEL, D_MODEL), D_MODEL),
        'norm_mlp_g': 1.0 + 0.02 * jax.random.normal(ks[7], (DEPTH, D_MODEL), f32),
        'w_up': nrm(ks[8], (DEPTH, D_MODEL, D_FF), D_MODEL),
        'w_down': nrm(ks[9], (DEPTH, D_FF, D_MODEL), D_FF),
        'norm_final_g': 1.0 + 0.02 * jax.random.normal(ks[10], (D_MODEL,), f32),
    }


def _fwd_reference(x, norm_attn_g, w_in, b_forget, w_branch_a, w_branch_b, w_out,
              norm_mlp_g, w_up, w_down, norm_final_g):
    B, S, _ = x.shape
    positions = jnp.arange(S)
    split_points = [int(v) for v in np.cumsum(IN_SPLIT_SIZES)[:-1]]
    for l in range(DEPTH):
        h = _rms_norm(x, norm_attn_g[l])
        proj = h @ w_in[l]
        qa, ka, va, fa, qb, kb, vb, ga, gb = jnp.split(proj, split_points, axis=-1)
        log_f = jax.nn.log_sigmoid(fa.astype(jnp.float32) + b_forget[l].astype(jnp.float32))
        oa = _fox_attention(qa.reshape(B, S, FOX_HEADS, HEAD_DIM),
                            ka.reshape(B, S, FOX_HEADS, HEAD_DIM),
                            va.reshape(B, S, FOX_HEADS, HEAD_DIM), log_f)
        qb = _partial_rope(qb.reshape(B, S, DIL_HEADS, HEAD_DIM), positions)
        kb = _partial_rope(kb.reshape(B, S, DIL_HEADS, HEAD_DIM), positions)
        ob = _dilated_mixture(qb, kb, vb.reshape(B, S, DIL_HEADS, HEAD_DIM))
        ya = oa.reshape(B, S, FOX_W) @ w_branch_a[l]
        yb = ob.reshape(B, S, DIL_OUT_W) @ w_branch_b[l]
        mixed = jax.nn.sigmoid(ga) * ya + jax.nn.sigmoid(gb) * yb
        x = x + mixed @ w_out[l]
        h = _rms_norm(x, norm_mlp_g[l])
        x = x + jnp.square(jax.nn.relu(h @ w_up[l])) @ w_down[l]
    return _rms_norm(x, norm_final_g)


import jax as _jax
import jax.numpy as _jnp

TWIN_FORMAT = 'train_step'
FWD_PARAMS = ['x', 'norm_attn_g', 'w_in', 'b_forget', 'w_branch_a', 'w_branch_b', 'w_out', 'norm_mlp_g', 'w_up', 'w_down', 'norm_final_g']
TWIN_WEIGHTS = ['norm_attn_g', 'w_in', 'b_forget', 'w_branch_a', 'w_branch_b', 'w_out', 'norm_mlp_g', 'w_up', 'w_down', 'norm_final_g']
TWIN_DIFF_INPUT = 'x'
TWIN_INPUTS = ['x', 'norm_attn_g', 'w_in', 'b_forget', 'w_branch_a', 'w_branch_b', 'w_out', 'norm_mlp_g', 'w_up', 'w_down', 'norm_final_g', 'loss_target', 'm_norm_attn_g', 'm_w_in', 'm_b_forget', 'm_w_branch_a', 'm_w_branch_b', 'm_w_out', 'm_norm_mlp_g', 'm_w_up', 'm_w_down', 'm_norm_final_g', 'v_norm_attn_g', 'v_w_in', 'v_b_forget', 'v_w_branch_a', 'v_w_branch_b', 'v_w_out', 'v_norm_mlp_g', 'v_w_up', 'v_w_down', 'v_norm_final_g']
TWIN_OUTPUTS = ['loss', 'grad_x', 'grad_norm_attn_g', 'grad_w_in', 'grad_b_forget', 'grad_w_branch_a', 'grad_w_branch_b', 'grad_w_out', 'grad_norm_mlp_g', 'grad_w_up', 'grad_w_down', 'grad_norm_final_g', 'delta_norm_attn_g', 'delta_w_in', 'delta_b_forget', 'delta_w_branch_a', 'delta_w_branch_b', 'delta_w_out', 'delta_norm_mlp_g', 'delta_w_up', 'delta_w_down', 'delta_norm_final_g', 'new_m_norm_attn_g', 'new_m_w_in', 'new_m_b_forget', 'new_m_w_branch_a', 'new_m_w_branch_b', 'new_m_w_out', 'new_m_norm_mlp_g', 'new_m_w_up', 'new_m_w_down', 'new_m_norm_final_g', 'new_v_norm_attn_g', 'new_v_w_in', 'new_v_b_forget', 'new_v_w_branch_a', 'new_v_w_branch_b', 'new_v_w_out', 'new_v_norm_mlp_g', 'new_v_w_up', 'new_v_w_down', 'new_v_norm_final_g']
TWIN_LEAF_KINDS = {'loss': 'loss', 'grad_x': 'grad_x', 'grad_norm_attn_g': 'grad_w', 'grad_w_in': 'grad_w', 'grad_b_forget': 'grad_w', 'grad_w_branch_a': 'grad_w', 'grad_w_branch_b': 'grad_w', 'grad_w_out': 'grad_w', 'grad_norm_mlp_g': 'grad_w', 'grad_w_up': 'grad_w', 'grad_w_down': 'grad_w', 'grad_norm_final_g': 'grad_w', 'delta_norm_attn_g': 'delta_w', 'delta_w_in': 'delta_w', 'delta_b_forget': 'delta_w', 'delta_w_branch_a': 'delta_w', 'delta_w_branch_b': 'delta_w', 'delta_w_out': 'delta_w', 'delta_norm_mlp_g': 'delta_w', 'delta_w_up': 'delta_w', 'delta_w_down': 'delta_w', 'delta_norm_final_g': 'delta_w', 'new_m_norm_attn_g': 'new_m', 'new_m_w_in': 'new_m', 'new_m_b_forget': 'new_m', 'new_m_w_branch_a': 'new_m', 'new_m_w_branch_b': 'new_m', 'new_m_w_out': 'new_m', 'new_m_norm_mlp_g': 'new_m', 'new_m_w_up': 'new_m', 'new_m_w_down': 'new_m', 'new_m_norm_final_g': 'new_m', 'new_v_norm_attn_g': 'new_v', 'new_v_w_in': 'new_v', 'new_v_b_forget': 'new_v', 'new_v_w_branch_a': 'new_v', 'new_v_w_branch_b': 'new_v', 'new_v_w_out': 'new_v', 'new_v_norm_mlp_g': 'new_v', 'new_v_w_up': 'new_v', 'new_v_w_down': 'new_v', 'new_v_norm_final_g': 'new_v'}


def _forward(args):
    return _fwd_reference(*[args[k] for k in FWD_PARAMS])


def _output_shape():
    out = _jax.eval_shape(lambda: _forward(_fwd_setup_inputs(0)))
    return out.shape, out.dtype

N_MICROBATCH = 1
ADAM_LR = 0.001
ADAM_B1 = 0.9
ADAM_B2 = 0.999
ADAM_EPS = 1e-08
ADAM_WD = 0.01
ADAM_STEP = 10
PER_EXAMPLE_BATCH_AXIS = {'x': 0, 'loss_target': 0}
SHARED_INPUTS = []
_WEIGHT_DTYPES = {'norm_attn_g': _jnp.float32, 'w_in': _jnp.float32, 'b_forget': _jnp.float32, 'w_branch_a': _jnp.float32, 'w_branch_b': _jnp.float32, 'w_out': _jnp.float32, 'norm_mlp_g': _jnp.float32, 'w_up': _jnp.float32, 'w_down': _jnp.float32, 'norm_final_g': _jnp.float32}
MOMENT_SCALE = {'norm_attn_g': 5.103325e-02, 'w_in': 2.115668e-02, 'b_forget': 1.947436e-01, 'w_branch_a': 3.048684e-02, 'w_branch_b': 1.321632e-02, 'w_out': 3.309099e-02, 'norm_mlp_g': 1.129006e-01, 'w_up': 5.555321e-02, 'w_down': 1.043178e-01, 'norm_final_g': 1.615414e+01}


def _to_microbatches(a, axis):
    t = _jnp.moveaxis(a, axis, 0)
    t = t.reshape((N_MICROBATCH, t.shape[0] // N_MICROBATCH) + t.shape[1:])
    return _jnp.moveaxis(t, 1, axis + 1)


def setup_inputs(seed: int = 0) -> dict:
    inp = _fwd_setup_inputs(seed)
    key = _jax.random.fold_in(_jax.random.key(seed), 7919)
    shape, _ = _output_shape()
    out = dict(inp)
    out["loss_target"] = _jax.random.normal(_jax.random.fold_in(key, 0), shape, _jnp.float32)
    for i, name in enumerate(TWIN_WEIGHTS):
        w = inp[name].astype(_jnp.float32)
        if MOMENT_SCALE is None:
            s = _jnp.sqrt(_jnp.mean(_jnp.square(w)) + 1e-30)
        else:
            s = MOMENT_SCALE[name]
        km, kv = _jax.random.split(_jax.random.fold_in(key, i + 1))
        out[name] = w
        out["m_" + name] = s * _jax.random.normal(km, w.shape, _jnp.float32)
        out["v_" + name] = (s * s) * _jax.random.uniform(kv, w.shape, _jnp.float32, 0.5, 1.5)
    if N_MICROBATCH > 1:
        for name, axis in PER_EXAMPLE_BATCH_AXIS.items():
            out[name] = _to_microbatches(out[name], axis)
    return {'x': out['x'], 'norm_attn_g': out['norm_attn_g'], 'w_in': out['w_in'], 'b_forget': out['b_forget'], 'w_branch_a': out['w_branch_a'], 'w_branch_b': out['w_branch_b'], 'w_out': out['w_out'], 'norm_mlp_g': out['norm_mlp_g'], 'w_up': out['w_up'], 'w_down': out['w_down'], 'norm_final_g': out['norm_final_g'], 'loss_target': out['loss_target'], 'm_norm_attn_g': out['m_norm_attn_g'], 'm_w_in': out['m_w_in'], 'm_b_forget': out['m_b_forget'], 'm_w_branch_a': out['m_w_branch_a'], 'm_w_branch_b': out['m_w_branch_b'], 'm_w_out': out['m_w_out'], 'm_norm_mlp_g': out['m_norm_mlp_g'], 'm_w_up': out['m_w_up'], 'm_w_down': out['m_w_down'], 'm_norm_final_g': out['m_norm_final_g'], 'v_norm_attn_g': out['v_norm_attn_g'], 'v_w_in': out['v_w_in'], 'v_b_forget': out['v_b_forget'], 'v_w_branch_a': out['v_w_branch_a'], 'v_w_branch_b': out['v_w_branch_b'], 'v_w_out': out['v_w_out'], 'v_norm_mlp_g': out['v_norm_mlp_g'], 'v_w_up': out['v_w_up'], 'v_w_down': out['v_w_down'], 'v_norm_final_g': out['v_norm_final_g']}


def _loss(weights, diff, rest, loss_target):
    with _jax.named_scope("forward"):
        args = {**rest, TWIN_DIFF_INPUT: diff, **{k: w.astype(_WEIGHT_DTYPES[k]) for k, w in weights.items()}}
        y = _forward(args)
    with _jax.named_scope("loss_head"):
        err = _jnp.square(y.astype(_jnp.float32) - loss_target)
        return 0.5 * _jnp.sum(_jnp.mean(err, axis=-1)) if err.ndim else 0.5 * err


def _adamw(w, g, m, v):
    m = ADAM_B1 * m + (1.0 - ADAM_B1) * g
    v = ADAM_B2 * v + (1.0 - ADAM_B2) * _jnp.square(g)
    m_hat = m / (1.0 - ADAM_B1 ** ADAM_STEP)
    v_hat = v / (1.0 - ADAM_B2 ** ADAM_STEP)
    delta = -ADAM_LR * (m_hat / (_jnp.sqrt(v_hat) + ADAM_EPS) + ADAM_WD * w)
    return delta, m, v


def reference(x, norm_attn_g, w_in, b_forget, w_branch_a, w_branch_b, w_out, norm_mlp_g, w_up, w_down, norm_final_g, loss_target, m_norm_attn_g, m_w_in, m_b_forget, m_w_branch_a, m_w_branch_b, m_w_out, m_norm_mlp_g, m_w_up, m_w_down, m_norm_final_g, v_norm_attn_g, v_w_in, v_b_forget, v_w_branch_a, v_w_branch_b, v_w_out, v_norm_mlp_g, v_w_up, v_w_down, v_norm_final_g):
    given = dict(x=x, norm_attn_g=norm_attn_g, w_in=w_in, b_forget=b_forget, w_branch_a=w_branch_a, w_branch_b=w_branch_b, w_out=w_out, norm_mlp_g=norm_mlp_g, w_up=w_up, w_down=w_down, norm_final_g=norm_final_g, loss_target=loss_target, m_norm_attn_g=m_norm_attn_g, m_w_in=m_w_in, m_b_forget=m_b_forget, m_w_branch_a=m_w_branch_a, m_w_branch_b=m_w_branch_b, m_w_out=m_w_out, m_norm_mlp_g=m_norm_mlp_g, m_w_up=m_w_up, m_w_down=m_w_down, m_norm_final_g=m_norm_final_g, v_norm_attn_g=v_norm_attn_g, v_w_in=v_w_in, v_b_forget=v_b_forget, v_w_branch_a=v_w_branch_a, v_w_branch_b=v_w_branch_b, v_w_out=v_w_out, v_norm_mlp_g=v_norm_mlp_g, v_w_up=v_w_up, v_w_down=v_w_down, v_norm_final_g=v_norm_final_g)
    weights = {n: given[n] for n in TWIN_WEIGHTS}
    shared = {n: given[n] for n in SHARED_INPUTS}
    per_example = {n: given[n] for n in ['x']}
    grad_fn = _jax.value_and_grad(_loss, argnums=(0, 1))

    def one_microbatch(ex, loss_target):
        ex = dict(ex)
        diff = ex.pop(TWIN_DIFF_INPUT)
        return grad_fn(weights, diff, {**shared, **ex}, loss_target)

    if N_MICROBATCH == 1:
        loss, (grad_w, grad_x) = one_microbatch(per_example, given["loss_target"])
    else:
        def body(carry, xs):
            loss_sum, grad_sum = carry
            l_k, (gw_k, gx_k) = one_microbatch(xs[0], xs[1])
            with _jax.named_scope("update"):
                return (loss_sum + l_k, _jax.tree.map(_jnp.add, grad_sum, gw_k)), gx_k

        init = (_jnp.zeros((), _jnp.float32), _jax.tree.map(_jnp.zeros_like, weights))
        (loss, grad_w), grad_x = _jax.lax.scan(body, init, (per_example, given["loss_target"]))
    with _jax.named_scope("update"):
        delta_w, new_m, new_v = {}, {}, {}
        for n in TWIN_WEIGHTS:
            delta_w[n], new_m[n], new_v[n] = _adamw(weights[n], grad_w[n], given["m_" + n], given["v_" + n])
    return (loss, grad_x, *[grad_w[n] for n in TWIN_WEIGHTS], *[delta_w[n] for n in TWIN_WEIGHTS],
            *[new_m[n] for n in TWIN_WEIGHTS], *[new_v[n] for n in TWIN_WEIGHTS])
```

```python
import functools

import jax
import jax.numpy as jnp
from jax import lax
from jax.experimental import pallas as pl
from jax.experimental.pallas import tpu as pltpu

F32 = jnp.float32
BF16 = jnp.bfloat16
MESH = pl.DeviceIdType.MESH

S = 2048
D = 1024
HD = 64
FOX_H = 8
FOX_W = FOX_H * HD
DIL_HG = 4
DIL_G = 3
DIL_W = DIL_G * DIL_HG * HD
DIL_OUT = DIL_HG * HD
DIL_BLK = 128
DIL_R = (1, 4, 16)
DFF = 4 * D
D_IN = 3 * FOX_W + FOX_H + 3 * DIL_W + 2 * D
EPS = 1e-6
NEG_INF = -1e30
SCALE = HD ** -0.5
ROPE_THETA = 500000.0
ROPE_DIM = HD // 4
N_DEV = 8

ADAM_LR = 0.001
ADAM_B1 = 0.9
ADAM_B2 = 0.999
ADAM_EPS = 1e-08
ADAM_WD = 0.01
ADAM_STEP = 10

LANES = 128
SEG_A = 0
SEG_B = 3 * FOX_W
SEG_G = SEG_B + 3 * DIL_W
SEG_F = SEG_G + 2 * D
D_PAD = SEG_F + 2 * LANES

PACK_ROWS = (D * D_IN // N_DEV // LANES, FOX_W, DIL_OUT, D * D // N_DEV // LANES,
             D * DFF // N_DEV // LANES, DFF * D // N_DEV // LANES)
PACK_OFF = tuple(sum(PACK_ROWS[:i]) for i in range(len(PACK_ROWS) + 1))
PACK_R = 16384
PACK_TR = 2048
SMALL_R = 32

VMEM_MB = 56


def _cparams(dims=None, vmem_mb=VMEM_MB, **kw):
    return pltpu.CompilerParams(dimension_semantics=dims, vmem_limit_bytes=vmem_mb << 20, **kw)


_NN = (((1,), (0,)), ((), ()))
_NT = (((1,), (1,)), ((), ()))
_TN = (((0,), (0,)), ((), ()))


def _dot(a, b, dims):
    return lax.dot_general(a.astype(BF16), b.astype(BF16), dims, preferred_element_type=F32)


def _mm(a, b, *, mode, tm, tn, out_dtypes, name, n=None, b_off=0, epilogue=None, extras=()):
    if mode == "nn":
        m, k = a.shape
        n = n or b.shape[1]
        a_spec = pl.BlockSpec((tm, k), lambda i, j: (i, 0))
        b_spec = pl.BlockSpec((k, tn), lambda i, j: (0, j + b_off))
        dims = _NN
    elif mode == "nt":
        m, k = a.shape
        n = n or b.shape[0]
        a_spec = pl.BlockSpec((tm, k), lambda i, j: (i, 0))
        b_spec = pl.BlockSpec((tn, k), lambda i, j: (j + b_off, 0))
        dims = _NT
    else:
        k, m = a.shape
        n = n or b.shape[1]
        a_spec = pl.BlockSpec((k, tm), lambda i, j: (0, i))
        b_spec = pl.BlockSpec((k, tn), lambda i, j: (0, j + b_off))
        dims = _TN
    assert m % tm == 0 and n % tn == 0, (name, m, n, tm, tn)
    n_extra = len(extras)
    tile = pl.BlockSpec((tm, tn), lambda i, j: (i, j))

    def body(a_ref, b_ref, *refs):
        acc = _dot(a_ref[...], b_ref[...], dims)
        ex = [r[...] for r in refs[:n_extra]]
        outs = epilogue(acc, *ex) if epilogue is not None else (acc,)
        for o_ref, o in zip(refs[n_extra:], outs):
            o_ref[...] = o.astype(o_ref.dtype)

    res = pl.pallas_call(
        body, name=name, grid=(m // tm, n // tn),
        in_specs=[a_spec, b_spec] + [tile] * n_extra,
        out_specs=[tile] * len(out_dtypes),
        out_shape=[jax.ShapeDtypeStruct((m, n), dt) for dt in out_dtypes],
        compiler_params=_cparams(("parallel", "parallel")),
    )(a, b, *extras)
    return res if len(out_dtypes) > 1 else res[0]


ROW_T = 256


def _rms_fwd(x, g, name):
    def body(x_ref, g_ref, o_ref):
        x = x_ref[...]
        r = lax.rsqrt(jnp.mean(x * x, axis=-1, keepdims=True) + EPS)
        o_ref[...] = ((x * r) * g_ref[...]).astype(BF16)

    return pl.pallas_call(
        body, name=name, grid=(S // ROW_T,),
        in_specs=[pl.BlockSpec((ROW_T, D), lambda i: (i, 0)), pl.BlockSpec((1, D), lambda i: (0, 0))],
        out_specs=pl.BlockSpec((ROW_T, D), lambda i: (i, 0)),
        out_shape=jax.ShapeDtypeStruct((S, D), BF16),
        compiler_params=_cparams(("parallel",)),
    )(x, g)


def _rms_bwd(x, g, dh, dres, name):
    def body(x_ref, g_ref, dh_ref, dres_ref, dx_ref, dg_ref):
        @pl.when(pl.program_id(0) == 0)
        def _():
            dg_ref[...] = jnp.zeros_like(dg_ref)

        x = x_ref[...]
        dh = dh_ref[...]
        r = lax.rsqrt(jnp.mean(x * x, axis=-1, keepdims=True) + EPS)
        xn = x * r
        dhn = dh * g_ref[...]
        dx_ref[...] = dres_ref[...] + r * (dhn - xn * jnp.mean(dhn * xn, axis=-1, keepdims=True))
        dg_ref[...] += jnp.sum(dh * xn, axis=0, keepdims=True)

    row = pl.BlockSpec((ROW_T, D), lambda i: (i, 0))
    vec = pl.BlockSpec((1, D), lambda i: (0, 0))
    return pl.pallas_call(
        body, name=name, grid=(S // ROW_T,),
        in_specs=[row, vec, row, row], out_specs=[row, vec],
        out_shape=[jax.ShapeDtypeStruct((S, D), F32), jax.ShapeDtypeStruct((1, D), F32)],
        compiler_params=_cparams(("arbitrary",)),
    )(x, g, dh, dres)


def _final_norm_loss(x, g, tgt):
    def body(x_ref, g_ref, t_ref, dx_ref, dg_ref, loss_ref):
        @pl.when(pl.program_id(0) == 0)
        def _():
            dg_ref[...] = jnp.zeros_like(dg_ref)
            loss_ref[...] = jnp.zeros_like(loss_ref)

        x = x_ref[...]
        g = g_ref[...]
        r = lax.rsqrt(jnp.mean(x * x, axis=-1, keepdims=True) + EPS)
        xn = x * r
        err = xn * g - t_ref[...]
        row_loss = jnp.mean(err * err, axis=-1, keepdims=True)
        loss_ref[...] += 0.5 * jnp.sum(row_loss, axis=0, keepdims=True) * jnp.ones((1, LANES), F32)
        dy = err * (1.0 / D)
        dyn = dy * g
        dx_ref[...] = r * (dyn - xn * jnp.mean(dyn * xn, axis=-1, keepdims=True))
        dg_ref[...] += jnp.sum(dy * xn, axis=0, keepdims=True)

    row = pl.BlockSpec((ROW_T, D), lambda i: (i, 0))
    vec = pl.BlockSpec((1, D), lambda i: (0, 0))
    return pl.pallas_call(
        body, name="final_norm_loss", grid=(S // ROW_T,),
        in_specs=[row, vec, row],
        out_specs=[row, vec, pl.BlockSpec((1, LANES), lambda i: (0, 0))],
        out_shape=[jax.ShapeDtypeStruct((S, D), F32), jax.ShapeDtypeStruct((1, D), F32),
                   jax.ShapeDtypeStruct((1, LANES), F32)],
        compiler_params=_cparams(("arbitrary",)),
    )(x, g, tgt)


def _sigmoid(z):
    return 1.0 / (1.0 + jnp.exp(-z))


def _gate_fwd(proj_g, ya, yb):
    def body(ga_ref, gb_ref, ya_ref, yb_ref, o_ref):
        o_ref[...] = (_sigmoid(ga_ref[...]) * ya_ref[...] + _sigmoid(gb_ref[...]) * yb_ref[...]).astype(BF16)

    row = pl.BlockSpec((ROW_T, D), lambda i: (i, 0))
    return pl.pallas_call(
        body, name="gate_fwd", grid=(S // ROW_T,),
        in_specs=[row, pl.BlockSpec((ROW_T, D), lambda i: (i, 1)), row, row],
        out_specs=row, out_shape=jax.ShapeDtypeStruct((S, D), BF16),
        compiler_params=_cparams(("parallel",)),
    )(proj_g, proj_g, ya, yb)


def _gate_bwd(dmixed, proj_g, ya, yb):
    def body(dm_ref, ga_ref, gb_ref, ya_ref, yb_ref, dya_ref, dyb_ref, dga_ref, dgb_ref):
        dm = dm_ref[...]
        sa = _sigmoid(ga_ref[...])
        sb = _sigmoid(gb_ref[...])
        dya_ref[...] = (dm * sa).astype(BF16)
        dyb_ref[...] = (dm * sb).astype(BF16)
        dga_ref[...] = (dm * ya_ref[...] * (sa * (1.0 - sa))).astype(BF16)
        dgb_ref[...] = (dm * yb_ref[...] * (sb * (1.0 - sb))).astype(BF16)

    row = pl.BlockSpec((ROW_T, D), lambda i: (i, 0))
    row1 = pl.BlockSpec((ROW_T, D), lambda i: (i, 1))
    return pl.pallas_call(
        body, name="gate_bwd", grid=(S // ROW_T,),
        in_specs=[row, row, row1, row, row],
        out_specs=[row, row, row, row],
        out_shape=[jax.ShapeDtypeStruct((S, D), BF16)] * 4,
        compiler_params=_cparams(("parallel",)),
    )(dmixed, proj_g, proj_g, ya, yb)


FOX_TQ = 256


def _scan_rows(x, reverse):
    n = x.shape[0]
    row = lax.broadcasted_iota(jnp.int32, x.shape, 0)
    k = 1
    while k < n:
        if reverse:
            x = x + jnp.where(row < n - k, pltpu.roll(x, n - k, 0), 0.0)
        else:
            x = x + jnp.where(row >= k, pltpu.roll(x, k, 0), 0.0)
        k *= 2
    return x


def _fox_scan(proj_f, b_pad):
    def body(f_ref, b_ref, fc_ref, fr_ref):
        z = f_ref[...] + b_ref[...]
        lf = jnp.minimum(z, 0.0) - jnp.log1p(jnp.exp(-jnp.abs(z)))
        f = _scan_rows(lf, reverse=False)
        fc_ref[...] = f
        fr_ref[...] = f.T

    return pl.pallas_call(
        body, name="fox_scan", grid=(1,),
        in_specs=[pl.BlockSpec((S, LANES), lambda i: (0, 0)), pl.BlockSpec((1, LANES), lambda i: (0, 0))],
        out_specs=[pl.BlockSpec((S, LANES), lambda i: (0, 0)), pl.BlockSpec((LANES, S), lambda i: (0, 0))],
        out_shape=[jax.ShapeDtypeStruct((S, LANES), F32), jax.ShapeDtypeStruct((LANES, S), F32)],
        compiler_params=_cparams(("arbitrary",)),
    )(proj_f, b_pad)


def _fox_dscan(dft, dfs, proj_f, b_pad):
    def body(dft_ref, dfs_ref, f_ref, b_ref, dfa_ref, db_ref):
        dfs_pad = jnp.concatenate([dfs_ref[...], jnp.zeros((LANES - FOX_H, S), F32)], axis=0)
        dlf = _scan_rows(dft_ref[...] + dfs_pad.T, reverse=True)
        z = f_ref[...] + b_ref[...]
        lane = lax.broadcasted_iota(jnp.int32, (S, LANES), 1)
        dfa = jnp.where(lane < FOX_H, dlf / (1.0 + jnp.exp(z)), 0.0)
        dfa_ref[...] = dfa.astype(BF16)
        db_ref[...] = jnp.sum(dfa, axis=0, keepdims=True)

    return pl.pallas_call(
        body, name="fox_dscan", grid=(1,),
        in_specs=[pl.BlockSpec((S, LANES), lambda i: (0, 0)), pl.BlockSpec((FOX_H, S), lambda i: (0, 0)),
                  pl.BlockSpec((S, LANES), lambda i: (0, 0)), pl.BlockSpec((1, LANES), lambda i: (0, 0))],
        out_specs=[pl.BlockSpec((S, LANES), lambda i: (0, 0)), pl.BlockSpec((1, LANES), lambda i: (0, 0))],
        out_shape=[jax.ShapeDtypeStruct((S, LANES), BF16), jax.ShapeDtypeStruct((1, LANES), F32)],
        compiler_params=_cparams(("arbitrary",)),
    )(dft, dfs, proj_f, b_pad)


def _fox_logits(q, k, ft, fs, causal):
    s = _dot(q, k, _NT) * SCALE
    s = (s + ft) - fs
    return jnp.where(causal, s, NEG_INF)


def _fox_causal(qi):
    row = qi * FOX_TQ + lax.broadcasted_iota(jnp.int32, (FOX_TQ, S), 0)
    col = lax.broadcasted_iota(jnp.int32, (FOX_TQ, S), 1)
    return col <= row


def _fox_fwd(proj_a, fcol, frow):
    def body(q_ref, k_ref, v_ref, ft_ref, fs_ref, o_ref, lse_ref):
        causal = _fox_causal(pl.program_id(0))
        ft = ft_ref[...]
        fs = fs_ref[...]
        lane = lax.broadcasted_iota(jnp.int32, (FOX_TQ, LANES), 1)
        lse = jnp.zeros((FOX_TQ, LANES), F32)
        for h in range(FOX_H):
            sl = slice(h * HD, (h + 1) * HD)
            s = _fox_logits(q_ref[:, sl], k_ref[:, sl], ft[:, h:h + 1], fs[h:h + 1, :], causal)
            m = jnp.max(s, axis=-1, keepdims=True)
            p = jnp.exp(s - m)
            l = jnp.sum(p, axis=-1, keepdims=True)
            o_ref[:, sl] = _dot(p / l, v_ref[:, sl], _NN)
            lse = jnp.where(lane == h, m + jnp.log(l), lse)
        lse_ref[...] = lse

    return pl.pallas_call(
        body, name="fox_fwd", grid=(S // FOX_TQ,),
        in_specs=[pl.BlockSpec((FOX_TQ, FOX_W), lambda i: (i, 0)),
                  pl.BlockSpec((S, FOX_W), lambda i: (0, 1)),
                  pl.BlockSpec((S, FOX_W), lambda i: (0, 2)),
                  pl.BlockSpec((FOX_TQ, LANES), lambda i: (i, 0)),
                  pl.BlockSpec((FOX_H, S), lambda i: (0, 0))],
        out_specs=[pl.BlockSpec((FOX_TQ, FOX_W), lambda i: (i, 0)),
                   pl.BlockSpec((FOX_TQ, LANES), lambda i: (i, 0))],
        out_shape=[jax.ShapeDtypeStruct((S, FOX_W), F32), jax.ShapeDtypeStruct((S, LANES), F32)],
        compiler_params=_cparams(("parallel",)),
    )(proj_a, proj_a, proj_a, fcol, frow)


def _fox_bwd(proj_a, fcol, frow, lse, do):
    n_q = S // FOX_TQ

    def body(q_ref, k_ref, v_ref, ft_ref, fs_ref, lse_ref, do_ref,
             dq_ref, dk_ref, dv_ref, dft_ref, dfs_ref, dk_acc, dv_acc):
        qi = pl.program_id(0)

        @pl.when(qi == 0)
        def _():
            dk_acc[...] = jnp.zeros_like(dk_acc)
            dv_acc[...] = jnp.zeros_like(dv_acc)
            dfs_ref[...] = jnp.zeros_like(dfs_ref)

        causal = _fox_causal(qi)
        ft = ft_ref[...]
        fs = fs_ref[...]
        lse = lse_ref[...]
        lane = lax.broadcasted_iota(jnp.int32, (FOX_TQ, LANES), 1)
        dft = jnp.zeros((FOX_TQ, LANES), F32)
        for h in range(FOX_H):
            sl = slice(h * HD, (h + 1) * HD)
            q = q_ref[:, sl]
            k = k_ref[:, sl]
            do_h = do_ref[:, sl]
            s = _fox_logits(q, k, ft[:, h:h + 1], fs[h:h + 1, :], causal)
            p = jnp.exp(s - lse[:, h:h + 1])
            dp = _dot(do_h, v_ref[:, sl], _NT)
            ds = p * (dp - jnp.sum(dp * p, axis=-1, keepdims=True))
            dft = jnp.where(lane == h, jnp.sum(ds, axis=-1, keepdims=True), dft)
            dfs_ref[h:h + 1, :] -= jnp.sum(ds, axis=0, keepdims=True)
            dsb = (ds * SCALE).astype(BF16)
            dq_ref[:, sl] = _dot(dsb, k, _NN).astype(BF16)
            dk_acc[:, sl] += _dot(dsb, q, _TN)
            dv_acc[:, sl] += _dot(p, do_h, _TN)
        dft_ref[...] = dft

        @pl.when(qi == n_q - 1)
        def _():
            dk_ref[...] = dk_acc[...].astype(BF16)
            dv_ref[...] = dv_acc[...].astype(BF16)

    qblk = pl.BlockSpec((FOX_TQ, FOX_W), lambda i: (i, 0))
    full = pl.BlockSpec((S, FOX_W), lambda i: (0, 0))
    lane_blk = pl.BlockSpec((FOX_TQ, LANES), lambda i: (i, 0))
    return pl.pallas_call(
        body, name="fox_bwd", grid=(n_q,),
        in_specs=[qblk, pl.BlockSpec((S, FOX_W), lambda i: (0, 1)), pl.BlockSpec((S, FOX_W), lambda i: (0, 2)),
                  lane_blk, pl.BlockSpec((FOX_H, S), lambda i: (0, 0)), lane_blk, qblk],
        out_specs=[qblk, full, full, lane_blk, pl.BlockSpec((FOX_H, S), lambda i: (0, 0))],
        out_shape=[jax.ShapeDtypeStruct((S, FOX_W), BF16)] * 3
                  + [jax.ShapeDtypeStruct((S, LANES), F32), jax.ShapeDtypeStruct((FOX_H, S), F32)],
        scratch_shapes=[pltpu.VMEM((S, FOX_W), F32), pltpu.VMEM((S, FOX_W), F32)],
        compiler_params=_cparams(("arbitrary",)),
    )(proj_a, proj_a, proj_a, fcol, frow, lse, do)


def _rope_tables():
    half = ROPE_DIM // 2
    inv_freq = jnp.power(jnp.float32(ROPE_THETA), -jnp.arange(half, dtype=F32) * 2.0 / ROPE_DIM)
    ang = jnp.arange(S).astype(F32)[:, None] * inv_freq[None, :]
    cos, sin = jnp.cos(ang), jnp.sin(ang)
    z = jnp.zeros((S, half), F32)
    rest = HD - ROPE_DIM
    t_self = jnp.concatenate([cos, cos, jnp.ones((S, rest), F32)], axis=1)
    t_up = jnp.concatenate([-sin, z, jnp.zeros((S, rest), F32)], axis=1)
    t_dn = jnp.concatenate([z, sin, jnp.zeros((S, rest), F32)], axis=1)
    rep = LANES // HD
    return tuple(jnp.tile(t, (1, rep)) for t in (t_self, t_up, t_dn))


ROPE_TN = 256


def _rope(x_arr, col_blk0, n_cols, tables, *, backward, out_dtype, name):
    half = ROPE_DIM // 2
    rep = ROPE_TN // LANES

    def body(x_ref, a_ref, u_ref, d_ref, o_ref):
        x = x_ref[...].astype(F32)
        a = jnp.tile(a_ref[...], (1, rep))
        u = jnp.tile(u_ref[...], (1, rep))
        d = jnp.tile(d_ref[...], (1, rep))
        if backward:
            y = x * a + pltpu.roll(x * u, half, 1) + pltpu.roll(x * d, ROPE_TN - half, 1)
        else:
            y = x * a + pltpu.roll(x, ROPE_TN - half, 1) * u + pltpu.roll(x, half, 1) * d
        o_ref[...] = y.astype(out_dtype)

    tab = pl.BlockSpec((ROW_T, LANES), lambda i, j: (i, 0))
    return pl.pallas_call(
        body, name=name, grid=(S // ROW_T, n_cols // ROPE_TN),
        in_specs=[pl.BlockSpec((ROW_T, ROPE_TN), lambda i, j: (i, j + col_blk0)), tab, tab, tab],
        out_specs=pl.BlockSpec((ROW_T, ROPE_TN), lambda i, j: (i, j)),
        out_shape=jax.ShapeDtypeStruct((S, n_cols), out_dtype),
        compiler_params=_cparams(("parallel", "parallel")),
    )(x_arr, *tables)


DIL_NB = S // DIL_BLK


def _to_sub(t):
    out = []
    for g, r in enumerate(DIL_R):
        tg = t[:, g * DIL_OUT:(g + 1) * DIL_OUT]
        out.append(tg.reshape(S // r, r, DIL_OUT).transpose(1, 0, 2).reshape(S, DIL_OUT))
    return jnp.stack(out, axis=0)


def _from_sub(t3):
    w = t3.shape[-1]
    out = []
    for g, r in enumerate(DIL_R):
        out.append(t3[g].reshape(r, S // r, w).transpose(1, 0, 2).reshape(S, w))
    return jnp.stack(out, axis=0)


def _dil_mask(g, n):
    blocks_per_seq = jnp.right_shift(DIL_NB, 2 * g)
    has_prev = jnp.bitwise_and(n, blocks_per_seq - 1) != 0
    a = lax.broadcasted_iota(jnp.int32, (DIL_BLK, 2 * DIL_BLK), 0)
    kk = lax.broadcasted_iota(jnp.int32, (DIL_BLK, 2 * DIL_BLK), 1)
    diff = DIL_BLK + a - kk
    return (diff >= 0) & (diff <= DIL_BLK) & ((kk >= DIL_BLK) | has_prev)


def _dil_specs():
    cur = pl.BlockSpec((None, DIL_BLK, DIL_OUT), lambda g, n: (g, n, 0))
    prev = pl.BlockSpec((None, DIL_BLK, DIL_OUT), lambda g, n: (g, jnp.maximum(n - 1, 0), 0))
    lane_blk = pl.BlockSpec((None, DIL_BLK, LANES), lambda g, n: (g, n, 0))
    return cur, prev, lane_blk


def _dil_fwd(q3, k3, v3):
    def body(q_ref, kc_ref, kp_ref, vc_ref, vp_ref, o_ref, lse_ref):
        ok = _dil_mask(pl.program_id(0), pl.program_id(1))
        k2 = jnp.concatenate([kp_ref[...], kc_ref[...]], axis=0)
        v2 = jnp.concatenate([vp_ref[...], vc_ref[...]], axis=0)
        lane = lax.broadcasted_iota(jnp.int32, (DIL_BLK, LANES), 1)
        lse = jnp.zeros((DIL_BLK, LANES), F32)
        for h in range(DIL_HG):
            sl = slice(h * HD, (h + 1) * HD)
            s = jnp.where(ok, _dot(q_ref[:, sl], k2[:, sl], _NT) * SCALE, NEG_INF)
            m = jnp.max(s, axis=-1, keepdims=True)
            p = jnp.exp(s - m)
            l = jnp.sum(p, axis=-1, keepdims=True)
            o_ref[:, sl] = _dot(p / l, v2[:, sl], _NN)
            lse = jnp.where(lane == h, m + jnp.log(l), lse)
        lse_ref[...] = lse

    cur, prev, lane_blk = _dil_specs()
    return pl.pallas_call(
        body, name="dil_fwd", grid=(DIL_G, DIL_NB),
        in_specs=[cur, cur, prev, cur, prev], out_specs=[cur, lane_blk],
        out_shape=[jax.ShapeDtypeStruct((DIL_G, S, DIL_OUT), F32), jax.ShapeDtypeStruct((DIL_G, S, LANES), F32)],
        compiler_params=_cparams(("parallel", "parallel")),
    )(q3, k3, k3, v3, v3)


def _dil_bwd(q3, k3, v3, lse3, do3, c3):
    def body(q_ref, kc_ref, kp_ref, vc_ref, vp_ref, lse_ref, do_ref, c_ref, dq_ref, dk_ref, dv_ref):
        n = pl.program_id(1)

        @pl.when(n == 0)
        def _():
            dk_ref[...] = jnp.zeros_like(dk_ref)
            dv_ref[...] = jnp.zeros_like(dv_ref)

        ok = _dil_mask(pl.program_id(0), n)
        k2 = jnp.concatenate([kp_ref[...], kc_ref[...]], axis=0)
        v2 = jnp.concatenate([vp_ref[...], vc_ref[...]], axis=0)
        lse = lse_ref[...]
        c = c_ref[...]
        cur_rows = pl.ds(pl.multiple_of(n * DIL_BLK, DIL_BLK), DIL_BLK)
        prev_rows = pl.ds(pl.multiple_of(jnp.maximum(n - 1, 0) * DIL_BLK, DIL_BLK), DIL_BLK)
        for h in range(DIL_HG):
            sl = slice(h * HD, (h + 1) * HD)
            q = q_ref[:, sl]
            do_h = do_ref[:, sl]
            s = jnp.where(ok, _dot(q, k2[:, sl], _NT) * SCALE, NEG_INF)
            p = jnp.exp(s - lse[:, h:h + 1])
            dp = _dot(do_h, v2[:, sl], _NT)
            ds = p * (dp - c[:, h:h + 1])
            dsb = (ds * SCALE).astype(BF16)
            dq_ref[:, sl] = _dot(dsb, k2[:, sl], _NN)
            dk2 = _dot(dsb, q, _TN)
            dv2 = _dot(p, do_h, _TN)
            dk_ref[prev_rows, sl] += dk2[:DIL_BLK]
            dk_ref[cur_rows, sl] += dk2[DIL_BLK:]
            dv_ref[prev_rows, sl] += dv2[:DIL_BLK]
            dv_ref[cur_rows, sl] += dv2[DIL_BLK:]

    cur, prev, lane_blk = _dil_specs()
    whole = pl.BlockSpec((None, S, DIL_OUT), lambda g, n: (g, 0, 0))
    return pl.pallas_call(
        body, name="dil_bwd", grid=(DIL_G, DIL_NB),
        in_specs=[cur, cur, prev, cur, prev, lane_blk, cur, lane_blk],
        out_specs=[cur, whole, whole],
        out_shape=[jax.ShapeDtypeStruct((DIL_G, S, DIL_OUT), F32)] * 3,
        compiler_params=_cparams(("parallel", "arbitrary")),
    )(q3, k3, k3, v3, v3, lse3, do3, c3)


def _dil_combine(o3, lse3):
    def body(o_ref, lse_ref, ob_ref, al_ref):
        lse = lse_ref[...]
        m = jnp.maximum(jnp.maximum(lse[0], lse[1]), lse[2])
        e = [jnp.exp(lse[g] - m) for g in range(DIL_G)]
        den = (e[0] + e[1]) + e[2]
        al = [e[g] / den for g in range(DIL_G)]
        for g in range(DIL_G):
            al_ref[g] = al[g]
        for h in range(DIL_HG):
            sl = slice(h * HD, (h + 1) * HD)
            acc = al[0][:, h:h + 1] * o_ref[0, :, sl]
            for g in range(1, DIL_G):
                acc = acc + al[g][:, h:h + 1] * o_ref[g, :, sl]
            ob_ref[:, sl] = acc

    return pl.pallas_call(
        body, name="dil_combine", grid=(S // ROW_T,),
        in_specs=[pl.BlockSpec((DIL_G, ROW_T, DIL_OUT), lambda i: (0, i, 0)),
                  pl.BlockSpec((DIL_G, ROW_T, LANES), lambda i: (0, i, 0))],
        out_specs=[pl.BlockSpec((ROW_T, DIL_OUT), lambda i: (i, 0)),
                   pl.BlockSpec((DIL_G, ROW_T, LANES), lambda i: (0, i, 0))],
        out_shape=[jax.ShapeDtypeStruct((S, DIL_OUT), F32), jax.ShapeDtypeStruct((DIL_G, S, LANES), F32)],
        compiler_params=_cparams(("parallel",)),
    )(o3, lse3)


def _dil_combine_bwd(dob, ob, alpha):
    def body(dob_ref, ob_ref, al_ref, do_ref, c_ref):
        dob = dob_ref[...]
        prod = dob * ob_ref[...]
        lane = lax.broadcasted_iota(jnp.int32, (ROW_T, LANES), 1)
        al = al_ref[...]
        c = [jnp.zeros((ROW_T, LANES), F32) for _ in range(DIL_G)]
        for h in range(DIL_HG):
            sl = slice(h * HD, (h + 1) * HD)
            delta = jnp.sum(prod[:, sl], axis=-1, keepdims=True)
            for g in range(DIL_G):
                a = al[g][:, h:h + 1]
                do_ref[g, :, sl] = (a * dob[:, sl]).astype(BF16)
                c[g] = jnp.where(lane == h, a * delta, c[g])
        for g in range(DIL_G):
            c_ref[g] = c[g]

    return pl.pallas_call(
        body, name="dil_combine_bwd", grid=(S // ROW_T,),
        in_specs=[pl.BlockSpec((ROW_T, DIL_OUT), lambda i: (i, 0)), pl.BlockSpec((ROW_T, DIL_OUT), lambda i: (i, 0)),
                  pl.BlockSpec((DIL_G, ROW_T, LANES), lambda i: (0, i, 0))],
        out_specs=[pl.BlockSpec((DIL_G, ROW_T, DIL_OUT), lambda i: (0, i, 0)),
                   pl.BlockSpec((DIL_G, ROW_T, LANES), lambda i: (0, i, 0))],
        out_shape=[jax.ShapeDtypeStruct((DIL_G, S, DIL_OUT), BF16), jax.ShapeDtypeStruct((DIL_G, S, LANES), F32)],
        compiler_params=_cparams(("parallel",)),
    )(dob, ob, alpha)


def _local_step(x, tgt, g_attn, g_mlp, g_final, b_pad, w_in_pad, w_a, w_b, w_out, w_up, w_down):
    tables = _rope_tables()

    h1 = _rms_fwd(x, g_attn, "rms_attn_fwd")
    proj_a = _mm(h1, w_in_pad, mode="nn", tm=S, tn=512, n=3 * FOX_W, b_off=0, out_dtypes=[BF16], name="proj_a")
    proj_b = _mm(h1, w_in_pad, mode="nn", tm=S, tn=256, n=3 * DIL_W, b_off=SEG_B // 256, out_dtypes=[F32], name="proj_b")
    proj_g = _mm(h1, w_in_pad, mode="nn", tm=S, tn=256, n=2 * D, b_off=SEG_G // 256, out_dtypes=[F32], name="proj_g")
    proj_f = _mm(h1, w_in_pad, mode="nn", tm=S, tn=LANES, n=LANES, b_off=SEG_F // LANES, out_dtypes=[F32], name="proj_f")

    fcol, frow = _fox_scan(proj_f, b_pad)
    oa, lse_a = _fox_fwd(proj_a, fcol, frow)

    qk_rot = _rope(proj_b, 0, 2 * DIL_W, tables, backward=False, out_dtype=BF16, name="rope_fwd")
    q3 = _to_sub(qk_rot[:, :DIL_W])
    k3 = _to_sub(qk_rot[:, DIL_W:])
    v3 = _to_sub(proj_b[:, 2 * DIL_W:].astype(BF16))
    o3, lse3 = _dil_fwd(q3, k3, v3)
    ob, alpha = _dil_combine(_from_sub(o3), _from_sub(lse3))

    ya = _mm(oa, w_a, mode="nn", tm=S, tn=256, out_dtypes=[F32], name="branch_a")
    yb = _mm(ob, w_b, mode="nn", tm=S, tn=256, out_dtypes=[F32], name="branch_b")
    mixed = _gate_fwd(proj_g, ya, yb)
    x2 = _mm(mixed, w_out, mode="nn", tm=S, tn=256, out_dtypes=[F32], name="out_proj",
             epilogue=lambda acc, res: (res + acc,), extras=(x,))

    h2 = _rms_fwd(x2, g_mlp, "rms_mlp_fwd")

    def up_epilogue(acc):
        r = jnp.maximum(acc, 0.0)
        return acc, r * r

    u, act = _mm(h2, w_up, mode="nn", tm=S, tn=256, out_dtypes=[F32, BF16], name="mlp_up", epilogue=up_epilogue)
    x3 = _mm(act, w_down, mode="nn", tm=512, tn=512, out_dtypes=[F32], name="mlp_down",
             epilogue=lambda acc, res: (res + acc,), extras=(x2,))

    dx3, dg_final, loss = _final_norm_loss(x3, g_final, tgt)

    du = _mm(dx3, w_down, mode="nt", tm=S, tn=256, out_dtypes=[BF16], name="mlp_down_bwd",
             epilogue=lambda acc, u_t: (acc * (2.0 * jnp.maximum(u_t, 0.0)),), extras=(u,))
    dw_down = _mm(act, dx3, mode="tn", tm=512, tn=512, out_dtypes=[F32], name="dw_down")
    dw_up = _mm(h2, du, mode="tn", tm=D, tn=256, out_dtypes=[F32], name="dw_up")
    dh2 = _mm(du, w_up, mode="nt", tm=512, tn=512, out_dtypes=[F32], name="mlp_up_bwd")
    dx2, dg_mlp = _rms_bwd(x2, g_mlp, dh2, dx3, "rms_mlp_bwd")

    dmixed = _mm(dx2, w_out, mode="nt", tm=S, tn=256, out_dtypes=[F32], name="out_proj_bwd")
    dw_out = _mm(mixed, dx2, mode="tn", tm=D, tn=256, out_dtypes=[F32], name="dw_out")
    dya, dyb, dga, dgb = _gate_bwd(dmixed, proj_g, ya, yb)
    doa = _mm(dya, w_a, mode="nt", tm=S, tn=256, out_dtypes=[BF16], name="branch_a_bwd")
    dw_a = _mm(oa, dya, mode="tn", tm=FOX_W, tn=256, out_dtypes=[F32], name="dw_branch_a")
    dob = _mm(dyb, w_b, mode="nt", tm=S, tn=256, out_dtypes=[F32], name="branch_b_bwd")
    dw_b = _mm(ob, dyb, mode="tn", tm=DIL_OUT, tn=256, out_dtypes=[F32], name="dw_branch_b")

    dqa, dka, dva, dft, dfs = _fox_bwd(proj_a, fcol, frow, lse_a, doa)
    dfa, db = _fox_dscan(dft, dfs, proj_f, b_pad)

    do3, c3 = _dil_combine_bwd(dob, ob, alpha)
    dq3, dk3, dv3 = _dil_bwd(q3, k3, v3, lse3, _to_sub3(do3), _to_sub3(c3))
    dqk = jnp.concatenate([_cat_groups(_from_sub(dq3)), _cat_groups(_from_sub(dk3))], axis=1)
    dqk = _rope(dqk, 0, 2 * DIL_W, tables, backward=True, out_dtype=BF16, name="rope_bwd")
    dvb = _cat_groups(_from_sub(dv3)).astype(BF16)

    dproj = jnp.concatenate([dqa, dka, dva, dqk, dvb, dga, dgb, dfa, jnp.zeros((S, LANES), BF16)], axis=1)
    dw_in_pad = _mm(h1, dproj, mode="tn", tm=D, tn=256, out_dtypes=[F32], name="dw_in")
    dh1 = _mm(dproj, w_in_pad, mode="nt", tm=512, tn=256, out_dtypes=[F32], name="proj_bwd")
    dx, dg_attn = _rms_bwd(x, g_attn, dh1, dx2, "rms_attn_bwd")

    return loss, dx, (dw_in_pad, dw_a, dw_b, dw_out, dw_up, dw_down), (dg_attn, db, dg_mlp, dg_final)


def _to_sub3(t3):
    w = t3.shape[-1]
    out = []
    for g, r in enumerate(DIL_R):
        out.append(t3[g].reshape(S // r, r, w).transpose(1, 0, 2).reshape(S, w))
    return jnp.stack(out, axis=0)


def _cat_groups(t3):
    return jnp.concatenate([t3[g] for g in range(DIL_G)], axis=1)


ANY = pl.BlockSpec(memory_space=pl.ANY)


def _place():
    return lax.axis_index("x"), lax.axis_index("y"), lax.axis_index("c")


def _all_gather(pack):
    def body(x_ref, out_ref, send_sems, recv_sems, local_sem):
        x, y, c = _place()
        me, sibling = (x, y, c), (x, y, 1 - c)
        chips = [(1 - x, y), (x, 1 - y), (1 - x, 1 - y)]

        def slot(px, py, pc):
            return out_ref.at[4 * px + 2 * py + pc]

        def copy(k, block, to, src=None):
            return pltpu.make_async_remote_copy(
                src_ref=slot(*block) if src is None else src, dst_ref=slot(*block),
                send_sem=send_sems.at[k], recv_sem=recv_sems.at[k], device_id=to, device_id_type=MESH)

        mine = pltpu.make_async_copy(x_ref, slot(*me), local_sem)
        mine.start()
        first = [copy(0, me, sibling, src=x_ref)]
        first += [copy(1 + j, me, (*chip, c), src=x_ref) for j, chip in enumerate(chips)]
        for cp in first:
            cp.start()
        passed = [copy(4 + j, (*chip, c), sibling) for j, chip in enumerate(chips)]
        for j, chip in enumerate(chips):
            copy(1 + j, (*chip, c), me).wait_recv()
            passed[j].start()
        copy(0, sibling, me).wait_recv()
        for j, chip in enumerate(chips):
            copy(4 + j, (*chip, 1 - c), me).wait_recv()
        for cp in first + passed:
            cp.wait_send()
        mine.wait()

    return pl.pallas_call(
        body, name="all_gather_weights",
        out_shape=jax.ShapeDtypeStruct((N_DEV,) + pack.shape, pack.dtype),
        in_specs=[ANY], out_specs=ANY,
        scratch_shapes=[pltpu.SemaphoreType.DMA((7,)), pltpu.SemaphoreType.DMA((7,)), pltpu.SemaphoreType.DMA(())],
    )(pack)


def _exchange_sibling(g_pack):
    def body(g_ref, recv_ref, send_sems, recv_sems):
        x, y, c = _place()
        copies = [pltpu.make_async_remote_copy(
            src_ref=g_ref.at[2 * k + (1 - c)], dst_ref=recv_ref.at[k],
            send_sem=send_sems.at[k], recv_sem=recv_sems.at[k],
            device_id=(x, y, 1 - c), device_id_type=MESH) for k in range(4)]
        for cp in copies:
            cp.start()
        for cp in copies:
            cp.wait()

    return pl.pallas_call(
        body, name="grad_exchange_sibling",
        out_shape=jax.ShapeDtypeStruct((4,) + g_pack.shape[1:], g_pack.dtype),
        in_specs=[ANY], out_specs=ANY,
        scratch_shapes=[pltpu.SemaphoreType.DMA((4,)), pltpu.SemaphoreType.DMA((4,))],
    )(g_pack)


def _exchange_chips(partial):
    def body(p_ref, recv_ref, send_sems, recv_sems):
        x, y, c = _place()
        chips = [(1 - x, y), (x, 1 - y), (1 - x, 1 - y)]
        copies = [pltpu.make_async_remote_copy(
            src_ref=p_ref.at[2 * cx + cy], dst_ref=recv_ref.at[j],
            send_sem=send_sems.at[j], recv_sem=recv_sems.at[j],
            device_id=(cx, cy, c), device_id_type=MESH) for j, (cx, cy) in enumerate(chips)]
        for cp in copies:
            cp.start()
        for cp in copies:
            cp.wait()

    return pl.pallas_call(
        body, name="grad_exchange_chips",
        out_shape=jax.ShapeDtypeStruct((3,) + partial.shape[1:], partial.dtype),
        in_specs=[ANY], out_specs=ANY,
        scratch_shapes=[pltpu.SemaphoreType.DMA((3,)), pltpu.SemaphoreType.DMA((3,))],
    )(partial)


def _chip_partial(ids, g_pack, recv1):
    def body(ids_ref, g_ref, r_ref, pb_ref, own_ref):
        s = g_ref[...] + r_ref[...]
        pb_ref[...] = s.astype(BF16)

        @pl.when(pl.program_id(1) == ids_ref[1])
        def _():
            own_ref[...] = s

    grid_spec = pltpu.PrefetchScalarGridSpec(
        num_scalar_prefetch=1, grid=(PACK_R // PACK_TR, 4),
        in_specs=[pl.BlockSpec((None, PACK_TR, LANES), lambda r, k, ids: (2 * k + ids[0], r, 0)),
                  pl.BlockSpec((None, PACK_TR, LANES), lambda r, k, ids: (k, r, 0))],
        out_specs=[pl.BlockSpec((None, PACK_TR, LANES), lambda r, k, ids: (k, r, 0)),
                   pl.BlockSpec((PACK_TR, LANES), lambda r, k, ids: (r, 0))])
    return pl.pallas_call(
        body, name="grad_chip_partial", grid_spec=grid_spec,
        out_shape=[jax.ShapeDtypeStruct((4, PACK_R, LANES), BF16), jax.ShapeDtypeStruct((PACK_R, LANES), F32)],
        compiler_params=_cparams(("parallel", "arbitrary")),
    )(ids, g_pack, recv1)


def _adamw(w, g, m, v):
    m = ADAM_B1 * m + (1.0 - ADAM_B1) * g
    v = ADAM_B2 * v + (1.0 - ADAM_B2) * (g * g)
    m_hat = m / (1.0 - ADAM_B1 ** ADAM_STEP)
    v_hat = v / (1.0 - ADAM_B2 ** ADAM_STEP)
    delta = -ADAM_LR * (m_hat / (jnp.sqrt(v_hat) + ADAM_EPS) + ADAM_WD * w)
    return delta, m, v


def _reduce_adamw(own, recv2, w, m, v):
    def body(own_ref, r_ref, w_ref, m_ref, v_ref, g_ref, d_ref, nm_ref, nv_ref):
        g = own_ref[...]
        for j in range(3):
            g = g + r_ref[j].astype(F32)
        delta, nm, nv = _adamw(w_ref[...], g, m_ref[...], v_ref[...])
        g_ref[...] = g
        d_ref[...] = delta
        nm_ref[...] = nm
        nv_ref[...] = nv

    blk = pl.BlockSpec((PACK_TR, LANES), lambda r: (r, 0))
    return pl.pallas_call(
        body, name="grad_reduce_adamw", grid=(PACK_R // PACK_TR,),
        in_specs=[blk, pl.BlockSpec((3, PACK_TR, LANES), lambda r: (0, r, 0)), blk, blk, blk],
        out_specs=[blk] * 4, out_shape=[jax.ShapeDtypeStruct((PACK_R, LANES), F32)] * 4,
        compiler_params=_cparams(("parallel",)),
    )(own, recv2, w, m, v)


def _small_allreduce_adamw(gvec, w, m, v):
    def body(g_ref, w_ref, m_ref, v_ref, go_ref, d_ref, nm_ref, nv_ref, buf, send_sems, recv_sems):
        x, y, c = _place()
        my_slot = 4 * x + 2 * y + c
        buf[my_slot] = g_ref[...]
        copies = []
        for k in range(1, N_DEV):
            px, py, pc = x ^ (k >> 2), y ^ ((k >> 1) & 1), c ^ (k & 1)
            copies.append(pltpu.make_async_remote_copy(
                src_ref=g_ref, dst_ref=buf.at[my_slot], send_sem=send_sems.at[k - 1], recv_sem=recv_sems.at[k - 1],
                device_id=(px, py, pc), device_id_type=MESH))
        for cp in copies:
            cp.start()
        for k in range(1, N_DEV):
            px, py, pc = x ^ (k >> 2), y ^ ((k >> 1) & 1), c ^ (k & 1)
            pltpu.make_async_remote_copy(
                src_ref=g_ref, dst_ref=buf.at[4 * px + 2 * py + pc], send_sem=send_sems.at[k - 1],
                recv_sem=recv_sems.at[k - 1], device_id=(px, py, pc), device_id_type=MESH).wait_recv()
        for cp in copies:
            cp.wait_send()
        g = buf[0]
        for s in range(1, N_DEV):
            g = g + buf[s]
        delta, nm, nv = _adamw(w_ref[...], g, m_ref[...], v_ref[...])
        go_ref[...] = g
        d_ref[...] = delta
        nm_ref[...] = nm
        nv_ref[...] = nv

    vm = pl.BlockSpec(memory_space=pltpu.VMEM)
    return pl.pallas_call(
        body, name="small_allreduce_adamw",
        in_specs=[vm] * 4, out_specs=[vm] * 4,
        out_shape=[jax.ShapeDtypeStruct((SMALL_R, LANES), F32)] * 4,
        scratch_shapes=[pltpu.VMEM((N_DEV, SMALL_R, LANES), F32),
                        pltpu.SemaphoreType.DMA((N_DEV - 1,)), pltpu.SemaphoreType.DMA((N_DEV - 1,))],
    )(gvec, w, m, v)


def _pack_shards(w_in, w_a, w_b, w_out, w_up, w_down):
    parts = [t.reshape(-1, LANES) for t in (w_in, w_a, w_b, w_out, w_up, w_down)]
    parts.append(jnp.zeros((PACK_R - PACK_OFF[-1], LANES), parts[0].dtype))
    return jnp.concatenate(parts, axis=0)


def _unpack_shard(p):
    shapes = ((1, D, D_IN // N_DEV), (1, FOX_W, D // N_DEV), (1, DIL_OUT, D // N_DEV), (1, D // N_DEV, D),
              (1, D, DFF // N_DEV), (1, DFF // N_DEV, D))
    return tuple(p[PACK_OFF[i]:PACK_OFF[i + 1]].reshape(sh) for i, sh in enumerate(shapes))


def _unpack_gathered(gat):
    def part(i):
        return gat[:, PACK_OFF[i]:PACK_OFF[i + 1]]

    def by_cols(t, rows):
        return t.reshape(N_DEV, rows, -1).transpose(1, 0, 2).reshape(rows, -1)

    w_in = by_cols(part(0), D)
    n_a = 3 * FOX_W
    w_in_pad = jnp.concatenate([w_in[:, :n_a], w_in[:, n_a + FOX_H:], w_in[:, n_a:n_a + FOX_H],
                                jnp.zeros((D, D_PAD - D_IN), w_in.dtype)], axis=1)
    return (w_in_pad, by_cols(part(1), FOX_W), by_cols(part(2), DIL_OUT), part(3).reshape(D, D),
            by_cols(part(4), D), part(5).reshape(DFF, D))


def _pack_grads(dw_in_pad, dw_a, dw_b, dw_out, dw_up, dw_down):
    n_a = 3 * FOX_W
    dw_in = jnp.concatenate([dw_in_pad[:, :n_a], dw_in_pad[:, SEG_F:SEG_F + FOX_H], dw_in_pad[:, n_a:SEG_F]], axis=1)

    def by_cols(t):
        rows = t.shape[0]
        return t.reshape(rows, N_DEV, -1).transpose(1, 0, 2).reshape(N_DEV, -1, LANES)

    parts = [by_cols(dw_in), by_cols(dw_a), by_cols(dw_b), dw_out.reshape(N_DEV, -1, LANES),
             by_cols(dw_up), dw_down.reshape(N_DEV, -1, LANES)]
    parts.append(jnp.zeros((N_DEV, PACK_R - PACK_OFF[-1], LANES), F32))
    return jnp.concatenate(parts, axis=1)


def _pack_small(g_attn, b, g_mlp, g_final):
    b_rows = jnp.pad(b, ((0, 7), (0, LANES - b.shape[1])))
    return jnp.concatenate([g_attn.reshape(8, LANES), g_mlp.reshape(8, LANES), g_final.reshape(8, LANES), b_rows], axis=0)


def _unpack_small(p):
    return (p[0:8].reshape(1, D), p[24:25, :FOX_H], p[8:16].reshape(1, D), p[16:24].reshape(D))


def kernel(x, norm_attn_g, w_in, b_forget, w_branch_a, w_branch_b, w_out, norm_mlp_g, w_up, w_down, norm_final_g, loss_target, m_norm_attn_g, m_w_in, m_b_forget, m_w_branch_a, m_w_branch_b, m_w_out, m_norm_mlp_g, m_w_up, m_w_down, m_norm_final_g, v_norm_attn_g, v_w_in, v_b_forget, v_w_branch_a, v_w_branch_b, v_w_out, v_norm_mlp_g, v_w_up, v_w_down, v_norm_final_g):
    cx, cy, cc = _place()
    ids = jnp.stack([cc, 2 * cx + cy]).astype(jnp.int32)

    w_pack = _pack_shards(w_in, w_branch_a, w_branch_b, w_out, w_up, w_down)
    m_pack = _pack_shards(m_w_in, m_w_branch_a, m_w_branch_b, m_w_out, m_w_up, m_w_down)
    v_pack = _pack_shards(v_w_in, v_w_branch_a, v_w_branch_b, v_w_out, v_w_up, v_w_down)

    gathered = _all_gather(w_pack.astype(BF16))
    w_in_pad, w_a, w_b, w_o, w_u, w_d = _unpack_gathered(gathered)

    b_pad = jnp.pad(b_forget, ((0, 0), (0, LANES - FOX_H)))
    loss_row, dx, dws, dsmall = _local_step(
        x[0], loss_target[0], norm_attn_g, norm_mlp_g, norm_final_g.reshape(1, D), b_pad,
        w_in_pad, w_a, w_b, w_o, w_u, w_d)
    loss = lax.psum(loss_row[0, 0], ("x", "y", "c"))

    g_pack = _pack_grads(*dws)
    recv1 = _exchange_sibling(g_pack)
    partial, own = _chip_partial(ids, g_pack, recv1)
    recv2 = _exchange_chips(partial)
    g_sh, d_sh, m_sh, v_sh = _reduce_adamw(own, recv2, w_pack, m_pack, v_pack)

    dg_attn, db, dg_mlp, dg_final = dsmall
    small = _small_allreduce_adamw(
        _pack_small(dg_attn, db, dg_mlp, dg_final),
        _pack_small(norm_attn_g, b_forget, norm_mlp_g, norm_final_g.reshape(1, D)),
        _pack_small(m_norm_attn_g, m_b_forget, m_norm_mlp_g, m_norm_final_g.reshape(1, D)),
        _pack_small(v_norm_attn_g, v_b_forget, v_norm_mlp_g, v_norm_final_g.reshape(1, D)))

    outs = [loss, dx[None]]
    for big, sm in zip((g_sh, d_sh, m_sh, v_sh), small):
        b_in, b_a, b_b, b_out, b_up, b_down = _unpack_shard(big)
        s_attn, s_b, s_mlp, s_final = _unpack_small(sm)
        outs += [s_attn, b_in, s_b, b_a, b_b, b_out, s_mlp, b_up, b_down, s_final]
    return tuple(outs)
```

```python
import functools

import jax
import jax.numpy as jnp
from jax import lax
from jax.experimental import pallas as pl
from jax.experimental.pallas import tpu as pltpu

F32 = jnp.float32
BF16 = jnp.bfloat16
MESH = pl.DeviceIdType.MESH

S = 2048
D = 1024
HD = 64
FOX_H = 8
FOX_W = FOX_H * HD
DIL_HG = 4
DIL_G = 3
DIL_W = DIL_G * DIL_HG * HD
DIL_OUT = DIL_HG * HD
DIL_BLK = 128
DIL_R = (1, 4, 16)
DFF = 4 * D
D_IN = 3 * FOX_W + FOX_H + 3 * DIL_W + 2 * D
EPS = 1e-6
NEG_INF = -1e30
SCALE = HD ** -0.5
ROPE_THETA = 500000.0
ROPE_DIM = HD // 4
N_DEV = 8

ADAM_LR = 0.001
ADAM_B1 = 0.9
ADAM_B2 = 0.999
ADAM_EPS = 1e-08
ADAM_WD = 0.01
ADAM_STEP = 10

LANES = 128
CB = 256
SEG_A = 0
SEG_B = 3 * FOX_W
SEG_Z = SEG_B + 3 * DIL_W
SEG_G = SEG_Z + CB
SEG_F = SEG_G + 2 * D
D_PAD = SEG_F + CB
SMALL_R = 32

VMEM_MB = 56


def _cparams(dims=None, vmem_mb=VMEM_MB, **kw):
    return pltpu.CompilerParams(dimension_semantics=dims, vmem_limit_bytes=vmem_mb << 20, **kw)


_NN = (((1,), (0,)), ((), ()))
_NT = (((1,), (1,)), ((), ()))
_TN = (((0,), (0,)), ((), ()))


def _dot(a, b, dims):
    return lax.dot_general(a.astype(BF16), b.astype(BF16), dims, preferred_element_type=F32)


def _mm(a, b, *, mode, tm, tn, out_dtypes, name, n=None, k=None, a_off=0, b_off=0,
        b_sharded=False, out_sharded=False, epilogue=None, extras=()):
    n_sh = b.shape[-1] if b_sharded else None
    if mode == "nn":
        m = a.shape[0]
        k = k or a.shape[1]
        a_spec = pl.BlockSpec((tm, k), lambda i, j: (i, a_off))
        if b_sharded:
            assert tn == n_sh
            n = N_DEV * n_sh
            b_spec = pl.BlockSpec((None, k, tn), lambda i, j: (j, 0, 0))
        else:
            n = n or b.shape[1]
            b_spec = pl.BlockSpec((k, tn), lambda i, j: (0, j + b_off))
        dims = _NN
    elif mode == "nt":
        m = a.shape[0]
        k = k or a.shape[1]
        a_spec = pl.BlockSpec((tm, k), lambda i, j: (i, a_off))
        if b_sharded:
            n = b.shape[1]
            b_spec = pl.BlockSpec((N_DEV, tn, n_sh), lambda i, j: (0, j, 0))
        else:
            n = n or b.shape[0]
            b_spec = pl.BlockSpec((tn, k), lambda i, j: (j + b_off, 0))
        dims = _NT
    else:
        k, m = a.shape
        n = n or b.shape[1]
        a_spec = pl.BlockSpec((k, tm), lambda i, j: (0, i))
        b_spec = pl.BlockSpec((k, tn), lambda i, j: (0, j + b_off))
        dims = _TN
    assert m % tm == 0 and n % tn == 0, (name, m, n, tm, tn)
    n_extra = len(extras)
    tile = pl.BlockSpec((tm, tn), lambda i, j: (i, j))
    if out_sharded:
        assert mode == "tn" and n // tn == N_DEV
        out_spec = pl.BlockSpec((None, tm, tn), lambda i, j: (j, i, 0))
        out_shape = (N_DEV, m, tn)
    else:
        out_spec, out_shape = tile, (m, n)

    def body(a_ref, b_ref, *refs):
        if mode == "nt" and b_sharded:
            acc = _dot(a_ref[:, 0:n_sh], b_ref[0], dims)
            for p in range(1, N_DEV):
                acc = acc + _dot(a_ref[:, p * n_sh:(p + 1) * n_sh], b_ref[p], dims)
        else:
            acc = _dot(a_ref[...], b_ref[...], dims)
        ex = [r[...] for r in refs[:n_extra]]
        outs = epilogue(acc, *ex) if epilogue is not None else (acc,)
        for o_ref, o in zip(refs[n_extra:], outs):
            o_ref[...] = o.astype(o_ref.dtype)

    res = pl.pallas_call(
        body, name=name, grid=(m // tm, n // tn),
        in_specs=[a_spec, b_spec] + [tile] * n_extra,
        out_specs=[out_spec] * len(out_dtypes),
        out_shape=[jax.ShapeDtypeStruct(out_shape, dt) for dt in out_dtypes],
        compiler_params=_cparams(("parallel", "parallel")),
    )(a, b, *extras)
    return res if len(out_dtypes) > 1 else res[0]


ROW_T = 256


def _rms_fwd(x, g, name):
    def body(x_ref, g_ref, o_ref):
        x = x_ref[...]
        r = lax.rsqrt(jnp.mean(x * x, axis=-1, keepdims=True) + EPS)
        o_ref[...] = ((x * r) * g_ref[...]).astype(BF16)

    return pl.pallas_call(
        body, name=name, grid=(S // ROW_T,),
        in_specs=[pl.BlockSpec((ROW_T, D), lambda i: (i, 0)), pl.BlockSpec((1, D), lambda i: (0, 0))],
        out_specs=pl.BlockSpec((ROW_T, D), lambda i: (i, 0)),
        out_shape=jax.ShapeDtypeStruct((S, D), BF16),
        compiler_params=_cparams(("parallel",)),
    )(x, g)


def _rms_bwd(x, g, dh, dres, name):
    def body(x_ref, g_ref, dh_ref, dres_ref, dx_ref, dg_ref):
        @pl.when(pl.program_id(0) == 0)
        def _():
            dg_ref[...] = jnp.zeros_like(dg_ref)

        x = x_ref[...]
        dh = dh_ref[...]
        r = lax.rsqrt(jnp.mean(x * x, axis=-1, keepdims=True) + EPS)
        xn = x * r
        dhn = dh * g_ref[...]
        dx_ref[...] = dres_ref[...] + r * (dhn - xn * jnp.mean(dhn * xn, axis=-1, keepdims=True))
        dg_ref[...] += jnp.sum(dh * xn, axis=0, keepdims=True)

    row = pl.BlockSpec((ROW_T, D), lambda i: (i, 0))
    vec = pl.BlockSpec((1, D), lambda i: (0, 0))
    return pl.pallas_call(
        body, name=name, grid=(S // ROW_T,),
        in_specs=[row, vec, row, row], out_specs=[row, vec],
        out_shape=[jax.ShapeDtypeStruct((S, D), F32), jax.ShapeDtypeStruct((1, D), F32)],
        compiler_params=_cparams(("arbitrary",)),
    )(x, g, dh, dres)


def _final_norm_loss(x, g, tgt):
    def body(x_ref, g_ref, t_ref, dx_ref, dg_ref, loss_ref):
        @pl.when(pl.program_id(0) == 0)
        def _():
            dg_ref[...] = jnp.zeros_like(dg_ref)
            loss_ref[...] = jnp.zeros_like(loss_ref)

        x = x_ref[...]
        g = g_ref[...]
        r = lax.rsqrt(jnp.mean(x * x, axis=-1, keepdims=True) + EPS)
        xn = x * r
        err = xn * g - t_ref[...]
        row_loss = jnp.mean(err * err, axis=-1, keepdims=True)
        loss_ref[...] += 0.5 * jnp.sum(row_loss, axis=0, keepdims=True) * jnp.ones((1, LANES), F32)
        dy = err * (1.0 / D)
        dyn = dy * g
        dx_ref[...] = r * (dyn - xn * jnp.mean(dyn * xn, axis=-1, keepdims=True))
        dg_ref[...] += jnp.sum(dy * xn, axis=0, keepdims=True)

    row = pl.BlockSpec((ROW_T, D), lambda i: (i, 0))
    vec = pl.BlockSpec((1, D), lambda i: (0, 0))
    return pl.pallas_call(
        body, name="final_norm_loss", grid=(S // ROW_T,),
        in_specs=[row, vec, row],
        out_specs=[row, vec, pl.BlockSpec((1, LANES), lambda i: (0, 0))],
        out_shape=[jax.ShapeDtypeStruct((S, D), F32), jax.ShapeDtypeStruct((1, D), F32),
                   jax.ShapeDtypeStruct((1, LANES), F32)],
        compiler_params=_cparams(("arbitrary",)),
    )(x, g, tgt)


def _sigmoid(z):
    return 1.0 / (1.0 + jnp.exp(-z))


def _gate_fwd(proj_g, ya, yb):
    def body(ga_ref, gb_ref, ya_ref, yb_ref, o_ref):
        o_ref[...] = (_sigmoid(ga_ref[...]) * ya_ref[...] + _sigmoid(gb_ref[...]) * yb_ref[...]).astype(BF16)

    row = pl.BlockSpec((ROW_T, D), lambda i: (i, 0))
    return pl.pallas_call(
        body, name="gate_fwd", grid=(S // ROW_T,),
        in_specs=[row, pl.BlockSpec((ROW_T, D), lambda i: (i, 1)), row, row],
        out_specs=row, out_shape=jax.ShapeDtypeStruct((S, D), BF16),
        compiler_params=_cparams(("parallel",)),
    )(proj_g, proj_g, ya, yb)


GATE_NB = 2 * D // CB


def _gate_bwd(dmixed, proj_g, ya, yb):
    half = GATE_NB // 2

    def body(dm_ref, g_ref, ya_ref, yb_ref, dy_ref, dg_ref):
        dm = dm_ref[...]
        s = _sigmoid(g_ref[...])
        y = jnp.where(pl.program_id(1) < half, ya_ref[...], yb_ref[...])
        dy_ref[...] = (dm * s).astype(BF16)
        dg_ref[...] = (dm * y * (s * (1.0 - s))).astype(BF16)

    blk = pl.BlockSpec((ROW_T, CB), lambda i, j: (i, j))
    wrapped = pl.BlockSpec((ROW_T, CB), lambda i, j: (i, j % half))
    return pl.pallas_call(
        body, name="gate_bwd", grid=(S // ROW_T, GATE_NB),
        in_specs=[wrapped, blk, wrapped, wrapped],
        out_specs=[blk, pl.BlockSpec((ROW_T, CB), lambda i, j: (i, j + SEG_G // CB))],
        out_shape=[jax.ShapeDtypeStruct((S, 2 * D), BF16), jax.ShapeDtypeStruct((S, D_PAD), BF16)],
        compiler_params=_cparams(("parallel", "parallel")),
    )(dmixed, proj_g, ya, yb)


FOX_TQ = 256
ANY = pl.BlockSpec(memory_space=pl.ANY)


def _scan_rows(x, reverse):
    n = x.shape[0]
    row = lax.broadcasted_iota(jnp.int32, x.shape, 0)
    k = 1
    while k < n:
        if reverse:
            x = x + jnp.where(row < n - k, pltpu.roll(x, n - k, 0), 0.0)
        else:
            x = x + jnp.where(row >= k, pltpu.roll(x, k, 0), 0.0)
        k *= 2
    return x


def _fox_scan(proj_f, b_pad):
    def body(f_ref, b_ref, fc_ref, fr_ref):
        z = f_ref[...] + b_ref[...]
        lf = jnp.minimum(z, 0.0) - jnp.log1p(jnp.exp(-jnp.abs(z)))
        f = _scan_rows(lf, reverse=False)
        fc_ref[...] = f
        fr_ref[...] = f.T

    return pl.pallas_call(
        body, name="fox_scan", grid=(1,),
        in_specs=[pl.BlockSpec((S, LANES), lambda i: (0, 0)), pl.BlockSpec((1, LANES), lambda i: (0, 0))],
        out_specs=[pl.BlockSpec((S, LANES), lambda i: (0, 0)), pl.BlockSpec((LANES, S), lambda i: (0, 0))],
        out_shape=[jax.ShapeDtypeStruct((S, LANES), F32), jax.ShapeDtypeStruct((LANES, S), F32)],
        compiler_params=_cparams(("arbitrary",)),
    )(proj_f, b_pad)


def _fox_dscan(dft, dfs, proj_f, b_pad, dproj):
    def body(dft_ref, dfs_ref, f_ref, b_ref, _, dfa_ref, db_ref):
        dfs_pad = jnp.concatenate([dfs_ref[...], jnp.zeros((LANES - FOX_H, S), F32)], axis=0)
        dlf = _scan_rows(dft_ref[...] + dfs_pad.T, reverse=True)
        z = f_ref[...] + b_ref[...]
        lane = lax.broadcasted_iota(jnp.int32, (S, LANES), 1)
        dfa = jnp.where(lane < FOX_H, dlf / (1.0 + jnp.exp(z)), 0.0)
        dfa_ref[:, :LANES] = dfa.astype(BF16)
        dfa_ref[:, LANES:] = jnp.zeros((S, CB - LANES), BF16)
        db_ref[...] = jnp.sum(dfa, axis=0, keepdims=True)

    return pl.pallas_call(
        body, name="fox_dscan", grid=(1,),
        in_specs=[pl.BlockSpec((S, LANES), lambda i: (0, 0)), pl.BlockSpec((FOX_H, S), lambda i: (0, 0)),
                  pl.BlockSpec((S, LANES), lambda i: (0, 0)), pl.BlockSpec((1, LANES), lambda i: (0, 0)), ANY],
        out_specs=[pl.BlockSpec((S, CB), lambda i: (0, SEG_F // CB)), pl.BlockSpec((1, LANES), lambda i: (0, 0))],
        out_shape=[jax.ShapeDtypeStruct((S, D_PAD), BF16), jax.ShapeDtypeStruct((1, LANES), F32)],
        input_output_aliases={4: 0},
        compiler_params=_cparams(("arbitrary",)),
    )(dft, dfs, proj_f, b_pad, dproj)


def _fox_logits(q, k, ft, fs, causal):
    s = _dot(q, k, _NT) * SCALE
    s = (s + ft) - fs
    return jnp.where(causal, s, NEG_INF)


def _fox_causal(qi):
    row = qi * FOX_TQ + lax.broadcasted_iota(jnp.int32, (FOX_TQ, S), 0)
    col = lax.broadcasted_iota(jnp.int32, (FOX_TQ, S), 1)
    return col <= row


def _fox_fwd(proj_a, fcol, frow):
    def body(q_ref, k_ref, v_ref, ft_ref, fs_ref, o_ref, lse_ref):
        causal = _fox_causal(pl.program_id(0))
        ft = ft_ref[...]
        fs = fs_ref[...]
        lane = lax.broadcasted_iota(jnp.int32, (FOX_TQ, LANES), 1)
        lse = jnp.zeros((FOX_TQ, LANES), F32)
        for h in range(FOX_H):
            sl = slice(h * HD, (h + 1) * HD)
            s = _fox_logits(q_ref[:, sl], k_ref[:, sl], ft[:, h:h + 1], fs[h:h + 1, :], causal)
            m = jnp.max(s, axis=-1, keepdims=True)
            p = jnp.exp(s - m)
            l = jnp.sum(p, axis=-1, keepdims=True)
            o_ref[:, sl] = _dot(p / l, v_ref[:, sl], _NN)
            lse = jnp.where(lane == h, m + jnp.log(l), lse)
        lse_ref[...] = lse

    return pl.pallas_call(
        body, name="fox_fwd", grid=(S // FOX_TQ,),
        in_specs=[pl.BlockSpec((FOX_TQ, FOX_W), lambda i: (i, 0)),
                  pl.BlockSpec((S, FOX_W), lambda i: (0, 1)),
                  pl.BlockSpec((S, FOX_W), lambda i: (0, 2)),
                  pl.BlockSpec((FOX_TQ, LANES), lambda i: (i, 0)),
                  pl.BlockSpec((FOX_H, S), lambda i: (0, 0))],
        out_specs=[pl.BlockSpec((FOX_TQ, FOX_W), lambda i: (i, 0)),
                   pl.BlockSpec((FOX_TQ, LANES), lambda i: (i, 0))],
        out_shape=[jax.ShapeDtypeStruct((S, FOX_W), F32), jax.ShapeDtypeStruct((S, LANES), F32)],
        compiler_params=_cparams(("parallel",)),
    )(proj_a, proj_a, proj_a, fcol, frow)


def _fox_bwd(proj_a, fcol, frow, lse, do, dproj):
    n_q = S // FOX_TQ

    def body(q_ref, k_ref, v_ref, ft_ref, fs_ref, lse_ref, do_ref, _,
             dqkv_ref, dft_ref, dfs_ref, dk_acc, dv_acc):
        t = pl.program_id(0)

        @pl.when(t == 0)
        def _():
            dk_acc[...] = jnp.zeros_like(dk_acc)
            dv_acc[...] = jnp.zeros_like(dv_acc)
            dfs_ref[...] = jnp.zeros_like(dfs_ref)

        @pl.when(t < n_q)
        def _():
            causal = _fox_causal(t)
            ft = ft_ref[...]
            fs = fs_ref[...]
            lse = lse_ref[...]
            lane = lax.broadcasted_iota(jnp.int32, (FOX_TQ, LANES), 1)
            dft = jnp.zeros((FOX_TQ, LANES), F32)
            for h in range(FOX_H):
                sl = slice(h * HD, (h + 1) * HD)
                q = q_ref[:, sl]
                k = k_ref[:, sl]
                do_h = do_ref[:, sl]
                s = _fox_logits(q, k, ft[:, h:h + 1], fs[h:h + 1, :], causal)
                p = jnp.exp(s - lse[:, h:h + 1])
                dp = _dot(do_h, v_ref[:, sl], _NT)
                ds = p * (dp - jnp.sum(dp * p, axis=-1, keepdims=True))
                dft = jnp.where(lane == h, jnp.sum(ds, axis=-1, keepdims=True), dft)
                dfs_ref[h:h + 1, :] -= jnp.sum(ds, axis=0, keepdims=True)
                dsb = (ds * SCALE).astype(BF16)
                dqkv_ref[:, sl] = _dot(dsb, k, _NN).astype(BF16)
                dk_acc[:, sl] += _dot(dsb, q, _TN)
                dv_acc[:, sl] += _dot(p, do_h, _TN)
            dft_ref[...] = dft

        rows = pl.ds(pl.multiple_of((t % n_q) * FOX_TQ, FOX_TQ), FOX_TQ)

        @pl.when((t >= n_q) & (t < 2 * n_q))
        def _():
            dqkv_ref[...] = dk_acc[rows, :].astype(BF16)

        @pl.when(t >= 2 * n_q)
        def _():
            dqkv_ref[...] = dv_acc[rows, :].astype(BF16)

    def qidx(t):
        return jnp.minimum(t, n_q - 1)

    qblk = pl.BlockSpec((FOX_TQ, FOX_W), lambda t: (qidx(t), 0))
    lane_blk = pl.BlockSpec((FOX_TQ, LANES), lambda t: (qidx(t), 0))
    return pl.pallas_call(
        body, name="fox_bwd", grid=(3 * n_q,),
        in_specs=[qblk, pl.BlockSpec((S, FOX_W), lambda t: (0, 1)), pl.BlockSpec((S, FOX_W), lambda t: (0, 2)),
                  lane_blk, pl.BlockSpec((FOX_H, S), lambda t: (0, 0)), lane_blk, qblk, ANY],
        out_specs=[pl.BlockSpec((FOX_TQ, FOX_W), lambda t: (t % n_q, t // n_q)), lane_blk,
                   pl.BlockSpec((FOX_H, S), lambda t: (0, 0))],
        out_shape=[jax.ShapeDtypeStruct((S, D_PAD), BF16),
                   jax.ShapeDtypeStruct((S, LANES), F32), jax.ShapeDtypeStruct((FOX_H, S), F32)],
        scratch_shapes=[pltpu.VMEM((S, FOX_W), F32), pltpu.VMEM((S, FOX_W), F32)],
        input_output_aliases={7: 0},
        compiler_params=_cparams(("arbitrary",)),
    )(proj_a, proj_a, proj_a, fcol, frow, lse, do, dproj)


def _rope_tables():
    half = ROPE_DIM // 2
    inv_freq = jnp.power(jnp.float32(ROPE_THETA), -jnp.arange(half, dtype=F32) * 2.0 / ROPE_DIM)
    ang = jnp.arange(S).astype(F32)[:, None] * inv_freq[None, :]
    cos, sin = jnp.cos(ang), jnp.sin(ang)
    z = jnp.zeros((S, half), F32)
    rest = HD - ROPE_DIM
    t_self = jnp.concatenate([cos, cos, jnp.ones((S, rest), F32)], axis=1)
    t_up = jnp.concatenate([-sin, z, jnp.zeros((S, rest), F32)], axis=1)
    t_dn = jnp.concatenate([z, sin, jnp.zeros((S, rest), F32)], axis=1)
    rep = LANES // HD
    return tuple(jnp.tile(t, (1, rep)) for t in (t_self, t_up, t_dn))


def _residue_pieces(r):
    if r == 1:
        return [(slice(i, i + 512), slice(i, i + 512)) for i in range(0, S, 512)]
    n = S // r
    return [(pl.ds(j, n, stride=r), slice(j * n, (j + 1) * n)) for j in range(r)]


def _rope_apply(x, a, u, d, backward):
    half = ROPE_DIM // 2
    if backward:
        return x * a + pltpu.roll(x * u, half, 1) + pltpu.roll(x * d, LANES - half, 1)
    return x * a + pltpu.roll(x, LANES - half, 1) * u + pltpu.roll(x, half, 1) * d


LB_PER_CB = CB // LANES
ROPE_NB = 3 * DIL_G * LB_PER_CB


def _rope_split(proj_b, tables):
    def body(x_ref, a_ref, u_ref, d_ref, o_ref):
        j = pl.program_id(0)
        g = (j // LB_PER_CB) % DIL_G
        rotate = j < 2 * DIL_G * LB_PER_CB
        for gs in range(DIL_G):
            @pl.when(g == gs)
            def _(gs=gs):
                for tok, sub in _residue_pieces(DIL_R[gs]):
                    x = x_ref[tok, :]
                    y = _rope_apply(x, a_ref[tok, :], u_ref[tok, :], d_ref[tok, :], False)
                    o_ref[sub, :] = jnp.where(rotate, y, x).astype(BF16)

    tab = pl.BlockSpec((S, LANES), lambda j: (0, 0))
    return pl.pallas_call(
        body, name="rope_split", grid=(ROPE_NB,),
        in_specs=[pl.BlockSpec((S, LANES), lambda j: (0, j)), tab, tab, tab],
        out_specs=pl.BlockSpec((None, S, LANES), lambda j: (j // LB_PER_CB, 0, j % LB_PER_CB)),
        out_shape=jax.ShapeDtypeStruct((3 * DIL_G, S, CB), BF16),
        compiler_params=_cparams(("parallel",)),
    )(proj_b, *tables)


def _rope_merge_bwd(dq3, dk3, dv3, tables, dproj):
    per_tensor = DIL_G * LB_PER_CB

    def body(dq_ref, dk_ref, dv_ref, a_ref, u_ref, d_ref, _, o_ref, tmp):
        j = pl.program_id(0)
        t = j // per_tensor
        g = (j // LB_PER_CB) % DIL_G
        for gs in range(DIL_G):
            @pl.when((g == gs) & (j < ROPE_NB))
            def _(gs=gs):
                for tok, sub in _residue_pieces(DIL_R[gs]):
                    x = jnp.where(t == 0, dq_ref[sub, :], jnp.where(t == 1, dk_ref[sub, :], dv_ref[sub, :]))
                    y = _rope_apply(x, a_ref[tok, :], u_ref[tok, :], d_ref[tok, :], True)
                    tmp[tok, :] = jnp.where(t < 2, y, x)
                o_ref[...] = tmp[...].astype(BF16)

        @pl.when(j >= ROPE_NB)
        def _():
            o_ref[...] = jnp.zeros_like(o_ref)

    def src_spec(t):
        def index(j):
            jb = j // LB_PER_CB - t * DIL_G
            lane_blk = jnp.where(jb < 0, 0, jnp.where(jb >= DIL_G, LB_PER_CB - 1, j % LB_PER_CB))
            return jnp.clip(jb, 0, DIL_G - 1), 0, lane_blk
        return pl.BlockSpec((None, S, LANES), index)

    tab = pl.BlockSpec((S, LANES), lambda j: (0, 0))
    return pl.pallas_call(
        body, name="rope_merge_bwd", grid=(ROPE_NB + LB_PER_CB,),
        in_specs=[src_spec(0), src_spec(1), src_spec(2), tab, tab, tab, ANY],
        out_specs=pl.BlockSpec((S, LANES), lambda j: (0, j + SEG_B // LANES)),
        out_shape=jax.ShapeDtypeStruct((S, D_PAD), BF16),
        scratch_shapes=[pltpu.VMEM((S, LANES), F32)],
        input_output_aliases={6: 0},
        compiler_params=_cparams(("arbitrary",)),
    )(dq3, dk3, dv3, *tables, dproj)


DIL_NB = S // DIL_BLK


def _dil_mask(g, n):
    blocks_per_seq = jnp.right_shift(DIL_NB, 2 * g)
    has_prev = jnp.bitwise_and(n, blocks_per_seq - 1) != 0
    a = lax.broadcasted_iota(jnp.int32, (DIL_BLK, 2 * DIL_BLK), 0)
    kk = lax.broadcasted_iota(jnp.int32, (DIL_BLK, 2 * DIL_BLK), 1)
    diff = DIL_BLK + a - kk
    return (diff >= 0) & (diff <= DIL_BLK) & ((kk >= DIL_BLK) | has_prev)


def _dil_specs():
    def cur(t):
        return pl.BlockSpec((None, DIL_BLK, DIL_OUT), lambda g, n: (t * DIL_G + g, n, 0))

    def prev(t):
        return pl.BlockSpec((None, DIL_BLK, DIL_OUT), lambda g, n: (t * DIL_G + g, jnp.maximum(n - 1, 0), 0))

    lane_blk = pl.BlockSpec((None, DIL_BLK, LANES), lambda g, n: (g, n, 0))
    return cur, prev, lane_blk


def _dil_fwd(qkv3):
    def body(q_ref, kc_ref, kp_ref, vc_ref, vp_ref, o_ref, lse_ref):
        ok = _dil_mask(pl.program_id(0), pl.program_id(1))
        k2 = jnp.concatenate([kp_ref[...], kc_ref[...]], axis=0)
        v2 = jnp.concatenate([vp_ref[...], vc_ref[...]], axis=0)
        lane = lax.broadcasted_iota(jnp.int32, (DIL_BLK, LANES), 1)
        lse = jnp.zeros((DIL_BLK, LANES), F32)
        for h in range(DIL_HG):
            sl = slice(h * HD, (h + 1) * HD)
            s = jnp.where(ok, _dot(q_ref[:, sl], k2[:, sl], _NT) * SCALE, NEG_INF)
            m = jnp.max(s, axis=-1, keepdims=True)
            p = jnp.exp(s - m)
            l = jnp.sum(p, axis=-1, keepdims=True)
            o_ref[:, sl] = _dot(p / l, v2[:, sl], _NN)
            lse = jnp.where(lane == h, m + jnp.log(l), lse)
        lse_ref[...] = lse

    cur, prev, lane_blk = _dil_specs()
    out_blk = pl.BlockSpec((None, DIL_BLK, DIL_OUT), lambda g, n: (g, n, 0))
    return pl.pallas_call(
        body, name="dil_fwd", grid=(DIL_G, DIL_NB),
        in_specs=[cur(0), cur(1), prev(1), cur(2), prev(2)], out_specs=[out_blk, lane_blk],
        out_shape=[jax.ShapeDtypeStruct((DIL_G, S, DIL_OUT), F32), jax.ShapeDtypeStruct((DIL_G, S, LANES), F32)],
        compiler_params=_cparams(("parallel", "parallel")),
    )(qkv3, qkv3, qkv3, qkv3, qkv3)


def _dil_bwd(qkv3, lse3, do3, c3):
    def body(q_ref, kc_ref, kp_ref, vc_ref, vp_ref, lse_ref, do_ref, c_ref, dq_ref, dk_ref, dv_ref):
        n = pl.program_id(1)

        @pl.when(n == 0)
        def _():
            dk_ref[...] = jnp.zeros_like(dk_ref)
            dv_ref[...] = jnp.zeros_like(dv_ref)

        ok = _dil_mask(pl.program_id(0), n)
        k2 = jnp.concatenate([kp_ref[...], kc_ref[...]], axis=0)
        v2 = jnp.concatenate([vp_ref[...], vc_ref[...]], axis=0)
        lse = lse_ref[...]
        c = c_ref[...]
        cur_rows = pl.ds(pl.multiple_of(n * DIL_BLK, DIL_BLK), DIL_BLK)
        prev_rows = pl.ds(pl.multiple_of(jnp.maximum(n - 1, 0) * DIL_BLK, DIL_BLK), DIL_BLK)
        for h in range(DIL_HG):
            sl = slice(h * HD, (h + 1) * HD)
            q = q_ref[:, sl]
            do_h = do_ref[:, sl]
            s = jnp.where(ok, _dot(q, k2[:, sl], _NT) * SCALE, NEG_INF)
            p = jnp.exp(s - lse[:, h:h + 1])
            dp = _dot(do_h, v2[:, sl], _NT)
            ds = p * (dp - c[:, h:h + 1])
            dsb = (ds * SCALE).astype(BF16)
            dq_ref[:, sl] = _dot(dsb, k2[:, sl], _NN)
            dk2 = _dot(dsb, q, _TN)
            dv2 = _dot(p, do_h, _TN)
            dk_ref[prev_rows, sl] += dk2[:DIL_BLK]
            dk_ref[cur_rows, sl] += dk2[DIL_BLK:]
            dv_ref[prev_rows, sl] += dv2[:DIL_BLK]
            dv_ref[cur_rows, sl] += dv2[DIL_BLK:]

    cur, prev, lane_blk = _dil_specs()
    blk = pl.BlockSpec((None, DIL_BLK, DIL_OUT), lambda g, n: (g, n, 0))
    whole = pl.BlockSpec((None, S, DIL_OUT), lambda g, n: (g, 0, 0))
    return pl.pallas_call(
        body, name="dil_bwd", grid=(DIL_G, DIL_NB),
        in_specs=[cur(0), cur(1), prev(1), cur(2), prev(2), lane_blk, blk, lane_blk],
        out_specs=[blk, whole, whole],
        out_shape=[jax.ShapeDtypeStruct((DIL_G, S, DIL_OUT), F32)] * 3,
        compiler_params=_cparams(("parallel", "arbitrary")),
    )(qkv3, qkv3, qkv3, qkv3, qkv3, lse3, do3, c3)


COMB_T = 256


def _dil_combine(o3, lse3):
    heads_per_lb = LANES // HD

    def body(o_ref, lse_ref, ob_ref, al_ref, *scratch):
        o_tok = [scratch[LB_PER_CB * gg:LB_PER_CB * (gg + 1)] for gg in range(DIL_G)]
        lse_tok = scratch[LB_PER_CB * DIL_G:]
        g = pl.program_id(0)
        for gs in range(DIL_G):
            @pl.when(g == gs)
            def _(gs=gs):
                for tok, sub in _residue_pieces(DIL_R[gs]):
                    for hf in range(LB_PER_CB):
                        o_tok[gs][hf][tok, :] = o_ref[sub, hf * LANES:(hf + 1) * LANES]
                    lse_tok[gs][tok, :] = lse_ref[sub, :]

        @pl.when(g == DIL_G - 1)
        def _():
            def chunk(i, carry):
                rows = pl.ds(pl.multiple_of(i * COMB_T, COMB_T), COMB_T)
                lse = [lse_tok[gg][rows, :] for gg in range(DIL_G)]
                m = jnp.maximum(jnp.maximum(lse[0], lse[1]), lse[2])
                e = [jnp.exp(lse[gg] - m) for gg in range(DIL_G)]
                den = (e[0] + e[1]) + e[2]
                al = [e[gg] / den for gg in range(DIL_G)]
                for gg in range(DIL_G):
                    al_ref[gg, rows, :] = al[gg]
                for h in range(DIL_HG):
                    hf, sl = h // heads_per_lb, slice((h % heads_per_lb) * HD, (h % heads_per_lb + 1) * HD)
                    acc = al[0][:, h:h + 1] * o_tok[0][hf][rows, sl]
                    for gg in range(1, DIL_G):
                        acc = acc + al[gg][:, h:h + 1] * o_tok[gg][hf][rows, sl]
                    ob_ref[rows, h * HD:(h + 1) * HD] = acc
                return carry

            lax.fori_loop(0, S // COMB_T, chunk, 0)

    return pl.pallas_call(
        body, name="dil_combine", grid=(DIL_G,),
        in_specs=[pl.BlockSpec((None, S, DIL_OUT), lambda g: (g, 0, 0)),
                  pl.BlockSpec((None, S, LANES), lambda g: (g, 0, 0))],
        out_specs=[pl.BlockSpec((S, DIL_OUT), lambda g: (0, 0)),
                   pl.BlockSpec((DIL_G, S, LANES), lambda g: (0, 0, 0))],
        out_shape=[jax.ShapeDtypeStruct((S, DIL_OUT), F32), jax.ShapeDtypeStruct((DIL_G, S, LANES), F32)],
        scratch_shapes=[pltpu.VMEM((S, LANES), F32)] * (DIL_G * (LB_PER_CB + 1)),
        compiler_params=_cparams(("arbitrary",)),
    )(o3, lse3)


def _dil_combine_bwd(dob, ob, alpha):
    heads_per_lb = LANES // HD

    def body(dob_ref, ob_ref, al_ref, do_ref, c_ref):
        g = pl.program_id(0)
        hf = pl.program_id(1)
        for gs in range(DIL_G):
            @pl.when(g == gs)
            def _(gs=gs):
                for tok, sub in _residue_pieces(DIL_R[gs]):
                    dob = dob_ref[tok, :]
                    prod = dob * ob_ref[tok, :]
                    al = al_ref[tok, :]
                    lane = lax.broadcasted_iota(jnp.int32, al.shape, 1)
                    c = jnp.where(hf == 0, 0.0, c_ref[sub, :])
                    for hh in range(heads_per_lb):
                        sl = slice(hh * HD, (hh + 1) * HD)
                        head = hf * heads_per_lb + hh
                        a = jnp.sum(jnp.where(lane == head, al, 0.0), axis=-1, keepdims=True)
                        do_ref[sub, sl] = (a * dob[:, sl]).astype(BF16)
                        c = jnp.where(lane == head, a * jnp.sum(prod[:, sl], axis=-1, keepdims=True), c)
                    c_ref[sub, :] = c

    half = pl.BlockSpec((S, LANES), lambda g, hf: (0, hf))
    return pl.pallas_call(
        body, name="dil_combine_bwd", grid=(DIL_G, LB_PER_CB),
        in_specs=[half, half, pl.BlockSpec((None, S, LANES), lambda g, hf: (g, 0, 0))],
        out_specs=[pl.BlockSpec((None, S, LANES), lambda g, hf: (g, 0, hf)),
                   pl.BlockSpec((None, S, LANES), lambda g, hf: (g, 0, 0))],
        out_shape=[jax.ShapeDtypeStruct((DIL_G, S, DIL_OUT), BF16), jax.ShapeDtypeStruct((DIL_G, S, LANES), F32)],
        compiler_params=_cparams(("parallel", "arbitrary")),
    )(dob, ob, alpha)


def _local_step(x, tgt, g_attn, g_mlp, g_final, b_pad, w_in_pad, w_a, w_b, w_out, w_up_sh, w_down):
    tables = _rope_tables()

    h1 = _rms_fwd(x, g_attn, "rms_attn_fwd")
    proj_a = _mm(h1, w_in_pad, mode="nn", tm=S, tn=512, n=3 * FOX_W, out_dtypes=[BF16], name="proj_a")
    proj_b = _mm(h1, w_in_pad, mode="nn", tm=S, tn=CB, n=3 * DIL_W, b_off=SEG_B // CB, out_dtypes=[F32], name="proj_b")
    proj_g = _mm(h1, w_in_pad, mode="nn", tm=S, tn=CB, n=2 * D, b_off=SEG_G // CB, out_dtypes=[F32], name="proj_g")
    proj_f = _mm(h1, w_in_pad, mode="nn", tm=S, tn=LANES, n=LANES, b_off=SEG_F // LANES, out_dtypes=[F32], name="proj_f")

    fcol, frow = _fox_scan(proj_f, b_pad)
    oa, lse_a = _fox_fwd(proj_a, fcol, frow)

    qkv3 = _rope_split(proj_b, tables)
    o3, lse3 = _dil_fwd(qkv3)
    ob, alpha = _dil_combine(o3, lse3)

    ya = _mm(oa, w_a, mode="nn", tm=S, tn=CB, out_dtypes=[F32], name="branch_a")
    yb = _mm(ob, w_b, mode="nn", tm=S, tn=CB, out_dtypes=[F32], name="branch_b")
    mixed = _gate_fwd(proj_g, ya, yb)
    x2 = _mm(mixed, w_out, mode="nn", tm=S, tn=CB, out_dtypes=[F32], name="out_proj",
             epilogue=lambda acc, res: (res + acc,), extras=(x,))

    h2 = _rms_fwd(x2, g_mlp, "rms_mlp_fwd")

    def up_epilogue(acc):
        r = jnp.maximum(acc, 0.0)
        return acc, r * r

    u, act = _mm(h2, w_up_sh, mode="nn", tm=S, tn=DFF // N_DEV, b_sharded=True, out_dtypes=[F32, BF16],
                 name="mlp_up", epilogue=up_epilogue)
    x3 = _mm(act, w_down, mode="nn", tm=512, tn=512, out_dtypes=[F32], name="mlp_down",
             epilogue=lambda acc, res: (res + acc,), extras=(x2,))

    dx3, dg_final, loss = _final_norm_loss(x3, g_final, tgt)

    du = _mm(dx3, w_down, mode="nt", tm=S, tn=CB, out_dtypes=[BF16], name="mlp_down_bwd",
             epilogue=lambda acc, u_t: (acc * (2.0 * jnp.maximum(u_t, 0.0)),), extras=(u,))
    dw_down = _mm(act, dx3, mode="tn", tm=512, tn=512, out_dtypes=[F32], name="dw_down")
    dw_up_sh = _mm(h2, du, mode="tn", tm=D, tn=DFF // N_DEV, out_sharded=True, out_dtypes=[F32], name="dw_up")
    dh2 = _mm(du, w_up_sh, mode="nt", tm=512, tn=512, b_sharded=True, out_dtypes=[F32], name="mlp_up_bwd")
    dx2, dg_mlp = _rms_bwd(x2, g_mlp, dh2, dx3, "rms_mlp_bwd")

    dmixed = _mm(dx2, w_out, mode="nt", tm=S, tn=CB, out_dtypes=[F32], name="out_proj_bwd")
    dw_out = _mm(mixed, dx2, mode="tn", tm=D, tn=CB, out_dtypes=[F32], name="dw_out")

    dyab, dproj = _gate_bwd(dmixed, proj_g, ya, yb)
    doa = _mm(dyab, w_a, mode="nt", tm=S, tn=CB, k=D, a_off=0, out_dtypes=[BF16], name="branch_a_bwd")
    dw_a = _mm(oa, dyab, mode="tn", tm=FOX_W, tn=CB, n=D, b_off=0, out_dtypes=[F32], name="dw_branch_a")
    dob = _mm(dyab, w_b, mode="nt", tm=S, tn=CB, k=D, a_off=1, out_dtypes=[F32], name="branch_b_bwd")
    dw_b = _mm(ob, dyab, mode="tn", tm=DIL_OUT, tn=CB, n=D, b_off=D // CB, out_dtypes=[F32], name="dw_branch_b")

    dproj, dft, dfs = _fox_bwd(proj_a, fcol, frow, lse_a, doa, dproj)
    dproj, db = _fox_dscan(dft, dfs, proj_f, b_pad, dproj)

    do3, c3 = _dil_combine_bwd(dob, ob, alpha)
    dq3, dk3, dv3 = _dil_bwd(qkv3, lse3, do3, c3)
    dproj = _rope_merge_bwd(dq3, dk3, dv3, tables, dproj)

    dw_in_pad = _mm(h1, dproj, mode="tn", tm=D, tn=CB, out_dtypes=[F32], name="dw_in")
    dh1 = _mm(dproj, w_in_pad, mode="nt", tm=512, tn=CB, out_dtypes=[F32], name="proj_bwd")
    dx, dg_attn = _rms_bwd(x, g_attn, dh1, dx2, "rms_attn_bwd")

    return loss, dx, (dw_in_pad, dw_a, dw_b, dw_out, dw_up_sh, dw_down), (dg_attn, db, dg_mlp, dg_final)


SHARD_SHAPES = ((D, D_IN // N_DEV), (FOX_W, D // N_DEV), (DIL_OUT, D // N_DEV), (D // N_DEV, D),
                (D, DFF // N_DEV), (DFF // N_DEV, D))
N_W = len(SHARD_SHAPES)
AG_COPIES = 7


def _place():
    return lax.axis_index("x"), lax.axis_index("y"), lax.axis_index("c")


def _all_gather(shards):
    def body(*refs):
        in_refs, out_refs = refs[:N_W], refs[N_W:2 * N_W]
        stage, cast = refs[2 * N_W:3 * N_W], refs[3 * N_W:4 * N_W]
        send_sems, recv_sems, local_sems, load_sems = refs[4 * N_W:]
        x, y, c = _place()
        me, sibling = (x, y, c), (x, y, 1 - c)
        chips = [(1 - x, y), (x, 1 - y), (1 - x, 1 - y)]

        def slot(i, px, py, pc):
            return out_refs[i].at[4 * px + 2 * py + pc]

        def copy(i, k, block, to, src=None):
            return pltpu.make_async_remote_copy(
                src_ref=slot(i, *block) if src is None else src, dst_ref=slot(i, *block),
                send_sem=send_sems.at[i * AG_COPIES + k], recv_sem=recv_sems.at[i * AG_COPIES + k],
                device_id=to, device_id_type=MESH)

        loads = [pltpu.make_async_copy(in_refs[i], stage[i], load_sems.at[i]) for i in range(N_W)]
        for ld in loads:
            ld.start()
        local, remote = [], []
        for i in range(N_W):
            loads[i].wait()
            cast[i][...] = stage[i][...].astype(BF16)
            local.append(pltpu.make_async_copy(cast[i], slot(i, *me), local_sems.at[i]))
            local[-1].start()
            first = [copy(i, 0, me, sibling, src=cast[i])]
            first += [copy(i, 1 + j, me, (*chip, c), src=cast[i]) for j, chip in enumerate(chips)]
            for cp in first:
                cp.start()
            remote += first
        for i in range(N_W):
            for j, chip in enumerate(chips):
                copy(i, 1 + j, (*chip, c), me).wait_recv()
                remote.append(copy(i, 4 + j, (*chip, c), sibling))
                remote[-1].start()
        for i in range(N_W):
            copy(i, 0, sibling, me).wait_recv()
            for j, chip in enumerate(chips):
                copy(i, 4 + j, (*chip, 1 - c), me).wait_recv()
        for cp in remote:
            cp.wait_send()
        for cp in local:
            cp.wait()

    return pl.pallas_call(
        body, name="all_gather_weights",
        out_shape=[jax.ShapeDtypeStruct((N_DEV,) + sh, BF16) for sh in SHARD_SHAPES],
        in_specs=[ANY] * N_W, out_specs=[ANY] * N_W,
        scratch_shapes=[pltpu.VMEM(sh, F32) for sh in SHARD_SHAPES] + [pltpu.VMEM(sh, BF16) for sh in SHARD_SHAPES]
                       + [pltpu.SemaphoreType.DMA((N_W * AG_COPIES,)), pltpu.SemaphoreType.DMA((N_W * AG_COPIES,)),
                          pltpu.SemaphoreType.DMA((N_W,)), pltpu.SemaphoreType.DMA((N_W,))],
        compiler_params=_cparams(None),
    )(*shards)


def _exchange_sibling(grads):
    def body(*refs):
        g_refs, r_refs = refs[:N_W], refs[N_W:2 * N_W]
        send_sems, recv_sems = refs[2 * N_W:]
        x, y, c = _place()
        copies = [pltpu.make_async_remote_copy(
            src_ref=g_refs[i].at[2 * k + (1 - c)], dst_ref=r_refs[i].at[k],
            send_sem=send_sems.at[4 * i + k], recv_sem=recv_sems.at[4 * i + k],
            device_id=(x, y, 1 - c), device_id_type=MESH) for i in range(N_W) for k in range(4)]
        for cp in copies:
            cp.start()
        for cp in copies:
            cp.wait()

    return pl.pallas_call(
        body, name="grad_exchange_sibling",
        out_shape=[jax.ShapeDtypeStruct((4,) + sh, F32) for sh in SHARD_SHAPES],
        in_specs=[ANY] * N_W, out_specs=[ANY] * N_W,
        scratch_shapes=[pltpu.SemaphoreType.DMA((4 * N_W,)), pltpu.SemaphoreType.DMA((4 * N_W,))],
    )(*grads)


def _exchange_chips(partials):
    def body(*refs):
        p_refs, r_refs = refs[:N_W], refs[N_W:2 * N_W]
        send_sems, recv_sems = refs[2 * N_W:]
        x, y, c = _place()
        chips = [(1 - x, y), (x, 1 - y), (1 - x, 1 - y)]
        copies = [pltpu.make_async_remote_copy(
            src_ref=p_refs[i].at[2 * cx + cy], dst_ref=r_refs[i].at[j],
            send_sem=send_sems.at[3 * i + j], recv_sem=recv_sems.at[3 * i + j],
            device_id=(cx, cy, c), device_id_type=MESH) for i in range(N_W) for j, (cx, cy) in enumerate(chips)]
        for cp in copies:
            cp.start()
        for cp in copies:
            cp.wait()

    return pl.pallas_call(
        body, name="grad_exchange_chips",
        out_shape=[jax.ShapeDtypeStruct((3,) + sh, BF16) for sh in SHARD_SHAPES],
        in_specs=[ANY] * N_W, out_specs=[ANY] * N_W,
        scratch_shapes=[pltpu.SemaphoreType.DMA((3 * N_W,)), pltpu.SemaphoreType.DMA((3 * N_W,))],
    )(*partials)


def _row_block(shape):
    rows, cols = shape
    return min(rows, 512 if cols <= LANES else 256)


def _chip_partial(ids, g_stack, recv1, name):
    shape = g_stack.shape[1:]
    tr = _row_block(shape)

    def body(ids_ref, g_ref, r_ref, pb_ref, own_ref):
        s = g_ref[...] + r_ref[...]
        pb_ref[...] = s.astype(BF16)

        @pl.when(pl.program_id(1) == ids_ref[1])
        def _():
            own_ref[...] = s

    grid_spec = pltpu.PrefetchScalarGridSpec(
        num_scalar_prefetch=1, grid=(shape[0] // tr, 4),
        in_specs=[pl.BlockSpec((None, tr, shape[1]), lambda r, k, ids: (2 * k + ids[0], r, 0)),
                  pl.BlockSpec((None, tr, shape[1]), lambda r, k, ids: (k, r, 0))],
        out_specs=[pl.BlockSpec((None, tr, shape[1]), lambda r, k, ids: (k, r, 0)),
                   pl.BlockSpec((tr, shape[1]), lambda r, k, ids: (r, 0))])
    return pl.pallas_call(
        body, name=name, grid_spec=grid_spec,
        out_shape=[jax.ShapeDtypeStruct((4,) + shape, BF16), jax.ShapeDtypeStruct(shape, F32)],
        compiler_params=_cparams(("parallel", "arbitrary")),
    )(ids, g_stack, recv1)


def _adamw(w, g, m, v):
    m = ADAM_B1 * m + (1.0 - ADAM_B1) * g
    v = ADAM_B2 * v + (1.0 - ADAM_B2) * (g * g)
    m_hat = m / (1.0 - ADAM_B1 ** ADAM_STEP)
    v_hat = v / (1.0 - ADAM_B2 ** ADAM_STEP)
    delta = -ADAM_LR * (m_hat / (jnp.sqrt(v_hat) + ADAM_EPS) + ADAM_WD * w)
    return delta, m, v


def _reduce_adamw(own, recv2, w, m, v, name):
    shape = own.shape
    tr = _row_block(shape)

    def body(own_ref, r_ref, w_ref, m_ref, v_ref, g_ref, d_ref, nm_ref, nv_ref):
        g = own_ref[...]
        for j in range(3):
            g = g + r_ref[j].astype(F32)
        delta, nm, nv = _adamw(w_ref[...], g, m_ref[...], v_ref[...])
        g_ref[...] = g
        d_ref[...] = delta
        nm_ref[...] = nm
        nv_ref[...] = nv

    blk = pl.BlockSpec((tr, shape[1]), lambda r: (r, 0))
    return pl.pallas_call(
        body, name=name, grid=(shape[0] // tr,),
        in_specs=[blk, pl.BlockSpec((3, tr, shape[1]), lambda r: (0, r, 0)), blk, blk, blk],
        out_specs=[blk] * 4, out_shape=[jax.ShapeDtypeStruct(shape, F32)] * 4,
        compiler_params=_cparams(("parallel",)),
    )(own, recv2, w, m, v)


def _small_allreduce_adamw(gvec, w, m, v):
    def body(g_ref, w_ref, m_ref, v_ref, go_ref, d_ref, nm_ref, nv_ref, buf, send_sems, recv_sems):
        x, y, c = _place()
        my_slot = 4 * x + 2 * y + c
        buf[my_slot] = g_ref[...]
        copies = []
        for k in range(1, N_DEV):
            px, py, pc = x ^ (k >> 2), y ^ ((k >> 1) & 1), c ^ (k & 1)
            copies.append(pltpu.make_async_remote_copy(
                src_ref=g_ref, dst_ref=buf.at[my_slot], send_sem=send_sems.at[k - 1], recv_sem=recv_sems.at[k - 1],
                device_id=(px, py, pc), device_id_type=MESH))
        for cp in copies:
            cp.start()
        for k in range(1, N_DEV):
            px, py, pc = x ^ (k >> 2), y ^ ((k >> 1) & 1), c ^ (k & 1)
            pltpu.make_async_remote_copy(
                src_ref=g_ref, dst_ref=buf.at[4 * px + 2 * py + pc], send_sem=send_sems.at[k - 1],
                recv_sem=recv_sems.at[k - 1], device_id=(px, py, pc), device_id_type=MESH).wait_recv()
        for cp in copies:
            cp.wait_send()
        g = buf[0]
        for s in range(1, N_DEV):
            g = g + buf[s]
        delta, nm, nv = _adamw(w_ref[...], g, m_ref[...], v_ref[...])
        go_ref[...] = g
        d_ref[...] = delta
        nm_ref[...] = nm
        nv_ref[...] = nv

    vm = pl.BlockSpec(memory_space=pltpu.VMEM)
    return pl.pallas_call(
        body, name="small_allreduce_adamw",
        in_specs=[vm] * 4, out_specs=[vm] * 4,
        out_shape=[jax.ShapeDtypeStruct((SMALL_R, LANES), F32)] * 4,
        scratch_shapes=[pltpu.VMEM((N_DEV, SMALL_R, LANES), F32),
                        pltpu.SemaphoreType.DMA((N_DEV - 1,)), pltpu.SemaphoreType.DMA((N_DEV - 1,))],
    )(gvec, w, m, v)


def _cols_to_whole(stack):
    rows = stack.shape[1]
    return stack.transpose(1, 0, 2).reshape(rows, -1)


def _whole_to_cols(t):
    rows = t.shape[0]
    return t.reshape(rows, N_DEV, -1).transpose(1, 0, 2)


def _pad_w_in(w):
    n_a = 3 * FOX_W
    n_b = n_a + FOX_H
    return jnp.concatenate([w[:, :n_a], w[:, n_b:n_b + 3 * DIL_W], jnp.zeros((D, CB), w.dtype),
                            w[:, n_b + 3 * DIL_W:], w[:, n_a:n_b], jnp.zeros((D, CB - FOX_H), w.dtype)], axis=1)


def _unpad_dw_in(dwp):
    return jnp.concatenate([dwp[:, :SEG_B], dwp[:, SEG_F:SEG_F + FOX_H], dwp[:, SEG_B:SEG_Z], dwp[:, SEG_G:SEG_F]], axis=1)


def _pack_small(g_attn, b, g_mlp, g_final):
    b_rows = jnp.pad(b, ((0, 7), (0, LANES - b.shape[1])))
    return jnp.concatenate([g_attn.reshape(8, LANES), g_mlp.reshape(8, LANES), g_final.reshape(8, LANES), b_rows], axis=0)


def _unpack_small(p):
    return (p[0:8].reshape(1, D), p[24:25, :FOX_H], p[8:16].reshape(1, D), p[16:24].reshape(D))


def kernel(x, norm_attn_g, w_in, b_forget, w_branch_a, w_branch_b, w_out, norm_mlp_g, w_up, w_down, norm_final_g, loss_target, m_norm_attn_g, m_w_in, m_b_forget, m_w_branch_a, m_w_branch_b, m_w_out, m_norm_mlp_g, m_w_up, m_w_down, m_norm_final_g, v_norm_attn_g, v_w_in, v_b_forget, v_w_branch_a, v_w_branch_b, v_w_out, v_norm_mlp_g, v_w_up, v_w_down, v_norm_final_g):
    cx, cy, cc = _place()
    ids = jnp.stack([cc, 2 * cx + cy]).astype(jnp.int32)

    w_sh = [t[0] for t in (w_in, w_branch_a, w_branch_b, w_out, w_up, w_down)]
    m_sh = [t[0] for t in (m_w_in, m_w_branch_a, m_w_branch_b, m_w_out, m_w_up, m_w_down)]
    v_sh = [t[0] for t in (v_w_in, v_w_branch_a, v_w_branch_b, v_w_out, v_w_up, v_w_down)]

    g_in, g_a, g_b, g_out, g_up, g_down = _all_gather(w_sh)
    w_in_pad = _pad_w_in(_cols_to_whole(g_in))
    b_pad = jnp.pad(b_forget, ((0, 0), (0, LANES - FOX_H)))

    loss_row, dx, dws, dsmall = _local_step(
        x[0], loss_target[0], norm_attn_g, norm_mlp_g, norm_final_g.reshape(1, D), b_pad,
        w_in_pad, _cols_to_whole(g_a), _cols_to_whole(g_b), g_out.reshape(D, D), g_up, g_down.reshape(DFF, D))
    loss = lax.psum(loss_row[0, 0], ("x", "y", "c"))

    dw_in_pad, dw_a, dw_b, dw_out, dw_up_sh, dw_down = dws
    grads = [_whole_to_cols(_unpad_dw_in(dw_in_pad)), _whole_to_cols(dw_a), _whole_to_cols(dw_b),
             dw_out.reshape((N_DEV,) + SHARD_SHAPES[3]), dw_up_sh, dw_down.reshape((N_DEV,) + SHARD_SHAPES[5])]
    recv1 = _exchange_sibling(grads)
    names = ("w_in", "w_a", "w_b", "w_out", "w_up", "w_down")
    parts = [_chip_partial(ids, grads[i], recv1[i], "grad_partial_" + names[i]) for i in range(N_W)]
    recv2 = _exchange_chips([p[0] for p in parts])
    big = [_reduce_adamw(parts[i][1], recv2[i], w_sh[i], m_sh[i], v_sh[i], "adamw_" + names[i]) for i in range(N_W)]

    dg_attn, db, dg_mlp, dg_final = dsmall
    small = _small_allreduce_adamw(
        _pack_small(dg_attn, db, dg_mlp, dg_final),
        _pack_small(norm_attn_g, b_forget, norm_mlp_g, norm_final_g.reshape(1, D)),
        _pack_small(m_norm_attn_g, m_b_forget, m_norm_mlp_g, m_norm_final_g.reshape(1, D)),
        _pack_small(v_norm_attn_g, v_b_forget, v_norm_mlp_g, v_norm_final_g.reshape(1, D)))

    outs = [loss, dx[None]]
    for q in range(4):
        s_attn, s_b, s_mlp, s_final = _unpack_small(small[q])
        b_in, b_a, b_b, b_out, b_up, b_down = [big[i][q][None] for i in range(N_W)]
        outs += [s_attn, b_in, s_b, b_a, b_b, b_out, s_mlp, b_up, b_down, s_final]
    return tuple(outs)
```

```python
import functools

import jax
import jax.numpy as jnp
from jax import lax
from jax.experimental import pallas as pl
from jax.experimental.pallas import tpu as pltpu

F32 = jnp.float32
BF16 = jnp.bfloat16
MESH = pl.DeviceIdType.MESH

S = 2048
D = 1024
HD = 64
FOX_H = 8
FOX_W = FOX_H * HD
DIL_HG = 4
DIL_G = 3
DIL_W = DIL_G * DIL_HG * HD
DIL_OUT = DIL_HG * HD
DIL_BLK = 128
DIL_R = (1, 4, 16)
DFF = 4 * D
D_IN = 3 * FOX_W + FOX_H + 3 * DIL_W + 2 * D
EPS = 1e-6
NEG_INF = -1e30
SCALE = HD ** -0.5
ROPE_THETA = 500000.0
ROPE_DIM = HD // 4
N_DEV = 8

ADAM_LR = 0.001
ADAM_B1 = 0.9
ADAM_B2 = 0.999
ADAM_EPS = 1e-08
ADAM_WD = 0.01
ADAM_STEP = 10

LANES = 128
CB = 256
SEG_A = 0
SEG_B = 3 * FOX_W
SEG_Z = SEG_B + 3 * DIL_W
SEG_G = SEG_Z + CB
SEG_F = SEG_G + 2 * D
D_PAD = SEG_F + CB
SMALL_R = 32

VMEM_MB = 56


def _cparams(dims=None, vmem_mb=VMEM_MB, **kw):
    return pltpu.CompilerParams(dimension_semantics=dims, vmem_limit_bytes=vmem_mb << 20, **kw)


_NN = (((1,), (0,)), ((), ()))
_NT = (((1,), (1,)), ((), ()))
_TN = (((0,), (0,)), ((), ()))


def _dot(a, b, dims):
    return lax.dot_general(a.astype(BF16), b.astype(BF16), dims, preferred_element_type=F32)


def _mm(a, b, *, mode, tm, tn, out_dtypes, name, n=None, k=None, a_off=0, b_off=0,
        b_sharded=False, out_sharded=False, epilogue=None, extras=()):
    n_sh = b.shape[-1] if b_sharded else None
    if mode == "nn":
        m = a.shape[0]
        k = k or a.shape[1]
        a_spec = pl.BlockSpec((tm, k), lambda i, j: (i, a_off))
        if b_sharded:
            assert tn == n_sh
            n = N_DEV * n_sh
            b_spec = pl.BlockSpec((None, k, tn), lambda i, j: (j, 0, 0))
        else:
            n = n or b.shape[1]
            b_spec = pl.BlockSpec((k, tn), lambda i, j: (0, j + b_off))
        dims = _NN
    elif mode == "nt":
        m = a.shape[0]
        k = k or a.shape[1]
        a_spec = pl.BlockSpec((tm, k), lambda i, j: (i, a_off))
        if b_sharded:
            n = b.shape[1]
            b_spec = pl.BlockSpec((N_DEV, tn, n_sh), lambda i, j: (0, j, 0))
        else:
            n = n or b.shape[0]
            b_spec = pl.BlockSpec((tn, k), lambda i, j: (j + b_off, 0))
        dims = _NT
    else:
        k, m = a.shape
        n = n or b.shape[1]
        a_spec = pl.BlockSpec((k, tm), lambda i, j: (0, i))
        b_spec = pl.BlockSpec((k, tn), lambda i, j: (0, j + b_off))
        dims = _TN
    assert m % tm == 0 and n % tn == 0, (name, m, n, tm, tn)
    n_extra = len(extras)
    tile = pl.BlockSpec((tm, tn), lambda i, j: (i, j))
    if out_sharded:
        assert mode == "tn" and n // tn == N_DEV
        out_spec = pl.BlockSpec((None, tm, tn), lambda i, j: (j, i, 0))
        out_shape = (N_DEV, m, tn)
    else:
        out_spec, out_shape = tile, (m, n)

    def body(a_ref, b_ref, *refs):
        if mode == "nt" and b_sharded:
            acc = _dot(a_ref[:, 0:n_sh], b_ref[0], dims)
            for p in range(1, N_DEV):
                acc = acc + _dot(a_ref[:, p * n_sh:(p + 1) * n_sh], b_ref[p], dims)
        else:
            acc = _dot(a_ref[...], b_ref[...], dims)
        ex = [r[...] for r in refs[:n_extra]]
        outs = epilogue(acc, *ex) if epilogue is not None else (acc,)
        for o_ref, o in zip(refs[n_extra:], outs):
            o_ref[...] = o.astype(o_ref.dtype)

    res = pl.pallas_call(
        body, name=name, grid=(m // tm, n // tn),
        in_specs=[a_spec, b_spec] + [tile] * n_extra,
        out_specs=[out_spec] * len(out_dtypes),
        out_shape=[jax.ShapeDtypeStruct(out_shape, dt) for dt in out_dtypes],
        compiler_params=_cparams(("parallel", "parallel")),
    )(a, b, *extras)
    return res if len(out_dtypes) > 1 else res[0]


ROW_T = 256


def _rms_fwd(x, g, name):
    def body(x_ref, g_ref, o_ref):
        x = x_ref[...]
        r = lax.rsqrt(jnp.mean(x * x, axis=-1, keepdims=True) + EPS)
        o_ref[...] = ((x * r) * g_ref[...]).astype(BF16)

    return pl.pallas_call(
        body, name=name, grid=(S // ROW_T,),
        in_specs=[pl.BlockSpec((ROW_T, D), lambda i: (i, 0)), pl.BlockSpec((1, D), lambda i: (0, 0))],
        out_specs=pl.BlockSpec((ROW_T, D), lambda i: (i, 0)),
        out_shape=jax.ShapeDtypeStruct((S, D), BF16),
        compiler_params=_cparams(("parallel",)),
    )(x, g)


def _rms_bwd(x, g, dh, dres, name):
    def body(x_ref, g_ref, dh_ref, dres_ref, dx_ref, dg_ref):
        @pl.when(pl.program_id(0) == 0)
        def _():
            dg_ref[...] = jnp.zeros_like(dg_ref)

        x = x_ref[...]
        dh = dh_ref[...]
        r = lax.rsqrt(jnp.mean(x * x, axis=-1, keepdims=True) + EPS)
        xn = x * r
        dhn = dh * g_ref[...]
        dx_ref[...] = dres_ref[...] + r * (dhn - xn * jnp.mean(dhn * xn, axis=-1, keepdims=True))
        dg_ref[...] += jnp.sum(dh * xn, axis=0, keepdims=True)

    row = pl.BlockSpec((ROW_T, D), lambda i: (i, 0))
    vec = pl.BlockSpec((1, D), lambda i: (0, 0))
    return pl.pallas_call(
        body, name=name, grid=(S // ROW_T,),
        in_specs=[row, vec, row, row], out_specs=[row, vec],
        out_shape=[jax.ShapeDtypeStruct((S, D), F32), jax.ShapeDtypeStruct((1, D), F32)],
        compiler_params=_cparams(("arbitrary",)),
    )(x, g, dh, dres)


def _final_norm_loss(x, g, tgt):
    def body(x_ref, g_ref, t_ref, dx_ref, dg_ref, loss_ref):
        @pl.when(pl.program_id(0) == 0)
        def _():
            dg_ref[...] = jnp.zeros_like(dg_ref)
            loss_ref[...] = jnp.zeros_like(loss_ref)

        x = x_ref[...]
        g = g_ref[...]
        r = lax.rsqrt(jnp.mean(x * x, axis=-1, keepdims=True) + EPS)
        xn = x * r
        err = xn * g - t_ref[...]
        row_loss = jnp.mean(err * err, axis=-1, keepdims=True)
        loss_ref[...] += 0.5 * jnp.sum(row_loss, axis=0, keepdims=True) * jnp.ones((1, LANES), F32)
        dy = err * (1.0 / D)
        dyn = dy * g
        dx_ref[...] = r * (dyn - xn * jnp.mean(dyn * xn, axis=-1, keepdims=True))
        dg_ref[...] += jnp.sum(dy * xn, axis=0, keepdims=True)

    row = pl.BlockSpec((ROW_T, D), lambda i: (i, 0))
    vec = pl.BlockSpec((1, D), lambda i: (0, 0))
    return pl.pallas_call(
        body, name="final_norm_loss", grid=(S // ROW_T,),
        in_specs=[row, vec, row],
        out_specs=[row, vec, pl.BlockSpec((1, LANES), lambda i: (0, 0))],
        out_shape=[jax.ShapeDtypeStruct((S, D), F32), jax.ShapeDtypeStruct((1, D), F32),
                   jax.ShapeDtypeStruct((1, LANES), F32)],
        compiler_params=_cparams(("arbitrary",)),
    )(x, g, tgt)


def _sigmoid(z):
    return 1.0 / (1.0 + jnp.exp(-z))


def _gate_fwd(proj_g, ya, yb):
    def body(ga_ref, gb_ref, ya_ref, yb_ref, o_ref):
        o_ref[...] = (_sigmoid(ga_ref[...]) * ya_ref[...] + _sigmoid(gb_ref[...]) * yb_ref[...]).astype(BF16)

    row = pl.BlockSpec((ROW_T, D), lambda i: (i, 0))
    return pl.pallas_call(
        body, name="gate_fwd", grid=(S // ROW_T,),
        in_specs=[row, pl.BlockSpec((ROW_T, D), lambda i: (i, 1)), row, row],
        out_specs=row, out_shape=jax.ShapeDtypeStruct((S, D), BF16),
        compiler_params=_cparams(("parallel",)),
    )(proj_g, proj_g, ya, yb)


GATE_TR = 1024
GATE_TC = 512
GATE_NB = 2 * D // GATE_TC


def _gate_bwd(dmixed, proj_g, ya, yb):
    half = GATE_NB // 2

    def body(dm_ref, g_ref, ya_ref, yb_ref, dy_ref, dg_ref):
        dm = dm_ref[...]
        s = _sigmoid(g_ref[...])
        y = jnp.where(pl.program_id(1) < half, ya_ref[...], yb_ref[...])
        dy_ref[...] = (dm * s).astype(BF16)
        dg_ref[...] = (dm * y * (s * (1.0 - s))).astype(BF16)

    blk = pl.BlockSpec((GATE_TR, GATE_TC), lambda i, j: (i, j))
    wrapped = pl.BlockSpec((GATE_TR, GATE_TC), lambda i, j: (i, j % half))
    return pl.pallas_call(
        body, name="gate_bwd", grid=(S // GATE_TR, GATE_NB),
        in_specs=[wrapped, blk, wrapped, wrapped],
        out_specs=[blk, pl.BlockSpec((GATE_TR, GATE_TC), lambda i, j: (i, j + SEG_G // GATE_TC))],
        out_shape=[jax.ShapeDtypeStruct((S, 2 * D), BF16), jax.ShapeDtypeStruct((S, D_PAD), BF16)],
        compiler_params=_cparams(("parallel", "parallel")),
    )(dmixed, proj_g, ya, yb)


FOX_TQ = 256
ANY = pl.BlockSpec(memory_space=pl.ANY)


def _scan_rows(x, reverse):
    n = x.shape[0]
    row = lax.broadcasted_iota(jnp.int32, x.shape, 0)
    k = 1
    while k < n:
        if reverse:
            x = x + jnp.where(row < n - k, pltpu.roll(x, n - k, 0), 0.0)
        else:
            x = x + jnp.where(row >= k, pltpu.roll(x, k, 0), 0.0)
        k *= 2
    return x


def _fox_scan(proj_f, b_pad):
    def body(f_ref, b_ref, fc_ref, fr_ref):
        z = f_ref[...] + b_ref[...]
        lf = jnp.minimum(z, 0.0) - jnp.log1p(jnp.exp(-jnp.abs(z)))
        f = _scan_rows(lf, reverse=False)
        fc_ref[...] = f
        fr_ref[...] = f.T

    return pl.pallas_call(
        body, name="fox_scan", grid=(1,),
        in_specs=[pl.BlockSpec((S, LANES), lambda i: (0, 0)), pl.BlockSpec((1, LANES), lambda i: (0, 0))],
        out_specs=[pl.BlockSpec((S, LANES), lambda i: (0, 0)), pl.BlockSpec((LANES, S), lambda i: (0, 0))],
        out_shape=[jax.ShapeDtypeStruct((S, LANES), F32), jax.ShapeDtypeStruct((LANES, S), F32)],
        compiler_params=_cparams(("arbitrary",)),
    )(proj_f, b_pad)


def _fox_dscan(dft, dfs, proj_f, b_pad, dproj):
    def body(dft_ref, dfs_ref, f_ref, b_ref, _, dfa_ref, db_ref):
        dfs_pad = jnp.concatenate([dfs_ref[...], jnp.zeros((LANES - FOX_H, S), F32)], axis=0)
        dlf = _scan_rows(dft_ref[...] + dfs_pad.T, reverse=True)
        z = f_ref[...] + b_ref[...]
        lane = lax.broadcasted_iota(jnp.int32, (S, LANES), 1)
        dfa = jnp.where(lane < FOX_H, dlf / (1.0 + jnp.exp(z)), 0.0)
        dfa_ref[:, :LANES] = dfa.astype(BF16)
        dfa_ref[:, LANES:] = jnp.zeros((S, CB - LANES), BF16)
        db_ref[...] = jnp.sum(dfa, axis=0, keepdims=True)

    return pl.pallas_call(
        body, name="fox_dscan", grid=(1,),
        in_specs=[pl.BlockSpec((S, LANES), lambda i: (0, 0)), pl.BlockSpec((FOX_H, S), lambda i: (0, 0)),
                  pl.BlockSpec((S, LANES), lambda i: (0, 0)), pl.BlockSpec((1, LANES), lambda i: (0, 0)), ANY],
        out_specs=[pl.BlockSpec((S, CB), lambda i: (0, SEG_F // CB)), pl.BlockSpec((1, LANES), lambda i: (0, 0))],
        out_shape=[jax.ShapeDtypeStruct((S, D_PAD), BF16), jax.ShapeDtypeStruct((1, LANES), F32)],
        input_output_aliases={4: 0},
        compiler_params=_cparams(("arbitrary",)),
    )(dft, dfs, proj_f, b_pad, dproj)


def _fox_logits(q, k, ft, fs, causal):
    s = _dot(q, k, _NT) * SCALE
    s = (s + ft) - fs
    return jnp.where(causal, s, NEG_INF)


def _fox_causal(qi):
    row = qi * FOX_TQ + lax.broadcasted_iota(jnp.int32, (FOX_TQ, S), 0)
    col = lax.broadcasted_iota(jnp.int32, (FOX_TQ, S), 1)
    return col <= row


def _fox_fwd(proj_a, fcol, frow):
    def body(q_ref, k_ref, v_ref, ft_ref, fs_ref, o_ref, lse_ref):
        causal = _fox_causal(pl.program_id(0))
        ft = ft_ref[...]
        fs = fs_ref[...]
        lane = lax.broadcasted_iota(jnp.int32, (FOX_TQ, LANES), 1)
        lse = jnp.zeros((FOX_TQ, LANES), F32)
        for h in range(FOX_H):
            sl = slice(h * HD, (h + 1) * HD)
            s = _fox_logits(q_ref[:, sl], k_ref[:, sl], ft[:, h:h + 1], fs[h:h + 1, :], causal)
            m = jnp.max(s, axis=-1, keepdims=True)
            p = jnp.exp(s - m)
            l = jnp.sum(p, axis=-1, keepdims=True)
            o_ref[:, sl] = _dot(p / l, v_ref[:, sl], _NN)
            lse = jnp.where(lane == h, m + jnp.log(l), lse)
        lse_ref[...] = lse

    return pl.pallas_call(
        body, name="fox_fwd", grid=(S // FOX_TQ,),
        in_specs=[pl.BlockSpec((FOX_TQ, FOX_W), lambda i: (i, 0)),
                  pl.BlockSpec((S, FOX_W), lambda i: (0, 1)),
                  pl.BlockSpec((S, FOX_W), lambda i: (0, 2)),
                  pl.BlockSpec((FOX_TQ, LANES), lambda i: (i, 0)),
                  pl.BlockSpec((FOX_H, S), lambda i: (0, 0))],
        out_specs=[pl.BlockSpec((FOX_TQ, FOX_W), lambda i: (i, 0)),
                   pl.BlockSpec((FOX_TQ, LANES), lambda i: (i, 0))],
        out_shape=[jax.ShapeDtypeStruct((S, FOX_W), F32), jax.ShapeDtypeStruct((S, LANES), F32)],
        compiler_params=_cparams(("parallel",)),
    )(proj_a, proj_a, proj_a, fcol, frow)


def _fox_bwd(proj_a, fcol, frow, lse, do, dproj):
    n_q = S // FOX_TQ

    def body(q_ref, k_ref, v_ref, ft_ref, fs_ref, lse_ref, do_ref, _,
             dqkv_ref, dft_ref, dfs_ref, dk_acc, dv_acc):
        t = pl.program_id(0)

        @pl.when(t == 0)
        def _():
            dk_acc[...] = jnp.zeros_like(dk_acc)
            dv_acc[...] = jnp.zeros_like(dv_acc)
            dfs_ref[...] = jnp.zeros_like(dfs_ref)

        @pl.when(t < n_q)
        def _():
            causal = _fox_causal(t)
            ft = ft_ref[...]
            fs = fs_ref[...]
            lse = lse_ref[...]
            lane = lax.broadcasted_iota(jnp.int32, (FOX_TQ, LANES), 1)
            dft = jnp.zeros((FOX_TQ, LANES), F32)
            for h in range(FOX_H):
                sl = slice(h * HD, (h + 1) * HD)
                q = q_ref[:, sl]
                k = k_ref[:, sl]
                do_h = do_ref[:, sl]
                s = _fox_logits(q, k, ft[:, h:h + 1], fs[h:h + 1, :], causal)
                p = jnp.exp(s - lse[:, h:h + 1])
                dp = _dot(do_h, v_ref[:, sl], _NT)
                ds = p * (dp - jnp.sum(dp * p, axis=-1, keepdims=True))
                dft = jnp.where(lane == h, jnp.sum(ds, axis=-1, keepdims=True), dft)
                dfs_ref[h:h + 1, :] -= jnp.sum(ds, axis=0, keepdims=True)
                dsb = (ds * SCALE).astype(BF16)
                dqkv_ref[:, sl] = _dot(dsb, k, _NN).astype(BF16)
                dk_acc[:, sl] += _dot(dsb, q, _TN)
                dv_acc[:, sl] += _dot(p, do_h, _TN)
            dft_ref[...] = dft

        rows = pl.ds(pl.multiple_of((t % n_q) * FOX_TQ, FOX_TQ), FOX_TQ)

        @pl.when((t >= n_q) & (t < 2 * n_q))
        def _():
            dqkv_ref[...] = dk_acc[rows, :].astype(BF16)

        @pl.when(t >= 2 * n_q)
        def _():
            dqkv_ref[...] = dv_acc[rows, :].astype(BF16)

    def qidx(t):
        return jnp.minimum(t, n_q - 1)

    qblk = pl.BlockSpec((FOX_TQ, FOX_W), lambda t: (qidx(t), 0))
    lane_blk = pl.BlockSpec((FOX_TQ, LANES), lambda t: (qidx(t), 0))
    return pl.pallas_call(
        body, name="fox_bwd", grid=(3 * n_q,),
        in_specs=[qblk, pl.BlockSpec((S, FOX_W), lambda t: (0, 1)), pl.BlockSpec((S, FOX_W), lambda t: (0, 2)),
                  lane_blk, pl.BlockSpec((FOX_H, S), lambda t: (0, 0)), lane_blk, qblk, ANY],
        out_specs=[pl.BlockSpec((FOX_TQ, FOX_W), lambda t: (t % n_q, t // n_q)), lane_blk,
                   pl.BlockSpec((FOX_H, S), lambda t: (0, 0))],
        out_shape=[jax.ShapeDtypeStruct((S, D_PAD), BF16),
                   jax.ShapeDtypeStruct((S, LANES), F32), jax.ShapeDtypeStruct((FOX_H, S), F32)],
        scratch_shapes=[pltpu.VMEM((S, FOX_W), F32), pltpu.VMEM((S, FOX_W), F32)],
        input_output_aliases={7: 0},
        compiler_params=_cparams(("arbitrary",)),
    )(proj_a, proj_a, proj_a, fcol, frow, lse, do, dproj)


def _rope_tables():
    half = ROPE_DIM // 2
    inv_freq = jnp.power(jnp.float32(ROPE_THETA), -jnp.arange(half, dtype=F32) * 2.0 / ROPE_DIM)
    ang = jnp.arange(S).astype(F32)[:, None] * inv_freq[None, :]
    cos, sin = jnp.cos(ang), jnp.sin(ang)
    z = jnp.zeros((S, half), F32)
    rest = HD - ROPE_DIM
    t_self = jnp.concatenate([cos, cos, jnp.ones((S, rest), F32)], axis=1)
    t_up = jnp.concatenate([-sin, z, jnp.zeros((S, rest), F32)], axis=1)
    t_dn = jnp.concatenate([z, sin, jnp.zeros((S, rest), F32)], axis=1)
    rep = LANES // HD
    return tuple(jnp.tile(t, (1, rep)) for t in (t_self, t_up, t_dn))


def _residue_pieces(r):
    if r == 1:
        return [(slice(i, i + 512), slice(i, i + 512)) for i in range(0, S, 512)]
    n = S // r
    return [(pl.ds(j, n, stride=r), slice(j * n, (j + 1) * n)) for j in range(r)]


def _rope_apply(x, a, u, d, backward):
    half = ROPE_DIM // 2
    if backward:
        return x * a + pltpu.roll(x * u, half, 1) + pltpu.roll(x * d, LANES - half, 1)
    return x * a + pltpu.roll(x, LANES - half, 1) * u + pltpu.roll(x, half, 1) * d


LB_PER_CB = CB // LANES
ROPE_NB = 3 * DIL_G * LB_PER_CB


def _rope_split(proj_b, tables):
    def body(x_ref, a_ref, u_ref, d_ref, o_ref):
        j = pl.program_id(0)
        g = (j // LB_PER_CB) % DIL_G
        rotate = j < 2 * DIL_G * LB_PER_CB
        for gs in range(DIL_G):
            @pl.when(g == gs)
            def _(gs=gs):
                for tok, sub in _residue_pieces(DIL_R[gs]):
                    x = x_ref[tok, :]
                    y = _rope_apply(x, a_ref[tok, :], u_ref[tok, :], d_ref[tok, :], False)
                    o_ref[sub, :] = jnp.where(rotate, y, x).astype(BF16)

    tab = pl.BlockSpec((S, LANES), lambda j: (0, 0))
    return pl.pallas_call(
        body, name="rope_split", grid=(ROPE_NB,),
        in_specs=[pl.BlockSpec((S, LANES), lambda j: (0, j)), tab, tab, tab],
        out_specs=pl.BlockSpec((None, S, LANES), lambda j: (j // LB_PER_CB, 0, j % LB_PER_CB)),
        out_shape=jax.ShapeDtypeStruct((3 * DIL_G, S, CB), BF16),
        compiler_params=_cparams(("parallel",)),
    )(proj_b, *tables)


def _rope_merge_bwd(dq3, dk3, dv3, tables, dproj):
    per_tensor = DIL_G * LB_PER_CB

    def body(dq_ref, dk_ref, dv_ref, a_ref, u_ref, d_ref, _, o_ref, tmp):
        j = pl.program_id(0)
        t = j // per_tensor
        g = (j // LB_PER_CB) % DIL_G
        for gs in range(DIL_G):
            @pl.when((g == gs) & (j < ROPE_NB))
            def _(gs=gs):
                for tok, sub in _residue_pieces(DIL_R[gs]):
                    x = jnp.where(t == 0, dq_ref[sub, :], jnp.where(t == 1, dk_ref[sub, :], dv_ref[sub, :]))
                    y = _rope_apply(x, a_ref[tok, :], u_ref[tok, :], d_ref[tok, :], True)
                    tmp[tok, :] = jnp.where(t < 2, y, x)
                o_ref[...] = tmp[...].astype(BF16)

        @pl.when(j >= ROPE_NB)
        def _():
            o_ref[...] = jnp.zeros_like(o_ref)

    def src_spec(t):
        def index(j):
            jb = j // LB_PER_CB - t * DIL_G
            lane_blk = jnp.where(jb < 0, 0, jnp.where(jb >= DIL_G, LB_PER_CB - 1, j % LB_PER_CB))
            return jnp.clip(jb, 0, DIL_G - 1), 0, lane_blk
        return pl.BlockSpec((None, S, LANES), index)

    tab = pl.BlockSpec((S, LANES), lambda j: (0, 0))
    return pl.pallas_call(
        body, name="rope_merge_bwd", grid=(ROPE_NB + LB_PER_CB,),
        in_specs=[src_spec(0), src_spec(1), src_spec(2), tab, tab, tab, ANY],
        out_specs=pl.BlockSpec((S, LANES), lambda j: (0, j + SEG_B // LANES)),
        out_shape=jax.ShapeDtypeStruct((S, D_PAD), BF16),
        scratch_shapes=[pltpu.VMEM((S, LANES), F32)],
        input_output_aliases={6: 0},
        compiler_params=_cparams(("arbitrary",)),
    )(dq3, dk3, dv3, *tables, dproj)


DIL_NB = S // DIL_BLK


def _dil_mask(g, n):
    blocks_per_seq = jnp.right_shift(DIL_NB, 2 * g)
    has_prev = jnp.bitwise_and(n, blocks_per_seq - 1) != 0
    a = lax.broadcasted_iota(jnp.int32, (DIL_BLK, 2 * DIL_BLK), 0)
    kk = lax.broadcasted_iota(jnp.int32, (DIL_BLK, 2 * DIL_BLK), 1)
    diff = DIL_BLK + a - kk
    return (diff >= 0) & (diff <= DIL_BLK) & ((kk >= DIL_BLK) | has_prev)


def _dil_specs():
    def cur(t):
        return pl.BlockSpec((None, DIL_BLK, DIL_OUT), lambda g, n: (t * DIL_G + g, n, 0))

    def prev(t):
        return pl.BlockSpec((None, DIL_BLK, DIL_OUT), lambda g, n: (t * DIL_G + g, jnp.maximum(n - 1, 0), 0))

    lane_blk = pl.BlockSpec((None, DIL_BLK, LANES), lambda g, n: (g, n, 0))
    return cur, prev, lane_blk


def _dil_fwd(qkv3):
    def body(q_ref, kc_ref, kp_ref, vc_ref, vp_ref, o_ref, lse_ref):
        ok = _dil_mask(pl.program_id(0), pl.program_id(1))
        k2 = jnp.concatenate([kp_ref[...], kc_ref[...]], axis=0)
        v2 = jnp.concatenate([vp_ref[...], vc_ref[...]], axis=0)
        lane = lax.broadcasted_iota(jnp.int32, (DIL_BLK, LANES), 1)
        lse = jnp.zeros((DIL_BLK, LANES), F32)
        for h in range(DIL_HG):
            sl = slice(h * HD, (h + 1) * HD)
            s = jnp.where(ok, _dot(q_ref[:, sl], k2[:, sl], _NT) * SCALE, NEG_INF)
            m = jnp.max(s, axis=-1, keepdims=True)
            p = jnp.exp(s - m)
            l = jnp.sum(p, axis=-1, keepdims=True)
            o_ref[:, sl] = _dot(p / l, v2[:, sl], _NN)
            lse = jnp.where(lane == h, m + jnp.log(l), lse)
        lse_ref[...] = lse

    cur, prev, lane_blk = _dil_specs()
    out_blk = pl.BlockSpec((None, DIL_BLK, DIL_OUT), lambda g, n: (g, n, 0))
    return pl.pallas_call(
        body, name="dil_fwd", grid=(DIL_G, DIL_NB),
        in_specs=[cur(0), cur(1), prev(1), cur(2), prev(2)], out_specs=[out_blk, lane_blk],
        out_shape=[jax.ShapeDtypeStruct((DIL_G, S, DIL_OUT), F32), jax.ShapeDtypeStruct((DIL_G, S, LANES), F32)],
        compiler_params=_cparams(("parallel", "parallel")),
    )(qkv3, qkv3, qkv3, qkv3, qkv3)


def _dil_bwd(qkv3, lse3, do3, c3):
    def body(q_ref, kc_ref, kp_ref, vc_ref, vp_ref, lse_ref, do_ref, c_ref, dq_ref, dk_ref, dv_ref):
        n = pl.program_id(1)

        @pl.when(n == 0)
        def _():
            dk_ref[...] = jnp.zeros_like(dk_ref)
            dv_ref[...] = jnp.zeros_like(dv_ref)

        ok = _dil_mask(pl.program_id(0), n)
        k2 = jnp.concatenate([kp_ref[...], kc_ref[...]], axis=0)
        v2 = jnp.concatenate([vp_ref[...], vc_ref[...]], axis=0)
        lse = lse_ref[...]
        c = c_ref[...]
        cur_rows = pl.ds(pl.multiple_of(n * DIL_BLK, DIL_BLK), DIL_BLK)
        prev_rows = pl.ds(pl.multiple_of(jnp.maximum(n - 1, 0) * DIL_BLK, DIL_BLK), DIL_BLK)
        for h in range(DIL_HG):
            sl = slice(h * HD, (h + 1) * HD)
            q = q_ref[:, sl]
            do_h = do_ref[:, sl]
            s = jnp.where(ok, _dot(q, k2[:, sl], _NT) * SCALE, NEG_INF)
            p = jnp.exp(s - lse[:, h:h + 1])
            dp = _dot(do_h, v2[:, sl], _NT)
            ds = p * (dp - c[:, h:h + 1])
            dsb = (ds * SCALE).astype(BF16)
            dq_ref[:, sl] = _dot(dsb, k2[:, sl], _NN)
            dk2 = _dot(dsb, q, _TN)
            dv2 = _dot(p, do_h, _TN)
            dk_ref[prev_rows, sl] += dk2[:DIL_BLK]
            dk_ref[cur_rows, sl] += dk2[DIL_BLK:]
            dv_ref[prev_rows, sl] += dv2[:DIL_BLK]
            dv_ref[cur_rows, sl] += dv2[DIL_BLK:]

    cur, prev, lane_blk = _dil_specs()
    blk = pl.BlockSpec((None, DIL_BLK, DIL_OUT), lambda g, n: (g, n, 0))
    whole = pl.BlockSpec((None, S, DIL_OUT), lambda g, n: (g, 0, 0))
    return pl.pallas_call(
        body, name="dil_bwd", grid=(DIL_G, DIL_NB),
        in_specs=[cur(0), cur(1), prev(1), cur(2), prev(2), lane_blk, blk, lane_blk],
        out_specs=[blk, whole, whole],
        out_shape=[jax.ShapeDtypeStruct((DIL_G, S, DIL_OUT), F32)] * 3,
        compiler_params=_cparams(("parallel", "arbitrary")),
    )(qkv3, qkv3, qkv3, qkv3, qkv3, lse3, do3, c3)


COMB_T = 256


def _dil_combine(o3, lse3):
    heads_per_lb = LANES // HD

    def body(o_ref, lse_ref, ob_ref, al_ref, *scratch):
        o_tok = [scratch[LB_PER_CB * gg:LB_PER_CB * (gg + 1)] for gg in range(DIL_G)]
        lse_tok = scratch[LB_PER_CB * DIL_G:]
        g = pl.program_id(0)
        for gs in range(DIL_G):
            @pl.when(g == gs)
            def _(gs=gs):
                for tok, sub in _residue_pieces(DIL_R[gs]):
                    for hf in range(LB_PER_CB):
                        o_tok[gs][hf][tok, :] = o_ref[sub, hf * LANES:(hf + 1) * LANES]
                    lse_tok[gs][tok, :] = lse_ref[sub, :]

        @pl.when(g == DIL_G - 1)
        def _():
            def chunk(i, carry):
                rows = pl.ds(pl.multiple_of(i * COMB_T, COMB_T), COMB_T)
                lse = [lse_tok[gg][rows, :] for gg in range(DIL_G)]
                m = jnp.maximum(jnp.maximum(lse[0], lse[1]), lse[2])
                e = [jnp.exp(lse[gg] - m) for gg in range(DIL_G)]
                den = (e[0] + e[1]) + e[2]
                al = [e[gg] / den for gg in range(DIL_G)]
                for gg in range(DIL_G):
                    al_ref[gg, rows, :] = al[gg]
                for h in range(DIL_HG):
                    hf, sl = h // heads_per_lb, slice((h % heads_per_lb) * HD, (h % heads_per_lb + 1) * HD)
                    acc = al[0][:, h:h + 1] * o_tok[0][hf][rows, sl]
                    for gg in range(1, DIL_G):
                        acc = acc + al[gg][:, h:h + 1] * o_tok[gg][hf][rows, sl]
                    ob_ref[rows, h * HD:(h + 1) * HD] = acc
                return carry

            lax.fori_loop(0, S // COMB_T, chunk, 0)

    return pl.pallas_call(
        body, name="dil_combine", grid=(DIL_G,),
        in_specs=[pl.BlockSpec((None, S, DIL_OUT), lambda g: (g, 0, 0)),
                  pl.BlockSpec((None, S, LANES), lambda g: (g, 0, 0))],
        out_specs=[pl.BlockSpec((S, DIL_OUT), lambda g: (0, 0)),
                   pl.BlockSpec((DIL_G, S, LANES), lambda g: (0, 0, 0))],
        out_shape=[jax.ShapeDtypeStruct((S, DIL_OUT), F32), jax.ShapeDtypeStruct((DIL_G, S, LANES), F32)],
        scratch_shapes=[pltpu.VMEM((S, LANES), F32)] * (DIL_G * (LB_PER_CB + 1)),
        compiler_params=_cparams(("arbitrary",)),
    )(o3, lse3)


def _dil_combine_bwd(dob, ob, alpha):
    heads_per_lb = LANES // HD

    def body(dob_ref, ob_ref, al_ref, do_ref, c_ref):
        g = pl.program_id(0)
        hf = pl.program_id(1)
        for gs in range(DIL_G):
            @pl.when(g == gs)
            def _(gs=gs):
                for tok, sub in _residue_pieces(DIL_R[gs]):
                    dob = dob_ref[tok, :]
                    prod = dob * ob_ref[tok, :]
                    al = al_ref[tok, :]
                    lane = lax.broadcasted_iota(jnp.int32, al.shape, 1)
                    c = jnp.where(hf == 0, 0.0, c_ref[sub, :])
                    for hh in range(heads_per_lb):
                        sl = slice(hh * HD, (hh + 1) * HD)
                        head = hf * heads_per_lb + hh
                        a = jnp.sum(jnp.where(lane == head, al, 0.0), axis=-1, keepdims=True)
                        do_ref[sub, sl] = (a * dob[:, sl]).astype(BF16)
                        c = jnp.where(lane == head, a * jnp.sum(prod[:, sl], axis=-1, keepdims=True), c)
                    c_ref[sub, :] = c

    half = pl.BlockSpec((S, LANES), lambda g, hf: (0, hf))
    return pl.pallas_call(
        body, name="dil_combine_bwd", grid=(DIL_G, LB_PER_CB),
        in_specs=[half, half, pl.BlockSpec((None, S, LANES), lambda g, hf: (g, 0, 0))],
        out_specs=[pl.BlockSpec((None, S, LANES), lambda g, hf: (g, 0, hf)),
                   pl.BlockSpec((None, S, LANES), lambda g, hf: (g, 0, 0))],
        out_shape=[jax.ShapeDtypeStruct((DIL_G, S, DIL_OUT), BF16), jax.ShapeDtypeStruct((DIL_G, S, LANES), F32)],
        compiler_params=_cparams(("parallel", "arbitrary")),
    )(dob, ob, alpha)


def _local_step(x, tgt, g_attn, g_mlp, g_final, b_pad, w_in_t, w_a, w_b, w_out, w_up_sh, w_down):
    tables = _rope_tables()

    h1 = _rms_fwd(x, g_attn, "rms_attn_fwd")
    proj_a = _mm(h1, w_in_t, mode="nt", tm=S, tn=512, n=3 * FOX_W, out_dtypes=[BF16], name="proj_a")
    proj_b = _mm(h1, w_in_t, mode="nt", tm=S, tn=CB, n=3 * DIL_W, b_off=SEG_B // CB, out_dtypes=[F32], name="proj_b")
    proj_g = _mm(h1, w_in_t, mode="nt", tm=S, tn=CB, n=2 * D, b_off=SEG_G // CB, out_dtypes=[F32], name="proj_g")
    proj_f = _mm(h1, w_in_t, mode="nt", tm=S, tn=LANES, n=LANES, b_off=SEG_F // LANES, out_dtypes=[F32], name="proj_f")

    fcol, frow = _fox_scan(proj_f, b_pad)
    oa, lse_a = _fox_fwd(proj_a, fcol, frow)

    qkv3 = _rope_split(proj_b, tables)
    o3, lse3 = _dil_fwd(qkv3)
    ob, alpha = _dil_combine(o3, lse3)

    ya = _mm(oa, w_a, mode="nn", tm=S, tn=CB, out_dtypes=[F32], name="branch_a")
    yb = _mm(ob, w_b, mode="nn", tm=S, tn=CB, out_dtypes=[F32], name="branch_b")
    mixed = _gate_fwd(proj_g, ya, yb)
    x2 = _mm(mixed, w_out, mode="nn", tm=S, tn=CB, out_dtypes=[F32], name="out_proj",
             epilogue=lambda acc, res: (res + acc,), extras=(x,))

    h2 = _rms_fwd(x2, g_mlp, "rms_mlp_fwd")

    def up_epilogue(acc):
        r = jnp.maximum(acc, 0.0)
        return acc, r * r

    u, act = _mm(h2, w_up_sh, mode="nn", tm=S, tn=DFF // N_DEV, b_sharded=True, out_dtypes=[F32, BF16],
                 name="mlp_up", epilogue=up_epilogue)
    x3 = _mm(act, w_down, mode="nn", tm=512, tn=512, out_dtypes=[F32], name="mlp_down",
             epilogue=lambda acc, res: (res + acc,), extras=(x2,))

    dx3, dg_final, loss = _final_norm_loss(x3, g_final, tgt)

    du = _mm(dx3, w_down, mode="nt", tm=S, tn=CB, out_dtypes=[BF16], name="mlp_down_bwd",
             epilogue=lambda acc, u_t: (acc * (2.0 * jnp.maximum(u_t, 0.0)),), extras=(u,))
    dw_down = _mm(act, dx3, mode="tn", tm=512, tn=512, out_dtypes=[F32], name="dw_down")
    dw_up_sh = _mm(h2, du, mode="tn", tm=D, tn=DFF // N_DEV, out_sharded=True, out_dtypes=[F32], name="dw_up")
    dh2 = _mm(du, w_up_sh, mode="nt", tm=512, tn=512, b_sharded=True, out_dtypes=[F32], name="mlp_up_bwd")
    dx2, dg_mlp = _rms_bwd(x2, g_mlp, dh2, dx3, "rms_mlp_bwd")

    dmixed = _mm(dx2, w_out, mode="nt", tm=S, tn=CB, out_dtypes=[F32], name="out_proj_bwd")
    dw_out = _mm(mixed, dx2, mode="tn", tm=D, tn=CB, out_dtypes=[F32], name="dw_out")

    dyab, dproj = _gate_bwd(dmixed, proj_g, ya, yb)
    doa = _mm(dyab, w_a, mode="nt", tm=S, tn=CB, k=D, a_off=0, out_dtypes=[BF16], name="branch_a_bwd")
    dw_a = _mm(oa, dyab, mode="tn", tm=FOX_W, tn=CB, n=D, b_off=0, out_dtypes=[F32], name="dw_branch_a")
    dob = _mm(dyab, w_b, mode="nt", tm=S, tn=CB, k=D, a_off=1, out_dtypes=[F32], name="branch_b_bwd")
    dw_b = _mm(ob, dyab, mode="tn", tm=DIL_OUT, tn=CB, n=D, b_off=D // CB, out_dtypes=[F32], name="dw_branch_b")

    dproj, dft, dfs = _fox_bwd(proj_a, fcol, frow, lse_a, doa, dproj)
    dproj, db = _fox_dscan(dft, dfs, proj_f, b_pad, dproj)

    do3, c3 = _dil_combine_bwd(dob, ob, alpha)
    dq3, dk3, dv3 = _dil_bwd(qkv3, lse3, do3, c3)
    dproj = _rope_merge_bwd(dq3, dk3, dv3, tables, dproj)

    dw_in_t = _mm(dproj, h1, mode="tn", tm=CB, tn=D, out_dtypes=[F32], name="dw_in")
    dh1 = _mm(dproj, w_in_t, mode="nn", tm=512, tn=CB, out_dtypes=[F32], name="proj_bwd")
    dx, dg_attn = _rms_bwd(x, g_attn, dh1, dx2, "rms_attn_bwd")

    return loss, dx, (dw_in_t, dw_a, dw_b, dw_out, dw_up_sh, dw_down), (dg_attn, db, dg_mlp, dg_final)


SHARD_SHAPES = ((D_IN // N_DEV, D), (FOX_W, D // N_DEV), (DIL_OUT, D // N_DEV), (D // N_DEV, D),
                (D, DFF // N_DEV), (DFF // N_DEV, D))
N_W = len(SHARD_SHAPES)
AG_COPIES = 7


def _place():
    return lax.axis_index("x"), lax.axis_index("y"), lax.axis_index("c")


def _all_gather(shards):
    def body(*refs):
        in_refs, out_refs = refs[:N_W], refs[N_W:2 * N_W]
        stage, cast = refs[2 * N_W:3 * N_W], refs[3 * N_W:4 * N_W]
        send_sems, recv_sems, local_sems, load_sems = refs[4 * N_W:]
        x, y, c = _place()
        me, sibling = (x, y, c), (x, y, 1 - c)
        chips = [(1 - x, y), (x, 1 - y), (1 - x, 1 - y)]

        def slot(i, px, py, pc):
            return out_refs[i].at[4 * px + 2 * py + pc]

        def copy(i, k, block, to, src=None):
            return pltpu.make_async_remote_copy(
                src_ref=slot(i, *block) if src is None else src, dst_ref=slot(i, *block),
                send_sem=send_sems.at[i * AG_COPIES + k], recv_sem=recv_sems.at[i * AG_COPIES + k],
                device_id=to, device_id_type=MESH)

        loads = [pltpu.make_async_copy(in_refs[i], stage[i], load_sems.at[i]) for i in range(N_W)]
        for ld in loads:
            ld.start()
        local, remote = [], []
        for i in range(N_W):
            loads[i].wait()
            cast[i][...] = stage[i][...].astype(BF16)
            local.append(pltpu.make_async_copy(cast[i], slot(i, *me), local_sems.at[i]))
            local[-1].start()
            first = [copy(i, 0, me, sibling, src=cast[i])]
            first += [copy(i, 1 + j, me, (*chip, c), src=cast[i]) for j, chip in enumerate(chips)]
            for cp in first:
                cp.start()
            remote += first
        for i in range(N_W):
            for j, chip in enumerate(chips):
                copy(i, 1 + j, (*chip, c), me).wait_recv()
                remote.append(copy(i, 4 + j, (*chip, c), sibling))
                remote[-1].start()
        for i in range(N_W):
            copy(i, 0, sibling, me).wait_recv()
            for j, chip in enumerate(chips):
                copy(i, 4 + j, (*chip, 1 - c), me).wait_recv()
        for cp in remote:
            cp.wait_send()
        for cp in local:
            cp.wait()

    return pl.pallas_call(
        body, name="all_gather_weights",
        out_shape=[jax.ShapeDtypeStruct((N_DEV,) + sh, BF16) for sh in SHARD_SHAPES],
        in_specs=[ANY] * N_W, out_specs=[ANY] * N_W,
        scratch_shapes=[pltpu.VMEM(sh, F32) for sh in SHARD_SHAPES] + [pltpu.VMEM(sh, BF16) for sh in SHARD_SHAPES]
                       + [pltpu.SemaphoreType.DMA((N_W * AG_COPIES,)), pltpu.SemaphoreType.DMA((N_W * AG_COPIES,)),
                          pltpu.SemaphoreType.DMA((N_W,)), pltpu.SemaphoreType.DMA((N_W,))],
        compiler_params=_cparams(None),
    )(*shards)


def _exchange_sibling(grads):
    def body(*refs):
        g_refs, r_refs = refs[:N_W], refs[N_W:2 * N_W]
        send_sems, recv_sems = refs[2 * N_W:]
        x, y, c = _place()
        copies = [pltpu.make_async_remote_copy(
            src_ref=g_refs[i].at[2 * k + (1 - c)], dst_ref=r_refs[i].at[k],
            send_sem=send_sems.at[4 * i + k], recv_sem=recv_sems.at[4 * i + k],
            device_id=(x, y, 1 - c), device_id_type=MESH) for i in range(N_W) for k in range(4)]
        for cp in copies:
            cp.start()
        for cp in copies:
            cp.wait()

    return pl.pallas_call(
        body, name="grad_exchange_sibling",
        out_shape=[jax.ShapeDtypeStruct((4,) + sh, F32) for sh in SHARD_SHAPES],
        in_specs=[ANY] * N_W, out_specs=[ANY] * N_W,
        scratch_shapes=[pltpu.SemaphoreType.DMA((4 * N_W,)), pltpu.SemaphoreType.DMA((4 * N_W,))],
    )(*grads)


def _exchange_chips(partials):
    def body(*refs):
        p_refs, r_refs = refs[:N_W], refs[N_W:2 * N_W]
        send_sems, recv_sems = refs[2 * N_W:]
        x, y, c = _place()
        chips = [(1 - x, y), (x, 1 - y), (1 - x, 1 - y)]
        copies = [pltpu.make_async_remote_copy(
            src_ref=p_refs[i].at[2 * cx + cy], dst_ref=r_refs[i].at[j],
            send_sem=send_sems.at[3 * i + j], recv_sem=recv_sems.at[3 * i + j],
            device_id=(cx, cy, c), device_id_type=MESH) for i in range(N_W) for j, (cx, cy) in enumerate(chips)]
        for cp in copies:
            cp.start()
        for cp in copies:
            cp.wait()

    return pl.pallas_call(
        body, name="grad_exchange_chips",
        out_shape=[jax.ShapeDtypeStruct((3,) + sh, BF16) for sh in SHARD_SHAPES],
        in_specs=[ANY] * N_W, out_specs=[ANY] * N_W,
        scratch_shapes=[pltpu.SemaphoreType.DMA((3 * N_W,)), pltpu.SemaphoreType.DMA((3 * N_W,))],
    )(*partials)


def _shard_block(shape):
    rows, cols = shape
    if rows % 256:
        return rows, 256
    return min(rows, 512 if cols <= LANES else 256), cols


def _chip_partial(ids, g_stack, recv1, name):
    shape = g_stack.shape[1:]
    br, bc = _shard_block(shape)

    def body(ids_ref, g_ref, r_ref, pb_ref, own_ref):
        s = g_ref[...] + r_ref[...]
        pb_ref[...] = s.astype(BF16)

        @pl.when(pl.program_id(2) == ids_ref[1])
        def _():
            own_ref[...] = s

    grid_spec = pltpu.PrefetchScalarGridSpec(
        num_scalar_prefetch=1, grid=(shape[0] // br, shape[1] // bc, 4),
        in_specs=[pl.BlockSpec((None, br, bc), lambda r, q, k, ids: (2 * k + ids[0], r, q)),
                  pl.BlockSpec((None, br, bc), lambda r, q, k, ids: (k, r, q))],
        out_specs=[pl.BlockSpec((None, br, bc), lambda r, q, k, ids: (k, r, q)),
                   pl.BlockSpec((br, bc), lambda r, q, k, ids: (r, q))])
    return pl.pallas_call(
        body, name=name, grid_spec=grid_spec,
        out_shape=[jax.ShapeDtypeStruct((4,) + shape, BF16), jax.ShapeDtypeStruct(shape, F32)],
        compiler_params=_cparams(("parallel", "parallel", "arbitrary")),
    )(ids, g_stack, recv1)


def _adamw(w, g, m, v):
    m = ADAM_B1 * m + (1.0 - ADAM_B1) * g
    v = ADAM_B2 * v + (1.0 - ADAM_B2) * (g * g)
    m_hat = m / (1.0 - ADAM_B1 ** ADAM_STEP)
    v_hat = v / (1.0 - ADAM_B2 ** ADAM_STEP)
    delta = -ADAM_LR * (m_hat / (jnp.sqrt(v_hat) + ADAM_EPS) + ADAM_WD * w)
    return delta, m, v


def _reduce_adamw(own, recv2, w, m, v, name):
    shape = own.shape
    br, bc = _shard_block(shape)

    def body(own_ref, r_ref, w_ref, m_ref, v_ref, g_ref, d_ref, nm_ref, nv_ref):
        g = own_ref[...]
        for j in range(3):
            g = g + r_ref[j].astype(F32)
        delta, nm, nv = _adamw(w_ref[...], g, m_ref[...], v_ref[...])
        g_ref[...] = g
        d_ref[...] = delta
        nm_ref[...] = nm
        nv_ref[...] = nv

    blk = pl.BlockSpec((br, bc), lambda r, q: (r, q))
    return pl.pallas_call(
        body, name=name, grid=(shape[0] // br, shape[1] // bc),
        in_specs=[blk, pl.BlockSpec((3, br, bc), lambda r, q: (0, r, q)), blk, blk, blk],
        out_specs=[blk] * 4, out_shape=[jax.ShapeDtypeStruct(shape, F32)] * 4,
        compiler_params=_cparams(("parallel", "parallel")),
    )(own, recv2, w, m, v)


def _small_allreduce_adamw(gvec, w, m, v):
    def body(g_ref, w_ref, m_ref, v_ref, go_ref, d_ref, nm_ref, nv_ref, buf, send_sems, recv_sems):
        x, y, c = _place()
        my_slot = 4 * x + 2 * y + c
        buf[my_slot] = g_ref[...]
        copies = []
        for k in range(1, N_DEV):
            px, py, pc = x ^ (k >> 2), y ^ ((k >> 1) & 1), c ^ (k & 1)
            copies.append(pltpu.make_async_remote_copy(
                src_ref=g_ref, dst_ref=buf.at[my_slot], send_sem=send_sems.at[k - 1], recv_sem=recv_sems.at[k - 1],
                device_id=(px, py, pc), device_id_type=MESH))
        for cp in copies:
            cp.start()
        for k in range(1, N_DEV):
            px, py, pc = x ^ (k >> 2), y ^ ((k >> 1) & 1), c ^ (k & 1)
            pltpu.make_async_remote_copy(
                src_ref=g_ref, dst_ref=buf.at[4 * px + 2 * py + pc], send_sem=send_sems.at[k - 1],
                recv_sem=recv_sems.at[k - 1], device_id=(px, py, pc), device_id_type=MESH).wait_recv()
        for cp in copies:
            cp.wait_send()
        g = buf[0]
        for s in range(1, N_DEV):
            g = g + buf[s]
        delta, nm, nv = _adamw(w_ref[...], g, m_ref[...], v_ref[...])
        go_ref[...] = g
        d_ref[...] = delta
        nm_ref[...] = nm
        nv_ref[...] = nv

    vm = pl.BlockSpec(memory_space=pltpu.VMEM)
    return pl.pallas_call(
        body, name="small_allreduce_adamw",
        in_specs=[vm] * 4, out_specs=[vm] * 4,
        out_shape=[jax.ShapeDtypeStruct((SMALL_R, LANES), F32)] * 4,
        scratch_shapes=[pltpu.VMEM((N_DEV, SMALL_R, LANES), F32),
                        pltpu.SemaphoreType.DMA((N_DEV - 1,)), pltpu.SemaphoreType.DMA((N_DEV - 1,))],
    )(gvec, w, m, v)


def _cols_to_whole(stack):
    rows = stack.shape[1]
    return stack.transpose(1, 0, 2).reshape(rows, -1)


def _whole_to_cols(t):
    rows = t.shape[0]
    return t.reshape(rows, N_DEV, -1).transpose(1, 0, 2)


def _pad_w_in_t(wt):
    n_a = 3 * FOX_W
    n_b = n_a + FOX_H
    return jnp.concatenate([wt[:n_a], wt[n_b:n_b + 3 * DIL_W], jnp.zeros((CB, D), wt.dtype),
                            wt[n_b + 3 * DIL_W:], wt[n_a:n_b], jnp.zeros((CB - FOX_H, D), wt.dtype)], axis=0)


def _unpad_dw_in_t(dwp):
    return jnp.concatenate([dwp[:SEG_B], dwp[SEG_F:SEG_F + FOX_H], dwp[SEG_B:SEG_Z], dwp[SEG_G:SEG_F]], axis=0)


def _pack_small(g_attn, b, g_mlp, g_final):
    b_rows = jnp.pad(b, ((0, 7), (0, LANES - b.shape[1])))
    return jnp.concatenate([g_attn.reshape(8, LANES), g_mlp.reshape(8, LANES), g_final.reshape(8, LANES), b_rows], axis=0)


def _unpack_small(p):
    return (p[0:8].reshape(1, D), p[24:25, :FOX_H], p[8:16].reshape(1, D), p[16:24].reshape(D))


def kernel(x, norm_attn_g, w_in, b_forget, w_branch_a, w_branch_b, w_out, norm_mlp_g, w_up, w_down, norm_final_g, loss_target, m_norm_attn_g, m_w_in, m_b_forget, m_w_branch_a, m_w_branch_b, m_w_out, m_norm_mlp_g, m_w_up, m_w_down, m_norm_final_g, v_norm_attn_g, v_w_in, v_b_forget, v_w_branch_a, v_w_branch_b, v_w_out, v_norm_mlp_g, v_w_up, v_w_down, v_norm_final_g):
    cx, cy, cc = _place()
    ids = jnp.stack([cc, 2 * cx + cy]).astype(jnp.int32)

    w_sh = [w_in[0].T] + [t[0] for t in (w_branch_a, w_branch_b, w_out, w_up, w_down)]
    m_sh = [m_w_in[0].T] + [t[0] for t in (m_w_branch_a, m_w_branch_b, m_w_out, m_w_up, m_w_down)]
    v_sh = [v_w_in[0].T] + [t[0] for t in (v_w_branch_a, v_w_branch_b, v_w_out, v_w_up, v_w_down)]

    g_in, g_a, g_b, g_out, g_up, g_down = _all_gather(w_sh)
    w_in_t = _pad_w_in_t(g_in.reshape(D_IN, D))
    b_pad = jnp.pad(b_forget, ((0, 0), (0, LANES - FOX_H)))

    loss_row, dx, dws, dsmall = _local_step(
        x[0], loss_target[0], norm_attn_g, norm_mlp_g, norm_final_g.reshape(1, D), b_pad,
        w_in_t, _cols_to_whole(g_a), _cols_to_whole(g_b), g_out.reshape(D, D), g_up, g_down.reshape(DFF, D))
    loss = lax.psum(loss_row[0, 0], ("x", "y", "c"))

    dw_in_t, dw_a, dw_b, dw_out, dw_up_sh, dw_down = dws
    grads = [_unpad_dw_in_t(dw_in_t).reshape((N_DEV,) + SHARD_SHAPES[0]), _whole_to_cols(dw_a), _whole_to_cols(dw_b),
             dw_out.reshape((N_DEV,) + SHARD_SHAPES[3]), dw_up_sh, dw_down.reshape((N_DEV,) + SHARD_SHAPES[5])]
    recv1 = _exchange_sibling(grads)
    names = ("w_in", "w_a", "w_b", "w_out", "w_up", "w_down")
    parts = [_chip_partial(ids, grads[i], recv1[i], "grad_partial_" + names[i]) for i in range(N_W)]
    recv2 = _exchange_chips([p[0] for p in parts])
    big = [_reduce_adamw(parts[i][1], recv2[i], w_sh[i], m_sh[i], v_sh[i], "adamw_" + names[i]) for i in range(N_W)]

    dg_attn, db, dg_mlp, dg_final = dsmall
    small = _small_allreduce_adamw(
        _pack_small(dg_attn, db, dg_mlp, dg_final),
        _pack_small(norm_attn_g, b_forget, norm_mlp_g, norm_final_g.reshape(1, D)),
        _pack_small(m_norm_attn_g, m_b_forget, m_norm_mlp_g, m_norm_final_g.reshape(1, D)),
        _pack_small(v_norm_attn_g, v_b_forget, v_norm_mlp_g, v_norm_final_g.reshape(1, D)))

    outs = [loss, dx[None]]
    for q in range(4):
        s_attn, s_b, s_mlp, s_final = _unpack_small(small[q])
        b_in, b_a, b_b, b_out, b_up, b_down = [big[i][q][None] for i in range(N_W)]
        b_in = jnp.swapaxes(b_in, 1, 2)
        outs += [s_attn, b_in, s_b, b_a, b_b, b_out, s_mlp, b_up, b_down, s_final]
    return tuple(outs)
```

```python
import functools

import jax
import jax.numpy as jnp
from jax import lax
from jax.experimental import pallas as pl
from jax.experimental.pallas import tpu as pltpu

F32 = jnp.float32
BF16 = jnp.bfloat16
MESH = pl.DeviceIdType.MESH

S = 2048
D = 1024
HD = 64
FOX_H = 8
FOX_W = FOX_H * HD
DIL_HG = 4
DIL_G = 3
DIL_W = DIL_G * DIL_HG * HD
DIL_OUT = DIL_HG * HD
DIL_BLK = 128
DIL_R = (1, 4, 16)
DFF = 4 * D
D_IN = 3 * FOX_W + FOX_H + 3 * DIL_W + 2 * D
EPS = 1e-6
NEG_INF = -1e30
SCALE = HD ** -0.5
ROPE_THETA = 500000.0
ROPE_DIM = HD // 4
N_DEV = 8

ADAM_LR = 0.001
ADAM_B1 = 0.9
ADAM_B2 = 0.999
ADAM_EPS = 1e-08
ADAM_WD = 0.01
ADAM_STEP = 10

LANES = 128
CB = 256
SEG_A = 0
SEG_B = 3 * FOX_W
SEG_Z = SEG_B + 3 * DIL_W
SEG_G = SEG_Z + CB
SEG_F = SEG_G + 2 * D
D_PAD = SEG_F + CB
SMALL_R = 32

VMEM_MB = 56


def _cparams(dims=None, vmem_mb=VMEM_MB, **kw):
    return pltpu.CompilerParams(dimension_semantics=dims, vmem_limit_bytes=vmem_mb << 20, **kw)


ANY = pl.BlockSpec(memory_space=pl.ANY)

_NN = (((1,), (0,)), ((), ()))
_NT = (((1,), (1,)), ((), ()))
_TN = (((0,), (0,)), ((), ()))


def _dot(a, b, dims):
    return lax.dot_general(a.astype(BF16), b.astype(BF16), dims, preferred_element_type=F32)


def _mm(a, b, *, mode, tm, tn, out_dtypes, name, n=None, k=None, a_off=0, b_off=0,
        b_sharded=False, out_sharded=False, epilogue=None, extras=(), deps=()):
    n_sh = b.shape[-1] if b_sharded else None
    if mode == "nn":
        m = a.shape[0]
        k = k or a.shape[1]
        a_spec = pl.BlockSpec((tm, k), lambda i, j: (i, a_off))
        if b_sharded:
            assert tn == n_sh
            n = N_DEV * n_sh
            b_spec = pl.BlockSpec((None, k, tn), lambda i, j: (j, 0, 0))
        else:
            n = n or b.shape[1]
            b_spec = pl.BlockSpec((k, tn), lambda i, j: (0, j + b_off))
        dims = _NN
    elif mode == "nt":
        m = a.shape[0]
        k = k or a.shape[1]
        a_spec = pl.BlockSpec((tm, k), lambda i, j: (i, a_off))
        if b_sharded:
            n = b.shape[1]
            b_spec = pl.BlockSpec((N_DEV, tn, n_sh), lambda i, j: (0, j, 0))
        else:
            n = n or b.shape[0]
            b_spec = pl.BlockSpec((tn, k), lambda i, j: (j + b_off, 0))
        dims = _NT
    else:
        k, m = a.shape
        n = n or b.shape[1]
        a_spec = pl.BlockSpec((k, tm), lambda i, j: (0, i))
        b_spec = pl.BlockSpec((k, tn), lambda i, j: (0, j + b_off))
        dims = _TN
    assert m % tm == 0 and n % tn == 0, (name, m, n, tm, tn)
    n_extra = len(extras)
    tile = pl.BlockSpec((tm, tn), lambda i, j: (i, j))
    if out_sharded:
        assert mode == "tn" and n // tn == N_DEV
        out_spec = pl.BlockSpec((None, tm, tn), lambda i, j: (j, i, 0))
        out_shape = (N_DEV, m, tn)
    else:
        out_spec, out_shape = tile, (m, n)

    def body(a_ref, b_ref, *refs):
        if mode == "nt" and b_sharded:
            acc = _dot(a_ref[:, 0:n_sh], b_ref[0], dims)
            for p in range(1, N_DEV):
                acc = acc + _dot(a_ref[:, p * n_sh:(p + 1) * n_sh], b_ref[p], dims)
        else:
            acc = _dot(a_ref[...], b_ref[...], dims)
        ex = [r[...] for r in refs[:n_extra]]
        outs = epilogue(acc, *ex) if epilogue is not None else (acc,)
        for o_ref, o in zip(refs[n_extra + len(deps):], outs):
            o_ref[...] = o.astype(o_ref.dtype)

    res = pl.pallas_call(
        body, name=name, grid=(m // tm, n // tn),
        in_specs=[a_spec, b_spec] + [tile] * n_extra + [ANY] * len(deps),
        out_specs=[out_spec] * len(out_dtypes),
        out_shape=[jax.ShapeDtypeStruct(out_shape, dt) for dt in out_dtypes],
        compiler_params=_cparams(("parallel", "parallel")),
    )(a, b, *extras, *deps)
    return res if len(out_dtypes) > 1 else res[0]


ROW_T = 256


def _rms_fwd(x, g, name, deps=()):
    def body(x_ref, g_ref, *refs):
        o_ref = refs[-1]
        x = x_ref[...]
        r = lax.rsqrt(jnp.mean(x * x, axis=-1, keepdims=True) + EPS)
        o_ref[...] = ((x * r) * g_ref[...]).astype(BF16)

    return pl.pallas_call(
        body, name=name, grid=(S // ROW_T,),
        in_specs=[pl.BlockSpec((ROW_T, D), lambda i: (i, 0)), pl.BlockSpec((1, D), lambda i: (0, 0))] + [ANY] * len(deps),
        out_specs=pl.BlockSpec((ROW_T, D), lambda i: (i, 0)),
        out_shape=jax.ShapeDtypeStruct((S, D), BF16),
        compiler_params=_cparams(("parallel",)),
    )(x, g, *deps)


def _rms_bwd(x, g, dh, dres, name):
    def body(x_ref, g_ref, dh_ref, dres_ref, dx_ref, dg_ref):
        @pl.when(pl.program_id(0) == 0)
        def _():
            dg_ref[...] = jnp.zeros_like(dg_ref)

        x = x_ref[...]
        dh = dh_ref[...]
        r = lax.rsqrt(jnp.mean(x * x, axis=-1, keepdims=True) + EPS)
        xn = x * r
        dhn = dh * g_ref[...]
        dx_ref[...] = dres_ref[...] + r * (dhn - xn * jnp.mean(dhn * xn, axis=-1, keepdims=True))
        dg_ref[...] += jnp.sum(dh * xn, axis=0, keepdims=True)

    row = pl.BlockSpec((ROW_T, D), lambda i: (i, 0))
    vec = pl.BlockSpec((1, D), lambda i: (0, 0))
    return pl.pallas_call(
        body, name=name, grid=(S // ROW_T,),
        in_specs=[row, vec, row, row], out_specs=[row, vec],
        out_shape=[jax.ShapeDtypeStruct((S, D), F32), jax.ShapeDtypeStruct((1, D), F32)],
        compiler_params=_cparams(("arbitrary",)),
    )(x, g, dh, dres)


def _final_norm_loss(x, g, tgt):
    def body(x_ref, g_ref, t_ref, dx_ref, dg_ref, loss_ref):
        @pl.when(pl.program_id(0) == 0)
        def _():
            dg_ref[...] = jnp.zeros_like(dg_ref)
            loss_ref[...] = jnp.zeros_like(loss_ref)

        x = x_ref[...]
        g = g_ref[...]
        r = lax.rsqrt(jnp.mean(x * x, axis=-1, keepdims=True) + EPS)
        xn = x * r
        err = xn * g - t_ref[...]
        row_loss = jnp.mean(err * err, axis=-1, keepdims=True)
        loss_ref[...] += 0.5 * jnp.sum(row_loss, axis=0, keepdims=True) * jnp.ones((1, LANES), F32)
        dy = err * (1.0 / D)
        dyn = dy * g
        dx_ref[...] = r * (dyn - xn * jnp.mean(dyn * xn, axis=-1, keepdims=True))
        dg_ref[...] += jnp.sum(dy * xn, axis=0, keepdims=True)

    row = pl.BlockSpec((ROW_T, D), lambda i: (i, 0))
    vec = pl.BlockSpec((1, D), lambda i: (0, 0))
    return pl.pallas_call(
        body, name="final_norm_loss", grid=(S // ROW_T,),
        in_specs=[row, vec, row],
        out_specs=[row, vec, pl.BlockSpec((1, LANES), lambda i: (0, 0))],
        out_shape=[jax.ShapeDtypeStruct((S, D), F32), jax.ShapeDtypeStruct((1, D), F32),
                   jax.ShapeDtypeStruct((1, LANES), F32)],
        compiler_params=_cparams(("arbitrary",)),
    )(x, g, tgt)


def _sigmoid(z):
    return 1.0 / (1.0 + jnp.exp(-z))


def _gate_fwd(proj_g, ya, yb):
    def body(ga_ref, gb_ref, ya_ref, yb_ref, o_ref):
        o_ref[...] = (_sigmoid(ga_ref[...]) * ya_ref[...] + _sigmoid(gb_ref[...]) * yb_ref[...]).astype(BF16)

    row = pl.BlockSpec((ROW_T, D), lambda i: (i, 0))
    return pl.pallas_call(
        body, name="gate_fwd", grid=(S // ROW_T,),
        in_specs=[row, pl.BlockSpec((ROW_T, D), lambda i: (i, 1)), row, row],
        out_specs=row, out_shape=jax.ShapeDtypeStruct((S, D), BF16),
        compiler_params=_cparams(("parallel",)),
    )(proj_g, proj_g, ya, yb)


GATE_TR = 1024
GATE_TC = 512
GATE_NB = 2 * D // GATE_TC


def _gate_bwd(dmixed, proj_g, ya, yb):
    half = GATE_NB // 2

    def body(dm_ref, g_ref, ya_ref, yb_ref, dy_ref, dg_ref):
        dm = dm_ref[...]
        s = _sigmoid(g_ref[...])
        y = jnp.where(pl.program_id(1) < half, ya_ref[...], yb_ref[...])
        dy_ref[...] = (dm * s).astype(BF16)
        dg_ref[...] = (dm * y * (s * (1.0 - s))).astype(BF16)

    blk = pl.BlockSpec((GATE_TR, GATE_TC), lambda i, j: (i, j))
    wrapped = pl.BlockSpec((GATE_TR, GATE_TC), lambda i, j: (i, j % half))
    return pl.pallas_call(
        body, name="gate_bwd", grid=(S // GATE_TR, GATE_NB),
        in_specs=[wrapped, blk, wrapped, wrapped],
        out_specs=[blk, pl.BlockSpec((GATE_TR, GATE_TC), lambda i, j: (i, j + SEG_G // GATE_TC))],
        out_shape=[jax.ShapeDtypeStruct((S, 2 * D), BF16), jax.ShapeDtypeStruct((S, D_PAD), BF16)],
        compiler_params=_cparams(("parallel", "parallel")),
    )(dmixed, proj_g, ya, yb)


FOX_TQ = 256


def _scan_rows(x, reverse):
    n = x.shape[0]
    row = lax.broadcasted_iota(jnp.int32, x.shape, 0)
    k = 1
    while k < n:
        if reverse:
            x = x + jnp.where(row < n - k, pltpu.roll(x, n - k, 0), 0.0)
        else:
            x = x + jnp.where(row >= k, pltpu.roll(x, k, 0), 0.0)
        k *= 2
    return x


def _fox_scan(proj_f, b_pad):
    def body(f_ref, b_ref, fc_ref, fr_ref):
        z = f_ref[...] + b_ref[...]
        lf = jnp.minimum(z, 0.0) - jnp.log1p(jnp.exp(-jnp.abs(z)))
        f = _scan_rows(lf, reverse=False)
        fc_ref[...] = f
        fr_ref[...] = f.T

    return pl.pallas_call(
        body, name="fox_scan", grid=(1,),
        in_specs=[pl.BlockSpec((S, LANES), lambda i: (0, 0)), pl.BlockSpec((1, LANES), lambda i: (0, 0))],
        out_specs=[pl.BlockSpec((S, LANES), lambda i: (0, 0)), pl.BlockSpec((LANES, S), lambda i: (0, 0))],
        out_shape=[jax.ShapeDtypeStruct((S, LANES), F32), jax.ShapeDtypeStruct((LANES, S), F32)],
        compiler_params=_cparams(("arbitrary",)),
    )(proj_f, b_pad)


def _fox_dscan(dft, dfs, proj_f, b_pad, dproj):
    def body(dft_ref, dfs_ref, f_ref, b_ref, _, dfa_ref, db_ref):
        dfs_pad = jnp.concatenate([dfs_ref[...], jnp.zeros((LANES - FOX_H, S), F32)], axis=0)
        dlf = _scan_rows(dft_ref[...] + dfs_pad.T, reverse=True)
        z = f_ref[...] + b_ref[...]
        lane = lax.broadcasted_iota(jnp.int32, (S, LANES), 1)
        dfa = jnp.where(lane < FOX_H, dlf / (1.0 + jnp.exp(z)), 0.0)
        dfa_ref[:, :LANES] = dfa.astype(BF16)
        dfa_ref[:, LANES:] = jnp.zeros((S, CB - LANES), BF16)
        db_ref[...] = jnp.sum(dfa, axis=0, keepdims=True)

    return pl.pallas_call(
        body, name="fox_dscan", grid=(1,),
        in_specs=[pl.BlockSpec((S, LANES), lambda i: (0, 0)), pl.BlockSpec((FOX_H, S), lambda i: (0, 0)),
                  pl.BlockSpec((S, LANES), lambda i: (0, 0)), pl.BlockSpec((1, LANES), lambda i: (0, 0)), ANY],
        out_specs=[pl.BlockSpec((S, CB), lambda i: (0, SEG_F // CB)), pl.BlockSpec((1, LANES), lambda i: (0, 0))],
        out_shape=[jax.ShapeDtypeStruct((S, D_PAD), BF16), jax.ShapeDtypeStruct((1, LANES), F32)],
        input_output_aliases={4: 0},
        compiler_params=_cparams(("arbitrary",)),
    )(dft, dfs, proj_f, b_pad, dproj)


def _fox_logits(q, k, ft, fs, causal):
    s = _dot(q, k, _NT) * SCALE
    s = (s + ft) - fs
    return jnp.where(causal, s, NEG_INF)


def _fox_causal(qi):
    row = qi * FOX_TQ + lax.broadcasted_iota(jnp.int32, (FOX_TQ, S), 0)
    col = lax.broadcasted_iota(jnp.int32, (FOX_TQ, S), 1)
    return col <= row


def _fox_fwd(proj_a, fcol, frow):
    def body(q_ref, k_ref, v_ref, ft_ref, fs_ref, o_ref, lse_ref):
        causal = _fox_causal(pl.program_id(0))
        ft = ft_ref[...]
        fs = fs_ref[...]
        lane = lax.broadcasted_iota(jnp.int32, (FOX_TQ, LANES), 1)
        lse = jnp.zeros((FOX_TQ, LANES), F32)
        for h in range(FOX_H):
            sl = slice(h * HD, (h + 1) * HD)
            s = _fox_logits(q_ref[:, sl], k_ref[:, sl], ft[:, h:h + 1], fs[h:h + 1, :], causal)
            m = jnp.max(s, axis=-1, keepdims=True)
            p = jnp.exp(s - m)
            l = jnp.sum(p, axis=-1, keepdims=True)
            o_ref[:, sl] = _dot(p / l, v_ref[:, sl], _NN)
            lse = jnp.where(lane == h, m + jnp.log(l), lse)
        lse_ref[...] = lse

    return pl.pallas_call(
        body, name="fox_fwd", grid=(S // FOX_TQ,),
        in_specs=[pl.BlockSpec((FOX_TQ, FOX_W), lambda i: (i, 0)),
                  pl.BlockSpec((S, FOX_W), lambda i: (0, 1)),
                  pl.BlockSpec((S, FOX_W), lambda i: (0, 2)),
                  pl.BlockSpec((FOX_TQ, LANES), lambda i: (i, 0)),
                  pl.BlockSpec((FOX_H, S), lambda i: (0, 0))],
        out_specs=[pl.BlockSpec((FOX_TQ, FOX_W), lambda i: (i, 0)),
                   pl.BlockSpec((FOX_TQ, LANES), lambda i: (i, 0))],
        out_shape=[jax.ShapeDtypeStruct((S, FOX_W), F32), jax.ShapeDtypeStruct((S, LANES), F32)],
        compiler_params=_cparams(("parallel",)),
    )(proj_a, proj_a, proj_a, fcol, frow)


def _fox_bwd(proj_a, fcol, frow, lse, do, dproj):
    n_q = S // FOX_TQ

    def body(q_ref, k_ref, v_ref, ft_ref, fs_ref, lse_ref, do_ref, _,
             dqkv_ref, dft_ref, dfs_ref, dk_acc, dv_acc):
        t = pl.program_id(0)

        @pl.when(t == 0)
        def _():
            dk_acc[...] = jnp.zeros_like(dk_acc)
            dv_acc[...] = jnp.zeros_like(dv_acc)
            dfs_ref[...] = jnp.zeros_like(dfs_ref)

        @pl.when(t < n_q)
        def _():
            causal = _fox_causal(t)
            ft = ft_ref[...]
            fs = fs_ref[...]
            lse = lse_ref[...]
            lane = lax.broadcasted_iota(jnp.int32, (FOX_TQ, LANES), 1)
            dft = jnp.zeros((FOX_TQ, LANES), F32)
            for h in range(FOX_H):
                sl = slice(h * HD, (h + 1) * HD)
                q = q_ref[:, sl]
                k = k_ref[:, sl]
                do_h = do_ref[:, sl]
                s = _fox_logits(q, k, ft[:, h:h + 1], fs[h:h + 1, :], causal)
                p = jnp.exp(s - lse[:, h:h + 1])
                dp = _dot(do_h, v_ref[:, sl], _NT)
                ds = p * (dp - jnp.sum(dp * p, axis=-1, keepdims=True))
                dft = jnp.where(lane == h, jnp.sum(ds, axis=-1, keepdims=True), dft)
                dfs_ref[h:h + 1, :] -= jnp.sum(ds, axis=0, keepdims=True)
                dsb = (ds * SCALE).astype(BF16)
                dqkv_ref[:, sl] = _dot(dsb, k, _NN).astype(BF16)
                dk_acc[:, sl] += _dot(dsb, q, _TN)
                dv_acc[:, sl] += _dot(p, do_h, _TN)
            dft_ref[...] = dft

        rows = pl.ds(pl.multiple_of((t % n_q) * FOX_TQ, FOX_TQ), FOX_TQ)

        @pl.when((t >= n_q) & (t < 2 * n_q))
        def _():
            dqkv_ref[...] = dk_acc[rows, :].astype(BF16)

        @pl.when(t >= 2 * n_q)
        def _():
            dqkv_ref[...] = dv_acc[rows, :].astype(BF16)

    def qidx(t):
        return jnp.minimum(t, n_q - 1)

    qblk = pl.BlockSpec((FOX_TQ, FOX_W), lambda t: (qidx(t), 0))
    lane_blk = pl.BlockSpec((FOX_TQ, LANES), lambda t: (qidx(t), 0))
    return pl.pallas_call(
        body, name="fox_bwd", grid=(3 * n_q,),
        in_specs=[qblk, pl.BlockSpec((S, FOX_W), lambda t: (0, 1)), pl.BlockSpec((S, FOX_W), lambda t: (0, 2)),
                  lane_blk, pl.BlockSpec((FOX_H, S), lambda t: (0, 0)), lane_blk, qblk, ANY],
        out_specs=[pl.BlockSpec((FOX_TQ, FOX_W), lambda t: (t % n_q, t // n_q)), lane_blk,
                   pl.BlockSpec((FOX_H, S), lambda t: (0, 0))],
        out_shape=[jax.ShapeDtypeStruct((S, D_PAD), BF16),
                   jax.ShapeDtypeStruct((S, LANES), F32), jax.ShapeDtypeStruct((FOX_H, S), F32)],
        scratch_shapes=[pltpu.VMEM((S, FOX_W), F32), pltpu.VMEM((S, FOX_W), F32)],
        input_output_aliases={7: 0},
        compiler_params=_cparams(("arbitrary",)),
    )(proj_a, proj_a, proj_a, fcol, frow, lse, do, dproj)


def _rope_tables():
    half = ROPE_DIM // 2
    inv_freq = jnp.power(jnp.float32(ROPE_THETA), -jnp.arange(half, dtype=F32) * 2.0 / ROPE_DIM)
    ang = jnp.arange(S).astype(F32)[:, None] * inv_freq[None, :]
    cos, sin = jnp.cos(ang), jnp.sin(ang)
    z = jnp.zeros((S, half), F32)
    rest = HD - ROPE_DIM
    t_self = jnp.concatenate([cos, cos, jnp.ones((S, rest), F32)], axis=1)
    t_up = jnp.concatenate([-sin, z, jnp.zeros((S, rest), F32)], axis=1)
    t_dn = jnp.concatenate([z, sin, jnp.zeros((S, rest), F32)], axis=1)
    rep = LANES // HD
    return tuple(jnp.tile(t, (1, rep)) for t in (t_self, t_up, t_dn))


def _residue_pieces(r):
    if r == 1:
        return [(slice(i, i + 512), slice(i, i + 512)) for i in range(0, S, 512)]
    n = S // r
    return [(pl.ds(j, n, stride=r), slice(j * n, (j + 1) * n)) for j in range(r)]


def _rope_apply(x, a, u, d, backward):
    half = ROPE_DIM // 2
    if backward:
        return x * a + pltpu.roll(x * u, half, 1) + pltpu.roll(x * d, LANES - half, 1)
    return x * a + pltpu.roll(x, LANES - half, 1) * u + pltpu.roll(x, half, 1) * d


LB_PER_CB = CB // LANES
ROPE_NB = 3 * DIL_G * LB_PER_CB


def _rope_split(proj_b, tables):
    def body(x_ref, a_ref, u_ref, d_ref, o_ref):
        j = pl.program_id(0)
        g = (j // LB_PER_CB) % DIL_G
        rotate = j < 2 * DIL_G * LB_PER_CB
        for gs in range(DIL_G):
            @pl.when(g == gs)
            def _(gs=gs):
                for tok, sub in _residue_pieces(DIL_R[gs]):
                    x = x_ref[tok, :]
                    y = _rope_apply(x, a_ref[tok, :], u_ref[tok, :], d_ref[tok, :], False)
                    o_ref[sub, :] = jnp.where(rotate, y, x).astype(BF16)

    tab = pl.BlockSpec((S, LANES), lambda j: (0, 0))
    return pl.pallas_call(
        body, name="rope_split", grid=(ROPE_NB,),
        in_specs=[pl.BlockSpec((S, LANES), lambda j: (0, j)), tab, tab, tab],
        out_specs=pl.BlockSpec((None, S, LANES), lambda j: (j // LB_PER_CB, 0, j % LB_PER_CB)),
        out_shape=jax.ShapeDtypeStruct((3 * DIL_G, S, CB), BF16),
        compiler_params=_cparams(("parallel",)),
    )(proj_b, *tables)


def _rope_merge_bwd(dq3, dk3, dv3, tables, dproj):
    per_tensor = DIL_G * LB_PER_CB

    def body(dq_ref, dk_ref, dv_ref, a_ref, u_ref, d_ref, _, o_ref, tmp):
        j = pl.program_id(0)
        t = j // per_tensor
        g = (j // LB_PER_CB) % DIL_G
        for gs in range(DIL_G):
            @pl.when((g == gs) & (j < ROPE_NB))
            def _(gs=gs):
                for tok, sub in _residue_pieces(DIL_R[gs]):
                    x = jnp.where(t == 0, dq_ref[sub, :], jnp.where(t == 1, dk_ref[sub, :], dv_ref[sub, :]))
                    y = _rope_apply(x, a_ref[tok, :], u_ref[tok, :], d_ref[tok, :], True)
                    tmp[tok, :] = jnp.where(t < 2, y, x)
                o_ref[...] = tmp[...].astype(BF16)

        @pl.when(j >= ROPE_NB)
        def _():
            o_ref[...] = jnp.zeros_like(o_ref)

    def src_spec(t):
        def index(j):
            jb = j // LB_PER_CB - t * DIL_G
            lane_blk = jnp.where(jb < 0, 0, jnp.where(jb >= DIL_G, LB_PER_CB - 1, j % LB_PER_CB))
            return jnp.clip(jb, 0, DIL_G - 1), 0, lane_blk
        return pl.BlockSpec((None, S, LANES), index)

    tab = pl.BlockSpec((S, LANES), lambda j: (0, 0))
    return pl.pallas_call(
        body, name="rope_merge_bwd", grid=(ROPE_NB + LB_PER_CB,),
        in_specs=[src_spec(0), src_spec(1), src_spec(2), tab, tab, tab, ANY],
        out_specs=pl.BlockSpec((S, LANES), lambda j: (0, j + SEG_B // LANES)),
        out_shape=jax.ShapeDtypeStruct((S, D_PAD), BF16),
        scratch_shapes=[pltpu.VMEM((S, LANES), F32)],
        input_output_aliases={6: 0},
        compiler_params=_cparams(("arbitrary",)),
    )(dq3, dk3, dv3, *tables, dproj)


DIL_NB = S // DIL_BLK


def _dil_mask(g, n):
    blocks_per_seq = jnp.right_shift(DIL_NB, 2 * g)
    has_prev = jnp.bitwise_and(n, blocks_per_seq - 1) != 0
    a = lax.broadcasted_iota(jnp.int32, (DIL_BLK, 2 * DIL_BLK), 0)
    kk = lax.broadcasted_iota(jnp.int32, (DIL_BLK, 2 * DIL_BLK), 1)
    diff = DIL_BLK + a - kk
    return (diff >= 0) & (diff <= DIL_BLK) & ((kk >= DIL_BLK) | has_prev)


def _dil_specs():
    def cur(t):
        return pl.BlockSpec((None, DIL_BLK, DIL_OUT), lambda g, n: (t * DIL_G + g, n, 0))

    def prev(t):
        return pl.BlockSpec((None, DIL_BLK, DIL_OUT), lambda g, n: (t * DIL_G + g, jnp.maximum(n - 1, 0), 0))

    lane_blk = pl.BlockSpec((None, DIL_BLK, LANES), lambda g, n: (g, n, 0))
    return cur, prev, lane_blk


def _dil_fwd(qkv3):
    def body(q_ref, kc_ref, kp_ref, vc_ref, vp_ref, o_ref, lse_ref):
        ok = _dil_mask(pl.program_id(0), pl.program_id(1))
        k2 = jnp.concatenate([kp_ref[...], kc_ref[...]], axis=0)
        v2 = jnp.concatenate([vp_ref[...], vc_ref[...]], axis=0)
        lane = lax.broadcasted_iota(jnp.int32, (DIL_BLK, LANES), 1)
        lse = jnp.zeros((DIL_BLK, LANES), F32)
        for h in range(DIL_HG):
            sl = slice(h * HD, (h + 1) * HD)
            s = jnp.where(ok, _dot(q_ref[:, sl], k2[:, sl], _NT) * SCALE, NEG_INF)
            m = jnp.max(s, axis=-1, keepdims=True)
            p = jnp.exp(s - m)
            l = jnp.sum(p, axis=-1, keepdims=True)
            o_ref[:, sl] = _dot(p / l, v2[:, sl], _NN)
            lse = jnp.where(lane == h, m + jnp.log(l), lse)
        lse_ref[...] = lse

    cur, prev, lane_blk = _dil_specs()
    out_blk = pl.BlockSpec((None, DIL_BLK, DIL_OUT), lambda g, n: (g, n, 0))
    return pl.pallas_call(
        body, name="dil_fwd", grid=(DIL_G, DIL_NB),
        in_specs=[cur(0), cur(1), prev(1), cur(2), prev(2)], out_specs=[out_blk, lane_blk],
        out_shape=[jax.ShapeDtypeStruct((DIL_G, S, DIL_OUT), F32), jax.ShapeDtypeStruct((DIL_G, S, LANES), F32)],
        compiler_params=_cparams(("parallel", "parallel")),
    )(qkv3, qkv3, qkv3, qkv3, qkv3)


def _dil_bwd(qkv3, lse3, do3, c3):
    def body(q_ref, kc_ref, kp_ref, vc_ref, vp_ref, lse_ref, do_ref, c_ref, dq_ref, dk_ref, dv_ref):
        n = pl.program_id(1)

        @pl.when(n == 0)
        def _():
            dk_ref[...] = jnp.zeros_like(dk_ref)
            dv_ref[...] = jnp.zeros_like(dv_ref)

        ok = _dil_mask(pl.program_id(0), n)
        k2 = jnp.concatenate([kp_ref[...], kc_ref[...]], axis=0)
        v2 = jnp.concatenate([vp_ref[...], vc_ref[...]], axis=0)
        lse = lse_ref[...]
        c = c_ref[...]
        cur_rows = pl.ds(pl.multiple_of(n * DIL_BLK, DIL_BLK), DIL_BLK)
        prev_rows = pl.ds(pl.multiple_of(jnp.maximum(n - 1, 0) * DIL_BLK, DIL_BLK), DIL_BLK)
        for h in range(DIL_HG):
            sl = slice(h * HD, (h + 1) * HD)
            q = q_ref[:, sl]
            do_h = do_ref[:, sl]
            s = jnp.where(ok, _dot(q, k2[:, sl], _NT) * SCALE, NEG_INF)
            p = jnp.exp(s - lse[:, h:h + 1])
            dp = _dot(do_h, v2[:, sl], _NT)
            ds = p * (dp - c[:, h:h + 1])
            dsb = (ds * SCALE).astype(BF16)
            dq_ref[:, sl] = _dot(dsb, k2[:, sl], _NN)
            dk2 = _dot(dsb, q, _TN)
            dv2 = _dot(p, do_h, _TN)
            dk_ref[prev_rows, sl] += dk2[:DIL_BLK]
            dk_ref[cur_rows, sl] += dk2[DIL_BLK:]
            dv_ref[prev_rows, sl] += dv2[:DIL_BLK]
            dv_ref[cur_rows, sl] += dv2[DIL_BLK:]

    cur, prev, lane_blk = _dil_specs()
    blk = pl.BlockSpec((None, DIL_BLK, DIL_OUT), lambda g, n: (g, n, 0))
    whole = pl.BlockSpec((None, S, DIL_OUT), lambda g, n: (g, 0, 0))
    return pl.pallas_call(
        body, name="dil_bwd", grid=(DIL_G, DIL_NB),
        in_specs=[cur(0), cur(1), prev(1), cur(2), prev(2), lane_blk, blk, lane_blk],
        out_specs=[blk, whole, whole],
        out_shape=[jax.ShapeDtypeStruct((DIL_G, S, DIL_OUT), F32)] * 3,
        compiler_params=_cparams(("parallel", "arbitrary")),
    )(qkv3, qkv3, qkv3, qkv3, qkv3, lse3, do3, c3)


COMB_T = 256


def _dil_combine(o3, lse3):
    heads_per_lb = LANES // HD

    def body(o_ref, lse_ref, ob_ref, al_ref, *scratch):
        o_tok = [scratch[LB_PER_CB * gg:LB_PER_CB * (gg + 1)] for gg in range(DIL_G)]
        lse_tok = scratch[LB_PER_CB * DIL_G:]
        g = pl.program_id(0)
        for gs in range(DIL_G):
            @pl.when(g == gs)
            def _(gs=gs):
                for tok, sub in _residue_pieces(DIL_R[gs]):
                    for hf in range(LB_PER_CB):
                        o_tok[gs][hf][tok, :] = o_ref[sub, hf * LANES:(hf + 1) * LANES]
                    lse_tok[gs][tok, :] = lse_ref[sub, :]

        @pl.when(g == DIL_G - 1)
        def _():
            def chunk(i, carry):
                rows = pl.ds(pl.multiple_of(i * COMB_T, COMB_T), COMB_T)
                lse = [lse_tok[gg][rows, :] for gg in range(DIL_G)]
                m = jnp.maximum(jnp.maximum(lse[0], lse[1]), lse[2])
                e = [jnp.exp(lse[gg] - m) for gg in range(DIL_G)]
                den = (e[0] + e[1]) + e[2]
                al = [e[gg] / den for gg in range(DIL_G)]
                for gg in range(DIL_G):
                    al_ref[gg, rows, :] = al[gg]
                for h in range(DIL_HG):
                    hf, sl = h // heads_per_lb, slice((h % heads_per_lb) * HD, (h % heads_per_lb + 1) * HD)
                    acc = al[0][:, h:h + 1] * o_tok[0][hf][rows, sl]
                    for gg in range(1, DIL_G):
                        acc = acc + al[gg][:, h:h + 1] * o_tok[gg][hf][rows, sl]
                    ob_ref[rows, h * HD:(h + 1) * HD] = acc
                return carry

            lax.fori_loop(0, S // COMB_T, chunk, 0)

    return pl.pallas_call(
        body, name="dil_combine", grid=(DIL_G,),
        in_specs=[pl.BlockSpec((None, S, DIL_OUT), lambda g: (g, 0, 0)),
                  pl.BlockSpec((None, S, LANES), lambda g: (g, 0, 0))],
        out_specs=[pl.BlockSpec((S, DIL_OUT), lambda g: (0, 0)),
                   pl.BlockSpec((DIL_G, S, LANES), lambda g: (0, 0, 0))],
        out_shape=[jax.ShapeDtypeStruct((S, DIL_OUT), F32), jax.ShapeDtypeStruct((DIL_G, S, LANES), F32)],
        scratch_shapes=[pltpu.VMEM((S, LANES), F32)] * (DIL_G * (LB_PER_CB + 1)),
        compiler_params=_cparams(("arbitrary",)),
    )(o3, lse3)


def _dil_combine_bwd(dob, ob, alpha):
    heads_per_lb = LANES // HD

    def body(dob_ref, ob_ref, al_ref, do_ref, c_ref):
        g = pl.program_id(0)
        hf = pl.program_id(1)
        for gs in range(DIL_G):
            @pl.when(g == gs)
            def _(gs=gs):
                for tok, sub in _residue_pieces(DIL_R[gs]):
                    dob = dob_ref[tok, :]
                    prod = dob * ob_ref[tok, :]
                    al = al_ref[tok, :]
                    lane = lax.broadcasted_iota(jnp.int32, al.shape, 1)
                    c = jnp.where(hf == 0, 0.0, c_ref[sub, :])
                    for hh in range(heads_per_lb):
                        sl = slice(hh * HD, (hh + 1) * HD)
                        head = hf * heads_per_lb + hh
                        a = jnp.sum(jnp.where(lane == head, al, 0.0), axis=-1, keepdims=True)
                        do_ref[sub, sl] = (a * dob[:, sl]).astype(BF16)
                        c = jnp.where(lane == head, a * jnp.sum(prod[:, sl], axis=-1, keepdims=True), c)
                    c_ref[sub, :] = c

    half = pl.BlockSpec((S, LANES), lambda g, hf: (0, hf))
    return pl.pallas_call(
        body, name="dil_combine_bwd", grid=(DIL_G, LB_PER_CB),
        in_specs=[half, half, pl.BlockSpec((None, S, LANES), lambda g, hf: (g, 0, 0))],
        out_specs=[pl.BlockSpec((None, S, LANES), lambda g, hf: (g, 0, hf)),
                   pl.BlockSpec((None, S, LANES), lambda g, hf: (g, 0, 0))],
        out_shape=[jax.ShapeDtypeStruct((DIL_G, S, DIL_OUT), BF16), jax.ShapeDtypeStruct((DIL_G, S, LANES), F32)],
        compiler_params=_cparams(("parallel", "arbitrary")),
    )(dob, ob, alpha)


def _local_step(x, tgt, g_attn, g_mlp, g_final, b_pad, w_in_t, hooks):
    tables = _rope_tables()

    h1 = _rms_fwd(x, g_attn, "rms_attn_fwd", deps=hooks.first_deps())
    proj_a = _mm(h1, w_in_t, mode="nt", tm=S, tn=512, n=3 * FOX_W, out_dtypes=[BF16], name="proj_a")
    proj_b = _mm(h1, w_in_t, mode="nt", tm=S, tn=CB, n=3 * DIL_W, b_off=SEG_B // CB, out_dtypes=[F32], name="proj_b")
    proj_g = _mm(h1, w_in_t, mode="nt", tm=S, tn=CB, n=2 * D, b_off=SEG_G // CB, out_dtypes=[F32], name="proj_g")
    proj_f = _mm(h1, w_in_t, mode="nt", tm=S, tn=LANES, n=LANES, b_off=SEG_F // LANES, out_dtypes=[F32], name="proj_f")

    fcol, frow = _fox_scan(proj_f, b_pad)
    oa, lse_a = _fox_fwd(proj_a, fcol, frow)

    qkv3 = _rope_split(proj_b, tables)
    o3, lse3 = _dil_fwd(qkv3)
    ob, alpha = _dil_combine(o3, lse3)

    w_a, w_b, w_out, w_up_sh, w_down = hooks.late_weights(ob)
    ya = _mm(oa, w_a, mode="nn", tm=S, tn=CB, out_dtypes=[F32], name="branch_a")
    yb = _mm(ob, w_b, mode="nn", tm=S, tn=CB, out_dtypes=[F32], name="branch_b")
    mixed = _gate_fwd(proj_g, ya, yb)
    x2 = _mm(mixed, w_out, mode="nn", tm=S, tn=CB, out_dtypes=[F32], name="out_proj",
             epilogue=lambda acc, res: (res + acc,), extras=(x,))

    h2 = _rms_fwd(x2, g_mlp, "rms_mlp_fwd")

    def up_epilogue(acc):
        r = jnp.maximum(acc, 0.0)
        return acc, r * r

    u, act = _mm(h2, w_up_sh, mode="nn", tm=S, tn=DFF // N_DEV, b_sharded=True, out_dtypes=[F32, BF16],
                 name="mlp_up", epilogue=up_epilogue)
    x3 = _mm(act, w_down, mode="nn", tm=512, tn=512, out_dtypes=[F32], name="mlp_down",
             epilogue=lambda acc, res: (res + acc,), extras=(x2,))

    dx3, dg_final, loss = _final_norm_loss(x3, g_final, tgt)

    du = _mm(dx3, w_down, mode="nt", tm=S, tn=CB, out_dtypes=[BF16], name="mlp_down_bwd",
             epilogue=lambda acc, u_t: (acc * (2.0 * jnp.maximum(u_t, 0.0)),), extras=(u,))
    dw_down = _mm(act, dx3, mode="tn", tm=512, tn=512, out_dtypes=[F32], name="dw_down")
    dw_up_sh = _mm(h2, du, mode="tn", tm=D, tn=DFF // N_DEV, out_sharded=True, out_dtypes=[F32], name="dw_up")
    dh2 = _mm(du, w_up_sh, mode="nt", tm=512, tn=512, b_sharded=True, out_dtypes=[F32], name="mlp_up_bwd",
              deps=hooks.mlp_grads(dw_up_sh, dw_down))
    dx2, dg_mlp = _rms_bwd(x2, g_mlp, dh2, dx3, "rms_mlp_bwd")

    dmixed = _mm(dx2, w_out, mode="nt", tm=S, tn=CB, out_dtypes=[F32], name="out_proj_bwd")
    dw_out = _mm(mixed, dx2, mode="tn", tm=D, tn=CB, out_dtypes=[F32], name="dw_out")

    dyab, dproj = _gate_bwd(dmixed, proj_g, ya, yb)
    doa = _mm(dyab, w_a, mode="nt", tm=S, tn=CB, k=D, a_off=0, out_dtypes=[BF16], name="branch_a_bwd",
              deps=hooks.mid_backward(dyab))
    dw_a = _mm(oa, dyab, mode="tn", tm=FOX_W, tn=CB, n=D, b_off=0, out_dtypes=[F32], name="dw_branch_a")
    dob = _mm(dyab, w_b, mode="nt", tm=S, tn=CB, k=D, a_off=1, out_dtypes=[F32], name="branch_b_bwd")
    dw_b = _mm(ob, dyab, mode="tn", tm=DIL_OUT, tn=CB, n=D, b_off=D // CB, out_dtypes=[F32], name="dw_branch_b")

    dproj, dft, dfs = _fox_bwd(proj_a, fcol, frow, lse_a, doa, dproj)
    dproj, db = _fox_dscan(dft, dfs, proj_f, b_pad, dproj)

    do3, c3 = _dil_combine_bwd(dob, ob, alpha)
    dq3, dk3, dv3 = _dil_bwd(qkv3, lse3, do3, c3)
    dproj = _rope_merge_bwd(dq3, dk3, dv3, tables, dproj)

    dw_in_t = _mm(dproj, h1, mode="tn", tm=CB, tn=D, out_dtypes=[F32], name="dw_in")
    dh1 = _mm(dproj, w_in_t, mode="nn", tm=512, tn=CB, out_dtypes=[F32], name="proj_bwd")
    dx, dg_attn = _rms_bwd(x, g_attn, dh1, dx2, "rms_attn_bwd")

    return loss, dx, (dw_in_t, dw_a, dw_b, dw_out), (dg_attn, db, dg_mlp, dg_final)


SHARD_SHAPES = ((D_IN // N_DEV, D), (FOX_W, D // N_DEV), (DIL_OUT, D // N_DEV), (D // N_DEV, D),
                (D, DFF // N_DEV), (DFF // N_DEV, D))
W_NAMES = ("w_in", "w_a", "w_b", "w_out", "w_up", "w_down")
AG_COPIES = 7
HBM = pl.BlockSpec(memory_space=pltpu.HBM)
SEM = pl.BlockSpec(memory_space=pltpu.SEMAPHORE)


def _place():
    return lax.axis_index("x"), lax.axis_index("y"), lax.axis_index("c")


def _all_gather(shards):
    n_w = len(shards)
    shapes = [t.shape for t in shards]

    def body(*refs):
        in_refs, out_refs = refs[:n_w], refs[n_w:2 * n_w]
        stage, cast = refs[2 * n_w:3 * n_w], refs[3 * n_w:4 * n_w]
        send_sems, recv_sems, local_sems, load_sems = refs[4 * n_w:]
        x, y, c = _place()
        me, sibling = (x, y, c), (x, y, 1 - c)
        chips = [(1 - x, y), (x, 1 - y), (1 - x, 1 - y)]

        def slot(i, px, py, pc):
            return out_refs[i].at[4 * px + 2 * py + pc]

        def copy(i, k, block, to, src=None):
            return pltpu.make_async_remote_copy(
                src_ref=slot(i, *block) if src is None else src, dst_ref=slot(i, *block),
                send_sem=send_sems.at[i * AG_COPIES + k], recv_sem=recv_sems.at[i * AG_COPIES + k],
                device_id=to, device_id_type=MESH)

        loads = [pltpu.make_async_copy(in_refs[i], stage[i], load_sems.at[i]) for i in range(n_w)]
        for ld in loads:
            ld.start()
        local, remote = [], []
        for i in range(n_w):
            loads[i].wait()
            cast[i][...] = stage[i][...].astype(BF16)
            local.append(pltpu.make_async_copy(cast[i], slot(i, *me), local_sems.at[i]))
            local[-1].start()
            first = [copy(i, 0, me, sibling, src=cast[i])]
            first += [copy(i, 1 + j, me, (*chip, c), src=cast[i]) for j, chip in enumerate(chips)]
            for cp in first:
                cp.start()
            remote += first
        for i in range(n_w):
            for j, chip in enumerate(chips):
                copy(i, 1 + j, (*chip, c), me).wait_recv()
                remote.append(copy(i, 4 + j, (*chip, c), sibling))
                remote[-1].start()
        for i in range(n_w):
            copy(i, 0, sibling, me).wait_recv()
            for j, chip in enumerate(chips):
                copy(i, 4 + j, (*chip, 1 - c), me).wait_recv()
        for cp in remote:
            cp.wait_send()
        for cp in local:
            cp.wait()

    return pl.pallas_call(
        body, name="all_gather_w_in",
        out_shape=[jax.ShapeDtypeStruct((N_DEV,) + sh, BF16) for sh in shapes],
        in_specs=[ANY] * n_w, out_specs=[ANY] * n_w,
        scratch_shapes=[pltpu.VMEM(sh, F32) for sh in shapes] + [pltpu.VMEM(sh, BF16) for sh in shapes]
                       + [pltpu.SemaphoreType.DMA((n_w * AG_COPIES,)), pltpu.SemaphoreType.DMA((n_w * AG_COPIES,)),
                          pltpu.SemaphoreType.DMA((n_w,)), pltpu.SemaphoreType.DMA((n_w,))],
        compiler_params=_cparams(None),
    )(*shards)


def _place_own(shards):
    n_w = len(shards)
    shapes = [t.shape for t in shards]

    def body(*refs):
        in_refs, out_refs = refs[:n_w], refs[n_w:2 * n_w]
        stage, cast = refs[2 * n_w:3 * n_w], refs[3 * n_w:4 * n_w]
        load_sems, store_sems = refs[4 * n_w:]
        x, y, c = _place()
        loads = [pltpu.make_async_copy(in_refs[i], stage[i], load_sems.at[i]) for i in range(n_w)]
        for ld in loads:
            ld.start()
        stores = []
        for i in range(n_w):
            loads[i].wait()
            cast[i][...] = stage[i][...].astype(BF16)
            stores.append(pltpu.make_async_copy(cast[i], out_refs[i].at[4 * x + 2 * y + c], store_sems.at[i]))
            stores[-1].start()
        for st in stores:
            st.wait()

    return pl.pallas_call(
        body, name="place_own_shards",
        out_shape=[jax.ShapeDtypeStruct((N_DEV,) + sh, BF16) for sh in shapes],
        in_specs=[ANY] * n_w, out_specs=[ANY] * n_w,
        scratch_shapes=[pltpu.VMEM(sh, F32) for sh in shapes] + [pltpu.VMEM(sh, BF16) for sh in shapes]
                       + [pltpu.SemaphoreType.DMA((n_w,)), pltpu.SemaphoreType.DMA((n_w,))],
        compiler_params=_cparams(None),
    )(*shards)


PEER_COPIES = N_DEV - 1
SIBLING_COPIES = 4
CHIP_COPIES = 3


def _peer_copies(_, land_refs, send_sems, recv_sems):
    x, y, c = _place()
    mine = 4 * x + 2 * y + c
    copies = []
    for i, ref in enumerate(land_refs):
        for k in range(1, N_DEV):
            peer = (x ^ (k >> 2), y ^ ((k >> 1) & 1), c ^ (k & 1))
            n = i * PEER_COPIES + k - 1
            copies.append(pltpu.make_async_remote_copy(
                src_ref=ref.at[mine], dst_ref=ref.at[mine], send_sem=send_sems.at[n], recv_sem=recv_sems.at[n],
                device_id=peer, device_id_type=MESH))
    return copies


def _sibling_copies(g_refs, r_refs, send_sems, recv_sems):
    x, y, c = _place()
    return [pltpu.make_async_remote_copy(
        src_ref=g_refs[i].at[2 * k + (1 - c)], dst_ref=r_refs[i].at[k],
        send_sem=send_sems.at[SIBLING_COPIES * i + k], recv_sem=recv_sems.at[SIBLING_COPIES * i + k],
        device_id=(x, y, 1 - c), device_id_type=MESH) for i in range(len(g_refs)) for k in range(SIBLING_COPIES)]


def _chip_copies(p_refs, r_refs, send_sems, recv_sems):
    x, y, c = _place()
    chips = [(1 - x, y), (x, 1 - y), (1 - x, 1 - y)]
    return [pltpu.make_async_remote_copy(
        src_ref=p_refs[i].at[2 * cx + cy], dst_ref=r_refs[i].at[j],
        send_sem=send_sems.at[CHIP_COPIES * i + j], recv_sem=recv_sems.at[CHIP_COPIES * i + j],
        device_id=(cx, cy, c), device_id_type=MESH) for i in range(len(p_refs)) for j, (cx, cy) in enumerate(chips)]


def _exchange(name, copies_fn, n_copies, srcs, out_shapes):
    n = len(srcs)

    def body(*refs):
        copies = copies_fn(refs[:n], refs[n:n + len(out_shapes)], refs[-2], refs[-1])
        for cp in copies:
            cp.start()
        for cp in copies:
            cp.wait()

    return pl.pallas_call(
        body, name=name, out_shape=out_shapes,
        in_specs=[ANY] * n, out_specs=[ANY] * len(out_shapes),
        scratch_shapes=[pltpu.SemaphoreType.DMA((n_copies,)), pltpu.SemaphoreType.DMA((n_copies,))],
    )(*srcs)


def _in_hbm(a):
    return pltpu.with_memory_space_constraint(a, pltpu.HBM)


def _exchange_start(name, copies_fn, n_copies, srcs, lands, after=()):
    n_s, n_l, n_a = len(srcs), len(lands), len(after)

    def body(*refs):
        outs = refs[n_s + n_l + n_a:]
        for cp in copies_fn(refs[:n_s], refs[n_s:n_s + n_l], outs[0], outs[1]):
            cp.start()
        outs[-1][...] = jnp.zeros_like(outs[-1])

    res = pl.pallas_call(
        body, name=name,
        out_shape=[pltpu.SemaphoreType.DMA((n_copies,)), pltpu.SemaphoreType.DMA((n_copies,))]
                  + [pltpu.HBM(t.shape, t.dtype) for t in (*srcs, *lands)] + [jax.ShapeDtypeStruct((8, LANES), F32)],
        in_specs=[HBM] * (n_s + n_l) + [ANY] * n_a,
        out_specs=[SEM, SEM] + [HBM] * (n_s + n_l) + [pl.BlockSpec(memory_space=pltpu.VMEM)],
        input_output_aliases={i: 2 + i for i in range(n_s + n_l)},
        compiler_params=pltpu.CompilerParams(has_side_effects=pltpu.SideEffectType.DATAFLOW_SIDE_EFFECTING),
    )(*[_in_hbm(t) for t in (*srcs, *lands)], *after)
    return res[0], res[1], list(res[2:2 + n_s]), list(res[2 + n_s:2 + n_s + n_l]), res[-1]


def _exchange_wait(name, copies_fn, started, after):
    send_sems, recv_sems, srcs, lands, _ = started
    n_s, n_l = len(srcs), len(lands)

    def body(*refs):
        for cp in copies_fn(refs[:n_s], refs[n_s:n_s + n_l], refs[n_s + n_l], refs[n_s + n_l + 1]):
            cp.wait_send()
            cp.wait_recv()

    res = pl.pallas_call(
        body, name=name,
        out_shape=[pltpu.HBM(t.shape, t.dtype) for t in (*srcs, *lands)],
        in_specs=[HBM] * (n_s + n_l) + [SEM, SEM, ANY],
        out_specs=[HBM] * (n_s + n_l),
        input_output_aliases={i: i for i in range(n_s + n_l)},
        compiler_params=pltpu.CompilerParams(has_side_effects=pltpu.SideEffectType.DATAFLOW_SIDE_EFFECTING),
    )(*srcs, *lands, send_sems, recv_sems, after)
    return list(res[:n_s]), list(res[n_s:])


def _shard_block(shape):
    rows, cols = shape
    if rows % 256:
        return rows, 256
    return min(rows, 512 if cols <= LANES else 256), cols


def _chip_partial(ids, g_stack, recv1, name):
    shape = g_stack.shape[1:]
    br, bc = _shard_block(shape)

    def body(ids_ref, g_ref, r_ref, pb_ref, own_ref):
        s = g_ref[...] + r_ref[...]
        pb_ref[...] = s.astype(BF16)

        @pl.when(pl.program_id(2) == ids_ref[1])
        def _():
            own_ref[...] = s

    grid_spec = pltpu.PrefetchScalarGridSpec(
        num_scalar_prefetch=1, grid=(shape[0] // br, shape[1] // bc, 4),
        in_specs=[pl.BlockSpec((None, br, bc), lambda r, q, k, ids: (2 * k + ids[0], r, q)),
                  pl.BlockSpec((None, br, bc), lambda r, q, k, ids: (k, r, q))],
        out_specs=[pl.BlockSpec((None, br, bc), lambda r, q, k, ids: (k, r, q)),
                   pl.BlockSpec((br, bc), lambda r, q, k, ids: (r, q))])
    return pl.pallas_call(
        body, name=name, grid_spec=grid_spec,
        out_shape=[jax.ShapeDtypeStruct((4,) + shape, BF16), jax.ShapeDtypeStruct(shape, F32)],
        compiler_params=_cparams(("parallel", "parallel", "arbitrary")),
    )(ids, g_stack, recv1)


def _adamw(w, g, m, v):
    m = ADAM_B1 * m + (1.0 - ADAM_B1) * g
    v = ADAM_B2 * v + (1.0 - ADAM_B2) * (g * g)
    m_hat = m / (1.0 - ADAM_B1 ** ADAM_STEP)
    v_hat = v / (1.0 - ADAM_B2 ** ADAM_STEP)
    delta = -ADAM_LR * (m_hat / (jnp.sqrt(v_hat) + ADAM_EPS) + ADAM_WD * w)
    return delta, m, v


def _reduce_adamw(own, recv2, w, m, v, name):
    shape = own.shape
    br, bc = _shard_block(shape)

    def body(own_ref, r_ref, w_ref, m_ref, v_ref, g_ref, d_ref, nm_ref, nv_ref):
        g = own_ref[...]
        for j in range(3):
            g = g + r_ref[j].astype(F32)
        delta, nm, nv = _adamw(w_ref[...], g, m_ref[...], v_ref[...])
        g_ref[...] = g
        d_ref[...] = delta
        nm_ref[...] = nm
        nv_ref[...] = nv

    blk = pl.BlockSpec((br, bc), lambda r, q: (r, q))
    return pl.pallas_call(
        body, name=name, grid=(shape[0] // br, shape[1] // bc),
        in_specs=[blk, pl.BlockSpec((3, br, bc), lambda r, q: (0, r, q)), blk, blk, blk],
        out_specs=[blk] * 4, out_shape=[jax.ShapeDtypeStruct(shape, F32)] * 4,
        compiler_params=_cparams(("parallel", "parallel")),
    )(own, recv2, w, m, v)


def _small_allreduce_adamw(gvec, w, m, v):
    def body(g_ref, w_ref, m_ref, v_ref, go_ref, d_ref, nm_ref, nv_ref, buf, send_sems, recv_sems):
        x, y, c = _place()
        my_slot = 4 * x + 2 * y + c
        buf[my_slot] = g_ref[...]
        copies = []
        for k in range(1, N_DEV):
            px, py, pc = x ^ (k >> 2), y ^ ((k >> 1) & 1), c ^ (k & 1)
            copies.append(pltpu.make_async_remote_copy(
                src_ref=g_ref, dst_ref=buf.at[my_slot], send_sem=send_sems.at[k - 1], recv_sem=recv_sems.at[k - 1],
                device_id=(px, py, pc), device_id_type=MESH))
        for cp in copies:
            cp.start()
        for k in range(1, N_DEV):
            px, py, pc = x ^ (k >> 2), y ^ ((k >> 1) & 1), c ^ (k & 1)
            pltpu.make_async_remote_copy(
                src_ref=g_ref, dst_ref=buf.at[4 * px + 2 * py + pc], send_sem=send_sems.at[k - 1],
                recv_sem=recv_sems.at[k - 1], device_id=(px, py, pc), device_id_type=MESH).wait_recv()
        for cp in copies:
            cp.wait_send()
        g = buf[0]
        for s in range(1, N_DEV):
            g = g + buf[s]
        delta, nm, nv = _adamw(w_ref[...], g, m_ref[...], v_ref[...])
        go_ref[...] = g
        d_ref[...] = delta
        nm_ref[...] = nm
        nv_ref[...] = nv

    vm = pl.BlockSpec(memory_space=pltpu.VMEM)
    return pl.pallas_call(
        body, name="small_allreduce_adamw",
        in_specs=[vm] * 4, out_specs=[vm] * 4,
        out_shape=[jax.ShapeDtypeStruct((SMALL_R, LANES), F32)] * 4,
        scratch_shapes=[pltpu.VMEM((N_DEV, SMALL_R, LANES), F32),
                        pltpu.SemaphoreType.DMA((N_DEV - 1,)), pltpu.SemaphoreType.DMA((N_DEV - 1,))],
    )(gvec, w, m, v)


def _cols_to_whole(stack):
    rows = stack.shape[1]
    return stack.transpose(1, 0, 2).reshape(rows, -1)


def _whole_to_cols(t):
    rows = t.shape[0]
    return t.reshape(rows, N_DEV, -1).transpose(1, 0, 2)


def _pad_w_in_t(wt):
    n_a = 3 * FOX_W
    n_b = n_a + FOX_H
    return jnp.concatenate([wt[:n_a], wt[n_b:n_b + 3 * DIL_W], jnp.zeros((CB, D), wt.dtype),
                            wt[n_b + 3 * DIL_W:], wt[n_a:n_b], jnp.zeros((CB - FOX_H, D), wt.dtype)], axis=0)


def _unpad_dw_in_t(dwp):
    return jnp.concatenate([dwp[:SEG_B], dwp[SEG_F:SEG_F + FOX_H], dwp[SEG_B:SEG_Z], dwp[SEG_G:SEG_F]], axis=0)


def _pack_small(g_attn, b, g_mlp, g_final):
    b_rows = jnp.pad(b, ((0, 7), (0, LANES - b.shape[1])))
    return jnp.concatenate([g_attn.reshape(8, LANES), g_mlp.reshape(8, LANES), g_final.reshape(8, LANES), b_rows], axis=0)


def _unpack_small(p):
    return (p[0:8].reshape(1, D), p[24:25, :FOX_H], p[8:16].reshape(1, D), p[16:24].reshape(D))


class _StepComm:
    def __init__(self, ids, late_shards, after_w_in):
        self.ids = ids
        lands = _place_own(late_shards)
        self.gather = _exchange_start("gather_late_start", _peer_copies, PEER_COPIES * len(lands), [], lands,
                                      after=(after_w_in,))
        self.sibling = None
        self.chips = None
        self.own = None

    def first_deps(self):
        return [self.gather[-1]]

    def late_weights(self, after):
        _, (g_a, g_b, g_out, g_up, g_down) = _exchange_wait("gather_late_wait", _peer_copies, self.gather, after)
        return _cols_to_whole(g_a), _cols_to_whole(g_b), g_out.reshape(D, D), g_up, g_down.reshape(DFF, D)

    def mlp_grads(self, dw_up_sh, dw_down):
        srcs = [dw_up_sh, dw_down.reshape((N_DEV,) + SHARD_SHAPES[5])]
        lands = [lax.empty((SIBLING_COPIES,) + t.shape[1:], F32) for t in srcs]
        self.sibling = _exchange_start("grad_mlp_sibling_start", _sibling_copies, SIBLING_COPIES * 2, srcs, lands)
        return [self.sibling[-1]]

    def mid_backward(self, after):
        grads, recv1 = _exchange_wait("grad_mlp_sibling_wait", _sibling_copies, self.sibling, after)
        parts = [_chip_partial(self.ids, grads[i], recv1[i], "grad_partial_" + W_NAMES[4 + i]) for i in range(2)]
        self.own = [p[1] for p in parts]
        srcs = [p[0] for p in parts]
        lands = [lax.empty((CHIP_COPIES,) + t.shape[1:], BF16) for t in srcs]
        self.chips = _exchange_start("grad_mlp_chips_start", _chip_copies, CHIP_COPIES * 2, srcs, lands)
        return [self.chips[-1]]

    def mlp_reduced(self, after):
        _, recv2 = _exchange_wait("grad_mlp_chips_wait", _chip_copies, self.chips, after)
        return list(zip(self.own, recv2))


def kernel(x, norm_attn_g, w_in, b_forget, w_branch_a, w_branch_b, w_out, norm_mlp_g, w_up, w_down, norm_final_g, loss_target, m_norm_attn_g, m_w_in, m_b_forget, m_w_branch_a, m_w_branch_b, m_w_out, m_norm_mlp_g, m_w_up, m_w_down, m_norm_final_g, v_norm_attn_g, v_w_in, v_b_forget, v_w_branch_a, v_w_branch_b, v_w_out, v_norm_mlp_g, v_w_up, v_w_down, v_norm_final_g):
    cx, cy, cc = _place()
    ids = jnp.stack([cc, 2 * cx + cy]).astype(jnp.int32)

    w_sh = [w_in[0].T] + [t[0] for t in (w_branch_a, w_branch_b, w_out, w_up, w_down)]
    m_sh = [m_w_in[0].T] + [t[0] for t in (m_w_branch_a, m_w_branch_b, m_w_out, m_w_up, m_w_down)]
    v_sh = [v_w_in[0].T] + [t[0] for t in (v_w_branch_a, v_w_branch_b, v_w_out, v_w_up, v_w_down)]

    (g_in,) = _all_gather(w_sh[:1])
    w_in_t = _pad_w_in_t(g_in.reshape(D_IN, D))
    comm = _StepComm(ids, w_sh[1:], g_in)
    b_pad = jnp.pad(b_forget, ((0, 0), (0, LANES - FOX_H)))

    loss_row, dx, dws, dsmall = _local_step(
        x[0], loss_target[0], norm_attn_g, norm_mlp_g, norm_final_g.reshape(1, D), b_pad, w_in_t, comm)
    loss = lax.psum(loss_row[0, 0], ("x", "y", "c"))

    dw_in_t, dw_a, dw_b, dw_out = dws
    grads = [_unpad_dw_in_t(dw_in_t).reshape((N_DEV,) + SHARD_SHAPES[0]), _whole_to_cols(dw_a), _whole_to_cols(dw_b),
             dw_out.reshape((N_DEV,) + SHARD_SHAPES[3])]
    n_g = len(grads)
    recv1 = _exchange("grad_sibling", _sibling_copies, SIBLING_COPIES * n_g, grads,
                      [jax.ShapeDtypeStruct((SIBLING_COPIES,) + t.shape[1:], F32) for t in grads])
    parts = [_chip_partial(ids, grads[i], recv1[i], "grad_partial_" + W_NAMES[i]) for i in range(n_g)]
    recv2 = _exchange("grad_chips", _chip_copies, CHIP_COPIES * n_g, [p[0] for p in parts],
                      [jax.ShapeDtypeStruct((CHIP_COPIES,) + t.shape[1:], BF16) for t in grads])
    reduced = [(parts[i][1], recv2[i]) for i in range(n_g)] + comm.mlp_reduced(recv2[0])
    big = [_reduce_adamw(*reduced[i], w_sh[i], m_sh[i], v_sh[i], "adamw_" + W_NAMES[i]) for i in range(len(w_sh))]

    dg_attn, db, dg_mlp, dg_final = dsmall
    small = _small_allreduce_adamw(
        _pack_small(dg_attn, db, dg_mlp, dg_final),
        _pack_small(norm_attn_g, b_forget, norm_mlp_g, norm_final_g.reshape(1, D)),
        _pack_small(m_norm_attn_g, m_b_forget, m_norm_mlp_g, m_norm_final_g.reshape(1, D)),
        _pack_small(v_norm_attn_g, v_b_forget, v_norm_mlp_g, v_norm_final_g.reshape(1, D)))

    outs = [loss, dx[None]]
    for q in range(4):
        s_attn, s_b, s_mlp, s_final = _unpack_small(small[q])
        b_in, b_a, b_b, b_out, b_up, b_down = [t[q][None] for t in big]
        b_in = jnp.swapaxes(b_in, 1, 2)
        outs += [s_attn, b_in, s_b, b_a, b_b, b_out, s_mlp, b_up, b_down, s_final]
    return tuple(outs)
```

```python
import functools

import jax
import jax.numpy as jnp
from jax import lax
from jax.experimental import pallas as pl
from jax.experimental.pallas import tpu as pltpu

F32 = jnp.float32
BF16 = jnp.bfloat16
MESH = pl.DeviceIdType.MESH

S = 2048
D = 1024
HD = 64
FOX_H = 8
FOX_W = FOX_H * HD
DIL_HG = 4
DIL_G = 3
DIL_W = DIL_G * DIL_HG * HD
DIL_OUT = DIL_HG * HD
DIL_BLK = 128
DIL_R = (1, 4, 16)
DFF = 4 * D
D_IN = 3 * FOX_W + FOX_H + 3 * DIL_W + 2 * D
EPS = 1e-6
NEG_INF = -1e30
SCALE = HD ** -0.5
ROPE_THETA = 500000.0
ROPE_DIM = HD // 4
N_DEV = 8

ADAM_LR = 0.001
ADAM_B1 = 0.9
ADAM_B2 = 0.999
ADAM_EPS = 1e-08
ADAM_WD = 0.01
ADAM_STEP = 10

LANES = 128
CB = 256
SEG_A = 0
SEG_B = 3 * FOX_W
SEG_Z = SEG_B + 3 * DIL_W
SEG_G = SEG_Z + CB
SEG_F = SEG_G + 2 * D
D_PAD = SEG_F + CB
SMALL_R = 32

VMEM_MB = 56


def _cparams(dims=None, vmem_mb=VMEM_MB, **kw):
    return pltpu.CompilerParams(dimension_semantics=dims, vmem_limit_bytes=vmem_mb << 20, **kw)


ANY = pl.BlockSpec(memory_space=pl.ANY)

_NN = (((1,), (0,)), ((), ()))
_NT = (((1,), (1,)), ((), ()))
_TN = (((0,), (0,)), ((), ()))


def _dot(a, b, dims):
    return lax.dot_general(a.astype(BF16), b.astype(BF16), dims, preferred_element_type=F32)


def _mm(a, b, *, mode, tm, tn, out_dtypes, name, n=None, k=None, a_off=0, b_off=0,
        b_sharded=False, out_sharded=False, epilogue=None, extras=(), deps=()):
    n_sh = b.shape[-1] if b_sharded else None
    if mode == "nn":
        m = a.shape[0]
        k = k or a.shape[1]
        a_spec = pl.BlockSpec((tm, k), lambda i, j: (i, a_off))
        if b_sharded:
            assert tn == n_sh
            n = N_DEV * n_sh
            b_spec = pl.BlockSpec((None, k, tn), lambda i, j: (j, 0, 0))
        else:
            n = n or b.shape[1]
            b_spec = pl.BlockSpec((k, tn), lambda i, j: (0, j + b_off))
        dims = _NN
    elif mode == "nt":
        m = a.shape[0]
        k = k or a.shape[1]
        a_spec = pl.BlockSpec((tm, k), lambda i, j: (i, a_off))
        if b_sharded:
            n = b.shape[1]
            b_spec = pl.BlockSpec((N_DEV, tn, n_sh), lambda i, j: (0, j, 0))
        else:
            n = n or b.shape[0]
            b_spec = pl.BlockSpec((tn, k), lambda i, j: (j + b_off, 0))
        dims = _NT
    else:
        k, m = a.shape
        n = n or b.shape[1]
        a_spec = pl.BlockSpec((k, tm), lambda i, j: (0, i))
        b_spec = pl.BlockSpec((k, tn), lambda i, j: (0, j + b_off))
        dims = _TN
    assert m % tm == 0 and n % tn == 0, (name, m, n, tm, tn)
    n_extra = len(extras)
    tile = pl.BlockSpec((tm, tn), lambda i, j: (i, j))
    if out_sharded:
        assert mode == "tn" and n // tn == N_DEV
        out_spec = pl.BlockSpec((None, tm, tn), lambda i, j: (j, i, 0))
        out_shape = (N_DEV, m, tn)
    else:
        out_spec, out_shape = tile, (m, n)

    def body(a_ref, b_ref, *refs):
        if mode == "nt" and b_sharded:
            acc = _dot(a_ref[:, 0:n_sh], b_ref[0], dims)
            for p in range(1, N_DEV):
                acc = acc + _dot(a_ref[:, p * n_sh:(p + 1) * n_sh], b_ref[p], dims)
        else:
            acc = _dot(a_ref[...], b_ref[...], dims)
        ex = [r[...] for r in refs[:n_extra]]
        outs = epilogue(acc, *ex) if epilogue is not None else (acc,)
        for o_ref, o in zip(refs[n_extra + len(deps):], outs):
            o_ref[...] = o.astype(o_ref.dtype)

    res = pl.pallas_call(
        body, name=name, grid=(m // tm, n // tn),
        in_specs=[a_spec, b_spec] + [tile] * n_extra + [ANY] * len(deps),
        out_specs=[out_spec] * len(out_dtypes),
        out_shape=[jax.ShapeDtypeStruct(out_shape, dt) for dt in out_dtypes],
        compiler_params=_cparams(("parallel", "parallel")),
    )(a, b, *extras, *deps)
    return res if len(out_dtypes) > 1 else res[0]


ROW_T = 256


def _rms_fwd(x, g, name, deps=()):
    def body(x_ref, g_ref, *refs):
        o_ref = refs[-1]
        x = x_ref[...]
        r = lax.rsqrt(jnp.mean(x * x, axis=-1, keepdims=True) + EPS)
        o_ref[...] = ((x * r) * g_ref[...]).astype(BF16)

    return pl.pallas_call(
        body, name=name, grid=(S // ROW_T,),
        in_specs=[pl.BlockSpec((ROW_T, D), lambda i: (i, 0)), pl.BlockSpec((1, D), lambda i: (0, 0))] + [ANY] * len(deps),
        out_specs=pl.BlockSpec((ROW_T, D), lambda i: (i, 0)),
        out_shape=jax.ShapeDtypeStruct((S, D), BF16),
        compiler_params=_cparams(("parallel",)),
    )(x, g, *deps)


def _rms_bwd(x, g, dh, dres, name):
    def body(x_ref, g_ref, dh_ref, dres_ref, dx_ref, dg_ref):
        @pl.when(pl.program_id(0) == 0)
        def _():
            dg_ref[...] = jnp.zeros_like(dg_ref)

        x = x_ref[...]
        dh = dh_ref[...]
        r = lax.rsqrt(jnp.mean(x * x, axis=-1, keepdims=True) + EPS)
        xn = x * r
        dhn = dh * g_ref[...]
        dx_ref[...] = dres_ref[...] + r * (dhn - xn * jnp.mean(dhn * xn, axis=-1, keepdims=True))
        dg_ref[...] += jnp.sum(dh * xn, axis=0, keepdims=True)

    row = pl.BlockSpec((ROW_T, D), lambda i: (i, 0))
    vec = pl.BlockSpec((1, D), lambda i: (0, 0))
    return pl.pallas_call(
        body, name=name, grid=(S // ROW_T,),
        in_specs=[row, vec, row, row], out_specs=[row, vec],
        out_shape=[jax.ShapeDtypeStruct((S, D), F32), jax.ShapeDtypeStruct((1, D), F32)],
        compiler_params=_cparams(("arbitrary",)),
    )(x, g, dh, dres)


def _final_norm_loss(x, g, tgt):
    def body(x_ref, g_ref, t_ref, dx_ref, dg_ref, loss_ref):
        @pl.when(pl.program_id(0) == 0)
        def _():
            dg_ref[...] = jnp.zeros_like(dg_ref)
            loss_ref[...] = jnp.zeros_like(loss_ref)

        x = x_ref[...]
        g = g_ref[...]
        r = lax.rsqrt(jnp.mean(x * x, axis=-1, keepdims=True) + EPS)
        xn = x * r
        err = xn * g - t_ref[...]
        row_loss = jnp.mean(err * err, axis=-1, keepdims=True)
        loss_ref[...] += 0.5 * jnp.sum(row_loss, axis=0, keepdims=True) * jnp.ones((1, LANES), F32)
        dy = err * (1.0 / D)
        dyn = dy * g
        dx_ref[...] = r * (dyn - xn * jnp.mean(dyn * xn, axis=-1, keepdims=True))
        dg_ref[...] += jnp.sum(dy * xn, axis=0, keepdims=True)

    row = pl.BlockSpec((ROW_T, D), lambda i: (i, 0))
    vec = pl.BlockSpec((1, D), lambda i: (0, 0))
    return pl.pallas_call(
        body, name="final_norm_loss", grid=(S // ROW_T,),
        in_specs=[row, vec, row],
        out_specs=[row, vec, pl.BlockSpec((1, LANES), lambda i: (0, 0))],
        out_shape=[jax.ShapeDtypeStruct((S, D), F32), jax.ShapeDtypeStruct((1, D), F32),
                   jax.ShapeDtypeStruct((1, LANES), F32)],
        compiler_params=_cparams(("arbitrary",)),
    )(x, g, tgt)


def _sigmoid(z):
    return 1.0 / (1.0 + jnp.exp(-z))


def _gate_fwd(proj_g, ya, yb):
    def body(ga_ref, gb_ref, ya_ref, yb_ref, o_ref):
        o_ref[...] = (_sigmoid(ga_ref[...]) * ya_ref[...] + _sigmoid(gb_ref[...]) * yb_ref[...]).astype(BF16)

    row = pl.BlockSpec((ROW_T, D), lambda i: (i, 0))
    return pl.pallas_call(
        body, name="gate_fwd", grid=(S // ROW_T,),
        in_specs=[row, pl.BlockSpec((ROW_T, D), lambda i: (i, 1)), row, row],
        out_specs=row, out_shape=jax.ShapeDtypeStruct((S, D), BF16),
        compiler_params=_cparams(("parallel",)),
    )(proj_g, proj_g, ya, yb)


GATE_TR = 1024
GATE_TC = 512
GATE_NB = 2 * D // GATE_TC


def _gate_bwd(dmixed, proj_g, ya, yb):
    half = GATE_NB // 2

    def body(dm_ref, g_ref, ya_ref, yb_ref, dy_ref, dg_ref):
        dm = dm_ref[...]
        s = _sigmoid(g_ref[...])
        y = jnp.where(pl.program_id(1) < half, ya_ref[...], yb_ref[...])
        dy_ref[...] = (dm * s).astype(BF16)
        dg_ref[...] = (dm * y * (s * (1.0 - s))).astype(BF16)

    blk = pl.BlockSpec((GATE_TR, GATE_TC), lambda i, j: (i, j))
    wrapped = pl.BlockSpec((GATE_TR, GATE_TC), lambda i, j: (i, j % half))
    return pl.pallas_call(
        body, name="gate_bwd", grid=(S // GATE_TR, GATE_NB),
        in_specs=[wrapped, blk, wrapped, wrapped],
        out_specs=[blk, pl.BlockSpec((GATE_TR, GATE_TC), lambda i, j: (i, j + SEG_G // GATE_TC))],
        out_shape=[jax.ShapeDtypeStruct((S, 2 * D), BF16), jax.ShapeDtypeStruct((S, D_PAD), BF16)],
        compiler_params=_cparams(("parallel", "parallel")),
    )(dmixed, proj_g, ya, yb)


FOX_TQ = 256


def _scan_rows(x, reverse):
    n = x.shape[0]
    row = lax.broadcasted_iota(jnp.int32, x.shape, 0)
    k = 1
    while k < n:
        if reverse:
            x = x + jnp.where(row < n - k, pltpu.roll(x, n - k, 0), 0.0)
        else:
            x = x + jnp.where(row >= k, pltpu.roll(x, k, 0), 0.0)
        k *= 2
    return x


def _fox_scan(proj_f, b_pad):
    def body(f_ref, b_ref, fc_ref, fr_ref):
        z = f_ref[...] + b_ref[...]
        lf = jnp.minimum(z, 0.0) - jnp.log1p(jnp.exp(-jnp.abs(z)))
        f = _scan_rows(lf, reverse=False)
        fc_ref[...] = f
        fr_ref[...] = f.T

    return pl.pallas_call(
        body, name="fox_scan", grid=(1,),
        in_specs=[pl.BlockSpec((S, LANES), lambda i: (0, 0)), pl.BlockSpec((1, LANES), lambda i: (0, 0))],
        out_specs=[pl.BlockSpec((S, LANES), lambda i: (0, 0)), pl.BlockSpec((LANES, S), lambda i: (0, 0))],
        out_shape=[jax.ShapeDtypeStruct((S, LANES), F32), jax.ShapeDtypeStruct((LANES, S), F32)],
        compiler_params=_cparams(("arbitrary",)),
    )(proj_f, b_pad)


def _fox_dscan(dft, dfs, proj_f, b_pad, dproj):
    def body(dft_ref, dfs_ref, f_ref, b_ref, _, dfa_ref, db_ref):
        dfs_pad = jnp.concatenate([dfs_ref[...], jnp.zeros((LANES - FOX_H, S), F32)], axis=0)
        df = dfs_pad.T + dft_ref[0]
        for hp in range(1, dft_ref.shape[0]):
            df = df + pltpu.roll(dft_ref[hp], 2 * hp, 1)
        dlf = _scan_rows(df, reverse=True)
        z = f_ref[...] + b_ref[...]
        lane = lax.broadcasted_iota(jnp.int32, (S, LANES), 1)
        dfa = jnp.where(lane < FOX_H, dlf / (1.0 + jnp.exp(z)), 0.0)
        dfa_ref[:, :LANES] = dfa.astype(BF16)
        dfa_ref[:, LANES:] = jnp.zeros((S, CB - LANES), BF16)
        db_ref[...] = jnp.sum(dfa, axis=0, keepdims=True)

    return pl.pallas_call(
        body, name="fox_dscan", grid=(1,),
        in_specs=[pl.BlockSpec(dft.shape, lambda i: (0, 0, 0)), pl.BlockSpec((FOX_H, S), lambda i: (0, 0)),
                  pl.BlockSpec((S, LANES), lambda i: (0, 0)), pl.BlockSpec((1, LANES), lambda i: (0, 0)), ANY],
        out_specs=[pl.BlockSpec((S, CB), lambda i: (0, SEG_F // CB)), pl.BlockSpec((1, LANES), lambda i: (0, 0))],
        out_shape=[jax.ShapeDtypeStruct((S, D_PAD), BF16), jax.ShapeDtypeStruct((1, LANES), F32)],
        input_output_aliases={4: 0},
        compiler_params=_cparams(("arbitrary",)),
    )(dft, dfs, proj_f, b_pad, dproj)


FOX_NQ = S // FOX_TQ
FOX_HP = FOX_W // LANES
HEADS_PER_LB = LANES // HD


def _fox_scores(q, k, ft_all, fs_all, head, w):
    lane = lax.broadcasted_iota(jnp.int32, ft_all.shape, 1)
    sub = lax.broadcasted_iota(jnp.int32, fs_all.shape, 0)
    ft = jnp.sum(jnp.where(lane == head, ft_all, 0.0), axis=-1, keepdims=True)
    fs = jnp.sum(jnp.where(sub == head, fs_all, 0.0), axis=0, keepdims=True)
    s = (_dot(q, k, _NT) * SCALE + ft) - fs
    row = lax.broadcasted_iota(jnp.int32, (FOX_TQ, FOX_TQ), 0)
    col = lax.broadcasted_iota(jnp.int32, (FOX_TQ, FOX_TQ), 1)
    diag = jnp.where(col <= row, s[:, w * FOX_TQ:], NEG_INF)
    return diag if w == 0 else jnp.concatenate([s[:, :w * FOX_TQ], diag], axis=1)


def _fox_fwd(proj_a, fcol, frow):
    def body(q_ref, k_ref, v_ref, ft_ref, fs_ref, o_ref, lse_ref):
        hp = pl.program_id(0)
        qi = pl.program_id(1)
        lane = lax.broadcasted_iota(jnp.int32, (FOX_TQ, LANES), 1)
        for w in range(FOX_NQ):
            @pl.when(qi == w)
            def _(w=w):
                width = (w + 1) * FOX_TQ
                ft_all = ft_ref[...]
                fs_all = fs_ref[:, :width]
                lse = jnp.zeros((FOX_TQ, LANES), F32)
                for hh in range(HEADS_PER_LB):
                    sl = slice(hh * HD, (hh + 1) * HD)
                    s = _fox_scores(q_ref[:, sl], k_ref[:width, sl], ft_all, fs_all, hp * HEADS_PER_LB + hh, w)
                    m = jnp.max(s, axis=-1, keepdims=True)
                    p = jnp.exp(s - m)
                    l = jnp.sum(p, axis=-1, keepdims=True)
                    o_ref[:, sl] = _dot(p / l, v_ref[:width, sl], _NN)
                    lse = jnp.where(lane == hh, m + jnp.log(l), lse)
                lse_ref[...] = lse

    return pl.pallas_call(
        body, name="fox_fwd", grid=(FOX_HP, FOX_NQ),
        in_specs=[pl.BlockSpec((FOX_TQ, LANES), lambda h, i: (i, h)),
                  pl.BlockSpec((S, LANES), lambda h, i: (0, FOX_HP + h)),
                  pl.BlockSpec((S, LANES), lambda h, i: (0, 2 * FOX_HP + h)),
                  pl.BlockSpec((FOX_TQ, LANES), lambda h, i: (i, 0)),
                  pl.BlockSpec((FOX_H, S), lambda h, i: (0, 0))],
        out_specs=[pl.BlockSpec((FOX_TQ, LANES), lambda h, i: (i, h)),
                   pl.BlockSpec((None, FOX_TQ, LANES), lambda h, i: (h, i, 0))],
        out_shape=[jax.ShapeDtypeStruct((S, FOX_W), F32), jax.ShapeDtypeStruct((FOX_HP, S, LANES), F32)],
        compiler_params=_cparams(("parallel", "parallel")),
    )(proj_a, proj_a, proj_a, fcol, frow)


def _fox_bwd(proj_a, fcol, frow, lse, do, dproj):
    n_q = FOX_NQ

    def body(q_ref, k_ref, v_ref, ft_ref, fs_ref, lse_ref, do_ref, _,
             dqkv_ref, dft_ref, dfs_ref, dk_acc, dv_acc):
        hp = pl.program_id(0)
        t = pl.program_id(1)

        @pl.when(t == 0)
        def _():
            dk_acc[...] = jnp.zeros_like(dk_acc)
            dv_acc[...] = jnp.zeros_like(dv_acc)

        @pl.when((t == 0) & (hp == 0))
        def _():
            dfs_ref[...] = jnp.zeros_like(dfs_ref)

        lane = lax.broadcasted_iota(jnp.int32, (FOX_TQ, LANES), 1)
        for w in range(n_q):
            @pl.when(t == w)
            def _(w=w):
                width = (w + 1) * FOX_TQ
                ft_all = ft_ref[...]
                fs_all = fs_ref[:, :width]
                lse_all = lse_ref[...]
                sub = lax.broadcasted_iota(jnp.int32, (FOX_H, width), 0)
                dft = jnp.zeros((FOX_TQ, LANES), F32)
                for hh in range(HEADS_PER_LB):
                    sl = slice(hh * HD, (hh + 1) * HD)
                    head = hp * HEADS_PER_LB + hh
                    q = q_ref[:, sl]
                    k = k_ref[:width, sl]
                    do_h = do_ref[:, sl]
                    s = _fox_scores(q, k, ft_all, fs_all, head, w)
                    p = jnp.exp(s - lse_all[:, hh:hh + 1])
                    dp = _dot(do_h, v_ref[:width, sl], _NT)
                    ds = p * (dp - jnp.sum(dp * p, axis=-1, keepdims=True))
                    dft = jnp.where(lane == hh, jnp.sum(ds, axis=-1, keepdims=True), dft)
                    dfs_ref[:, :width] -= jnp.where(sub == head, jnp.sum(ds, axis=0, keepdims=True), 0.0)
                    dsb = (ds * SCALE).astype(BF16)
                    dqkv_ref[:, sl] = _dot(dsb, k, _NN).astype(BF16)
                    dk_acc[:width, sl] += _dot(dsb, q, _TN)
                    dv_acc[:width, sl] += _dot(p, do_h, _TN)
                dft_ref[...] = dft

        rows = pl.ds(pl.multiple_of((t % n_q) * FOX_TQ, FOX_TQ), FOX_TQ)

        @pl.when((t >= n_q) & (t < 2 * n_q))
        def _():
            dqkv_ref[...] = dk_acc[rows, :].astype(BF16)

        @pl.when(t >= 2 * n_q)
        def _():
            dqkv_ref[...] = dv_acc[rows, :].astype(BF16)

    def qidx(t):
        return jnp.minimum(t, n_q - 1)

    qblk = pl.BlockSpec((FOX_TQ, LANES), lambda h, t: (qidx(t), h))
    lane_blk = pl.BlockSpec((None, FOX_TQ, LANES), lambda h, t: (h, qidx(t), 0))
    return pl.pallas_call(
        body, name="fox_bwd", grid=(FOX_HP, 3 * n_q),
        in_specs=[qblk, pl.BlockSpec((S, LANES), lambda h, t: (0, FOX_HP + h)),
                  pl.BlockSpec((S, LANES), lambda h, t: (0, 2 * FOX_HP + h)),
                  pl.BlockSpec((FOX_TQ, LANES), lambda h, t: (qidx(t), 0)),
                  pl.BlockSpec((FOX_H, S), lambda h, t: (0, 0)), lane_blk, qblk, ANY],
        out_specs=[pl.BlockSpec((FOX_TQ, LANES), lambda h, t: (t % n_q, (t // n_q) * FOX_HP + h)), lane_blk,
                   pl.BlockSpec((FOX_H, S), lambda h, t: (0, 0))],
        out_shape=[jax.ShapeDtypeStruct((S, D_PAD), BF16),
                   jax.ShapeDtypeStruct((FOX_HP, S, LANES), F32), jax.ShapeDtypeStruct((FOX_H, S), F32)],
        scratch_shapes=[pltpu.VMEM((S, LANES), F32), pltpu.VMEM((S, LANES), F32)],
        input_output_aliases={7: 0},
        compiler_params=_cparams(("arbitrary", "arbitrary")),
    )(proj_a, proj_a, proj_a, fcol, frow, lse, do, dproj)


def _rope_tables():
    half = ROPE_DIM // 2
    inv_freq = jnp.power(jnp.float32(ROPE_THETA), -jnp.arange(half, dtype=F32) * 2.0 / ROPE_DIM)
    ang = jnp.arange(S).astype(F32)[:, None] * inv_freq[None, :]
    cos, sin = jnp.cos(ang), jnp.sin(ang)
    z = jnp.zeros((S, half), F32)
    rest = HD - ROPE_DIM
    t_self = jnp.concatenate([cos, cos, jnp.ones((S, rest), F32)], axis=1)
    t_up = jnp.concatenate([-sin, z, jnp.zeros((S, rest), F32)], axis=1)
    t_dn = jnp.concatenate([z, sin, jnp.zeros((S, rest), F32)], axis=1)
    rep = LANES // HD
    return tuple(jnp.tile(t, (1, rep)) for t in (t_self, t_up, t_dn))


def _residue_pieces(r):
    if r == 1:
        return [(slice(i, i + 512), slice(i, i + 512)) for i in range(0, S, 512)]
    n = S // r
    return [(pl.ds(j, n, stride=r), slice(j * n, (j + 1) * n)) for j in range(r)]


def _rope_apply(x, a, u, d, backward):
    half = ROPE_DIM // 2
    if backward:
        return x * a + pltpu.roll(x * u, half, 1) + pltpu.roll(x * d, LANES - half, 1)
    return x * a + pltpu.roll(x, LANES - half, 1) * u + pltpu.roll(x, half, 1) * d


LB_PER_CB = CB // LANES
ROPE_NB = 3 * DIL_G * LB_PER_CB


def _rope_split(proj_b, tables):
    def body(x_ref, a_ref, u_ref, d_ref, o_ref):
        j = pl.program_id(0)
        g = (j // LB_PER_CB) % DIL_G
        rotate = j < 2 * DIL_G * LB_PER_CB
        for gs in range(DIL_G):
            @pl.when(g == gs)
            def _(gs=gs):
                for tok, sub in _residue_pieces(DIL_R[gs]):
                    x = x_ref[tok, :]
                    y = _rope_apply(x, a_ref[tok, :], u_ref[tok, :], d_ref[tok, :], False)
                    o_ref[sub, :] = jnp.where(rotate, y, x).astype(BF16)

    tab = pl.BlockSpec((S, LANES), lambda j: (0, 0))
    return pl.pallas_call(
        body, name="rope_split", grid=(ROPE_NB,),
        in_specs=[pl.BlockSpec((S, LANES), lambda j: (0, j)), tab, tab, tab],
        out_specs=pl.BlockSpec((None, S, LANES), lambda j: (j // LB_PER_CB, 0, j % LB_PER_CB)),
        out_shape=jax.ShapeDtypeStruct((3 * DIL_G, S, CB), BF16),
        compiler_params=_cparams(("parallel",)),
    )(proj_b, *tables)


def _rope_merge_bwd(dq3, dk3, dv3, tables, dproj):
    per_tensor = DIL_G * LB_PER_CB

    def body(dq_ref, dk_ref, dv_ref, a_ref, u_ref, d_ref, _, o_ref, tmp):
        j = pl.program_id(0)
        t = j // per_tensor
        g = (j // LB_PER_CB) % DIL_G
        for gs in range(DIL_G):
            @pl.when((g == gs) & (j < ROPE_NB))
            def _(gs=gs):
                for tok, sub in _residue_pieces(DIL_R[gs]):
                    x = jnp.where(t == 0, dq_ref[sub, :], jnp.where(t == 1, dk_ref[sub, :], dv_ref[sub, :]))
                    y = _rope_apply(x, a_ref[tok, :], u_ref[tok, :], d_ref[tok, :], True)
                    tmp[tok, :] = jnp.where(t < 2, y, x)
                o_ref[...] = tmp[...].astype(BF16)

        @pl.when(j >= ROPE_NB)
        def _():
            o_ref[...] = jnp.zeros_like(o_ref)

    def src_spec(t):
        def index(j):
            jb = j // LB_PER_CB - t * DIL_G
            lane_blk = jnp.where(jb < 0, 0, jnp.where(jb >= DIL_G, LB_PER_CB - 1, j % LB_PER_CB))
            return jnp.clip(jb, 0, DIL_G - 1), 0, lane_blk
        return pl.BlockSpec((None, S, LANES), index)

    tab = pl.BlockSpec((S, LANES), lambda j: (0, 0))
    return pl.pallas_call(
        body, name="rope_merge_bwd", grid=(ROPE_NB + LB_PER_CB,),
        in_specs=[src_spec(0), src_spec(1), src_spec(2), tab, tab, tab, ANY],
        out_specs=pl.BlockSpec((S, LANES), lambda j: (0, j + SEG_B // LANES)),
        out_shape=jax.ShapeDtypeStruct((S, D_PAD), BF16),
        scratch_shapes=[pltpu.VMEM((S, LANES), F32)],
        input_output_aliases={6: 0},
        compiler_params=_cparams(("arbitrary",)),
    )(dq3, dk3, dv3, *tables, dproj)


DIL_NB = S // DIL_BLK


def _dil_mask(g, n):
    blocks_per_seq = jnp.right_shift(DIL_NB, 2 * g)
    has_prev = jnp.bitwise_and(n, blocks_per_seq - 1) != 0
    a = lax.broadcasted_iota(jnp.int32, (DIL_BLK, 2 * DIL_BLK), 0)
    kk = lax.broadcasted_iota(jnp.int32, (DIL_BLK, 2 * DIL_BLK), 1)
    diff = DIL_BLK + a - kk
    return (diff >= 0) & (diff <= DIL_BLK) & ((kk >= DIL_BLK) | has_prev)


def _dil_specs():
    def cur(t):
        return pl.BlockSpec((None, DIL_BLK, DIL_OUT), lambda g, n: (t * DIL_G + g, n, 0))

    def prev(t):
        return pl.BlockSpec((None, DIL_BLK, DIL_OUT), lambda g, n: (t * DIL_G + g, jnp.maximum(n - 1, 0), 0))

    lane_blk = pl.BlockSpec((None, DIL_BLK, LANES), lambda g, n: (g, n, 0))
    return cur, prev, lane_blk


def _dil_fwd(qkv3):
    def body(q_ref, kc_ref, kp_ref, vc_ref, vp_ref, o_ref, lse_ref):
        ok = _dil_mask(pl.program_id(0), pl.program_id(1))
        k2 = jnp.concatenate([kp_ref[...], kc_ref[...]], axis=0)
        v2 = jnp.concatenate([vp_ref[...], vc_ref[...]], axis=0)
        lane = lax.broadcasted_iota(jnp.int32, (DIL_BLK, LANES), 1)
        lse = jnp.zeros((DIL_BLK, LANES), F32)
        for h in range(DIL_HG):
            sl = slice(h * HD, (h + 1) * HD)
            s = jnp.where(ok, _dot(q_ref[:, sl], k2[:, sl], _NT) * SCALE, NEG_INF)
            m = jnp.max(s, axis=-1, keepdims=True)
            p = jnp.exp(s - m)
            l = jnp.sum(p, axis=-1, keepdims=True)
            o_ref[:, sl] = _dot(p / l, v2[:, sl], _NN)
            lse = jnp.where(lane == h, m + jnp.log(l), lse)
        lse_ref[...] = lse

    cur, prev, lane_blk = _dil_specs()
    out_blk = pl.BlockSpec((None, DIL_BLK, DIL_OUT), lambda g, n: (g, n, 0))
    return pl.pallas_call(
        body, name="dil_fwd", grid=(DIL_G, DIL_NB),
        in_specs=[cur(0), cur(1), prev(1), cur(2), prev(2)], out_specs=[out_blk, lane_blk],
        out_shape=[jax.ShapeDtypeStruct((DIL_G, S, DIL_OUT), F32), jax.ShapeDtypeStruct((DIL_G, S, LANES), F32)],
        compiler_params=_cparams(("parallel", "parallel")),
    )(qkv3, qkv3, qkv3, qkv3, qkv3)


def _dil_bwd(qkv3, lse3, do3, c3):
    def body(q_ref, kc_ref, kp_ref, vc_ref, vp_ref, lse_ref, do_ref, c_ref, dq_ref, dk_ref, dv_ref):
        n = pl.program_id(1)

        @pl.when(n == 0)
        def _():
            dk_ref[...] = jnp.zeros_like(dk_ref)
            dv_ref[...] = jnp.zeros_like(dv_ref)

        ok = _dil_mask(pl.program_id(0), n)
        k2 = jnp.concatenate([kp_ref[...], kc_ref[...]], axis=0)
        v2 = jnp.concatenate([vp_ref[...], vc_ref[...]], axis=0)
        lse = lse_ref[...]
        c = c_ref[...]
        cur_rows = pl.ds(pl.multiple_of(n * DIL_BLK, DIL_BLK), DIL_BLK)
        prev_rows = pl.ds(pl.multiple_of(jnp.maximum(n - 1, 0) * DIL_BLK, DIL_BLK), DIL_BLK)
        for h in range(DIL_HG):
            sl = slice(h * HD, (h + 1) * HD)
            q = q_ref[:, sl]
            do_h = do_ref[:, sl]
            s = jnp.where(ok, _dot(q, k2[:, sl], _NT) * SCALE, NEG_INF)
            p = jnp.exp(s - lse[:, h:h + 1])
            dp = _dot(do_h, v2[:, sl], _NT)
            ds = p * (dp - c[:, h:h + 1])
            dsb = (ds * SCALE).astype(BF16)
            dq_ref[:, sl] = _dot(dsb, k2[:, sl], _NN)
            dk2 = _dot(dsb, q, _TN)
            dv2 = _dot(p, do_h, _TN)
            dk_ref[prev_rows, sl] += dk2[:DIL_BLK]
            dk_ref[cur_rows, sl] += dk2[DIL_BLK:]
            dv_ref[prev_rows, sl] += dv2[:DIL_BLK]
            dv_ref[cur_rows, sl] += dv2[DIL_BLK:]

    cur, prev, lane_blk = _dil_specs()
    blk = pl.BlockSpec((None, DIL_BLK, DIL_OUT), lambda g, n: (g, n, 0))
    whole = pl.BlockSpec((None, S, DIL_OUT), lambda g, n: (g, 0, 0))
    return pl.pallas_call(
        body, name="dil_bwd", grid=(DIL_G, DIL_NB),
        in_specs=[cur(0), cur(1), prev(1), cur(2), prev(2), lane_blk, blk, lane_blk],
        out_specs=[blk, whole, whole],
        out_shape=[jax.ShapeDtypeStruct((DIL_G, S, DIL_OUT), F32)] * 3,
        compiler_params=_cparams(("parallel", "arbitrary")),
    )(qkv3, qkv3, qkv3, qkv3, qkv3, lse3, do3, c3)


COMB_T = 256


def _dil_combine(o3, lse3):
    heads_per_lb = LANES // HD

    def body(o_ref, lse_ref, ob_ref, al_ref, *scratch):
        o_tok = [scratch[LB_PER_CB * gg:LB_PER_CB * (gg + 1)] for gg in range(DIL_G)]
        lse_tok = scratch[LB_PER_CB * DIL_G:]
        g = pl.program_id(0)
        for gs in range(DIL_G):
            @pl.when(g == gs)
            def _(gs=gs):
                for tok, sub in _residue_pieces(DIL_R[gs]):
                    for hf in range(LB_PER_CB):
                        o_tok[gs][hf][tok, :] = o_ref[sub, hf * LANES:(hf + 1) * LANES]
                    lse_tok[gs][tok, :] = lse_ref[sub, :]

        @pl.when(g == DIL_G - 1)
        def _():
            def chunk(i, carry):
                rows = pl.ds(pl.multiple_of(i * COMB_T, COMB_T), COMB_T)
                lse = [lse_tok[gg][rows, :] for gg in range(DIL_G)]
                m = jnp.maximum(jnp.maximum(lse[0], lse[1]), lse[2])
                e = [jnp.exp(lse[gg] - m) for gg in range(DIL_G)]
                den = (e[0] + e[1]) + e[2]
                al = [e[gg] / den for gg in range(DIL_G)]
                for gg in range(DIL_G):
                    al_ref[gg, rows, :] = al[gg]
                for h in range(DIL_HG):
                    hf, sl = h // heads_per_lb, slice((h % heads_per_lb) * HD, (h % heads_per_lb + 1) * HD)
                    acc = al[0][:, h:h + 1] * o_tok[0][hf][rows, sl]
                    for gg in range(1, DIL_G):
                        acc = acc + al[gg][:, h:h + 1] * o_tok[gg][hf][rows, sl]
                    ob_ref[rows, h * HD:(h + 1) * HD] = acc
                return carry

            lax.fori_loop(0, S // COMB_T, chunk, 0)

    return pl.pallas_call(
        body, name="dil_combine", grid=(DIL_G,),
        in_specs=[pl.BlockSpec((None, S, DIL_OUT), lambda g: (g, 0, 0)),
                  pl.BlockSpec((None, S, LANES), lambda g: (g, 0, 0))],
        out_specs=[pl.BlockSpec((S, DIL_OUT), lambda g: (0, 0)),
                   pl.BlockSpec((DIL_G, S, LANES), lambda g: (0, 0, 0))],
        out_shape=[jax.ShapeDtypeStruct((S, DIL_OUT), F32), jax.ShapeDtypeStruct((DIL_G, S, LANES), F32)],
        scratch_shapes=[pltpu.VMEM((S, LANES), F32)] * (DIL_G * (LB_PER_CB + 1)),
        compiler_params=_cparams(("arbitrary",)),
    )(o3, lse3)


def _dil_combine_bwd(dob, ob, alpha):
    heads_per_lb = LANES // HD

    def body(dob_ref, ob_ref, al_ref, do_ref, c_ref):
        g = pl.program_id(0)
        hf = pl.program_id(1)
        for gs in range(DIL_G):
            @pl.when(g == gs)
            def _(gs=gs):
                for tok, sub in _residue_pieces(DIL_R[gs]):
                    dob = dob_ref[tok, :]
                    prod = dob * ob_ref[tok, :]
                    al = al_ref[tok, :]
                    lane = lax.broadcasted_iota(jnp.int32, al.shape, 1)
                    c = jnp.where(hf == 0, 0.0, c_ref[sub, :])
                    for hh in range(heads_per_lb):
                        sl = slice(hh * HD, (hh + 1) * HD)
                        head = hf * heads_per_lb + hh
                        a = jnp.sum(jnp.where(lane == head, al, 0.0), axis=-1, keepdims=True)
                        do_ref[sub, sl] = (a * dob[:, sl]).astype(BF16)
                        c = jnp.where(lane == head, a * jnp.sum(prod[:, sl], axis=-1, keepdims=True), c)
                    c_ref[sub, :] = c

    half = pl.BlockSpec((S, LANES), lambda g, hf: (0, hf))
    return pl.pallas_call(
        body, name="dil_combine_bwd", grid=(DIL_G, LB_PER_CB),
        in_specs=[half, half, pl.BlockSpec((None, S, LANES), lambda g, hf: (g, 0, 0))],
        out_specs=[pl.BlockSpec((None, S, LANES), lambda g, hf: (g, 0, hf)),
                   pl.BlockSpec((None, S, LANES), lambda g, hf: (g, 0, 0))],
        out_shape=[jax.ShapeDtypeStruct((DIL_G, S, DIL_OUT), BF16), jax.ShapeDtypeStruct((DIL_G, S, LANES), F32)],
        compiler_params=_cparams(("parallel", "arbitrary")),
    )(dob, ob, alpha)


def _local_step(x, tgt, g_attn, g_mlp, g_final, b_pad, w_in_t, hooks):
    tables = _rope_tables()

    h1 = _rms_fwd(x, g_attn, "rms_attn_fwd", deps=hooks.first_deps())
    proj_a = _mm(h1, w_in_t, mode="nt", tm=S, tn=512, n=3 * FOX_W, out_dtypes=[BF16], name="proj_a")
    proj_b = _mm(h1, w_in_t, mode="nt", tm=S, tn=CB, n=3 * DIL_W, b_off=SEG_B // CB, out_dtypes=[F32], name="proj_b")
    proj_g = _mm(h1, w_in_t, mode="nt", tm=S, tn=CB, n=2 * D, b_off=SEG_G // CB, out_dtypes=[F32], name="proj_g")
    proj_f = _mm(h1, w_in_t, mode="nt", tm=S, tn=LANES, n=LANES, b_off=SEG_F // LANES, out_dtypes=[F32], name="proj_f")

    fcol, frow = _fox_scan(proj_f, b_pad)
    oa, lse_a = _fox_fwd(proj_a, fcol, frow)

    qkv3 = _rope_split(proj_b, tables)
    o3, lse3 = _dil_fwd(qkv3)
    ob, alpha = _dil_combine(o3, lse3)

    w_a, w_b, w_out = hooks.mixer_weights(ob)
    ya = _mm(oa, w_a, mode="nn", tm=S, tn=CB, out_dtypes=[F32], name="branch_a")
    yb = _mm(ob, w_b, mode="nn", tm=S, tn=CB, out_dtypes=[F32], name="branch_b")
    mixed = _gate_fwd(proj_g, ya, yb)
    x2 = _mm(mixed, w_out, mode="nn", tm=S, tn=CB, out_dtypes=[F32], name="out_proj",
             epilogue=lambda acc, res: (res + acc,), extras=(x,))

    h2 = _rms_fwd(x2, g_mlp, "rms_mlp_fwd")
    w_up_sh, w_down = hooks.mlp_weights(h2)

    def up_epilogue(acc):
        r = jnp.maximum(acc, 0.0)
        return acc, r * r

    u, act = _mm(h2, w_up_sh, mode="nn", tm=S, tn=DFF // N_DEV, b_sharded=True, out_dtypes=[F32, BF16],
                 name="mlp_up", epilogue=up_epilogue)
    x3 = _mm(act, w_down, mode="nn", tm=512, tn=512, out_dtypes=[F32], name="mlp_down",
             epilogue=lambda acc, res: (res + acc,), extras=(x2,))

    dx3, dg_final, loss = _final_norm_loss(x3, g_final, tgt)

    du = _mm(dx3, w_down, mode="nt", tm=S, tn=CB, out_dtypes=[BF16], name="mlp_down_bwd",
             epilogue=lambda acc, u_t: (acc * (2.0 * jnp.maximum(u_t, 0.0)),), extras=(u,))
    dw_down = _mm(act, dx3, mode="tn", tm=512, tn=512, out_dtypes=[F32], name="dw_down")
    dw_up_sh = _mm(h2, du, mode="tn", tm=D, tn=DFF // N_DEV, out_sharded=True, out_dtypes=[F32], name="dw_up")
    dh2 = _mm(du, w_up_sh, mode="nt", tm=512, tn=512, b_sharded=True, out_dtypes=[F32], name="mlp_up_bwd",
              deps=hooks.mlp_grads(dw_up_sh, dw_down))
    dx2, dg_mlp = _rms_bwd(x2, g_mlp, dh2, dx3, "rms_mlp_bwd")

    dmixed = _mm(dx2, w_out, mode="nt", tm=S, tn=CB, out_dtypes=[F32], name="out_proj_bwd")
    dw_out = _mm(mixed, dx2, mode="tn", tm=D, tn=CB, out_dtypes=[F32], name="dw_out")

    dyab, dproj = _gate_bwd(dmixed, proj_g, ya, yb)
    doa = _mm(dyab, w_a, mode="nt", tm=S, tn=CB, k=D, a_off=0, out_dtypes=[BF16], name="branch_a_bwd",
              deps=hooks.mid_backward(dyab))
    dw_a = _mm(oa, dyab, mode="tn", tm=FOX_W, tn=CB, n=D, b_off=0, out_dtypes=[F32], name="dw_branch_a")
    dob = _mm(dyab, w_b, mode="nt", tm=S, tn=CB, k=D, a_off=1, out_dtypes=[F32], name="branch_b_bwd")
    dw_b = _mm(ob, dyab, mode="tn", tm=DIL_OUT, tn=CB, n=D, b_off=D // CB, out_dtypes=[F32], name="dw_branch_b")

    dproj, dft, dfs = _fox_bwd(proj_a, fcol, frow, lse_a, doa, dproj)
    dproj, db = _fox_dscan(dft, dfs, proj_f, b_pad, dproj)

    do3, c3 = _dil_combine_bwd(dob, ob, alpha)
    dq3, dk3, dv3 = _dil_bwd(qkv3, lse3, do3, c3)
    dproj = _rope_merge_bwd(dq3, dk3, dv3, tables, dproj)

    dw_in_t = _mm(dproj, h1, mode="tn", tm=CB, tn=D, out_dtypes=[F32], name="dw_in")
    dh1 = _mm(dproj, w_in_t, mode="nn", tm=512, tn=CB, out_dtypes=[F32], name="proj_bwd")
    dx, dg_attn = _rms_bwd(x, g_attn, dh1, dx2, "rms_attn_bwd")

    return loss, dx, (dw_in_t, dw_a, dw_b, dw_out), (dg_attn, db, dg_mlp, dg_final)


SHARD_SHAPES = ((D_IN // N_DEV, D), (FOX_W, D // N_DEV), (DIL_OUT, D // N_DEV), (D // N_DEV, D),
                (D, DFF // N_DEV), (DFF // N_DEV, D))
W_NAMES = ("w_in", "w_a", "w_b", "w_out", "w_up", "w_down")
AG_COPIES = 7
HBM = pl.BlockSpec(memory_space=pltpu.HBM)
SEM = pl.BlockSpec(memory_space=pltpu.SEMAPHORE)


def _place():
    return lax.axis_index("x"), lax.axis_index("y"), lax.axis_index("c")


def _all_gather(shards):
    n_w = len(shards)
    shapes = [t.shape for t in shards]

    def body(*refs):
        in_refs, out_refs = refs[:n_w], refs[n_w:2 * n_w]
        stage, cast = refs[2 * n_w:3 * n_w], refs[3 * n_w:4 * n_w]
        send_sems, recv_sems, local_sems, load_sems = refs[4 * n_w:]
        x, y, c = _place()
        me, sibling = (x, y, c), (x, y, 1 - c)
        chips = [(1 - x, y), (x, 1 - y), (1 - x, 1 - y)]

        def slot(i, px, py, pc):
            return out_refs[i].at[4 * px + 2 * py + pc]

        def copy(i, k, block, to, src=None):
            return pltpu.make_async_remote_copy(
                src_ref=slot(i, *block) if src is None else src, dst_ref=slot(i, *block),
                send_sem=send_sems.at[i * AG_COPIES + k], recv_sem=recv_sems.at[i * AG_COPIES + k],
                device_id=to, device_id_type=MESH)

        loads = [pltpu.make_async_copy(in_refs[i], stage[i], load_sems.at[i]) for i in range(n_w)]
        for ld in loads:
            ld.start()
        local, remote = [], []
        for i in range(n_w):
            loads[i].wait()
            cast[i][...] = stage[i][...].astype(BF16)
            local.append(pltpu.make_async_copy(cast[i], slot(i, *me), local_sems.at[i]))
            local[-1].start()
            first = [copy(i, 0, me, sibling, src=cast[i])]
            first += [copy(i, 1 + j, me, (*chip, c), src=cast[i]) for j, chip in enumerate(chips)]
            for cp in first:
                cp.start()
            remote += first
        for i in range(n_w):
            for j, chip in enumerate(chips):
                copy(i, 1 + j, (*chip, c), me).wait_recv()
                remote.append(copy(i, 4 + j, (*chip, c), sibling))
                remote[-1].start()
        for i in range(n_w):
            copy(i, 0, sibling, me).wait_recv()
            for j, chip in enumerate(chips):
                copy(i, 4 + j, (*chip, 1 - c), me).wait_recv()
        for cp in remote:
            cp.wait_send()
        for cp in local:
            cp.wait()

    return pl.pallas_call(
        body, name="all_gather_w_in",
        out_shape=[jax.ShapeDtypeStruct((N_DEV,) + sh, BF16) for sh in shapes],
        in_specs=[ANY] * n_w, out_specs=[ANY] * n_w,
        scratch_shapes=[pltpu.VMEM(sh, F32) for sh in shapes] + [pltpu.VMEM(sh, BF16) for sh in shapes]
                       + [pltpu.SemaphoreType.DMA((n_w * AG_COPIES,)), pltpu.SemaphoreType.DMA((n_w * AG_COPIES,)),
                          pltpu.SemaphoreType.DMA((n_w,)), pltpu.SemaphoreType.DMA((n_w,))],
        compiler_params=_cparams(None),
    )(*shards)


def _place_own(shards):
    n_w = len(shards)
    shapes = [t.shape for t in shards]

    def body(*refs):
        in_refs, out_refs = refs[:n_w], refs[n_w:2 * n_w]
        stage, cast = refs[2 * n_w:3 * n_w], refs[3 * n_w:4 * n_w]
        load_sems, store_sems = refs[4 * n_w:]
        x, y, c = _place()
        loads = [pltpu.make_async_copy(in_refs[i], stage[i], load_sems.at[i]) for i in range(n_w)]
        for ld in loads:
            ld.start()
        stores = []
        for i in range(n_w):
            loads[i].wait()
            cast[i][...] = stage[i][...].astype(BF16)
            stores.append(pltpu.make_async_copy(cast[i], out_refs[i].at[4 * x + 2 * y + c], store_sems.at[i]))
            stores[-1].start()
        for st in stores:
            st.wait()

    return pl.pallas_call(
        body, name="place_own_shards",
        out_shape=[jax.ShapeDtypeStruct((N_DEV,) + sh, BF16) for sh in shapes],
        in_specs=[ANY] * n_w, out_specs=[ANY] * n_w,
        scratch_shapes=[pltpu.VMEM(sh, F32) for sh in shapes] + [pltpu.VMEM(sh, BF16) for sh in shapes]
                       + [pltpu.SemaphoreType.DMA((n_w,)), pltpu.SemaphoreType.DMA((n_w,))],
        compiler_params=_cparams(None),
    )(*shards)


PEER_COPIES = N_DEV - 1
SIBLING_COPIES = 4
CHIP_COPIES = 3


def _peer_copies(_, land_refs, send_sems, recv_sems):
    x, y, c = _place()
    mine = 4 * x + 2 * y + c
    copies = []
    for i, ref in enumerate(land_refs):
        for k in range(1, N_DEV):
            peer = (x ^ (k >> 2), y ^ ((k >> 1) & 1), c ^ (k & 1))
            n = i * PEER_COPIES + k - 1
            copies.append(pltpu.make_async_remote_copy(
                src_ref=ref.at[mine], dst_ref=ref.at[mine], send_sem=send_sems.at[n], recv_sem=recv_sems.at[n],
                device_id=peer, device_id_type=MESH))
    return copies


def _sibling_copies(g_refs, r_refs, send_sems, recv_sems):
    x, y, c = _place()
    return [pltpu.make_async_remote_copy(
        src_ref=g_refs[i].at[2 * k + (1 - c)], dst_ref=r_refs[i].at[k],
        send_sem=send_sems.at[SIBLING_COPIES * i + k], recv_sem=recv_sems.at[SIBLING_COPIES * i + k],
        device_id=(x, y, 1 - c), device_id_type=MESH) for i in range(len(g_refs)) for k in range(SIBLING_COPIES)]


def _chip_copies(p_refs, r_refs, send_sems, recv_sems):
    x, y, c = _place()
    chips = [(1 - x, y), (x, 1 - y), (1 - x, 1 - y)]
    return [pltpu.make_async_remote_copy(
        src_ref=p_refs[i].at[2 * cx + cy], dst_ref=r_refs[i].at[j],
        send_sem=send_sems.at[CHIP_COPIES * i + j], recv_sem=recv_sems.at[CHIP_COPIES * i + j],
        device_id=(cx, cy, c), device_id_type=MESH) for i in range(len(p_refs)) for j, (cx, cy) in enumerate(chips)]


def _exchange(name, copies_fn, n_copies, srcs, out_shapes):
    n = len(srcs)

    def body(*refs):
        copies = copies_fn(refs[:n], refs[n:n + len(out_shapes)], refs[-2], refs[-1])
        for cp in copies:
            cp.start()
        for cp in copies:
            cp.wait()

    return pl.pallas_call(
        body, name=name, out_shape=out_shapes,
        in_specs=[ANY] * n, out_specs=[ANY] * len(out_shapes),
        scratch_shapes=[pltpu.SemaphoreType.DMA((n_copies,)), pltpu.SemaphoreType.DMA((n_copies,))],
    )(*srcs)


def _in_hbm(a):
    return pltpu.with_memory_space_constraint(a, pltpu.HBM)


def _exchange_start(name, copies_fn, n_copies, srcs, lands, after=()):
    n_s, n_l, n_a = len(srcs), len(lands), len(after)

    def body(*refs):
        outs = refs[n_s + n_l + n_a:]
        for cp in copies_fn(refs[:n_s], refs[n_s:n_s + n_l], outs[0], outs[1]):
            cp.start()
        outs[-1][...] = jnp.zeros_like(outs[-1])

    res = pl.pallas_call(
        body, name=name,
        out_shape=[pltpu.SemaphoreType.DMA((n_copies,)), pltpu.SemaphoreType.DMA((n_copies,))]
                  + [pltpu.HBM(t.shape, t.dtype) for t in (*srcs, *lands)] + [jax.ShapeDtypeStruct((8, LANES), F32)],
        in_specs=[HBM] * (n_s + n_l) + [ANY] * n_a,
        out_specs=[SEM, SEM] + [HBM] * (n_s + n_l) + [pl.BlockSpec(memory_space=pltpu.VMEM)],
        input_output_aliases={i: 2 + i for i in range(n_s + n_l)},
        compiler_params=pltpu.CompilerParams(has_side_effects=pltpu.SideEffectType.DATAFLOW_SIDE_EFFECTING),
    )(*[_in_hbm(t) for t in (*srcs, *lands)], *after)
    return res[0], res[1], list(res[2:2 + n_s]), list(res[2 + n_s:2 + n_s + n_l]), res[-1]


def _exchange_wait(name, copies_fn, started, after):
    send_sems, recv_sems, srcs, lands, _ = started
    n_s, n_l = len(srcs), len(lands)

    def body(*refs):
        for cp in copies_fn(refs[:n_s], refs[n_s:n_s + n_l], refs[n_s + n_l], refs[n_s + n_l + 1]):
            cp.wait_send()
            cp.wait_recv()

    res = pl.pallas_call(
        body, name=name,
        out_shape=[pltpu.HBM(t.shape, t.dtype) for t in (*srcs, *lands)],
        in_specs=[HBM] * (n_s + n_l) + [SEM, SEM, ANY],
        out_specs=[HBM] * (n_s + n_l),
        input_output_aliases={i: i for i in range(n_s + n_l)},
        compiler_params=pltpu.CompilerParams(has_side_effects=pltpu.SideEffectType.DATAFLOW_SIDE_EFFECTING),
    )(*srcs, *lands, send_sems, recv_sems, after)
    return list(res[:n_s]), list(res[n_s:])


def _shard_block(shape):
    rows, cols = shape
    if rows % 256:
        return rows, 256
    return min(rows, 512 if cols <= LANES else 256), cols


def _chip_partial(ids, g_stack, recv1, name):
    shape = g_stack.shape[1:]
    br, bc = _shard_block(shape)

    def body(ids_ref, g_ref, r_ref, pb_ref, own_ref):
        s = g_ref[...] + r_ref[...]
        pb_ref[...] = s.astype(BF16)

        @pl.when(pl.program_id(2) == ids_ref[1])
        def _():
            own_ref[...] = s

    grid_spec = pltpu.PrefetchScalarGridSpec(
        num_scalar_prefetch=1, grid=(shape[0] // br, shape[1] // bc, 4),
        in_specs=[pl.BlockSpec((None, br, bc), lambda r, q, k, ids: (2 * k + ids[0], r, q)),
                  pl.BlockSpec((None, br, bc), lambda r, q, k, ids: (k, r, q))],
        out_specs=[pl.BlockSpec((None, br, bc), lambda r, q, k, ids: (k, r, q)),
                   pl.BlockSpec((br, bc), lambda r, q, k, ids: (r, q))])
    return pl.pallas_call(
        body, name=name, grid_spec=grid_spec,
        out_shape=[jax.ShapeDtypeStruct((4,) + shape, BF16), jax.ShapeDtypeStruct(shape, F32)],
        compiler_params=_cparams(("parallel", "parallel", "arbitrary")),
    )(ids, g_stack, recv1)


def _adamw(w, g, m, v):
    m = ADAM_B1 * m + (1.0 - ADAM_B1) * g
    v = ADAM_B2 * v + (1.0 - ADAM_B2) * (g * g)
    m_hat = m / (1.0 - ADAM_B1 ** ADAM_STEP)
    v_hat = v / (1.0 - ADAM_B2 ** ADAM_STEP)
    delta = -ADAM_LR * (m_hat / (jnp.sqrt(v_hat) + ADAM_EPS) + ADAM_WD * w)
    return delta, m, v


def _reduce_adamw(own, recv2, w, m, v, name):
    shape = own.shape
    br, bc = _shard_block(shape)

    def body(own_ref, r_ref, w_ref, m_ref, v_ref, g_ref, d_ref, nm_ref, nv_ref):
        g = own_ref[...]
        for j in range(3):
            g = g + r_ref[j].astype(F32)
        delta, nm, nv = _adamw(w_ref[...], g, m_ref[...], v_ref[...])
        g_ref[...] = g
        d_ref[...] = delta
        nm_ref[...] = nm
        nv_ref[...] = nv

    blk = pl.BlockSpec((br, bc), lambda r, q: (r, q))
    return pl.pallas_call(
        body, name=name, grid=(shape[0] // br, shape[1] // bc),
        in_specs=[blk, pl.BlockSpec((3, br, bc), lambda r, q: (0, r, q)), blk, blk, blk],
        out_specs=[blk] * 4, out_shape=[jax.ShapeDtypeStruct(shape, F32)] * 4,
        compiler_params=_cparams(("parallel", "parallel")),
    )(own, recv2, w, m, v)


def _small_allreduce_adamw(gvec, w, m, v):
    def body(g_ref, w_ref, m_ref, v_ref, go_ref, d_ref, nm_ref, nv_ref, buf, send_sems, recv_sems):
        x, y, c = _place()
        my_slot = 4 * x + 2 * y + c
        buf[my_slot] = g_ref[...]
        copies = []
        for k in range(1, N_DEV):
            px, py, pc = x ^ (k >> 2), y ^ ((k >> 1) & 1), c ^ (k & 1)
            copies.append(pltpu.make_async_remote_copy(
                src_ref=g_ref, dst_ref=buf.at[my_slot], send_sem=send_sems.at[k - 1], recv_sem=recv_sems.at[k - 1],
                device_id=(px, py, pc), device_id_type=MESH))
        for cp in copies:
            cp.start()
        for k in range(1, N_DEV):
            px, py, pc = x ^ (k >> 2), y ^ ((k >> 1) & 1), c ^ (k & 1)
            pltpu.make_async_remote_copy(
                src_ref=g_ref, dst_ref=buf.at[4 * px + 2 * py + pc], send_sem=send_sems.at[k - 1],
                recv_sem=recv_sems.at[k - 1], device_id=(px, py, pc), device_id_type=MESH).wait_recv()
        for cp in copies:
            cp.wait_send()
        g = buf[0]
        for s in range(1, N_DEV):
            g = g + buf[s]
        delta, nm, nv = _adamw(w_ref[...], g, m_ref[...], v_ref[...])
        go_ref[...] = g
        d_ref[...] = delta
        nm_ref[...] = nm
        nv_ref[...] = nv

    vm = pl.BlockSpec(memory_space=pltpu.VMEM)
    return pl.pallas_call(
        body, name="small_allreduce_adamw",
        in_specs=[vm] * 4, out_specs=[vm] * 4,
        out_shape=[jax.ShapeDtypeStruct((SMALL_R, LANES), F32)] * 4,
        scratch_shapes=[pltpu.VMEM((N_DEV, SMALL_R, LANES), F32),
                        pltpu.SemaphoreType.DMA((N_DEV - 1,)), pltpu.SemaphoreType.DMA((N_DEV - 1,))],
    )(gvec, w, m, v)


def _cols_to_whole(stack):
    rows = stack.shape[1]
    return stack.transpose(1, 0, 2).reshape(rows, -1)


def _whole_to_cols(t):
    rows = t.shape[0]
    return t.reshape(rows, N_DEV, -1).transpose(1, 0, 2)


def _pad_w_in_t(wt):
    n_a = 3 * FOX_W
    n_b = n_a + FOX_H
    return jnp.concatenate([wt[:n_a], wt[n_b:n_b + 3 * DIL_W], jnp.zeros((CB, D), wt.dtype),
                            wt[n_b + 3 * DIL_W:], wt[n_a:n_b], jnp.zeros((CB - FOX_H, D), wt.dtype)], axis=0)


def _unpad_dw_in_t(dwp):
    return jnp.concatenate([dwp[:SEG_B], dwp[SEG_F:SEG_F + FOX_H], dwp[SEG_B:SEG_Z], dwp[SEG_G:SEG_F]], axis=0)


def _pack_small(g_attn, b, g_mlp, g_final):
    b_rows = jnp.pad(b, ((0, 7), (0, LANES - b.shape[1])))
    return jnp.concatenate([g_attn.reshape(8, LANES), g_mlp.reshape(8, LANES), g_final.reshape(8, LANES), b_rows], axis=0)


def _unpack_small(p):
    return (p[0:8].reshape(1, D), p[24:25, :FOX_H], p[8:16].reshape(1, D), p[16:24].reshape(D))


class _StepComm:
    def __init__(self, ids, late_shards, after_w_in):
        self.ids = ids
        lands = _place_own(late_shards)
        self.gather_mixer = _exchange_start("gather_mixer_start", _peer_copies, PEER_COPIES * 3, [], lands[:3],
                                            after=(after_w_in,))
        self.gather_mlp = _exchange_start("gather_mlp_start", _peer_copies, PEER_COPIES * 2, [], lands[3:],
                                          after=(self.gather_mixer[-1],))
        self.sibling = None
        self.chips = None
        self.own = None

    def first_deps(self):
        return [self.gather_mlp[-1]]

    def mixer_weights(self, after):
        _, (g_a, g_b, g_out) = _exchange_wait("gather_mixer_wait", _peer_copies, self.gather_mixer, after)
        return _cols_to_whole(g_a), _cols_to_whole(g_b), g_out.reshape(D, D)

    def mlp_weights(self, after):
        _, (g_up, g_down) = _exchange_wait("gather_mlp_wait", _peer_copies, self.gather_mlp, after)
        return g_up, g_down.reshape(DFF, D)

    def mlp_grads(self, dw_up_sh, dw_down):
        srcs = [dw_up_sh, dw_down.reshape((N_DEV,) + SHARD_SHAPES[5])]
        lands = [lax.empty((SIBLING_COPIES,) + t.shape[1:], F32) for t in srcs]
        self.sibling = _exchange_start("grad_mlp_sibling_start", _sibling_copies, SIBLING_COPIES * 2, srcs, lands)
        return [self.sibling[-1]]

    def mid_backward(self, after):
        grads, recv1 = _exchange_wait("grad_mlp_sibling_wait", _sibling_copies, self.sibling, after)
        parts = [_chip_partial(self.ids, grads[i], recv1[i], "grad_partial_" + W_NAMES[4 + i]) for i in range(2)]
        self.own = [p[1] for p in parts]
        srcs = [p[0] for p in parts]
        lands = [lax.empty((CHIP_COPIES,) + t.shape[1:], BF16) for t in srcs]
        self.chips = _exchange_start("grad_mlp_chips_start", _chip_copies, CHIP_COPIES * 2, srcs, lands)
        return [self.chips[-1]]

    def mlp_reduced(self, after):
        _, recv2 = _exchange_wait("grad_mlp_chips_wait", _chip_copies, self.chips, after)
        return list(zip(self.own, recv2))


def kernel(x, norm_attn_g, w_in, b_forget, w_branch_a, w_branch_b, w_out, norm_mlp_g, w_up, w_down, norm_final_g, loss_target, m_norm_attn_g, m_w_in, m_b_forget, m_w_branch_a, m_w_branch_b, m_w_out, m_norm_mlp_g, m_w_up, m_w_down, m_norm_final_g, v_norm_attn_g, v_w_in, v_b_forget, v_w_branch_a, v_w_branch_b, v_w_out, v_norm_mlp_g, v_w_up, v_w_down, v_norm_final_g):
    cx, cy, cc = _place()
    ids = jnp.stack([cc, 2 * cx + cy]).astype(jnp.int32)

    w_sh = [w_in[0].T] + [t[0] for t in (w_branch_a, w_branch_b, w_out, w_up, w_down)]
    m_sh = [m_w_in[0].T] + [t[0] for t in (m_w_branch_a, m_w_branch_b, m_w_out, m_w_up, m_w_down)]
    v_sh = [v_w_in[0].T] + [t[0] for t in (v_w_branch_a, v_w_branch_b, v_w_out, v_w_up, v_w_down)]

    (g_in,) = _all_gather(w_sh[:1])
    w_in_t = _pad_w_in_t(g_in.reshape(D_IN, D))
    comm = _StepComm(ids, w_sh[1:], g_in)
    b_pad = jnp.pad(b_forget, ((0, 0), (0, LANES - FOX_H)))

    loss_row, dx, dws, dsmall = _local_step(
        x[0], loss_target[0], norm_attn_g, norm_mlp_g, norm_final_g.reshape(1, D), b_pad, w_in_t, comm)
    loss = lax.psum(loss_row[0, 0], ("x", "y", "c"))

    dw_in_t, dw_a, dw_b, dw_out = dws
    grads = [_unpad_dw_in_t(dw_in_t).reshape((N_DEV,) + SHARD_SHAPES[0]), _whole_to_cols(dw_a), _whole_to_cols(dw_b),
             dw_out.reshape((N_DEV,) + SHARD_SHAPES[3])]
    n_g = len(grads)
    recv1 = _exchange("grad_sibling", _sibling_copies, SIBLING_COPIES * n_g, grads,
                      [jax.ShapeDtypeStruct((SIBLING_COPIES,) + t.shape[1:], F32) for t in grads])
    parts = [_chip_partial(ids, grads[i], recv1[i], "grad_partial_" + W_NAMES[i]) for i in range(n_g)]
    recv2 = _exchange("grad_chips", _chip_copies, CHIP_COPIES * n_g, [p[0] for p in parts],
                      [jax.ShapeDtypeStruct((CHIP_COPIES,) + t.shape[1:], BF16) for t in grads])
    reduced = [(parts[i][1], recv2[i]) for i in range(n_g)] + comm.mlp_reduced(recv2[0])
    big = [_reduce_adamw(*reduced[i], w_sh[i], m_sh[i], v_sh[i], "adamw_" + W_NAMES[i]) for i in range(len(w_sh))]

    dg_attn, db, dg_mlp, dg_final = dsmall
    small = _small_allreduce_adamw(
        _pack_small(dg_attn, db, dg_mlp, dg_final),
        _pack_small(norm_attn_g, b_forget, norm_mlp_g, norm_final_g.reshape(1, D)),
        _pack_small(m_norm_attn_g, m_b_forget, m_norm_mlp_g, m_norm_final_g.reshape(1, D)),
        _pack_small(v_norm_attn_g, v_b_forget, v_norm_mlp_g, v_norm_final_g.reshape(1, D)))

    outs = [loss, dx[None]]
    for q in range(4):
        s_attn, s_b, s_mlp, s_final = _unpack_small(small[q])
        b_in, b_a, b_b, b_out, b_up, b_down = [t[q][None] for t in big]
        b_in = jnp.swapaxes(b_in, 1, 2)
        outs += [s_attn, b_in, s_b, b_a, b_b, b_out, s_mlp, b_up, b_down, s_final]
    return tuple(outs)
```

```python
import functools

import jax
import jax.numpy as jnp
from jax import lax
from jax.experimental import pallas as pl
from jax.experimental.pallas import tpu as pltpu

F32 = jnp.float32
BF16 = jnp.bfloat16
MESH = pl.DeviceIdType.MESH

S = 2048
D = 1024
HD = 64
FOX_H = 8
FOX_W = FOX_H * HD
DIL_HG = 4
DIL_G = 3
DIL_W = DIL_G * DIL_HG * HD
DIL_OUT = DIL_HG * HD
DIL_BLK = 128
DIL_R = (1, 4, 16)
DFF = 4 * D
D_IN = 3 * FOX_W + FOX_H + 3 * DIL_W + 2 * D
EPS = 1e-6
NEG_INF = -1e30
SCALE = HD ** -0.5
ROPE_THETA = 500000.0
ROPE_DIM = HD // 4
N_DEV = 8

ADAM_LR = 0.001
ADAM_B1 = 0.9
ADAM_B2 = 0.999
ADAM_EPS = 1e-08
ADAM_WD = 0.01
ADAM_STEP = 10

LANES = 128
CB = 256
SEG_A = 0
SEG_B = 3 * FOX_W
SEG_Z = SEG_B + 3 * DIL_W
SEG_G = SEG_Z + CB
SEG_F = SEG_G + 2 * D
D_PAD = SEG_F + CB
SMALL_R = 32

VMEM_MB = 56


def _cparams(dims=None, vmem_mb=VMEM_MB, **kw):
    return pltpu.CompilerParams(dimension_semantics=dims, vmem_limit_bytes=vmem_mb << 20, **kw)


ANY = pl.BlockSpec(memory_space=pl.ANY)

_NN = (((1,), (0,)), ((), ()))
_NT = (((1,), (1,)), ((), ()))
_TN = (((0,), (0,)), ((), ()))


def _dot(a, b, dims):
    return lax.dot_general(a.astype(BF16), b.astype(BF16), dims, preferred_element_type=F32)


def _mm(a, b, *, mode, tm, tn, out_dtypes, name, n=None, k=None, a_off=0, b_off=0,
        b_sharded=False, out_sharded=False, epilogue=None, extras=(), deps=()):
    n_sh = b.shape[-1] if b_sharded else None
    if mode == "nn":
        m = a.shape[0]
        k = k or a.shape[1]
        a_spec = pl.BlockSpec((tm, k), lambda i, j: (i, a_off))
        if b_sharded:
            assert tn == n_sh
            n = N_DEV * n_sh
            b_spec = pl.BlockSpec((None, k, tn), lambda i, j: (j, 0, 0))
        else:
            n = n or b.shape[1]
            b_spec = pl.BlockSpec((k, tn), lambda i, j: (0, j + b_off))
        dims = _NN
    elif mode == "nt":
        m = a.shape[0]
        k = k or a.shape[1]
        a_spec = pl.BlockSpec((tm, k), lambda i, j: (i, a_off))
        if b_sharded:
            n = b.shape[1]
            b_spec = pl.BlockSpec((N_DEV, tn, n_sh), lambda i, j: (0, j, 0))
        else:
            n = n or b.shape[0]
            b_spec = pl.BlockSpec((tn, k), lambda i, j: (j + b_off, 0))
        dims = _NT
    else:
        k, m = a.shape
        n = n or b.shape[1]
        a_spec = pl.BlockSpec((k, tm), lambda i, j: (0, i))
        b_spec = pl.BlockSpec((k, tn), lambda i, j: (0, j + b_off))
        dims = _TN
    assert m % tm == 0 and n % tn == 0, (name, m, n, tm, tn)
    n_extra = len(extras)
    tile = pl.BlockSpec((tm, tn), lambda i, j: (i, j))
    if out_sharded:
        assert mode == "tn" and n // tn == N_DEV
        out_spec = pl.BlockSpec((None, tm, tn), lambda i, j: (j, i, 0))
        out_shape = (N_DEV, m, tn)
    else:
        out_spec, out_shape = tile, (m, n)

    def body(a_ref, b_ref, *refs):
        if mode == "nt" and b_sharded:
            acc = _dot(a_ref[:, 0:n_sh], b_ref[0], dims)
            for p in range(1, N_DEV):
                acc = acc + _dot(a_ref[:, p * n_sh:(p + 1) * n_sh], b_ref[p], dims)
        else:
            acc = _dot(a_ref[...], b_ref[...], dims)
        ex = [r[...] for r in refs[:n_extra]]
        outs = epilogue(acc, *ex) if epilogue is not None else (acc,)
        for o_ref, o in zip(refs[n_extra + len(deps):], outs):
            o_ref[...] = o.astype(o_ref.dtype)

    res = pl.pallas_call(
        body, name=name, grid=(m // tm, n // tn),
        in_specs=[a_spec, b_spec] + [tile] * n_extra + [ANY] * len(deps),
        out_specs=[out_spec] * len(out_dtypes),
        out_shape=[jax.ShapeDtypeStruct(out_shape, dt) for dt in out_dtypes],
        compiler_params=_cparams(("parallel", "parallel")),
    )(a, b, *extras, *deps)
    return res if len(out_dtypes) > 1 else res[0]


ROW_T = 256


def _rms_fwd(x, g, name, deps=()):
    def body(x_ref, g_ref, *refs):
        o_ref = refs[-1]
        x = x_ref[...]
        r = lax.rsqrt(jnp.mean(x * x, axis=-1, keepdims=True) + EPS)
        o_ref[...] = ((x * r) * g_ref[...]).astype(BF16)

    return pl.pallas_call(
        body, name=name, grid=(S // ROW_T,),
        in_specs=[pl.BlockSpec((ROW_T, D), lambda i: (i, 0)), pl.BlockSpec((1, D), lambda i: (0, 0))] + [ANY] * len(deps),
        out_specs=pl.BlockSpec((ROW_T, D), lambda i: (i, 0)),
        out_shape=jax.ShapeDtypeStruct((S, D), BF16),
        compiler_params=_cparams(("parallel",)),
    )(x, g, *deps)


def _rms_bwd(x, g, dh, dres, name):
    def body(x_ref, g_ref, dh_ref, dres_ref, dx_ref, dg_ref):
        @pl.when(pl.program_id(0) == 0)
        def _():
            dg_ref[...] = jnp.zeros_like(dg_ref)

        x = x_ref[...]
        dh = dh_ref[...]
        r = lax.rsqrt(jnp.mean(x * x, axis=-1, keepdims=True) + EPS)
        xn = x * r
        dhn = dh * g_ref[...]
        dx_ref[...] = dres_ref[...] + r * (dhn - xn * jnp.mean(dhn * xn, axis=-1, keepdims=True))
        dg_ref[...] += jnp.sum(dh * xn, axis=0, keepdims=True)

    row = pl.BlockSpec((ROW_T, D), lambda i: (i, 0))
    vec = pl.BlockSpec((1, D), lambda i: (0, 0))
    return pl.pallas_call(
        body, name=name, grid=(S // ROW_T,),
        in_specs=[row, vec, row, row], out_specs=[row, vec],
        out_shape=[jax.ShapeDtypeStruct((S, D), F32), jax.ShapeDtypeStruct((1, D), F32)],
        compiler_params=_cparams(("arbitrary",)),
    )(x, g, dh, dres)


def _final_norm_loss(x, g, tgt):
    def body(x_ref, g_ref, t_ref, dx_ref, dg_ref, loss_ref):
        @pl.when(pl.program_id(0) == 0)
        def _():
            dg_ref[...] = jnp.zeros_like(dg_ref)
            loss_ref[...] = jnp.zeros_like(loss_ref)

        x = x_ref[...]
        g = g_ref[...]
        r = lax.rsqrt(jnp.mean(x * x, axis=-1, keepdims=True) + EPS)
        xn = x * r
        err = xn * g - t_ref[...]
        row_loss = jnp.mean(err * err, axis=-1, keepdims=True)
        loss_ref[...] += 0.5 * jnp.sum(row_loss, axis=0, keepdims=True) * jnp.ones((1, LANES), F32)
        dy = err * (1.0 / D)
        dyn = dy * g
        dx_ref[...] = r * (dyn - xn * jnp.mean(dyn * xn, axis=-1, keepdims=True))
        dg_ref[...] += jnp.sum(dy * xn, axis=0, keepdims=True)

    row = pl.BlockSpec((ROW_T, D), lambda i: (i, 0))
    vec = pl.BlockSpec((1, D), lambda i: (0, 0))
    return pl.pallas_call(
        body, name="final_norm_loss", grid=(S // ROW_T,),
        in_specs=[row, vec, row],
        out_specs=[row, vec, pl.BlockSpec((1, LANES), lambda i: (0, 0))],
        out_shape=[jax.ShapeDtypeStruct((S, D), F32), jax.ShapeDtypeStruct((1, D), F32),
                   jax.ShapeDtypeStruct((1, LANES), F32)],
        compiler_params=_cparams(("arbitrary",)),
    )(x, g, tgt)


def _sigmoid(z):
    return 1.0 / (1.0 + jnp.exp(-z))


def _gate_fwd(proj_g, ya, yb):
    def body(ga_ref, gb_ref, ya_ref, yb_ref, o_ref):
        o_ref[...] = (_sigmoid(ga_ref[...]) * ya_ref[...] + _sigmoid(gb_ref[...]) * yb_ref[...]).astype(BF16)

    row = pl.BlockSpec((ROW_T, D), lambda i: (i, 0))
    return pl.pallas_call(
        body, name="gate_fwd", grid=(S // ROW_T,),
        in_specs=[row, pl.BlockSpec((ROW_T, D), lambda i: (i, 1)), row, row],
        out_specs=row, out_shape=jax.ShapeDtypeStruct((S, D), BF16),
        compiler_params=_cparams(("parallel",)),
    )(proj_g, proj_g, ya, yb)


GATE_TR = 1024
GATE_TC = 512
GATE_NB = 2 * D // GATE_TC


def _gate_bwd(dmixed, proj_g, ya, yb):
    half = GATE_NB // 2

    def body(dm_ref, g_ref, ya_ref, yb_ref, dy_ref, dg_ref):
        dm = dm_ref[...]
        s = _sigmoid(g_ref[...])
        y = jnp.where(pl.program_id(1) < half, ya_ref[...], yb_ref[...])
        dy_ref[...] = (dm * s).astype(BF16)
        dg_ref[...] = (dm * y * (s * (1.0 - s))).astype(BF16)

    blk = pl.BlockSpec((GATE_TR, GATE_TC), lambda i, j: (i, j))
    wrapped = pl.BlockSpec((GATE_TR, GATE_TC), lambda i, j: (i, j % half))
    return pl.pallas_call(
        body, name="gate_bwd", grid=(S // GATE_TR, GATE_NB),
        in_specs=[wrapped, blk, wrapped, wrapped],
        out_specs=[blk, pl.BlockSpec((GATE_TR, GATE_TC), lambda i, j: (i, j + SEG_G // GATE_TC))],
        out_shape=[jax.ShapeDtypeStruct((S, 2 * D), BF16), jax.ShapeDtypeStruct((S, D_PAD), BF16)],
        compiler_params=_cparams(("parallel", "parallel")),
    )(dmixed, proj_g, ya, yb)


FOX_TQ = 256


def _scan_rows(x, reverse):
    n = x.shape[0]
    row = lax.broadcasted_iota(jnp.int32, x.shape, 0)
    k = 1
    while k < n:
        if reverse:
            x = x + jnp.where(row < n - k, pltpu.roll(x, n - k, 0), 0.0)
        else:
            x = x + jnp.where(row >= k, pltpu.roll(x, k, 0), 0.0)
        k *= 2
    return x


def _fox_scan(proj_f, b_pad):
    def body(f_ref, b_ref, fc_ref, fr_ref):
        z = f_ref[...] + b_ref[...]
        lf = jnp.minimum(z, 0.0) - jnp.log1p(jnp.exp(-jnp.abs(z)))
        f = _scan_rows(lf, reverse=False)
        fc_ref[...] = f
        fr_ref[...] = f.T

    return pl.pallas_call(
        body, name="fox_scan", grid=(1,),
        in_specs=[pl.BlockSpec((S, LANES), lambda i: (0, 0)), pl.BlockSpec((1, LANES), lambda i: (0, 0))],
        out_specs=[pl.BlockSpec((S, LANES), lambda i: (0, 0)), pl.BlockSpec((LANES, S), lambda i: (0, 0))],
        out_shape=[jax.ShapeDtypeStruct((S, LANES), F32), jax.ShapeDtypeStruct((LANES, S), F32)],
        compiler_params=_cparams(("arbitrary",)),
    )(proj_f, b_pad)


def _fox_dscan(dft, dfs, proj_f, b_pad, dproj):
    def body(dft_ref, dfs_ref, f_ref, b_ref, _, dfa_ref, db_ref):
        dfs_pad = jnp.concatenate([dfs_ref[...], jnp.zeros((LANES - FOX_H, S), F32)], axis=0)
        df = dfs_pad.T + dft_ref[0]
        for hp in range(1, dft_ref.shape[0]):
            df = df + pltpu.roll(dft_ref[hp], 2 * hp, 1)
        dlf = _scan_rows(df, reverse=True)
        z = f_ref[...] + b_ref[...]
        lane = lax.broadcasted_iota(jnp.int32, (S, LANES), 1)
        dfa = jnp.where(lane < FOX_H, dlf / (1.0 + jnp.exp(z)), 0.0)
        dfa_ref[:, :LANES] = dfa.astype(BF16)
        dfa_ref[:, LANES:] = jnp.zeros((S, CB - LANES), BF16)
        db_ref[...] = jnp.sum(dfa, axis=0, keepdims=True)

    return pl.pallas_call(
        body, name="fox_dscan", grid=(1,),
        in_specs=[pl.BlockSpec(dft.shape, lambda i: (0, 0, 0)), pl.BlockSpec((FOX_H, S), lambda i: (0, 0)),
                  pl.BlockSpec((S, LANES), lambda i: (0, 0)), pl.BlockSpec((1, LANES), lambda i: (0, 0)), ANY],
        out_specs=[pl.BlockSpec((S, CB), lambda i: (0, SEG_F // CB)), pl.BlockSpec((1, LANES), lambda i: (0, 0))],
        out_shape=[jax.ShapeDtypeStruct((S, D_PAD), BF16), jax.ShapeDtypeStruct((1, LANES), F32)],
        input_output_aliases={4: 0},
        compiler_params=_cparams(("arbitrary",)),
    )(dft, dfs, proj_f, b_pad, dproj)


FOX_NQ = S // FOX_TQ
FOX_HP = FOX_W // LANES
HEADS_PER_LB = LANES // HD


def _fox_scores(q, k, ft_all, fs_all, head, w):
    lane = lax.broadcasted_iota(jnp.int32, ft_all.shape, 1)
    sub = lax.broadcasted_iota(jnp.int32, fs_all.shape, 0)
    ft = jnp.sum(jnp.where(lane == head, ft_all, 0.0), axis=-1, keepdims=True)
    fs = jnp.sum(jnp.where(sub == head, fs_all, 0.0), axis=0, keepdims=True)
    s = (_dot(q, k, _NT) * SCALE + ft) - fs
    row = lax.broadcasted_iota(jnp.int32, (FOX_TQ, FOX_TQ), 0)
    col = lax.broadcasted_iota(jnp.int32, (FOX_TQ, FOX_TQ), 1)
    diag = jnp.where(col <= row, s[:, w * FOX_TQ:], NEG_INF)
    return diag if w == 0 else jnp.concatenate([s[:, :w * FOX_TQ], diag], axis=1)


def _fox_fwd(proj_a, fcol, frow):
    def body(q_ref, k_ref, v_ref, ft_ref, fs_ref, o_ref, lse_ref):
        hp = pl.program_id(0)
        qi = pl.program_id(1)
        lane = lax.broadcasted_iota(jnp.int32, (FOX_TQ, LANES), 1)
        for w in range(FOX_NQ):
            @pl.when(qi == w)
            def _(w=w):
                width = (w + 1) * FOX_TQ
                ft_all = ft_ref[...]
                fs_all = fs_ref[:, :width]
                lse = jnp.zeros((FOX_TQ, LANES), F32)
                for hh in range(HEADS_PER_LB):
                    sl = slice(hh * HD, (hh + 1) * HD)
                    s = _fox_scores(q_ref[:, sl], k_ref[:width, sl], ft_all, fs_all, hp * HEADS_PER_LB + hh, w)
                    m = jnp.max(s, axis=-1, keepdims=True)
                    p = jnp.exp(s - m)
                    l = jnp.sum(p, axis=-1, keepdims=True)
                    o_ref[:, sl] = _dot(p / l, v_ref[:width, sl], _NN)
                    lse = jnp.where(lane == hh, m + jnp.log(l), lse)
                lse_ref[...] = lse

    return pl.pallas_call(
        body, name="fox_fwd", grid=(FOX_HP, FOX_NQ),
        in_specs=[pl.BlockSpec((FOX_TQ, LANES), lambda h, i: (i, h)),
                  pl.BlockSpec((S, LANES), lambda h, i: (0, FOX_HP + h)),
                  pl.BlockSpec((S, LANES), lambda h, i: (0, 2 * FOX_HP + h)),
                  pl.BlockSpec((FOX_TQ, LANES), lambda h, i: (i, 0)),
                  pl.BlockSpec((FOX_H, S), lambda h, i: (0, 0))],
        out_specs=[pl.BlockSpec((FOX_TQ, LANES), lambda h, i: (i, h)),
                   pl.BlockSpec((None, FOX_TQ, LANES), lambda h, i: (h, i, 0))],
        out_shape=[jax.ShapeDtypeStruct((S, FOX_W), F32), jax.ShapeDtypeStruct((FOX_HP, S, LANES), F32)],
        compiler_params=_cparams(("parallel", "parallel")),
    )(proj_a, proj_a, proj_a, fcol, frow)


def _fox_bwd(proj_a, fcol, frow, lse, do, dproj, deps=()):
    n_q = FOX_NQ

    def body(q_ref, k_ref, v_ref, ft_ref, fs_ref, lse_ref, do_ref, *refs):
        dqkv_ref, dft_ref, dfs_ref, dk_acc, dv_acc = refs[1 + len(deps):]
        hp = pl.program_id(0)
        t = pl.program_id(1)

        @pl.when(t == 0)
        def _():
            dk_acc[...] = jnp.zeros_like(dk_acc)
            dv_acc[...] = jnp.zeros_like(dv_acc)

        @pl.when((t == 0) & (hp == 0))
        def _():
            dfs_ref[...] = jnp.zeros_like(dfs_ref)

        lane = lax.broadcasted_iota(jnp.int32, (FOX_TQ, LANES), 1)
        for w in range(n_q):
            @pl.when(t == w)
            def _(w=w):
                width = (w + 1) * FOX_TQ
                ft_all = ft_ref[...]
                fs_all = fs_ref[:, :width]
                lse_all = lse_ref[...]
                sub = lax.broadcasted_iota(jnp.int32, (FOX_H, width), 0)
                dft = jnp.zeros((FOX_TQ, LANES), F32)
                for hh in range(HEADS_PER_LB):
                    sl = slice(hh * HD, (hh + 1) * HD)
                    head = hp * HEADS_PER_LB + hh
                    q = q_ref[:, sl]
                    k = k_ref[:width, sl]
                    do_h = do_ref[:, sl]
                    s = _fox_scores(q, k, ft_all, fs_all, head, w)
                    p = jnp.exp(s - lse_all[:, hh:hh + 1])
                    dp = _dot(do_h, v_ref[:width, sl], _NT)
                    ds = p * (dp - jnp.sum(dp * p, axis=-1, keepdims=True))
                    dft = jnp.where(lane == hh, jnp.sum(ds, axis=-1, keepdims=True), dft)
                    dfs_ref[:, :width] -= jnp.where(sub == head, jnp.sum(ds, axis=0, keepdims=True), 0.0)
                    dsb = (ds * SCALE).astype(BF16)
                    dqkv_ref[:, sl] = _dot(dsb, k, _NN).astype(BF16)
                    dk_acc[:width, sl] += _dot(dsb, q, _TN)
                    dv_acc[:width, sl] += _dot(p, do_h, _TN)
                dft_ref[...] = dft

        rows = pl.ds(pl.multiple_of((t % n_q) * FOX_TQ, FOX_TQ), FOX_TQ)

        @pl.when((t >= n_q) & (t < 2 * n_q))
        def _():
            dqkv_ref[...] = dk_acc[rows, :].astype(BF16)

        @pl.when(t >= 2 * n_q)
        def _():
            dqkv_ref[...] = dv_acc[rows, :].astype(BF16)

    def qidx(t):
        return jnp.minimum(t, n_q - 1)

    qblk = pl.BlockSpec((FOX_TQ, LANES), lambda h, t: (qidx(t), h))
    lane_blk = pl.BlockSpec((None, FOX_TQ, LANES), lambda h, t: (h, qidx(t), 0))
    return pl.pallas_call(
        body, name="fox_bwd", grid=(FOX_HP, 3 * n_q),
        in_specs=[qblk, pl.BlockSpec((S, LANES), lambda h, t: (0, FOX_HP + h)),
                  pl.BlockSpec((S, LANES), lambda h, t: (0, 2 * FOX_HP + h)),
                  pl.BlockSpec((FOX_TQ, LANES), lambda h, t: (qidx(t), 0)),
                  pl.BlockSpec((FOX_H, S), lambda h, t: (0, 0)), lane_blk, qblk, ANY] + [ANY] * len(deps),
        out_specs=[pl.BlockSpec((FOX_TQ, LANES), lambda h, t: (t % n_q, (t // n_q) * FOX_HP + h)), lane_blk,
                   pl.BlockSpec((FOX_H, S), lambda h, t: (0, 0))],
        out_shape=[jax.ShapeDtypeStruct((S, D_PAD), BF16),
                   jax.ShapeDtypeStruct((FOX_HP, S, LANES), F32), jax.ShapeDtypeStruct((FOX_H, S), F32)],
        scratch_shapes=[pltpu.VMEM((S, LANES), F32), pltpu.VMEM((S, LANES), F32)],
        input_output_aliases={7: 0},
        compiler_params=_cparams(("arbitrary", "arbitrary")),
    )(proj_a, proj_a, proj_a, fcol, frow, lse, do, dproj, *deps)


def _rope_tables():
    half = ROPE_DIM // 2
    inv_freq = jnp.power(jnp.float32(ROPE_THETA), -jnp.arange(half, dtype=F32) * 2.0 / ROPE_DIM)
    ang = jnp.arange(S).astype(F32)[:, None] * inv_freq[None, :]
    cos, sin = jnp.cos(ang), jnp.sin(ang)
    z = jnp.zeros((S, half), F32)
    rest = HD - ROPE_DIM
    t_self = jnp.concatenate([cos, cos, jnp.ones((S, rest), F32)], axis=1)
    t_up = jnp.concatenate([-sin, z, jnp.zeros((S, rest), F32)], axis=1)
    t_dn = jnp.concatenate([z, sin, jnp.zeros((S, rest), F32)], axis=1)
    rep = LANES // HD

    def by_residue(t):
        return jnp.stack([t.reshape(S // r, r, LANES).transpose(1, 0, 2).reshape(S, LANES) for r in DIL_R])

    return tuple(by_residue(jnp.tile(t, (1, rep))) for t in (t_self, t_up, t_dn))


def _residue_pieces(r):
    if r == 1:
        return [(slice(i, i + 512), slice(i, i + 512)) for i in range(0, S, 512)]
    n = S // r
    return [(pl.ds(j, n, stride=r), slice(j * n, (j + 1) * n)) for j in range(r)]


def _rope_apply(x, a, u, d, backward):
    half = ROPE_DIM // 2
    if backward:
        return x * a + pltpu.roll(x * u, half, 1) + pltpu.roll(x * d, LANES - half, 1)
    return x * a + pltpu.roll(x, LANES - half, 1) * u + pltpu.roll(x, half, 1) * d


LB_PER_CB = CB // LANES
ROPE_NB = 3 * DIL_G * LB_PER_CB


def _rope_split(proj_b, tables):
    def body(x_ref, a_ref, u_ref, d_ref, o_ref):
        j = pl.program_id(0)
        g = (j // LB_PER_CB) % DIL_G
        rotate = j < 2 * DIL_G * LB_PER_CB
        for gs in range(DIL_G):
            @pl.when(g == gs)
            def _(gs=gs):
                for tok, sub in _residue_pieces(DIL_R[gs]):
                    x = x_ref[tok, :]
                    y = _rope_apply(x, a_ref[sub, :], u_ref[sub, :], d_ref[sub, :], False)
                    o_ref[sub, :] = jnp.where(rotate, y, x).astype(BF16)

    tab = pl.BlockSpec((None, S, LANES), lambda j: ((j // LB_PER_CB) % DIL_G, 0, 0))
    return pl.pallas_call(
        body, name="rope_split", grid=(ROPE_NB,),
        in_specs=[pl.BlockSpec((S, LANES), lambda j: (0, j)), tab, tab, tab],
        out_specs=pl.BlockSpec((None, S, LANES), lambda j: (j // LB_PER_CB, 0, j % LB_PER_CB)),
        out_shape=jax.ShapeDtypeStruct((3 * DIL_G, S, CB), BF16),
        compiler_params=_cparams(("parallel",)),
    )(proj_b, *tables)


def _rope_merge_bwd(dq3, dk3, dv3, tables, dproj):
    per_tensor = DIL_G * LB_PER_CB

    def body(dq_ref, dk_ref, dv_ref, a_ref, u_ref, d_ref, _, o_ref, tmp):
        j = pl.program_id(0)
        t = j // per_tensor
        g = (j // LB_PER_CB) % DIL_G
        for gs in range(DIL_G):
            @pl.when((g == gs) & (j < ROPE_NB))
            def _(gs=gs):
                for tok, sub in _residue_pieces(DIL_R[gs]):
                    x = jnp.where(t == 0, dq_ref[sub, :], jnp.where(t == 1, dk_ref[sub, :], dv_ref[sub, :]))
                    y = _rope_apply(x, a_ref[sub, :], u_ref[sub, :], d_ref[sub, :], True)
                    tmp[tok, :] = jnp.where(t < 2, y, x)
                o_ref[...] = tmp[...].astype(BF16)

        @pl.when(j >= ROPE_NB)
        def _():
            o_ref[...] = jnp.zeros_like(o_ref)

    def src_spec(t):
        def index(j):
            jb = j // LB_PER_CB - t * DIL_G
            lane_blk = jnp.where(jb < 0, 0, jnp.where(jb >= DIL_G, LB_PER_CB - 1, j % LB_PER_CB))
            return jnp.clip(jb, 0, DIL_G - 1), 0, lane_blk
        return pl.BlockSpec((None, S, LANES), index)

    tab = pl.BlockSpec((None, S, LANES), lambda j: ((j // LB_PER_CB) % DIL_G, 0, 0))
    return pl.pallas_call(
        body, name="rope_merge_bwd", grid=(ROPE_NB + LB_PER_CB,),
        in_specs=[src_spec(0), src_spec(1), src_spec(2), tab, tab, tab, ANY],
        out_specs=pl.BlockSpec((S, LANES), lambda j: (0, j + SEG_B // LANES)),
        out_shape=jax.ShapeDtypeStruct((S, D_PAD), BF16),
        scratch_shapes=[pltpu.VMEM((S, LANES), F32)],
        input_output_aliases={6: 0},
        compiler_params=_cparams(("arbitrary",)),
    )(dq3, dk3, dv3, *tables, dproj)


DIL_NB = S // DIL_BLK


def _dil_mask(g, n):
    blocks_per_seq = jnp.right_shift(DIL_NB, 2 * g)
    has_prev = jnp.bitwise_and(n, blocks_per_seq - 1) != 0
    a = lax.broadcasted_iota(jnp.int32, (DIL_BLK, 2 * DIL_BLK), 0)
    kk = lax.broadcasted_iota(jnp.int32, (DIL_BLK, 2 * DIL_BLK), 1)
    diff = DIL_BLK + a - kk
    return (diff >= 0) & (diff <= DIL_BLK) & ((kk >= DIL_BLK) | has_prev)


def _dil_specs():
    def cur(t):
        return pl.BlockSpec((None, DIL_BLK, DIL_OUT), lambda g, n: (t * DIL_G + g, n, 0))

    def prev(t):
        return pl.BlockSpec((None, DIL_BLK, DIL_OUT), lambda g, n: (t * DIL_G + g, jnp.maximum(n - 1, 0), 0))

    lane_blk = pl.BlockSpec((None, DIL_BLK, LANES), lambda g, n: (g, n, 0))
    return cur, prev, lane_blk


def _dil_fwd(qkv3):
    def body(q_ref, kc_ref, kp_ref, vc_ref, vp_ref, o_ref, lse_ref):
        ok = _dil_mask(pl.program_id(0), pl.program_id(1))
        k2 = jnp.concatenate([kp_ref[...], kc_ref[...]], axis=0)
        v2 = jnp.concatenate([vp_ref[...], vc_ref[...]], axis=0)
        lane = lax.broadcasted_iota(jnp.int32, (DIL_BLK, LANES), 1)
        lse = jnp.zeros((DIL_BLK, LANES), F32)
        for h in range(DIL_HG):
            sl = slice(h * HD, (h + 1) * HD)
            s = jnp.where(ok, _dot(q_ref[:, sl], k2[:, sl], _NT) * SCALE, NEG_INF)
            m = jnp.max(s, axis=-1, keepdims=True)
            p = jnp.exp(s - m)
            l = jnp.sum(p, axis=-1, keepdims=True)
            o_ref[:, sl] = _dot(p / l, v2[:, sl], _NN)
            lse = jnp.where(lane == h, m + jnp.log(l), lse)
        lse_ref[...] = lse

    cur, prev, lane_blk = _dil_specs()
    out_blk = pl.BlockSpec((None, DIL_BLK, DIL_OUT), lambda g, n: (g, n, 0))
    return pl.pallas_call(
        body, name="dil_fwd", grid=(DIL_G, DIL_NB),
        in_specs=[cur(0), cur(1), prev(1), cur(2), prev(2)], out_specs=[out_blk, lane_blk],
        out_shape=[jax.ShapeDtypeStruct((DIL_G, S, DIL_OUT), F32), jax.ShapeDtypeStruct((DIL_G, S, LANES), F32)],
        compiler_params=_cparams(("parallel", "parallel")),
    )(qkv3, qkv3, qkv3, qkv3, qkv3)


def _dil_bwd(qkv3, lse3, do3, c3):
    def body(q_ref, kc_ref, kp_ref, vc_ref, vp_ref, lse_ref, do_ref, c_ref, dq_ref, dk_ref, dv_ref):
        n = pl.program_id(1)

        @pl.when(n == 0)
        def _():
            dk_ref[...] = jnp.zeros_like(dk_ref)
            dv_ref[...] = jnp.zeros_like(dv_ref)

        ok = _dil_mask(pl.program_id(0), n)
        k2 = jnp.concatenate([kp_ref[...], kc_ref[...]], axis=0)
        v2 = jnp.concatenate([vp_ref[...], vc_ref[...]], axis=0)
        lse = lse_ref[...]
        c = c_ref[...]
        cur_rows = pl.ds(pl.multiple_of(n * DIL_BLK, DIL_BLK), DIL_BLK)
        prev_rows = pl.ds(pl.multiple_of(jnp.maximum(n - 1, 0) * DIL_BLK, DIL_BLK), DIL_BLK)
        for h in range(DIL_HG):
            sl = slice(h * HD, (h + 1) * HD)
            q = q_ref[:, sl]
            do_h = do_ref[:, sl]
            s = jnp.where(ok, _dot(q, k2[:, sl], _NT) * SCALE, NEG_INF)
            p = jnp.exp(s - lse[:, h:h + 1])
            dp = _dot(do_h, v2[:, sl], _NT)
            ds = p * (dp - c[:, h:h + 1])
            dsb = (ds * SCALE).astype(BF16)
            dq_ref[:, sl] = _dot(dsb, k2[:, sl], _NN)
            dk2 = _dot(dsb, q, _TN)
            dv2 = _dot(p, do_h, _TN)
            dk_ref[prev_rows, sl] += dk2[:DIL_BLK]
            dk_ref[cur_rows, sl] += dk2[DIL_BLK:]
            dv_ref[prev_rows, sl] += dv2[:DIL_BLK]
            dv_ref[cur_rows, sl] += dv2[DIL_BLK:]

    cur, prev, lane_blk = _dil_specs()
    blk = pl.BlockSpec((None, DIL_BLK, DIL_OUT), lambda g, n: (g, n, 0))
    whole = pl.BlockSpec((None, S, DIL_OUT), lambda g, n: (g, 0, 0))
    return pl.pallas_call(
        body, name="dil_bwd", grid=(DIL_G, DIL_NB),
        in_specs=[cur(0), cur(1), prev(1), cur(2), prev(2), lane_blk, blk, lane_blk],
        out_specs=[blk, whole, whole],
        out_shape=[jax.ShapeDtypeStruct((DIL_G, S, DIL_OUT), F32)] * 3,
        compiler_params=_cparams(("parallel", "arbitrary")),
    )(qkv3, qkv3, qkv3, qkv3, qkv3, lse3, do3, c3)


COMB_T = 256


def _dil_combine(o3, lse3):
    heads_per_lb = LANES // HD

    def body(o_ref, lse_ref, ob_ref, al_ref, *scratch):
        o_tok = [scratch[LB_PER_CB * gg:LB_PER_CB * (gg + 1)] for gg in range(DIL_G)]
        lse_tok = scratch[LB_PER_CB * DIL_G:]
        g = pl.program_id(0)
        for gs in range(DIL_G):
            @pl.when(g == gs)
            def _(gs=gs):
                for tok, sub in _residue_pieces(DIL_R[gs]):
                    for hf in range(LB_PER_CB):
                        o_tok[gs][hf][tok, :] = o_ref[sub, hf * LANES:(hf + 1) * LANES]
                    lse_tok[gs][tok, :] = lse_ref[sub, :]

        @pl.when(g == DIL_G - 1)
        def _():
            def chunk(i, carry):
                rows = pl.ds(pl.multiple_of(i * COMB_T, COMB_T), COMB_T)
                lse = [lse_tok[gg][rows, :] for gg in range(DIL_G)]
                m = jnp.maximum(jnp.maximum(lse[0], lse[1]), lse[2])
                e = [jnp.exp(lse[gg] - m) for gg in range(DIL_G)]
                den = (e[0] + e[1]) + e[2]
                al = [e[gg] / den for gg in range(DIL_G)]
                for gg in range(DIL_G):
                    al_ref[gg, rows, :] = al[gg]
                for h in range(DIL_HG):
                    hf, sl = h // heads_per_lb, slice((h % heads_per_lb) * HD, (h % heads_per_lb + 1) * HD)
                    acc = al[0][:, h:h + 1] * o_tok[0][hf][rows, sl]
                    for gg in range(1, DIL_G):
                        acc = acc + al[gg][:, h:h + 1] * o_tok[gg][hf][rows, sl]
                    ob_ref[rows, h * HD:(h + 1) * HD] = acc
                return carry

            lax.fori_loop(0, S // COMB_T, chunk, 0)

    return pl.pallas_call(
        body, name="dil_combine", grid=(DIL_G,),
        in_specs=[pl.BlockSpec((None, S, DIL_OUT), lambda g: (g, 0, 0)),
                  pl.BlockSpec((None, S, LANES), lambda g: (g, 0, 0))],
        out_specs=[pl.BlockSpec((S, DIL_OUT), lambda g: (0, 0)),
                   pl.BlockSpec((DIL_G, S, LANES), lambda g: (0, 0, 0))],
        out_shape=[jax.ShapeDtypeStruct((S, DIL_OUT), F32), jax.ShapeDtypeStruct((DIL_G, S, LANES), F32)],
        scratch_shapes=[pltpu.VMEM((S, LANES), F32)] * (DIL_G * (LB_PER_CB + 1)),
        compiler_params=_cparams(("arbitrary",)),
    )(o3, lse3)


def _dil_combine_bwd(dob, ob, alpha, deps=()):
    heads_per_lb = LANES // HD

    def body(dob_ref, ob_ref, al_ref, *refs):
        do_ref, c_ref = refs[len(deps):]
        g = pl.program_id(0)
        hf = pl.program_id(1)
        for gs in range(DIL_G):
            @pl.when(g == gs)
            def _(gs=gs):
                for tok, sub in _residue_pieces(DIL_R[gs]):
                    dob = dob_ref[tok, :]
                    prod = dob * ob_ref[tok, :]
                    al = al_ref[tok, :]
                    lane = lax.broadcasted_iota(jnp.int32, al.shape, 1)
                    c = jnp.where(hf == 0, 0.0, c_ref[sub, :])
                    for hh in range(heads_per_lb):
                        sl = slice(hh * HD, (hh + 1) * HD)
                        head = hf * heads_per_lb + hh
                        a = jnp.sum(jnp.where(lane == head, al, 0.0), axis=-1, keepdims=True)
                        do_ref[sub, sl] = (a * dob[:, sl]).astype(BF16)
                        c = jnp.where(lane == head, a * jnp.sum(prod[:, sl], axis=-1, keepdims=True), c)
                    c_ref[sub, :] = c

    half = pl.BlockSpec((S, LANES), lambda g, hf: (0, hf))
    return pl.pallas_call(
        body, name="dil_combine_bwd", grid=(DIL_G, LB_PER_CB),
        in_specs=[half, half, pl.BlockSpec((None, S, LANES), lambda g, hf: (g, 0, 0))] + [ANY] * len(deps),
        out_specs=[pl.BlockSpec((None, S, LANES), lambda g, hf: (g, 0, hf)),
                   pl.BlockSpec((None, S, LANES), lambda g, hf: (g, 0, 0))],
        out_shape=[jax.ShapeDtypeStruct((DIL_G, S, DIL_OUT), BF16), jax.ShapeDtypeStruct((DIL_G, S, LANES), F32)],
        compiler_params=_cparams(("parallel", "arbitrary")),
    )(dob, ob, alpha, *deps)


def _local_step(x, tgt, g_attn, g_mlp, g_final, b_pad, w_in_t, hooks):
    tables = _rope_tables()

    h1 = _rms_fwd(x, g_attn, "rms_attn_fwd", deps=hooks.first_deps())
    proj_a = _mm(h1, w_in_t, mode="nt", tm=S, tn=512, n=3 * FOX_W, out_dtypes=[BF16], name="proj_a")
    proj_b = _mm(h1, w_in_t, mode="nt", tm=S, tn=CB, n=3 * DIL_W, b_off=SEG_B // CB, out_dtypes=[F32], name="proj_b")
    proj_g = _mm(h1, w_in_t, mode="nt", tm=S, tn=CB, n=2 * D, b_off=SEG_G // CB, out_dtypes=[F32], name="proj_g")
    proj_f = _mm(h1, w_in_t, mode="nt", tm=S, tn=LANES, n=LANES, b_off=SEG_F // LANES, out_dtypes=[F32], name="proj_f")

    fcol, frow = _fox_scan(proj_f, b_pad)
    oa, lse_a = _fox_fwd(proj_a, fcol, frow)

    qkv3 = _rope_split(proj_b, tables)
    o3, lse3 = _dil_fwd(qkv3)
    ob, alpha = _dil_combine(o3, lse3)

    w_a, w_b, w_out = hooks.mixer_weights(ob)
    ya = _mm(oa, w_a, mode="nn", tm=S, tn=CB, out_dtypes=[F32], name="branch_a")
    yb = _mm(ob, w_b, mode="nn", tm=S, tn=CB, out_dtypes=[F32], name="branch_b")
    mixed = _gate_fwd(proj_g, ya, yb)
    x2 = _mm(mixed, w_out, mode="nn", tm=S, tn=CB, out_dtypes=[F32], name="out_proj",
             epilogue=lambda acc, res: (res + acc,), extras=(x,))

    h2 = _rms_fwd(x2, g_mlp, "rms_mlp_fwd")
    w_up_sh, w_down = hooks.mlp_weights(h2)

    def up_epilogue(acc):
        r = jnp.maximum(acc, 0.0)
        return acc, r * r

    u, act = _mm(h2, w_up_sh, mode="nn", tm=S, tn=DFF // N_DEV, b_sharded=True, out_dtypes=[F32, BF16],
                 name="mlp_up", epilogue=up_epilogue)
    x3 = _mm(act, w_down, mode="nn", tm=512, tn=512, out_dtypes=[F32], name="mlp_down",
             epilogue=lambda acc, res: (res + acc,), extras=(x2,))

    dx3, dg_final, loss = _final_norm_loss(x3, g_final, tgt)

    du = _mm(dx3, w_down, mode="nt", tm=S, tn=CB, out_dtypes=[BF16], name="mlp_down_bwd",
             epilogue=lambda acc, u_t: (acc * (2.0 * jnp.maximum(u_t, 0.0)),), extras=(u,))
    dw_down = _mm(act, dx3, mode="tn", tm=512, tn=512, out_dtypes=[F32], name="dw_down")
    dw_up_sh = _mm(h2, du, mode="tn", tm=D, tn=DFF // N_DEV, out_sharded=True, out_dtypes=[F32], name="dw_up")
    dh2 = _mm(du, w_up_sh, mode="nt", tm=512, tn=512, b_sharded=True, out_dtypes=[F32], name="mlp_up_bwd",
              deps=hooks.mlp_grads(dw_up_sh, dw_down))
    dx2, dg_mlp = _rms_bwd(x2, g_mlp, dh2, dx3, "rms_mlp_bwd")

    dmixed = _mm(dx2, w_out, mode="nt", tm=S, tn=CB, out_dtypes=[F32], name="out_proj_bwd")
    dw_out = _mm(mixed, dx2, mode="tn", tm=D, tn=CB, out_dtypes=[F32], name="dw_out")

    dyab, dproj = _gate_bwd(dmixed, proj_g, ya, yb)
    doa = _mm(dyab, w_a, mode="nt", tm=S, tn=CB, k=D, a_off=0, out_dtypes=[BF16], name="branch_a_bwd")
    dw_a = _mm(oa, dyab, mode="tn", tm=FOX_W, tn=CB, n=D, b_off=0, out_dtypes=[F32], name="dw_branch_a")
    dob = _mm(dyab, w_b, mode="nt", tm=S, tn=CB, k=D, a_off=1, out_dtypes=[F32], name="branch_b_bwd")
    dw_b = _mm(ob, dyab, mode="tn", tm=DIL_OUT, tn=CB, n=D, b_off=D // CB, out_dtypes=[F32], name="dw_branch_b")

    dproj, dft, dfs = _fox_bwd(proj_a, fcol, frow, lse_a, doa, dproj, deps=hooks.mixer_grads(dw_a, dw_b, dw_out))
    dproj, db = _fox_dscan(dft, dfs, proj_f, b_pad, dproj)

    do3, c3 = _dil_combine_bwd(dob, ob, alpha, deps=hooks.mid_backward(db))
    dq3, dk3, dv3 = _dil_bwd(qkv3, lse3, do3, c3)
    dproj = _rope_merge_bwd(dq3, dk3, dv3, tables, dproj)

    dw_in_t = _mm(dproj, h1, mode="tn", tm=CB, tn=D, out_dtypes=[F32], name="dw_in")
    dh1 = _mm(dproj, w_in_t, mode="nn", tm=512, tn=CB, out_dtypes=[F32], name="proj_bwd",
              deps=hooks.w_in_grad(dw_in_t))
    dx, dg_attn = _rms_bwd(x, g_attn, dh1, dx2, "rms_attn_bwd")

    return loss, dx, (dg_attn, db, dg_mlp, dg_final)


SHARD_SHAPES = ((D_IN // N_DEV, D), (FOX_W, D // N_DEV), (DIL_OUT, D // N_DEV), (D // N_DEV, D),
                (D, DFF // N_DEV), (DFF // N_DEV, D))
W_NAMES = ("w_in", "w_a", "w_b", "w_out", "w_up", "w_down")
AG_COPIES = 7
HBM = pl.BlockSpec(memory_space=pltpu.HBM)
SEM = pl.BlockSpec(memory_space=pltpu.SEMAPHORE)


def _place():
    return lax.axis_index("x"), lax.axis_index("y"), lax.axis_index("c")


def _all_gather(shards):
    n_w = len(shards)
    shapes = [t.shape for t in shards]

    def body(*refs):
        in_refs, out_refs = refs[:n_w], refs[n_w:2 * n_w]
        stage, cast = refs[2 * n_w:3 * n_w], refs[3 * n_w:4 * n_w]
        send_sems, recv_sems, local_sems, load_sems = refs[4 * n_w:]
        x, y, c = _place()
        me, sibling = (x, y, c), (x, y, 1 - c)
        chips = [(1 - x, y), (x, 1 - y), (1 - x, 1 - y)]

        def slot(i, px, py, pc):
            return out_refs[i].at[4 * px + 2 * py + pc]

        def copy(i, k, block, to, src=None):
            return pltpu.make_async_remote_copy(
                src_ref=slot(i, *block) if src is None else src, dst_ref=slot(i, *block),
                send_sem=send_sems.at[i * AG_COPIES + k], recv_sem=recv_sems.at[i * AG_COPIES + k],
                device_id=to, device_id_type=MESH)

        loads = [pltpu.make_async_copy(in_refs[i], stage[i], load_sems.at[i]) for i in range(n_w)]
        for ld in loads:
            ld.start()
        local, remote = [], []
        for i in range(n_w):
            loads[i].wait()
            cast[i][...] = stage[i][...].astype(BF16)
            local.append(pltpu.make_async_copy(cast[i], slot(i, *me), local_sems.at[i]))
            local[-1].start()
            first = [copy(i, 0, me, sibling, src=cast[i])]
            first += [copy(i, 1 + j, me, (*chip, c), src=cast[i]) for j, chip in enumerate(chips)]
            for cp in first:
                cp.start()
            remote += first
        for i in range(n_w):
            for j, chip in enumerate(chips):
                copy(i, 1 + j, (*chip, c), me).wait_recv()
                remote.append(copy(i, 4 + j, (*chip, c), sibling))
                remote[-1].start()
        for i in range(n_w):
            copy(i, 0, sibling, me).wait_recv()
            for j, chip in enumerate(chips):
                copy(i, 4 + j, (*chip, 1 - c), me).wait_recv()
        for cp in remote:
            cp.wait_send()
        for cp in local:
            cp.wait()

    return pl.pallas_call(
        body, name="all_gather_w_in",
        out_shape=[jax.ShapeDtypeStruct((N_DEV,) + sh, BF16) for sh in shapes],
        in_specs=[ANY] * n_w, out_specs=[ANY] * n_w,
        scratch_shapes=[pltpu.VMEM(sh, F32) for sh in shapes] + [pltpu.VMEM(sh, BF16) for sh in shapes]
                       + [pltpu.SemaphoreType.DMA((n_w * AG_COPIES,)), pltpu.SemaphoreType.DMA((n_w * AG_COPIES,)),
                          pltpu.SemaphoreType.DMA((n_w,)), pltpu.SemaphoreType.DMA((n_w,))],
        compiler_params=_cparams(None),
    )(*shards)


def _place_own(shards):
    n_w = len(shards)
    shapes = [t.shape for t in shards]

    def body(*refs):
        in_refs, out_refs = refs[:n_w], refs[n_w:2 * n_w]
        stage, cast = refs[2 * n_w:3 * n_w], refs[3 * n_w:4 * n_w]
        load_sems, store_sems = refs[4 * n_w:]
        x, y, c = _place()
        loads = [pltpu.make_async_copy(in_refs[i], stage[i], load_sems.at[i]) for i in range(n_w)]
        for ld in loads:
            ld.start()
        stores = []
        for i in range(n_w):
            loads[i].wait()
            cast[i][...] = stage[i][...].astype(BF16)
            stores.append(pltpu.make_async_copy(cast[i], out_refs[i].at[4 * x + 2 * y + c], store_sems.at[i]))
            stores[-1].start()
        for st in stores:
            st.wait()

    return pl.pallas_call(
        body, name="place_own_shards",
        out_shape=[jax.ShapeDtypeStruct((N_DEV,) + sh, BF16) for sh in shapes],
        in_specs=[ANY] * n_w, out_specs=[ANY] * n_w,
        scratch_shapes=[pltpu.VMEM(sh, F32) for sh in shapes] + [pltpu.VMEM(sh, BF16) for sh in shapes]
                       + [pltpu.SemaphoreType.DMA((n_w,)), pltpu.SemaphoreType.DMA((n_w,))],
        compiler_params=_cparams(None),
    )(*shards)


PEER_COPIES = N_DEV - 1
SIBLING_COPIES = 4
CHIP_COPIES = 3


def _peer_copies(_, land_refs, send_sems, recv_sems):
    x, y, c = _place()
    mine = 4 * x + 2 * y + c
    copies = []
    for i, ref in enumerate(land_refs):
        for k in range(1, N_DEV):
            peer = (x ^ (k >> 2), y ^ ((k >> 1) & 1), c ^ (k & 1))
            n = i * PEER_COPIES + k - 1
            copies.append(pltpu.make_async_remote_copy(
                src_ref=ref.at[mine], dst_ref=ref.at[mine], send_sem=send_sems.at[n], recv_sem=recv_sems.at[n],
                device_id=peer, device_id_type=MESH))
    return copies


def _sibling_copies(g_refs, r_refs, send_sems, recv_sems):
    x, y, c = _place()
    return [pltpu.make_async_remote_copy(
        src_ref=g_refs[i].at[2 * k + (1 - c)], dst_ref=r_refs[i].at[k],
        send_sem=send_sems.at[SIBLING_COPIES * i + k], recv_sem=recv_sems.at[SIBLING_COPIES * i + k],
        device_id=(x, y, 1 - c), device_id_type=MESH) for i in range(len(g_refs)) for k in range(SIBLING_COPIES)]


def _chip_copies(p_refs, r_refs, send_sems, recv_sems):
    x, y, c = _place()
    chips = [(1 - x, y), (x, 1 - y), (1 - x, 1 - y)]
    return [pltpu.make_async_remote_copy(
        src_ref=p_refs[i].at[2 * cx + cy], dst_ref=r_refs[i].at[j],
        send_sem=send_sems.at[CHIP_COPIES * i + j], recv_sem=recv_sems.at[CHIP_COPIES * i + j],
        device_id=(cx, cy, c), device_id_type=MESH) for i in range(len(p_refs)) for j, (cx, cy) in enumerate(chips)]


def _in_hbm(a):
    return pltpu.with_memory_space_constraint(a, pltpu.HBM)


def _exchange_start(name, copies_fn, n_copies, srcs, lands, after=()):
    n_s, n_l, n_a = len(srcs), len(lands), len(after)

    def body(*refs):
        outs = refs[n_s + n_l + n_a:]
        for cp in copies_fn(refs[:n_s], refs[n_s:n_s + n_l], outs[0], outs[1]):
            cp.start()
        outs[-1][...] = jnp.zeros_like(outs[-1])

    res = pl.pallas_call(
        body, name=name,
        out_shape=[pltpu.SemaphoreType.DMA((n_copies,)), pltpu.SemaphoreType.DMA((n_copies,))]
                  + [pltpu.HBM(t.shape, t.dtype) for t in (*srcs, *lands)] + [jax.ShapeDtypeStruct((8, LANES), F32)],
        in_specs=[HBM] * (n_s + n_l) + [ANY] * n_a,
        out_specs=[SEM, SEM] + [HBM] * (n_s + n_l) + [pl.BlockSpec(memory_space=pltpu.VMEM)],
        input_output_aliases={i: 2 + i for i in range(n_s + n_l)},
        compiler_params=pltpu.CompilerParams(has_side_effects=pltpu.SideEffectType.DATAFLOW_SIDE_EFFECTING),
    )(*[_in_hbm(t) for t in (*srcs, *lands)], *after)
    return res[0], res[1], list(res[2:2 + n_s]), list(res[2 + n_s:2 + n_s + n_l]), res[-1]


def _exchange_wait(name, copies_fn, started, after):
    send_sems, recv_sems, srcs, lands, _ = started
    n_s, n_l = len(srcs), len(lands)

    def body(*refs):
        for cp in copies_fn(refs[:n_s], refs[n_s:n_s + n_l], refs[n_s + n_l], refs[n_s + n_l + 1]):
            cp.wait_send()
            cp.wait_recv()

    res = pl.pallas_call(
        body, name=name,
        out_shape=[pltpu.HBM(t.shape, t.dtype) for t in (*srcs, *lands)],
        in_specs=[HBM] * (n_s + n_l) + [SEM, SEM, ANY],
        out_specs=[HBM] * (n_s + n_l),
        input_output_aliases={i: i for i in range(n_s + n_l)},
        compiler_params=pltpu.CompilerParams(has_side_effects=pltpu.SideEffectType.DATAFLOW_SIDE_EFFECTING),
    )(*srcs, *lands, send_sems, recv_sems, after)
    return list(res[:n_s]), list(res[n_s:])


def _shard_block(shape):
    rows, cols = shape
    if rows % 256:
        return rows, 256
    return min(rows, 512 if cols <= LANES else 256), cols


def _chip_partial(ids, g_stack, recv1, name):
    shape = g_stack.shape[1:]
    br, bc = _shard_block(shape)

    def body(ids_ref, g_ref, r_ref, pb_ref, own_ref):
        s = g_ref[...] + r_ref[...]
        pb_ref[...] = s.astype(BF16)

        @pl.when(pl.program_id(2) == ids_ref[1])
        def _():
            own_ref[...] = s

    grid_spec = pltpu.PrefetchScalarGridSpec(
        num_scalar_prefetch=1, grid=(shape[0] // br, shape[1] // bc, 4),
        in_specs=[pl.BlockSpec((None, br, bc), lambda r, q, k, ids: (2 * k + ids[0], r, q)),
                  pl.BlockSpec((None, br, bc), lambda r, q, k, ids: (k, r, q))],
        out_specs=[pl.BlockSpec((None, br, bc), lambda r, q, k, ids: (k, r, q)),
                   pl.BlockSpec((br, bc), lambda r, q, k, ids: (r, q))])
    return pl.pallas_call(
        body, name=name, grid_spec=grid_spec,
        out_shape=[jax.ShapeDtypeStruct((4,) + shape, BF16), jax.ShapeDtypeStruct(shape, F32)],
        compiler_params=_cparams(("parallel", "parallel", "arbitrary")),
    )(ids, g_stack, recv1)


def _adamw(w, g, m, v):
    m = ADAM_B1 * m + (1.0 - ADAM_B1) * g
    v = ADAM_B2 * v + (1.0 - ADAM_B2) * (g * g)
    m_hat = m / (1.0 - ADAM_B1 ** ADAM_STEP)
    v_hat = v / (1.0 - ADAM_B2 ** ADAM_STEP)
    delta = -ADAM_LR * (m_hat / (jnp.sqrt(v_hat) + ADAM_EPS) + ADAM_WD * w)
    return delta, m, v


def _reduce_adamw(own, recv2, w, m, v, name, deps=()):
    shape = own.shape
    br, bc = _shard_block(shape)

    def body(own_ref, r_ref, w_ref, m_ref, v_ref, *refs):
        g_ref, d_ref, nm_ref, nv_ref = refs[len(deps):]
        g = own_ref[...]
        for j in range(3):
            g = g + r_ref[j].astype(F32)
        delta, nm, nv = _adamw(w_ref[...], g, m_ref[...], v_ref[...])
        g_ref[...] = g
        d_ref[...] = delta
        nm_ref[...] = nm
        nv_ref[...] = nv

    blk = pl.BlockSpec((br, bc), lambda r, q: (r, q))
    return pl.pallas_call(
        body, name=name, grid=(shape[0] // br, shape[1] // bc),
        in_specs=[blk, pl.BlockSpec((3, br, bc), lambda r, q: (0, r, q)), blk, blk, blk] + [ANY] * len(deps),
        out_specs=[blk] * 4, out_shape=[jax.ShapeDtypeStruct(shape, F32)] * 4,
        compiler_params=_cparams(("parallel", "parallel")),
    )(own, recv2, w, m, v, *deps)


def _small_allreduce_adamw(gvec, w, m, v, deps=()):
    def body(g_ref, w_ref, m_ref, v_ref, *refs):
        go_ref, d_ref, nm_ref, nv_ref, buf, send_sems, recv_sems = refs[len(deps):]
        x, y, c = _place()
        my_slot = 4 * x + 2 * y + c
        buf[my_slot] = g_ref[...]
        copies = []
        for k in range(1, N_DEV):
            px, py, pc = x ^ (k >> 2), y ^ ((k >> 1) & 1), c ^ (k & 1)
            copies.append(pltpu.make_async_remote_copy(
                src_ref=g_ref, dst_ref=buf.at[my_slot], send_sem=send_sems.at[k - 1], recv_sem=recv_sems.at[k - 1],
                device_id=(px, py, pc), device_id_type=MESH))
        for cp in copies:
            cp.start()
        for k in range(1, N_DEV):
            px, py, pc = x ^ (k >> 2), y ^ ((k >> 1) & 1), c ^ (k & 1)
            pltpu.make_async_remote_copy(
                src_ref=g_ref, dst_ref=buf.at[4 * px + 2 * py + pc], send_sem=send_sems.at[k - 1],
                recv_sem=recv_sems.at[k - 1], device_id=(px, py, pc), device_id_type=MESH).wait_recv()
        for cp in copies:
            cp.wait_send()
        g = buf[0]
        for s in range(1, N_DEV):
            g = g + buf[s]
        delta, nm, nv = _adamw(w_ref[...], g, m_ref[...], v_ref[...])
        go_ref[...] = g
        d_ref[...] = delta
        nm_ref[...] = nm
        nv_ref[...] = nv

    vm = pl.BlockSpec(memory_space=pltpu.VMEM)
    return pl.pallas_call(
        body, name="small_allreduce_adamw",
        in_specs=[vm] * 4 + [ANY] * len(deps), out_specs=[vm] * 4,
        out_shape=[jax.ShapeDtypeStruct((SMALL_R, LANES), F32)] * 4,
        scratch_shapes=[pltpu.VMEM((N_DEV, SMALL_R, LANES), F32),
                        pltpu.SemaphoreType.DMA((N_DEV - 1,)), pltpu.SemaphoreType.DMA((N_DEV - 1,))],
    )(gvec, w, m, v, *deps)


def _cols_to_whole(stack):
    rows = stack.shape[1]
    return stack.transpose(1, 0, 2).reshape(rows, -1)


def _whole_to_cols(t):
    rows = t.shape[0]
    return t.reshape(rows, N_DEV, -1).transpose(1, 0, 2)


def _pad_w_in_t(wt):
    n_a = 3 * FOX_W
    n_b = n_a + FOX_H
    return jnp.concatenate([wt[:n_a], wt[n_b:n_b + 3 * DIL_W], jnp.zeros((CB, D), wt.dtype),
                            wt[n_b + 3 * DIL_W:], wt[n_a:n_b], jnp.zeros((CB - FOX_H, D), wt.dtype)], axis=0)


def _unpad_dw_in_t(dwp):
    return jnp.concatenate([dwp[:SEG_B], dwp[SEG_F:SEG_F + FOX_H], dwp[SEG_B:SEG_Z], dwp[SEG_G:SEG_F]], axis=0)


def _pack_small(g_attn, b, g_mlp, g_final):
    b_rows = jnp.pad(b, ((0, 7), (0, LANES - b.shape[1])))
    return jnp.concatenate([g_attn.reshape(8, LANES), g_mlp.reshape(8, LANES), g_final.reshape(8, LANES), b_rows], axis=0)


def _unpack_small(p):
    return (p[0:8].reshape(1, D), p[24:25, :FOX_H], p[8:16].reshape(1, D), p[16:24].reshape(D))


class _StepComm:
    def __init__(self, ids, late_shards, after_w_in):
        self.ids = ids
        lands = _place_own(late_shards)
        self.gather_mixer = _exchange_start("gather_mixer_start", _peer_copies, PEER_COPIES * 3, [], lands[:3],
                                            after=(after_w_in,))
        self.gather_mlp = _exchange_start("gather_mlp_start", _peer_copies, PEER_COPIES * 2, [], lands[3:],
                                          after=(self.gather_mixer[-1],))
        self.sibling = {}
        self.chips = {}
        self.own = {}
        self.names = {}

    def _reduce_start(self, group, names, grads, after=()):
        lands = [lax.empty((SIBLING_COPIES,) + t.shape[1:], F32) for t in grads]
        self.sibling[group] = _exchange_start("grad_%s_sibling_start" % group, _sibling_copies,
                                              SIBLING_COPIES * len(grads), grads, lands, after=after)
        self.names[group] = names
        return self.sibling[group][-1]

    def _reduce_mid(self, group, after):
        grads, recv1 = _exchange_wait("grad_%s_sibling_wait" % group, _sibling_copies, self.sibling[group], after)
        parts = [_chip_partial(self.ids, g, r, "grad_partial_" + n) for g, r, n in zip(grads, recv1, self.names[group])]
        self.own[group] = [p[1] for p in parts]
        srcs = [p[0] for p in parts]
        lands = [lax.empty((CHIP_COPIES,) + t.shape[1:], BF16) for t in srcs]
        self.chips[group] = _exchange_start("grad_%s_chips_start" % group, _chip_copies, CHIP_COPIES * len(srcs),
                                            srcs, lands)
        return self.chips[group][-1]

    def reduced(self, group, after):
        _, recv2 = _exchange_wait("grad_%s_chips_wait" % group, _chip_copies, self.chips[group], after)
        return list(zip(self.own[group], recv2))

    def first_deps(self):
        return [self.gather_mlp[-1]]

    def mixer_weights(self, after):
        _, (g_a, g_b, g_out) = _exchange_wait("gather_mixer_wait", _peer_copies, self.gather_mixer, after)
        return _cols_to_whole(g_a), _cols_to_whole(g_b), g_out.reshape(D, D)

    def mlp_weights(self, after):
        _, (g_up, g_down) = _exchange_wait("gather_mlp_wait", _peer_copies, self.gather_mlp, after)
        return g_up, g_down.reshape(DFF, D)

    def mlp_grads(self, dw_up_sh, dw_down):
        return [self._reduce_start("mlp", W_NAMES[4:], [dw_up_sh, dw_down.reshape((N_DEV,) + SHARD_SHAPES[5])])]

    def mixer_grads(self, dw_a, dw_b, dw_out):
        token = self._reduce_mid("mlp", dw_b)
        grads = [_whole_to_cols(dw_a), _whole_to_cols(dw_b), dw_out.reshape((N_DEV,) + SHARD_SHAPES[3])]
        return [self._reduce_start("mixer", W_NAMES[1:4], grads, after=(token,))]

    def mid_backward(self, after):
        return [self._reduce_mid("mixer", after)]

    def w_in_grad(self, dw_in_t):
        grads = [_unpad_dw_in_t(dw_in_t).reshape((N_DEV,) + SHARD_SHAPES[0])]
        return [self._reduce_start("w_in", W_NAMES[:1], grads)]

    def mid_w_in(self, after):
        return self._reduce_mid("w_in", after)


def kernel(x, norm_attn_g, w_in, b_forget, w_branch_a, w_branch_b, w_out, norm_mlp_g, w_up, w_down, norm_final_g, loss_target, m_norm_attn_g, m_w_in, m_b_forget, m_w_branch_a, m_w_branch_b, m_w_out, m_norm_mlp_g, m_w_up, m_w_down, m_norm_final_g, v_norm_attn_g, v_w_in, v_b_forget, v_w_branch_a, v_w_branch_b, v_w_out, v_norm_mlp_g, v_w_up, v_w_down, v_norm_final_g):
    cx, cy, cc = _place()
    ids = jnp.stack([cc, 2 * cx + cy]).astype(jnp.int32)

    w_sh = [w_in[0].T] + [t[0] for t in (w_branch_a, w_branch_b, w_out, w_up, w_down)]
    m_sh = [m_w_in[0].T] + [t[0] for t in (m_w_branch_a, m_w_branch_b, m_w_out, m_w_up, m_w_down)]
    v_sh = [v_w_in[0].T] + [t[0] for t in (v_w_branch_a, v_w_branch_b, v_w_out, v_w_up, v_w_down)]

    (g_in,) = _all_gather(w_sh[:1])
    w_in_t = _pad_w_in_t(g_in.reshape(D_IN, D))
    comm = _StepComm(ids, w_sh[1:], g_in)
    b_pad = jnp.pad(b_forget, ((0, 0), (0, LANES - FOX_H)))

    loss_row, dx, dsmall = _local_step(
        x[0], loss_target[0], norm_attn_g, norm_mlp_g, norm_final_g.reshape(1, D), b_pad, w_in_t, comm)
    loss = lax.psum(loss_row[0, 0], ("x", "y", "c"))

    token = comm.mid_w_in(dx)
    reduced = comm.reduced("mixer", token) + comm.reduced("mlp", token)
    big = [_reduce_adamw(*reduced[i], w_sh[1 + i], m_sh[1 + i], v_sh[1 + i], "adamw_" + W_NAMES[1 + i], deps=[token])
           for i in range(len(reduced))]
    dg_attn, db, dg_mlp, dg_final = dsmall
    small = _small_allreduce_adamw(
        _pack_small(dg_attn, db, dg_mlp, dg_final),
        _pack_small(norm_attn_g, b_forget, norm_mlp_g, norm_final_g.reshape(1, D)),
        _pack_small(m_norm_attn_g, m_b_forget, m_norm_mlp_g, m_norm_final_g.reshape(1, D)),
        _pack_small(v_norm_attn_g, v_b_forget, v_norm_mlp_g, v_norm_final_g.reshape(1, D)), deps=[token])
    (reduced_in,) = comm.reduced("w_in", small[0])
    big = [_reduce_adamw(*reduced_in, w_sh[0], m_sh[0], v_sh[0], "adamw_" + W_NAMES[0])] + big

    outs = [loss, dx[None]]
    for q in range(4):
        s_attn, s_b, s_mlp, s_final = _unpack_small(small[q])
        b_in, b_a, b_b, b_out, b_up, b_down = [t[q][None] for t in big]
        b_in = jnp.swapaxes(b_in, 1, 2)
        outs += [s_attn, b_in, s_b, b_a, b_b, b_out, s_mlp, b_up, b_down, s_final]
    return tuple(outs)
```

```python
import functools

import jax
import jax.numpy as jnp
from jax import lax
from jax.experimental import pallas as pl
from jax.experimental.pallas import tpu as pltpu

F32 = jnp.float32
BF16 = jnp.bfloat16
MESH = pl.DeviceIdType.MESH

S = 2048
D = 1024
HD = 64
FOX_H = 8
FOX_W = FOX_H * HD
DIL_HG = 4
DIL_G = 3
DIL_W = DIL_G * DIL_HG * HD
DIL_OUT = DIL_HG * HD
DIL_BLK = 128
DIL_R = (1, 4, 16)
DFF = 4 * D
D_IN = 3 * FOX_W + FOX_H + 3 * DIL_W + 2 * D
EPS = 1e-6
NEG_INF = -1e30
SCALE = HD ** -0.5
ROPE_THETA = 500000.0
ROPE_DIM = HD // 4
N_DEV = 8

ADAM_LR = 0.001
ADAM_B1 = 0.9
ADAM_B2 = 0.999
ADAM_EPS = 1e-08
ADAM_WD = 0.01
ADAM_STEP = 10

LANES = 128
CB = 256
SEG_A = 0
SEG_B = 3 * FOX_W
SEG_Z = SEG_B + 3 * DIL_W
SEG_G = SEG_Z + CB
SEG_F = SEG_G + 2 * D
D_PAD = SEG_F + CB
SMALL_R = 32

VMEM_MB = 56


def _cparams(dims=None, vmem_mb=VMEM_MB, **kw):
    return pltpu.CompilerParams(dimension_semantics=dims, vmem_limit_bytes=vmem_mb << 20, **kw)


ANY = pl.BlockSpec(memory_space=pl.ANY)

_NN = (((1,), (0,)), ((), ()))
_NT = (((1,), (1,)), ((), ()))
_TN = (((0,), (0,)), ((), ()))


def _dot(a, b, dims):
    return lax.dot_general(a.astype(BF16), b.astype(BF16), dims, preferred_element_type=F32)


def _mm(a, b, *, mode, tm, tn, out_dtypes, name, n=None, k=None, a_off=0, b_off=0,
        b_sharded=False, out_sharded=False, epilogue=None, extras=(), deps=()):
    n_sh = b.shape[-1] if b_sharded else None
    if mode == "nn":
        m = a.shape[0]
        k = k or a.shape[1]
        a_spec = pl.BlockSpec((tm, k), lambda i, j: (i, a_off))
        if b_sharded:
            assert tn == n_sh
            n = N_DEV * n_sh
            b_spec = pl.BlockSpec((None, k, tn), lambda i, j: (j, 0, 0))
        else:
            n = n or b.shape[1]
            b_spec = pl.BlockSpec((k, tn), lambda i, j: (0, j + b_off))
        dims = _NN
    elif mode == "nt":
        m = a.shape[0]
        k = k or a.shape[1]
        a_spec = pl.BlockSpec((tm, k), lambda i, j: (i, a_off))
        if b_sharded:
            n = b.shape[1]
            b_spec = pl.BlockSpec((N_DEV, tn, n_sh), lambda i, j: (0, j, 0))
        else:
            n = n or b.shape[0]
            b_spec = pl.BlockSpec((tn, k), lambda i, j: (j + b_off, 0))
        dims = _NT
    else:
        k, m = a.shape
        n = n or b.shape[1]
        a_spec = pl.BlockSpec((k, tm), lambda i, j: (0, i))
        b_spec = pl.BlockSpec((k, tn), lambda i, j: (0, j + b_off))
        dims = _TN
    assert m % tm == 0 and n % tn == 0, (name, m, n, tm, tn)
    n_extra = len(extras)
    tile = pl.BlockSpec((tm, tn), lambda i, j: (i, j))
    if out_sharded:
        assert mode == "tn" and n // tn == N_DEV
        out_spec = pl.BlockSpec((None, tm, tn), lambda i, j: (j, i, 0))
        out_shape = (N_DEV, m, tn)
    else:
        out_spec, out_shape = tile, (m, n)

    def body(a_ref, b_ref, *refs):
        if mode == "nt" and b_sharded:
            acc = _dot(a_ref[:, 0:n_sh], b_ref[0], dims)
            for p in range(1, N_DEV):
                acc = acc + _dot(a_ref[:, p * n_sh:(p + 1) * n_sh], b_ref[p], dims)
        else:
            acc = _dot(a_ref[...], b_ref[...], dims)
        ex = [r[...] for r in refs[:n_extra]]
        outs = epilogue(acc, *ex) if epilogue is not None else (acc,)
        for o_ref, o in zip(refs[n_extra + len(deps):], outs):
            o_ref[...] = o.astype(o_ref.dtype)

    res = pl.pallas_call(
        body, name=name, grid=(m // tm, n // tn),
        in_specs=[a_spec, b_spec] + [tile] * n_extra + [ANY] * len(deps),
        out_specs=[out_spec] * len(out_dtypes),
        out_shape=[jax.ShapeDtypeStruct(out_shape, dt) for dt in out_dtypes],
        compiler_params=_cparams(("parallel", "parallel")),
    )(a, b, *extras, *deps)
    return res if len(out_dtypes) > 1 else res[0]


ROW_T = 256


def _rms_fwd(x, g, name, deps=()):
    def body(x_ref, g_ref, *refs):
        o_ref = refs[-1]
        x = x_ref[...]
        r = lax.rsqrt(jnp.mean(x * x, axis=-1, keepdims=True) + EPS)
        o_ref[...] = ((x * r) * g_ref[...]).astype(BF16)

    return pl.pallas_call(
        body, name=name, grid=(S // ROW_T,),
        in_specs=[pl.BlockSpec((ROW_T, D), lambda i: (i, 0)), pl.BlockSpec((1, D), lambda i: (0, 0))] + [ANY] * len(deps),
        out_specs=pl.BlockSpec((ROW_T, D), lambda i: (i, 0)),
        out_shape=jax.ShapeDtypeStruct((S, D), BF16),
        compiler_params=_cparams(("parallel",)),
    )(x, g, *deps)


def _rms_bwd(x, g, dh, dres, name):
    def body(x_ref, g_ref, dh_ref, dres_ref, dx_ref, dg_ref):
        @pl.when(pl.program_id(0) == 0)
        def _():
            dg_ref[...] = jnp.zeros_like(dg_ref)

        x = x_ref[...]
        dh = dh_ref[...]
        r = lax.rsqrt(jnp.mean(x * x, axis=-1, keepdims=True) + EPS)
        xn = x * r
        dhn = dh * g_ref[...]
        dx_ref[...] = dres_ref[...] + r * (dhn - xn * jnp.mean(dhn * xn, axis=-1, keepdims=True))
        dg_ref[...] += jnp.sum(dh * xn, axis=0, keepdims=True)

    row = pl.BlockSpec((ROW_T, D), lambda i: (i, 0))
    vec = pl.BlockSpec((1, D), lambda i: (0, 0))
    return pl.pallas_call(
        body, name=name, grid=(S // ROW_T,),
        in_specs=[row, vec, row, row], out_specs=[row, vec],
        out_shape=[jax.ShapeDtypeStruct((S, D), F32), jax.ShapeDtypeStruct((1, D), F32)],
        compiler_params=_cparams(("arbitrary",)),
    )(x, g, dh, dres)


def _final_norm_loss(x, g, tgt):
    def body(x_ref, g_ref, t_ref, dx_ref, dg_ref, loss_ref):
        @pl.when(pl.program_id(0) == 0)
        def _():
            dg_ref[...] = jnp.zeros_like(dg_ref)
            loss_ref[...] = jnp.zeros_like(loss_ref)

        x = x_ref[...]
        g = g_ref[...]
        r = lax.rsqrt(jnp.mean(x * x, axis=-1, keepdims=True) + EPS)
        xn = x * r
        err = xn * g - t_ref[...]
        row_loss = jnp.mean(err * err, axis=-1, keepdims=True)
        loss_ref[...] += 0.5 * jnp.sum(row_loss, axis=0, keepdims=True) * jnp.ones((1, LANES), F32)
        dy = err * (1.0 / D)
        dyn = dy * g
        dx_ref[...] = r * (dyn - xn * jnp.mean(dyn * xn, axis=-1, keepdims=True))
        dg_ref[...] += jnp.sum(dy * xn, axis=0, keepdims=True)

    row = pl.BlockSpec((ROW_T, D), lambda i: (i, 0))
    vec = pl.BlockSpec((1, D), lambda i: (0, 0))
    return pl.pallas_call(
        body, name="final_norm_loss", grid=(S // ROW_T,),
        in_specs=[row, vec, row],
        out_specs=[row, vec, pl.BlockSpec((1, LANES), lambda i: (0, 0))],
        out_shape=[jax.ShapeDtypeStruct((S, D), F32), jax.ShapeDtypeStruct((1, D), F32),
                   jax.ShapeDtypeStruct((1, LANES), F32)],
        compiler_params=_cparams(("arbitrary",)),
    )(x, g, tgt)


def _sigmoid(z):
    return 1.0 / (1.0 + jnp.exp(-z))


def _gate_fwd(proj_g, ya, yb):
    def body(ga_ref, gb_ref, ya_ref, yb_ref, o_ref):
        o_ref[...] = (_sigmoid(ga_ref[...]) * ya_ref[...] + _sigmoid(gb_ref[...]) * yb_ref[...]).astype(BF16)

    row = pl.BlockSpec((ROW_T, D), lambda i: (i, 0))
    return pl.pallas_call(
        body, name="gate_fwd", grid=(S // ROW_T,),
        in_specs=[row, pl.BlockSpec((ROW_T, D), lambda i: (i, 1)), row, row],
        out_specs=row, out_shape=jax.ShapeDtypeStruct((S, D), BF16),
        compiler_params=_cparams(("parallel",)),
    )(proj_g, proj_g, ya, yb)


GATE_TR = 1024
GATE_TC = 512
GATE_NB = 2 * D // GATE_TC


def _gate_bwd(dmixed, proj_g, ya, yb):
    half = GATE_NB // 2

    def body(dm_ref, g_ref, ya_ref, yb_ref, dy_ref, dg_ref):
        dm = dm_ref[...]
        s = _sigmoid(g_ref[...])
        y = jnp.where(pl.program_id(1) < half, ya_ref[...], yb_ref[...])
        dy_ref[...] = (dm * s).astype(BF16)
        dg_ref[...] = (dm * y * (s * (1.0 - s))).astype(BF16)

    blk = pl.BlockSpec((GATE_TR, GATE_TC), lambda i, j: (i, j))
    wrapped = pl.BlockSpec((GATE_TR, GATE_TC), lambda i, j: (i, j % half))
    return pl.pallas_call(
        body, name="gate_bwd", grid=(S // GATE_TR, GATE_NB),
        in_specs=[wrapped, blk, wrapped, wrapped],
        out_specs=[blk, pl.BlockSpec((GATE_TR, GATE_TC), lambda i, j: (i, j + SEG_G // GATE_TC))],
        out_shape=[jax.ShapeDtypeStruct((S, 2 * D), BF16), jax.ShapeDtypeStruct((S, D_PAD), BF16)],
        compiler_params=_cparams(("parallel", "parallel")),
    )(dmixed, proj_g, ya, yb)


FOX_TQ = 256


def _scan_rows(x, reverse):
    n = x.shape[0]
    row = lax.broadcasted_iota(jnp.int32, x.shape, 0)
    k = 1
    while k < n:
        if reverse:
            x = x + jnp.where(row < n - k, pltpu.roll(x, n - k, 0), 0.0)
        else:
            x = x + jnp.where(row >= k, pltpu.roll(x, k, 0), 0.0)
        k *= 2
    return x


def _fox_scan(proj_f, b_pad):
    def body(f_ref, b_ref, fc_ref, fr_ref):
        z = f_ref[...] + b_ref[...]
        lf = jnp.minimum(z, 0.0) - jnp.log1p(jnp.exp(-jnp.abs(z)))
        f = _scan_rows(lf, reverse=False)
        fc_ref[...] = f
        fr_ref[...] = f.T

    return pl.pallas_call(
        body, name="fox_scan", grid=(1,),
        in_specs=[pl.BlockSpec((S, LANES), lambda i: (0, 0)), pl.BlockSpec((1, LANES), lambda i: (0, 0))],
        out_specs=[pl.BlockSpec((S, LANES), lambda i: (0, 0)), pl.BlockSpec((LANES, S), lambda i: (0, 0))],
        out_shape=[jax.ShapeDtypeStruct((S, LANES), F32), jax.ShapeDtypeStruct((LANES, S), F32)],
        compiler_params=_cparams(("arbitrary",)),
    )(proj_f, b_pad)


def _fox_dscan(dft, dfs, proj_f, b_pad, dproj):
    def body(dft_ref, dfs_ref, f_ref, b_ref, _, dfa_ref, db_ref):
        dfs_pad = jnp.concatenate([dfs_ref[...], jnp.zeros((LANES - FOX_H, S), F32)], axis=0)
        df = dfs_pad.T + dft_ref[0]
        for hp in range(1, dft_ref.shape[0]):
            df = df + pltpu.roll(dft_ref[hp], 2 * hp, 1)
        dlf = _scan_rows(df, reverse=True)
        z = f_ref[...] + b_ref[...]
        lane = lax.broadcasted_iota(jnp.int32, (S, LANES), 1)
        dfa = jnp.where(lane < FOX_H, dlf / (1.0 + jnp.exp(z)), 0.0)
        dfa_ref[:, :LANES] = dfa.astype(BF16)
        dfa_ref[:, LANES:] = jnp.zeros((S, CB - LANES), BF16)
        db_ref[...] = jnp.sum(dfa, axis=0, keepdims=True)

    return pl.pallas_call(
        body, name="fox_dscan", grid=(1,),
        in_specs=[pl.BlockSpec(dft.shape, lambda i: (0, 0, 0)), pl.BlockSpec((FOX_H, S), lambda i: (0, 0)),
                  pl.BlockSpec((S, LANES), lambda i: (0, 0)), pl.BlockSpec((1, LANES), lambda i: (0, 0)), ANY],
        out_specs=[pl.BlockSpec((S, CB), lambda i: (0, SEG_F // CB)), pl.BlockSpec((1, LANES), lambda i: (0, 0))],
        out_shape=[jax.ShapeDtypeStruct((S, D_PAD), BF16), jax.ShapeDtypeStruct((1, LANES), F32)],
        input_output_aliases={4: 0},
        compiler_params=_cparams(("arbitrary",)),
    )(dft, dfs, proj_f, b_pad, dproj)


FOX_NQ = S // FOX_TQ
FOX_HP = FOX_W // LANES
HEADS_PER_LB = LANES // HD


def _fox_scores(q, k, ft_all, fs_all, head, w):
    lane = lax.broadcasted_iota(jnp.int32, ft_all.shape, 1)
    sub = lax.broadcasted_iota(jnp.int32, fs_all.shape, 0)
    ft = jnp.sum(jnp.where(lane == head, ft_all, 0.0), axis=-1, keepdims=True)
    fs = jnp.sum(jnp.where(sub == head, fs_all, 0.0), axis=0, keepdims=True)
    s = (_dot(q, k, _NT) * SCALE + ft) - fs
    row = lax.broadcasted_iota(jnp.int32, (FOX_TQ, FOX_TQ), 0)
    col = lax.broadcasted_iota(jnp.int32, (FOX_TQ, FOX_TQ), 1)
    diag = jnp.where(col <= row, s[:, w * FOX_TQ:], NEG_INF)
    return diag if w == 0 else jnp.concatenate([s[:, :w * FOX_TQ], diag], axis=1)


def _fox_fwd(proj_a, fcol, frow):
    def body(q_ref, k_ref, v_ref, ft_ref, fs_ref, o_ref, lse_ref):
        hp = pl.program_id(0)
        qi = pl.program_id(1)
        lane = lax.broadcasted_iota(jnp.int32, (FOX_TQ, LANES), 1)
        for w in range(FOX_NQ):
            @pl.when(qi == w)
            def _(w=w):
                width = (w + 1) * FOX_TQ
                ft_all = ft_ref[...]
                fs_all = fs_ref[:, :width]
                lse = jnp.zeros((FOX_TQ, LANES), F32)
                for hh in range(HEADS_PER_LB):
                    sl = slice(hh * HD, (hh + 1) * HD)
                    s = _fox_scores(q_ref[:, sl], k_ref[:width, sl], ft_all, fs_all, hp * HEADS_PER_LB + hh, w)
                    m = jnp.max(s, axis=-1, keepdims=True)
                    p = jnp.exp(s - m)
                    l = jnp.sum(p, axis=-1, keepdims=True)
                    o_ref[:, sl] = _dot(p / l, v_ref[:width, sl], _NN)
                    lse = jnp.where(lane == hh, m + jnp.log(l), lse)
                lse_ref[...] = lse

    return pl.pallas_call(
        body, name="fox_fwd", grid=(FOX_HP, FOX_NQ),
        in_specs=[pl.BlockSpec((FOX_TQ, LANES), lambda h, i: (i, h)),
                  pl.BlockSpec((S, LANES), lambda h, i: (0, FOX_HP + h)),
                  pl.BlockSpec((S, LANES), lambda h, i: (0, 2 * FOX_HP + h)),
                  pl.BlockSpec((FOX_TQ, LANES), lambda h, i: (i, 0)),
                  pl.BlockSpec((FOX_H, S), lambda h, i: (0, 0))],
        out_specs=[pl.BlockSpec((FOX_TQ, LANES), lambda h, i: (i, h)),
                   pl.BlockSpec((None, FOX_TQ, LANES), lambda h, i: (h, i, 0))],
        out_shape=[jax.ShapeDtypeStruct((S, FOX_W), F32), jax.ShapeDtypeStruct((FOX_HP, S, LANES), F32)],
        compiler_params=_cparams(("parallel", "parallel")),
    )(proj_a, proj_a, proj_a, fcol, frow)


def _fox_bwd(proj_a, fcol, frow, lse, do, dproj, deps=()):
    n_q = FOX_NQ

    def body(q_ref, k_ref, v_ref, ft_ref, fs_ref, lse_ref, do_ref, *refs):
        dqkv_ref, dft_ref, dfs_ref, dk_acc, dv_acc = refs[1 + len(deps):]
        hp = pl.program_id(0)
        t = pl.program_id(1)

        @pl.when(t == 0)
        def _():
            dk_acc[...] = jnp.zeros_like(dk_acc)
            dv_acc[...] = jnp.zeros_like(dv_acc)

        @pl.when((t == 0) & (hp == 0))
        def _():
            dfs_ref[...] = jnp.zeros_like(dfs_ref)

        lane = lax.broadcasted_iota(jnp.int32, (FOX_TQ, LANES), 1)
        for w in range(n_q):
            @pl.when(t == w)
            def _(w=w):
                width = (w + 1) * FOX_TQ
                ft_all = ft_ref[...]
                fs_all = fs_ref[:, :width]
                lse_all = lse_ref[...]
                sub = lax.broadcasted_iota(jnp.int32, (FOX_H, width), 0)
                dft = jnp.zeros((FOX_TQ, LANES), F32)
                for hh in range(HEADS_PER_LB):
                    sl = slice(hh * HD, (hh + 1) * HD)
                    head = hp * HEADS_PER_LB + hh
                    q = q_ref[:, sl]
                    k = k_ref[:width, sl]
                    do_h = do_ref[:, sl]
                    s = _fox_scores(q, k, ft_all, fs_all, head, w)
                    p = jnp.exp(s - lse_all[:, hh:hh + 1])
                    dp = _dot(do_h, v_ref[:width, sl], _NT)
                    ds = p * (dp - jnp.sum(dp * p, axis=-1, keepdims=True))
                    dft = jnp.where(lane == hh, jnp.sum(ds, axis=-1, keepdims=True), dft)
                    dfs_ref[:, :width] -= jnp.where(sub == head, jnp.sum(ds, axis=0, keepdims=True), 0.0)
                    dsb = (ds * SCALE).astype(BF16)
                    dqkv_ref[:, sl] = _dot(dsb, k, _NN).astype(BF16)
                    dk_acc[:width, sl] += _dot(dsb, q, _TN)
                    dv_acc[:width, sl] += _dot(p, do_h, _TN)
                dft_ref[...] = dft

        rows = pl.ds(pl.multiple_of((t % n_q) * FOX_TQ, FOX_TQ), FOX_TQ)

        @pl.when((t >= n_q) & (t < 2 * n_q))
        def _():
            dqkv_ref[...] = dk_acc[rows, :].astype(BF16)

        @pl.when(t >= 2 * n_q)
        def _():
            dqkv_ref[...] = dv_acc[rows, :].astype(BF16)

    def qidx(t):
        return jnp.minimum(t, n_q - 1)

    qblk = pl.BlockSpec((FOX_TQ, LANES), lambda h, t: (qidx(t), h))
    lane_blk = pl.BlockSpec((None, FOX_TQ, LANES), lambda h, t: (h, qidx(t), 0))
    return pl.pallas_call(
        body, name="fox_bwd", grid=(FOX_HP, 3 * n_q),
        in_specs=[qblk, pl.BlockSpec((S, LANES), lambda h, t: (0, FOX_HP + h)),
                  pl.BlockSpec((S, LANES), lambda h, t: (0, 2 * FOX_HP + h)),
                  pl.BlockSpec((FOX_TQ, LANES), lambda h, t: (qidx(t), 0)),
                  pl.BlockSpec((FOX_H, S), lambda h, t: (0, 0)), lane_blk, qblk, ANY] + [ANY] * len(deps),
        out_specs=[pl.BlockSpec((FOX_TQ, LANES), lambda h, t: (t % n_q, (t // n_q) * FOX_HP + h)), lane_blk,
                   pl.BlockSpec((FOX_H, S), lambda h, t: (0, 0))],
        out_shape=[jax.ShapeDtypeStruct((S, D_PAD), BF16),
                   jax.ShapeDtypeStruct((FOX_HP, S, LANES), F32), jax.ShapeDtypeStruct((FOX_H, S), F32)],
        scratch_shapes=[pltpu.VMEM((S, LANES), F32), pltpu.VMEM((S, LANES), F32)],
        input_output_aliases={7: 0},
        compiler_params=_cparams(("arbitrary", "arbitrary")),
    )(proj_a, proj_a, proj_a, fcol, frow, lse, do, dproj, *deps)


def _rope_tables():
    half = ROPE_DIM // 2
    inv_freq = jnp.power(jnp.float32(ROPE_THETA), -jnp.arange(half, dtype=F32) * 2.0 / ROPE_DIM)
    ang = jnp.arange(S).astype(F32)[:, None] * inv_freq[None, :]
    cos, sin = jnp.cos(ang), jnp.sin(ang)
    z = jnp.zeros((S, half), F32)
    rest = HD - ROPE_DIM
    t_self = jnp.concatenate([cos, cos, jnp.ones((S, rest), F32)], axis=1)
    t_up = jnp.concatenate([-sin, z, jnp.zeros((S, rest), F32)], axis=1)
    t_dn = jnp.concatenate([z, sin, jnp.zeros((S, rest), F32)], axis=1)
    rep = LANES // HD

    def by_residue(t):
        return jnp.stack([t.reshape(S // r, r, LANES).transpose(1, 0, 2).reshape(S, LANES) for r in DIL_R])

    return tuple(by_residue(jnp.tile(t, (1, rep))) for t in (t_self, t_up, t_dn))


def _residue_pieces(r):
    if r == 1:
        return [(slice(i, i + 512), slice(i, i + 512)) for i in range(0, S, 512)]
    n = S // r
    return [(pl.ds(j, n, stride=r), slice(j * n, (j + 1) * n)) for j in range(r)]


def _rope_apply(x, a, u, d, backward):
    half = ROPE_DIM // 2
    if backward:
        return x * a + pltpu.roll(x * u, half, 1) + pltpu.roll(x * d, LANES - half, 1)
    return x * a + pltpu.roll(x, LANES - half, 1) * u + pltpu.roll(x, half, 1) * d


LB_PER_CB = CB // LANES
ROPE_NB = 3 * DIL_G * LB_PER_CB


def _rope_split(proj_b, tables):
    def body(x_ref, a_ref, u_ref, d_ref, o_ref):
        j = pl.program_id(0)
        g = (j // LB_PER_CB) % DIL_G
        rotate = j < 2 * DIL_G * LB_PER_CB
        for gs in range(DIL_G):
            @pl.when(g == gs)
            def _(gs=gs):
                for tok, sub in _residue_pieces(DIL_R[gs]):
                    x = x_ref[tok, :]
                    y = _rope_apply(x, a_ref[sub, :], u_ref[sub, :], d_ref[sub, :], False)
                    o_ref[sub, :] = jnp.where(rotate, y, x).astype(BF16)

    tab = pl.BlockSpec((None, S, LANES), lambda j: ((j // LB_PER_CB) % DIL_G, 0, 0))
    return pl.pallas_call(
        body, name="rope_split", grid=(ROPE_NB,),
        in_specs=[pl.BlockSpec((S, LANES), lambda j: (0, j)), tab, tab, tab],
        out_specs=pl.BlockSpec((None, S, LANES), lambda j: (j // LB_PER_CB, 0, j % LB_PER_CB)),
        out_shape=jax.ShapeDtypeStruct((3 * DIL_G, S, CB), BF16),
        compiler_params=_cparams(("parallel",)),
    )(proj_b, *tables)


def _rope_merge_bwd(dq3, dk3, dv3, tables, dproj):
    per_tensor = DIL_G * LB_PER_CB

    def body(dq_ref, dk_ref, dv_ref, a_ref, u_ref, d_ref, _, o_ref, tmp):
        j = pl.program_id(0)
        t = j // per_tensor
        g = (j // LB_PER_CB) % DIL_G
        for gs in range(DIL_G):
            @pl.when((g == gs) & (j < ROPE_NB))
            def _(gs=gs):
                for tok, sub in _residue_pieces(DIL_R[gs]):
                    x = jnp.where(t == 0, dq_ref[sub, :], jnp.where(t == 1, dk_ref[sub, :], dv_ref[sub, :]))
                    y = _rope_apply(x, a_ref[sub, :], u_ref[sub, :], d_ref[sub, :], True)
                    tmp[tok, :] = jnp.where(t < 2, y, x)
                o_ref[...] = tmp[...].astype(BF16)

        @pl.when(j >= ROPE_NB)
        def _():
            o_ref[...] = jnp.zeros_like(o_ref)

    def src_spec(t):
        def index(j):
            jb = j // LB_PER_CB - t * DIL_G
            lane_blk = jnp.where(jb < 0, 0, jnp.where(jb >= DIL_G, LB_PER_CB - 1, j % LB_PER_CB))
            return jnp.clip(jb, 0, DIL_G - 1), 0, lane_blk
        return pl.BlockSpec((None, S, LANES), index)

    tab = pl.BlockSpec((None, S, LANES), lambda j: ((j // LB_PER_CB) % DIL_G, 0, 0))
    return pl.pallas_call(
        body, name="rope_merge_bwd", grid=(ROPE_NB + LB_PER_CB,),
        in_specs=[src_spec(0), src_spec(1), src_spec(2), tab, tab, tab, ANY],
        out_specs=pl.BlockSpec((S, LANES), lambda j: (0, j + SEG_B // LANES)),
        out_shape=jax.ShapeDtypeStruct((S, D_PAD), BF16),
        scratch_shapes=[pltpu.VMEM((S, LANES), F32)],
        input_output_aliases={6: 0},
        compiler_params=_cparams(("arbitrary",)),
    )(dq3, dk3, dv3, *tables, dproj)


DIL_NB = S // DIL_BLK


_BQK = (((2,), (2,)), ((0,), (0,)))
_BQD = (((2,), (1,)), ((0,), (0,)))
_BKD = (((1,), (1,)), ((0,), (0,)))


def _dil_mask(g):
    shape = (DIL_NB, DIL_BLK, 2 * DIL_BLK)
    blocks_per_seq = jnp.right_shift(DIL_NB, 2 * g)
    has_prev = jnp.bitwise_and(lax.broadcasted_iota(jnp.int32, shape, 0), blocks_per_seq - 1) != 0
    a = lax.broadcasted_iota(jnp.int32, shape, 1)
    kk = lax.broadcasted_iota(jnp.int32, shape, 2)
    diff = DIL_BLK + a - kk
    return (diff >= 0) & (diff <= DIL_BLK) & ((kk >= DIL_BLK) | has_prev)


def _blocks(t):
    return t.reshape(DIL_NB, DIL_BLK, t.shape[-1])


def _with_prev(t):
    prev = jnp.concatenate([jnp.zeros((DIL_BLK, t.shape[-1]), t.dtype), t[:-DIL_BLK]], axis=0)
    return jnp.concatenate([_blocks(prev), _blocks(t)], axis=1)


def _fold_prev(t2):
    n = t2.shape[-1]
    to_prev = t2[:, :DIL_BLK].reshape(S, n)
    own = t2[:, DIL_BLK:].reshape(S, n)
    return own + jnp.concatenate([to_prev[DIL_BLK:], jnp.zeros((DIL_BLK, n), t2.dtype)], axis=0)


def _dil_group_spec(t, width=DIL_OUT):
    return pl.BlockSpec((None, S, width), lambda g: (t * DIL_G + g, 0, 0))


def _dil_fwd(qkv3):
    def body(q_ref, k_ref, v_ref, o_ref, lse_ref):
        ok = _dil_mask(pl.program_id(0))
        lane = lax.broadcasted_iota(jnp.int32, (S, LANES), 1)
        lse = jnp.zeros((S, LANES), F32)
        for h in range(DIL_HG):
            sl = slice(h * HD, (h + 1) * HD)
            s = jnp.where(ok, _dot(_blocks(q_ref[:, sl]), _with_prev(k_ref[:, sl]), _BQK) * SCALE, NEG_INF)
            m = jnp.max(s, axis=-1, keepdims=True)
            p = jnp.exp(s - m)
            l = jnp.sum(p, axis=-1, keepdims=True)
            o_ref[:, sl] = _dot(p / l, _with_prev(v_ref[:, sl]), _BQD).reshape(S, HD)
            lse = jnp.where(lane == h, (m + jnp.log(l)).reshape(S, 1), lse)
        lse_ref[...] = lse

    return pl.pallas_call(
        body, name="dil_fwd", grid=(DIL_G,),
        in_specs=[_dil_group_spec(0), _dil_group_spec(1), _dil_group_spec(2)],
        out_specs=[_dil_group_spec(0), _dil_group_spec(0, LANES)],
        out_shape=[jax.ShapeDtypeStruct((DIL_G, S, DIL_OUT), F32), jax.ShapeDtypeStruct((DIL_G, S, LANES), F32)],
        compiler_params=_cparams(("parallel",)),
    )(qkv3, qkv3, qkv3)


def _dil_bwd(qkv3, lse3, do3, c3):
    def body(q_ref, k_ref, v_ref, lse_ref, do_ref, c_ref, dq_ref, dk_ref, dv_ref):
        ok = _dil_mask(pl.program_id(0))
        lse = lse_ref[...]
        c = c_ref[...]
        for h in range(DIL_HG):
            sl = slice(h * HD, (h + 1) * HD)
            q = _blocks(q_ref[:, sl])
            k2 = _with_prev(k_ref[:, sl])
            do_h = _blocks(do_ref[:, sl])
            s = jnp.where(ok, _dot(q, k2, _BQK) * SCALE, NEG_INF)
            p = jnp.exp(s - _blocks(lse[:, h:h + 1]))
            dp = _dot(do_h, _with_prev(v_ref[:, sl]), _BQK)
            ds = p * (dp - _blocks(c[:, h:h + 1]))
            dsb = (ds * SCALE).astype(BF16)
            dq_ref[:, sl] = _dot(dsb, k2, _BQD).reshape(S, HD)
            dk_ref[:, sl] = _fold_prev(_dot(dsb, q, _BKD))
            dv_ref[:, sl] = _fold_prev(_dot(p, do_h, _BKD))

    return pl.pallas_call(
        body, name="dil_bwd", grid=(DIL_G,),
        in_specs=[_dil_group_spec(0), _dil_group_spec(1), _dil_group_spec(2), _dil_group_spec(0, LANES),
                  _dil_group_spec(0), _dil_group_spec(0, LANES)],
        out_specs=[_dil_group_spec(0)] * 3,
        out_shape=[jax.ShapeDtypeStruct((DIL_G, S, DIL_OUT), F32)] * 3,
        compiler_params=_cparams(("parallel",)),
    )(qkv3, qkv3, qkv3, lse3, do3, c3)


COMB_T = 256


def _dil_combine(o3, lse3):
    heads_per_lb = LANES // HD

    def body(o_ref, lse_ref, ob_ref, al_ref, *scratch):
        o_tok = [scratch[LB_PER_CB * gg:LB_PER_CB * (gg + 1)] for gg in range(DIL_G)]
        lse_tok = scratch[LB_PER_CB * DIL_G:]
        g = pl.program_id(0)
        for gs in range(DIL_G):
            @pl.when(g == gs)
            def _(gs=gs):
                for tok, sub in _residue_pieces(DIL_R[gs]):
                    for hf in range(LB_PER_CB):
                        o_tok[gs][hf][tok, :] = o_ref[sub, hf * LANES:(hf + 1) * LANES]
                    lse_tok[gs][tok, :] = lse_ref[sub, :]

        @pl.when(g == DIL_G - 1)
        def _():
            def chunk(i, carry):
                rows = pl.ds(pl.multiple_of(i * COMB_T, COMB_T), COMB_T)
                lse = [lse_tok[gg][rows, :] for gg in range(DIL_G)]
                m = jnp.maximum(jnp.maximum(lse[0], lse[1]), lse[2])
                e = [jnp.exp(lse[gg] - m) for gg in range(DIL_G)]
                den = (e[0] + e[1]) + e[2]
                al = [e[gg] / den for gg in range(DIL_G)]
                for gg in range(DIL_G):
                    al_ref[gg, rows, :] = al[gg]
                for h in range(DIL_HG):
                    hf, sl = h // heads_per_lb, slice((h % heads_per_lb) * HD, (h % heads_per_lb + 1) * HD)
                    acc = al[0][:, h:h + 1] * o_tok[0][hf][rows, sl]
                    for gg in range(1, DIL_G):
                        acc = acc + al[gg][:, h:h + 1] * o_tok[gg][hf][rows, sl]
                    ob_ref[rows, h * HD:(h + 1) * HD] = acc
                return carry

            lax.fori_loop(0, S // COMB_T, chunk, 0)

    return pl.pallas_call(
        body, name="dil_combine", grid=(DIL_G,),
        in_specs=[pl.BlockSpec((None, S, DIL_OUT), lambda g: (g, 0, 0)),
                  pl.BlockSpec((None, S, LANES), lambda g: (g, 0, 0))],
        out_specs=[pl.BlockSpec((S, DIL_OUT), lambda g: (0, 0)),
                   pl.BlockSpec((DIL_G, S, LANES), lambda g: (0, 0, 0))],
        out_shape=[jax.ShapeDtypeStruct((S, DIL_OUT), F32), jax.ShapeDtypeStruct((DIL_G, S, LANES), F32)],
        scratch_shapes=[pltpu.VMEM((S, LANES), F32)] * (DIL_G * (LB_PER_CB + 1)),
        compiler_params=_cparams(("arbitrary",)),
    )(o3, lse3)


def _dil_combine_bwd(dob, ob, alpha, deps=()):
    heads_per_lb = LANES // HD

    def body(dob_ref, ob_ref, al_ref, *refs):
        do_ref, c_ref = refs[len(deps):]
        g = pl.program_id(0)
        hf = pl.program_id(1)
        for gs in range(DIL_G):
            @pl.when(g == gs)
            def _(gs=gs):
                for tok, sub in _residue_pieces(DIL_R[gs]):
                    dob = dob_ref[tok, :]
                    prod = dob * ob_ref[tok, :]
                    al = al_ref[tok, :]
                    lane = lax.broadcasted_iota(jnp.int32, al.shape, 1)
                    c = jnp.where(hf == 0, 0.0, c_ref[sub, :])
                    for hh in range(heads_per_lb):
                        sl = slice(hh * HD, (hh + 1) * HD)
                        head = hf * heads_per_lb + hh
                        a = jnp.sum(jnp.where(lane == head, al, 0.0), axis=-1, keepdims=True)
                        do_ref[sub, sl] = (a * dob[:, sl]).astype(BF16)
                        c = jnp.where(lane == head, a * jnp.sum(prod[:, sl], axis=-1, keepdims=True), c)
                    c_ref[sub, :] = c

    half = pl.BlockSpec((S, LANES), lambda g, hf: (0, hf))
    return pl.pallas_call(
        body, name="dil_combine_bwd", grid=(DIL_G, LB_PER_CB),
        in_specs=[half, half, pl.BlockSpec((None, S, LANES), lambda g, hf: (g, 0, 0))] + [ANY] * len(deps),
        out_specs=[pl.BlockSpec((None, S, LANES), lambda g, hf: (g, 0, hf)),
                   pl.BlockSpec((None, S, LANES), lambda g, hf: (g, 0, 0))],
        out_shape=[jax.ShapeDtypeStruct((DIL_G, S, DIL_OUT), BF16), jax.ShapeDtypeStruct((DIL_G, S, LANES), F32)],
        compiler_params=_cparams(("parallel", "arbitrary")),
    )(dob, ob, alpha, *deps)


def _local_step(x, tgt, g_attn, g_mlp, g_final, b_pad, w_in_t, hooks):
    tables = _rope_tables()

    h1 = _rms_fwd(x, g_attn, "rms_attn_fwd", deps=hooks.first_deps())
    proj_a = _mm(h1, w_in_t, mode="nt", tm=S, tn=512, n=3 * FOX_W, out_dtypes=[BF16], name="proj_a")
    proj_b = _mm(h1, w_in_t, mode="nt", tm=S, tn=CB, n=3 * DIL_W, b_off=SEG_B // CB, out_dtypes=[F32], name="proj_b")
    proj_g = _mm(h1, w_in_t, mode="nt", tm=S, tn=CB, n=2 * D, b_off=SEG_G // CB, out_dtypes=[F32], name="proj_g")
    proj_f = _mm(h1, w_in_t, mode="nt", tm=S, tn=LANES, n=LANES, b_off=SEG_F // LANES, out_dtypes=[F32], name="proj_f")

    fcol, frow = _fox_scan(proj_f, b_pad)
    oa, lse_a = _fox_fwd(proj_a, fcol, frow)

    qkv3 = _rope_split(proj_b, tables)
    o3, lse3 = _dil_fwd(qkv3)
    ob, alpha = _dil_combine(o3, lse3)

    w_a, w_b, w_out = hooks.mixer_weights(ob)
    ya = _mm(oa, w_a, mode="nn", tm=S, tn=CB, out_dtypes=[F32], name="branch_a")
    yb = _mm(ob, w_b, mode="nn", tm=S, tn=CB, out_dtypes=[F32], name="branch_b")
    mixed = _gate_fwd(proj_g, ya, yb)
    x2 = _mm(mixed, w_out, mode="nn", tm=S, tn=CB, out_dtypes=[F32], name="out_proj",
             epilogue=lambda acc, res: (res + acc,), extras=(x,))

    h2 = _rms_fwd(x2, g_mlp, "rms_mlp_fwd")
    w_up_sh, w_down = hooks.mlp_weights(h2)

    def up_epilogue(acc):
        r = jnp.maximum(acc, 0.0)
        return acc, r * r

    u, act = _mm(h2, w_up_sh, mode="nn", tm=S, tn=DFF // N_DEV, b_sharded=True, out_dtypes=[F32, BF16],
                 name="mlp_up", epilogue=up_epilogue)
    x3 = _mm(act, w_down, mode="nn", tm=512, tn=512, out_dtypes=[F32], name="mlp_down",
             epilogue=lambda acc, res: (res + acc,), extras=(x2,))

    dx3, dg_final, loss = _final_norm_loss(x3, g_final, tgt)

    du = _mm(dx3, w_down, mode="nt", tm=S, tn=CB, out_dtypes=[BF16], name="mlp_down_bwd",
             epilogue=lambda acc, u_t: (acc * (2.0 * jnp.maximum(u_t, 0.0)),), extras=(u,))
    dw_down = _mm(act, dx3, mode="tn", tm=512, tn=512, out_dtypes=[F32], name="dw_down")
    dw_up_sh = _mm(h2, du, mode="tn", tm=D, tn=DFF // N_DEV, out_sharded=True, out_dtypes=[F32], name="dw_up")
    dh2 = _mm(du, w_up_sh, mode="nt", tm=512, tn=512, b_sharded=True, out_dtypes=[F32], name="mlp_up_bwd",
              deps=hooks.mlp_grads(dw_up_sh, dw_down))
    dx2, dg_mlp = _rms_bwd(x2, g_mlp, dh2, dx3, "rms_mlp_bwd")

    dmixed = _mm(dx2, w_out, mode="nt", tm=S, tn=CB, out_dtypes=[F32], name="out_proj_bwd")
    dw_out = _mm(mixed, dx2, mode="tn", tm=D, tn=CB, out_dtypes=[F32], name="dw_out")

    dyab, dproj = _gate_bwd(dmixed, proj_g, ya, yb)
    doa = _mm(dyab, w_a, mode="nt", tm=S, tn=CB, k=D, a_off=0, out_dtypes=[BF16], name="branch_a_bwd")
    dw_a = _mm(oa, dyab, mode="tn", tm=FOX_W, tn=CB, n=D, b_off=0, out_dtypes=[F32], name="dw_branch_a")
    dob = _mm(dyab, w_b, mode="nt", tm=S, tn=CB, k=D, a_off=1, out_dtypes=[F32], name="branch_b_bwd")
    dw_b = _mm(ob, dyab, mode="tn", tm=DIL_OUT, tn=CB, n=D, b_off=D // CB, out_dtypes=[F32], name="dw_branch_b")

    dproj, dft, dfs = _fox_bwd(proj_a, fcol, frow, lse_a, doa, dproj, deps=hooks.mixer_grads(dw_a, dw_b, dw_out))
    dproj, db = _fox_dscan(dft, dfs, proj_f, b_pad, dproj)

    do3, c3 = _dil_combine_bwd(dob, ob, alpha, deps=hooks.mid_backward(db))
    dq3, dk3, dv3 = _dil_bwd(qkv3, lse3, do3, c3)
    dproj = _rope_merge_bwd(dq3, dk3, dv3, tables, dproj)

    dw_in_t = _mm(dproj, h1, mode="tn", tm=CB, tn=D, out_dtypes=[F32], name="dw_in")
    dh1 = _mm(dproj, w_in_t, mode="nn", tm=512, tn=CB, out_dtypes=[F32], name="proj_bwd",
              deps=hooks.w_in_grad(dw_in_t))
    dx, dg_attn = _rms_bwd(x, g_attn, dh1, dx2, "rms_attn_bwd")

    return loss, dx, (dg_attn, db, dg_mlp, dg_final)


SHARD_SHAPES = ((D_IN // N_DEV, D), (FOX_W, D // N_DEV), (DIL_OUT, D // N_DEV), (D // N_DEV, D),
                (D, DFF // N_DEV), (DFF // N_DEV, D))
W_NAMES = ("w_in", "w_a", "w_b", "w_out", "w_up", "w_down")
AG_COPIES = 7
HBM = pl.BlockSpec(memory_space=pltpu.HBM)
SEM = pl.BlockSpec(memory_space=pltpu.SEMAPHORE)


def _place():
    return lax.axis_index("x"), lax.axis_index("y"), lax.axis_index("c")


def _all_gather(shards):
    n_w = len(shards)
    shapes = [t.shape for t in shards]

    def body(*refs):
        in_refs, out_refs = refs[:n_w], refs[n_w:2 * n_w]
        stage, cast = refs[2 * n_w:3 * n_w], refs[3 * n_w:4 * n_w]
        send_sems, recv_sems, local_sems, load_sems = refs[4 * n_w:]
        x, y, c = _place()
        me, sibling = (x, y, c), (x, y, 1 - c)
        chips = [(1 - x, y), (x, 1 - y), (1 - x, 1 - y)]

        def slot(i, px, py, pc):
            return out_refs[i].at[4 * px + 2 * py + pc]

        def copy(i, k, block, to, src=None):
            return pltpu.make_async_remote_copy(
                src_ref=slot(i, *block) if src is None else src, dst_ref=slot(i, *block),
                send_sem=send_sems.at[i * AG_COPIES + k], recv_sem=recv_sems.at[i * AG_COPIES + k],
                device_id=to, device_id_type=MESH)

        loads = [pltpu.make_async_copy(in_refs[i], stage[i], load_sems.at[i]) for i in range(n_w)]
        for ld in loads:
            ld.start()
        local, remote = [], []
        for i in range(n_w):
            loads[i].wait()
            cast[i][...] = stage[i][...].astype(BF16)
            local.append(pltpu.make_async_copy(cast[i], slot(i, *me), local_sems.at[i]))
            local[-1].start()
            first = [copy(i, 0, me, sibling, src=cast[i])]
            first += [copy(i, 1 + j, me, (*chip, c), src=cast[i]) for j, chip in enumerate(chips)]
            for cp in first:
                cp.start()
            remote += first
        for i in range(n_w):
            for j, chip in enumerate(chips):
                copy(i, 1 + j, (*chip, c), me).wait_recv()
                remote.append(copy(i, 4 + j, (*chip, c), sibling))
                remote[-1].start()
        for i in range(n_w):
            copy(i, 0, sibling, me).wait_recv()
            for j, chip in enumerate(chips):
                copy(i, 4 + j, (*chip, 1 - c), me).wait_recv()
        for cp in remote:
            cp.wait_send()
        for cp in local:
            cp.wait()

    return pl.pallas_call(
        body, name="all_gather_w_in",
        out_shape=[jax.ShapeDtypeStruct((N_DEV,) + sh, BF16) for sh in shapes],
        in_specs=[ANY] * n_w, out_specs=[ANY] * n_w,
        scratch_shapes=[pltpu.VMEM(sh, F32) for sh in shapes] + [pltpu.VMEM(sh, BF16) for sh in shapes]
                       + [pltpu.SemaphoreType.DMA((n_w * AG_COPIES,)), pltpu.SemaphoreType.DMA((n_w * AG_COPIES,)),
                          pltpu.SemaphoreType.DMA((n_w,)), pltpu.SemaphoreType.DMA((n_w,))],
        compiler_params=_cparams(None),
    )(*shards)


def _place_own(shards):
    n_w = len(shards)
    shapes = [t.shape for t in shards]

    def body(*refs):
        in_refs, out_refs = refs[:n_w], refs[n_w:2 * n_w]
        stage, cast = refs[2 * n_w:3 * n_w], refs[3 * n_w:4 * n_w]
        load_sems, store_sems = refs[4 * n_w:]
        x, y, c = _place()
        loads = [pltpu.make_async_copy(in_refs[i], stage[i], load_sems.at[i]) for i in range(n_w)]
        for ld in loads:
            ld.start()
        stores = []
        for i in range(n_w):
            loads[i].wait()
            cast[i][...] = stage[i][...].astype(BF16)
            stores.append(pltpu.make_async_copy(cast[i], out_refs[i].at[4 * x + 2 * y + c], store_sems.at[i]))
            stores[-1].start()
        for st in stores:
            st.wait()

    return pl.pallas_call(
        body, name="place_own_shards",
        out_shape=[jax.ShapeDtypeStruct((N_DEV,) + sh, BF16) for sh in shapes],
        in_specs=[ANY] * n_w, out_specs=[ANY] * n_w,
        scratch_shapes=[pltpu.VMEM(sh, F32) for sh in shapes] + [pltpu.VMEM(sh, BF16) for sh in shapes]
                       + [pltpu.SemaphoreType.DMA((n_w,)), pltpu.SemaphoreType.DMA((n_w,))],
        compiler_params=_cparams(None),
    )(*shards)


PEER_COPIES = N_DEV - 1
SIBLING_COPIES = 4
CHIP_COPIES = 3


def _peer_copies(_, land_refs, send_sems, recv_sems):
    x, y, c = _place()
    mine = 4 * x + 2 * y + c
    copies = []
    for i, ref in enumerate(land_refs):
        for k in range(1, N_DEV):
            peer = (x ^ (k >> 2), y ^ ((k >> 1) & 1), c ^ (k & 1))
            n = i * PEER_COPIES + k - 1
            copies.append(pltpu.make_async_remote_copy(
                src_ref=ref.at[mine], dst_ref=ref.at[mine], send_sem=send_sems.at[n], recv_sem=recv_sems.at[n],
                device_id=peer, device_id_type=MESH))
    return copies


def _sibling_copies(g_refs, r_refs, send_sems, recv_sems):
    x, y, c = _place()
    return [pltpu.make_async_remote_copy(
        src_ref=g_refs[i].at[2 * k + (1 - c)], dst_ref=r_refs[i].at[k],
        send_sem=send_sems.at[SIBLING_COPIES * i + k], recv_sem=recv_sems.at[SIBLING_COPIES * i + k],
        device_id=(x, y, 1 - c), device_id_type=MESH) for i in range(len(g_refs)) for k in range(SIBLING_COPIES)]


def _chip_copies(p_refs, r_refs, send_sems, recv_sems):
    x, y, c = _place()
    chips = [(1 - x, y), (x, 1 - y), (1 - x, 1 - y)]
    return [pltpu.make_async_remote_copy(
        src_ref=p_refs[i].at[2 * cx + cy], dst_ref=r_refs[i].at[j],
        send_sem=send_sems.at[CHIP_COPIES * i + j], recv_sem=recv_sems.at[CHIP_COPIES * i + j],
        device_id=(cx, cy, c), device_id_type=MESH) for i in range(len(p_refs)) for j, (cx, cy) in enumerate(chips)]


def _in_hbm(a):
    return pltpu.with_memory_space_constraint(a, pltpu.HBM)


def _exchange_start(name, copies_fn, n_copies, srcs, lands, after=()):
    n_s, n_l, n_a = len(srcs), len(lands), len(after)

    def body(*refs):
        outs = refs[n_s + n_l + n_a:]
        for cp in copies_fn(refs[:n_s], refs[n_s:n_s + n_l], outs[0], outs[1]):
            cp.start()
        outs[-1][...] = jnp.zeros_like(outs[-1])

    res = pl.pallas_call(
        body, name=name,
        out_shape=[pltpu.SemaphoreType.DMA((n_copies,)), pltpu.SemaphoreType.DMA((n_copies,))]
                  + [pltpu.HBM(t.shape, t.dtype) for t in (*srcs, *lands)] + [jax.ShapeDtypeStruct((8, LANES), F32)],
        in_specs=[HBM] * (n_s + n_l) + [ANY] * n_a,
        out_specs=[SEM, SEM] + [HBM] * (n_s + n_l) + [pl.BlockSpec(memory_space=pltpu.VMEM)],
        input_output_aliases={i: 2 + i for i in range(n_s + n_l)},
        compiler_params=pltpu.CompilerParams(has_side_effects=pltpu.SideEffectType.DATAFLOW_SIDE_EFFECTING),
    )(*[_in_hbm(t) for t in (*srcs, *lands)], *after)
    return res[0], res[1], list(res[2:2 + n_s]), list(res[2 + n_s:2 + n_s + n_l]), res[-1]


def _exchange_wait(name, copies_fn, started, after):
    send_sems, recv_sems, srcs, lands, _ = started
    n_s, n_l = len(srcs), len(lands)

    def body(*refs):
        for cp in copies_fn(refs[:n_s], refs[n_s:n_s + n_l], refs[n_s + n_l], refs[n_s + n_l + 1]):
            cp.wait_send()
            cp.wait_recv()

    res = pl.pallas_call(
        body, name=name,
        out_shape=[pltpu.HBM(t.shape, t.dtype) for t in (*srcs, *lands)],
        in_specs=[HBM] * (n_s + n_l) + [SEM, SEM, ANY],
        out_specs=[HBM] * (n_s + n_l),
        input_output_aliases={i: i for i in range(n_s + n_l)},
        compiler_params=pltpu.CompilerParams(has_side_effects=pltpu.SideEffectType.DATAFLOW_SIDE_EFFECTING),
    )(*srcs, *lands, send_sems, recv_sems, after)
    return list(res[:n_s]), list(res[n_s:])


def _shard_block(shape):
    rows, cols = shape
    if rows % 256:
        return rows, 256
    return min(rows, 512 if cols <= LANES else 256), cols


def _chip_partial(ids, g_stack, recv1, name):
    shape = g_stack.shape[1:]
    br, bc = _shard_block(shape)

    def body(ids_ref, g_ref, r_ref, pb_ref, own_ref):
        s = g_ref[...] + r_ref[...]
        pb_ref[...] = s.astype(BF16)

        @pl.when(pl.program_id(2) == ids_ref[1])
        def _():
            own_ref[...] = s

    grid_spec = pltpu.PrefetchScalarGridSpec(
        num_scalar_prefetch=1, grid=(shape[0] // br, shape[1] // bc, 4),
        in_specs=[pl.BlockSpec((None, br, bc), lambda r, q, k, ids: (2 * k + ids[0], r, q)),
                  pl.BlockSpec((None, br, bc), lambda r, q, k, ids: (k, r, q))],
        out_specs=[pl.BlockSpec((None, br, bc), lambda r, q, k, ids: (k, r, q)),
                   pl.BlockSpec((br, bc), lambda r, q, k, ids: (r, q))])
    return pl.pallas_call(
        body, name=name, grid_spec=grid_spec,
        out_shape=[jax.ShapeDtypeStruct((4,) + shape, BF16), jax.ShapeDtypeStruct(shape, F32)],
        compiler_params=_cparams(("parallel", "parallel", "arbitrary")),
    )(ids, g_stack, recv1)


def _adamw(w, g, m, v):
    m = ADAM_B1 * m + (1.0 - ADAM_B1) * g
    v = ADAM_B2 * v + (1.0 - ADAM_B2) * (g * g)
    m_hat = m / (1.0 - ADAM_B1 ** ADAM_STEP)
    v_hat = v / (1.0 - ADAM_B2 ** ADAM_STEP)
    delta = -ADAM_LR * (m_hat / (jnp.sqrt(v_hat) + ADAM_EPS) + ADAM_WD * w)
    return delta, m, v


def _reduce_adamw(own, recv2, w, m, v, name, deps=()):
    shape = own.shape
    br, bc = _shard_block(shape)

    def body(own_ref, r_ref, w_ref, m_ref, v_ref, *refs):
        g_ref, d_ref, nm_ref, nv_ref = refs[len(deps):]
        g = own_ref[...]
        for j in range(3):
            g = g + r_ref[j].astype(F32)
        delta, nm, nv = _adamw(w_ref[...], g, m_ref[...], v_ref[...])
        g_ref[...] = g
        d_ref[...] = delta
        nm_ref[...] = nm
        nv_ref[...] = nv

    blk = pl.BlockSpec((br, bc), lambda r, q: (r, q))
    return pl.pallas_call(
        body, name=name, grid=(shape[0] // br, shape[1] // bc),
        in_specs=[blk, pl.BlockSpec((3, br, bc), lambda r, q: (0, r, q)), blk, blk, blk] + [ANY] * len(deps),
        out_specs=[blk] * 4, out_shape=[jax.ShapeDtypeStruct(shape, F32)] * 4,
        compiler_params=_cparams(("parallel", "parallel")),
    )(own, recv2, w, m, v, *deps)


def _small_allreduce_adamw(gvec, w, m, v, deps=()):
    def body(g_ref, w_ref, m_ref, v_ref, *refs):
        go_ref, d_ref, nm_ref, nv_ref, buf, send_sems, recv_sems = refs[len(deps):]
        x, y, c = _place()
        my_slot = 4 * x + 2 * y + c
        buf[my_slot] = g_ref[...]
        copies = []
        for k in range(1, N_DEV):
            px, py, pc = x ^ (k >> 2), y ^ ((k >> 1) & 1), c ^ (k & 1)
            copies.append(pltpu.make_async_remote_copy(
                src_ref=g_ref, dst_ref=buf.at[my_slot], send_sem=send_sems.at[k - 1], recv_sem=recv_sems.at[k - 1],
                device_id=(px, py, pc), device_id_type=MESH))
        for cp in copies:
            cp.start()
        for k in range(1, N_DEV):
            px, py, pc = x ^ (k >> 2), y ^ ((k >> 1) & 1), c ^ (k & 1)
            pltpu.make_async_remote_copy(
                src_ref=g_ref, dst_ref=buf.at[4 * px + 2 * py + pc], send_sem=send_sems.at[k - 1],
                recv_sem=recv_sems.at[k - 1], device_id=(px, py, pc), device_id_type=MESH).wait_recv()
        for cp in copies:
            cp.wait_send()
        g = buf[0]
        for s in range(1, N_DEV):
            g = g + buf[s]
        delta, nm, nv = _adamw(w_ref[...], g, m_ref[...], v_ref[...])
        go_ref[...] = g
        d_ref[...] = delta
        nm_ref[...] = nm
        nv_ref[...] = nv

    vm = pl.BlockSpec(memory_space=pltpu.VMEM)
    return pl.pallas_call(
        body, name="small_allreduce_adamw",
        in_specs=[vm] * 4 + [ANY] * len(deps), out_specs=[vm] * 4,
        out_shape=[jax.ShapeDtypeStruct((SMALL_R, LANES), F32)] * 4,
        scratch_shapes=[pltpu.VMEM((N_DEV, SMALL_R, LANES), F32),
                        pltpu.SemaphoreType.DMA((N_DEV - 1,)), pltpu.SemaphoreType.DMA((N_DEV - 1,))],
    )(gvec, w, m, v, *deps)


def _cols_to_whole(stack):
    rows = stack.shape[1]
    return stack.transpose(1, 0, 2).reshape(rows, -1)


def _whole_to_cols(t):
    rows = t.shape[0]
    return t.reshape(rows, N_DEV, -1).transpose(1, 0, 2)


def _pad_w_in_t(wt):
    n_a = 3 * FOX_W
    n_b = n_a + FOX_H
    return jnp.concatenate([wt[:n_a], wt[n_b:n_b + 3 * DIL_W], jnp.zeros((CB, D), wt.dtype),
                            wt[n_b + 3 * DIL_W:], wt[n_a:n_b], jnp.zeros((CB - FOX_H, D), wt.dtype)], axis=0)


def _unpad_dw_in_t(dwp):
    return jnp.concatenate([dwp[:SEG_B], dwp[SEG_F:SEG_F + FOX_H], dwp[SEG_B:SEG_Z], dwp[SEG_G:SEG_F]], axis=0)


def _pack_small(g_attn, b, g_mlp, g_final):
    b_rows = jnp.pad(b, ((0, 7), (0, LANES - b.shape[1])))
    return jnp.concatenate([g_attn.reshape(8, LANES), g_mlp.reshape(8, LANES), g_final.reshape(8, LANES), b_rows], axis=0)


def _unpack_small(p):
    return (p[0:8].reshape(1, D), p[24:25, :FOX_H], p[8:16].reshape(1, D), p[16:24].reshape(D))


class _StepComm:
    def __init__(self, ids, w_sh, m_sh, v_sh, after_w_in):
        self.ids = ids
        self.w_sh, self.m_sh, self.v_sh = w_sh, m_sh, v_sh
        self.updates = None
        lands = _place_own(w_sh[1:])
        self.gather_mixer = _exchange_start("gather_mixer_start", _peer_copies, PEER_COPIES * 3, [], lands[:3],
                                            after=(after_w_in,))
        self.gather_mlp = _exchange_start("gather_mlp_start", _peer_copies, PEER_COPIES * 2, [], lands[3:],
                                          after=(self.gather_mixer[-1],))
        self.sibling = {}
        self.chips = {}
        self.own = {}
        self.names = {}

    def _reduce_start(self, group, names, grads, after=()):
        lands = [lax.empty((SIBLING_COPIES,) + t.shape[1:], F32) for t in grads]
        self.sibling[group] = _exchange_start("grad_%s_sibling_start" % group, _sibling_copies,
                                              SIBLING_COPIES * len(grads), grads, lands, after=after)
        self.names[group] = names
        return self.sibling[group][-1]

    def _reduce_mid(self, group, after):
        grads, recv1 = _exchange_wait("grad_%s_sibling_wait" % group, _sibling_copies, self.sibling[group], after)
        parts = [_chip_partial(self.ids, g, r, "grad_partial_" + n) for g, r, n in zip(grads, recv1, self.names[group])]
        self.own[group] = [p[1] for p in parts]
        srcs = [p[0] for p in parts]
        lands = [lax.empty((CHIP_COPIES,) + t.shape[1:], BF16) for t in srcs]
        self.chips[group] = _exchange_start("grad_%s_chips_start" % group, _chip_copies, CHIP_COPIES * len(srcs),
                                            srcs, lands)
        return self.chips[group][-1]

    def reduced(self, group, after):
        _, recv2 = _exchange_wait("grad_%s_chips_wait" % group, _chip_copies, self.chips[group], after)
        return list(zip(self.own[group], recv2))

    def first_deps(self):
        return [self.gather_mlp[-1]]

    def mixer_weights(self, after):
        _, (g_a, g_b, g_out) = _exchange_wait("gather_mixer_wait", _peer_copies, self.gather_mixer, after)
        return _cols_to_whole(g_a), _cols_to_whole(g_b), g_out.reshape(D, D)

    def mlp_weights(self, after):
        _, (g_up, g_down) = _exchange_wait("gather_mlp_wait", _peer_copies, self.gather_mlp, after)
        return g_up, g_down.reshape(DFF, D)

    def mlp_grads(self, dw_up_sh, dw_down):
        return [self._reduce_start("mlp", W_NAMES[4:], [dw_up_sh, dw_down.reshape((N_DEV,) + SHARD_SHAPES[5])])]

    def mixer_grads(self, dw_a, dw_b, dw_out):
        token = self._reduce_mid("mlp", dw_b)
        grads = [_whole_to_cols(dw_a), _whole_to_cols(dw_b), dw_out.reshape((N_DEV,) + SHARD_SHAPES[3])]
        return [self._reduce_start("mixer", W_NAMES[1:4], grads, after=(token,))]

    def mid_backward(self, after):
        return [self._reduce_mid("mixer", after)]

    def w_in_grad(self, dw_in_t):
        grads = [_unpad_dw_in_t(dw_in_t).reshape((N_DEV,) + SHARD_SHAPES[0])]
        token = self._reduce_start("w_in", W_NAMES[:1], grads)
        reduced = self.reduced("mixer", token) + self.reduced("mlp", token)
        self.updates = [
            _reduce_adamw(*r, self.w_sh[1 + i], self.m_sh[1 + i], self.v_sh[1 + i], "adamw_" + W_NAMES[1 + i], deps=[token])
            for i, r in enumerate(reduced)]
        return [self._reduce_mid("w_in", self.updates[-1][0])]

    def w_in_update(self, after):
        (reduced,) = self.reduced("w_in", after)
        return _reduce_adamw(*reduced, self.w_sh[0], self.m_sh[0], self.v_sh[0], "adamw_" + W_NAMES[0])


def kernel(x, norm_attn_g, w_in, b_forget, w_branch_a, w_branch_b, w_out, norm_mlp_g, w_up, w_down, norm_final_g, loss_target, m_norm_attn_g, m_w_in, m_b_forget, m_w_branch_a, m_w_branch_b, m_w_out, m_norm_mlp_g, m_w_up, m_w_down, m_norm_final_g, v_norm_attn_g, v_w_in, v_b_forget, v_w_branch_a, v_w_branch_b, v_w_out, v_norm_mlp_g, v_w_up, v_w_down, v_norm_final_g):
    cx, cy, cc = _place()
    ids = jnp.stack([cc, 2 * cx + cy]).astype(jnp.int32)

    w_sh = [w_in[0].T] + [t[0] for t in (w_branch_a, w_branch_b, w_out, w_up, w_down)]
    m_sh = [m_w_in[0].T] + [t[0] for t in (m_w_branch_a, m_w_branch_b, m_w_out, m_w_up, m_w_down)]
    v_sh = [v_w_in[0].T] + [t[0] for t in (v_w_branch_a, v_w_branch_b, v_w_out, v_w_up, v_w_down)]

    (g_in,) = _all_gather(w_sh[:1])
    w_in_t = _pad_w_in_t(g_in.reshape(D_IN, D))
    comm = _StepComm(ids, w_sh, m_sh, v_sh, g_in)
    b_pad = jnp.pad(b_forget, ((0, 0), (0, LANES - FOX_H)))

    loss_row, dx, dsmall = _local_step(
        x[0], loss_target[0], norm_attn_g, norm_mlp_g, norm_final_g.reshape(1, D), b_pad, w_in_t, comm)
    loss = lax.psum(loss_row[0, 0], ("x", "y", "c"))

    dg_attn, db, dg_mlp, dg_final = dsmall
    small = _small_allreduce_adamw(
        _pack_small(dg_attn, db, dg_mlp, dg_final),
        _pack_small(norm_attn_g, b_forget, norm_mlp_g, norm_final_g.reshape(1, D)),
        _pack_small(m_norm_attn_g, m_b_forget, m_norm_mlp_g, m_norm_final_g.reshape(1, D)),
        _pack_small(v_norm_attn_g, v_b_forget, v_norm_mlp_g, v_norm_final_g.reshape(1, D)))
    big = [comm.w_in_update(small[0])] + comm.updates

    outs = [loss, dx[None]]
    for q in range(4):
        s_attn, s_b, s_mlp, s_final = _unpack_small(small[q])
        b_in, b_a, b_b, b_out, b_up, b_down = [t[q][None] for t in big]
        b_in = jnp.swapaxes(b_in, 1, 2)
        outs += [s_attn, b_in, s_b, b_a, b_b, b_out, s_mlp, b_up, b_down, s_final]
    return tuple(outs)
```

```python
import functools

import jax
import jax.numpy as jnp
from jax import lax
from jax.experimental import pallas as pl
from jax.experimental.pallas import tpu as pltpu

F32 = jnp.float32
BF16 = jnp.bfloat16
MESH = pl.DeviceIdType.MESH

S = 2048
D = 1024
HD = 64
FOX_H = 8
FOX_W = FOX_H * HD
DIL_HG = 4
DIL_G = 3
DIL_W = DIL_G * DIL_HG * HD
DIL_OUT = DIL_HG * HD
DIL_BLK = 128
DIL_R = (1, 4, 16)
DFF = 4 * D
D_IN = 3 * FOX_W + FOX_H + 3 * DIL_W + 2 * D
EPS = 1e-6
NEG_INF = -1e30
SCALE = HD ** -0.5
ROPE_THETA = 500000.0
ROPE_DIM = HD // 4
N_DEV = 8

ADAM_LR = 0.001
ADAM_B1 = 0.9
ADAM_B2 = 0.999
ADAM_EPS = 1e-08
ADAM_WD = 0.01
ADAM_STEP = 10

LANES = 128
CB = 256
SEG_A = 0
SEG_B = 3 * FOX_W
SEG_Z = SEG_B + 3 * DIL_W
SEG_G = SEG_Z + CB
SEG_F = SEG_G + 2 * D
D_PAD = SEG_F + CB
SMALL_R = 32

VMEM_MB = 56


def _cparams(dims=None, vmem_mb=VMEM_MB, **kw):
    return pltpu.CompilerParams(dimension_semantics=dims, vmem_limit_bytes=vmem_mb << 20, **kw)


ANY = pl.BlockSpec(memory_space=pl.ANY)

_NN = (((1,), (0,)), ((), ()))
_NT = (((1,), (1,)), ((), ()))
_TN = (((0,), (0,)), ((), ()))


def _dot(a, b, dims):
    return lax.dot_general(a.astype(BF16), b.astype(BF16), dims, preferred_element_type=F32)


def _mm(a, b, *, mode, tm, tn, out_dtypes, name, n=None, k=None, a_off=0, b_off=0,
        b_sharded=False, out_sharded=False, epilogue=None, extras=(), deps=()):
    n_sh = b.shape[-1] if b_sharded else None
    if mode == "nn":
        m = a.shape[0]
        k = k or a.shape[1]
        a_spec = pl.BlockSpec((tm, k), lambda i, j: (i, a_off))
        if b_sharded:
            assert tn == n_sh
            n = N_DEV * n_sh
            b_spec = pl.BlockSpec((None, k, tn), lambda i, j: (j, 0, 0))
        else:
            n = n or b.shape[1]
            b_spec = pl.BlockSpec((k, tn), lambda i, j: (0, j + b_off))
        dims = _NN
    elif mode == "nt":
        m = a.shape[0]
        k = k or a.shape[1]
        a_spec = pl.BlockSpec((tm, k), lambda i, j: (i, a_off))
        if b_sharded:
            n = b.shape[1]
            b_spec = pl.BlockSpec((N_DEV, tn, n_sh), lambda i, j: (0, j, 0))
        else:
            n = n or b.shape[0]
            b_spec = pl.BlockSpec((tn, k), lambda i, j: (j + b_off, 0))
        dims = _NT
    else:
        k, m = a.shape
        n = n or b.shape[1]
        a_spec = pl.BlockSpec((k, tm), lambda i, j: (0, i))
        b_spec = pl.BlockSpec((k, tn), lambda i, j: (0, j + b_off))
        dims = _TN
    assert m % tm == 0 and n % tn == 0, (name, m, n, tm, tn)
    n_extra = len(extras)
    tile = pl.BlockSpec((tm, tn), lambda i, j: (i, j))
    if out_sharded:
        assert mode == "tn" and n // tn == N_DEV
        out_spec = pl.BlockSpec((None, tm, tn), lambda i, j: (j, i, 0))
        out_shape = (N_DEV, m, tn)
    else:
        out_spec, out_shape = tile, (m, n)

    def body(a_ref, b_ref, *refs):
        if mode == "nt" and b_sharded:
            acc = _dot(a_ref[:, 0:n_sh], b_ref[0], dims)
            for p in range(1, N_DEV):
                acc = acc + _dot(a_ref[:, p * n_sh:(p + 1) * n_sh], b_ref[p], dims)
        else:
            acc = _dot(a_ref[...], b_ref[...], dims)
        ex = [r[...] for r in refs[:n_extra]]
        outs = epilogue(acc, *ex) if epilogue is not None else (acc,)
        for o_ref, o in zip(refs[n_extra + len(deps):], outs):
            o_ref[...] = o.astype(o_ref.dtype)

    res = pl.pallas_call(
        body, name=name, grid=(m // tm, n // tn),
        in_specs=[a_spec, b_spec] + [tile] * n_extra + [ANY] * len(deps),
        out_specs=[out_spec] * len(out_dtypes),
        out_shape=[jax.ShapeDtypeStruct(out_shape, dt) for dt in out_dtypes],
        compiler_params=_cparams(("parallel", "parallel")),
    )(a, b, *extras, *deps)
    return res if len(out_dtypes) > 1 else res[0]


ROW_T = 256


def _rms_fwd(x, g, name, deps=()):
    def body(x_ref, g_ref, *refs):
        o_ref = refs[-1]
        x = x_ref[...]
        r = lax.rsqrt(jnp.mean(x * x, axis=-1, keepdims=True) + EPS)
        o_ref[...] = ((x * r) * g_ref[...]).astype(BF16)

    return pl.pallas_call(
        body, name=name, grid=(S // ROW_T,),
        in_specs=[pl.BlockSpec((ROW_T, D), lambda i: (i, 0)), pl.BlockSpec((1, D), lambda i: (0, 0))] + [ANY] * len(deps),
        out_specs=pl.BlockSpec((ROW_T, D), lambda i: (i, 0)),
        out_shape=jax.ShapeDtypeStruct((S, D), BF16),
        compiler_params=_cparams(("parallel",)),
    )(x, g, *deps)


def _rms_bwd(x, g, dh, dres, name):
    def body(x_ref, g_ref, dh_ref, dres_ref, dx_ref, dg_ref):
        @pl.when(pl.program_id(0) == 0)
        def _():
            dg_ref[...] = jnp.zeros_like(dg_ref)

        x = x_ref[...]
        dh = dh_ref[...]
        r = lax.rsqrt(jnp.mean(x * x, axis=-1, keepdims=True) + EPS)
        xn = x * r
        dhn = dh * g_ref[...]
        dx_ref[...] = dres_ref[...] + r * (dhn - xn * jnp.mean(dhn * xn, axis=-1, keepdims=True))
        dg_ref[...] += jnp.sum(dh * xn, axis=0, keepdims=True)

    row = pl.BlockSpec((ROW_T, D), lambda i: (i, 0))
    vec = pl.BlockSpec((1, D), lambda i: (0, 0))
    return pl.pallas_call(
        body, name=name, grid=(S // ROW_T,),
        in_specs=[row, vec, row, row], out_specs=[row, vec],
        out_shape=[jax.ShapeDtypeStruct((S, D), F32), jax.ShapeDtypeStruct((1, D), F32)],
        compiler_params=_cparams(("arbitrary",)),
    )(x, g, dh, dres)


def _final_norm_loss(x, g, tgt):
    def body(x_ref, g_ref, t_ref, dx_ref, dg_ref, loss_ref):
        @pl.when(pl.program_id(0) == 0)
        def _():
            dg_ref[...] = jnp.zeros_like(dg_ref)
            loss_ref[...] = jnp.zeros_like(loss_ref)

        x = x_ref[...]
        g = g_ref[...]
        r = lax.rsqrt(jnp.mean(x * x, axis=-1, keepdims=True) + EPS)
        xn = x * r
        err = xn * g - t_ref[...]
        row_loss = jnp.mean(err * err, axis=-1, keepdims=True)
        loss_ref[...] += 0.5 * jnp.sum(row_loss, axis=0, keepdims=True) * jnp.ones((1, LANES), F32)
        dy = err * (1.0 / D)
        dyn = dy * g
        dx_ref[...] = r * (dyn - xn * jnp.mean(dyn * xn, axis=-1, keepdims=True))
        dg_ref[...] += jnp.sum(dy * xn, axis=0, keepdims=True)

    row = pl.BlockSpec((ROW_T, D), lambda i: (i, 0))
    vec = pl.BlockSpec((1, D), lambda i: (0, 0))
    return pl.pallas_call(
        body, name="final_norm_loss", grid=(S // ROW_T,),
        in_specs=[row, vec, row],
        out_specs=[row, vec, pl.BlockSpec((1, LANES), lambda i: (0, 0))],
        out_shape=[jax.ShapeDtypeStruct((S, D), F32), jax.ShapeDtypeStruct((1, D), F32),
                   jax.ShapeDtypeStruct((1, LANES), F32)],
        compiler_params=_cparams(("arbitrary",)),
    )(x, g, tgt)


def _sigmoid(z):
    return 1.0 / (1.0 + jnp.exp(-z))


def _gate_fwd(proj_g, ya, yb):
    def body(ga_ref, gb_ref, ya_ref, yb_ref, o_ref):
        o_ref[...] = (_sigmoid(ga_ref[...]) * ya_ref[...] + _sigmoid(gb_ref[...]) * yb_ref[...]).astype(BF16)

    row = pl.BlockSpec((ROW_T, D), lambda i: (i, 0))
    return pl.pallas_call(
        body, name="gate_fwd", grid=(S // ROW_T,),
        in_specs=[row, pl.BlockSpec((ROW_T, D), lambda i: (i, 1)), row, row],
        out_specs=row, out_shape=jax.ShapeDtypeStruct((S, D), BF16),
        compiler_params=_cparams(("parallel",)),
    )(proj_g, proj_g, ya, yb)


GATE_TR = 1024
GATE_TC = 512
GATE_NB = 2 * D // GATE_TC


def _gate_bwd(dmixed, proj_g, ya, yb):
    half = GATE_NB // 2

    def body(dm_ref, g_ref, ya_ref, yb_ref, dy_ref, dg_ref):
        dm = dm_ref[...]
        s = _sigmoid(g_ref[...])
        y = jnp.where(pl.program_id(1) < half, ya_ref[...], yb_ref[...])
        dy_ref[...] = (dm * s).astype(BF16)
        dg_ref[...] = (dm * y * (s * (1.0 - s))).astype(BF16)

    blk = pl.BlockSpec((GATE_TR, GATE_TC), lambda i, j: (i, j))
    wrapped = pl.BlockSpec((GATE_TR, GATE_TC), lambda i, j: (i, j % half))
    return pl.pallas_call(
        body, name="gate_bwd", grid=(S // GATE_TR, GATE_NB),
        in_specs=[wrapped, blk, wrapped, wrapped],
        out_specs=[blk, pl.BlockSpec((GATE_TR, GATE_TC), lambda i, j: (i, j + SEG_G // GATE_TC))],
        out_shape=[jax.ShapeDtypeStruct((S, 2 * D), BF16), jax.ShapeDtypeStruct((S, D_PAD), BF16)],
        compiler_params=_cparams(("parallel", "parallel")),
    )(dmixed, proj_g, ya, yb)


FOX_TQ = 256


def _scan_rows(x, reverse):
    n = x.shape[0]
    row = lax.broadcasted_iota(jnp.int32, x.shape, 0)
    k = 1
    while k < n:
        if reverse:
            x = x + jnp.where(row < n - k, pltpu.roll(x, n - k, 0), 0.0)
        else:
            x = x + jnp.where(row >= k, pltpu.roll(x, k, 0), 0.0)
        k *= 2
    return x


def _fox_scan(proj_f, b_pad):
    def body(f_ref, b_ref, fc_ref, fr_ref):
        z = f_ref[...] + b_ref[...]
        lf = jnp.minimum(z, 0.0) - jnp.log1p(jnp.exp(-jnp.abs(z)))
        f = _scan_rows(lf, reverse=False)
        fc_ref[...] = f
        fr_ref[...] = f.T

    return pl.pallas_call(
        body, name="fox_scan", grid=(1,),
        in_specs=[pl.BlockSpec((S, LANES), lambda i: (0, 0)), pl.BlockSpec((1, LANES), lambda i: (0, 0))],
        out_specs=[pl.BlockSpec((S, LANES), lambda i: (0, 0)), pl.BlockSpec((LANES, S), lambda i: (0, 0))],
        out_shape=[jax.ShapeDtypeStruct((S, LANES), F32), jax.ShapeDtypeStruct((LANES, S), F32)],
        compiler_params=_cparams(("arbitrary",)),
    )(proj_f, b_pad)


def _fox_dscan(dft, dfs, proj_f, b_pad, dproj):
    def body(dft_ref, dfs_ref, f_ref, b_ref, _, dfa_ref, db_ref):
        dfs_pad = jnp.concatenate([dfs_ref[...], jnp.zeros((LANES - FOX_H, S), F32)], axis=0)
        df = dfs_pad.T + dft_ref[0]
        for hp in range(1, dft_ref.shape[0]):
            df = df + pltpu.roll(dft_ref[hp], 2 * hp, 1)
        dlf = _scan_rows(df, reverse=True)
        z = f_ref[...] + b_ref[...]
        lane = lax.broadcasted_iota(jnp.int32, (S, LANES), 1)
        dfa = jnp.where(lane < FOX_H, dlf / (1.0 + jnp.exp(z)), 0.0)
        dfa_ref[:, :LANES] = dfa.astype(BF16)
        dfa_ref[:, LANES:] = jnp.zeros((S, CB - LANES), BF16)
        db_ref[...] = jnp.sum(dfa, axis=0, keepdims=True)

    return pl.pallas_call(
        body, name="fox_dscan", grid=(1,),
        in_specs=[pl.BlockSpec(dft.shape, lambda i: (0, 0, 0)), pl.BlockSpec((FOX_H, S), lambda i: (0, 0)),
                  pl.BlockSpec((S, LANES), lambda i: (0, 0)), pl.BlockSpec((1, LANES), lambda i: (0, 0)), ANY],
        out_specs=[pl.BlockSpec((S, CB), lambda i: (0, SEG_F // CB)), pl.BlockSpec((1, LANES), lambda i: (0, 0))],
        out_shape=[jax.ShapeDtypeStruct((S, D_PAD), BF16), jax.ShapeDtypeStruct((1, LANES), F32)],
        input_output_aliases={4: 0},
        compiler_params=_cparams(("arbitrary",)),
    )(dft, dfs, proj_f, b_pad, dproj)


FOX_NQ = S // FOX_TQ
FOX_HP = FOX_W // LANES
HEADS_PER_LB = LANES // HD


def _fox_scores(q, k, ft_all, fs_all, head, w):
    lane = lax.broadcasted_iota(jnp.int32, ft_all.shape, 1)
    sub = lax.broadcasted_iota(jnp.int32, fs_all.shape, 0)
    ft = jnp.sum(jnp.where(lane == head, ft_all, 0.0), axis=-1, keepdims=True)
    fs = jnp.sum(jnp.where(sub == head, fs_all, 0.0), axis=0, keepdims=True)
    s = (_dot(q, k, _NT) * SCALE + ft) - fs
    row = lax.broadcasted_iota(jnp.int32, (FOX_TQ, FOX_TQ), 0)
    col = lax.broadcasted_iota(jnp.int32, (FOX_TQ, FOX_TQ), 1)
    diag = jnp.where(col <= row, s[:, w * FOX_TQ:], NEG_INF)
    return diag if w == 0 else jnp.concatenate([s[:, :w * FOX_TQ], diag], axis=1)


def _fox_fwd(proj_a, fcol, frow):
    def body(q_ref, k_ref, v_ref, ft_ref, fs_ref, o_ref, lse_ref):
        hp = pl.program_id(0)
        qi = pl.program_id(1)
        lane = lax.broadcasted_iota(jnp.int32, (FOX_TQ, LANES), 1)
        for w in range(FOX_NQ):
            @pl.when(qi == w)
            def _(w=w):
                width = (w + 1) * FOX_TQ
                ft_all = ft_ref[...]
                fs_all = fs_ref[:, :width]
                lse = jnp.zeros((FOX_TQ, LANES), F32)
                for hh in range(HEADS_PER_LB):
                    sl = slice(hh * HD, (hh + 1) * HD)
                    s = _fox_scores(q_ref[:, sl], k_ref[:width, sl], ft_all, fs_all, hp * HEADS_PER_LB + hh, w)
                    m = jnp.max(s, axis=-1, keepdims=True)
                    p = jnp.exp(s - m)
                    l = jnp.sum(p, axis=-1, keepdims=True)
                    o_ref[:, sl] = _dot(p / l, v_ref[:width, sl], _NN)
                    lse = jnp.where(lane == hh, m + jnp.log(l), lse)
                lse_ref[...] = lse

    return pl.pallas_call(
        body, name="fox_fwd", grid=(FOX_HP, FOX_NQ),
        in_specs=[pl.BlockSpec((FOX_TQ, LANES), lambda h, i: (i, h)),
                  pl.BlockSpec((S, LANES), lambda h, i: (0, FOX_HP + h)),
                  pl.BlockSpec((S, LANES), lambda h, i: (0, 2 * FOX_HP + h)),
                  pl.BlockSpec((FOX_TQ, LANES), lambda h, i: (i, 0)),
                  pl.BlockSpec((FOX_H, S), lambda h, i: (0, 0))],
        out_specs=[pl.BlockSpec((FOX_TQ, LANES), lambda h, i: (i, h)),
                   pl.BlockSpec((None, FOX_TQ, LANES), lambda h, i: (h, i, 0))],
        out_shape=[jax.ShapeDtypeStruct((S, FOX_W), F32), jax.ShapeDtypeStruct((FOX_HP, S, LANES), F32)],
        compiler_params=_cparams(("parallel", "parallel")),
    )(proj_a, proj_a, proj_a, fcol, frow)


def _fox_bwd(proj_a, fcol, frow, lse, do, dproj, deps=()):
    n_q = FOX_NQ

    def body(q_ref, k_ref, v_ref, ft_ref, fs_ref, lse_ref, do_ref, *refs):
        dqkv_ref, dft_ref, dfs_ref, dk_acc, dv_acc = refs[1 + len(deps):]
        hp = pl.program_id(0)
        t = pl.program_id(1)

        @pl.when(t == 0)
        def _():
            dk_acc[...] = jnp.zeros_like(dk_acc)
            dv_acc[...] = jnp.zeros_like(dv_acc)

        @pl.when((t == 0) & (hp == 0))
        def _():
            dfs_ref[...] = jnp.zeros_like(dfs_ref)

        lane = lax.broadcasted_iota(jnp.int32, (FOX_TQ, LANES), 1)
        for w in range(n_q):
            @pl.when(t == w)
            def _(w=w):
                width = (w + 1) * FOX_TQ
                ft_all = ft_ref[...]
                fs_all = fs_ref[:, :width]
                lse_all = lse_ref[...]
                sub = lax.broadcasted_iota(jnp.int32, (FOX_H, width), 0)
                dft = jnp.zeros((FOX_TQ, LANES), F32)
                for hh in range(HEADS_PER_LB):
                    sl = slice(hh * HD, (hh + 1) * HD)
                    head = hp * HEADS_PER_LB + hh
                    q = q_ref[:, sl]
                    k = k_ref[:width, sl]
                    do_h = do_ref[:, sl]
                    s = _fox_scores(q, k, ft_all, fs_all, head, w)
                    p = jnp.exp(s - lse_all[:, hh:hh + 1])
                    dp = _dot(do_h, v_ref[:width, sl], _NT)
                    ds = p * (dp - jnp.sum(dp * p, axis=-1, keepdims=True))
                    dft = jnp.where(lane == hh, jnp.sum(ds, axis=-1, keepdims=True), dft)
                    dfs_ref[:, :width] -= jnp.where(sub == head, jnp.sum(ds, axis=0, keepdims=True), 0.0)
                    dsb = (ds * SCALE).astype(BF16)
                    dqkv_ref[:, sl] = _dot(dsb, k, _NN).astype(BF16)
                    dk_acc[:width, sl] += _dot(dsb, q, _TN)
                    dv_acc[:width, sl] += _dot(p, do_h, _TN)
                dft_ref[...] = dft

        rows = pl.ds(pl.multiple_of((t % n_q) * FOX_TQ, FOX_TQ), FOX_TQ)

        @pl.when((t >= n_q) & (t < 2 * n_q))
        def _():
            dqkv_ref[...] = dk_acc[rows, :].astype(BF16)

        @pl.when(t >= 2 * n_q)
        def _():
            dqkv_ref[...] = dv_acc[rows, :].astype(BF16)

    def qidx(t):
        return jnp.minimum(t, n_q - 1)

    qblk = pl.BlockSpec((FOX_TQ, LANES), lambda h, t: (qidx(t), h))
    lane_blk = pl.BlockSpec((None, FOX_TQ, LANES), lambda h, t: (h, qidx(t), 0))
    return pl.pallas_call(
        body, name="fox_bwd", grid=(FOX_HP, 3 * n_q),
        in_specs=[qblk, pl.BlockSpec((S, LANES), lambda h, t: (0, FOX_HP + h)),
                  pl.BlockSpec((S, LANES), lambda h, t: (0, 2 * FOX_HP + h)),
                  pl.BlockSpec((FOX_TQ, LANES), lambda h, t: (qidx(t), 0)),
                  pl.BlockSpec((FOX_H, S), lambda h, t: (0, 0)), lane_blk, qblk, ANY] + [ANY] * len(deps),
        out_specs=[pl.BlockSpec((FOX_TQ, LANES), lambda h, t: (t % n_q, (t // n_q) * FOX_HP + h)), lane_blk,
                   pl.BlockSpec((FOX_H, S), lambda h, t: (0, 0))],
        out_shape=[jax.ShapeDtypeStruct((S, D_PAD), BF16),
                   jax.ShapeDtypeStruct((FOX_HP, S, LANES), F32), jax.ShapeDtypeStruct((FOX_H, S), F32)],
        scratch_shapes=[pltpu.VMEM((S, LANES), F32), pltpu.VMEM((S, LANES), F32)],
        input_output_aliases={7: 0},
        compiler_params=_cparams(("arbitrary", "arbitrary")),
    )(proj_a, proj_a, proj_a, fcol, frow, lse, do, dproj, *deps)


def _rope_tables():
    half = ROPE_DIM // 2
    inv_freq = jnp.power(jnp.float32(ROPE_THETA), -jnp.arange(half, dtype=F32) * 2.0 / ROPE_DIM)
    ang = jnp.arange(S).astype(F32)[:, None] * inv_freq[None, :]
    cos, sin = jnp.cos(ang), jnp.sin(ang)
    z = jnp.zeros((S, half), F32)
    rest = HD - ROPE_DIM
    t_self = jnp.concatenate([cos, cos, jnp.ones((S, rest), F32)], axis=1)
    t_up = jnp.concatenate([-sin, z, jnp.zeros((S, rest), F32)], axis=1)
    t_dn = jnp.concatenate([z, sin, jnp.zeros((S, rest), F32)], axis=1)
    rep = LANES // HD

    def by_residue(t):
        return jnp.stack([t.reshape(S // r, r, LANES).transpose(1, 0, 2).reshape(S, LANES) for r in DIL_R])

    return tuple(by_residue(jnp.tile(t, (1, rep))) for t in (t_self, t_up, t_dn))


def _residue_pieces(r):
    if r == 1:
        return [(slice(i, i + 512), slice(i, i + 512)) for i in range(0, S, 512)]
    n = S // r
    return [(pl.ds(j, n, stride=r), slice(j * n, (j + 1) * n)) for j in range(r)]


def _rope_apply(x, a, u, d, backward):
    half = ROPE_DIM // 2
    if backward:
        return x * a + pltpu.roll(x * u, half, 1) + pltpu.roll(x * d, LANES - half, 1)
    return x * a + pltpu.roll(x, LANES - half, 1) * u + pltpu.roll(x, half, 1) * d


LB_PER_CB = CB // LANES
ROPE_NB = 3 * DIL_G * LB_PER_CB


def _rope_step(s):
    per_group = 3 * LB_PER_CB
    return s // per_group, (s % per_group) // LB_PER_CB, s % LB_PER_CB


def _rope_split(proj_b, tables):
    def body(x_ref, a_ref, u_ref, d_ref, o_ref):
        g, t, _ = _rope_step(pl.program_id(0))
        for gs in range(DIL_G):
            @pl.when(g == gs)
            def _(gs=gs):
                for tok, sub in _residue_pieces(DIL_R[gs]):
                    x = x_ref[tok, :]
                    y = _rope_apply(x, a_ref[sub, :], u_ref[sub, :], d_ref[sub, :], False)
                    o_ref[sub, :] = jnp.where(t < 2, y, x).astype(BF16)

    def in_index(s):
        g, t, hf = _rope_step(s)
        return 0, (t * DIL_G + g) * LB_PER_CB + hf

    def out_index(s):
        g, t, hf = _rope_step(s)
        return t * DIL_G + g, 0, hf

    tab = pl.BlockSpec((None, S, LANES), lambda s: (_rope_step(s)[0], 0, 0))
    return pl.pallas_call(
        body, name="rope_split", grid=(ROPE_NB,),
        in_specs=[pl.BlockSpec((S, LANES), in_index), tab, tab, tab],
        out_specs=pl.BlockSpec((None, S, LANES), out_index),
        out_shape=jax.ShapeDtypeStruct((3 * DIL_G, S, CB), BF16),
        compiler_params=_cparams(("parallel",)),
    )(proj_b, *tables)


def _rope_merge_bwd(dq3, dk3, dv3, tables, dproj):
    def body(dq_ref, dk_ref, dv_ref, a_ref, u_ref, d_ref, _, o_ref, tmp):
        s = pl.program_id(0)
        g, t, _ = _rope_step(s)
        for gs in range(DIL_G):
            @pl.when((g == gs) & (s < ROPE_NB))
            def _(gs=gs):
                for tok, sub in _residue_pieces(DIL_R[gs]):
                    x = jnp.where(t == 0, dq_ref[sub, :], jnp.where(t == 1, dk_ref[sub, :], dv_ref[sub, :]))
                    y = _rope_apply(x, a_ref[sub, :], u_ref[sub, :], d_ref[sub, :], True)
                    tmp[tok, :] = jnp.where(t < 2, y, x)
                o_ref[...] = tmp[...].astype(BF16)

        @pl.when(s >= ROPE_NB)
        def _():
            o_ref[...] = jnp.zeros_like(o_ref)

    def src_spec(own):
        def index(s):
            g, t, hf = _rope_step(jnp.minimum(s, ROPE_NB - 1))
            return g, 0, jnp.where(t == own, hf, jnp.where(t > own, LB_PER_CB - 1, 0))
        return pl.BlockSpec((None, S, LANES), index)

    def out_index(s):
        g, t, hf = _rope_step(s)
        col = SEG_B // LANES + (t * DIL_G + g) * LB_PER_CB + hf
        return 0, jnp.where(s < ROPE_NB, col, SEG_Z // LANES + s - ROPE_NB)

    tab = pl.BlockSpec((None, S, LANES), lambda s: (_rope_step(jnp.minimum(s, ROPE_NB - 1))[0], 0, 0))
    return pl.pallas_call(
        body, name="rope_merge_bwd", grid=(ROPE_NB + LB_PER_CB,),
        in_specs=[src_spec(0), src_spec(1), src_spec(2), tab, tab, tab, ANY],
        out_specs=pl.BlockSpec((S, LANES), out_index),
        out_shape=jax.ShapeDtypeStruct((S, D_PAD), BF16),
        scratch_shapes=[pltpu.VMEM((S, LANES), F32)],
        input_output_aliases={6: 0},
        compiler_params=_cparams(("arbitrary",)),
    )(dq3, dk3, dv3, *tables, dproj)


DIL_NB = S // DIL_BLK


_BQK = (((2,), (2,)), ((0,), (0,)))
_BQD = (((2,), (1,)), ((0,), (0,)))
_BKD = (((1,), (1,)), ((0,), (0,)))


def _dil_mask(g):
    shape = (DIL_NB, DIL_BLK, 2 * DIL_BLK)
    blocks_per_seq = jnp.right_shift(DIL_NB, 2 * g)
    has_prev = jnp.bitwise_and(lax.broadcasted_iota(jnp.int32, shape, 0), blocks_per_seq - 1) != 0
    a = lax.broadcasted_iota(jnp.int32, shape, 1)
    kk = lax.broadcasted_iota(jnp.int32, shape, 2)
    diff = DIL_BLK + a - kk
    return (diff >= 0) & (diff <= DIL_BLK) & ((kk >= DIL_BLK) | has_prev)


def _blocks(t):
    return t.reshape(DIL_NB, DIL_BLK, t.shape[-1])


def _with_prev(t):
    prev = jnp.concatenate([jnp.zeros((DIL_BLK, t.shape[-1]), t.dtype), t[:-DIL_BLK]], axis=0)
    return jnp.concatenate([_blocks(prev), _blocks(t)], axis=1)


def _fold_prev(t2):
    n = t2.shape[-1]
    to_prev = t2[:, :DIL_BLK].reshape(S, n)
    own = t2[:, DIL_BLK:].reshape(S, n)
    return own + jnp.concatenate([to_prev[DIL_BLK:], jnp.zeros((DIL_BLK, n), t2.dtype)], axis=0)


def _dil_group_spec(t, width=DIL_OUT):
    return pl.BlockSpec((None, S, width), lambda g: (t * DIL_G + g, 0, 0))


def _dil_fwd(qkv3):
    def body(q_ref, k_ref, v_ref, o_ref, lse_ref):
        ok = _dil_mask(pl.program_id(0))
        lane = lax.broadcasted_iota(jnp.int32, (S, LANES), 1)
        lse = jnp.zeros((S, LANES), F32)
        for h in range(DIL_HG):
            sl = slice(h * HD, (h + 1) * HD)
            s = jnp.where(ok, _dot(_blocks(q_ref[:, sl]), _with_prev(k_ref[:, sl]), _BQK) * SCALE, NEG_INF)
            m = jnp.max(s, axis=-1, keepdims=True)
            p = jnp.exp(s - m)
            l = jnp.sum(p, axis=-1, keepdims=True)
            o_ref[:, sl] = _dot(p / l, _with_prev(v_ref[:, sl]), _BQD).reshape(S, HD)
            lse = jnp.where(lane == h, (m + jnp.log(l)).reshape(S, 1), lse)
        lse_ref[...] = lse

    return pl.pallas_call(
        body, name="dil_fwd", grid=(DIL_G,),
        in_specs=[_dil_group_spec(0), _dil_group_spec(1), _dil_group_spec(2)],
        out_specs=[_dil_group_spec(0), _dil_group_spec(0, LANES)],
        out_shape=[jax.ShapeDtypeStruct((DIL_G, S, DIL_OUT), F32), jax.ShapeDtypeStruct((DIL_G, S, LANES), F32)],
        compiler_params=_cparams(("parallel",)),
    )(qkv3, qkv3, qkv3)


def _dil_bwd(qkv3, lse3, do3, c3):
    def body(q_ref, k_ref, v_ref, lse_ref, do_ref, c_ref, dq_ref, dk_ref, dv_ref):
        ok = _dil_mask(pl.program_id(0))
        lse = lse_ref[...]
        c = c_ref[...]
        for h in range(DIL_HG):
            sl = slice(h * HD, (h + 1) * HD)
            q = _blocks(q_ref[:, sl])
            k2 = _with_prev(k_ref[:, sl])
            do_h = _blocks(do_ref[:, sl])
            s = jnp.where(ok, _dot(q, k2, _BQK) * SCALE, NEG_INF)
            p = jnp.exp(s - _blocks(lse[:, h:h + 1]))
            dp = _dot(do_h, _with_prev(v_ref[:, sl]), _BQK)
            ds = p * (dp - _blocks(c[:, h:h + 1]))
            dsb = (ds * SCALE).astype(BF16)
            dq_ref[:, sl] = _dot(dsb, k2, _BQD).reshape(S, HD)
            dk_ref[:, sl] = _fold_prev(_dot(dsb, q, _BKD))
            dv_ref[:, sl] = _fold_prev(_dot(p, do_h, _BKD))

    return pl.pallas_call(
        body, name="dil_bwd", grid=(DIL_G,),
        in_specs=[_dil_group_spec(0), _dil_group_spec(1), _dil_group_spec(2), _dil_group_spec(0, LANES),
                  _dil_group_spec(0), _dil_group_spec(0, LANES)],
        out_specs=[_dil_group_spec(0)] * 3,
        out_shape=[jax.ShapeDtypeStruct((DIL_G, S, DIL_OUT), F32)] * 3,
        compiler_params=_cparams(("parallel",)),
    )(qkv3, qkv3, qkv3, lse3, do3, c3)


COMB_T = 256


def _dil_combine(o3, lse3):
    heads_per_lb = LANES // HD

    def body(o_ref, lse_ref, ob_ref, al_ref, *scratch):
        o_tok = [scratch[LB_PER_CB * gg:LB_PER_CB * (gg + 1)] for gg in range(DIL_G)]
        lse_tok = scratch[LB_PER_CB * DIL_G:]
        g = pl.program_id(0)
        for gs in range(DIL_G):
            @pl.when(g == gs)
            def _(gs=gs):
                for tok, sub in _residue_pieces(DIL_R[gs]):
                    for hf in range(LB_PER_CB):
                        o_tok[gs][hf][tok, :] = o_ref[sub, hf * LANES:(hf + 1) * LANES]
                    lse_tok[gs][tok, :] = lse_ref[sub, :]

        @pl.when(g == DIL_G - 1)
        def _():
            def chunk(i, carry):
                rows = pl.ds(pl.multiple_of(i * COMB_T, COMB_T), COMB_T)
                lse = [lse_tok[gg][rows, :] for gg in range(DIL_G)]
                m = jnp.maximum(jnp.maximum(lse[0], lse[1]), lse[2])
                e = [jnp.exp(lse[gg] - m) for gg in range(DIL_G)]
                den = (e[0] + e[1]) + e[2]
                al = [e[gg] / den for gg in range(DIL_G)]
                for gg in range(DIL_G):
                    al_ref[gg, rows, :] = al[gg]
                for h in range(DIL_HG):
                    hf, sl = h // heads_per_lb, slice((h % heads_per_lb) * HD, (h % heads_per_lb + 1) * HD)
                    acc = al[0][:, h:h + 1] * o_tok[0][hf][rows, sl]
                    for gg in range(1, DIL_G):
                        acc = acc + al[gg][:, h:h + 1] * o_tok[gg][hf][rows, sl]
                    ob_ref[rows, h * HD:(h + 1) * HD] = acc
                return carry

            lax.fori_loop(0, S // COMB_T, chunk, 0)

    return pl.pallas_call(
        body, name="dil_combine", grid=(DIL_G,),
        in_specs=[pl.BlockSpec((None, S, DIL_OUT), lambda g: (g, 0, 0)),
                  pl.BlockSpec((None, S, LANES), lambda g: (g, 0, 0))],
        out_specs=[pl.BlockSpec((S, DIL_OUT), lambda g: (0, 0)),
                   pl.BlockSpec((DIL_G, S, LANES), lambda g: (0, 0, 0))],
        out_shape=[jax.ShapeDtypeStruct((S, DIL_OUT), F32), jax.ShapeDtypeStruct((DIL_G, S, LANES), F32)],
        scratch_shapes=[pltpu.VMEM((S, LANES), F32)] * (DIL_G * (LB_PER_CB + 1)),
        compiler_params=_cparams(("arbitrary",)),
    )(o3, lse3)


def _dil_combine_bwd(dob, ob, alpha, deps=()):
    heads_per_lb = LANES // HD

    def body(dob_ref, ob_ref, al_ref, *refs):
        do_ref, c_ref = refs[len(deps):]
        g = pl.program_id(0)
        hf = pl.program_id(1)
        for gs in range(DIL_G):
            @pl.when(g == gs)
            def _(gs=gs):
                for tok, sub in _residue_pieces(DIL_R[gs]):
                    dob = dob_ref[tok, :]
                    prod = dob * ob_ref[tok, :]
                    al = al_ref[tok, :]
                    lane = lax.broadcasted_iota(jnp.int32, al.shape, 1)
                    c = jnp.where(hf == 0, 0.0, c_ref[sub, :])
                    for hh in range(heads_per_lb):
                        sl = slice(hh * HD, (hh + 1) * HD)
                        head = hf * heads_per_lb + hh
                        a = jnp.sum(jnp.where(lane == head, al, 0.0), axis=-1, keepdims=True)
                        do_ref[sub, sl] = (a * dob[:, sl]).astype(BF16)
                        c = jnp.where(lane == head, a * jnp.sum(prod[:, sl], axis=-1, keepdims=True), c)
                    c_ref[sub, :] = c

    half = pl.BlockSpec((S, LANES), lambda g, hf: (0, hf))
    return pl.pallas_call(
        body, name="dil_combine_bwd", grid=(DIL_G, LB_PER_CB),
        in_specs=[half, half, pl.BlockSpec((None, S, LANES), lambda g, hf: (g, 0, 0))] + [ANY] * len(deps),
        out_specs=[pl.BlockSpec((None, S, LANES), lambda g, hf: (g, 0, hf)),
                   pl.BlockSpec((None, S, LANES), lambda g, hf: (g, 0, 0))],
        out_shape=[jax.ShapeDtypeStruct((DIL_G, S, DIL_OUT), BF16), jax.ShapeDtypeStruct((DIL_G, S, LANES), F32)],
        compiler_params=_cparams(("parallel", "arbitrary")),
    )(dob, ob, alpha, *deps)


def _local_step(x, tgt, g_attn, g_mlp, g_final, b_pad, w_in_t, hooks):
    tables = _rope_tables()

    h1 = _rms_fwd(x, g_attn, "rms_attn_fwd", deps=hooks.first_deps())
    proj_a = _mm(h1, w_in_t, mode="nt", tm=S, tn=512, n=3 * FOX_W, out_dtypes=[BF16], name="proj_a")
    proj_b = _mm(h1, w_in_t, mode="nt", tm=S, tn=CB, n=3 * DIL_W, b_off=SEG_B // CB, out_dtypes=[F32], name="proj_b")
    proj_g = _mm(h1, w_in_t, mode="nt", tm=S, tn=CB, n=2 * D, b_off=SEG_G // CB, out_dtypes=[F32], name="proj_g")
    proj_f = _mm(h1, w_in_t, mode="nt", tm=S, tn=LANES, n=LANES, b_off=SEG_F // LANES, out_dtypes=[F32], name="proj_f")

    fcol, frow = _fox_scan(proj_f, b_pad)
    oa, lse_a = _fox_fwd(proj_a, fcol, frow)

    qkv3 = _rope_split(proj_b, tables)
    o3, lse3 = _dil_fwd(qkv3)
    ob, alpha = _dil_combine(o3, lse3)

    w_a, w_b, w_out = hooks.mixer_weights(ob)
    ya = _mm(oa, w_a, mode="nn", tm=S, tn=CB, out_dtypes=[F32], name="branch_a")
    yb = _mm(ob, w_b, mode="nn", tm=S, tn=CB, out_dtypes=[F32], name="branch_b")
    mixed = _gate_fwd(proj_g, ya, yb)
    x2 = _mm(mixed, w_out, mode="nn", tm=S, tn=CB, out_dtypes=[F32], name="out_proj",
             epilogue=lambda acc, res: (res + acc,), extras=(x,))

    h2 = _rms_fwd(x2, g_mlp, "rms_mlp_fwd")
    w_up_sh, w_down = hooks.mlp_weights(h2)

    def up_epilogue(acc):
        r = jnp.maximum(acc, 0.0)
        return acc, r * r

    u, act = _mm(h2, w_up_sh, mode="nn", tm=S, tn=DFF // N_DEV, b_sharded=True, out_dtypes=[F32, BF16],
                 name="mlp_up", epilogue=up_epilogue)
    x3 = _mm(act, w_down, mode="nn", tm=512, tn=512, out_dtypes=[F32], name="mlp_down",
             epilogue=lambda acc, res: (res + acc,), extras=(x2,))

    dx3, dg_final, loss = _final_norm_loss(x3, g_final, tgt)

    du = _mm(dx3, w_down, mode="nt", tm=S, tn=CB, out_dtypes=[BF16], name="mlp_down_bwd",
             epilogue=lambda acc, u_t: (acc * (2.0 * jnp.maximum(u_t, 0.0)),), extras=(u,))
    dw_down = _mm(act, dx3, mode="tn", tm=512, tn=512, out_dtypes=[F32], name="dw_down")
    dw_up_sh = _mm(h2, du, mode="tn", tm=D, tn=DFF // N_DEV, out_sharded=True, out_dtypes=[F32], name="dw_up")
    dh2 = _mm(du, w_up_sh, mode="nt", tm=512, tn=512, b_sharded=True, out_dtypes=[F32], name="mlp_up_bwd",
              deps=hooks.mlp_grads(dw_up_sh, dw_down))
    dx2, dg_mlp = _rms_bwd(x2, g_mlp, dh2, dx3, "rms_mlp_bwd")

    dmixed = _mm(dx2, w_out, mode="nt", tm=S, tn=CB, out_dtypes=[F32], name="out_proj_bwd")
    dw_out = _mm(mixed, dx2, mode="tn", tm=D, tn=CB, out_dtypes=[F32], name="dw_out")

    dyab, dproj = _gate_bwd(dmixed, proj_g, ya, yb)
    doa = _mm(dyab, w_a, mode="nt", tm=S, tn=CB, k=D, a_off=0, out_dtypes=[BF16], name="branch_a_bwd")
    dw_a = _mm(oa, dyab, mode="tn", tm=FOX_W, tn=CB, n=D, b_off=0, out_dtypes=[F32], name="dw_branch_a")
    dob = _mm(dyab, w_b, mode="nt", tm=S, tn=CB, k=D, a_off=1, out_dtypes=[F32], name="branch_b_bwd")
    dw_b = _mm(ob, dyab, mode="tn", tm=DIL_OUT, tn=CB, n=D, b_off=D // CB, out_dtypes=[F32], name="dw_branch_b")

    dproj, dft, dfs = _fox_bwd(proj_a, fcol, frow, lse_a, doa, dproj, deps=hooks.mixer_grads(dw_a, dw_b, dw_out))
    dproj, db = _fox_dscan(dft, dfs, proj_f, b_pad, dproj)

    do3, c3 = _dil_combine_bwd(dob, ob, alpha, deps=hooks.mid_backward(db))
    dq3, dk3, dv3 = _dil_bwd(qkv3, lse3, do3, c3)
    dproj = _rope_merge_bwd(dq3, dk3, dv3, tables, dproj)

    dw_in_t = _mm(dproj, h1, mode="tn", tm=CB, tn=D, out_dtypes=[F32], name="dw_in")
    dh1 = _mm(dproj, w_in_t, mode="nn", tm=512, tn=CB, out_dtypes=[F32], name="proj_bwd",
              deps=hooks.w_in_grad(dw_in_t))
    dx, dg_attn = _rms_bwd(x, g_attn, dh1, dx2, "rms_attn_bwd")

    return loss, dx, (dg_attn, db, dg_mlp, dg_final)


SHARD_SHAPES = ((D_IN // N_DEV, D), (FOX_W, D // N_DEV), (DIL_OUT, D // N_DEV), (D // N_DEV, D),
                (D, DFF // N_DEV), (DFF // N_DEV, D))
W_NAMES = ("w_in", "w_a", "w_b", "w_out", "w_up", "w_down")
AG_COPIES = 7
HBM = pl.BlockSpec(memory_space=pltpu.HBM)
SEM = pl.BlockSpec(memory_space=pltpu.SEMAPHORE)


def _place():
    return lax.axis_index("x"), lax.axis_index("y"), lax.axis_index("c")


def _all_gather(shards):
    n_w = len(shards)
    shapes = [t.shape for t in shards]

    def body(*refs):
        in_refs, out_refs = refs[:n_w], refs[n_w:2 * n_w]
        stage, cast = refs[2 * n_w:3 * n_w], refs[3 * n_w:4 * n_w]
        send_sems, recv_sems, local_sems, load_sems = refs[4 * n_w:]
        x, y, c = _place()
        me, sibling = (x, y, c), (x, y, 1 - c)
        chips = [(1 - x, y), (x, 1 - y), (1 - x, 1 - y)]

        def slot(i, px, py, pc):
            return out_refs[i].at[4 * px + 2 * py + pc]

        def copy(i, k, block, to, src=None):
            return pltpu.make_async_remote_copy(
                src_ref=slot(i, *block) if src is None else src, dst_ref=slot(i, *block),
                send_sem=send_sems.at[i * AG_COPIES + k], recv_sem=recv_sems.at[i * AG_COPIES + k],
                device_id=to, device_id_type=MESH)

        loads = [pltpu.make_async_copy(in_refs[i], stage[i], load_sems.at[i]) for i in range(n_w)]
        for ld in loads:
            ld.start()
        local, remote = [], []
        for i in range(n_w):
            loads[i].wait()
            cast[i][...] = stage[i][...].astype(BF16)
            local.append(pltpu.make_async_copy(cast[i], slot(i, *me), local_sems.at[i]))
            local[-1].start()
            first = [copy(i, 0, me, sibling, src=cast[i])]
            first += [copy(i, 1 + j, me, (*chip, c), src=cast[i]) for j, chip in enumerate(chips)]
            for cp in first:
                cp.start()
            remote += first
        for i in range(n_w):
            for j, chip in enumerate(chips):
                copy(i, 1 + j, (*chip, c), me).wait_recv()
                remote.append(copy(i, 4 + j, (*chip, c), sibling))
                remote[-1].start()
        for i in range(n_w):
            copy(i, 0, sibling, me).wait_recv()
            for j, chip in enumerate(chips):
                copy(i, 4 + j, (*chip, 1 - c), me).wait_recv()
        for cp in remote:
            cp.wait_send()
        for cp in local:
            cp.wait()

    return pl.pallas_call(
        body, name="all_gather_w_in",
        out_shape=[jax.ShapeDtypeStruct((N_DEV,) + sh, BF16) for sh in shapes],
        in_specs=[ANY] * n_w, out_specs=[ANY] * n_w,
        scratch_shapes=[pltpu.VMEM(sh, F32) for sh in shapes] + [pltpu.VMEM(sh, BF16) for sh in shapes]
                       + [pltpu.SemaphoreType.DMA((n_w * AG_COPIES,)), pltpu.SemaphoreType.DMA((n_w * AG_COPIES,)),
                          pltpu.SemaphoreType.DMA((n_w,)), pltpu.SemaphoreType.DMA((n_w,))],
        compiler_params=_cparams(None),
    )(*shards)


def _place_own(shards):
    n_w = len(shards)
    shapes = [t.shape for t in shards]

    def body(*refs):
        in_refs, out_refs = refs[:n_w], refs[n_w:2 * n_w]
        stage, cast = refs[2 * n_w:3 * n_w], refs[3 * n_w:4 * n_w]
        load_sems, store_sems = refs[4 * n_w:]
        x, y, c = _place()
        loads = [pltpu.make_async_copy(in_refs[i], stage[i], load_sems.at[i]) for i in range(n_w)]
        for ld in loads:
            ld.start()
        stores = []
        for i in range(n_w):
            loads[i].wait()
            cast[i][...] = stage[i][...].astype(BF16)
            stores.append(pltpu.make_async_copy(cast[i], out_refs[i].at[4 * x + 2 * y + c], store_sems.at[i]))
            stores[-1].start()
        for st in stores:
            st.wait()

    return pl.pallas_call(
        body, name="place_own_shards",
        out_shape=[jax.ShapeDtypeStruct((N_DEV,) + sh, BF16) for sh in shapes],
        in_specs=[ANY] * n_w, out_specs=[ANY] * n_w,
        scratch_shapes=[pltpu.VMEM(sh, F32) for sh in shapes] + [pltpu.VMEM(sh, BF16) for sh in shapes]
                       + [pltpu.SemaphoreType.DMA((n_w,)), pltpu.SemaphoreType.DMA((n_w,))],
        compiler_params=_cparams(None),
    )(*shards)


PEER_COPIES = N_DEV - 1
SIBLING_COPIES = 4
CHIP_COPIES = 3


def _peer_copies(_, land_refs, send_sems, recv_sems):
    x, y, c = _place()
    mine = 4 * x + 2 * y + c
    copies = []
    for i, ref in enumerate(land_refs):
        for k in range(1, N_DEV):
            peer = (x ^ (k >> 2), y ^ ((k >> 1) & 1), c ^ (k & 1))
            n = i * PEER_COPIES + k - 1
            copies.append(pltpu.make_async_remote_copy(
                src_ref=ref.at[mine], dst_ref=ref.at[mine], send_sem=send_sems.at[n], recv_sem=recv_sems.at[n],
                device_id=peer, device_id_type=MESH))
    return copies


def _sibling_copies(g_refs, r_refs, send_sems, recv_sems):
    x, y, c = _place()
    return [pltpu.make_async_remote_copy(
        src_ref=g_refs[i].at[2 * k + (1 - c)], dst_ref=r_refs[i].at[k],
        send_sem=send_sems.at[SIBLING_COPIES * i + k], recv_sem=recv_sems.at[SIBLING_COPIES * i + k],
        device_id=(x, y, 1 - c), device_id_type=MESH) for i in range(len(g_refs)) for k in range(SIBLING_COPIES)]


def _chip_copies(p_refs, r_refs, send_sems, recv_sems):
    x, y, c = _place()
    chips = [(1 - x, y), (x, 1 - y), (1 - x, 1 - y)]
    return [pltpu.make_async_remote_copy(
        src_ref=p_refs[i].at[2 * cx + cy], dst_ref=r_refs[i].at[j],
        send_sem=send_sems.at[CHIP_COPIES * i + j], recv_sem=recv_sems.at[CHIP_COPIES * i + j],
        device_id=(cx, cy, c), device_id_type=MESH) for i in range(len(p_refs)) for j, (cx, cy) in enumerate(chips)]


def _in_hbm(a):
    return pltpu.with_memory_space_constraint(a, pltpu.HBM)


def _exchange_start(name, copies_fn, n_copies, srcs, lands, after=()):
    n_s, n_l, n_a = len(srcs), len(lands), len(after)

    def body(*refs):
        outs = refs[n_s + n_l + n_a:]
        for cp in copies_fn(refs[:n_s], refs[n_s:n_s + n_l], outs[0], outs[1]):
            cp.start()
        outs[-1][...] = jnp.zeros_like(outs[-1])

    res = pl.pallas_call(
        body, name=name,
        out_shape=[pltpu.SemaphoreType.DMA((n_copies,)), pltpu.SemaphoreType.DMA((n_copies,))]
                  + [pltpu.HBM(t.shape, t.dtype) for t in (*srcs, *lands)] + [jax.ShapeDtypeStruct((8, LANES), F32)],
        in_specs=[HBM] * (n_s + n_l) + [ANY] * n_a,
        out_specs=[SEM, SEM] + [HBM] * (n_s + n_l) + [pl.BlockSpec(memory_space=pltpu.VMEM)],
        input_output_aliases={i: 2 + i for i in range(n_s + n_l)},
        compiler_params=pltpu.CompilerParams(has_side_effects=pltpu.SideEffectType.DATAFLOW_SIDE_EFFECTING),
    )(*[_in_hbm(t) for t in (*srcs, *lands)], *after)
    return res[0], res[1], list(res[2:2 + n_s]), list(res[2 + n_s:2 + n_s + n_l]), res[-1]


def _exchange_wait(name, copies_fn, started, after):
    send_sems, recv_sems, srcs, lands, _ = started
    n_s, n_l = len(srcs), len(lands)

    def body(*refs):
        for cp in copies_fn(refs[:n_s], refs[n_s:n_s + n_l], refs[n_s + n_l], refs[n_s + n_l + 1]):
            cp.wait_send()
            cp.wait_recv()

    res = pl.pallas_call(
        body, name=name,
        out_shape=[pltpu.HBM(t.shape, t.dtype) for t in (*srcs, *lands)],
        in_specs=[HBM] * (n_s + n_l) + [SEM, SEM, ANY],
        out_specs=[HBM] * (n_s + n_l),
        input_output_aliases={i: i for i in range(n_s + n_l)},
        compiler_params=pltpu.CompilerParams(has_side_effects=pltpu.SideEffectType.DATAFLOW_SIDE_EFFECTING),
    )(*srcs, *lands, send_sems, recv_sems, after)
    return list(res[:n_s]), list(res[n_s:])


def _shard_block(shape):
    rows, cols = shape
    if rows % 256:
        return rows, 256
    return min(rows, 512 if cols <= LANES else 256), cols


def _chip_partial(ids, g_stack, recv1, name):
    shape = g_stack.shape[1:]
    br, bc = _shard_block(shape)

    def body(ids_ref, g_ref, r_ref, pb_ref, own_ref):
        s = g_ref[...] + r_ref[...]
        pb_ref[...] = s.astype(BF16)

        @pl.when(pl.program_id(2) == ids_ref[1])
        def _():
            own_ref[...] = s

    grid_spec = pltpu.PrefetchScalarGridSpec(
        num_scalar_prefetch=1, grid=(shape[0] // br, shape[1] // bc, 4),
        in_specs=[pl.BlockSpec((None, br, bc), lambda r, q, k, ids: (2 * k + ids[0], r, q)),
                  pl.BlockSpec((None, br, bc), lambda r, q, k, ids: (k, r, q))],
        out_specs=[pl.BlockSpec((None, br, bc), lambda r, q, k, ids: (k, r, q)),
                   pl.BlockSpec((br, bc), lambda r, q, k, ids: (r, q))])
    return pl.pallas_call(
        body, name=name, grid_spec=grid_spec,
        out_shape=[jax.ShapeDtypeStruct((4,) + shape, BF16), jax.ShapeDtypeStruct(shape, F32)],
        compiler_params=_cparams(("parallel", "parallel", "arbitrary")),
    )(ids, g_stack, recv1)


def _adamw(w, g, m, v):
    m = ADAM_B1 * m + (1.0 - ADAM_B1) * g
    v = ADAM_B2 * v + (1.0 - ADAM_B2) * (g * g)
    m_hat = m / (1.0 - ADAM_B1 ** ADAM_STEP)
    v_hat = v / (1.0 - ADAM_B2 ** ADAM_STEP)
    delta = -ADAM_LR * (m_hat / (jnp.sqrt(v_hat) + ADAM_EPS) + ADAM_WD * w)
    return delta, m, v


def _reduce_adamw(own, recv2, w, m, v, name, deps=()):
    shape = own.shape
    br, bc = _shard_block(shape)

    def body(own_ref, r_ref, w_ref, m_ref, v_ref, *refs):
        g_ref, d_ref, nm_ref, nv_ref = refs[len(deps):]
        g = own_ref[...]
        for j in range(3):
            g = g + r_ref[j].astype(F32)
        delta, nm, nv = _adamw(w_ref[...], g, m_ref[...], v_ref[...])
        g_ref[...] = g
        d_ref[...] = delta
        nm_ref[...] = nm
        nv_ref[...] = nv

    blk = pl.BlockSpec((br, bc), lambda r, q: (r, q))
    return pl.pallas_call(
        body, name=name, grid=(shape[0] // br, shape[1] // bc),
        in_specs=[blk, pl.BlockSpec((3, br, bc), lambda r, q: (0, r, q)), blk, blk, blk] + [ANY] * len(deps),
        out_specs=[blk] * 4, out_shape=[jax.ShapeDtypeStruct(shape, F32)] * 4,
        compiler_params=_cparams(("parallel", "parallel")),
    )(own, recv2, w, m, v, *deps)


def _small_allreduce_adamw(gvec, w, m, v, deps=()):
    def body(g_ref, w_ref, m_ref, v_ref, *refs):
        go_ref, d_ref, nm_ref, nv_ref, buf, send_sems, recv_sems = refs[len(deps):]
        x, y, c = _place()
        my_slot = 4 * x + 2 * y + c
        buf[my_slot] = g_ref[...]
        copies = []
        for k in range(1, N_DEV):
            px, py, pc = x ^ (k >> 2), y ^ ((k >> 1) & 1), c ^ (k & 1)
            copies.append(pltpu.make_async_remote_copy(
                src_ref=g_ref, dst_ref=buf.at[my_slot], send_sem=send_sems.at[k - 1], recv_sem=recv_sems.at[k - 1],
                device_id=(px, py, pc), device_id_type=MESH))
        for cp in copies:
            cp.start()
        for k in range(1, N_DEV):
            px, py, pc = x ^ (k >> 2), y ^ ((k >> 1) & 1), c ^ (k & 1)
            pltpu.make_async_remote_copy(
                src_ref=g_ref, dst_ref=buf.at[4 * px + 2 * py + pc], send_sem=send_sems.at[k - 1],
                recv_sem=recv_sems.at[k - 1], device_id=(px, py, pc), device_id_type=MESH).wait_recv()
        for cp in copies:
            cp.wait_send()
        g = buf[0]
        for s in range(1, N_DEV):
            g = g + buf[s]
        delta, nm, nv = _adamw(w_ref[...], g, m_ref[...], v_ref[...])
        go_ref[...] = g
        d_ref[...] = delta
        nm_ref[...] = nm
        nv_ref[...] = nv

    vm = pl.BlockSpec(memory_space=pltpu.VMEM)
    return pl.pallas_call(
        body, name="small_allreduce_adamw",
        in_specs=[vm] * 4 + [ANY] * len(deps), out_specs=[vm] * 4,
        out_shape=[jax.ShapeDtypeStruct((SMALL_R, LANES), F32)] * 4,
        scratch_shapes=[pltpu.VMEM((N_DEV, SMALL_R, LANES), F32),
                        pltpu.SemaphoreType.DMA((N_DEV - 1,)), pltpu.SemaphoreType.DMA((N_DEV - 1,))],
    )(gvec, w, m, v, *deps)


def _cols_to_whole(stack):
    rows = stack.shape[1]
    return stack.transpose(1, 0, 2).reshape(rows, -1)


def _whole_to_cols(t):
    rows = t.shape[0]
    return t.reshape(rows, N_DEV, -1).transpose(1, 0, 2)


W_IN_SEGMENTS = ((0, 3 * FOX_W, SEG_A), (3 * FOX_W, 3 * FOX_W + FOX_H, SEG_F),
                 (3 * FOX_W + FOX_H, 3 * FOX_W + FOX_H + 3 * DIL_W, SEG_B), (3 * FOX_W + FOX_H + 3 * DIL_W, D_IN, SEG_G))


def _shard_pieces(p):
    rows = D_IN // N_DEV
    lo, hi = p * rows, (p + 1) * rows
    return [(max(lo, a) - lo, min(hi, b) - lo, pad + max(lo, a) - a)
            for a, b, pad in W_IN_SEGMENTS if max(lo, a) < min(hi, b)]


def _pad_w_in_t(stack):
    wt = stack.reshape(D_IN, D)
    placed = sorted((pad, wt[a:b]) for a, b, pad in W_IN_SEGMENTS)
    parts, row = [], 0
    for pad, piece in placed + [(D_PAD, None)]:
        if pad > row:
            parts.append(jnp.zeros((pad - row, D), wt.dtype))
        if piece is not None:
            parts.append(piece)
            row = pad + piece.shape[0]
    return jnp.concatenate(parts, axis=0)


def _unpad_dw_in_t(dwp):
    return jnp.stack([jnp.concatenate([dwp[pad:pad + b - a] for a, b, pad in _shard_pieces(p)], axis=0)
                      for p in range(N_DEV)])


def _pack_small(g_attn, b, g_mlp, g_final):
    b_rows = jnp.pad(b, ((0, 7), (0, LANES - b.shape[1])))
    return jnp.concatenate([g_attn.reshape(8, LANES), g_mlp.reshape(8, LANES), g_final.reshape(8, LANES), b_rows], axis=0)


def _unpack_small(p):
    return (p[0:8].reshape(1, D), p[24:25, :FOX_H], p[8:16].reshape(1, D), p[16:24].reshape(D))


class _StepComm:
    def __init__(self, ids, w_sh, m_sh, v_sh, after_w_in):
        self.ids = ids
        self.w_sh, self.m_sh, self.v_sh = w_sh, m_sh, v_sh
        self.updates = None
        lands = _place_own(w_sh[1:])
        self.gather_mixer = _exchange_start("gather_mixer_start", _peer_copies, PEER_COPIES * 3, [], lands[:3],
                                            after=(after_w_in,))
        self.gather_mlp = _exchange_start("gather_mlp_start", _peer_copies, PEER_COPIES * 2, [], lands[3:],
                                          after=(self.gather_mixer[-1],))
        self.sibling = {}
        self.chips = {}
        self.own = {}
        self.names = {}

    def _reduce_start(self, group, names, grads, after=()):
        lands = [lax.empty((SIBLING_COPIES,) + t.shape[1:], F32) for t in grads]
        self.sibling[group] = _exchange_start("grad_%s_sibling_start" % group, _sibling_copies,
                                              SIBLING_COPIES * len(grads), grads, lands, after=after)
        self.names[group] = names
        return self.sibling[group][-1]

    def _reduce_mid(self, group, after):
        grads, recv1 = _exchange_wait("grad_%s_sibling_wait" % group, _sibling_copies, self.sibling[group], after)
        parts = [_chip_partial(self.ids, g, r, "grad_partial_" + n) for g, r, n in zip(grads, recv1, self.names[group])]
        self.own[group] = [p[1] for p in parts]
        srcs = [p[0] for p in parts]
        lands = [lax.empty((CHIP_COPIES,) + t.shape[1:], BF16) for t in srcs]
        self.chips[group] = _exchange_start("grad_%s_chips_start" % group, _chip_copies, CHIP_COPIES * len(srcs),
                                            srcs, lands)
        return self.chips[group][-1]

    def reduced(self, group, after):
        _, recv2 = _exchange_wait("grad_%s_chips_wait" % group, _chip_copies, self.chips[group], after)
        return list(zip(self.own[group], recv2))

    def first_deps(self):
        return [self.gather_mlp[-1]]

    def mixer_weights(self, after):
        _, (g_a, g_b, g_out) = _exchange_wait("gather_mixer_wait", _peer_copies, self.gather_mixer, after)
        return _cols_to_whole(g_a), _cols_to_whole(g_b), g_out.reshape(D, D)

    def mlp_weights(self, after):
        _, (g_up, g_down) = _exchange_wait("gather_mlp_wait", _peer_copies, self.gather_mlp, after)
        return g_up, g_down.reshape(DFF, D)

    def mlp_grads(self, dw_up_sh, dw_down):
        return [self._reduce_start("mlp", W_NAMES[4:], [dw_up_sh, dw_down.reshape((N_DEV,) + SHARD_SHAPES[5])])]

    def mixer_grads(self, dw_a, dw_b, dw_out):
        token = self._reduce_mid("mlp", dw_b)
        grads = [_whole_to_cols(dw_a), _whole_to_cols(dw_b), dw_out.reshape((N_DEV,) + SHARD_SHAPES[3])]
        return [self._reduce_start("mixer", W_NAMES[1:4], grads, after=(token,))]

    def mid_backward(self, after):
        return [self._reduce_mid("mixer", after)]

    def w_in_grad(self, dw_in_t):
        grads = [_unpad_dw_in_t(dw_in_t)]
        token = self._reduce_start("w_in", W_NAMES[:1], grads)
        reduced = self.reduced("mixer", token) + self.reduced("mlp", token)
        self.updates = [
            _reduce_adamw(*r, self.w_sh[1 + i], self.m_sh[1 + i], self.v_sh[1 + i], "adamw_" + W_NAMES[1 + i], deps=[token])
            for i, r in enumerate(reduced)]
        return [self._reduce_mid("w_in", self.updates[-1][0])]

    def w_in_update(self, after):
        (reduced,) = self.reduced("w_in", after)
        return _reduce_adamw(*reduced, self.w_sh[0], self.m_sh[0], self.v_sh[0], "adamw_" + W_NAMES[0])


def kernel(x, norm_attn_g, w_in, b_forget, w_branch_a, w_branch_b, w_out, norm_mlp_g, w_up, w_down, norm_final_g, loss_target, m_norm_attn_g, m_w_in, m_b_forget, m_w_branch_a, m_w_branch_b, m_w_out, m_norm_mlp_g, m_w_up, m_w_down, m_norm_final_g, v_norm_attn_g, v_w_in, v_b_forget, v_w_branch_a, v_w_branch_b, v_w_out, v_norm_mlp_g, v_w_up, v_w_down, v_norm_final_g):
    cx, cy, cc = _place()
    ids = jnp.stack([cc, 2 * cx + cy]).astype(jnp.int32)

    w_sh = [w_in[0].T] + [t[0] for t in (w_branch_a, w_branch_b, w_out, w_up, w_down)]
    m_sh = [m_w_in[0].T] + [t[0] for t in (m_w_branch_a, m_w_branch_b, m_w_out, m_w_up, m_w_down)]
    v_sh = [v_w_in[0].T] + [t[0] for t in (v_w_branch_a, v_w_branch_b, v_w_out, v_w_up, v_w_down)]

    (g_in,) = _all_gather(w_sh[:1])
    w_in_t = _pad_w_in_t(g_in)
    comm = _StepComm(ids, w_sh, m_sh, v_sh, g_in)
    b_pad = jnp.pad(b_forget, ((0, 0), (0, LANES - FOX_H)))

    loss_row, dx, dsmall = _local_step(
        x[0], loss_target[0], norm_attn_g, norm_mlp_g, norm_final_g.reshape(1, D), b_pad, w_in_t, comm)
    loss = lax.psum(loss_row[0, 0], ("x", "y", "c"))

    dg_attn, db, dg_mlp, dg_final = dsmall
    small = _small_allreduce_adamw(
        _pack_small(dg_attn, db, dg_mlp, dg_final),
        _pack_small(norm_attn_g, b_forget, norm_mlp_g, norm_final_g.reshape(1, D)),
        _pack_small(m_norm_attn_g, m_b_forget, m_norm_mlp_g, m_norm_final_g.reshape(1, D)),
        _pack_small(v_norm_attn_g, v_b_forget, v_norm_mlp_g, v_norm_final_g.reshape(1, D)))
    big = [comm.w_in_update(small[0])] + comm.updates

    outs = [loss, dx[None]]
    for q in range(4):
        s_attn, s_b, s_mlp, s_final = _unpack_small(small[q])
        b_in, b_a, b_b, b_out, b_up, b_down = [t[q][None] for t in big]
        b_in = jnp.swapaxes(b_in, 1, 2)
        outs += [s_attn, b_in, s_b, b_a, b_b, b_out, s_mlp, b_up, b_down, s_final]
    return tuple(outs)
```

```python
import functools

import jax
import jax.numpy as jnp
from jax import lax
from jax.experimental import pallas as pl
from jax.experimental.pallas import tpu as pltpu

F32 = jnp.float32
BF16 = jnp.bfloat16
MESH = pl.DeviceIdType.MESH

S = 2048
D = 1024
HD = 64
FOX_H = 8
FOX_W = FOX_H * HD
DIL_HG = 4
DIL_G = 3
DIL_W = DIL_G * DIL_HG * HD
DIL_OUT = DIL_HG * HD
DIL_BLK = 128
DIL_R = (1, 4, 16)
DFF = 4 * D
D_IN = 3 * FOX_W + FOX_H + 3 * DIL_W + 2 * D
EPS = 1e-6
NEG_INF = -1e30
SCALE = HD ** -0.5
ROPE_THETA = 500000.0
ROPE_DIM = HD // 4
N_DEV = 8

ADAM_LR = 0.001
ADAM_B1 = 0.9
ADAM_B2 = 0.999
ADAM_EPS = 1e-08
ADAM_WD = 0.01
ADAM_STEP = 10

LANES = 128
CB = 256
SEG_A = 0
SEG_B = 3 * FOX_W
SEG_Z = SEG_B + 3 * DIL_W
SEG_G = SEG_Z + CB
SEG_F = SEG_G + 2 * D
D_PAD = SEG_F + CB
SMALL_R = 32

VMEM_MB = 56


def _cparams(dims=None, vmem_mb=VMEM_MB, **kw):
    return pltpu.CompilerParams(dimension_semantics=dims, vmem_limit_bytes=vmem_mb << 20, **kw)


ANY = pl.BlockSpec(memory_space=pl.ANY)

_NN = (((1,), (0,)), ((), ()))
_NT = (((1,), (1,)), ((), ()))
_TN = (((0,), (0,)), ((), ()))


def _dot(a, b, dims):
    return lax.dot_general(a.astype(BF16), b.astype(BF16), dims, preferred_element_type=F32)


def _mm(a, b, *, mode, tm, tn, out_dtypes, name, n=None, k=None, a_off=0, b_off=0,
        b_sharded=False, out_sharded=False, epilogue=None, extras=(), deps=()):
    n_sh = b.shape[-1] if b_sharded else None
    if mode == "nn":
        m = a.shape[0]
        k = k or a.shape[1]
        a_spec = pl.BlockSpec((tm, k), lambda i, j: (i, a_off))
        if b_sharded:
            assert tn == n_sh
            n = N_DEV * n_sh
            b_spec = pl.BlockSpec((None, k, tn), lambda i, j: (j, 0, 0))
        else:
            n = n or b.shape[1]
            b_spec = pl.BlockSpec((k, tn), lambda i, j: (0, j + b_off))
        dims = _NN
    elif mode == "nt":
        m = a.shape[0]
        k = k or a.shape[1]
        a_spec = pl.BlockSpec((tm, k), lambda i, j: (i, a_off))
        if b_sharded:
            n = b.shape[1]
            b_spec = pl.BlockSpec((N_DEV, tn, n_sh), lambda i, j: (0, j, 0))
        else:
            n = n or b.shape[0]
            b_spec = pl.BlockSpec((tn, k), lambda i, j: (j + b_off, 0))
        dims = _NT
    else:
        k, m = a.shape
        n = n or b.shape[1]
        a_spec = pl.BlockSpec((k, tm), lambda i, j: (0, i))
        b_spec = pl.BlockSpec((k, tn), lambda i, j: (0, j + b_off))
        dims = _TN
    assert m % tm == 0 and n % tn == 0, (name, m, n, tm, tn)
    n_extra = len(extras)
    tile = pl.BlockSpec((tm, tn), lambda i, j: (i, j))
    if out_sharded:
        assert mode == "tn" and n // tn == N_DEV
        out_spec = pl.BlockSpec((None, tm, tn), lambda i, j: (j, i, 0))
        out_shape = (N_DEV, m, tn)
    else:
        out_spec, out_shape = tile, (m, n)

    def body(a_ref, b_ref, *refs):
        if mode == "nt" and b_sharded:
            acc = _dot(a_ref[:, 0:n_sh], b_ref[0], dims)
            for p in range(1, N_DEV):
                acc = acc + _dot(a_ref[:, p * n_sh:(p + 1) * n_sh], b_ref[p], dims)
        else:
            acc = _dot(a_ref[...], b_ref[...], dims)
        ex = [r[...] for r in refs[:n_extra]]
        outs = epilogue(acc, *ex) if epilogue is not None else (acc,)
        for o_ref, o in zip(refs[n_extra + len(deps):], outs):
            o_ref[...] = o.astype(o_ref.dtype)

    res = pl.pallas_call(
        body, name=name, grid=(m // tm, n // tn),
        in_specs=[a_spec, b_spec] + [tile] * n_extra + [ANY] * len(deps),
        out_specs=[out_spec] * len(out_dtypes),
        out_shape=[jax.ShapeDtypeStruct(out_shape, dt) for dt in out_dtypes],
        compiler_params=_cparams(("parallel", "parallel")),
    )(a, b, *extras, *deps)
    return res if len(out_dtypes) > 1 else res[0]


ROW_T = 256


def _rms_fwd(x, g, name, deps=()):
    def body(x_ref, g_ref, *refs):
        o_ref = refs[-1]
        x = x_ref[...]
        r = lax.rsqrt(jnp.mean(x * x, axis=-1, keepdims=True) + EPS)
        o_ref[...] = ((x * r) * g_ref[...]).astype(BF16)

    return pl.pallas_call(
        body, name=name, grid=(S // ROW_T,),
        in_specs=[pl.BlockSpec((ROW_T, D), lambda i: (i, 0)), pl.BlockSpec((1, D), lambda i: (0, 0))] + [ANY] * len(deps),
        out_specs=pl.BlockSpec((ROW_T, D), lambda i: (i, 0)),
        out_shape=jax.ShapeDtypeStruct((S, D), BF16),
        compiler_params=_cparams(("parallel",)),
    )(x, g, *deps)


def _rms_bwd(x, g, dh, dres, name):
    def body(x_ref, g_ref, dh_ref, dres_ref, dx_ref, dg_ref):
        @pl.when(pl.program_id(0) == 0)
        def _():
            dg_ref[...] = jnp.zeros_like(dg_ref)

        x = x_ref[...]
        dh = dh_ref[...]
        r = lax.rsqrt(jnp.mean(x * x, axis=-1, keepdims=True) + EPS)
        xn = x * r
        dhn = dh * g_ref[...]
        dx_ref[...] = dres_ref[...] + r * (dhn - xn * jnp.mean(dhn * xn, axis=-1, keepdims=True))
        dg_ref[...] += jnp.sum(dh * xn, axis=0, keepdims=True)

    row = pl.BlockSpec((ROW_T, D), lambda i: (i, 0))
    vec = pl.BlockSpec((1, D), lambda i: (0, 0))
    return pl.pallas_call(
        body, name=name, grid=(S // ROW_T,),
        in_specs=[row, vec, row, row], out_specs=[row, vec],
        out_shape=[jax.ShapeDtypeStruct((S, D), F32), jax.ShapeDtypeStruct((1, D), F32)],
        compiler_params=_cparams(("arbitrary",)),
    )(x, g, dh, dres)


def _final_norm_loss(x, g, tgt):
    def body(x_ref, g_ref, t_ref, dx_ref, dg_ref, loss_ref):
        @pl.when(pl.program_id(0) == 0)
        def _():
            dg_ref[...] = jnp.zeros_like(dg_ref)
            loss_ref[...] = jnp.zeros_like(loss_ref)

        x = x_ref[...]
        g = g_ref[...]
        r = lax.rsqrt(jnp.mean(x * x, axis=-1, keepdims=True) + EPS)
        xn = x * r
        err = xn * g - t_ref[...]
        row_loss = jnp.mean(err * err, axis=-1, keepdims=True)
        loss_ref[...] += 0.5 * jnp.sum(row_loss, axis=0, keepdims=True) * jnp.ones((1, LANES), F32)
        dy = err * (1.0 / D)
        dyn = dy * g
        dx_ref[...] = r * (dyn - xn * jnp.mean(dyn * xn, axis=-1, keepdims=True))
        dg_ref[...] += jnp.sum(dy * xn, axis=0, keepdims=True)

    row = pl.BlockSpec((ROW_T, D), lambda i: (i, 0))
    vec = pl.BlockSpec((1, D), lambda i: (0, 0))
    return pl.pallas_call(
        body, name="final_norm_loss", grid=(S // ROW_T,),
        in_specs=[row, vec, row],
        out_specs=[row, vec, pl.BlockSpec((1, LANES), lambda i: (0, 0))],
        out_shape=[jax.ShapeDtypeStruct((S, D), F32), jax.ShapeDtypeStruct((1, D), F32),
                   jax.ShapeDtypeStruct((1, LANES), F32)],
        compiler_params=_cparams(("arbitrary",)),
    )(x, g, tgt)


def _sigmoid(z):
    return 1.0 / (1.0 + jnp.exp(-z))


def _gate_fwd(proj_g, ya, yb):
    def body(ga_ref, gb_ref, ya_ref, yb_ref, o_ref):
        o_ref[...] = (_sigmoid(ga_ref[...]) * ya_ref[...] + _sigmoid(gb_ref[...]) * yb_ref[...]).astype(BF16)

    row = pl.BlockSpec((ROW_T, D), lambda i: (i, 0))
    return pl.pallas_call(
        body, name="gate_fwd", grid=(S // ROW_T,),
        in_specs=[row, pl.BlockSpec((ROW_T, D), lambda i: (i, 1)), row, row],
        out_specs=row, out_shape=jax.ShapeDtypeStruct((S, D), BF16),
        compiler_params=_cparams(("parallel",)),
    )(proj_g, proj_g, ya, yb)


GATE_TR = 1024
GATE_TC = 512
GATE_NB = 2 * D // GATE_TC


def _gate_bwd(dmixed, proj_g, ya, yb):
    half = GATE_NB // 2

    def body(dm_ref, g_ref, ya_ref, yb_ref, dy_ref, dg_ref):
        dm = dm_ref[...]
        s = _sigmoid(g_ref[...])
        y = jnp.where(pl.program_id(1) < half, ya_ref[...], yb_ref[...])
        dy_ref[...] = (dm * s).astype(BF16)
        dg_ref[...] = (dm * y * (s * (1.0 - s))).astype(BF16)

    blk = pl.BlockSpec((GATE_TR, GATE_TC), lambda i, j: (i, j))
    wrapped = pl.BlockSpec((GATE_TR, GATE_TC), lambda i, j: (i, j % half))
    return pl.pallas_call(
        body, name="gate_bwd", grid=(S // GATE_TR, GATE_NB),
        in_specs=[wrapped, blk, wrapped, wrapped],
        out_specs=[blk, pl.BlockSpec((GATE_TR, GATE_TC), lambda i, j: (i, j + SEG_G // GATE_TC))],
        out_shape=[jax.ShapeDtypeStruct((S, 2 * D), BF16), jax.ShapeDtypeStruct((S, D_PAD), BF16)],
        compiler_params=_cparams(("parallel", "parallel")),
    )(dmixed, proj_g, ya, yb)


FOX_TQ = 256


def _scan_rows(x, reverse):
    n = x.shape[0]
    row = lax.broadcasted_iota(jnp.int32, x.shape, 0)
    k = 1
    while k < n:
        if reverse:
            x = x + jnp.where(row < n - k, pltpu.roll(x, n - k, 0), 0.0)
        else:
            x = x + jnp.where(row >= k, pltpu.roll(x, k, 0), 0.0)
        k *= 2
    return x


def _fox_dscan(dft, dfs, proj_f, b_pad, dproj):
    def body(dft_ref, dfs_ref, f_ref, b_ref, _, dfa_ref, db_ref):
        dfs_pad = jnp.concatenate([dfs_ref[...], jnp.zeros((LANES - FOX_H, S), F32)], axis=0)
        df = dfs_pad.T + dft_ref[0]
        for hp in range(1, dft_ref.shape[0]):
            df = df + pltpu.roll(dft_ref[hp], 2 * hp, 1)
        dlf = _scan_rows(df, reverse=True)
        z = f_ref[...] + b_ref[...]
        lane = lax.broadcasted_iota(jnp.int32, (S, LANES), 1)
        dfa = jnp.where(lane < FOX_H, dlf / (1.0 + jnp.exp(z)), 0.0)
        dfa_ref[:, :LANES] = dfa.astype(BF16)
        dfa_ref[:, LANES:] = jnp.zeros((S, CB - LANES), BF16)
        db_ref[...] = jnp.sum(dfa, axis=0, keepdims=True)

    return pl.pallas_call(
        body, name="fox_dscan", grid=(1,),
        in_specs=[pl.BlockSpec(dft.shape, lambda i: (0, 0, 0)), pl.BlockSpec((FOX_H, S), lambda i: (0, 0)),
                  pl.BlockSpec((S, LANES), lambda i: (0, 0)), pl.BlockSpec((1, LANES), lambda i: (0, 0)), ANY],
        out_specs=[pl.BlockSpec((S, CB), lambda i: (0, SEG_F // CB)), pl.BlockSpec((1, LANES), lambda i: (0, 0))],
        out_shape=[jax.ShapeDtypeStruct((S, D_PAD), BF16), jax.ShapeDtypeStruct((1, LANES), F32)],
        input_output_aliases={4: 0},
        compiler_params=_cparams(("arbitrary",)),
    )(dft, dfs, proj_f, b_pad, dproj)


FOX_NQ = S // FOX_TQ
FOX_HP = FOX_W // LANES
HEADS_PER_LB = LANES // HD
FOX_AUG = 2 * LANES


def _fox_prepare(proj_a, proj_f, b_pad):
    def split3(x):
        hi = x.astype(BF16)
        r1 = x - hi.astype(F32)
        mid = r1.astype(BF16)
        lo = (r1 - mid.astype(F32)).astype(BF16)
        return hi.astype(F32), mid.astype(F32), lo.astype(F32)

    def body(q_ref, k_ref, f_ref, b_ref, qa_ref, ka_ref):
        z = f_ref[...] + b_ref[...]
        lf = jnp.minimum(z, 0.0) - jnp.log1p(jnp.exp(-jnp.abs(z)))
        f = _scan_rows(lf, reverse=False)
        lane = lax.broadcasted_iota(jnp.int32, (S, HD), 1)
        one = jnp.ones((S, HD), F32)
        for h in range(FOX_H):
            a1, a2, a3 = split3(f[:, h:h + 1])
            q_extra = jnp.where(lane == 0, a1, jnp.where(lane == 1, a2, jnp.where(lane == 2, a3, one)))
            k_extra = jnp.where(lane == 3, -a1, jnp.where(lane == 4, -a2, jnp.where(lane == 5, -a3, one)))
            used = lane < 6
            base = h * LANES
            qa_ref[:, base:base + HD] = (q_ref[:, h * HD:(h + 1) * HD].astype(F32) * SCALE).astype(BF16)
            qa_ref[:, base + HD:base + LANES] = jnp.where(used, q_extra, 0.0).astype(BF16)
            ka_ref[:, base:base + HD] = k_ref[:, h * HD:(h + 1) * HD]
            ka_ref[:, base + HD:base + LANES] = jnp.where(used, k_extra, 0.0).astype(BF16)

    whole = pl.BlockSpec((S, FOX_H * LANES), lambda i: (0, 0))
    return pl.pallas_call(
        body, name="fox_prepare", grid=(1,),
        in_specs=[pl.BlockSpec((S, FOX_W), lambda i: (0, 0)), pl.BlockSpec((S, FOX_W), lambda i: (0, 1)),
                  pl.BlockSpec((S, LANES), lambda i: (0, 0)), pl.BlockSpec((1, LANES), lambda i: (0, 0))],
        out_specs=[whole, whole],
        out_shape=[jax.ShapeDtypeStruct((S, FOX_H * LANES), BF16)] * 2,
        compiler_params=_cparams(("arbitrary",)),
    )(proj_a, proj_a, proj_f, b_pad)


def _fox_scores(qa, ka, w):
    s = _dot(qa, ka, _NT)
    row = lax.broadcasted_iota(jnp.int32, (FOX_TQ, FOX_TQ), 0)
    col = lax.broadcasted_iota(jnp.int32, (FOX_TQ, FOX_TQ), 1)
    diag = jnp.where(col <= row, s[:, w * FOX_TQ:], NEG_INF)
    return diag if w == 0 else jnp.concatenate([s[:, :w * FOX_TQ], diag], axis=1)


def _fox_fwd(q_aug, k_aug, proj_a):
    def body(qa_ref, ka_ref, v_ref, o_ref, lse_ref):
        qi = pl.program_id(1)
        lane = lax.broadcasted_iota(jnp.int32, (FOX_TQ, LANES), 1)
        for w in range(FOX_NQ):
            @pl.when(qi == w)
            def _(w=w):
                width = (w + 1) * FOX_TQ
                lse = jnp.zeros((FOX_TQ, LANES), F32)
                for hh in range(HEADS_PER_LB):
                    aug = slice(hh * LANES, (hh + 1) * LANES)
                    sl = slice(hh * HD, (hh + 1) * HD)
                    s = _fox_scores(qa_ref[:, aug], ka_ref[:width, aug], w)
                    m = jnp.max(s, axis=-1, keepdims=True)
                    p = jnp.exp(s - m)
                    l = jnp.sum(p, axis=-1, keepdims=True)
                    o_ref[:, sl] = _dot(p / l, v_ref[:width, sl], _NN)
                    lse = jnp.where(lane == hh, m + jnp.log(l), lse)
                lse_ref[...] = lse

    return pl.pallas_call(
        body, name="fox_fwd", grid=(FOX_HP, FOX_NQ),
        in_specs=[pl.BlockSpec((FOX_TQ, FOX_AUG), lambda h, i: (i, h)),
                  pl.BlockSpec((S, FOX_AUG), lambda h, i: (0, h)),
                  pl.BlockSpec((S, LANES), lambda h, i: (0, 2 * FOX_HP + h))],
        out_specs=[pl.BlockSpec((FOX_TQ, LANES), lambda h, i: (i, h)),
                   pl.BlockSpec((None, FOX_TQ, LANES), lambda h, i: (h, i, 0))],
        out_shape=[jax.ShapeDtypeStruct((S, FOX_W), F32), jax.ShapeDtypeStruct((FOX_HP, S, LANES), F32)],
        compiler_params=_cparams(("parallel", "parallel")),
    )(q_aug, k_aug, proj_a)


def _fox_bwd(q_aug, k_aug, proj_a, lse, do, dproj, deps=()):
    n_q = FOX_NQ

    def body(qa_ref, ka_ref, v_ref, lse_ref, do_ref, *refs):
        dproj_ref, dft_ref, dfs_ref, dk_acc, dv_acc, dq_buf, kv_buf, dq_sems, kv_sems = refs[1 + len(deps):]
        hp = pl.program_id(0)
        t = pl.program_id(1)
        slot = t % 2
        col = pl.multiple_of(hp * LANES, LANES)

        def dq_copy(step, buf_slot):
            rows = pl.ds(pl.multiple_of(step * FOX_TQ, FOX_TQ), FOX_TQ)
            return pltpu.make_async_copy(dq_buf.at[buf_slot], dproj_ref.at[rows, pl.ds(col, LANES)], dq_sems.at[buf_slot])

        @pl.when(t == 0)
        def _():
            dk_acc[...] = jnp.zeros_like(dk_acc)
            dv_acc[...] = jnp.zeros_like(dv_acc)

        @pl.when((t == 0) & (hp == 0))
        def _():
            dfs_ref[...] = jnp.zeros_like(dfs_ref)

        @pl.when(t >= 2)
        def _():
            dq_copy(t - 2, slot).wait()

        lane = lax.broadcasted_iota(jnp.int32, (FOX_TQ, LANES), 1)
        for w in range(n_q):
            @pl.when(t == w)
            def _(w=w):
                width = (w + 1) * FOX_TQ
                lse_all = lse_ref[...]
                sub = lax.broadcasted_iota(jnp.int32, (FOX_H, width), 0)
                dft = jnp.zeros((FOX_TQ, LANES), F32)
                for hh in range(HEADS_PER_LB):
                    aug = slice(hh * LANES, (hh + 1) * LANES)
                    sl = slice(hh * HD, (hh + 1) * HD)
                    head = hp * HEADS_PER_LB + hh
                    qa = qa_ref[:, aug]
                    ka = ka_ref[:width, aug]
                    do_h = do_ref[:, sl]
                    s = _fox_scores(qa, ka, w)
                    p = jnp.exp(s - lse_all[:, hh:hh + 1])
                    dp = _dot(do_h, v_ref[:width, sl], _NT)
                    ds = p * (dp - jnp.sum(dp * p, axis=-1, keepdims=True))
                    dft = jnp.where(lane == hh, jnp.sum(ds, axis=-1, keepdims=True), dft)
                    dfs_ref[:, :width] -= jnp.where(sub == head, jnp.sum(ds, axis=0, keepdims=True), 0.0)
                    dq_buf[slot, :, sl] = (_dot(ds, ka[:, :HD], _NN) * SCALE).astype(BF16)
                    dk_acc[:width, sl] += _dot(ds, qa[:, :HD], _TN)
                    dv_acc[:width, sl] += _dot(p, do_h, _TN)
                dft_ref[...] = dft

        dq_copy(t, slot).start()

        @pl.when(t == n_q - 1)
        def _():
            dq_copy(t - 1, 1 - slot).wait()
            dq_copy(t, slot).wait()
            kv_buf[0] = dk_acc[...].astype(BF16)
            kv_buf[1] = dv_acc[...].astype(BF16)
            copies = [pltpu.make_async_copy(
                kv_buf.at[j], dproj_ref.at[:, pl.ds(pl.multiple_of((j + 1) * FOX_W + hp * LANES, LANES), LANES)],
                kv_sems.at[j]) for j in range(2)]
            for cp in copies:
                cp.start()
            for cp in copies:
                cp.wait()

    qblk = pl.BlockSpec((FOX_TQ, FOX_AUG), lambda h, t: (t, h))
    lane_blk = pl.BlockSpec((None, FOX_TQ, LANES), lambda h, t: (h, t, 0))
    return pl.pallas_call(
        body, name="fox_bwd", grid=(FOX_HP, n_q),
        in_specs=[qblk, pl.BlockSpec((S, FOX_AUG), lambda h, t: (0, h)),
                  pl.BlockSpec((S, LANES), lambda h, t: (0, 2 * FOX_HP + h)),
                  lane_blk, pl.BlockSpec((FOX_TQ, LANES), lambda h, t: (t, h)), ANY] + [ANY] * len(deps),
        out_specs=[ANY, lane_blk, pl.BlockSpec((FOX_H, S), lambda h, t: (0, 0))],
        out_shape=[jax.ShapeDtypeStruct((S, D_PAD), BF16),
                   jax.ShapeDtypeStruct((FOX_HP, S, LANES), F32), jax.ShapeDtypeStruct((FOX_H, S), F32)],
        scratch_shapes=[pltpu.VMEM((S, LANES), F32), pltpu.VMEM((S, LANES), F32),
                        pltpu.VMEM((2, FOX_TQ, LANES), BF16), pltpu.VMEM((2, S, LANES), BF16),
                        pltpu.SemaphoreType.DMA((2,)), pltpu.SemaphoreType.DMA((2,))],
        input_output_aliases={5: 0},
        compiler_params=_cparams(("arbitrary", "arbitrary")),
    )(q_aug, k_aug, proj_a, lse, do, dproj, *deps)


def _rope_tables():
    half = ROPE_DIM // 2
    inv_freq = jnp.power(jnp.float32(ROPE_THETA), -jnp.arange(half, dtype=F32) * 2.0 / ROPE_DIM)
    ang = jnp.arange(S).astype(F32)[:, None] * inv_freq[None, :]
    cos, sin = jnp.cos(ang), jnp.sin(ang)
    z = jnp.zeros((S, half), F32)
    rest = HD - ROPE_DIM
    t_self = jnp.concatenate([cos, cos, jnp.ones((S, rest), F32)], axis=1)
    t_up = jnp.concatenate([-sin, z, jnp.zeros((S, rest), F32)], axis=1)
    t_dn = jnp.concatenate([z, sin, jnp.zeros((S, rest), F32)], axis=1)
    rep = LANES // HD

    def by_residue(t):
        return jnp.stack([t.reshape(S // r, r, LANES).transpose(1, 0, 2).reshape(S, LANES) for r in DIL_R])

    return tuple(by_residue(jnp.tile(t, (1, rep))) for t in (t_self, t_up, t_dn))


def _residue_pieces(r):
    if r == 1:
        return [(slice(i, i + 512), slice(i, i + 512)) for i in range(0, S, 512)]
    n = S // r
    return [(pl.ds(j, n, stride=r), slice(j * n, (j + 1) * n)) for j in range(r)]


def _rope_apply(x, a, u, d, backward):
    half = ROPE_DIM // 2
    if backward:
        return x * a + pltpu.roll(x * u, half, 1) + pltpu.roll(x * d, LANES - half, 1)
    return x * a + pltpu.roll(x, LANES - half, 1) * u + pltpu.roll(x, half, 1) * d


LB_PER_CB = CB // LANES
ROPE_NB = 3 * DIL_G * LB_PER_CB


def _rope_step(s):
    per_group = 3 * LB_PER_CB
    return s // per_group, (s % per_group) // LB_PER_CB, s % LB_PER_CB


def _rope_split(proj_b, tables):
    def body(x_ref, a_ref, u_ref, d_ref, o_ref):
        g, t, _ = _rope_step(pl.program_id(0))
        for gs in range(DIL_G):
            @pl.when(g == gs)
            def _(gs=gs):
                for tok, sub in _residue_pieces(DIL_R[gs]):
                    x = x_ref[tok, :]
                    y = _rope_apply(x, a_ref[sub, :], u_ref[sub, :], d_ref[sub, :], False)
                    o_ref[sub, :] = jnp.where(t < 2, y, x).astype(BF16)

    def in_index(s):
        g, t, hf = _rope_step(s)
        return 0, (t * DIL_G + g) * LB_PER_CB + hf

    def out_index(s):
        g, t, hf = _rope_step(s)
        return t * DIL_G + g, 0, hf

    tab = pl.BlockSpec((None, S, LANES), lambda s: (_rope_step(s)[0], 0, 0))
    return pl.pallas_call(
        body, name="rope_split", grid=(ROPE_NB,),
        in_specs=[pl.BlockSpec((S, LANES), in_index), tab, tab, tab],
        out_specs=pl.BlockSpec((None, S, LANES), out_index),
        out_shape=jax.ShapeDtypeStruct((3 * DIL_G, S, CB), BF16),
        compiler_params=_cparams(("parallel",)),
    )(proj_b, *tables)


def _rope_merge_bwd(dq3, dk3, dv3, tables, dproj):
    def body(dq_ref, dk_ref, dv_ref, a_ref, u_ref, d_ref, _, o_ref, tmp):
        s = pl.program_id(0)
        g, t, _ = _rope_step(s)
        for gs in range(DIL_G):
            @pl.when((g == gs) & (s < ROPE_NB))
            def _(gs=gs):
                for tok, sub in _residue_pieces(DIL_R[gs]):
                    x = jnp.where(t == 0, dq_ref[sub, :], jnp.where(t == 1, dk_ref[sub, :], dv_ref[sub, :]))
                    y = _rope_apply(x, a_ref[sub, :], u_ref[sub, :], d_ref[sub, :], True)
                    tmp[tok, :] = jnp.where(t < 2, y, x)
                o_ref[...] = tmp[...].astype(BF16)

        @pl.when(s >= ROPE_NB)
        def _():
            o_ref[...] = jnp.zeros_like(o_ref)

    def src_spec(own):
        def index(s):
            g, t, hf = _rope_step(jnp.minimum(s, ROPE_NB - 1))
            return g, 0, jnp.where(t == own, hf, jnp.where(t > own, LB_PER_CB - 1, 0))
        return pl.BlockSpec((None, S, LANES), index)

    def out_index(s):
        g, t, hf = _rope_step(s)
        col = SEG_B // LANES + (t * DIL_G + g) * LB_PER_CB + hf
        return 0, jnp.where(s < ROPE_NB, col, SEG_Z // LANES + s - ROPE_NB)

    tab = pl.BlockSpec((None, S, LANES), lambda s: (_rope_step(jnp.minimum(s, ROPE_NB - 1))[0], 0, 0))
    return pl.pallas_call(
        body, name="rope_merge_bwd", grid=(ROPE_NB + LB_PER_CB,),
        in_specs=[src_spec(0), src_spec(1), src_spec(2), tab, tab, tab, ANY],
        out_specs=pl.BlockSpec((S, LANES), out_index),
        out_shape=jax.ShapeDtypeStruct((S, D_PAD), BF16),
        scratch_shapes=[pltpu.VMEM((S, LANES), F32)],
        input_output_aliases={6: 0},
        compiler_params=_cparams(("arbitrary",)),
    )(dq3, dk3, dv3, *tables, dproj)


DIL_NB = S // DIL_BLK


_BQK = (((2,), (2,)), ((0,), (0,)))
_BQD = (((2,), (1,)), ((0,), (0,)))
_BKD = (((1,), (1,)), ((0,), (0,)))


def _dil_mask(g):
    shape = (DIL_NB, DIL_BLK, 2 * DIL_BLK)
    blocks_per_seq = jnp.right_shift(DIL_NB, 2 * g)
    has_prev = jnp.bitwise_and(lax.broadcasted_iota(jnp.int32, shape, 0), blocks_per_seq - 1) != 0
    a = lax.broadcasted_iota(jnp.int32, shape, 1)
    kk = lax.broadcasted_iota(jnp.int32, shape, 2)
    diff = DIL_BLK + a - kk
    return (diff >= 0) & (diff <= DIL_BLK) & ((kk >= DIL_BLK) | has_prev)


def _blocks(t):
    return t.reshape(DIL_NB, DIL_BLK, t.shape[-1])


def _with_prev(t):
    prev = jnp.concatenate([jnp.zeros((DIL_BLK, t.shape[-1]), t.dtype), t[:-DIL_BLK]], axis=0)
    return jnp.concatenate([_blocks(prev), _blocks(t)], axis=1)


def _fold_prev(t2):
    n = t2.shape[-1]
    to_prev = t2[:, :DIL_BLK].reshape(S, n)
    own = t2[:, DIL_BLK:].reshape(S, n)
    return own + jnp.concatenate([to_prev[DIL_BLK:], jnp.zeros((DIL_BLK, n), t2.dtype)], axis=0)


def _dil_group_spec(t, width=DIL_OUT):
    return pl.BlockSpec((None, S, width), lambda g: (t * DIL_G + g, 0, 0))


def _dil_fwd(qkv3):
    def body(q_ref, k_ref, v_ref, o_ref, lse_ref):
        ok = _dil_mask(pl.program_id(0))
        lane = lax.broadcasted_iota(jnp.int32, (S, LANES), 1)
        lse = jnp.zeros((S, LANES), F32)
        for h in range(DIL_HG):
            sl = slice(h * HD, (h + 1) * HD)
            s = jnp.where(ok, _dot(_blocks(q_ref[:, sl]), _with_prev(k_ref[:, sl]), _BQK) * SCALE, NEG_INF)
            m = jnp.max(s, axis=-1, keepdims=True)
            p = jnp.exp(s - m)
            l = jnp.sum(p, axis=-1, keepdims=True)
            o_ref[:, sl] = _dot(p / l, _with_prev(v_ref[:, sl]), _BQD).reshape(S, HD)
            lse = jnp.where(lane == h, (m + jnp.log(l)).reshape(S, 1), lse)
        lse_ref[...] = lse

    return pl.pallas_call(
        body, name="dil_fwd", grid=(DIL_G,),
        in_specs=[_dil_group_spec(0), _dil_group_spec(1), _dil_group_spec(2)],
        out_specs=[_dil_group_spec(0), _dil_group_spec(0, LANES)],
        out_shape=[jax.ShapeDtypeStruct((DIL_G, S, DIL_OUT), F32), jax.ShapeDtypeStruct((DIL_G, S, LANES), F32)],
        compiler_params=_cparams(("parallel",)),
    )(qkv3, qkv3, qkv3)


def _dil_bwd(qkv3, lse3, do3, c3):
    def body(q_ref, k_ref, v_ref, lse_ref, do_ref, c_ref, dq_ref, dk_ref, dv_ref):
        ok = _dil_mask(pl.program_id(0))
        lse = lse_ref[...]
        c = c_ref[...]
        for h in range(DIL_HG):
            sl = slice(h * HD, (h + 1) * HD)
            q = _blocks(q_ref[:, sl])
            k2 = _with_prev(k_ref[:, sl])
            do_h = _blocks(do_ref[:, sl])
            s = jnp.where(ok, _dot(q, k2, _BQK) * SCALE, NEG_INF)
            p = jnp.exp(s - _blocks(lse[:, h:h + 1]))
            dp = _dot(do_h, _with_prev(v_ref[:, sl]), _BQK)
            ds = p * (dp - _blocks(c[:, h:h + 1]))
            dsb = (ds * SCALE).astype(BF16)
            dq_ref[:, sl] = _dot(dsb, k2, _BQD).reshape(S, HD)
            dk_ref[:, sl] = _fold_prev(_dot(dsb, q, _BKD))
            dv_ref[:, sl] = _fold_prev(_dot(p, do_h, _BKD))

    return pl.pallas_call(
        body, name="dil_bwd", grid=(DIL_G,),
        in_specs=[_dil_group_spec(0), _dil_group_spec(1), _dil_group_spec(2), _dil_group_spec(0, LANES),
                  _dil_group_spec(0), _dil_group_spec(0, LANES)],
        out_specs=[_dil_group_spec(0)] * 3,
        out_shape=[jax.ShapeDtypeStruct((DIL_G, S, DIL_OUT), F32)] * 3,
        compiler_params=_cparams(("parallel",)),
    )(qkv3, qkv3, qkv3, lse3, do3, c3)


COMB_T = 256


def _dil_combine(o3, lse3):
    heads_per_lb = LANES // HD

    def body(o_ref, lse_ref, ob_ref, al_ref, *scratch):
        o_tok = [scratch[LB_PER_CB * gg:LB_PER_CB * (gg + 1)] for gg in range(DIL_G)]
        lse_tok = scratch[LB_PER_CB * DIL_G:]
        g = pl.program_id(0)
        for gs in range(DIL_G):
            @pl.when(g == gs)
            def _(gs=gs):
                for tok, sub in _residue_pieces(DIL_R[gs]):
                    for hf in range(LB_PER_CB):
                        o_tok[gs][hf][tok, :] = o_ref[sub, hf * LANES:(hf + 1) * LANES]
                    lse_tok[gs][tok, :] = lse_ref[sub, :]

        @pl.when(g == DIL_G - 1)
        def _():
            def chunk(i, carry):
                rows = pl.ds(pl.multiple_of(i * COMB_T, COMB_T), COMB_T)
                lse = [lse_tok[gg][rows, :] for gg in range(DIL_G)]
                m = jnp.maximum(jnp.maximum(lse[0], lse[1]), lse[2])
                e = [jnp.exp(lse[gg] - m) for gg in range(DIL_G)]
                den = (e[0] + e[1]) + e[2]
                al = [e[gg] / den for gg in range(DIL_G)]
                for gg in range(DIL_G):
                    al_ref[gg, rows, :] = al[gg]
                for h in range(DIL_HG):
                    hf, sl = h // heads_per_lb, slice((h % heads_per_lb) * HD, (h % heads_per_lb + 1) * HD)
                    acc = al[0][:, h:h + 1] * o_tok[0][hf][rows, sl]
                    for gg in range(1, DIL_G):
                        acc = acc + al[gg][:, h:h + 1] * o_tok[gg][hf][rows, sl]
                    ob_ref[rows, h * HD:(h + 1) * HD] = acc
                return carry

            lax.fori_loop(0, S // COMB_T, chunk, 0)

    return pl.pallas_call(
        body, name="dil_combine", grid=(DIL_G,),
        in_specs=[pl.BlockSpec((None, S, DIL_OUT), lambda g: (g, 0, 0)),
                  pl.BlockSpec((None, S, LANES), lambda g: (g, 0, 0))],
        out_specs=[pl.BlockSpec((S, DIL_OUT), lambda g: (0, 0)),
                   pl.BlockSpec((DIL_G, S, LANES), lambda g: (0, 0, 0))],
        out_shape=[jax.ShapeDtypeStruct((S, DIL_OUT), F32), jax.ShapeDtypeStruct((DIL_G, S, LANES), F32)],
        scratch_shapes=[pltpu.VMEM((S, LANES), F32)] * (DIL_G * (LB_PER_CB + 1)),
        compiler_params=_cparams(("arbitrary",)),
    )(o3, lse3)


def _dil_combine_bwd(dob, ob, alpha, deps=()):
    heads_per_lb = LANES // HD

    def body(dob_ref, ob_ref, al_ref, *refs):
        do_ref, c_ref = refs[len(deps):]
        g = pl.program_id(0)
        hf = pl.program_id(1)
        for gs in range(DIL_G):
            @pl.when(g == gs)
            def _(gs=gs):
                for tok, sub in _residue_pieces(DIL_R[gs]):
                    dob = dob_ref[tok, :]
                    prod = dob * ob_ref[tok, :]
                    al = al_ref[tok, :]
                    lane = lax.broadcasted_iota(jnp.int32, al.shape, 1)
                    c = jnp.where(hf == 0, 0.0, c_ref[sub, :])
                    for hh in range(heads_per_lb):
                        sl = slice(hh * HD, (hh + 1) * HD)
                        head = hf * heads_per_lb + hh
                        a = jnp.sum(jnp.where(lane == head, al, 0.0), axis=-1, keepdims=True)
                        do_ref[sub, sl] = (a * dob[:, sl]).astype(BF16)
                        c = jnp.where(lane == head, a * jnp.sum(prod[:, sl], axis=-1, keepdims=True), c)
                    c_ref[sub, :] = c

    half = pl.BlockSpec((S, LANES), lambda g, hf: (0, hf))
    return pl.pallas_call(
        body, name="dil_combine_bwd", grid=(DIL_G, LB_PER_CB),
        in_specs=[half, half, pl.BlockSpec((None, S, LANES), lambda g, hf: (g, 0, 0))] + [ANY] * len(deps),
        out_specs=[pl.BlockSpec((None, S, LANES), lambda g, hf: (g, 0, hf)),
                   pl.BlockSpec((None, S, LANES), lambda g, hf: (g, 0, 0))],
        out_shape=[jax.ShapeDtypeStruct((DIL_G, S, DIL_OUT), BF16), jax.ShapeDtypeStruct((DIL_G, S, LANES), F32)],
        compiler_params=_cparams(("parallel", "arbitrary")),
    )(dob, ob, alpha, *deps)


def _local_step(x, tgt, g_attn, g_mlp, g_final, b_pad, w_in_t, hooks):
    tables = _rope_tables()

    h1 = _rms_fwd(x, g_attn, "rms_attn_fwd", deps=hooks.first_deps())
    proj_a = _mm(h1, w_in_t, mode="nt", tm=S, tn=512, n=3 * FOX_W, out_dtypes=[BF16], name="proj_a")
    proj_b = _mm(h1, w_in_t, mode="nt", tm=S, tn=CB, n=3 * DIL_W, b_off=SEG_B // CB, out_dtypes=[F32], name="proj_b")
    proj_g = _mm(h1, w_in_t, mode="nt", tm=S, tn=CB, n=2 * D, b_off=SEG_G // CB, out_dtypes=[F32], name="proj_g")
    proj_f = _mm(h1, w_in_t, mode="nt", tm=S, tn=LANES, n=LANES, b_off=SEG_F // LANES, out_dtypes=[F32], name="proj_f")

    q_aug, k_aug = _fox_prepare(proj_a, proj_f, b_pad)
    oa, lse_a = _fox_fwd(q_aug, k_aug, proj_a)

    qkv3 = _rope_split(proj_b, tables)
    o3, lse3 = _dil_fwd(qkv3)
    ob, alpha = _dil_combine(o3, lse3)

    w_a, w_b, w_out = hooks.mixer_weights(ob)
    ya = _mm(oa, w_a, mode="nn", tm=S, tn=CB, out_dtypes=[F32], name="branch_a")
    yb = _mm(ob, w_b, mode="nn", tm=S, tn=CB, out_dtypes=[F32], name="branch_b")
    mixed = _gate_fwd(proj_g, ya, yb)
    x2 = _mm(mixed, w_out, mode="nn", tm=S, tn=CB, out_dtypes=[F32], name="out_proj",
             epilogue=lambda acc, res: (res + acc,), extras=(x,))

    h2 = _rms_fwd(x2, g_mlp, "rms_mlp_fwd")
    w_up_sh, w_down = hooks.mlp_weights(h2)

    def up_epilogue(acc):
        r = jnp.maximum(acc, 0.0)
        return acc, r * r

    u, act = _mm(h2, w_up_sh, mode="nn", tm=S, tn=DFF // N_DEV, b_sharded=True, out_dtypes=[F32, BF16],
                 name="mlp_up", epilogue=up_epilogue)
    x3 = _mm(act, w_down, mode="nn", tm=512, tn=512, out_dtypes=[F32], name="mlp_down",
             epilogue=lambda acc, res: (res + acc,), extras=(x2,))

    dx3, dg_final, loss = _final_norm_loss(x3, g_final, tgt)

    du = _mm(dx3, w_down, mode="nt", tm=S, tn=CB, out_dtypes=[BF16], name="mlp_down_bwd",
             epilogue=lambda acc, u_t: (acc * (2.0 * jnp.maximum(u_t, 0.0)),), extras=(u,))
    dw_down = _mm(act, dx3, mode="tn", tm=512, tn=512, out_dtypes=[F32], name="dw_down")
    dw_up_sh = _mm(h2, du, mode="tn", tm=D, tn=DFF // N_DEV, out_sharded=True, out_dtypes=[F32], name="dw_up")
    dh2 = _mm(du, w_up_sh, mode="nt", tm=512, tn=512, b_sharded=True, out_dtypes=[F32], name="mlp_up_bwd",
              deps=hooks.mlp_grads(dw_up_sh, dw_down))
    dx2, dg_mlp = _rms_bwd(x2, g_mlp, dh2, dx3, "rms_mlp_bwd")

    dmixed = _mm(dx2, w_out, mode="nt", tm=S, tn=CB, out_dtypes=[F32], name="out_proj_bwd")
    dw_out = _mm(mixed, dx2, mode="tn", tm=D, tn=CB, out_dtypes=[F32], name="dw_out")

    dyab, dproj = _gate_bwd(dmixed, proj_g, ya, yb)
    doa = _mm(dyab, w_a, mode="nt", tm=S, tn=CB, k=D, a_off=0, out_dtypes=[BF16], name="branch_a_bwd")
    dw_a = _mm(oa, dyab, mode="tn", tm=FOX_W, tn=CB, n=D, b_off=0, out_dtypes=[F32], name="dw_branch_a")
    dob = _mm(dyab, w_b, mode="nt", tm=S, tn=CB, k=D, a_off=1, out_dtypes=[F32], name="branch_b_bwd")
    dw_b = _mm(ob, dyab, mode="tn", tm=DIL_OUT, tn=CB, n=D, b_off=D // CB, out_dtypes=[F32], name="dw_branch_b")

    dproj, dft, dfs = _fox_bwd(q_aug, k_aug, proj_a, lse_a, doa, dproj, deps=hooks.mixer_grads(dw_a, dw_b, dw_out))
    dproj, db = _fox_dscan(dft, dfs, proj_f, b_pad, dproj)

    do3, c3 = _dil_combine_bwd(dob, ob, alpha, deps=hooks.mid_backward(db))
    dq3, dk3, dv3 = _dil_bwd(qkv3, lse3, do3, c3)
    dproj = _rope_merge_bwd(dq3, dk3, dv3, tables, dproj)

    dw_in_t = _mm(dproj, h1, mode="tn", tm=CB, tn=D, out_dtypes=[F32], name="dw_in")
    dh1 = _mm(dproj, w_in_t, mode="nn", tm=512, tn=CB, out_dtypes=[F32], name="proj_bwd",
              deps=hooks.w_in_grad(dw_in_t))
    dx, dg_attn = _rms_bwd(x, g_attn, dh1, dx2, "rms_attn_bwd")

    return loss, dx, (dg_attn, db, dg_mlp, dg_final)


SHARD_SHAPES = ((D_IN // N_DEV, D), (FOX_W, D // N_DEV), (DIL_OUT, D // N_DEV), (D // N_DEV, D),
                (D, DFF // N_DEV), (DFF // N_DEV, D))
W_NAMES = ("w_in", "w_a", "w_b", "w_out", "w_up", "w_down")
AG_COPIES = 7
HBM = pl.BlockSpec(memory_space=pltpu.HBM)
SEM = pl.BlockSpec(memory_space=pltpu.SEMAPHORE)


def _place():
    return lax.axis_index("x"), lax.axis_index("y"), lax.axis_index("c")


def _all_gather(shards):
    n_w = len(shards)
    shapes = [t.shape for t in shards]

    def body(*refs):
        in_refs, out_refs = refs[:n_w], refs[n_w:2 * n_w]
        stage, cast = refs[2 * n_w:3 * n_w], refs[3 * n_w:4 * n_w]
        send_sems, recv_sems, local_sems, load_sems = refs[4 * n_w:]
        x, y, c = _place()
        me, sibling = (x, y, c), (x, y, 1 - c)
        chips = [(1 - x, y), (x, 1 - y), (1 - x, 1 - y)]

        def slot(i, px, py, pc):
            return out_refs[i].at[4 * px + 2 * py + pc]

        def copy(i, k, block, to, src=None):
            return pltpu.make_async_remote_copy(
                src_ref=slot(i, *block) if src is None else src, dst_ref=slot(i, *block),
                send_sem=send_sems.at[i * AG_COPIES + k], recv_sem=recv_sems.at[i * AG_COPIES + k],
                device_id=to, device_id_type=MESH)

        loads = [pltpu.make_async_copy(in_refs[i], stage[i], load_sems.at[i]) for i in range(n_w)]
        for ld in loads:
            ld.start()
        local, remote = [], []
        for i in range(n_w):
            loads[i].wait()
            cast[i][...] = stage[i][...].astype(BF16)
            local.append(pltpu.make_async_copy(cast[i], slot(i, *me), local_sems.at[i]))
            local[-1].start()
            first = [copy(i, 0, me, sibling, src=cast[i])]
            first += [copy(i, 1 + j, me, (*chip, c), src=cast[i]) for j, chip in enumerate(chips)]
            for cp in first:
                cp.start()
            remote += first
        for i in range(n_w):
            for j, chip in enumerate(chips):
                copy(i, 1 + j, (*chip, c), me).wait_recv()
                remote.append(copy(i, 4 + j, (*chip, c), sibling))
                remote[-1].start()
        for i in range(n_w):
            copy(i, 0, sibling, me).wait_recv()
            for j, chip in enumerate(chips):
                copy(i, 4 + j, (*chip, 1 - c), me).wait_recv()
        for cp in remote:
            cp.wait_send()
        for cp in local:
            cp.wait()

    return pl.pallas_call(
        body, name="all_gather_w_in",
        out_shape=[jax.ShapeDtypeStruct((N_DEV,) + sh, BF16) for sh in shapes],
        in_specs=[ANY] * n_w, out_specs=[ANY] * n_w,
        scratch_shapes=[pltpu.VMEM(sh, F32) for sh in shapes] + [pltpu.VMEM(sh, BF16) for sh in shapes]
                       + [pltpu.SemaphoreType.DMA((n_w * AG_COPIES,)), pltpu.SemaphoreType.DMA((n_w * AG_COPIES,)),
                          pltpu.SemaphoreType.DMA((n_w,)), pltpu.SemaphoreType.DMA((n_w,))],
        compiler_params=_cparams(None),
    )(*shards)


def _place_own(shards):
    n_w = len(shards)
    shapes = [t.shape for t in shards]

    def body(*refs):
        in_refs, out_refs = refs[:n_w], refs[n_w:2 * n_w]
        stage, cast = refs[2 * n_w:3 * n_w], refs[3 * n_w:4 * n_w]
        load_sems, store_sems = refs[4 * n_w:]
        x, y, c = _place()
        loads = [pltpu.make_async_copy(in_refs[i], stage[i], load_sems.at[i]) for i in range(n_w)]
        for ld in loads:
            ld.start()
        stores = []
        for i in range(n_w):
            loads[i].wait()
            cast[i][...] = stage[i][...].astype(BF16)
            stores.append(pltpu.make_async_copy(cast[i], out_refs[i].at[4 * x + 2 * y + c], store_sems.at[i]))
            stores[-1].start()
        for st in stores:
            st.wait()

    return pl.pallas_call(
        body, name="place_own_shards",
        out_shape=[jax.ShapeDtypeStruct((N_DEV,) + sh, BF16) for sh in shapes],
        in_specs=[ANY] * n_w, out_specs=[ANY] * n_w,
        scratch_shapes=[pltpu.VMEM(sh, F32) for sh in shapes] + [pltpu.VMEM(sh, BF16) for sh in shapes]
                       + [pltpu.SemaphoreType.DMA((n_w,)), pltpu.SemaphoreType.DMA((n_w,))],
        compiler_params=_cparams(None),
    )(*shards)


PEER_COPIES = N_DEV - 1
SIBLING_COPIES = 4
CHIP_COPIES = 3


def _peer_copies(_, land_refs, send_sems, recv_sems):
    x, y, c = _place()
    mine = 4 * x + 2 * y + c
    copies = []
    for i, ref in enumerate(land_refs):
        for k in range(1, N_DEV):
            peer = (x ^ (k >> 2), y ^ ((k >> 1) & 1), c ^ (k & 1))
            n = i * PEER_COPIES + k - 1
            copies.append(pltpu.make_async_remote_copy(
                src_ref=ref.at[mine], dst_ref=ref.at[mine], send_sem=send_sems.at[n], recv_sem=recv_sems.at[n],
                device_id=peer, device_id_type=MESH))
    return copies


def _sibling_copies(g_refs, r_refs, send_sems, recv_sems):
    x, y, c = _place()
    return [pltpu.make_async_remote_copy(
        src_ref=g_refs[i].at[2 * k + (1 - c)], dst_ref=r_refs[i].at[k],
        send_sem=send_sems.at[SIBLING_COPIES * i + k], recv_sem=recv_sems.at[SIBLING_COPIES * i + k],
        device_id=(x, y, 1 - c), device_id_type=MESH) for i in range(len(g_refs)) for k in range(SIBLING_COPIES)]


def _chip_copies(p_refs, r_refs, send_sems, recv_sems):
    x, y, c = _place()
    chips = [(1 - x, y), (x, 1 - y), (1 - x, 1 - y)]
    return [pltpu.make_async_remote_copy(
        src_ref=p_refs[i].at[2 * cx + cy], dst_ref=r_refs[i].at[j],
        send_sem=send_sems.at[CHIP_COPIES * i + j], recv_sem=recv_sems.at[CHIP_COPIES * i + j],
        device_id=(cx, cy, c), device_id_type=MESH) for i in range(len(p_refs)) for j, (cx, cy) in enumerate(chips)]


def _in_hbm(a):
    return pltpu.with_memory_space_constraint(a, pltpu.HBM)


def _exchange_start(name, copies_fn, n_copies, srcs, lands, after=()):
    n_s, n_l, n_a = len(srcs), len(lands), len(after)

    def body(*refs):
        outs = refs[n_s + n_l + n_a:]
        for cp in copies_fn(refs[:n_s], refs[n_s:n_s + n_l], outs[0], outs[1]):
            cp.start()
        outs[-1][...] = jnp.zeros_like(outs[-1])

    res = pl.pallas_call(
        body, name=name,
        out_shape=[pltpu.SemaphoreType.DMA((n_copies,)), pltpu.SemaphoreType.DMA((n_copies,))]
                  + [pltpu.HBM(t.shape, t.dtype) for t in (*srcs, *lands)] + [jax.ShapeDtypeStruct((8, LANES), F32)],
        in_specs=[HBM] * (n_s + n_l) + [ANY] * n_a,
        out_specs=[SEM, SEM] + [HBM] * (n_s + n_l) + [pl.BlockSpec(memory_space=pltpu.VMEM)],
        input_output_aliases={i: 2 + i for i in range(n_s + n_l)},
        compiler_params=pltpu.CompilerParams(has_side_effects=pltpu.SideEffectType.DATAFLOW_SIDE_EFFECTING),
    )(*[_in_hbm(t) for t in (*srcs, *lands)], *after)
    return res[0], res[1], list(res[2:2 + n_s]), list(res[2 + n_s:2 + n_s + n_l]), res[-1]


def _exchange_wait(name, copies_fn, started, after):
    send_sems, recv_sems, srcs, lands, _ = started
    n_s, n_l = len(srcs), len(lands)

    def body(*refs):
        for cp in copies_fn(refs[:n_s], refs[n_s:n_s + n_l], refs[n_s + n_l], refs[n_s + n_l + 1]):
            cp.wait_send()
            cp.wait_recv()

    res = pl.pallas_call(
        body, name=name,
        out_shape=[pltpu.HBM(t.shape, t.dtype) for t in (*srcs, *lands)],
        in_specs=[HBM] * (n_s + n_l) + [SEM, SEM, ANY],
        out_specs=[HBM] * (n_s + n_l),
        input_output_aliases={i: i for i in range(n_s + n_l)},
        compiler_params=pltpu.CompilerParams(has_side_effects=pltpu.SideEffectType.DATAFLOW_SIDE_EFFECTING),
    )(*srcs, *lands, send_sems, recv_sems, after)
    return list(res[:n_s]), list(res[n_s:])


def _shard_block(shape):
    rows, cols = shape
    if rows % 256:
        return rows, 256
    return min(rows, 512 if cols <= LANES else 256), cols


def _chip_partial(ids, g_stack, recv1, name):
    shape = g_stack.shape[1:]
    br, bc = _shard_block(shape)

    def body(ids_ref, g_ref, r_ref, pb_ref, own_ref):
        s = g_ref[...] + r_ref[...]
        pb_ref[...] = s.astype(BF16)

        @pl.when(pl.program_id(2) == ids_ref[1])
        def _():
            own_ref[...] = s

    grid_spec = pltpu.PrefetchScalarGridSpec(
        num_scalar_prefetch=1, grid=(shape[0] // br, shape[1] // bc, 4),
        in_specs=[pl.BlockSpec((None, br, bc), lambda r, q, k, ids: (2 * k + ids[0], r, q)),
                  pl.BlockSpec((None, br, bc), lambda r, q, k, ids: (k, r, q))],
        out_specs=[pl.BlockSpec((None, br, bc), lambda r, q, k, ids: (k, r, q)),
                   pl.BlockSpec((br, bc), lambda r, q, k, ids: (r, q))])
    return pl.pallas_call(
        body, name=name, grid_spec=grid_spec,
        out_shape=[jax.ShapeDtypeStruct((4,) + shape, BF16), jax.ShapeDtypeStruct(shape, F32)],
        compiler_params=_cparams(("parallel", "parallel", "arbitrary")),
    )(ids, g_stack, recv1)


def _adamw(w, g, m, v):
    m = ADAM_B1 * m + (1.0 - ADAM_B1) * g
    v = ADAM_B2 * v + (1.0 - ADAM_B2) * (g * g)
    m_hat = m / (1.0 - ADAM_B1 ** ADAM_STEP)
    v_hat = v / (1.0 - ADAM_B2 ** ADAM_STEP)
    delta = -ADAM_LR * (m_hat / (jnp.sqrt(v_hat) + ADAM_EPS) + ADAM_WD * w)
    return delta, m, v


def _reduce_adamw(own, recv2, w, m, v, name, deps=()):
    shape = own.shape
    br, bc = _shard_block(shape)

    def body(own_ref, r_ref, w_ref, m_ref, v_ref, *refs):
        g_ref, d_ref, nm_ref, nv_ref = refs[len(deps):]
        g = own_ref[...]
        for j in range(3):
            g = g + r_ref[j].astype(F32)
        delta, nm, nv = _adamw(w_ref[...], g, m_ref[...], v_ref[...])
        g_ref[...] = g
        d_ref[...] = delta
        nm_ref[...] = nm
        nv_ref[...] = nv

    blk = pl.BlockSpec((br, bc), lambda r, q: (r, q))
    return pl.pallas_call(
        body, name=name, grid=(shape[0] // br, shape[1] // bc),
        in_specs=[blk, pl.BlockSpec((3, br, bc), lambda r, q: (0, r, q)), blk, blk, blk] + [ANY] * len(deps),
        out_specs=[blk] * 4, out_shape=[jax.ShapeDtypeStruct(shape, F32)] * 4,
        compiler_params=_cparams(("parallel", "parallel")),
    )(own, recv2, w, m, v, *deps)


def _small_allreduce_adamw(gvec, w, m, v, deps=()):
    def body(g_ref, w_ref, m_ref, v_ref, *refs):
        go_ref, d_ref, nm_ref, nv_ref, buf, send_sems, recv_sems = refs[len(deps):]
        x, y, c = _place()
        my_slot = 4 * x + 2 * y + c
        buf[my_slot] = g_ref[...]
        copies = []
        for k in range(1, N_DEV):
            px, py, pc = x ^ (k >> 2), y ^ ((k >> 1) & 1), c ^ (k & 1)
            copies.append(pltpu.make_async_remote_copy(
                src_ref=g_ref, dst_ref=buf.at[my_slot], send_sem=send_sems.at[k - 1], recv_sem=recv_sems.at[k - 1],
                device_id=(px, py, pc), device_id_type=MESH))
        for cp in copies:
            cp.start()
        for k in range(1, N_DEV):
            px, py, pc = x ^ (k >> 2), y ^ ((k >> 1) & 1), c ^ (k & 1)
            pltpu.make_async_remote_copy(
                src_ref=g_ref, dst_ref=buf.at[4 * px + 2 * py + pc], send_sem=send_sems.at[k - 1],
                recv_sem=recv_sems.at[k - 1], device_id=(px, py, pc), device_id_type=MESH).wait_recv()
        for cp in copies:
            cp.wait_send()
        g = buf[0]
        for s in range(1, N_DEV):
            g = g + buf[s]
        delta, nm, nv = _adamw(w_ref[...], g, m_ref[...], v_ref[...])
        go_ref[...] = g
        d_ref[...] = delta
        nm_ref[...] = nm
        nv_ref[...] = nv

    vm = pl.BlockSpec(memory_space=pltpu.VMEM)
    return pl.pallas_call(
        body, name="small_allreduce_adamw",
        in_specs=[vm] * 4 + [ANY] * len(deps), out_specs=[vm] * 4,
        out_shape=[jax.ShapeDtypeStruct((SMALL_R, LANES), F32)] * 4,
        scratch_shapes=[pltpu.VMEM((N_DEV, SMALL_R, LANES), F32),
                        pltpu.SemaphoreType.DMA((N_DEV - 1,)), pltpu.SemaphoreType.DMA((N_DEV - 1,))],
    )(gvec, w, m, v, *deps)


def _cols_to_whole(stack):
    rows = stack.shape[1]
    return stack.transpose(1, 0, 2).reshape(rows, -1)


def _whole_to_cols(t):
    rows = t.shape[0]
    return t.reshape(rows, N_DEV, -1).transpose(1, 0, 2)


W_IN_SEGMENTS = ((0, 3 * FOX_W, SEG_A), (3 * FOX_W, 3 * FOX_W + FOX_H, SEG_F),
                 (3 * FOX_W + FOX_H, 3 * FOX_W + FOX_H + 3 * DIL_W, SEG_B), (3 * FOX_W + FOX_H + 3 * DIL_W, D_IN, SEG_G))


def _shard_pieces(p):
    rows = D_IN // N_DEV
    lo, hi = p * rows, (p + 1) * rows
    return [(max(lo, a) - lo, min(hi, b) - lo, pad + max(lo, a) - a)
            for a, b, pad in W_IN_SEGMENTS if max(lo, a) < min(hi, b)]


def _pad_w_in_t(stack):
    wt = stack.reshape(D_IN, D)
    placed = sorted((pad, wt[a:b]) for a, b, pad in W_IN_SEGMENTS)
    parts, row = [], 0
    for pad, piece in placed + [(D_PAD, None)]:
        if pad > row:
            parts.append(jnp.zeros((pad - row, D), wt.dtype))
        if piece is not None:
            parts.append(piece)
            row = pad + piece.shape[0]
    return jnp.concatenate(parts, axis=0)


def _unpad_dw_in_t(dwp):
    return jnp.stack([jnp.concatenate([dwp[pad:pad + b - a] for a, b, pad in _shard_pieces(p)], axis=0)
                      for p in range(N_DEV)])


def _pack_small(g_attn, b, g_mlp, g_final):
    b_rows = jnp.pad(b, ((0, 7), (0, LANES - b.shape[1])))
    return jnp.concatenate([g_attn.reshape(8, LANES), g_mlp.reshape(8, LANES), g_final.reshape(8, LANES), b_rows], axis=0)


def _unpack_small(p):
    return (p[0:8].reshape(1, D), p[24:25, :FOX_H], p[8:16].reshape(1, D), p[16:24].reshape(D))


class _StepComm:
    def __init__(self, ids, w_sh, m_sh, v_sh, after_w_in):
        self.ids = ids
        self.w_sh, self.m_sh, self.v_sh = w_sh, m_sh, v_sh
        self.updates = None
        lands = _place_own(w_sh[1:])
        self.gather_mixer = _exchange_start("gather_mixer_start", _peer_copies, PEER_COPIES * 3, [], lands[:3],
                                            after=(after_w_in,))
        self.gather_mlp = _exchange_start("gather_mlp_start", _peer_copies, PEER_COPIES * 2, [], lands[3:],
                                          after=(self.gather_mixer[-1],))
        self.sibling = {}
        self.chips = {}
        self.own = {}
        self.names = {}

    def _reduce_start(self, group, names, grads, after=()):
        lands = [lax.empty((SIBLING_COPIES,) + t.shape[1:], F32) for t in grads]
        self.sibling[group] = _exchange_start("grad_%s_sibling_start" % group, _sibling_copies,
                                              SIBLING_COPIES * len(grads), grads, lands, after=after)
        self.names[group] = names
        return self.sibling[group][-1]

    def _reduce_mid(self, group, after):
        grads, recv1 = _exchange_wait("grad_%s_sibling_wait" % group, _sibling_copies, self.sibling[group], after)
        parts = [_chip_partial(self.ids, g, r, "grad_partial_" + n) for g, r, n in zip(grads, recv1, self.names[group])]
        self.own[group] = [p[1] for p in parts]
        srcs = [p[0] for p in parts]
        lands = [lax.empty((CHIP_COPIES,) + t.shape[1:], BF16) for t in srcs]
        self.chips[group] = _exchange_start("grad_%s_chips_start" % group, _chip_copies, CHIP_COPIES * len(srcs),
                                            srcs, lands)
        return self.chips[group][-1]

    def reduced(self, group, after):
        _, recv2 = _exchange_wait("grad_%s_chips_wait" % group, _chip_copies, self.chips[group], after)
        return list(zip(self.own[group], recv2))

    def first_deps(self):
        return [self.gather_mlp[-1]]

    def mixer_weights(self, after):
        _, (g_a, g_b, g_out) = _exchange_wait("gather_mixer_wait", _peer_copies, self.gather_mixer, after)
        return _cols_to_whole(g_a), _cols_to_whole(g_b), g_out.reshape(D, D)

    def mlp_weights(self, after):
        _, (g_up, g_down) = _exchange_wait("gather_mlp_wait", _peer_copies, self.gather_mlp, after)
        return g_up, g_down.reshape(DFF, D)

    def mlp_grads(self, dw_up_sh, dw_down):
        return [self._reduce_start("mlp", W_NAMES[4:], [dw_up_sh, dw_down.reshape((N_DEV,) + SHARD_SHAPES[5])])]

    def mixer_grads(self, dw_a, dw_b, dw_out):
        token = self._reduce_mid("mlp", dw_b)
        grads = [_whole_to_cols(dw_a), _whole_to_cols(dw_b), dw_out.reshape((N_DEV,) + SHARD_SHAPES[3])]
        return [self._reduce_start("mixer", W_NAMES[1:4], grads, after=(token,))]

    def mid_backward(self, after):
        return [self._reduce_mid("mixer", after)]

    def w_in_grad(self, dw_in_t):
        grads = [_unpad_dw_in_t(dw_in_t)]
        token = self._reduce_start("w_in", W_NAMES[:1], grads)
        reduced = self.reduced("mixer", token) + self.reduced("mlp", token)
        self.updates = [
            _reduce_adamw(*r, self.w_sh[1 + i], self.m_sh[1 + i], self.v_sh[1 + i], "adamw_" + W_NAMES[1 + i], deps=[token])
            for i, r in enumerate(reduced)]
        return [self._reduce_mid("w_in", self.updates[-1][0])]

    def w_in_update(self, after):
        (reduced,) = self.reduced("w_in", after)
        return _reduce_adamw(*reduced, self.w_sh[0], self.m_sh[0], self.v_sh[0], "adamw_" + W_NAMES[0])


def kernel(x, norm_attn_g, w_in, b_forget, w_branch_a, w_branch_b, w_out, norm_mlp_g, w_up, w_down, norm_final_g, loss_target, m_norm_attn_g, m_w_in, m_b_forget, m_w_branch_a, m_w_branch_b, m_w_out, m_norm_mlp_g, m_w_up, m_w_down, m_norm_final_g, v_norm_attn_g, v_w_in, v_b_forget, v_w_branch_a, v_w_branch_b, v_w_out, v_norm_mlp_g, v_w_up, v_w_down, v_norm_final_g):
    cx, cy, cc = _place()
    ids = jnp.stack([cc, 2 * cx + cy]).astype(jnp.int32)

    w_sh = [w_in[0].T] + [t[0] for t in (w_branch_a, w_branch_b, w_out, w_up, w_down)]
    m_sh = [m_w_in[0].T] + [t[0] for t in (m_w_branch_a, m_w_branch_b, m_w_out, m_w_up, m_w_down)]
    v_sh = [v_w_in[0].T] + [t[0] for t in (v_w_branch_a, v_w_branch_b, v_w_out, v_w_up, v_w_down)]

    (g_in,) = _all_gather(w_sh[:1])
    w_in_t = _pad_w_in_t(g_in)
    comm = _StepComm(ids, w_sh, m_sh, v_sh, g_in)
    b_pad = jnp.pad(b_forget, ((0, 0), (0, LANES - FOX_H)))

    loss_row, dx, dsmall = _local_step(
        x[0], loss_target[0], norm_attn_g, norm_mlp_g, norm_final_g.reshape(1, D), b_pad, w_in_t, comm)
    loss = lax.psum(loss_row[0, 0], ("x", "y", "c"))

    dg_attn, db, dg_mlp, dg_final = dsmall
    small = _small_allreduce_adamw(
        _pack_small(dg_attn, db, dg_mlp, dg_final),
        _pack_small(norm_attn_g, b_forget, norm_mlp_g, norm_final_g.reshape(1, D)),
        _pack_small(m_norm_attn_g, m_b_forget, m_norm_mlp_g, m_norm_final_g.reshape(1, D)),
        _pack_small(v_norm_attn_g, v_b_forget, v_norm_mlp_g, v_norm_final_g.reshape(1, D)))
    big = [comm.w_in_update(small[0])] + comm.updates

    outs = [loss, dx[None]]
    for q in range(4):
        s_attn, s_b, s_mlp, s_final = _unpack_small(small[q])
        b_in, b_a, b_b, b_out, b_up, b_down = [t[q][None] for t in big]
        b_in = jnp.swapaxes(b_in, 1, 2)
        outs += [s_attn, b_in, s_b, b_a, b_b, b_out, s_mlp, b_up, b_down, s_final]
    return tuple(outs)
```

```python
import functools

import jax
import jax.numpy as jnp
from jax import lax
from jax.experimental import pallas as pl
from jax.experimental.pallas import tpu as pltpu

F32 = jnp.float32
BF16 = jnp.bfloat16
MESH = pl.DeviceIdType.MESH

S = 2048
D = 1024
HD = 64
FOX_H = 8
FOX_W = FOX_H * HD
DIL_HG = 4
DIL_G = 3
DIL_W = DIL_G * DIL_HG * HD
DIL_OUT = DIL_HG * HD
DIL_BLK = 128
DIL_R = (1, 4, 16)
DFF = 4 * D
D_IN = 3 * FOX_W + FOX_H + 3 * DIL_W + 2 * D
EPS = 1e-6
NEG_INF = -1e30
SCALE = HD ** -0.5
ROPE_THETA = 500000.0
ROPE_DIM = HD // 4
N_DEV = 8

ADAM_LR = 0.001
ADAM_B1 = 0.9
ADAM_B2 = 0.999
ADAM_EPS = 1e-08
ADAM_WD = 0.01
ADAM_STEP = 10

LANES = 128
CB = 256
SEG_A = 0
SEG_B = 3 * FOX_W
SEG_Z = SEG_B + 3 * DIL_W
SEG_G = SEG_Z + CB
SEG_F = SEG_G + 2 * D
D_PAD = SEG_F + CB
SMALL_R = 32

VMEM_MB = 56


def _cparams(dims=None, vmem_mb=VMEM_MB, **kw):
    return pltpu.CompilerParams(dimension_semantics=dims, vmem_limit_bytes=vmem_mb << 20, **kw)


ANY = pl.BlockSpec(memory_space=pl.ANY)

_NN = (((1,), (0,)), ((), ()))
_NT = (((1,), (1,)), ((), ()))
_TN = (((0,), (0,)), ((), ()))


def _dot(a, b, dims):
    return lax.dot_general(a.astype(BF16), b.astype(BF16), dims, preferred_element_type=F32)


def _mm(a, b, *, mode, tm, tn, out_dtypes, name, n=None, k=None, a_off=0, b_off=0,
        b_sharded=False, out_sharded=False, epilogue=None, extras=(), deps=()):
    n_sh = b.shape[-1] if b_sharded else None
    if mode == "nn":
        m = a.shape[0]
        k = k or a.shape[1]
        a_spec = pl.BlockSpec((tm, k), lambda i, j: (i, a_off))
        if b_sharded:
            assert tn == n_sh
            n = N_DEV * n_sh
            b_spec = pl.BlockSpec((None, k, tn), lambda i, j: (j, 0, 0))
        else:
            n = n or b.shape[1]
            b_spec = pl.BlockSpec((k, tn), lambda i, j: (0, j + b_off))
        dims = _NN
    elif mode == "nt":
        m = a.shape[0]
        k = k or a.shape[1]
        a_spec = pl.BlockSpec((tm, k), lambda i, j: (i, a_off))
        if b_sharded:
            n = b.shape[1]
            b_spec = pl.BlockSpec((N_DEV, tn, n_sh), lambda i, j: (0, j, 0))
        else:
            n = n or b.shape[0]
            b_spec = pl.BlockSpec((tn, k), lambda i, j: (j + b_off, 0))
        dims = _NT
    else:
        k, m = a.shape
        n = n or b.shape[1]
        a_spec = pl.BlockSpec((k, tm), lambda i, j: (0, i))
        b_spec = pl.BlockSpec((k, tn), lambda i, j: (0, j + b_off))
        dims = _TN
    assert m % tm == 0 and n % tn == 0, (name, m, n, tm, tn)
    n_extra = len(extras)
    tile = pl.BlockSpec((tm, tn), lambda i, j: (i, j))
    if out_sharded:
        assert mode == "tn" and n // tn == N_DEV
        out_spec = pl.BlockSpec((None, tm, tn), lambda i, j: (j, i, 0))
        out_shape = (N_DEV, m, tn)
    else:
        out_spec, out_shape = tile, (m, n)

    def body(a_ref, b_ref, *refs):
        if mode == "nt" and b_sharded:
            acc = _dot(a_ref[:, 0:n_sh], b_ref[0], dims)
            for p in range(1, N_DEV):
                acc = acc + _dot(a_ref[:, p * n_sh:(p + 1) * n_sh], b_ref[p], dims)
        else:
            acc = _dot(a_ref[...], b_ref[...], dims)
        ex = [r[...] for r in refs[:n_extra]]
        outs = epilogue(acc, *ex) if epilogue is not None else (acc,)
        for o_ref, o in zip(refs[n_extra + len(deps):], outs):
            o_ref[...] = o.astype(o_ref.dtype)

    res = pl.pallas_call(
        body, name=name, grid=(m // tm, n // tn),
        in_specs=[a_spec, b_spec] + [tile] * n_extra + [ANY] * len(deps),
        out_specs=[out_spec] * len(out_dtypes),
        out_shape=[jax.ShapeDtypeStruct(out_shape, dt) for dt in out_dtypes],
        compiler_params=_cparams(("parallel", "parallel")),
    )(a, b, *extras, *deps)
    return res if len(out_dtypes) > 1 else res[0]


ROW_T = 256


def _rms_fwd(x, g, name, deps=()):
    def body(x_ref, g_ref, *refs):
        o_ref = refs[-1]
        x = x_ref[...]
        r = lax.rsqrt(jnp.mean(x * x, axis=-1, keepdims=True) + EPS)
        o_ref[...] = ((x * r) * g_ref[...]).astype(BF16)

    return pl.pallas_call(
        body, name=name, grid=(S // ROW_T,),
        in_specs=[pl.BlockSpec((ROW_T, D), lambda i: (i, 0)), pl.BlockSpec((1, D), lambda i: (0, 0))] + [ANY] * len(deps),
        out_specs=pl.BlockSpec((ROW_T, D), lambda i: (i, 0)),
        out_shape=jax.ShapeDtypeStruct((S, D), BF16),
        compiler_params=_cparams(("parallel",)),
    )(x, g, *deps)


def _rms_bwd(x, g, dh, dres, name):
    def body(x_ref, g_ref, dh_ref, dres_ref, dx_ref, dg_ref):
        @pl.when(pl.program_id(0) == 0)
        def _():
            dg_ref[...] = jnp.zeros_like(dg_ref)

        x = x_ref[...]
        dh = dh_ref[...]
        r = lax.rsqrt(jnp.mean(x * x, axis=-1, keepdims=True) + EPS)
        xn = x * r
        dhn = dh * g_ref[...]
        dx_ref[...] = dres_ref[...] + r * (dhn - xn * jnp.mean(dhn * xn, axis=-1, keepdims=True))
        dg_ref[...] += jnp.sum(dh * xn, axis=0, keepdims=True)

    row = pl.BlockSpec((ROW_T, D), lambda i: (i, 0))
    vec = pl.BlockSpec((1, D), lambda i: (0, 0))
    return pl.pallas_call(
        body, name=name, grid=(S // ROW_T,),
        in_specs=[row, vec, row, row], out_specs=[row, vec],
        out_shape=[jax.ShapeDtypeStruct((S, D), F32), jax.ShapeDtypeStruct((1, D), F32)],
        compiler_params=_cparams(("arbitrary",)),
    )(x, g, dh, dres)


def _final_norm_loss(x, g, tgt):
    def body(x_ref, g_ref, t_ref, dx_ref, dg_ref, loss_ref):
        @pl.when(pl.program_id(0) == 0)
        def _():
            dg_ref[...] = jnp.zeros_like(dg_ref)
            loss_ref[...] = jnp.zeros_like(loss_ref)

        x = x_ref[...]
        g = g_ref[...]
        r = lax.rsqrt(jnp.mean(x * x, axis=-1, keepdims=True) + EPS)
        xn = x * r
        err = xn * g - t_ref[...]
        row_loss = jnp.mean(err * err, axis=-1, keepdims=True)
        loss_ref[...] += 0.5 * jnp.sum(row_loss, axis=0, keepdims=True) * jnp.ones((1, LANES), F32)
        dy = err * (1.0 / D)
        dyn = dy * g
        dx_ref[...] = r * (dyn - xn * jnp.mean(dyn * xn, axis=-1, keepdims=True))
        dg_ref[...] += jnp.sum(dy * xn, axis=0, keepdims=True)

    row = pl.BlockSpec((ROW_T, D), lambda i: (i, 0))
    vec = pl.BlockSpec((1, D), lambda i: (0, 0))
    return pl.pallas_call(
        body, name="final_norm_loss", grid=(S // ROW_T,),
        in_specs=[row, vec, row],
        out_specs=[row, vec, pl.BlockSpec((1, LANES), lambda i: (0, 0))],
        out_shape=[jax.ShapeDtypeStruct((S, D), F32), jax.ShapeDtypeStruct((1, D), F32),
                   jax.ShapeDtypeStruct((1, LANES), F32)],
        compiler_params=_cparams(("arbitrary",)),
    )(x, g, tgt)


def _sigmoid(z):
    return 1.0 / (1.0 + jnp.exp(-z))


def _gate_fwd(proj_g, ya, yb):
    def body(ga_ref, gb_ref, ya_ref, yb_ref, o_ref):
        o_ref[...] = (_sigmoid(ga_ref[...]) * ya_ref[...] + _sigmoid(gb_ref[...]) * yb_ref[...]).astype(BF16)

    row = pl.BlockSpec((ROW_T, D), lambda i: (i, 0))
    return pl.pallas_call(
        body, name="gate_fwd", grid=(S // ROW_T,),
        in_specs=[row, pl.BlockSpec((ROW_T, D), lambda i: (i, 1)), row, row],
        out_specs=row, out_shape=jax.ShapeDtypeStruct((S, D), BF16),
        compiler_params=_cparams(("parallel",)),
    )(proj_g, proj_g, ya, yb)


GATE_TR = 1024
GATE_TC = 512
GATE_NB = 2 * D // GATE_TC


def _gate_bwd(dmixed, proj_g, ya, yb):
    half = GATE_NB // 2

    def body(dm_ref, g_ref, ya_ref, yb_ref, dy_ref, dg_ref):
        dm = dm_ref[...]
        s = _sigmoid(g_ref[...])
        y = jnp.where(pl.program_id(1) < half, ya_ref[...], yb_ref[...])
        dy_ref[...] = (dm * s).astype(BF16)
        dg_ref[...] = (dm * y * (s * (1.0 - s))).astype(BF16)

    blk = pl.BlockSpec((GATE_TR, GATE_TC), lambda i, j: (i, j))
    wrapped = pl.BlockSpec((GATE_TR, GATE_TC), lambda i, j: (i, j % half))
    return pl.pallas_call(
        body, name="gate_bwd", grid=(S // GATE_TR, GATE_NB),
        in_specs=[wrapped, blk, wrapped, wrapped],
        out_specs=[blk, pl.BlockSpec((GATE_TR, GATE_TC), lambda i, j: (i, j + SEG_G // GATE_TC))],
        out_shape=[jax.ShapeDtypeStruct((S, 2 * D), BF16), jax.ShapeDtypeStruct((S, D_PAD), BF16)],
        compiler_params=_cparams(("parallel", "parallel")),
    )(dmixed, proj_g, ya, yb)


FOX_TQ = 256


def _scan_rows(x, reverse):
    n = x.shape[0]
    row = lax.broadcasted_iota(jnp.int32, x.shape, 0)
    k = 1
    while k < n:
        if reverse:
            x = x + jnp.where(row < n - k, pltpu.roll(x, n - k, 0), 0.0)
        else:
            x = x + jnp.where(row >= k, pltpu.roll(x, k, 0), 0.0)
        k *= 2
    return x


def _fox_dscan(dft, dfs, proj_f, b_pad, dproj):
    def body(dft_ref, dfs_ref, f_ref, b_ref, _, dfa_ref, db_ref):
        dfs_pad = jnp.concatenate([dfs_ref[...], jnp.zeros((LANES - FOX_H, S), F32)], axis=0)
        df = dfs_pad.T + dft_ref[0]
        for hp in range(1, dft_ref.shape[0]):
            df = df + pltpu.roll(dft_ref[hp], 2 * hp, 1)
        dlf = _scan_rows(df, reverse=True)
        z = f_ref[...] + b_ref[...]
        lane = lax.broadcasted_iota(jnp.int32, (S, LANES), 1)
        dfa = jnp.where(lane < FOX_H, dlf / (1.0 + jnp.exp(z)), 0.0)
        dfa_ref[:, :LANES] = dfa.astype(BF16)
        dfa_ref[:, LANES:] = jnp.zeros((S, CB - LANES), BF16)
        db_ref[...] = jnp.sum(dfa, axis=0, keepdims=True)

    return pl.pallas_call(
        body, name="fox_dscan", grid=(1,),
        in_specs=[pl.BlockSpec(dft.shape, lambda i: (0, 0, 0)), pl.BlockSpec((FOX_H, S), lambda i: (0, 0)),
                  pl.BlockSpec((S, LANES), lambda i: (0, 0)), pl.BlockSpec((1, LANES), lambda i: (0, 0)), ANY],
        out_specs=[pl.BlockSpec((S, CB), lambda i: (0, SEG_F // CB)), pl.BlockSpec((1, LANES), lambda i: (0, 0))],
        out_shape=[jax.ShapeDtypeStruct((S, D_PAD), BF16), jax.ShapeDtypeStruct((1, LANES), F32)],
        input_output_aliases={4: 0},
        compiler_params=_cparams(("arbitrary",)),
    )(dft, dfs, proj_f, b_pad, dproj)


FOX_NQ = S // FOX_TQ
FOX_HP = FOX_W // LANES
HEADS_PER_LB = LANES // HD
FOX_AUG = 2 * LANES


def _fox_prepare(proj_a, proj_f, b_pad):
    tr = 512

    def body(q_ref, k_ref, f_ref, b_ref, qa_ref, ka_ref, hi_s, mid_s, lo_s):
        i = pl.program_id(0)

        @pl.when(i == 0)
        def _():
            z = f_ref[...] + b_ref[...]
            lf = jnp.minimum(z, 0.0) - jnp.log1p(jnp.exp(-jnp.abs(z)))
            f = _scan_rows(lf, reverse=False)
            hi = f.astype(BF16).astype(F32)
            r1 = f - hi
            mid = r1.astype(BF16).astype(F32)
            hi_s[...] = hi
            mid_s[...] = mid
            lo_s[...] = (r1 - mid).astype(BF16).astype(F32)

        rows = pl.ds(pl.multiple_of(i * tr, tr), tr)
        hi, mid, lo = hi_s[rows, :], mid_s[rows, :], lo_s[rows, :]
        lane = lax.broadcasted_iota(jnp.int32, (tr, HD), 1)
        q_ones = jnp.where((lane >= 3) & (lane < 6), 1.0, 0.0)
        k_ones = jnp.where(lane < 3, 1.0, 0.0)
        for h in range(FOX_H):
            a1, a2, a3 = hi[:, h:h + 1], mid[:, h:h + 1], lo[:, h:h + 1]
            q_extra = jnp.where(lane == 0, a1, jnp.where(lane == 1, a2, jnp.where(lane == 2, a3, q_ones)))
            k_extra = jnp.where(lane == 3, -a1, jnp.where(lane == 4, -a2, jnp.where(lane == 5, -a3, k_ones)))
            base = h * LANES
            qa_ref[:, base:base + HD] = (q_ref[:, h * HD:(h + 1) * HD].astype(F32) * SCALE).astype(BF16)
            qa_ref[:, base + HD:base + LANES] = q_extra.astype(BF16)
            ka_ref[:, base:base + HD] = k_ref[:, h * HD:(h + 1) * HD]
            ka_ref[:, base + HD:base + LANES] = k_extra.astype(BF16)

    out_blk = pl.BlockSpec((tr, FOX_H * LANES), lambda i: (i, 0))
    return pl.pallas_call(
        body, name="fox_prepare", grid=(S // tr,),
        in_specs=[pl.BlockSpec((tr, FOX_W), lambda i: (i, 0)), pl.BlockSpec((tr, FOX_W), lambda i: (i, 1)),
                  pl.BlockSpec((S, LANES), lambda i: (0, 0)), pl.BlockSpec((1, LANES), lambda i: (0, 0))],
        out_specs=[out_blk, out_blk],
        out_shape=[jax.ShapeDtypeStruct((S, FOX_H * LANES), BF16)] * 2,
        scratch_shapes=[pltpu.VMEM((S, LANES), F32)] * 3,
        compiler_params=_cparams(("arbitrary",)),
    )(proj_a, proj_a, proj_f, b_pad)


def _fox_scores(qa, ka, w):
    s = _dot(qa, ka, _NT)
    row = lax.broadcasted_iota(jnp.int32, (FOX_TQ, FOX_TQ), 0)
    col = lax.broadcasted_iota(jnp.int32, (FOX_TQ, FOX_TQ), 1)
    diag = jnp.where(col <= row, s[:, w * FOX_TQ:], NEG_INF)
    return diag if w == 0 else jnp.concatenate([s[:, :w * FOX_TQ], diag], axis=1)


def _fox_fwd(q_aug, k_aug, proj_a):
    def body(qa_ref, ka_ref, v_ref, o_ref, lse_ref):
        qi = pl.program_id(1)
        lane = lax.broadcasted_iota(jnp.int32, (FOX_TQ, LANES), 1)
        for w in range(FOX_NQ):
            @pl.when(qi == w)
            def _(w=w):
                width = (w + 1) * FOX_TQ
                lse = jnp.zeros((FOX_TQ, LANES), F32)
                for hh in range(HEADS_PER_LB):
                    aug = slice(hh * LANES, (hh + 1) * LANES)
                    sl = slice(hh * HD, (hh + 1) * HD)
                    s = _fox_scores(qa_ref[:, aug], ka_ref[:width, aug], w)
                    m = jnp.max(s, axis=-1, keepdims=True)
                    p = jnp.exp(s - m)
                    l = jnp.sum(p, axis=-1, keepdims=True)
                    o_ref[:, sl] = _dot(p / l, v_ref[:width, sl], _NN)
                    lse = jnp.where(lane == hh, m + jnp.log(l), lse)
                lse_ref[...] = lse

    return pl.pallas_call(
        body, name="fox_fwd", grid=(FOX_HP, FOX_NQ),
        in_specs=[pl.BlockSpec((FOX_TQ, FOX_AUG), lambda h, i: (i, h)),
                  pl.BlockSpec((S, FOX_AUG), lambda h, i: (0, h)),
                  pl.BlockSpec((S, LANES), lambda h, i: (0, 2 * FOX_HP + h))],
        out_specs=[pl.BlockSpec((FOX_TQ, LANES), lambda h, i: (i, h)),
                   pl.BlockSpec((None, FOX_TQ, LANES), lambda h, i: (h, i, 0))],
        out_shape=[jax.ShapeDtypeStruct((S, FOX_W), F32), jax.ShapeDtypeStruct((FOX_HP, S, LANES), F32)],
        compiler_params=_cparams(("parallel", "parallel")),
    )(q_aug, k_aug, proj_a)


def _fox_bwd(q_aug, k_aug, proj_a, lse, do, dproj, deps=()):
    n_q = FOX_NQ

    def body(qa_ref, ka_ref, v_ref, lse_ref, do_ref, *refs):
        dproj_ref, dft_ref, dfs_ref, dk_acc, dv_acc, dq_buf, kv_buf, dq_sems, kv_sems = refs[1 + len(deps):]
        hp = pl.program_id(0)
        t = pl.program_id(1)
        slot = t % 2
        col = pl.multiple_of(hp * LANES, LANES)

        def dq_copy(step, buf_slot):
            rows = pl.ds(pl.multiple_of(step * FOX_TQ, FOX_TQ), FOX_TQ)
            return pltpu.make_async_copy(dq_buf.at[buf_slot], dproj_ref.at[rows, pl.ds(col, LANES)], dq_sems.at[buf_slot])

        @pl.when(t == 0)
        def _():
            dk_acc[...] = jnp.zeros_like(dk_acc)
            dv_acc[...] = jnp.zeros_like(dv_acc)

        @pl.when((t == 0) & (hp == 0))
        def _():
            dfs_ref[...] = jnp.zeros_like(dfs_ref)

        @pl.when(t >= 2)
        def _():
            dq_copy(t - 2, slot).wait()

        lane = lax.broadcasted_iota(jnp.int32, (FOX_TQ, LANES), 1)
        for w in range(n_q):
            @pl.when(t == w)
            def _(w=w):
                width = (w + 1) * FOX_TQ
                lse_all = lse_ref[...]
                sub = lax.broadcasted_iota(jnp.int32, (FOX_H, width), 0)
                dft = jnp.zeros((FOX_TQ, LANES), F32)
                for hh in range(HEADS_PER_LB):
                    aug = slice(hh * LANES, (hh + 1) * LANES)
                    sl = slice(hh * HD, (hh + 1) * HD)
                    head = hp * HEADS_PER_LB + hh
                    qa = qa_ref[:, aug]
                    ka = ka_ref[:width, aug]
                    do_h = do_ref[:, sl]
                    s = _fox_scores(qa, ka, w)
                    p = jnp.exp(s - lse_all[:, hh:hh + 1])
                    dp = _dot(do_h, v_ref[:width, sl], _NT)
                    ds = p * (dp - jnp.sum(dp * p, axis=-1, keepdims=True))
                    dft = jnp.where(lane == hh, jnp.sum(ds, axis=-1, keepdims=True), dft)
                    dfs_ref[:, :width] -= jnp.where(sub == head, jnp.sum(ds, axis=0, keepdims=True), 0.0)
                    dq_buf[slot, :, sl] = (_dot(ds, ka[:, :HD], _NN) * SCALE).astype(BF16)
                    dk_acc[:width, sl] += _dot(ds, qa[:, :HD], _TN)
                    dv_acc[:width, sl] += _dot(p, do_h, _TN)
                dft_ref[...] = dft

        dq_copy(t, slot).start()

        @pl.when(t == n_q - 1)
        def _():
            dq_copy(t - 1, 1 - slot).wait()
            dq_copy(t, slot).wait()
            kv_buf[0] = dk_acc[...].astype(BF16)
            kv_buf[1] = dv_acc[...].astype(BF16)
            copies = [pltpu.make_async_copy(
                kv_buf.at[j], dproj_ref.at[:, pl.ds(pl.multiple_of((j + 1) * FOX_W + hp * LANES, LANES), LANES)],
                kv_sems.at[j]) for j in range(2)]
            for cp in copies:
                cp.start()
            for cp in copies:
                cp.wait()

    qblk = pl.BlockSpec((FOX_TQ, FOX_AUG), lambda h, t: (t, h))
    lane_blk = pl.BlockSpec((None, FOX_TQ, LANES), lambda h, t: (h, t, 0))
    return pl.pallas_call(
        body, name="fox_bwd", grid=(FOX_HP, n_q),
        in_specs=[qblk, pl.BlockSpec((S, FOX_AUG), lambda h, t: (0, h)),
                  pl.BlockSpec((S, LANES), lambda h, t: (0, 2 * FOX_HP + h)),
                  lane_blk, pl.BlockSpec((FOX_TQ, LANES), lambda h, t: (t, h)), ANY] + [ANY] * len(deps),
        out_specs=[ANY, lane_blk, pl.BlockSpec((FOX_H, S), lambda h, t: (0, 0))],
        out_shape=[jax.ShapeDtypeStruct((S, D_PAD), BF16),
                   jax.ShapeDtypeStruct((FOX_HP, S, LANES), F32), jax.ShapeDtypeStruct((FOX_H, S), F32)],
        scratch_shapes=[pltpu.VMEM((S, LANES), F32), pltpu.VMEM((S, LANES), F32),
                        pltpu.VMEM((2, FOX_TQ, LANES), BF16), pltpu.VMEM((2, S, LANES), BF16),
                        pltpu.SemaphoreType.DMA((2,)), pltpu.SemaphoreType.DMA((2,))],
        input_output_aliases={5: 0},
        compiler_params=_cparams(("arbitrary", "arbitrary")),
    )(q_aug, k_aug, proj_a, lse, do, dproj, *deps)


def _rope_tables():
    half = ROPE_DIM // 2
    inv_freq = jnp.power(jnp.float32(ROPE_THETA), -jnp.arange(half, dtype=F32) * 2.0 / ROPE_DIM)
    ang = jnp.arange(S).astype(F32)[:, None] * inv_freq[None, :]
    cos, sin = jnp.cos(ang), jnp.sin(ang)
    z = jnp.zeros((S, half), F32)
    rest = HD - ROPE_DIM
    t_self = jnp.concatenate([cos, cos, jnp.ones((S, rest), F32)], axis=1)
    t_up = jnp.concatenate([-sin, z, jnp.zeros((S, rest), F32)], axis=1)
    t_dn = jnp.concatenate([z, sin, jnp.zeros((S, rest), F32)], axis=1)
    rep = LANES // HD

    def by_residue(t):
        return jnp.stack([t.reshape(S // r, r, LANES).transpose(1, 0, 2).reshape(S, LANES) for r in DIL_R])

    return tuple(by_residue(jnp.tile(t, (1, rep))) for t in (t_self, t_up, t_dn))


def _residue_pieces(r):
    if r == 1:
        return [(slice(i, i + 512), slice(i, i + 512)) for i in range(0, S, 512)]
    n = S // r
    return [(pl.ds(j, n, stride=r), slice(j * n, (j + 1) * n)) for j in range(r)]


def _rope_apply(x, a, u, d, backward):
    half = ROPE_DIM // 2
    if backward:
        return x * a + pltpu.roll(x * u, half, 1) + pltpu.roll(x * d, LANES - half, 1)
    return x * a + pltpu.roll(x, LANES - half, 1) * u + pltpu.roll(x, half, 1) * d


LB_PER_CB = CB // LANES
ROPE_NB = 3 * DIL_G * LB_PER_CB


def _rope_step(s):
    per_group = 3 * LB_PER_CB
    return s // per_group, (s % per_group) // LB_PER_CB, s % LB_PER_CB


def _rope_split(proj_b, tables):
    def body(x_ref, a_ref, u_ref, d_ref, o_ref):
        g, t, _ = _rope_step(pl.program_id(0))
        for gs in range(DIL_G):
            @pl.when(g == gs)
            def _(gs=gs):
                for tok, sub in _residue_pieces(DIL_R[gs]):
                    x = x_ref[tok, :]
                    y = _rope_apply(x, a_ref[sub, :], u_ref[sub, :], d_ref[sub, :], False)
                    o_ref[sub, :] = jnp.where(t < 2, y, x).astype(BF16)

    def in_index(s):
        g, t, hf = _rope_step(s)
        return 0, (t * DIL_G + g) * LB_PER_CB + hf

    def out_index(s):
        g, t, hf = _rope_step(s)
        return t * DIL_G + g, 0, hf

    tab = pl.BlockSpec((None, S, LANES), lambda s: (_rope_step(s)[0], 0, 0))
    return pl.pallas_call(
        body, name="rope_split", grid=(ROPE_NB,),
        in_specs=[pl.BlockSpec((S, LANES), in_index), tab, tab, tab],
        out_specs=pl.BlockSpec((None, S, LANES), out_index),
        out_shape=jax.ShapeDtypeStruct((3 * DIL_G, S, CB), BF16),
        compiler_params=_cparams(("parallel",)),
    )(proj_b, *tables)


def _rope_merge_bwd(dq3, dk3, dv3, tables, dproj):
    def body(dq_ref, dk_ref, dv_ref, a_ref, u_ref, d_ref, _, o_ref, tmp):
        s = pl.program_id(0)
        g, t, _ = _rope_step(s)
        for gs in range(DIL_G):
            @pl.when((g == gs) & (s < ROPE_NB))
            def _(gs=gs):
                for tok, sub in _residue_pieces(DIL_R[gs]):
                    x = jnp.where(t == 0, dq_ref[sub, :], jnp.where(t == 1, dk_ref[sub, :], dv_ref[sub, :]))
                    y = _rope_apply(x, a_ref[sub, :], u_ref[sub, :], d_ref[sub, :], True)
                    tmp[tok, :] = jnp.where(t < 2, y, x)
                o_ref[...] = tmp[...].astype(BF16)

        @pl.when(s >= ROPE_NB)
        def _():
            o_ref[...] = jnp.zeros_like(o_ref)

    def src_spec(own):
        def index(s):
            g, t, hf = _rope_step(jnp.minimum(s, ROPE_NB - 1))
            return g, 0, jnp.where(t == own, hf, jnp.where(t > own, LB_PER_CB - 1, 0))
        return pl.BlockSpec((None, S, LANES), index)

    def out_index(s):
        g, t, hf = _rope_step(s)
        col = SEG_B // LANES + (t * DIL_G + g) * LB_PER_CB + hf
        return 0, jnp.where(s < ROPE_NB, col, SEG_Z // LANES + s - ROPE_NB)

    tab = pl.BlockSpec((None, S, LANES), lambda s: (_rope_step(jnp.minimum(s, ROPE_NB - 1))[0], 0, 0))
    return pl.pallas_call(
        body, name="rope_merge_bwd", grid=(ROPE_NB + LB_PER_CB,),
        in_specs=[src_spec(0), src_spec(1), src_spec(2), tab, tab, tab, ANY],
        out_specs=pl.BlockSpec((S, LANES), out_index),
        out_shape=jax.ShapeDtypeStruct((S, D_PAD), BF16),
        scratch_shapes=[pltpu.VMEM((S, LANES), F32)],
        input_output_aliases={6: 0},
        compiler_params=_cparams(("arbitrary",)),
    )(dq3, dk3, dv3, *tables, dproj)


DIL_NB = S // DIL_BLK


_BQK = (((2,), (2,)), ((0,), (0,)))
_BQD = (((2,), (1,)), ((0,), (0,)))
_BKD = (((1,), (1,)), ((0,), (0,)))


def _dil_mask(g):
    shape = (DIL_NB, DIL_BLK, 2 * DIL_BLK)
    blocks_per_seq = jnp.right_shift(DIL_NB, 2 * g)
    has_prev = jnp.bitwise_and(lax.broadcasted_iota(jnp.int32, shape, 0), blocks_per_seq - 1) != 0
    a = lax.broadcasted_iota(jnp.int32, shape, 1)
    kk = lax.broadcasted_iota(jnp.int32, shape, 2)
    diff = DIL_BLK + a - kk
    return (diff >= 0) & (diff <= DIL_BLK) & ((kk >= DIL_BLK) | has_prev)


def _blocks(t):
    return t.reshape(DIL_NB, DIL_BLK, t.shape[-1])


def _with_prev(t):
    prev = jnp.concatenate([jnp.zeros((DIL_BLK, t.shape[-1]), t.dtype), t[:-DIL_BLK]], axis=0)
    return jnp.concatenate([_blocks(prev), _blocks(t)], axis=1)


def _fold_prev(t2):
    n = t2.shape[-1]
    to_prev = t2[:, :DIL_BLK].reshape(S, n)
    own = t2[:, DIL_BLK:].reshape(S, n)
    return own + jnp.concatenate([to_prev[DIL_BLK:], jnp.zeros((DIL_BLK, n), t2.dtype)], axis=0)


def _dil_group_spec(t, width=DIL_OUT):
    return pl.BlockSpec((None, S, width), lambda g: (t * DIL_G + g, 0, 0))


def _dil_fwd(qkv3):
    def body(q_ref, k_ref, v_ref, o_ref, lse_ref):
        ok = _dil_mask(pl.program_id(0))
        lane = lax.broadcasted_iota(jnp.int32, (S, LANES), 1)
        lse = jnp.zeros((S, LANES), F32)
        for h in range(DIL_HG):
            sl = slice(h * HD, (h + 1) * HD)
            s = jnp.where(ok, _dot(_blocks(q_ref[:, sl]), _with_prev(k_ref[:, sl]), _BQK) * SCALE, NEG_INF)
            m = jnp.max(s, axis=-1, keepdims=True)
            p = jnp.exp(s - m)
            l = jnp.sum(p, axis=-1, keepdims=True)
            o_ref[:, sl] = _dot(p / l, _with_prev(v_ref[:, sl]), _BQD).reshape(S, HD)
            lse = jnp.where(lane == h, (m + jnp.log(l)).reshape(S, 1), lse)
        lse_ref[...] = lse

    return pl.pallas_call(
        body, name="dil_fwd", grid=(DIL_G,),
        in_specs=[_dil_group_spec(0), _dil_group_spec(1), _dil_group_spec(2)],
        out_specs=[_dil_group_spec(0), _dil_group_spec(0, LANES)],
        out_shape=[jax.ShapeDtypeStruct((DIL_G, S, DIL_OUT), F32), jax.ShapeDtypeStruct((DIL_G, S, LANES), F32)],
        compiler_params=_cparams(("parallel",)),
    )(qkv3, qkv3, qkv3)


def _dil_bwd(qkv3, lse3, do3, c3):
    def body(q_ref, k_ref, v_ref, lse_ref, do_ref, c_ref, dq_ref, dk_ref, dv_ref):
        ok = _dil_mask(pl.program_id(0))
        lse = lse_ref[...]
        c = c_ref[...]
        for h in range(DIL_HG):
            sl = slice(h * HD, (h + 1) * HD)
            q = _blocks(q_ref[:, sl])
            k2 = _with_prev(k_ref[:, sl])
            do_h = _blocks(do_ref[:, sl])
            s = jnp.where(ok, _dot(q, k2, _BQK) * SCALE, NEG_INF)
            p = jnp.exp(s - _blocks(lse[:, h:h + 1]))
            dp = _dot(do_h, _with_prev(v_ref[:, sl]), _BQK)
            ds = p * (dp - _blocks(c[:, h:h + 1]))
            dsb = (ds * SCALE).astype(BF16)
            dq_ref[:, sl] = _dot(dsb, k2, _BQD).reshape(S, HD)
            dk_ref[:, sl] = _fold_prev(_dot(dsb, q, _BKD))
            dv_ref[:, sl] = _fold_prev(_dot(p, do_h, _BKD))

    return pl.pallas_call(
        body, name="dil_bwd", grid=(DIL_G,),
        in_specs=[_dil_group_spec(0), _dil_group_spec(1), _dil_group_spec(2), _dil_group_spec(0, LANES),
                  _dil_group_spec(0), _dil_group_spec(0, LANES)],
        out_specs=[_dil_group_spec(0)] * 3,
        out_shape=[jax.ShapeDtypeStruct((DIL_G, S, DIL_OUT), F32)] * 3,
        compiler_params=_cparams(("parallel",)),
    )(qkv3, qkv3, qkv3, lse3, do3, c3)


COMB_T = 256


def _dil_combine(o3, lse3):
    heads_per_lb = LANES // HD

    def body(o_ref, lse_ref, ob_ref, al_ref, *scratch):
        o_tok = [scratch[LB_PER_CB * gg:LB_PER_CB * (gg + 1)] for gg in range(DIL_G)]
        lse_tok = scratch[LB_PER_CB * DIL_G:]
        g = pl.program_id(0)
        for gs in range(DIL_G):
            @pl.when(g == gs)
            def _(gs=gs):
                for tok, sub in _residue_pieces(DIL_R[gs]):
                    for hf in range(LB_PER_CB):
                        o_tok[gs][hf][tok, :] = o_ref[sub, hf * LANES:(hf + 1) * LANES]
                    lse_tok[gs][tok, :] = lse_ref[sub, :]

        @pl.when(g == DIL_G - 1)
        def _():
            def chunk(i, carry):
                rows = pl.ds(pl.multiple_of(i * COMB_T, COMB_T), COMB_T)
                lse = [lse_tok[gg][rows, :] for gg in range(DIL_G)]
                m = jnp.maximum(jnp.maximum(lse[0], lse[1]), lse[2])
                e = [jnp.exp(lse[gg] - m) for gg in range(DIL_G)]
                den = (e[0] + e[1]) + e[2]
                al = [e[gg] / den for gg in range(DIL_G)]
                for gg in range(DIL_G):
                    al_ref[gg, rows, :] = al[gg]
                for h in range(DIL_HG):
                    hf, sl = h // heads_per_lb, slice((h % heads_per_lb) * HD, (h % heads_per_lb + 1) * HD)
                    acc = al[0][:, h:h + 1] * o_tok[0][hf][rows, sl]
                    for gg in range(1, DIL_G):
                        acc = acc + al[gg][:, h:h + 1] * o_tok[gg][hf][rows, sl]
                    ob_ref[rows, h * HD:(h + 1) * HD] = acc
                return carry

            lax.fori_loop(0, S // COMB_T, chunk, 0)

    return pl.pallas_call(
        body, name="dil_combine", grid=(DIL_G,),
        in_specs=[pl.BlockSpec((None, S, DIL_OUT), lambda g: (g, 0, 0)),
                  pl.BlockSpec((None, S, LANES), lambda g: (g, 0, 0))],
        out_specs=[pl.BlockSpec((S, DIL_OUT), lambda g: (0, 0)),
                   pl.BlockSpec((DIL_G, S, LANES), lambda g: (0, 0, 0))],
        out_shape=[jax.ShapeDtypeStruct((S, DIL_OUT), F32), jax.ShapeDtypeStruct((DIL_G, S, LANES), F32)],
        scratch_shapes=[pltpu.VMEM((S, LANES), F32)] * (DIL_G * (LB_PER_CB + 1)),
        compiler_params=_cparams(("arbitrary",)),
    )(o3, lse3)


def _dil_combine_bwd(dob, ob, alpha, deps=()):
    heads_per_lb = LANES // HD

    def body(dob_ref, ob_ref, al_ref, *refs):
        do_ref, c_ref = refs[len(deps):]
        g = pl.program_id(0)
        hf = pl.program_id(1)
        for gs in range(DIL_G):
            @pl.when(g == gs)
            def _(gs=gs):
                for tok, sub in _residue_pieces(DIL_R[gs]):
                    dob = dob_ref[tok, :]
                    prod = dob * ob_ref[tok, :]
                    al = al_ref[tok, :]
                    lane = lax.broadcasted_iota(jnp.int32, al.shape, 1)
                    c = jnp.where(hf == 0, 0.0, c_ref[sub, :])
                    for hh in range(heads_per_lb):
                        sl = slice(hh * HD, (hh + 1) * HD)
                        head = hf * heads_per_lb + hh
                        a = jnp.sum(jnp.where(lane == head, al, 0.0), axis=-1, keepdims=True)
                        do_ref[sub, sl] = (a * dob[:, sl]).astype(BF16)
                        c = jnp.where(lane == head, a * jnp.sum(prod[:, sl], axis=-1, keepdims=True), c)
                    c_ref[sub, :] = c

    half = pl.BlockSpec((S, LANES), lambda g, hf: (0, hf))
    return pl.pallas_call(
        body, name="dil_combine_bwd", grid=(DIL_G, LB_PER_CB),
        in_specs=[half, half, pl.BlockSpec((None, S, LANES), lambda g, hf: (g, 0, 0))] + [ANY] * len(deps),
        out_specs=[pl.BlockSpec((None, S, LANES), lambda g, hf: (g, 0, hf)),
                   pl.BlockSpec((None, S, LANES), lambda g, hf: (g, 0, 0))],
        out_shape=[jax.ShapeDtypeStruct((DIL_G, S, DIL_OUT), BF16), jax.ShapeDtypeStruct((DIL_G, S, LANES), F32)],
        compiler_params=_cparams(("parallel", "arbitrary")),
    )(dob, ob, alpha, *deps)


def _local_step(x, tgt, g_attn, g_mlp, g_final, b_pad, w_in_t, hooks):
    tables = _rope_tables()

    h1 = _rms_fwd(x, g_attn, "rms_attn_fwd", deps=hooks.first_deps())
    proj_a = _mm(h1, w_in_t, mode="nt", tm=S, tn=512, n=3 * FOX_W, out_dtypes=[BF16], name="proj_a")
    proj_b = _mm(h1, w_in_t, mode="nt", tm=S, tn=CB, n=3 * DIL_W, b_off=SEG_B // CB, out_dtypes=[F32], name="proj_b")
    proj_g = _mm(h1, w_in_t, mode="nt", tm=S, tn=CB, n=2 * D, b_off=SEG_G // CB, out_dtypes=[F32], name="proj_g")
    proj_f = _mm(h1, w_in_t, mode="nt", tm=S, tn=LANES, n=LANES, b_off=SEG_F // LANES, out_dtypes=[F32], name="proj_f")

    q_aug, k_aug = _fox_prepare(proj_a, proj_f, b_pad)
    oa, lse_a = _fox_fwd(q_aug, k_aug, proj_a)

    qkv3 = _rope_split(proj_b, tables)
    o3, lse3 = _dil_fwd(qkv3)
    ob, alpha = _dil_combine(o3, lse3)

    w_a, w_b, w_out = hooks.mixer_weights(ob)
    ya = _mm(oa, w_a, mode="nn", tm=S, tn=CB, out_dtypes=[F32], name="branch_a")
    yb = _mm(ob, w_b, mode="nn", tm=S, tn=CB, out_dtypes=[F32], name="branch_b")
    mixed = _gate_fwd(proj_g, ya, yb)
    x2 = _mm(mixed, w_out, mode="nn", tm=S, tn=CB, out_dtypes=[F32], name="out_proj",
             epilogue=lambda acc, res: (res + acc,), extras=(x,))

    h2 = _rms_fwd(x2, g_mlp, "rms_mlp_fwd")
    w_up_sh, w_down = hooks.mlp_weights(h2)

    def up_epilogue(acc):
        r = jnp.maximum(acc, 0.0)
        return acc, r * r

    u, act = _mm(h2, w_up_sh, mode="nn", tm=S, tn=DFF // N_DEV, b_sharded=True, out_dtypes=[F32, BF16],
                 name="mlp_up", epilogue=up_epilogue)
    x3 = _mm(act, w_down, mode="nn", tm=512, tn=512, out_dtypes=[F32], name="mlp_down",
             epilogue=lambda acc, res: (res + acc,), extras=(x2,))

    dx3, dg_final, loss = _final_norm_loss(x3, g_final, tgt)

    du = _mm(dx3, w_down, mode="nt", tm=S, tn=CB, out_dtypes=[BF16], name="mlp_down_bwd",
             epilogue=lambda acc, u_t: (acc * (2.0 * jnp.maximum(u_t, 0.0)),), extras=(u,))
    dw_down = _mm(act, dx3, mode="tn", tm=512, tn=512, out_dtypes=[F32], name="dw_down")
    dw_up_sh = _mm(h2, du, mode="tn", tm=D, tn=DFF // N_DEV, out_sharded=True, out_dtypes=[F32], name="dw_up")
    dh2 = _mm(du, w_up_sh, mode="nt", tm=512, tn=512, b_sharded=True, out_dtypes=[F32], name="mlp_up_bwd",
              deps=hooks.mlp_grads(dw_up_sh, dw_down))
    dx2, dg_mlp = _rms_bwd(x2, g_mlp, dh2, dx3, "rms_mlp_bwd")

    dmixed = _mm(dx2, w_out, mode="nt", tm=S, tn=CB, out_dtypes=[F32], name="out_proj_bwd")
    dw_out = _mm(mixed, dx2, mode="tn", tm=D, tn=CB, out_dtypes=[F32], name="dw_out")

    dyab, dproj = _gate_bwd(dmixed, proj_g, ya, yb)
    doa = _mm(dyab, w_a, mode="nt", tm=S, tn=CB, k=D, a_off=0, out_dtypes=[BF16], name="branch_a_bwd")
    dw_a = _mm(oa, dyab, mode="tn", tm=FOX_W, tn=CB, n=D, b_off=0, out_dtypes=[F32], name="dw_branch_a")
    dob = _mm(dyab, w_b, mode="nt", tm=S, tn=CB, k=D, a_off=1, out_dtypes=[F32], name="branch_b_bwd")
    dw_b = _mm(ob, dyab, mode="tn", tm=DIL_OUT, tn=CB, n=D, b_off=D // CB, out_dtypes=[F32], name="dw_branch_b")

    dproj, dft, dfs = _fox_bwd(q_aug, k_aug, proj_a, lse_a, doa, dproj, deps=hooks.mixer_grads(dw_a, dw_b, dw_out))
    dproj, db = _fox_dscan(dft, dfs, proj_f, b_pad, dproj)

    do3, c3 = _dil_combine_bwd(dob, ob, alpha, deps=hooks.mid_backward(db))
    dq3, dk3, dv3 = _dil_bwd(qkv3, lse3, do3, c3)
    dproj = _rope_merge_bwd(dq3, dk3, dv3, tables, dproj)

    dw_in_t = _mm(dproj, h1, mode="tn", tm=CB, tn=D, out_dtypes=[F32], name="dw_in")
    dh1 = _mm(dproj, w_in_t, mode="nn", tm=512, tn=CB, out_dtypes=[F32], name="proj_bwd",
              deps=hooks.w_in_grad(dw_in_t))
    dx, dg_attn = _rms_bwd(x, g_attn, dh1, dx2, "rms_attn_bwd")

    return loss, dx, (dg_attn, db, dg_mlp, dg_final)


SHARD_SHAPES = ((D_IN // N_DEV, D), (FOX_W, D // N_DEV), (DIL_OUT, D // N_DEV), (D // N_DEV, D),
                (D, DFF // N_DEV), (DFF // N_DEV, D))
W_NAMES = ("w_in", "w_a", "w_b", "w_out", "w_up", "w_down")
AG_COPIES = 7
HBM = pl.BlockSpec(memory_space=pltpu.HBM)
SEM = pl.BlockSpec(memory_space=pltpu.SEMAPHORE)


def _place():
    return lax.axis_index("x"), lax.axis_index("y"), lax.axis_index("c")


def _all_gather(shards):
    n_w = len(shards)
    shapes = [t.shape for t in shards]

    def body(*refs):
        in_refs, out_refs = refs[:n_w], refs[n_w:2 * n_w]
        stage, cast = refs[2 * n_w:3 * n_w], refs[3 * n_w:4 * n_w]
        send_sems, recv_sems, local_sems, load_sems = refs[4 * n_w:]
        x, y, c = _place()
        me, sibling = (x, y, c), (x, y, 1 - c)
        chips = [(1 - x, y), (x, 1 - y), (1 - x, 1 - y)]

        def slot(i, px, py, pc):
            return out_refs[i].at[4 * px + 2 * py + pc]

        def copy(i, k, block, to, src=None):
            return pltpu.make_async_remote_copy(
                src_ref=slot(i, *block) if src is None else src, dst_ref=slot(i, *block),
                send_sem=send_sems.at[i * AG_COPIES + k], recv_sem=recv_sems.at[i * AG_COPIES + k],
                device_id=to, device_id_type=MESH)

        loads = [pltpu.make_async_copy(in_refs[i], stage[i], load_sems.at[i]) for i in range(n_w)]
        for ld in loads:
            ld.start()
        local, remote = [], []
        for i in range(n_w):
            loads[i].wait()
            cast[i][...] = stage[i][...].astype(BF16)
            local.append(pltpu.make_async_copy(cast[i], slot(i, *me), local_sems.at[i]))
            local[-1].start()
            first = [copy(i, 0, me, sibling, src=cast[i])]
            first += [copy(i, 1 + j, me, (*chip, c), src=cast[i]) for j, chip in enumerate(chips)]
            for cp in first:
                cp.start()
            remote += first
        for i in range(n_w):
            for j, chip in enumerate(chips):
                copy(i, 1 + j, (*chip, c), me).wait_recv()
                remote.append(copy(i, 4 + j, (*chip, c), sibling))
                remote[-1].start()
        for i in range(n_w):
            copy(i, 0, sibling, me).wait_recv()
            for j, chip in enumerate(chips):
                copy(i, 4 + j, (*chip, 1 - c), me).wait_recv()
        for cp in remote:
            cp.wait_send()
        for cp in local:
            cp.wait()

    return pl.pallas_call(
        body, name="all_gather_w_in",
        out_shape=[jax.ShapeDtypeStruct((N_DEV,) + sh, BF16) for sh in shapes],
        in_specs=[ANY] * n_w, out_specs=[ANY] * n_w,
        scratch_shapes=[pltpu.VMEM(sh, F32) for sh in shapes] + [pltpu.VMEM(sh, BF16) for sh in shapes]
                       + [pltpu.SemaphoreType.DMA((n_w * AG_COPIES,)), pltpu.SemaphoreType.DMA((n_w * AG_COPIES,)),
                          pltpu.SemaphoreType.DMA((n_w,)), pltpu.SemaphoreType.DMA((n_w,))],
        compiler_params=_cparams(None),
    )(*shards)


def _place_own(shards):
    n_w = len(shards)
    shapes = [t.shape for t in shards]

    def body(*refs):
        in_refs, out_refs = refs[:n_w], refs[n_w:2 * n_w]
        stage, cast = refs[2 * n_w:3 * n_w], refs[3 * n_w:4 * n_w]
        load_sems, store_sems = refs[4 * n_w:]
        x, y, c = _place()
        loads = [pltpu.make_async_copy(in_refs[i], stage[i], load_sems.at[i]) for i in range(n_w)]
        for ld in loads:
            ld.start()
        stores = []
        for i in range(n_w):
            loads[i].wait()
            cast[i][...] = stage[i][...].astype(BF16)
            stores.append(pltpu.make_async_copy(cast[i], out_refs[i].at[4 * x + 2 * y + c], store_sems.at[i]))
            stores[-1].start()
        for st in stores:
            st.wait()

    return pl.pallas_call(
        body, name="place_own_shards",
        out_shape=[jax.ShapeDtypeStruct((N_DEV,) + sh, BF16) for sh in shapes],
        in_specs=[ANY] * n_w, out_specs=[ANY] * n_w,
        scratch_shapes=[pltpu.VMEM(sh, F32) for sh in shapes] + [pltpu.VMEM(sh, BF16) for sh in shapes]
                       + [pltpu.SemaphoreType.DMA((n_w,)), pltpu.SemaphoreType.DMA((n_w,))],
        compiler_params=_cparams(None),
    )(*shards)


PEER_COPIES = N_DEV - 1
SIBLING_COPIES = 4
CHIP_COPIES = 3


def _peer_copies(_, land_refs, send_sems, recv_sems):
    x, y, c = _place()
    mine = 4 * x + 2 * y + c
    copies = []
    for i, ref in enumerate(land_refs):
        for k in range(1, N_DEV):
            peer = (x ^ (k >> 2), y ^ ((k >> 1) & 1), c ^ (k & 1))
            n = i * PEER_COPIES + k - 1
            copies.append(pltpu.make_async_remote_copy(
                src_ref=ref.at[mine], dst_ref=ref.at[mine], send_sem=send_sems.at[n], recv_sem=recv_sems.at[n],
                device_id=peer, device_id_type=MESH))
    return copies


def _sibling_copies(g_refs, r_refs, send_sems, recv_sems):
    x, y, c = _place()
    return [pltpu.make_async_remote_copy(
        src_ref=g_refs[i].at[2 * k + (1 - c)], dst_ref=r_refs[i].at[k],
        send_sem=send_sems.at[SIBLING_COPIES * i + k], recv_sem=recv_sems.at[SIBLING_COPIES * i + k],
        device_id=(x, y, 1 - c), device_id_type=MESH) for i in range(len(g_refs)) for k in range(SIBLING_COPIES)]


def _chip_copies(p_refs, r_refs, send_sems, recv_sems):
    x, y, c = _place()
    chips = [(1 - x, y), (x, 1 - y), (1 - x, 1 - y)]
    return [pltpu.make_async_remote_copy(
        src_ref=p_refs[i].at[2 * cx + cy], dst_ref=r_refs[i].at[j],
        send_sem=send_sems.at[CHIP_COPIES * i + j], recv_sem=recv_sems.at[CHIP_COPIES * i + j],
        device_id=(cx, cy, c), device_id_type=MESH) for i in range(len(p_refs)) for j, (cx, cy) in enumerate(chips)]


def _in_hbm(a):
    return pltpu.with_memory_space_constraint(a, pltpu.HBM)


def _exchange_start(name, copies_fn, n_copies, srcs, lands, after=()):
    n_s, n_l, n_a = len(srcs), len(lands), len(after)

    def body(*refs):
        outs = refs[n_s + n_l + n_a:]
        for cp in copies_fn(refs[:n_s], refs[n_s:n_s + n_l], outs[0], outs[1]):
            cp.start()
        outs[-1][...] = jnp.zeros_like(outs[-1])

    res = pl.pallas_call(
        body, name=name,
        out_shape=[pltpu.SemaphoreType.DMA((n_copies,)), pltpu.SemaphoreType.DMA((n_copies,))]
                  + [pltpu.HBM(t.shape, t.dtype) for t in (*srcs, *lands)] + [jax.ShapeDtypeStruct((8, LANES), F32)],
        in_specs=[HBM] * (n_s + n_l) + [ANY] * n_a,
        out_specs=[SEM, SEM] + [HBM] * (n_s + n_l) + [pl.BlockSpec(memory_space=pltpu.VMEM)],
        input_output_aliases={i: 2 + i for i in range(n_s + n_l)},
        compiler_params=pltpu.CompilerParams(has_side_effects=pltpu.SideEffectType.DATAFLOW_SIDE_EFFECTING),
    )(*[_in_hbm(t) for t in (*srcs, *lands)], *after)
    return res[0], res[1], list(res[2:2 + n_s]), list(res[2 + n_s:2 + n_s + n_l]), res[-1]


def _exchange_wait(name, copies_fn, started, after):
    send_sems, recv_sems, srcs, lands, _ = started
    n_s, n_l = len(srcs), len(lands)

    def body(*refs):
        for cp in copies_fn(refs[:n_s], refs[n_s:n_s + n_l], refs[n_s + n_l], refs[n_s + n_l + 1]):
            cp.wait_send()
            cp.wait_recv()

    res = pl.pallas_call(
        body, name=name,
        out_shape=[pltpu.HBM(t.shape, t.dtype) for t in (*srcs, *lands)],
        in_specs=[HBM] * (n_s + n_l) + [SEM, SEM, ANY],
        out_specs=[HBM] * (n_s + n_l),
        input_output_aliases={i: i for i in range(n_s + n_l)},
        compiler_params=pltpu.CompilerParams(has_side_effects=pltpu.SideEffectType.DATAFLOW_SIDE_EFFECTING),
    )(*srcs, *lands, send_sems, recv_sems, after)
    return list(res[:n_s]), list(res[n_s:])


def _shard_block(shape):
    rows, cols = shape
    if rows % 256:
        return rows, 256
    return min(rows, 512 if cols <= LANES else 256), cols


def _chip_partial(ids, g_stack, recv1, name):
    shape = g_stack.shape[1:]
    br, bc = _shard_block(shape)

    def body(ids_ref, g_ref, r_ref, pb_ref, own_ref):
        s = g_ref[...] + r_ref[...]
        pb_ref[...] = s.astype(BF16)

        @pl.when(pl.program_id(2) == ids_ref[1])
        def _():
            own_ref[...] = s

    grid_spec = pltpu.PrefetchScalarGridSpec(
        num_scalar_prefetch=1, grid=(shape[0] // br, shape[1] // bc, 4),
        in_specs=[pl.BlockSpec((None, br, bc), lambda r, q, k, ids: (2 * k + ids[0], r, q)),
                  pl.BlockSpec((None, br, bc), lambda r, q, k, ids: (k, r, q))],
        out_specs=[pl.BlockSpec((None, br, bc), lambda r, q, k, ids: (k, r, q)),
                   pl.BlockSpec((br, bc), lambda r, q, k, ids: (r, q))])
    return pl.pallas_call(
        body, name=name, grid_spec=grid_spec,
        out_shape=[jax.ShapeDtypeStruct((4,) + shape, BF16), jax.ShapeDtypeStruct(shape, F32)],
        compiler_params=_cparams(("parallel", "parallel", "arbitrary")),
    )(ids, g_stack, recv1)


def _adamw(w, g, m, v):
    m = ADAM_B1 * m + (1.0 - ADAM_B1) * g
    v = ADAM_B2 * v + (1.0 - ADAM_B2) * (g * g)
    m_hat = m / (1.0 - ADAM_B1 ** ADAM_STEP)
    v_hat = v / (1.0 - ADAM_B2 ** ADAM_STEP)
    delta = -ADAM_LR * (m_hat / (jnp.sqrt(v_hat) + ADAM_EPS) + ADAM_WD * w)
    return delta, m, v


def _reduce_adamw(own, recv2, w, m, v, name, deps=()):
    shape = own.shape
    br, bc = _shard_block(shape)

    def body(own_ref, r_ref, w_ref, m_ref, v_ref, *refs):
        g_ref, d_ref, nm_ref, nv_ref = refs[len(deps):]
        g = own_ref[...]
        for j in range(3):
            g = g + r_ref[j].astype(F32)
        delta, nm, nv = _adamw(w_ref[...], g, m_ref[...], v_ref[...])
        g_ref[...] = g
        d_ref[...] = delta
        nm_ref[...] = nm
        nv_ref[...] = nv

    blk = pl.BlockSpec((br, bc), lambda r, q: (r, q))
    return pl.pallas_call(
        body, name=name, grid=(shape[0] // br, shape[1] // bc),
        in_specs=[blk, pl.BlockSpec((3, br, bc), lambda r, q: (0, r, q)), blk, blk, blk] + [ANY] * len(deps),
        out_specs=[blk] * 4, out_shape=[jax.ShapeDtypeStruct(shape, F32)] * 4,
        compiler_params=_cparams(("parallel", "parallel")),
    )(own, recv2, w, m, v, *deps)


def _small_allreduce_adamw(gvec, w, m, v, deps=()):
    def body(g_ref, w_ref, m_ref, v_ref, *refs):
        go_ref, d_ref, nm_ref, nv_ref, buf, send_sems, recv_sems = refs[len(deps):]
        x, y, c = _place()
        my_slot = 4 * x + 2 * y + c
        buf[my_slot] = g_ref[...]
        copies = []
        for k in range(1, N_DEV):
            px, py, pc = x ^ (k >> 2), y ^ ((k >> 1) & 1), c ^ (k & 1)
            copies.append(pltpu.make_async_remote_copy(
                src_ref=g_ref, dst_ref=buf.at[my_slot], send_sem=send_sems.at[k - 1], recv_sem=recv_sems.at[k - 1],
                device_id=(px, py, pc), device_id_type=MESH))
        for cp in copies:
            cp.start()
        for k in range(1, N_DEV):
            px, py, pc = x ^ (k >> 2), y ^ ((k >> 1) & 1), c ^ (k & 1)
            pltpu.make_async_remote_copy(
                src_ref=g_ref, dst_ref=buf.at[4 * px + 2 * py + pc], send_sem=send_sems.at[k - 1],
                recv_sem=recv_sems.at[k - 1], device_id=(px, py, pc), device_id_type=MESH).wait_recv()
        for cp in copies:
            cp.wait_send()
        g = buf[0]
        for s in range(1, N_DEV):
            g = g + buf[s]
        delta, nm, nv = _adamw(w_ref[...], g, m_ref[...], v_ref[...])
        go_ref[...] = g
        d_ref[...] = delta
        nm_ref[...] = nm
        nv_ref[...] = nv

    vm = pl.BlockSpec(memory_space=pltpu.VMEM)
    return pl.pallas_call(
        body, name="small_allreduce_adamw",
        in_specs=[vm] * 4 + [ANY] * len(deps), out_specs=[vm] * 4,
        out_shape=[jax.ShapeDtypeStruct((SMALL_R, LANES), F32)] * 4,
        scratch_shapes=[pltpu.VMEM((N_DEV, SMALL_R, LANES), F32),
                        pltpu.SemaphoreType.DMA((N_DEV - 1,)), pltpu.SemaphoreType.DMA((N_DEV - 1,))],
    )(gvec, w, m, v, *deps)


def _cols_to_whole(stack):
    rows = stack.shape[1]
    return stack.transpose(1, 0, 2).reshape(rows, -1)


def _whole_to_cols(t):
    rows = t.shape[0]
    return t.reshape(rows, N_DEV, -1).transpose(1, 0, 2)


W_IN_SEGMENTS = ((0, 3 * FOX_W, SEG_A), (3 * FOX_W, 3 * FOX_W + FOX_H, SEG_F),
                 (3 * FOX_W + FOX_H, 3 * FOX_W + FOX_H + 3 * DIL_W, SEG_B), (3 * FOX_W + FOX_H + 3 * DIL_W, D_IN, SEG_G))


def _shard_pieces(p):
    rows = D_IN // N_DEV
    lo, hi = p * rows, (p + 1) * rows
    return [(max(lo, a) - lo, min(hi, b) - lo, pad + max(lo, a) - a)
            for a, b, pad in W_IN_SEGMENTS if max(lo, a) < min(hi, b)]


def _pad_w_in_t(stack):
    wt = stack.reshape(D_IN, D)
    placed = sorted((pad, wt[a:b]) for a, b, pad in W_IN_SEGMENTS)
    parts, row = [], 0
    for pad, piece in placed + [(D_PAD, None)]:
        if pad > row:
            parts.append(jnp.zeros((pad - row, D), wt.dtype))
        if piece is not None:
            parts.append(piece)
            row = pad + piece.shape[0]
    return jnp.concatenate(parts, axis=0)


def _unpad_dw_in_t(dwp):
    return jnp.stack([jnp.concatenate([dwp[pad:pad + b - a] for a, b, pad in _shard_pieces(p)], axis=0)
                      for p in range(N_DEV)])


LOSS_ROW = 25


def _pack_small(g_attn, b, g_mlp, g_final, loss_row=None):
    tail = jnp.pad(b, ((0, 7), (0, LANES - b.shape[1])))
    if loss_row is not None:
        tail = tail + jnp.pad(loss_row, ((LOSS_ROW - 24, 31 - LOSS_ROW), (0, 0)))
    return jnp.concatenate([g_attn.reshape(8, LANES), g_mlp.reshape(8, LANES), g_final.reshape(8, LANES), tail], axis=0)


def _unpack_small(p):
    return (p[0:8].reshape(1, D), p[24:25, :FOX_H], p[8:16].reshape(1, D), p[16:24].reshape(D))


class _StepComm:
    def __init__(self, ids, w_sh, m_sh, v_sh, after_w_in):
        self.ids = ids
        self.w_sh, self.m_sh, self.v_sh = w_sh, m_sh, v_sh
        self.updates = None
        lands = _place_own(w_sh[1:])
        self.gather_mixer = _exchange_start("gather_mixer_start", _peer_copies, PEER_COPIES * 3, [], lands[:3],
                                            after=(after_w_in,))
        self.gather_mlp = _exchange_start("gather_mlp_start", _peer_copies, PEER_COPIES * 2, [], lands[3:],
                                          after=(self.gather_mixer[-1],))
        self.sibling = {}
        self.chips = {}
        self.own = {}
        self.names = {}

    def _reduce_start(self, group, names, grads, after=()):
        lands = [lax.empty((SIBLING_COPIES,) + t.shape[1:], F32) for t in grads]
        self.sibling[group] = _exchange_start("grad_%s_sibling_start" % group, _sibling_copies,
                                              SIBLING_COPIES * len(grads), grads, lands, after=after)
        self.names[group] = names
        return self.sibling[group][-1]

    def _reduce_mid(self, group, after):
        grads, recv1 = _exchange_wait("grad_%s_sibling_wait" % group, _sibling_copies, self.sibling[group], after)
        parts = [_chip_partial(self.ids, g, r, "grad_partial_" + n) for g, r, n in zip(grads, recv1, self.names[group])]
        self.own[group] = [p[1] for p in parts]
        srcs = [p[0] for p in parts]
        lands = [lax.empty((CHIP_COPIES,) + t.shape[1:], BF16) for t in srcs]
        self.chips[group] = _exchange_start("grad_%s_chips_start" % group, _chip_copies, CHIP_COPIES * len(srcs),
                                            srcs, lands)
        return self.chips[group][-1]

    def reduced(self, group, after):
        _, recv2 = _exchange_wait("grad_%s_chips_wait" % group, _chip_copies, self.chips[group], after)
        return list(zip(self.own[group], recv2))

    def first_deps(self):
        return [self.gather_mlp[-1]]

    def mixer_weights(self, after):
        _, (g_a, g_b, g_out) = _exchange_wait("gather_mixer_wait", _peer_copies, self.gather_mixer, after)
        return _cols_to_whole(g_a), _cols_to_whole(g_b), g_out.reshape(D, D)

    def mlp_weights(self, after):
        _, (g_up, g_down) = _exchange_wait("gather_mlp_wait", _peer_copies, self.gather_mlp, after)
        return g_up, g_down.reshape(DFF, D)

    def mlp_grads(self, dw_up_sh, dw_down):
        return [self._reduce_start("mlp", W_NAMES[4:], [dw_up_sh, dw_down.reshape((N_DEV,) + SHARD_SHAPES[5])])]

    def mixer_grads(self, dw_a, dw_b, dw_out):
        token = self._reduce_mid("mlp", dw_b)
        grads = [_whole_to_cols(dw_a), _whole_to_cols(dw_b), dw_out.reshape((N_DEV,) + SHARD_SHAPES[3])]
        return [self._reduce_start("mixer", W_NAMES[1:4], grads, after=(token,))]

    def mid_backward(self, after):
        return [self._reduce_mid("mixer", after)]

    def w_in_grad(self, dw_in_t):
        grads = [_unpad_dw_in_t(dw_in_t)]
        token = self._reduce_start("w_in", W_NAMES[:1], grads)
        reduced = self.reduced("mixer", token) + self.reduced("mlp", token)
        self.updates = [
            _reduce_adamw(*r, self.w_sh[1 + i], self.m_sh[1 + i], self.v_sh[1 + i], "adamw_" + W_NAMES[1 + i], deps=[token])
            for i, r in enumerate(reduced)]
        return [self._reduce_mid("w_in", self.updates[-1][0])]

    def w_in_update(self, after):
        (reduced,) = self.reduced("w_in", after)
        return _reduce_adamw(*reduced, self.w_sh[0], self.m_sh[0], self.v_sh[0], "adamw_" + W_NAMES[0])


def kernel(x, norm_attn_g, w_in, b_forget, w_branch_a, w_branch_b, w_out, norm_mlp_g, w_up, w_down, norm_final_g, loss_target, m_norm_attn_g, m_w_in, m_b_forget, m_w_branch_a, m_w_branch_b, m_w_out, m_norm_mlp_g, m_w_up, m_w_down, m_norm_final_g, v_norm_attn_g, v_w_in, v_b_forget, v_w_branch_a, v_w_branch_b, v_w_out, v_norm_mlp_g, v_w_up, v_w_down, v_norm_final_g):
    cx, cy, cc = _place()
    ids = jnp.stack([cc, 2 * cx + cy]).astype(jnp.int32)

    w_sh = [w_in[0].T] + [t[0] for t in (w_branch_a, w_branch_b, w_out, w_up, w_down)]
    m_sh = [m_w_in[0].T] + [t[0] for t in (m_w_branch_a, m_w_branch_b, m_w_out, m_w_up, m_w_down)]
    v_sh = [v_w_in[0].T] + [t[0] for t in (v_w_branch_a, v_w_branch_b, v_w_out, v_w_up, v_w_down)]

    (g_in,) = _all_gather(w_sh[:1])
    w_in_t = _pad_w_in_t(g_in)
    comm = _StepComm(ids, w_sh, m_sh, v_sh, g_in)
    b_pad = jnp.pad(b_forget, ((0, 0), (0, LANES - FOX_H)))

    loss_row, dx, dsmall = _local_step(
        x[0], loss_target[0], norm_attn_g, norm_mlp_g, norm_final_g.reshape(1, D), b_pad, w_in_t, comm)

    dg_attn, db, dg_mlp, dg_final = dsmall
    small = _small_allreduce_adamw(
        _pack_small(dg_attn, db, dg_mlp, dg_final, loss_row),
        _pack_small(norm_attn_g, b_forget, norm_mlp_g, norm_final_g.reshape(1, D)),
        _pack_small(m_norm_attn_g, m_b_forget, m_norm_mlp_g, m_norm_final_g.reshape(1, D)),
        _pack_small(v_norm_attn_g, v_b_forget, v_norm_mlp_g, v_norm_final_g.reshape(1, D)))
    big = [comm.w_in_update(small[0])] + comm.updates

    outs = [small[0][LOSS_ROW, 0], dx[None]]
    for q in range(4):
        s_attn, s_b, s_mlp, s_final = _unpack_small(small[q])
        b_in, b_a, b_b, b_out, b_up, b_down = [t[q][None] for t in big]
        b_in = jnp.swapaxes(b_in, 1, 2)
        outs += [s_attn, b_in, s_b, b_a, b_b, b_out, s_mlp, b_up, b_down, s_final]
    return tuple(outs)
```

```python
import functools

import jax
import jax.numpy as jnp
from jax import lax
from jax.experimental import pallas as pl
from jax.experimental.pallas import tpu as pltpu

F32 = jnp.float32
BF16 = jnp.bfloat16
MESH = pl.DeviceIdType.MESH

S = 2048
D = 1024
HD = 64
FOX_H = 8
FOX_W = FOX_H * HD
DIL_HG = 4
DIL_G = 3
DIL_W = DIL_G * DIL_HG * HD
DIL_OUT = DIL_HG * HD
DIL_BLK = 128
DIL_R = (1, 4, 16)
DFF = 4 * D
D_IN = 3 * FOX_W + FOX_H + 3 * DIL_W + 2 * D
EPS = 1e-6
NEG_INF = -1e30
SCALE = HD ** -0.5
ROPE_THETA = 500000.0
ROPE_DIM = HD // 4
N_DEV = 8

ADAM_LR = 0.001
ADAM_B1 = 0.9
ADAM_B2 = 0.999
ADAM_EPS = 1e-08
ADAM_WD = 0.01
ADAM_STEP = 10

LANES = 128
CB = 256
SEG_A = 0
SEG_B = 3 * FOX_W
SEG_Z = SEG_B + 3 * DIL_W
SEG_G = SEG_Z + CB
SEG_F = SEG_G + 2 * D
D_PAD = SEG_F + CB
SMALL_R = 32

VMEM_MB = 56


def _cparams(dims=None, vmem_mb=VMEM_MB, **kw):
    return pltpu.CompilerParams(dimension_semantics=dims, vmem_limit_bytes=vmem_mb << 20, **kw)


ANY = pl.BlockSpec(memory_space=pl.ANY)

_NN = (((1,), (0,)), ((), ()))
_NT = (((1,), (1,)), ((), ()))
_TN = (((0,), (0,)), ((), ()))


def _dot(a, b, dims):
    return lax.dot_general(a.astype(BF16), b.astype(BF16), dims, preferred_element_type=F32)


def _mm(a, b, *, mode, tm, tn, out_dtypes, name, n=None, k=None, a_off=0, b_off=0,
        b_sharded=False, out_sharded=False, epilogue=None, extras=(), deps=()):
    n_sh = b.shape[-1] if b_sharded else None
    if mode == "nn":
        m = a.shape[0]
        k = k or a.shape[1]
        a_spec = pl.BlockSpec((tm, k), lambda i, j: (i, a_off))
        if b_sharded:
            assert tn == n_sh
            n = N_DEV * n_sh
            b_spec = pl.BlockSpec((None, k, tn), lambda i, j: (j, 0, 0))
        else:
            n = n or b.shape[1]
            b_spec = pl.BlockSpec((k, tn), lambda i, j: (0, j + b_off))
        dims = _NN
    elif mode == "nt":
        m = a.shape[0]
        k = k or a.shape[1]
        a_spec = pl.BlockSpec((tm, k), lambda i, j: (i, a_off))
        if b_sharded:
            n = b.shape[1]
            b_spec = pl.BlockSpec((N_DEV, tn, n_sh), lambda i, j: (0, j, 0))
        else:
            n = n or b.shape[0]
            b_spec = pl.BlockSpec((tn, k), lambda i, j: (j + b_off, 0))
        dims = _NT
    else:
        k, m = a.shape
        n = n or b.shape[1]
        a_spec = pl.BlockSpec((k, tm), lambda i, j: (0, i))
        b_spec = pl.BlockSpec((k, tn), lambda i, j: (0, j + b_off))
        dims = _TN
    assert m % tm == 0 and n % tn == 0, (name, m, n, tm, tn)
    n_extra = len(extras)
    tile = pl.BlockSpec((tm, tn), lambda i, j: (i, j))
    if out_sharded:
        assert mode == "tn" and n // tn == N_DEV
        out_spec = pl.BlockSpec((None, tm, tn), lambda i, j: (j, i, 0))
        out_shape = (N_DEV, m, tn)
    else:
        out_spec, out_shape = tile, (m, n)

    def body(a_ref, b_ref, *refs):
        if mode == "nt" and b_sharded:
            acc = _dot(a_ref[:, 0:n_sh], b_ref[0], dims)
            for p in range(1, N_DEV):
                acc = acc + _dot(a_ref[:, p * n_sh:(p + 1) * n_sh], b_ref[p], dims)
        else:
            acc = _dot(a_ref[...], b_ref[...], dims)
        ex = [r[...] for r in refs[:n_extra]]
        outs = epilogue(acc, *ex) if epilogue is not None else (acc,)
        for o_ref, o in zip(refs[n_extra + len(deps):], outs):
            o_ref[...] = o.astype(o_ref.dtype)

    res = pl.pallas_call(
        body, name=name, grid=(m // tm, n // tn),
        in_specs=[a_spec, b_spec] + [tile] * n_extra + [ANY] * len(deps),
        out_specs=[out_spec] * len(out_dtypes),
        out_shape=[jax.ShapeDtypeStruct(out_shape, dt) for dt in out_dtypes],
        compiler_params=_cparams(("parallel", "parallel")),
    )(a, b, *extras, *deps)
    return res if len(out_dtypes) > 1 else res[0]


ROW_T = 256


def _rms_fwd(x, g, name, deps=()):
    def body(x_ref, g_ref, *refs):
        o_ref = refs[-1]
        x = x_ref[...]
        r = lax.rsqrt(jnp.mean(x * x, axis=-1, keepdims=True) + EPS)
        o_ref[...] = ((x * r) * g_ref[...]).astype(BF16)

    return pl.pallas_call(
        body, name=name, grid=(S // ROW_T,),
        in_specs=[pl.BlockSpec((ROW_T, D), lambda i: (i, 0)), pl.BlockSpec((1, D), lambda i: (0, 0))] + [ANY] * len(deps),
        out_specs=pl.BlockSpec((ROW_T, D), lambda i: (i, 0)),
        out_shape=jax.ShapeDtypeStruct((S, D), BF16),
        compiler_params=_cparams(("parallel",)),
    )(x, g, *deps)


def _rms_bwd(x, g, dh, dres, name):
    def body(x_ref, g_ref, dh_ref, dres_ref, dx_ref, dg_ref):
        @pl.when(pl.program_id(0) == 0)
        def _():
            dg_ref[...] = jnp.zeros_like(dg_ref)

        x = x_ref[...]
        dh = dh_ref[...]
        r = lax.rsqrt(jnp.mean(x * x, axis=-1, keepdims=True) + EPS)
        xn = x * r
        dhn = dh * g_ref[...]
        dx_ref[...] = dres_ref[...] + r * (dhn - xn * jnp.mean(dhn * xn, axis=-1, keepdims=True))
        dg_ref[...] += jnp.sum(dh * xn, axis=0, keepdims=True)

    row = pl.BlockSpec((ROW_T, D), lambda i: (i, 0))
    vec = pl.BlockSpec((1, D), lambda i: (0, 0))
    return pl.pallas_call(
        body, name=name, grid=(S // ROW_T,),
        in_specs=[row, vec, row, row], out_specs=[row, vec],
        out_shape=[jax.ShapeDtypeStruct((S, D), F32), jax.ShapeDtypeStruct((1, D), F32)],
        compiler_params=_cparams(("arbitrary",)),
    )(x, g, dh, dres)


def _final_norm_loss(x, g, tgt):
    def body(x_ref, g_ref, t_ref, dx_ref, dg_ref, loss_ref):
        @pl.when(pl.program_id(0) == 0)
        def _():
            dg_ref[...] = jnp.zeros_like(dg_ref)
            loss_ref[...] = jnp.zeros_like(loss_ref)

        x = x_ref[...]
        g = g_ref[...]
        r = lax.rsqrt(jnp.mean(x * x, axis=-1, keepdims=True) + EPS)
        xn = x * r
        err = xn * g - t_ref[...]
        row_loss = jnp.mean(err * err, axis=-1, keepdims=True)
        loss_ref[...] += 0.5 * jnp.sum(row_loss, axis=0, keepdims=True) * jnp.ones((1, LANES), F32)
        dy = err * (1.0 / D)
        dyn = dy * g
        dx_ref[...] = r * (dyn - xn * jnp.mean(dyn * xn, axis=-1, keepdims=True))
        dg_ref[...] += jnp.sum(dy * xn, axis=0, keepdims=True)

    row = pl.BlockSpec((ROW_T, D), lambda i: (i, 0))
    vec = pl.BlockSpec((1, D), lambda i: (0, 0))
    return pl.pallas_call(
        body, name="final_norm_loss", grid=(S // ROW_T,),
        in_specs=[row, vec, row],
        out_specs=[row, vec, pl.BlockSpec((1, LANES), lambda i: (0, 0))],
        out_shape=[jax.ShapeDtypeStruct((S, D), F32), jax.ShapeDtypeStruct((1, D), F32),
                   jax.ShapeDtypeStruct((1, LANES), F32)],
        compiler_params=_cparams(("arbitrary",)),
    )(x, g, tgt)


def _sigmoid(z):
    return 1.0 / (1.0 + jnp.exp(-z))


def _gate_fwd(proj_g, ya, yb):
    def body(ga_ref, gb_ref, ya_ref, yb_ref, o_ref):
        o_ref[...] = (_sigmoid(ga_ref[...]) * ya_ref[...] + _sigmoid(gb_ref[...]) * yb_ref[...]).astype(BF16)

    row = pl.BlockSpec((ROW_T, D), lambda i: (i, 0))
    return pl.pallas_call(
        body, name="gate_fwd", grid=(S // ROW_T,),
        in_specs=[row, pl.BlockSpec((ROW_T, D), lambda i: (i, 1)), row, row],
        out_specs=row, out_shape=jax.ShapeDtypeStruct((S, D), BF16),
        compiler_params=_cparams(("parallel",)),
    )(proj_g, proj_g, ya, yb)


GATE_TR = 1024
GATE_TC = 512
GATE_NB = 2 * D // GATE_TC


def _gate_bwd(dmixed, proj_g, ya, yb):
    half = GATE_NB // 2

    def body(dm_ref, g_ref, ya_ref, yb_ref, dy_ref, dg_ref):
        dm = dm_ref[...]
        s = _sigmoid(g_ref[...])
        y = jnp.where(pl.program_id(1) < half, ya_ref[...], yb_ref[...])
        dy_ref[...] = (dm * s).astype(BF16)
        dg_ref[...] = (dm * y * (s * (1.0 - s))).astype(BF16)

    blk = pl.BlockSpec((GATE_TR, GATE_TC), lambda i, j: (i, j))
    wrapped = pl.BlockSpec((GATE_TR, GATE_TC), lambda i, j: (i, j % half))
    return pl.pallas_call(
        body, name="gate_bwd", grid=(S // GATE_TR, GATE_NB),
        in_specs=[wrapped, blk, wrapped, wrapped],
        out_specs=[blk, pl.BlockSpec((GATE_TR, GATE_TC), lambda i, j: (i, j + SEG_G // GATE_TC))],
        out_shape=[jax.ShapeDtypeStruct((S, 2 * D), BF16), jax.ShapeDtypeStruct((S, D_PAD), BF16)],
        compiler_params=_cparams(("parallel", "parallel")),
    )(dmixed, proj_g, ya, yb)


FOX_TQ = 256


def _scan_rows(x, reverse):
    n = x.shape[0]
    row = lax.broadcasted_iota(jnp.int32, x.shape, 0)
    k = 1
    while k < n:
        if reverse:
            x = x + jnp.where(row < n - k, pltpu.roll(x, n - k, 0), 0.0)
        else:
            x = x + jnp.where(row >= k, pltpu.roll(x, k, 0), 0.0)
        k *= 2
    return x


def _fox_dscan(dft, dfs, proj_f, b_pad, dproj):
    def body(dft_ref, dfs_ref, f_ref, b_ref, _, dfa_ref, db_ref):
        dfs_pad = jnp.concatenate([dfs_ref[...], jnp.zeros((LANES - FOX_H, S), F32)], axis=0)
        df = dfs_pad.T + dft_ref[0]
        for hp in range(1, dft_ref.shape[0]):
            df = df + pltpu.roll(dft_ref[hp], 2 * hp, 1)
        dlf = _scan_rows(df, reverse=True)
        z = f_ref[...] + b_ref[...]
        lane = lax.broadcasted_iota(jnp.int32, (S, LANES), 1)
        dfa = jnp.where(lane < FOX_H, dlf / (1.0 + jnp.exp(z)), 0.0)
        dfa_ref[:, :LANES] = dfa.astype(BF16)
        dfa_ref[:, LANES:] = jnp.zeros((S, CB - LANES), BF16)
        db_ref[...] = jnp.sum(dfa, axis=0, keepdims=True)

    return pl.pallas_call(
        body, name="fox_dscan", grid=(1,),
        in_specs=[pl.BlockSpec(dft.shape, lambda i: (0, 0, 0)), pl.BlockSpec((FOX_H, S), lambda i: (0, 0)),
                  pl.BlockSpec((S, LANES), lambda i: (0, 0)), pl.BlockSpec((1, LANES), lambda i: (0, 0)), ANY],
        out_specs=[pl.BlockSpec((S, CB), lambda i: (0, SEG_F // CB)), pl.BlockSpec((1, LANES), lambda i: (0, 0))],
        out_shape=[jax.ShapeDtypeStruct((S, D_PAD), BF16), jax.ShapeDtypeStruct((1, LANES), F32)],
        input_output_aliases={4: 0},
        compiler_params=_cparams(("arbitrary",)),
    )(dft, dfs, proj_f, b_pad, dproj)


FOX_NQ = S // FOX_TQ
FOX_HP = FOX_W // LANES
HEADS_PER_LB = LANES // HD
FOX_AUG = 2 * LANES


def _fox_prepare(proj_a, proj_f, b_pad):
    tr = 512

    def body(q_ref, k_ref, f_ref, b_ref, qa_ref, ka_ref, hi_s, mid_s, lo_s):
        i = pl.program_id(0)

        @pl.when(i == 0)
        def _():
            z = f_ref[...] + b_ref[...]
            lf = jnp.minimum(z, 0.0) - jnp.log1p(jnp.exp(-jnp.abs(z)))
            f = _scan_rows(lf, reverse=False)
            hi = f.astype(BF16).astype(F32)
            r1 = f - hi
            mid = r1.astype(BF16).astype(F32)
            hi_s[...] = hi
            mid_s[...] = mid
            lo_s[...] = (r1 - mid).astype(BF16).astype(F32)

        rows = pl.ds(pl.multiple_of(i * tr, tr), tr)
        hi, mid, lo = hi_s[rows, :], mid_s[rows, :], lo_s[rows, :]
        lane = lax.broadcasted_iota(jnp.int32, (tr, HD), 1)
        q_ones = jnp.where((lane >= 3) & (lane < 6), 1.0, 0.0)
        k_ones = jnp.where(lane < 3, 1.0, 0.0)
        for h in range(FOX_H):
            a1, a2, a3 = hi[:, h:h + 1], mid[:, h:h + 1], lo[:, h:h + 1]
            q_extra = jnp.where(lane == 0, a1, jnp.where(lane == 1, a2, jnp.where(lane == 2, a3, q_ones)))
            k_extra = jnp.where(lane == 3, -a1, jnp.where(lane == 4, -a2, jnp.where(lane == 5, -a3, k_ones)))
            base = h * LANES
            qa_ref[:, base:base + HD] = (q_ref[:, h * HD:(h + 1) * HD].astype(F32) * SCALE).astype(BF16)
            qa_ref[:, base + HD:base + LANES] = q_extra.astype(BF16)
            ka_ref[:, base:base + HD] = k_ref[:, h * HD:(h + 1) * HD]
            ka_ref[:, base + HD:base + LANES] = k_extra.astype(BF16)

    out_blk = pl.BlockSpec((tr, FOX_H * LANES), lambda i: (i, 0))
    return pl.pallas_call(
        body, name="fox_prepare", grid=(S // tr,),
        in_specs=[pl.BlockSpec((tr, FOX_W), lambda i: (i, 0)), pl.BlockSpec((tr, FOX_W), lambda i: (i, 1)),
                  pl.BlockSpec((S, LANES), lambda i: (0, 0)), pl.BlockSpec((1, LANES), lambda i: (0, 0))],
        out_specs=[out_blk, out_blk],
        out_shape=[jax.ShapeDtypeStruct((S, FOX_H * LANES), BF16)] * 2,
        scratch_shapes=[pltpu.VMEM((S, LANES), F32)] * 3,
        compiler_params=_cparams(("arbitrary",)),
    )(proj_a, proj_a, proj_f, b_pad)


def _fox_scores(qa, ka, w):
    s = _dot(qa, ka, _NT)
    row = lax.broadcasted_iota(jnp.int32, (FOX_TQ, FOX_TQ), 0)
    col = lax.broadcasted_iota(jnp.int32, (FOX_TQ, FOX_TQ), 1)
    diag = jnp.where(col <= row, s[:, w * FOX_TQ:], NEG_INF)
    return diag if w == 0 else jnp.concatenate([s[:, :w * FOX_TQ], diag], axis=1)


def _fox_fwd(q_aug, k_aug, proj_a):
    def body(qa_ref, ka_ref, v_ref, o_ref, lse_ref):
        qi = pl.program_id(1)
        lane = lax.broadcasted_iota(jnp.int32, (FOX_TQ, LANES), 1)
        for w in range(FOX_NQ):
            @pl.when(qi == w)
            def _(w=w):
                width = (w + 1) * FOX_TQ
                lse = jnp.zeros((FOX_TQ, LANES), F32)
                for hh in range(HEADS_PER_LB):
                    aug = slice(hh * LANES, (hh + 1) * LANES)
                    sl = slice(hh * HD, (hh + 1) * HD)
                    s = _fox_scores(qa_ref[:, aug], ka_ref[:width, aug], w)
                    m = jnp.max(s, axis=-1, keepdims=True)
                    p = jnp.exp(s - m)
                    l = jnp.sum(p, axis=-1, keepdims=True)
                    o_ref[:, sl] = _dot(p / l, v_ref[:width, sl], _NN)
                    lse = jnp.where(lane == hh, m + jnp.log(l), lse)
                lse_ref[...] = lse

    return pl.pallas_call(
        body, name="fox_fwd", grid=(FOX_HP, FOX_NQ),
        in_specs=[pl.BlockSpec((FOX_TQ, FOX_AUG), lambda h, i: (i, h)),
                  pl.BlockSpec((S, FOX_AUG), lambda h, i: (0, h)),
                  pl.BlockSpec((S, LANES), lambda h, i: (0, 2 * FOX_HP + h))],
        out_specs=[pl.BlockSpec((FOX_TQ, LANES), lambda h, i: (i, h)),
                   pl.BlockSpec((None, FOX_TQ, LANES), lambda h, i: (h, i, 0))],
        out_shape=[jax.ShapeDtypeStruct((S, FOX_W), F32), jax.ShapeDtypeStruct((FOX_HP, S, LANES), F32)],
        compiler_params=_cparams(("parallel", "parallel")),
    )(q_aug, k_aug, proj_a)


def _fox_bwd(q_aug, k_aug, proj_a, lse, do, dproj, deps=()):
    n_q = FOX_NQ

    def body(qa_ref, ka_ref, v_ref, lse_ref, do_ref, *refs):
        dproj_ref, dft_ref, dfs_ref, dk_acc, dv_acc, dq_buf, kv_buf, dq_sems, kv_sems = refs[1 + len(deps):]
        hp = pl.program_id(0)
        t = pl.program_id(1)
        slot = t % 2
        col = pl.multiple_of(hp * LANES, LANES)

        def dq_copy(step, buf_slot):
            rows = pl.ds(pl.multiple_of(step * FOX_TQ, FOX_TQ), FOX_TQ)
            return pltpu.make_async_copy(dq_buf.at[buf_slot], dproj_ref.at[rows, pl.ds(col, LANES)], dq_sems.at[buf_slot])

        @pl.when(t == 0)
        def _():
            dk_acc[...] = jnp.zeros_like(dk_acc)
            dv_acc[...] = jnp.zeros_like(dv_acc)

        @pl.when((t == 0) & (hp == 0))
        def _():
            dfs_ref[...] = jnp.zeros_like(dfs_ref)

        @pl.when(t >= 2)
        def _():
            dq_copy(t - 2, slot).wait()

        lane = lax.broadcasted_iota(jnp.int32, (FOX_TQ, LANES), 1)
        for w in range(n_q):
            @pl.when(t == w)
            def _(w=w):
                width = (w + 1) * FOX_TQ
                lse_all = lse_ref[...]
                sub = lax.broadcasted_iota(jnp.int32, (FOX_H, width), 0)
                dft = jnp.zeros((FOX_TQ, LANES), F32)
                for hh in range(HEADS_PER_LB):
                    aug = slice(hh * LANES, (hh + 1) * LANES)
                    sl = slice(hh * HD, (hh + 1) * HD)
                    head = hp * HEADS_PER_LB + hh
                    qa = qa_ref[:, aug]
                    ka = ka_ref[:width, aug]
                    do_h = do_ref[:, sl]
                    s = _fox_scores(qa, ka, w)
                    p = jnp.exp(s - lse_all[:, hh:hh + 1])
                    dp = _dot(do_h, v_ref[:width, sl], _NT)
                    ds = p * (dp - jnp.sum(dp * p, axis=-1, keepdims=True))
                    dft = jnp.where(lane == hh, jnp.sum(ds, axis=-1, keepdims=True), dft)
                    dfs_ref[:, :width] -= jnp.where(sub == head, jnp.sum(ds, axis=0, keepdims=True), 0.0)
                    dq_buf[slot, :, sl] = (_dot(ds, ka[:, :HD], _NN) * SCALE).astype(BF16)
                    dk_acc[:width, sl] += _dot(ds, qa[:, :HD], _TN)
                    dv_acc[:width, sl] += _dot(p, do_h, _TN)
                dft_ref[...] = dft

        dq_copy(t, slot).start()

        @pl.when(t == n_q - 1)
        def _():
            dq_copy(t - 1, 1 - slot).wait()
            dq_copy(t, slot).wait()
            kv_buf[0] = dk_acc[...].astype(BF16)
            kv_buf[1] = dv_acc[...].astype(BF16)
            copies = [pltpu.make_async_copy(
                kv_buf.at[j], dproj_ref.at[:, pl.ds(pl.multiple_of((j + 1) * FOX_W + hp * LANES, LANES), LANES)],
                kv_sems.at[j]) for j in range(2)]
            for cp in copies:
                cp.start()
            for cp in copies:
                cp.wait()

    qblk = pl.BlockSpec((FOX_TQ, FOX_AUG), lambda h, t: (t, h))
    lane_blk = pl.BlockSpec((None, FOX_TQ, LANES), lambda h, t: (h, t, 0))
    return pl.pallas_call(
        body, name="fox_bwd", grid=(FOX_HP, n_q),
        in_specs=[qblk, pl.BlockSpec((S, FOX_AUG), lambda h, t: (0, h)),
                  pl.BlockSpec((S, LANES), lambda h, t: (0, 2 * FOX_HP + h)),
                  lane_blk, pl.BlockSpec((FOX_TQ, LANES), lambda h, t: (t, h)), ANY] + [ANY] * len(deps),
        out_specs=[ANY, lane_blk, pl.BlockSpec((FOX_H, S), lambda h, t: (0, 0))],
        out_shape=[jax.ShapeDtypeStruct((S, D_PAD), BF16),
                   jax.ShapeDtypeStruct((FOX_HP, S, LANES), F32), jax.ShapeDtypeStruct((FOX_H, S), F32)],
        scratch_shapes=[pltpu.VMEM((S, LANES), F32), pltpu.VMEM((S, LANES), F32),
                        pltpu.VMEM((2, FOX_TQ, LANES), BF16), pltpu.VMEM((2, S, LANES), BF16),
                        pltpu.SemaphoreType.DMA((2,)), pltpu.SemaphoreType.DMA((2,))],
        input_output_aliases={5: 0},
        compiler_params=_cparams(("arbitrary", "arbitrary")),
    )(q_aug, k_aug, proj_a, lse, do, dproj, *deps)


def _rope_tables():
    half = ROPE_DIM // 2
    inv_freq = jnp.power(jnp.float32(ROPE_THETA), -jnp.arange(half, dtype=F32) * 2.0 / ROPE_DIM)
    ang = jnp.arange(S).astype(F32)[:, None] * inv_freq[None, :]
    cos, sin = jnp.cos(ang), jnp.sin(ang)
    z = jnp.zeros((S, half), F32)
    rest = HD - ROPE_DIM
    t_self = jnp.concatenate([cos, cos, jnp.ones((S, rest), F32)], axis=1)
    t_up = jnp.concatenate([-sin, z, jnp.zeros((S, rest), F32)], axis=1)
    t_dn = jnp.concatenate([z, sin, jnp.zeros((S, rest), F32)], axis=1)
    rep = LANES // HD

    def by_residue(t):
        return jnp.stack([t.reshape(S // r, r, LANES).transpose(1, 0, 2).reshape(S, LANES) for r in DIL_R])

    return tuple(by_residue(jnp.tile(t, (1, rep))) for t in (t_self, t_up, t_dn))


def _residue_pieces(r):
    if r == 1:
        return [(slice(i, i + 512), slice(i, i + 512)) for i in range(0, S, 512)]
    n = S // r
    return [(pl.ds(j, n, stride=r), slice(j * n, (j + 1) * n)) for j in range(r)]


def _rope_apply(x, a, u, d, backward):
    half = ROPE_DIM // 2
    if backward:
        return x * a + pltpu.roll(x * u, half, 1) + pltpu.roll(x * d, LANES - half, 1)
    return x * a + pltpu.roll(x, LANES - half, 1) * u + pltpu.roll(x, half, 1) * d


LB_PER_CB = CB // LANES
ROPE_NB = 3 * DIL_G * LB_PER_CB


def _rope_step(s):
    per_group = 3 * LB_PER_CB
    return s // per_group, (s % per_group) // LB_PER_CB, s % LB_PER_CB


def _rope_split(proj_b, tables):
    def body(x_ref, a_ref, u_ref, d_ref, o_ref):
        g, t, _ = _rope_step(pl.program_id(0))
        for gs in range(DIL_G):
            @pl.when(g == gs)
            def _(gs=gs):
                for tok, sub in _residue_pieces(DIL_R[gs]):
                    x = x_ref[tok, :]
                    y = _rope_apply(x, a_ref[sub, :], u_ref[sub, :], d_ref[sub, :], False)
                    o_ref[sub, :] = jnp.where(t < 2, y, x).astype(BF16)

    def in_index(s):
        g, t, hf = _rope_step(s)
        return 0, (t * DIL_G + g) * LB_PER_CB + hf

    def out_index(s):
        g, t, hf = _rope_step(s)
        return t * DIL_G + g, 0, hf

    tab = pl.BlockSpec((None, S, LANES), lambda s: (_rope_step(s)[0], 0, 0))
    return pl.pallas_call(
        body, name="rope_split", grid=(ROPE_NB,),
        in_specs=[pl.BlockSpec((S, LANES), in_index), tab, tab, tab],
        out_specs=pl.BlockSpec((None, S, LANES), out_index),
        out_shape=jax.ShapeDtypeStruct((3 * DIL_G, S, CB), BF16),
        compiler_params=_cparams(("parallel",)),
    )(proj_b, *tables)


def _rope_merge_bwd(dq3, dk3, dv3, tables, dproj):
    def body(dq_ref, dk_ref, dv_ref, a_ref, u_ref, d_ref, _, o_ref, tmp):
        s = pl.program_id(0)
        g, t, _ = _rope_step(s)
        for gs in range(DIL_G):
            @pl.when((g == gs) & (s < ROPE_NB))
            def _(gs=gs):
                for tok, sub in _residue_pieces(DIL_R[gs]):
                    x = jnp.where(t == 0, dq_ref[sub, :], jnp.where(t == 1, dk_ref[sub, :], dv_ref[sub, :]))
                    y = _rope_apply(x, a_ref[sub, :], u_ref[sub, :], d_ref[sub, :], True)
                    tmp[tok, :] = jnp.where(t < 2, y, x)
                o_ref[...] = tmp[...].astype(BF16)

        @pl.when(s >= ROPE_NB)
        def _():
            o_ref[...] = jnp.zeros_like(o_ref)

    def src_spec(own):
        def index(s):
            g, t, hf = _rope_step(jnp.minimum(s, ROPE_NB - 1))
            return g, 0, jnp.where(t == own, hf, jnp.where(t > own, LB_PER_CB - 1, 0))
        return pl.BlockSpec((None, S, LANES), index)

    def out_index(s):
        g, t, hf = _rope_step(s)
        col = SEG_B // LANES + (t * DIL_G + g) * LB_PER_CB + hf
        return 0, jnp.where(s < ROPE_NB, col, SEG_Z // LANES + s - ROPE_NB)

    tab = pl.BlockSpec((None, S, LANES), lambda s: (_rope_step(jnp.minimum(s, ROPE_NB - 1))[0], 0, 0))
    return pl.pallas_call(
        body, name="rope_merge_bwd", grid=(ROPE_NB + LB_PER_CB,),
        in_specs=[src_spec(0), src_spec(1), src_spec(2), tab, tab, tab, ANY],
        out_specs=pl.BlockSpec((S, LANES), out_index),
        out_shape=jax.ShapeDtypeStruct((S, D_PAD), BF16),
        scratch_shapes=[pltpu.VMEM((S, LANES), F32)],
        input_output_aliases={6: 0},
        compiler_params=_cparams(("arbitrary",)),
    )(dq3, dk3, dv3, *tables, dproj)


DIL_NB = S // DIL_BLK


_BQK = (((2,), (2,)), ((0,), (0,)))
_BQD = (((2,), (1,)), ((0,), (0,)))
_BKD = (((1,), (1,)), ((0,), (0,)))


def _dil_mask(g):
    shape = (DIL_NB, DIL_BLK, 2 * DIL_BLK)
    blocks_per_seq = jnp.right_shift(DIL_NB, 2 * g)
    has_prev = jnp.bitwise_and(lax.broadcasted_iota(jnp.int32, shape, 0), blocks_per_seq - 1) != 0
    a = lax.broadcasted_iota(jnp.int32, shape, 1)
    kk = lax.broadcasted_iota(jnp.int32, shape, 2)
    diff = DIL_BLK + a - kk
    return (diff >= 0) & (diff <= DIL_BLK) & ((kk >= DIL_BLK) | has_prev)


def _blocks(t):
    return t.reshape(DIL_NB, DIL_BLK, t.shape[-1])


def _with_prev(t):
    prev = jnp.concatenate([jnp.zeros((DIL_BLK, t.shape[-1]), t.dtype), t[:-DIL_BLK]], axis=0)
    return jnp.concatenate([_blocks(prev), _blocks(t)], axis=1)


def _fold_prev(t2):
    n = t2.shape[-1]
    to_prev = t2[:, :DIL_BLK].reshape(S, n)
    own = t2[:, DIL_BLK:].reshape(S, n)
    return own + jnp.concatenate([to_prev[DIL_BLK:], jnp.zeros((DIL_BLK, n), t2.dtype)], axis=0)


def _dil_group_spec(t, width=DIL_OUT):
    return pl.BlockSpec((None, S, width), lambda g: (t * DIL_G + g, 0, 0))


def _dil_fwd(qkv3):
    def body(q_ref, k_ref, v_ref, o_ref, lse_ref):
        ok = _dil_mask(pl.program_id(0))
        lane = lax.broadcasted_iota(jnp.int32, (S, LANES), 1)
        lse = jnp.zeros((S, LANES), F32)
        for h in range(DIL_HG):
            sl = slice(h * HD, (h + 1) * HD)
            s = jnp.where(ok, _dot(_blocks(q_ref[:, sl]), _with_prev(k_ref[:, sl]), _BQK) * SCALE, NEG_INF)
            m = jnp.max(s, axis=-1, keepdims=True)
            p = jnp.exp(s - m)
            l = jnp.sum(p, axis=-1, keepdims=True)
            o_ref[:, sl] = _dot(p / l, _with_prev(v_ref[:, sl]), _BQD).reshape(S, HD)
            lse = jnp.where(lane == h, (m + jnp.log(l)).reshape(S, 1), lse)
        lse_ref[...] = lse

    return pl.pallas_call(
        body, name="dil_fwd", grid=(DIL_G,),
        in_specs=[_dil_group_spec(0), _dil_group_spec(1), _dil_group_spec(2)],
        out_specs=[_dil_group_spec(0), _dil_group_spec(0, LANES)],
        out_shape=[jax.ShapeDtypeStruct((DIL_G, S, DIL_OUT), F32), jax.ShapeDtypeStruct((DIL_G, S, LANES), F32)],
        compiler_params=_cparams(("parallel",)),
    )(qkv3, qkv3, qkv3)


def _dil_bwd(qkv3, lse3, do3, c3):
    def body(q_ref, k_ref, v_ref, lse_ref, do_ref, c_ref, dq_ref, dk_ref, dv_ref):
        ok = _dil_mask(pl.program_id(0))
        lse = lse_ref[...]
        c = c_ref[...]
        for h in range(DIL_HG):
            sl = slice(h * HD, (h + 1) * HD)
            q = _blocks(q_ref[:, sl])
            k2 = _with_prev(k_ref[:, sl])
            do_h = _blocks(do_ref[:, sl])
            s = jnp.where(ok, _dot(q, k2, _BQK) * SCALE, NEG_INF)
            p = jnp.exp(s - _blocks(lse[:, h:h + 1]))
            dp = _dot(do_h, _with_prev(v_ref[:, sl]), _BQK)
            ds = p * (dp - _blocks(c[:, h:h + 1]))
            dsb = (ds * SCALE).astype(BF16)
            dq_ref[:, sl] = _dot(dsb, k2, _BQD).reshape(S, HD)
            dk_ref[:, sl] = _fold_prev(_dot(dsb, q, _BKD))
            dv_ref[:, sl] = _fold_prev(_dot(p, do_h, _BKD))

    return pl.pallas_call(
        body, name="dil_bwd", grid=(DIL_G,),
        in_specs=[_dil_group_spec(0), _dil_group_spec(1), _dil_group_spec(2), _dil_group_spec(0, LANES),
                  _dil_group_spec(0), _dil_group_spec(0, LANES)],
        out_specs=[_dil_group_spec(0)] * 3,
        out_shape=[jax.ShapeDtypeStruct((DIL_G, S, DIL_OUT), F32)] * 3,
        compiler_params=_cparams(("parallel",)),
    )(qkv3, qkv3, qkv3, lse3, do3, c3)


COMB_T = 256


def _dil_combine(o3, lse3):
    heads_per_lb = LANES // HD

    def body(o_ref, lse_ref, ob_ref, al_ref, *scratch):
        o_tok = [scratch[LB_PER_CB * gg:LB_PER_CB * (gg + 1)] for gg in range(DIL_G)]
        lse_tok = scratch[LB_PER_CB * DIL_G:]
        g = pl.program_id(0)
        for gs in range(DIL_G):
            @pl.when(g == gs)
            def _(gs=gs):
                for tok, sub in _residue_pieces(DIL_R[gs]):
                    for hf in range(LB_PER_CB):
                        o_tok[gs][hf][tok, :] = o_ref[sub, hf * LANES:(hf + 1) * LANES]
                    lse_tok[gs][tok, :] = lse_ref[sub, :]

        @pl.when(g == DIL_G - 1)
        def _():
            def chunk(i, carry):
                rows = pl.ds(pl.multiple_of(i * COMB_T, COMB_T), COMB_T)
                lse = [lse_tok[gg][rows, :] for gg in range(DIL_G)]
                m = jnp.maximum(jnp.maximum(lse[0], lse[1]), lse[2])
                e = [jnp.exp(lse[gg] - m) for gg in range(DIL_G)]
                den = (e[0] + e[1]) + e[2]
                al = [e[gg] / den for gg in range(DIL_G)]
                for gg in range(DIL_G):
                    al_ref[gg, rows, :] = al[gg]
                for h in range(DIL_HG):
                    hf, sl = h // heads_per_lb, slice((h % heads_per_lb) * HD, (h % heads_per_lb + 1) * HD)
                    acc = al[0][:, h:h + 1] * o_tok[0][hf][rows, sl]
                    for gg in range(1, DIL_G):
                        acc = acc + al[gg][:, h:h + 1] * o_tok[gg][hf][rows, sl]
                    ob_ref[rows, h * HD:(h + 1) * HD] = acc
                return carry

            lax.fori_loop(0, S // COMB_T, chunk, 0)

    return pl.pallas_call(
        body, name="dil_combine", grid=(DIL_G,),
        in_specs=[pl.BlockSpec((None, S, DIL_OUT), lambda g: (g, 0, 0)),
                  pl.BlockSpec((None, S, LANES), lambda g: (g, 0, 0))],
        out_specs=[pl.BlockSpec((S, DIL_OUT), lambda g: (0, 0)),
                   pl.BlockSpec((DIL_G, S, LANES), lambda g: (0, 0, 0))],
        out_shape=[jax.ShapeDtypeStruct((S, DIL_OUT), F32), jax.ShapeDtypeStruct((DIL_G, S, LANES), F32)],
        scratch_shapes=[pltpu.VMEM((S, LANES), F32)] * (DIL_G * (LB_PER_CB + 1)),
        compiler_params=_cparams(("arbitrary",)),
    )(o3, lse3)


def _dil_combine_bwd(dob, ob, alpha, deps=()):
    heads_per_lb = LANES // HD

    def body(dob_ref, ob_ref, al_ref, *refs):
        do_ref, c_ref = refs[len(deps):]
        g = pl.program_id(0)
        hf = pl.program_id(1)
        for gs in range(DIL_G):
            @pl.when(g == gs)
            def _(gs=gs):
                for tok, sub in _residue_pieces(DIL_R[gs]):
                    dob = dob_ref[tok, :]
                    prod = dob * ob_ref[tok, :]
                    al = al_ref[tok, :]
                    lane = lax.broadcasted_iota(jnp.int32, al.shape, 1)
                    c = jnp.where(hf == 0, 0.0, c_ref[sub, :])
                    for hh in range(heads_per_lb):
                        sl = slice(hh * HD, (hh + 1) * HD)
                        head = hf * heads_per_lb + hh
                        a = jnp.sum(jnp.where(lane == head, al, 0.0), axis=-1, keepdims=True)
                        do_ref[sub, sl] = (a * dob[:, sl]).astype(BF16)
                        c = jnp.where(lane == head, a * jnp.sum(prod[:, sl], axis=-1, keepdims=True), c)
                    c_ref[sub, :] = c

    half = pl.BlockSpec((S, LANES), lambda g, hf: (0, hf))
    return pl.pallas_call(
        body, name="dil_combine_bwd", grid=(DIL_G, LB_PER_CB),
        in_specs=[half, half, pl.BlockSpec((None, S, LANES), lambda g, hf: (g, 0, 0))] + [ANY] * len(deps),
        out_specs=[pl.BlockSpec((None, S, LANES), lambda g, hf: (g, 0, hf)),
                   pl.BlockSpec((None, S, LANES), lambda g, hf: (g, 0, 0))],
        out_shape=[jax.ShapeDtypeStruct((DIL_G, S, DIL_OUT), BF16), jax.ShapeDtypeStruct((DIL_G, S, LANES), F32)],
        compiler_params=_cparams(("parallel", "arbitrary")),
    )(dob, ob, alpha, *deps)


def _local_step(x, tgt, g_attn, g_mlp, g_final, b_pad, w_in_t, hooks):
    tables = _rope_tables()

    h1 = _rms_fwd(x, g_attn, "rms_attn_fwd", deps=hooks.first_deps())
    proj_a = _mm(h1, w_in_t, mode="nt", tm=S, tn=512, n=3 * FOX_W, out_dtypes=[BF16], name="proj_a")
    proj_b = _mm(h1, w_in_t, mode="nt", tm=S, tn=CB, n=3 * DIL_W, b_off=SEG_B // CB, out_dtypes=[F32], name="proj_b")
    proj_g = _mm(h1, w_in_t, mode="nt", tm=S, tn=CB, n=2 * D, b_off=SEG_G // CB, out_dtypes=[F32], name="proj_g")
    proj_f = _mm(h1, w_in_t, mode="nt", tm=S, tn=LANES, n=LANES, b_off=SEG_F // LANES, out_dtypes=[F32], name="proj_f")

    q_aug, k_aug = _fox_prepare(proj_a, proj_f, b_pad)
    oa, lse_a = _fox_fwd(q_aug, k_aug, proj_a)

    qkv3 = _rope_split(proj_b, tables)
    o3, lse3 = _dil_fwd(qkv3)
    ob, alpha = _dil_combine(o3, lse3)

    w_a, w_b, w_out = hooks.mixer_weights(ob)
    ya = _mm(oa, w_a, mode="nn", tm=S, tn=CB, out_dtypes=[F32], name="branch_a")
    yb = _mm(ob, w_b, mode="nn", tm=S, tn=CB, out_dtypes=[F32], name="branch_b")
    mixed = _gate_fwd(proj_g, ya, yb)
    x2 = _mm(mixed, w_out, mode="nn", tm=S, tn=CB, out_dtypes=[F32], name="out_proj",
             epilogue=lambda acc, res: (res + acc,), extras=(x,))

    h2 = _rms_fwd(x2, g_mlp, "rms_mlp_fwd")
    w_up_sh, w_down = hooks.mlp_weights(h2)

    def up_epilogue(acc):
        r = jnp.maximum(acc, 0.0)
        return acc, r * r

    u, act = _mm(h2, w_up_sh, mode="nn", tm=S, tn=DFF // N_DEV, b_sharded=True, out_dtypes=[F32, BF16],
                 name="mlp_up", epilogue=up_epilogue)
    x3 = _mm(act, w_down, mode="nn", tm=512, tn=512, out_dtypes=[F32], name="mlp_down",
             epilogue=lambda acc, res: (res + acc,), extras=(x2,))

    dx3, dg_final, loss = _final_norm_loss(x3, g_final, tgt)

    du = _mm(dx3, w_down, mode="nt", tm=S, tn=CB, out_dtypes=[BF16], name="mlp_down_bwd",
             epilogue=lambda acc, u_t: (acc * (2.0 * jnp.maximum(u_t, 0.0)),), extras=(u,))
    dw_down = _mm(act, dx3, mode="tn", tm=512, tn=512, out_dtypes=[F32], name="dw_down")
    dw_up_sh = _mm(h2, du, mode="tn", tm=D, tn=DFF // N_DEV, out_sharded=True, out_dtypes=[F32], name="dw_up")
    dh2 = _mm(du, w_up_sh, mode="nt", tm=512, tn=512, b_sharded=True, out_dtypes=[F32], name="mlp_up_bwd",
              deps=hooks.mlp_grads(dw_up_sh, dw_down))
    dx2, dg_mlp = _rms_bwd(x2, g_mlp, dh2, dx3, "rms_mlp_bwd")

    dmixed = _mm(dx2, w_out, mode="nt", tm=S, tn=CB, out_dtypes=[F32], name="out_proj_bwd")
    dw_out = _mm(mixed, dx2, mode="tn", tm=D, tn=CB, out_dtypes=[F32], name="dw_out")

    dyab, dproj = _gate_bwd(dmixed, proj_g, ya, yb)
    doa = _mm(dyab, w_a, mode="nt", tm=S, tn=CB, k=D, a_off=0, out_dtypes=[BF16], name="branch_a_bwd")
    dw_a = _mm(oa, dyab, mode="tn", tm=FOX_W, tn=CB, n=D, b_off=0, out_dtypes=[F32], name="dw_branch_a")
    dob = _mm(dyab, w_b, mode="nt", tm=S, tn=CB, k=D, a_off=1, out_dtypes=[F32], name="branch_b_bwd")
    dw_b = _mm(ob, dyab, mode="tn", tm=DIL_OUT, tn=CB, n=D, b_off=D // CB, out_dtypes=[F32], name="dw_branch_b")

    dproj, dft, dfs = _fox_bwd(q_aug, k_aug, proj_a, lse_a, doa, dproj, deps=hooks.mixer_grads(dw_a, dw_b, dw_out))
    dproj, db = _fox_dscan(dft, dfs, proj_f, b_pad, dproj)

    do3, c3 = _dil_combine_bwd(dob, ob, alpha, deps=hooks.mid_backward(db))
    dq3, dk3, dv3 = _dil_bwd(qkv3, lse3, do3, c3)
    dproj = _rope_merge_bwd(dq3, dk3, dv3, tables, dproj)

    dw_in_t = _mm(dproj, h1, mode="tn", tm=CB, tn=D, out_dtypes=[F32], name="dw_in")
    dh1 = _mm(dproj, w_in_t, mode="nn", tm=512, tn=CB, out_dtypes=[F32], name="proj_bwd",
              deps=hooks.w_in_grad(dw_in_t))
    dx, dg_attn = _rms_bwd(x, g_attn, dh1, dx2, "rms_attn_bwd")

    return loss, dx, (dg_attn, db, dg_mlp, dg_final)


SHARD_SHAPES = ((D_IN // N_DEV, D), (FOX_W, D // N_DEV), (DIL_OUT, D // N_DEV), (D // N_DEV, D),
                (D, DFF // N_DEV), (DFF // N_DEV, D))
W_NAMES = ("w_in", "w_a", "w_b", "w_out", "w_up", "w_down")
AG_COPIES = 7
HBM = pl.BlockSpec(memory_space=pltpu.HBM)
SEM = pl.BlockSpec(memory_space=pltpu.SEMAPHORE)


def _place():
    return lax.axis_index("x"), lax.axis_index("y"), lax.axis_index("c")


W_SLAB = 752
W_WIN = 272
PAD_BLK = 256


def _pad_runs():
    runs = sorted((pad, pad + b - a, p, a) for p in range(N_DEV) for a, b, pad in _shard_pieces(p))
    blocks = []
    for k in range(D_PAD // PAD_BLK):
        lo, hi = k * PAD_BLK, (k + 1) * PAD_BLK
        blocks.append([(max(lo, r0) - lo, min(hi, r1) - lo, p, a + max(lo, r0) - r0)
                       for r0, r1, p, a in runs if max(lo, r0) < min(hi, r1)])
    return blocks


def _gather_w_in(shard):
    blocks = _pad_runs()

    def body(x_ref, out_ref, stage, land, send_sems, recv_sems, load_sem):
        x, y, c = _place()
        me, sibling = (x, y, c), (x, y, 1 - c)
        chips = [(1 - x, y), (x, 1 - y), (1 - x, 1 - y)]

        def slot(px, py, pc):
            return land.at[4 * px + 2 * py + pc]

        def copy(k, block, to):
            return pltpu.make_async_remote_copy(
                src_ref=slot(*block), dst_ref=slot(*block), send_sem=send_sems.at[k], recv_sem=recv_sems.at[k],
                device_id=to, device_id_type=MESH)

        load = pltpu.make_async_copy(x_ref, stage, load_sem)
        load.start()
        load.wait()
        land[4 * x + 2 * y + c] = stage[...].astype(BF16)
        sent = [copy(0, me, sibling)] + [copy(1 + j, me, (*chip, c)) for j, chip in enumerate(chips)]
        for cp in sent:
            cp.start()
        for j, chip in enumerate(chips):
            copy(1 + j, (*chip, c), me).wait_recv()
            sent.append(copy(4 + j, (*chip, c), sibling))
            sent[-1].start()
        copy(0, sibling, me).wait_recv()
        for j, chip in enumerate(chips):
            copy(4 + j, (*chip, 1 - c), me).wait_recv()
        for cp in sent:
            cp.wait_send()

        row = lax.broadcasted_iota(jnp.int32, (PAD_BLK, D), 0)
        for k, runs in enumerate(blocks):
            out = jnp.zeros((PAD_BLK, D), F32)
            for o_lo, o_hi, p, a in runs:
                start = min(a // 16 * 16, W_SLAB - W_WIN)
                win = land[p, start:start + W_WIN, :].astype(F32)
                moved = pltpu.roll(win, (o_lo - (a - start)) % W_WIN, 0)[:PAD_BLK]
                out = jnp.where((row >= o_lo) & (row < o_hi), moved, out)
            out_ref[k * PAD_BLK:(k + 1) * PAD_BLK, :] = out.astype(BF16)

    return pl.pallas_call(
        body, name="all_gather_w_in",
        out_shape=jax.ShapeDtypeStruct((D_PAD, D), BF16),
        in_specs=[ANY], out_specs=pl.BlockSpec(memory_space=pltpu.VMEM),
        scratch_shapes=[pltpu.VMEM((W_SLAB, D), F32), pltpu.VMEM((N_DEV, W_SLAB, D), BF16),
                        pltpu.SemaphoreType.DMA((AG_COPIES,)), pltpu.SemaphoreType.DMA((AG_COPIES,)),
                        pltpu.SemaphoreType.DMA(())],
        compiler_params=_cparams(None),
    )(shard)


def _place_own(shards):
    n_w = len(shards)
    shapes = [t.shape for t in shards]

    def body(*refs):
        in_refs, out_refs = refs[:n_w], refs[n_w:2 * n_w]
        stage, cast = refs[2 * n_w:3 * n_w], refs[3 * n_w:4 * n_w]
        load_sems, store_sems = refs[4 * n_w:]
        x, y, c = _place()
        loads = [pltpu.make_async_copy(in_refs[i], stage[i], load_sems.at[i]) for i in range(n_w)]
        for ld in loads:
            ld.start()
        stores = []
        for i in range(n_w):
            loads[i].wait()
            cast[i][...] = stage[i][...].astype(BF16)
            stores.append(pltpu.make_async_copy(cast[i], out_refs[i].at[4 * x + 2 * y + c], store_sems.at[i]))
            stores[-1].start()
        for st in stores:
            st.wait()

    return pl.pallas_call(
        body, name="place_own_shards",
        out_shape=[jax.ShapeDtypeStruct((N_DEV,) + sh, BF16) for sh in shapes],
        in_specs=[ANY] * n_w, out_specs=[ANY] * n_w,
        scratch_shapes=[pltpu.VMEM(sh, F32) for sh in shapes] + [pltpu.VMEM(sh, BF16) for sh in shapes]
                       + [pltpu.SemaphoreType.DMA((n_w,)), pltpu.SemaphoreType.DMA((n_w,))],
        compiler_params=_cparams(None),
    )(*shards)


PEER_COPIES = N_DEV - 1
SIBLING_COPIES = 4
CHIP_COPIES = 3


def _peer_copies(_, land_refs, send_sems, recv_sems):
    x, y, c = _place()
    mine = 4 * x + 2 * y + c
    copies = []
    for i, ref in enumerate(land_refs):
        for k in range(1, N_DEV):
            peer = (x ^ (k >> 2), y ^ ((k >> 1) & 1), c ^ (k & 1))
            n = i * PEER_COPIES + k - 1
            copies.append(pltpu.make_async_remote_copy(
                src_ref=ref.at[mine], dst_ref=ref.at[mine], send_sem=send_sems.at[n], recv_sem=recv_sems.at[n],
                device_id=peer, device_id_type=MESH))
    return copies


def _sibling_copies(g_refs, r_refs, send_sems, recv_sems):
    x, y, c = _place()
    return [pltpu.make_async_remote_copy(
        src_ref=g_refs[i].at[2 * k + (1 - c)], dst_ref=r_refs[i].at[k],
        send_sem=send_sems.at[SIBLING_COPIES * i + k], recv_sem=recv_sems.at[SIBLING_COPIES * i + k],
        device_id=(x, y, 1 - c), device_id_type=MESH) for i in range(len(g_refs)) for k in range(SIBLING_COPIES)]


def _chip_copies(p_refs, r_refs, send_sems, recv_sems):
    x, y, c = _place()
    chips = [(1 - x, y), (x, 1 - y), (1 - x, 1 - y)]
    return [pltpu.make_async_remote_copy(
        src_ref=p_refs[i].at[2 * cx + cy], dst_ref=r_refs[i].at[j],
        send_sem=send_sems.at[CHIP_COPIES * i + j], recv_sem=recv_sems.at[CHIP_COPIES * i + j],
        device_id=(cx, cy, c), device_id_type=MESH) for i in range(len(p_refs)) for j, (cx, cy) in enumerate(chips)]


def _in_hbm(a):
    return pltpu.with_memory_space_constraint(a, pltpu.HBM)


def _exchange_start(name, copies_fn, n_copies, srcs, lands, after=()):
    n_s, n_l, n_a = len(srcs), len(lands), len(after)

    def body(*refs):
        outs = refs[n_s + n_l + n_a:]
        for cp in copies_fn(refs[:n_s], refs[n_s:n_s + n_l], outs[0], outs[1]):
            cp.start()
        outs[-1][...] = jnp.zeros_like(outs[-1])

    res = pl.pallas_call(
        body, name=name,
        out_shape=[pltpu.SemaphoreType.DMA((n_copies,)), pltpu.SemaphoreType.DMA((n_copies,))]
                  + [pltpu.HBM(t.shape, t.dtype) for t in (*srcs, *lands)] + [jax.ShapeDtypeStruct((8, LANES), F32)],
        in_specs=[HBM] * (n_s + n_l) + [ANY] * n_a,
        out_specs=[SEM, SEM] + [HBM] * (n_s + n_l) + [pl.BlockSpec(memory_space=pltpu.VMEM)],
        input_output_aliases={i: 2 + i for i in range(n_s + n_l)},
        compiler_params=pltpu.CompilerParams(has_side_effects=pltpu.SideEffectType.DATAFLOW_SIDE_EFFECTING),
    )(*[_in_hbm(t) for t in (*srcs, *lands)], *after)
    return res[0], res[1], list(res[2:2 + n_s]), list(res[2 + n_s:2 + n_s + n_l]), res[-1]


def _exchange_wait(name, copies_fn, started, after):
    send_sems, recv_sems, srcs, lands, _ = started
    n_s, n_l = len(srcs), len(lands)

    def body(*refs):
        for cp in copies_fn(refs[:n_s], refs[n_s:n_s + n_l], refs[n_s + n_l], refs[n_s + n_l + 1]):
            cp.wait_send()
            cp.wait_recv()

    res = pl.pallas_call(
        body, name=name,
        out_shape=[pltpu.HBM(t.shape, t.dtype) for t in (*srcs, *lands)],
        in_specs=[HBM] * (n_s + n_l) + [SEM, SEM, ANY],
        out_specs=[HBM] * (n_s + n_l),
        input_output_aliases={i: i for i in range(n_s + n_l)},
        compiler_params=pltpu.CompilerParams(has_side_effects=pltpu.SideEffectType.DATAFLOW_SIDE_EFFECTING),
    )(*srcs, *lands, send_sems, recv_sems, after)
    return list(res[:n_s]), list(res[n_s:])


def _shard_block(shape):
    rows, cols = shape
    if rows % 256:
        return rows, 256
    return min(rows, 512 if cols <= LANES else 256), cols


def _chip_partial(ids, g_stack, recv1, name):
    shape = g_stack.shape[1:]
    br, bc = _shard_block(shape)

    def body(ids_ref, g_ref, r_ref, pb_ref, own_ref):
        s = g_ref[...] + r_ref[...]
        pb_ref[...] = s.astype(BF16)

        @pl.when(pl.program_id(2) == ids_ref[1])
        def _():
            own_ref[...] = s

    grid_spec = pltpu.PrefetchScalarGridSpec(
        num_scalar_prefetch=1, grid=(shape[0] // br, shape[1] // bc, 4),
        in_specs=[pl.BlockSpec((None, br, bc), lambda r, q, k, ids: (2 * k + ids[0], r, q)),
                  pl.BlockSpec((None, br, bc), lambda r, q, k, ids: (k, r, q))],
        out_specs=[pl.BlockSpec((None, br, bc), lambda r, q, k, ids: (k, r, q)),
                   pl.BlockSpec((br, bc), lambda r, q, k, ids: (r, q))])
    return pl.pallas_call(
        body, name=name, grid_spec=grid_spec,
        out_shape=[jax.ShapeDtypeStruct((4,) + shape, BF16), jax.ShapeDtypeStruct(shape, F32)],
        compiler_params=_cparams(("parallel", "parallel", "arbitrary")),
    )(ids, g_stack, recv1)


def _adamw(w, g, m, v):
    m = ADAM_B1 * m + (1.0 - ADAM_B1) * g
    v = ADAM_B2 * v + (1.0 - ADAM_B2) * (g * g)
    m_hat = m / (1.0 - ADAM_B1 ** ADAM_STEP)
    v_hat = v / (1.0 - ADAM_B2 ** ADAM_STEP)
    delta = -ADAM_LR * (m_hat / (jnp.sqrt(v_hat) + ADAM_EPS) + ADAM_WD * w)
    return delta, m, v


def _reduce_adamw(own, recv2, w, m, v, name, deps=()):
    shape = own.shape
    br, bc = _shard_block(shape)

    def body(own_ref, r_ref, w_ref, m_ref, v_ref, *refs):
        g_ref, d_ref, nm_ref, nv_ref = refs[len(deps):]
        g = own_ref[...]
        for j in range(3):
            g = g + r_ref[j].astype(F32)
        delta, nm, nv = _adamw(w_ref[...], g, m_ref[...], v_ref[...])
        g_ref[...] = g
        d_ref[...] = delta
        nm_ref[...] = nm
        nv_ref[...] = nv

    blk = pl.BlockSpec((br, bc), lambda r, q: (r, q))
    return pl.pallas_call(
        body, name=name, grid=(shape[0] // br, shape[1] // bc),
        in_specs=[blk, pl.BlockSpec((3, br, bc), lambda r, q: (0, r, q)), blk, blk, blk] + [ANY] * len(deps),
        out_specs=[blk] * 4, out_shape=[jax.ShapeDtypeStruct(shape, F32)] * 4,
        compiler_params=_cparams(("parallel", "parallel")),
    )(own, recv2, w, m, v, *deps)


def _small_allreduce_adamw(gvec, w, m, v, deps=()):
    def body(g_ref, w_ref, m_ref, v_ref, *refs):
        go_ref, d_ref, nm_ref, nv_ref, buf, send_sems, recv_sems = refs[len(deps):]
        x, y, c = _place()
        my_slot = 4 * x + 2 * y + c
        buf[my_slot] = g_ref[...]
        copies = []
        for k in range(1, N_DEV):
            px, py, pc = x ^ (k >> 2), y ^ ((k >> 1) & 1), c ^ (k & 1)
            copies.append(pltpu.make_async_remote_copy(
                src_ref=g_ref, dst_ref=buf.at[my_slot], send_sem=send_sems.at[k - 1], recv_sem=recv_sems.at[k - 1],
                device_id=(px, py, pc), device_id_type=MESH))
        for cp in copies:
            cp.start()
        for k in range(1, N_DEV):
            px, py, pc = x ^ (k >> 2), y ^ ((k >> 1) & 1), c ^ (k & 1)
            pltpu.make_async_remote_copy(
                src_ref=g_ref, dst_ref=buf.at[4 * px + 2 * py + pc], send_sem=send_sems.at[k - 1],
                recv_sem=recv_sems.at[k - 1], device_id=(px, py, pc), device_id_type=MESH).wait_recv()
        for cp in copies:
            cp.wait_send()
        g = buf[0]
        for s in range(1, N_DEV):
            g = g + buf[s]
        delta, nm, nv = _adamw(w_ref[...], g, m_ref[...], v_ref[...])
        go_ref[...] = g
        d_ref[...] = delta
        nm_ref[...] = nm
        nv_ref[...] = nv

    vm = pl.BlockSpec(memory_space=pltpu.VMEM)
    return pl.pallas_call(
        body, name="small_allreduce_adamw",
        in_specs=[vm] * 4 + [ANY] * len(deps), out_specs=[vm] * 4,
        out_shape=[jax.ShapeDtypeStruct((SMALL_R, LANES), F32)] * 4,
        scratch_shapes=[pltpu.VMEM((N_DEV, SMALL_R, LANES), F32),
                        pltpu.SemaphoreType.DMA((N_DEV - 1,)), pltpu.SemaphoreType.DMA((N_DEV - 1,))],
    )(gvec, w, m, v, *deps)


def _cols_to_whole(stack):
    rows = stack.shape[1]
    return stack.transpose(1, 0, 2).reshape(rows, -1)


def _whole_to_cols(t):
    rows = t.shape[0]
    return t.reshape(rows, N_DEV, -1).transpose(1, 0, 2)


W_IN_SEGMENTS = ((0, 3 * FOX_W, SEG_A), (3 * FOX_W, 3 * FOX_W + FOX_H, SEG_F),
                 (3 * FOX_W + FOX_H, 3 * FOX_W + FOX_H + 3 * DIL_W, SEG_B), (3 * FOX_W + FOX_H + 3 * DIL_W, D_IN, SEG_G))


def _shard_pieces(p):
    rows = D_IN // N_DEV
    lo, hi = p * rows, (p + 1) * rows
    return [(max(lo, a) - lo, min(hi, b) - lo, pad + max(lo, a) - a)
            for a, b, pad in W_IN_SEGMENTS if max(lo, a) < min(hi, b)]


def _unpad_dw_in_t(dwp):
    return jnp.stack([jnp.concatenate([dwp[pad:pad + b - a] for a, b, pad in _shard_pieces(p)], axis=0)
                      for p in range(N_DEV)])


LOSS_ROW = 25


def _pack_small(g_attn, b, g_mlp, g_final, loss_row=None):
    tail = jnp.pad(b, ((0, 7), (0, LANES - b.shape[1])))
    if loss_row is not None:
        tail = tail + jnp.pad(loss_row, ((LOSS_ROW - 24, 31 - LOSS_ROW), (0, 0)))
    return jnp.concatenate([g_attn.reshape(8, LANES), g_mlp.reshape(8, LANES), g_final.reshape(8, LANES), tail], axis=0)


def _unpack_small(p):
    return (p[0:8].reshape(1, D), p[24:25, :FOX_H], p[8:16].reshape(1, D), p[16:24].reshape(D))


class _StepComm:
    def __init__(self, ids, w_sh, m_sh, v_sh, after_w_in):
        self.ids = ids
        self.w_sh, self.m_sh, self.v_sh = w_sh, m_sh, v_sh
        self.updates = None
        lands = _place_own(w_sh[1:])
        self.gather_mixer = _exchange_start("gather_mixer_start", _peer_copies, PEER_COPIES * 3, [], lands[:3],
                                            after=(after_w_in,))
        self.gather_mlp = _exchange_start("gather_mlp_start", _peer_copies, PEER_COPIES * 2, [], lands[3:],
                                          after=(self.gather_mixer[-1],))
        self.sibling = {}
        self.chips = {}
        self.own = {}
        self.names = {}

    def _reduce_start(self, group, names, grads, after=()):
        lands = [lax.empty((SIBLING_COPIES,) + t.shape[1:], F32) for t in grads]
        self.sibling[group] = _exchange_start("grad_%s_sibling_start" % group, _sibling_copies,
                                              SIBLING_COPIES * len(grads), grads, lands, after=after)
        self.names[group] = names
        return self.sibling[group][-1]

    def _reduce_mid(self, group, after):
        grads, recv1 = _exchange_wait("grad_%s_sibling_wait" % group, _sibling_copies, self.sibling[group], after)
        parts = [_chip_partial(self.ids, g, r, "grad_partial_" + n) for g, r, n in zip(grads, recv1, self.names[group])]
        self.own[group] = [p[1] for p in parts]
        srcs = [p[0] for p in parts]
        lands = [lax.empty((CHIP_COPIES,) + t.shape[1:], BF16) for t in srcs]
        self.chips[group] = _exchange_start("grad_%s_chips_start" % group, _chip_copies, CHIP_COPIES * len(srcs),
                                            srcs, lands)
        return self.chips[group][-1]

    def reduced(self, group, after):
        _, recv2 = _exchange_wait("grad_%s_chips_wait" % group, _chip_copies, self.chips[group], after)
        return list(zip(self.own[group], recv2))

    def first_deps(self):
        return [self.gather_mlp[-1]]

    def mixer_weights(self, after):
        _, (g_a, g_b, g_out) = _exchange_wait("gather_mixer_wait", _peer_copies, self.gather_mixer, after)
        return _cols_to_whole(g_a), _cols_to_whole(g_b), g_out.reshape(D, D)

    def mlp_weights(self, after):
        _, (g_up, g_down) = _exchange_wait("gather_mlp_wait", _peer_copies, self.gather_mlp, after)
        return g_up, g_down.reshape(DFF, D)

    def mlp_grads(self, dw_up_sh, dw_down):
        return [self._reduce_start("mlp", W_NAMES[4:], [dw_up_sh, dw_down.reshape((N_DEV,) + SHARD_SHAPES[5])])]

    def mixer_grads(self, dw_a, dw_b, dw_out):
        token = self._reduce_mid("mlp", dw_b)
        grads = [_whole_to_cols(dw_a), _whole_to_cols(dw_b), dw_out.reshape((N_DEV,) + SHARD_SHAPES[3])]
        return [self._reduce_start("mixer", W_NAMES[1:4], grads, after=(token,))]

    def mid_backward(self, after):
        return [self._reduce_mid("mixer", after)]

    def w_in_grad(self, dw_in_t):
        grads = [_unpad_dw_in_t(dw_in_t)]
        token = self._reduce_start("w_in", W_NAMES[:1], grads)
        reduced = self.reduced("mixer", token) + self.reduced("mlp", token)
        self.updates, last = [], token
        for i, r in enumerate(reduced):
            self.updates.append(_reduce_adamw(*r, self.w_sh[1 + i], self.m_sh[1 + i], self.v_sh[1 + i],
                                              "adamw_" + W_NAMES[1 + i], deps=[last]))
            last = self.updates[-1][0]
        return [self._reduce_mid("w_in", last)]

    def w_in_update(self, after):
        (reduced,) = self.reduced("w_in", after)
        return _reduce_adamw(*reduced, self.w_sh[0], self.m_sh[0], self.v_sh[0], "adamw_" + W_NAMES[0])


def kernel(x, norm_attn_g, w_in, b_forget, w_branch_a, w_branch_b, w_out, norm_mlp_g, w_up, w_down, norm_final_g, loss_target, m_norm_attn_g, m_w_in, m_b_forget, m_w_branch_a, m_w_branch_b, m_w_out, m_norm_mlp_g, m_w_up, m_w_down, m_norm_final_g, v_norm_attn_g, v_w_in, v_b_forget, v_w_branch_a, v_w_branch_b, v_w_out, v_norm_mlp_g, v_w_up, v_w_down, v_norm_final_g):
    cx, cy, cc = _place()
    ids = jnp.stack([cc, 2 * cx + cy]).astype(jnp.int32)

    w_sh = [w_in[0].T] + [t[0] for t in (w_branch_a, w_branch_b, w_out, w_up, w_down)]
    m_sh = [m_w_in[0].T] + [t[0] for t in (m_w_branch_a, m_w_branch_b, m_w_out, m_w_up, m_w_down)]
    v_sh = [v_w_in[0].T] + [t[0] for t in (v_w_branch_a, v_w_branch_b, v_w_out, v_w_up, v_w_down)]

    w_in_t = _gather_w_in(jnp.pad(w_sh[0], ((0, W_SLAB - w_sh[0].shape[0]), (0, 0))))
    comm = _StepComm(ids, w_sh, m_sh, v_sh, w_in_t)
    b_pad = jnp.pad(b_forget, ((0, 0), (0, LANES - FOX_H)))

    loss_row, dx, dsmall = _local_step(
        x[0], loss_target[0], norm_attn_g, norm_mlp_g, norm_final_g.reshape(1, D), b_pad, w_in_t, comm)

    dg_attn, db, dg_mlp, dg_final = dsmall
    small = _small_allreduce_adamw(
        _pack_small(dg_attn, db, dg_mlp, dg_final, loss_row),
        _pack_small(norm_attn_g, b_forget, norm_mlp_g, norm_final_g.reshape(1, D)),
        _pack_small(m_norm_attn_g, m_b_forget, m_norm_mlp_g, m_norm_final_g.reshape(1, D)),
        _pack_small(v_norm_attn_g, v_b_forget, v_norm_mlp_g, v_norm_final_g.reshape(1, D)))
    big = [comm.w_in_update(small[0])] + comm.updates

    outs = [small[0][LOSS_ROW, 0], dx[None]]
    for q in range(4):
        s_attn, s_b, s_mlp, s_final = _unpack_small(small[q])
        b_in, b_a, b_b, b_out, b_up, b_down = [t[q][None] for t in big]
        b_in = jnp.swapaxes(b_in, 1, 2)
        outs += [s_attn, b_in, s_b, b_a, b_b, b_out, s_mlp, b_up, b_down, s_final]
    return tuple(outs)
```

```python
import functools

import jax
import jax.numpy as jnp
from jax import lax
from jax.experimental import pallas as pl
from jax.experimental.pallas import tpu as pltpu

F32 = jnp.float32
BF16 = jnp.bfloat16
MESH = pl.DeviceIdType.MESH

S = 2048
D = 1024
HD = 64
FOX_H = 8
FOX_W = FOX_H * HD
DIL_HG = 4
DIL_G = 3
DIL_W = DIL_G * DIL_HG * HD
DIL_OUT = DIL_HG * HD
DIL_BLK = 128
DIL_R = (1, 4, 16)
DFF = 4 * D
D_IN = 3 * FOX_W + FOX_H + 3 * DIL_W + 2 * D
EPS = 1e-6
NEG_INF = -1e30
SCALE = HD ** -0.5
ROPE_THETA = 500000.0
ROPE_DIM = HD // 4
N_DEV = 8

ADAM_LR = 0.001
ADAM_B1 = 0.9
ADAM_B2 = 0.999
ADAM_EPS = 1e-08
ADAM_WD = 0.01
ADAM_STEP = 10

LANES = 128
CB = 256
SEG_A = 0
SEG_B = 3 * FOX_W
SEG_Z = SEG_B + 3 * DIL_W
SEG_G = SEG_Z + CB
SEG_F = SEG_G + 2 * D
D_PAD = SEG_F + CB
SMALL_R = 32

VMEM_MB = 56


def _cparams(dims=None, vmem_mb=VMEM_MB, **kw):
    return pltpu.CompilerParams(dimension_semantics=dims, vmem_limit_bytes=vmem_mb << 20, **kw)


ANY = pl.BlockSpec(memory_space=pl.ANY)

_NN = (((1,), (0,)), ((), ()))
_NT = (((1,), (1,)), ((), ()))
_TN = (((0,), (0,)), ((), ()))


def _dot(a, b, dims):
    return lax.dot_general(a.astype(BF16), b.astype(BF16), dims, preferred_element_type=F32)


def _mm(a, b, *, mode, tm, tn, out_dtypes, name, n=None, k=None, a_off=0, b_off=0,
        b_sharded=False, out_sharded=False, epilogue=None, extras=(), deps=()):
    n_sh = b.shape[-1] if b_sharded else None
    if mode == "nn":
        m = a.shape[0]
        k = k or a.shape[1]
        a_spec = pl.BlockSpec((tm, k), lambda i, j: (i, a_off))
        if b_sharded:
            assert tn == n_sh
            n = N_DEV * n_sh
            b_spec = pl.BlockSpec((None, k, tn), lambda i, j: (j, 0, 0))
        else:
            n = n or b.shape[1]
            b_spec = pl.BlockSpec((k, tn), lambda i, j: (0, j + b_off))
        dims = _NN
    elif mode == "nt":
        m = a.shape[0]
        k = k or a.shape[1]
        a_spec = pl.BlockSpec((tm, k), lambda i, j: (i, a_off))
        if b_sharded:
            n = b.shape[1]
            b_spec = pl.BlockSpec((N_DEV, tn, n_sh), lambda i, j: (0, j, 0))
        else:
            n = n or b.shape[0]
            b_spec = pl.BlockSpec((tn, k), lambda i, j: (j + b_off, 0))
        dims = _NT
    else:
        k, m = a.shape
        n = n or b.shape[1]
        a_spec = pl.BlockSpec((k, tm), lambda i, j: (0, i))
        b_spec = pl.BlockSpec((k, tn), lambda i, j: (0, j + b_off))
        dims = _TN
    assert m % tm == 0 and n % tn == 0, (name, m, n, tm, tn)
    n_extra = len(extras)
    tile = pl.BlockSpec((tm, tn), lambda i, j: (i, j))
    if out_sharded:
        assert mode == "tn" and n // tn == N_DEV
        out_spec = pl.BlockSpec((None, tm, tn), lambda i, j: (j, i, 0))
        out_shape = (N_DEV, m, tn)
    else:
        out_spec, out_shape = tile, (m, n)

    def body(a_ref, b_ref, *refs):
        if mode == "nt" and b_sharded:
            acc = _dot(a_ref[:, 0:n_sh], b_ref[0], dims)
            for p in range(1, N_DEV):
                acc = acc + _dot(a_ref[:, p * n_sh:(p + 1) * n_sh], b_ref[p], dims)
        else:
            acc = _dot(a_ref[...], b_ref[...], dims)
        ex = [r[...] for r in refs[:n_extra]]
        outs = epilogue(acc, *ex) if epilogue is not None else (acc,)
        for o_ref, o in zip(refs[n_extra + len(deps):], outs):
            o_ref[...] = o.astype(o_ref.dtype)

    res = pl.pallas_call(
        body, name=name, grid=(m // tm, n // tn),
        in_specs=[a_spec, b_spec] + [tile] * n_extra + [ANY] * len(deps),
        out_specs=[out_spec] * len(out_dtypes),
        out_shape=[jax.ShapeDtypeStruct(out_shape, dt) for dt in out_dtypes],
        compiler_params=_cparams(("parallel", "parallel")),
    )(a, b, *extras, *deps)
    return res if len(out_dtypes) > 1 else res[0]


def _mm_rows(a, b, *, tm, name, epilogue, rows=(), vecs=(), row_out=(), vec_out=(), b_sharded=False, deps=()):
    m, k = a.shape
    n_rows, n_vecs, n_deps = len(rows), len(vecs), len(deps)
    n_sh = b.shape[-1] if b_sharded else None

    def body(a_ref, b_ref, *refs):
        row_refs, vec_refs = refs[:n_rows], refs[n_rows:n_rows + n_vecs]
        outs = refs[n_rows + n_vecs + n_deps:]
        if b_sharded:
            acc = _dot(a_ref[:, 0:n_sh], b_ref[0], _NT)
            for p in range(1, N_DEV):
                acc = acc + _dot(a_ref[:, p * n_sh:(p + 1) * n_sh], b_ref[p], _NT)
        else:
            acc = _dot(a_ref[...], b_ref[...], _NN)
        row_vals, vec_incs = epilogue(acc, [r[...] for r in row_refs], [v[...] for v in vec_refs])
        for o_ref, val in zip(outs[:len(row_out)], row_vals):
            o_ref[...] = val.astype(o_ref.dtype)

        @pl.when(pl.program_id(0) == 0)
        def _():
            for o_ref in outs[len(row_out):]:
                o_ref[...] = jnp.zeros_like(o_ref)

        for o_ref, inc in zip(outs[len(row_out):], vec_incs):
            o_ref[...] += inc

    tile = pl.BlockSpec((tm, D), lambda i: (i, 0))
    b_spec = pl.BlockSpec(b.shape, lambda i: (0,) * b.ndim)
    return pl.pallas_call(
        body, name=name, grid=(m // tm,),
        in_specs=[pl.BlockSpec((tm, k), lambda i: (i, 0)), b_spec] + [tile] * n_rows
                 + [pl.BlockSpec(v.shape, lambda i: (0, 0)) for v in vecs] + [ANY] * n_deps,
        out_specs=[tile] * len(row_out) + [pl.BlockSpec((1, w), lambda i: (0, 0)) for w in vec_out],
        out_shape=[jax.ShapeDtypeStruct((m, D), dt) for dt in row_out]
                  + [jax.ShapeDtypeStruct((1, w), F32) for w in vec_out],
        compiler_params=_cparams(("arbitrary",)),
    )(a, b, *rows, *vecs, *deps)


ROW_T = 256
ROWS_TM = 512


def _rms_rows(x, g):
    r = lax.rsqrt(jnp.mean(x * x, axis=-1, keepdims=True) + EPS)
    return (x * r) * g


def _rms_fwd(x, g, name, deps=()):
    def body(x_ref, g_ref, *refs):
        refs[-1][...] = _rms_rows(x_ref[...], g_ref[...]).astype(BF16)

    return pl.pallas_call(
        body, name=name, grid=(S // ROW_T,),
        in_specs=[pl.BlockSpec((ROW_T, D), lambda i: (i, 0)), pl.BlockSpec((1, D), lambda i: (0, 0))] + [ANY] * len(deps),
        out_specs=pl.BlockSpec((ROW_T, D), lambda i: (i, 0)),
        out_shape=jax.ShapeDtypeStruct((S, D), BF16),
        compiler_params=_cparams(("parallel",)),
    )(x, g, *deps)


def _rms_bwd_rows(dh, x, dres, g):
    r = lax.rsqrt(jnp.mean(x * x, axis=-1, keepdims=True) + EPS)
    xn = x * r
    dhn = dh * g
    dx = dres + r * (dhn - xn * jnp.mean(dhn * xn, axis=-1, keepdims=True))
    return dx, jnp.sum(dh * xn, axis=0, keepdims=True)


def _final_loss_rows(x, g, tgt):
    r = lax.rsqrt(jnp.mean(x * x, axis=-1, keepdims=True) + EPS)
    xn = x * r
    err = xn * g - tgt
    row_loss = jnp.mean(err * err, axis=-1, keepdims=True)
    loss = 0.5 * jnp.sum(row_loss, axis=0, keepdims=True) * jnp.ones((1, LANES), F32)
    dy = err * (1.0 / D)
    dyn = dy * g
    dx = r * (dyn - xn * jnp.mean(dyn * xn, axis=-1, keepdims=True))
    return dx, jnp.sum(dy * xn, axis=0, keepdims=True), loss


def _sigmoid(z):
    return 1.0 / (1.0 + jnp.exp(-z))


def _gate_fwd(proj_g, ya, yb):
    def body(ga_ref, gb_ref, ya_ref, yb_ref, o_ref):
        o_ref[...] = (_sigmoid(ga_ref[...]) * ya_ref[...] + _sigmoid(gb_ref[...]) * yb_ref[...]).astype(BF16)

    row = pl.BlockSpec((ROW_T, D), lambda i: (i, 0))
    return pl.pallas_call(
        body, name="gate_fwd", grid=(S // ROW_T,),
        in_specs=[row, pl.BlockSpec((ROW_T, D), lambda i: (i, 1)), row, row],
        out_specs=row, out_shape=jax.ShapeDtypeStruct((S, D), BF16),
        compiler_params=_cparams(("parallel",)),
    )(proj_g, proj_g, ya, yb)


GATE_TR = 1024
GATE_TC = 512
GATE_NB = 2 * D // GATE_TC


def _gate_bwd(dmixed, proj_g, ya, yb):
    half = GATE_NB // 2

    def body(dm_ref, g_ref, ya_ref, yb_ref, dy_ref, dg_ref):
        dm = dm_ref[...]
        s = _sigmoid(g_ref[...])
        y = jnp.where(pl.program_id(1) < half, ya_ref[...], yb_ref[...])
        dy_ref[...] = (dm * s).astype(BF16)
        dg_ref[...] = (dm * y * (s * (1.0 - s))).astype(BF16)

    blk = pl.BlockSpec((GATE_TR, GATE_TC), lambda i, j: (i, j))
    wrapped = pl.BlockSpec((GATE_TR, GATE_TC), lambda i, j: (i, j % half))
    return pl.pallas_call(
        body, name="gate_bwd", grid=(S // GATE_TR, GATE_NB),
        in_specs=[wrapped, blk, wrapped, wrapped],
        out_specs=[blk, pl.BlockSpec((GATE_TR, GATE_TC), lambda i, j: (i, j + SEG_G // GATE_TC))],
        out_shape=[jax.ShapeDtypeStruct((S, 2 * D), BF16), jax.ShapeDtypeStruct((S, D_PAD), BF16)],
        compiler_params=_cparams(("parallel", "parallel")),
    )(dmixed, proj_g, ya, yb)


FOX_TQ = 256


def _scan_rows(x, reverse):
    n = x.shape[0]
    row = lax.broadcasted_iota(jnp.int32, x.shape, 0)
    k = 1
    while k < n:
        if reverse:
            x = x + jnp.where(row < n - k, pltpu.roll(x, n - k, 0), 0.0)
        else:
            x = x + jnp.where(row >= k, pltpu.roll(x, k, 0), 0.0)
        k *= 2
    return x


def _fox_dscan(dft, dfs, proj_f, b_pad, dproj):
    def body(dft_ref, dfs_ref, f_ref, b_ref, _, dfa_ref, db_ref):
        dfs_pad = jnp.concatenate([dfs_ref[...], jnp.zeros((LANES - FOX_H, S), F32)], axis=0)
        df = dfs_pad.T + dft_ref[0]
        for hp in range(1, dft_ref.shape[0]):
            df = df + pltpu.roll(dft_ref[hp], 2 * hp, 1)
        dlf = _scan_rows(df, reverse=True)
        z = f_ref[...] + b_ref[...]
        lane = lax.broadcasted_iota(jnp.int32, (S, LANES), 1)
        dfa = jnp.where(lane < FOX_H, dlf / (1.0 + jnp.exp(z)), 0.0)
        dfa_ref[:, :LANES] = dfa.astype(BF16)
        dfa_ref[:, LANES:] = jnp.zeros((S, CB - LANES), BF16)
        db_ref[...] = jnp.sum(dfa, axis=0, keepdims=True)

    return pl.pallas_call(
        body, name="fox_dscan", grid=(1,),
        in_specs=[pl.BlockSpec(dft.shape, lambda i: (0, 0, 0)), pl.BlockSpec((FOX_H, S), lambda i: (0, 0)),
                  pl.BlockSpec((S, LANES), lambda i: (0, 0)), pl.BlockSpec((1, LANES), lambda i: (0, 0)), ANY],
        out_specs=[pl.BlockSpec((S, CB), lambda i: (0, SEG_F // CB)), pl.BlockSpec((1, LANES), lambda i: (0, 0))],
        out_shape=[jax.ShapeDtypeStruct((S, D_PAD), BF16), jax.ShapeDtypeStruct((1, LANES), F32)],
        input_output_aliases={4: 0},
        compiler_params=_cparams(("arbitrary",)),
    )(dft, dfs, proj_f, b_pad, dproj)


FOX_NQ = S // FOX_TQ
FOX_HP = FOX_W // LANES
HEADS_PER_LB = LANES // HD
FOX_AUG = 2 * LANES


def _fox_prepare(proj_a, proj_f, b_pad):
    tr = 512

    def body(q_ref, k_ref, f_ref, b_ref, qa_ref, ka_ref, hi_s, mid_s, lo_s):
        i = pl.program_id(0)

        @pl.when(i == 0)
        def _():
            z = f_ref[...] + b_ref[...]
            lf = jnp.minimum(z, 0.0) - jnp.log1p(jnp.exp(-jnp.abs(z)))
            f = _scan_rows(lf, reverse=False)
            hi = f.astype(BF16).astype(F32)
            r1 = f - hi
            mid = r1.astype(BF16).astype(F32)
            hi_s[...] = hi
            mid_s[...] = mid
            lo_s[...] = (r1 - mid).astype(BF16).astype(F32)

        rows = pl.ds(pl.multiple_of(i * tr, tr), tr)
        hi, mid, lo = hi_s[rows, :], mid_s[rows, :], lo_s[rows, :]
        lane = lax.broadcasted_iota(jnp.int32, (tr, HD), 1)
        q_ones = jnp.where((lane >= 3) & (lane < 6), 1.0, 0.0)
        k_ones = jnp.where(lane < 3, 1.0, 0.0)
        for h in range(FOX_H):
            a1, a2, a3 = hi[:, h:h + 1], mid[:, h:h + 1], lo[:, h:h + 1]
            q_extra = jnp.where(lane == 0, a1, jnp.where(lane == 1, a2, jnp.where(lane == 2, a3, q_ones)))
            k_extra = jnp.where(lane == 3, -a1, jnp.where(lane == 4, -a2, jnp.where(lane == 5, -a3, k_ones)))
            base = h * LANES
            qa_ref[:, base:base + HD] = (q_ref[:, h * HD:(h + 1) * HD].astype(F32) * SCALE).astype(BF16)
            qa_ref[:, base + HD:base + LANES] = q_extra.astype(BF16)
            ka_ref[:, base:base + HD] = k_ref[:, h * HD:(h + 1) * HD]
            ka_ref[:, base + HD:base + LANES] = k_extra.astype(BF16)

    out_blk = pl.BlockSpec((tr, FOX_H * LANES), lambda i: (i, 0))
    return pl.pallas_call(
        body, name="fox_prepare", grid=(S // tr,),
        in_specs=[pl.BlockSpec((tr, FOX_W), lambda i: (i, 0)), pl.BlockSpec((tr, FOX_W), lambda i: (i, 1)),
                  pl.BlockSpec((S, LANES), lambda i: (0, 0)), pl.BlockSpec((1, LANES), lambda i: (0, 0))],
        out_specs=[out_blk, out_blk],
        out_shape=[jax.ShapeDtypeStruct((S, FOX_H * LANES), BF16)] * 2,
        scratch_shapes=[pltpu.VMEM((S, LANES), F32)] * 3,
        compiler_params=_cparams(("arbitrary",)),
    )(proj_a, proj_a, proj_f, b_pad)


def _fox_scores(qa, ka, w):
    s = _dot(qa, ka, _NT)
    row = lax.broadcasted_iota(jnp.int32, (FOX_TQ, FOX_TQ), 0)
    col = lax.broadcasted_iota(jnp.int32, (FOX_TQ, FOX_TQ), 1)
    diag = jnp.where(col <= row, s[:, w * FOX_TQ:], NEG_INF)
    return diag if w == 0 else jnp.concatenate([s[:, :w * FOX_TQ], diag], axis=1)


def _fox_fwd(q_aug, k_aug, proj_a):
    def body(qa_ref, ka_ref, v_ref, o_ref, lse_ref):
        qi = pl.program_id(1)
        lane = lax.broadcasted_iota(jnp.int32, (FOX_TQ, LANES), 1)
        for w in range(FOX_NQ):
            @pl.when(qi == w)
            def _(w=w):
                width = (w + 1) * FOX_TQ
                lse = jnp.zeros((FOX_TQ, LANES), F32)
                for hh in range(HEADS_PER_LB):
                    aug = slice(hh * LANES, (hh + 1) * LANES)
                    sl = slice(hh * HD, (hh + 1) * HD)
                    s = _fox_scores(qa_ref[:, aug], ka_ref[:width, aug], w)
                    m = jnp.max(s, axis=-1, keepdims=True)
                    p = jnp.exp(s - m)
                    l = jnp.sum(p, axis=-1, keepdims=True)
                    o_ref[:, sl] = _dot(p / l, v_ref[:width, sl], _NN)
                    lse = jnp.where(lane == hh, m + jnp.log(l), lse)
                lse_ref[...] = lse

    return pl.pallas_call(
        body, name="fox_fwd", grid=(FOX_HP, FOX_NQ),
        in_specs=[pl.BlockSpec((FOX_TQ, FOX_AUG), lambda h, i: (i, h)),
                  pl.BlockSpec((S, FOX_AUG), lambda h, i: (0, h)),
                  pl.BlockSpec((S, LANES), lambda h, i: (0, 2 * FOX_HP + h))],
        out_specs=[pl.BlockSpec((FOX_TQ, LANES), lambda h, i: (i, h)),
                   pl.BlockSpec((None, FOX_TQ, LANES), lambda h, i: (h, i, 0))],
        out_shape=[jax.ShapeDtypeStruct((S, FOX_W), F32), jax.ShapeDtypeStruct((FOX_HP, S, LANES), F32)],
        compiler_params=_cparams(("parallel", "parallel")),
    )(q_aug, k_aug, proj_a)


def _fox_bwd(q_aug, k_aug, proj_a, lse, do, dproj, deps=()):
    n_q = FOX_NQ

    def body(qa_ref, ka_ref, v_ref, lse_ref, do_ref, *refs):
        dproj_ref, dft_ref, dfs_ref, dk_acc, dv_acc, dq_buf, kv_buf, dq_sems, kv_sems = refs[1 + len(deps):]
        hp = pl.program_id(0)
        t = pl.program_id(1)
        slot = t % 2
        col = pl.multiple_of(hp * LANES, LANES)

        def dq_copy(step, buf_slot):
            rows = pl.ds(pl.multiple_of(step * FOX_TQ, FOX_TQ), FOX_TQ)
            return pltpu.make_async_copy(dq_buf.at[buf_slot], dproj_ref.at[rows, pl.ds(col, LANES)], dq_sems.at[buf_slot])

        @pl.when(t == 0)
        def _():
            dk_acc[...] = jnp.zeros_like(dk_acc)
            dv_acc[...] = jnp.zeros_like(dv_acc)

        @pl.when((t == 0) & (hp == 0))
        def _():
            dfs_ref[...] = jnp.zeros_like(dfs_ref)

        @pl.when(t >= 2)
        def _():
            dq_copy(t - 2, slot).wait()

        lane = lax.broadcasted_iota(jnp.int32, (FOX_TQ, LANES), 1)
        for w in range(n_q):
            @pl.when(t == w)
            def _(w=w):
                width = (w + 1) * FOX_TQ
                lse_all = lse_ref[...]
                sub = lax.broadcasted_iota(jnp.int32, (FOX_H, width), 0)
                dft = jnp.zeros((FOX_TQ, LANES), F32)
                for hh in range(HEADS_PER_LB):
                    aug = slice(hh * LANES, (hh + 1) * LANES)
                    sl = slice(hh * HD, (hh + 1) * HD)
                    head = hp * HEADS_PER_LB + hh
                    qa = qa_ref[:, aug]
                    ka = ka_ref[:width, aug]
                    do_h = do_ref[:, sl]
                    s = _fox_scores(qa, ka, w)
                    p = jnp.exp(s - lse_all[:, hh:hh + 1])
                    dp = _dot(do_h, v_ref[:width, sl], _NT)
                    ds = p * (dp - jnp.sum(dp * p, axis=-1, keepdims=True))
                    dft = jnp.where(lane == hh, jnp.sum(ds, axis=-1, keepdims=True), dft)
                    dfs_ref[:, :width] -= jnp.where(sub == head, jnp.sum(ds, axis=0, keepdims=True), 0.0)
                    dq_buf[slot, :, sl] = (_dot(ds, ka[:, :HD], _NN) * SCALE).astype(BF16)
                    dk_acc[:width, sl] += _dot(ds, qa[:, :HD], _TN)
                    dv_acc[:width, sl] += _dot(p, do_h, _TN)
                dft_ref[...] = dft

        dq_copy(t, slot).start()

        @pl.when(t == n_q - 1)
        def _():
            dq_copy(t - 1, 1 - slot).wait()
            dq_copy(t, slot).wait()
            kv_buf[0] = dk_acc[...].astype(BF16)
            kv_buf[1] = dv_acc[...].astype(BF16)
            copies = [pltpu.make_async_copy(
                kv_buf.at[j], dproj_ref.at[:, pl.ds(pl.multiple_of((j + 1) * FOX_W + hp * LANES, LANES), LANES)],
                kv_sems.at[j]) for j in range(2)]
            for cp in copies:
                cp.start()
            for cp in copies:
                cp.wait()

    qblk = pl.BlockSpec((FOX_TQ, FOX_AUG), lambda h, t: (t, h))
    lane_blk = pl.BlockSpec((None, FOX_TQ, LANES), lambda h, t: (h, t, 0))
    return pl.pallas_call(
        body, name="fox_bwd", grid=(FOX_HP, n_q),
        in_specs=[qblk, pl.BlockSpec((S, FOX_AUG), lambda h, t: (0, h)),
                  pl.BlockSpec((S, LANES), lambda h, t: (0, 2 * FOX_HP + h)),
                  lane_blk, pl.BlockSpec((FOX_TQ, LANES), lambda h, t: (t, h)), ANY] + [ANY] * len(deps),
        out_specs=[ANY, lane_blk, pl.BlockSpec((FOX_H, S), lambda h, t: (0, 0))],
        out_shape=[jax.ShapeDtypeStruct((S, D_PAD), BF16),
                   jax.ShapeDtypeStruct((FOX_HP, S, LANES), F32), jax.ShapeDtypeStruct((FOX_H, S), F32)],
        scratch_shapes=[pltpu.VMEM((S, LANES), F32), pltpu.VMEM((S, LANES), F32),
                        pltpu.VMEM((2, FOX_TQ, LANES), BF16), pltpu.VMEM((2, S, LANES), BF16),
                        pltpu.SemaphoreType.DMA((2,)), pltpu.SemaphoreType.DMA((2,))],
        input_output_aliases={5: 0},
        compiler_params=_cparams(("arbitrary", "arbitrary")),
    )(q_aug, k_aug, proj_a, lse, do, dproj, *deps)


def _rope_tables():
    half = ROPE_DIM // 2
    shape = (DIL_G, S, LANES)
    g = lax.broadcasted_iota(jnp.int32, shape, 0)
    row = lax.broadcasted_iota(jnp.int32, shape, 1)
    lane = lax.broadcasted_iota(jnp.int32, shape, 2) % HD
    r = jnp.left_shift(1, 2 * g)
    per_class = S // r
    pos = (row % per_class) * r + row // per_class
    inv_freq = jnp.power(jnp.float32(ROPE_THETA), -(lane % half).astype(F32) * 2.0 / ROPE_DIM)
    ang = pos.astype(F32) * inv_freq
    cos, sin = jnp.cos(ang), jnp.sin(ang)
    t_self = jnp.where(lane < ROPE_DIM, cos, 1.0)
    t_up = jnp.where(lane < half, -sin, 0.0)
    t_dn = jnp.where((lane >= half) & (lane < ROPE_DIM), sin, 0.0)
    return t_self, t_up, t_dn


def _residue_pieces(r):
    if r == 1:
        return [(slice(i, i + 512), slice(i, i + 512)) for i in range(0, S, 512)]
    n = S // r
    return [(pl.ds(j, n, stride=r), slice(j * n, (j + 1) * n)) for j in range(r)]


def _rope_apply(x, a, u, d, backward):
    half = ROPE_DIM // 2
    if backward:
        return x * a + pltpu.roll(x * u, half, 1) + pltpu.roll(x * d, LANES - half, 1)
    return x * a + pltpu.roll(x, LANES - half, 1) * u + pltpu.roll(x, half, 1) * d


LB_PER_CB = CB // LANES
ROPE_NB = 3 * DIL_G * LB_PER_CB


def _rope_step(s):
    per_group = 3 * LB_PER_CB
    return s // per_group, (s % per_group) // LB_PER_CB, s % LB_PER_CB


def _rope_split(proj_b, tables):
    def body(x_ref, a_ref, u_ref, d_ref, o_ref):
        g, t, _ = _rope_step(pl.program_id(0))
        for gs in range(DIL_G):
            @pl.when(g == gs)
            def _(gs=gs):
                for tok, sub in _residue_pieces(DIL_R[gs]):
                    x = x_ref[tok, :]
                    y = _rope_apply(x, a_ref[sub, :], u_ref[sub, :], d_ref[sub, :], False)
                    o_ref[sub, :] = jnp.where(t < 2, y, x).astype(BF16)

    def in_index(s):
        g, t, hf = _rope_step(s)
        return 0, (t * DIL_G + g) * LB_PER_CB + hf

    def out_index(s):
        g, t, hf = _rope_step(s)
        return t * DIL_G + g, 0, hf

    tab = pl.BlockSpec((None, S, LANES), lambda s: (_rope_step(s)[0], 0, 0))
    return pl.pallas_call(
        body, name="rope_split", grid=(ROPE_NB,),
        in_specs=[pl.BlockSpec((S, LANES), in_index), tab, tab, tab],
        out_specs=pl.BlockSpec((None, S, LANES), out_index),
        out_shape=jax.ShapeDtypeStruct((3 * DIL_G, S, CB), BF16),
        compiler_params=_cparams(("parallel",)),
    )(proj_b, *tables)


def _rope_merge_bwd(dq3, dk3, dv3, tables, dproj):
    def body(dq_ref, dk_ref, dv_ref, a_ref, u_ref, d_ref, _, o_ref, tmp):
        s = pl.program_id(0)
        g, t, _ = _rope_step(s)
        for gs in range(DIL_G):
            @pl.when((g == gs) & (s < ROPE_NB))
            def _(gs=gs):
                for tok, sub in _residue_pieces(DIL_R[gs]):
                    x = jnp.where(t == 0, dq_ref[sub, :], jnp.where(t == 1, dk_ref[sub, :], dv_ref[sub, :]))
                    y = _rope_apply(x, a_ref[sub, :], u_ref[sub, :], d_ref[sub, :], True)
                    tmp[tok, :] = jnp.where(t < 2, y, x)
                o_ref[...] = tmp[...].astype(BF16)

        @pl.when(s >= ROPE_NB)
        def _():
            o_ref[...] = jnp.zeros_like(o_ref)

    def src_spec(own):
        def index(s):
            g, t, hf = _rope_step(jnp.minimum(s, ROPE_NB - 1))
            return g, 0, jnp.where(t == own, hf, jnp.where(t > own, LB_PER_CB - 1, 0))
        return pl.BlockSpec((None, S, LANES), index)

    def out_index(s):
        g, t, hf = _rope_step(s)
        col = SEG_B // LANES + (t * DIL_G + g) * LB_PER_CB + hf
        return 0, jnp.where(s < ROPE_NB, col, SEG_Z // LANES + s - ROPE_NB)

    tab = pl.BlockSpec((None, S, LANES), lambda s: (_rope_step(jnp.minimum(s, ROPE_NB - 1))[0], 0, 0))
    return pl.pallas_call(
        body, name="rope_merge_bwd", grid=(ROPE_NB + LB_PER_CB,),
        in_specs=[src_spec(0), src_spec(1), src_spec(2), tab, tab, tab, ANY],
        out_specs=pl.BlockSpec((S, LANES), out_index),
        out_shape=jax.ShapeDtypeStruct((S, D_PAD), BF16),
        scratch_shapes=[pltpu.VMEM((S, LANES), F32)],
        input_output_aliases={6: 0},
        compiler_params=_cparams(("arbitrary",)),
    )(dq3, dk3, dv3, *tables, dproj)


DIL_NB = S // DIL_BLK


_BQK = (((2,), (2,)), ((0,), (0,)))
_BQD = (((2,), (1,)), ((0,), (0,)))
_BKD = (((1,), (1,)), ((0,), (0,)))


def _dil_mask(g):
    shape = (DIL_NB, DIL_BLK, 2 * DIL_BLK)
    blocks_per_seq = jnp.right_shift(DIL_NB, 2 * g)
    has_prev = jnp.bitwise_and(lax.broadcasted_iota(jnp.int32, shape, 0), blocks_per_seq - 1) != 0
    a = lax.broadcasted_iota(jnp.int32, shape, 1)
    kk = lax.broadcasted_iota(jnp.int32, shape, 2)
    diff = DIL_BLK + a - kk
    return (diff >= 0) & (diff <= DIL_BLK) & ((kk >= DIL_BLK) | has_prev)


def _blocks(t):
    return t.reshape(DIL_NB, DIL_BLK, t.shape[-1])


def _with_prev(t):
    prev = jnp.concatenate([jnp.zeros((DIL_BLK, t.shape[-1]), t.dtype), t[:-DIL_BLK]], axis=0)
    return jnp.concatenate([_blocks(prev), _blocks(t)], axis=1)


def _fold_prev(t2):
    n = t2.shape[-1]
    to_prev = t2[:, :DIL_BLK].reshape(S, n)
    own = t2[:, DIL_BLK:].reshape(S, n)
    return own + jnp.concatenate([to_prev[DIL_BLK:], jnp.zeros((DIL_BLK, n), t2.dtype)], axis=0)


def _dil_group_spec(t, width=DIL_OUT):
    return pl.BlockSpec((None, S, width), lambda g: (t * DIL_G + g, 0, 0))


def _dil_fwd(qkv3):
    def body(q_ref, k_ref, v_ref, o_ref, lse_ref):
        ok = _dil_mask(pl.program_id(0))
        lane = lax.broadcasted_iota(jnp.int32, (S, LANES), 1)
        lse = jnp.zeros((S, LANES), F32)
        for h in range(DIL_HG):
            sl = slice(h * HD, (h + 1) * HD)
            s = jnp.where(ok, _dot(_blocks(q_ref[:, sl]), _with_prev(k_ref[:, sl]), _BQK) * SCALE, NEG_INF)
            m = jnp.max(s, axis=-1, keepdims=True)
            p = jnp.exp(s - m)
            l = jnp.sum(p, axis=-1, keepdims=True)
            o_ref[:, sl] = _dot(p / l, _with_prev(v_ref[:, sl]), _BQD).reshape(S, HD)
            lse = jnp.where(lane == h, (m + jnp.log(l)).reshape(S, 1), lse)
        lse_ref[...] = lse

    return pl.pallas_call(
        body, name="dil_fwd", grid=(DIL_G,),
        in_specs=[_dil_group_spec(0), _dil_group_spec(1), _dil_group_spec(2)],
        out_specs=[_dil_group_spec(0), _dil_group_spec(0, LANES)],
        out_shape=[jax.ShapeDtypeStruct((DIL_G, S, DIL_OUT), F32), jax.ShapeDtypeStruct((DIL_G, S, LANES), F32)],
        compiler_params=_cparams(("parallel",)),
    )(qkv3, qkv3, qkv3)


def _dil_bwd(qkv3, lse3, do3, c3):
    def body(q_ref, k_ref, v_ref, lse_ref, do_ref, c_ref, dq_ref, dk_ref, dv_ref):
        ok = _dil_mask(pl.program_id(0))
        lse = lse_ref[...]
        c = c_ref[...]
        for h in range(DIL_HG):
            sl = slice(h * HD, (h + 1) * HD)
            q = _blocks(q_ref[:, sl])
            k2 = _with_prev(k_ref[:, sl])
            do_h = _blocks(do_ref[:, sl])
            s = jnp.where(ok, _dot(q, k2, _BQK) * SCALE, NEG_INF)
            p = jnp.exp(s - _blocks(lse[:, h:h + 1]))
            dp = _dot(do_h, _with_prev(v_ref[:, sl]), _BQK)
            ds = p * (dp - _blocks(c[:, h:h + 1]))
            dsb = (ds * SCALE).astype(BF16)
            dq_ref[:, sl] = _dot(dsb, k2, _BQD).reshape(S, HD)
            dk_ref[:, sl] = _fold_prev(_dot(dsb, q, _BKD))
            dv_ref[:, sl] = _fold_prev(_dot(p, do_h, _BKD))

    return pl.pallas_call(
        body, name="dil_bwd", grid=(DIL_G,),
        in_specs=[_dil_group_spec(0), _dil_group_spec(1), _dil_group_spec(2), _dil_group_spec(0, LANES),
                  _dil_group_spec(0), _dil_group_spec(0, LANES)],
        out_specs=[_dil_group_spec(0)] * 3,
        out_shape=[jax.ShapeDtypeStruct((DIL_G, S, DIL_OUT), F32)] * 3,
        compiler_params=_cparams(("parallel",)),
    )(qkv3, qkv3, qkv3, lse3, do3, c3)


COMB_T = 256


def _dil_combine(o3, lse3):
    heads_per_lb = LANES // HD

    def body(o_ref, lse_ref, ob_ref, al_ref, *scratch):
        o_tok = [scratch[LB_PER_CB * gg:LB_PER_CB * (gg + 1)] for gg in range(DIL_G)]
        lse_tok = scratch[LB_PER_CB * DIL_G:]
        g = pl.program_id(0)
        for gs in range(DIL_G):
            @pl.when(g == gs)
            def _(gs=gs):
                for tok, sub in _residue_pieces(DIL_R[gs]):
                    for hf in range(LB_PER_CB):
                        o_tok[gs][hf][tok, :] = o_ref[sub, hf * LANES:(hf + 1) * LANES]
                    lse_tok[gs][tok, :] = lse_ref[sub, :]

        @pl.when(g == DIL_G - 1)
        def _():
            def chunk(i, carry):
                rows = pl.ds(pl.multiple_of(i * COMB_T, COMB_T), COMB_T)
                lse = [lse_tok[gg][rows, :] for gg in range(DIL_G)]
                m = jnp.maximum(jnp.maximum(lse[0], lse[1]), lse[2])
                e = [jnp.exp(lse[gg] - m) for gg in range(DIL_G)]
                den = (e[0] + e[1]) + e[2]
                al = [e[gg] / den for gg in range(DIL_G)]
                for gg in range(DIL_G):
                    al_ref[gg, rows, :] = al[gg]
                for h in range(DIL_HG):
                    hf, sl = h // heads_per_lb, slice((h % heads_per_lb) * HD, (h % heads_per_lb + 1) * HD)
                    acc = al[0][:, h:h + 1] * o_tok[0][hf][rows, sl]
                    for gg in range(1, DIL_G):
                        acc = acc + al[gg][:, h:h + 1] * o_tok[gg][hf][rows, sl]
                    ob_ref[rows, h * HD:(h + 1) * HD] = acc
                return carry

            lax.fori_loop(0, S // COMB_T, chunk, 0)

    return pl.pallas_call(
        body, name="dil_combine", grid=(DIL_G,),
        in_specs=[pl.BlockSpec((None, S, DIL_OUT), lambda g: (g, 0, 0)),
                  pl.BlockSpec((None, S, LANES), lambda g: (g, 0, 0))],
        out_specs=[pl.BlockSpec((S, DIL_OUT), lambda g: (0, 0)),
                   pl.BlockSpec((DIL_G, S, LANES), lambda g: (0, 0, 0))],
        out_shape=[jax.ShapeDtypeStruct((S, DIL_OUT), F32), jax.ShapeDtypeStruct((DIL_G, S, LANES), F32)],
        scratch_shapes=[pltpu.VMEM((S, LANES), F32)] * (DIL_G * (LB_PER_CB + 1)),
        compiler_params=_cparams(("arbitrary",)),
    )(o3, lse3)


def _dil_combine_bwd(dob, ob, alpha, deps=()):
    heads_per_lb = LANES // HD

    def body(dob_ref, ob_ref, al_ref, *refs):
        do_ref, c_ref = refs[len(deps):]
        g = pl.program_id(0)
        hf = pl.program_id(1)
        for gs in range(DIL_G):
            @pl.when(g == gs)
            def _(gs=gs):
                for tok, sub in _residue_pieces(DIL_R[gs]):
                    dob = dob_ref[tok, :]
                    prod = dob * ob_ref[tok, :]
                    al = al_ref[tok, :]
                    lane = lax.broadcasted_iota(jnp.int32, al.shape, 1)
                    c = jnp.where(hf == 0, 0.0, c_ref[sub, :])
                    for hh in range(heads_per_lb):
                        sl = slice(hh * HD, (hh + 1) * HD)
                        head = hf * heads_per_lb + hh
                        a = jnp.sum(jnp.where(lane == head, al, 0.0), axis=-1, keepdims=True)
                        do_ref[sub, sl] = (a * dob[:, sl]).astype(BF16)
                        c = jnp.where(lane == head, a * jnp.sum(prod[:, sl], axis=-1, keepdims=True), c)
                    c_ref[sub, :] = c

    half = pl.BlockSpec((S, LANES), lambda g, hf: (0, hf))
    return pl.pallas_call(
        body, name="dil_combine_bwd", grid=(DIL_G, LB_PER_CB),
        in_specs=[half, half, pl.BlockSpec((None, S, LANES), lambda g, hf: (g, 0, 0))] + [ANY] * len(deps),
        out_specs=[pl.BlockSpec((None, S, LANES), lambda g, hf: (g, 0, hf)),
                   pl.BlockSpec((None, S, LANES), lambda g, hf: (g, 0, 0))],
        out_shape=[jax.ShapeDtypeStruct((DIL_G, S, DIL_OUT), BF16), jax.ShapeDtypeStruct((DIL_G, S, LANES), F32)],
        compiler_params=_cparams(("parallel", "arbitrary")),
    )(dob, ob, alpha, *deps)


def _local_step(x, tgt, g_attn, g_mlp, g_final, b_pad, w_in_t, hooks):
    tables = _rope_tables()

    h1 = _rms_fwd(x, g_attn, "rms_attn_fwd", deps=hooks.first_deps())
    proj_a = _mm(h1, w_in_t, mode="nt", tm=S, tn=512, n=3 * FOX_W, out_dtypes=[BF16], name="proj_a")
    proj_b = _mm(h1, w_in_t, mode="nt", tm=S, tn=CB, n=3 * DIL_W, b_off=SEG_B // CB, out_dtypes=[F32], name="proj_b")
    proj_g = _mm(h1, w_in_t, mode="nt", tm=S, tn=CB, n=2 * D, b_off=SEG_G // CB, out_dtypes=[F32], name="proj_g")
    proj_f = _mm(h1, w_in_t, mode="nt", tm=S, tn=LANES, n=LANES, b_off=SEG_F // LANES, out_dtypes=[F32], name="proj_f")

    q_aug, k_aug = _fox_prepare(proj_a, proj_f, b_pad)
    oa, lse_a = _fox_fwd(q_aug, k_aug, proj_a)

    qkv3 = _rope_split(proj_b, tables)
    o3, lse3 = _dil_fwd(qkv3)
    ob, alpha = _dil_combine(o3, lse3)

    w_a, w_b, w_out = hooks.mixer_weights(ob)
    ya = _mm(oa, w_a, mode="nn", tm=S, tn=CB, out_dtypes=[F32], name="branch_a")
    yb = _mm(ob, w_b, mode="nn", tm=S, tn=CB, out_dtypes=[F32], name="branch_b")
    mixed = _gate_fwd(proj_g, ya, yb)

    def out_epilogue(acc, rows, vecs):
        x2 = rows[0] + acc
        return (x2, _rms_rows(x2, vecs[0])), ()

    x2, h2 = _mm_rows(mixed, w_out, tm=ROWS_TM, name="out_proj", epilogue=out_epilogue, rows=(x,), vecs=(g_mlp,),
                      row_out=(F32, BF16))
    w_up_sh, w_down = hooks.mlp_weights(h2)

    def up_epilogue(acc):
        r = jnp.maximum(acc, 0.0)
        return acc, r * r

    u, act = _mm(h2, w_up_sh, mode="nn", tm=S, tn=DFF // N_DEV, b_sharded=True, out_dtypes=[F32, BF16],
                 name="mlp_up", epilogue=up_epilogue)

    def loss_epilogue(acc, rows, vecs):
        dx3, dg, loss = _final_loss_rows(rows[0] + acc, vecs[0], rows[1])
        return (dx3,), (dg, loss)

    dx3, dg_final, loss = _mm_rows(act, w_down, tm=ROWS_TM, name="mlp_down_loss", epilogue=loss_epilogue,
                                   rows=(x2, tgt), vecs=(g_final,), row_out=(F32,), vec_out=(D, LANES))

    def rms_bwd_epilogue(acc, rows, vecs):
        dx, dg = _rms_bwd_rows(acc, rows[0], rows[1], vecs[0])
        return (dx,), (dg,)

    du = _mm(dx3, w_down, mode="nt", tm=S, tn=CB, out_dtypes=[BF16], name="mlp_down_bwd",
             epilogue=lambda acc, u_t: (acc * (2.0 * jnp.maximum(u_t, 0.0)),), extras=(u,))
    dw_down = _mm(act, dx3, mode="tn", tm=512, tn=512, out_dtypes=[F32], name="dw_down")
    dw_up_sh = _mm(h2, du, mode="tn", tm=D, tn=DFF // N_DEV, out_sharded=True, out_dtypes=[F32], name="dw_up")
    dx2, dg_mlp = _mm_rows(du, w_up_sh, b_sharded=True, tm=ROWS_TM, name="mlp_up_bwd", epilogue=rms_bwd_epilogue,
                           rows=(x2, dx3), vecs=(g_mlp,), row_out=(F32,), vec_out=(D,),
                           deps=hooks.mlp_grads(dw_up_sh, dw_down))

    dmixed = _mm(dx2, w_out, mode="nt", tm=S, tn=CB, out_dtypes=[F32], name="out_proj_bwd")
    dw_out = _mm(mixed, dx2, mode="tn", tm=D, tn=CB, out_dtypes=[F32], name="dw_out")

    dyab, dproj = _gate_bwd(dmixed, proj_g, ya, yb)
    doa = _mm(dyab, w_a, mode="nt", tm=S, tn=CB, k=D, a_off=0, out_dtypes=[BF16], name="branch_a_bwd")
    dw_a = _mm(oa, dyab, mode="tn", tm=FOX_W, tn=CB, n=D, b_off=0, out_dtypes=[F32], name="dw_branch_a")
    dob = _mm(dyab, w_b, mode="nt", tm=S, tn=CB, k=D, a_off=1, out_dtypes=[F32], name="branch_b_bwd")
    dw_b = _mm(ob, dyab, mode="tn", tm=DIL_OUT, tn=CB, n=D, b_off=D // CB, out_dtypes=[F32], name="dw_branch_b")

    dproj, dft, dfs = _fox_bwd(q_aug, k_aug, proj_a, lse_a, doa, dproj, deps=hooks.mixer_grads(dw_a, dw_b, dw_out))
    dproj, db = _fox_dscan(dft, dfs, proj_f, b_pad, dproj)

    do3, c3 = _dil_combine_bwd(dob, ob, alpha, deps=hooks.mid_backward(db))
    dq3, dk3, dv3 = _dil_bwd(qkv3, lse3, do3, c3)
    dproj = _rope_merge_bwd(dq3, dk3, dv3, tables, dproj)

    dw_in_t = _mm(dproj, h1, mode="tn", tm=CB, tn=D, out_dtypes=[F32], name="dw_in")
    dx, dg_attn = _mm_rows(dproj, w_in_t, tm=ROWS_TM // 2, name="proj_bwd", epilogue=rms_bwd_epilogue,
                           rows=(x, dx2), vecs=(g_attn,), row_out=(F32,), vec_out=(D,),
                           deps=hooks.w_in_grad(dw_in_t))

    return loss, dx, (dg_attn, db, dg_mlp, dg_final)


SHARD_SHAPES = ((D_IN // N_DEV, D), (FOX_W, D // N_DEV), (DIL_OUT, D // N_DEV), (D // N_DEV, D),
                (D, DFF // N_DEV), (DFF // N_DEV, D))
W_NAMES = ("w_in", "w_a", "w_b", "w_out", "w_up", "w_down")
AG_COPIES = 7
HBM = pl.BlockSpec(memory_space=pltpu.HBM)
SEM = pl.BlockSpec(memory_space=pltpu.SEMAPHORE)


def _place():
    return lax.axis_index("x"), lax.axis_index("y"), lax.axis_index("c")


W_SLAB = 752
W_WIN = 272
PAD_BLK = 256


def _pad_runs():
    runs = sorted((pad, pad + b - a, p, a) for p in range(N_DEV) for a, b, pad in _shard_pieces(p))
    blocks = []
    for k in range(D_PAD // PAD_BLK):
        lo, hi = k * PAD_BLK, (k + 1) * PAD_BLK
        blocks.append([(max(lo, r0) - lo, min(hi, r1) - lo, p, a + max(lo, r0) - r0)
                       for r0, r1, p, a in runs if max(lo, r0) < min(hi, r1)])
    return blocks


def _gather_w_in(shard):
    blocks = _pad_runs()

    def body(x_ref, out_ref, stage, land, send_sems, recv_sems, load_sem):
        x, y, c = _place()
        me, sibling = (x, y, c), (x, y, 1 - c)
        chips = [(1 - x, y), (x, 1 - y), (1 - x, 1 - y)]

        def slot(px, py, pc):
            return land.at[4 * px + 2 * py + pc]

        def copy(k, block, to):
            return pltpu.make_async_remote_copy(
                src_ref=slot(*block), dst_ref=slot(*block), send_sem=send_sems.at[k], recv_sem=recv_sems.at[k],
                device_id=to, device_id_type=MESH)

        load = pltpu.make_async_copy(x_ref, stage, load_sem)
        load.start()
        load.wait()
        land[4 * x + 2 * y + c] = stage[...].astype(BF16)
        sent = [copy(0, me, sibling)] + [copy(1 + j, me, (*chip, c)) for j, chip in enumerate(chips)]
        for cp in sent:
            cp.start()
        for j, chip in enumerate(chips):
            copy(1 + j, (*chip, c), me).wait_recv()
            sent.append(copy(4 + j, (*chip, c), sibling))
            sent[-1].start()
        copy(0, sibling, me).wait_recv()
        for j, chip in enumerate(chips):
            copy(4 + j, (*chip, 1 - c), me).wait_recv()
        for cp in sent:
            cp.wait_send()

        row = lax.broadcasted_iota(jnp.int32, (PAD_BLK, D), 0)
        for k, runs in enumerate(blocks):
            out = jnp.zeros((PAD_BLK, D), F32)
            for o_lo, o_hi, p, a in runs:
                start = min(a // 16 * 16, W_SLAB - W_WIN)
                win = land[p, start:start + W_WIN, :].astype(F32)
                moved = pltpu.roll(win, (o_lo - (a - start)) % W_WIN, 0)[:PAD_BLK]
                out = jnp.where((row >= o_lo) & (row < o_hi), moved, out)
            out_ref[k * PAD_BLK:(k + 1) * PAD_BLK, :] = out.astype(BF16)

    return pl.pallas_call(
        body, name="all_gather_w_in",
        out_shape=jax.ShapeDtypeStruct((D_PAD, D), BF16),
        in_specs=[ANY], out_specs=pl.BlockSpec(memory_space=pltpu.VMEM),
        scratch_shapes=[pltpu.VMEM((W_SLAB, D), F32), pltpu.VMEM((N_DEV, W_SLAB, D), BF16),
                        pltpu.SemaphoreType.DMA((AG_COPIES,)), pltpu.SemaphoreType.DMA((AG_COPIES,)),
                        pltpu.SemaphoreType.DMA(())],
        compiler_params=_cparams(None),
    )(shard)


def _place_own(shards):
    n_w = len(shards)
    shapes = [t.shape for t in shards]

    def body(*refs):
        in_refs, out_refs = refs[:n_w], refs[n_w:2 * n_w]
        stage, cast = refs[2 * n_w:3 * n_w], refs[3 * n_w:4 * n_w]
        load_sems, store_sems = refs[4 * n_w:]
        x, y, c = _place()
        loads = [pltpu.make_async_copy(in_refs[i], stage[i], load_sems.at[i]) for i in range(n_w)]
        for ld in loads:
            ld.start()
        stores = []
        for i in range(n_w):
            loads[i].wait()
            cast[i][...] = stage[i][...].astype(BF16)
            stores.append(pltpu.make_async_copy(cast[i], out_refs[i].at[4 * x + 2 * y + c], store_sems.at[i]))
            stores[-1].start()
        for st in stores:
            st.wait()

    return pl.pallas_call(
        body, name="place_own_shards",
        out_shape=[jax.ShapeDtypeStruct((N_DEV,) + sh, BF16) for sh in shapes],
        in_specs=[ANY] * n_w, out_specs=[ANY] * n_w,
        scratch_shapes=[pltpu.VMEM(sh, F32) for sh in shapes] + [pltpu.VMEM(sh, BF16) for sh in shapes]
                       + [pltpu.SemaphoreType.DMA((n_w,)), pltpu.SemaphoreType.DMA((n_w,))],
        compiler_params=_cparams(None),
    )(*shards)


PEER_COPIES = N_DEV - 1
SIBLING_COPIES = 4
CHIP_COPIES = 3


def _peer_copies(_, land_refs, send_sems, recv_sems):
    x, y, c = _place()
    mine = 4 * x + 2 * y + c
    copies = []
    for i, ref in enumerate(land_refs):
        for k in range(1, N_DEV):
            peer = (x ^ (k >> 2), y ^ ((k >> 1) & 1), c ^ (k & 1))
            n = i * PEER_COPIES + k - 1
            copies.append(pltpu.make_async_remote_copy(
                src_ref=ref.at[mine], dst_ref=ref.at[mine], send_sem=send_sems.at[n], recv_sem=recv_sems.at[n],
                device_id=peer, device_id_type=MESH))
    return copies


def _sibling_copies(g_refs, r_refs, send_sems, recv_sems):
    x, y, c = _place()
    return [pltpu.make_async_remote_copy(
        src_ref=g_refs[i].at[2 * k + (1 - c)], dst_ref=r_refs[i].at[k],
        send_sem=send_sems.at[SIBLING_COPIES * i + k], recv_sem=recv_sems.at[SIBLING_COPIES * i + k],
        device_id=(x, y, 1 - c), device_id_type=MESH) for i in range(len(g_refs)) for k in range(SIBLING_COPIES)]


def _chip_copies(p_refs, r_refs, send_sems, recv_sems):
    x, y, c = _place()
    chips = [(1 - x, y), (x, 1 - y), (1 - x, 1 - y)]
    return [pltpu.make_async_remote_copy(
        src_ref=p_refs[i].at[2 * cx + cy], dst_ref=r_refs[i].at[j],
        send_sem=send_sems.at[CHIP_COPIES * i + j], recv_sem=recv_sems.at[CHIP_COPIES * i + j],
        device_id=(cx, cy, c), device_id_type=MESH) for i in range(len(p_refs)) for j, (cx, cy) in enumerate(chips)]


def _in_hbm(a):
    return pltpu.with_memory_space_constraint(a, pltpu.HBM)


def _exchange_start(name, copies_fn, n_copies, srcs, lands, after=()):
    n_s, n_l, n_a = len(srcs), len(lands), len(after)

    def body(*refs):
        outs = refs[n_s + n_l + n_a:]
        for cp in copies_fn(refs[:n_s], refs[n_s:n_s + n_l], outs[0], outs[1]):
            cp.start()
        outs[-1][...] = jnp.zeros_like(outs[-1])

    res = pl.pallas_call(
        body, name=name,
        out_shape=[pltpu.SemaphoreType.DMA((n_copies,)), pltpu.SemaphoreType.DMA((n_copies,))]
                  + [pltpu.HBM(t.shape, t.dtype) for t in (*srcs, *lands)] + [jax.ShapeDtypeStruct((8, LANES), F32)],
        in_specs=[HBM] * (n_s + n_l) + [ANY] * n_a,
        out_specs=[SEM, SEM] + [HBM] * (n_s + n_l) + [pl.BlockSpec(memory_space=pltpu.VMEM)],
        input_output_aliases={i: 2 + i for i in range(n_s + n_l)},
        compiler_params=pltpu.CompilerParams(has_side_effects=pltpu.SideEffectType.DATAFLOW_SIDE_EFFECTING),
    )(*[_in_hbm(t) for t in (*srcs, *lands)], *after)
    return res[0], res[1], list(res[2:2 + n_s]), list(res[2 + n_s:2 + n_s + n_l]), res[-1]


def _exchange_wait(name, copies_fn, started, after):
    send_sems, recv_sems, srcs, lands, _ = started
    n_s, n_l = len(srcs), len(lands)

    def body(*refs):
        for cp in copies_fn(refs[:n_s], refs[n_s:n_s + n_l], refs[n_s + n_l], refs[n_s + n_l + 1]):
            cp.wait_send()
            cp.wait_recv()

    res = pl.pallas_call(
        body, name=name,
        out_shape=[pltpu.HBM(t.shape, t.dtype) for t in (*srcs, *lands)],
        in_specs=[HBM] * (n_s + n_l) + [SEM, SEM, ANY],
        out_specs=[HBM] * (n_s + n_l),
        input_output_aliases={i: i for i in range(n_s + n_l)},
        compiler_params=pltpu.CompilerParams(has_side_effects=pltpu.SideEffectType.DATAFLOW_SIDE_EFFECTING),
    )(*srcs, *lands, send_sems, recv_sems, after)
    return list(res[:n_s]), list(res[n_s:])


def _shard_block(shape):
    rows, cols = shape
    if rows % 256:
        return rows, 256
    return min(rows, 512 if cols <= LANES else 256), cols


def _chip_partial(ids, g_stack, recv1, name):
    shape = g_stack.shape[1:]
    br, bc = _shard_block(shape)

    def body(ids_ref, g_ref, r_ref, pb_ref, own_ref):
        s = g_ref[...] + r_ref[...]
        pb_ref[...] = s.astype(BF16)

        @pl.when(pl.program_id(2) == ids_ref[1])
        def _():
            own_ref[...] = s

    grid_spec = pltpu.PrefetchScalarGridSpec(
        num_scalar_prefetch=1, grid=(shape[0] // br, shape[1] // bc, 4),
        in_specs=[pl.BlockSpec((None, br, bc), lambda r, q, k, ids: (2 * k + ids[0], r, q)),
                  pl.BlockSpec((None, br, bc), lambda r, q, k, ids: (k, r, q))],
        out_specs=[pl.BlockSpec((None, br, bc), lambda r, q, k, ids: (k, r, q)),
                   pl.BlockSpec((br, bc), lambda r, q, k, ids: (r, q))])
    return pl.pallas_call(
        body, name=name, grid_spec=grid_spec,
        out_shape=[jax.ShapeDtypeStruct((4,) + shape, BF16), jax.ShapeDtypeStruct(shape, F32)],
        compiler_params=_cparams(("parallel", "parallel", "arbitrary")),
    )(ids, g_stack, recv1)


def _adamw(w, g, m, v):
    m = ADAM_B1 * m + (1.0 - ADAM_B1) * g
    v = ADAM_B2 * v + (1.0 - ADAM_B2) * (g * g)
    m_hat = m / (1.0 - ADAM_B1 ** ADAM_STEP)
    v_hat = v / (1.0 - ADAM_B2 ** ADAM_STEP)
    delta = -ADAM_LR * (m_hat / (jnp.sqrt(v_hat) + ADAM_EPS) + ADAM_WD * w)
    return delta, m, v


def _reduce_adamw(own, recv2, w, m, v, name, deps=()):
    shape = own.shape
    br, bc = _shard_block(shape)

    def body(own_ref, r_ref, w_ref, m_ref, v_ref, *refs):
        g_ref, d_ref, nm_ref, nv_ref = refs[len(deps):]
        g = own_ref[...]
        for j in range(3):
            g = g + r_ref[j].astype(F32)
        delta, nm, nv = _adamw(w_ref[...], g, m_ref[...], v_ref[...])
        g_ref[...] = g
        d_ref[...] = delta
        nm_ref[...] = nm
        nv_ref[...] = nv

    blk = pl.BlockSpec((br, bc), lambda r, q: (r, q))
    return pl.pallas_call(
        body, name=name, grid=(shape[0] // br, shape[1] // bc),
        in_specs=[blk, pl.BlockSpec((3, br, bc), lambda r, q: (0, r, q)), blk, blk, blk] + [ANY] * len(deps),
        out_specs=[blk] * 4, out_shape=[jax.ShapeDtypeStruct(shape, F32)] * 4,
        compiler_params=_cparams(("parallel", "parallel")),
    )(own, recv2, w, m, v, *deps)


def _small_allreduce_adamw(gvec, w, m, v, deps=()):
    def body(g_ref, w_ref, m_ref, v_ref, *refs):
        go_ref, d_ref, nm_ref, nv_ref, buf, send_sems, recv_sems = refs[len(deps):]
        x, y, c = _place()
        my_slot = 4 * x + 2 * y + c
        buf[my_slot] = g_ref[...]
        copies = []
        for k in range(1, N_DEV):
            px, py, pc = x ^ (k >> 2), y ^ ((k >> 1) & 1), c ^ (k & 1)
            copies.append(pltpu.make_async_remote_copy(
                src_ref=g_ref, dst_ref=buf.at[my_slot], send_sem=send_sems.at[k - 1], recv_sem=recv_sems.at[k - 1],
                device_id=(px, py, pc), device_id_type=MESH))
        for cp in copies:
            cp.start()
        for k in range(1, N_DEV):
            px, py, pc = x ^ (k >> 2), y ^ ((k >> 1) & 1), c ^ (k & 1)
            pltpu.make_async_remote_copy(
                src_ref=g_ref, dst_ref=buf.at[4 * px + 2 * py + pc], send_sem=send_sems.at[k - 1],
                recv_sem=recv_sems.at[k - 1], device_id=(px, py, pc), device_id_type=MESH).wait_recv()
        for cp in copies:
            cp.wait_send()
        g = buf[0]
        for s in range(1, N_DEV):
            g = g + buf[s]
        delta, nm, nv = _adamw(w_ref[...], g, m_ref[...], v_ref[...])
        go_ref[...] = g
        d_ref[...] = delta
        nm_ref[...] = nm
        nv_ref[...] = nv

    vm = pl.BlockSpec(memory_space=pltpu.VMEM)
    return pl.pallas_call(
        body, name="small_allreduce_adamw",
        in_specs=[vm] * 4 + [ANY] * len(deps), out_specs=[vm] * 4,
        out_shape=[jax.ShapeDtypeStruct((SMALL_R, LANES), F32)] * 4,
        scratch_shapes=[pltpu.VMEM((N_DEV, SMALL_R, LANES), F32),
                        pltpu.SemaphoreType.DMA((N_DEV - 1,)), pltpu.SemaphoreType.DMA((N_DEV - 1,))],
    )(gvec, w, m, v, *deps)


def _cols_to_whole(stack):
    rows = stack.shape[1]
    return stack.transpose(1, 0, 2).reshape(rows, -1)


def _whole_to_cols(t):
    rows = t.shape[0]
    return t.reshape(rows, N_DEV, -1).transpose(1, 0, 2)


W_IN_SEGMENTS = ((0, 3 * FOX_W, SEG_A), (3 * FOX_W, 3 * FOX_W + FOX_H, SEG_F),
                 (3 * FOX_W + FOX_H, 3 * FOX_W + FOX_H + 3 * DIL_W, SEG_B), (3 * FOX_W + FOX_H + 3 * DIL_W, D_IN, SEG_G))


def _shard_pieces(p):
    rows = D_IN // N_DEV
    lo, hi = p * rows, (p + 1) * rows
    return [(max(lo, a) - lo, min(hi, b) - lo, pad + max(lo, a) - a)
            for a, b, pad in W_IN_SEGMENTS if max(lo, a) < min(hi, b)]


def _unpad_dw_in_t(dwp):
    return jnp.stack([jnp.concatenate([dwp[pad:pad + b - a] for a, b, pad in _shard_pieces(p)], axis=0)
                      for p in range(N_DEV)])


LOSS_ROW = 25


def _pack_small(g_attn, b, g_mlp, g_final, loss_row=None):
    tail = jnp.pad(b, ((0, 7), (0, LANES - b.shape[1])))
    if loss_row is not None:
        tail = tail + jnp.pad(loss_row, ((LOSS_ROW - 24, 31 - LOSS_ROW), (0, 0)))
    return jnp.concatenate([g_attn.reshape(8, LANES), g_mlp.reshape(8, LANES), g_final.reshape(8, LANES), tail], axis=0)


def _unpack_small(p):
    return (p[0:8].reshape(1, D), p[24:25, :FOX_H], p[8:16].reshape(1, D), p[16:24].reshape(D))


class _StepComm:
    def __init__(self, ids, w_sh, m_sh, v_sh, after_w_in):
        self.ids = ids
        self.w_sh, self.m_sh, self.v_sh = w_sh, m_sh, v_sh
        self.updates = None
        lands = _place_own(w_sh[1:])
        self.gather_mixer = _exchange_start("gather_mixer_start", _peer_copies, PEER_COPIES * 3, [], lands[:3],
                                            after=(after_w_in,))
        self.gather_mlp = _exchange_start("gather_mlp_start", _peer_copies, PEER_COPIES * 2, [], lands[3:],
                                          after=(self.gather_mixer[-1],))
        self.sibling = {}
        self.chips = {}
        self.own = {}
        self.names = {}

    def _reduce_start(self, group, names, grads, after=()):
        lands = [lax.empty((SIBLING_COPIES,) + t.shape[1:], F32) for t in grads]
        self.sibling[group] = _exchange_start("grad_%s_sibling_start" % group, _sibling_copies,
                                              SIBLING_COPIES * len(grads), grads, lands, after=after)
        self.names[group] = names
        return self.sibling[group][-1]

    def _reduce_mid(self, group, after):
        grads, recv1 = _exchange_wait("grad_%s_sibling_wait" % group, _sibling_copies, self.sibling[group], after)
        parts = [_chip_partial(self.ids, g, r, "grad_partial_" + n) for g, r, n in zip(grads, recv1, self.names[group])]
        self.own[group] = [p[1] for p in parts]
        srcs = [p[0] for p in parts]
        lands = [lax.empty((CHIP_COPIES,) + t.shape[1:], BF16) for t in srcs]
        self.chips[group] = _exchange_start("grad_%s_chips_start" % group, _chip_copies, CHIP_COPIES * len(srcs),
                                            srcs, lands)
        return self.chips[group][-1]

    def reduced(self, group, after):
        _, recv2 = _exchange_wait("grad_%s_chips_wait" % group, _chip_copies, self.chips[group], after)
        return list(zip(self.own[group], recv2))

    def first_deps(self):
        return [self.gather_mlp[-1]]

    def mixer_weights(self, after):
        _, (g_a, g_b, g_out) = _exchange_wait("gather_mixer_wait", _peer_copies, self.gather_mixer, after)
        return _cols_to_whole(g_a), _cols_to_whole(g_b), g_out.reshape(D, D)

    def mlp_weights(self, after):
        _, (g_up, g_down) = _exchange_wait("gather_mlp_wait", _peer_copies, self.gather_mlp, after)
        return g_up, g_down.reshape(DFF, D)

    def mlp_grads(self, dw_up_sh, dw_down):
        return [self._reduce_start("mlp", W_NAMES[4:], [dw_up_sh, dw_down.reshape((N_DEV,) + SHARD_SHAPES[5])])]

    def mixer_grads(self, dw_a, dw_b, dw_out):
        token = self._reduce_mid("mlp", dw_b)
        grads = [_whole_to_cols(dw_a), _whole_to_cols(dw_b), dw_out.reshape((N_DEV,) + SHARD_SHAPES[3])]
        return [self._reduce_start("mixer", W_NAMES[1:4], grads, after=(token,))]

    def mid_backward(self, after):
        return [self._reduce_mid("mixer", after)]

    def w_in_grad(self, dw_in_t):
        grads = [_unpad_dw_in_t(dw_in_t)]
        token = self._reduce_start("w_in", W_NAMES[:1], grads)
        reduced = self.reduced("mixer", token) + self.reduced("mlp", token)
        self.updates, last = [], token
        for i, r in enumerate(reduced):
            self.updates.append(_reduce_adamw(*r, self.w_sh[1 + i], self.m_sh[1 + i], self.v_sh[1 + i],
                                              "adamw_" + W_NAMES[1 + i], deps=[last]))
            last = self.updates[-1][0]
        return [self._reduce_mid("w_in", last)]

    def w_in_update(self, after):
        (reduced,) = self.reduced("w_in", after)
        return _reduce_adamw(*reduced, self.w_sh[0], self.m_sh[0], self.v_sh[0], "adamw_" + W_NAMES[0])


def kernel(x, norm_attn_g, w_in, b_forget, w_branch_a, w_branch_b, w_out, norm_mlp_g, w_up, w_down, norm_final_g, loss_target, m_norm_attn_g, m_w_in, m_b_forget, m_w_branch_a, m_w_branch_b, m_w_out, m_norm_mlp_g, m_w_up, m_w_down, m_norm_final_g, v_norm_attn_g, v_w_in, v_b_forget, v_w_branch_a, v_w_branch_b, v_w_out, v_norm_mlp_g, v_w_up, v_w_down, v_norm_final_g):
    cx, cy, cc = _place()
    ids = jnp.stack([cc, 2 * cx + cy]).astype(jnp.int32)

    w_sh = [w_in[0].T] + [t[0] for t in (w_branch_a, w_branch_b, w_out, w_up, w_down)]
    m_sh = [m_w_in[0].T] + [t[0] for t in (m_w_branch_a, m_w_branch_b, m_w_out, m_w_up, m_w_down)]
    v_sh = [v_w_in[0].T] + [t[0] for t in (v_w_branch_a, v_w_branch_b, v_w_out, v_w_up, v_w_down)]

    w_in_t = _gather_w_in(jnp.pad(w_sh[0], ((0, W_SLAB - w_sh[0].shape[0]), (0, 0))))
    comm = _StepComm(ids, w_sh, m_sh, v_sh, w_in_t)
    b_pad = jnp.pad(b_forget, ((0, 0), (0, LANES - FOX_H)))

    loss_row, dx, dsmall = _local_step(
        x[0], loss_target[0], norm_attn_g, norm_mlp_g, norm_final_g.reshape(1, D), b_pad, w_in_t, comm)

    dg_attn, db, dg_mlp, dg_final = dsmall
    small = _small_allreduce_adamw(
        _pack_small(dg_attn, db, dg_mlp, dg_final, loss_row),
        _pack_small(norm_attn_g, b_forget, norm_mlp_g, norm_final_g.reshape(1, D)),
        _pack_small(m_norm_attn_g, m_b_forget, m_norm_mlp_g, m_norm_final_g.reshape(1, D)),
        _pack_small(v_norm_attn_g, v_b_forget, v_norm_mlp_g, v_norm_final_g.reshape(1, D)))
    big = [comm.w_in_update(small[0])] + comm.updates

    outs = [small[0][LOSS_ROW, 0], dx[None]]
    for q in range(4):
        s_attn, s_b, s_mlp, s_final = _unpack_small(small[q])
        b_in, b_a, b_b, b_out, b_up, b_down = [t[q][None] for t in big]
        b_in = jnp.swapaxes(b_in, 1, 2)
        outs += [s_attn, b_in, s_b, b_a, b_b, b_out, s_mlp, b_up, b_down, s_final]
    return tuple(outs)
```

```python
import functools

import jax
import jax.numpy as jnp
from jax import lax
from jax.experimental import pallas as pl
from jax.experimental.pallas import tpu as pltpu

F32 = jnp.float32
BF16 = jnp.bfloat16
MESH = pl.DeviceIdType.MESH

S = 2048
D = 1024
HD = 64
FOX_H = 8
FOX_W = FOX_H * HD
DIL_HG = 4
DIL_G = 3
DIL_W = DIL_G * DIL_HG * HD
DIL_OUT = DIL_HG * HD
DIL_BLK = 128
DIL_R = (1, 4, 16)
DFF = 4 * D
D_IN = 3 * FOX_W + FOX_H + 3 * DIL_W + 2 * D
EPS = 1e-6
NEG_INF = -1e30
SCALE = HD ** -0.5
ROPE_THETA = 500000.0
ROPE_DIM = HD // 4
N_DEV = 8

ADAM_LR = 0.001
ADAM_B1 = 0.9
ADAM_B2 = 0.999
ADAM_EPS = 1e-08
ADAM_WD = 0.01
ADAM_STEP = 10

LANES = 128
CB = 256
SEG_A = 0
SEG_B = 3 * FOX_W
SEG_Z = SEG_B + 3 * DIL_W
SEG_G = SEG_Z + CB
SEG_F = SEG_G + 2 * D
D_PAD = SEG_F + CB
SMALL_R = 32

VMEM_MB = 56


def _cparams(dims=None, vmem_mb=VMEM_MB, **kw):
    return pltpu.CompilerParams(dimension_semantics=dims, vmem_limit_bytes=vmem_mb << 20, **kw)


ANY = pl.BlockSpec(memory_space=pl.ANY)

_NN = (((1,), (0,)), ((), ()))
_NT = (((1,), (1,)), ((), ()))
_TN = (((0,), (0,)), ((), ()))


def _dot(a, b, dims):
    return lax.dot_general(a.astype(BF16), b.astype(BF16), dims, preferred_element_type=F32)


def _mm(a, b, *, mode, tm, tn, out_dtypes, name, n=None, k=None, a_off=0, b_off=0,
        b_sharded=False, out_sharded=False, epilogue=None, extras=(), deps=()):
    n_sh = b.shape[-1] if b_sharded else None
    if mode == "nn":
        m = a.shape[0]
        k = k or a.shape[1]
        a_spec = pl.BlockSpec((tm, k), lambda i, j: (i, a_off))
        if b_sharded:
            assert tn == n_sh
            n = N_DEV * n_sh
            b_spec = pl.BlockSpec((None, k, tn), lambda i, j: (j, 0, 0))
        else:
            n = n or b.shape[1]
            b_spec = pl.BlockSpec((k, tn), lambda i, j: (0, j + b_off))
        dims = _NN
    elif mode == "nt":
        m = a.shape[0]
        k = k or a.shape[1]
        a_spec = pl.BlockSpec((tm, k), lambda i, j: (i, a_off))
        if b_sharded:
            n = b.shape[1]
            b_spec = pl.BlockSpec((N_DEV, tn, n_sh), lambda i, j: (0, j, 0))
        else:
            n = n or b.shape[0]
            b_spec = pl.BlockSpec((tn, k), lambda i, j: (j + b_off, 0))
        dims = _NT
    else:
        k, m = a.shape
        n = n or b.shape[1]
        a_spec = pl.BlockSpec((k, tm), lambda i, j: (0, i))
        b_spec = pl.BlockSpec((k, tn), lambda i, j: (0, j + b_off))
        dims = _TN
    assert m % tm == 0 and n % tn == 0, (name, m, n, tm, tn)
    n_extra = len(extras)
    tile = pl.BlockSpec((tm, tn), lambda i, j: (i, j))
    if out_sharded:
        assert mode == "tn" and n // tn == N_DEV
        out_spec = pl.BlockSpec((None, tm, tn), lambda i, j: (j, i, 0))
        out_shape = (N_DEV, m, tn)
    else:
        out_spec, out_shape = tile, (m, n)

    def body(a_ref, b_ref, *refs):
        if mode == "nt" and b_sharded:
            acc = _dot(a_ref[:, 0:n_sh], b_ref[0], dims)
            for p in range(1, N_DEV):
                acc = acc + _dot(a_ref[:, p * n_sh:(p + 1) * n_sh], b_ref[p], dims)
        else:
            acc = _dot(a_ref[...], b_ref[...], dims)
        ex = [r[...] for r in refs[:n_extra]]
        outs = epilogue(acc, *ex) if epilogue is not None else (acc,)
        for o_ref, o in zip(refs[n_extra + len(deps):], outs):
            o_ref[...] = o.astype(o_ref.dtype)

    res = pl.pallas_call(
        body, name=name, grid=(m // tm, n // tn),
        in_specs=[a_spec, b_spec] + [tile] * n_extra + [ANY] * len(deps),
        out_specs=[out_spec] * len(out_dtypes),
        out_shape=[jax.ShapeDtypeStruct(out_shape, dt) for dt in out_dtypes],
        compiler_params=_cparams(("parallel", "parallel")),
    )(a, b, *extras, *deps)
    return res if len(out_dtypes) > 1 else res[0]


def _mm_rows(a, b, *, tm, name, epilogue, rows=(), vecs=(), row_out=(), vec_out=(), b_sharded=False, deps=()):
    m, k = a.shape
    n_rows, n_vecs, n_deps = len(rows), len(vecs), len(deps)
    n_sh = b.shape[-1] if b_sharded else None

    def body(a_ref, b_ref, *refs):
        row_refs, vec_refs = refs[:n_rows], refs[n_rows:n_rows + n_vecs]
        outs = refs[n_rows + n_vecs + n_deps:]
        if b_sharded:
            acc = _dot(a_ref[:, 0:n_sh], b_ref[0], _NT)
            for p in range(1, N_DEV):
                acc = acc + _dot(a_ref[:, p * n_sh:(p + 1) * n_sh], b_ref[p], _NT)
        else:
            acc = _dot(a_ref[...], b_ref[...], _NN)
        row_vals, vec_incs = epilogue(acc, [r[...] for r in row_refs], [v[...] for v in vec_refs])
        for o_ref, val in zip(outs[:len(row_out)], row_vals):
            o_ref[...] = val.astype(o_ref.dtype)

        @pl.when(pl.program_id(0) == 0)
        def _():
            for o_ref in outs[len(row_out):]:
                o_ref[...] = jnp.zeros_like(o_ref)

        for o_ref, inc in zip(outs[len(row_out):], vec_incs):
            o_ref[...] += inc

    tile = pl.BlockSpec((tm, D), lambda i: (i, 0))
    b_spec = pl.BlockSpec(b.shape, lambda i: (0,) * b.ndim)
    return pl.pallas_call(
        body, name=name, grid=(m // tm,),
        in_specs=[pl.BlockSpec((tm, k), lambda i: (i, 0)), b_spec] + [tile] * n_rows
                 + [pl.BlockSpec(v.shape, lambda i: (0, 0)) for v in vecs] + [ANY] * n_deps,
        out_specs=[tile] * len(row_out) + [pl.BlockSpec((1, w), lambda i: (0, 0)) for w in vec_out],
        out_shape=[jax.ShapeDtypeStruct((m, D), dt) for dt in row_out]
                  + [jax.ShapeDtypeStruct((1, w), F32) for w in vec_out],
        compiler_params=_cparams(("arbitrary",)),
    )(a, b, *rows, *vecs, *deps)


ROW_T = 256
ROWS_TM = 512


def _rms_rows(x, g):
    r = lax.rsqrt(jnp.mean(x * x, axis=-1, keepdims=True) + EPS)
    return (x * r) * g


def _rms_fwd(x, g, name, deps=()):
    def body(x_ref, g_ref, *refs):
        refs[-1][...] = _rms_rows(x_ref[...], g_ref[...]).astype(BF16)

    return pl.pallas_call(
        body, name=name, grid=(S // ROW_T,),
        in_specs=[pl.BlockSpec((ROW_T, D), lambda i: (i, 0)), pl.BlockSpec((1, D), lambda i: (0, 0))] + [ANY] * len(deps),
        out_specs=pl.BlockSpec((ROW_T, D), lambda i: (i, 0)),
        out_shape=jax.ShapeDtypeStruct((S, D), BF16),
        compiler_params=_cparams(("parallel",)),
    )(x, g, *deps)


def _rms_bwd_rows(dh, x, dres, g):
    r = lax.rsqrt(jnp.mean(x * x, axis=-1, keepdims=True) + EPS)
    xn = x * r
    dhn = dh * g
    dx = dres + r * (dhn - xn * jnp.mean(dhn * xn, axis=-1, keepdims=True))
    return dx, jnp.sum(dh * xn, axis=0, keepdims=True)


def _final_loss_rows(x, g, tgt):
    r = lax.rsqrt(jnp.mean(x * x, axis=-1, keepdims=True) + EPS)
    xn = x * r
    err = xn * g - tgt
    row_loss = jnp.mean(err * err, axis=-1, keepdims=True)
    loss = 0.5 * jnp.sum(row_loss, axis=0, keepdims=True) * jnp.ones((1, LANES), F32)
    dy = err * (1.0 / D)
    dyn = dy * g
    dx = r * (dyn - xn * jnp.mean(dyn * xn, axis=-1, keepdims=True))
    return dx, jnp.sum(dy * xn, axis=0, keepdims=True), loss


def _sigmoid(z):
    return 1.0 / (1.0 + jnp.exp(-z))


def _gate_fwd(proj_g, ya, yb):
    def body(ga_ref, gb_ref, ya_ref, yb_ref, o_ref):
        o_ref[...] = (_sigmoid(ga_ref[...]) * ya_ref[...] + _sigmoid(gb_ref[...]) * yb_ref[...]).astype(BF16)

    row = pl.BlockSpec((ROW_T, D), lambda i: (i, 0))
    return pl.pallas_call(
        body, name="gate_fwd", grid=(S // ROW_T,),
        in_specs=[row, pl.BlockSpec((ROW_T, D), lambda i: (i, 1)), row, row],
        out_specs=row, out_shape=jax.ShapeDtypeStruct((S, D), BF16),
        compiler_params=_cparams(("parallel",)),
    )(proj_g, proj_g, ya, yb)


GATE_TR = 1024
GATE_TC = 512
GATE_NB = 2 * D // GATE_TC


def _gate_bwd(dmixed, proj_g, ya, yb):
    half = GATE_NB // 2

    def body(dm_ref, g_ref, ya_ref, yb_ref, dy_ref, dg_ref):
        dm = dm_ref[...]
        s = _sigmoid(g_ref[...])
        y = jnp.where(pl.program_id(1) < half, ya_ref[...], yb_ref[...])
        dy_ref[...] = (dm * s).astype(BF16)
        dg_ref[...] = (dm * y * (s * (1.0 - s))).astype(BF16)

    blk = pl.BlockSpec((GATE_TR, GATE_TC), lambda i, j: (i, j))
    wrapped = pl.BlockSpec((GATE_TR, GATE_TC), lambda i, j: (i, j % half))
    return pl.pallas_call(
        body, name="gate_bwd", grid=(S // GATE_TR, GATE_NB),
        in_specs=[wrapped, blk, wrapped, wrapped],
        out_specs=[blk, pl.BlockSpec((GATE_TR, GATE_TC), lambda i, j: (i, j + SEG_G // GATE_TC))],
        out_shape=[jax.ShapeDtypeStruct((S, 2 * D), BF16), jax.ShapeDtypeStruct((S, D_PAD), BF16)],
        compiler_params=_cparams(("parallel", "parallel")),
    )(dmixed, proj_g, ya, yb)


FOX_TQ = 256


def _scan_rows(x, reverse):
    n = x.shape[0]
    row = lax.broadcasted_iota(jnp.int32, x.shape, 0)
    k = 1
    while k < n:
        if reverse:
            x = x + jnp.where(row < n - k, pltpu.roll(x, n - k, 0), 0.0)
        else:
            x = x + jnp.where(row >= k, pltpu.roll(x, k, 0), 0.0)
        k *= 2
    return x


def _fox_dscan(dft, dfs, proj_f, b_pad, dproj):
    def body(dft_ref, dfs_ref, f_ref, b_ref, _, dfa_ref, db_ref):
        dfs_pad = jnp.concatenate([dfs_ref[...], jnp.zeros((LANES - FOX_H, S), F32)], axis=0)
        df = dfs_pad.T + dft_ref[0]
        for hp in range(1, dft_ref.shape[0]):
            df = df + pltpu.roll(dft_ref[hp], 2 * hp, 1)
        dlf = _scan_rows(df, reverse=True)
        z = f_ref[...] + b_ref[...]
        lane = lax.broadcasted_iota(jnp.int32, (S, LANES), 1)
        dfa = jnp.where(lane < FOX_H, dlf / (1.0 + jnp.exp(z)), 0.0)
        dfa_ref[:, :LANES] = dfa.astype(BF16)
        dfa_ref[:, LANES:] = jnp.zeros((S, CB - LANES), BF16)
        db_ref[...] = jnp.sum(dfa, axis=0, keepdims=True)

    return pl.pallas_call(
        body, name="fox_dscan", grid=(1,),
        in_specs=[pl.BlockSpec(dft.shape, lambda i: (0, 0, 0)), pl.BlockSpec((FOX_H, S), lambda i: (0, 0)),
                  pl.BlockSpec((S, LANES), lambda i: (0, 0)), pl.BlockSpec((1, LANES), lambda i: (0, 0)), ANY],
        out_specs=[pl.BlockSpec((S, CB), lambda i: (0, SEG_F // CB)), pl.BlockSpec((1, LANES), lambda i: (0, 0))],
        out_shape=[jax.ShapeDtypeStruct((S, D_PAD), BF16), jax.ShapeDtypeStruct((1, LANES), F32)],
        input_output_aliases={4: 0},
        compiler_params=_cparams(("arbitrary",)),
    )(dft, dfs, proj_f, b_pad, dproj)


FOX_NQ = S // FOX_TQ
FOX_HP = FOX_W // LANES
HEADS_PER_LB = LANES // HD
FOX_AUG = 2 * LANES


def _fox_prepare(proj_a, proj_f, b_pad):
    tr = 512

    def body(q_ref, k_ref, f_ref, b_ref, qa_ref, ka_ref, hi_s, mid_s, lo_s):
        i = pl.program_id(0)

        @pl.when(i == 0)
        def _():
            z = f_ref[...] + b_ref[...]
            lf = jnp.minimum(z, 0.0) - jnp.log1p(jnp.exp(-jnp.abs(z)))
            f = _scan_rows(lf, reverse=False)
            hi = f.astype(BF16).astype(F32)
            r1 = f - hi
            mid = r1.astype(BF16).astype(F32)
            hi_s[...] = hi
            mid_s[...] = mid
            lo_s[...] = (r1 - mid).astype(BF16).astype(F32)

        rows = pl.ds(pl.multiple_of(i * tr, tr), tr)
        hi, mid, lo = hi_s[rows, :], mid_s[rows, :], lo_s[rows, :]
        lane = lax.broadcasted_iota(jnp.int32, (tr, HD), 1)
        q_ones = jnp.where((lane >= 3) & (lane < 6), 1.0, 0.0)
        k_ones = jnp.where(lane < 3, 1.0, 0.0)
        for h in range(FOX_H):
            a1, a2, a3 = hi[:, h:h + 1], mid[:, h:h + 1], lo[:, h:h + 1]
            q_extra = jnp.where(lane == 0, a1, jnp.where(lane == 1, a2, jnp.where(lane == 2, a3, q_ones)))
            k_extra = jnp.where(lane == 3, -a1, jnp.where(lane == 4, -a2, jnp.where(lane == 5, -a3, k_ones)))
            base = h * LANES
            qa_ref[:, base:base + HD] = (q_ref[:, h * HD:(h + 1) * HD].astype(F32) * SCALE).astype(BF16)
            qa_ref[:, base + HD:base + LANES] = q_extra.astype(BF16)
            ka_ref[:, base:base + HD] = k_ref[:, h * HD:(h + 1) * HD]
            ka_ref[:, base + HD:base + LANES] = k_extra.astype(BF16)

    out_blk = pl.BlockSpec((tr, FOX_H * LANES), lambda i: (i, 0))
    return pl.pallas_call(
        body, name="fox_prepare", grid=(S // tr,),
        in_specs=[pl.BlockSpec((tr, FOX_W), lambda i: (i, 0)), pl.BlockSpec((tr, FOX_W), lambda i: (i, 1)),
                  pl.BlockSpec((S, LANES), lambda i: (0, 0)), pl.BlockSpec((1, LANES), lambda i: (0, 0))],
        out_specs=[out_blk, out_blk],
        out_shape=[jax.ShapeDtypeStruct((S, FOX_H * LANES), BF16)] * 2,
        scratch_shapes=[pltpu.VMEM((S, LANES), F32)] * 3,
        compiler_params=_cparams(("arbitrary",)),
    )(proj_a, proj_a, proj_f, b_pad)


def _fox_scores(qa, ka, w):
    s = _dot(qa, ka, _NT)
    row = lax.broadcasted_iota(jnp.int32, (FOX_TQ, FOX_TQ), 0)
    col = lax.broadcasted_iota(jnp.int32, (FOX_TQ, FOX_TQ), 1)
    diag = jnp.where(col <= row, s[:, w * FOX_TQ:], NEG_INF)
    return diag if w == 0 else jnp.concatenate([s[:, :w * FOX_TQ], diag], axis=1)


def _fox_fwd(q_aug, k_aug, proj_a):
    def body(qa_ref, ka_ref, v_ref, o_ref, lse_ref):
        qi = pl.program_id(1)
        lane = lax.broadcasted_iota(jnp.int32, (FOX_TQ, LANES), 1)
        for w in range(FOX_NQ):
            @pl.when(qi == w)
            def _(w=w):
                width = (w + 1) * FOX_TQ
                lse = jnp.zeros((FOX_TQ, LANES), F32)
                for hh in range(HEADS_PER_LB):
                    aug = slice(hh * LANES, (hh + 1) * LANES)
                    sl = slice(hh * HD, (hh + 1) * HD)
                    s = _fox_scores(qa_ref[:, aug], ka_ref[:width, aug], w)
                    m = jnp.max(s, axis=-1, keepdims=True)
                    p = jnp.exp(s - m)
                    l = jnp.sum(p, axis=-1, keepdims=True)
                    o_ref[:, sl] = _dot(p / l, v_ref[:width, sl], _NN)
                    lse = jnp.where(lane == hh, m + jnp.log(l), lse)
                lse_ref[...] = lse

    return pl.pallas_call(
        body, name="fox_fwd", grid=(FOX_HP, FOX_NQ),
        in_specs=[pl.BlockSpec((FOX_TQ, FOX_AUG), lambda h, i: (i, h)),
                  pl.BlockSpec((S, FOX_AUG), lambda h, i: (0, h)),
                  pl.BlockSpec((S, LANES), lambda h, i: (0, 2 * FOX_HP + h))],
        out_specs=[pl.BlockSpec((FOX_TQ, LANES), lambda h, i: (i, h)),
                   pl.BlockSpec((None, FOX_TQ, LANES), lambda h, i: (h, i, 0))],
        out_shape=[jax.ShapeDtypeStruct((S, FOX_W), F32), jax.ShapeDtypeStruct((FOX_HP, S, LANES), F32)],
        compiler_params=_cparams(("parallel", "parallel")),
    )(q_aug, k_aug, proj_a)


def _fox_bwd(q_aug, k_aug, proj_a, lse, do, dproj, deps=()):
    n_q = FOX_NQ

    def body(qa_ref, ka_ref, v_ref, lse_ref, do_ref, *refs):
        dproj_ref, dft_ref, dfs_ref, dk_acc, dv_acc, dq_buf, kv_buf, dq_sems, kv_sems = refs[1 + len(deps):]
        hp = pl.program_id(0)
        t = pl.program_id(1)
        slot = t % 2
        col = pl.multiple_of(hp * LANES, LANES)

        def dq_copy(step, buf_slot):
            rows = pl.ds(pl.multiple_of(step * FOX_TQ, FOX_TQ), FOX_TQ)
            return pltpu.make_async_copy(dq_buf.at[buf_slot], dproj_ref.at[rows, pl.ds(col, LANES)], dq_sems.at[buf_slot])

        @pl.when(t == 0)
        def _():
            dk_acc[...] = jnp.zeros_like(dk_acc)
            dv_acc[...] = jnp.zeros_like(dv_acc)

        @pl.when((t == 0) & (hp == 0))
        def _():
            dfs_ref[...] = jnp.zeros_like(dfs_ref)

        @pl.when(t >= 2)
        def _():
            dq_copy(t - 2, slot).wait()

        lane = lax.broadcasted_iota(jnp.int32, (FOX_TQ, LANES), 1)
        for w in range(n_q):
            @pl.when(t == w)
            def _(w=w):
                width = (w + 1) * FOX_TQ
                lse_all = lse_ref[...]
                sub = lax.broadcasted_iota(jnp.int32, (FOX_H, width), 0)
                dft = jnp.zeros((FOX_TQ, LANES), F32)
                for hh in range(HEADS_PER_LB):
                    aug = slice(hh * LANES, (hh + 1) * LANES)
                    sl = slice(hh * HD, (hh + 1) * HD)
                    head = hp * HEADS_PER_LB + hh
                    qa = qa_ref[:, aug]
                    ka = ka_ref[:width, aug]
                    do_h = do_ref[:, sl]
                    s = _fox_scores(qa, ka, w)
                    p = jnp.exp(s - lse_all[:, hh:hh + 1])
                    dp = _dot(do_h, v_ref[:width, sl], _NT)
                    ds = p * (dp - jnp.sum(dp * p, axis=-1, keepdims=True))
                    dft = jnp.where(lane == hh, jnp.sum(ds, axis=-1, keepdims=True), dft)
                    dfs_ref[:, :width] -= jnp.where(sub == head, jnp.sum(ds, axis=0, keepdims=True), 0.0)
                    dq_buf[slot, :, sl] = (_dot(ds, ka[:, :HD], _NN) * SCALE).astype(BF16)
                    dk_acc[:width, sl] += _dot(ds, qa[:, :HD], _TN)
                    dv_acc[:width, sl] += _dot(p, do_h, _TN)
                dft_ref[...] = dft

        dq_copy(t, slot).start()

        @pl.when(t == n_q - 1)
        def _():
            dq_copy(t - 1, 1 - slot).wait()
            dq_copy(t, slot).wait()
            kv_buf[0] = dk_acc[...].astype(BF16)
            kv_buf[1] = dv_acc[...].astype(BF16)
            copies = [pltpu.make_async_copy(
                kv_buf.at[j], dproj_ref.at[:, pl.ds(pl.multiple_of((j + 1) * FOX_W + hp * LANES, LANES), LANES)],
                kv_sems.at[j]) for j in range(2)]
            for cp in copies:
                cp.start()
            for cp in copies:
                cp.wait()

    qblk = pl.BlockSpec((FOX_TQ, FOX_AUG), lambda h, t: (t, h))
    lane_blk = pl.BlockSpec((None, FOX_TQ, LANES), lambda h, t: (h, t, 0))
    return pl.pallas_call(
        body, name="fox_bwd", grid=(FOX_HP, n_q),
        in_specs=[qblk, pl.BlockSpec((S, FOX_AUG), lambda h, t: (0, h)),
                  pl.BlockSpec((S, LANES), lambda h, t: (0, 2 * FOX_HP + h)),
                  lane_blk, pl.BlockSpec((FOX_TQ, LANES), lambda h, t: (t, h)), ANY] + [ANY] * len(deps),
        out_specs=[ANY, lane_blk, pl.BlockSpec((FOX_H, S), lambda h, t: (0, 0))],
        out_shape=[jax.ShapeDtypeStruct((S, D_PAD), BF16),
                   jax.ShapeDtypeStruct((FOX_HP, S, LANES), F32), jax.ShapeDtypeStruct((FOX_H, S), F32)],
        scratch_shapes=[pltpu.VMEM((S, LANES), F32), pltpu.VMEM((S, LANES), F32),
                        pltpu.VMEM((2, FOX_TQ, LANES), BF16), pltpu.VMEM((2, S, LANES), BF16),
                        pltpu.SemaphoreType.DMA((2,)), pltpu.SemaphoreType.DMA((2,))],
        input_output_aliases={5: 0},
        compiler_params=_cparams(("arbitrary", "arbitrary")),
    )(q_aug, k_aug, proj_a, lse, do, dproj, *deps)


def _rope_tables():
    half = ROPE_DIM // 2
    shape = (DIL_G, S, half)
    g = lax.broadcasted_iota(jnp.int32, shape, 0)
    row = lax.broadcasted_iota(jnp.int32, shape, 1)
    r = jnp.left_shift(1, 2 * g)
    per_class = S // r
    pos = (row % per_class) * r + row // per_class
    inv_freq = jnp.power(jnp.float32(ROPE_THETA), -lax.broadcasted_iota(F32, shape, 2) * 2.0 / ROPE_DIM)
    ang = pos.astype(F32) * inv_freq
    cos = jnp.tile(jnp.cos(ang), (1, 1, LANES // half))
    sin = jnp.tile(jnp.sin(ang), (1, 1, LANES // half))
    lane = lax.broadcasted_iota(jnp.int32, (DIL_G, S, LANES), 2) % HD
    t_self = jnp.where(lane < ROPE_DIM, cos, 1.0)
    t_up = jnp.where(lane < half, -sin, 0.0)
    t_dn = jnp.where((lane >= half) & (lane < ROPE_DIM), sin, 0.0)
    return t_self, t_up, t_dn


def _residue_pieces(r):
    if r == 1:
        return [(slice(i, i + 512), slice(i, i + 512)) for i in range(0, S, 512)]
    n = S // r
    return [(pl.ds(j, n, stride=r), slice(j * n, (j + 1) * n)) for j in range(r)]


def _rope_apply(x, a, u, d, backward):
    half = ROPE_DIM // 2
    if backward:
        return x * a + pltpu.roll(x * u, half, 1) + pltpu.roll(x * d, LANES - half, 1)
    return x * a + pltpu.roll(x, LANES - half, 1) * u + pltpu.roll(x, half, 1) * d


LB_PER_CB = CB // LANES
ROPE_NB = 3 * DIL_G * LB_PER_CB


def _rope_step(s):
    per_group = 3 * LB_PER_CB
    return s // per_group, (s % per_group) // LB_PER_CB, s % LB_PER_CB


def _rope_split(proj_b, tables):
    def body(x_ref, a_ref, u_ref, d_ref, o_ref):
        g, t, _ = _rope_step(pl.program_id(0))
        for gs in range(DIL_G):
            @pl.when(g == gs)
            def _(gs=gs):
                for tok, sub in _residue_pieces(DIL_R[gs]):
                    x = x_ref[tok, :]
                    y = _rope_apply(x, a_ref[sub, :], u_ref[sub, :], d_ref[sub, :], False)
                    o_ref[sub, :] = jnp.where(t < 2, y, x).astype(BF16)

    def in_index(s):
        g, t, hf = _rope_step(s)
        return 0, (t * DIL_G + g) * LB_PER_CB + hf

    def out_index(s):
        g, t, hf = _rope_step(s)
        return t * DIL_G + g, 0, hf

    tab = pl.BlockSpec((None, S, LANES), lambda s: (_rope_step(s)[0], 0, 0))
    return pl.pallas_call(
        body, name="rope_split", grid=(ROPE_NB,),
        in_specs=[pl.BlockSpec((S, LANES), in_index), tab, tab, tab],
        out_specs=pl.BlockSpec((None, S, LANES), out_index),
        out_shape=jax.ShapeDtypeStruct((3 * DIL_G, S, CB), BF16),
        compiler_params=_cparams(("parallel",)),
    )(proj_b, *tables)


def _rope_merge_bwd(dq3, dk3, dv3, tables, dproj):
    def body(dq_ref, dk_ref, dv_ref, a_ref, u_ref, d_ref, _, o_ref, tmp):
        s = pl.program_id(0)
        g, t, _ = _rope_step(s)
        for gs in range(DIL_G):
            @pl.when((g == gs) & (s < ROPE_NB))
            def _(gs=gs):
                for tok, sub in _residue_pieces(DIL_R[gs]):
                    x = jnp.where(t == 0, dq_ref[sub, :], jnp.where(t == 1, dk_ref[sub, :], dv_ref[sub, :]))
                    y = _rope_apply(x, a_ref[sub, :], u_ref[sub, :], d_ref[sub, :], True)
                    tmp[tok, :] = jnp.where(t < 2, y, x)
                o_ref[...] = tmp[...].astype(BF16)

        @pl.when(s >= ROPE_NB)
        def _():
            o_ref[...] = jnp.zeros_like(o_ref)

    def src_spec(own):
        def index(s):
            g, t, hf = _rope_step(jnp.minimum(s, ROPE_NB - 1))
            return g, 0, jnp.where(t == own, hf, jnp.where(t > own, LB_PER_CB - 1, 0))
        return pl.BlockSpec((None, S, LANES), index)

    def out_index(s):
        g, t, hf = _rope_step(s)
        col = SEG_B // LANES + (t * DIL_G + g) * LB_PER_CB + hf
        return 0, jnp.where(s < ROPE_NB, col, SEG_Z // LANES + s - ROPE_NB)

    tab = pl.BlockSpec((None, S, LANES), lambda s: (_rope_step(jnp.minimum(s, ROPE_NB - 1))[0], 0, 0))
    return pl.pallas_call(
        body, name="rope_merge_bwd", grid=(ROPE_NB + LB_PER_CB,),
        in_specs=[src_spec(0), src_spec(1), src_spec(2), tab, tab, tab, ANY],
        out_specs=pl.BlockSpec((S, LANES), out_index),
        out_shape=jax.ShapeDtypeStruct((S, D_PAD), BF16),
        scratch_shapes=[pltpu.VMEM((S, LANES), F32)],
        input_output_aliases={6: 0},
        compiler_params=_cparams(("arbitrary",)),
    )(dq3, dk3, dv3, *tables, dproj)


DIL_NB = S // DIL_BLK


_BQK = (((2,), (2,)), ((0,), (0,)))
_BQD = (((2,), (1,)), ((0,), (0,)))
_BKD = (((1,), (1,)), ((0,), (0,)))


def _dil_mask(g):
    shape = (DIL_NB, DIL_BLK, 2 * DIL_BLK)
    blocks_per_seq = jnp.right_shift(DIL_NB, 2 * g)
    has_prev = jnp.bitwise_and(lax.broadcasted_iota(jnp.int32, shape, 0), blocks_per_seq - 1) != 0
    a = lax.broadcasted_iota(jnp.int32, shape, 1)
    kk = lax.broadcasted_iota(jnp.int32, shape, 2)
    diff = DIL_BLK + a - kk
    return (diff >= 0) & (diff <= DIL_BLK) & ((kk >= DIL_BLK) | has_prev)


def _blocks(t):
    return t.reshape(DIL_NB, DIL_BLK, t.shape[-1])


def _with_prev(t):
    prev = jnp.concatenate([jnp.zeros((DIL_BLK, t.shape[-1]), t.dtype), t[:-DIL_BLK]], axis=0)
    return jnp.concatenate([_blocks(prev), _blocks(t)], axis=1)


def _fold_prev(t2):
    n = t2.shape[-1]
    to_prev = t2[:, :DIL_BLK].reshape(S, n)
    own = t2[:, DIL_BLK:].reshape(S, n)
    return own + jnp.concatenate([to_prev[DIL_BLK:], jnp.zeros((DIL_BLK, n), t2.dtype)], axis=0)


def _dil_group_spec(t, width=DIL_OUT):
    return pl.BlockSpec((None, S, width), lambda g: (t * DIL_G + g, 0, 0))


def _dil_fwd(qkv3):
    def body(q_ref, k_ref, v_ref, o_ref, lse_ref):
        ok = _dil_mask(pl.program_id(0))
        lane = lax.broadcasted_iota(jnp.int32, (S, LANES), 1)
        lse = jnp.zeros((S, LANES), F32)
        for h in range(DIL_HG):
            sl = slice(h * HD, (h + 1) * HD)
            s = jnp.where(ok, _dot(_blocks(q_ref[:, sl]), _with_prev(k_ref[:, sl]), _BQK) * SCALE, NEG_INF)
            m = jnp.max(s, axis=-1, keepdims=True)
            p = jnp.exp(s - m)
            l = jnp.sum(p, axis=-1, keepdims=True)
            o_ref[:, sl] = _dot(p / l, _with_prev(v_ref[:, sl]), _BQD).reshape(S, HD)
            lse = jnp.where(lane == h, (m + jnp.log(l)).reshape(S, 1), lse)
        lse_ref[...] = lse

    return pl.pallas_call(
        body, name="dil_fwd", grid=(DIL_G,),
        in_specs=[_dil_group_spec(0), _dil_group_spec(1), _dil_group_spec(2)],
        out_specs=[_dil_group_spec(0), _dil_group_spec(0, LANES)],
        out_shape=[jax.ShapeDtypeStruct((DIL_G, S, DIL_OUT), F32), jax.ShapeDtypeStruct((DIL_G, S, LANES), F32)],
        compiler_params=_cparams(("parallel",)),
    )(qkv3, qkv3, qkv3)


def _dil_bwd(qkv3, lse3, do3, c3):
    def body(q_ref, k_ref, v_ref, lse_ref, do_ref, c_ref, dq_ref, dk_ref, dv_ref):
        ok = _dil_mask(pl.program_id(0))
        lse = lse_ref[...]
        c = c_ref[...]
        for h in range(DIL_HG):
            sl = slice(h * HD, (h + 1) * HD)
            q = _blocks(q_ref[:, sl])
            k2 = _with_prev(k_ref[:, sl])
            do_h = _blocks(do_ref[:, sl])
            s = jnp.where(ok, _dot(q, k2, _BQK) * SCALE, NEG_INF)
            p = jnp.exp(s - _blocks(lse[:, h:h + 1]))
            dp = _dot(do_h, _with_prev(v_ref[:, sl]), _BQK)
            ds = p * (dp - _blocks(c[:, h:h + 1]))
            dsb = (ds * SCALE).astype(BF16)
            dq_ref[:, sl] = _dot(dsb, k2, _BQD).reshape(S, HD)
            dk_ref[:, sl] = _fold_prev(_dot(dsb, q, _BKD))
            dv_ref[:, sl] = _fold_prev(_dot(p, do_h, _BKD))

    return pl.pallas_call(
        body, name="dil_bwd", grid=(DIL_G,),
        in_specs=[_dil_group_spec(0), _dil_group_spec(1), _dil_group_spec(2), _dil_group_spec(0, LANES),
                  _dil_group_spec(0), _dil_group_spec(0, LANES)],
        out_specs=[_dil_group_spec(0)] * 3,
        out_shape=[jax.ShapeDtypeStruct((DIL_G, S, DIL_OUT), F32)] * 3,
        compiler_params=_cparams(("parallel",)),
    )(qkv3, qkv3, qkv3, lse3, do3, c3)


COMB_T = 256


def _dil_combine(o3, lse3):
    heads_per_lb = LANES // HD

    def body(o_ref, lse_ref, ob_ref, al_ref, *scratch):
        o_tok = [scratch[LB_PER_CB * gg:LB_PER_CB * (gg + 1)] for gg in range(DIL_G)]
        lse_tok = scratch[LB_PER_CB * DIL_G:]
        g = pl.program_id(0)
        for gs in range(DIL_G):
            @pl.when(g == gs)
            def _(gs=gs):
                for tok, sub in _residue_pieces(DIL_R[gs]):
                    for hf in range(LB_PER_CB):
                        o_tok[gs][hf][tok, :] = o_ref[sub, hf * LANES:(hf + 1) * LANES]
                    lse_tok[gs][tok, :] = lse_ref[sub, :]

        @pl.when(g == DIL_G - 1)
        def _():
            def chunk(i, carry):
                rows = pl.ds(pl.multiple_of(i * COMB_T, COMB_T), COMB_T)
                lse = [lse_tok[gg][rows, :] for gg in range(DIL_G)]
                m = jnp.maximum(jnp.maximum(lse[0], lse[1]), lse[2])
                e = [jnp.exp(lse[gg] - m) for gg in range(DIL_G)]
                den = (e[0] + e[1]) + e[2]
                al = [e[gg] / den for gg in range(DIL_G)]
                for gg in range(DIL_G):
                    al_ref[gg, rows, :] = al[gg]
                for h in range(DIL_HG):
                    hf, sl = h // heads_per_lb, slice((h % heads_per_lb) * HD, (h % heads_per_lb + 1) * HD)
                    acc = al[0][:, h:h + 1] * o_tok[0][hf][rows, sl]
                    for gg in range(1, DIL_G):
                        acc = acc + al[gg][:, h:h + 1] * o_tok[gg][hf][rows, sl]
                    ob_ref[rows, h * HD:(h + 1) * HD] = acc
                return carry

            lax.fori_loop(0, S // COMB_T, chunk, 0)

    return pl.pallas_call(
        body, name="dil_combine", grid=(DIL_G,),
        in_specs=[pl.BlockSpec((None, S, DIL_OUT), lambda g: (g, 0, 0)),
                  pl.BlockSpec((None, S, LANES), lambda g: (g, 0, 0))],
        out_specs=[pl.BlockSpec((S, DIL_OUT), lambda g: (0, 0)),
                   pl.BlockSpec((DIL_G, S, LANES), lambda g: (0, 0, 0))],
        out_shape=[jax.ShapeDtypeStruct((S, DIL_OUT), F32), jax.ShapeDtypeStruct((DIL_G, S, LANES), F32)],
        scratch_shapes=[pltpu.VMEM((S, LANES), F32)] * (DIL_G * (LB_PER_CB + 1)),
        compiler_params=_cparams(("arbitrary",)),
    )(o3, lse3)


def _dil_combine_bwd(dob, ob, alpha, deps=()):
    heads_per_lb = LANES // HD

    def body(dob_ref, ob_ref, al_ref, *refs):
        do_ref, c_ref = refs[len(deps):]
        g = pl.program_id(0)
        hf = pl.program_id(1)
        for gs in range(DIL_G):
            @pl.when(g == gs)
            def _(gs=gs):
                for tok, sub in _residue_pieces(DIL_R[gs]):
                    dob = dob_ref[tok, :]
                    prod = dob * ob_ref[tok, :]
                    al = al_ref[tok, :]
                    lane = lax.broadcasted_iota(jnp.int32, al.shape, 1)
                    c = jnp.where(hf == 0, 0.0, c_ref[sub, :])
                    for hh in range(heads_per_lb):
                        sl = slice(hh * HD, (hh + 1) * HD)
                        head = hf * heads_per_lb + hh
                        a = jnp.sum(jnp.where(lane == head, al, 0.0), axis=-1, keepdims=True)
                        do_ref[sub, sl] = (a * dob[:, sl]).astype(BF16)
                        c = jnp.where(lane == head, a * jnp.sum(prod[:, sl], axis=-1, keepdims=True), c)
                    c_ref[sub, :] = c

    half = pl.BlockSpec((S, LANES), lambda g, hf: (0, hf))
    return pl.pallas_call(
        body, name="dil_combine_bwd", grid=(DIL_G, LB_PER_CB),
        in_specs=[half, half, pl.BlockSpec((None, S, LANES), lambda g, hf: (g, 0, 0))] + [ANY] * len(deps),
        out_specs=[pl.BlockSpec((None, S, LANES), lambda g, hf: (g, 0, hf)),
                   pl.BlockSpec((None, S, LANES), lambda g, hf: (g, 0, 0))],
        out_shape=[jax.ShapeDtypeStruct((DIL_G, S, DIL_OUT), BF16), jax.ShapeDtypeStruct((DIL_G, S, LANES), F32)],
        compiler_params=_cparams(("parallel", "arbitrary")),
    )(dob, ob, alpha, *deps)


def _local_step(x, tgt, g_attn, g_mlp, g_final, b_pad, w_in_t, hooks):
    tables = _rope_tables()

    h1 = _rms_fwd(x, g_attn, "rms_attn_fwd", deps=hooks.first_deps())
    proj_a = _mm(h1, w_in_t, mode="nt", tm=S, tn=512, n=3 * FOX_W, out_dtypes=[BF16], name="proj_a")
    proj_b = _mm(h1, w_in_t, mode="nt", tm=S, tn=CB, n=3 * DIL_W, b_off=SEG_B // CB, out_dtypes=[F32], name="proj_b")
    proj_g = _mm(h1, w_in_t, mode="nt", tm=S, tn=CB, n=2 * D, b_off=SEG_G // CB, out_dtypes=[F32], name="proj_g")
    proj_f = _mm(h1, w_in_t, mode="nt", tm=S, tn=LANES, n=LANES, b_off=SEG_F // LANES, out_dtypes=[F32], name="proj_f")

    q_aug, k_aug = _fox_prepare(proj_a, proj_f, b_pad)
    oa, lse_a = _fox_fwd(q_aug, k_aug, proj_a)

    qkv3 = _rope_split(proj_b, tables)
    o3, lse3 = _dil_fwd(qkv3)
    ob, alpha = _dil_combine(o3, lse3)

    w_a, w_b, w_out = hooks.mixer_weights(ob)
    ya = _mm(oa, w_a, mode="nn", tm=S, tn=CB, out_dtypes=[F32], name="branch_a")
    yb = _mm(ob, w_b, mode="nn", tm=S, tn=CB, out_dtypes=[F32], name="branch_b")
    mixed = _gate_fwd(proj_g, ya, yb)

    def out_epilogue(acc, rows, vecs):
        x2 = rows[0] + acc
        return (x2, _rms_rows(x2, vecs[0])), ()

    x2, h2 = _mm_rows(mixed, w_out, tm=ROWS_TM, name="out_proj", epilogue=out_epilogue, rows=(x,), vecs=(g_mlp,),
                      row_out=(F32, BF16))
    w_up_sh, w_down = hooks.mlp_weights(h2)

    def up_epilogue(acc):
        r = jnp.maximum(acc, 0.0)
        return acc, r * r

    u, act = _mm(h2, w_up_sh, mode="nn", tm=S, tn=DFF // N_DEV, b_sharded=True, out_dtypes=[F32, BF16],
                 name="mlp_up", epilogue=up_epilogue)

    def loss_epilogue(acc, rows, vecs):
        dx3, dg, loss = _final_loss_rows(rows[0] + acc, vecs[0], rows[1])
        return (dx3,), (dg, loss)

    dx3, dg_final, loss = _mm_rows(act, w_down, tm=ROWS_TM, name="mlp_down_loss", epilogue=loss_epilogue,
                                   rows=(x2, tgt), vecs=(g_final,), row_out=(F32,), vec_out=(D, LANES))

    def rms_bwd_epilogue(acc, rows, vecs):
        dx, dg = _rms_bwd_rows(acc, rows[0], rows[1], vecs[0])
        return (dx,), (dg,)

    du = _mm(dx3, w_down, mode="nt", tm=S, tn=CB, out_dtypes=[BF16], name="mlp_down_bwd",
             epilogue=lambda acc, u_t: (acc * (2.0 * jnp.maximum(u_t, 0.0)),), extras=(u,))
    dw_down = _mm(act, dx3, mode="tn", tm=512, tn=512, out_dtypes=[F32], name="dw_down")
    dw_up_sh = _mm(h2, du, mode="tn", tm=D, tn=DFF // N_DEV, out_sharded=True, out_dtypes=[F32], name="dw_up")
    dx2, dg_mlp = _mm_rows(du, w_up_sh, b_sharded=True, tm=ROWS_TM, name="mlp_up_bwd", epilogue=rms_bwd_epilogue,
                           rows=(x2, dx3), vecs=(g_mlp,), row_out=(F32,), vec_out=(D,),
                           deps=hooks.mlp_grads(dw_up_sh, dw_down))

    dmixed = _mm(dx2, w_out, mode="nt", tm=S, tn=CB, out_dtypes=[F32], name="out_proj_bwd")
    dw_out = _mm(mixed, dx2, mode="tn", tm=D, tn=CB, out_dtypes=[F32], name="dw_out")

    dyab, dproj = _gate_bwd(dmixed, proj_g, ya, yb)
    doa = _mm(dyab, w_a, mode="nt", tm=S, tn=CB, k=D, a_off=0, out_dtypes=[BF16], name="branch_a_bwd")
    dw_a = _mm(oa, dyab, mode="tn", tm=FOX_W, tn=CB, n=D, b_off=0, out_dtypes=[F32], name="dw_branch_a")
    dob = _mm(dyab, w_b, mode="nt", tm=S, tn=CB, k=D, a_off=1, out_dtypes=[F32], name="branch_b_bwd")
    dw_b = _mm(ob, dyab, mode="tn", tm=DIL_OUT, tn=CB, n=D, b_off=D // CB, out_dtypes=[F32], name="dw_branch_b")

    dproj, dft, dfs = _fox_bwd(q_aug, k_aug, proj_a, lse_a, doa, dproj, deps=hooks.mixer_grads(dw_a, dw_b, dw_out))
    dproj, db = _fox_dscan(dft, dfs, proj_f, b_pad, dproj)

    do3, c3 = _dil_combine_bwd(dob, ob, alpha, deps=hooks.mid_backward(db))
    dq3, dk3, dv3 = _dil_bwd(qkv3, lse3, do3, c3)
    dproj = _rope_merge_bwd(dq3, dk3, dv3, tables, dproj)

    dw_in_t = _mm(dproj, h1, mode="tn", tm=CB, tn=D, out_dtypes=[F32], name="dw_in")
    dx, dg_attn = _mm_rows(dproj, w_in_t, tm=ROWS_TM // 2, name="proj_bwd", epilogue=rms_bwd_epilogue,
                           rows=(x, dx2), vecs=(g_attn,), row_out=(F32,), vec_out=(D,),
                           deps=hooks.w_in_grad(dw_in_t))

    return loss, dx, (dg_attn, db, dg_mlp, dg_final)


SHARD_SHAPES = ((D_IN // N_DEV, D), (FOX_W, D // N_DEV), (DIL_OUT, D // N_DEV), (D // N_DEV, D),
                (D, DFF // N_DEV), (DFF // N_DEV, D))
W_NAMES = ("w_in", "w_a", "w_b", "w_out", "w_up", "w_down")
AG_COPIES = 7
HBM = pl.BlockSpec(memory_space=pltpu.HBM)
SEM = pl.BlockSpec(memory_space=pltpu.SEMAPHORE)


def _place():
    return lax.axis_index("x"), lax.axis_index("y"), lax.axis_index("c")


W_SLAB = 752
W_WIN = 272
PAD_BLK = 256


def _pad_runs():
    runs = sorted((pad, pad + b - a, p, a) for p in range(N_DEV) for a, b, pad in _shard_pieces(p))
    blocks = []
    for k in range(D_PAD // PAD_BLK):
        lo, hi = k * PAD_BLK, (k + 1) * PAD_BLK
        blocks.append([(max(lo, r0) - lo, min(hi, r1) - lo, p, a + max(lo, r0) - r0)
                       for r0, r1, p, a in runs if max(lo, r0) < min(hi, r1)])
    return blocks


def _gather_w_in(shard):
    blocks = _pad_runs()

    def body(x_ref, out_ref, stage, land, send_sems, recv_sems, load_sem):
        x, y, c = _place()
        me, sibling = (x, y, c), (x, y, 1 - c)
        chips = [(1 - x, y), (x, 1 - y), (1 - x, 1 - y)]

        def slot(px, py, pc):
            return land.at[4 * px + 2 * py + pc]

        def copy(k, block, to):
            return pltpu.make_async_remote_copy(
                src_ref=slot(*block), dst_ref=slot(*block), send_sem=send_sems.at[k], recv_sem=recv_sems.at[k],
                device_id=to, device_id_type=MESH)

        load = pltpu.make_async_copy(x_ref, stage, load_sem)
        load.start()
        load.wait()
        land[4 * x + 2 * y + c] = stage[...].astype(BF16)
        sent = [copy(0, me, sibling)] + [copy(1 + j, me, (*chip, c)) for j, chip in enumerate(chips)]
        for cp in sent:
            cp.start()
        for j, chip in enumerate(chips):
            copy(1 + j, (*chip, c), me).wait_recv()
            sent.append(copy(4 + j, (*chip, c), sibling))
            sent[-1].start()
        copy(0, sibling, me).wait_recv()
        for j, chip in enumerate(chips):
            copy(4 + j, (*chip, 1 - c), me).wait_recv()
        for cp in sent:
            cp.wait_send()

        row = lax.broadcasted_iota(jnp.int32, (PAD_BLK, D), 0)
        for k, runs in enumerate(blocks):
            out = jnp.zeros((PAD_BLK, D), F32)
            for o_lo, o_hi, p, a in runs:
                start = min(a // 16 * 16, W_SLAB - W_WIN)
                win = land[p, start:start + W_WIN, :].astype(F32)
                moved = pltpu.roll(win, (o_lo - (a - start)) % W_WIN, 0)[:PAD_BLK]
                out = jnp.where((row >= o_lo) & (row < o_hi), moved, out)
            out_ref[k * PAD_BLK:(k + 1) * PAD_BLK, :] = out.astype(BF16)

    return pl.pallas_call(
        body, name="all_gather_w_in",
        out_shape=jax.ShapeDtypeStruct((D_PAD, D), BF16),
        in_specs=[ANY], out_specs=pl.BlockSpec(memory_space=pltpu.VMEM),
        scratch_shapes=[pltpu.VMEM((W_SLAB, D), F32), pltpu.VMEM((N_DEV, W_SLAB, D), BF16),
                        pltpu.SemaphoreType.DMA((AG_COPIES,)), pltpu.SemaphoreType.DMA((AG_COPIES,)),
                        pltpu.SemaphoreType.DMA(())],
        compiler_params=_cparams(None),
    )(shard)


def _place_own(shards):
    n_w = len(shards)
    shapes = [t.shape for t in shards]

    def body(*refs):
        in_refs, out_refs = refs[:n_w], refs[n_w:2 * n_w]
        stage, cast = refs[2 * n_w:3 * n_w], refs[3 * n_w:4 * n_w]
        load_sems, store_sems = refs[4 * n_w:]
        x, y, c = _place()
        loads = [pltpu.make_async_copy(in_refs[i], stage[i], load_sems.at[i]) for i in range(n_w)]
        for ld in loads:
            ld.start()
        stores = []
        for i in range(n_w):
            loads[i].wait()
            cast[i][...] = stage[i][...].astype(BF16)
            stores.append(pltpu.make_async_copy(cast[i], out_refs[i].at[4 * x + 2 * y + c], store_sems.at[i]))
            stores[-1].start()
        for st in stores:
            st.wait()

    return pl.pallas_call(
        body, name="place_own_shards",
        out_shape=[jax.ShapeDtypeStruct((N_DEV,) + sh, BF16) for sh in shapes],
        in_specs=[ANY] * n_w, out_specs=[ANY] * n_w,
        scratch_shapes=[pltpu.VMEM(sh, F32) for sh in shapes] + [pltpu.VMEM(sh, BF16) for sh in shapes]
                       + [pltpu.SemaphoreType.DMA((n_w,)), pltpu.SemaphoreType.DMA((n_w,))],
        compiler_params=_cparams(None),
    )(*shards)


PEER_COPIES = N_DEV - 1
SIBLING_COPIES = 4
CHIP_COPIES = 3


def _peer_copies(_, land_refs, send_sems, recv_sems):
    x, y, c = _place()
    mine = 4 * x + 2 * y + c
    copies = []
    for i, ref in enumerate(land_refs):
        for k in range(1, N_DEV):
            peer = (x ^ (k >> 2), y ^ ((k >> 1) & 1), c ^ (k & 1))
            n = i * PEER_COPIES + k - 1
            copies.append(pltpu.make_async_remote_copy(
                src_ref=ref.at[mine], dst_ref=ref.at[mine], send_sem=send_sems.at[n], recv_sem=recv_sems.at[n],
                device_id=peer, device_id_type=MESH))
    return copies


def _sibling_copies(g_refs, r_refs, send_sems, recv_sems):
    x, y, c = _place()
    return [pltpu.make_async_remote_copy(
        src_ref=g_refs[i].at[2 * k + (1 - c)], dst_ref=r_refs[i].at[k],
        send_sem=send_sems.at[SIBLING_COPIES * i + k], recv_sem=recv_sems.at[SIBLING_COPIES * i + k],
        device_id=(x, y, 1 - c), device_id_type=MESH) for i in range(len(g_refs)) for k in range(SIBLING_COPIES)]


def _chip_copies(p_refs, r_refs, send_sems, recv_sems):
    x, y, c = _place()
    chips = [(1 - x, y), (x, 1 - y), (1 - x, 1 - y)]
    return [pltpu.make_async_remote_copy(
        src_ref=p_refs[i].at[2 * cx + cy], dst_ref=r_refs[i].at[j],
        send_sem=send_sems.at[CHIP_COPIES * i + j], recv_sem=recv_sems.at[CHIP_COPIES * i + j],
        device_id=(cx, cy, c), device_id_type=MESH) for i in range(len(p_refs)) for j, (cx, cy) in enumerate(chips)]


def _in_hbm(a):
    return pltpu.with_memory_space_constraint(a, pltpu.HBM)


def _exchange_start(name, copies_fn, n_copies, srcs, lands, after=()):
    n_s, n_l, n_a = len(srcs), len(lands), len(after)

    def body(*refs):
        outs = refs[n_s + n_l + n_a:]
        for cp in copies_fn(refs[:n_s], refs[n_s:n_s + n_l], outs[0], outs[1]):
            cp.start()
        outs[-1][...] = jnp.zeros_like(outs[-1])

    res = pl.pallas_call(
        body, name=name,
        out_shape=[pltpu.SemaphoreType.DMA((n_copies,)), pltpu.SemaphoreType.DMA((n_copies,))]
                  + [pltpu.HBM(t.shape, t.dtype) for t in (*srcs, *lands)] + [jax.ShapeDtypeStruct((8, LANES), F32)],
        in_specs=[HBM] * (n_s + n_l) + [ANY] * n_a,
        out_specs=[SEM, SEM] + [HBM] * (n_s + n_l) + [pl.BlockSpec(memory_space=pltpu.VMEM)],
        input_output_aliases={i: 2 + i for i in range(n_s + n_l)},
        compiler_params=pltpu.CompilerParams(has_side_effects=pltpu.SideEffectType.DATAFLOW_SIDE_EFFECTING),
    )(*[_in_hbm(t) for t in (*srcs, *lands)], *after)
    return res[0], res[1], list(res[2:2 + n_s]), list(res[2 + n_s:2 + n_s + n_l]), res[-1]


def _exchange_wait(name, copies_fn, started, after):
    send_sems, recv_sems, srcs, lands, _ = started
    n_s, n_l = len(srcs), len(lands)

    def body(*refs):
        for cp in copies_fn(refs[:n_s], refs[n_s:n_s + n_l], refs[n_s + n_l], refs[n_s + n_l + 1]):
            cp.wait_send()
            cp.wait_recv()

    res = pl.pallas_call(
        body, name=name,
        out_shape=[pltpu.HBM(t.shape, t.dtype) for t in (*srcs, *lands)],
        in_specs=[HBM] * (n_s + n_l) + [SEM, SEM, ANY],
        out_specs=[HBM] * (n_s + n_l),
        input_output_aliases={i: i for i in range(n_s + n_l)},
        compiler_params=pltpu.CompilerParams(has_side_effects=pltpu.SideEffectType.DATAFLOW_SIDE_EFFECTING),
    )(*srcs, *lands, send_sems, recv_sems, after)
    return list(res[:n_s]), list(res[n_s:])


def _shard_block(shape):
    rows, cols = shape
    if rows % 256:
        return rows, 256
    return min(rows, 512 if cols <= LANES else 256), cols


def _chip_partial(ids, g_stack, recv1, name):
    shape = g_stack.shape[1:]
    br, bc = _shard_block(shape)

    def body(ids_ref, g_ref, r_ref, pb_ref, own_ref):
        s = g_ref[...] + r_ref[...]
        pb_ref[...] = s.astype(BF16)

        @pl.when(pl.program_id(2) == ids_ref[1])
        def _():
            own_ref[...] = s

    grid_spec = pltpu.PrefetchScalarGridSpec(
        num_scalar_prefetch=1, grid=(shape[0] // br, shape[1] // bc, 4),
        in_specs=[pl.BlockSpec((None, br, bc), lambda r, q, k, ids: (2 * k + ids[0], r, q)),
                  pl.BlockSpec((None, br, bc), lambda r, q, k, ids: (k, r, q))],
        out_specs=[pl.BlockSpec((None, br, bc), lambda r, q, k, ids: (k, r, q)),
                   pl.BlockSpec((br, bc), lambda r, q, k, ids: (r, q))])
    return pl.pallas_call(
        body, name=name, grid_spec=grid_spec,
        out_shape=[jax.ShapeDtypeStruct((4,) + shape, BF16), jax.ShapeDtypeStruct(shape, F32)],
        compiler_params=_cparams(("parallel", "parallel", "arbitrary")),
    )(ids, g_stack, recv1)


def _adamw(w, g, m, v):
    m = ADAM_B1 * m + (1.0 - ADAM_B1) * g
    v = ADAM_B2 * v + (1.0 - ADAM_B2) * (g * g)
    m_hat = m / (1.0 - ADAM_B1 ** ADAM_STEP)
    v_hat = v / (1.0 - ADAM_B2 ** ADAM_STEP)
    delta = -ADAM_LR * (m_hat / (jnp.sqrt(v_hat) + ADAM_EPS) + ADAM_WD * w)
    return delta, m, v


def _reduce_adamw(own, recv2, w, m, v, name, deps=()):
    shape = own.shape
    br, bc = _shard_block(shape)

    def body(own_ref, r_ref, w_ref, m_ref, v_ref, *refs):
        g_ref, d_ref, nm_ref, nv_ref = refs[len(deps):]
        g = own_ref[...]
        for j in range(3):
            g = g + r_ref[j].astype(F32)
        delta, nm, nv = _adamw(w_ref[...], g, m_ref[...], v_ref[...])
        g_ref[...] = g
        d_ref[...] = delta
        nm_ref[...] = nm
        nv_ref[...] = nv

    blk = pl.BlockSpec((br, bc), lambda r, q: (r, q))
    return pl.pallas_call(
        body, name=name, grid=(shape[0] // br, shape[1] // bc),
        in_specs=[blk, pl.BlockSpec((3, br, bc), lambda r, q: (0, r, q)), blk, blk, blk] + [ANY] * len(deps),
        out_specs=[blk] * 4, out_shape=[jax.ShapeDtypeStruct(shape, F32)] * 4,
        compiler_params=_cparams(("parallel", "parallel")),
    )(own, recv2, w, m, v, *deps)


def _small_allreduce_adamw(gvec, w, m, v, deps=()):
    def body(g_ref, w_ref, m_ref, v_ref, *refs):
        go_ref, d_ref, nm_ref, nv_ref, buf, send_sems, recv_sems = refs[len(deps):]
        x, y, c = _place()
        my_slot = 4 * x + 2 * y + c
        buf[my_slot] = g_ref[...]
        copies = []
        for k in range(1, N_DEV):
            px, py, pc = x ^ (k >> 2), y ^ ((k >> 1) & 1), c ^ (k & 1)
            copies.append(pltpu.make_async_remote_copy(
                src_ref=g_ref, dst_ref=buf.at[my_slot], send_sem=send_sems.at[k - 1], recv_sem=recv_sems.at[k - 1],
                device_id=(px, py, pc), device_id_type=MESH))
        for cp in copies:
            cp.start()
        for k in range(1, N_DEV):
            px, py, pc = x ^ (k >> 2), y ^ ((k >> 1) & 1), c ^ (k & 1)
            pltpu.make_async_remote_copy(
                src_ref=g_ref, dst_ref=buf.at[4 * px + 2 * py + pc], send_sem=send_sems.at[k - 1],
                recv_sem=recv_sems.at[k - 1], device_id=(px, py, pc), device_id_type=MESH).wait_recv()
        for cp in copies:
            cp.wait_send()
        g = buf[0]
        for s in range(1, N_DEV):
            g = g + buf[s]
        delta, nm, nv = _adamw(w_ref[...], g, m_ref[...], v_ref[...])
        go_ref[...] = g
        d_ref[...] = delta
        nm_ref[...] = nm
        nv_ref[...] = nv

    vm = pl.BlockSpec(memory_space=pltpu.VMEM)
    return pl.pallas_call(
        body, name="small_allreduce_adamw",
        in_specs=[vm] * 4 + [ANY] * len(deps), out_specs=[vm] * 4,
        out_shape=[jax.ShapeDtypeStruct((SMALL_R, LANES), F32)] * 4,
        scratch_shapes=[pltpu.VMEM((N_DEV, SMALL_R, LANES), F32),
                        pltpu.SemaphoreType.DMA((N_DEV - 1,)), pltpu.SemaphoreType.DMA((N_DEV - 1,))],
    )(gvec, w, m, v, *deps)


def _cols_to_whole(stack):
    rows = stack.shape[1]
    return stack.transpose(1, 0, 2).reshape(rows, -1)


def _whole_to_cols(t):
    rows = t.shape[0]
    return t.reshape(rows, N_DEV, -1).transpose(1, 0, 2)


W_IN_SEGMENTS = ((0, 3 * FOX_W, SEG_A), (3 * FOX_W, 3 * FOX_W + FOX_H, SEG_F),
                 (3 * FOX_W + FOX_H, 3 * FOX_W + FOX_H + 3 * DIL_W, SEG_B), (3 * FOX_W + FOX_H + 3 * DIL_W, D_IN, SEG_G))


def _shard_pieces(p):
    rows = D_IN // N_DEV
    lo, hi = p * rows, (p + 1) * rows
    return [(max(lo, a) - lo, min(hi, b) - lo, pad + max(lo, a) - a)
            for a, b, pad in W_IN_SEGMENTS if max(lo, a) < min(hi, b)]


def _unpad_dw_in_t(dwp):
    return jnp.stack([jnp.concatenate([dwp[pad:pad + b - a] for a, b, pad in _shard_pieces(p)], axis=0)
                      for p in range(N_DEV)])


LOSS_ROW = 25


def _pack_small(g_attn, b, g_mlp, g_final, loss_row=None):
    tail = jnp.pad(b, ((0, 7), (0, LANES - b.shape[1])))
    if loss_row is not None:
        tail = tail + jnp.pad(loss_row, ((LOSS_ROW - 24, 31 - LOSS_ROW), (0, 0)))
    return jnp.concatenate([g_attn.reshape(8, LANES), g_mlp.reshape(8, LANES), g_final.reshape(8, LANES), tail], axis=0)


def _unpack_small(p):
    return (p[0:8].reshape(1, D), p[24:25, :FOX_H], p[8:16].reshape(1, D), p[16:24].reshape(D))


class _StepComm:
    def __init__(self, ids, w_sh, m_sh, v_sh, after_w_in):
        self.ids = ids
        self.w_sh, self.m_sh, self.v_sh = w_sh, m_sh, v_sh
        self.updates = None
        lands = _place_own(w_sh[1:])
        self.gather_mixer = _exchange_start("gather_mixer_start", _peer_copies, PEER_COPIES * 3, [], lands[:3],
                                            after=(after_w_in,))
        self.gather_mlp = _exchange_start("gather_mlp_start", _peer_copies, PEER_COPIES * 2, [], lands[3:],
                                          after=(self.gather_mixer[-1],))
        self.sibling = {}
        self.chips = {}
        self.own = {}
        self.names = {}

    def _reduce_start(self, group, names, grads, after=()):
        lands = [lax.empty((SIBLING_COPIES,) + t.shape[1:], F32) for t in grads]
        self.sibling[group] = _exchange_start("grad_%s_sibling_start" % group, _sibling_copies,
                                              SIBLING_COPIES * len(grads), grads, lands, after=after)
        self.names[group] = names
        return self.sibling[group][-1]

    def _reduce_mid(self, group, after):
        grads, recv1 = _exchange_wait("grad_%s_sibling_wait" % group, _sibling_copies, self.sibling[group], after)
        parts = [_chip_partial(self.ids, g, r, "grad_partial_" + n) for g, r, n in zip(grads, recv1, self.names[group])]
        self.own[group] = [p[1] for p in parts]
        srcs = [p[0] for p in parts]
        lands = [lax.empty((CHIP_COPIES,) + t.shape[1:], BF16) for t in srcs]
        self.chips[group] = _exchange_start("grad_%s_chips_start" % group, _chip_copies, CHIP_COPIES * len(srcs),
                                            srcs, lands)
        return self.chips[group][-1]

    def reduced(self, group, after):
        _, recv2 = _exchange_wait("grad_%s_chips_wait" % group, _chip_copies, self.chips[group], after)
        return list(zip(self.own[group], recv2))

    def first_deps(self):
        return [self.gather_mlp[-1]]

    def mixer_weights(self, after):
        _, (g_a, g_b, g_out) = _exchange_wait("gather_mixer_wait", _peer_copies, self.gather_mixer, after)
        return _cols_to_whole(g_a), _cols_to_whole(g_b), g_out.reshape(D, D)

    def mlp_weights(self, after):
        _, (g_up, g_down) = _exchange_wait("gather_mlp_wait", _peer_copies, self.gather_mlp, after)
        return g_up, g_down.reshape(DFF, D)

    def mlp_grads(self, dw_up_sh, dw_down):
        return [self._reduce_start("mlp", W_NAMES[4:], [dw_up_sh, dw_down.reshape((N_DEV,) + SHARD_SHAPES[5])])]

    def mixer_grads(self, dw_a, dw_b, dw_out):
        token = self._reduce_mid("mlp", dw_b)
        grads = [_whole_to_cols(dw_a), _whole_to_cols(dw_b), dw_out.reshape((N_DEV,) + SHARD_SHAPES[3])]
        return [self._reduce_start("mixer", W_NAMES[1:4], grads, after=(token,))]

    def mid_backward(self, after):
        return [self._reduce_mid("mixer", after)]

    def w_in_grad(self, dw_in_t):
        grads = [_unpad_dw_in_t(dw_in_t)]
        token = self._reduce_start("w_in", W_NAMES[:1], grads)
        reduced = self.reduced("mixer", token) + self.reduced("mlp", token)
        self.updates, last = [], token
        for i, r in enumerate(reduced):
            self.updates.append(_reduce_adamw(*r, self.w_sh[1 + i], self.m_sh[1 + i], self.v_sh[1 + i],
                                              "adamw_" + W_NAMES[1 + i], deps=[last]))
            last = self.updates[-1][0]
        return [self._reduce_mid("w_in", last)]

    def w_in_update(self, after):
        (reduced,) = self.reduced("w_in", after)
        return _reduce_adamw(*reduced, self.w_sh[0], self.m_sh[0], self.v_sh[0], "adamw_" + W_NAMES[0])


def kernel(x, norm_attn_g, w_in, b_forget, w_branch_a, w_branch_b, w_out, norm_mlp_g, w_up, w_down, norm_final_g, loss_target, m_norm_attn_g, m_w_in, m_b_forget, m_w_branch_a, m_w_branch_b, m_w_out, m_norm_mlp_g, m_w_up, m_w_down, m_norm_final_g, v_norm_attn_g, v_w_in, v_b_forget, v_w_branch_a, v_w_branch_b, v_w_out, v_norm_mlp_g, v_w_up, v_w_down, v_norm_final_g):
    cx, cy, cc = _place()
    ids = jnp.stack([cc, 2 * cx + cy]).astype(jnp.int32)

    w_sh = [w_in[0].T] + [t[0] for t in (w_branch_a, w_branch_b, w_out, w_up, w_down)]
    m_sh = [m_w_in[0].T] + [t[0] for t in (m_w_branch_a, m_w_branch_b, m_w_out, m_w_up, m_w_down)]
    v_sh = [v_w_in[0].T] + [t[0] for t in (v_w_branch_a, v_w_branch_b, v_w_out, v_w_up, v_w_down)]

    w_in_t = _gather_w_in(jnp.pad(w_sh[0], ((0, W_SLAB - w_sh[0].shape[0]), (0, 0))))
    comm = _StepComm(ids, w_sh, m_sh, v_sh, w_in_t)
    b_pad = jnp.pad(b_forget, ((0, 0), (0, LANES - FOX_H)))

    loss_row, dx, dsmall = _local_step(
        x[0], loss_target[0], norm_attn_g, norm_mlp_g, norm_final_g.reshape(1, D), b_pad, w_in_t, comm)

    dg_attn, db, dg_mlp, dg_final = dsmall
    small = _small_allreduce_adamw(
        _pack_small(dg_attn, db, dg_mlp, dg_final, loss_row),
        _pack_small(norm_attn_g, b_forget, norm_mlp_g, norm_final_g.reshape(1, D)),
        _pack_small(m_norm_attn_g, m_b_forget, m_norm_mlp_g, m_norm_final_g.reshape(1, D)),
        _pack_small(v_norm_attn_g, v_b_forget, v_norm_mlp_g, v_norm_final_g.reshape(1, D)))
    big = [comm.w_in_update(small[0])] + comm.updates

    outs = [small[0][LOSS_ROW, 0], dx[None]]
    for q in range(4):
        s_attn, s_b, s_mlp, s_final = _unpack_small(small[q])
        b_in, b_a, b_b, b_out, b_up, b_down = [t[q][None] for t in big]
        b_in = jnp.swapaxes(b_in, 1, 2)
        outs += [s_attn, b_in, s_b, b_a, b_b, b_out, s_mlp, b_up, b_down, s_final]
    return tuple(outs)
```

```python
import functools

import jax
import jax.numpy as jnp
from jax import lax
from jax.experimental import pallas as pl
from jax.experimental.pallas import tpu as pltpu

F32 = jnp.float32
BF16 = jnp.bfloat16
MESH = pl.DeviceIdType.MESH

S = 2048
D = 1024
HD = 64
FOX_H = 8
FOX_W = FOX_H * HD
DIL_HG = 4
DIL_G = 3
DIL_W = DIL_G * DIL_HG * HD
DIL_OUT = DIL_HG * HD
DIL_BLK = 128
DIL_R = (1, 4, 16)
DFF = 4 * D
D_IN = 3 * FOX_W + FOX_H + 3 * DIL_W + 2 * D
EPS = 1e-6
NEG_INF = -1e30
SCALE = HD ** -0.5
ROPE_THETA = 500000.0
ROPE_DIM = HD // 4
N_DEV = 8

ADAM_LR = 0.001
ADAM_B1 = 0.9
ADAM_B2 = 0.999
ADAM_EPS = 1e-08
ADAM_WD = 0.01
ADAM_STEP = 10

LANES = 128
CB = 256
SEG_A = 0
SEG_B = 3 * FOX_W
SEG_Z = SEG_B + 3 * DIL_W
SEG_G = SEG_Z + CB
SEG_F = SEG_G + 2 * D
D_PAD = SEG_F + CB
SMALL_R = 32

VMEM_MB = 56


def _cparams(dims=None, vmem_mb=VMEM_MB, **kw):
    return pltpu.CompilerParams(dimension_semantics=dims, vmem_limit_bytes=vmem_mb << 20, **kw)


ANY = pl.BlockSpec(memory_space=pl.ANY)

_NN = (((1,), (0,)), ((), ()))
_NT = (((1,), (1,)), ((), ()))
_TN = (((0,), (0,)), ((), ()))


def _dot(a, b, dims):
    return lax.dot_general(a.astype(BF16), b.astype(BF16), dims, preferred_element_type=F32)


def _mm(a, b, *, mode, tm, tn, out_dtypes, name, n=None, k=None, a_off=0, b_off=0,
        b_sharded=False, out_sharded=False, epilogue=None, extras=(), deps=()):
    n_sh = b.shape[-1] if b_sharded else None
    if mode == "nn":
        m = a.shape[0]
        k = k or a.shape[1]
        a_spec = pl.BlockSpec((tm, k), lambda i, j: (i, a_off))
        if b_sharded:
            assert tn == n_sh
            n = N_DEV * n_sh
            b_spec = pl.BlockSpec((None, k, tn), lambda i, j: (j, 0, 0))
        else:
            n = n or b.shape[1]
            b_spec = pl.BlockSpec((k, tn), lambda i, j: (0, j + b_off))
        dims = _NN
    elif mode == "nt":
        m = a.shape[0]
        k = k or a.shape[1]
        a_spec = pl.BlockSpec((tm, k), lambda i, j: (i, a_off))
        if b_sharded:
            n = b.shape[1]
            b_spec = pl.BlockSpec((N_DEV, tn, n_sh), lambda i, j: (0, j, 0))
        else:
            n = n or b.shape[0]
            b_spec = pl.BlockSpec((tn, k), lambda i, j: (j + b_off, 0))
        dims = _NT
    else:
        k, m = a.shape
        n = n or b.shape[1]
        a_spec = pl.BlockSpec((k, tm), lambda i, j: (0, i))
        b_spec = pl.BlockSpec((k, tn), lambda i, j: (0, j + b_off))
        dims = _TN
    assert m % tm == 0 and n % tn == 0, (name, m, n, tm, tn)
    n_extra = len(extras)
    tile = pl.BlockSpec((tm, tn), lambda i, j: (i, j))
    if out_sharded:
        assert mode == "tn" and n // tn == N_DEV
        out_spec = pl.BlockSpec((None, tm, tn), lambda i, j: (j, i, 0))
        out_shape = (N_DEV, m, tn)
    else:
        out_spec, out_shape = tile, (m, n)

    def body(a_ref, b_ref, *refs):
        if mode == "nt" and b_sharded:
            acc = _dot(a_ref[:, 0:n_sh], b_ref[0], dims)
            for p in range(1, N_DEV):
                acc = acc + _dot(a_ref[:, p * n_sh:(p + 1) * n_sh], b_ref[p], dims)
        else:
            acc = _dot(a_ref[...], b_ref[...], dims)
        ex = [r[...] for r in refs[:n_extra]]
        outs = epilogue(acc, *ex) if epilogue is not None else (acc,)
        for o_ref, o in zip(refs[n_extra + len(deps):], outs):
            o_ref[...] = o.astype(o_ref.dtype)

    res = pl.pallas_call(
        body, name=name, grid=(m // tm, n // tn),
        in_specs=[a_spec, b_spec] + [tile] * n_extra + [ANY] * len(deps),
        out_specs=[out_spec] * len(out_dtypes),
        out_shape=[jax.ShapeDtypeStruct(out_shape, dt) for dt in out_dtypes],
        compiler_params=_cparams(("parallel", "parallel")),
    )(a, b, *extras, *deps)
    return res if len(out_dtypes) > 1 else res[0]


def _mm_rows(a, b, *, tm, name, epilogue, rows=(), vecs=(), row_out=(), vec_out=(), b_sharded=False, deps=()):
    m, k = a.shape
    n_rows, n_vecs, n_deps = len(rows), len(vecs), len(deps)
    n_sh = b.shape[-1] if b_sharded else None

    def body(a_ref, b_ref, *refs):
        row_refs, vec_refs = refs[:n_rows], refs[n_rows:n_rows + n_vecs]
        outs = refs[n_rows + n_vecs + n_deps:]
        if b_sharded:
            acc = _dot(a_ref[:, 0:n_sh], b_ref[0], _NT)
            for p in range(1, N_DEV):
                acc = acc + _dot(a_ref[:, p * n_sh:(p + 1) * n_sh], b_ref[p], _NT)
        else:
            acc = _dot(a_ref[...], b_ref[...], _NN)
        row_vals, vec_incs = epilogue(acc, [r[...] for r in row_refs], [v[...] for v in vec_refs])
        for o_ref, val in zip(outs[:len(row_out)], row_vals):
            o_ref[...] = val.astype(o_ref.dtype)

        @pl.when(pl.program_id(0) == 0)
        def _():
            for o_ref in outs[len(row_out):]:
                o_ref[...] = jnp.zeros_like(o_ref)

        for o_ref, inc in zip(outs[len(row_out):], vec_incs):
            o_ref[...] += inc

    tile = pl.BlockSpec((tm, D), lambda i: (i, 0))
    b_spec = pl.BlockSpec(b.shape, lambda i: (0,) * b.ndim)
    return pl.pallas_call(
        body, name=name, grid=(m // tm,),
        in_specs=[pl.BlockSpec((tm, k), lambda i: (i, 0)), b_spec] + [tile] * n_rows
                 + [pl.BlockSpec(v.shape, lambda i: (0, 0)) for v in vecs] + [ANY] * n_deps,
        out_specs=[tile] * len(row_out) + [pl.BlockSpec((1, w), lambda i: (0, 0)) for w in vec_out],
        out_shape=[jax.ShapeDtypeStruct((m, D), dt) for dt in row_out]
                  + [jax.ShapeDtypeStruct((1, w), F32) for w in vec_out],
        compiler_params=_cparams(("arbitrary",)),
    )(a, b, *rows, *vecs, *deps)


ROW_T = 256
ROWS_TM = 512


def _rms_rows(x, g):
    r = lax.rsqrt(jnp.mean(x * x, axis=-1, keepdims=True) + EPS)
    return (x * r) * g


def _rms_fwd(x, g, name, deps=()):
    def body(x_ref, g_ref, *refs):
        refs[-1][...] = _rms_rows(x_ref[...], g_ref[...]).astype(BF16)

    return pl.pallas_call(
        body, name=name, grid=(S // ROW_T,),
        in_specs=[pl.BlockSpec((ROW_T, D), lambda i: (i, 0)), pl.BlockSpec((1, D), lambda i: (0, 0))] + [ANY] * len(deps),
        out_specs=pl.BlockSpec((ROW_T, D), lambda i: (i, 0)),
        out_shape=jax.ShapeDtypeStruct((S, D), BF16),
        compiler_params=_cparams(("parallel",)),
    )(x, g, *deps)


def _rms_bwd_rows(dh, x, dres, g):
    r = lax.rsqrt(jnp.mean(x * x, axis=-1, keepdims=True) + EPS)
    xn = x * r
    dhn = dh * g
    dx = dres + r * (dhn - xn * jnp.mean(dhn * xn, axis=-1, keepdims=True))
    return dx, jnp.sum(dh * xn, axis=0, keepdims=True)


def _final_loss_rows(x, g, tgt):
    r = lax.rsqrt(jnp.mean(x * x, axis=-1, keepdims=True) + EPS)
    xn = x * r
    err = xn * g - tgt
    row_loss = jnp.mean(err * err, axis=-1, keepdims=True)
    loss = 0.5 * jnp.sum(row_loss, axis=0, keepdims=True) * jnp.ones((1, LANES), F32)
    dy = err * (1.0 / D)
    dyn = dy * g
    dx = r * (dyn - xn * jnp.mean(dyn * xn, axis=-1, keepdims=True))
    return dx, jnp.sum(dy * xn, axis=0, keepdims=True), loss


def _sigmoid(z):
    return 1.0 / (1.0 + jnp.exp(-z))


def _gate_fwd(proj_g, ya, yb):
    def body(ga_ref, gb_ref, ya_ref, yb_ref, o_ref):
        o_ref[...] = (_sigmoid(ga_ref[...]) * ya_ref[...] + _sigmoid(gb_ref[...]) * yb_ref[...]).astype(BF16)

    row = pl.BlockSpec((ROW_T, D), lambda i: (i, 0))
    return pl.pallas_call(
        body, name="gate_fwd", grid=(S // ROW_T,),
        in_specs=[row, pl.BlockSpec((ROW_T, D), lambda i: (i, 1)), row, row],
        out_specs=row, out_shape=jax.ShapeDtypeStruct((S, D), BF16),
        compiler_params=_cparams(("parallel",)),
    )(proj_g, proj_g, ya, yb)


GATE_TR = 1024
GATE_TC = 512
GATE_NB = 2 * D // GATE_TC


def _gate_bwd(dmixed, proj_g, ya, yb):
    half = GATE_NB // 2

    def body(dm_ref, g_ref, ya_ref, yb_ref, dy_ref, dg_ref):
        dm = dm_ref[...]
        s = _sigmoid(g_ref[...])
        y = jnp.where(pl.program_id(1) < half, ya_ref[...], yb_ref[...])
        dy_ref[...] = (dm * s).astype(BF16)
        dg_ref[...] = (dm * y * (s * (1.0 - s))).astype(BF16)

    blk = pl.BlockSpec((GATE_TR, GATE_TC), lambda i, j: (i, j))
    wrapped = pl.BlockSpec((GATE_TR, GATE_TC), lambda i, j: (i, j % half))
    return pl.pallas_call(
        body, name="gate_bwd", grid=(S // GATE_TR, GATE_NB),
        in_specs=[wrapped, blk, wrapped, wrapped],
        out_specs=[blk, pl.BlockSpec((GATE_TR, GATE_TC), lambda i, j: (i, j + SEG_G // GATE_TC))],
        out_shape=[jax.ShapeDtypeStruct((S, 2 * D), BF16), jax.ShapeDtypeStruct((S, D_PAD), BF16)],
        compiler_params=_cparams(("parallel", "parallel")),
    )(dmixed, proj_g, ya, yb)


FOX_TQ = 256


def _scan_rows(x, reverse):
    n = x.shape[0]
    row = lax.broadcasted_iota(jnp.int32, x.shape, 0)
    k = 1
    while k < n:
        if reverse:
            x = x + jnp.where(row < n - k, pltpu.roll(x, n - k, 0), 0.0)
        else:
            x = x + jnp.where(row >= k, pltpu.roll(x, k, 0), 0.0)
        k *= 2
    return x


def _fox_dscan(dft, dfs, proj_f, b_pad, dproj):
    def body(dft_ref, dfs_ref, f_ref, b_ref, _, dfa_ref, db_ref):
        dfs_pad = jnp.concatenate([dfs_ref[...], jnp.zeros((LANES - FOX_H, S), F32)], axis=0)
        df = dfs_pad.T + dft_ref[0]
        for hp in range(1, dft_ref.shape[0]):
            df = df + pltpu.roll(dft_ref[hp], 2 * hp, 1)
        dlf = _scan_rows(df, reverse=True)
        z = f_ref[...] + b_ref[...]
        lane = lax.broadcasted_iota(jnp.int32, (S, LANES), 1)
        dfa = jnp.where(lane < FOX_H, dlf / (1.0 + jnp.exp(z)), 0.0)
        dfa_ref[:, :LANES] = dfa.astype(BF16)
        dfa_ref[:, LANES:] = jnp.zeros((S, CB - LANES), BF16)
        db_ref[...] = jnp.sum(dfa, axis=0, keepdims=True)

    return pl.pallas_call(
        body, name="fox_dscan", grid=(1,),
        in_specs=[pl.BlockSpec(dft.shape, lambda i: (0, 0, 0)), pl.BlockSpec((FOX_H, S), lambda i: (0, 0)),
                  pl.BlockSpec((S, LANES), lambda i: (0, 0)), pl.BlockSpec((1, LANES), lambda i: (0, 0)), ANY],
        out_specs=[pl.BlockSpec((S, CB), lambda i: (0, SEG_F // CB)), pl.BlockSpec((1, LANES), lambda i: (0, 0))],
        out_shape=[jax.ShapeDtypeStruct((S, D_PAD), BF16), jax.ShapeDtypeStruct((1, LANES), F32)],
        input_output_aliases={4: 0},
        compiler_params=_cparams(("arbitrary",)),
    )(dft, dfs, proj_f, b_pad, dproj)


FOX_NQ = S // FOX_TQ
FOX_HP = FOX_W // LANES
HEADS_PER_LB = LANES // HD
FOX_AUG = 2 * LANES


def _fox_prepare(proj_a, proj_f, b_pad):
    tr = 512

    def body(q_ref, k_ref, f_ref, b_ref, qa_ref, ka_ref, hi_s, mid_s, lo_s):
        i = pl.program_id(0)

        @pl.when(i == 0)
        def _():
            z = f_ref[...] + b_ref[...]
            lf = jnp.minimum(z, 0.0) - jnp.log1p(jnp.exp(-jnp.abs(z)))
            f = _scan_rows(lf, reverse=False)
            hi = f.astype(BF16).astype(F32)
            r1 = f - hi
            mid = r1.astype(BF16).astype(F32)
            hi_s[...] = hi
            mid_s[...] = mid
            lo_s[...] = (r1 - mid).astype(BF16).astype(F32)

        rows = pl.ds(pl.multiple_of(i * tr, tr), tr)
        hi, mid, lo = hi_s[rows, :], mid_s[rows, :], lo_s[rows, :]
        lane = lax.broadcasted_iota(jnp.int32, (tr, HD), 1)
        q_ones = jnp.where((lane >= 3) & (lane < 6), 1.0, 0.0)
        k_ones = jnp.where(lane < 3, 1.0, 0.0)
        for h in range(FOX_H):
            a1, a2, a3 = hi[:, h:h + 1], mid[:, h:h + 1], lo[:, h:h + 1]
            q_extra = jnp.where(lane == 0, a1, jnp.where(lane == 1, a2, jnp.where(lane == 2, a3, q_ones)))
            k_extra = jnp.where(lane == 3, -a1, jnp.where(lane == 4, -a2, jnp.where(lane == 5, -a3, k_ones)))
            base = h * LANES
            qa_ref[:, base:base + HD] = (q_ref[:, h * HD:(h + 1) * HD].astype(F32) * SCALE).astype(BF16)
            qa_ref[:, base + HD:base + LANES] = q_extra.astype(BF16)
            ka_ref[:, base:base + HD] = k_ref[:, h * HD:(h + 1) * HD]
            ka_ref[:, base + HD:base + LANES] = k_extra.astype(BF16)

    out_blk = pl.BlockSpec((tr, FOX_H * LANES), lambda i: (i, 0))
    return pl.pallas_call(
        body, name="fox_prepare", grid=(S // tr,),
        in_specs=[pl.BlockSpec((tr, FOX_W), lambda i: (i, 0)), pl.BlockSpec((tr, FOX_W), lambda i: (i, 1)),
                  pl.BlockSpec((S, LANES), lambda i: (0, 0)), pl.BlockSpec((1, LANES), lambda i: (0, 0))],
        out_specs=[out_blk, out_blk],
        out_shape=[jax.ShapeDtypeStruct((S, FOX_H * LANES), BF16)] * 2,
        scratch_shapes=[pltpu.VMEM((S, LANES), F32)] * 3,
        compiler_params=_cparams(("arbitrary",)),
    )(proj_a, proj_a, proj_f, b_pad)


def _fox_scores(qa, ka, w):
    s = _dot(qa, ka, _NT)
    row = lax.broadcasted_iota(jnp.int32, (FOX_TQ, FOX_TQ), 0)
    col = lax.broadcasted_iota(jnp.int32, (FOX_TQ, FOX_TQ), 1)
    diag = jnp.where(col <= row, s[:, w * FOX_TQ:], NEG_INF)
    return diag if w == 0 else jnp.concatenate([s[:, :w * FOX_TQ], diag], axis=1)


def _fox_fwd(q_aug, k_aug, proj_a):
    def body(qa_ref, ka_ref, v_ref, o_ref, lse_ref):
        qi = pl.program_id(1)
        lane = lax.broadcasted_iota(jnp.int32, (FOX_TQ, LANES), 1)
        for w in range(FOX_NQ):
            @pl.when(qi == w)
            def _(w=w):
                width = (w + 1) * FOX_TQ
                lse = jnp.zeros((FOX_TQ, LANES), F32)
                for hh in range(HEADS_PER_LB):
                    aug = slice(hh * LANES, (hh + 1) * LANES)
                    sl = slice(hh * HD, (hh + 1) * HD)
                    s = _fox_scores(qa_ref[:, aug], ka_ref[:width, aug], w)
                    m = jnp.max(s, axis=-1, keepdims=True)
                    p = jnp.exp(s - m)
                    l = jnp.sum(p, axis=-1, keepdims=True)
                    o_ref[:, sl] = _dot(p / l, v_ref[:width, sl], _NN)
                    lse = jnp.where(lane == hh, m + jnp.log(l), lse)
                lse_ref[...] = lse

    return pl.pallas_call(
        body, name="fox_fwd", grid=(FOX_HP, FOX_NQ),
        in_specs=[pl.BlockSpec((FOX_TQ, FOX_AUG), lambda h, i: (i, h)),
                  pl.BlockSpec((S, FOX_AUG), lambda h, i: (0, h)),
                  pl.BlockSpec((S, LANES), lambda h, i: (0, 2 * FOX_HP + h))],
        out_specs=[pl.BlockSpec((FOX_TQ, LANES), lambda h, i: (i, h)),
                   pl.BlockSpec((None, FOX_TQ, LANES), lambda h, i: (h, i, 0))],
        out_shape=[jax.ShapeDtypeStruct((S, FOX_W), F32), jax.ShapeDtypeStruct((FOX_HP, S, LANES), F32)],
        compiler_params=_cparams(("parallel", "parallel")),
    )(q_aug, k_aug, proj_a)


def _fox_bwd(q_aug, k_aug, proj_a, lse, do, dproj, deps=()):
    n_q = FOX_NQ

    def body(qa_ref, ka_ref, v_ref, lse_ref, do_ref, *refs):
        dproj_ref, dft_ref, dfs_ref, dk_acc, dv_acc, dq_buf, kv_buf, dq_sems, kv_sems = refs[1 + len(deps):]
        hp = pl.program_id(0)
        t = pl.program_id(1)
        slot = t % 2
        col = pl.multiple_of(hp * LANES, LANES)

        def dq_copy(step, buf_slot):
            rows = pl.ds(pl.multiple_of(step * FOX_TQ, FOX_TQ), FOX_TQ)
            return pltpu.make_async_copy(dq_buf.at[buf_slot], dproj_ref.at[rows, pl.ds(col, LANES)], dq_sems.at[buf_slot])

        @pl.when(t == 0)
        def _():
            dk_acc[...] = jnp.zeros_like(dk_acc)
            dv_acc[...] = jnp.zeros_like(dv_acc)

        @pl.when((t == 0) & (hp == 0))
        def _():
            dfs_ref[...] = jnp.zeros_like(dfs_ref)

        @pl.when(t >= 2)
        def _():
            dq_copy(t - 2, slot).wait()

        lane = lax.broadcasted_iota(jnp.int32, (FOX_TQ, LANES), 1)
        for w in range(n_q):
            @pl.when(t == w)
            def _(w=w):
                width = (w + 1) * FOX_TQ
                lse_all = lse_ref[...]
                sub = lax.broadcasted_iota(jnp.int32, (FOX_H, width), 0)
                dft = jnp.zeros((FOX_TQ, LANES), F32)
                for hh in range(HEADS_PER_LB):
                    aug = slice(hh * LANES, (hh + 1) * LANES)
                    sl = slice(hh * HD, (hh + 1) * HD)
                    head = hp * HEADS_PER_LB + hh
                    qa = qa_ref[:, aug]
                    ka = ka_ref[:width, aug]
                    do_h = do_ref[:, sl]
                    s = _fox_scores(qa, ka, w)
                    p = jnp.exp(s - lse_all[:, hh:hh + 1])
                    dp = _dot(do_h, v_ref[:width, sl], _NT)
                    ds = p * (dp - jnp.sum(dp * p, axis=-1, keepdims=True))
                    dft = jnp.where(lane == hh, jnp.sum(ds, axis=-1, keepdims=True), dft)
                    dfs_ref[:, :width] -= jnp.where(sub == head, jnp.sum(ds, axis=0, keepdims=True), 0.0)
                    dq_buf[slot, :, sl] = (_dot(ds, ka[:, :HD], _NN) * SCALE).astype(BF16)
                    dk_acc[:width, sl] += _dot(ds, qa[:, :HD], _TN)
                    dv_acc[:width, sl] += _dot(p, do_h, _TN)
                dft_ref[...] = dft

        dq_copy(t, slot).start()

        @pl.when(t == n_q - 1)
        def _():
            dq_copy(t - 1, 1 - slot).wait()
            dq_copy(t, slot).wait()
            kv_buf[0] = dk_acc[...].astype(BF16)
            kv_buf[1] = dv_acc[...].astype(BF16)
            copies = [pltpu.make_async_copy(
                kv_buf.at[j], dproj_ref.at[:, pl.ds(pl.multiple_of((j + 1) * FOX_W + hp * LANES, LANES), LANES)],
                kv_sems.at[j]) for j in range(2)]
            for cp in copies:
                cp.start()
            for cp in copies:
                cp.wait()

    qblk = pl.BlockSpec((FOX_TQ, FOX_AUG), lambda h, t: (t, h))
    lane_blk = pl.BlockSpec((None, FOX_TQ, LANES), lambda h, t: (h, t, 0))
    return pl.pallas_call(
        body, name="fox_bwd", grid=(FOX_HP, n_q),
        in_specs=[qblk, pl.BlockSpec((S, FOX_AUG), lambda h, t: (0, h)),
                  pl.BlockSpec((S, LANES), lambda h, t: (0, 2 * FOX_HP + h)),
                  lane_blk, pl.BlockSpec((FOX_TQ, LANES), lambda h, t: (t, h)), ANY] + [ANY] * len(deps),
        out_specs=[ANY, lane_blk, pl.BlockSpec((FOX_H, S), lambda h, t: (0, 0))],
        out_shape=[jax.ShapeDtypeStruct((S, D_PAD), BF16),
                   jax.ShapeDtypeStruct((FOX_HP, S, LANES), F32), jax.ShapeDtypeStruct((FOX_H, S), F32)],
        scratch_shapes=[pltpu.VMEM((S, LANES), F32), pltpu.VMEM((S, LANES), F32),
                        pltpu.VMEM((2, FOX_TQ, LANES), BF16), pltpu.VMEM((2, S, LANES), BF16),
                        pltpu.SemaphoreType.DMA((2,)), pltpu.SemaphoreType.DMA((2,))],
        input_output_aliases={5: 0},
        compiler_params=_cparams(("arbitrary", "arbitrary")),
    )(q_aug, k_aug, proj_a, lse, do, dproj, *deps)


def _rope_tables():
    half = ROPE_DIM // 2
    shape = (DIL_G, S, half)
    g = lax.broadcasted_iota(jnp.int32, shape, 0)
    row = lax.broadcasted_iota(jnp.int32, shape, 1)
    r = jnp.left_shift(1, 2 * g)
    per_class = S // r
    pos = (row % per_class) * r + row // per_class
    inv_freq = jnp.power(jnp.float32(ROPE_THETA), -lax.broadcasted_iota(F32, shape, 2) * 2.0 / ROPE_DIM)
    ang = pos.astype(F32) * inv_freq
    cos = jnp.tile(jnp.cos(ang), (1, 1, LANES // half))
    sin = jnp.tile(jnp.sin(ang), (1, 1, LANES // half))
    lane = lax.broadcasted_iota(jnp.int32, (DIL_G, S, LANES), 2) % HD
    t_self = jnp.where(lane < ROPE_DIM, cos, 1.0)
    t_up = jnp.where(lane < half, -sin, 0.0)
    t_dn = jnp.where((lane >= half) & (lane < ROPE_DIM), sin, 0.0)
    return t_self, t_up, t_dn


def _residue_pieces(r):
    if r == 1:
        return [(slice(i, i + 512), slice(i, i + 512)) for i in range(0, S, 512)]
    n = S // r
    return [(pl.ds(j, n, stride=r), slice(j * n, (j + 1) * n)) for j in range(r)]


def _rope_apply(x, a, u, d, backward):
    half = ROPE_DIM // 2
    if backward:
        return x * a + pltpu.roll(x * u, half, 1) + pltpu.roll(x * d, LANES - half, 1)
    return x * a + pltpu.roll(x, LANES - half, 1) * u + pltpu.roll(x, half, 1) * d


LB_PER_CB = CB // LANES
ROPE_NB = 3 * DIL_G * LB_PER_CB


def _rope_step(s):
    per_group = 3 * LB_PER_CB
    return s // per_group, (s % per_group) // LB_PER_CB, s % LB_PER_CB


def _rope_split(proj_b, tables):
    def body(x_ref, a_ref, u_ref, d_ref, o_ref):
        g, t, _ = _rope_step(pl.program_id(0))
        for gs in range(DIL_G):
            @pl.when(g == gs)
            def _(gs=gs):
                for tok, sub in _residue_pieces(DIL_R[gs]):
                    x = x_ref[tok, :]
                    y = _rope_apply(x, a_ref[sub, :], u_ref[sub, :], d_ref[sub, :], False)
                    o_ref[sub, :] = jnp.where(t < 2, y, x).astype(BF16)

    def in_index(s):
        g, t, hf = _rope_step(s)
        return 0, (t * DIL_G + g) * LB_PER_CB + hf

    def out_index(s):
        g, t, hf = _rope_step(s)
        return t * DIL_G + g, 0, hf

    tab = pl.BlockSpec((None, S, LANES), lambda s: (_rope_step(s)[0], 0, 0))
    return pl.pallas_call(
        body, name="rope_split", grid=(ROPE_NB,),
        in_specs=[pl.BlockSpec((S, LANES), in_index), tab, tab, tab],
        out_specs=pl.BlockSpec((None, S, LANES), out_index),
        out_shape=jax.ShapeDtypeStruct((3 * DIL_G, S, CB), BF16),
        compiler_params=_cparams(("parallel",)),
    )(proj_b, *tables)


def _rope_merge_bwd(dq3, dk3, dv3, tables, dproj):
    def body(dq_ref, dk_ref, dv_ref, a_ref, u_ref, d_ref, _, o_ref, tmp):
        s = pl.program_id(0)
        g, t, _ = _rope_step(s)
        for gs in range(DIL_G):
            @pl.when((g == gs) & (s < ROPE_NB))
            def _(gs=gs):
                for tok, sub in _residue_pieces(DIL_R[gs]):
                    x = jnp.where(t == 0, dq_ref[sub, :], jnp.where(t == 1, dk_ref[sub, :], dv_ref[sub, :]))
                    y = _rope_apply(x, a_ref[sub, :], u_ref[sub, :], d_ref[sub, :], True)
                    tmp[tok, :] = jnp.where(t < 2, y, x)
                o_ref[...] = tmp[...].astype(BF16)

        @pl.when(s >= ROPE_NB)
        def _():
            o_ref[...] = jnp.zeros_like(o_ref)

    def src_spec(own):
        def index(s):
            g, t, hf = _rope_step(jnp.minimum(s, ROPE_NB - 1))
            return g, 0, jnp.where(t == own, hf, jnp.where(t > own, LB_PER_CB - 1, 0))
        return pl.BlockSpec((None, S, LANES), index)

    def out_index(s):
        g, t, hf = _rope_step(s)
        col = SEG_B // LANES + (t * DIL_G + g) * LB_PER_CB + hf
        return 0, jnp.where(s < ROPE_NB, col, SEG_Z // LANES + s - ROPE_NB)

    tab = pl.BlockSpec((None, S, LANES), lambda s: (_rope_step(jnp.minimum(s, ROPE_NB - 1))[0], 0, 0))
    return pl.pallas_call(
        body, name="rope_merge_bwd", grid=(ROPE_NB + LB_PER_CB,),
        in_specs=[src_spec(0), src_spec(1), src_spec(2), tab, tab, tab, ANY],
        out_specs=pl.BlockSpec((S, LANES), out_index),
        out_shape=jax.ShapeDtypeStruct((S, D_PAD), BF16),
        scratch_shapes=[pltpu.VMEM((S, LANES), F32)],
        input_output_aliases={6: 0},
        compiler_params=_cparams(("arbitrary",)),
    )(dq3, dk3, dv3, *tables, dproj)


DIL_NB = S // DIL_BLK


_BQK = (((2,), (2,)), ((0,), (0,)))
_BQD = (((2,), (1,)), ((0,), (0,)))
_BKD = (((1,), (1,)), ((0,), (0,)))


def _dil_mask(g):
    shape = (DIL_NB, DIL_BLK, 2 * DIL_BLK)
    blocks_per_seq = jnp.right_shift(DIL_NB, 2 * g)
    has_prev = jnp.bitwise_and(lax.broadcasted_iota(jnp.int32, shape, 0), blocks_per_seq - 1) != 0
    a = lax.broadcasted_iota(jnp.int32, shape, 1)
    kk = lax.broadcasted_iota(jnp.int32, shape, 2)
    diff = DIL_BLK + a - kk
    return (diff >= 0) & (diff <= DIL_BLK) & ((kk >= DIL_BLK) | has_prev)


def _blocks(t):
    return t.reshape(DIL_NB, DIL_BLK, t.shape[-1])


def _with_prev(t):
    prev = jnp.concatenate([jnp.zeros((DIL_BLK, t.shape[-1]), t.dtype), t[:-DIL_BLK]], axis=0)
    return jnp.concatenate([_blocks(prev), _blocks(t)], axis=1)


def _fold_prev(t2):
    n = t2.shape[-1]
    to_prev = t2[:, :DIL_BLK].reshape(S, n)
    own = t2[:, DIL_BLK:].reshape(S, n)
    return own + jnp.concatenate([to_prev[DIL_BLK:], jnp.zeros((DIL_BLK, n), t2.dtype)], axis=0)


def _dil_group_spec(t, width=DIL_OUT):
    return pl.BlockSpec((None, S, width), lambda g: (t * DIL_G + g, 0, 0))


def _dil_fwd(qkv3):
    def body(q_ref, k_ref, v_ref, o_ref, lse_ref):
        ok = _dil_mask(pl.program_id(0))
        lane = lax.broadcasted_iota(jnp.int32, (S, LANES), 1)
        lse = jnp.zeros((S, LANES), F32)
        for h in range(DIL_HG):
            sl = slice(h * HD, (h + 1) * HD)
            s = jnp.where(ok, _dot(_blocks(q_ref[:, sl]), _with_prev(k_ref[:, sl]), _BQK) * SCALE, NEG_INF)
            m = jnp.max(s, axis=-1, keepdims=True)
            p = jnp.exp(s - m)
            l = jnp.sum(p, axis=-1, keepdims=True)
            o_ref[:, sl] = _dot(p / l, _with_prev(v_ref[:, sl]), _BQD).reshape(S, HD)
            lse = jnp.where(lane == h, (m + jnp.log(l)).reshape(S, 1), lse)
        lse_ref[...] = lse

    return pl.pallas_call(
        body, name="dil_fwd", grid=(DIL_G,),
        in_specs=[_dil_group_spec(0), _dil_group_spec(1), _dil_group_spec(2)],
        out_specs=[_dil_group_spec(0), _dil_group_spec(0, LANES)],
        out_shape=[jax.ShapeDtypeStruct((DIL_G, S, DIL_OUT), F32), jax.ShapeDtypeStruct((DIL_G, S, LANES), F32)],
        compiler_params=_cparams(("parallel",)),
    )(qkv3, qkv3, qkv3)


def _dil_bwd(qkv3, lse3, do3, c3):
    def body(q_ref, k_ref, v_ref, lse_ref, do_ref, c_ref, dq_ref, dk_ref, dv_ref):
        ok = _dil_mask(pl.program_id(0))
        lse = lse_ref[...]
        c = c_ref[...]
        for h in range(DIL_HG):
            sl = slice(h * HD, (h + 1) * HD)
            q = _blocks(q_ref[:, sl])
            k2 = _with_prev(k_ref[:, sl])
            do_h = _blocks(do_ref[:, sl])
            s = jnp.where(ok, _dot(q, k2, _BQK) * SCALE, NEG_INF)
            p = jnp.exp(s - _blocks(lse[:, h:h + 1]))
            dp = _dot(do_h, _with_prev(v_ref[:, sl]), _BQK)
            ds = p * (dp - _blocks(c[:, h:h + 1]))
            dsb = (ds * SCALE).astype(BF16)
            dq_ref[:, sl] = _dot(dsb, k2, _BQD).reshape(S, HD)
            dk_ref[:, sl] = _fold_prev(_dot(dsb, q, _BKD))
            dv_ref[:, sl] = _fold_prev(_dot(p, do_h, _BKD))

    return pl.pallas_call(
        body, name="dil_bwd", grid=(DIL_G,),
        in_specs=[_dil_group_spec(0), _dil_group_spec(1), _dil_group_spec(2), _dil_group_spec(0, LANES),
                  _dil_group_spec(0), _dil_group_spec(0, LANES)],
        out_specs=[_dil_group_spec(0)] * 3,
        out_shape=[jax.ShapeDtypeStruct((DIL_G, S, DIL_OUT), F32)] * 3,
        compiler_params=_cparams(("parallel",)),
    )(qkv3, qkv3, qkv3, lse3, do3, c3)


COMB_T = 256


def _dil_combine(o3, lse3):
    heads_per_lb = LANES // HD

    def body(o_ref, lse_ref, ob_ref, al_ref, *scratch):
        o_tok = [scratch[LB_PER_CB * gg:LB_PER_CB * (gg + 1)] for gg in range(DIL_G)]
        lse_tok = scratch[LB_PER_CB * DIL_G:]
        g = pl.program_id(0)
        for gs in range(DIL_G):
            @pl.when(g == gs)
            def _(gs=gs):
                for tok, sub in _residue_pieces(DIL_R[gs]):
                    for hf in range(LB_PER_CB):
                        o_tok[gs][hf][tok, :] = o_ref[sub, hf * LANES:(hf + 1) * LANES]
                    lse_tok[gs][tok, :] = lse_ref[sub, :]

        @pl.when(g == DIL_G - 1)
        def _():
            def chunk(i, carry):
                rows = pl.ds(pl.multiple_of(i * COMB_T, COMB_T), COMB_T)
                lse = [lse_tok[gg][rows, :] for gg in range(DIL_G)]
                m = jnp.maximum(jnp.maximum(lse[0], lse[1]), lse[2])
                e = [jnp.exp(lse[gg] - m) for gg in range(DIL_G)]
                den = (e[0] + e[1]) + e[2]
                al = [e[gg] / den for gg in range(DIL_G)]
                for gg in range(DIL_G):
                    al_ref[gg, rows, :] = al[gg]
                for h in range(DIL_HG):
                    hf, sl = h // heads_per_lb, slice((h % heads_per_lb) * HD, (h % heads_per_lb + 1) * HD)
                    acc = al[0][:, h:h + 1] * o_tok[0][hf][rows, sl]
                    for gg in range(1, DIL_G):
                        acc = acc + al[gg][:, h:h + 1] * o_tok[gg][hf][rows, sl]
                    ob_ref[rows, h * HD:(h + 1) * HD] = acc
                return carry

            lax.fori_loop(0, S // COMB_T, chunk, 0)

    return pl.pallas_call(
        body, name="dil_combine", grid=(DIL_G,),
        in_specs=[pl.BlockSpec((None, S, DIL_OUT), lambda g: (g, 0, 0)),
                  pl.BlockSpec((None, S, LANES), lambda g: (g, 0, 0))],
        out_specs=[pl.BlockSpec((S, DIL_OUT), lambda g: (0, 0)),
                   pl.BlockSpec((DIL_G, S, LANES), lambda g: (0, 0, 0))],
        out_shape=[jax.ShapeDtypeStruct((S, DIL_OUT), F32), jax.ShapeDtypeStruct((DIL_G, S, LANES), F32)],
        scratch_shapes=[pltpu.VMEM((S, LANES), F32)] * (DIL_G * (LB_PER_CB + 1)),
        compiler_params=_cparams(("arbitrary",)),
    )(o3, lse3)


def _dil_combine_bwd(dob, ob, alpha, deps=()):
    heads_per_lb = LANES // HD

    def body(dob_ref, ob_ref, al_ref, *refs):
        do_ref, c_ref = refs[len(deps):]
        g = pl.program_id(0)
        hf = pl.program_id(1)
        for gs in range(DIL_G):
            @pl.when(g == gs)
            def _(gs=gs):
                for tok, sub in _residue_pieces(DIL_R[gs]):
                    dob = dob_ref[tok, :]
                    prod = dob * ob_ref[tok, :]
                    al = al_ref[tok, :]
                    lane = lax.broadcasted_iota(jnp.int32, al.shape, 1)
                    c = jnp.where(hf == 0, 0.0, c_ref[sub, :])
                    for hh in range(heads_per_lb):
                        sl = slice(hh * HD, (hh + 1) * HD)
                        head = hf * heads_per_lb + hh
                        a = jnp.sum(jnp.where(lane == head, al, 0.0), axis=-1, keepdims=True)
                        do_ref[sub, sl] = (a * dob[:, sl]).astype(BF16)
                        c = jnp.where(lane == head, a * jnp.sum(prod[:, sl], axis=-1, keepdims=True), c)
                    c_ref[sub, :] = c

    half = pl.BlockSpec((S, LANES), lambda g, hf: (0, hf))
    return pl.pallas_call(
        body, name="dil_combine_bwd", grid=(DIL_G, LB_PER_CB),
        in_specs=[half, half, pl.BlockSpec((None, S, LANES), lambda g, hf: (g, 0, 0))] + [ANY] * len(deps),
        out_specs=[pl.BlockSpec((None, S, LANES), lambda g, hf: (g, 0, hf)),
                   pl.BlockSpec((None, S, LANES), lambda g, hf: (g, 0, 0))],
        out_shape=[jax.ShapeDtypeStruct((DIL_G, S, DIL_OUT), BF16), jax.ShapeDtypeStruct((DIL_G, S, LANES), F32)],
        compiler_params=_cparams(("parallel", "arbitrary")),
    )(dob, ob, alpha, *deps)


def _local_step(x, tgt, g_attn, g_mlp, g_final, b_pad, w_in_t, hooks):
    tables = _rope_tables()

    h1 = _rms_fwd(x, g_attn, "rms_attn_fwd", deps=hooks.first_deps())
    proj_a = _mm(h1, w_in_t, mode="nt", tm=S, tn=512, n=3 * FOX_W, out_dtypes=[BF16], name="proj_a")
    proj_b = _mm(h1, w_in_t, mode="nt", tm=S, tn=CB, n=3 * DIL_W, b_off=SEG_B // CB, out_dtypes=[F32], name="proj_b")
    proj_g = _mm(h1, w_in_t, mode="nt", tm=S, tn=CB, n=2 * D, b_off=SEG_G // CB, out_dtypes=[F32], name="proj_g")
    proj_f = _mm(h1, w_in_t, mode="nt", tm=S, tn=LANES, n=LANES, b_off=SEG_F // LANES, out_dtypes=[F32], name="proj_f")

    q_aug, k_aug = _fox_prepare(proj_a, proj_f, b_pad)
    oa, lse_a = _fox_fwd(q_aug, k_aug, proj_a)

    qkv3 = _rope_split(proj_b, tables)
    o3, lse3 = _dil_fwd(qkv3)
    ob, alpha = _dil_combine(o3, lse3)

    w_a, w_b, w_out = hooks.mixer_weights(ob)
    ya = _mm(oa, w_a, mode="nn", tm=S, tn=CB, out_dtypes=[F32], name="branch_a")
    yb = _mm(ob, w_b, mode="nn", tm=S, tn=CB, out_dtypes=[F32], name="branch_b")
    mixed = _gate_fwd(proj_g, ya, yb)

    def out_epilogue(acc, rows, vecs):
        x2 = rows[0] + acc
        return (x2, _rms_rows(x2, vecs[0])), ()

    x2, h2 = _mm_rows(mixed, w_out, tm=ROWS_TM, name="out_proj", epilogue=out_epilogue, rows=(x,), vecs=(g_mlp,),
                      row_out=(F32, BF16))
    w_up_sh, w_down = hooks.mlp_weights(h2)

    def up_epilogue(acc):
        r = jnp.maximum(acc, 0.0)
        return acc, r * r

    u, act = _mm(h2, w_up_sh, mode="nn", tm=S, tn=DFF // N_DEV, b_sharded=True, out_dtypes=[F32, BF16],
                 name="mlp_up", epilogue=up_epilogue)

    def loss_epilogue(acc, rows, vecs):
        dx3, dg, loss = _final_loss_rows(rows[0] + acc, vecs[0], rows[1])
        return (dx3,), (dg, loss)

    dx3, dg_final, loss = _mm_rows(act, w_down, tm=ROWS_TM, name="mlp_down_loss", epilogue=loss_epilogue,
                                   rows=(x2, tgt), vecs=(g_final,), row_out=(F32,), vec_out=(D, LANES))

    def rms_bwd_epilogue(acc, rows, vecs):
        dx, dg = _rms_bwd_rows(acc, rows[0], rows[1], vecs[0])
        return (dx,), (dg,)

    du = _mm(dx3, w_down, mode="nt", tm=S, tn=CB, out_dtypes=[BF16], name="mlp_down_bwd",
             epilogue=lambda acc, u_t: (acc * (2.0 * jnp.maximum(u_t, 0.0)),), extras=(u,))
    dw_down = _mm(act, dx3, mode="tn", tm=512, tn=D, out_dtypes=[F32], name="dw_down")
    dw_up_sh = _mm(h2, du, mode="tn", tm=D, tn=DFF // N_DEV, out_sharded=True, out_dtypes=[F32], name="dw_up")
    dx2, dg_mlp = _mm_rows(du, w_up_sh, b_sharded=True, tm=ROWS_TM, name="mlp_up_bwd", epilogue=rms_bwd_epilogue,
                           rows=(x2, dx3), vecs=(g_mlp,), row_out=(F32,), vec_out=(D,),
                           deps=hooks.mlp_grads(dw_up_sh, dw_down))

    dmixed = _mm(dx2, w_out, mode="nt", tm=S, tn=CB, out_dtypes=[F32], name="out_proj_bwd")
    dw_out = _mm(mixed, dx2, mode="tn", tm=D, tn=D, out_dtypes=[F32], name="dw_out")

    dyab, dproj = _gate_bwd(dmixed, proj_g, ya, yb)
    doa = _mm(dyab, w_a, mode="nt", tm=S, tn=CB, k=D, a_off=0, out_dtypes=[BF16], name="branch_a_bwd")
    dw_a = _mm(oa, dyab, mode="tn", tm=FOX_W, tn=CB, n=D, b_off=0, out_dtypes=[F32], name="dw_branch_a")
    dob = _mm(dyab, w_b, mode="nt", tm=S, tn=CB, k=D, a_off=1, out_dtypes=[F32], name="branch_b_bwd")
    dw_b = _mm(ob, dyab, mode="tn", tm=DIL_OUT, tn=CB, n=D, b_off=D // CB, out_dtypes=[F32], name="dw_branch_b")

    dproj, dft, dfs = _fox_bwd(q_aug, k_aug, proj_a, lse_a, doa, dproj, deps=hooks.mixer_grads(dw_a, dw_b, dw_out))
    dproj, db = _fox_dscan(dft, dfs, proj_f, b_pad, dproj)

    do3, c3 = _dil_combine_bwd(dob, ob, alpha, deps=hooks.mid_backward(db))
    dq3, dk3, dv3 = _dil_bwd(qkv3, lse3, do3, c3)
    dproj = _rope_merge_bwd(dq3, dk3, dv3, tables, dproj)

    dw_in_t = _mm(dproj, h1, mode="tn", tm=CB, tn=D, out_dtypes=[F32], name="dw_in")
    dx, dg_attn = _mm_rows(dproj, w_in_t, tm=ROWS_TM // 2, name="proj_bwd", epilogue=rms_bwd_epilogue,
                           rows=(x, dx2), vecs=(g_attn,), row_out=(F32,), vec_out=(D,),
                           deps=hooks.w_in_grad(dw_in_t))

    return loss, dx, (dg_attn, db, dg_mlp, dg_final)


SHARD_SHAPES = ((D_IN // N_DEV, D), (FOX_W, D // N_DEV), (DIL_OUT, D // N_DEV), (D // N_DEV, D),
                (D, DFF // N_DEV), (DFF // N_DEV, D))
W_NAMES = ("w_in", "w_a", "w_b", "w_out", "w_up", "w_down")
AG_COPIES = 7
HBM = pl.BlockSpec(memory_space=pltpu.HBM)
SEM = pl.BlockSpec(memory_space=pltpu.SEMAPHORE)


def _place():
    return lax.axis_index("x"), lax.axis_index("y"), lax.axis_index("c")


W_SLAB = 752
W_WIN = 272
PAD_BLK = 256


def _pad_runs():
    runs = sorted((pad, pad + b - a, p, a) for p in range(N_DEV) for a, b, pad in _shard_pieces(p))
    blocks = []
    for k in range(D_PAD // PAD_BLK):
        lo, hi = k * PAD_BLK, (k + 1) * PAD_BLK
        blocks.append([(max(lo, r0) - lo, min(hi, r1) - lo, p, a + max(lo, r0) - r0)
                       for r0, r1, p, a in runs if max(lo, r0) < min(hi, r1)])
    return blocks


def _gather_w_in(shard):
    blocks = _pad_runs()

    def body(x_ref, out_ref, stage, land, send_sems, recv_sems, load_sem):
        x, y, c = _place()
        me, sibling = (x, y, c), (x, y, 1 - c)
        chips = [(1 - x, y), (x, 1 - y), (1 - x, 1 - y)]

        def slot(px, py, pc):
            return land.at[4 * px + 2 * py + pc]

        def copy(k, block, to):
            return pltpu.make_async_remote_copy(
                src_ref=slot(*block), dst_ref=slot(*block), send_sem=send_sems.at[k], recv_sem=recv_sems.at[k],
                device_id=to, device_id_type=MESH)

        load = pltpu.make_async_copy(x_ref, stage, load_sem)
        load.start()
        load.wait()
        land[4 * x + 2 * y + c] = stage[...].astype(BF16)
        sent = [copy(0, me, sibling)] + [copy(1 + j, me, (*chip, c)) for j, chip in enumerate(chips)]
        for cp in sent:
            cp.start()
        for j, chip in enumerate(chips):
            copy(1 + j, (*chip, c), me).wait_recv()
            sent.append(copy(4 + j, (*chip, c), sibling))
            sent[-1].start()
        copy(0, sibling, me).wait_recv()
        for j, chip in enumerate(chips):
            copy(4 + j, (*chip, 1 - c), me).wait_recv()
        for cp in sent:
            cp.wait_send()

        row = lax.broadcasted_iota(jnp.int32, (PAD_BLK, D), 0)
        for k, runs in enumerate(blocks):
            out = jnp.zeros((PAD_BLK, D), F32)
            for o_lo, o_hi, p, a in runs:
                start = min(a // 16 * 16, W_SLAB - W_WIN)
                win = land[p, start:start + W_WIN, :].astype(F32)
                moved = pltpu.roll(win, (o_lo - (a - start)) % W_WIN, 0)[:PAD_BLK]
                out = jnp.where((row >= o_lo) & (row < o_hi), moved, out)
            out_ref[k * PAD_BLK:(k + 1) * PAD_BLK, :] = out.astype(BF16)

    return pl.pallas_call(
        body, name="all_gather_w_in",
        out_shape=jax.ShapeDtypeStruct((D_PAD, D), BF16),
        in_specs=[ANY], out_specs=pl.BlockSpec(memory_space=pltpu.VMEM),
        scratch_shapes=[pltpu.VMEM((W_SLAB, D), F32), pltpu.VMEM((N_DEV, W_SLAB, D), BF16),
                        pltpu.SemaphoreType.DMA((AG_COPIES,)), pltpu.SemaphoreType.DMA((AG_COPIES,)),
                        pltpu.SemaphoreType.DMA(())],
        compiler_params=_cparams(None),
    )(shard)


def _place_own(shards):
    n_w = len(shards)
    shapes = [t.shape for t in shards]

    def body(*refs):
        in_refs, out_refs = refs[:n_w], refs[n_w:2 * n_w]
        stage, cast = refs[2 * n_w:3 * n_w], refs[3 * n_w:4 * n_w]
        load_sems, store_sems = refs[4 * n_w:]
        x, y, c = _place()
        loads = [pltpu.make_async_copy(in_refs[i], stage[i], load_sems.at[i]) for i in range(n_w)]
        for ld in loads:
            ld.start()
        stores = []
        for i in range(n_w):
            loads[i].wait()
            cast[i][...] = stage[i][...].astype(BF16)
            stores.append(pltpu.make_async_copy(cast[i], out_refs[i].at[4 * x + 2 * y + c], store_sems.at[i]))
            stores[-1].start()
        for st in stores:
            st.wait()

    return pl.pallas_call(
        body, name="place_own_shards",
        out_shape=[jax.ShapeDtypeStruct((N_DEV,) + sh, BF16) for sh in shapes],
        in_specs=[ANY] * n_w, out_specs=[ANY] * n_w,
        scratch_shapes=[pltpu.VMEM(sh, F32) for sh in shapes] + [pltpu.VMEM(sh, BF16) for sh in shapes]
                       + [pltpu.SemaphoreType.DMA((n_w,)), pltpu.SemaphoreType.DMA((n_w,))],
        compiler_params=_cparams(None),
    )(*shards)


PEER_COPIES = N_DEV - 1
SIBLING_COPIES = 4
CHIP_COPIES = 3


def _peer_copies(_, land_refs, send_sems, recv_sems):
    x, y, c = _place()
    mine = 4 * x + 2 * y + c
    copies = []
    for i, ref in enumerate(land_refs):
        for k in range(1, N_DEV):
            peer = (x ^ (k >> 2), y ^ ((k >> 1) & 1), c ^ (k & 1))
            n = i * PEER_COPIES + k - 1
            copies.append(pltpu.make_async_remote_copy(
                src_ref=ref.at[mine], dst_ref=ref.at[mine], send_sem=send_sems.at[n], recv_sem=recv_sems.at[n],
                device_id=peer, device_id_type=MESH))
    return copies


def _sibling_copies(g_refs, r_refs, send_sems, recv_sems):
    x, y, c = _place()
    return [pltpu.make_async_remote_copy(
        src_ref=g_refs[i].at[2 * k + (1 - c)], dst_ref=r_refs[i].at[k],
        send_sem=send_sems.at[SIBLING_COPIES * i + k], recv_sem=recv_sems.at[SIBLING_COPIES * i + k],
        device_id=(x, y, 1 - c), device_id_type=MESH) for i in range(len(g_refs)) for k in range(SIBLING_COPIES)]


def _chip_copies(p_refs, r_refs, send_sems, recv_sems):
    x, y, c = _place()
    chips = [(1 - x, y), (x, 1 - y), (1 - x, 1 - y)]
    return [pltpu.make_async_remote_copy(
        src_ref=p_refs[i].at[2 * cx + cy], dst_ref=r_refs[i].at[j],
        send_sem=send_sems.at[CHIP_COPIES * i + j], recv_sem=recv_sems.at[CHIP_COPIES * i + j],
        device_id=(cx, cy, c), device_id_type=MESH) for i in range(len(p_refs)) for j, (cx, cy) in enumerate(chips)]


def _in_hbm(a):
    return pltpu.with_memory_space_constraint(a, pltpu.HBM)


def _exchange_start(name, copies_fn, n_copies, srcs, lands, after=()):
    n_s, n_l, n_a = len(srcs), len(lands), len(after)

    def body(*refs):
        outs = refs[n_s + n_l + n_a:]
        for cp in copies_fn(refs[:n_s], refs[n_s:n_s + n_l], outs[0], outs[1]):
            cp.start()
        outs[-1][...] = jnp.zeros_like(outs[-1])

    res = pl.pallas_call(
        body, name=name,
        out_shape=[pltpu.SemaphoreType.DMA((n_copies,)), pltpu.SemaphoreType.DMA((n_copies,))]
                  + [pltpu.HBM(t.shape, t.dtype) for t in (*srcs, *lands)] + [jax.ShapeDtypeStruct((8, LANES), F32)],
        in_specs=[HBM] * (n_s + n_l) + [ANY] * n_a,
        out_specs=[SEM, SEM] + [HBM] * (n_s + n_l) + [pl.BlockSpec(memory_space=pltpu.VMEM)],
        input_output_aliases={i: 2 + i for i in range(n_s + n_l)},
        compiler_params=pltpu.CompilerParams(has_side_effects=pltpu.SideEffectType.DATAFLOW_SIDE_EFFECTING),
    )(*[_in_hbm(t) for t in (*srcs, *lands)], *after)
    return res[0], res[1], list(res[2:2 + n_s]), list(res[2 + n_s:2 + n_s + n_l]), res[-1]


def _exchange_wait(name, copies_fn, started, after):
    send_sems, recv_sems, srcs, lands, _ = started
    n_s, n_l = len(srcs), len(lands)

    def body(*refs):
        for cp in copies_fn(refs[:n_s], refs[n_s:n_s + n_l], refs[n_s + n_l], refs[n_s + n_l + 1]):
            cp.wait_send()
            cp.wait_recv()

    res = pl.pallas_call(
        body, name=name,
        out_shape=[pltpu.HBM(t.shape, t.dtype) for t in (*srcs, *lands)],
        in_specs=[HBM] * (n_s + n_l) + [SEM, SEM, ANY],
        out_specs=[HBM] * (n_s + n_l),
        input_output_aliases={i: i for i in range(n_s + n_l)},
        compiler_params=pltpu.CompilerParams(has_side_effects=pltpu.SideEffectType.DATAFLOW_SIDE_EFFECTING),
    )(*srcs, *lands, send_sems, recv_sems, after)
    return list(res[:n_s]), list(res[n_s:])


def _shard_block(shape):
    rows, cols = shape
    if rows % 256:
        return rows, 256
    return min(rows, 512 if cols <= LANES else 256), cols


def _chip_partial(ids, g_stack, recv1, name):
    shape = g_stack.shape[1:]
    br, bc = _shard_block(shape)

    def body(ids_ref, g_ref, r_ref, pb_ref, own_ref):
        s = g_ref[...] + r_ref[...]
        pb_ref[...] = s.astype(BF16)

        @pl.when(pl.program_id(2) == ids_ref[1])
        def _():
            own_ref[...] = s

    grid_spec = pltpu.PrefetchScalarGridSpec(
        num_scalar_prefetch=1, grid=(shape[0] // br, shape[1] // bc, 4),
        in_specs=[pl.BlockSpec((None, br, bc), lambda r, q, k, ids: (2 * k + ids[0], r, q)),
                  pl.BlockSpec((None, br, bc), lambda r, q, k, ids: (k, r, q))],
        out_specs=[pl.BlockSpec((None, br, bc), lambda r, q, k, ids: (k, r, q)),
                   pl.BlockSpec((br, bc), lambda r, q, k, ids: (r, q))])
    return pl.pallas_call(
        body, name=name, grid_spec=grid_spec,
        out_shape=[jax.ShapeDtypeStruct((4,) + shape, BF16), jax.ShapeDtypeStruct(shape, F32)],
        compiler_params=_cparams(("parallel", "parallel", "arbitrary")),
    )(ids, g_stack, recv1)


def _adamw(w, g, m, v):
    m = ADAM_B1 * m + (1.0 - ADAM_B1) * g
    v = ADAM_B2 * v + (1.0 - ADAM_B2) * (g * g)
    m_hat = m / (1.0 - ADAM_B1 ** ADAM_STEP)
    v_hat = v / (1.0 - ADAM_B2 ** ADAM_STEP)
    delta = -ADAM_LR * (m_hat / (jnp.sqrt(v_hat) + ADAM_EPS) + ADAM_WD * w)
    return delta, m, v


def _reduce_adamw(own, recv2, w, m, v, name, deps=()):
    shape = own.shape
    br, bc = _shard_block(shape)

    def body(own_ref, r_ref, w_ref, m_ref, v_ref, *refs):
        g_ref, d_ref, nm_ref, nv_ref = refs[len(deps):]
        g = own_ref[...]
        for j in range(3):
            g = g + r_ref[j].astype(F32)
        delta, nm, nv = _adamw(w_ref[...], g, m_ref[...], v_ref[...])
        g_ref[...] = g
        d_ref[...] = delta
        nm_ref[...] = nm
        nv_ref[...] = nv

    blk = pl.BlockSpec((br, bc), lambda r, q: (r, q))
    return pl.pallas_call(
        body, name=name, grid=(shape[0] // br, shape[1] // bc),
        in_specs=[blk, pl.BlockSpec((3, br, bc), lambda r, q: (0, r, q)), blk, blk, blk] + [ANY] * len(deps),
        out_specs=[blk] * 4, out_shape=[jax.ShapeDtypeStruct(shape, F32)] * 4,
        compiler_params=_cparams(("parallel", "parallel")),
    )(own, recv2, w, m, v, *deps)


def _small_allreduce_adamw(gvec, w, m, v, deps=()):
    def body(g_ref, w_ref, m_ref, v_ref, *refs):
        go_ref, d_ref, nm_ref, nv_ref, buf, send_sems, recv_sems = refs[len(deps):]
        x, y, c = _place()
        my_slot = 4 * x + 2 * y + c
        buf[my_slot] = g_ref[...]
        copies = []
        for k in range(1, N_DEV):
            px, py, pc = x ^ (k >> 2), y ^ ((k >> 1) & 1), c ^ (k & 1)
            copies.append(pltpu.make_async_remote_copy(
                src_ref=g_ref, dst_ref=buf.at[my_slot], send_sem=send_sems.at[k - 1], recv_sem=recv_sems.at[k - 1],
                device_id=(px, py, pc), device_id_type=MESH))
        for cp in copies:
            cp.start()
        for k in range(1, N_DEV):
            px, py, pc = x ^ (k >> 2), y ^ ((k >> 1) & 1), c ^ (k & 1)
            pltpu.make_async_remote_copy(
                src_ref=g_ref, dst_ref=buf.at[4 * px + 2 * py + pc], send_sem=send_sems.at[k - 1],
                recv_sem=recv_sems.at[k - 1], device_id=(px, py, pc), device_id_type=MESH).wait_recv()
        for cp in copies:
            cp.wait_send()
        g = buf[0]
        for s in range(1, N_DEV):
            g = g + buf[s]
        delta, nm, nv = _adamw(w_ref[...], g, m_ref[...], v_ref[...])
        go_ref[...] = g
        d_ref[...] = delta
        nm_ref[...] = nm
        nv_ref[...] = nv

    vm = pl.BlockSpec(memory_space=pltpu.VMEM)
    return pl.pallas_call(
        body, name="small_allreduce_adamw",
        in_specs=[vm] * 4 + [ANY] * len(deps), out_specs=[vm] * 4,
        out_shape=[jax.ShapeDtypeStruct((SMALL_R, LANES), F32)] * 4,
        scratch_shapes=[pltpu.VMEM((N_DEV, SMALL_R, LANES), F32),
                        pltpu.SemaphoreType.DMA((N_DEV - 1,)), pltpu.SemaphoreType.DMA((N_DEV - 1,))],
    )(gvec, w, m, v, *deps)


def _cols_to_whole(stack):
    rows = stack.shape[1]
    return stack.transpose(1, 0, 2).reshape(rows, -1)


def _whole_to_cols(t):
    rows = t.shape[0]
    return t.reshape(rows, N_DEV, -1).transpose(1, 0, 2)


W_IN_SEGMENTS = ((0, 3 * FOX_W, SEG_A), (3 * FOX_W, 3 * FOX_W + FOX_H, SEG_F),
                 (3 * FOX_W + FOX_H, 3 * FOX_W + FOX_H + 3 * DIL_W, SEG_B), (3 * FOX_W + FOX_H + 3 * DIL_W, D_IN, SEG_G))


def _shard_pieces(p):
    rows = D_IN // N_DEV
    lo, hi = p * rows, (p + 1) * rows
    return [(max(lo, a) - lo, min(hi, b) - lo, pad + max(lo, a) - a)
            for a, b, pad in W_IN_SEGMENTS if max(lo, a) < min(hi, b)]


def _unpad_dw_in_t(dwp):
    return jnp.stack([jnp.concatenate([dwp[pad:pad + b - a] for a, b, pad in _shard_pieces(p)], axis=0)
                      for p in range(N_DEV)])


LOSS_ROW = 25


def _pack_small(g_attn, b, g_mlp, g_final, loss_row=None):
    tail = jnp.pad(b, ((0, 7), (0, LANES - b.shape[1])))
    if loss_row is not None:
        tail = tail + jnp.pad(loss_row, ((LOSS_ROW - 24, 31 - LOSS_ROW), (0, 0)))
    return jnp.concatenate([g_attn.reshape(8, LANES), g_mlp.reshape(8, LANES), g_final.reshape(8, LANES), tail], axis=0)


def _unpack_small(p):
    return (p[0:8].reshape(1, D), p[24:25, :FOX_H], p[8:16].reshape(1, D), p[16:24].reshape(D))


class _StepComm:
    def __init__(self, ids, w_sh, m_sh, v_sh, after_w_in):
        self.ids = ids
        self.w_sh, self.m_sh, self.v_sh = w_sh, m_sh, v_sh
        self.updates = None
        lands = _place_own(w_sh[1:])
        self.gather_mixer = _exchange_start("gather_mixer_start", _peer_copies, PEER_COPIES * 3, [], lands[:3],
                                            after=(after_w_in,))
        self.gather_mlp = _exchange_start("gather_mlp_start", _peer_copies, PEER_COPIES * 2, [], lands[3:],
                                          after=(self.gather_mixer[-1],))
        self.sibling = {}
        self.chips = {}
        self.own = {}
        self.names = {}

    def _reduce_start(self, group, names, grads, after=()):
        lands = [lax.empty((SIBLING_COPIES,) + t.shape[1:], F32) for t in grads]
        self.sibling[group] = _exchange_start("grad_%s_sibling_start" % group, _sibling_copies,
                                              SIBLING_COPIES * len(grads), grads, lands, after=after)
        self.names[group] = names
        return self.sibling[group][-1]

    def _reduce_mid(self, group, after):
        grads, recv1 = _exchange_wait("grad_%s_sibling_wait" % group, _sibling_copies, self.sibling[group], after)
        parts = [_chip_partial(self.ids, g, r, "grad_partial_" + n) for g, r, n in zip(grads, recv1, self.names[group])]
        self.own[group] = [p[1] for p in parts]
        srcs = [p[0] for p in parts]
        lands = [lax.empty((CHIP_COPIES,) + t.shape[1:], BF16) for t in srcs]
        self.chips[group] = _exchange_start("grad_%s_chips_start" % group, _chip_copies, CHIP_COPIES * len(srcs),
                                            srcs, lands)
        return self.chips[group][-1]

    def reduced(self, group, after):
        _, recv2 = _exchange_wait("grad_%s_chips_wait" % group, _chip_copies, self.chips[group], after)
        return list(zip(self.own[group], recv2))

    def first_deps(self):
        return [self.gather_mlp[-1]]

    def mixer_weights(self, after):
        _, (g_a, g_b, g_out) = _exchange_wait("gather_mixer_wait", _peer_copies, self.gather_mixer, after)
        return _cols_to_whole(g_a), _cols_to_whole(g_b), g_out.reshape(D, D)

    def mlp_weights(self, after):
        _, (g_up, g_down) = _exchange_wait("gather_mlp_wait", _peer_copies, self.gather_mlp, after)
        return g_up, g_down.reshape(DFF, D)

    def mlp_grads(self, dw_up_sh, dw_down):
        return [self._reduce_start("mlp", W_NAMES[4:], [dw_up_sh, dw_down.reshape((N_DEV,) + SHARD_SHAPES[5])])]

    def mixer_grads(self, dw_a, dw_b, dw_out):
        token = self._reduce_mid("mlp", dw_b)
        grads = [_whole_to_cols(dw_a), _whole_to_cols(dw_b), dw_out.reshape((N_DEV,) + SHARD_SHAPES[3])]
        return [self._reduce_start("mixer", W_NAMES[1:4], grads, after=(token,))]

    def mid_backward(self, after):
        return [self._reduce_mid("mixer", after)]

    def w_in_grad(self, dw_in_t):
        grads = [_unpad_dw_in_t(dw_in_t)]
        token = self._reduce_start("w_in", W_NAMES[:1], grads)
        reduced = self.reduced("mixer", token) + self.reduced("mlp", token)
        self.updates, last = [None] * len(reduced), token
        for i in (3, 4, 0, 1, 2):
            if i == 0:
                last = self._reduce_mid("w_in", last)
            self.updates[i] = _reduce_adamw(*reduced[i], self.w_sh[1 + i], self.m_sh[1 + i], self.v_sh[1 + i],
                                            "adamw_" + W_NAMES[1 + i], deps=[last])
            last = self.updates[i][0]
        return [last]

    def w_in_update(self, after):
        (reduced,) = self.reduced("w_in", after)
        return _reduce_adamw(*reduced, self.w_sh[0], self.m_sh[0], self.v_sh[0], "adamw_" + W_NAMES[0])


def kernel(x, norm_attn_g, w_in, b_forget, w_branch_a, w_branch_b, w_out, norm_mlp_g, w_up, w_down, norm_final_g, loss_target, m_norm_attn_g, m_w_in, m_b_forget, m_w_branch_a, m_w_branch_b, m_w_out, m_norm_mlp_g, m_w_up, m_w_down, m_norm_final_g, v_norm_attn_g, v_w_in, v_b_forget, v_w_branch_a, v_w_branch_b, v_w_out, v_norm_mlp_g, v_w_up, v_w_down, v_norm_final_g):
    cx, cy, cc = _place()
    ids = jnp.stack([cc, 2 * cx + cy]).astype(jnp.int32)

    w_sh = [w_in[0].T] + [t[0] for t in (w_branch_a, w_branch_b, w_out, w_up, w_down)]
    m_sh = [m_w_in[0].T] + [t[0] for t in (m_w_branch_a, m_w_branch_b, m_w_out, m_w_up, m_w_down)]
    v_sh = [v_w_in[0].T] + [t[0] for t in (v_w_branch_a, v_w_branch_b, v_w_out, v_w_up, v_w_down)]

    w_in_t = _gather_w_in(jnp.pad(w_sh[0], ((0, W_SLAB - w_sh[0].shape[0]), (0, 0))))
    comm = _StepComm(ids, w_sh, m_sh, v_sh, w_in_t)
    b_pad = jnp.pad(b_forget, ((0, 0), (0, LANES - FOX_H)))

    loss_row, dx, dsmall = _local_step(
        x[0], loss_target[0], norm_attn_g, norm_mlp_g, norm_final_g.reshape(1, D), b_pad, w_in_t, comm)

    dg_attn, db, dg_mlp, dg_final = dsmall
    small = _small_allreduce_adamw(
        _pack_small(dg_attn, db, dg_mlp, dg_final, loss_row),
        _pack_small(norm_attn_g, b_forget, norm_mlp_g, norm_final_g.reshape(1, D)),
        _pack_small(m_norm_attn_g, m_b_forget, m_norm_mlp_g, m_norm_final_g.reshape(1, D)),
        _pack_small(v_norm_attn_g, v_b_forget, v_norm_mlp_g, v_norm_final_g.reshape(1, D)))
    big = [comm.w_in_update(small[0])] + comm.updates

    outs = [small[0][LOSS_ROW, 0], dx[None]]
    for q in range(4):
        s_attn, s_b, s_mlp, s_final = _unpack_small(small[q])
        b_in, b_a, b_b, b_out, b_up, b_down = [t[q][None] for t in big]
        b_in = jnp.swapaxes(b_in, 1, 2)
        outs += [s_attn, b_in, s_b, b_a, b_b, b_out, s_mlp, b_up, b_down, s_final]
    return tuple(outs)
```

```python
import functools

import jax
import jax.numpy as jnp
from jax import lax
from jax.experimental import pallas as pl
from jax.experimental.pallas import tpu as pltpu

F32 = jnp.float32
BF16 = jnp.bfloat16
MESH = pl.DeviceIdType.MESH

S = 2048
D = 1024
HD = 64
FOX_H = 8
FOX_W = FOX_H * HD
DIL_HG = 4
DIL_G = 3
DIL_W = DIL_G * DIL_HG * HD
DIL_OUT = DIL_HG * HD
DIL_BLK = 128
DIL_R = (1, 4, 16)
DFF = 4 * D
D_IN = 3 * FOX_W + FOX_H + 3 * DIL_W + 2 * D
EPS = 1e-6
NEG_INF = -1e30
SCALE = HD ** -0.5
ROPE_THETA = 500000.0
ROPE_DIM = HD // 4
N_DEV = 8

ADAM_LR = 0.001
ADAM_B1 = 0.9
ADAM_B2 = 0.999
ADAM_EPS = 1e-08
ADAM_WD = 0.01
ADAM_STEP = 10

LANES = 128
CB = 256
SEG_A = 0
SEG_B = 3 * FOX_W
SEG_Z = SEG_B + 3 * DIL_W
SEG_G = SEG_Z + CB
SEG_F = SEG_G + 2 * D
D_PAD = SEG_F + CB
SMALL_R = 32

VMEM_MB = 56


def _cparams(dims=None, vmem_mb=VMEM_MB, **kw):
    return pltpu.CompilerParams(dimension_semantics=dims, vmem_limit_bytes=vmem_mb << 20, **kw)


ANY = pl.BlockSpec(memory_space=pl.ANY)

_NN = (((1,), (0,)), ((), ()))
_NT = (((1,), (1,)), ((), ()))
_TN = (((0,), (0,)), ((), ()))


def _dot(a, b, dims):
    return lax.dot_general(a.astype(BF16), b.astype(BF16), dims, preferred_element_type=F32)


def _mm(a, b, *, mode, tm, tn, out_dtypes, name, n=None, k=None, a_off=0, b_off=0,
        b_sharded=False, out_sharded=False, epilogue=None, extras=(), deps=()):
    n_sh = b.shape[-1] if b_sharded else None
    if mode == "nn":
        m = a.shape[0]
        k = k or a.shape[1]
        a_spec = pl.BlockSpec((tm, k), lambda i, j: (i, a_off))
        if b_sharded:
            assert tn == n_sh
            n = N_DEV * n_sh
            b_spec = pl.BlockSpec((None, k, tn), lambda i, j: (j, 0, 0))
        else:
            n = n or b.shape[1]
            b_spec = pl.BlockSpec((k, tn), lambda i, j: (0, j + b_off))
        dims = _NN
    elif mode == "nt":
        m = a.shape[0]
        k = k or a.shape[1]
        a_spec = pl.BlockSpec((tm, k), lambda i, j: (i, a_off))
        if b_sharded:
            n = b.shape[1]
            b_spec = pl.BlockSpec((N_DEV, tn, n_sh), lambda i, j: (0, j, 0))
        else:
            n = n or b.shape[0]
            b_spec = pl.BlockSpec((tn, k), lambda i, j: (j + b_off, 0))
        dims = _NT
    else:
        k, m = a.shape
        n = n or b.shape[1]
        a_spec = pl.BlockSpec((k, tm), lambda i, j: (0, i))
        b_spec = pl.BlockSpec((k, tn), lambda i, j: (0, j + b_off))
        dims = _TN
    assert m % tm == 0 and n % tn == 0, (name, m, n, tm, tn)
    n_extra = len(extras)
    tile = pl.BlockSpec((tm, tn), lambda i, j: (i, j))
    if out_sharded:
        assert mode == "tn" and n // tn == N_DEV
        out_spec = pl.BlockSpec((None, tm, tn), lambda i, j: (j, i, 0))
        out_shape = (N_DEV, m, tn)
    else:
        out_spec, out_shape = tile, (m, n)

    def body(a_ref, b_ref, *refs):
        if mode == "nt" and b_sharded:
            acc = _dot(a_ref[:, 0:n_sh], b_ref[0], dims)
            for p in range(1, N_DEV):
                acc = acc + _dot(a_ref[:, p * n_sh:(p + 1) * n_sh], b_ref[p], dims)
        else:
            acc = _dot(a_ref[...], b_ref[...], dims)
        ex = [r[...] for r in refs[:n_extra]]
        outs = epilogue(acc, *ex) if epilogue is not None else (acc,)
        for o_ref, o in zip(refs[n_extra + len(deps):], outs):
            o_ref[...] = o.astype(o_ref.dtype)

    res = pl.pallas_call(
        body, name=name, grid=(m // tm, n // tn),
        in_specs=[a_spec, b_spec] + [tile] * n_extra + [ANY] * len(deps),
        out_specs=[out_spec] * len(out_dtypes),
        out_shape=[jax.ShapeDtypeStruct(out_shape, dt) for dt in out_dtypes],
        compiler_params=_cparams(("parallel", "parallel")),
    )(a, b, *extras, *deps)
    return res if len(out_dtypes) > 1 else res[0]


def _mm_rows(a, b, *, tm, name, epilogue, rows=(), vecs=(), row_out=(), vec_out=(), b_sharded=False, deps=()):
    m, k = a.shape
    n_rows, n_vecs, n_deps = len(rows), len(vecs), len(deps)
    n_sh = b.shape[-1] if b_sharded else None

    def body(a_ref, b_ref, *refs):
        row_refs, vec_refs = refs[:n_rows], refs[n_rows:n_rows + n_vecs]
        outs = refs[n_rows + n_vecs + n_deps:]
        if b_sharded:
            acc = _dot(a_ref[:, 0:n_sh], b_ref[0], _NT)
            for p in range(1, N_DEV):
                acc = acc + _dot(a_ref[:, p * n_sh:(p + 1) * n_sh], b_ref[p], _NT)
        else:
            acc = _dot(a_ref[...], b_ref[...], _NN)
        row_vals, vec_incs = epilogue(acc, [r[...] for r in row_refs], [v[...] for v in vec_refs])
        for o_ref, val in zip(outs[:len(row_out)], row_vals):
            o_ref[...] = val.astype(o_ref.dtype)

        @pl.when(pl.program_id(0) == 0)
        def _():
            for o_ref in outs[len(row_out):]:
                o_ref[...] = jnp.zeros_like(o_ref)

        for o_ref, inc in zip(outs[len(row_out):], vec_incs):
            o_ref[...] += inc

    tile = pl.BlockSpec((tm, D), lambda i: (i, 0))
    b_spec = pl.BlockSpec(b.shape, lambda i: (0,) * b.ndim)
    return pl.pallas_call(
        body, name=name, grid=(m // tm,),
        in_specs=[pl.BlockSpec((tm, k), lambda i: (i, 0)), b_spec] + [tile] * n_rows
                 + [pl.BlockSpec(v.shape, lambda i: (0, 0)) for v in vecs] + [ANY] * n_deps,
        out_specs=[tile] * len(row_out) + [pl.BlockSpec((1, w), lambda i: (0, 0)) for w in vec_out],
        out_shape=[jax.ShapeDtypeStruct((m, D), dt) for dt in row_out]
                  + [jax.ShapeDtypeStruct((1, w), F32) for w in vec_out],
        compiler_params=_cparams(("arbitrary",)),
    )(a, b, *rows, *vecs, *deps)


ROW_T = 256
ROWS_TM = 512


def _rms_rows(x, g):
    r = lax.rsqrt(jnp.mean(x * x, axis=-1, keepdims=True) + EPS)
    return (x * r) * g


def _rms_fwd(x, g, name, deps=()):
    def body(x_ref, g_ref, *refs):
        refs[-1][...] = _rms_rows(x_ref[...], g_ref[...]).astype(BF16)

    return pl.pallas_call(
        body, name=name, grid=(S // ROW_T,),
        in_specs=[pl.BlockSpec((ROW_T, D), lambda i: (i, 0)), pl.BlockSpec((1, D), lambda i: (0, 0))] + [ANY] * len(deps),
        out_specs=pl.BlockSpec((ROW_T, D), lambda i: (i, 0)),
        out_shape=jax.ShapeDtypeStruct((S, D), BF16),
        compiler_params=_cparams(("parallel",)),
    )(x, g, *deps)


def _rms_bwd_rows(dh, x, dres, g):
    r = lax.rsqrt(jnp.mean(x * x, axis=-1, keepdims=True) + EPS)
    xn = x * r
    dhn = dh * g
    dx = dres + r * (dhn - xn * jnp.mean(dhn * xn, axis=-1, keepdims=True))
    return dx, jnp.sum(dh * xn, axis=0, keepdims=True)


def _final_loss_rows(x, g, tgt):
    r = lax.rsqrt(jnp.mean(x * x, axis=-1, keepdims=True) + EPS)
    xn = x * r
    err = xn * g - tgt
    row_loss = jnp.mean(err * err, axis=-1, keepdims=True)
    loss = 0.5 * jnp.sum(row_loss, axis=0, keepdims=True) * jnp.ones((1, LANES), F32)
    dy = err * (1.0 / D)
    dyn = dy * g
    dx = r * (dyn - xn * jnp.mean(dyn * xn, axis=-1, keepdims=True))
    return dx, jnp.sum(dy * xn, axis=0, keepdims=True), loss


def _sigmoid(z):
    return 1.0 / (1.0 + jnp.exp(-z))


def _gate_fwd(proj_g, ya, yb):
    def body(ga_ref, gb_ref, ya_ref, yb_ref, o_ref):
        o_ref[...] = (_sigmoid(ga_ref[...]) * ya_ref[...] + _sigmoid(gb_ref[...]) * yb_ref[...]).astype(BF16)

    row = pl.BlockSpec((ROW_T, D), lambda i: (i, 0))
    return pl.pallas_call(
        body, name="gate_fwd", grid=(S // ROW_T,),
        in_specs=[row, pl.BlockSpec((ROW_T, D), lambda i: (i, 1)), row, row],
        out_specs=row, out_shape=jax.ShapeDtypeStruct((S, D), BF16),
        compiler_params=_cparams(("parallel",)),
    )(proj_g, proj_g, ya, yb)


GATE_TR = 1024
GATE_TC = 512
GATE_NB = 2 * D // GATE_TC


def _gate_bwd(dmixed, proj_g, ya, yb):
    half = GATE_NB // 2

    def body(dm_ref, g_ref, ya_ref, yb_ref, dy_ref, dg_ref):
        dm = dm_ref[...]
        s = _sigmoid(g_ref[...])
        y = jnp.where(pl.program_id(1) < half, ya_ref[...], yb_ref[...])
        dy_ref[...] = (dm * s).astype(BF16)
        dg_ref[...] = (dm * y * (s * (1.0 - s))).astype(BF16)

    blk = pl.BlockSpec((GATE_TR, GATE_TC), lambda i, j: (i, j))
    wrapped = pl.BlockSpec((GATE_TR, GATE_TC), lambda i, j: (i, j % half))
    return pl.pallas_call(
        body, name="gate_bwd", grid=(S // GATE_TR, GATE_NB),
        in_specs=[wrapped, blk, wrapped, wrapped],
        out_specs=[blk, pl.BlockSpec((GATE_TR, GATE_TC), lambda i, j: (i, j + SEG_G // GATE_TC))],
        out_shape=[jax.ShapeDtypeStruct((S, 2 * D), BF16), jax.ShapeDtypeStruct((S, D_PAD), BF16)],
        compiler_params=_cparams(("parallel", "parallel")),
    )(dmixed, proj_g, ya, yb)


FOX_TQ = 256


def _scan_rows(x, reverse):
    n = x.shape[0]
    row = lax.broadcasted_iota(jnp.int32, x.shape, 0)
    k = 1
    while k < n:
        if reverse:
            x = x + jnp.where(row < n - k, pltpu.roll(x, n - k, 0), 0.0)
        else:
            x = x + jnp.where(row >= k, pltpu.roll(x, k, 0), 0.0)
        k *= 2
    return x


def _fox_dscan(dft, dfs, proj_f, b_pad, dproj):
    def body(dft_ref, dfs_ref, f_ref, b_ref, _, dfa_ref, db_ref):
        dfs_pad = jnp.concatenate([dfs_ref[...], jnp.zeros((LANES - FOX_H, S), F32)], axis=0)
        df = dfs_pad.T + dft_ref[0]
        for hp in range(1, dft_ref.shape[0]):
            df = df + pltpu.roll(dft_ref[hp], 2 * hp, 1)
        dlf = _scan_rows(df, reverse=True)
        z = f_ref[...] + b_ref[...]
        lane = lax.broadcasted_iota(jnp.int32, (S, LANES), 1)
        dfa = jnp.where(lane < FOX_H, dlf / (1.0 + jnp.exp(z)), 0.0)
        dfa_ref[:, :LANES] = dfa.astype(BF16)
        dfa_ref[:, LANES:] = jnp.zeros((S, CB - LANES), BF16)
        db_ref[...] = jnp.sum(dfa, axis=0, keepdims=True)

    return pl.pallas_call(
        body, name="fox_dscan", grid=(1,),
        in_specs=[pl.BlockSpec(dft.shape, lambda i: (0, 0, 0)), pl.BlockSpec((FOX_H, S), lambda i: (0, 0)),
                  pl.BlockSpec((S, LANES), lambda i: (0, 0)), pl.BlockSpec((1, LANES), lambda i: (0, 0)), ANY],
        out_specs=[pl.BlockSpec((S, CB), lambda i: (0, SEG_F // CB)), pl.BlockSpec((1, LANES), lambda i: (0, 0))],
        out_shape=[jax.ShapeDtypeStruct((S, D_PAD), BF16), jax.ShapeDtypeStruct((1, LANES), F32)],
        input_output_aliases={4: 0},
        compiler_params=_cparams(("arbitrary",)),
    )(dft, dfs, proj_f, b_pad, dproj)


FOX_NQ = S // FOX_TQ
FOX_HP = FOX_W // LANES
HEADS_PER_LB = LANES // HD
FOX_AUG = 2 * LANES


def _fox_prepare(proj_a, proj_f, b_pad):
    tr = 512

    def body(q_ref, k_ref, f_ref, b_ref, qa_ref, ka_ref, hi_s, mid_s, lo_s):
        i = pl.program_id(0)

        @pl.when(i == 0)
        def _():
            z = f_ref[...] + b_ref[...]
            lf = jnp.minimum(z, 0.0) - jnp.log1p(jnp.exp(-jnp.abs(z)))
            f = _scan_rows(lf, reverse=False)
            hi = f.astype(BF16).astype(F32)
            r1 = f - hi
            mid = r1.astype(BF16).astype(F32)
            hi_s[...] = hi
            mid_s[...] = mid
            lo_s[...] = (r1 - mid).astype(BF16).astype(F32)

        rows = pl.ds(pl.multiple_of(i * tr, tr), tr)
        hi, mid, lo = hi_s[rows, :], mid_s[rows, :], lo_s[rows, :]
        lane = lax.broadcasted_iota(jnp.int32, (tr, HD), 1)
        q_ones = jnp.where((lane >= 3) & (lane < 6), 1.0, 0.0)
        k_ones = jnp.where(lane < 3, 1.0, 0.0)
        for h in range(FOX_H):
            a1, a2, a3 = hi[:, h:h + 1], mid[:, h:h + 1], lo[:, h:h + 1]
            q_extra = jnp.where(lane == 0, a1, jnp.where(lane == 1, a2, jnp.where(lane == 2, a3, q_ones)))
            k_extra = jnp.where(lane == 3, -a1, jnp.where(lane == 4, -a2, jnp.where(lane == 5, -a3, k_ones)))
            base = h * LANES
            qa_ref[:, base:base + HD] = (q_ref[:, h * HD:(h + 1) * HD].astype(F32) * SCALE).astype(BF16)
            qa_ref[:, base + HD:base + LANES] = q_extra.astype(BF16)
            ka_ref[:, base:base + HD] = k_ref[:, h * HD:(h + 1) * HD]
            ka_ref[:, base + HD:base + LANES] = k_extra.astype(BF16)

    out_blk = pl.BlockSpec((tr, FOX_H * LANES), lambda i: (i, 0))
    return pl.pallas_call(
        body, name="fox_prepare", grid=(S // tr,),
        in_specs=[pl.BlockSpec((tr, FOX_W), lambda i: (i, 0)), pl.BlockSpec((tr, FOX_W), lambda i: (i, 1)),
                  pl.BlockSpec((S, LANES), lambda i: (0, 0)), pl.BlockSpec((1, LANES), lambda i: (0, 0))],
        out_specs=[out_blk, out_blk],
        out_shape=[jax.ShapeDtypeStruct((S, FOX_H * LANES), BF16)] * 2,
        scratch_shapes=[pltpu.VMEM((S, LANES), F32)] * 3,
        compiler_params=_cparams(("arbitrary",)),
    )(proj_a, proj_a, proj_f, b_pad)


def _fox_scores(qa, ka, w):
    s = _dot(qa, ka, _NT)
    row = lax.broadcasted_iota(jnp.int32, (FOX_TQ, FOX_TQ), 0)
    col = lax.broadcasted_iota(jnp.int32, (FOX_TQ, FOX_TQ), 1)
    diag = jnp.where(col <= row, s[:, w * FOX_TQ:], NEG_INF)
    return diag if w == 0 else jnp.concatenate([s[:, :w * FOX_TQ], diag], axis=1)


def _fox_fwd(q_aug, k_aug, proj_a):
    def body(qa_ref, ka_ref, v_ref, o_ref, lse_ref):
        qi = pl.program_id(1)
        lane = lax.broadcasted_iota(jnp.int32, (FOX_TQ, LANES), 1)
        for w in range(FOX_NQ):
            @pl.when(qi == w)
            def _(w=w):
                width = (w + 1) * FOX_TQ
                lse = jnp.zeros((FOX_TQ, LANES), F32)
                for hh in range(HEADS_PER_LB):
                    aug = slice(hh * LANES, (hh + 1) * LANES)
                    sl = slice(hh * HD, (hh + 1) * HD)
                    s = _fox_scores(qa_ref[:, aug], ka_ref[:width, aug], w)
                    m = jnp.max(s, axis=-1, keepdims=True)
                    p = jnp.exp(s - m)
                    l = jnp.sum(p, axis=-1, keepdims=True)
                    o_ref[:, sl] = _dot(p / l, v_ref[:width, sl], _NN)
                    lse = jnp.where(lane == hh, m + jnp.log(l), lse)
                lse_ref[...] = lse

    return pl.pallas_call(
        body, name="fox_fwd", grid=(FOX_HP, FOX_NQ),
        in_specs=[pl.BlockSpec((FOX_TQ, FOX_AUG), lambda h, i: (i, h)),
                  pl.BlockSpec((S, FOX_AUG), lambda h, i: (0, h)),
                  pl.BlockSpec((S, LANES), lambda h, i: (0, 2 * FOX_HP + h))],
        out_specs=[pl.BlockSpec((FOX_TQ, LANES), lambda h, i: (i, h)),
                   pl.BlockSpec((None, FOX_TQ, LANES), lambda h, i: (h, i, 0))],
        out_shape=[jax.ShapeDtypeStruct((S, FOX_W), F32), jax.ShapeDtypeStruct((FOX_HP, S, LANES), F32)],
        compiler_params=_cparams(("parallel", "parallel")),
    )(q_aug, k_aug, proj_a)


def _fox_bwd(q_aug, k_aug, proj_a, lse, do, dproj, deps=()):
    n_q = FOX_NQ

    def body(qa_ref, ka_ref, v_ref, lse_ref, do_ref, *refs):
        dproj_ref, dft_ref, dfs_ref, dk_acc, dv_acc, dq_buf, kv_buf, dq_sems, kv_sems = refs[1 + len(deps):]
        hp = pl.program_id(0)
        t = pl.program_id(1)
        slot = t % 2
        col = pl.multiple_of(hp * LANES, LANES)

        def dq_copy(step, buf_slot):
            rows = pl.ds(pl.multiple_of(step * FOX_TQ, FOX_TQ), FOX_TQ)
            return pltpu.make_async_copy(dq_buf.at[buf_slot], dproj_ref.at[rows, pl.ds(col, LANES)], dq_sems.at[buf_slot])

        @pl.when(t == 0)
        def _():
            dk_acc[...] = jnp.zeros_like(dk_acc)
            dv_acc[...] = jnp.zeros_like(dv_acc)

        @pl.when((t == 0) & (hp == 0))
        def _():
            dfs_ref[...] = jnp.zeros_like(dfs_ref)

        @pl.when(t >= 2)
        def _():
            dq_copy(t - 2, slot).wait()

        lane = lax.broadcasted_iota(jnp.int32, (FOX_TQ, LANES), 1)
        for w in range(n_q):
            @pl.when(t == w)
            def _(w=w):
                width = (w + 1) * FOX_TQ
                lse_all = lse_ref[...]
                sub = lax.broadcasted_iota(jnp.int32, (FOX_H, width), 0)
                dft = jnp.zeros((FOX_TQ, LANES), F32)
                for hh in range(HEADS_PER_LB):
                    aug = slice(hh * LANES, (hh + 1) * LANES)
                    sl = slice(hh * HD, (hh + 1) * HD)
                    head = hp * HEADS_PER_LB + hh
                    qa = qa_ref[:, aug]
                    ka = ka_ref[:width, aug]
                    do_h = do_ref[:, sl]
                    s = _fox_scores(qa, ka, w)
                    p = jnp.exp(s - lse_all[:, hh:hh + 1])
                    dp = _dot(do_h, v_ref[:width, sl], _NT)
                    ds = p * (dp - jnp.sum(dp * p, axis=-1, keepdims=True))
                    dft = jnp.where(lane == hh, jnp.sum(ds, axis=-1, keepdims=True), dft)
                    dfs_ref[:, :width] -= jnp.where(sub == head, jnp.sum(ds, axis=0, keepdims=True), 0.0)
                    dq_buf[slot, :, sl] = (_dot(ds, ka[:, :HD], _NN) * SCALE).astype(BF16)
                    dk_acc[:width, sl] += _dot(ds, qa[:, :HD], _TN)
                    dv_acc[:width, sl] += _dot(p, do_h, _TN)
                dft_ref[...] = dft

        dq_copy(t, slot).start()

        @pl.when(t == n_q - 1)
        def _():
            dq_copy(t - 1, 1 - slot).wait()
            dq_copy(t, slot).wait()
            kv_buf[0] = dk_acc[...].astype(BF16)
            kv_buf[1] = dv_acc[...].astype(BF16)
            copies = [pltpu.make_async_copy(
                kv_buf.at[j], dproj_ref.at[:, pl.ds(pl.multiple_of((j + 1) * FOX_W + hp * LANES, LANES), LANES)],
                kv_sems.at[j]) for j in range(2)]
            for cp in copies:
                cp.start()
            for cp in copies:
                cp.wait()

    qblk = pl.BlockSpec((FOX_TQ, FOX_AUG), lambda h, t: (t, h))
    lane_blk = pl.BlockSpec((None, FOX_TQ, LANES), lambda h, t: (h, t, 0))
    return pl.pallas_call(
        body, name="fox_bwd", grid=(FOX_HP, n_q),
        in_specs=[qblk, pl.BlockSpec((S, FOX_AUG), lambda h, t: (0, h)),
                  pl.BlockSpec((S, LANES), lambda h, t: (0, 2 * FOX_HP + h)),
                  lane_blk, pl.BlockSpec((FOX_TQ, LANES), lambda h, t: (t, h)), ANY] + [ANY] * len(deps),
        out_specs=[ANY, lane_blk, pl.BlockSpec((FOX_H, S), lambda h, t: (0, 0))],
        out_shape=[jax.ShapeDtypeStruct((S, D_PAD), BF16),
                   jax.ShapeDtypeStruct((FOX_HP, S, LANES), F32), jax.ShapeDtypeStruct((FOX_H, S), F32)],
        scratch_shapes=[pltpu.VMEM((S, LANES), F32), pltpu.VMEM((S, LANES), F32),
                        pltpu.VMEM((2, FOX_TQ, LANES), BF16), pltpu.VMEM((2, S, LANES), BF16),
                        pltpu.SemaphoreType.DMA((2,)), pltpu.SemaphoreType.DMA((2,))],
        input_output_aliases={5: 0},
        compiler_params=_cparams(("arbitrary", "arbitrary")),
    )(q_aug, k_aug, proj_a, lse, do, dproj, *deps)


def _rope_tables():
    half = ROPE_DIM // 2
    shape = (DIL_G, S, half)
    g = lax.broadcasted_iota(jnp.int32, shape, 0)
    row = lax.broadcasted_iota(jnp.int32, shape, 1)
    r = jnp.left_shift(1, 2 * g)
    per_class = S // r
    pos = (row % per_class) * r + row // per_class
    inv_freq = jnp.power(jnp.float32(ROPE_THETA), -lax.broadcasted_iota(F32, shape, 2) * 2.0 / ROPE_DIM)
    ang = pos.astype(F32) * inv_freq
    cos = jnp.tile(jnp.cos(ang), (1, 1, LANES // half))
    sin = jnp.tile(jnp.sin(ang), (1, 1, LANES // half))
    lane = lax.broadcasted_iota(jnp.int32, (DIL_G, S, LANES), 2) % HD
    t_self = jnp.where(lane < ROPE_DIM, cos, 1.0)
    t_up = jnp.where(lane < half, -sin, 0.0)
    t_dn = jnp.where((lane >= half) & (lane < ROPE_DIM), sin, 0.0)
    return t_self, t_up, t_dn


def _residue_pieces(r):
    if r == 1:
        return [(slice(i, i + 512), slice(i, i + 512)) for i in range(0, S, 512)]
    n = S // r
    return [(pl.ds(j, n, stride=r), slice(j * n, (j + 1) * n)) for j in range(r)]


def _rope_apply(x, a, u, d, backward):
    half = ROPE_DIM // 2
    if backward:
        return x * a + pltpu.roll(x * u, half, 1) + pltpu.roll(x * d, LANES - half, 1)
    return x * a + pltpu.roll(x, LANES - half, 1) * u + pltpu.roll(x, half, 1) * d


LB_PER_CB = CB // LANES
ROPE_NB = 3 * DIL_G * LB_PER_CB


def _rope_step(s):
    per_group = 3 * LB_PER_CB
    return s // per_group, (s % per_group) // LB_PER_CB, s % LB_PER_CB


def _rope_split(proj_b, tables):
    def body(x_ref, a_ref, u_ref, d_ref, o_ref):
        g, t, _ = _rope_step(pl.program_id(0))
        for gs in range(DIL_G):
            @pl.when(g == gs)
            def _(gs=gs):
                for tok, sub in _residue_pieces(DIL_R[gs]):
                    x = x_ref[tok, :]
                    y = _rope_apply(x, a_ref[sub, :], u_ref[sub, :], d_ref[sub, :], False)
                    o_ref[sub, :] = jnp.where(t < 2, y, x).astype(BF16)

    def in_index(s):
        g, t, hf = _rope_step(s)
        return 0, (t * DIL_G + g) * LB_PER_CB + hf

    def out_index(s):
        g, t, hf = _rope_step(s)
        return t * DIL_G + g, 0, hf

    tab = pl.BlockSpec((None, S, LANES), lambda s: (_rope_step(s)[0], 0, 0))
    return pl.pallas_call(
        body, name="rope_split", grid=(ROPE_NB,),
        in_specs=[pl.BlockSpec((S, LANES), in_index), tab, tab, tab],
        out_specs=pl.BlockSpec((None, S, LANES), out_index),
        out_shape=jax.ShapeDtypeStruct((3 * DIL_G, S, CB), BF16),
        compiler_params=_cparams(("parallel",)),
    )(proj_b, *tables)


def _rope_merge_bwd(dq3, dk3, dv3, tables, dproj):
    def body(dq_ref, dk_ref, dv_ref, a_ref, u_ref, d_ref, _, o_ref, tmp):
        s = pl.program_id(0)
        g, t, _ = _rope_step(s)
        for gs in range(DIL_G):
            @pl.when((g == gs) & (s < ROPE_NB))
            def _(gs=gs):
                for tok, sub in _residue_pieces(DIL_R[gs]):
                    x = jnp.where(t == 0, dq_ref[sub, :], jnp.where(t == 1, dk_ref[sub, :], dv_ref[sub, :]))
                    y = _rope_apply(x, a_ref[sub, :], u_ref[sub, :], d_ref[sub, :], True)
                    tmp[tok, :] = jnp.where(t < 2, y, x)
                o_ref[...] = tmp[...].astype(BF16)

        @pl.when(s >= ROPE_NB)
        def _():
            o_ref[...] = jnp.zeros_like(o_ref)

    def src_spec(own):
        def index(s):
            g, t, hf = _rope_step(jnp.minimum(s, ROPE_NB - 1))
            return g, 0, jnp.where(t == own, hf, jnp.where(t > own, LB_PER_CB - 1, 0))
        return pl.BlockSpec((None, S, LANES), index)

    def out_index(s):
        g, t, hf = _rope_step(s)
        col = SEG_B // LANES + (t * DIL_G + g) * LB_PER_CB + hf
        return 0, jnp.where(s < ROPE_NB, col, SEG_Z // LANES + s - ROPE_NB)

    tab = pl.BlockSpec((None, S, LANES), lambda s: (_rope_step(jnp.minimum(s, ROPE_NB - 1))[0], 0, 0))
    return pl.pallas_call(
        body, name="rope_merge_bwd", grid=(ROPE_NB + LB_PER_CB,),
        in_specs=[src_spec(0), src_spec(1), src_spec(2), tab, tab, tab, ANY],
        out_specs=pl.BlockSpec((S, LANES), out_index),
        out_shape=jax.ShapeDtypeStruct((S, D_PAD), BF16),
        scratch_shapes=[pltpu.VMEM((S, LANES), F32)],
        input_output_aliases={6: 0},
        compiler_params=_cparams(("arbitrary",)),
    )(dq3, dk3, dv3, *tables, dproj)


DIL_NB = S // DIL_BLK


_BQK = (((2,), (2,)), ((0,), (0,)))
_BQD = (((2,), (1,)), ((0,), (0,)))
_BKD = (((1,), (1,)), ((0,), (0,)))


def _dil_mask(g):
    shape = (DIL_NB, DIL_BLK, 2 * DIL_BLK)
    blocks_per_seq = jnp.right_shift(DIL_NB, 2 * g)
    has_prev = jnp.bitwise_and(lax.broadcasted_iota(jnp.int32, shape, 0), blocks_per_seq - 1) != 0
    a = lax.broadcasted_iota(jnp.int32, shape, 1)
    kk = lax.broadcasted_iota(jnp.int32, shape, 2)
    diff = DIL_BLK + a - kk
    return (diff >= 0) & (diff <= DIL_BLK) & ((kk >= DIL_BLK) | has_prev)


def _blocks(t):
    return t.reshape(DIL_NB, DIL_BLK, t.shape[-1])


def _with_prev(t):
    prev = jnp.concatenate([jnp.zeros((DIL_BLK, t.shape[-1]), t.dtype), t[:-DIL_BLK]], axis=0)
    return jnp.concatenate([_blocks(prev), _blocks(t)], axis=1)


def _fold_prev(t2):
    n = t2.shape[-1]
    to_prev = t2[:, :DIL_BLK].reshape(S, n)
    own = t2[:, DIL_BLK:].reshape(S, n)
    return own + jnp.concatenate([to_prev[DIL_BLK:], jnp.zeros((DIL_BLK, n), t2.dtype)], axis=0)


def _dil_group_spec(t, width=DIL_OUT):
    return pl.BlockSpec((None, S, width), lambda g: (t * DIL_G + g, 0, 0))


def _dil_fwd(qkv3):
    def body(q_ref, k_ref, v_ref, o_ref, lse_ref):
        ok = _dil_mask(pl.program_id(0))
        lane = lax.broadcasted_iota(jnp.int32, (S, LANES), 1)
        lse = jnp.zeros((S, LANES), F32)
        for h in range(DIL_HG):
            sl = slice(h * HD, (h + 1) * HD)
            s = jnp.where(ok, _dot(_blocks(q_ref[:, sl]), _with_prev(k_ref[:, sl]), _BQK) * SCALE, NEG_INF)
            m = jnp.max(s, axis=-1, keepdims=True)
            p = jnp.exp(s - m)
            l = jnp.sum(p, axis=-1, keepdims=True)
            o_ref[:, sl] = _dot(p / l, _with_prev(v_ref[:, sl]), _BQD).reshape(S, HD)
            lse = jnp.where(lane == h, (m + jnp.log(l)).reshape(S, 1), lse)
        lse_ref[...] = lse

    return pl.pallas_call(
        body, name="dil_fwd", grid=(DIL_G,),
        in_specs=[_dil_group_spec(0), _dil_group_spec(1), _dil_group_spec(2)],
        out_specs=[_dil_group_spec(0), _dil_group_spec(0, LANES)],
        out_shape=[jax.ShapeDtypeStruct((DIL_G, S, DIL_OUT), F32), jax.ShapeDtypeStruct((DIL_G, S, LANES), F32)],
        compiler_params=_cparams(("parallel",)),
    )(qkv3, qkv3, qkv3)


def _dil_bwd(qkv3, lse3, do3, c3):
    def body(q_ref, k_ref, v_ref, lse_ref, do_ref, c_ref, dq_ref, dk_ref, dv_ref):
        ok = _dil_mask(pl.program_id(0))
        lse = lse_ref[...]
        c = c_ref[...]
        for h in range(DIL_HG):
            sl = slice(h * HD, (h + 1) * HD)
            q = _blocks(q_ref[:, sl])
            k2 = _with_prev(k_ref[:, sl])
            do_h = _blocks(do_ref[:, sl])
            s = jnp.where(ok, _dot(q, k2, _BQK) * SCALE, NEG_INF)
            p = jnp.exp(s - _blocks(lse[:, h:h + 1]))
            dp = _dot(do_h, _with_prev(v_ref[:, sl]), _BQK)
            ds = p * (dp - _blocks(c[:, h:h + 1]))
            dsb = (ds * SCALE).astype(BF16)
            dq_ref[:, sl] = _dot(dsb, k2, _BQD).reshape(S, HD)
            dk_ref[:, sl] = _fold_prev(_dot(dsb, q, _BKD))
            dv_ref[:, sl] = _fold_prev(_dot(p, do_h, _BKD))

    return pl.pallas_call(
        body, name="dil_bwd", grid=(DIL_G,),
        in_specs=[_dil_group_spec(0), _dil_group_spec(1), _dil_group_spec(2), _dil_group_spec(0, LANES),
                  _dil_group_spec(0), _dil_group_spec(0, LANES)],
        out_specs=[_dil_group_spec(0)] * 3,
        out_shape=[jax.ShapeDtypeStruct((DIL_G, S, DIL_OUT), F32)] * 3,
        compiler_params=_cparams(("parallel",)),
    )(qkv3, qkv3, qkv3, lse3, do3, c3)


COMB_T = 256


def _dil_combine(o3, lse3):
    heads_per_lb = LANES // HD

    def body(o_ref, lse_ref, ob_ref, al_ref, *scratch):
        o_tok = [scratch[LB_PER_CB * gg:LB_PER_CB * (gg + 1)] for gg in range(DIL_G)]
        lse_tok = scratch[LB_PER_CB * DIL_G:]
        g = pl.program_id(0)
        for gs in range(DIL_G):
            @pl.when(g == gs)
            def _(gs=gs):
                for tok, sub in _residue_pieces(DIL_R[gs]):
                    for hf in range(LB_PER_CB):
                        o_tok[gs][hf][tok, :] = o_ref[sub, hf * LANES:(hf + 1) * LANES]
                    lse_tok[gs][tok, :] = lse_ref[sub, :]

        @pl.when(g == DIL_G - 1)
        def _():
            def chunk(i, carry):
                rows = pl.ds(pl.multiple_of(i * COMB_T, COMB_T), COMB_T)
                lse = [lse_tok[gg][rows, :] for gg in range(DIL_G)]
                m = jnp.maximum(jnp.maximum(lse[0], lse[1]), lse[2])
                e = [jnp.exp(lse[gg] - m) for gg in range(DIL_G)]
                den = (e[0] + e[1]) + e[2]
                al = [e[gg] / den for gg in range(DIL_G)]
                for gg in range(DIL_G):
                    al_ref[gg, rows, :] = al[gg]
                for h in range(DIL_HG):
                    hf, sl = h // heads_per_lb, slice((h % heads_per_lb) * HD, (h % heads_per_lb + 1) * HD)
                    acc = al[0][:, h:h + 1] * o_tok[0][hf][rows, sl]
                    for gg in range(1, DIL_G):
                        acc = acc + al[gg][:, h:h + 1] * o_tok[gg][hf][rows, sl]
                    ob_ref[rows, h * HD:(h + 1) * HD] = acc
                return carry

            lax.fori_loop(0, S // COMB_T, chunk, 0)

    return pl.pallas_call(
        body, name="dil_combine", grid=(DIL_G,),
        in_specs=[pl.BlockSpec((None, S, DIL_OUT), lambda g: (g, 0, 0)),
                  pl.BlockSpec((None, S, LANES), lambda g: (g, 0, 0))],
        out_specs=[pl.BlockSpec((S, DIL_OUT), lambda g: (0, 0)),
                   pl.BlockSpec((DIL_G, S, LANES), lambda g: (0, 0, 0))],
        out_shape=[jax.ShapeDtypeStruct((S, DIL_OUT), F32), jax.ShapeDtypeStruct((DIL_G, S, LANES), F32)],
        scratch_shapes=[pltpu.VMEM((S, LANES), F32)] * (DIL_G * (LB_PER_CB + 1)),
        compiler_params=_cparams(("arbitrary",)),
    )(o3, lse3)


def _dil_combine_bwd(dob, ob, alpha, deps=()):
    heads_per_lb = LANES // HD

    def body(dob_ref, ob_ref, al_ref, *refs):
        do_ref, c_ref = refs[len(deps):]
        g = pl.program_id(0)
        hf = pl.program_id(1)
        for gs in range(DIL_G):
            @pl.when(g == gs)
            def _(gs=gs):
                for tok, sub in _residue_pieces(DIL_R[gs]):
                    dob = dob_ref[tok, :]
                    prod = dob * ob_ref[tok, :]
                    al = al_ref[tok, :]
                    lane = lax.broadcasted_iota(jnp.int32, al.shape, 1)
                    c = jnp.where(hf == 0, 0.0, c_ref[sub, :])
                    for hh in range(heads_per_lb):
                        sl = slice(hh * HD, (hh + 1) * HD)
                        head = hf * heads_per_lb + hh
                        a = jnp.sum(jnp.where(lane == head, al, 0.0), axis=-1, keepdims=True)
                        do_ref[sub, sl] = (a * dob[:, sl]).astype(BF16)
                        c = jnp.where(lane == head, a * jnp.sum(prod[:, sl], axis=-1, keepdims=True), c)
                    c_ref[sub, :] = c

    half = pl.BlockSpec((S, LANES), lambda g, hf: (0, hf))
    return pl.pallas_call(
        body, name="dil_combine_bwd", grid=(DIL_G, LB_PER_CB),
        in_specs=[half, half, pl.BlockSpec((None, S, LANES), lambda g, hf: (g, 0, 0))] + [ANY] * len(deps),
        out_specs=[pl.BlockSpec((None, S, LANES), lambda g, hf: (g, 0, hf)),
                   pl.BlockSpec((None, S, LANES), lambda g, hf: (g, 0, 0))],
        out_shape=[jax.ShapeDtypeStruct((DIL_G, S, DIL_OUT), BF16), jax.ShapeDtypeStruct((DIL_G, S, LANES), F32)],
        compiler_params=_cparams(("parallel", "arbitrary")),
    )(dob, ob, alpha, *deps)


def _local_step(x, tgt, g_attn, g_mlp, g_final, b_pad, w_in_t, hooks):
    tables = _rope_tables()

    h1 = _rms_fwd(x, g_attn, "rms_attn_fwd", deps=hooks.first_deps())
    proj_a = _mm(h1, w_in_t, mode="nt", tm=S, tn=512, n=3 * FOX_W, out_dtypes=[BF16], name="proj_a")
    proj_b = _mm(h1, w_in_t, mode="nt", tm=S, tn=DIL_W, n=3 * DIL_W, b_off=SEG_B // DIL_W, out_dtypes=[F32], name="proj_b")
    proj_g = _mm(h1, w_in_t, mode="nt", tm=S, tn=D, n=2 * D, b_off=SEG_G // D, out_dtypes=[F32], name="proj_g")
    proj_f = _mm(h1, w_in_t, mode="nt", tm=S, tn=LANES, n=LANES, b_off=SEG_F // LANES, out_dtypes=[F32], name="proj_f")

    q_aug, k_aug = _fox_prepare(proj_a, proj_f, b_pad)
    oa, lse_a = _fox_fwd(q_aug, k_aug, proj_a)

    qkv3 = _rope_split(proj_b, tables)
    o3, lse3 = _dil_fwd(qkv3)
    ob, alpha = _dil_combine(o3, lse3)

    w_a, w_b, w_out = hooks.mixer_weights(ob)
    ya = _mm(oa, w_a, mode="nn", tm=S, tn=512, out_dtypes=[F32], name="branch_a")
    yb = _mm(ob, w_b, mode="nn", tm=S, tn=512, out_dtypes=[F32], name="branch_b")
    mixed = _gate_fwd(proj_g, ya, yb)

    def out_epilogue(acc, rows, vecs):
        x2 = rows[0] + acc
        return (x2, _rms_rows(x2, vecs[0])), ()

    x2, h2 = _mm_rows(mixed, w_out, tm=ROWS_TM, name="out_proj", epilogue=out_epilogue, rows=(x,), vecs=(g_mlp,),
                      row_out=(F32, BF16))
    w_up_sh, w_down = hooks.mlp_weights(h2)

    def up_epilogue(acc):
        r = jnp.maximum(acc, 0.0)
        return acc, r * r

    u, act = _mm(h2, w_up_sh, mode="nn", tm=S, tn=DFF // N_DEV, b_sharded=True, out_dtypes=[F32, BF16],
                 name="mlp_up", epilogue=up_epilogue)

    def loss_epilogue(acc, rows, vecs):
        dx3, dg, loss = _final_loss_rows(rows[0] + acc, vecs[0], rows[1])
        return (dx3,), (dg, loss)

    dx3, dg_final, loss = _mm_rows(act, w_down, tm=ROWS_TM, name="mlp_down_loss", epilogue=loss_epilogue,
                                   rows=(x2, tgt), vecs=(g_final,), row_out=(F32,), vec_out=(D, LANES))

    def rms_bwd_epilogue(acc, rows, vecs):
        dx, dg = _rms_bwd_rows(acc, rows[0], rows[1], vecs[0])
        return (dx,), (dg,)

    du = _mm(dx3, w_down, mode="nt", tm=S, tn=512, out_dtypes=[BF16], name="mlp_down_bwd",
             epilogue=lambda acc, u_t: (acc * (2.0 * jnp.maximum(u_t, 0.0)),), extras=(u,))
    dw_down = _mm(act, dx3, mode="tn", tm=512, tn=D, out_dtypes=[F32], name="dw_down")
    dw_up_sh = _mm(h2, du, mode="tn", tm=D, tn=DFF // N_DEV, out_sharded=True, out_dtypes=[F32], name="dw_up")
    dx2, dg_mlp = _mm_rows(du, w_up_sh, b_sharded=True, tm=ROWS_TM, name="mlp_up_bwd", epilogue=rms_bwd_epilogue,
                           rows=(x2, dx3), vecs=(g_mlp,), row_out=(F32,), vec_out=(D,),
                           deps=hooks.mlp_grads(dw_up_sh, dw_down))

    dmixed = _mm(dx2, w_out, mode="nt", tm=S, tn=512, out_dtypes=[F32], name="out_proj_bwd")
    dw_out = _mm(mixed, dx2, mode="tn", tm=D, tn=D, out_dtypes=[F32], name="dw_out")

    dyab, dproj = _gate_bwd(dmixed, proj_g, ya, yb)
    doa = _mm(dyab, w_a, mode="nt", tm=S, tn=FOX_W, k=D, a_off=0, out_dtypes=[BF16], name="branch_a_bwd")
    dw_a = _mm(oa, dyab, mode="tn", tm=FOX_W, tn=D, n=D, b_off=0, out_dtypes=[F32], name="dw_branch_a")
    dob = _mm(dyab, w_b, mode="nt", tm=S, tn=CB, k=D, a_off=1, out_dtypes=[F32], name="branch_b_bwd")
    dw_b = _mm(ob, dyab, mode="tn", tm=DIL_OUT, tn=D, n=D, b_off=1, out_dtypes=[F32], name="dw_branch_b")

    dproj, dft, dfs = _fox_bwd(q_aug, k_aug, proj_a, lse_a, doa, dproj, deps=hooks.mixer_grads(dw_a, dw_b, dw_out))
    dproj, db = _fox_dscan(dft, dfs, proj_f, b_pad, dproj)

    do3, c3 = _dil_combine_bwd(dob, ob, alpha, deps=hooks.mid_backward(db))
    dq3, dk3, dv3 = _dil_bwd(qkv3, lse3, do3, c3)
    dproj = _rope_merge_bwd(dq3, dk3, dv3, tables, dproj)

    dw_in_t = _mm(dproj, h1, mode="tn", tm=CB, tn=D, out_dtypes=[F32], name="dw_in")
    dx, dg_attn = _mm_rows(dproj, w_in_t, tm=ROWS_TM // 2, name="proj_bwd", epilogue=rms_bwd_epilogue,
                           rows=(x, dx2), vecs=(g_attn,), row_out=(F32,), vec_out=(D,),
                           deps=hooks.w_in_grad(dw_in_t))

    return loss, dx, (dg_attn, db, dg_mlp, dg_final)


SHARD_SHAPES = ((D_IN // N_DEV, D), (FOX_W, D // N_DEV), (DIL_OUT, D // N_DEV), (D // N_DEV, D),
                (D, DFF // N_DEV), (DFF // N_DEV, D))
W_NAMES = ("w_in", "w_a", "w_b", "w_out", "w_up", "w_down")
AG_COPIES = 7
HBM = pl.BlockSpec(memory_space=pltpu.HBM)
SEM = pl.BlockSpec(memory_space=pltpu.SEMAPHORE)


def _place():
    return lax.axis_index("x"), lax.axis_index("y"), lax.axis_index("c")


W_SLAB = 752
W_WIN = 272
PAD_BLK = 256


def _pad_runs():
    runs = sorted((pad, pad + b - a, p, a) for p in range(N_DEV) for a, b, pad in _shard_pieces(p))
    blocks = []
    for k in range(D_PAD // PAD_BLK):
        lo, hi = k * PAD_BLK, (k + 1) * PAD_BLK
        blocks.append([(max(lo, r0) - lo, min(hi, r1) - lo, p, a + max(lo, r0) - r0)
                       for r0, r1, p, a in runs if max(lo, r0) < min(hi, r1)])
    return blocks


def _gather_w_in(shard):
    blocks = _pad_runs()

    def body(x_ref, out_ref, stage, land, send_sems, recv_sems, load_sem):
        x, y, c = _place()
        me, sibling = (x, y, c), (x, y, 1 - c)
        chips = [(1 - x, y), (x, 1 - y), (1 - x, 1 - y)]

        def slot(px, py, pc):
            return land.at[4 * px + 2 * py + pc]

        def copy(k, block, to):
            return pltpu.make_async_remote_copy(
                src_ref=slot(*block), dst_ref=slot(*block), send_sem=send_sems.at[k], recv_sem=recv_sems.at[k],
                device_id=to, device_id_type=MESH)

        load = pltpu.make_async_copy(x_ref, stage, load_sem)
        load.start()
        load.wait()
        land[4 * x + 2 * y + c] = stage[...].astype(BF16)
        sent = [copy(0, me, sibling)] + [copy(1 + j, me, (*chip, c)) for j, chip in enumerate(chips)]
        for cp in sent:
            cp.start()
        for j, chip in enumerate(chips):
            copy(1 + j, (*chip, c), me).wait_recv()
            sent.append(copy(4 + j, (*chip, c), sibling))
            sent[-1].start()
        copy(0, sibling, me).wait_recv()
        for j, chip in enumerate(chips):
            copy(4 + j, (*chip, 1 - c), me).wait_recv()
        for cp in sent:
            cp.wait_send()

        row = lax.broadcasted_iota(jnp.int32, (PAD_BLK, D), 0)
        for k, runs in enumerate(blocks):
            out = jnp.zeros((PAD_BLK, D), F32)
            for o_lo, o_hi, p, a in runs:
                start = min(a // 16 * 16, W_SLAB - W_WIN)
                win = land[p, start:start + W_WIN, :].astype(F32)
                moved = pltpu.roll(win, (o_lo - (a - start)) % W_WIN, 0)[:PAD_BLK]
                out = jnp.where((row >= o_lo) & (row < o_hi), moved, out)
            out_ref[k * PAD_BLK:(k + 1) * PAD_BLK, :] = out.astype(BF16)

    return pl.pallas_call(
        body, name="all_gather_w_in",
        out_shape=jax.ShapeDtypeStruct((D_PAD, D), BF16),
        in_specs=[ANY], out_specs=pl.BlockSpec(memory_space=pltpu.VMEM),
        scratch_shapes=[pltpu.VMEM((W_SLAB, D), F32), pltpu.VMEM((N_DEV, W_SLAB, D), BF16),
                        pltpu.SemaphoreType.DMA((AG_COPIES,)), pltpu.SemaphoreType.DMA((AG_COPIES,)),
                        pltpu.SemaphoreType.DMA(())],
        compiler_params=_cparams(None),
    )(shard)


def _place_own(shards):
    n_w = len(shards)
    shapes = [t.shape for t in shards]

    def body(*refs):
        in_refs, out_refs = refs[:n_w], refs[n_w:2 * n_w]
        stage, cast = refs[2 * n_w:3 * n_w], refs[3 * n_w:4 * n_w]
        load_sems, store_sems = refs[4 * n_w:]
        x, y, c = _place()
        loads = [pltpu.make_async_copy(in_refs[i], stage[i], load_sems.at[i]) for i in range(n_w)]
        for ld in loads:
            ld.start()
        stores = []
        for i in range(n_w):
            loads[i].wait()
            cast[i][...] = stage[i][...].astype(BF16)
            stores.append(pltpu.make_async_copy(cast[i], out_refs[i].at[4 * x + 2 * y + c], store_sems.at[i]))
            stores[-1].start()
        for st in stores:
            st.wait()

    return pl.pallas_call(
        body, name="place_own_shards",
        out_shape=[jax.ShapeDtypeStruct((N_DEV,) + sh, BF16) for sh in shapes],
        in_specs=[ANY] * n_w, out_specs=[ANY] * n_w,
        scratch_shapes=[pltpu.VMEM(sh, F32) for sh in shapes] + [pltpu.VMEM(sh, BF16) for sh in shapes]
                       + [pltpu.SemaphoreType.DMA((n_w,)), pltpu.SemaphoreType.DMA((n_w,))],
        compiler_params=_cparams(None),
    )(*shards)


PEER_COPIES = N_DEV - 1
SIBLING_COPIES = 4
CHIP_COPIES = 3


def _peer_copies(_, land_refs, send_sems, recv_sems):
    x, y, c = _place()
    mine = 4 * x + 2 * y + c
    copies = []
    for i, ref in enumerate(land_refs):
        for k in range(1, N_DEV):
            peer = (x ^ (k >> 2), y ^ ((k >> 1) & 1), c ^ (k & 1))
            n = i * PEER_COPIES + k - 1
            copies.append(pltpu.make_async_remote_copy(
                src_ref=ref.at[mine], dst_ref=ref.at[mine], send_sem=send_sems.at[n], recv_sem=recv_sems.at[n],
                device_id=peer, device_id_type=MESH))
    return copies


def _sibling_copies(g_refs, r_refs, send_sems, recv_sems):
    x, y, c = _place()
    return [pltpu.make_async_remote_copy(
        src_ref=g_refs[i].at[2 * k + (1 - c)], dst_ref=r_refs[i].at[k],
        send_sem=send_sems.at[SIBLING_COPIES * i + k], recv_sem=recv_sems.at[SIBLING_COPIES * i + k],
        device_id=(x, y, 1 - c), device_id_type=MESH) for i in range(len(g_refs)) for k in range(SIBLING_COPIES)]


def _chip_copies(p_refs, r_refs, send_sems, recv_sems):
    x, y, c = _place()
    chips = [(1 - x, y), (x, 1 - y), (1 - x, 1 - y)]
    return [pltpu.make_async_remote_copy(
        src_ref=p_refs[i].at[2 * cx + cy], dst_ref=r_refs[i].at[j],
        send_sem=send_sems.at[CHIP_COPIES * i + j], recv_sem=recv_sems.at[CHIP_COPIES * i + j],
        device_id=(cx, cy, c), device_id_type=MESH) for i in range(len(p_refs)) for j, (cx, cy) in enumerate(chips)]


def _in_hbm(a):
    return pltpu.with_memory_space_constraint(a, pltpu.HBM)


def _exchange_start(name, copies_fn, n_copies, srcs, lands, after=()):
    n_s, n_l, n_a = len(srcs), len(lands), len(after)

    def body(*refs):
        outs = refs[n_s + n_l + n_a:]
        for cp in copies_fn(refs[:n_s], refs[n_s:n_s + n_l], outs[0], outs[1]):
            cp.start()
        outs[-1][...] = jnp.zeros_like(outs[-1])

    res = pl.pallas_call(
        body, name=name,
        out_shape=[pltpu.SemaphoreType.DMA((n_copies,)), pltpu.SemaphoreType.DMA((n_copies,))]
                  + [pltpu.HBM(t.shape, t.dtype) for t in (*srcs, *lands)] + [jax.ShapeDtypeStruct((8, LANES), F32)],
        in_specs=[HBM] * (n_s + n_l) + [ANY] * n_a,
        out_specs=[SEM, SEM] + [HBM] * (n_s + n_l) + [pl.BlockSpec(memory_space=pltpu.VMEM)],
        input_output_aliases={i: 2 + i for i in range(n_s + n_l)},
        compiler_params=pltpu.CompilerParams(has_side_effects=pltpu.SideEffectType.DATAFLOW_SIDE_EFFECTING),
    )(*[_in_hbm(t) for t in (*srcs, *lands)], *after)
    return res[0], res[1], list(res[2:2 + n_s]), list(res[2 + n_s:2 + n_s + n_l]), res[-1]


def _exchange_wait(name, copies_fn, started, after):
    send_sems, recv_sems, srcs, lands, _ = started
    n_s, n_l = len(srcs), len(lands)

    def body(*refs):
        for cp in copies_fn(refs[:n_s], refs[n_s:n_s + n_l], refs[n_s + n_l], refs[n_s + n_l + 1]):
            cp.wait_send()
            cp.wait_recv()

    res = pl.pallas_call(
        body, name=name,
        out_shape=[pltpu.HBM(t.shape, t.dtype) for t in (*srcs, *lands)],
        in_specs=[HBM] * (n_s + n_l) + [SEM, SEM, ANY],
        out_specs=[HBM] * (n_s + n_l),
        input_output_aliases={i: i for i in range(n_s + n_l)},
        compiler_params=pltpu.CompilerParams(has_side_effects=pltpu.SideEffectType.DATAFLOW_SIDE_EFFECTING),
    )(*srcs, *lands, send_sems, recv_sems, after)
    return list(res[:n_s]), list(res[n_s:])


def _shard_block(shape):
    rows, cols = shape
    if rows % 256:
        return rows, 256
    return min(rows, 512 if cols <= LANES else 256), cols


def _chip_partial(ids, g_stack, recv1, name):
    shape = g_stack.shape[1:]
    br, bc = _shard_block(shape)

    def body(ids_ref, g_ref, r_ref, pb_ref, own_ref):
        s = g_ref[...] + r_ref[...]
        pb_ref[...] = s.astype(BF16)

        @pl.when(pl.program_id(2) == ids_ref[1])
        def _():
            own_ref[...] = s

    grid_spec = pltpu.PrefetchScalarGridSpec(
        num_scalar_prefetch=1, grid=(shape[0] // br, shape[1] // bc, 4),
        in_specs=[pl.BlockSpec((None, br, bc), lambda r, q, k, ids: (2 * k + ids[0], r, q)),
                  pl.BlockSpec((None, br, bc), lambda r, q, k, ids: (k, r, q))],
        out_specs=[pl.BlockSpec((None, br, bc), lambda r, q, k, ids: (k, r, q)),
                   pl.BlockSpec((br, bc), lambda r, q, k, ids: (r, q))])
    return pl.pallas_call(
        body, name=name, grid_spec=grid_spec,
        out_shape=[jax.ShapeDtypeStruct((4,) + shape, BF16), jax.ShapeDtypeStruct(shape, F32)],
        compiler_params=_cparams(("parallel", "parallel", "arbitrary")),
    )(ids, g_stack, recv1)


def _adamw(w, g, m, v):
    m = ADAM_B1 * m + (1.0 - ADAM_B1) * g
    v = ADAM_B2 * v + (1.0 - ADAM_B2) * (g * g)
    m_hat = m / (1.0 - ADAM_B1 ** ADAM_STEP)
    v_hat = v / (1.0 - ADAM_B2 ** ADAM_STEP)
    delta = -ADAM_LR * (m_hat / (jnp.sqrt(v_hat) + ADAM_EPS) + ADAM_WD * w)
    return delta, m, v


def _reduce_adamw(own, recv2, w, m, v, name, deps=()):
    shape = own.shape
    br, bc = _shard_block(shape)

    def body(own_ref, r_ref, w_ref, m_ref, v_ref, *refs):
        g_ref, d_ref, nm_ref, nv_ref = refs[len(deps):]
        g = own_ref[...]
        for j in range(3):
            g = g + r_ref[j].astype(F32)
        delta, nm, nv = _adamw(w_ref[...], g, m_ref[...], v_ref[...])
        g_ref[...] = g
        d_ref[...] = delta
        nm_ref[...] = nm
        nv_ref[...] = nv

    blk = pl.BlockSpec((br, bc), lambda r, q: (r, q))
    return pl.pallas_call(
        body, name=name, grid=(shape[0] // br, shape[1] // bc),
        in_specs=[blk, pl.BlockSpec((3, br, bc), lambda r, q: (0, r, q)), blk, blk, blk] + [ANY] * len(deps),
        out_specs=[blk] * 4, out_shape=[jax.ShapeDtypeStruct(shape, F32)] * 4,
        compiler_params=_cparams(("parallel", "parallel")),
    )(own, recv2, w, m, v, *deps)


def _small_allreduce_adamw(gvec, w, m, v, deps=()):
    def body(g_ref, w_ref, m_ref, v_ref, *refs):
        go_ref, d_ref, nm_ref, nv_ref, buf, send_sems, recv_sems = refs[len(deps):]
        x, y, c = _place()
        my_slot = 4 * x + 2 * y + c
        buf[my_slot] = g_ref[...]
        copies = []
        for k in range(1, N_DEV):
            px, py, pc = x ^ (k >> 2), y ^ ((k >> 1) & 1), c ^ (k & 1)
            copies.append(pltpu.make_async_remote_copy(
                src_ref=g_ref, dst_ref=buf.at[my_slot], send_sem=send_sems.at[k - 1], recv_sem=recv_sems.at[k - 1],
                device_id=(px, py, pc), device_id_type=MESH))
        for cp in copies:
            cp.start()
        for k in range(1, N_DEV):
            px, py, pc = x ^ (k >> 2), y ^ ((k >> 1) & 1), c ^ (k & 1)
            pltpu.make_async_remote_copy(
                src_ref=g_ref, dst_ref=buf.at[4 * px + 2 * py + pc], send_sem=send_sems.at[k - 1],
                recv_sem=recv_sems.at[k - 1], device_id=(px, py, pc), device_id_type=MESH).wait_recv()
        for cp in copies:
            cp.wait_send()
        g = buf[0]
        for s in range(1, N_DEV):
            g = g + buf[s]
        delta, nm, nv = _adamw(w_ref[...], g, m_ref[...], v_ref[...])
        go_ref[...] = g
        d_ref[...] = delta
        nm_ref[...] = nm
        nv_ref[...] = nv

    vm = pl.BlockSpec(memory_space=pltpu.VMEM)
    return pl.pallas_call(
        body, name="small_allreduce_adamw",
        in_specs=[vm] * 4 + [ANY] * len(deps), out_specs=[vm] * 4,
        out_shape=[jax.ShapeDtypeStruct((SMALL_R, LANES), F32)] * 4,
        scratch_shapes=[pltpu.VMEM((N_DEV, SMALL_R, LANES), F32),
                        pltpu.SemaphoreType.DMA((N_DEV - 1,)), pltpu.SemaphoreType.DMA((N_DEV - 1,))],
    )(gvec, w, m, v, *deps)


def _cols_to_whole(stack):
    rows = stack.shape[1]
    return stack.transpose(1, 0, 2).reshape(rows, -1)


def _whole_to_cols(t):
    rows = t.shape[0]
    return t.reshape(rows, N_DEV, -1).transpose(1, 0, 2)


W_IN_SEGMENTS = ((0, 3 * FOX_W, SEG_A), (3 * FOX_W, 3 * FOX_W + FOX_H, SEG_F),
                 (3 * FOX_W + FOX_H, 3 * FOX_W + FOX_H + 3 * DIL_W, SEG_B), (3 * FOX_W + FOX_H + 3 * DIL_W, D_IN, SEG_G))


def _shard_pieces(p):
    rows = D_IN // N_DEV
    lo, hi = p * rows, (p + 1) * rows
    return [(max(lo, a) - lo, min(hi, b) - lo, pad + max(lo, a) - a)
            for a, b, pad in W_IN_SEGMENTS if max(lo, a) < min(hi, b)]


def _unpad_dw_in_t(dwp):
    return jnp.stack([jnp.concatenate([dwp[pad:pad + b - a] for a, b, pad in _shard_pieces(p)], axis=0)
                      for p in range(N_DEV)])


LOSS_ROW = 25


def _pack_small(g_attn, b, g_mlp, g_final, loss_row=None):
    tail = jnp.pad(b, ((0, 7), (0, LANES - b.shape[1])))
    if loss_row is not None:
        tail = tail + jnp.pad(loss_row, ((LOSS_ROW - 24, 31 - LOSS_ROW), (0, 0)))
    return jnp.concatenate([g_attn.reshape(8, LANES), g_mlp.reshape(8, LANES), g_final.reshape(8, LANES), tail], axis=0)


def _unpack_small(p):
    return (p[0:8].reshape(1, D), p[24:25, :FOX_H], p[8:16].reshape(1, D), p[16:24].reshape(D))


class _StepComm:
    def __init__(self, ids, w_sh, m_sh, v_sh, after_w_in):
        self.ids = ids
        self.w_sh, self.m_sh, self.v_sh = w_sh, m_sh, v_sh
        self.updates = None
        lands = _place_own(w_sh[1:])
        self.gather_mixer = _exchange_start("gather_mixer_start", _peer_copies, PEER_COPIES * 3, [], lands[:3],
                                            after=(after_w_in,))
        self.gather_mlp = _exchange_start("gather_mlp_start", _peer_copies, PEER_COPIES * 2, [], lands[3:],
                                          after=(self.gather_mixer[-1],))
        self.sibling = {}
        self.chips = {}
        self.own = {}
        self.names = {}

    def _reduce_start(self, group, names, grads, after=()):
        lands = [lax.empty((SIBLING_COPIES,) + t.shape[1:], F32) for t in grads]
        self.sibling[group] = _exchange_start("grad_%s_sibling_start" % group, _sibling_copies,
                                              SIBLING_COPIES * len(grads), grads, lands, after=after)
        self.names[group] = names
        return self.sibling[group][-1]

    def _reduce_mid(self, group, after):
        grads, recv1 = _exchange_wait("grad_%s_sibling_wait" % group, _sibling_copies, self.sibling[group], after)
        parts = [_chip_partial(self.ids, g, r, "grad_partial_" + n) for g, r, n in zip(grads, recv1, self.names[group])]
        self.own[group] = [p[1] for p in parts]
        srcs = [p[0] for p in parts]
        lands = [lax.empty((CHIP_COPIES,) + t.shape[1:], BF16) for t in srcs]
        self.chips[group] = _exchange_start("grad_%s_chips_start" % group, _chip_copies, CHIP_COPIES * len(srcs),
                                            srcs, lands)
        return self.chips[group][-1]

    def reduced(self, group, after):
        _, recv2 = _exchange_wait("grad_%s_chips_wait" % group, _chip_copies, self.chips[group], after)
        return list(zip(self.own[group], recv2))

    def first_deps(self):
        return [self.gather_mlp[-1]]

    def mixer_weights(self, after):
        _, (g_a, g_b, g_out) = _exchange_wait("gather_mixer_wait", _peer_copies, self.gather_mixer, after)
        return _cols_to_whole(g_a), _cols_to_whole(g_b), g_out.reshape(D, D)

    def mlp_weights(self, after):
        _, (g_up, g_down) = _exchange_wait("gather_mlp_wait", _peer_copies, self.gather_mlp, after)
        return g_up, g_down.reshape(DFF, D)

    def mlp_grads(self, dw_up_sh, dw_down):
        return [self._reduce_start("mlp", W_NAMES[4:], [dw_up_sh, dw_down.reshape((N_DEV,) + SHARD_SHAPES[5])])]

    def mixer_grads(self, dw_a, dw_b, dw_out):
        token = self._reduce_mid("mlp", dw_b)
        grads = [_whole_to_cols(dw_a), _whole_to_cols(dw_b), dw_out.reshape((N_DEV,) + SHARD_SHAPES[3])]
        return [self._reduce_start("mixer", W_NAMES[1:4], grads, after=(token,))]

    def mid_backward(self, after):
        return [self._reduce_mid("mixer", after)]

    def w_in_grad(self, dw_in_t):
        grads = [_unpad_dw_in_t(dw_in_t)]
        token = self._reduce_start("w_in", W_NAMES[:1], grads)
        reduced = self.reduced("mixer", token) + self.reduced("mlp", token)
        self.updates, last = [None] * len(reduced), token
        for i in (3, 4, 0, 1, 2):
            if i == 0:
                last = self._reduce_mid("w_in", last)
            self.updates[i] = _reduce_adamw(*reduced[i], self.w_sh[1 + i], self.m_sh[1 + i], self.v_sh[1 + i],
                                            "adamw_" + W_NAMES[1 + i], deps=[last])
            last = self.updates[i][0]
        return [last]

    def w_in_update(self, after):
        (reduced,) = self.reduced("w_in", after)
        return _reduce_adamw(*reduced, self.w_sh[0], self.m_sh[0], self.v_sh[0], "adamw_" + W_NAMES[0])


def kernel(x, norm_attn_g, w_in, b_forget, w_branch_a, w_branch_b, w_out, norm_mlp_g, w_up, w_down, norm_final_g, loss_target, m_norm_attn_g, m_w_in, m_b_forget, m_w_branch_a, m_w_branch_b, m_w_out, m_norm_mlp_g, m_w_up, m_w_down, m_norm_final_g, v_norm_attn_g, v_w_in, v_b_forget, v_w_branch_a, v_w_branch_b, v_w_out, v_norm_mlp_g, v_w_up, v_w_down, v_norm_final_g):
    cx, cy, cc = _place()
    ids = jnp.stack([cc, 2 * cx + cy]).astype(jnp.int32)

    w_sh = [w_in[0].T] + [t[0] for t in (w_branch_a, w_branch_b, w_out, w_up, w_down)]
    m_sh = [m_w_in[0].T] + [t[0] for t in (m_w_branch_a, m_w_branch_b, m_w_out, m_w_up, m_w_down)]
    v_sh = [v_w_in[0].T] + [t[0] for t in (v_w_branch_a, v_w_branch_b, v_w_out, v_w_up, v_w_down)]

    w_in_t = _gather_w_in(jnp.pad(w_sh[0], ((0, W_SLAB - w_sh[0].shape[0]), (0, 0))))
    comm = _StepComm(ids, w_sh, m_sh, v_sh, w_in_t)
    b_pad = jnp.pad(b_forget, ((0, 0), (0, LANES - FOX_H)))

    loss_row, dx, dsmall = _local_step(
        x[0], loss_target[0], norm_attn_g, norm_mlp_g, norm_final_g.reshape(1, D), b_pad, w_in_t, comm)

    dg_attn, db, dg_mlp, dg_final = dsmall
    small = _small_allreduce_adamw(
        _pack_small(dg_attn, db, dg_mlp, dg_final, loss_row),
        _pack_small(norm_attn_g, b_forget, norm_mlp_g, norm_final_g.reshape(1, D)),
        _pack_small(m_norm_attn_g, m_b_forget, m_norm_mlp_g, m_norm_final_g.reshape(1, D)),
        _pack_small(v_norm_attn_g, v_b_forget, v_norm_mlp_g, v_norm_final_g.reshape(1, D)))
    big = [comm.w_in_update(small[0])] + comm.updates

    outs = [small[0][LOSS_ROW, 0], dx[None]]
    for q in range(4):
        s_attn, s_b, s_mlp, s_final = _unpack_small(small[q])
        b_in, b_a, b_b, b_out, b_up, b_down = [t[q][None] for t in big]
        b_in = jnp.swapaxes(b_in, 1, 2)
        outs += [s_attn, b_in, s_b, b_a, b_b, b_out, s_mlp, b_up, b_down, s_final]
    return tuple(outs)
```

```python
import functools

import jax
import jax.numpy as jnp
from jax import lax
from jax.experimental import pallas as pl
from jax.experimental.pallas import tpu as pltpu

F32 = jnp.float32
BF16 = jnp.bfloat16
MESH = pl.DeviceIdType.MESH

S = 2048
D = 1024
HD = 64
FOX_H = 8
FOX_W = FOX_H * HD
DIL_HG = 4
DIL_G = 3
DIL_W = DIL_G * DIL_HG * HD
DIL_OUT = DIL_HG * HD
DIL_BLK = 128
DIL_R = (1, 4, 16)
DFF = 4 * D
D_IN = 3 * FOX_W + FOX_H + 3 * DIL_W + 2 * D
EPS = 1e-6
NEG_INF = -1e30
SCALE = HD ** -0.5
ROPE_THETA = 500000.0
ROPE_DIM = HD // 4
N_DEV = 8

ADAM_LR = 0.001
ADAM_B1 = 0.9
ADAM_B2 = 0.999
ADAM_EPS = 1e-08
ADAM_WD = 0.01
ADAM_STEP = 10

LANES = 128
CB = 256
SEG_A = 0
SEG_B = 3 * FOX_W
SEG_Z = SEG_B + 3 * DIL_W
SEG_G = SEG_Z + CB
SEG_F = SEG_G + 2 * D
D_PAD = SEG_F + CB
SMALL_R = 32

VMEM_MB = 56


def _cparams(dims=None, vmem_mb=VMEM_MB, **kw):
    return pltpu.CompilerParams(dimension_semantics=dims, vmem_limit_bytes=vmem_mb << 20, **kw)


ANY = pl.BlockSpec(memory_space=pl.ANY)

_NN = (((1,), (0,)), ((), ()))
_NT = (((1,), (1,)), ((), ()))
_TN = (((0,), (0,)), ((), ()))


def _dot(a, b, dims):
    return lax.dot_general(a.astype(BF16), b.astype(BF16), dims, preferred_element_type=F32)


def _mm(a, b, *, mode, tm, tn, out_dtypes, name, n=None, k=None, a_off=0, b_off=0,
        b_sharded=False, out_sharded=False, epilogue=None, extras=(), deps=()):
    n_sh = b.shape[-1] if b_sharded else None
    if mode == "nn":
        m = a.shape[0]
        k = k or a.shape[1]
        a_spec = pl.BlockSpec((tm, k), lambda i, j: (i, a_off))
        if b_sharded:
            assert tn == n_sh
            n = N_DEV * n_sh
            b_spec = pl.BlockSpec((None, k, tn), lambda i, j: (j, 0, 0))
        else:
            n = n or b.shape[1]
            b_spec = pl.BlockSpec((k, tn), lambda i, j: (0, j + b_off))
        dims = _NN
    elif mode == "nt":
        m = a.shape[0]
        k = k or a.shape[1]
        a_spec = pl.BlockSpec((tm, k), lambda i, j: (i, a_off))
        if b_sharded:
            n = b.shape[1]
            b_spec = pl.BlockSpec((N_DEV, tn, n_sh), lambda i, j: (0, j, 0))
        else:
            n = n or b.shape[0]
            b_spec = pl.BlockSpec((tn, k), lambda i, j: (j + b_off, 0))
        dims = _NT
    else:
        k, m = a.shape
        n = n or b.shape[1]
        a_spec = pl.BlockSpec((k, tm), lambda i, j: (0, i))
        b_spec = pl.BlockSpec((k, tn), lambda i, j: (0, j + b_off))
        dims = _TN
    assert m % tm == 0 and n % tn == 0, (name, m, n, tm, tn)
    n_extra = len(extras)
    tile = pl.BlockSpec((tm, tn), lambda i, j: (i, j))
    if out_sharded:
        assert mode == "tn" and n // tn == N_DEV
        out_spec = pl.BlockSpec((None, tm, tn), lambda i, j: (j, i, 0))
        out_shape = (N_DEV, m, tn)
    else:
        out_spec, out_shape = tile, (m, n)

    def body(a_ref, b_ref, *refs):
        if mode == "nt" and b_sharded:
            acc = _dot(a_ref[:, 0:n_sh], b_ref[0], dims)
            for p in range(1, N_DEV):
                acc = acc + _dot(a_ref[:, p * n_sh:(p + 1) * n_sh], b_ref[p], dims)
        else:
            acc = _dot(a_ref[...], b_ref[...], dims)
        ex = [r[...] for r in refs[:n_extra]]
        outs = epilogue(acc, *ex) if epilogue is not None else (acc,)
        for o_ref, o in zip(refs[n_extra + len(deps):], outs):
            o_ref[...] = o.astype(o_ref.dtype)

    res = pl.pallas_call(
        body, name=name, grid=(m // tm, n // tn),
        in_specs=[a_spec, b_spec] + [tile] * n_extra + [ANY] * len(deps),
        out_specs=[out_spec] * len(out_dtypes),
        out_shape=[jax.ShapeDtypeStruct(out_shape, dt) for dt in out_dtypes],
        compiler_params=_cparams(("parallel", "parallel")),
    )(a, b, *extras, *deps)
    return res if len(out_dtypes) > 1 else res[0]


def _mm_rows(a, b, *, tm, name, epilogue, rows=(), vecs=(), row_out=(), vec_out=(), b_sharded=False, deps=()):
    m, k = a.shape
    n_rows, n_vecs, n_deps = len(rows), len(vecs), len(deps)
    n_sh = b.shape[-1] if b_sharded else None

    def body(a_ref, b_ref, *refs):
        row_refs, vec_refs = refs[:n_rows], refs[n_rows:n_rows + n_vecs]
        outs = refs[n_rows + n_vecs + n_deps:]
        if b_sharded:
            acc = _dot(a_ref[:, 0:n_sh], b_ref[0], _NT)
            for p in range(1, N_DEV):
                acc = acc + _dot(a_ref[:, p * n_sh:(p + 1) * n_sh], b_ref[p], _NT)
        else:
            acc = _dot(a_ref[...], b_ref[...], _NN)
        row_vals, vec_incs = epilogue(acc, [r[...] for r in row_refs], [v[...] for v in vec_refs])
        for o_ref, val in zip(outs[:len(row_out)], row_vals):
            o_ref[...] = val.astype(o_ref.dtype)

        @pl.when(pl.program_id(0) == 0)
        def _():
            for o_ref in outs[len(row_out):]:
                o_ref[...] = jnp.zeros_like(o_ref)

        for o_ref, inc in zip(outs[len(row_out):], vec_incs):
            o_ref[...] += inc

    tile = pl.BlockSpec((tm, D), lambda i: (i, 0))
    b_spec = pl.BlockSpec(b.shape, lambda i: (0,) * b.ndim)
    return pl.pallas_call(
        body, name=name, grid=(m // tm,),
        in_specs=[pl.BlockSpec((tm, k), lambda i: (i, 0)), b_spec] + [tile] * n_rows
                 + [pl.BlockSpec(v.shape, lambda i: (0, 0)) for v in vecs] + [ANY] * n_deps,
        out_specs=[tile] * len(row_out) + [pl.BlockSpec((1, w), lambda i: (0, 0)) for w in vec_out],
        out_shape=[jax.ShapeDtypeStruct((m, D), dt) for dt in row_out]
                  + [jax.ShapeDtypeStruct((1, w), F32) for w in vec_out],
        compiler_params=_cparams(("arbitrary",)),
    )(a, b, *rows, *vecs, *deps)


ROW_T = 256
ROWS_TM = 512


def _rms_rows(x, g):
    r = lax.rsqrt(jnp.mean(x * x, axis=-1, keepdims=True) + EPS)
    return (x * r) * g


def _rms_bwd_rows(dh, x, dres, g):
    r = lax.rsqrt(jnp.mean(x * x, axis=-1, keepdims=True) + EPS)
    xn = x * r
    dhn = dh * g
    dx = dres + r * (dhn - xn * jnp.mean(dhn * xn, axis=-1, keepdims=True))
    return dx, jnp.sum(dh * xn, axis=0, keepdims=True)


def _final_loss_rows(x, g, tgt):
    r = lax.rsqrt(jnp.mean(x * x, axis=-1, keepdims=True) + EPS)
    xn = x * r
    err = xn * g - tgt
    row_loss = jnp.mean(err * err, axis=-1, keepdims=True)
    loss = 0.5 * jnp.sum(row_loss, axis=0, keepdims=True) * jnp.ones((1, LANES), F32)
    dy = err * (1.0 / D)
    dyn = dy * g
    dx = r * (dyn - xn * jnp.mean(dyn * xn, axis=-1, keepdims=True))
    return dx, jnp.sum(dy * xn, axis=0, keepdims=True), loss


def _sigmoid(z):
    return 1.0 / (1.0 + jnp.exp(-z))


def _gate_fwd(proj_g, ya, yb):
    def body(ga_ref, gb_ref, ya_ref, yb_ref, o_ref):
        o_ref[...] = (_sigmoid(ga_ref[...]) * ya_ref[...] + _sigmoid(gb_ref[...]) * yb_ref[...]).astype(BF16)

    row = pl.BlockSpec((ROW_T, D), lambda i: (i, 0))
    return pl.pallas_call(
        body, name="gate_fwd", grid=(S // ROW_T,),
        in_specs=[row, pl.BlockSpec((ROW_T, D), lambda i: (i, 1)), row, row],
        out_specs=row, out_shape=jax.ShapeDtypeStruct((S, D), BF16),
        compiler_params=_cparams(("parallel",)),
    )(proj_g, proj_g, ya, yb)


GATE_TR = 1024
GATE_TC = 512
GATE_NB = 2 * D // GATE_TC


def _gate_bwd(dmixed, proj_g, ya, yb):
    half = GATE_NB // 2

    def body(dm_ref, g_ref, ya_ref, yb_ref, dy_ref, dg_ref):
        dm = dm_ref[...]
        s = _sigmoid(g_ref[...])
        y = jnp.where(pl.program_id(1) < half, ya_ref[...], yb_ref[...])
        dy_ref[...] = (dm * s).astype(BF16)
        dg_ref[...] = (dm * y * (s * (1.0 - s))).astype(BF16)

    blk = pl.BlockSpec((GATE_TR, GATE_TC), lambda i, j: (i, j))
    wrapped = pl.BlockSpec((GATE_TR, GATE_TC), lambda i, j: (i, j % half))
    return pl.pallas_call(
        body, name="gate_bwd", grid=(S // GATE_TR, GATE_NB),
        in_specs=[wrapped, blk, wrapped, wrapped],
        out_specs=[blk, pl.BlockSpec((GATE_TR, GATE_TC), lambda i, j: (i, j + SEG_G // GATE_TC))],
        out_shape=[jax.ShapeDtypeStruct((S, 2 * D), BF16), jax.ShapeDtypeStruct((S, D_PAD), BF16)],
        compiler_params=_cparams(("parallel", "parallel")),
    )(dmixed, proj_g, ya, yb)


FOX_TQ = 256


def _scan_rows(x, reverse):
    n = x.shape[0]
    row = lax.broadcasted_iota(jnp.int32, x.shape, 0)
    k = 1
    while k < n:
        if reverse:
            x = x + jnp.where(row < n - k, pltpu.roll(x, n - k, 0), 0.0)
        else:
            x = x + jnp.where(row >= k, pltpu.roll(x, k, 0), 0.0)
        k *= 2
    return x


def _fox_dscan(dft, dfs, proj_f, b_pad, dproj):
    def body(dft_ref, dfs_ref, f_ref, b_ref, _, dfa_ref, db_ref):
        dfs_pad = jnp.concatenate([dfs_ref[...], jnp.zeros((LANES - FOX_H, S), F32)], axis=0)
        df = dfs_pad.T + dft_ref[0]
        for hp in range(1, dft_ref.shape[0]):
            df = df + pltpu.roll(dft_ref[hp], 2 * hp, 1)
        dlf = _scan_rows(df, reverse=True)
        z = f_ref[...] + b_ref[...]
        lane = lax.broadcasted_iota(jnp.int32, (S, LANES), 1)
        dfa = jnp.where(lane < FOX_H, dlf / (1.0 + jnp.exp(z)), 0.0)
        dfa_ref[:, :LANES] = dfa.astype(BF16)
        dfa_ref[:, LANES:] = jnp.zeros((S, CB - LANES), BF16)
        db_ref[...] = jnp.sum(dfa, axis=0, keepdims=True)

    return pl.pallas_call(
        body, name="fox_dscan", grid=(1,),
        in_specs=[pl.BlockSpec(dft.shape, lambda i: (0, 0, 0)), pl.BlockSpec((FOX_H, S), lambda i: (0, 0)),
                  pl.BlockSpec((S, LANES), lambda i: (0, 0)), pl.BlockSpec((1, LANES), lambda i: (0, 0)), ANY],
        out_specs=[pl.BlockSpec((S, CB), lambda i: (0, SEG_F // CB)), pl.BlockSpec((1, LANES), lambda i: (0, 0))],
        out_shape=[jax.ShapeDtypeStruct((S, D_PAD), BF16), jax.ShapeDtypeStruct((1, LANES), F32)],
        input_output_aliases={4: 0},
        compiler_params=_cparams(("arbitrary",)),
    )(dft, dfs, proj_f, b_pad, dproj)


FOX_NQ = S // FOX_TQ
FOX_HP = FOX_W // LANES
HEADS_PER_LB = LANES // HD
FOX_AUG = 2 * LANES


def _fox_prepare(proj_a, proj_f, b_pad):
    tr = 512

    def body(q_ref, k_ref, f_ref, b_ref, qa_ref, ka_ref, hi_s, mid_s, lo_s):
        i = pl.program_id(0)

        @pl.when(i == 0)
        def _():
            z = f_ref[...] + b_ref[...]
            lf = jnp.minimum(z, 0.0) - jnp.log1p(jnp.exp(-jnp.abs(z)))
            f = _scan_rows(lf, reverse=False)
            hi = f.astype(BF16).astype(F32)
            r1 = f - hi
            mid = r1.astype(BF16).astype(F32)
            hi_s[...] = hi
            mid_s[...] = mid
            lo_s[...] = (r1 - mid).astype(BF16).astype(F32)

        rows = pl.ds(pl.multiple_of(i * tr, tr), tr)
        hi, mid, lo = hi_s[rows, :], mid_s[rows, :], lo_s[rows, :]
        lane = lax.broadcasted_iota(jnp.int32, (tr, HD), 1)
        q_ones = jnp.where((lane >= 3) & (lane < 6), 1.0, 0.0)
        k_ones = jnp.where(lane < 3, 1.0, 0.0)
        for h in range(FOX_H):
            a1, a2, a3 = hi[:, h:h + 1], mid[:, h:h + 1], lo[:, h:h + 1]
            q_extra = jnp.where(lane == 0, a1, jnp.where(lane == 1, a2, jnp.where(lane == 2, a3, q_ones)))
            k_extra = jnp.where(lane == 3, -a1, jnp.where(lane == 4, -a2, jnp.where(lane == 5, -a3, k_ones)))
            base = h * LANES
            qa_ref[:, base:base + HD] = (q_ref[:, h * HD:(h + 1) * HD].astype(F32) * SCALE).astype(BF16)
            qa_ref[:, base + HD:base + LANES] = q_extra.astype(BF16)
            ka_ref[:, base:base + HD] = k_ref[:, h * HD:(h + 1) * HD]
            ka_ref[:, base + HD:base + LANES] = k_extra.astype(BF16)

    out_blk = pl.BlockSpec((tr, FOX_H * LANES), lambda i: (i, 0))
    return pl.pallas_call(
        body, name="fox_prepare", grid=(S // tr,),
        in_specs=[pl.BlockSpec((tr, FOX_W), lambda i: (i, 0)), pl.BlockSpec((tr, FOX_W), lambda i: (i, 1)),
                  pl.BlockSpec((S, LANES), lambda i: (0, 0)), pl.BlockSpec((1, LANES), lambda i: (0, 0))],
        out_specs=[out_blk, out_blk],
        out_shape=[jax.ShapeDtypeStruct((S, FOX_H * LANES), BF16)] * 2,
        scratch_shapes=[pltpu.VMEM((S, LANES), F32)] * 3,
        compiler_params=_cparams(("arbitrary",)),
    )(proj_a, proj_a, proj_f, b_pad)


def _fox_scores(qa, ka, w):
    s = _dot(qa, ka, _NT)
    row = lax.broadcasted_iota(jnp.int32, (FOX_TQ, FOX_TQ), 0)
    col = lax.broadcasted_iota(jnp.int32, (FOX_TQ, FOX_TQ), 1)
    diag = jnp.where(col <= row, s[:, w * FOX_TQ:], NEG_INF)
    return diag if w == 0 else jnp.concatenate([s[:, :w * FOX_TQ], diag], axis=1)


def _fox_fwd(q_aug, k_aug, proj_a):
    def body(qa_ref, ka_ref, v_ref, o_ref, lse_ref):
        qi = pl.program_id(1)
        lane = lax.broadcasted_iota(jnp.int32, (FOX_TQ, LANES), 1)
        for w in range(FOX_NQ):
            @pl.when(qi == w)
            def _(w=w):
                width = (w + 1) * FOX_TQ
                lse = jnp.zeros((FOX_TQ, LANES), F32)
                for hh in range(HEADS_PER_LB):
                    aug = slice(hh * LANES, (hh + 1) * LANES)
                    sl = slice(hh * HD, (hh + 1) * HD)
                    s = _fox_scores(qa_ref[:, aug], ka_ref[:width, aug], w)
                    m = jnp.max(s, axis=-1, keepdims=True)
                    p = jnp.exp(s - m)
                    l = jnp.sum(p, axis=-1, keepdims=True)
                    o_ref[:, sl] = _dot(p / l, v_ref[:width, sl], _NN)
                    lse = jnp.where(lane == hh, m + jnp.log(l), lse)
                lse_ref[...] = lse

    return pl.pallas_call(
        body, name="fox_fwd", grid=(FOX_HP, FOX_NQ),
        in_specs=[pl.BlockSpec((FOX_TQ, FOX_AUG), lambda h, i: (i, h)),
                  pl.BlockSpec((S, FOX_AUG), lambda h, i: (0, h)),
                  pl.BlockSpec((S, LANES), lambda h, i: (0, 2 * FOX_HP + h))],
        out_specs=[pl.BlockSpec((FOX_TQ, LANES), lambda h, i: (i, h)),
                   pl.BlockSpec((None, FOX_TQ, LANES), lambda h, i: (h, i, 0))],
        out_shape=[jax.ShapeDtypeStruct((S, FOX_W), F32), jax.ShapeDtypeStruct((FOX_HP, S, LANES), F32)],
        compiler_params=_cparams(("parallel", "parallel")),
    )(q_aug, k_aug, proj_a)


def _fox_bwd(q_aug, k_aug, proj_a, lse, do, dproj, deps=()):
    n_q = FOX_NQ

    def body(qa_ref, ka_ref, v_ref, lse_ref, do_ref, *refs):
        dproj_ref, dft_ref, dfs_ref, dk_acc, dv_acc, dq_buf, kv_buf, dq_sems, kv_sems = refs[1 + len(deps):]
        hp = pl.program_id(0)
        t = pl.program_id(1)
        slot = t % 2
        col = pl.multiple_of(hp * LANES, LANES)

        def dq_copy(step, buf_slot):
            rows = pl.ds(pl.multiple_of(step * FOX_TQ, FOX_TQ), FOX_TQ)
            return pltpu.make_async_copy(dq_buf.at[buf_slot], dproj_ref.at[rows, pl.ds(col, LANES)], dq_sems.at[buf_slot])

        @pl.when(t == 0)
        def _():
            dk_acc[...] = jnp.zeros_like(dk_acc)
            dv_acc[...] = jnp.zeros_like(dv_acc)

        @pl.when((t == 0) & (hp == 0))
        def _():
            dfs_ref[...] = jnp.zeros_like(dfs_ref)

        @pl.when(t >= 2)
        def _():
            dq_copy(t - 2, slot).wait()

        lane = lax.broadcasted_iota(jnp.int32, (FOX_TQ, LANES), 1)
        for w in range(n_q):
            @pl.when(t == w)
            def _(w=w):
                width = (w + 1) * FOX_TQ
                lse_all = lse_ref[...]
                sub = lax.broadcasted_iota(jnp.int32, (FOX_H, width), 0)
                dft = jnp.zeros((FOX_TQ, LANES), F32)
                for hh in range(HEADS_PER_LB):
                    aug = slice(hh * LANES, (hh + 1) * LANES)
                    sl = slice(hh * HD, (hh + 1) * HD)
                    head = hp * HEADS_PER_LB + hh
                    qa = qa_ref[:, aug]
                    ka = ka_ref[:width, aug]
                    do_h = do_ref[:, sl]
                    s = _fox_scores(qa, ka, w)
                    p = jnp.exp(s - lse_all[:, hh:hh + 1])
                    dp = _dot(do_h, v_ref[:width, sl], _NT)
                    ds = p * (dp - jnp.sum(dp * p, axis=-1, keepdims=True))
                    dft = jnp.where(lane == hh, jnp.sum(ds, axis=-1, keepdims=True), dft)
                    dfs_ref[:, :width] -= jnp.where(sub == head, jnp.sum(ds, axis=0, keepdims=True), 0.0)
                    dq_buf[slot, :, sl] = (_dot(ds, ka[:, :HD], _NN) * SCALE).astype(BF16)
                    dk_acc[:width, sl] += _dot(ds, qa[:, :HD], _TN)
                    dv_acc[:width, sl] += _dot(p, do_h, _TN)
                dft_ref[...] = dft

        dq_copy(t, slot).start()

        @pl.when(t == n_q - 1)
        def _():
            dq_copy(t - 1, 1 - slot).wait()
            dq_copy(t, slot).wait()
            kv_buf[0] = dk_acc[...].astype(BF16)
            kv_buf[1] = dv_acc[...].astype(BF16)
            copies = [pltpu.make_async_copy(
                kv_buf.at[j], dproj_ref.at[:, pl.ds(pl.multiple_of((j + 1) * FOX_W + hp * LANES, LANES), LANES)],
                kv_sems.at[j]) for j in range(2)]
            for cp in copies:
                cp.start()
            for cp in copies:
                cp.wait()

    qblk = pl.BlockSpec((FOX_TQ, FOX_AUG), lambda h, t: (t, h))
    lane_blk = pl.BlockSpec((None, FOX_TQ, LANES), lambda h, t: (h, t, 0))
    return pl.pallas_call(
        body, name="fox_bwd", grid=(FOX_HP, n_q),
        in_specs=[qblk, pl.BlockSpec((S, FOX_AUG), lambda h, t: (0, h)),
                  pl.BlockSpec((S, LANES), lambda h, t: (0, 2 * FOX_HP + h)),
                  lane_blk, pl.BlockSpec((FOX_TQ, LANES), lambda h, t: (t, h)), ANY] + [ANY] * len(deps),
        out_specs=[ANY, lane_blk, pl.BlockSpec((FOX_H, S), lambda h, t: (0, 0))],
        out_shape=[jax.ShapeDtypeStruct((S, D_PAD), BF16),
                   jax.ShapeDtypeStruct((FOX_HP, S, LANES), F32), jax.ShapeDtypeStruct((FOX_H, S), F32)],
        scratch_shapes=[pltpu.VMEM((S, LANES), F32), pltpu.VMEM((S, LANES), F32),
                        pltpu.VMEM((2, FOX_TQ, LANES), BF16), pltpu.VMEM((2, S, LANES), BF16),
                        pltpu.SemaphoreType.DMA((2,)), pltpu.SemaphoreType.DMA((2,))],
        input_output_aliases={5: 0},
        compiler_params=_cparams(("arbitrary", "arbitrary")),
    )(q_aug, k_aug, proj_a, lse, do, dproj, *deps)


def _rope_tables():
    half = ROPE_DIM // 2
    shape = (DIL_G, S, half)
    g = lax.broadcasted_iota(jnp.int32, shape, 0)
    row = lax.broadcasted_iota(jnp.int32, shape, 1)
    r = jnp.left_shift(1, 2 * g)
    per_class = S // r
    pos = (row % per_class) * r + row // per_class
    inv_freq = jnp.power(jnp.float32(ROPE_THETA), -lax.broadcasted_iota(F32, shape, 2) * 2.0 / ROPE_DIM)
    ang = pos.astype(F32) * inv_freq
    cos = jnp.tile(jnp.cos(ang), (1, 1, LANES // half))
    sin = jnp.tile(jnp.sin(ang), (1, 1, LANES // half))
    lane = lax.broadcasted_iota(jnp.int32, (DIL_G, S, LANES), 2) % HD
    t_self = jnp.where(lane < ROPE_DIM, cos, 1.0)
    t_up = jnp.where(lane < half, -sin, 0.0)
    t_dn = jnp.where((lane >= half) & (lane < ROPE_DIM), sin, 0.0)
    return t_self, t_up, t_dn


def _residue_pieces(r):
    if r == 1:
        return [(slice(i, i + 512), slice(i, i + 512)) for i in range(0, S, 512)]
    n = S // r
    return [(pl.ds(j, n, stride=r), slice(j * n, (j + 1) * n)) for j in range(r)]


def _rope_apply(x, a, u, d, backward):
    half = ROPE_DIM // 2
    if backward:
        return x * a + pltpu.roll(x * u, half, 1) + pltpu.roll(x * d, LANES - half, 1)
    return x * a + pltpu.roll(x, LANES - half, 1) * u + pltpu.roll(x, half, 1) * d


LB_PER_CB = CB // LANES
ROPE_NB = 3 * DIL_G * LB_PER_CB


def _rope_step(s):
    per_group = 3 * LB_PER_CB
    return s // per_group, (s % per_group) // LB_PER_CB, s % LB_PER_CB


def _rope_split(proj_b, tables):
    def body(x_ref, a_ref, u_ref, d_ref, o_ref):
        g, t, _ = _rope_step(pl.program_id(0))
        for gs in range(DIL_G):
            @pl.when(g == gs)
            def _(gs=gs):
                for tok, sub in _residue_pieces(DIL_R[gs]):
                    x = x_ref[tok, :]
                    y = _rope_apply(x, a_ref[sub, :], u_ref[sub, :], d_ref[sub, :], False)
                    o_ref[sub, :] = jnp.where(t < 2, y, x).astype(BF16)

    def in_index(s):
        g, t, hf = _rope_step(s)
        return 0, (t * DIL_G + g) * LB_PER_CB + hf

    def out_index(s):
        g, t, hf = _rope_step(s)
        return t * DIL_G + g, 0, hf

    tab = pl.BlockSpec((None, S, LANES), lambda s: (_rope_step(s)[0], 0, 0))
    return pl.pallas_call(
        body, name="rope_split", grid=(ROPE_NB,),
        in_specs=[pl.BlockSpec((S, LANES), in_index), tab, tab, tab],
        out_specs=pl.BlockSpec((None, S, LANES), out_index),
        out_shape=jax.ShapeDtypeStruct((3 * DIL_G, S, CB), BF16),
        compiler_params=_cparams(("parallel",)),
    )(proj_b, *tables)


def _rope_merge_bwd(dq3, dk3, dv3, tables, dproj):
    def body(dq_ref, dk_ref, dv_ref, a_ref, u_ref, d_ref, _, o_ref, tmp):
        s = pl.program_id(0)
        g, t, _ = _rope_step(s)
        for gs in range(DIL_G):
            @pl.when((g == gs) & (s < ROPE_NB))
            def _(gs=gs):
                for tok, sub in _residue_pieces(DIL_R[gs]):
                    x = jnp.where(t == 0, dq_ref[sub, :], jnp.where(t == 1, dk_ref[sub, :], dv_ref[sub, :]))
                    y = _rope_apply(x, a_ref[sub, :], u_ref[sub, :], d_ref[sub, :], True)
                    tmp[tok, :] = jnp.where(t < 2, y, x)
                o_ref[...] = tmp[...].astype(BF16)

        @pl.when(s >= ROPE_NB)
        def _():
            o_ref[...] = jnp.zeros_like(o_ref)

    def src_spec(own):
        def index(s):
            g, t, hf = _rope_step(jnp.minimum(s, ROPE_NB - 1))
            return g, 0, jnp.where(t == own, hf, jnp.where(t > own, LB_PER_CB - 1, 0))
        return pl.BlockSpec((None, S, LANES), index)

    def out_index(s):
        g, t, hf = _rope_step(s)
        col = SEG_B // LANES + (t * DIL_G + g) * LB_PER_CB + hf
        return 0, jnp.where(s < ROPE_NB, col, SEG_Z // LANES + s - ROPE_NB)

    tab = pl.BlockSpec((None, S, LANES), lambda s: (_rope_step(jnp.minimum(s, ROPE_NB - 1))[0], 0, 0))
    return pl.pallas_call(
        body, name="rope_merge_bwd", grid=(ROPE_NB + LB_PER_CB,),
        in_specs=[src_spec(0), src_spec(1), src_spec(2), tab, tab, tab, ANY],
        out_specs=pl.BlockSpec((S, LANES), out_index),
        out_shape=jax.ShapeDtypeStruct((S, D_PAD), BF16),
        scratch_shapes=[pltpu.VMEM((S, LANES), F32)],
        input_output_aliases={6: 0},
        compiler_params=_cparams(("arbitrary",)),
    )(dq3, dk3, dv3, *tables, dproj)


DIL_NB = S // DIL_BLK


_BQK = (((2,), (2,)), ((0,), (0,)))
_BQD = (((2,), (1,)), ((0,), (0,)))
_BKD = (((1,), (1,)), ((0,), (0,)))


def _dil_mask(g):
    shape = (DIL_NB, DIL_BLK, 2 * DIL_BLK)
    blocks_per_seq = jnp.right_shift(DIL_NB, 2 * g)
    has_prev = jnp.bitwise_and(lax.broadcasted_iota(jnp.int32, shape, 0), blocks_per_seq - 1) != 0
    a = lax.broadcasted_iota(jnp.int32, shape, 1)
    kk = lax.broadcasted_iota(jnp.int32, shape, 2)
    diff = DIL_BLK + a - kk
    return (diff >= 0) & (diff <= DIL_BLK) & ((kk >= DIL_BLK) | has_prev)


def _blocks(t):
    return t.reshape(DIL_NB, DIL_BLK, t.shape[-1])


def _with_prev(t):
    prev = jnp.concatenate([jnp.zeros((DIL_BLK, t.shape[-1]), t.dtype), t[:-DIL_BLK]], axis=0)
    return jnp.concatenate([_blocks(prev), _blocks(t)], axis=1)


def _fold_prev(t2):
    n = t2.shape[-1]
    to_prev = t2[:, :DIL_BLK].reshape(S, n)
    own = t2[:, DIL_BLK:].reshape(S, n)
    return own + jnp.concatenate([to_prev[DIL_BLK:], jnp.zeros((DIL_BLK, n), t2.dtype)], axis=0)


def _dil_group_spec(t, width=DIL_OUT):
    return pl.BlockSpec((None, S, width), lambda g: (t * DIL_G + g, 0, 0))


def _dil_fwd(qkv3):
    def body(q_ref, k_ref, v_ref, o_ref, lse_ref):
        ok = _dil_mask(pl.program_id(0))
        lane = lax.broadcasted_iota(jnp.int32, (S, LANES), 1)
        lse = jnp.zeros((S, LANES), F32)
        for h in range(DIL_HG):
            sl = slice(h * HD, (h + 1) * HD)
            s = jnp.where(ok, _dot(_blocks(q_ref[:, sl]), _with_prev(k_ref[:, sl]), _BQK) * SCALE, NEG_INF)
            m = jnp.max(s, axis=-1, keepdims=True)
            p = jnp.exp(s - m)
            l = jnp.sum(p, axis=-1, keepdims=True)
            o_ref[:, sl] = _dot(p / l, _with_prev(v_ref[:, sl]), _BQD).reshape(S, HD)
            lse = jnp.where(lane == h, (m + jnp.log(l)).reshape(S, 1), lse)
        lse_ref[...] = lse

    return pl.pallas_call(
        body, name="dil_fwd", grid=(DIL_G,),
        in_specs=[_dil_group_spec(0), _dil_group_spec(1), _dil_group_spec(2)],
        out_specs=[_dil_group_spec(0), _dil_group_spec(0, LANES)],
        out_shape=[jax.ShapeDtypeStruct((DIL_G, S, DIL_OUT), F32), jax.ShapeDtypeStruct((DIL_G, S, LANES), F32)],
        compiler_params=_cparams(("parallel",)),
    )(qkv3, qkv3, qkv3)


def _dil_bwd(qkv3, lse3, do3, c3):
    def body(q_ref, k_ref, v_ref, lse_ref, do_ref, c_ref, dq_ref, dk_ref, dv_ref):
        ok = _dil_mask(pl.program_id(0))
        lse = lse_ref[...]
        c = c_ref[...]
        for h in range(DIL_HG):
            sl = slice(h * HD, (h + 1) * HD)
            q = _blocks(q_ref[:, sl])
            k2 = _with_prev(k_ref[:, sl])
            do_h = _blocks(do_ref[:, sl])
            s = jnp.where(ok, _dot(q, k2, _BQK) * SCALE, NEG_INF)
            p = jnp.exp(s - _blocks(lse[:, h:h + 1]))
            dp = _dot(do_h, _with_prev(v_ref[:, sl]), _BQK)
            ds = p * (dp - _blocks(c[:, h:h + 1]))
            dsb = (ds * SCALE).astype(BF16)
            dq_ref[:, sl] = _dot(dsb, k2, _BQD).reshape(S, HD)
            dk_ref[:, sl] = _fold_prev(_dot(dsb, q, _BKD))
            dv_ref[:, sl] = _fold_prev(_dot(p, do_h, _BKD))

    return pl.pallas_call(
        body, name="dil_bwd", grid=(DIL_G,),
        in_specs=[_dil_group_spec(0), _dil_group_spec(1), _dil_group_spec(2), _dil_group_spec(0, LANES),
                  _dil_group_spec(0), _dil_group_spec(0, LANES)],
        out_specs=[_dil_group_spec(0)] * 3,
        out_shape=[jax.ShapeDtypeStruct((DIL_G, S, DIL_OUT), F32)] * 3,
        compiler_params=_cparams(("parallel",)),
    )(qkv3, qkv3, qkv3, lse3, do3, c3)


COMB_T = 256


def _dil_combine(o3, lse3):
    heads_per_lb = LANES // HD

    def body(o_ref, lse_ref, ob_ref, al_ref, *scratch):
        o_tok = [scratch[LB_PER_CB * gg:LB_PER_CB * (gg + 1)] for gg in range(DIL_G)]
        lse_tok = scratch[LB_PER_CB * DIL_G:]
        g = pl.program_id(0)
        for gs in range(DIL_G):
            @pl.when(g == gs)
            def _(gs=gs):
                for tok, sub in _residue_pieces(DIL_R[gs]):
                    for hf in range(LB_PER_CB):
                        o_tok[gs][hf][tok, :] = o_ref[sub, hf * LANES:(hf + 1) * LANES]
                    lse_tok[gs][tok, :] = lse_ref[sub, :]

        @pl.when(g == DIL_G - 1)
        def _():
            def chunk(i, carry):
                rows = pl.ds(pl.multiple_of(i * COMB_T, COMB_T), COMB_T)
                lse = [lse_tok[gg][rows, :] for gg in range(DIL_G)]
                m = jnp.maximum(jnp.maximum(lse[0], lse[1]), lse[2])
                e = [jnp.exp(lse[gg] - m) for gg in range(DIL_G)]
                den = (e[0] + e[1]) + e[2]
                al = [e[gg] / den for gg in range(DIL_G)]
                for gg in range(DIL_G):
                    al_ref[gg, rows, :] = al[gg]
                for h in range(DIL_HG):
                    hf, sl = h // heads_per_lb, slice((h % heads_per_lb) * HD, (h % heads_per_lb + 1) * HD)
                    acc = al[0][:, h:h + 1] * o_tok[0][hf][rows, sl]
                    for gg in range(1, DIL_G):
                        acc = acc + al[gg][:, h:h + 1] * o_tok[gg][hf][rows, sl]
                    ob_ref[rows, h * HD:(h + 1) * HD] = acc
                return carry

            lax.fori_loop(0, S // COMB_T, chunk, 0)

    return pl.pallas_call(
        body, name="dil_combine", grid=(DIL_G,),
        in_specs=[pl.BlockSpec((None, S, DIL_OUT), lambda g: (g, 0, 0)),
                  pl.BlockSpec((None, S, LANES), lambda g: (g, 0, 0))],
        out_specs=[pl.BlockSpec((S, DIL_OUT), lambda g: (0, 0)),
                   pl.BlockSpec((DIL_G, S, LANES), lambda g: (0, 0, 0))],
        out_shape=[jax.ShapeDtypeStruct((S, DIL_OUT), F32), jax.ShapeDtypeStruct((DIL_G, S, LANES), F32)],
        scratch_shapes=[pltpu.VMEM((S, LANES), F32)] * (DIL_G * (LB_PER_CB + 1)),
        compiler_params=_cparams(("arbitrary",)),
    )(o3, lse3)


def _dil_combine_bwd(dob, ob, alpha, deps=()):
    heads_per_lb = LANES // HD

    def body(dob_ref, ob_ref, al_ref, *refs):
        do_ref, c_ref = refs[len(deps):]
        g = pl.program_id(0)
        hf = pl.program_id(1)
        for gs in range(DIL_G):
            @pl.when(g == gs)
            def _(gs=gs):
                for tok, sub in _residue_pieces(DIL_R[gs]):
                    dob = dob_ref[tok, :]
                    prod = dob * ob_ref[tok, :]
                    al = al_ref[tok, :]
                    lane = lax.broadcasted_iota(jnp.int32, al.shape, 1)
                    c = jnp.where(hf == 0, 0.0, c_ref[sub, :])
                    for hh in range(heads_per_lb):
                        sl = slice(hh * HD, (hh + 1) * HD)
                        head = hf * heads_per_lb + hh
                        a = jnp.sum(jnp.where(lane == head, al, 0.0), axis=-1, keepdims=True)
                        do_ref[sub, sl] = (a * dob[:, sl]).astype(BF16)
                        c = jnp.where(lane == head, a * jnp.sum(prod[:, sl], axis=-1, keepdims=True), c)
                    c_ref[sub, :] = c

    half = pl.BlockSpec((S, LANES), lambda g, hf: (0, hf))
    return pl.pallas_call(
        body, name="dil_combine_bwd", grid=(DIL_G, LB_PER_CB),
        in_specs=[half, half, pl.BlockSpec((None, S, LANES), lambda g, hf: (g, 0, 0))] + [ANY] * len(deps),
        out_specs=[pl.BlockSpec((None, S, LANES), lambda g, hf: (g, 0, hf)),
                   pl.BlockSpec((None, S, LANES), lambda g, hf: (g, 0, 0))],
        out_shape=[jax.ShapeDtypeStruct((DIL_G, S, DIL_OUT), BF16), jax.ShapeDtypeStruct((DIL_G, S, LANES), F32)],
        compiler_params=_cparams(("parallel", "arbitrary")),
    )(dob, ob, alpha, *deps)


def _local_step(x, tgt, h1, g_attn, g_mlp, g_final, b_pad, w_in_t, hooks):
    tables = _rope_tables()

    first = hooks.first_deps()
    proj_a = _mm(h1, w_in_t, mode="nt", tm=S, tn=512, n=3 * FOX_W, out_dtypes=[BF16], name="proj_a", deps=first)
    proj_b = _mm(h1, w_in_t, mode="nt", tm=S, tn=DIL_W, n=3 * DIL_W, b_off=SEG_B // DIL_W, out_dtypes=[F32], name="proj_b",
                deps=first)
    proj_g = _mm(h1, w_in_t, mode="nt", tm=S, tn=D, n=2 * D, b_off=SEG_G // D, out_dtypes=[F32], name="proj_g",
                deps=first)
    proj_f = _mm(h1, w_in_t, mode="nt", tm=S, tn=LANES, n=LANES, b_off=SEG_F // LANES, out_dtypes=[F32], name="proj_f",
                deps=first)

    q_aug, k_aug = _fox_prepare(proj_a, proj_f, b_pad)
    oa, lse_a = _fox_fwd(q_aug, k_aug, proj_a)

    qkv3 = _rope_split(proj_b, tables)
    o3, lse3 = _dil_fwd(qkv3)
    ob, alpha = _dil_combine(o3, lse3)

    w_a, w_b, w_out = hooks.mixer_weights(ob)
    ya = _mm(oa, w_a, mode="nn", tm=S, tn=512, out_dtypes=[F32], name="branch_a")
    yb = _mm(ob, w_b, mode="nn", tm=S, tn=512, out_dtypes=[F32], name="branch_b")
    mixed = _gate_fwd(proj_g, ya, yb)

    def out_epilogue(acc, rows, vecs):
        x2 = rows[0] + acc
        return (x2, _rms_rows(x2, vecs[0])), ()

    x2, h2 = _mm_rows(mixed, w_out, tm=ROWS_TM, name="out_proj", epilogue=out_epilogue, rows=(x,), vecs=(g_mlp,),
                      row_out=(F32, BF16))
    w_up_sh, w_down = hooks.mlp_weights(h2)

    def up_epilogue(acc):
        r = jnp.maximum(acc, 0.0)
        return acc, r * r

    u, act = _mm(h2, w_up_sh, mode="nn", tm=S, tn=DFF // N_DEV, b_sharded=True, out_dtypes=[F32, BF16],
                 name="mlp_up", epilogue=up_epilogue)

    def loss_epilogue(acc, rows, vecs):
        dx3, dg, loss = _final_loss_rows(rows[0] + acc, vecs[0], rows[1])
        return (dx3,), (dg, loss)

    dx3, dg_final, loss = _mm_rows(act, w_down, tm=ROWS_TM, name="mlp_down_loss", epilogue=loss_epilogue,
                                   rows=(x2, tgt), vecs=(g_final,), row_out=(F32,), vec_out=(D, LANES))

    def rms_bwd_epilogue(acc, rows, vecs):
        dx, dg = _rms_bwd_rows(acc, rows[0], rows[1], vecs[0])
        return (dx,), (dg,)

    du = _mm(dx3, w_down, mode="nt", tm=S, tn=512, out_dtypes=[BF16], name="mlp_down_bwd",
             epilogue=lambda acc, u_t: (acc * (2.0 * jnp.maximum(u_t, 0.0)),), extras=(u,))
    dw_down = _mm(act, dx3, mode="tn", tm=512, tn=D, out_dtypes=[F32], name="dw_down")
    dw_up_sh = _mm(h2, du, mode="tn", tm=D, tn=DFF // N_DEV, out_sharded=True, out_dtypes=[F32], name="dw_up")
    dx2, dg_mlp = _mm_rows(du, w_up_sh, b_sharded=True, tm=ROWS_TM, name="mlp_up_bwd", epilogue=rms_bwd_epilogue,
                           rows=(x2, dx3), vecs=(g_mlp,), row_out=(F32,), vec_out=(D,),
                           deps=hooks.mlp_grads(dw_up_sh, dw_down))

    dmixed = _mm(dx2, w_out, mode="nt", tm=S, tn=512, out_dtypes=[F32], name="out_proj_bwd")
    dw_out = _mm(mixed, dx2, mode="tn", tm=D, tn=D, out_dtypes=[F32], name="dw_out")

    dyab, dproj = _gate_bwd(dmixed, proj_g, ya, yb)
    doa = _mm(dyab, w_a, mode="nt", tm=S, tn=FOX_W, k=D, a_off=0, out_dtypes=[BF16], name="branch_a_bwd")
    dw_a = _mm(oa, dyab, mode="tn", tm=FOX_W, tn=D, n=D, b_off=0, out_dtypes=[F32], name="dw_branch_a")
    dob = _mm(dyab, w_b, mode="nt", tm=S, tn=CB, k=D, a_off=1, out_dtypes=[F32], name="branch_b_bwd")
    dw_b = _mm(ob, dyab, mode="tn", tm=DIL_OUT, tn=D, n=D, b_off=1, out_dtypes=[F32], name="dw_branch_b")

    dproj, dft, dfs = _fox_bwd(q_aug, k_aug, proj_a, lse_a, doa, dproj, deps=hooks.mixer_grads(dw_a, dw_b, dw_out))
    dproj, db = _fox_dscan(dft, dfs, proj_f, b_pad, dproj)

    do3, c3 = _dil_combine_bwd(dob, ob, alpha, deps=hooks.mid_backward(db))
    dq3, dk3, dv3 = _dil_bwd(qkv3, lse3, do3, c3)
    dproj = _rope_merge_bwd(dq3, dk3, dv3, tables, dproj)

    dw_in_t = _mm(dproj, h1, mode="tn", tm=CB, tn=D, out_dtypes=[F32], name="dw_in")
    dx, dg_attn = _mm_rows(dproj, w_in_t, tm=ROWS_TM // 2, name="proj_bwd", epilogue=rms_bwd_epilogue,
                           rows=(x, dx2), vecs=(g_attn,), row_out=(F32,), vec_out=(D,),
                           deps=hooks.w_in_grad(dw_in_t))

    return loss, dx, (dg_attn, db, dg_mlp, dg_final)


SHARD_SHAPES = ((D_IN // N_DEV, D), (FOX_W, D // N_DEV), (DIL_OUT, D // N_DEV), (D // N_DEV, D),
                (D, DFF // N_DEV), (DFF // N_DEV, D))
W_NAMES = ("w_in", "w_a", "w_b", "w_out", "w_up", "w_down")
AG_COPIES = 7
HBM = pl.BlockSpec(memory_space=pltpu.HBM)
SEM = pl.BlockSpec(memory_space=pltpu.SEMAPHORE)


def _place():
    return lax.axis_index("x"), lax.axis_index("y"), lax.axis_index("c")


W_SLAB = 752
W_WIN = 272
PAD_BLK = 256


def _pad_runs():
    runs = sorted((pad, pad + b - a, p, a) for p in range(N_DEV) for a, b, pad in _shard_pieces(p))
    blocks = []
    for k in range(D_PAD // PAD_BLK):
        lo, hi = k * PAD_BLK, (k + 1) * PAD_BLK
        blocks.append([(max(lo, r0) - lo, min(hi, r1) - lo, p, a + max(lo, r0) - r0)
                       for r0, r1, p, a in runs if max(lo, r0) < min(hi, r1)])
    return blocks


def _gather_w_in(shard, x_in, g_attn):
    blocks = _pad_runs()

    norm_t = 512
    n_chunks = S // norm_t

    def body(x_ref, xin_ref, g_ref, out_ref, h1_ref, stage, land, xbuf, send_sems, recv_sems, load_sem, x_sems):
        x, y, c = _place()
        me, sibling = (x, y, c), (x, y, 1 - c)
        chips = [(1 - x, y), (x, 1 - y), (1 - x, 1 - y)]

        def slot(px, py, pc):
            return land.at[4 * px + 2 * py + pc]

        def copy(k, block, to):
            return pltpu.make_async_remote_copy(
                src_ref=slot(*block), dst_ref=slot(*block), send_sem=send_sems.at[k], recv_sem=recv_sems.at[k],
                device_id=to, device_id_type=MESH)

        load = pltpu.make_async_copy(x_ref, stage, load_sem)
        load.start()
        load.wait()
        land[4 * x + 2 * y + c] = stage[...].astype(BF16)
        sent = [copy(0, me, sibling)] + [copy(1 + j, me, (*chip, c)) for j, chip in enumerate(chips)]
        for cp in sent:
            cp.start()

        def x_load(i):
            return pltpu.make_async_copy(xin_ref.at[pl.ds(i * norm_t, norm_t)], xbuf.at[i % 2], x_sems.at[i % 2])

        x_load(0).start()
        for i in range(n_chunks):
            if i + 1 < n_chunks:
                x_load(i + 1).start()
            x_load(i).wait()
            h1_ref[i * norm_t:(i + 1) * norm_t, :] = _rms_rows(xbuf[i % 2], g_ref[...]).astype(BF16)

        for j, chip in enumerate(chips):
            copy(1 + j, (*chip, c), me).wait_recv()
            sent.append(copy(4 + j, (*chip, c), sibling))
            sent[-1].start()
        copy(0, sibling, me).wait_recv()
        for j, chip in enumerate(chips):
            copy(4 + j, (*chip, 1 - c), me).wait_recv()
        for cp in sent:
            cp.wait_send()

        row = lax.broadcasted_iota(jnp.int32, (PAD_BLK, D), 0)
        for k, runs in enumerate(blocks):
            out = jnp.zeros((PAD_BLK, D), F32)
            for o_lo, o_hi, p, a in runs:
                start = min(a // 16 * 16, W_SLAB - W_WIN)
                win = land[p, start:start + W_WIN, :].astype(F32)
                moved = pltpu.roll(win, (o_lo - (a - start)) % W_WIN, 0)[:PAD_BLK]
                out = jnp.where((row >= o_lo) & (row < o_hi), moved, out)
            out_ref[k * PAD_BLK:(k + 1) * PAD_BLK, :] = out.astype(BF16)

    return pl.pallas_call(
        body, name="all_gather_w_in",
        out_shape=[jax.ShapeDtypeStruct((D_PAD, D), BF16), jax.ShapeDtypeStruct((S, D), BF16)],
        in_specs=[ANY, ANY, pl.BlockSpec(memory_space=pltpu.VMEM)],
        out_specs=[pl.BlockSpec(memory_space=pltpu.VMEM)] * 2,
        scratch_shapes=[pltpu.VMEM((W_SLAB, D), F32), pltpu.VMEM((N_DEV, W_SLAB, D), BF16),
                        pltpu.VMEM((2, norm_t, D), F32),
                        pltpu.SemaphoreType.DMA((AG_COPIES,)), pltpu.SemaphoreType.DMA((AG_COPIES,)),
                        pltpu.SemaphoreType.DMA(()), pltpu.SemaphoreType.DMA((2,))],
        compiler_params=_cparams(None),
    )(shard, x_in, g_attn)


def _place_own(shards):
    n_w = len(shards)
    shapes = [t.shape for t in shards]

    def body(*refs):
        in_refs, out_refs = refs[:n_w], refs[n_w:2 * n_w]
        stage, cast = refs[2 * n_w:3 * n_w], refs[3 * n_w:4 * n_w]
        load_sems, store_sems = refs[4 * n_w:]
        x, y, c = _place()
        loads = [pltpu.make_async_copy(in_refs[i], stage[i], load_sems.at[i]) for i in range(n_w)]
        for ld in loads:
            ld.start()
        stores = []
        for i in range(n_w):
            loads[i].wait()
            cast[i][...] = stage[i][...].astype(BF16)
            stores.append(pltpu.make_async_copy(cast[i], out_refs[i].at[4 * x + 2 * y + c], store_sems.at[i]))
            stores[-1].start()
        for st in stores:
            st.wait()

    return pl.pallas_call(
        body, name="place_own_shards",
        out_shape=[jax.ShapeDtypeStruct((N_DEV,) + sh, BF16) for sh in shapes],
        in_specs=[ANY] * n_w, out_specs=[ANY] * n_w,
        scratch_shapes=[pltpu.VMEM(sh, F32) for sh in shapes] + [pltpu.VMEM(sh, BF16) for sh in shapes]
                       + [pltpu.SemaphoreType.DMA((n_w,)), pltpu.SemaphoreType.DMA((n_w,))],
        compiler_params=_cparams(None),
    )(*shards)


PEER_COPIES = N_DEV - 1
SIBLING_COPIES = 4
CHIP_COPIES = 3


def _peer_copies(_, land_refs, send_sems, recv_sems):
    x, y, c = _place()
    mine = 4 * x + 2 * y + c
    copies = []
    for i, ref in enumerate(land_refs):
        for k in range(1, N_DEV):
            peer = (x ^ (k >> 2), y ^ ((k >> 1) & 1), c ^ (k & 1))
            n = i * PEER_COPIES + k - 1
            copies.append(pltpu.make_async_remote_copy(
                src_ref=ref.at[mine], dst_ref=ref.at[mine], send_sem=send_sems.at[n], recv_sem=recv_sems.at[n],
                device_id=peer, device_id_type=MESH))
    return copies


def _sibling_copies(g_refs, r_refs, send_sems, recv_sems):
    x, y, c = _place()
    return [pltpu.make_async_remote_copy(
        src_ref=g_refs[i].at[2 * k + (1 - c)], dst_ref=r_refs[i].at[k],
        send_sem=send_sems.at[SIBLING_COPIES * i + k], recv_sem=recv_sems.at[SIBLING_COPIES * i + k],
        device_id=(x, y, 1 - c), device_id_type=MESH) for i in range(len(g_refs)) for k in range(SIBLING_COPIES)]


def _chip_copies(p_refs, r_refs, send_sems, recv_sems):
    x, y, c = _place()
    chips = [(1 - x, y), (x, 1 - y), (1 - x, 1 - y)]
    return [pltpu.make_async_remote_copy(
        src_ref=p_refs[i].at[2 * cx + cy], dst_ref=r_refs[i].at[j],
        send_sem=send_sems.at[CHIP_COPIES * i + j], recv_sem=recv_sems.at[CHIP_COPIES * i + j],
        device_id=(cx, cy, c), device_id_type=MESH) for i in range(len(p_refs)) for j, (cx, cy) in enumerate(chips)]


def _in_hbm(a):
    return pltpu.with_memory_space_constraint(a, pltpu.HBM)


def _exchange_start(name, copies_fn, n_copies, srcs, lands, after=()):
    n_s, n_l, n_a = len(srcs), len(lands), len(after)

    def body(*refs):
        outs = refs[n_s + n_l + n_a:]
        for cp in copies_fn(refs[:n_s], refs[n_s:n_s + n_l], outs[0], outs[1]):
            cp.start()
        outs[-1][...] = jnp.zeros_like(outs[-1])

    res = pl.pallas_call(
        body, name=name,
        out_shape=[pltpu.SemaphoreType.DMA((n_copies,)), pltpu.SemaphoreType.DMA((n_copies,))]
                  + [pltpu.HBM(t.shape, t.dtype) for t in (*srcs, *lands)] + [jax.ShapeDtypeStruct((8, LANES), F32)],
        in_specs=[HBM] * (n_s + n_l) + [ANY] * n_a,
        out_specs=[SEM, SEM] + [HBM] * (n_s + n_l) + [pl.BlockSpec(memory_space=pltpu.VMEM)],
        input_output_aliases={i: 2 + i for i in range(n_s + n_l)},
        compiler_params=pltpu.CompilerParams(has_side_effects=pltpu.SideEffectType.DATAFLOW_SIDE_EFFECTING),
    )(*[_in_hbm(t) for t in (*srcs, *lands)], *after)
    return res[0], res[1], list(res[2:2 + n_s]), list(res[2 + n_s:2 + n_s + n_l]), res[-1]


def _exchange_wait(name, copies_fn, started, after):
    send_sems, recv_sems, srcs, lands, _ = started
    n_s, n_l = len(srcs), len(lands)

    def body(*refs):
        for cp in copies_fn(refs[:n_s], refs[n_s:n_s + n_l], refs[n_s + n_l], refs[n_s + n_l + 1]):
            cp.wait_send()
            cp.wait_recv()

    res = pl.pallas_call(
        body, name=name,
        out_shape=[pltpu.HBM(t.shape, t.dtype) for t in (*srcs, *lands)],
        in_specs=[HBM] * (n_s + n_l) + [SEM, SEM, ANY],
        out_specs=[HBM] * (n_s + n_l),
        input_output_aliases={i: i for i in range(n_s + n_l)},
        compiler_params=pltpu.CompilerParams(has_side_effects=pltpu.SideEffectType.DATAFLOW_SIDE_EFFECTING),
    )(*srcs, *lands, send_sems, recv_sems, after)
    return list(res[:n_s]), list(res[n_s:])


def _shard_block(shape):
    rows, cols = shape
    if rows % 256:
        return rows, 256
    return min(rows, 512 if cols <= LANES else 256), cols


def _chip_partial(ids, g_stack, recv1, name):
    shape = g_stack.shape[1:]
    br, bc = _shard_block(shape)

    def body(ids_ref, g_ref, r_ref, pb_ref, own_ref):
        s = g_ref[...] + r_ref[...]
        pb_ref[...] = s.astype(BF16)

        @pl.when(pl.program_id(2) == ids_ref[1])
        def _():
            own_ref[...] = s

    grid_spec = pltpu.PrefetchScalarGridSpec(
        num_scalar_prefetch=1, grid=(shape[0] // br, shape[1] // bc, 4),
        in_specs=[pl.BlockSpec((None, br, bc), lambda r, q, k, ids: (2 * k + ids[0], r, q)),
                  pl.BlockSpec((None, br, bc), lambda r, q, k, ids: (k, r, q))],
        out_specs=[pl.BlockSpec((None, br, bc), lambda r, q, k, ids: (k, r, q)),
                   pl.BlockSpec((br, bc), lambda r, q, k, ids: (r, q))])
    return pl.pallas_call(
        body, name=name, grid_spec=grid_spec,
        out_shape=[jax.ShapeDtypeStruct((4,) + shape, BF16), jax.ShapeDtypeStruct(shape, F32)],
        compiler_params=_cparams(("parallel", "parallel", "arbitrary")),
    )(ids, g_stack, recv1)


def _adamw(w, g, m, v):
    m = ADAM_B1 * m + (1.0 - ADAM_B1) * g
    v = ADAM_B2 * v + (1.0 - ADAM_B2) * (g * g)
    m_hat = m / (1.0 - ADAM_B1 ** ADAM_STEP)
    v_hat = v / (1.0 - ADAM_B2 ** ADAM_STEP)
    delta = -ADAM_LR * (m_hat / (jnp.sqrt(v_hat) + ADAM_EPS) + ADAM_WD * w)
    return delta, m, v


def _reduce_adamw(own, recv2, w, m, v, name, deps=()):
    shape = own.shape
    br, bc = _shard_block(shape)

    def body(own_ref, r_ref, w_ref, m_ref, v_ref, *refs):
        g_ref, d_ref, nm_ref, nv_ref = refs[len(deps):]
        g = own_ref[...]
        for j in range(3):
            g = g + r_ref[j].astype(F32)
        delta, nm, nv = _adamw(w_ref[...], g, m_ref[...], v_ref[...])
        g_ref[...] = g
        d_ref[...] = delta
        nm_ref[...] = nm
        nv_ref[...] = nv

    blk = pl.BlockSpec((br, bc), lambda r, q: (r, q))
    return pl.pallas_call(
        body, name=name, grid=(shape[0] // br, shape[1] // bc),
        in_specs=[blk, pl.BlockSpec((3, br, bc), lambda r, q: (0, r, q)), blk, blk, blk] + [ANY] * len(deps),
        out_specs=[blk] * 4, out_shape=[jax.ShapeDtypeStruct(shape, F32)] * 4,
        compiler_params=_cparams(("parallel", "parallel")),
    )(own, recv2, w, m, v, *deps)


def _small_allreduce_adamw(gvec, w, m, v, deps=()):
    def body(g_ref, w_ref, m_ref, v_ref, *refs):
        go_ref, d_ref, nm_ref, nv_ref, buf, send_sems, recv_sems = refs[len(deps):]
        x, y, c = _place()
        my_slot = 4 * x + 2 * y + c
        buf[my_slot] = g_ref[...]
        copies = []
        for k in range(1, N_DEV):
            px, py, pc = x ^ (k >> 2), y ^ ((k >> 1) & 1), c ^ (k & 1)
            copies.append(pltpu.make_async_remote_copy(
                src_ref=g_ref, dst_ref=buf.at[my_slot], send_sem=send_sems.at[k - 1], recv_sem=recv_sems.at[k - 1],
                device_id=(px, py, pc), device_id_type=MESH))
        for cp in copies:
            cp.start()
        for k in range(1, N_DEV):
            px, py, pc = x ^ (k >> 2), y ^ ((k >> 1) & 1), c ^ (k & 1)
            pltpu.make_async_remote_copy(
                src_ref=g_ref, dst_ref=buf.at[4 * px + 2 * py + pc], send_sem=send_sems.at[k - 1],
                recv_sem=recv_sems.at[k - 1], device_id=(px, py, pc), device_id_type=MESH).wait_recv()
        for cp in copies:
            cp.wait_send()
        g = buf[0]
        for s in range(1, N_DEV):
            g = g + buf[s]
        delta, nm, nv = _adamw(w_ref[...], g, m_ref[...], v_ref[...])
        go_ref[...] = g
        d_ref[...] = delta
        nm_ref[...] = nm
        nv_ref[...] = nv

    vm = pl.BlockSpec(memory_space=pltpu.VMEM)
    return pl.pallas_call(
        body, name="small_allreduce_adamw",
        in_specs=[vm] * 4 + [ANY] * len(deps), out_specs=[vm] * 4,
        out_shape=[jax.ShapeDtypeStruct((SMALL_R, LANES), F32)] * 4,
        scratch_shapes=[pltpu.VMEM((N_DEV, SMALL_R, LANES), F32),
                        pltpu.SemaphoreType.DMA((N_DEV - 1,)), pltpu.SemaphoreType.DMA((N_DEV - 1,))],
    )(gvec, w, m, v, *deps)


def _cols_to_whole(stack):
    rows = stack.shape[1]
    return stack.transpose(1, 0, 2).reshape(rows, -1)


def _whole_to_cols(t):
    rows = t.shape[0]
    return t.reshape(rows, N_DEV, -1).transpose(1, 0, 2)


W_IN_SEGMENTS = ((0, 3 * FOX_W, SEG_A), (3 * FOX_W, 3 * FOX_W + FOX_H, SEG_F),
                 (3 * FOX_W + FOX_H, 3 * FOX_W + FOX_H + 3 * DIL_W, SEG_B), (3 * FOX_W + FOX_H + 3 * DIL_W, D_IN, SEG_G))


def _shard_pieces(p):
    rows = D_IN // N_DEV
    lo, hi = p * rows, (p + 1) * rows
    return [(max(lo, a) - lo, min(hi, b) - lo, pad + max(lo, a) - a)
            for a, b, pad in W_IN_SEGMENTS if max(lo, a) < min(hi, b)]


def _unpad_dw_in_t(dwp):
    return jnp.stack([jnp.concatenate([dwp[pad:pad + b - a] for a, b, pad in _shard_pieces(p)], axis=0)
                      for p in range(N_DEV)])


LOSS_ROW = 25


def _pack_small(g_attn, b, g_mlp, g_final, loss_row=None):
    tail = jnp.pad(b, ((0, 7), (0, LANES - b.shape[1])))
    if loss_row is not None:
        tail = tail + jnp.pad(loss_row, ((LOSS_ROW - 24, 31 - LOSS_ROW), (0, 0)))
    return jnp.concatenate([g_attn.reshape(8, LANES), g_mlp.reshape(8, LANES), g_final.reshape(8, LANES), tail], axis=0)


def _unpack_small(p):
    return (p[0:8].reshape(1, D), p[24:25, :FOX_H], p[8:16].reshape(1, D), p[16:24].reshape(D))


class _StepComm:
    def __init__(self, ids, w_sh, m_sh, v_sh, after_w_in):
        self.ids = ids
        self.w_sh, self.m_sh, self.v_sh = w_sh, m_sh, v_sh
        self.updates = None
        lands = _place_own(w_sh[1:])
        self.gather_mixer = _exchange_start("gather_mixer_start", _peer_copies, PEER_COPIES * 3, [], lands[:3],
                                            after=(after_w_in,))
        self.gather_mlp = _exchange_start("gather_mlp_start", _peer_copies, PEER_COPIES * 2, [], lands[3:],
                                          after=(self.gather_mixer[-1],))
        self.sibling = {}
        self.chips = {}
        self.own = {}
        self.names = {}

    def _reduce_start(self, group, names, grads, after=()):
        lands = [lax.empty((SIBLING_COPIES,) + t.shape[1:], F32) for t in grads]
        self.sibling[group] = _exchange_start("grad_%s_sibling_start" % group, _sibling_copies,
                                              SIBLING_COPIES * len(grads), grads, lands, after=after)
        self.names[group] = names
        return self.sibling[group][-1]

    def _reduce_mid(self, group, after):
        grads, recv1 = _exchange_wait("grad_%s_sibling_wait" % group, _sibling_copies, self.sibling[group], after)
        parts = [_chip_partial(self.ids, g, r, "grad_partial_" + n) for g, r, n in zip(grads, recv1, self.names[group])]
        self.own[group] = [p[1] for p in parts]
        srcs = [p[0] for p in parts]
        lands = [lax.empty((CHIP_COPIES,) + t.shape[1:], BF16) for t in srcs]
        self.chips[group] = _exchange_start("grad_%s_chips_start" % group, _chip_copies, CHIP_COPIES * len(srcs),
                                            srcs, lands)
        return self.chips[group][-1]

    def reduced(self, group, after):
        _, recv2 = _exchange_wait("grad_%s_chips_wait" % group, _chip_copies, self.chips[group], after)
        return list(zip(self.own[group], recv2))

    def first_deps(self):
        return [self.gather_mlp[-1]]

    def mixer_weights(self, after):
        _, (g_a, g_b, g_out) = _exchange_wait("gather_mixer_wait", _peer_copies, self.gather_mixer, after)
        return _cols_to_whole(g_a), _cols_to_whole(g_b), g_out.reshape(D, D)

    def mlp_weights(self, after):
        _, (g_up, g_down) = _exchange_wait("gather_mlp_wait", _peer_copies, self.gather_mlp, after)
        return g_up, g_down.reshape(DFF, D)

    def mlp_grads(self, dw_up_sh, dw_down):
        return [self._reduce_start("mlp", W_NAMES[4:], [dw_up_sh, dw_down.reshape((N_DEV,) + SHARD_SHAPES[5])])]

    def mixer_grads(self, dw_a, dw_b, dw_out):
        token = self._reduce_mid("mlp", dw_b)
        grads = [_whole_to_cols(dw_a), _whole_to_cols(dw_b), dw_out.reshape((N_DEV,) + SHARD_SHAPES[3])]
        return [self._reduce_start("mixer", W_NAMES[1:4], grads, after=(token,))]

    def mid_backward(self, after):
        return [self._reduce_mid("mixer", after)]

    def w_in_grad(self, dw_in_t):
        grads = [_unpad_dw_in_t(dw_in_t)]
        token = self._reduce_start("w_in", W_NAMES[:1], grads)
        reduced = self.reduced("mixer", token) + self.reduced("mlp", token)
        self.updates, last = [None] * len(reduced), token
        for i in (3, 4, 0, 1, 2):
            if i == 0:
                last = self._reduce_mid("w_in", last)
            self.updates[i] = _reduce_adamw(*reduced[i], self.w_sh[1 + i], self.m_sh[1 + i], self.v_sh[1 + i],
                                            "adamw_" + W_NAMES[1 + i], deps=[last])
            last = self.updates[i][0]
        return [last]

    def w_in_update(self, after):
        (reduced,) = self.reduced("w_in", after)
        return _reduce_adamw(*reduced, self.w_sh[0], self.m_sh[0], self.v_sh[0], "adamw_" + W_NAMES[0])


def kernel(x, norm_attn_g, w_in, b_forget, w_branch_a, w_branch_b, w_out, norm_mlp_g, w_up, w_down, norm_final_g, loss_target, m_norm_attn_g, m_w_in, m_b_forget, m_w_branch_a, m_w_branch_b, m_w_out, m_norm_mlp_g, m_w_up, m_w_down, m_norm_final_g, v_norm_attn_g, v_w_in, v_b_forget, v_w_branch_a, v_w_branch_b, v_w_out, v_norm_mlp_g, v_w_up, v_w_down, v_norm_final_g):
    cx, cy, cc = _place()
    ids = jnp.stack([cc, 2 * cx + cy]).astype(jnp.int32)

    w_sh = [w_in[0].T] + [t[0] for t in (w_branch_a, w_branch_b, w_out, w_up, w_down)]
    m_sh = [m_w_in[0].T] + [t[0] for t in (m_w_branch_a, m_w_branch_b, m_w_out, m_w_up, m_w_down)]
    v_sh = [v_w_in[0].T] + [t[0] for t in (v_w_branch_a, v_w_branch_b, v_w_out, v_w_up, v_w_down)]

    w_in_t, h1 = _gather_w_in(jnp.pad(w_sh[0], ((0, W_SLAB - w_sh[0].shape[0]), (0, 0))), x[0], norm_attn_g)
    comm = _StepComm(ids, w_sh, m_sh, v_sh, w_in_t)
    b_pad = jnp.pad(b_forget, ((0, 0), (0, LANES - FOX_H)))

    loss_row, dx, dsmall = _local_step(
        x[0], loss_target[0], h1, norm_attn_g, norm_mlp_g, norm_final_g.reshape(1, D), b_pad, w_in_t, comm)

    dg_attn, db, dg_mlp, dg_final = dsmall
    small = _small_allreduce_adamw(
        _pack_small(dg_attn, db, dg_mlp, dg_final, loss_row),
        _pack_small(norm_attn_g, b_forget, norm_mlp_g, norm_final_g.reshape(1, D)),
        _pack_small(m_norm_attn_g, m_b_forget, m_norm_mlp_g, m_norm_final_g.reshape(1, D)),
        _pack_small(v_norm_attn_g, v_b_forget, v_norm_mlp_g, v_norm_final_g.reshape(1, D)))
    big = [comm.w_in_update(small[0])] + comm.updates

    outs = [small[0][LOSS_ROW, 0], dx[None]]
    for q in range(4):
        s_attn, s_b, s_mlp, s_final = _unpack_small(small[q])
        b_in, b_a, b_b, b_out, b_up, b_down = [t[q][None] for t in big]
        b_in = jnp.swapaxes(b_in, 1, 2)
        outs += [s_attn, b_in, s_b, b_a, b_b, b_out, s_mlp, b_up, b_down, s_final]
    return tuple(outs)
```

```python
import functools

import jax
import jax.numpy as jnp
from jax import lax
from jax.experimental import pallas as pl
from jax.experimental.pallas import tpu as pltpu

F32 = jnp.float32
BF16 = jnp.bfloat16
MESH = pl.DeviceIdType.MESH

S = 2048
D = 1024
HD = 64
FOX_H = 8
FOX_W = FOX_H * HD
DIL_HG = 4
DIL_G = 3
DIL_W = DIL_G * DIL_HG * HD
DIL_OUT = DIL_HG * HD
DIL_BLK = 128
DIL_R = (1, 4, 16)
DFF = 4 * D
D_IN = 3 * FOX_W + FOX_H + 3 * DIL_W + 2 * D
EPS = 1e-6
NEG_INF = -1e30
SCALE = HD ** -0.5
ROPE_THETA = 500000.0
ROPE_DIM = HD // 4
N_DEV = 8

ADAM_LR = 0.001
ADAM_B1 = 0.9
ADAM_B2 = 0.999
ADAM_EPS = 1e-08
ADAM_WD = 0.01
ADAM_STEP = 10

LANES = 128
CB = 256
SEG_A = 0
SEG_B = 3 * FOX_W
SEG_Z = SEG_B + 3 * DIL_W
SEG_G = SEG_Z + CB
SEG_F = SEG_G + 2 * D
D_PAD = SEG_F + CB
SMALL_R = 32

VMEM_MB = 56


def _cparams(dims=None, vmem_mb=VMEM_MB, **kw):
    return pltpu.CompilerParams(dimension_semantics=dims, vmem_limit_bytes=vmem_mb << 20, **kw)


ANY = pl.BlockSpec(memory_space=pl.ANY)

_NN = (((1,), (0,)), ((), ()))
_NT = (((1,), (1,)), ((), ()))
_TN = (((0,), (0,)), ((), ()))


def _dot(a, b, dims):
    return lax.dot_general(a.astype(BF16), b.astype(BF16), dims, preferred_element_type=F32)


def _mm(a, b, *, mode, tm, tn, out_dtypes, name, n=None, k=None, a_off=0, b_off=0,
        b_sharded=False, out_sharded=False, epilogue=None, extras=(), deps=()):
    n_sh = b.shape[-1] if b_sharded else None
    if mode == "nn":
        m = a.shape[0]
        k = k or a.shape[1]
        a_spec = pl.BlockSpec((tm, k), lambda i, j: (i, a_off))
        if b_sharded:
            assert tn == n_sh
            n = N_DEV * n_sh
            b_spec = pl.BlockSpec((None, k, tn), lambda i, j: (j, 0, 0))
        else:
            n = n or b.shape[1]
            b_spec = pl.BlockSpec((k, tn), lambda i, j: (0, j + b_off))
        dims = _NN
    elif mode == "nt":
        m = a.shape[0]
        k = k or a.shape[1]
        a_spec = pl.BlockSpec((tm, k), lambda i, j: (i, a_off))
        if b_sharded:
            n = b.shape[1]
            b_spec = pl.BlockSpec((N_DEV, tn, n_sh), lambda i, j: (0, j, 0))
        else:
            n = n or b.shape[0]
            b_spec = pl.BlockSpec((tn, k), lambda i, j: (j + b_off, 0))
        dims = _NT
    else:
        k, m = a.shape
        n = n or b.shape[1]
        a_spec = pl.BlockSpec((k, tm), lambda i, j: (0, i))
        b_spec = pl.BlockSpec((k, tn), lambda i, j: (0, j + b_off))
        dims = _TN
    assert m % tm == 0 and n % tn == 0, (name, m, n, tm, tn)
    n_extra = len(extras)
    tile = pl.BlockSpec((tm, tn), lambda i, j: (i, j))
    if out_sharded:
        assert mode == "tn" and n // tn == N_DEV
        out_spec = pl.BlockSpec((None, tm, tn), lambda i, j: (j, i, 0))
        out_shape = (N_DEV, m, tn)
    else:
        out_spec, out_shape = tile, (m, n)

    def body(a_ref, b_ref, *refs):
        if mode == "nt" and b_sharded:
            acc = _dot(a_ref[:, 0:n_sh], b_ref[0], dims)
            for p in range(1, N_DEV):
                acc = acc + _dot(a_ref[:, p * n_sh:(p + 1) * n_sh], b_ref[p], dims)
        else:
            acc = _dot(a_ref[...], b_ref[...], dims)
        ex = [r[...] for r in refs[:n_extra]]
        outs = epilogue(acc, *ex) if epilogue is not None else (acc,)
        for o_ref, o in zip(refs[n_extra + len(deps):], outs):
            o_ref[...] = o.astype(o_ref.dtype)

    res = pl.pallas_call(
        body, name=name, grid=(m // tm, n // tn),
        in_specs=[a_spec, b_spec] + [tile] * n_extra + [ANY] * len(deps),
        out_specs=[out_spec] * len(out_dtypes),
        out_shape=[jax.ShapeDtypeStruct(out_shape, dt) for dt in out_dtypes],
        compiler_params=_cparams(("parallel", "parallel")),
    )(a, b, *extras, *deps)
    return res if len(out_dtypes) > 1 else res[0]


def _mm_rows(a, b, *, tm, name, epilogue, rows=(), vecs=(), row_out=(), vec_out=(), b_sharded=False, deps=()):
    m, k = a.shape
    n_rows, n_vecs, n_deps = len(rows), len(vecs), len(deps)
    n_sh = b.shape[-1] if b_sharded else None

    def body(a_ref, b_ref, *refs):
        row_refs, vec_refs = refs[:n_rows], refs[n_rows:n_rows + n_vecs]
        outs = refs[n_rows + n_vecs + n_deps:]
        if b_sharded:
            acc = _dot(a_ref[:, 0:n_sh], b_ref[0], _NT)
            for p in range(1, N_DEV):
                acc = acc + _dot(a_ref[:, p * n_sh:(p + 1) * n_sh], b_ref[p], _NT)
        else:
            acc = _dot(a_ref[...], b_ref[...], _NN)
        row_vals, vec_incs = epilogue(acc, [r[...] for r in row_refs], [v[...] for v in vec_refs])
        for o_ref, val in zip(outs[:len(row_out)], row_vals):
            o_ref[...] = val.astype(o_ref.dtype)

        @pl.when(pl.program_id(0) == 0)
        def _():
            for o_ref in outs[len(row_out):]:
                o_ref[...] = jnp.zeros_like(o_ref)

        for o_ref, inc in zip(outs[len(row_out):], vec_incs):
            o_ref[...] += inc

    tile = pl.BlockSpec((tm, D), lambda i: (i, 0))
    b_spec = pl.BlockSpec(b.shape, lambda i: (0,) * b.ndim)
    return pl.pallas_call(
        body, name=name, grid=(m // tm,),
        in_specs=[pl.BlockSpec((tm, k), lambda i: (i, 0)), b_spec] + [tile] * n_rows
                 + [pl.BlockSpec(v.shape, lambda i: (0, 0)) for v in vecs] + [ANY] * n_deps,
        out_specs=[tile] * len(row_out) + [pl.BlockSpec((1, w), lambda i: (0, 0)) for w in vec_out],
        out_shape=[jax.ShapeDtypeStruct((m, D), dt) for dt in row_out]
                  + [jax.ShapeDtypeStruct((1, w), F32) for w in vec_out],
        compiler_params=_cparams(("arbitrary",)),
    )(a, b, *rows, *vecs, *deps)


ROW_T = 256
ROWS_TM = 512


def _rms_rows(x, g):
    r = lax.rsqrt(jnp.mean(x * x, axis=-1, keepdims=True) + EPS)
    return (x * r) * g


def _rms_bwd_rows(dh, x, dres, g):
    r = lax.rsqrt(jnp.mean(x * x, axis=-1, keepdims=True) + EPS)
    xn = x * r
    dhn = dh * g
    dx = dres + r * (dhn - xn * jnp.mean(dhn * xn, axis=-1, keepdims=True))
    return dx, jnp.sum(dh * xn, axis=0, keepdims=True)


def _final_loss_rows(x, g, tgt):
    r = lax.rsqrt(jnp.mean(x * x, axis=-1, keepdims=True) + EPS)
    xn = x * r
    err = xn * g - tgt
    row_loss = jnp.mean(err * err, axis=-1, keepdims=True)
    loss = 0.5 * jnp.sum(row_loss, axis=0, keepdims=True) * jnp.ones((1, LANES), F32)
    dy = err * (1.0 / D)
    dyn = dy * g
    dx = r * (dyn - xn * jnp.mean(dyn * xn, axis=-1, keepdims=True))
    return dx, jnp.sum(dy * xn, axis=0, keepdims=True), loss


def _sigmoid(z):
    return 1.0 / (1.0 + jnp.exp(-z))


def _gate_fwd(proj_g, ya, yb):
    def body(ga_ref, gb_ref, ya_ref, yb_ref, o_ref):
        o_ref[...] = (_sigmoid(ga_ref[...]) * ya_ref[...] + _sigmoid(gb_ref[...]) * yb_ref[...]).astype(BF16)

    row = pl.BlockSpec((ROW_T, D), lambda i: (i, 0))
    return pl.pallas_call(
        body, name="gate_fwd", grid=(S // ROW_T,),
        in_specs=[row, pl.BlockSpec((ROW_T, D), lambda i: (i, 1)), row, row],
        out_specs=row, out_shape=jax.ShapeDtypeStruct((S, D), BF16),
        compiler_params=_cparams(("parallel",)),
    )(proj_g, proj_g, ya, yb)


GATE_TR = 1024
GATE_TC = 512
GATE_NB = 2 * D // GATE_TC


def _gate_bwd(dmixed, proj_g, ya, yb):
    half = GATE_NB // 2

    def body(dm_ref, g_ref, ya_ref, yb_ref, dy_ref, dg_ref):
        dm = dm_ref[...]
        s = _sigmoid(g_ref[...])
        y = jnp.where(pl.program_id(1) < half, ya_ref[...], yb_ref[...])
        dy_ref[...] = (dm * s).astype(BF16)
        dg_ref[...] = (dm * y * (s * (1.0 - s))).astype(BF16)

    blk = pl.BlockSpec((GATE_TR, GATE_TC), lambda i, j: (i, j))
    wrapped = pl.BlockSpec((GATE_TR, GATE_TC), lambda i, j: (i, j % half))
    return pl.pallas_call(
        body, name="gate_bwd", grid=(S // GATE_TR, GATE_NB),
        in_specs=[wrapped, blk, wrapped, wrapped],
        out_specs=[blk, pl.BlockSpec((GATE_TR, GATE_TC), lambda i, j: (i, j + SEG_G // GATE_TC))],
        out_shape=[jax.ShapeDtypeStruct((S, 2 * D), BF16), jax.ShapeDtypeStruct((S, D_PAD), BF16)],
        compiler_params=_cparams(("parallel", "parallel")),
    )(dmixed, proj_g, ya, yb)


FOX_TQ = 256


def _scan_rows(x, reverse):
    n = x.shape[0]
    row = lax.broadcasted_iota(jnp.int32, x.shape, 0)
    k = 1
    while k < n:
        if reverse:
            x = x + jnp.where(row < n - k, pltpu.roll(x, n - k, 0), 0.0)
        else:
            x = x + jnp.where(row >= k, pltpu.roll(x, k, 0), 0.0)
        k *= 2
    return x


def _fox_dscan(dft, dfs, proj_f, b_pad, dproj):
    def body(dft_ref, dfs_ref, f_ref, b_ref, _, dfa_ref, db_ref):
        dfs_pad = jnp.concatenate([dfs_ref[...], jnp.zeros((LANES - FOX_H, S), F32)], axis=0)
        df = dfs_pad.T + dft_ref[0]
        for hp in range(1, dft_ref.shape[0]):
            df = df + pltpu.roll(dft_ref[hp], 2 * hp, 1)
        dlf = _scan_rows(df, reverse=True)
        z = f_ref[...] + b_ref[...]
        lane = lax.broadcasted_iota(jnp.int32, (S, LANES), 1)
        dfa = jnp.where(lane < FOX_H, dlf / (1.0 + jnp.exp(z)), 0.0)
        dfa_ref[:, :LANES] = dfa.astype(BF16)
        dfa_ref[:, LANES:] = jnp.zeros((S, CB - LANES), BF16)
        db_ref[...] = jnp.sum(dfa, axis=0, keepdims=True)

    return pl.pallas_call(
        body, name="fox_dscan", grid=(1,),
        in_specs=[pl.BlockSpec(dft.shape, lambda i: (0, 0, 0)), pl.BlockSpec((FOX_H, S), lambda i: (0, 0)),
                  pl.BlockSpec((S, LANES), lambda i: (0, 0)), pl.BlockSpec((1, LANES), lambda i: (0, 0)), ANY],
        out_specs=[pl.BlockSpec((S, CB), lambda i: (0, SEG_F // CB)), pl.BlockSpec((1, LANES), lambda i: (0, 0))],
        out_shape=[jax.ShapeDtypeStruct((S, D_PAD), BF16), jax.ShapeDtypeStruct((1, LANES), F32)],
        input_output_aliases={4: 0},
        compiler_params=_cparams(("arbitrary",)),
    )(dft, dfs, proj_f, b_pad, dproj)


FOX_NQ = S // FOX_TQ
FOX_HP = FOX_W // LANES
HEADS_PER_LB = LANES // HD
FOX_AUG = 2 * LANES


def _fox_prepare(proj_a, proj_f, b_pad):
    tr = 512

    def body(q_ref, k_ref, f_ref, b_ref, qa_ref, ka_ref, hi_s, mid_s, lo_s):
        i = pl.program_id(0)

        @pl.when(i == 0)
        def _():
            z = f_ref[...] + b_ref[...]
            lf = jnp.minimum(z, 0.0) - jnp.log1p(jnp.exp(-jnp.abs(z)))
            f = _scan_rows(lf, reverse=False)
            hi = f.astype(BF16).astype(F32)
            r1 = f - hi
            mid = r1.astype(BF16).astype(F32)
            hi_s[...] = hi
            mid_s[...] = mid
            lo_s[...] = (r1 - mid).astype(BF16).astype(F32)

        rows = pl.ds(pl.multiple_of(i * tr, tr), tr)
        hi, mid, lo = hi_s[rows, :], mid_s[rows, :], lo_s[rows, :]
        lane = lax.broadcasted_iota(jnp.int32, (tr, HD), 1)
        q_ones = jnp.where((lane >= 3) & (lane < 6), 1.0, 0.0)
        k_ones = jnp.where(lane < 3, 1.0, 0.0)
        for h in range(FOX_H):
            a1, a2, a3 = hi[:, h:h + 1], mid[:, h:h + 1], lo[:, h:h + 1]
            q_extra = jnp.where(lane == 0, a1, jnp.where(lane == 1, a2, jnp.where(lane == 2, a3, q_ones)))
            k_extra = jnp.where(lane == 3, -a1, jnp.where(lane == 4, -a2, jnp.where(lane == 5, -a3, k_ones)))
            base = h * LANES
            qa_ref[:, base:base + HD] = (q_ref[:, h * HD:(h + 1) * HD].astype(F32) * SCALE).astype(BF16)
            qa_ref[:, base + HD:base + LANES] = q_extra.astype(BF16)
            ka_ref[:, base:base + HD] = k_ref[:, h * HD:(h + 1) * HD]
            ka_ref[:, base + HD:base + LANES] = k_extra.astype(BF16)

    out_blk = pl.BlockSpec((tr, FOX_H * LANES), lambda i: (i, 0))
    return pl.pallas_call(
        body, name="fox_prepare", grid=(S // tr,),
        in_specs=[pl.BlockSpec((tr, FOX_W), lambda i: (i, 0)), pl.BlockSpec((tr, FOX_W), lambda i: (i, 1)),
                  pl.BlockSpec((S, LANES), lambda i: (0, 0)), pl.BlockSpec((1, LANES), lambda i: (0, 0))],
        out_specs=[out_blk, out_blk],
        out_shape=[jax.ShapeDtypeStruct((S, FOX_H * LANES), BF16)] * 2,
        scratch_shapes=[pltpu.VMEM((S, LANES), F32)] * 3,
        compiler_params=_cparams(("arbitrary",)),
    )(proj_a, proj_a, proj_f, b_pad)


def _fox_scores(qa, ka, w):
    s = _dot(qa, ka, _NT)
    row = lax.broadcasted_iota(jnp.int32, (FOX_TQ, FOX_TQ), 0)
    col = lax.broadcasted_iota(jnp.int32, (FOX_TQ, FOX_TQ), 1)
    diag = jnp.where(col <= row, s[:, w * FOX_TQ:], NEG_INF)
    return diag if w == 0 else jnp.concatenate([s[:, :w * FOX_TQ], diag], axis=1)


def _fox_fwd(q_aug, k_aug, proj_a):
    def body(qa_ref, ka_ref, v_ref, o_ref, lse_ref):
        qi = pl.program_id(1)
        lane = lax.broadcasted_iota(jnp.int32, (FOX_TQ, LANES), 1)
        for w in range(FOX_NQ):
            @pl.when(qi == w)
            def _(w=w):
                width = (w + 1) * FOX_TQ
                lse = jnp.zeros((FOX_TQ, LANES), F32)
                for hh in range(HEADS_PER_LB):
                    aug = slice(hh * LANES, (hh + 1) * LANES)
                    sl = slice(hh * HD, (hh + 1) * HD)
                    s = _fox_scores(qa_ref[:, aug], ka_ref[:width, aug], w)
                    m = jnp.max(s, axis=-1, keepdims=True)
                    p = jnp.exp(s - m)
                    l = jnp.sum(p, axis=-1, keepdims=True)
                    o_ref[:, sl] = _dot(p / l, v_ref[:width, sl], _NN)
                    lse = jnp.where(lane == hh, m + jnp.log(l), lse)
                lse_ref[...] = lse

    return pl.pallas_call(
        body, name="fox_fwd", grid=(FOX_HP, FOX_NQ),
        in_specs=[pl.BlockSpec((FOX_TQ, FOX_AUG), lambda h, i: (i, h)),
                  pl.BlockSpec((S, FOX_AUG), lambda h, i: (0, h)),
                  pl.BlockSpec((S, LANES), lambda h, i: (0, 2 * FOX_HP + h))],
        out_specs=[pl.BlockSpec((FOX_TQ, LANES), lambda h, i: (i, h)),
                   pl.BlockSpec((None, FOX_TQ, LANES), lambda h, i: (h, i, 0))],
        out_shape=[jax.ShapeDtypeStruct((S, FOX_W), F32), jax.ShapeDtypeStruct((FOX_HP, S, LANES), F32)],
        compiler_params=_cparams(("parallel", "parallel")),
    )(q_aug, k_aug, proj_a)


def _fox_bwd(q_aug, k_aug, proj_a, lse, do, dproj, deps=()):
    n_q = FOX_NQ

    def body(qa_ref, ka_ref, v_ref, lse_ref, do_ref, *refs):
        dproj_ref, dft_ref, dfs_ref, dk_acc, dv_acc, dq_buf, kv_buf, dq_sems, kv_sems = refs[1 + len(deps):]
        hp = pl.program_id(0)
        t = pl.program_id(1)
        slot = t % 2
        col = pl.multiple_of(hp * LANES, LANES)

        def dq_copy(step, buf_slot):
            rows = pl.ds(pl.multiple_of(step * FOX_TQ, FOX_TQ), FOX_TQ)
            return pltpu.make_async_copy(dq_buf.at[buf_slot], dproj_ref.at[rows, pl.ds(col, LANES)], dq_sems.at[buf_slot])

        @pl.when(t == 0)
        def _():
            dk_acc[...] = jnp.zeros_like(dk_acc)
            dv_acc[...] = jnp.zeros_like(dv_acc)

        @pl.when((t == 0) & (hp == 0))
        def _():
            dfs_ref[...] = jnp.zeros_like(dfs_ref)

        @pl.when(t >= 2)
        def _():
            dq_copy(t - 2, slot).wait()

        lane = lax.broadcasted_iota(jnp.int32, (FOX_TQ, LANES), 1)
        for w in range(n_q):
            @pl.when(t == w)
            def _(w=w):
                width = (w + 1) * FOX_TQ
                lse_all = lse_ref[...]
                sub = lax.broadcasted_iota(jnp.int32, (FOX_H, width), 0)
                dft = jnp.zeros((FOX_TQ, LANES), F32)
                for hh in range(HEADS_PER_LB):
                    aug = slice(hh * LANES, (hh + 1) * LANES)
                    sl = slice(hh * HD, (hh + 1) * HD)
                    head = hp * HEADS_PER_LB + hh
                    qa = qa_ref[:, aug]
                    ka = ka_ref[:width, aug]
                    do_h = do_ref[:, sl]
                    s = _fox_scores(qa, ka, w)
                    p = jnp.exp(s - lse_all[:, hh:hh + 1])
                    dp = _dot(do_h, v_ref[:width, sl], _NT)
                    ds = p * (dp - jnp.sum(dp * p, axis=-1, keepdims=True))
                    dft = jnp.where(lane == hh, jnp.sum(ds, axis=-1, keepdims=True), dft)
                    dfs_ref[:, :width] -= jnp.where(sub == head, jnp.sum(ds, axis=0, keepdims=True), 0.0)
                    dq_buf[slot, :, sl] = (_dot(ds, ka[:, :HD], _NN) * SCALE).astype(BF16)
                    dk_acc[:width, sl] += _dot(ds, qa[:, :HD], _TN)
                    dv_acc[:width, sl] += _dot(p, do_h, _TN)
                dft_ref[...] = dft

        dq_copy(t, slot).start()

        @pl.when(t == n_q - 1)
        def _():
            dq_copy(t - 1, 1 - slot).wait()
            dq_copy(t, slot).wait()
            kv_buf[0] = dk_acc[...].astype(BF16)
            kv_buf[1] = dv_acc[...].astype(BF16)
            copies = [pltpu.make_async_copy(
                kv_buf.at[j], dproj_ref.at[:, pl.ds(pl.multiple_of((j + 1) * FOX_W + hp * LANES, LANES), LANES)],
                kv_sems.at[j]) for j in range(2)]
            for cp in copies:
                cp.start()
            for cp in copies:
                cp.wait()

    qblk = pl.BlockSpec((FOX_TQ, FOX_AUG), lambda h, t: (t, h))
    lane_blk = pl.BlockSpec((None, FOX_TQ, LANES), lambda h, t: (h, t, 0))
    return pl.pallas_call(
        body, name="fox_bwd", grid=(FOX_HP, n_q),
        in_specs=[qblk, pl.BlockSpec((S, FOX_AUG), lambda h, t: (0, h)),
                  pl.BlockSpec((S, LANES), lambda h, t: (0, 2 * FOX_HP + h)),
                  lane_blk, pl.BlockSpec((FOX_TQ, LANES), lambda h, t: (t, h)), ANY] + [ANY] * len(deps),
        out_specs=[ANY, lane_blk, pl.BlockSpec((FOX_H, S), lambda h, t: (0, 0))],
        out_shape=[jax.ShapeDtypeStruct((S, D_PAD), BF16),
                   jax.ShapeDtypeStruct((FOX_HP, S, LANES), F32), jax.ShapeDtypeStruct((FOX_H, S), F32)],
        scratch_shapes=[pltpu.VMEM((S, LANES), F32), pltpu.VMEM((S, LANES), F32),
                        pltpu.VMEM((2, FOX_TQ, LANES), BF16), pltpu.VMEM((2, S, LANES), BF16),
                        pltpu.SemaphoreType.DMA((2,)), pltpu.SemaphoreType.DMA((2,))],
        input_output_aliases={5: 0},
        compiler_params=_cparams(("arbitrary", "arbitrary")),
    )(q_aug, k_aug, proj_a, lse, do, dproj, *deps)


def _rope_tables():
    half = ROPE_DIM // 2
    shape = (DIL_G, S, half)
    g = lax.broadcasted_iota(jnp.int32, shape, 0)
    row = lax.broadcasted_iota(jnp.int32, shape, 1)
    r = jnp.left_shift(1, 2 * g)
    per_class = S // r
    pos = (row % per_class) * r + row // per_class
    inv_freq = jnp.power(jnp.float32(ROPE_THETA), -lax.broadcasted_iota(F32, shape, 2) * 2.0 / ROPE_DIM)
    ang = pos.astype(F32) * inv_freq
    return jnp.concatenate([jnp.cos(ang), jnp.sin(ang)], axis=-1)


def _rope_factors(cs):
    half = ROPE_DIM // 2
    i = lax.broadcasted_iota(jnp.int32, (2 * half, 2 * LANES), 0)
    l = lax.broadcasted_iota(jnp.int32, (2 * half, 2 * LANES), 1)
    hit = ((l < LANES) & (i == l % half)) | ((l >= LANES) & (i - half == l % half))
    spread = jnp.where(hit, 1.0, 0.0).astype(BF16)
    hi = cs.astype(BF16)
    r1 = cs - hi.astype(F32)
    mid = r1.astype(BF16)
    lo = (r1 - mid.astype(F32)).astype(BF16)
    full = (_dot(hi, spread, _NN) + _dot(mid, spread, _NN)) + _dot(lo, spread, _NN)
    cos, sin = full[:, :LANES], full[:, LANES:]
    lane = lax.broadcasted_iota(jnp.int32, cos.shape, 1) % HD
    t_self = jnp.where(lane < ROPE_DIM, cos, 1.0)
    t_up = jnp.where(lane < half, -sin, 0.0)
    t_dn = jnp.where((lane >= half) & (lane < ROPE_DIM), sin, 0.0)
    return t_self, t_up, t_dn


def _residue_pieces(r):
    if r == 1:
        return [(slice(i, i + 512), slice(i, i + 512)) for i in range(0, S, 512)]
    n = S // r
    return [(pl.ds(j, n, stride=r), slice(j * n, (j + 1) * n)) for j in range(r)]


def _rope_apply(x, a, u, d, backward):
    half = ROPE_DIM // 2
    if backward:
        return x * a + pltpu.roll(x * u, half, 1) + pltpu.roll(x * d, LANES - half, 1)
    return x * a + pltpu.roll(x, LANES - half, 1) * u + pltpu.roll(x, half, 1) * d


LB_PER_CB = CB // LANES
ROPE_NB = 3 * DIL_G * LB_PER_CB


def _rope_step(s):
    per_group = 3 * LB_PER_CB
    return s // per_group, (s % per_group) // LB_PER_CB, s % LB_PER_CB


def _rope_split(proj_b, tables):
    def body(x_ref, cs_ref, o_ref):
        g, t, _ = _rope_step(pl.program_id(0))
        for gs in range(DIL_G):
            @pl.when(g == gs)
            def _(gs=gs):
                for tok, sub in _residue_pieces(DIL_R[gs]):
                    x = x_ref[tok, :]
                    y = _rope_apply(x, *_rope_factors(cs_ref[sub, :]), False)
                    o_ref[sub, :] = jnp.where(t < 2, y, x).astype(BF16)

    def in_index(s):
        g, t, hf = _rope_step(s)
        return 0, (t * DIL_G + g) * LB_PER_CB + hf

    def out_index(s):
        g, t, hf = _rope_step(s)
        return t * DIL_G + g, 0, hf

    tab = pl.BlockSpec((None, S, 2 * (ROPE_DIM // 2)), lambda s: (_rope_step(s)[0], 0, 0))
    return pl.pallas_call(
        body, name="rope_split", grid=(ROPE_NB,),
        in_specs=[pl.BlockSpec((S, LANES), in_index), tab],
        out_specs=pl.BlockSpec((None, S, LANES), out_index),
        out_shape=jax.ShapeDtypeStruct((3 * DIL_G, S, CB), BF16),
        compiler_params=_cparams(("parallel",)),
    )(proj_b, tables)


def _rope_merge_bwd(dq3, dk3, dv3, tables, dproj):
    def body(dq_ref, dk_ref, dv_ref, cs_ref, _, o_ref, tmp):
        s = pl.program_id(0)
        g, t, _ = _rope_step(s)
        for gs in range(DIL_G):
            @pl.when((g == gs) & (s < ROPE_NB))
            def _(gs=gs):
                for tok, sub in _residue_pieces(DIL_R[gs]):
                    x = jnp.where(t == 0, dq_ref[sub, :], jnp.where(t == 1, dk_ref[sub, :], dv_ref[sub, :]))
                    y = _rope_apply(x, *_rope_factors(cs_ref[sub, :]), True)
                    tmp[tok, :] = jnp.where(t < 2, y, x)
                o_ref[...] = tmp[...].astype(BF16)

        @pl.when(s >= ROPE_NB)
        def _():
            o_ref[...] = jnp.zeros_like(o_ref)

    def src_spec(own):
        def index(s):
            g, t, hf = _rope_step(jnp.minimum(s, ROPE_NB - 1))
            return g, 0, jnp.where(t == own, hf, jnp.where(t > own, LB_PER_CB - 1, 0))
        return pl.BlockSpec((None, S, LANES), index)

    def out_index(s):
        g, t, hf = _rope_step(s)
        col = SEG_B // LANES + (t * DIL_G + g) * LB_PER_CB + hf
        return 0, jnp.where(s < ROPE_NB, col, SEG_Z // LANES + s - ROPE_NB)

    tab = pl.BlockSpec((None, S, 2 * (ROPE_DIM // 2)), lambda s: (_rope_step(jnp.minimum(s, ROPE_NB - 1))[0], 0, 0))
    return pl.pallas_call(
        body, name="rope_merge_bwd", grid=(ROPE_NB + LB_PER_CB,),
        in_specs=[src_spec(0), src_spec(1), src_spec(2), tab, ANY],
        out_specs=pl.BlockSpec((S, LANES), out_index),
        out_shape=jax.ShapeDtypeStruct((S, D_PAD), BF16),
        scratch_shapes=[pltpu.VMEM((S, LANES), F32)],
        input_output_aliases={4: 0},
        compiler_params=_cparams(("arbitrary",)),
    )(dq3, dk3, dv3, tables, dproj)


DIL_NB = S // DIL_BLK


_BQK = (((2,), (2,)), ((0,), (0,)))
_BQD = (((2,), (1,)), ((0,), (0,)))
_BKD = (((1,), (1,)), ((0,), (0,)))


def _dil_mask(g):
    shape = (DIL_NB, DIL_BLK, 2 * DIL_BLK)
    blocks_per_seq = jnp.right_shift(DIL_NB, 2 * g)
    has_prev = jnp.bitwise_and(lax.broadcasted_iota(jnp.int32, shape, 0), blocks_per_seq - 1) != 0
    a = lax.broadcasted_iota(jnp.int32, shape, 1)
    kk = lax.broadcasted_iota(jnp.int32, shape, 2)
    diff = DIL_BLK + a - kk
    return (diff >= 0) & (diff <= DIL_BLK) & ((kk >= DIL_BLK) | has_prev)


def _blocks(t):
    return t.reshape(DIL_NB, DIL_BLK, t.shape[-1])


def _with_prev(t):
    prev = jnp.concatenate([jnp.zeros((DIL_BLK, t.shape[-1]), t.dtype), t[:-DIL_BLK]], axis=0)
    return jnp.concatenate([_blocks(prev), _blocks(t)], axis=1)


def _fold_prev(t2):
    n = t2.shape[-1]
    to_prev = t2[:, :DIL_BLK].reshape(S, n)
    own = t2[:, DIL_BLK:].reshape(S, n)
    return own + jnp.concatenate([to_prev[DIL_BLK:], jnp.zeros((DIL_BLK, n), t2.dtype)], axis=0)


def _dil_group_spec(t, width=DIL_OUT):
    return pl.BlockSpec((None, S, width), lambda g: (t * DIL_G + g, 0, 0))


def _dil_fwd(qkv3):
    def body(q_ref, k_ref, v_ref, o_ref, lse_ref):
        ok = _dil_mask(pl.program_id(0))
        lane = lax.broadcasted_iota(jnp.int32, (S, LANES), 1)
        lse = jnp.zeros((S, LANES), F32)
        for h in range(DIL_HG):
            sl = slice(h * HD, (h + 1) * HD)
            s = jnp.where(ok, _dot(_blocks(q_ref[:, sl]), _with_prev(k_ref[:, sl]), _BQK) * SCALE, NEG_INF)
            m = jnp.max(s, axis=-1, keepdims=True)
            p = jnp.exp(s - m)
            l = jnp.sum(p, axis=-1, keepdims=True)
            o_ref[:, sl] = _dot(p / l, _with_prev(v_ref[:, sl]), _BQD).reshape(S, HD)
            lse = jnp.where(lane == h, (m + jnp.log(l)).reshape(S, 1), lse)
        lse_ref[...] = lse

    return pl.pallas_call(
        body, name="dil_fwd", grid=(DIL_G,),
        in_specs=[_dil_group_spec(0), _dil_group_spec(1), _dil_group_spec(2)],
        out_specs=[_dil_group_spec(0), _dil_group_spec(0, LANES)],
        out_shape=[jax.ShapeDtypeStruct((DIL_G, S, DIL_OUT), F32), jax.ShapeDtypeStruct((DIL_G, S, LANES), F32)],
        compiler_params=_cparams(("parallel",)),
    )(qkv3, qkv3, qkv3)


def _dil_bwd(qkv3, lse3, do3, c3):
    def body(q_ref, k_ref, v_ref, lse_ref, do_ref, c_ref, dq_ref, dk_ref, dv_ref):
        ok = _dil_mask(pl.program_id(0))
        lse = lse_ref[...]
        c = c_ref[...]
        for h in range(DIL_HG):
            sl = slice(h * HD, (h + 1) * HD)
            q = _blocks(q_ref[:, sl])
            k2 = _with_prev(k_ref[:, sl])
            do_h = _blocks(do_ref[:, sl])
            s = jnp.where(ok, _dot(q, k2, _BQK) * SCALE, NEG_INF)
            p = jnp.exp(s - _blocks(lse[:, h:h + 1]))
            dp = _dot(do_h, _with_prev(v_ref[:, sl]), _BQK)
            ds = p * (dp - _blocks(c[:, h:h + 1]))
            dsb = (ds * SCALE).astype(BF16)
            dq_ref[:, sl] = _dot(dsb, k2, _BQD).reshape(S, HD)
            dk_ref[:, sl] = _fold_prev(_dot(dsb, q, _BKD))
            dv_ref[:, sl] = _fold_prev(_dot(p, do_h, _BKD))

    return pl.pallas_call(
        body, name="dil_bwd", grid=(DIL_G,),
        in_specs=[_dil_group_spec(0), _dil_group_spec(1), _dil_group_spec(2), _dil_group_spec(0, LANES),
                  _dil_group_spec(0), _dil_group_spec(0, LANES)],
        out_specs=[_dil_group_spec(0)] * 3,
        out_shape=[jax.ShapeDtypeStruct((DIL_G, S, DIL_OUT), F32)] * 3,
        compiler_params=_cparams(("parallel",)),
    )(qkv3, qkv3, qkv3, lse3, do3, c3)


COMB_T = 256


def _dil_combine(o3, lse3):
    heads_per_lb = LANES // HD

    def body(o_ref, lse_ref, ob_ref, al_ref, *scratch):
        o_tok = [scratch[LB_PER_CB * gg:LB_PER_CB * (gg + 1)] for gg in range(DIL_G)]
        lse_tok = scratch[LB_PER_CB * DIL_G:]
        g = pl.program_id(0)
        for gs in range(DIL_G):
            @pl.when(g == gs)
            def _(gs=gs):
                for tok, sub in _residue_pieces(DIL_R[gs]):
                    for hf in range(LB_PER_CB):
                        o_tok[gs][hf][tok, :] = o_ref[sub, hf * LANES:(hf + 1) * LANES]
                    lse_tok[gs][tok, :] = lse_ref[sub, :]

        @pl.when(g == DIL_G - 1)
        def _():
            def chunk(i, carry):
                rows = pl.ds(pl.multiple_of(i * COMB_T, COMB_T), COMB_T)
                lse = [lse_tok[gg][rows, :] for gg in range(DIL_G)]
                m = jnp.maximum(jnp.maximum(lse[0], lse[1]), lse[2])
                e = [jnp.exp(lse[gg] - m) for gg in range(DIL_G)]
                den = (e[0] + e[1]) + e[2]
                al = [e[gg] / den for gg in range(DIL_G)]
                for gg in range(DIL_G):
                    al_ref[gg, rows, :] = al[gg]
                for h in range(DIL_HG):
                    hf, sl = h // heads_per_lb, slice((h % heads_per_lb) * HD, (h % heads_per_lb + 1) * HD)
                    acc = al[0][:, h:h + 1] * o_tok[0][hf][rows, sl]
                    for gg in range(1, DIL_G):
                        acc = acc + al[gg][:, h:h + 1] * o_tok[gg][hf][rows, sl]
                    ob_ref[rows, h * HD:(h + 1) * HD] = acc
                return carry

            lax.fori_loop(0, S // COMB_T, chunk, 0)

    return pl.pallas_call(
        body, name="dil_combine", grid=(DIL_G,),
        in_specs=[pl.BlockSpec((None, S, DIL_OUT), lambda g: (g, 0, 0)),
                  pl.BlockSpec((None, S, LANES), lambda g: (g, 0, 0))],
        out_specs=[pl.BlockSpec((S, DIL_OUT), lambda g: (0, 0)),
                   pl.BlockSpec((DIL_G, S, LANES), lambda g: (0, 0, 0))],
        out_shape=[jax.ShapeDtypeStruct((S, DIL_OUT), F32), jax.ShapeDtypeStruct((DIL_G, S, LANES), F32)],
        scratch_shapes=[pltpu.VMEM((S, LANES), F32)] * (DIL_G * (LB_PER_CB + 1)),
        compiler_params=_cparams(("arbitrary",)),
    )(o3, lse3)


def _dil_combine_bwd(dob, ob, alpha, deps=()):
    heads_per_lb = LANES // HD

    def body(dob_ref, ob_ref, al_ref, *refs):
        do_ref, c_ref = refs[len(deps):]
        g = pl.program_id(0)
        hf = pl.program_id(1)
        for gs in range(DIL_G):
            @pl.when(g == gs)
            def _(gs=gs):
                for tok, sub in _residue_pieces(DIL_R[gs]):
                    dob = dob_ref[tok, :]
                    prod = dob * ob_ref[tok, :]
                    al = al_ref[tok, :]
                    lane = lax.broadcasted_iota(jnp.int32, al.shape, 1)
                    c = jnp.where(hf == 0, 0.0, c_ref[sub, :])
                    for hh in range(heads_per_lb):
                        sl = slice(hh * HD, (hh + 1) * HD)
                        head = hf * heads_per_lb + hh
                        a = jnp.sum(jnp.where(lane == head, al, 0.0), axis=-1, keepdims=True)
                        do_ref[sub, sl] = (a * dob[:, sl]).astype(BF16)
                        c = jnp.where(lane == head, a * jnp.sum(prod[:, sl], axis=-1, keepdims=True), c)
                    c_ref[sub, :] = c

    half = pl.BlockSpec((S, LANES), lambda g, hf: (0, hf))
    return pl.pallas_call(
        body, name="dil_combine_bwd", grid=(DIL_G, LB_PER_CB),
        in_specs=[half, half, pl.BlockSpec((None, S, LANES), lambda g, hf: (g, 0, 0))] + [ANY] * len(deps),
        out_specs=[pl.BlockSpec((None, S, LANES), lambda g, hf: (g, 0, hf)),
                   pl.BlockSpec((None, S, LANES), lambda g, hf: (g, 0, 0))],
        out_shape=[jax.ShapeDtypeStruct((DIL_G, S, DIL_OUT), BF16), jax.ShapeDtypeStruct((DIL_G, S, LANES), F32)],
        compiler_params=_cparams(("parallel", "arbitrary")),
    )(dob, ob, alpha, *deps)


def _local_step(x, tgt, h1, g_attn, g_mlp, g_final, b_pad, w_in_t, hooks):
    tables = _rope_tables()

    first = hooks.first_deps()
    proj_a = _mm(h1, w_in_t, mode="nt", tm=S, tn=512, n=3 * FOX_W, out_dtypes=[BF16], name="proj_a", deps=first)
    proj_b = _mm(h1, w_in_t, mode="nt", tm=S, tn=DIL_W, n=3 * DIL_W, b_off=SEG_B // DIL_W, out_dtypes=[F32], name="proj_b",
                deps=first)
    proj_g = _mm(h1, w_in_t, mode="nt", tm=S, tn=D, n=2 * D, b_off=SEG_G // D, out_dtypes=[F32], name="proj_g",
                deps=first)
    proj_f = _mm(h1, w_in_t, mode="nt", tm=S, tn=LANES, n=LANES, b_off=SEG_F // LANES, out_dtypes=[F32], name="proj_f",
                deps=first)

    q_aug, k_aug = _fox_prepare(proj_a, proj_f, b_pad)
    oa, lse_a = _fox_fwd(q_aug, k_aug, proj_a)

    qkv3 = _rope_split(proj_b, tables)
    o3, lse3 = _dil_fwd(qkv3)
    ob, alpha = _dil_combine(o3, lse3)

    w_a, w_b, w_out = hooks.mixer_weights(ob)
    ya = _mm(oa, w_a, mode="nn", tm=S, tn=512, out_dtypes=[F32], name="branch_a")
    yb = _mm(ob, w_b, mode="nn", tm=S, tn=512, out_dtypes=[F32], name="branch_b")
    mixed = _gate_fwd(proj_g, ya, yb)

    def out_epilogue(acc, rows, vecs):
        x2 = rows[0] + acc
        return (x2, _rms_rows(x2, vecs[0])), ()

    x2, h2 = _mm_rows(mixed, w_out, tm=ROWS_TM, name="out_proj", epilogue=out_epilogue, rows=(x,), vecs=(g_mlp,),
                      row_out=(F32, BF16))
    w_up_sh, w_down = hooks.mlp_weights(h2)

    def up_epilogue(acc):
        r = jnp.maximum(acc, 0.0)
        return acc, r * r

    u, act = _mm(h2, w_up_sh, mode="nn", tm=S, tn=DFF // N_DEV, b_sharded=True, out_dtypes=[F32, BF16],
                 name="mlp_up", epilogue=up_epilogue)

    def loss_epilogue(acc, rows, vecs):
        dx3, dg, loss = _final_loss_rows(rows[0] + acc, vecs[0], rows[1])
        return (dx3,), (dg, loss)

    dx3, dg_final, loss = _mm_rows(act, w_down, tm=ROWS_TM, name="mlp_down_loss", epilogue=loss_epilogue,
                                   rows=(x2, tgt), vecs=(g_final,), row_out=(F32,), vec_out=(D, LANES))

    def rms_bwd_epilogue(acc, rows, vecs):
        dx, dg = _rms_bwd_rows(acc, rows[0], rows[1], vecs[0])
        return (dx,), (dg,)

    du = _mm(dx3, w_down, mode="nt", tm=S, tn=512, out_dtypes=[BF16], name="mlp_down_bwd",
             epilogue=lambda acc, u_t: (acc * (2.0 * jnp.maximum(u_t, 0.0)),), extras=(u,))
    dw_down = _mm(act, dx3, mode="tn", tm=512, tn=D, out_dtypes=[F32], name="dw_down")
    dw_up_sh = _mm(h2, du, mode="tn", tm=D, tn=DFF // N_DEV, out_sharded=True, out_dtypes=[F32], name="dw_up")
    dx2, dg_mlp = _mm_rows(du, w_up_sh, b_sharded=True, tm=ROWS_TM, name="mlp_up_bwd", epilogue=rms_bwd_epilogue,
                           rows=(x2, dx3), vecs=(g_mlp,), row_out=(F32,), vec_out=(D,),
                           deps=hooks.mlp_grads(dw_up_sh, dw_down))

    dmixed = _mm(dx2, w_out, mode="nt", tm=S, tn=512, out_dtypes=[F32], name="out_proj_bwd")
    dw_out = _mm(mixed, dx2, mode="tn", tm=D, tn=D, out_dtypes=[F32], name="dw_out")

    dyab, dproj = _gate_bwd(dmixed, proj_g, ya, yb)
    doa = _mm(dyab, w_a, mode="nt", tm=S, tn=FOX_W, k=D, a_off=0, out_dtypes=[BF16], name="branch_a_bwd")
    dw_a = _mm(oa, dyab, mode="tn", tm=FOX_W, tn=D, n=D, b_off=0, out_dtypes=[F32], name="dw_branch_a")
    dob = _mm(dyab, w_b, mode="nt", tm=S, tn=CB, k=D, a_off=1, out_dtypes=[F32], name="branch_b_bwd")
    dw_b = _mm(ob, dyab, mode="tn", tm=DIL_OUT, tn=D, n=D, b_off=1, out_dtypes=[F32], name="dw_branch_b")

    dproj, dft, dfs = _fox_bwd(q_aug, k_aug, proj_a, lse_a, doa, dproj, deps=hooks.mixer_grads(dw_a, dw_b, dw_out))
    dproj, db = _fox_dscan(dft, dfs, proj_f, b_pad, dproj)

    do3, c3 = _dil_combine_bwd(dob, ob, alpha, deps=hooks.mid_backward(db))
    dq3, dk3, dv3 = _dil_bwd(qkv3, lse3, do3, c3)
    dproj = _rope_merge_bwd(dq3, dk3, dv3, tables, dproj)

    dw_in_t = _mm(dproj, h1, mode="tn", tm=CB, tn=D, out_dtypes=[F32], name="dw_in")
    dx, dg_attn = _mm_rows(dproj, w_in_t, tm=ROWS_TM // 2, name="proj_bwd", epilogue=rms_bwd_epilogue,
                           rows=(x, dx2), vecs=(g_attn,), row_out=(F32,), vec_out=(D,),
                           deps=hooks.w_in_grad(dw_in_t))

    return loss, dx, (dg_attn, db, dg_mlp, dg_final)


SHARD_SHAPES = ((D_IN // N_DEV, D), (FOX_W, D // N_DEV), (DIL_OUT, D // N_DEV), (D // N_DEV, D),
                (D, DFF // N_DEV), (DFF // N_DEV, D))
W_NAMES = ("w_in", "w_a", "w_b", "w_out", "w_up", "w_down")
AG_COPIES = 7
HBM = pl.BlockSpec(memory_space=pltpu.HBM)
SEM = pl.BlockSpec(memory_space=pltpu.SEMAPHORE)


def _place():
    return lax.axis_index("x"), lax.axis_index("y"), lax.axis_index("c")


W_SLAB = 752
W_WIN = 272
PAD_BLK = 256


def _pad_runs():
    runs = sorted((pad, pad + b - a, p, a) for p in range(N_DEV) for a, b, pad in _shard_pieces(p))
    blocks = []
    for k in range(D_PAD // PAD_BLK):
        lo, hi = k * PAD_BLK, (k + 1) * PAD_BLK
        blocks.append([(max(lo, r0) - lo, min(hi, r1) - lo, p, a + max(lo, r0) - r0)
                       for r0, r1, p, a in runs if max(lo, r0) < min(hi, r1)])
    return blocks


def _gather_w_in(shard, x_in, g_attn):
    blocks = _pad_runs()

    norm_t = 512
    n_chunks = S // norm_t

    def body(x_ref, xin_ref, g_ref, out_ref, h1_ref, stage, land, xbuf, send_sems, recv_sems, load_sem, x_sems):
        x, y, c = _place()
        me, sibling = (x, y, c), (x, y, 1 - c)
        chips = [(1 - x, y), (x, 1 - y), (1 - x, 1 - y)]

        def slot(px, py, pc):
            return land.at[4 * px + 2 * py + pc]

        def copy(k, block, to):
            return pltpu.make_async_remote_copy(
                src_ref=slot(*block), dst_ref=slot(*block), send_sem=send_sems.at[k], recv_sem=recv_sems.at[k],
                device_id=to, device_id_type=MESH)

        load = pltpu.make_async_copy(x_ref, stage, load_sem)
        load.start()
        load.wait()
        land[4 * x + 2 * y + c] = stage[...].astype(BF16)
        sent = [copy(0, me, sibling)] + [copy(1 + j, me, (*chip, c)) for j, chip in enumerate(chips)]
        for cp in sent:
            cp.start()

        def x_load(i):
            return pltpu.make_async_copy(xin_ref.at[pl.ds(i * norm_t, norm_t)], xbuf.at[i % 2], x_sems.at[i % 2])

        x_load(0).start()
        for i in range(n_chunks):
            if i + 1 < n_chunks:
                x_load(i + 1).start()
            x_load(i).wait()
            h1_ref[i * norm_t:(i + 1) * norm_t, :] = _rms_rows(xbuf[i % 2], g_ref[...]).astype(BF16)

        for j, chip in enumerate(chips):
            copy(1 + j, (*chip, c), me).wait_recv()
            sent.append(copy(4 + j, (*chip, c), sibling))
            sent[-1].start()
        copy(0, sibling, me).wait_recv()
        for j, chip in enumerate(chips):
            copy(4 + j, (*chip, 1 - c), me).wait_recv()
        for cp in sent:
            cp.wait_send()

        row = lax.broadcasted_iota(jnp.int32, (PAD_BLK, D), 0)
        for k, runs in enumerate(blocks):
            out = jnp.zeros((PAD_BLK, D), F32)
            for o_lo, o_hi, p, a in runs:
                start = min(a // 16 * 16, W_SLAB - W_WIN)
                win = land[p, start:start + W_WIN, :].astype(F32)
                moved = pltpu.roll(win, (o_lo - (a - start)) % W_WIN, 0)[:PAD_BLK]
                out = jnp.where((row >= o_lo) & (row < o_hi), moved, out)
            out_ref[k * PAD_BLK:(k + 1) * PAD_BLK, :] = out.astype(BF16)

    return pl.pallas_call(
        body, name="all_gather_w_in",
        out_shape=[jax.ShapeDtypeStruct((D_PAD, D), BF16), jax.ShapeDtypeStruct((S, D), BF16)],
        in_specs=[ANY, ANY, pl.BlockSpec(memory_space=pltpu.VMEM)],
        out_specs=[pl.BlockSpec(memory_space=pltpu.VMEM)] * 2,
        scratch_shapes=[pltpu.VMEM((W_SLAB, D), F32), pltpu.VMEM((N_DEV, W_SLAB, D), BF16),
                        pltpu.VMEM((2, norm_t, D), F32),
                        pltpu.SemaphoreType.DMA((AG_COPIES,)), pltpu.SemaphoreType.DMA((AG_COPIES,)),
                        pltpu.SemaphoreType.DMA(()), pltpu.SemaphoreType.DMA((2,))],
        compiler_params=_cparams(None),
    )(shard, x_in, g_attn)


def _place_own(shards):
    n_w = len(shards)
    shapes = [t.shape for t in shards]

    def body(*refs):
        in_refs, out_refs = refs[:n_w], refs[n_w:2 * n_w]
        stage, cast = refs[2 * n_w:3 * n_w], refs[3 * n_w:4 * n_w]
        load_sems, store_sems = refs[4 * n_w:]
        x, y, c = _place()
        loads = [pltpu.make_async_copy(in_refs[i], stage[i], load_sems.at[i]) for i in range(n_w)]
        for ld in loads:
            ld.start()
        stores = []
        for i in range(n_w):
            loads[i].wait()
            cast[i][...] = stage[i][...].astype(BF16)
            stores.append(pltpu.make_async_copy(cast[i], out_refs[i].at[4 * x + 2 * y + c], store_sems.at[i]))
            stores[-1].start()
        for st in stores:
            st.wait()

    return pl.pallas_call(
        body, name="place_own_shards",
        out_shape=[jax.ShapeDtypeStruct((N_DEV,) + sh, BF16) for sh in shapes],
        in_specs=[ANY] * n_w, out_specs=[ANY] * n_w,
        scratch_shapes=[pltpu.VMEM(sh, F32) for sh in shapes] + [pltpu.VMEM(sh, BF16) for sh in shapes]
                       + [pltpu.SemaphoreType.DMA((n_w,)), pltpu.SemaphoreType.DMA((n_w,))],
        compiler_params=_cparams(None),
    )(*shards)


PEER_COPIES = N_DEV - 1
SIBLING_COPIES = 4
CHIP_COPIES = 3


def _peer_copies(_, land_refs, send_sems, recv_sems):
    x, y, c = _place()
    mine = 4 * x + 2 * y + c
    copies = []
    for i, ref in enumerate(land_refs):
        for k in range(1, N_DEV):
            peer = (x ^ (k >> 2), y ^ ((k >> 1) & 1), c ^ (k & 1))
            n = i * PEER_COPIES + k - 1
            copies.append(pltpu.make_async_remote_copy(
                src_ref=ref.at[mine], dst_ref=ref.at[mine], send_sem=send_sems.at[n], recv_sem=recv_sems.at[n],
                device_id=peer, device_id_type=MESH))
    return copies


def _sibling_copies(g_refs, r_refs, send_sems, recv_sems):
    x, y, c = _place()
    return [pltpu.make_async_remote_copy(
        src_ref=g_refs[i].at[2 * k + (1 - c)], dst_ref=r_refs[i].at[k],
        send_sem=send_sems.at[SIBLING_COPIES * i + k], recv_sem=recv_sems.at[SIBLING_COPIES * i + k],
        device_id=(x, y, 1 - c), device_id_type=MESH) for i in range(len(g_refs)) for k in range(SIBLING_COPIES)]


def _chip_copies(p_refs, r_refs, send_sems, recv_sems):
    x, y, c = _place()
    chips = [(1 - x, y), (x, 1 - y), (1 - x, 1 - y)]
    return [pltpu.make_async_remote_copy(
        src_ref=p_refs[i].at[2 * cx + cy], dst_ref=r_refs[i].at[j],
        send_sem=send_sems.at[CHIP_COPIES * i + j], recv_sem=recv_sems.at[CHIP_COPIES * i + j],
        device_id=(cx, cy, c), device_id_type=MESH) for i in range(len(p_refs)) for j, (cx, cy) in enumerate(chips)]


def _in_hbm(a):
    return pltpu.with_memory_space_constraint(a, pltpu.HBM)


def _exchange_start(name, copies_fn, n_copies, srcs, lands, after=()):
    n_s, n_l, n_a = len(srcs), len(lands), len(after)

    def body(*refs):
        outs = refs[n_s + n_l + n_a:]
        for cp in copies_fn(refs[:n_s], refs[n_s:n_s + n_l], outs[0], outs[1]):
            cp.start()
        outs[-1][...] = jnp.zeros_like(outs[-1])

    res = pl.pallas_call(
        body, name=name,
        out_shape=[pltpu.SemaphoreType.DMA((n_copies,)), pltpu.SemaphoreType.DMA((n_copies,))]
                  + [pltpu.HBM(t.shape, t.dtype) for t in (*srcs, *lands)] + [jax.ShapeDtypeStruct((8, LANES), F32)],
        in_specs=[HBM] * (n_s + n_l) + [ANY] * n_a,
        out_specs=[SEM, SEM] + [HBM] * (n_s + n_l) + [pl.BlockSpec(memory_space=pltpu.VMEM)],
        input_output_aliases={i: 2 + i for i in range(n_s + n_l)},
        compiler_params=pltpu.CompilerParams(has_side_effects=pltpu.SideEffectType.DATAFLOW_SIDE_EFFECTING),
    )(*[_in_hbm(t) for t in (*srcs, *lands)], *after)
    return res[0], res[1], list(res[2:2 + n_s]), list(res[2 + n_s:2 + n_s + n_l]), res[-1]


def _exchange_wait(name, copies_fn, started, after):
    send_sems, recv_sems, srcs, lands, _ = started
    n_s, n_l = len(srcs), len(lands)

    def body(*refs):
        for cp in copies_fn(refs[:n_s], refs[n_s:n_s + n_l], refs[n_s + n_l], refs[n_s + n_l + 1]):
            cp.wait_send()
            cp.wait_recv()

    res = pl.pallas_call(
        body, name=name,
        out_shape=[pltpu.HBM(t.shape, t.dtype) for t in (*srcs, *lands)],
        in_specs=[HBM] * (n_s + n_l) + [SEM, SEM, ANY],
        out_specs=[HBM] * (n_s + n_l),
        input_output_aliases={i: i for i in range(n_s + n_l)},
        compiler_params=pltpu.CompilerParams(has_side_effects=pltpu.SideEffectType.DATAFLOW_SIDE_EFFECTING),
    )(*srcs, *lands, send_sems, recv_sems, after)
    return list(res[:n_s]), list(res[n_s:])


def _shard_block(shape):
    rows, cols = shape
    if rows % 256:
        return rows, 256
    return min(rows, 512 if cols <= LANES else 256), cols


def _chip_partial(ids, g_stack, recv1, name):
    shape = g_stack.shape[1:]
    br, bc = _shard_block(shape)

    def body(ids_ref, g_ref, r_ref, pb_ref, own_ref):
        s = g_ref[...] + r_ref[...]
        pb_ref[...] = s.astype(BF16)

        @pl.when(pl.program_id(2) == ids_ref[1])
        def _():
            own_ref[...] = s

    grid_spec = pltpu.PrefetchScalarGridSpec(
        num_scalar_prefetch=1, grid=(shape[0] // br, shape[1] // bc, 4),
        in_specs=[pl.BlockSpec((None, br, bc), lambda r, q, k, ids: (2 * k + ids[0], r, q)),
                  pl.BlockSpec((None, br, bc), lambda r, q, k, ids: (k, r, q))],
        out_specs=[pl.BlockSpec((None, br, bc), lambda r, q, k, ids: (k, r, q)),
                   pl.BlockSpec((br, bc), lambda r, q, k, ids: (r, q))])
    return pl.pallas_call(
        body, name=name, grid_spec=grid_spec,
        out_shape=[jax.ShapeDtypeStruct((4,) + shape, BF16), jax.ShapeDtypeStruct(shape, F32)],
        compiler_params=_cparams(("parallel", "parallel", "arbitrary")),
    )(ids, g_stack, recv1)


def _adamw(w, g, m, v):
    m = ADAM_B1 * m + (1.0 - ADAM_B1) * g
    v = ADAM_B2 * v + (1.0 - ADAM_B2) * (g * g)
    m_hat = m / (1.0 - ADAM_B1 ** ADAM_STEP)
    v_hat = v / (1.0 - ADAM_B2 ** ADAM_STEP)
    delta = -ADAM_LR * (m_hat / (jnp.sqrt(v_hat) + ADAM_EPS) + ADAM_WD * w)
    return delta, m, v


def _reduce_adamw(own, recv2, w, m, v, name, deps=()):
    shape = own.shape
    br, bc = _shard_block(shape)

    def body(own_ref, r_ref, w_ref, m_ref, v_ref, *refs):
        g_ref, d_ref, nm_ref, nv_ref = refs[len(deps):]
        g = own_ref[...]
        for j in range(3):
            g = g + r_ref[j].astype(F32)
        delta, nm, nv = _adamw(w_ref[...], g, m_ref[...], v_ref[...])
        g_ref[...] = g
        d_ref[...] = delta
        nm_ref[...] = nm
        nv_ref[...] = nv

    blk = pl.BlockSpec((br, bc), lambda r, q: (r, q))
    return pl.pallas_call(
        body, name=name, grid=(shape[0] // br, shape[1] // bc),
        in_specs=[blk, pl.BlockSpec((3, br, bc), lambda r, q: (0, r, q)), blk, blk, blk] + [ANY] * len(deps),
        out_specs=[blk] * 4, out_shape=[jax.ShapeDtypeStruct(shape, F32)] * 4,
        compiler_params=_cparams(("parallel", "parallel")),
    )(own, recv2, w, m, v, *deps)


def _small_allreduce_adamw(gvec, w, m, v, deps=()):
    def body(g_ref, w_ref, m_ref, v_ref, *refs):
        go_ref, d_ref, nm_ref, nv_ref, buf, send_sems, recv_sems = refs[len(deps):]
        x, y, c = _place()
        my_slot = 4 * x + 2 * y + c
        buf[my_slot] = g_ref[...]
        copies = []
        for k in range(1, N_DEV):
            px, py, pc = x ^ (k >> 2), y ^ ((k >> 1) & 1), c ^ (k & 1)
            copies.append(pltpu.make_async_remote_copy(
                src_ref=g_ref, dst_ref=buf.at[my_slot], send_sem=send_sems.at[k - 1], recv_sem=recv_sems.at[k - 1],
                device_id=(px, py, pc), device_id_type=MESH))
        for cp in copies:
            cp.start()
        for k in range(1, N_DEV):
            px, py, pc = x ^ (k >> 2), y ^ ((k >> 1) & 1), c ^ (k & 1)
            pltpu.make_async_remote_copy(
                src_ref=g_ref, dst_ref=buf.at[4 * px + 2 * py + pc], send_sem=send_sems.at[k - 1],
                recv_sem=recv_sems.at[k - 1], device_id=(px, py, pc), device_id_type=MESH).wait_recv()
        for cp in copies:
            cp.wait_send()
        g = buf[0]
        for s in range(1, N_DEV):
            g = g + buf[s]
        delta, nm, nv = _adamw(w_ref[...], g, m_ref[...], v_ref[...])
        go_ref[...] = g
        d_ref[...] = delta
        nm_ref[...] = nm
        nv_ref[...] = nv

    vm = pl.BlockSpec(memory_space=pltpu.VMEM)
    return pl.pallas_call(
        body, name="small_allreduce_adamw",
        in_specs=[vm] * 4 + [ANY] * len(deps), out_specs=[vm] * 4,
        out_shape=[jax.ShapeDtypeStruct((SMALL_R, LANES), F32)] * 4,
        scratch_shapes=[pltpu.VMEM((N_DEV, SMALL_R, LANES), F32),
                        pltpu.SemaphoreType.DMA((N_DEV - 1,)), pltpu.SemaphoreType.DMA((N_DEV - 1,))],
    )(gvec, w, m, v, *deps)


def _cols_to_whole(stack):
    rows = stack.shape[1]
    return stack.transpose(1, 0, 2).reshape(rows, -1)


def _whole_to_cols(t):
    rows = t.shape[0]
    return t.reshape(rows, N_DEV, -1).transpose(1, 0, 2)


W_IN_SEGMENTS = ((0, 3 * FOX_W, SEG_A), (3 * FOX_W, 3 * FOX_W + FOX_H, SEG_F),
                 (3 * FOX_W + FOX_H, 3 * FOX_W + FOX_H + 3 * DIL_W, SEG_B), (3 * FOX_W + FOX_H + 3 * DIL_W, D_IN, SEG_G))


def _shard_pieces(p):
    rows = D_IN // N_DEV
    lo, hi = p * rows, (p + 1) * rows
    return [(max(lo, a) - lo, min(hi, b) - lo, pad + max(lo, a) - a)
            for a, b, pad in W_IN_SEGMENTS if max(lo, a) < min(hi, b)]


def _unpad_dw_in_t(dwp):
    return jnp.stack([jnp.concatenate([dwp[pad:pad + b - a] for a, b, pad in _shard_pieces(p)], axis=0)
                      for p in range(N_DEV)])


LOSS_ROW = 25


def _pack_small(g_attn, b, g_mlp, g_final, loss_row=None):
    tail = jnp.pad(b, ((0, 7), (0, LANES - b.shape[1])))
    if loss_row is not None:
        tail = tail + jnp.pad(loss_row, ((LOSS_ROW - 24, 31 - LOSS_ROW), (0, 0)))
    return jnp.concatenate([g_attn.reshape(8, LANES), g_mlp.reshape(8, LANES), g_final.reshape(8, LANES), tail], axis=0)


def _unpack_small(p):
    return (p[0:8].reshape(1, D), p[24:25, :FOX_H], p[8:16].reshape(1, D), p[16:24].reshape(D))


class _StepComm:
    def __init__(self, ids, w_sh, m_sh, v_sh, after_w_in):
        self.ids = ids
        self.w_sh, self.m_sh, self.v_sh = w_sh, m_sh, v_sh
        self.updates = None
        lands = _place_own(w_sh[1:])
        self.gather_mixer = _exchange_start("gather_mixer_start", _peer_copies, PEER_COPIES * 3, [], lands[:3],
                                            after=(after_w_in,))
        self.gather_mlp = _exchange_start("gather_mlp_start", _peer_copies, PEER_COPIES * 2, [], lands[3:],
                                          after=(self.gather_mixer[-1],))
        self.sibling = {}
        self.chips = {}
        self.own = {}
        self.names = {}

    def _reduce_start(self, group, names, grads, after=()):
        lands = [lax.empty((SIBLING_COPIES,) + t.shape[1:], F32) for t in grads]
        self.sibling[group] = _exchange_start("grad_%s_sibling_start" % group, _sibling_copies,
                                              SIBLING_COPIES * len(grads), grads, lands, after=after)
        self.names[group] = names
        return self.sibling[group][-1]

    def _reduce_mid(self, group, after):
        grads, recv1 = _exchange_wait("grad_%s_sibling_wait" % group, _sibling_copies, self.sibling[group], after)
        parts = [_chip_partial(self.ids, g, r, "grad_partial_" + n) for g, r, n in zip(grads, recv1, self.names[group])]
        self.own[group] = [p[1] for p in parts]
        srcs = [p[0] for p in parts]
        lands = [lax.empty((CHIP_COPIES,) + t.shape[1:], BF16) for t in srcs]
        self.chips[group] = _exchange_start("grad_%s_chips_start" % group, _chip_copies, CHIP_COPIES * len(srcs),
                                            srcs, lands)
        return self.chips[group][-1]

    def reduced(self, group, after):
        _, recv2 = _exchange_wait("grad_%s_chips_wait" % group, _chip_copies, self.chips[group], after)
        return list(zip(self.own[group], recv2))

    def first_deps(self):
        return [self.gather_mlp[-1]]

    def mixer_weights(self, after):
        _, (g_a, g_b, g_out) = _exchange_wait("gather_mixer_wait", _peer_copies, self.gather_mixer, after)
        return _cols_to_whole(g_a), _cols_to_whole(g_b), g_out.reshape(D, D)

    def mlp_weights(self, after):
        _, (g_up, g_down) = _exchange_wait("gather_mlp_wait", _peer_copies, self.gather_mlp, after)
        return g_up, g_down.reshape(DFF, D)

    def mlp_grads(self, dw_up_sh, dw_down):
        return [self._reduce_start("mlp", W_NAMES[4:], [dw_up_sh, dw_down.reshape((N_DEV,) + SHARD_SHAPES[5])])]

    def mixer_grads(self, dw_a, dw_b, dw_out):
        token = self._reduce_mid("mlp", dw_b)
        grads = [_whole_to_cols(dw_a), _whole_to_cols(dw_b), dw_out.reshape((N_DEV,) + SHARD_SHAPES[3])]
        return [self._reduce_start("mixer", W_NAMES[1:4], grads, after=(token,))]

    def mid_backward(self, after):
        return [self._reduce_mid("mixer", after)]

    def w_in_grad(self, dw_in_t):
        grads = [_unpad_dw_in_t(dw_in_t)]
        token = self._reduce_start("w_in", W_NAMES[:1], grads)
        reduced = self.reduced("mixer", token) + self.reduced("mlp", token)
        self.updates, last = [None] * len(reduced), token
        for i in (3, 4, 0, 1, 2):
            if i == 0:
                last = self._reduce_mid("w_in", last)
            self.updates[i] = _reduce_adamw(*reduced[i], self.w_sh[1 + i], self.m_sh[1 + i], self.v_sh[1 + i],
                                            "adamw_" + W_NAMES[1 + i], deps=[last])
            last = self.updates[i][0]
        return [last]

    def w_in_update(self, after):
        (reduced,) = self.reduced("w_in", after)
        return _reduce_adamw(*reduced, self.w_sh[0], self.m_sh[0], self.v_sh[0], "adamw_" + W_NAMES[0])


def kernel(x, norm_attn_g, w_in, b_forget, w_branch_a, w_branch_b, w_out, norm_mlp_g, w_up, w_down, norm_final_g, loss_target, m_norm_attn_g, m_w_in, m_b_forget, m_w_branch_a, m_w_branch_b, m_w_out, m_norm_mlp_g, m_w_up, m_w_down, m_norm_final_g, v_norm_attn_g, v_w_in, v_b_forget, v_w_branch_a, v_w_branch_b, v_w_out, v_norm_mlp_g, v_w_up, v_w_down, v_norm_final_g):
    cx, cy, cc = _place()
    ids = jnp.stack([cc, 2 * cx + cy]).astype(jnp.int32)

    w_sh = [w_in[0].T] + [t[0] for t in (w_branch_a, w_branch_b, w_out, w_up, w_down)]
    m_sh = [m_w_in[0].T] + [t[0] for t in (m_w_branch_a, m_w_branch_b, m_w_out, m_w_up, m_w_down)]
    v_sh = [v_w_in[0].T] + [t[0] for t in (v_w_branch_a, v_w_branch_b, v_w_out, v_w_up, v_w_down)]

    w_in_t, h1 = _gather_w_in(jnp.pad(w_sh[0], ((0, W_SLAB - w_sh[0].shape[0]), (0, 0))), x[0], norm_attn_g)
    comm = _StepComm(ids, w_sh, m_sh, v_sh, w_in_t)
    b_pad = jnp.pad(b_forget, ((0, 0), (0, LANES - FOX_H)))

    loss_row, dx, dsmall = _local_step(
        x[0], loss_target[0], h1, norm_attn_g, norm_mlp_g, norm_final_g.reshape(1, D), b_pad, w_in_t, comm)

    dg_attn, db, dg_mlp, dg_final = dsmall
    small = _small_allreduce_adamw(
        _pack_small(dg_attn, db, dg_mlp, dg_final, loss_row),
        _pack_small(norm_attn_g, b_forget, norm_mlp_g, norm_final_g.reshape(1, D)),
        _pack_small(m_norm_attn_g, m_b_forget, m_norm_mlp_g, m_norm_final_g.reshape(1, D)),
        _pack_small(v_norm_attn_g, v_b_forget, v_norm_mlp_g, v_norm_final_g.reshape(1, D)))
    big = [comm.w_in_update(small[0])] + comm.updates

    outs = [small[0][LOSS_ROW, 0], dx[None]]
    for q in range(4):
        s_attn, s_b, s_mlp, s_final = _unpack_small(small[q])
        b_in, b_a, b_b, b_out, b_up, b_down = [t[q][None] for t in big]
        b_in = jnp.swapaxes(b_in, 1, 2)
        outs += [s_attn, b_in, s_b, b_a, b_b, b_out, s_mlp, b_up, b_down, s_final]
    return tuple(outs)
```

```python
import functools

import jax
import jax.numpy as jnp
from jax import lax
from jax.experimental import pallas as pl
from jax.experimental.pallas import tpu as pltpu

F32 = jnp.float32
BF16 = jnp.bfloat16
MESH = pl.DeviceIdType.MESH

S = 2048
D = 1024
HD = 64
FOX_H = 8
FOX_W = FOX_H * HD
DIL_HG = 4
DIL_G = 3
DIL_W = DIL_G * DIL_HG * HD
DIL_OUT = DIL_HG * HD
DIL_BLK = 128
DIL_R = (1, 4, 16)
DFF = 4 * D
D_IN = 3 * FOX_W + FOX_H + 3 * DIL_W + 2 * D
EPS = 1e-6
NEG_INF = -1e30
SCALE = HD ** -0.5
ROPE_THETA = 500000.0
ROPE_DIM = HD // 4
N_DEV = 8

ADAM_LR = 0.001
ADAM_B1 = 0.9
ADAM_B2 = 0.999
ADAM_EPS = 1e-08
ADAM_WD = 0.01
ADAM_STEP = 10

LANES = 128
CB = 256
SEG_A = 0
SEG_B = 3 * FOX_W
SEG_Z = SEG_B + 3 * DIL_W
SEG_G = SEG_Z + CB
SEG_F = SEG_G + 2 * D
D_PAD = SEG_F + CB
SMALL_R = 32

VMEM_MB = 56


def _cparams(dims=None, vmem_mb=VMEM_MB, **kw):
    return pltpu.CompilerParams(dimension_semantics=dims, vmem_limit_bytes=vmem_mb << 20, **kw)


ANY = pl.BlockSpec(memory_space=pl.ANY)

_NN = (((1,), (0,)), ((), ()))
_NT = (((1,), (1,)), ((), ()))
_TN = (((0,), (0,)), ((), ()))


def _dot(a, b, dims):
    return lax.dot_general(a.astype(BF16), b.astype(BF16), dims, preferred_element_type=F32)


def _mm(a, b, *, mode, tm, tn, out_dtypes, name, n=None, k=None, a_off=0, b_off=0,
        b_sharded=False, out_sharded=False, epilogue=None, extras=(), deps=()):
    n_sh = b.shape[-1] if b_sharded else None
    if mode == "nn":
        m = a.shape[0]
        k = k or a.shape[1]
        a_spec = pl.BlockSpec((tm, k), lambda i, j: (i, a_off))
        if b_sharded:
            assert tn == n_sh
            n = N_DEV * n_sh
            b_spec = pl.BlockSpec((None, k, tn), lambda i, j: (j, 0, 0))
        else:
            n = n or b.shape[1]
            b_spec = pl.BlockSpec((k, tn), lambda i, j: (0, j + b_off))
        dims = _NN
    elif mode == "nt":
        m = a.shape[0]
        k = k or a.shape[1]
        a_spec = pl.BlockSpec((tm, k), lambda i, j: (i, a_off))
        if b_sharded:
            n = b.shape[1]
            b_spec = pl.BlockSpec((N_DEV, tn, n_sh), lambda i, j: (0, j, 0))
        else:
            n = n or b.shape[0]
            b_spec = pl.BlockSpec((tn, k), lambda i, j: (j + b_off, 0))
        dims = _NT
    else:
        k, m = a.shape
        n = n or b.shape[1]
        a_spec = pl.BlockSpec((k, tm), lambda i, j: (0, i))
        b_spec = pl.BlockSpec((k, tn), lambda i, j: (0, j + b_off))
        dims = _TN
    assert m % tm == 0 and n % tn == 0, (name, m, n, tm, tn)
    n_extra = len(extras)
    tile = pl.BlockSpec((tm, tn), lambda i, j: (i, j))
    if out_sharded:
        assert mode == "tn" and n // tn == N_DEV
        out_spec = pl.BlockSpec((None, tm, tn), lambda i, j: (j, i, 0))
        out_shape = (N_DEV, m, tn)
    else:
        out_spec, out_shape = tile, (m, n)

    def body(a_ref, b_ref, *refs):
        if mode == "nt" and b_sharded:
            acc = _dot(a_ref[:, 0:n_sh], b_ref[0], dims)
            for p in range(1, N_DEV):
                acc = acc + _dot(a_ref[:, p * n_sh:(p + 1) * n_sh], b_ref[p], dims)
        else:
            acc = _dot(a_ref[...], b_ref[...], dims)
        ex = [r[...] for r in refs[:n_extra]]
        outs = epilogue(acc, *ex) if epilogue is not None else (acc,)
        for o_ref, o in zip(refs[n_extra + len(deps):], outs):
            o_ref[...] = o.astype(o_ref.dtype)

    res = pl.pallas_call(
        body, name=name, grid=(m // tm, n // tn),
        in_specs=[a_spec, b_spec] + [tile] * n_extra + [ANY] * len(deps),
        out_specs=[out_spec] * len(out_dtypes),
        out_shape=[jax.ShapeDtypeStruct(out_shape, dt) for dt in out_dtypes],
        compiler_params=_cparams(("parallel", "parallel")),
    )(a, b, *extras, *deps)
    return res if len(out_dtypes) > 1 else res[0]


def _mm_rows(a, b, *, tm, name, epilogue, rows=(), vecs=(), row_out=(), vec_out=(), b_sharded=False, deps=()):
    m, k = a.shape
    n_rows, n_vecs, n_deps = len(rows), len(vecs), len(deps)
    n_sh = b.shape[-1] if b_sharded else None

    def body(a_ref, b_ref, *refs):
        row_refs, vec_refs = refs[:n_rows], refs[n_rows:n_rows + n_vecs]
        outs = refs[n_rows + n_vecs + n_deps:]
        if b_sharded:
            acc = _dot(a_ref[:, 0:n_sh], b_ref[0], _NT)
            for p in range(1, N_DEV):
                acc = acc + _dot(a_ref[:, p * n_sh:(p + 1) * n_sh], b_ref[p], _NT)
        else:
            acc = _dot(a_ref[...], b_ref[...], _NN)
        row_vals, vec_incs = epilogue(acc, [r[...] for r in row_refs], [v[...] for v in vec_refs])
        for o_ref, val in zip(outs[:len(row_out)], row_vals):
            o_ref[...] = val.astype(o_ref.dtype)

        @pl.when(pl.program_id(0) == 0)
        def _():
            for o_ref in outs[len(row_out):]:
                o_ref[...] = jnp.zeros_like(o_ref)

        for o_ref, inc in zip(outs[len(row_out):], vec_incs):
            o_ref[...] += inc

    tile = pl.BlockSpec((tm, D), lambda i: (i, 0))
    b_spec = pl.BlockSpec(b.shape, lambda i: (0,) * b.ndim)
    return pl.pallas_call(
        body, name=name, grid=(m // tm,),
        in_specs=[pl.BlockSpec((tm, k), lambda i: (i, 0)), b_spec] + [tile] * n_rows
                 + [pl.BlockSpec(v.shape, lambda i: (0, 0)) for v in vecs] + [ANY] * n_deps,
        out_specs=[tile] * len(row_out) + [pl.BlockSpec((1, w), lambda i: (0, 0)) for w in vec_out],
        out_shape=[jax.ShapeDtypeStruct((m, D), dt) for dt in row_out]
                  + [jax.ShapeDtypeStruct((1, w), F32) for w in vec_out],
        compiler_params=_cparams(("arbitrary",)),
    )(a, b, *rows, *vecs, *deps)


ROW_T = 256
ROWS_TM = 512


def _rms_rows(x, g):
    r = lax.rsqrt(jnp.mean(x * x, axis=-1, keepdims=True) + EPS)
    return (x * r) * g


def _rms_bwd_rows(dh, x, dres, g):
    r = lax.rsqrt(jnp.mean(x * x, axis=-1, keepdims=True) + EPS)
    xn = x * r
    dhn = dh * g
    dx = dres + r * (dhn - xn * jnp.mean(dhn * xn, axis=-1, keepdims=True))
    return dx, jnp.sum(dh * xn, axis=0, keepdims=True)


def _final_loss_rows(x, g, tgt):
    r = lax.rsqrt(jnp.mean(x * x, axis=-1, keepdims=True) + EPS)
    xn = x * r
    err = xn * g - tgt
    row_loss = jnp.mean(err * err, axis=-1, keepdims=True)
    loss = 0.5 * jnp.sum(row_loss, axis=0, keepdims=True) * jnp.ones((1, LANES), F32)
    dy = err * (1.0 / D)
    dyn = dy * g
    dx = r * (dyn - xn * jnp.mean(dyn * xn, axis=-1, keepdims=True))
    return dx, jnp.sum(dy * xn, axis=0, keepdims=True), loss


def _sigmoid(z):
    return 1.0 / (1.0 + jnp.exp(-z))


def _gate_fwd(proj_g, ya, yb):
    def body(ga_ref, gb_ref, ya_ref, yb_ref, o_ref):
        o_ref[...] = (_sigmoid(ga_ref[...]) * ya_ref[...] + _sigmoid(gb_ref[...]) * yb_ref[...]).astype(BF16)

    row = pl.BlockSpec((ROW_T, D), lambda i: (i, 0))
    return pl.pallas_call(
        body, name="gate_fwd", grid=(S // ROW_T,),
        in_specs=[row, pl.BlockSpec((ROW_T, D), lambda i: (i, 1)), row, row],
        out_specs=row, out_shape=jax.ShapeDtypeStruct((S, D), BF16),
        compiler_params=_cparams(("parallel",)),
    )(proj_g, proj_g, ya, yb)


GATE_TR = 1024
GATE_TC = 512
GATE_NB = 2 * D // GATE_TC


def _gate_bwd(dmixed, proj_g, ya, yb):
    half = GATE_NB // 2

    def body(dm_ref, g_ref, ya_ref, yb_ref, dy_ref, dg_ref):
        dm = dm_ref[...]
        s = _sigmoid(g_ref[...])
        y = jnp.where(pl.program_id(1) < half, ya_ref[...], yb_ref[...])
        dy_ref[...] = (dm * s).astype(BF16)
        dg_ref[...] = (dm * y * (s * (1.0 - s))).astype(BF16)

    blk = pl.BlockSpec((GATE_TR, GATE_TC), lambda i, j: (i, j))
    wrapped = pl.BlockSpec((GATE_TR, GATE_TC), lambda i, j: (i, j % half))
    return pl.pallas_call(
        body, name="gate_bwd", grid=(S // GATE_TR, GATE_NB),
        in_specs=[wrapped, blk, wrapped, wrapped],
        out_specs=[blk, pl.BlockSpec((GATE_TR, GATE_TC), lambda i, j: (i, j + SEG_G // GATE_TC))],
        out_shape=[jax.ShapeDtypeStruct((S, 2 * D), BF16), jax.ShapeDtypeStruct((S, D_PAD), BF16)],
        compiler_params=_cparams(("parallel", "parallel")),
    )(dmixed, proj_g, ya, yb)


FOX_TQ = 256


def _scan_rows(x, reverse):
    n = x.shape[0]
    row = lax.broadcasted_iota(jnp.int32, x.shape, 0)
    k = 1
    while k < n:
        if reverse:
            x = x + jnp.where(row < n - k, pltpu.roll(x, n - k, 0), 0.0)
        else:
            x = x + jnp.where(row >= k, pltpu.roll(x, k, 0), 0.0)
        k *= 2
    return x


def _fox_dscan(dft, dfs, proj_f, b_pad, dproj):
    def body(dft_ref, dfs_ref, f_ref, b_ref, _, dfa_ref, db_ref):
        dfs_pad = jnp.concatenate([dfs_ref[...], jnp.zeros((LANES - FOX_H, S), F32)], axis=0)
        df = dfs_pad.T + dft_ref[0]
        for hp in range(1, dft_ref.shape[0]):
            df = df + pltpu.roll(dft_ref[hp], 2 * hp, 1)
        dlf = _scan_rows(df, reverse=True)
        z = f_ref[...] + b_ref[...]
        lane = lax.broadcasted_iota(jnp.int32, (S, LANES), 1)
        dfa = jnp.where(lane < FOX_H, dlf / (1.0 + jnp.exp(z)), 0.0)
        dfa_ref[:, :LANES] = dfa.astype(BF16)
        dfa_ref[:, LANES:] = jnp.zeros((S, CB - LANES), BF16)
        db_ref[...] = jnp.sum(dfa, axis=0, keepdims=True)

    return pl.pallas_call(
        body, name="fox_dscan", grid=(1,),
        in_specs=[pl.BlockSpec(dft.shape, lambda i: (0, 0, 0)), pl.BlockSpec((FOX_H, S), lambda i: (0, 0)),
                  pl.BlockSpec((S, LANES), lambda i: (0, 0)), pl.BlockSpec((1, LANES), lambda i: (0, 0)), ANY],
        out_specs=[pl.BlockSpec((S, CB), lambda i: (0, SEG_F // CB)), pl.BlockSpec((1, LANES), lambda i: (0, 0))],
        out_shape=[jax.ShapeDtypeStruct((S, D_PAD), BF16), jax.ShapeDtypeStruct((1, LANES), F32)],
        input_output_aliases={4: 0},
        compiler_params=_cparams(("arbitrary",)),
    )(dft, dfs, proj_f, b_pad, dproj)


FOX_NQ = S // FOX_TQ
FOX_HP = FOX_W // LANES
HEADS_PER_LB = LANES // HD
FOX_AUG = 2 * LANES


def _fox_prepare(proj_a, proj_f, b_pad):
    tr = 512

    def body(q_ref, k_ref, f_ref, b_ref, qa_ref, ka_ref, hi_s, mid_s, lo_s):
        i = pl.program_id(0)

        @pl.when(i == 0)
        def _():
            z = f_ref[...] + b_ref[...]
            lf = jnp.minimum(z, 0.0) - jnp.log1p(jnp.exp(-jnp.abs(z)))
            f = _scan_rows(lf, reverse=False)
            hi = f.astype(BF16).astype(F32)
            r1 = f - hi
            mid = r1.astype(BF16).astype(F32)
            hi_s[...] = hi
            mid_s[...] = mid
            lo_s[...] = (r1 - mid).astype(BF16).astype(F32)

        rows = pl.ds(pl.multiple_of(i * tr, tr), tr)
        hi, mid, lo = hi_s[rows, :], mid_s[rows, :], lo_s[rows, :]
        lane = lax.broadcasted_iota(jnp.int32, (tr, HD), 1)
        q_ones = jnp.where((lane >= 3) & (lane < 6), 1.0, 0.0)
        k_ones = jnp.where(lane < 3, 1.0, 0.0)
        for h in range(FOX_H):
            a1, a2, a3 = hi[:, h:h + 1], mid[:, h:h + 1], lo[:, h:h + 1]
            q_extra = jnp.where(lane == 0, a1, jnp.where(lane == 1, a2, jnp.where(lane == 2, a3, q_ones)))
            k_extra = jnp.where(lane == 3, -a1, jnp.where(lane == 4, -a2, jnp.where(lane == 5, -a3, k_ones)))
            base = h * LANES
            qa_ref[:, base:base + HD] = (q_ref[:, h * HD:(h + 1) * HD].astype(F32) * SCALE).astype(BF16)
            qa_ref[:, base + HD:base + LANES] = q_extra.astype(BF16)
            ka_ref[:, base:base + HD] = k_ref[:, h * HD:(h + 1) * HD]
            ka_ref[:, base + HD:base + LANES] = k_extra.astype(BF16)

    out_blk = pl.BlockSpec((tr, FOX_H * LANES), lambda i: (i, 0))
    return pl.pallas_call(
        body, name="fox_prepare", grid=(S // tr,),
        in_specs=[pl.BlockSpec((tr, FOX_W), lambda i: (i, 0)), pl.BlockSpec((tr, FOX_W), lambda i: (i, 1)),
                  pl.BlockSpec((S, LANES), lambda i: (0, 0)), pl.BlockSpec((1, LANES), lambda i: (0, 0))],
        out_specs=[out_blk, out_blk],
        out_shape=[jax.ShapeDtypeStruct((S, FOX_H * LANES), BF16)] * 2,
        scratch_shapes=[pltpu.VMEM((S, LANES), F32)] * 3,
        compiler_params=_cparams(("arbitrary",)),
    )(proj_a, proj_a, proj_f, b_pad)


def _fox_scores(qa, ka, w):
    s = _dot(qa, ka, _NT)
    row = lax.broadcasted_iota(jnp.int32, (FOX_TQ, FOX_TQ), 0)
    col = lax.broadcasted_iota(jnp.int32, (FOX_TQ, FOX_TQ), 1)
    diag = jnp.where(col <= row, s[:, w * FOX_TQ:], NEG_INF)
    return diag if w == 0 else jnp.concatenate([s[:, :w * FOX_TQ], diag], axis=1)


def _fox_fwd(q_aug, k_aug, proj_a):
    def body(qa_ref, ka_ref, v_ref, o_ref, lse_ref):
        qi = pl.program_id(1)
        lane = lax.broadcasted_iota(jnp.int32, (FOX_TQ, LANES), 1)
        for w in range(FOX_NQ):
            @pl.when(qi == w)
            def _(w=w):
                width = (w + 1) * FOX_TQ
                lse = jnp.zeros((FOX_TQ, LANES), F32)
                for hh in range(HEADS_PER_LB):
                    aug = slice(hh * LANES, (hh + 1) * LANES)
                    sl = slice(hh * HD, (hh + 1) * HD)
                    s = _fox_scores(qa_ref[:, aug], ka_ref[:width, aug], w)
                    m = jnp.max(s, axis=-1, keepdims=True)
                    p = jnp.exp(s - m)
                    l = jnp.sum(p, axis=-1, keepdims=True)
                    o_ref[:, sl] = _dot(p / l, v_ref[:width, sl], _NN)
                    lse = jnp.where(lane == hh, m + jnp.log(l), lse)
                lse_ref[...] = lse

    return pl.pallas_call(
        body, name="fox_fwd", grid=(FOX_HP, FOX_NQ),
        in_specs=[pl.BlockSpec((FOX_TQ, FOX_AUG), lambda h, i: (i, h)),
                  pl.BlockSpec((S, FOX_AUG), lambda h, i: (0, h)),
                  pl.BlockSpec((S, LANES), lambda h, i: (0, 2 * FOX_HP + h))],
        out_specs=[pl.BlockSpec((FOX_TQ, LANES), lambda h, i: (i, h)),
                   pl.BlockSpec((None, FOX_TQ, LANES), lambda h, i: (h, i, 0))],
        out_shape=[jax.ShapeDtypeStruct((S, FOX_W), F32), jax.ShapeDtypeStruct((FOX_HP, S, LANES), F32)],
        compiler_params=_cparams(("parallel", "parallel")),
    )(q_aug, k_aug, proj_a)


def _fox_bwd(q_aug, k_aug, proj_a, lse, do, dproj, deps=()):
    n_q = FOX_NQ

    def body(qa_ref, ka_ref, v_ref, lse_ref, do_ref, *refs):
        dproj_ref, dft_ref, dfs_ref, dk_acc, dv_acc, dq_buf, kv_buf, dq_sems, kv_sems = refs[1 + len(deps):]
        hp = pl.program_id(0)
        t = pl.program_id(1)
        slot = t % 2
        col = pl.multiple_of(hp * LANES, LANES)

        def dq_copy(step, buf_slot):
            rows = pl.ds(pl.multiple_of(step * FOX_TQ, FOX_TQ), FOX_TQ)
            return pltpu.make_async_copy(dq_buf.at[buf_slot], dproj_ref.at[rows, pl.ds(col, LANES)], dq_sems.at[buf_slot])

        @pl.when(t == 0)
        def _():
            dk_acc[...] = jnp.zeros_like(dk_acc)
            dv_acc[...] = jnp.zeros_like(dv_acc)

        @pl.when((t == 0) & (hp == 0))
        def _():
            dfs_ref[...] = jnp.zeros_like(dfs_ref)

        @pl.when(t >= 2)
        def _():
            dq_copy(t - 2, slot).wait()

        lane = lax.broadcasted_iota(jnp.int32, (FOX_TQ, LANES), 1)
        for w in range(n_q):
            @pl.when(t == w)
            def _(w=w):
                width = (w + 1) * FOX_TQ
                lse_all = lse_ref[...]
                sub = lax.broadcasted_iota(jnp.int32, (FOX_H, width), 0)
                dft = jnp.zeros((FOX_TQ, LANES), F32)
                for hh in range(HEADS_PER_LB):
                    aug = slice(hh * LANES, (hh + 1) * LANES)
                    sl = slice(hh * HD, (hh + 1) * HD)
                    head = hp * HEADS_PER_LB + hh
                    qa = qa_ref[:, aug]
                    ka = ka_ref[:width, aug]
                    do_h = do_ref[:, sl]
                    s = _fox_scores(qa, ka, w)
                    p = jnp.exp(s - lse_all[:, hh:hh + 1])
                    dp = _dot(do_h, v_ref[:width, sl], _NT)
                    ds = p * (dp - jnp.sum(dp * p, axis=-1, keepdims=True))
                    dft = jnp.where(lane == hh, jnp.sum(ds, axis=-1, keepdims=True), dft)
                    dfs_ref[:, :width] -= jnp.where(sub == head, jnp.sum(ds, axis=0, keepdims=True), 0.0)
                    dq_buf[slot, :, sl] = (_dot(ds, ka[:, :HD], _NN) * SCALE).astype(BF16)
                    dk_acc[:width, sl] += _dot(ds, qa[:, :HD], _TN)
                    dv_acc[:width, sl] += _dot(p, do_h, _TN)
                dft_ref[...] = dft

        dq_copy(t, slot).start()

        @pl.when(t == n_q - 1)
        def _():
            dq_copy(t - 1, 1 - slot).wait()
            dq_copy(t, slot).wait()
            kv_buf[0] = dk_acc[...].astype(BF16)
            kv_buf[1] = dv_acc[...].astype(BF16)
            copies = [pltpu.make_async_copy(
                kv_buf.at[j], dproj_ref.at[:, pl.ds(pl.multiple_of((j + 1) * FOX_W + hp * LANES, LANES), LANES)],
                kv_sems.at[j]) for j in range(2)]
            for cp in copies:
                cp.start()
            for cp in copies:
                cp.wait()

    qblk = pl.BlockSpec((FOX_TQ, FOX_AUG), lambda h, t: (t, h))
    lane_blk = pl.BlockSpec((None, FOX_TQ, LANES), lambda h, t: (h, t, 0))
    return pl.pallas_call(
        body, name="fox_bwd", grid=(FOX_HP, n_q),
        in_specs=[qblk, pl.BlockSpec((S, FOX_AUG), lambda h, t: (0, h)),
                  pl.BlockSpec((S, LANES), lambda h, t: (0, 2 * FOX_HP + h)),
                  lane_blk, pl.BlockSpec((FOX_TQ, LANES), lambda h, t: (t, h)), ANY] + [ANY] * len(deps),
        out_specs=[ANY, lane_blk, pl.BlockSpec((FOX_H, S), lambda h, t: (0, 0))],
        out_shape=[jax.ShapeDtypeStruct((S, D_PAD), BF16),
                   jax.ShapeDtypeStruct((FOX_HP, S, LANES), F32), jax.ShapeDtypeStruct((FOX_H, S), F32)],
        scratch_shapes=[pltpu.VMEM((S, LANES), F32), pltpu.VMEM((S, LANES), F32),
                        pltpu.VMEM((2, FOX_TQ, LANES), BF16), pltpu.VMEM((2, S, LANES), BF16),
                        pltpu.SemaphoreType.DMA((2,)), pltpu.SemaphoreType.DMA((2,))],
        input_output_aliases={5: 0},
        compiler_params=_cparams(("arbitrary", "arbitrary")),
    )(q_aug, k_aug, proj_a, lse, do, dproj, *deps)


def _rope_tables():
    half = ROPE_DIM // 2
    shape = (DIL_G, S, half)
    g = lax.broadcasted_iota(jnp.int32, shape, 0)
    row = lax.broadcasted_iota(jnp.int32, shape, 1)
    r = jnp.left_shift(1, 2 * g)
    per_class = S // r
    pos = (row % per_class) * r + row // per_class
    inv_freq = jnp.power(jnp.float32(ROPE_THETA), -lax.broadcasted_iota(F32, shape, 2) * 2.0 / ROPE_DIM)
    ang = pos.astype(F32) * inv_freq
    return jnp.concatenate([jnp.cos(ang), jnp.sin(ang)], axis=-1)


def _rope_factors(cs):
    half = ROPE_DIM // 2
    i = lax.broadcasted_iota(jnp.int32, (2 * half, 2 * LANES), 0)
    l = lax.broadcasted_iota(jnp.int32, (2 * half, 2 * LANES), 1)
    hit = ((l < LANES) & (i == l % half)) | ((l >= LANES) & (i - half == l % half))
    spread = jnp.where(hit, 1.0, 0.0).astype(BF16)
    hi = cs.astype(BF16)
    r1 = cs - hi.astype(F32)
    mid = r1.astype(BF16)
    lo = (r1 - mid.astype(F32)).astype(BF16)
    full = (_dot(hi, spread, _NN) + _dot(mid, spread, _NN)) + _dot(lo, spread, _NN)
    cos, sin = full[:, :LANES], full[:, LANES:]
    lane = lax.broadcasted_iota(jnp.int32, cos.shape, 1) % HD
    t_self = jnp.where(lane < ROPE_DIM, cos, 1.0)
    t_up = jnp.where(lane < half, -sin, 0.0)
    t_dn = jnp.where((lane >= half) & (lane < ROPE_DIM), sin, 0.0)
    return t_self, t_up, t_dn


def _residue_pieces(r):
    if r == 1:
        return [(slice(i, i + 512), slice(i, i + 512)) for i in range(0, S, 512)]
    n = S // r
    return [(pl.ds(j, n, stride=r), slice(j * n, (j + 1) * n)) for j in range(r)]


def _rope_apply(x, a, u, d, backward):
    half = ROPE_DIM // 2
    if backward:
        return x * a + pltpu.roll(x * u, half, 1) + pltpu.roll(x * d, LANES - half, 1)
    return x * a + pltpu.roll(x, LANES - half, 1) * u + pltpu.roll(x, half, 1) * d


def _rope_cache_factors(cs_ref, fac, first_step_of_group):
    @pl.when(first_step_of_group)
    def _():
        for lo in range(0, S, 512):
            for j, f in enumerate(_rope_factors(cs_ref[lo:lo + 512, :])):
                fac[j, lo:lo + 512, :] = f


LB_PER_CB = CB // LANES
ROPE_NB = 3 * DIL_G * LB_PER_CB


def _rope_step(s):
    per_group = 3 * LB_PER_CB
    return s // per_group, (s % per_group) // LB_PER_CB, s % LB_PER_CB


def _rope_split(proj_b, tables):
    def body(x_ref, cs_ref, o_ref, fac):
        g, t, hf = _rope_step(pl.program_id(0))
        _rope_cache_factors(cs_ref, fac, (t == 0) & (hf == 0))
        for gs in range(DIL_G):
            @pl.when(g == gs)
            def _(gs=gs):
                for tok, sub in _residue_pieces(DIL_R[gs]):
                    x = x_ref[tok, :]
                    y = _rope_apply(x, fac[0, sub, :], fac[1, sub, :], fac[2, sub, :], False)
                    o_ref[sub, :] = jnp.where(t < 2, y, x).astype(BF16)

    def in_index(s):
        g, t, hf = _rope_step(s)
        return 0, (t * DIL_G + g) * LB_PER_CB + hf

    def out_index(s):
        g, t, hf = _rope_step(s)
        return t * DIL_G + g, 0, hf

    tab = pl.BlockSpec((None, S, 2 * (ROPE_DIM // 2)), lambda s: (_rope_step(s)[0], 0, 0))
    return pl.pallas_call(
        body, name="rope_split", grid=(ROPE_NB,),
        in_specs=[pl.BlockSpec((S, LANES), in_index), tab],
        out_specs=pl.BlockSpec((None, S, LANES), out_index),
        out_shape=jax.ShapeDtypeStruct((3 * DIL_G, S, CB), BF16),
        scratch_shapes=[pltpu.VMEM((3, S, LANES), F32)],
        compiler_params=_cparams(("arbitrary",)),
    )(proj_b, tables)


def _rope_merge_bwd(dq3, dk3, dv3, tables, dproj):
    def body(dq_ref, dk_ref, dv_ref, cs_ref, _, o_ref, tmp, fac):
        s = pl.program_id(0)
        g, t, hf = _rope_step(s)
        _rope_cache_factors(cs_ref, fac, (t == 0) & (hf == 0) & (s < ROPE_NB))
        for gs in range(DIL_G):
            @pl.when((g == gs) & (s < ROPE_NB))
            def _(gs=gs):
                for tok, sub in _residue_pieces(DIL_R[gs]):
                    x = jnp.where(t == 0, dq_ref[sub, :], jnp.where(t == 1, dk_ref[sub, :], dv_ref[sub, :]))
                    y = _rope_apply(x, fac[0, sub, :], fac[1, sub, :], fac[2, sub, :], True)
                    tmp[tok, :] = jnp.where(t < 2, y, x)
                o_ref[...] = tmp[...].astype(BF16)

        @pl.when(s >= ROPE_NB)
        def _():
            o_ref[...] = jnp.zeros_like(o_ref)

    def src_spec(own):
        def index(s):
            g, t, hf = _rope_step(jnp.minimum(s, ROPE_NB - 1))
            return g, 0, jnp.where(t == own, hf, jnp.where(t > own, LB_PER_CB - 1, 0))
        return pl.BlockSpec((None, S, LANES), index)

    def out_index(s):
        g, t, hf = _rope_step(s)
        col = SEG_B // LANES + (t * DIL_G + g) * LB_PER_CB + hf
        return 0, jnp.where(s < ROPE_NB, col, SEG_Z // LANES + s - ROPE_NB)

    tab = pl.BlockSpec((None, S, 2 * (ROPE_DIM // 2)), lambda s: (_rope_step(jnp.minimum(s, ROPE_NB - 1))[0], 0, 0))
    return pl.pallas_call(
        body, name="rope_merge_bwd", grid=(ROPE_NB + LB_PER_CB,),
        in_specs=[src_spec(0), src_spec(1), src_spec(2), tab, ANY],
        out_specs=pl.BlockSpec((S, LANES), out_index),
        out_shape=jax.ShapeDtypeStruct((S, D_PAD), BF16),
        scratch_shapes=[pltpu.VMEM((S, LANES), F32), pltpu.VMEM((3, S, LANES), F32)],
        input_output_aliases={4: 0},
        compiler_params=_cparams(("arbitrary",)),
    )(dq3, dk3, dv3, tables, dproj)


DIL_NB = S // DIL_BLK


_BQK = (((2,), (2,)), ((0,), (0,)))
_BQD = (((2,), (1,)), ((0,), (0,)))
_BKD = (((1,), (1,)), ((0,), (0,)))


def _dil_mask(g):
    shape = (DIL_NB, DIL_BLK, 2 * DIL_BLK)
    blocks_per_seq = jnp.right_shift(DIL_NB, 2 * g)
    has_prev = jnp.bitwise_and(lax.broadcasted_iota(jnp.int32, shape, 0), blocks_per_seq - 1) != 0
    a = lax.broadcasted_iota(jnp.int32, shape, 1)
    kk = lax.broadcasted_iota(jnp.int32, shape, 2)
    diff = DIL_BLK + a - kk
    return (diff >= 0) & (diff <= DIL_BLK) & ((kk >= DIL_BLK) | has_prev)


def _blocks(t):
    return t.reshape(DIL_NB, DIL_BLK, t.shape[-1])


def _with_prev(t):
    prev = jnp.concatenate([jnp.zeros((DIL_BLK, t.shape[-1]), t.dtype), t[:-DIL_BLK]], axis=0)
    return jnp.concatenate([_blocks(prev), _blocks(t)], axis=1)


def _fold_prev(t2):
    n = t2.shape[-1]
    to_prev = t2[:, :DIL_BLK].reshape(S, n)
    own = t2[:, DIL_BLK:].reshape(S, n)
    return own + jnp.concatenate([to_prev[DIL_BLK:], jnp.zeros((DIL_BLK, n), t2.dtype)], axis=0)


def _dil_group_spec(t, width=DIL_OUT):
    return pl.BlockSpec((None, S, width), lambda g: (t * DIL_G + g, 0, 0))


def _dil_fwd(qkv3):
    def body(q_ref, k_ref, v_ref, o_ref, lse_ref):
        ok = _dil_mask(pl.program_id(0))
        lane = lax.broadcasted_iota(jnp.int32, (S, LANES), 1)
        lse = jnp.zeros((S, LANES), F32)
        for h in range(DIL_HG):
            sl = slice(h * HD, (h + 1) * HD)
            s = jnp.where(ok, _dot(_blocks(q_ref[:, sl]), _with_prev(k_ref[:, sl]), _BQK) * SCALE, NEG_INF)
            m = jnp.max(s, axis=-1, keepdims=True)
            p = jnp.exp(s - m)
            l = jnp.sum(p, axis=-1, keepdims=True)
            o_ref[:, sl] = _dot(p / l, _with_prev(v_ref[:, sl]), _BQD).reshape(S, HD)
            lse = jnp.where(lane == h, (m + jnp.log(l)).reshape(S, 1), lse)
        lse_ref[...] = lse

    return pl.pallas_call(
        body, name="dil_fwd", grid=(DIL_G,),
        in_specs=[_dil_group_spec(0), _dil_group_spec(1), _dil_group_spec(2)],
        out_specs=[_dil_group_spec(0), _dil_group_spec(0, LANES)],
        out_shape=[jax.ShapeDtypeStruct((DIL_G, S, DIL_OUT), F32), jax.ShapeDtypeStruct((DIL_G, S, LANES), F32)],
        compiler_params=_cparams(("parallel",)),
    )(qkv3, qkv3, qkv3)


def _dil_bwd(qkv3, lse3, do3, c3):
    def body(q_ref, k_ref, v_ref, lse_ref, do_ref, c_ref, dq_ref, dk_ref, dv_ref):
        ok = _dil_mask(pl.program_id(0))
        lse = lse_ref[...]
        c = c_ref[...]
        for h in range(DIL_HG):
            sl = slice(h * HD, (h + 1) * HD)
            q = _blocks(q_ref[:, sl])
            k2 = _with_prev(k_ref[:, sl])
            do_h = _blocks(do_ref[:, sl])
            s = jnp.where(ok, _dot(q, k2, _BQK) * SCALE, NEG_INF)
            p = jnp.exp(s - _blocks(lse[:, h:h + 1]))
            dp = _dot(do_h, _with_prev(v_ref[:, sl]), _BQK)
            ds = p * (dp - _blocks(c[:, h:h + 1]))
            dsb = (ds * SCALE).astype(BF16)
            dq_ref[:, sl] = _dot(dsb, k2, _BQD).reshape(S, HD)
            dk_ref[:, sl] = _fold_prev(_dot(dsb, q, _BKD))
            dv_ref[:, sl] = _fold_prev(_dot(p, do_h, _BKD))

    return pl.pallas_call(
        body, name="dil_bwd", grid=(DIL_G,),
        in_specs=[_dil_group_spec(0), _dil_group_spec(1), _dil_group_spec(2), _dil_group_spec(0, LANES),
                  _dil_group_spec(0), _dil_group_spec(0, LANES)],
        out_specs=[_dil_group_spec(0)] * 3,
        out_shape=[jax.ShapeDtypeStruct((DIL_G, S, DIL_OUT), F32)] * 3,
        compiler_params=_cparams(("parallel",)),
    )(qkv3, qkv3, qkv3, lse3, do3, c3)


COMB_T = 256


def _dil_combine(o3, lse3):
    heads_per_lb = LANES // HD

    def body(o_ref, lse_ref, ob_ref, al_ref, *scratch):
        o_tok = [scratch[LB_PER_CB * gg:LB_PER_CB * (gg + 1)] for gg in range(DIL_G)]
        lse_tok = scratch[LB_PER_CB * DIL_G:]
        g = pl.program_id(0)
        for gs in range(DIL_G):
            @pl.when(g == gs)
            def _(gs=gs):
                for tok, sub in _residue_pieces(DIL_R[gs]):
                    for hf in range(LB_PER_CB):
                        o_tok[gs][hf][tok, :] = o_ref[sub, hf * LANES:(hf + 1) * LANES]
                    lse_tok[gs][tok, :] = lse_ref[sub, :]

        @pl.when(g == DIL_G - 1)
        def _():
            def chunk(i, carry):
                rows = pl.ds(pl.multiple_of(i * COMB_T, COMB_T), COMB_T)
                lse = [lse_tok[gg][rows, :] for gg in range(DIL_G)]
                m = jnp.maximum(jnp.maximum(lse[0], lse[1]), lse[2])
                e = [jnp.exp(lse[gg] - m) for gg in range(DIL_G)]
                den = (e[0] + e[1]) + e[2]
                al = [e[gg] / den for gg in range(DIL_G)]
                for gg in range(DIL_G):
                    al_ref[gg, rows, :] = al[gg]
                for h in range(DIL_HG):
                    hf, sl = h // heads_per_lb, slice((h % heads_per_lb) * HD, (h % heads_per_lb + 1) * HD)
                    acc = al[0][:, h:h + 1] * o_tok[0][hf][rows, sl]
                    for gg in range(1, DIL_G):
                        acc = acc + al[gg][:, h:h + 1] * o_tok[gg][hf][rows, sl]
                    ob_ref[rows, h * HD:(h + 1) * HD] = acc
                return carry

            lax.fori_loop(0, S // COMB_T, chunk, 0)

    return pl.pallas_call(
        body, name="dil_combine", grid=(DIL_G,),
        in_specs=[pl.BlockSpec((None, S, DIL_OUT), lambda g: (g, 0, 0)),
                  pl.BlockSpec((None, S, LANES), lambda g: (g, 0, 0))],
        out_specs=[pl.BlockSpec((S, DIL_OUT), lambda g: (0, 0)),
                   pl.BlockSpec((DIL_G, S, LANES), lambda g: (0, 0, 0))],
        out_shape=[jax.ShapeDtypeStruct((S, DIL_OUT), F32), jax.ShapeDtypeStruct((DIL_G, S, LANES), F32)],
        scratch_shapes=[pltpu.VMEM((S, LANES), F32)] * (DIL_G * (LB_PER_CB + 1)),
        compiler_params=_cparams(("arbitrary",)),
    )(o3, lse3)


def _dil_combine_bwd(dob, ob, alpha, deps=()):
    heads_per_lb = LANES // HD

    def body(dob_ref, ob_ref, al_ref, *refs):
        do_ref, c_ref = refs[len(deps):]
        g = pl.program_id(0)
        hf = pl.program_id(1)
        for gs in range(DIL_G):
            @pl.when(g == gs)
            def _(gs=gs):
                for tok, sub in _residue_pieces(DIL_R[gs]):
                    dob = dob_ref[tok, :]
                    prod = dob * ob_ref[tok, :]
                    al = al_ref[tok, :]
                    lane = lax.broadcasted_iota(jnp.int32, al.shape, 1)
                    c = jnp.where(hf == 0, 0.0, c_ref[sub, :])
                    for hh in range(heads_per_lb):
                        sl = slice(hh * HD, (hh + 1) * HD)
                        head = hf * heads_per_lb + hh
                        a = jnp.sum(jnp.where(lane == head, al, 0.0), axis=-1, keepdims=True)
                        do_ref[sub, sl] = (a * dob[:, sl]).astype(BF16)
                        c = jnp.where(lane == head, a * jnp.sum(prod[:, sl], axis=-1, keepdims=True), c)
                    c_ref[sub, :] = c

    half = pl.BlockSpec((S, LANES), lambda g, hf: (0, hf))
    return pl.pallas_call(
        body, name="dil_combine_bwd", grid=(DIL_G, LB_PER_CB),
        in_specs=[half, half, pl.BlockSpec((None, S, LANES), lambda g, hf: (g, 0, 0))] + [ANY] * len(deps),
        out_specs=[pl.BlockSpec((None, S, LANES), lambda g, hf: (g, 0, hf)),
                   pl.BlockSpec((None, S, LANES), lambda g, hf: (g, 0, 0))],
        out_shape=[jax.ShapeDtypeStruct((DIL_G, S, DIL_OUT), BF16), jax.ShapeDtypeStruct((DIL_G, S, LANES), F32)],
        compiler_params=_cparams(("parallel", "arbitrary")),
    )(dob, ob, alpha, *deps)


def _local_step(x, tgt, h1, g_attn, g_mlp, g_final, b_pad, w_in_t, hooks):
    tables = _rope_tables()

    first = hooks.first_deps()
    proj_a = _mm(h1, w_in_t, mode="nt", tm=S, tn=512, n=3 * FOX_W, out_dtypes=[BF16], name="proj_a", deps=first)
    proj_b = _mm(h1, w_in_t, mode="nt", tm=S, tn=DIL_W, n=3 * DIL_W, b_off=SEG_B // DIL_W, out_dtypes=[F32], name="proj_b",
                deps=first)
    proj_g = _mm(h1, w_in_t, mode="nt", tm=S, tn=D, n=2 * D, b_off=SEG_G // D, out_dtypes=[F32], name="proj_g",
                deps=first)
    proj_f = _mm(h1, w_in_t, mode="nt", tm=S, tn=LANES, n=LANES, b_off=SEG_F // LANES, out_dtypes=[F32], name="proj_f",
                deps=first)

    q_aug, k_aug = _fox_prepare(proj_a, proj_f, b_pad)
    oa, lse_a = _fox_fwd(q_aug, k_aug, proj_a)

    qkv3 = _rope_split(proj_b, tables)
    o3, lse3 = _dil_fwd(qkv3)
    ob, alpha = _dil_combine(o3, lse3)

    w_a, w_b, w_out = hooks.mixer_weights(ob)
    ya = _mm(oa, w_a, mode="nn", tm=S, tn=512, out_dtypes=[F32], name="branch_a")
    yb = _mm(ob, w_b, mode="nn", tm=S, tn=512, out_dtypes=[F32], name="branch_b")
    mixed = _gate_fwd(proj_g, ya, yb)

    def out_epilogue(acc, rows, vecs):
        x2 = rows[0] + acc
        return (x2, _rms_rows(x2, vecs[0])), ()

    x2, h2 = _mm_rows(mixed, w_out, tm=ROWS_TM, name="out_proj", epilogue=out_epilogue, rows=(x,), vecs=(g_mlp,),
                      row_out=(F32, BF16))
    w_up_sh, w_down = hooks.mlp_weights(h2)

    def up_epilogue(acc):
        r = jnp.maximum(acc, 0.0)
        return acc, r * r

    u, act = _mm(h2, w_up_sh, mode="nn", tm=S, tn=DFF // N_DEV, b_sharded=True, out_dtypes=[F32, BF16],
                 name="mlp_up", epilogue=up_epilogue)

    def loss_epilogue(acc, rows, vecs):
        dx3, dg, loss = _final_loss_rows(rows[0] + acc, vecs[0], rows[1])
        return (dx3,), (dg, loss)

    dx3, dg_final, loss = _mm_rows(act, w_down, tm=ROWS_TM, name="mlp_down_loss", epilogue=loss_epilogue,
                                   rows=(x2, tgt), vecs=(g_final,), row_out=(F32,), vec_out=(D, LANES))

    def rms_bwd_epilogue(acc, rows, vecs):
        dx, dg = _rms_bwd_rows(acc, rows[0], rows[1], vecs[0])
        return (dx,), (dg,)

    du = _mm(dx3, w_down, mode="nt", tm=S, tn=512, out_dtypes=[BF16], name="mlp_down_bwd",
             epilogue=lambda acc, u_t: (acc * (2.0 * jnp.maximum(u_t, 0.0)),), extras=(u,))
    dw_down = _mm(act, dx3, mode="tn", tm=512, tn=D, out_dtypes=[F32], name="dw_down")
    dw_up_sh = _mm(h2, du, mode="tn", tm=D, tn=DFF // N_DEV, out_sharded=True, out_dtypes=[F32], name="dw_up")
    dx2, dg_mlp = _mm_rows(du, w_up_sh, b_sharded=True, tm=ROWS_TM, name="mlp_up_bwd", epilogue=rms_bwd_epilogue,
                           rows=(x2, dx3), vecs=(g_mlp,), row_out=(F32,), vec_out=(D,),
                           deps=hooks.mlp_grads(dw_up_sh, dw_down))

    dmixed = _mm(dx2, w_out, mode="nt", tm=S, tn=512, out_dtypes=[F32], name="out_proj_bwd")
    dw_out = _mm(mixed, dx2, mode="tn", tm=D, tn=D, out_dtypes=[F32], name="dw_out")

    dyab, dproj = _gate_bwd(dmixed, proj_g, ya, yb)
    doa = _mm(dyab, w_a, mode="nt", tm=S, tn=FOX_W, k=D, a_off=0, out_dtypes=[BF16], name="branch_a_bwd")
    dw_a = _mm(oa, dyab, mode="tn", tm=FOX_W, tn=D, n=D, b_off=0, out_dtypes=[F32], name="dw_branch_a")
    dob = _mm(dyab, w_b, mode="nt", tm=S, tn=CB, k=D, a_off=1, out_dtypes=[F32], name="branch_b_bwd")
    dw_b = _mm(ob, dyab, mode="tn", tm=DIL_OUT, tn=D, n=D, b_off=1, out_dtypes=[F32], name="dw_branch_b")

    dproj, dft, dfs = _fox_bwd(q_aug, k_aug, proj_a, lse_a, doa, dproj, deps=hooks.mixer_grads(dw_a, dw_b, dw_out))
    dproj, db = _fox_dscan(dft, dfs, proj_f, b_pad, dproj)

    do3, c3 = _dil_combine_bwd(dob, ob, alpha, deps=hooks.mid_backward(db))
    dq3, dk3, dv3 = _dil_bwd(qkv3, lse3, do3, c3)
    dproj = _rope_merge_bwd(dq3, dk3, dv3, tables, dproj)

    dw_in_t = _mm(dproj, h1, mode="tn", tm=CB, tn=D, out_dtypes=[F32], name="dw_in")
    dx, dg_attn = _mm_rows(dproj, w_in_t, tm=ROWS_TM // 2, name="proj_bwd", epilogue=rms_bwd_epilogue,
                           rows=(x, dx2), vecs=(g_attn,), row_out=(F32,), vec_out=(D,),
                           deps=hooks.w_in_grad(dw_in_t))

    return loss, dx, (dg_attn, db, dg_mlp, dg_final)


SHARD_SHAPES = ((D_IN // N_DEV, D), (FOX_W, D // N_DEV), (DIL_OUT, D // N_DEV), (D // N_DEV, D),
                (D, DFF // N_DEV), (DFF // N_DEV, D))
W_NAMES = ("w_in", "w_a", "w_b", "w_out", "w_up", "w_down")
AG_COPIES = 7
HBM = pl.BlockSpec(memory_space=pltpu.HBM)
SEM = pl.BlockSpec(memory_space=pltpu.SEMAPHORE)


def _place():
    return lax.axis_index("x"), lax.axis_index("y"), lax.axis_index("c")


W_SLAB = 752
W_WIN = 272
PAD_BLK = 256


def _pad_runs():
    runs = sorted((pad, pad + b - a, p, a) for p in range(N_DEV) for a, b, pad in _shard_pieces(p))
    blocks = []
    for k in range(D_PAD // PAD_BLK):
        lo, hi = k * PAD_BLK, (k + 1) * PAD_BLK
        blocks.append([(max(lo, r0) - lo, min(hi, r1) - lo, p, a + max(lo, r0) - r0)
                       for r0, r1, p, a in runs if max(lo, r0) < min(hi, r1)])
    return blocks


def _gather_w_in(shard, x_in, g_attn):
    blocks = _pad_runs()

    norm_t = 512
    n_chunks = S // norm_t

    def body(x_ref, xin_ref, g_ref, out_ref, h1_ref, stage, land, xbuf, send_sems, recv_sems, load_sem, x_sems):
        x, y, c = _place()
        me, sibling = (x, y, c), (x, y, 1 - c)
        chips = [(1 - x, y), (x, 1 - y), (1 - x, 1 - y)]

        def slot(px, py, pc):
            return land.at[4 * px + 2 * py + pc]

        def copy(k, block, to):
            return pltpu.make_async_remote_copy(
                src_ref=slot(*block), dst_ref=slot(*block), send_sem=send_sems.at[k], recv_sem=recv_sems.at[k],
                device_id=to, device_id_type=MESH)

        load = pltpu.make_async_copy(x_ref, stage, load_sem)
        load.start()
        load.wait()
        land[4 * x + 2 * y + c] = stage[...].astype(BF16)
        sent = [copy(0, me, sibling)] + [copy(1 + j, me, (*chip, c)) for j, chip in enumerate(chips)]
        for cp in sent:
            cp.start()

        def x_load(i):
            return pltpu.make_async_copy(xin_ref.at[pl.ds(i * norm_t, norm_t)], xbuf.at[i % 2], x_sems.at[i % 2])

        x_load(0).start()
        for i in range(n_chunks):
            if i + 1 < n_chunks:
                x_load(i + 1).start()
            x_load(i).wait()
            h1_ref[i * norm_t:(i + 1) * norm_t, :] = _rms_rows(xbuf[i % 2], g_ref[...]).astype(BF16)

        for j, chip in enumerate(chips):
            copy(1 + j, (*chip, c), me).wait_recv()
            sent.append(copy(4 + j, (*chip, c), sibling))
            sent[-1].start()
        copy(0, sibling, me).wait_recv()
        for j, chip in enumerate(chips):
            copy(4 + j, (*chip, 1 - c), me).wait_recv()
        for cp in sent:
            cp.wait_send()

        row = lax.broadcasted_iota(jnp.int32, (PAD_BLK, D), 0)
        for k, runs in enumerate(blocks):
            out = jnp.zeros((PAD_BLK, D), F32)
            for o_lo, o_hi, p, a in runs:
                start = min(a // 16 * 16, W_SLAB - W_WIN)
                win = land[p, start:start + W_WIN, :].astype(F32)
                moved = pltpu.roll(win, (o_lo - (a - start)) % W_WIN, 0)[:PAD_BLK]
                out = jnp.where((row >= o_lo) & (row < o_hi), moved, out)
            out_ref[k * PAD_BLK:(k + 1) * PAD_BLK, :] = out.astype(BF16)

    return pl.pallas_call(
        body, name="all_gather_w_in",
        out_shape=[jax.ShapeDtypeStruct((D_PAD, D), BF16), jax.ShapeDtypeStruct((S, D), BF16)],
        in_specs=[ANY, ANY, pl.BlockSpec(memory_space=pltpu.VMEM)],
        out_specs=[pl.BlockSpec(memory_space=pltpu.VMEM)] * 2,
        scratch_shapes=[pltpu.VMEM((W_SLAB, D), F32), pltpu.VMEM((N_DEV, W_SLAB, D), BF16),
                        pltpu.VMEM((2, norm_t, D), F32),
                        pltpu.SemaphoreType.DMA((AG_COPIES,)), pltpu.SemaphoreType.DMA((AG_COPIES,)),
                        pltpu.SemaphoreType.DMA(()), pltpu.SemaphoreType.DMA((2,))],
        compiler_params=_cparams(None),
    )(shard, x_in, g_attn)


def _place_own(shards):
    n_w = len(shards)
    shapes = [t.shape for t in shards]

    def body(*refs):
        in_refs, out_refs = refs[:n_w], refs[n_w:2 * n_w]
        stage, cast = refs[2 * n_w:3 * n_w], refs[3 * n_w:4 * n_w]
        load_sems, store_sems = refs[4 * n_w:]
        x, y, c = _place()
        loads = [pltpu.make_async_copy(in_refs[i], stage[i], load_sems.at[i]) for i in range(n_w)]
        for ld in loads:
            ld.start()
        stores = []
        for i in range(n_w):
            loads[i].wait()
            cast[i][...] = stage[i][...].astype(BF16)
            stores.append(pltpu.make_async_copy(cast[i], out_refs[i].at[4 * x + 2 * y + c], store_sems.at[i]))
            stores[-1].start()
        for st in stores:
            st.wait()

    return pl.pallas_call(
        body, name="place_own_shards",
        out_shape=[jax.ShapeDtypeStruct((N_DEV,) + sh, BF16) for sh in shapes],
        in_specs=[ANY] * n_w, out_specs=[ANY] * n_w,
        scratch_shapes=[pltpu.VMEM(sh, F32) for sh in shapes] + [pltpu.VMEM(sh, BF16) for sh in shapes]
                       + [pltpu.SemaphoreType.DMA((n_w,)), pltpu.SemaphoreType.DMA((n_w,))],
        compiler_params=_cparams(None),
    )(*shards)


PEER_COPIES = N_DEV - 1
SIBLING_COPIES = 4
CHIP_COPIES = 3


def _peer_copies(_, land_refs, send_sems, recv_sems):
    x, y, c = _place()
    mine = 4 * x + 2 * y + c
    copies = []
    for i, ref in enumerate(land_refs):
        for k in range(1, N_DEV):
            peer = (x ^ (k >> 2), y ^ ((k >> 1) & 1), c ^ (k & 1))
            n = i * PEER_COPIES + k - 1
            copies.append(pltpu.make_async_remote_copy(
                src_ref=ref.at[mine], dst_ref=ref.at[mine], send_sem=send_sems.at[n], recv_sem=recv_sems.at[n],
                device_id=peer, device_id_type=MESH))
    return copies


def _sibling_copies(g_refs, r_refs, send_sems, recv_sems):
    x, y, c = _place()
    return [pltpu.make_async_remote_copy(
        src_ref=g_refs[i].at[2 * k + (1 - c)], dst_ref=r_refs[i].at[k],
        send_sem=send_sems.at[SIBLING_COPIES * i + k], recv_sem=recv_sems.at[SIBLING_COPIES * i + k],
        device_id=(x, y, 1 - c), device_id_type=MESH) for i in range(len(g_refs)) for k in range(SIBLING_COPIES)]


def _chip_copies(p_refs, r_refs, send_sems, recv_sems):
    x, y, c = _place()
    chips = [(1 - x, y), (x, 1 - y), (1 - x, 1 - y)]
    return [pltpu.make_async_remote_copy(
        src_ref=p_refs[i].at[2 * cx + cy], dst_ref=r_refs[i].at[j],
        send_sem=send_sems.at[CHIP_COPIES * i + j], recv_sem=recv_sems.at[CHIP_COPIES * i + j],
        device_id=(cx, cy, c), device_id_type=MESH) for i in range(len(p_refs)) for j, (cx, cy) in enumerate(chips)]


def _in_hbm(a):
    return pltpu.with_memory_space_constraint(a, pltpu.HBM)


def _exchange_start(name, copies_fn, n_copies, srcs, lands, after=()):
    n_s, n_l, n_a = len(srcs), len(lands), len(after)

    def body(*refs):
        outs = refs[n_s + n_l + n_a:]
        for cp in copies_fn(refs[:n_s], refs[n_s:n_s + n_l], outs[0], outs[1]):
            cp.start()
        outs[-1][...] = jnp.zeros_like(outs[-1])

    res = pl.pallas_call(
        body, name=name,
        out_shape=[pltpu.SemaphoreType.DMA((n_copies,)), pltpu.SemaphoreType.DMA((n_copies,))]
                  + [pltpu.HBM(t.shape, t.dtype) for t in (*srcs, *lands)] + [jax.ShapeDtypeStruct((8, LANES), F32)],
        in_specs=[HBM] * (n_s + n_l) + [ANY] * n_a,
        out_specs=[SEM, SEM] + [HBM] * (n_s + n_l) + [pl.BlockSpec(memory_space=pltpu.VMEM)],
        input_output_aliases={i: 2 + i for i in range(n_s + n_l)},
        compiler_params=pltpu.CompilerParams(has_side_effects=pltpu.SideEffectType.DATAFLOW_SIDE_EFFECTING),
    )(*[_in_hbm(t) for t in (*srcs, *lands)], *after)
    return res[0], res[1], list(res[2:2 + n_s]), list(res[2 + n_s:2 + n_s + n_l]), res[-1]


def _exchange_wait(name, copies_fn, started, after):
    send_sems, recv_sems, srcs, lands, _ = started
    n_s, n_l = len(srcs), len(lands)

    def body(*refs):
        for cp in copies_fn(refs[:n_s], refs[n_s:n_s + n_l], refs[n_s + n_l], refs[n_s + n_l + 1]):
            cp.wait_send()
            cp.wait_recv()

    res = pl.pallas_call(
        body, name=name,
        out_shape=[pltpu.HBM(t.shape, t.dtype) for t in (*srcs, *lands)],
        in_specs=[HBM] * (n_s + n_l) + [SEM, SEM, ANY],
        out_specs=[HBM] * (n_s + n_l),
        input_output_aliases={i: i for i in range(n_s + n_l)},
        compiler_params=pltpu.CompilerParams(has_side_effects=pltpu.SideEffectType.DATAFLOW_SIDE_EFFECTING),
    )(*srcs, *lands, send_sems, recv_sems, after)
    return list(res[:n_s]), list(res[n_s:])


def _shard_block(shape):
    rows, cols = shape
    if rows % 256:
        return rows, 256
    return min(rows, 512 if cols <= LANES else 256), cols


def _chip_partial(ids, g_stack, recv1, name):
    shape = g_stack.shape[1:]
    br, bc = _shard_block(shape)

    def body(ids_ref, g_ref, r_ref, pb_ref, own_ref):
        s = g_ref[...] + r_ref[...]
        pb_ref[...] = s.astype(BF16)

        @pl.when(pl.program_id(2) == ids_ref[1])
        def _():
            own_ref[...] = s

    grid_spec = pltpu.PrefetchScalarGridSpec(
        num_scalar_prefetch=1, grid=(shape[0] // br, shape[1] // bc, 4),
        in_specs=[pl.BlockSpec((None, br, bc), lambda r, q, k, ids: (2 * k + ids[0], r, q)),
                  pl.BlockSpec((None, br, bc), lambda r, q, k, ids: (k, r, q))],
        out_specs=[pl.BlockSpec((None, br, bc), lambda r, q, k, ids: (k, r, q)),
                   pl.BlockSpec((br, bc), lambda r, q, k, ids: (r, q))])
    return pl.pallas_call(
        body, name=name, grid_spec=grid_spec,
        out_shape=[jax.ShapeDtypeStruct((4,) + shape, BF16), jax.ShapeDtypeStruct(shape, F32)],
        compiler_params=_cparams(("parallel", "parallel", "arbitrary")),
    )(ids, g_stack, recv1)


def _adamw(w, g, m, v):
    m = ADAM_B1 * m + (1.0 - ADAM_B1) * g
    v = ADAM_B2 * v + (1.0 - ADAM_B2) * (g * g)
    m_hat = m / (1.0 - ADAM_B1 ** ADAM_STEP)
    v_hat = v / (1.0 - ADAM_B2 ** ADAM_STEP)
    delta = -ADAM_LR * (m_hat / (jnp.sqrt(v_hat) + ADAM_EPS) + ADAM_WD * w)
    return delta, m, v


def _reduce_adamw(own, recv2, w, m, v, name, deps=()):
    shape = own.shape
    br, bc = _shard_block(shape)

    def body(own_ref, r_ref, w_ref, m_ref, v_ref, *refs):
        g_ref, d_ref, nm_ref, nv_ref = refs[len(deps):]
        g = own_ref[...]
        for j in range(3):
            g = g + r_ref[j].astype(F32)
        delta, nm, nv = _adamw(w_ref[...], g, m_ref[...], v_ref[...])
        g_ref[...] = g
        d_ref[...] = delta
        nm_ref[...] = nm
        nv_ref[...] = nv

    blk = pl.BlockSpec((br, bc), lambda r, q: (r, q))
    return pl.pallas_call(
        body, name=name, grid=(shape[0] // br, shape[1] // bc),
        in_specs=[blk, pl.BlockSpec((3, br, bc), lambda r, q: (0, r, q)), blk, blk, blk] + [ANY] * len(deps),
        out_specs=[blk] * 4, out_shape=[jax.ShapeDtypeStruct(shape, F32)] * 4,
        compiler_params=_cparams(("parallel", "parallel")),
    )(own, recv2, w, m, v, *deps)


def _small_allreduce_adamw(gvec, w, m, v):
    def body(g_ref, w_ref, m_ref, v_ref, go_ref, d_ref, nm_ref, nv_ref, buf, send_sems, recv_sems):
        x, y, c = _place()
        my_slot = 4 * x + 2 * y + c
        buf[my_slot] = g_ref[...]
        copies = []
        for k in range(1, N_DEV):
            px, py, pc = x ^ (k >> 2), y ^ ((k >> 1) & 1), c ^ (k & 1)
            copies.append(pltpu.make_async_remote_copy(
                src_ref=g_ref, dst_ref=buf.at[my_slot], send_sem=send_sems.at[k - 1], recv_sem=recv_sems.at[k - 1],
                device_id=(px, py, pc), device_id_type=MESH))
        for cp in copies:
            cp.start()
        for k in range(1, N_DEV):
            px, py, pc = x ^ (k >> 2), y ^ ((k >> 1) & 1), c ^ (k & 1)
            pltpu.make_async_remote_copy(
                src_ref=g_ref, dst_ref=buf.at[4 * px + 2 * py + pc], send_sem=send_sems.at[k - 1],
                recv_sem=recv_sems.at[k - 1], device_id=(px, py, pc), device_id_type=MESH).wait_recv()
        for cp in copies:
            cp.wait_send()
        g = buf[0]
        for s in range(1, N_DEV):
            g = g + buf[s]
        delta, nm, nv = _adamw(w_ref[...], g, m_ref[...], v_ref[...])
        go_ref[...] = g
        d_ref[...] = delta
        nm_ref[...] = nm
        nv_ref[...] = nv

    vm = pl.BlockSpec(memory_space=pltpu.VMEM)
    return pl.pallas_call(
        body, name="small_allreduce_adamw",
        in_specs=[vm] * 4, out_specs=[vm] * 4,
        out_shape=[jax.ShapeDtypeStruct((SMALL_R, LANES), F32)] * 4,
        scratch_shapes=[pltpu.VMEM((N_DEV, SMALL_R, LANES), F32),
                        pltpu.SemaphoreType.DMA((N_DEV - 1,)), pltpu.SemaphoreType.DMA((N_DEV - 1,))],
    )(gvec, w, m, v)


def _cols_to_whole(stack):
    rows = stack.shape[1]
    return stack.transpose(1, 0, 2).reshape(rows, -1)


def _whole_to_cols(t):
    rows = t.shape[0]
    return t.reshape(rows, N_DEV, -1).transpose(1, 0, 2)


W_IN_SEGMENTS = ((0, 3 * FOX_W, SEG_A), (3 * FOX_W, 3 * FOX_W + FOX_H, SEG_F),
                 (3 * FOX_W + FOX_H, 3 * FOX_W + FOX_H + 3 * DIL_W, SEG_B), (3 * FOX_W + FOX_H + 3 * DIL_W, D_IN, SEG_G))


def _shard_pieces(p):
    rows = D_IN // N_DEV
    lo, hi = p * rows, (p + 1) * rows
    return [(max(lo, a) - lo, min(hi, b) - lo, pad + max(lo, a) - a)
            for a, b, pad in W_IN_SEGMENTS if max(lo, a) < min(hi, b)]


def _unpad_dw_in_t(dwp):
    return jnp.stack([jnp.concatenate([dwp[pad:pad + b - a] for a, b, pad in _shard_pieces(p)], axis=0)
                      for p in range(N_DEV)])


LOSS_ROW = 25


def _pack_small(g_attn, b, g_mlp, g_final, loss_row=None):
    tail = jnp.pad(b, ((0, 7), (0, LANES - b.shape[1])))
    if loss_row is not None:
        tail = tail + jnp.pad(loss_row, ((LOSS_ROW - 24, 31 - LOSS_ROW), (0, 0)))
    return jnp.concatenate([g_attn.reshape(8, LANES), g_mlp.reshape(8, LANES), g_final.reshape(8, LANES), tail], axis=0)


def _unpack_small(p):
    return (p[0:8].reshape(1, D), p[24:25, :FOX_H], p[8:16].reshape(1, D), p[16:24].reshape(D))


class _StepComm:
    def __init__(self, ids, w_sh, m_sh, v_sh, after_w_in):
        self.ids = ids
        self.w_sh, self.m_sh, self.v_sh = w_sh, m_sh, v_sh
        self.updates = None
        lands = _place_own(w_sh[1:])
        self.gather_mixer = _exchange_start("gather_mixer_start", _peer_copies, PEER_COPIES * 3, [], lands[:3],
                                            after=(after_w_in,))
        self.gather_mlp = _exchange_start("gather_mlp_start", _peer_copies, PEER_COPIES * 2, [], lands[3:],
                                          after=(self.gather_mixer[-1],))
        self.sibling = {}
        self.chips = {}
        self.own = {}
        self.names = {}

    def _reduce_start(self, group, names, grads, after=()):
        lands = [lax.empty((SIBLING_COPIES,) + t.shape[1:], F32) for t in grads]
        self.sibling[group] = _exchange_start("grad_%s_sibling_start" % group, _sibling_copies,
                                              SIBLING_COPIES * len(grads), grads, lands, after=after)
        self.names[group] = names
        return self.sibling[group][-1]

    def _reduce_mid(self, group, after):
        grads, recv1 = _exchange_wait("grad_%s_sibling_wait" % group, _sibling_copies, self.sibling[group], after)
        parts = [_chip_partial(self.ids, g, r, "grad_partial_" + n) for g, r, n in zip(grads, recv1, self.names[group])]
        self.own[group] = [p[1] for p in parts]
        srcs = [p[0] for p in parts]
        lands = [lax.empty((CHIP_COPIES,) + t.shape[1:], BF16) for t in srcs]
        self.chips[group] = _exchange_start("grad_%s_chips_start" % group, _chip_copies, CHIP_COPIES * len(srcs),
                                            srcs, lands)
        return self.chips[group][-1]

    def reduced(self, group, after):
        _, recv2 = _exchange_wait("grad_%s_chips_wait" % group, _chip_copies, self.chips[group], after)
        return list(zip(self.own[group], recv2))

    def first_deps(self):
        return [self.gather_mlp[-1]]

    def mixer_weights(self, after):
        _, (g_a, g_b, g_out) = _exchange_wait("gather_mixer_wait", _peer_copies, self.gather_mixer, after)
        return _cols_to_whole(g_a), _cols_to_whole(g_b), g_out.reshape(D, D)

    def mlp_weights(self, after):
        _, (g_up, g_down) = _exchange_wait("gather_mlp_wait", _peer_copies, self.gather_mlp, after)
        return g_up, g_down.reshape(DFF, D)

    def mlp_grads(self, dw_up_sh, dw_down):
        return [self._reduce_start("mlp", W_NAMES[4:], [dw_up_sh, dw_down.reshape((N_DEV,) + SHARD_SHAPES[5])])]

    def mixer_grads(self, dw_a, dw_b, dw_out):
        token = self._reduce_mid("mlp", dw_b)
        grads = [_whole_to_cols(dw_a), _whole_to_cols(dw_b), dw_out.reshape((N_DEV,) + SHARD_SHAPES[3])]
        return [self._reduce_start("mixer", W_NAMES[1:4], grads, after=(token,))]

    def mid_backward(self, after):
        return [self._reduce_mid("mixer", after)]

    def w_in_grad(self, dw_in_t):
        grads = [_unpad_dw_in_t(dw_in_t)]
        token = self._reduce_start("w_in", W_NAMES[:1], grads)
        reduced = self.reduced("mixer", token) + self.reduced("mlp", token)
        self.updates, last = [None] * len(reduced), token
        for i in (3, 4, 0, 1, 2):
            if i == 0:
                last = self._reduce_mid("w_in", last)
            self.updates[i] = _reduce_adamw(*reduced[i], self.w_sh[1 + i], self.m_sh[1 + i], self.v_sh[1 + i],
                                            "adamw_" + W_NAMES[1 + i], deps=[last])
            last = self.updates[i][0]
        return [last]

    def w_in_update(self, after):
        (reduced,) = self.reduced("w_in", after)
        return _reduce_adamw(*reduced, self.w_sh[0], self.m_sh[0], self.v_sh[0], "adamw_" + W_NAMES[0])


def kernel(x, norm_attn_g, w_in, b_forget, w_branch_a, w_branch_b, w_out, norm_mlp_g, w_up, w_down, norm_final_g, loss_target, m_norm_attn_g, m_w_in, m_b_forget, m_w_branch_a, m_w_branch_b, m_w_out, m_norm_mlp_g, m_w_up, m_w_down, m_norm_final_g, v_norm_attn_g, v_w_in, v_b_forget, v_w_branch_a, v_w_branch_b, v_w_out, v_norm_mlp_g, v_w_up, v_w_down, v_norm_final_g):
    cx, cy, cc = _place()
    ids = jnp.stack([cc, 2 * cx + cy]).astype(jnp.int32)

    w_sh = [w_in[0].T] + [t[0] for t in (w_branch_a, w_branch_b, w_out, w_up, w_down)]
    m_sh = [m_w_in[0].T] + [t[0] for t in (m_w_branch_a, m_w_branch_b, m_w_out, m_w_up, m_w_down)]
    v_sh = [v_w_in[0].T] + [t[0] for t in (v_w_branch_a, v_w_branch_b, v_w_out, v_w_up, v_w_down)]

    w_in_t, h1 = _gather_w_in(jnp.pad(w_sh[0], ((0, W_SLAB - w_sh[0].shape[0]), (0, 0))), x[0], norm_attn_g)
    comm = _StepComm(ids, w_sh, m_sh, v_sh, w_in_t)
    b_pad = jnp.pad(b_forget, ((0, 0), (0, LANES - FOX_H)))

    loss_row, dx, dsmall = _local_step(
        x[0], loss_target[0], h1, norm_attn_g, norm_mlp_g, norm_final_g.reshape(1, D), b_pad, w_in_t, comm)

    dg_attn, db, dg_mlp, dg_final = dsmall
    small = _small_allreduce_adamw(
        _pack_small(dg_attn, db, dg_mlp, dg_final, loss_row),
        _pack_small(norm_attn_g, b_forget, norm_mlp_g, norm_final_g.reshape(1, D)),
        _pack_small(m_norm_attn_g, m_b_forget, m_norm_mlp_g, m_norm_final_g.reshape(1, D)),
        _pack_small(v_norm_attn_g, v_b_forget, v_norm_mlp_g, v_norm_final_g.reshape(1, D)))
    big = [comm.w_in_update(small[0])] + comm.updates

    outs = [small[0][LOSS_ROW, 0], dx[None]]
    for q in range(4):
        s_attn, s_b, s_mlp, s_final = _unpack_small(small[q])
        b_in, b_a, b_b, b_out, b_up, b_down = [t[q][None] for t in big]
        b_in = jnp.swapaxes(b_in, 1, 2)
        outs += [s_attn, b_in, s_b, b_a, b_b, b_out, s_mlp, b_up, b_down, s_final]
    return tuple(outs)
```

```python
import functools

import jax
import jax.numpy as jnp
from jax import lax
from jax.experimental import pallas as pl
from jax.experimental.pallas import tpu as pltpu

F32 = jnp.float32
BF16 = jnp.bfloat16
MESH = pl.DeviceIdType.MESH

S = 2048
D = 1024
HD = 64
FOX_H = 8
FOX_W = FOX_H * HD
DIL_HG = 4
DIL_G = 3
DIL_W = DIL_G * DIL_HG * HD
DIL_OUT = DIL_HG * HD
DIL_BLK = 128
DIL_R = (1, 4, 16)
DFF = 4 * D
D_IN = 3 * FOX_W + FOX_H + 3 * DIL_W + 2 * D
EPS = 1e-6
NEG_INF = -1e30
SCALE = HD ** -0.5
ROPE_THETA = 500000.0
ROPE_DIM = HD // 4
N_DEV = 8

ADAM_LR = 0.001
ADAM_B1 = 0.9
ADAM_B2 = 0.999
ADAM_EPS = 1e-08
ADAM_WD = 0.01
ADAM_STEP = 10

LANES = 128
CB = 256
SEG_A = 0
SEG_B = 3 * FOX_W
SEG_Z = SEG_B + 3 * DIL_W
SEG_G = SEG_Z + CB
SEG_F = SEG_G + 2 * D
D_PAD = SEG_F + CB
SMALL_R = 32

VMEM_MB = 56


def _cparams(dims=None, vmem_mb=VMEM_MB, **kw):
    return pltpu.CompilerParams(dimension_semantics=dims, vmem_limit_bytes=vmem_mb << 20, **kw)


ANY = pl.BlockSpec(memory_space=pl.ANY)

_NN = (((1,), (0,)), ((), ()))
_NT = (((1,), (1,)), ((), ()))
_TN = (((0,), (0,)), ((), ()))


def _dot(a, b, dims):
    return lax.dot_general(a.astype(BF16), b.astype(BF16), dims, preferred_element_type=F32)


def _mm(a, b, *, mode, tm, tn, out_dtypes, name, n=None, k=None, a_off=0, b_off=0,
        b_sharded=False, out_sharded=False, epilogue=None, extras=(), deps=()):
    n_sh = b.shape[-1] if b_sharded else None
    if mode == "nn":
        m = a.shape[0]
        k = k or a.shape[1]
        a_spec = pl.BlockSpec((tm, k), lambda i, j: (i, a_off))
        if b_sharded:
            assert tn == n_sh
            n = N_DEV * n_sh
            b_spec = pl.BlockSpec((None, k, tn), lambda i, j: (j, 0, 0))
        else:
            n = n or b.shape[1]
            b_spec = pl.BlockSpec((k, tn), lambda i, j: (0, j + b_off))
        dims = _NN
    elif mode == "nt":
        m = a.shape[0]
        k = k or a.shape[1]
        a_spec = pl.BlockSpec((tm, k), lambda i, j: (i, a_off))
        if b_sharded:
            n = b.shape[1]
            b_spec = pl.BlockSpec((N_DEV, tn, n_sh), lambda i, j: (0, j, 0))
        else:
            n = n or b.shape[0]
            b_spec = pl.BlockSpec((tn, k), lambda i, j: (j + b_off, 0))
        dims = _NT
    else:
        k, m = a.shape
        n = n or b.shape[1]
        a_spec = pl.BlockSpec((k, tm), lambda i, j: (0, i))
        b_spec = pl.BlockSpec((k, tn), lambda i, j: (0, j + b_off))
        dims = _TN
    assert m % tm == 0 and n % tn == 0, (name, m, n, tm, tn)
    n_extra = len(extras)
    tile = pl.BlockSpec((tm, tn), lambda i, j: (i, j))
    if out_sharded:
        assert mode == "tn" and n // tn == N_DEV
        out_spec = pl.BlockSpec((None, tm, tn), lambda i, j: (j, i, 0))
        out_shape = (N_DEV, m, tn)
    else:
        out_spec, out_shape = tile, (m, n)

    def body(a_ref, b_ref, *refs):
        if mode == "nt" and b_sharded:
            acc = _dot(a_ref[:, 0:n_sh], b_ref[0], dims)
            for p in range(1, N_DEV):
                acc = acc + _dot(a_ref[:, p * n_sh:(p + 1) * n_sh], b_ref[p], dims)
        else:
            acc = _dot(a_ref[...], b_ref[...], dims)
        ex = [r[...] for r in refs[:n_extra]]
        outs = epilogue(acc, *ex) if epilogue is not None else (acc,)
        for o_ref, o in zip(refs[n_extra + len(deps):], outs):
            o_ref[...] = o.astype(o_ref.dtype)

    res = pl.pallas_call(
        body, name=name, grid=(m // tm, n // tn),
        in_specs=[a_spec, b_spec] + [tile] * n_extra + [ANY] * len(deps),
        out_specs=[out_spec] * len(out_dtypes),
        out_shape=[jax.ShapeDtypeStruct(out_shape, dt) for dt in out_dtypes],
        compiler_params=_cparams(("parallel", "parallel")),
    )(a, b, *extras, *deps)
    return res if len(out_dtypes) > 1 else res[0]


def _mm_rows(a, b, *, tm, name, epilogue, rows=(), vecs=(), row_out=(), vec_out=(), b_sharded=False, deps=()):
    m, k = a.shape
    n_rows, n_vecs, n_deps = len(rows), len(vecs), len(deps)
    n_sh = b.shape[-1] if b_sharded else None

    def body(a_ref, b_ref, *refs):
        row_refs, vec_refs = refs[:n_rows], refs[n_rows:n_rows + n_vecs]
        outs = refs[n_rows + n_vecs + n_deps:]
        if b_sharded:
            acc = _dot(a_ref[:, 0:n_sh], b_ref[0], _NT)
            for p in range(1, N_DEV):
                acc = acc + _dot(a_ref[:, p * n_sh:(p + 1) * n_sh], b_ref[p], _NT)
        else:
            acc = _dot(a_ref[...], b_ref[...], _NN)
        row_vals, vec_incs = epilogue(acc, [r[...] for r in row_refs], [v[...] for v in vec_refs])
        for o_ref, val in zip(outs[:len(row_out)], row_vals):
            o_ref[...] = val.astype(o_ref.dtype)

        @pl.when(pl.program_id(0) == 0)
        def _():
            for o_ref in outs[len(row_out):]:
                o_ref[...] = jnp.zeros_like(o_ref)

        for o_ref, inc in zip(outs[len(row_out):], vec_incs):
            o_ref[...] += inc

    tile = pl.BlockSpec((tm, D), lambda i: (i, 0))
    b_spec = pl.BlockSpec(b.shape, lambda i: (0,) * b.ndim)
    return pl.pallas_call(
        body, name=name, grid=(m // tm,),
        in_specs=[pl.BlockSpec((tm, k), lambda i: (i, 0)), b_spec] + [tile] * n_rows
                 + [pl.BlockSpec(v.shape, lambda i: (0, 0)) for v in vecs] + [ANY] * n_deps,
        out_specs=[tile] * len(row_out) + [pl.BlockSpec((1, w), lambda i: (0, 0)) for w in vec_out],
        out_shape=[jax.ShapeDtypeStruct((m, D), dt) for dt in row_out]
                  + [jax.ShapeDtypeStruct((1, w), F32) for w in vec_out],
        compiler_params=_cparams(("arbitrary",)),
    )(a, b, *rows, *vecs, *deps)


ROW_T = 256
ROWS_TM = 512


def _rms_rows(x, g):
    r = lax.rsqrt(jnp.mean(x * x, axis=-1, keepdims=True) + EPS)
    return (x * r) * g


def _rms_bwd_rows(dh, x, dres, g):
    r = lax.rsqrt(jnp.mean(x * x, axis=-1, keepdims=True) + EPS)
    xn = x * r
    dhn = dh * g
    dx = dres + r * (dhn - xn * jnp.mean(dhn * xn, axis=-1, keepdims=True))
    return dx, jnp.sum(dh * xn, axis=0, keepdims=True)


def _final_loss_rows(x, g, tgt):
    r = lax.rsqrt(jnp.mean(x * x, axis=-1, keepdims=True) + EPS)
    xn = x * r
    err = xn * g - tgt
    row_loss = jnp.mean(err * err, axis=-1, keepdims=True)
    loss = 0.5 * jnp.sum(row_loss, axis=0, keepdims=True) * jnp.ones((1, LANES), F32)
    dy = err * (1.0 / D)
    dyn = dy * g
    dx = r * (dyn - xn * jnp.mean(dyn * xn, axis=-1, keepdims=True))
    return dx, jnp.sum(dy * xn, axis=0, keepdims=True), loss


def _sigmoid(z):
    return 1.0 / (1.0 + jnp.exp(-z))


def _gate_fwd(proj_g, ya, yb):
    def body(ga_ref, gb_ref, ya_ref, yb_ref, o_ref):
        o_ref[...] = (_sigmoid(ga_ref[...]) * ya_ref[...] + _sigmoid(gb_ref[...]) * yb_ref[...]).astype(BF16)

    row = pl.BlockSpec((ROW_T, D), lambda i: (i, 0))
    return pl.pallas_call(
        body, name="gate_fwd", grid=(S // ROW_T,),
        in_specs=[row, pl.BlockSpec((ROW_T, D), lambda i: (i, 1)), row, row],
        out_specs=row, out_shape=jax.ShapeDtypeStruct((S, D), BF16),
        compiler_params=_cparams(("parallel",)),
    )(proj_g, proj_g, ya, yb)


GATE_TR = 1024
GATE_TC = 512
GATE_NB = 2 * D // GATE_TC


def _gate_bwd(dmixed, proj_g, ya, yb):
    half = GATE_NB // 2

    def body(dm_ref, g_ref, ya_ref, yb_ref, dy_ref, dg_ref):
        dm = dm_ref[...]
        s = _sigmoid(g_ref[...])
        y = jnp.where(pl.program_id(1) < half, ya_ref[...], yb_ref[...])
        dy_ref[...] = (dm * s).astype(BF16)
        dg_ref[...] = (dm * y * (s * (1.0 - s))).astype(BF16)

    blk = pl.BlockSpec((GATE_TR, GATE_TC), lambda i, j: (i, j))
    wrapped = pl.BlockSpec((GATE_TR, GATE_TC), lambda i, j: (i, j % half))
    return pl.pallas_call(
        body, name="gate_bwd", grid=(S // GATE_TR, GATE_NB),
        in_specs=[wrapped, blk, wrapped, wrapped],
        out_specs=[blk, pl.BlockSpec((GATE_TR, GATE_TC), lambda i, j: (i, j + SEG_G // GATE_TC))],
        out_shape=[jax.ShapeDtypeStruct((S, 2 * D), BF16), jax.ShapeDtypeStruct((S, D_PAD), BF16)],
        compiler_params=_cparams(("parallel", "parallel")),
    )(dmixed, proj_g, ya, yb)


FOX_TQ = 256


def _scan_rows(x, reverse):
    n = x.shape[0]
    row = lax.broadcasted_iota(jnp.int32, x.shape, 0)
    k = 1
    while k < n:
        if reverse:
            x = x + jnp.where(row < n - k, pltpu.roll(x, n - k, 0), 0.0)
        else:
            x = x + jnp.where(row >= k, pltpu.roll(x, k, 0), 0.0)
        k *= 2
    return x


def _fox_dscan(dft, dfs, proj_f, b_pad, dproj):
    def body(dft_ref, dfs_ref, f_ref, b_ref, _, dfa_ref, db_ref):
        dfs_pad = jnp.concatenate([dfs_ref[...], jnp.zeros((LANES - FOX_H, S), F32)], axis=0)
        df = dfs_pad.T + dft_ref[0]
        for hp in range(1, dft_ref.shape[0]):
            df = df + pltpu.roll(dft_ref[hp], 2 * hp, 1)
        dlf = _scan_rows(df, reverse=True)
        z = f_ref[...] + b_ref[...]
        lane = lax.broadcasted_iota(jnp.int32, (S, LANES), 1)
        dfa = jnp.where(lane < FOX_H, dlf / (1.0 + jnp.exp(z)), 0.0)
        dfa_ref[:, :LANES] = dfa.astype(BF16)
        dfa_ref[:, LANES:] = jnp.zeros((S, CB - LANES), BF16)
        db_ref[...] = jnp.sum(dfa, axis=0, keepdims=True)

    return pl.pallas_call(
        body, name="fox_dscan", grid=(1,),
        in_specs=[pl.BlockSpec(dft.shape, lambda i: (0, 0, 0)), pl.BlockSpec((FOX_H, S), lambda i: (0, 0)),
                  pl.BlockSpec((S, LANES), lambda i: (0, 0)), pl.BlockSpec((1, LANES), lambda i: (0, 0)), ANY],
        out_specs=[pl.BlockSpec((S, CB), lambda i: (0, SEG_F // CB)), pl.BlockSpec((1, LANES), lambda i: (0, 0))],
        out_shape=[jax.ShapeDtypeStruct((S, D_PAD), BF16), jax.ShapeDtypeStruct((1, LANES), F32)],
        input_output_aliases={4: 0},
        compiler_params=_cparams(("arbitrary",)),
    )(dft, dfs, proj_f, b_pad, dproj)


FOX_NQ = S // FOX_TQ
FOX_HP = FOX_W // LANES
HEADS_PER_LB = LANES // HD
FOX_AUG = 2 * LANES


def _fox_prepare(proj_a, proj_f, b_pad):
    tr = 512

    def body(q_ref, k_ref, f_ref, b_ref, qa_ref, ka_ref, hi_s, mid_s, lo_s):
        i = pl.program_id(0)

        @pl.when(i == 0)
        def _():
            z = f_ref[...] + b_ref[...]
            lf = jnp.minimum(z, 0.0) - jnp.log1p(jnp.exp(-jnp.abs(z)))
            f = _scan_rows(lf, reverse=False)
            hi = f.astype(BF16).astype(F32)
            r1 = f - hi
            mid = r1.astype(BF16).astype(F32)
            hi_s[...] = hi
            mid_s[...] = mid
            lo_s[...] = (r1 - mid).astype(BF16).astype(F32)

        rows = pl.ds(pl.multiple_of(i * tr, tr), tr)
        hi, mid, lo = hi_s[rows, :], mid_s[rows, :], lo_s[rows, :]
        lane = lax.broadcasted_iota(jnp.int32, (tr, HD), 1)
        q_ones = jnp.where((lane >= 3) & (lane < 6), 1.0, 0.0)
        k_ones = jnp.where(lane < 3, 1.0, 0.0)
        for h in range(FOX_H):
            a1, a2, a3 = hi[:, h:h + 1], mid[:, h:h + 1], lo[:, h:h + 1]
            q_extra = jnp.where(lane == 0, a1, jnp.where(lane == 1, a2, jnp.where(lane == 2, a3, q_ones)))
            k_extra = jnp.where(lane == 3, -a1, jnp.where(lane == 4, -a2, jnp.where(lane == 5, -a3, k_ones)))
            base = h * LANES
            qa_ref[:, base:base + HD] = (q_ref[:, h * HD:(h + 1) * HD].astype(F32) * SCALE).astype(BF16)
            qa_ref[:, base + HD:base + LANES] = q_extra.astype(BF16)
            ka_ref[:, base:base + HD] = k_ref[:, h * HD:(h + 1) * HD]
            ka_ref[:, base + HD:base + LANES] = k_extra.astype(BF16)

    out_blk = pl.BlockSpec((tr, FOX_H * LANES), lambda i: (i, 0))
    return pl.pallas_call(
        body, name="fox_prepare", grid=(S // tr,),
        in_specs=[pl.BlockSpec((tr, FOX_W), lambda i: (i, 0)), pl.BlockSpec((tr, FOX_W), lambda i: (i, 1)),
                  pl.BlockSpec((S, LANES), lambda i: (0, 0)), pl.BlockSpec((1, LANES), lambda i: (0, 0))],
        out_specs=[out_blk, out_blk],
        out_shape=[jax.ShapeDtypeStruct((S, FOX_H * LANES), BF16)] * 2,
        scratch_shapes=[pltpu.VMEM((S, LANES), F32)] * 3,
        compiler_params=_cparams(("arbitrary",)),
    )(proj_a, proj_a, proj_f, b_pad)


def _fox_scores(qa, ka, w):
    s = _dot(qa, ka, _NT)
    row = lax.broadcasted_iota(jnp.int32, (FOX_TQ, FOX_TQ), 0)
    col = lax.broadcasted_iota(jnp.int32, (FOX_TQ, FOX_TQ), 1)
    diag = jnp.where(col <= row, s[:, w * FOX_TQ:], NEG_INF)
    return diag if w == 0 else jnp.concatenate([s[:, :w * FOX_TQ], diag], axis=1)


def _fox_fwd(q_aug, k_aug, proj_a):
    def body(qa_ref, ka_ref, v_ref, o_ref, lse_ref):
        qi = pl.program_id(1)
        lane = lax.broadcasted_iota(jnp.int32, (FOX_TQ, LANES), 1)
        for w in range(FOX_NQ):
            @pl.when(qi == w)
            def _(w=w):
                width = (w + 1) * FOX_TQ
                lse = jnp.zeros((FOX_TQ, LANES), F32)
                for hh in range(HEADS_PER_LB):
                    aug = slice(hh * LANES, (hh + 1) * LANES)
                    sl = slice(hh * HD, (hh + 1) * HD)
                    s = _fox_scores(qa_ref[:, aug], ka_ref[:width, aug], w)
                    m = jnp.max(s, axis=-1, keepdims=True)
                    p = jnp.exp(s - m)
                    l = jnp.sum(p, axis=-1, keepdims=True)
                    o_ref[:, sl] = _dot(p / l, v_ref[:width, sl], _NN)
                    lse = jnp.where(lane == hh, m + jnp.log(l), lse)
                lse_ref[...] = lse

    return pl.pallas_call(
        body, name="fox_fwd", grid=(FOX_HP, FOX_NQ),
        in_specs=[pl.BlockSpec((FOX_TQ, FOX_AUG), lambda h, i: (i, h)),
                  pl.BlockSpec((S, FOX_AUG), lambda h, i: (0, h)),
                  pl.BlockSpec((S, LANES), lambda h, i: (0, 2 * FOX_HP + h))],
        out_specs=[pl.BlockSpec((FOX_TQ, LANES), lambda h, i: (i, h)),
                   pl.BlockSpec((None, FOX_TQ, LANES), lambda h, i: (h, i, 0))],
        out_shape=[jax.ShapeDtypeStruct((S, FOX_W), F32), jax.ShapeDtypeStruct((FOX_HP, S, LANES), F32)],
        compiler_params=_cparams(("parallel", "parallel")),
    )(q_aug, k_aug, proj_a)


def _fox_bwd(q_aug, k_aug, proj_a, lse, do, dproj, deps=()):
    n_q = FOX_NQ

    def body(qa_ref, ka_ref, v_ref, lse_ref, do_ref, *refs):
        dproj_ref, dft_ref, dfs_ref, dk_acc, dv_acc, dq_buf, kv_buf, dq_sems, kv_sems = refs[1 + len(deps):]
        hp = pl.program_id(0)
        t = pl.program_id(1)
        slot = t % 2
        col = pl.multiple_of(hp * LANES, LANES)

        def dq_copy(step, buf_slot):
            rows = pl.ds(pl.multiple_of(step * FOX_TQ, FOX_TQ), FOX_TQ)
            return pltpu.make_async_copy(dq_buf.at[buf_slot], dproj_ref.at[rows, pl.ds(col, LANES)], dq_sems.at[buf_slot])

        @pl.when(t == 0)
        def _():
            dk_acc[...] = jnp.zeros_like(dk_acc)
            dv_acc[...] = jnp.zeros_like(dv_acc)

        @pl.when((t == 0) & (hp == 0))
        def _():
            dfs_ref[...] = jnp.zeros_like(dfs_ref)

        @pl.when(t >= 2)
        def _():
            dq_copy(t - 2, slot).wait()

        lane = lax.broadcasted_iota(jnp.int32, (FOX_TQ, LANES), 1)
        for w in range(n_q):
            @pl.when(t == w)
            def _(w=w):
                width = (w + 1) * FOX_TQ
                lse_all = lse_ref[...]
                sub = lax.broadcasted_iota(jnp.int32, (FOX_H, width), 0)
                dft = jnp.zeros((FOX_TQ, LANES), F32)
                for hh in range(HEADS_PER_LB):
                    aug = slice(hh * LANES, (hh + 1) * LANES)
                    sl = slice(hh * HD, (hh + 1) * HD)
                    head = hp * HEADS_PER_LB + hh
                    qa = qa_ref[:, aug]
                    ka = ka_ref[:width, aug]
                    do_h = do_ref[:, sl]
                    s = _fox_scores(qa, ka, w)
                    p = jnp.exp(s - lse_all[:, hh:hh + 1])
                    dp = _dot(do_h, v_ref[:width, sl], _NT)
                    ds = p * (dp - jnp.sum(dp * p, axis=-1, keepdims=True))
                    dft = jnp.where(lane == hh, jnp.sum(ds, axis=-1, keepdims=True), dft)
                    dfs_ref[:, :width] -= jnp.where(sub == head, jnp.sum(ds, axis=0, keepdims=True), 0.0)
                    dq_buf[slot, :, sl] = (_dot(ds, ka[:, :HD], _NN) * SCALE).astype(BF16)
                    dk_acc[:width, sl] += _dot(ds, qa[:, :HD], _TN)
                    dv_acc[:width, sl] += _dot(p, do_h, _TN)
                dft_ref[...] = dft

        dq_copy(t, slot).start()

        @pl.when(t == n_q - 1)
        def _():
            dq_copy(t - 1, 1 - slot).wait()
            dq_copy(t, slot).wait()
            kv_buf[0] = dk_acc[...].astype(BF16)
            kv_buf[1] = dv_acc[...].astype(BF16)
            copies = [pltpu.make_async_copy(
                kv_buf.at[j], dproj_ref.at[:, pl.ds(pl.multiple_of((j + 1) * FOX_W + hp * LANES, LANES), LANES)],
                kv_sems.at[j]) for j in range(2)]
            for cp in copies:
                cp.start()
            for cp in copies:
                cp.wait()

    qblk = pl.BlockSpec((FOX_TQ, FOX_AUG), lambda h, t: (t, h))
    lane_blk = pl.BlockSpec((None, FOX_TQ, LANES), lambda h, t: (h, t, 0))
    return pl.pallas_call(
        body, name="fox_bwd", grid=(FOX_HP, n_q),
        in_specs=[qblk, pl.BlockSpec((S, FOX_AUG), lambda h, t: (0, h)),
                  pl.BlockSpec((S, LANES), lambda h, t: (0, 2 * FOX_HP + h)),
                  lane_blk, pl.BlockSpec((FOX_TQ, LANES), lambda h, t: (t, h)), ANY] + [ANY] * len(deps),
        out_specs=[ANY, lane_blk, pl.BlockSpec((FOX_H, S), lambda h, t: (0, 0))],
        out_shape=[jax.ShapeDtypeStruct((S, D_PAD), BF16),
                   jax.ShapeDtypeStruct((FOX_HP, S, LANES), F32), jax.ShapeDtypeStruct((FOX_H, S), F32)],
        scratch_shapes=[pltpu.VMEM((S, LANES), F32), pltpu.VMEM((S, LANES), F32),
                        pltpu.VMEM((2, FOX_TQ, LANES), BF16), pltpu.VMEM((2, S, LANES), BF16),
                        pltpu.SemaphoreType.DMA((2,)), pltpu.SemaphoreType.DMA((2,))],
        input_output_aliases={5: 0},
        compiler_params=_cparams(("arbitrary", "arbitrary")),
    )(q_aug, k_aug, proj_a, lse, do, dproj, *deps)


def _rope_tables():
    half = ROPE_DIM // 2
    shape = (DIL_G, S, half)
    g = lax.broadcasted_iota(jnp.int32, shape, 0)
    row = lax.broadcasted_iota(jnp.int32, shape, 1)
    r = jnp.left_shift(1, 2 * g)
    per_class = S // r
    pos = (row % per_class) * r + row // per_class
    inv_freq = jnp.power(jnp.float32(ROPE_THETA), -lax.broadcasted_iota(F32, shape, 2) * 2.0 / ROPE_DIM)
    ang = pos.astype(F32) * inv_freq
    return jnp.concatenate([jnp.cos(ang), jnp.sin(ang)], axis=-1)


def _rope_factors(cs):
    half = ROPE_DIM // 2
    i = lax.broadcasted_iota(jnp.int32, (2 * half, 2 * LANES), 0)
    l = lax.broadcasted_iota(jnp.int32, (2 * half, 2 * LANES), 1)
    hit = ((l < LANES) & (i == l % half)) | ((l >= LANES) & (i - half == l % half))
    spread = jnp.where(hit, 1.0, 0.0).astype(BF16)
    hi = cs.astype(BF16)
    r1 = cs - hi.astype(F32)
    mid = r1.astype(BF16)
    lo = (r1 - mid.astype(F32)).astype(BF16)
    full = (_dot(hi, spread, _NN) + _dot(mid, spread, _NN)) + _dot(lo, spread, _NN)
    cos, sin = full[:, :LANES], full[:, LANES:]
    lane = lax.broadcasted_iota(jnp.int32, cos.shape, 1) % HD
    t_self = jnp.where(lane < ROPE_DIM, cos, 1.0)
    t_up = jnp.where(lane < half, -sin, 0.0)
    t_dn = jnp.where((lane >= half) & (lane < ROPE_DIM), sin, 0.0)
    return t_self, t_up, t_dn


def _residue_pieces(r):
    if r == 1:
        return [(slice(i, i + 512), slice(i, i + 512)) for i in range(0, S, 512)]
    n = S // r
    return [(pl.ds(j, n, stride=r), slice(j * n, (j + 1) * n)) for j in range(r)]


def _rope_apply(x, a, u, d, backward):
    half = ROPE_DIM // 2
    if backward:
        return x * a + pltpu.roll(x * u, half, 1) + pltpu.roll(x * d, LANES - half, 1)
    return x * a + pltpu.roll(x, LANES - half, 1) * u + pltpu.roll(x, half, 1) * d


def _rope_cache_factors(cs_ref, fac, first_step_of_group):
    @pl.when(first_step_of_group)
    def _():
        for lo in range(0, S, 512):
            for j, f in enumerate(_rope_factors(cs_ref[lo:lo + 512, :])):
                fac[j, lo:lo + 512, :] = f


LB_PER_CB = CB // LANES
ROPE_NB = 3 * DIL_G * LB_PER_CB


def _rope_step(s):
    per_group = 3 * LB_PER_CB
    return s // per_group, (s % per_group) // LB_PER_CB, s % LB_PER_CB


def _rope_split(proj_b, tables):
    def body(x_ref, cs_ref, o_ref, fac):
        g, t, hf = _rope_step(pl.program_id(0))
        _rope_cache_factors(cs_ref, fac, (t == 0) & (hf == 0))
        for gs in range(DIL_G):
            @pl.when(g == gs)
            def _(gs=gs):
                for tok, sub in _residue_pieces(DIL_R[gs]):
                    x = x_ref[tok, :]
                    y = _rope_apply(x, fac[0, sub, :], fac[1, sub, :], fac[2, sub, :], False)
                    o_ref[sub, :] = jnp.where(t < 2, y, x).astype(BF16)

    def in_index(s):
        g, t, hf = _rope_step(s)
        return 0, (t * DIL_G + g) * LB_PER_CB + hf

    def out_index(s):
        g, t, hf = _rope_step(s)
        return t * DIL_G + g, 0, hf

    tab = pl.BlockSpec((None, S, 2 * (ROPE_DIM // 2)), lambda s: (_rope_step(s)[0], 0, 0))
    return pl.pallas_call(
        body, name="rope_split", grid=(ROPE_NB,),
        in_specs=[pl.BlockSpec((S, LANES), in_index), tab],
        out_specs=pl.BlockSpec((None, S, LANES), out_index),
        out_shape=jax.ShapeDtypeStruct((3 * DIL_G, S, CB), BF16),
        scratch_shapes=[pltpu.VMEM((3, S, LANES), F32)],
        compiler_params=_cparams(("arbitrary",)),
    )(proj_b, tables)


def _rope_merge_bwd(dq3, dk3, dv3, tables, dproj):
    def body(dq_ref, dk_ref, dv_ref, cs_ref, _, o_ref, tmp, fac):
        s = pl.program_id(0)
        g, t, hf = _rope_step(s)
        _rope_cache_factors(cs_ref, fac, (t == 0) & (hf == 0) & (s < ROPE_NB))
        for gs in range(DIL_G):
            @pl.when((g == gs) & (s < ROPE_NB))
            def _(gs=gs):
                for tok, sub in _residue_pieces(DIL_R[gs]):
                    x = jnp.where(t == 0, dq_ref[sub, :], jnp.where(t == 1, dk_ref[sub, :], dv_ref[sub, :]))
                    y = _rope_apply(x, fac[0, sub, :], fac[1, sub, :], fac[2, sub, :], True)
                    tmp[tok, :] = jnp.where(t < 2, y, x)
                o_ref[...] = tmp[...].astype(BF16)

        @pl.when(s >= ROPE_NB)
        def _():
            o_ref[...] = jnp.zeros_like(o_ref)

    def src_spec(own):
        def index(s):
            g, t, hf = _rope_step(jnp.minimum(s, ROPE_NB - 1))
            return g, 0, jnp.where(t == own, hf, jnp.where(t > own, LB_PER_CB - 1, 0))
        return pl.BlockSpec((None, S, LANES), index)

    def out_index(s):
        g, t, hf = _rope_step(s)
        col = SEG_B // LANES + (t * DIL_G + g) * LB_PER_CB + hf
        return 0, jnp.where(s < ROPE_NB, col, SEG_Z // LANES + s - ROPE_NB)

    tab = pl.BlockSpec((None, S, 2 * (ROPE_DIM // 2)), lambda s: (_rope_step(jnp.minimum(s, ROPE_NB - 1))[0], 0, 0))
    return pl.pallas_call(
        body, name="rope_merge_bwd", grid=(ROPE_NB + LB_PER_CB,),
        in_specs=[src_spec(0), src_spec(1), src_spec(2), tab, ANY],
        out_specs=pl.BlockSpec((S, LANES), out_index),
        out_shape=jax.ShapeDtypeStruct((S, D_PAD), BF16),
        scratch_shapes=[pltpu.VMEM((S, LANES), F32), pltpu.VMEM((3, S, LANES), F32)],
        input_output_aliases={4: 0},
        compiler_params=_cparams(("arbitrary",)),
    )(dq3, dk3, dv3, tables, dproj)


DIL_NB = S // DIL_BLK


_BQK = (((2,), (2,)), ((0,), (0,)))
_BQD = (((2,), (1,)), ((0,), (0,)))
_BKD = (((1,), (1,)), ((0,), (0,)))


def _dil_mask(g):
    shape = (DIL_NB, DIL_BLK, 2 * DIL_BLK)
    blocks_per_seq = jnp.right_shift(DIL_NB, 2 * g)
    has_prev = jnp.bitwise_and(lax.broadcasted_iota(jnp.int32, shape, 0), blocks_per_seq - 1) != 0
    a = lax.broadcasted_iota(jnp.int32, shape, 1)
    kk = lax.broadcasted_iota(jnp.int32, shape, 2)
    diff = DIL_BLK + a - kk
    return (diff >= 0) & (diff <= DIL_BLK) & ((kk >= DIL_BLK) | has_prev)


def _blocks(t):
    return t.reshape(DIL_NB, DIL_BLK, t.shape[-1])


def _with_prev(t):
    prev = jnp.concatenate([jnp.zeros((DIL_BLK, t.shape[-1]), t.dtype), t[:-DIL_BLK]], axis=0)
    return jnp.concatenate([_blocks(prev), _blocks(t)], axis=1)


def _fold_prev(t2):
    n = t2.shape[-1]
    to_prev = t2[:, :DIL_BLK].reshape(S, n)
    own = t2[:, DIL_BLK:].reshape(S, n)
    return own + jnp.concatenate([to_prev[DIL_BLK:], jnp.zeros((DIL_BLK, n), t2.dtype)], axis=0)


def _dil_group_spec(t, width=DIL_OUT):
    return pl.BlockSpec((None, S, width), lambda g: (t * DIL_G + g, 0, 0))


def _dil_fwd(qkv3):
    def body(q_ref, k_ref, v_ref, o_ref, lse_ref):
        ok = _dil_mask(pl.program_id(0))
        lane = lax.broadcasted_iota(jnp.int32, (S, LANES), 1)
        lse = jnp.zeros((S, LANES), F32)
        for h in range(DIL_HG):
            sl = slice(h * HD, (h + 1) * HD)
            s = jnp.where(ok, _dot(_blocks(q_ref[:, sl]), _with_prev(k_ref[:, sl]), _BQK) * SCALE, NEG_INF)
            m = jnp.max(s, axis=-1, keepdims=True)
            p = jnp.exp(s - m)
            l = jnp.sum(p, axis=-1, keepdims=True)
            o_ref[:, sl] = _dot(p / l, _with_prev(v_ref[:, sl]), _BQD).reshape(S, HD)
            lse = jnp.where(lane == h, (m + jnp.log(l)).reshape(S, 1), lse)
        lse_ref[...] = lse

    return pl.pallas_call(
        body, name="dil_fwd", grid=(DIL_G,),
        in_specs=[_dil_group_spec(0), _dil_group_spec(1), _dil_group_spec(2)],
        out_specs=[_dil_group_spec(0), _dil_group_spec(0, LANES)],
        out_shape=[jax.ShapeDtypeStruct((DIL_G, S, DIL_OUT), F32), jax.ShapeDtypeStruct((DIL_G, S, LANES), F32)],
        compiler_params=_cparams(("parallel",)),
    )(qkv3, qkv3, qkv3)


def _dil_bwd(qkv3, lse3, do3, c3):
    def body(q_ref, k_ref, v_ref, lse_ref, do_ref, c_ref, dq_ref, dk_ref, dv_ref):
        ok = _dil_mask(pl.program_id(0))
        lse = lse_ref[...]
        c = c_ref[...]
        for h in range(DIL_HG):
            sl = slice(h * HD, (h + 1) * HD)
            q = _blocks(q_ref[:, sl])
            k2 = _with_prev(k_ref[:, sl])
            do_h = _blocks(do_ref[:, sl])
            s = jnp.where(ok, _dot(q, k2, _BQK) * SCALE, NEG_INF)
            p = jnp.exp(s - _blocks(lse[:, h:h + 1]))
            dp = _dot(do_h, _with_prev(v_ref[:, sl]), _BQK)
            ds = p * (dp - _blocks(c[:, h:h + 1]))
            dsb = (ds * SCALE).astype(BF16)
            dq_ref[:, sl] = _dot(dsb, k2, _BQD).reshape(S, HD)
            dk_ref[:, sl] = _fold_prev(_dot(dsb, q, _BKD))
            dv_ref[:, sl] = _fold_prev(_dot(p, do_h, _BKD))

    return pl.pallas_call(
        body, name="dil_bwd", grid=(DIL_G,),
        in_specs=[_dil_group_spec(0), _dil_group_spec(1), _dil_group_spec(2), _dil_group_spec(0, LANES),
                  _dil_group_spec(0), _dil_group_spec(0, LANES)],
        out_specs=[_dil_group_spec(0)] * 3,
        out_shape=[jax.ShapeDtypeStruct((DIL_G, S, DIL_OUT), F32)] * 3,
        compiler_params=_cparams(("parallel",)),
    )(qkv3, qkv3, qkv3, lse3, do3, c3)


COMB_T = 256


def _dil_combine(o3, lse3):
    heads_per_lb = LANES // HD

    def body(o_ref, lse_ref, ob_ref, al_ref, *scratch):
        o_tok = [scratch[LB_PER_CB * gg:LB_PER_CB * (gg + 1)] for gg in range(DIL_G)]
        lse_tok = scratch[LB_PER_CB * DIL_G:]
        g = pl.program_id(0)
        for gs in range(DIL_G):
            @pl.when(g == gs)
            def _(gs=gs):
                for tok, sub in _residue_pieces(DIL_R[gs]):
                    for hf in range(LB_PER_CB):
                        o_tok[gs][hf][tok, :] = o_ref[sub, hf * LANES:(hf + 1) * LANES]
                    lse_tok[gs][tok, :] = lse_ref[sub, :]

        @pl.when(g == DIL_G - 1)
        def _():
            def chunk(i, carry):
                rows = pl.ds(pl.multiple_of(i * COMB_T, COMB_T), COMB_T)
                lse = [lse_tok[gg][rows, :] for gg in range(DIL_G)]
                m = jnp.maximum(jnp.maximum(lse[0], lse[1]), lse[2])
                e = [jnp.exp(lse[gg] - m) for gg in range(DIL_G)]
                den = (e[0] + e[1]) + e[2]
                al = [e[gg] / den for gg in range(DIL_G)]
                for gg in range(DIL_G):
                    al_ref[gg, rows, :] = al[gg]
                for h in range(DIL_HG):
                    hf, sl = h // heads_per_lb, slice((h % heads_per_lb) * HD, (h % heads_per_lb + 1) * HD)
                    acc = al[0][:, h:h + 1] * o_tok[0][hf][rows, sl]
                    for gg in range(1, DIL_G):
                        acc = acc + al[gg][:, h:h + 1] * o_tok[gg][hf][rows, sl]
                    ob_ref[rows, h * HD:(h + 1) * HD] = acc
                return carry

            lax.fori_loop(0, S // COMB_T, chunk, 0)

    return pl.pallas_call(
        body, name="dil_combine", grid=(DIL_G,),
        in_specs=[pl.BlockSpec((None, S, DIL_OUT), lambda g: (g, 0, 0)),
                  pl.BlockSpec((None, S, LANES), lambda g: (g, 0, 0))],
        out_specs=[pl.BlockSpec((S, DIL_OUT), lambda g: (0, 0)),
                   pl.BlockSpec((DIL_G, S, LANES), lambda g: (0, 0, 0))],
        out_shape=[jax.ShapeDtypeStruct((S, DIL_OUT), F32), jax.ShapeDtypeStruct((DIL_G, S, LANES), F32)],
        scratch_shapes=[pltpu.VMEM((S, LANES), F32)] * (DIL_G * (LB_PER_CB + 1)),
        compiler_params=_cparams(("arbitrary",)),
    )(o3, lse3)


def _dil_combine_bwd(dob, ob, alpha, deps=()):
    heads_per_lb = LANES // HD

    def body(dob_ref, ob_ref, al_ref, *refs):
        do_ref, c_ref = refs[len(deps):]
        g = pl.program_id(0)
        hf = pl.program_id(1)
        for gs in range(DIL_G):
            @pl.when(g == gs)
            def _(gs=gs):
                for tok, sub in _residue_pieces(DIL_R[gs]):
                    dob = dob_ref[tok, :]
                    prod = dob * ob_ref[tok, :]
                    al = al_ref[tok, :]
                    lane = lax.broadcasted_iota(jnp.int32, al.shape, 1)
                    c = jnp.where(hf == 0, 0.0, c_ref[sub, :])
                    for hh in range(heads_per_lb):
                        sl = slice(hh * HD, (hh + 1) * HD)
                        head = hf * heads_per_lb + hh
                        a = jnp.sum(jnp.where(lane == head, al, 0.0), axis=-1, keepdims=True)
                        do_ref[sub, sl] = (a * dob[:, sl]).astype(BF16)
                        c = jnp.where(lane == head, a * jnp.sum(prod[:, sl], axis=-1, keepdims=True), c)
                    c_ref[sub, :] = c

    half = pl.BlockSpec((S, LANES), lambda g, hf: (0, hf))
    return pl.pallas_call(
        body, name="dil_combine_bwd", grid=(DIL_G, LB_PER_CB),
        in_specs=[half, half, pl.BlockSpec((None, S, LANES), lambda g, hf: (g, 0, 0))] + [ANY] * len(deps),
        out_specs=[pl.BlockSpec((None, S, LANES), lambda g, hf: (g, 0, hf)),
                   pl.BlockSpec((None, S, LANES), lambda g, hf: (g, 0, 0))],
        out_shape=[jax.ShapeDtypeStruct((DIL_G, S, DIL_OUT), BF16), jax.ShapeDtypeStruct((DIL_G, S, LANES), F32)],
        compiler_params=_cparams(("parallel", "arbitrary")),
    )(dob, ob, alpha, *deps)


def _local_step(x, tgt, h1, g_attn, g_mlp, g_final, b_pad, w_in_t, hooks):
    tables = _rope_tables()

    first = hooks.first_deps()
    proj_a = _mm(h1, w_in_t, mode="nt", tm=S, tn=512, n=3 * FOX_W, out_dtypes=[BF16], name="proj_a", deps=first)
    proj_b = _mm(h1, w_in_t, mode="nt", tm=S, tn=DIL_W, n=3 * DIL_W, b_off=SEG_B // DIL_W, out_dtypes=[F32], name="proj_b",
                deps=first)
    proj_g = _mm(h1, w_in_t, mode="nt", tm=S, tn=D, n=2 * D, b_off=SEG_G // D, out_dtypes=[F32], name="proj_g",
                deps=first)
    proj_f = _mm(h1, w_in_t, mode="nt", tm=S, tn=LANES, n=LANES, b_off=SEG_F // LANES, out_dtypes=[F32], name="proj_f",
                deps=first)

    q_aug, k_aug = _fox_prepare(proj_a, proj_f, b_pad)
    oa, lse_a = _fox_fwd(q_aug, k_aug, proj_a)

    qkv3 = _rope_split(proj_b, tables)
    o3, lse3 = _dil_fwd(qkv3)
    ob, alpha = _dil_combine(o3, lse3)

    w_a, w_b, w_out = hooks.mixer_weights(ob)
    ya = _mm(oa, w_a, mode="nn", tm=S, tn=512, out_dtypes=[F32], name="branch_a")
    yb = _mm(ob, w_b, mode="nn", tm=S, tn=512, out_dtypes=[F32], name="branch_b")
    mixed = _gate_fwd(proj_g, ya, yb)

    def out_epilogue(acc, rows, vecs):
        x2 = rows[0] + acc
        return (x2, _rms_rows(x2, vecs[0])), ()

    x2, h2 = _mm_rows(mixed, w_out, tm=ROWS_TM, name="out_proj", epilogue=out_epilogue, rows=(x,), vecs=(g_mlp,),
                      row_out=(F32, BF16))
    w_up_sh, w_down = hooks.mlp_weights(h2)

    def up_epilogue(acc):
        r = jnp.maximum(acc, 0.0)
        return acc, r * r

    u, act = _mm(h2, w_up_sh, mode="nn", tm=S, tn=DFF // N_DEV, b_sharded=True, out_dtypes=[F32, BF16],
                 name="mlp_up", epilogue=up_epilogue)

    def loss_epilogue(acc, rows, vecs):
        dx3, dg, loss = _final_loss_rows(rows[0] + acc, vecs[0], rows[1])
        return (dx3,), (dg, loss)

    dx3, dg_final, loss = _mm_rows(act, w_down, tm=ROWS_TM, name="mlp_down_loss", epilogue=loss_epilogue,
                                   rows=(x2, tgt), vecs=(g_final,), row_out=(F32,), vec_out=(D, LANES))

    def rms_bwd_epilogue(acc, rows, vecs):
        dx, dg = _rms_bwd_rows(acc, rows[0], rows[1], vecs[0])
        return (dx,), (dg,)

    du = _mm(dx3, w_down, mode="nt", tm=S, tn=512, out_dtypes=[BF16], name="mlp_down_bwd",
             epilogue=lambda acc, u_t: (acc * (2.0 * jnp.maximum(u_t, 0.0)),), extras=(u,))
    dw_down = _mm(act, dx3, mode="tn", tm=512, tn=D, out_dtypes=[F32], name="dw_down")
    dw_up_sh = _mm(h2, du, mode="tn", tm=D, tn=DFF // N_DEV, out_sharded=True, out_dtypes=[F32], name="dw_up")
    dx2, dg_mlp = _mm_rows(du, w_up_sh, b_sharded=True, tm=ROWS_TM, name="mlp_up_bwd", epilogue=rms_bwd_epilogue,
                           rows=(x2, dx3), vecs=(g_mlp,), row_out=(F32,), vec_out=(D,),
                           deps=hooks.mlp_grads(dw_up_sh, dw_down))

    dmixed = _mm(dx2, w_out, mode="nt", tm=S, tn=512, out_dtypes=[F32], name="out_proj_bwd")
    dw_out = _mm(mixed, dx2, mode="tn", tm=D, tn=D, out_dtypes=[F32], name="dw_out")

    dyab, dproj = _gate_bwd(dmixed, proj_g, ya, yb)
    doa = _mm(dyab, w_a, mode="nt", tm=S, tn=FOX_W, k=D, a_off=0, out_dtypes=[BF16], name="branch_a_bwd")
    dw_a = _mm(oa, dyab, mode="tn", tm=FOX_W, tn=D, n=D, b_off=0, out_dtypes=[F32], name="dw_branch_a")
    dob = _mm(dyab, w_b, mode="nt", tm=S, tn=CB, k=D, a_off=1, out_dtypes=[F32], name="branch_b_bwd")
    dw_b = _mm(ob, dyab, mode="tn", tm=DIL_OUT, tn=D, n=D, b_off=1, out_dtypes=[F32], name="dw_branch_b")

    dproj, dft, dfs = _fox_bwd(q_aug, k_aug, proj_a, lse_a, doa, dproj, deps=hooks.mixer_grads(dw_a, dw_b, dw_out))
    dproj, db = _fox_dscan(dft, dfs, proj_f, b_pad, dproj)

    do3, c3 = _dil_combine_bwd(dob, ob, alpha, deps=hooks.mid_backward(db))
    dq3, dk3, dv3 = _dil_bwd(qkv3, lse3, do3, c3)
    dproj = _rope_merge_bwd(dq3, dk3, dv3, tables, dproj)

    dw_in_t = _mm(dproj, h1, mode="tn", tm=CB, tn=D, out_dtypes=[F32], name="dw_in")
    dx, dg_attn = _mm_rows(dproj, w_in_t, tm=ROWS_TM // 2, name="proj_bwd", epilogue=rms_bwd_epilogue,
                           rows=(x, dx2), vecs=(g_attn,), row_out=(F32,), vec_out=(D,),
                           deps=hooks.w_in_grad(dw_in_t))

    return loss, dx, (dg_attn, db, dg_mlp, dg_final)


SHARD_SHAPES = ((D_IN // N_DEV, D), (FOX_W, D // N_DEV), (DIL_OUT, D // N_DEV), (D // N_DEV, D),
                (D, DFF // N_DEV), (DFF // N_DEV, D))
W_NAMES = ("w_in", "w_a", "w_b", "w_out", "w_up", "w_down")
AG_COPIES = 7
HBM = pl.BlockSpec(memory_space=pltpu.HBM)
SEM = pl.BlockSpec(memory_space=pltpu.SEMAPHORE)


def _place():
    return lax.axis_index("x"), lax.axis_index("y"), lax.axis_index("c")


W_SLAB = 752
W_WIN = 272
PAD_BLK = 256


def _pad_runs():
    runs = sorted((pad, pad + b - a, p, a) for p in range(N_DEV) for a, b, pad in _shard_pieces(p))
    blocks = []
    for k in range(D_PAD // PAD_BLK):
        lo, hi = k * PAD_BLK, (k + 1) * PAD_BLK
        blocks.append([(max(lo, r0) - lo, min(hi, r1) - lo, p, a + max(lo, r0) - r0)
                       for r0, r1, p, a in runs if max(lo, r0) < min(hi, r1)])
    return blocks


def _gather_w_in(shard, x_in, g_attn, late_shards):
    blocks = _pad_runs()

    norm_t = 512
    n_chunks = S // norm_t

    n_late = len(late_shards)
    late_shapes = [t.shape for t in late_shards]

    def body(x_ref, xin_ref, g_ref, *refs):
        late_in, (out_ref, h1_ref), late_out = refs[:n_late], refs[n_late:n_late + 2], refs[n_late + 2:2 * n_late + 2]
        scratch = refs[2 * n_late + 2:]
        stage, land, xbuf = scratch[:3]
        late_stage, late_cast = scratch[3:3 + n_late], scratch[3 + n_late:3 + 2 * n_late]
        send_sems, recv_sems, load_sem, x_sems, late_sems, store_sems = scratch[3 + 2 * n_late:]
        x, y, c = _place()
        me, sibling = (x, y, c), (x, y, 1 - c)
        chips = [(1 - x, y), (x, 1 - y), (1 - x, 1 - y)]

        def slot(px, py, pc):
            return land.at[4 * px + 2 * py + pc]

        def copy(k, block, to):
            return pltpu.make_async_remote_copy(
                src_ref=slot(*block), dst_ref=slot(*block), send_sem=send_sems.at[k], recv_sem=recv_sems.at[k],
                device_id=to, device_id_type=MESH)

        load = pltpu.make_async_copy(x_ref, stage, load_sem)
        load.start()
        load.wait()
        land[4 * x + 2 * y + c] = stage[...].astype(BF16)
        sent = [copy(0, me, sibling)] + [copy(1 + j, me, (*chip, c)) for j, chip in enumerate(chips)]
        for cp in sent:
            cp.start()

        def x_load(i):
            return pltpu.make_async_copy(xin_ref.at[pl.ds(i * norm_t, norm_t)], xbuf.at[i % 2], x_sems.at[i % 2])

        late_loads = [pltpu.make_async_copy(late_in[i], late_stage[i], late_sems.at[i]) for i in range(n_late)]
        for ld in late_loads:
            ld.start()
        x_load(0).start()
        for i in range(n_chunks):
            if i + 1 < n_chunks:
                x_load(i + 1).start()
            x_load(i).wait()
            h1_ref[i * norm_t:(i + 1) * norm_t, :] = _rms_rows(xbuf[i % 2], g_ref[...]).astype(BF16)
        stores = []
        for i in range(n_late):
            late_loads[i].wait()
            late_cast[i][...] = late_stage[i][...].astype(BF16)
            stores.append(pltpu.make_async_copy(late_cast[i], late_out[i].at[4 * x + 2 * y + c], store_sems.at[i]))
            stores[-1].start()

        for j, chip in enumerate(chips):
            copy(1 + j, (*chip, c), me).wait_recv()
            sent.append(copy(4 + j, (*chip, c), sibling))
            sent[-1].start()
        copy(0, sibling, me).wait_recv()
        for j, chip in enumerate(chips):
            copy(4 + j, (*chip, 1 - c), me).wait_recv()
        for cp in sent:
            cp.wait_send()
        for st in stores:
            st.wait()

        row = lax.broadcasted_iota(jnp.int32, (PAD_BLK, D), 0)
        for k, runs in enumerate(blocks):
            out = jnp.zeros((PAD_BLK, D), F32)
            for o_lo, o_hi, p, a in runs:
                start = min(a // 16 * 16, W_SLAB - W_WIN)
                win = land[p, start:start + W_WIN, :].astype(F32)
                moved = pltpu.roll(win, (o_lo - (a - start)) % W_WIN, 0)[:PAD_BLK]
                out = jnp.where((row >= o_lo) & (row < o_hi), moved, out)
            out_ref[k * PAD_BLK:(k + 1) * PAD_BLK, :] = out.astype(BF16)

    return pl.pallas_call(
        body, name="all_gather_w_in",
        out_shape=[jax.ShapeDtypeStruct((D_PAD, D), BF16), jax.ShapeDtypeStruct((S, D), BF16)]
                  + [jax.ShapeDtypeStruct((N_DEV,) + sh, BF16) for sh in late_shapes],
        in_specs=[ANY, ANY, pl.BlockSpec(memory_space=pltpu.VMEM)] + [ANY] * n_late,
        out_specs=[pl.BlockSpec(memory_space=pltpu.VMEM)] * 2 + [ANY] * n_late,
        scratch_shapes=[pltpu.VMEM((W_SLAB, D), F32), pltpu.VMEM((N_DEV, W_SLAB, D), BF16),
                        pltpu.VMEM((2, norm_t, D), F32)]
                       + [pltpu.VMEM(sh, F32) for sh in late_shapes] + [pltpu.VMEM(sh, BF16) for sh in late_shapes]
                       + [pltpu.SemaphoreType.DMA((AG_COPIES,)), pltpu.SemaphoreType.DMA((AG_COPIES,)),
                          pltpu.SemaphoreType.DMA(()), pltpu.SemaphoreType.DMA((2,)),
                          pltpu.SemaphoreType.DMA((n_late,)), pltpu.SemaphoreType.DMA((n_late,))],
        compiler_params=_cparams(None),
    )(shard, x_in, g_attn, *late_shards)


PEER_COPIES = N_DEV - 1
SIBLING_COPIES = 4
CHIP_COPIES = 3


def _peer_copies(_, land_refs, send_sems, recv_sems):
    x, y, c = _place()
    mine = 4 * x + 2 * y + c
    copies = []
    for i, ref in enumerate(land_refs):
        for k in range(1, N_DEV):
            peer = (x ^ (k >> 2), y ^ ((k >> 1) & 1), c ^ (k & 1))
            n = i * PEER_COPIES + k - 1
            copies.append(pltpu.make_async_remote_copy(
                src_ref=ref.at[mine], dst_ref=ref.at[mine], send_sem=send_sems.at[n], recv_sem=recv_sems.at[n],
                device_id=peer, device_id_type=MESH))
    return copies


def _sibling_copies(g_refs, r_refs, send_sems, recv_sems):
    x, y, c = _place()
    return [pltpu.make_async_remote_copy(
        src_ref=g_refs[i].at[2 * k + (1 - c)], dst_ref=r_refs[i].at[k],
        send_sem=send_sems.at[SIBLING_COPIES * i + k], recv_sem=recv_sems.at[SIBLING_COPIES * i + k],
        device_id=(x, y, 1 - c), device_id_type=MESH) for i in range(len(g_refs)) for k in range(SIBLING_COPIES)]


def _chip_copies(p_refs, r_refs, send_sems, recv_sems):
    x, y, c = _place()
    chips = [(1 - x, y), (x, 1 - y), (1 - x, 1 - y)]
    return [pltpu.make_async_remote_copy(
        src_ref=p_refs[i].at[2 * cx + cy], dst_ref=r_refs[i].at[j],
        send_sem=send_sems.at[CHIP_COPIES * i + j], recv_sem=recv_sems.at[CHIP_COPIES * i + j],
        device_id=(cx, cy, c), device_id_type=MESH) for i in range(len(p_refs)) for j, (cx, cy) in enumerate(chips)]


def _in_hbm(a):
    return pltpu.with_memory_space_constraint(a, pltpu.HBM)


def _exchange_start(name, copies_fn, n_copies, srcs, lands, after=()):
    n_s, n_l, n_a = len(srcs), len(lands), len(after)

    def body(*refs):
        outs = refs[n_s + n_l + n_a:]
        for cp in copies_fn(refs[:n_s], refs[n_s:n_s + n_l], outs[0], outs[1]):
            cp.start()
        outs[-1][...] = jnp.zeros_like(outs[-1])

    res = pl.pallas_call(
        body, name=name,
        out_shape=[pltpu.SemaphoreType.DMA((n_copies,)), pltpu.SemaphoreType.DMA((n_copies,))]
                  + [pltpu.HBM(t.shape, t.dtype) for t in (*srcs, *lands)] + [jax.ShapeDtypeStruct((8, LANES), F32)],
        in_specs=[HBM] * (n_s + n_l) + [ANY] * n_a,
        out_specs=[SEM, SEM] + [HBM] * (n_s + n_l) + [pl.BlockSpec(memory_space=pltpu.VMEM)],
        input_output_aliases={i: 2 + i for i in range(n_s + n_l)},
        compiler_params=pltpu.CompilerParams(has_side_effects=pltpu.SideEffectType.DATAFLOW_SIDE_EFFECTING),
    )(*[_in_hbm(t) for t in (*srcs, *lands)], *after)
    return res[0], res[1], list(res[2:2 + n_s]), list(res[2 + n_s:2 + n_s + n_l]), res[-1]


def _exchange_wait(name, copies_fn, started, after):
    send_sems, recv_sems, srcs, lands, _ = started
    n_s, n_l = len(srcs), len(lands)

    def body(*refs):
        for cp in copies_fn(refs[:n_s], refs[n_s:n_s + n_l], refs[n_s + n_l], refs[n_s + n_l + 1]):
            cp.wait_send()
            cp.wait_recv()

    res = pl.pallas_call(
        body, name=name,
        out_shape=[pltpu.HBM(t.shape, t.dtype) for t in (*srcs, *lands)],
        in_specs=[HBM] * (n_s + n_l) + [SEM, SEM, ANY],
        out_specs=[HBM] * (n_s + n_l),
        input_output_aliases={i: i for i in range(n_s + n_l)},
        compiler_params=pltpu.CompilerParams(has_side_effects=pltpu.SideEffectType.DATAFLOW_SIDE_EFFECTING),
    )(*srcs, *lands, send_sems, recv_sems, after)
    return list(res[:n_s]), list(res[n_s:])


def _shard_block(shape):
    rows, cols = shape
    if rows % 256:
        return rows, 256
    return min(rows, 512 if cols <= LANES else 256), cols


def _chip_partial(ids, g_stack, recv1, name):
    shape = g_stack.shape[1:]
    br, bc = _shard_block(shape)

    def body(ids_ref, g_ref, r_ref, pb_ref, own_ref):
        s = g_ref[...] + r_ref[...]
        pb_ref[...] = s.astype(BF16)

        @pl.when(pl.program_id(2) == ids_ref[1])
        def _():
            own_ref[...] = s

    grid_spec = pltpu.PrefetchScalarGridSpec(
        num_scalar_prefetch=1, grid=(shape[0] // br, shape[1] // bc, 4),
        in_specs=[pl.BlockSpec((None, br, bc), lambda r, q, k, ids: (2 * k + ids[0], r, q)),
                  pl.BlockSpec((None, br, bc), lambda r, q, k, ids: (k, r, q))],
        out_specs=[pl.BlockSpec((None, br, bc), lambda r, q, k, ids: (k, r, q)),
                   pl.BlockSpec((br, bc), lambda r, q, k, ids: (r, q))])
    return pl.pallas_call(
        body, name=name, grid_spec=grid_spec,
        out_shape=[jax.ShapeDtypeStruct((4,) + shape, BF16), jax.ShapeDtypeStruct(shape, F32)],
        compiler_params=_cparams(("parallel", "parallel", "arbitrary")),
    )(ids, g_stack, recv1)


def _adamw(w, g, m, v):
    m = ADAM_B1 * m + (1.0 - ADAM_B1) * g
    v = ADAM_B2 * v + (1.0 - ADAM_B2) * (g * g)
    m_hat = m / (1.0 - ADAM_B1 ** ADAM_STEP)
    v_hat = v / (1.0 - ADAM_B2 ** ADAM_STEP)
    delta = -ADAM_LR * (m_hat / (jnp.sqrt(v_hat) + ADAM_EPS) + ADAM_WD * w)
    return delta, m, v


def _reduce_adamw(own, recv2, w, m, v, name, deps=()):
    shape = own.shape
    br, bc = _shard_block(shape)

    def body(own_ref, r_ref, w_ref, m_ref, v_ref, *refs):
        g_ref, d_ref, nm_ref, nv_ref = refs[len(deps):]
        g = own_ref[...]
        for j in range(3):
            g = g + r_ref[j].astype(F32)
        delta, nm, nv = _adamw(w_ref[...], g, m_ref[...], v_ref[...])
        g_ref[...] = g
        d_ref[...] = delta
        nm_ref[...] = nm
        nv_ref[...] = nv

    blk = pl.BlockSpec((br, bc), lambda r, q: (r, q))
    return pl.pallas_call(
        body, name=name, grid=(shape[0] // br, shape[1] // bc),
        in_specs=[blk, pl.BlockSpec((3, br, bc), lambda r, q: (0, r, q)), blk, blk, blk] + [ANY] * len(deps),
        out_specs=[blk] * 4, out_shape=[jax.ShapeDtypeStruct(shape, F32)] * 4,
        compiler_params=_cparams(("parallel", "parallel")),
    )(own, recv2, w, m, v, *deps)


def _small_allreduce_adamw(gvec, w, m, v):
    def body(g_ref, w_ref, m_ref, v_ref, go_ref, d_ref, nm_ref, nv_ref, buf, send_sems, recv_sems):
        x, y, c = _place()
        my_slot = 4 * x + 2 * y + c
        buf[my_slot] = g_ref[...]
        copies = []
        for k in range(1, N_DEV):
            px, py, pc = x ^ (k >> 2), y ^ ((k >> 1) & 1), c ^ (k & 1)
            copies.append(pltpu.make_async_remote_copy(
                src_ref=g_ref, dst_ref=buf.at[my_slot], send_sem=send_sems.at[k - 1], recv_sem=recv_sems.at[k - 1],
                device_id=(px, py, pc), device_id_type=MESH))
        for cp in copies:
            cp.start()
        for k in range(1, N_DEV):
            px, py, pc = x ^ (k >> 2), y ^ ((k >> 1) & 1), c ^ (k & 1)
            pltpu.make_async_remote_copy(
                src_ref=g_ref, dst_ref=buf.at[4 * px + 2 * py + pc], send_sem=send_sems.at[k - 1],
                recv_sem=recv_sems.at[k - 1], device_id=(px, py, pc), device_id_type=MESH).wait_recv()
        for cp in copies:
            cp.wait_send()
        g = buf[0]
        for s in range(1, N_DEV):
            g = g + buf[s]
        delta, nm, nv = _adamw(w_ref[...], g, m_ref[...], v_ref[...])
        go_ref[...] = g
        d_ref[...] = delta
        nm_ref[...] = nm
        nv_ref[...] = nv

    vm = pl.BlockSpec(memory_space=pltpu.VMEM)
    return pl.pallas_call(
        body, name="small_allreduce_adamw",
        in_specs=[vm] * 4, out_specs=[vm] * 4,
        out_shape=[jax.ShapeDtypeStruct((SMALL_R, LANES), F32)] * 4,
        scratch_shapes=[pltpu.VMEM((N_DEV, SMALL_R, LANES), F32),
                        pltpu.SemaphoreType.DMA((N_DEV - 1,)), pltpu.SemaphoreType.DMA((N_DEV - 1,))],
    )(gvec, w, m, v)


def _cols_to_whole(stack):
    rows = stack.shape[1]
    return stack.transpose(1, 0, 2).reshape(rows, -1)


def _whole_to_cols(t):
    rows = t.shape[0]
    return t.reshape(rows, N_DEV, -1).transpose(1, 0, 2)


W_IN_SEGMENTS = ((0, 3 * FOX_W, SEG_A), (3 * FOX_W, 3 * FOX_W + FOX_H, SEG_F),
                 (3 * FOX_W + FOX_H, 3 * FOX_W + FOX_H + 3 * DIL_W, SEG_B), (3 * FOX_W + FOX_H + 3 * DIL_W, D_IN, SEG_G))


def _shard_pieces(p):
    rows = D_IN // N_DEV
    lo, hi = p * rows, (p + 1) * rows
    return [(max(lo, a) - lo, min(hi, b) - lo, pad + max(lo, a) - a)
            for a, b, pad in W_IN_SEGMENTS if max(lo, a) < min(hi, b)]


def _unpad_dw_in_t(dwp):
    return jnp.stack([jnp.concatenate([dwp[pad:pad + b - a] for a, b, pad in _shard_pieces(p)], axis=0)
                      for p in range(N_DEV)])


LOSS_ROW = 25


def _pack_small(g_attn, b, g_mlp, g_final, loss_row=None):
    tail = jnp.pad(b, ((0, 7), (0, LANES - b.shape[1])))
    if loss_row is not None:
        tail = tail + jnp.pad(loss_row, ((LOSS_ROW - 24, 31 - LOSS_ROW), (0, 0)))
    return jnp.concatenate([g_attn.reshape(8, LANES), g_mlp.reshape(8, LANES), g_final.reshape(8, LANES), tail], axis=0)


def _unpack_small(p):
    return (p[0:8].reshape(1, D), p[24:25, :FOX_H], p[8:16].reshape(1, D), p[16:24].reshape(D))


class _StepComm:
    def __init__(self, ids, w_sh, m_sh, v_sh, lands, after_w_in):
        self.ids = ids
        self.w_sh, self.m_sh, self.v_sh = w_sh, m_sh, v_sh
        self.updates = None
        self.gather_mixer = _exchange_start("gather_mixer_start", _peer_copies, PEER_COPIES * 3, [], lands[:3],
                                            after=(after_w_in,))
        self.gather_mlp = _exchange_start("gather_mlp_start", _peer_copies, PEER_COPIES * 2, [], lands[3:],
                                          after=(self.gather_mixer[-1],))
        self.sibling = {}
        self.chips = {}
        self.own = {}
        self.names = {}

    def _reduce_start(self, group, names, grads, after=()):
        lands = [lax.empty((SIBLING_COPIES,) + t.shape[1:], F32) for t in grads]
        self.sibling[group] = _exchange_start("grad_%s_sibling_start" % group, _sibling_copies,
                                              SIBLING_COPIES * len(grads), grads, lands, after=after)
        self.names[group] = names
        return self.sibling[group][-1]

    def _reduce_mid(self, group, after):
        grads, recv1 = _exchange_wait("grad_%s_sibling_wait" % group, _sibling_copies, self.sibling[group], after)
        parts = [_chip_partial(self.ids, g, r, "grad_partial_" + n) for g, r, n in zip(grads, recv1, self.names[group])]
        self.own[group] = [p[1] for p in parts]
        srcs = [p[0] for p in parts]
        lands = [lax.empty((CHIP_COPIES,) + t.shape[1:], BF16) for t in srcs]
        self.chips[group] = _exchange_start("grad_%s_chips_start" % group, _chip_copies, CHIP_COPIES * len(srcs),
                                            srcs, lands)
        return self.chips[group][-1]

    def reduced(self, group, after):
        _, recv2 = _exchange_wait("grad_%s_chips_wait" % group, _chip_copies, self.chips[group], after)
        return list(zip(self.own[group], recv2))

    def first_deps(self):
        return [self.gather_mlp[-1]]

    def mixer_weights(self, after):
        _, (g_a, g_b, g_out) = _exchange_wait("gather_mixer_wait", _peer_copies, self.gather_mixer, after)
        return _cols_to_whole(g_a), _cols_to_whole(g_b), g_out.reshape(D, D)

    def mlp_weights(self, after):
        _, (g_up, g_down) = _exchange_wait("gather_mlp_wait", _peer_copies, self.gather_mlp, after)
        return g_up, g_down.reshape(DFF, D)

    def mlp_grads(self, dw_up_sh, dw_down):
        return [self._reduce_start("mlp", W_NAMES[4:], [dw_up_sh, dw_down.reshape((N_DEV,) + SHARD_SHAPES[5])])]

    def mixer_grads(self, dw_a, dw_b, dw_out):
        token = self._reduce_mid("mlp", dw_b)
        grads = [_whole_to_cols(dw_a), _whole_to_cols(dw_b), dw_out.reshape((N_DEV,) + SHARD_SHAPES[3])]
        return [self._reduce_start("mixer", W_NAMES[1:4], grads, after=(token,))]

    def mid_backward(self, after):
        return [self._reduce_mid("mixer", after)]

    def w_in_grad(self, dw_in_t):
        grads = [_unpad_dw_in_t(dw_in_t)]
        token = self._reduce_start("w_in", W_NAMES[:1], grads)
        reduced = self.reduced("mixer", token) + self.reduced("mlp", token)
        self.updates, last = [None] * len(reduced), token
        for i in (3, 4, 0, 1, 2):
            if i == 0:
                last = self._reduce_mid("w_in", last)
            self.updates[i] = _reduce_adamw(*reduced[i], self.w_sh[1 + i], self.m_sh[1 + i], self.v_sh[1 + i],
                                            "adamw_" + W_NAMES[1 + i], deps=[last])
            last = self.updates[i][0]
        return [last]

    def w_in_update(self, after):
        (reduced,) = self.reduced("w_in", after)
        return _reduce_adamw(*reduced, self.w_sh[0], self.m_sh[0], self.v_sh[0], "adamw_" + W_NAMES[0])


def kernel(x, norm_attn_g, w_in, b_forget, w_branch_a, w_branch_b, w_out, norm_mlp_g, w_up, w_down, norm_final_g, loss_target, m_norm_attn_g, m_w_in, m_b_forget, m_w_branch_a, m_w_branch_b, m_w_out, m_norm_mlp_g, m_w_up, m_w_down, m_norm_final_g, v_norm_attn_g, v_w_in, v_b_forget, v_w_branch_a, v_w_branch_b, v_w_out, v_norm_mlp_g, v_w_up, v_w_down, v_norm_final_g):
    cx, cy, cc = _place()
    ids = jnp.stack([cc, 2 * cx + cy]).astype(jnp.int32)

    w_sh = [w_in[0].T] + [t[0] for t in (w_branch_a, w_branch_b, w_out, w_up, w_down)]
    m_sh = [m_w_in[0].T] + [t[0] for t in (m_w_branch_a, m_w_branch_b, m_w_out, m_w_up, m_w_down)]
    v_sh = [v_w_in[0].T] + [t[0] for t in (v_w_branch_a, v_w_branch_b, v_w_out, v_w_up, v_w_down)]

    w_in_t, h1, *lands = _gather_w_in(jnp.pad(w_sh[0], ((0, W_SLAB - w_sh[0].shape[0]), (0, 0))), x[0], norm_attn_g,
                                      w_sh[1:])
    comm = _StepComm(ids, w_sh, m_sh, v_sh, lands, w_in_t)
    b_pad = jnp.pad(b_forget, ((0, 0), (0, LANES - FOX_H)))

    loss_row, dx, dsmall = _local_step(
        x[0], loss_target[0], h1, norm_attn_g, norm_mlp_g, norm_final_g.reshape(1, D), b_pad, w_in_t, comm)

    dg_attn, db, dg_mlp, dg_final = dsmall
    small = _small_allreduce_adamw(
        _pack_small(dg_attn, db, dg_mlp, dg_final, loss_row),
        _pack_small(norm_attn_g, b_forget, norm_mlp_g, norm_final_g.reshape(1, D)),
        _pack_small(m_norm_attn_g, m_b_forget, m_norm_mlp_g, m_norm_final_g.reshape(1, D)),
        _pack_small(v_norm_attn_g, v_b_forget, v_norm_mlp_g, v_norm_final_g.reshape(1, D)))
    big = [comm.w_in_update(small[0])] + comm.updates

    outs = [small[0][LOSS_ROW, 0], dx[None]]
    for q in range(4):
        s_attn, s_b, s_mlp, s_final = _unpack_small(small[q])
        b_in, b_a, b_b, b_out, b_up, b_down = [t[q][None] for t in big]
        b_in = jnp.swapaxes(b_in, 1, 2)
        outs += [s_attn, b_in, s_b, b_a, b_b, b_out, s_mlp, b_up, b_down, s_final]
    return tuple(outs)
```

```python
import functools

import jax
import jax.numpy as jnp
from jax import lax
from jax.experimental import pallas as pl
from jax.experimental.pallas import tpu as pltpu

F32 = jnp.float32
BF16 = jnp.bfloat16
MESH = pl.DeviceIdType.MESH

S = 2048
D = 1024
HD = 64
FOX_H = 8
FOX_W = FOX_H * HD
DIL_HG = 4
DIL_G = 3
DIL_W = DIL_G * DIL_HG * HD
DIL_OUT = DIL_HG * HD
DIL_BLK = 128
DIL_R = (1, 4, 16)
DFF = 4 * D
D_IN = 3 * FOX_W + FOX_H + 3 * DIL_W + 2 * D
EPS = 1e-6
NEG_INF = -1e30
SCALE = HD ** -0.5
ROPE_THETA = 500000.0
ROPE_DIM = HD // 4
N_DEV = 8

ADAM_LR = 0.001
ADAM_B1 = 0.9
ADAM_B2 = 0.999
ADAM_EPS = 1e-08
ADAM_WD = 0.01
ADAM_STEP = 10

LANES = 128
CB = 256
SEG_A = 0
SEG_B = 3 * FOX_W
SEG_Z = SEG_B + 3 * DIL_W
SEG_G = SEG_Z + CB
SEG_F = SEG_G + 2 * D
D_PAD = SEG_F + CB
SMALL_R = 32

VMEM_MB = 56


def _cparams(dims=None, vmem_mb=VMEM_MB, **kw):
    return pltpu.CompilerParams(dimension_semantics=dims, vmem_limit_bytes=vmem_mb << 20, **kw)


ANY = pl.BlockSpec(memory_space=pl.ANY)

_NN = (((1,), (0,)), ((), ()))
_NT = (((1,), (1,)), ((), ()))
_TN = (((0,), (0,)), ((), ()))


def _dot(a, b, dims):
    return lax.dot_general(a.astype(BF16), b.astype(BF16), dims, preferred_element_type=F32)


def _mm(a, b, *, mode, tm, tn, out_dtypes, name, n=None, k=None, a_off=0, b_off=0,
        b_sharded=False, out_sharded=False, epilogue=None, extras=(), deps=()):
    n_sh = b.shape[-1] if b_sharded else None
    if mode == "nn":
        m = a.shape[0]
        k = k or a.shape[1]
        a_spec = pl.BlockSpec((tm, k), lambda i, j: (i, a_off))
        if b_sharded:
            assert tn == n_sh
            n = N_DEV * n_sh
            b_spec = pl.BlockSpec((None, k, tn), lambda i, j: (j, 0, 0))
        else:
            n = n or b.shape[1]
            b_spec = pl.BlockSpec((k, tn), lambda i, j: (0, j + b_off))
        dims = _NN
    elif mode == "nt":
        m = a.shape[0]
        k = k or a.shape[1]
        a_spec = pl.BlockSpec((tm, k), lambda i, j: (i, a_off))
        if b_sharded:
            n = b.shape[1]
            b_spec = pl.BlockSpec((N_DEV, tn, n_sh), lambda i, j: (0, j, 0))
        else:
            n = n or b.shape[0]
            b_spec = pl.BlockSpec((tn, k), lambda i, j: (j + b_off, 0))
        dims = _NT
    else:
        k, m = a.shape
        n = n or b.shape[1]
        a_spec = pl.BlockSpec((k, tm), lambda i, j: (0, i))
        b_spec = pl.BlockSpec((k, tn), lambda i, j: (0, j + b_off))
        dims = _TN
    assert m % tm == 0 and n % tn == 0, (name, m, n, tm, tn)
    n_extra = len(extras)
    tile = pl.BlockSpec((tm, tn), lambda i, j: (i, j))
    if out_sharded:
        assert mode == "tn" and n // tn == N_DEV
        out_spec = pl.BlockSpec((None, tm, tn), lambda i, j: (j, i, 0))
        out_shape = (N_DEV, m, tn)
    else:
        out_spec, out_shape = tile, (m, n)

    def body(a_ref, b_ref, *refs):
        if mode == "nt" and b_sharded:
            acc = _dot(a_ref[:, 0:n_sh], b_ref[0], dims)
            for p in range(1, N_DEV):
                acc = acc + _dot(a_ref[:, p * n_sh:(p + 1) * n_sh], b_ref[p], dims)
        else:
            acc = _dot(a_ref[...], b_ref[...], dims)
        ex = [r[...] for r in refs[:n_extra]]
        outs = epilogue(acc, *ex) if epilogue is not None else (acc,)
        for o_ref, o in zip(refs[n_extra + len(deps):], outs):
            o_ref[...] = o.astype(o_ref.dtype)

    res = pl.pallas_call(
        body, name=name, grid=(m // tm, n // tn),
        in_specs=[a_spec, b_spec] + [tile] * n_extra + [ANY] * len(deps),
        out_specs=[out_spec] * len(out_dtypes),
        out_shape=[jax.ShapeDtypeStruct(out_shape, dt) for dt in out_dtypes],
        compiler_params=_cparams(("parallel", "parallel")),
    )(a, b, *extras, *deps)
    return res if len(out_dtypes) > 1 else res[0]


def _mm_rows(a, b, *, tm, name, epilogue, rows=(), vecs=(), row_out=(), vec_out=(), b_sharded=False, deps=()):
    m, k = a.shape
    n_rows, n_vecs, n_deps = len(rows), len(vecs), len(deps)
    n_sh = b.shape[-1] if b_sharded else None

    def body(a_ref, b_ref, *refs):
        row_refs, vec_refs = refs[:n_rows], refs[n_rows:n_rows + n_vecs]
        outs = refs[n_rows + n_vecs + n_deps:]
        if b_sharded:
            acc = _dot(a_ref[:, 0:n_sh], b_ref[0], _NT)
            for p in range(1, N_DEV):
                acc = acc + _dot(a_ref[:, p * n_sh:(p + 1) * n_sh], b_ref[p], _NT)
        else:
            acc = _dot(a_ref[...], b_ref[...], _NN)
        row_vals, vec_incs = epilogue(acc, [r[...] for r in row_refs], [v[...] for v in vec_refs])
        for o_ref, val in zip(outs[:len(row_out)], row_vals):
            o_ref[...] = val.astype(o_ref.dtype)

        @pl.when(pl.program_id(0) == 0)
        def _():
            for o_ref in outs[len(row_out):]:
                o_ref[...] = jnp.zeros_like(o_ref)

        for o_ref, inc in zip(outs[len(row_out):], vec_incs):
            o_ref[...] += inc

    tile = pl.BlockSpec((tm, D), lambda i: (i, 0))
    b_spec = pl.BlockSpec(b.shape, lambda i: (0,) * b.ndim)
    return pl.pallas_call(
        body, name=name, grid=(m // tm,),
        in_specs=[pl.BlockSpec((tm, k), lambda i: (i, 0)), b_spec] + [tile] * n_rows
                 + [pl.BlockSpec(v.shape, lambda i: (0, 0)) for v in vecs] + [ANY] * n_deps,
        out_specs=[tile] * len(row_out) + [pl.BlockSpec((1, w), lambda i: (0, 0)) for w in vec_out],
        out_shape=[jax.ShapeDtypeStruct((m, D), dt) for dt in row_out]
                  + [jax.ShapeDtypeStruct((1, w), F32) for w in vec_out],
        compiler_params=_cparams(("arbitrary",)),
    )(a, b, *rows, *vecs, *deps)


ROW_T = 256
ROWS_TM = 512


def _rms_rows(x, g):
    r = lax.rsqrt(jnp.mean(x * x, axis=-1, keepdims=True) + EPS)
    return (x * r) * g


def _rms_bwd_rows(dh, x, dres, g):
    r = lax.rsqrt(jnp.mean(x * x, axis=-1, keepdims=True) + EPS)
    xn = x * r
    dhn = dh * g
    dx = dres + r * (dhn - xn * jnp.mean(dhn * xn, axis=-1, keepdims=True))
    return dx, jnp.sum(dh * xn, axis=0, keepdims=True)


def _final_loss_rows(x, g, tgt):
    r = lax.rsqrt(jnp.mean(x * x, axis=-1, keepdims=True) + EPS)
    xn = x * r
    err = xn * g - tgt
    row_loss = jnp.mean(err * err, axis=-1, keepdims=True)
    loss = 0.5 * jnp.sum(row_loss, axis=0, keepdims=True) * jnp.ones((1, LANES), F32)
    dy = err * (1.0 / D)
    dyn = dy * g
    dx = r * (dyn - xn * jnp.mean(dyn * xn, axis=-1, keepdims=True))
    return dx, jnp.sum(dy * xn, axis=0, keepdims=True), loss


def _sigmoid(z):
    return 1.0 / (1.0 + jnp.exp(-z))


def _gate_fwd(proj_g, ya, yb):
    def body(ga_ref, gb_ref, ya_ref, yb_ref, o_ref):
        o_ref[...] = (_sigmoid(ga_ref[...]) * ya_ref[...] + _sigmoid(gb_ref[...]) * yb_ref[...]).astype(BF16)

    row = pl.BlockSpec((ROW_T, D), lambda i: (i, 0))
    return pl.pallas_call(
        body, name="gate_fwd", grid=(S // ROW_T,),
        in_specs=[row, pl.BlockSpec((ROW_T, D), lambda i: (i, 1)), row, row],
        out_specs=row, out_shape=jax.ShapeDtypeStruct((S, D), BF16),
        compiler_params=_cparams(("parallel",)),
    )(proj_g, proj_g, ya, yb)


GATE_TR = 512


def _out_proj_gate_bwd(dx2, w_out, proj_g, ya, yb):
    def body(dx_ref, w_ref, g_ref, ya_ref, yb_ref, dy_ref, dg_ref):
        dm = _dot(dx_ref[...], w_ref[...], _NT)
        for half, y_ref in enumerate((ya_ref, yb_ref)):
            cols = slice(half * D, (half + 1) * D)
            s = _sigmoid(g_ref[:, cols])
            dy_ref[:, cols] = (dm * s).astype(BF16)
            dg_ref[:, cols] = (dm * y_ref[...] * (s * (1.0 - s))).astype(BF16)

    row = pl.BlockSpec((GATE_TR, D), lambda i: (i, 0))
    wide = pl.BlockSpec((GATE_TR, 2 * D), lambda i: (i, 0))
    return pl.pallas_call(
        body, name="out_proj_gate_bwd", grid=(S // GATE_TR,),
        in_specs=[row, pl.BlockSpec((D, D), lambda i: (0, 0)), wide, row, row],
        out_specs=[wide, pl.BlockSpec((GATE_TR, 2 * D), lambda i: (i, SEG_G // (2 * D)))],
        out_shape=[jax.ShapeDtypeStruct((S, 2 * D), BF16), jax.ShapeDtypeStruct((S, D_PAD), BF16)],
        compiler_params=_cparams(("parallel",)),
    )(dx2, w_out, proj_g, ya, yb)


FOX_TQ = 256


def _scan_rows(x, reverse):
    n = x.shape[0]
    row = lax.broadcasted_iota(jnp.int32, x.shape, 0)
    k = 1
    while k < n:
        if reverse:
            x = x + jnp.where(row < n - k, pltpu.roll(x, n - k, 0), 0.0)
        else:
            x = x + jnp.where(row >= k, pltpu.roll(x, k, 0), 0.0)
        k *= 2
    return x


def _fox_dscan(dft, dfs, proj_f, b_pad, dproj):
    def body(dft_ref, dfs_ref, f_ref, b_ref, _, dfa_ref, db_ref):
        dfs_pad = jnp.concatenate([dfs_ref[...], jnp.zeros((LANES - FOX_H, S), F32)], axis=0)
        df = dfs_pad.T + dft_ref[0]
        for hp in range(1, dft_ref.shape[0]):
            df = df + pltpu.roll(dft_ref[hp], 2 * hp, 1)
        dlf = _scan_rows(df, reverse=True)
        z = f_ref[...] + b_ref[...]
        lane = lax.broadcasted_iota(jnp.int32, (S, LANES), 1)
        dfa = jnp.where(lane < FOX_H, dlf / (1.0 + jnp.exp(z)), 0.0)
        dfa_ref[:, :LANES] = dfa.astype(BF16)
        dfa_ref[:, LANES:] = jnp.zeros((S, CB - LANES), BF16)
        db_ref[...] = jnp.sum(dfa, axis=0, keepdims=True)

    return pl.pallas_call(
        body, name="fox_dscan", grid=(1,),
        in_specs=[pl.BlockSpec(dft.shape, lambda i: (0, 0, 0)), pl.BlockSpec((FOX_H, S), lambda i: (0, 0)),
                  pl.BlockSpec((S, LANES), lambda i: (0, 0)), pl.BlockSpec((1, LANES), lambda i: (0, 0)), ANY],
        out_specs=[pl.BlockSpec((S, CB), lambda i: (0, SEG_F // CB)), pl.BlockSpec((1, LANES), lambda i: (0, 0))],
        out_shape=[jax.ShapeDtypeStruct((S, D_PAD), BF16), jax.ShapeDtypeStruct((1, LANES), F32)],
        input_output_aliases={4: 0},
        compiler_params=_cparams(("arbitrary",)),
    )(dft, dfs, proj_f, b_pad, dproj)


FOX_NQ = S // FOX_TQ
FOX_HP = FOX_W // LANES
HEADS_PER_LB = LANES // HD
FOX_AUG = 2 * LANES


def _fox_prepare(proj_a, proj_f, b_pad):
    tr = 512

    def body(q_ref, k_ref, f_ref, b_ref, qa_ref, ka_ref, hi_s, mid_s, lo_s):
        i = pl.program_id(0)

        @pl.when(i == 0)
        def _():
            z = f_ref[...] + b_ref[...]
            lf = jnp.minimum(z, 0.0) - jnp.log1p(jnp.exp(-jnp.abs(z)))
            f = _scan_rows(lf, reverse=False)
            hi = f.astype(BF16).astype(F32)
            r1 = f - hi
            mid = r1.astype(BF16).astype(F32)
            hi_s[...] = hi
            mid_s[...] = mid
            lo_s[...] = (r1 - mid).astype(BF16).astype(F32)

        rows = pl.ds(pl.multiple_of(i * tr, tr), tr)
        hi, mid, lo = hi_s[rows, :], mid_s[rows, :], lo_s[rows, :]
        lane = lax.broadcasted_iota(jnp.int32, (tr, HD), 1)
        q_ones = jnp.where((lane >= 3) & (lane < 6), 1.0, 0.0)
        k_ones = jnp.where(lane < 3, 1.0, 0.0)
        for h in range(FOX_H):
            a1, a2, a3 = hi[:, h:h + 1], mid[:, h:h + 1], lo[:, h:h + 1]
            q_extra = jnp.where(lane == 0, a1, jnp.where(lane == 1, a2, jnp.where(lane == 2, a3, q_ones)))
            k_extra = jnp.where(lane == 3, -a1, jnp.where(lane == 4, -a2, jnp.where(lane == 5, -a3, k_ones)))
            base = h * LANES
            qa_ref[:, base:base + HD] = q_ref[:, h * HD:(h + 1) * HD] * jnp.asarray(SCALE, BF16)
            qa_ref[:, base + HD:base + LANES] = q_extra.astype(BF16)
            ka_ref[:, base:base + HD] = k_ref[:, h * HD:(h + 1) * HD]
            ka_ref[:, base + HD:base + LANES] = k_extra.astype(BF16)

    out_blk = pl.BlockSpec((tr, FOX_H * LANES), lambda i: (i, 0))
    return pl.pallas_call(
        body, name="fox_prepare", grid=(S // tr,),
        in_specs=[pl.BlockSpec((tr, FOX_W), lambda i: (i, 0)), pl.BlockSpec((tr, FOX_W), lambda i: (i, 1)),
                  pl.BlockSpec((S, LANES), lambda i: (0, 0)), pl.BlockSpec((1, LANES), lambda i: (0, 0))],
        out_specs=[out_blk, out_blk],
        out_shape=[jax.ShapeDtypeStruct((S, FOX_H * LANES), BF16)] * 2,
        scratch_shapes=[pltpu.VMEM((S, LANES), F32)] * 3,
        compiler_params=_cparams(("arbitrary",)),
    )(proj_a, proj_a, proj_f, b_pad)


def _fox_scores(qa, ka, w):
    s = _dot(qa, ka, _NT)
    row = lax.broadcasted_iota(jnp.int32, (FOX_TQ, FOX_TQ), 0)
    col = lax.broadcasted_iota(jnp.int32, (FOX_TQ, FOX_TQ), 1)
    diag = jnp.where(col <= row, s[:, w * FOX_TQ:], NEG_INF)
    return diag if w == 0 else jnp.concatenate([s[:, :w * FOX_TQ], diag], axis=1)


def _fox_fwd(q_aug, k_aug, proj_a):
    def body(qa_ref, ka_ref, v_ref, o_ref, lse_ref):
        qi = pl.program_id(1)
        lane = lax.broadcasted_iota(jnp.int32, (FOX_TQ, LANES), 1)
        for w in range(FOX_NQ):
            @pl.when(qi == w)
            def _(w=w):
                width = (w + 1) * FOX_TQ
                lse = jnp.zeros((FOX_TQ, LANES), F32)
                for hh in range(HEADS_PER_LB):
                    aug = slice(hh * LANES, (hh + 1) * LANES)
                    sl = slice(hh * HD, (hh + 1) * HD)
                    s = _fox_scores(qa_ref[:, aug], ka_ref[:width, aug], w)
                    m = jnp.max(s, axis=-1, keepdims=True)
                    p = jnp.exp(s - m)
                    l = jnp.sum(p, axis=-1, keepdims=True)
                    o_ref[:, sl] = _dot(p / l, v_ref[:width, sl], _NN)
                    lse = jnp.where(lane == hh, m + jnp.log(l), lse)
                lse_ref[...] = lse

    return pl.pallas_call(
        body, name="fox_fwd", grid=(FOX_HP, FOX_NQ),
        in_specs=[pl.BlockSpec((FOX_TQ, FOX_AUG), lambda h, i: (i, h)),
                  pl.BlockSpec((S, FOX_AUG), lambda h, i: (0, h)),
                  pl.BlockSpec((S, LANES), lambda h, i: (0, 2 * FOX_HP + h))],
        out_specs=[pl.BlockSpec((FOX_TQ, LANES), lambda h, i: (i, h)),
                   pl.BlockSpec((None, FOX_TQ, LANES), lambda h, i: (h, i, 0))],
        out_shape=[jax.ShapeDtypeStruct((S, FOX_W), F32), jax.ShapeDtypeStruct((FOX_HP, S, LANES), F32)],
        compiler_params=_cparams(("parallel", "parallel")),
    )(q_aug, k_aug, proj_a)


def _fox_bwd(q_aug, k_aug, proj_a, lse, do, dproj, deps=()):
    n_q = FOX_NQ

    def body(qa_ref, ka_ref, v_ref, lse_ref, do_ref, *refs):
        dproj_ref, dft_ref, dfs_ref, dk_acc, dv_acc, dq_buf, kv_buf, dq_sems, kv_sems = refs[1 + len(deps):]
        hp = pl.program_id(0)
        t = pl.program_id(1)
        slot = t % 2
        col = pl.multiple_of(hp * LANES, LANES)

        def dq_copy(step, buf_slot):
            rows = pl.ds(pl.multiple_of(step * FOX_TQ, FOX_TQ), FOX_TQ)
            return pltpu.make_async_copy(dq_buf.at[buf_slot], dproj_ref.at[rows, pl.ds(col, LANES)], dq_sems.at[buf_slot])

        @pl.when(t == 0)
        def _():
            dk_acc[...] = jnp.zeros_like(dk_acc)
            dv_acc[...] = jnp.zeros_like(dv_acc)

        @pl.when((t == 0) & (hp == 0))
        def _():
            dfs_ref[...] = jnp.zeros_like(dfs_ref)

        @pl.when(t >= 2)
        def _():
            dq_copy(t - 2, slot).wait()

        lane = lax.broadcasted_iota(jnp.int32, (FOX_TQ, LANES), 1)
        for w in range(n_q):
            @pl.when(t == w)
            def _(w=w):
                width = (w + 1) * FOX_TQ
                lse_all = lse_ref[...]
                sub = lax.broadcasted_iota(jnp.int32, (FOX_H, width), 0)
                dft = jnp.zeros((FOX_TQ, LANES), F32)
                for hh in range(HEADS_PER_LB):
                    aug = slice(hh * LANES, (hh + 1) * LANES)
                    sl = slice(hh * HD, (hh + 1) * HD)
                    head = hp * HEADS_PER_LB + hh
                    qa = qa_ref[:, aug]
                    ka = ka_ref[:width, aug]
                    do_h = do_ref[:, sl]
                    s = _fox_scores(qa, ka, w)
                    p = jnp.exp(s - lse_all[:, hh:hh + 1])
                    dp = _dot(do_h, v_ref[:width, sl], _NT)
                    ds = p * (dp - jnp.sum(dp * p, axis=-1, keepdims=True))
                    dft = jnp.where(lane == hh, jnp.sum(ds, axis=-1, keepdims=True), dft)
                    dfs_ref[:, :width] -= jnp.where(sub == head, jnp.sum(ds, axis=0, keepdims=True), 0.0)
                    dq_buf[slot, :, sl] = (_dot(ds, ka[:, :HD], _NN) * SCALE).astype(BF16)
                    dk_acc[:width, sl] += _dot(ds, qa[:, :HD], _TN)
                    dv_acc[:width, sl] += _dot(p, do_h, _TN)
                dft_ref[...] = dft

        dq_copy(t, slot).start()

        @pl.when(t == n_q - 1)
        def _():
            dq_copy(t - 1, 1 - slot).wait()
            dq_copy(t, slot).wait()
            kv_buf[0] = dk_acc[...].astype(BF16)
            kv_buf[1] = dv_acc[...].astype(BF16)
            copies = [pltpu.make_async_copy(
                kv_buf.at[j], dproj_ref.at[:, pl.ds(pl.multiple_of((j + 1) * FOX_W + hp * LANES, LANES), LANES)],
                kv_sems.at[j]) for j in range(2)]
            for cp in copies:
                cp.start()
            for cp in copies:
                cp.wait()

    qblk = pl.BlockSpec((FOX_TQ, FOX_AUG), lambda h, t: (t, h))
    lane_blk = pl.BlockSpec((None, FOX_TQ, LANES), lambda h, t: (h, t, 0))
    return pl.pallas_call(
        body, name="fox_bwd", grid=(FOX_HP, n_q),
        in_specs=[qblk, pl.BlockSpec((S, FOX_AUG), lambda h, t: (0, h)),
                  pl.BlockSpec((S, LANES), lambda h, t: (0, 2 * FOX_HP + h)),
                  lane_blk, pl.BlockSpec((FOX_TQ, LANES), lambda h, t: (t, h)), ANY] + [ANY] * len(deps),
        out_specs=[ANY, lane_blk, pl.BlockSpec((FOX_H, S), lambda h, t: (0, 0))],
        out_shape=[jax.ShapeDtypeStruct((S, D_PAD), BF16),
                   jax.ShapeDtypeStruct((FOX_HP, S, LANES), F32), jax.ShapeDtypeStruct((FOX_H, S), F32)],
        scratch_shapes=[pltpu.VMEM((S, LANES), F32), pltpu.VMEM((S, LANES), F32),
                        pltpu.VMEM((2, FOX_TQ, LANES), BF16), pltpu.VMEM((2, S, LANES), BF16),
                        pltpu.SemaphoreType.DMA((2,)), pltpu.SemaphoreType.DMA((2,))],
        input_output_aliases={5: 0},
        compiler_params=_cparams(("arbitrary", "arbitrary")),
    )(q_aug, k_aug, proj_a, lse, do, dproj, *deps)


def _rope_tables():
    half = ROPE_DIM // 2
    shape = (DIL_G, S, half)
    g = lax.broadcasted_iota(jnp.int32, shape, 0)
    row = lax.broadcasted_iota(jnp.int32, shape, 1)
    r = jnp.left_shift(1, 2 * g)
    per_class = S // r
    pos = (row % per_class) * r + row // per_class
    inv_freq = jnp.power(jnp.float32(ROPE_THETA), -lax.broadcasted_iota(F32, shape, 2) * 2.0 / ROPE_DIM)
    ang = pos.astype(F32) * inv_freq
    return jnp.concatenate([jnp.cos(ang), jnp.sin(ang)], axis=-1)


def _rope_factors(cs):
    half = ROPE_DIM // 2
    i = lax.broadcasted_iota(jnp.int32, (2 * half, 2 * LANES), 0)
    l = lax.broadcasted_iota(jnp.int32, (2 * half, 2 * LANES), 1)
    hit = ((l < LANES) & (i == l % half)) | ((l >= LANES) & (i - half == l % half))
    spread = jnp.where(hit, 1.0, 0.0).astype(BF16)
    hi = cs.astype(BF16)
    r1 = cs - hi.astype(F32)
    mid = r1.astype(BF16)
    lo = (r1 - mid.astype(F32)).astype(BF16)
    full = (_dot(hi, spread, _NN) + _dot(mid, spread, _NN)) + _dot(lo, spread, _NN)
    cos, sin = full[:, :LANES], full[:, LANES:]
    lane = lax.broadcasted_iota(jnp.int32, cos.shape, 1) % HD
    t_self = jnp.where(lane < ROPE_DIM, cos, 1.0)
    t_up = jnp.where(lane < half, -sin, 0.0)
    t_dn = jnp.where((lane >= half) & (lane < ROPE_DIM), sin, 0.0)
    return t_self, t_up, t_dn


def _residue_pieces(r):
    if r == 1:
        return [(slice(i, i + 512), slice(i, i + 512)) for i in range(0, S, 512)]
    n = S // r
    return [(pl.ds(j, n, stride=r), slice(j * n, (j + 1) * n)) for j in range(r)]


def _rope_apply(x, a, u, d, backward):
    half = ROPE_DIM // 2
    if backward:
        return x * a + pltpu.roll(x * u, half, 1) + pltpu.roll(x * d, LANES - half, 1)
    return x * a + pltpu.roll(x, LANES - half, 1) * u + pltpu.roll(x, half, 1) * d


def _rope_cache_factors(cs_ref, fac, first_step_of_group):
    @pl.when(first_step_of_group)
    def _():
        for lo in range(0, S, 512):
            for j, f in enumerate(_rope_factors(cs_ref[lo:lo + 512, :])):
                fac[j, lo:lo + 512, :] = f


LB_PER_CB = CB // LANES
ROPE_NB = 3 * DIL_G * LB_PER_CB


def _rope_step(s):
    per_group = 3 * LB_PER_CB
    return s // per_group, (s % per_group) // LB_PER_CB, s % LB_PER_CB


def _rope_split(proj_b, tables):
    def body(x_ref, cs_ref, o_ref, fac):
        g, t, hf = _rope_step(pl.program_id(0))
        _rope_cache_factors(cs_ref, fac, (t == 0) & (hf == 0))
        for gs in range(DIL_G):
            @pl.when(g == gs)
            def _(gs=gs):
                for tok, sub in _residue_pieces(DIL_R[gs]):
                    x = x_ref[tok, :]
                    y = _rope_apply(x, fac[0, sub, :], fac[1, sub, :], fac[2, sub, :], False)
                    o_ref[sub, :] = jnp.where(t < 2, y, x).astype(BF16)

    def in_index(s):
        g, t, hf = _rope_step(s)
        return 0, (t * DIL_G + g) * LB_PER_CB + hf

    def out_index(s):
        g, t, hf = _rope_step(s)
        return t * DIL_G + g, 0, hf

    tab = pl.BlockSpec((None, S, 2 * (ROPE_DIM // 2)), lambda s: (_rope_step(s)[0], 0, 0))
    return pl.pallas_call(
        body, name="rope_split", grid=(ROPE_NB,),
        in_specs=[pl.BlockSpec((S, LANES), in_index), tab],
        out_specs=pl.BlockSpec((None, S, LANES), out_index),
        out_shape=jax.ShapeDtypeStruct((3 * DIL_G, S, CB), BF16),
        scratch_shapes=[pltpu.VMEM((3, S, LANES), F32)],
        compiler_params=_cparams(("arbitrary",)),
    )(proj_b, tables)


def _rope_merge_bwd(dq3, dk3, dv3, tables, dproj):
    def body(dq_ref, dk_ref, dv_ref, cs_ref, _, o_ref, tmp, fac):
        s = pl.program_id(0)
        g, t, hf = _rope_step(s)
        _rope_cache_factors(cs_ref, fac, (t == 0) & (hf == 0) & (s < ROPE_NB))
        for gs in range(DIL_G):
            @pl.when((g == gs) & (s < ROPE_NB))
            def _(gs=gs):
                for tok, sub in _residue_pieces(DIL_R[gs]):
                    x = jnp.where(t == 0, dq_ref[sub, :], jnp.where(t == 1, dk_ref[sub, :], dv_ref[sub, :]))
                    y = _rope_apply(x, fac[0, sub, :], fac[1, sub, :], fac[2, sub, :], True)
                    tmp[tok, :] = jnp.where(t < 2, y, x)
                o_ref[...] = tmp[...].astype(BF16)

        @pl.when(s >= ROPE_NB)
        def _():
            o_ref[...] = jnp.zeros_like(o_ref)

    def src_spec(own):
        def index(s):
            g, t, hf = _rope_step(jnp.minimum(s, ROPE_NB - 1))
            return g, 0, jnp.where(t == own, hf, jnp.where(t > own, LB_PER_CB - 1, 0))
        return pl.BlockSpec((None, S, LANES), index)

    def out_index(s):
        g, t, hf = _rope_step(s)
        col = SEG_B // LANES + (t * DIL_G + g) * LB_PER_CB + hf
        return 0, jnp.where(s < ROPE_NB, col, SEG_Z // LANES + s - ROPE_NB)

    tab = pl.BlockSpec((None, S, 2 * (ROPE_DIM // 2)), lambda s: (_rope_step(jnp.minimum(s, ROPE_NB - 1))[0], 0, 0))
    return pl.pallas_call(
        body, name="rope_merge_bwd", grid=(ROPE_NB + LB_PER_CB,),
        in_specs=[src_spec(0), src_spec(1), src_spec(2), tab, ANY],
        out_specs=pl.BlockSpec((S, LANES), out_index),
        out_shape=jax.ShapeDtypeStruct((S, D_PAD), BF16),
        scratch_shapes=[pltpu.VMEM((S, LANES), F32), pltpu.VMEM((3, S, LANES), F32)],
        input_output_aliases={4: 0},
        compiler_params=_cparams(("arbitrary",)),
    )(dq3, dk3, dv3, tables, dproj)


DIL_NB = S // DIL_BLK


_BQK = (((2,), (2,)), ((0,), (0,)))
_BQD = (((2,), (1,)), ((0,), (0,)))
_BKD = (((1,), (1,)), ((0,), (0,)))


def _dil_mask(g):
    shape = (DIL_NB, DIL_BLK, 2 * DIL_BLK)
    blocks_per_seq = jnp.right_shift(DIL_NB, 2 * g)
    has_prev = jnp.bitwise_and(lax.broadcasted_iota(jnp.int32, shape, 0), blocks_per_seq - 1) != 0
    a = lax.broadcasted_iota(jnp.int32, shape, 1)
    kk = lax.broadcasted_iota(jnp.int32, shape, 2)
    diff = DIL_BLK + a - kk
    return (diff >= 0) & (diff <= DIL_BLK) & ((kk >= DIL_BLK) | has_prev)


def _blocks(t):
    return t.reshape(DIL_NB, DIL_BLK, t.shape[-1])


def _with_prev(t):
    prev = jnp.concatenate([jnp.zeros((DIL_BLK, t.shape[-1]), t.dtype), t[:-DIL_BLK]], axis=0)
    return jnp.concatenate([_blocks(prev), _blocks(t)], axis=1)


def _fold_prev(t2):
    n = t2.shape[-1]
    to_prev = t2[:, :DIL_BLK].reshape(S, n)
    own = t2[:, DIL_BLK:].reshape(S, n)
    return own + jnp.concatenate([to_prev[DIL_BLK:], jnp.zeros((DIL_BLK, n), t2.dtype)], axis=0)


def _dil_group_spec(t, width=DIL_OUT):
    return pl.BlockSpec((None, S, width), lambda g: (t * DIL_G + g, 0, 0))


def _dil_fwd(qkv3):
    def body(q_ref, k_ref, v_ref, o_ref, lse_ref):
        ok = _dil_mask(pl.program_id(0))
        lane = lax.broadcasted_iota(jnp.int32, (S, LANES), 1)
        lse = jnp.zeros((S, LANES), F32)
        for h in range(DIL_HG):
            sl = slice(h * HD, (h + 1) * HD)
            s = jnp.where(ok, _dot(_blocks(q_ref[:, sl]), _with_prev(k_ref[:, sl]), _BQK) * SCALE, NEG_INF)
            m = jnp.max(s, axis=-1, keepdims=True)
            p = jnp.exp(s - m)
            l = jnp.sum(p, axis=-1, keepdims=True)
            o_ref[:, sl] = _dot(p / l, _with_prev(v_ref[:, sl]), _BQD).reshape(S, HD)
            lse = jnp.where(lane == h, (m + jnp.log(l)).reshape(S, 1), lse)
        lse_ref[...] = lse

    return pl.pallas_call(
        body, name="dil_fwd", grid=(DIL_G,),
        in_specs=[_dil_group_spec(0), _dil_group_spec(1), _dil_group_spec(2)],
        out_specs=[_dil_group_spec(0), _dil_group_spec(0, LANES)],
        out_shape=[jax.ShapeDtypeStruct((DIL_G, S, DIL_OUT), F32), jax.ShapeDtypeStruct((DIL_G, S, LANES), F32)],
        compiler_params=_cparams(("parallel",)),
    )(qkv3, qkv3, qkv3)


def _dil_bwd(qkv3, lse3, do3, c3):
    def body(q_ref, k_ref, v_ref, lse_ref, do_ref, c_ref, dq_ref, dk_ref, dv_ref):
        ok = _dil_mask(pl.program_id(0))
        lse = lse_ref[...]
        c = c_ref[...]
        for h in range(DIL_HG):
            sl = slice(h * HD, (h + 1) * HD)
            q = _blocks(q_ref[:, sl])
            k2 = _with_prev(k_ref[:, sl])
            do_h = _blocks(do_ref[:, sl])
            s = jnp.where(ok, _dot(q, k2, _BQK) * SCALE, NEG_INF)
            p = jnp.exp(s - _blocks(lse[:, h:h + 1]))
            dp = _dot(do_h, _with_prev(v_ref[:, sl]), _BQK)
            ds = p * (dp - _blocks(c[:, h:h + 1]))
            dsb = (ds * SCALE).astype(BF16)
            dq_ref[:, sl] = _dot(dsb, k2, _BQD).reshape(S, HD)
            dk_ref[:, sl] = _fold_prev(_dot(dsb, q, _BKD))
            dv_ref[:, sl] = _fold_prev(_dot(p, do_h, _BKD))

    return pl.pallas_call(
        body, name="dil_bwd", grid=(DIL_G,),
        in_specs=[_dil_group_spec(0), _dil_group_spec(1), _dil_group_spec(2), _dil_group_spec(0, LANES),
                  _dil_group_spec(0), _dil_group_spec(0, LANES)],
        out_specs=[_dil_group_spec(0)] * 3,
        out_shape=[jax.ShapeDtypeStruct((DIL_G, S, DIL_OUT), F32)] * 3,
        compiler_params=_cparams(("parallel",)),
    )(qkv3, qkv3, qkv3, lse3, do3, c3)


COMB_T = 256


def _dil_combine(o3, lse3):
    heads_per_lb = LANES // HD

    def body(o_ref, lse_ref, ob_ref, al_ref, *scratch):
        o_tok = [scratch[LB_PER_CB * gg:LB_PER_CB * (gg + 1)] for gg in range(DIL_G)]
        lse_tok = scratch[LB_PER_CB * DIL_G:]
        g = pl.program_id(0)
        for gs in range(DIL_G):
            @pl.when(g == gs)
            def _(gs=gs):
                for tok, sub in _residue_pieces(DIL_R[gs]):
                    for hf in range(LB_PER_CB):
                        o_tok[gs][hf][tok, :] = o_ref[sub, hf * LANES:(hf + 1) * LANES]
                    lse_tok[gs][tok, :] = lse_ref[sub, :]

        @pl.when(g == DIL_G - 1)
        def _():
            def chunk(i, carry):
                rows = pl.ds(pl.multiple_of(i * COMB_T, COMB_T), COMB_T)
                lse = [lse_tok[gg][rows, :] for gg in range(DIL_G)]
                m = jnp.maximum(jnp.maximum(lse[0], lse[1]), lse[2])
                e = [jnp.exp(lse[gg] - m) for gg in range(DIL_G)]
                den = (e[0] + e[1]) + e[2]
                al = [e[gg] / den for gg in range(DIL_G)]
                for gg in range(DIL_G):
                    al_ref[gg, rows, :] = al[gg]
                for h in range(DIL_HG):
                    hf, sl = h // heads_per_lb, slice((h % heads_per_lb) * HD, (h % heads_per_lb + 1) * HD)
                    acc = al[0][:, h:h + 1] * o_tok[0][hf][rows, sl]
                    for gg in range(1, DIL_G):
                        acc = acc + al[gg][:, h:h + 1] * o_tok[gg][hf][rows, sl]
                    ob_ref[rows, h * HD:(h + 1) * HD] = acc
                return carry

            lax.fori_loop(0, S // COMB_T, chunk, 0)

    return pl.pallas_call(
        body, name="dil_combine", grid=(DIL_G,),
        in_specs=[pl.BlockSpec((None, S, DIL_OUT), lambda g: (g, 0, 0)),
                  pl.BlockSpec((None, S, LANES), lambda g: (g, 0, 0))],
        out_specs=[pl.BlockSpec((S, DIL_OUT), lambda g: (0, 0)),
                   pl.BlockSpec((DIL_G, S, LANES), lambda g: (0, 0, 0))],
        out_shape=[jax.ShapeDtypeStruct((S, DIL_OUT), F32), jax.ShapeDtypeStruct((DIL_G, S, LANES), F32)],
        scratch_shapes=[pltpu.VMEM((S, LANES), F32)] * (DIL_G * (LB_PER_CB + 1)),
        compiler_params=_cparams(("arbitrary",)),
    )(o3, lse3)


def _dil_combine_bwd(dob, ob, alpha, deps=()):
    heads_per_lb = LANES // HD

    def body(dob_ref, ob_ref, al_ref, *refs):
        do_ref, c_ref = refs[len(deps):]
        g = pl.program_id(0)
        hf = pl.program_id(1)
        for gs in range(DIL_G):
            @pl.when(g == gs)
            def _(gs=gs):
                for tok, sub in _residue_pieces(DIL_R[gs]):
                    dob = dob_ref[tok, :]
                    prod = dob * ob_ref[tok, :]
                    al = al_ref[tok, :]
                    lane = lax.broadcasted_iota(jnp.int32, al.shape, 1)
                    c = jnp.where(hf == 0, 0.0, c_ref[sub, :])
                    for hh in range(heads_per_lb):
                        sl = slice(hh * HD, (hh + 1) * HD)
                        head = hf * heads_per_lb + hh
                        a = jnp.sum(jnp.where(lane == head, al, 0.0), axis=-1, keepdims=True)
                        do_ref[sub, sl] = (a * dob[:, sl]).astype(BF16)
                        c = jnp.where(lane == head, a * jnp.sum(prod[:, sl], axis=-1, keepdims=True), c)
                    c_ref[sub, :] = c

    half = pl.BlockSpec((S, LANES), lambda g, hf: (0, hf))
    return pl.pallas_call(
        body, name="dil_combine_bwd", grid=(DIL_G, LB_PER_CB),
        in_specs=[half, half, pl.BlockSpec((None, S, LANES), lambda g, hf: (g, 0, 0))] + [ANY] * len(deps),
        out_specs=[pl.BlockSpec((None, S, LANES), lambda g, hf: (g, 0, hf)),
                   pl.BlockSpec((None, S, LANES), lambda g, hf: (g, 0, 0))],
        out_shape=[jax.ShapeDtypeStruct((DIL_G, S, DIL_OUT), BF16), jax.ShapeDtypeStruct((DIL_G, S, LANES), F32)],
        compiler_params=_cparams(("parallel", "arbitrary")),
    )(dob, ob, alpha, *deps)


def _local_step(x, tgt, h1, g_attn, g_mlp, g_final, b_pad, w_in_t, hooks):
    tables = _rope_tables()

    first = hooks.first_deps()
    proj_a = _mm(h1, w_in_t, mode="nt", tm=S, tn=512, n=3 * FOX_W, out_dtypes=[BF16], name="proj_a", deps=first)
    proj_b = _mm(h1, w_in_t, mode="nt", tm=S, tn=DIL_W, n=3 * DIL_W, b_off=SEG_B // DIL_W, out_dtypes=[F32], name="proj_b",
                deps=first)
    proj_g = _mm(h1, w_in_t, mode="nt", tm=S, tn=D, n=2 * D, b_off=SEG_G // D, out_dtypes=[F32], name="proj_g",
                deps=first)
    proj_f = _mm(h1, w_in_t, mode="nt", tm=S, tn=LANES, n=LANES, b_off=SEG_F // LANES, out_dtypes=[F32], name="proj_f",
                deps=first)

    q_aug, k_aug = _fox_prepare(proj_a, proj_f, b_pad)
    oa, lse_a = _fox_fwd(q_aug, k_aug, proj_a)

    qkv3 = _rope_split(proj_b, tables)
    o3, lse3 = _dil_fwd(qkv3)
    ob, alpha = _dil_combine(o3, lse3)

    w_a, w_b, w_out = hooks.mixer_weights(ob)
    ya = _mm(oa, w_a, mode="nn", tm=S, tn=512, out_dtypes=[F32], name="branch_a")
    yb = _mm(ob, w_b, mode="nn", tm=S, tn=512, out_dtypes=[F32], name="branch_b")
    mixed = _gate_fwd(proj_g, ya, yb)

    def out_epilogue(acc, rows, vecs):
        x2 = rows[0] + acc
        return (x2, _rms_rows(x2, vecs[0])), ()

    x2, h2 = _mm_rows(mixed, w_out, tm=ROWS_TM, name="out_proj", epilogue=out_epilogue, rows=(x,), vecs=(g_mlp,),
                      row_out=(F32, BF16))
    w_up_sh, w_down = hooks.mlp_weights(h2)

    def up_epilogue(acc):
        r = jnp.maximum(acc, 0.0)
        return acc, r * r

    u, act = _mm(h2, w_up_sh, mode="nn", tm=S, tn=DFF // N_DEV, b_sharded=True, out_dtypes=[F32, BF16],
                 name="mlp_up", epilogue=up_epilogue)

    def loss_epilogue(acc, rows, vecs):
        dx3, dg, loss = _final_loss_rows(rows[0] + acc, vecs[0], rows[1])
        return (dx3,), (dg, loss)

    dx3, dg_final, loss = _mm_rows(act, w_down, tm=ROWS_TM, name="mlp_down_loss", epilogue=loss_epilogue,
                                   rows=(x2, tgt), vecs=(g_final,), row_out=(F32,), vec_out=(D, LANES))

    def rms_bwd_epilogue(acc, rows, vecs):
        dx, dg = _rms_bwd_rows(acc, rows[0], rows[1], vecs[0])
        return (dx,), (dg,)

    du = _mm(dx3, w_down, mode="nt", tm=S, tn=512, out_dtypes=[BF16], name="mlp_down_bwd",
             epilogue=lambda acc, u_t: (acc * (2.0 * jnp.maximum(u_t, 0.0)),), extras=(u,))
    dw_down = _mm(act, dx3, mode="tn", tm=512, tn=D, out_dtypes=[F32], name="dw_down")
    dw_up_sh = _mm(h2, du, mode="tn", tm=D, tn=DFF // N_DEV, out_sharded=True, out_dtypes=[F32], name="dw_up")
    dx2, dg_mlp = _mm_rows(du, w_up_sh, b_sharded=True, tm=ROWS_TM, name="mlp_up_bwd", epilogue=rms_bwd_epilogue,
                           rows=(x2, dx3), vecs=(g_mlp,), row_out=(F32,), vec_out=(D,),
                           deps=hooks.mlp_grads(dw_up_sh, dw_down))

    dw_out = _mm(mixed, dx2, mode="tn", tm=D, tn=D, out_dtypes=[F32], name="dw_out")

    dyab, dproj = _out_proj_gate_bwd(dx2, w_out, proj_g, ya, yb)
    doa = _mm(dyab, w_a, mode="nt", tm=S, tn=FOX_W, k=D, a_off=0, out_dtypes=[BF16], name="branch_a_bwd")
    dw_a = _mm(oa, dyab, mode="tn", tm=FOX_W, tn=D, n=D, b_off=0, out_dtypes=[F32], name="dw_branch_a")
    dob = _mm(dyab, w_b, mode="nt", tm=S, tn=CB, k=D, a_off=1, out_dtypes=[F32], name="branch_b_bwd")
    dw_b = _mm(ob, dyab, mode="tn", tm=DIL_OUT, tn=D, n=D, b_off=1, out_dtypes=[F32], name="dw_branch_b")

    dproj, dft, dfs = _fox_bwd(q_aug, k_aug, proj_a, lse_a, doa, dproj, deps=hooks.mixer_grads(dw_a, dw_b, dw_out))
    dproj, db = _fox_dscan(dft, dfs, proj_f, b_pad, dproj)

    do3, c3 = _dil_combine_bwd(dob, ob, alpha, deps=hooks.mid_backward(db))
    dq3, dk3, dv3 = _dil_bwd(qkv3, lse3, do3, c3)
    dproj = _rope_merge_bwd(dq3, dk3, dv3, tables, dproj)

    dw_in_t = _mm(dproj, h1, mode="tn", tm=CB, tn=D, out_dtypes=[F32], name="dw_in")
    dx, dg_attn = _mm_rows(dproj, w_in_t, tm=ROWS_TM // 2, name="proj_bwd", epilogue=rms_bwd_epilogue,
                           rows=(x, dx2), vecs=(g_attn,), row_out=(F32,), vec_out=(D,),
                           deps=hooks.w_in_grad(dw_in_t))

    return loss, dx, (dg_attn, db, dg_mlp, dg_final)


SHARD_SHAPES = ((D_IN // N_DEV, D), (FOX_W, D // N_DEV), (DIL_OUT, D // N_DEV), (D // N_DEV, D),
                (D, DFF // N_DEV), (DFF // N_DEV, D))
W_NAMES = ("w_in", "w_a", "w_b", "w_out", "w_up", "w_down")
AG_COPIES = 7
HBM = pl.BlockSpec(memory_space=pltpu.HBM)
SEM = pl.BlockSpec(memory_space=pltpu.SEMAPHORE)


def _place():
    return lax.axis_index("x"), lax.axis_index("y"), lax.axis_index("c")


W_SLAB = 752
W_WIN = 272
PAD_BLK = 256


def _pad_runs():
    runs = sorted((pad, pad + b - a, p, a) for p in range(N_DEV) for a, b, pad in _shard_pieces(p))
    blocks = []
    for k in range(D_PAD // PAD_BLK):
        lo, hi = k * PAD_BLK, (k + 1) * PAD_BLK
        blocks.append([(max(lo, r0) - lo, min(hi, r1) - lo, p, a + max(lo, r0) - r0)
                       for r0, r1, p, a in runs if max(lo, r0) < min(hi, r1)])
    return blocks


def _gather_w_in(shard, x_in, g_attn, late_shards):
    blocks = _pad_runs()

    norm_t = 512
    n_chunks = S // norm_t

    n_late = len(late_shards)
    late_shapes = [t.shape for t in late_shards]

    def body(x_ref, xin_ref, g_ref, *refs):
        late_in, (out_ref, h1_ref), late_out = refs[:n_late], refs[n_late:n_late + 2], refs[n_late + 2:2 * n_late + 2]
        scratch = refs[2 * n_late + 2:]
        stage, land, xbuf = scratch[:3]
        late_stage, late_cast = scratch[3:3 + n_late], scratch[3 + n_late:3 + 2 * n_late]
        send_sems, recv_sems, load_sem, x_sems, late_sems, store_sems = scratch[3 + 2 * n_late:]
        x, y, c = _place()
        me, sibling = (x, y, c), (x, y, 1 - c)
        chips = [(1 - x, y), (x, 1 - y), (1 - x, 1 - y)]

        def slot(px, py, pc):
            return land.at[4 * px + 2 * py + pc]

        def copy(k, block, to):
            return pltpu.make_async_remote_copy(
                src_ref=slot(*block), dst_ref=slot(*block), send_sem=send_sems.at[k], recv_sem=recv_sems.at[k],
                device_id=to, device_id_type=MESH)

        load = pltpu.make_async_copy(x_ref, stage, load_sem)
        load.start()
        load.wait()
        land[4 * x + 2 * y + c] = stage[...].astype(BF16)
        sent = [copy(0, me, sibling)] + [copy(1 + j, me, (*chip, c)) for j, chip in enumerate(chips)]
        for cp in sent:
            cp.start()

        def x_load(i):
            return pltpu.make_async_copy(xin_ref.at[pl.ds(i * norm_t, norm_t)], xbuf.at[i % 2], x_sems.at[i % 2])

        late_loads = [pltpu.make_async_copy(late_in[i], late_stage[i], late_sems.at[i]) for i in range(n_late)]
        for ld in late_loads:
            ld.start()
        x_load(0).start()
        for i in range(n_chunks):
            if i + 1 < n_chunks:
                x_load(i + 1).start()
            x_load(i).wait()
            h1_ref[i * norm_t:(i + 1) * norm_t, :] = _rms_rows(xbuf[i % 2], g_ref[...]).astype(BF16)
        stores = []
        for i in range(n_late):
            late_loads[i].wait()
            late_cast[i][...] = late_stage[i][...].astype(BF16)
            stores.append(pltpu.make_async_copy(late_cast[i], late_out[i].at[4 * x + 2 * y + c], store_sems.at[i]))
            stores[-1].start()

        for j, chip in enumerate(chips):
            copy(1 + j, (*chip, c), me).wait_recv()
            sent.append(copy(4 + j, (*chip, c), sibling))
            sent[-1].start()
        copy(0, sibling, me).wait_recv()
        for j, chip in enumerate(chips):
            copy(4 + j, (*chip, 1 - c), me).wait_recv()
        for cp in sent:
            cp.wait_send()
        for st in stores:
            st.wait()

        row = lax.broadcasted_iota(jnp.int32, (PAD_BLK, D), 0)
        for k, runs in enumerate(blocks):
            out = jnp.zeros((PAD_BLK, D), F32)
            for o_lo, o_hi, p, a in runs:
                start = min(a // 16 * 16, W_SLAB - W_WIN)
                win = land[p, start:start + W_WIN, :].astype(F32)
                moved = pltpu.roll(win, (o_lo - (a - start)) % W_WIN, 0)[:PAD_BLK]
                out = jnp.where((row >= o_lo) & (row < o_hi), moved, out)
            out_ref[k * PAD_BLK:(k + 1) * PAD_BLK, :] = out.astype(BF16)

    return pl.pallas_call(
        body, name="all_gather_w_in",
        out_shape=[jax.ShapeDtypeStruct((D_PAD, D), BF16), jax.ShapeDtypeStruct((S, D), BF16)]
                  + [jax.ShapeDtypeStruct((N_DEV,) + sh, BF16) for sh in late_shapes],
        in_specs=[ANY, ANY, pl.BlockSpec(memory_space=pltpu.VMEM)] + [ANY] * n_late,
        out_specs=[pl.BlockSpec(memory_space=pltpu.VMEM)] * 2 + [ANY] * n_late,
        scratch_shapes=[pltpu.VMEM((W_SLAB, D), F32), pltpu.VMEM((N_DEV, W_SLAB, D), BF16),
                        pltpu.VMEM((2, norm_t, D), F32)]
                       + [pltpu.VMEM(sh, F32) for sh in late_shapes] + [pltpu.VMEM(sh, BF16) for sh in late_shapes]
                       + [pltpu.SemaphoreType.DMA((AG_COPIES,)), pltpu.SemaphoreType.DMA((AG_COPIES,)),
                          pltpu.SemaphoreType.DMA(()), pltpu.SemaphoreType.DMA((2,)),
                          pltpu.SemaphoreType.DMA((n_late,)), pltpu.SemaphoreType.DMA((n_late,))],
        compiler_params=_cparams(None),
    )(shard, x_in, g_attn, *late_shards)


PEER_COPIES = N_DEV - 1
SIBLING_COPIES = 4
CHIP_COPIES = 3


def _peer_copies(_, land_refs, send_sems, recv_sems):
    x, y, c = _place()
    mine = 4 * x + 2 * y + c
    copies = []
    for i, ref in enumerate(land_refs):
        for k in range(1, N_DEV):
            peer = (x ^ (k >> 2), y ^ ((k >> 1) & 1), c ^ (k & 1))
            n = i * PEER_COPIES + k - 1
            copies.append(pltpu.make_async_remote_copy(
                src_ref=ref.at[mine], dst_ref=ref.at[mine], send_sem=send_sems.at[n], recv_sem=recv_sems.at[n],
                device_id=peer, device_id_type=MESH))
    return copies


def _sibling_copies(g_refs, r_refs, send_sems, recv_sems):
    x, y, c = _place()
    return [pltpu.make_async_remote_copy(
        src_ref=g_refs[i].at[2 * k + (1 - c)], dst_ref=r_refs[i].at[k],
        send_sem=send_sems.at[SIBLING_COPIES * i + k], recv_sem=recv_sems.at[SIBLING_COPIES * i + k],
        device_id=(x, y, 1 - c), device_id_type=MESH) for i in range(len(g_refs)) for k in range(SIBLING_COPIES)]


def _chip_copies(p_refs, r_refs, send_sems, recv_sems):
    x, y, c = _place()
    chips = [(1 - x, y), (x, 1 - y), (1 - x, 1 - y)]
    return [pltpu.make_async_remote_copy(
        src_ref=p_refs[i].at[2 * cx + cy], dst_ref=r_refs[i].at[j],
        send_sem=send_sems.at[CHIP_COPIES * i + j], recv_sem=recv_sems.at[CHIP_COPIES * i + j],
        device_id=(cx, cy, c), device_id_type=MESH) for i in range(len(p_refs)) for j, (cx, cy) in enumerate(chips)]


def _in_hbm(a):
    return pltpu.with_memory_space_constraint(a, pltpu.HBM)


def _exchange_start(name, copies_fn, n_copies, srcs, lands, after=()):
    n_s, n_l, n_a = len(srcs), len(lands), len(after)

    def body(*refs):
        outs = refs[n_s + n_l + n_a:]
        for cp in copies_fn(refs[:n_s], refs[n_s:n_s + n_l], outs[0], outs[1]):
            cp.start()
        outs[-1][...] = jnp.zeros_like(outs[-1])

    res = pl.pallas_call(
        body, name=name,
        out_shape=[pltpu.SemaphoreType.DMA((n_copies,)), pltpu.SemaphoreType.DMA((n_copies,))]
                  + [pltpu.HBM(t.shape, t.dtype) for t in (*srcs, *lands)] + [jax.ShapeDtypeStruct((8, LANES), F32)],
        in_specs=[HBM] * (n_s + n_l) + [ANY] * n_a,
        out_specs=[SEM, SEM] + [HBM] * (n_s + n_l) + [pl.BlockSpec(memory_space=pltpu.VMEM)],
        input_output_aliases={i: 2 + i for i in range(n_s + n_l)},
        compiler_params=pltpu.CompilerParams(has_side_effects=pltpu.SideEffectType.DATAFLOW_SIDE_EFFECTING),
    )(*[_in_hbm(t) for t in (*srcs, *lands)], *after)
    return res[0], res[1], list(res[2:2 + n_s]), list(res[2 + n_s:2 + n_s + n_l]), res[-1]


def _exchange_wait(name, copies_fn, started, after):
    send_sems, recv_sems, srcs, lands, _ = started
    n_s, n_l = len(srcs), len(lands)

    def body(*refs):
        for cp in copies_fn(refs[:n_s], refs[n_s:n_s + n_l], refs[n_s + n_l], refs[n_s + n_l + 1]):
            cp.wait_send()
            cp.wait_recv()

    res = pl.pallas_call(
        body, name=name,
        out_shape=[pltpu.HBM(t.shape, t.dtype) for t in (*srcs, *lands)],
        in_specs=[HBM] * (n_s + n_l) + [SEM, SEM, ANY],
        out_specs=[HBM] * (n_s + n_l),
        input_output_aliases={i: i for i in range(n_s + n_l)},
        compiler_params=pltpu.CompilerParams(has_side_effects=pltpu.SideEffectType.DATAFLOW_SIDE_EFFECTING),
    )(*srcs, *lands, send_sems, recv_sems, after)
    return list(res[:n_s]), list(res[n_s:])


def _shard_block(shape):
    rows, cols = shape
    if rows % 256:
        return rows, 256
    return min(rows, 512 if cols <= LANES else 256), cols


def _chip_partial(ids, g_stack, recv1, name):
    shape = g_stack.shape[1:]
    br, bc = _shard_block(shape)

    def body(ids_ref, g_ref, r_ref, pb_ref, own_ref):
        s = g_ref[...] + r_ref[...]
        pb_ref[...] = s.astype(BF16)

        @pl.when(pl.program_id(2) == ids_ref[1])
        def _():
            own_ref[...] = s

    grid_spec = pltpu.PrefetchScalarGridSpec(
        num_scalar_prefetch=1, grid=(shape[0] // br, shape[1] // bc, 4),
        in_specs=[pl.BlockSpec((None, br, bc), lambda r, q, k, ids: (2 * k + ids[0], r, q)),
                  pl.BlockSpec((None, br, bc), lambda r, q, k, ids: (k, r, q))],
        out_specs=[pl.BlockSpec((None, br, bc), lambda r, q, k, ids: (k, r, q)),
                   pl.BlockSpec((br, bc), lambda r, q, k, ids: (r, q))])
    return pl.pallas_call(
        body, name=name, grid_spec=grid_spec,
        out_shape=[jax.ShapeDtypeStruct((4,) + shape, BF16), jax.ShapeDtypeStruct(shape, F32)],
        compiler_params=_cparams(("parallel", "parallel", "arbitrary")),
    )(ids, g_stack, recv1)


def _adamw(w, g, m, v):
    m = ADAM_B1 * m + (1.0 - ADAM_B1) * g
    v = ADAM_B2 * v + (1.0 - ADAM_B2) * (g * g)
    m_hat = m / (1.0 - ADAM_B1 ** ADAM_STEP)
    v_hat = v / (1.0 - ADAM_B2 ** ADAM_STEP)
    delta = -ADAM_LR * (m_hat / (jnp.sqrt(v_hat) + ADAM_EPS) + ADAM_WD * w)
    return delta, m, v


def _reduce_adamw(own, recv2, w, m, v, name, deps=()):
    shape = own.shape
    br, bc = _shard_block(shape)

    def body(own_ref, r_ref, w_ref, m_ref, v_ref, *refs):
        g_ref, d_ref, nm_ref, nv_ref = refs[len(deps):]
        g = own_ref[...]
        for j in range(3):
            g = g + r_ref[j].astype(F32)
        delta, nm, nv = _adamw(w_ref[...], g, m_ref[...], v_ref[...])
        g_ref[...] = g
        d_ref[...] = delta
        nm_ref[...] = nm
        nv_ref[...] = nv

    blk = pl.BlockSpec((br, bc), lambda r, q: (r, q))
    return pl.pallas_call(
        body, name=name, grid=(shape[0] // br, shape[1] // bc),
        in_specs=[blk, pl.BlockSpec((3, br, bc), lambda r, q: (0, r, q)), blk, blk, blk] + [ANY] * len(deps),
        out_specs=[blk] * 4, out_shape=[jax.ShapeDtypeStruct(shape, F32)] * 4,
        compiler_params=_cparams(("parallel", "parallel")),
    )(own, recv2, w, m, v, *deps)


def _small_allreduce_adamw(gvec, w, m, v):
    def body(g_ref, w_ref, m_ref, v_ref, go_ref, d_ref, nm_ref, nv_ref, buf, send_sems, recv_sems):
        x, y, c = _place()
        my_slot = 4 * x + 2 * y + c
        buf[my_slot] = g_ref[...]
        copies = []
        for k in range(1, N_DEV):
            px, py, pc = x ^ (k >> 2), y ^ ((k >> 1) & 1), c ^ (k & 1)
            copies.append(pltpu.make_async_remote_copy(
                src_ref=g_ref, dst_ref=buf.at[my_slot], send_sem=send_sems.at[k - 1], recv_sem=recv_sems.at[k - 1],
                device_id=(px, py, pc), device_id_type=MESH))
        for cp in copies:
            cp.start()
        for k in range(1, N_DEV):
            px, py, pc = x ^ (k >> 2), y ^ ((k >> 1) & 1), c ^ (k & 1)
            pltpu.make_async_remote_copy(
                src_ref=g_ref, dst_ref=buf.at[4 * px + 2 * py + pc], send_sem=send_sems.at[k - 1],
                recv_sem=recv_sems.at[k - 1], device_id=(px, py, pc), device_id_type=MESH).wait_recv()
        for cp in copies:
            cp.wait_send()
        g = buf[0]
        for s in range(1, N_DEV):
            g = g + buf[s]
        delta, nm, nv = _adamw(w_ref[...], g, m_ref[...], v_ref[...])
        go_ref[...] = g
        d_ref[...] = delta
        nm_ref[...] = nm
        nv_ref[...] = nv

    vm = pl.BlockSpec(memory_space=pltpu.VMEM)
    return pl.pallas_call(
        body, name="small_allreduce_adamw",
        in_specs=[vm] * 4, out_specs=[vm] * 4,
        out_shape=[jax.ShapeDtypeStruct((SMALL_R, LANES), F32)] * 4,
        scratch_shapes=[pltpu.VMEM((N_DEV, SMALL_R, LANES), F32),
                        pltpu.SemaphoreType.DMA((N_DEV - 1,)), pltpu.SemaphoreType.DMA((N_DEV - 1,))],
    )(gvec, w, m, v)


def _cols_to_whole(stack):
    rows = stack.shape[1]
    return stack.transpose(1, 0, 2).reshape(rows, -1)


def _whole_to_cols(t):
    rows = t.shape[0]
    return t.reshape(rows, N_DEV, -1).transpose(1, 0, 2)


W_IN_SEGMENTS = ((0, 3 * FOX_W, SEG_A), (3 * FOX_W, 3 * FOX_W + FOX_H, SEG_F),
                 (3 * FOX_W + FOX_H, 3 * FOX_W + FOX_H + 3 * DIL_W, SEG_B), (3 * FOX_W + FOX_H + 3 * DIL_W, D_IN, SEG_G))


def _shard_pieces(p):
    rows = D_IN // N_DEV
    lo, hi = p * rows, (p + 1) * rows
    return [(max(lo, a) - lo, min(hi, b) - lo, pad + max(lo, a) - a)
            for a, b, pad in W_IN_SEGMENTS if max(lo, a) < min(hi, b)]


def _unpad_dw_in_t(dwp):
    return jnp.stack([jnp.concatenate([dwp[pad:pad + b - a] for a, b, pad in _shard_pieces(p)], axis=0)
                      for p in range(N_DEV)])


LOSS_ROW = 25


def _pack_small(g_attn, b, g_mlp, g_final, loss_row=None):
    tail = jnp.pad(b, ((0, 7), (0, LANES - b.shape[1])))
    if loss_row is not None:
        tail = tail + jnp.pad(loss_row, ((LOSS_ROW - 24, 31 - LOSS_ROW), (0, 0)))
    return jnp.concatenate([g_attn.reshape(8, LANES), g_mlp.reshape(8, LANES), g_final.reshape(8, LANES), tail], axis=0)


def _unpack_small(p):
    return (p[0:8].reshape(1, D), p[24:25, :FOX_H], p[8:16].reshape(1, D), p[16:24].reshape(D))


class _StepComm:
    def __init__(self, ids, w_sh, m_sh, v_sh, lands, after_w_in):
        self.ids = ids
        self.w_sh, self.m_sh, self.v_sh = w_sh, m_sh, v_sh
        self.updates = None
        self.gather_mixer = _exchange_start("gather_mixer_start", _peer_copies, PEER_COPIES * 3, [], lands[:3],
                                            after=(after_w_in,))
        self.gather_mlp = _exchange_start("gather_mlp_start", _peer_copies, PEER_COPIES * 2, [], lands[3:],
                                          after=(self.gather_mixer[-1],))
        self.sibling = {}
        self.chips = {}
        self.own = {}
        self.names = {}

    def _reduce_start(self, group, names, grads, after=()):
        lands = [lax.empty((SIBLING_COPIES,) + t.shape[1:], F32) for t in grads]
        self.sibling[group] = _exchange_start("grad_%s_sibling_start" % group, _sibling_copies,
                                              SIBLING_COPIES * len(grads), grads, lands, after=after)
        self.names[group] = names
        return self.sibling[group][-1]

    def _reduce_mid(self, group, after):
        grads, recv1 = _exchange_wait("grad_%s_sibling_wait" % group, _sibling_copies, self.sibling[group], after)
        parts = [_chip_partial(self.ids, g, r, "grad_partial_" + n) for g, r, n in zip(grads, recv1, self.names[group])]
        self.own[group] = [p[1] for p in parts]
        srcs = [p[0] for p in parts]
        lands = [lax.empty((CHIP_COPIES,) + t.shape[1:], BF16) for t in srcs]
        self.chips[group] = _exchange_start("grad_%s_chips_start" % group, _chip_copies, CHIP_COPIES * len(srcs),
                                            srcs, lands)
        return self.chips[group][-1]

    def reduced(self, group, after):
        _, recv2 = _exchange_wait("grad_%s_chips_wait" % group, _chip_copies, self.chips[group], after)
        return list(zip(self.own[group], recv2))

    def first_deps(self):
        return [self.gather_mlp[-1]]

    def mixer_weights(self, after):
        _, (g_a, g_b, g_out) = _exchange_wait("gather_mixer_wait", _peer_copies, self.gather_mixer, after)
        return _cols_to_whole(g_a), _cols_to_whole(g_b), g_out.reshape(D, D)

    def mlp_weights(self, after):
        _, (g_up, g_down) = _exchange_wait("gather_mlp_wait", _peer_copies, self.gather_mlp, after)
        return g_up, g_down.reshape(DFF, D)

    def mlp_grads(self, dw_up_sh, dw_down):
        return [self._reduce_start("mlp", W_NAMES[4:], [dw_up_sh, dw_down.reshape((N_DEV,) + SHARD_SHAPES[5])])]

    def mixer_grads(self, dw_a, dw_b, dw_out):
        token = self._reduce_mid("mlp", dw_b)
        grads = [_whole_to_cols(dw_a), _whole_to_cols(dw_b), dw_out.reshape((N_DEV,) + SHARD_SHAPES[3])]
        return [self._reduce_start("mixer", W_NAMES[1:4], grads, after=(token,))]

    def mid_backward(self, after):
        return [self._reduce_mid("mixer", after)]

    def w_in_grad(self, dw_in_t):
        grads = [_unpad_dw_in_t(dw_in_t)]
        token = self._reduce_start("w_in", W_NAMES[:1], grads)
        reduced = self.reduced("mixer", token) + self.reduced("mlp", token)
        self.updates, last = [None] * len(reduced), token
        for i in (3, 4, 0, 1, 2):
            if i == 0:
                last = self._reduce_mid("w_in", last)
            self.updates[i] = _reduce_adamw(*reduced[i], self.w_sh[1 + i], self.m_sh[1 + i], self.v_sh[1 + i],
                                            "adamw_" + W_NAMES[1 + i], deps=[last])
            last = self.updates[i][0]
        return [last]

    def w_in_update(self, after):
        (reduced,) = self.reduced("w_in", after)
        return _reduce_adamw(*reduced, self.w_sh[0], self.m_sh[0], self.v_sh[0], "adamw_" + W_NAMES[0])


def kernel(x, norm_attn_g, w_in, b_forget, w_branch_a, w_branch_b, w_out, norm_mlp_g, w_up, w_down, norm_final_g, loss_target, m_norm_attn_g, m_w_in, m_b_forget, m_w_branch_a, m_w_branch_b, m_w_out, m_norm_mlp_g, m_w_up, m_w_down, m_norm_final_g, v_norm_attn_g, v_w_in, v_b_forget, v_w_branch_a, v_w_branch_b, v_w_out, v_norm_mlp_g, v_w_up, v_w_down, v_norm_final_g):
    cx, cy, cc = _place()
    ids = jnp.stack([cc, 2 * cx + cy]).astype(jnp.int32)

    w_sh = [w_in[0].T] + [t[0] for t in (w_branch_a, w_branch_b, w_out, w_up, w_down)]
    m_sh = [m_w_in[0].T] + [t[0] for t in (m_w_branch_a, m_w_branch_b, m_w_out, m_w_up, m_w_down)]
    v_sh = [v_w_in[0].T] + [t[0] for t in (v_w_branch_a, v_w_branch_b, v_w_out, v_w_up, v_w_down)]

    w_in_t, h1, *lands = _gather_w_in(jnp.pad(w_sh[0], ((0, W_SLAB - w_sh[0].shape[0]), (0, 0))), x[0], norm_attn_g,
                                      w_sh[1:])
    comm = _StepComm(ids, w_sh, m_sh, v_sh, lands, w_in_t)
    b_pad = jnp.pad(b_forget, ((0, 0), (0, LANES - FOX_H)))

    loss_row, dx, dsmall = _local_step(
        x[0], loss_target[0], h1, norm_attn_g, norm_mlp_g, norm_final_g.reshape(1, D), b_pad, w_in_t, comm)

    dg_attn, db, dg_mlp, dg_final = dsmall
    small = _small_allreduce_adamw(
        _pack_small(dg_attn, db, dg_mlp, dg_final, loss_row),
        _pack_small(norm_attn_g, b_forget, norm_mlp_g, norm_final_g.reshape(1, D)),
        _pack_small(m_norm_attn_g, m_b_forget, m_norm_mlp_g, m_norm_final_g.reshape(1, D)),
        _pack_small(v_norm_attn_g, v_b_forget, v_norm_mlp_g, v_norm_final_g.reshape(1, D)))
    big = [comm.w_in_update(small[0])] + comm.updates

    outs = [small[0][LOSS_ROW, 0], dx[None]]
    for q in range(4):
        s_attn, s_b, s_mlp, s_final = _unpack_small(small[q])
        b_in, b_a, b_b, b_out, b_up, b_down = [t[q][None] for t in big]
        b_in = jnp.swapaxes(b_in, 1, 2)
        outs += [s_attn, b_in, s_b, b_a, b_b, b_out, s_mlp, b_up, b_down, s_final]
    return tuple(outs)
```

```python
import functools

import jax
import jax.numpy as jnp
from jax import lax
from jax.experimental import pallas as pl
from jax.experimental.pallas import tpu as pltpu

F32 = jnp.float32
BF16 = jnp.bfloat16
MESH = pl.DeviceIdType.MESH

S = 2048
D = 1024
HD = 64
FOX_H = 8
FOX_W = FOX_H * HD
DIL_HG = 4
DIL_G = 3
DIL_W = DIL_G * DIL_HG * HD
DIL_OUT = DIL_HG * HD
DIL_BLK = 128
DIL_R = (1, 4, 16)
DFF = 4 * D
D_IN = 3 * FOX_W + FOX_H + 3 * DIL_W + 2 * D
EPS = 1e-6
NEG_INF = -1e30
SCALE = HD ** -0.5
ROPE_THETA = 500000.0
ROPE_DIM = HD // 4
N_DEV = 8

ADAM_LR = 0.001
ADAM_B1 = 0.9
ADAM_B2 = 0.999
ADAM_EPS = 1e-08
ADAM_WD = 0.01
ADAM_STEP = 10

LANES = 128
CB = 256
SEG_A = 0
SEG_B = 3 * FOX_W
SEG_Z = SEG_B + 3 * DIL_W
SEG_G = SEG_Z + CB
SEG_F = SEG_G + 2 * D
D_PAD = SEG_F + CB
SMALL_R = 32

VMEM_MB = 56


def _cparams(dims=None, vmem_mb=VMEM_MB, **kw):
    return pltpu.CompilerParams(dimension_semantics=dims, vmem_limit_bytes=vmem_mb << 20, **kw)


ANY = pl.BlockSpec(memory_space=pl.ANY)

_NN = (((1,), (0,)), ((), ()))
_NT = (((1,), (1,)), ((), ()))
_TN = (((0,), (0,)), ((), ()))


def _dot(a, b, dims):
    return lax.dot_general(a.astype(BF16), b.astype(BF16), dims, preferred_element_type=F32)


def _mm(a, b, *, mode, tm, tn, out_dtypes, name, n=None, k=None, a_off=0, b_off=0,
        b_sharded=False, out_sharded=False, epilogue=None, extras=(), deps=()):
    n_sh = b.shape[-1] if b_sharded else None
    if mode == "nn":
        m = a.shape[0]
        k = k or a.shape[1]
        a_spec = pl.BlockSpec((tm, k), lambda i, j: (i, a_off))
        if b_sharded:
            assert tn == n_sh
            n = N_DEV * n_sh
            b_spec = pl.BlockSpec((None, k, tn), lambda i, j: (j, 0, 0))
        else:
            n = n or b.shape[1]
            b_spec = pl.BlockSpec((k, tn), lambda i, j: (0, j + b_off))
        dims = _NN
    elif mode == "nt":
        m = a.shape[0]
        k = k or a.shape[1]
        a_spec = pl.BlockSpec((tm, k), lambda i, j: (i, a_off))
        if b_sharded:
            n = b.shape[1]
            b_spec = pl.BlockSpec((N_DEV, tn, n_sh), lambda i, j: (0, j, 0))
        else:
            n = n or b.shape[0]
            b_spec = pl.BlockSpec((tn, k), lambda i, j: (j + b_off, 0))
        dims = _NT
    else:
        k, m = a.shape
        n = n or b.shape[1]
        a_spec = pl.BlockSpec((k, tm), lambda i, j: (0, i))
        b_spec = pl.BlockSpec((k, tn), lambda i, j: (0, j + b_off))
        dims = _TN
    assert m % tm == 0 and n % tn == 0, (name, m, n, tm, tn)
    n_extra = len(extras)
    tile = pl.BlockSpec((tm, tn), lambda i, j: (i, j))
    if out_sharded:
        assert mode == "tn" and n // tn == N_DEV
        out_spec = pl.BlockSpec((None, tm, tn), lambda i, j: (j, i, 0))
        out_shape = (N_DEV, m, tn)
    else:
        out_spec, out_shape = tile, (m, n)

    def body(a_ref, b_ref, *refs):
        if mode == "nt" and b_sharded:
            acc = _dot(a_ref[:, 0:n_sh], b_ref[0], dims)
            for p in range(1, N_DEV):
                acc = acc + _dot(a_ref[:, p * n_sh:(p + 1) * n_sh], b_ref[p], dims)
        else:
            acc = _dot(a_ref[...], b_ref[...], dims)
        ex = [r[...] for r in refs[:n_extra]]
        outs = epilogue(acc, *ex) if epilogue is not None else (acc,)
        for o_ref, o in zip(refs[n_extra + len(deps):], outs):
            o_ref[...] = o.astype(o_ref.dtype)

    res = pl.pallas_call(
        body, name=name, grid=(m // tm, n // tn),
        in_specs=[a_spec, b_spec] + [tile] * n_extra + [ANY] * len(deps),
        out_specs=[out_spec] * len(out_dtypes),
        out_shape=[jax.ShapeDtypeStruct(out_shape, dt) for dt in out_dtypes],
        compiler_params=_cparams(("parallel", "parallel")),
    )(a, b, *extras, *deps)
    return res if len(out_dtypes) > 1 else res[0]


def _mm_rows(a, b, *, tm, name, epilogue, rows=(), vecs=(), row_out=(), vec_out=(), b_sharded=False, deps=()):
    m, k = a.shape
    n_rows, n_vecs, n_deps = len(rows), len(vecs), len(deps)
    n_sh = b.shape[-1] if b_sharded else None

    def body(a_ref, b_ref, *refs):
        row_refs, vec_refs = refs[:n_rows], refs[n_rows:n_rows + n_vecs]
        outs = refs[n_rows + n_vecs + n_deps:]
        if b_sharded:
            acc = _dot(a_ref[:, 0:n_sh], b_ref[0], _NT)
            for p in range(1, N_DEV):
                acc = acc + _dot(a_ref[:, p * n_sh:(p + 1) * n_sh], b_ref[p], _NT)
        else:
            acc = _dot(a_ref[...], b_ref[...], _NN)
        row_vals, vec_incs = epilogue(acc, [r[...] for r in row_refs], [v[...] for v in vec_refs])
        for o_ref, val in zip(outs[:len(row_out)], row_vals):
            o_ref[...] = val.astype(o_ref.dtype)

        @pl.when(pl.program_id(0) == 0)
        def _():
            for o_ref in outs[len(row_out):]:
                o_ref[...] = jnp.zeros_like(o_ref)

        for o_ref, inc in zip(outs[len(row_out):], vec_incs):
            o_ref[...] += inc

    tile = pl.BlockSpec((tm, D), lambda i: (i, 0))
    b_spec = pl.BlockSpec(b.shape, lambda i: (0,) * b.ndim)
    return pl.pallas_call(
        body, name=name, grid=(m // tm,),
        in_specs=[pl.BlockSpec((tm, k), lambda i: (i, 0)), b_spec] + [tile] * n_rows
                 + [pl.BlockSpec(v.shape, lambda i: (0, 0)) for v in vecs] + [ANY] * n_deps,
        out_specs=[tile] * len(row_out) + [pl.BlockSpec((1, w), lambda i: (0, 0)) for w in vec_out],
        out_shape=[jax.ShapeDtypeStruct((m, D), dt) for dt in row_out]
                  + [jax.ShapeDtypeStruct((1, w), F32) for w in vec_out],
        compiler_params=_cparams(("arbitrary",)),
    )(a, b, *rows, *vecs, *deps)


ROW_T = 256
ROWS_TM = 512


def _rms_rows(x, g):
    r = lax.rsqrt(jnp.mean(x * x, axis=-1, keepdims=True) + EPS)
    return (x * r) * g


def _rms_bwd_rows(dh, x, dres, g):
    r = lax.rsqrt(jnp.mean(x * x, axis=-1, keepdims=True) + EPS)
    xn = x * r
    dhn = dh * g
    dx = dres + r * (dhn - xn * jnp.mean(dhn * xn, axis=-1, keepdims=True))
    return dx, jnp.sum(dh * xn, axis=0, keepdims=True)


def _final_loss_rows(x, g, tgt):
    r = lax.rsqrt(jnp.mean(x * x, axis=-1, keepdims=True) + EPS)
    xn = x * r
    err = xn * g - tgt
    row_loss = jnp.mean(err * err, axis=-1, keepdims=True)
    loss = 0.5 * jnp.sum(row_loss, axis=0, keepdims=True) * jnp.ones((1, LANES), F32)
    dy = err * (1.0 / D)
    dyn = dy * g
    dx = r * (dyn - xn * jnp.mean(dyn * xn, axis=-1, keepdims=True))
    return dx, jnp.sum(dy * xn, axis=0, keepdims=True), loss


def _sigmoid(z):
    return 1.0 / (1.0 + jnp.exp(-z))


GATE_TR = 512


def _mixer_tail(oa, ob, w_a, w_b, w_out, proj_g, x, g_mlp):
    def body(oa_ref, ob_ref, wa_ref, wb_ref, wo_ref, g_ref, x_ref, gm_ref, ya_ref, yb_ref, mixed_ref, x2_ref, h2_ref):
        ya = _dot(oa_ref[...], wa_ref[...], _NN)
        yb = _dot(ob_ref[...], wb_ref[...], _NN)
        ya_ref[...] = ya
        yb_ref[...] = yb
        mixed = (_sigmoid(g_ref[:, :D]) * ya + _sigmoid(g_ref[:, D:]) * yb).astype(BF16)
        mixed_ref[...] = mixed
        x2 = x_ref[...] + _dot(mixed, wo_ref[...], _NN)
        x2_ref[...] = x2
        h2_ref[...] = _rms_rows(x2, gm_ref[...]).astype(BF16)

    def rows(width):
        return pl.BlockSpec((GATE_TR, width), lambda i: (i, 0))

    def whole(t):
        return pl.BlockSpec(t.shape, lambda i: (0, 0))

    return pl.pallas_call(
        body, name="mixer_tail", grid=(S // GATE_TR,),
        in_specs=[rows(FOX_W), rows(DIL_OUT), whole(w_a), whole(w_b), whole(w_out), rows(2 * D), rows(D), whole(g_mlp)],
        out_specs=[rows(D)] * 5,
        out_shape=[jax.ShapeDtypeStruct((S, D), dt) for dt in (F32, F32, BF16, F32, BF16)],
        compiler_params=_cparams(("parallel",)),
    )(oa, ob, w_a, w_b, w_out, proj_g, x, g_mlp)


def _out_proj_gate_bwd(dx2, w_out, proj_g, ya, yb):
    def body(dx_ref, w_ref, g_ref, ya_ref, yb_ref, dy_ref, dg_ref):
        dm = _dot(dx_ref[...], w_ref[...], _NT)
        for half, y_ref in enumerate((ya_ref, yb_ref)):
            cols = slice(half * D, (half + 1) * D)
            s = _sigmoid(g_ref[:, cols])
            dy_ref[:, cols] = (dm * s).astype(BF16)
            dg_ref[:, cols] = (dm * y_ref[...] * (s * (1.0 - s))).astype(BF16)

    row = pl.BlockSpec((GATE_TR, D), lambda i: (i, 0))
    wide = pl.BlockSpec((GATE_TR, 2 * D), lambda i: (i, 0))
    return pl.pallas_call(
        body, name="out_proj_gate_bwd", grid=(S // GATE_TR,),
        in_specs=[row, pl.BlockSpec((D, D), lambda i: (0, 0)), wide, row, row],
        out_specs=[wide, pl.BlockSpec((GATE_TR, 2 * D), lambda i: (i, SEG_G // (2 * D)))],
        out_shape=[jax.ShapeDtypeStruct((S, 2 * D), BF16), jax.ShapeDtypeStruct((S, D_PAD), BF16)],
        compiler_params=_cparams(("parallel",)),
    )(dx2, w_out, proj_g, ya, yb)


FOX_TQ = 256


def _scan_rows(x, reverse):
    n = x.shape[0]
    row = lax.broadcasted_iota(jnp.int32, x.shape, 0)
    k = 1
    while k < n:
        if reverse:
            x = x + jnp.where(row < n - k, pltpu.roll(x, n - k, 0), 0.0)
        else:
            x = x + jnp.where(row >= k, pltpu.roll(x, k, 0), 0.0)
        k *= 2
    return x


def _fox_dscan(dft, dfs, proj_f, b_pad, dproj):
    def body(dft_ref, dfs_ref, f_ref, b_ref, _, dfa_ref, db_ref):
        dfs_pad = jnp.concatenate([dfs_ref[...], jnp.zeros((LANES - FOX_H, S), F32)], axis=0)
        df = dfs_pad.T + dft_ref[0]
        for hp in range(1, dft_ref.shape[0]):
            df = df + pltpu.roll(dft_ref[hp], 2 * hp, 1)
        dlf = _scan_rows(df, reverse=True)
        z = f_ref[...] + b_ref[...]
        lane = lax.broadcasted_iota(jnp.int32, (S, LANES), 1)
        dfa = jnp.where(lane < FOX_H, dlf / (1.0 + jnp.exp(z)), 0.0)
        dfa_ref[:, :LANES] = dfa.astype(BF16)
        dfa_ref[:, LANES:] = jnp.zeros((S, CB - LANES), BF16)
        db_ref[...] = jnp.sum(dfa, axis=0, keepdims=True)

    return pl.pallas_call(
        body, name="fox_dscan", grid=(1,),
        in_specs=[pl.BlockSpec(dft.shape, lambda i: (0, 0, 0)), pl.BlockSpec((FOX_H, S), lambda i: (0, 0)),
                  pl.BlockSpec((S, LANES), lambda i: (0, 0)), pl.BlockSpec((1, LANES), lambda i: (0, 0)), ANY],
        out_specs=[pl.BlockSpec((S, CB), lambda i: (0, SEG_F // CB)), pl.BlockSpec((1, LANES), lambda i: (0, 0))],
        out_shape=[jax.ShapeDtypeStruct((S, D_PAD), BF16), jax.ShapeDtypeStruct((1, LANES), F32)],
        input_output_aliases={4: 0},
        compiler_params=_cparams(("arbitrary",)),
    )(dft, dfs, proj_f, b_pad, dproj)


FOX_NQ = S // FOX_TQ
FOX_HP = FOX_W // LANES
HEADS_PER_LB = LANES // HD
FOX_AUG = 2 * LANES


def _fox_prepare(proj_a, proj_f, b_pad):
    tr = 512

    def body(q_ref, k_ref, f_ref, b_ref, qa_ref, ka_ref, hi_s, mid_s, lo_s):
        i = pl.program_id(0)

        @pl.when(i == 0)
        def _():
            z = f_ref[...] + b_ref[...]
            lf = jnp.minimum(z, 0.0) - jnp.log1p(jnp.exp(-jnp.abs(z)))
            f = _scan_rows(lf, reverse=False)
            hi = f.astype(BF16).astype(F32)
            r1 = f - hi
            mid = r1.astype(BF16).astype(F32)
            hi_s[...] = hi
            mid_s[...] = mid
            lo_s[...] = (r1 - mid).astype(BF16).astype(F32)

        rows = pl.ds(pl.multiple_of(i * tr, tr), tr)
        hi, mid, lo = hi_s[rows, :], mid_s[rows, :], lo_s[rows, :]
        lane = lax.broadcasted_iota(jnp.int32, (tr, HD), 1)
        q_ones = jnp.where((lane >= 3) & (lane < 6), 1.0, 0.0)
        k_ones = jnp.where(lane < 3, 1.0, 0.0)
        for h in range(FOX_H):
            a1, a2, a3 = hi[:, h:h + 1], mid[:, h:h + 1], lo[:, h:h + 1]
            q_extra = jnp.where(lane == 0, a1, jnp.where(lane == 1, a2, jnp.where(lane == 2, a3, q_ones)))
            k_extra = jnp.where(lane == 3, -a1, jnp.where(lane == 4, -a2, jnp.where(lane == 5, -a3, k_ones)))
            base = h * LANES
            qa_ref[:, base:base + HD] = q_ref[:, h * HD:(h + 1) * HD] * jnp.asarray(SCALE, BF16)
            qa_ref[:, base + HD:base + LANES] = q_extra.astype(BF16)
            ka_ref[:, base:base + HD] = k_ref[:, h * HD:(h + 1) * HD]
            ka_ref[:, base + HD:base + LANES] = k_extra.astype(BF16)

    out_blk = pl.BlockSpec((tr, FOX_H * LANES), lambda i: (i, 0))
    return pl.pallas_call(
        body, name="fox_prepare", grid=(S // tr,),
        in_specs=[pl.BlockSpec((tr, FOX_W), lambda i: (i, 0)), pl.BlockSpec((tr, FOX_W), lambda i: (i, 1)),
                  pl.BlockSpec((S, LANES), lambda i: (0, 0)), pl.BlockSpec((1, LANES), lambda i: (0, 0))],
        out_specs=[out_blk, out_blk],
        out_shape=[jax.ShapeDtypeStruct((S, FOX_H * LANES), BF16)] * 2,
        scratch_shapes=[pltpu.VMEM((S, LANES), F32)] * 3,
        compiler_params=_cparams(("arbitrary",)),
    )(proj_a, proj_a, proj_f, b_pad)


def _fox_scores(qa, ka, w):
    s = _dot(qa, ka, _NT)
    row = lax.broadcasted_iota(jnp.int32, (FOX_TQ, FOX_TQ), 0)
    col = lax.broadcasted_iota(jnp.int32, (FOX_TQ, FOX_TQ), 1)
    diag = jnp.where(col <= row, s[:, w * FOX_TQ:], NEG_INF)
    return diag if w == 0 else jnp.concatenate([s[:, :w * FOX_TQ], diag], axis=1)


def _fox_fwd(q_aug, k_aug, proj_a):
    def body(qa_ref, ka_ref, v_ref, o_ref, lse_ref):
        qi = pl.program_id(1)
        lane = lax.broadcasted_iota(jnp.int32, (FOX_TQ, LANES), 1)
        for w in range(FOX_NQ):
            @pl.when(qi == w)
            def _(w=w):
                width = (w + 1) * FOX_TQ
                lse = jnp.zeros((FOX_TQ, LANES), F32)
                for hh in range(HEADS_PER_LB):
                    aug = slice(hh * LANES, (hh + 1) * LANES)
                    sl = slice(hh * HD, (hh + 1) * HD)
                    s = _fox_scores(qa_ref[:, aug], ka_ref[:width, aug], w)
                    m = jnp.max(s, axis=-1, keepdims=True)
                    p = jnp.exp(s - m)
                    l = jnp.sum(p, axis=-1, keepdims=True)
                    o_ref[:, sl] = _dot(p / l, v_ref[:width, sl], _NN)
                    lse = jnp.where(lane == hh, m + jnp.log(l), lse)
                lse_ref[...] = lse

    return pl.pallas_call(
        body, name="fox_fwd", grid=(FOX_HP, FOX_NQ),
        in_specs=[pl.BlockSpec((FOX_TQ, FOX_AUG), lambda h, i: (i, h)),
                  pl.BlockSpec((S, FOX_AUG), lambda h, i: (0, h)),
                  pl.BlockSpec((S, LANES), lambda h, i: (0, 2 * FOX_HP + h))],
        out_specs=[pl.BlockSpec((FOX_TQ, LANES), lambda h, i: (i, h)),
                   pl.BlockSpec((None, FOX_TQ, LANES), lambda h, i: (h, i, 0))],
        out_shape=[jax.ShapeDtypeStruct((S, FOX_W), F32), jax.ShapeDtypeStruct((FOX_HP, S, LANES), F32)],
        compiler_params=_cparams(("parallel", "parallel")),
    )(q_aug, k_aug, proj_a)


def _fox_bwd(q_aug, k_aug, proj_a, lse, do, dproj, deps=()):
    n_q = FOX_NQ

    def body(qa_ref, ka_ref, v_ref, lse_ref, do_ref, *refs):
        dproj_ref, dft_ref, dfs_ref, dk_acc, dv_acc, dq_buf, kv_buf, dq_sems, kv_sems = refs[1 + len(deps):]
        hp = pl.program_id(0)
        t = pl.program_id(1)
        slot = t % 2
        col = pl.multiple_of(hp * LANES, LANES)

        def dq_copy(step, buf_slot):
            rows = pl.ds(pl.multiple_of(step * FOX_TQ, FOX_TQ), FOX_TQ)
            return pltpu.make_async_copy(dq_buf.at[buf_slot], dproj_ref.at[rows, pl.ds(col, LANES)], dq_sems.at[buf_slot])

        @pl.when(t == 0)
        def _():
            dk_acc[...] = jnp.zeros_like(dk_acc)
            dv_acc[...] = jnp.zeros_like(dv_acc)

        @pl.when((t == 0) & (hp == 0))
        def _():
            dfs_ref[...] = jnp.zeros_like(dfs_ref)

        @pl.when(t >= 2)
        def _():
            dq_copy(t - 2, slot).wait()

        lane = lax.broadcasted_iota(jnp.int32, (FOX_TQ, LANES), 1)
        for w in range(n_q):
            @pl.when(t == w)
            def _(w=w):
                width = (w + 1) * FOX_TQ
                lse_all = lse_ref[...]
                sub = lax.broadcasted_iota(jnp.int32, (FOX_H, width), 0)
                dft = jnp.zeros((FOX_TQ, LANES), F32)
                for hh in range(HEADS_PER_LB):
                    aug = slice(hh * LANES, (hh + 1) * LANES)
                    sl = slice(hh * HD, (hh + 1) * HD)
                    head = hp * HEADS_PER_LB + hh
                    qa = qa_ref[:, aug]
                    ka = ka_ref[:width, aug]
                    do_h = do_ref[:, sl]
                    s = _fox_scores(qa, ka, w)
                    p = jnp.exp(s - lse_all[:, hh:hh + 1])
                    dp = _dot(do_h, v_ref[:width, sl], _NT)
                    ds = p * (dp - jnp.sum(dp * p, axis=-1, keepdims=True))
                    dft = jnp.where(lane == hh, jnp.sum(ds, axis=-1, keepdims=True), dft)
                    dfs_ref[:, :width] -= jnp.where(sub == head, jnp.sum(ds, axis=0, keepdims=True), 0.0)
                    dq_buf[slot, :, sl] = (_dot(ds, ka[:, :HD], _NN) * SCALE).astype(BF16)
                    dk_acc[:width, sl] += _dot(ds, qa[:, :HD], _TN)
                    dv_acc[:width, sl] += _dot(p, do_h, _TN)
                dft_ref[...] = dft

        dq_copy(t, slot).start()

        @pl.when(t == n_q - 1)
        def _():
            dq_copy(t - 1, 1 - slot).wait()
            dq_copy(t, slot).wait()
            kv_buf[0] = dk_acc[...].astype(BF16)
            kv_buf[1] = dv_acc[...].astype(BF16)
            copies = [pltpu.make_async_copy(
                kv_buf.at[j], dproj_ref.at[:, pl.ds(pl.multiple_of((j + 1) * FOX_W + hp * LANES, LANES), LANES)],
                kv_sems.at[j]) for j in range(2)]
            for cp in copies:
                cp.start()
            for cp in copies:
                cp.wait()

    qblk = pl.BlockSpec((FOX_TQ, FOX_AUG), lambda h, t: (t, h))
    lane_blk = pl.BlockSpec((None, FOX_TQ, LANES), lambda h, t: (h, t, 0))
    return pl.pallas_call(
        body, name="fox_bwd", grid=(FOX_HP, n_q),
        in_specs=[qblk, pl.BlockSpec((S, FOX_AUG), lambda h, t: (0, h)),
                  pl.BlockSpec((S, LANES), lambda h, t: (0, 2 * FOX_HP + h)),
                  lane_blk, pl.BlockSpec((FOX_TQ, LANES), lambda h, t: (t, h)), ANY] + [ANY] * len(deps),
        out_specs=[ANY, lane_blk, pl.BlockSpec((FOX_H, S), lambda h, t: (0, 0))],
        out_shape=[jax.ShapeDtypeStruct((S, D_PAD), BF16),
                   jax.ShapeDtypeStruct((FOX_HP, S, LANES), F32), jax.ShapeDtypeStruct((FOX_H, S), F32)],
        scratch_shapes=[pltpu.VMEM((S, LANES), F32), pltpu.VMEM((S, LANES), F32),
                        pltpu.VMEM((2, FOX_TQ, LANES), BF16), pltpu.VMEM((2, S, LANES), BF16),
                        pltpu.SemaphoreType.DMA((2,)), pltpu.SemaphoreType.DMA((2,))],
        input_output_aliases={5: 0},
        compiler_params=_cparams(("arbitrary", "arbitrary")),
    )(q_aug, k_aug, proj_a, lse, do, dproj, *deps)


def _rope_tables():
    half = ROPE_DIM // 2
    shape = (DIL_G, S, half)
    g = lax.broadcasted_iota(jnp.int32, shape, 0)
    row = lax.broadcasted_iota(jnp.int32, shape, 1)
    r = jnp.left_shift(1, 2 * g)
    per_class = S // r
    pos = (row % per_class) * r + row // per_class
    inv_freq = jnp.power(jnp.float32(ROPE_THETA), -lax.broadcasted_iota(F32, shape, 2) * 2.0 / ROPE_DIM)
    ang = pos.astype(F32) * inv_freq
    return jnp.concatenate([jnp.cos(ang), jnp.sin(ang)], axis=-1)


def _rope_factors(cs):
    half = ROPE_DIM // 2
    i = lax.broadcasted_iota(jnp.int32, (2 * half, 2 * LANES), 0)
    l = lax.broadcasted_iota(jnp.int32, (2 * half, 2 * LANES), 1)
    hit = ((l < LANES) & (i == l % half)) | ((l >= LANES) & (i - half == l % half))
    spread = jnp.where(hit, 1.0, 0.0).astype(BF16)
    hi = cs.astype(BF16)
    r1 = cs - hi.astype(F32)
    mid = r1.astype(BF16)
    lo = (r1 - mid.astype(F32)).astype(BF16)
    full = (_dot(hi, spread, _NN) + _dot(mid, spread, _NN)) + _dot(lo, spread, _NN)
    cos, sin = full[:, :LANES], full[:, LANES:]
    lane = lax.broadcasted_iota(jnp.int32, cos.shape, 1) % HD
    t_self = jnp.where(lane < ROPE_DIM, cos, 1.0)
    t_up = jnp.where(lane < half, -sin, 0.0)
    t_dn = jnp.where((lane >= half) & (lane < ROPE_DIM), sin, 0.0)
    return t_self, t_up, t_dn


def _residue_pieces(r):
    if r == 1:
        return [(slice(i, i + 512), slice(i, i + 512)) for i in range(0, S, 512)]
    n = S // r
    return [(pl.ds(j, n, stride=r), slice(j * n, (j + 1) * n)) for j in range(r)]


def _rope_apply(x, a, u, d, backward):
    half = ROPE_DIM // 2
    if backward:
        return x * a + pltpu.roll(x * u, half, 1) + pltpu.roll(x * d, LANES - half, 1)
    return x * a + pltpu.roll(x, LANES - half, 1) * u + pltpu.roll(x, half, 1) * d


def _rope_cache_factors(cs_ref, fac, first_step_of_group):
    @pl.when(first_step_of_group)
    def _():
        for lo in range(0, S, 512):
            for j, f in enumerate(_rope_factors(cs_ref[lo:lo + 512, :])):
                fac[j, lo:lo + 512, :] = f


LB_PER_CB = CB // LANES
ROPE_NB = 3 * DIL_G * LB_PER_CB


def _rope_step(s):
    per_group = 3 * LB_PER_CB
    return s // per_group, (s % per_group) // LB_PER_CB, s % LB_PER_CB


def _rope_split(proj_b, tables):
    def body(x_ref, cs_ref, o_ref, fac):
        g, t, hf = _rope_step(pl.program_id(0))
        _rope_cache_factors(cs_ref, fac, (t == 0) & (hf == 0))
        for gs in range(DIL_G):
            @pl.when(g == gs)
            def _(gs=gs):
                for tok, sub in _residue_pieces(DIL_R[gs]):
                    x = x_ref[tok, :]
                    y = _rope_apply(x, fac[0, sub, :], fac[1, sub, :], fac[2, sub, :], False)
                    o_ref[sub, :] = jnp.where(t < 2, y, x).astype(BF16)

    def in_index(s):
        g, t, hf = _rope_step(s)
        return 0, (t * DIL_G + g) * LB_PER_CB + hf

    def out_index(s):
        g, t, hf = _rope_step(s)
        return t * DIL_G + g, 0, hf

    tab = pl.BlockSpec((None, S, 2 * (ROPE_DIM // 2)), lambda s: (_rope_step(s)[0], 0, 0))
    return pl.pallas_call(
        body, name="rope_split", grid=(ROPE_NB,),
        in_specs=[pl.BlockSpec((S, LANES), in_index), tab],
        out_specs=pl.BlockSpec((None, S, LANES), out_index),
        out_shape=jax.ShapeDtypeStruct((3 * DIL_G, S, CB), BF16),
        scratch_shapes=[pltpu.VMEM((3, S, LANES), F32)],
        compiler_params=_cparams(("arbitrary",)),
    )(proj_b, tables)


def _rope_merge_bwd(dq3, dk3, dv3, tables, dproj):
    def body(dq_ref, dk_ref, dv_ref, cs_ref, _, o_ref, tmp, fac):
        s = pl.program_id(0)
        g, t, hf = _rope_step(s)
        _rope_cache_factors(cs_ref, fac, (t == 0) & (hf == 0) & (s < ROPE_NB))
        for gs in range(DIL_G):
            @pl.when((g == gs) & (s < ROPE_NB))
            def _(gs=gs):
                for tok, sub in _residue_pieces(DIL_R[gs]):
                    x = jnp.where(t == 0, dq_ref[sub, :], jnp.where(t == 1, dk_ref[sub, :], dv_ref[sub, :]))
                    y = _rope_apply(x, fac[0, sub, :], fac[1, sub, :], fac[2, sub, :], True)
                    tmp[tok, :] = jnp.where(t < 2, y, x)
                o_ref[...] = tmp[...].astype(BF16)

        @pl.when(s >= ROPE_NB)
        def _():
            o_ref[...] = jnp.zeros_like(o_ref)

    def src_spec(own):
        def index(s):
            g, t, hf = _rope_step(jnp.minimum(s, ROPE_NB - 1))
            return g, 0, jnp.where(t == own, hf, jnp.where(t > own, LB_PER_CB - 1, 0))
        return pl.BlockSpec((None, S, LANES), index)

    def out_index(s):
        g, t, hf = _rope_step(s)
        col = SEG_B // LANES + (t * DIL_G + g) * LB_PER_CB + hf
        return 0, jnp.where(s < ROPE_NB, col, SEG_Z // LANES + s - ROPE_NB)

    tab = pl.BlockSpec((None, S, 2 * (ROPE_DIM // 2)), lambda s: (_rope_step(jnp.minimum(s, ROPE_NB - 1))[0], 0, 0))
    return pl.pallas_call(
        body, name="rope_merge_bwd", grid=(ROPE_NB + LB_PER_CB,),
        in_specs=[src_spec(0), src_spec(1), src_spec(2), tab, ANY],
        out_specs=pl.BlockSpec((S, LANES), out_index),
        out_shape=jax.ShapeDtypeStruct((S, D_PAD), BF16),
        scratch_shapes=[pltpu.VMEM((S, LANES), F32), pltpu.VMEM((3, S, LANES), F32)],
        input_output_aliases={4: 0},
        compiler_params=_cparams(("arbitrary",)),
    )(dq3, dk3, dv3, tables, dproj)


DIL_NB = S // DIL_BLK


_BQK = (((2,), (2,)), ((0,), (0,)))
_BQD = (((2,), (1,)), ((0,), (0,)))
_BKD = (((1,), (1,)), ((0,), (0,)))


def _dil_mask(g):
    shape = (DIL_NB, DIL_BLK, 2 * DIL_BLK)
    blocks_per_seq = jnp.right_shift(DIL_NB, 2 * g)
    has_prev = jnp.bitwise_and(lax.broadcasted_iota(jnp.int32, shape, 0), blocks_per_seq - 1) != 0
    a = lax.broadcasted_iota(jnp.int32, shape, 1)
    kk = lax.broadcasted_iota(jnp.int32, shape, 2)
    diff = DIL_BLK + a - kk
    return (diff >= 0) & (diff <= DIL_BLK) & ((kk >= DIL_BLK) | has_prev)


def _blocks(t):
    return t.reshape(DIL_NB, DIL_BLK, t.shape[-1])


def _with_prev(t):
    prev = jnp.concatenate([jnp.zeros((DIL_BLK, t.shape[-1]), t.dtype), t[:-DIL_BLK]], axis=0)
    return jnp.concatenate([_blocks(prev), _blocks(t)], axis=1)


def _fold_prev(t2):
    n = t2.shape[-1]
    to_prev = t2[:, :DIL_BLK].reshape(S, n)
    own = t2[:, DIL_BLK:].reshape(S, n)
    return own + jnp.concatenate([to_prev[DIL_BLK:], jnp.zeros((DIL_BLK, n), t2.dtype)], axis=0)


def _dil_group_spec(t, width=DIL_OUT):
    return pl.BlockSpec((None, S, width), lambda g: (t * DIL_G + g, 0, 0))


def _dil_fwd(qkv3):
    def body(q_ref, k_ref, v_ref, o_ref, lse_ref):
        ok = _dil_mask(pl.program_id(0))
        lane = lax.broadcasted_iota(jnp.int32, (S, LANES), 1)
        lse = jnp.zeros((S, LANES), F32)
        for h in range(DIL_HG):
            sl = slice(h * HD, (h + 1) * HD)
            s = jnp.where(ok, _dot(_blocks(q_ref[:, sl]), _with_prev(k_ref[:, sl]), _BQK) * SCALE, NEG_INF)
            m = jnp.max(s, axis=-1, keepdims=True)
            p = jnp.exp(s - m)
            l = jnp.sum(p, axis=-1, keepdims=True)
            o_ref[:, sl] = _dot(p / l, _with_prev(v_ref[:, sl]), _BQD).reshape(S, HD)
            lse = jnp.where(lane == h, (m + jnp.log(l)).reshape(S, 1), lse)
        lse_ref[...] = lse

    return pl.pallas_call(
        body, name="dil_fwd", grid=(DIL_G,),
        in_specs=[_dil_group_spec(0), _dil_group_spec(1), _dil_group_spec(2)],
        out_specs=[_dil_group_spec(0), _dil_group_spec(0, LANES)],
        out_shape=[jax.ShapeDtypeStruct((DIL_G, S, DIL_OUT), F32), jax.ShapeDtypeStruct((DIL_G, S, LANES), F32)],
        compiler_params=_cparams(("parallel",)),
    )(qkv3, qkv3, qkv3)


def _dil_bwd(qkv3, lse3, do3, c3):
    def body(q_ref, k_ref, v_ref, lse_ref, do_ref, c_ref, dq_ref, dk_ref, dv_ref):
        ok = _dil_mask(pl.program_id(0))
        lse = lse_ref[...]
        c = c_ref[...]
        for h in range(DIL_HG):
            sl = slice(h * HD, (h + 1) * HD)
            q = _blocks(q_ref[:, sl])
            k2 = _with_prev(k_ref[:, sl])
            do_h = _blocks(do_ref[:, sl])
            s = jnp.where(ok, _dot(q, k2, _BQK) * SCALE, NEG_INF)
            p = jnp.exp(s - _blocks(lse[:, h:h + 1]))
            dp = _dot(do_h, _with_prev(v_ref[:, sl]), _BQK)
            ds = p * (dp - _blocks(c[:, h:h + 1]))
            dsb = (ds * SCALE).astype(BF16)
            dq_ref[:, sl] = _dot(dsb, k2, _BQD).reshape(S, HD)
            dk_ref[:, sl] = _fold_prev(_dot(dsb, q, _BKD))
            dv_ref[:, sl] = _fold_prev(_dot(p, do_h, _BKD))

    return pl.pallas_call(
        body, name="dil_bwd", grid=(DIL_G,),
        in_specs=[_dil_group_spec(0), _dil_group_spec(1), _dil_group_spec(2), _dil_group_spec(0, LANES),
                  _dil_group_spec(0), _dil_group_spec(0, LANES)],
        out_specs=[_dil_group_spec(0)] * 3,
        out_shape=[jax.ShapeDtypeStruct((DIL_G, S, DIL_OUT), F32)] * 3,
        compiler_params=_cparams(("parallel",)),
    )(qkv3, qkv3, qkv3, lse3, do3, c3)


COMB_T = 256


def _dil_combine(o3, lse3):
    heads_per_lb = LANES // HD

    def body(o_ref, lse_ref, ob_ref, al_ref, *scratch):
        o_tok = [scratch[LB_PER_CB * gg:LB_PER_CB * (gg + 1)] for gg in range(DIL_G)]
        lse_tok = scratch[LB_PER_CB * DIL_G:]
        g = pl.program_id(0)
        for gs in range(DIL_G):
            @pl.when(g == gs)
            def _(gs=gs):
                for tok, sub in _residue_pieces(DIL_R[gs]):
                    for hf in range(LB_PER_CB):
                        o_tok[gs][hf][tok, :] = o_ref[sub, hf * LANES:(hf + 1) * LANES]
                    lse_tok[gs][tok, :] = lse_ref[sub, :]

        @pl.when(g == DIL_G - 1)
        def _():
            def chunk(i, carry):
                rows = pl.ds(pl.multiple_of(i * COMB_T, COMB_T), COMB_T)
                lse = [lse_tok[gg][rows, :] for gg in range(DIL_G)]
                m = jnp.maximum(jnp.maximum(lse[0], lse[1]), lse[2])
                e = [jnp.exp(lse[gg] - m) for gg in range(DIL_G)]
                den = (e[0] + e[1]) + e[2]
                al = [e[gg] / den for gg in range(DIL_G)]
                for gg in range(DIL_G):
                    al_ref[gg, rows, :] = al[gg]
                for h in range(DIL_HG):
                    hf, sl = h // heads_per_lb, slice((h % heads_per_lb) * HD, (h % heads_per_lb + 1) * HD)
                    acc = al[0][:, h:h + 1] * o_tok[0][hf][rows, sl]
                    for gg in range(1, DIL_G):
                        acc = acc + al[gg][:, h:h + 1] * o_tok[gg][hf][rows, sl]
                    ob_ref[rows, h * HD:(h + 1) * HD] = acc
                return carry

            lax.fori_loop(0, S // COMB_T, chunk, 0)

    return pl.pallas_call(
        body, name="dil_combine", grid=(DIL_G,),
        in_specs=[pl.BlockSpec((None, S, DIL_OUT), lambda g: (g, 0, 0)),
                  pl.BlockSpec((None, S, LANES), lambda g: (g, 0, 0))],
        out_specs=[pl.BlockSpec((S, DIL_OUT), lambda g: (0, 0)),
                   pl.BlockSpec((DIL_G, S, LANES), lambda g: (0, 0, 0))],
        out_shape=[jax.ShapeDtypeStruct((S, DIL_OUT), F32), jax.ShapeDtypeStruct((DIL_G, S, LANES), F32)],
        scratch_shapes=[pltpu.VMEM((S, LANES), F32)] * (DIL_G * (LB_PER_CB + 1)),
        compiler_params=_cparams(("arbitrary",)),
    )(o3, lse3)


def _dil_combine_bwd(dob, ob, alpha, deps=()):
    heads_per_lb = LANES // HD

    def body(dob_ref, ob_ref, al_ref, *refs):
        do_ref, c_ref = refs[len(deps):]
        g = pl.program_id(0)
        hf = pl.program_id(1)
        for gs in range(DIL_G):
            @pl.when(g == gs)
            def _(gs=gs):
                for tok, sub in _residue_pieces(DIL_R[gs]):
                    dob = dob_ref[tok, :]
                    prod = dob * ob_ref[tok, :]
                    al = al_ref[tok, :]
                    lane = lax.broadcasted_iota(jnp.int32, al.shape, 1)
                    c = jnp.where(hf == 0, 0.0, c_ref[sub, :])
                    for hh in range(heads_per_lb):
                        sl = slice(hh * HD, (hh + 1) * HD)
                        head = hf * heads_per_lb + hh
                        a = jnp.sum(jnp.where(lane == head, al, 0.0), axis=-1, keepdims=True)
                        do_ref[sub, sl] = (a * dob[:, sl]).astype(BF16)
                        c = jnp.where(lane == head, a * jnp.sum(prod[:, sl], axis=-1, keepdims=True), c)
                    c_ref[sub, :] = c

    half = pl.BlockSpec((S, LANES), lambda g, hf: (0, hf))
    return pl.pallas_call(
        body, name="dil_combine_bwd", grid=(DIL_G, LB_PER_CB),
        in_specs=[half, half, pl.BlockSpec((None, S, LANES), lambda g, hf: (g, 0, 0))] + [ANY] * len(deps),
        out_specs=[pl.BlockSpec((None, S, LANES), lambda g, hf: (g, 0, hf)),
                   pl.BlockSpec((None, S, LANES), lambda g, hf: (g, 0, 0))],
        out_shape=[jax.ShapeDtypeStruct((DIL_G, S, DIL_OUT), BF16), jax.ShapeDtypeStruct((DIL_G, S, LANES), F32)],
        compiler_params=_cparams(("parallel", "arbitrary")),
    )(dob, ob, alpha, *deps)


def _local_step(x, tgt, h1, g_attn, g_mlp, g_final, b_pad, w_in_t, hooks):
    tables = _rope_tables()

    first = hooks.first_deps()
    proj_a = _mm(h1, w_in_t, mode="nt", tm=S, tn=512, n=3 * FOX_W, out_dtypes=[BF16], name="proj_a", deps=first)
    proj_b = _mm(h1, w_in_t, mode="nt", tm=S, tn=DIL_W, n=3 * DIL_W, b_off=SEG_B // DIL_W, out_dtypes=[F32], name="proj_b",
                deps=first)
    proj_g = _mm(h1, w_in_t, mode="nt", tm=S, tn=D, n=2 * D, b_off=SEG_G // D, out_dtypes=[F32], name="proj_g",
                deps=first)
    proj_f = _mm(h1, w_in_t, mode="nt", tm=S, tn=LANES, n=LANES, b_off=SEG_F // LANES, out_dtypes=[F32], name="proj_f",
                deps=first)

    q_aug, k_aug = _fox_prepare(proj_a, proj_f, b_pad)
    oa, lse_a = _fox_fwd(q_aug, k_aug, proj_a)

    qkv3 = _rope_split(proj_b, tables)
    o3, lse3 = _dil_fwd(qkv3)
    ob, alpha = _dil_combine(o3, lse3)

    w_a, w_b, w_out = hooks.mixer_weights(ob)
    ya, yb, mixed, x2, h2 = _mixer_tail(oa, ob, w_a, w_b, w_out, proj_g, x, g_mlp)
    w_up_sh, w_down = hooks.mlp_weights(h2)

    def up_epilogue(acc):
        r = jnp.maximum(acc, 0.0)
        return acc, r * r

    u, act = _mm(h2, w_up_sh, mode="nn", tm=S, tn=DFF // N_DEV, b_sharded=True, out_dtypes=[F32, BF16],
                 name="mlp_up", epilogue=up_epilogue)

    def loss_epilogue(acc, rows, vecs):
        dx3, dg, loss = _final_loss_rows(rows[0] + acc, vecs[0], rows[1])
        return (dx3,), (dg, loss)

    dx3, dg_final, loss = _mm_rows(act, w_down, tm=ROWS_TM, name="mlp_down_loss", epilogue=loss_epilogue,
                                   rows=(x2, tgt), vecs=(g_final,), row_out=(F32,), vec_out=(D, LANES))

    def rms_bwd_epilogue(acc, rows, vecs):
        dx, dg = _rms_bwd_rows(acc, rows[0], rows[1], vecs[0])
        return (dx,), (dg,)

    du = _mm(dx3, w_down, mode="nt", tm=S, tn=512, out_dtypes=[BF16], name="mlp_down_bwd",
             epilogue=lambda acc, u_t: (acc * (2.0 * jnp.maximum(u_t, 0.0)),), extras=(u,))
    dw_down = _mm(act, dx3, mode="tn", tm=512, tn=D, out_dtypes=[F32], name="dw_down")
    dw_up_sh = _mm(h2, du, mode="tn", tm=D, tn=DFF // N_DEV, out_sharded=True, out_dtypes=[F32], name="dw_up")
    dx2, dg_mlp = _mm_rows(du, w_up_sh, b_sharded=True, tm=ROWS_TM, name="mlp_up_bwd", epilogue=rms_bwd_epilogue,
                           rows=(x2, dx3), vecs=(g_mlp,), row_out=(F32,), vec_out=(D,),
                           deps=hooks.mlp_grads(dw_up_sh, dw_down))

    dw_out = _mm(mixed, dx2, mode="tn", tm=D, tn=D, out_dtypes=[F32], name="dw_out")

    dyab, dproj = _out_proj_gate_bwd(dx2, w_out, proj_g, ya, yb)
    doa = _mm(dyab, w_a, mode="nt", tm=S, tn=FOX_W, k=D, a_off=0, out_dtypes=[BF16], name="branch_a_bwd")
    dw_a = _mm(oa, dyab, mode="tn", tm=FOX_W, tn=D, n=D, b_off=0, out_dtypes=[F32], name="dw_branch_a")
    dob = _mm(dyab, w_b, mode="nt", tm=S, tn=CB, k=D, a_off=1, out_dtypes=[F32], name="branch_b_bwd")
    dw_b = _mm(ob, dyab, mode="tn", tm=DIL_OUT, tn=D, n=D, b_off=1, out_dtypes=[F32], name="dw_branch_b")

    dproj, dft, dfs = _fox_bwd(q_aug, k_aug, proj_a, lse_a, doa, dproj, deps=hooks.mixer_grads(dw_a, dw_b, dw_out))
    dproj, db = _fox_dscan(dft, dfs, proj_f, b_pad, dproj)

    do3, c3 = _dil_combine_bwd(dob, ob, alpha, deps=hooks.mid_backward(db))
    dq3, dk3, dv3 = _dil_bwd(qkv3, lse3, do3, c3)
    dproj = _rope_merge_bwd(dq3, dk3, dv3, tables, dproj)

    dw_in_t = _mm(dproj, h1, mode="tn", tm=CB, tn=D, out_dtypes=[F32], name="dw_in")
    dx, dg_attn = _mm_rows(dproj, w_in_t, tm=ROWS_TM // 2, name="proj_bwd", epilogue=rms_bwd_epilogue,
                           rows=(x, dx2), vecs=(g_attn,), row_out=(F32,), vec_out=(D,),
                           deps=hooks.w_in_grad(dw_in_t))

    return loss, dx, (dg_attn, db, dg_mlp, dg_final)


SHARD_SHAPES = ((D_IN // N_DEV, D), (FOX_W, D // N_DEV), (DIL_OUT, D // N_DEV), (D // N_DEV, D),
                (D, DFF // N_DEV), (DFF // N_DEV, D))
W_NAMES = ("w_in", "w_a", "w_b", "w_out", "w_up", "w_down")
AG_COPIES = 7
HBM = pl.BlockSpec(memory_space=pltpu.HBM)
SEM = pl.BlockSpec(memory_space=pltpu.SEMAPHORE)


def _place():
    return lax.axis_index("x"), lax.axis_index("y"), lax.axis_index("c")


W_SLAB = 752
W_WIN = 272
PAD_BLK = 256


def _pad_runs():
    runs = sorted((pad, pad + b - a, p, a) for p in range(N_DEV) for a, b, pad in _shard_pieces(p))
    blocks = []
    for k in range(D_PAD // PAD_BLK):
        lo, hi = k * PAD_BLK, (k + 1) * PAD_BLK
        blocks.append([(max(lo, r0) - lo, min(hi, r1) - lo, p, a + max(lo, r0) - r0)
                       for r0, r1, p, a in runs if max(lo, r0) < min(hi, r1)])
    return blocks


def _gather_w_in(shard, x_in, g_attn, late_shards):
    blocks = _pad_runs()

    norm_t = 512
    n_chunks = S // norm_t

    n_late = len(late_shards)
    late_shapes = [t.shape for t in late_shards]

    def body(x_ref, xin_ref, g_ref, *refs):
        late_in, (out_ref, h1_ref), late_out = refs[:n_late], refs[n_late:n_late + 2], refs[n_late + 2:2 * n_late + 2]
        scratch = refs[2 * n_late + 2:]
        stage, land, xbuf = scratch[:3]
        late_stage, late_cast = scratch[3:3 + n_late], scratch[3 + n_late:3 + 2 * n_late]
        send_sems, recv_sems, load_sem, x_sems, late_sems, store_sems = scratch[3 + 2 * n_late:]
        x, y, c = _place()
        me, sibling = (x, y, c), (x, y, 1 - c)
        chips = [(1 - x, y), (x, 1 - y), (1 - x, 1 - y)]

        def slot(px, py, pc):
            return land.at[4 * px + 2 * py + pc]

        def copy(k, block, to):
            return pltpu.make_async_remote_copy(
                src_ref=slot(*block), dst_ref=slot(*block), send_sem=send_sems.at[k], recv_sem=recv_sems.at[k],
                device_id=to, device_id_type=MESH)

        load = pltpu.make_async_copy(x_ref, stage, load_sem)
        load.start()
        load.wait()
        land[4 * x + 2 * y + c] = stage[...].astype(BF16)
        sent = [copy(0, me, sibling)] + [copy(1 + j, me, (*chip, c)) for j, chip in enumerate(chips)]
        for cp in sent:
            cp.start()

        def x_load(i):
            return pltpu.make_async_copy(xin_ref.at[pl.ds(i * norm_t, norm_t)], xbuf.at[i % 2], x_sems.at[i % 2])

        late_loads = [pltpu.make_async_copy(late_in[i], late_stage[i], late_sems.at[i]) for i in range(n_late)]
        for ld in late_loads:
            ld.start()
        x_load(0).start()
        for i in range(n_chunks):
            if i + 1 < n_chunks:
                x_load(i + 1).start()
            x_load(i).wait()
            h1_ref[i * norm_t:(i + 1) * norm_t, :] = _rms_rows(xbuf[i % 2], g_ref[...]).astype(BF16)
        stores = []
        for i in range(n_late):
            late_loads[i].wait()
            late_cast[i][...] = late_stage[i][...].astype(BF16)
            stores.append(pltpu.make_async_copy(late_cast[i], late_out[i].at[4 * x + 2 * y + c], store_sems.at[i]))
            stores[-1].start()

        for j, chip in enumerate(chips):
            copy(1 + j, (*chip, c), me).wait_recv()
            sent.append(copy(4 + j, (*chip, c), sibling))
            sent[-1].start()
        copy(0, sibling, me).wait_recv()
        for j, chip in enumerate(chips):
            copy(4 + j, (*chip, 1 - c), me).wait_recv()
        for cp in sent:
            cp.wait_send()
        for st in stores:
            st.wait()

        row = lax.broadcasted_iota(jnp.int32, (PAD_BLK, D), 0)
        for k, runs in enumerate(blocks):
            out = jnp.zeros((PAD_BLK, D), F32)
            for o_lo, o_hi, p, a in runs:
                start = min(a // 16 * 16, W_SLAB - W_WIN)
                win = land[p, start:start + W_WIN, :].astype(F32)
                moved = pltpu.roll(win, (o_lo - (a - start)) % W_WIN, 0)[:PAD_BLK]
                out = jnp.where((row >= o_lo) & (row < o_hi), moved, out)
            out_ref[k * PAD_BLK:(k + 1) * PAD_BLK, :] = out.astype(BF16)

    return pl.pallas_call(
        body, name="all_gather_w_in",
        out_shape=[jax.ShapeDtypeStruct((D_PAD, D), BF16), jax.ShapeDtypeStruct((S, D), BF16)]
                  + [jax.ShapeDtypeStruct((N_DEV,) + sh, BF16) for sh in late_shapes],
        in_specs=[ANY, ANY, pl.BlockSpec(memory_space=pltpu.VMEM)] + [ANY] * n_late,
        out_specs=[pl.BlockSpec(memory_space=pltpu.VMEM)] * 2 + [ANY] * n_late,
        scratch_shapes=[pltpu.VMEM((W_SLAB, D), F32), pltpu.VMEM((N_DEV, W_SLAB, D), BF16),
                        pltpu.VMEM((2, norm_t, D), F32)]
                       + [pltpu.VMEM(sh, F32) for sh in late_shapes] + [pltpu.VMEM(sh, BF16) for sh in late_shapes]
                       + [pltpu.SemaphoreType.DMA((AG_COPIES,)), pltpu.SemaphoreType.DMA((AG_COPIES,)),
                          pltpu.SemaphoreType.DMA(()), pltpu.SemaphoreType.DMA((2,)),
                          pltpu.SemaphoreType.DMA((n_late,)), pltpu.SemaphoreType.DMA((n_late,))],
        compiler_params=_cparams(None),
    )(shard, x_in, g_attn, *late_shards)


PEER_COPIES = N_DEV - 1
SIBLING_COPIES = 4
CHIP_COPIES = 3


def _peer_copies(_, land_refs, send_sems, recv_sems):
    x, y, c = _place()
    mine = 4 * x + 2 * y + c
    copies = []
    for i, ref in enumerate(land_refs):
        for k in range(1, N_DEV):
            peer = (x ^ (k >> 2), y ^ ((k >> 1) & 1), c ^ (k & 1))
            n = i * PEER_COPIES + k - 1
            copies.append(pltpu.make_async_remote_copy(
                src_ref=ref.at[mine], dst_ref=ref.at[mine], send_sem=send_sems.at[n], recv_sem=recv_sems.at[n],
                device_id=peer, device_id_type=MESH))
    return copies


def _sibling_copies(g_refs, r_refs, send_sems, recv_sems):
    x, y, c = _place()
    return [pltpu.make_async_remote_copy(
        src_ref=g_refs[i].at[2 * k + (1 - c)], dst_ref=r_refs[i].at[k],
        send_sem=send_sems.at[SIBLING_COPIES * i + k], recv_sem=recv_sems.at[SIBLING_COPIES * i + k],
        device_id=(x, y, 1 - c), device_id_type=MESH) for i in range(len(g_refs)) for k in range(SIBLING_COPIES)]


def _chip_copies(p_refs, r_refs, send_sems, recv_sems):
    x, y, c = _place()
    chips = [(1 - x, y), (x, 1 - y), (1 - x, 1 - y)]
    return [pltpu.make_async_remote_copy(
        src_ref=p_refs[i].at[2 * cx + cy], dst_ref=r_refs[i].at[j],
        send_sem=send_sems.at[CHIP_COPIES * i + j], recv_sem=recv_sems.at[CHIP_COPIES * i + j],
        device_id=(cx, cy, c), device_id_type=MESH) for i in range(len(p_refs)) for j, (cx, cy) in enumerate(chips)]


def _in_hbm(a):
    return pltpu.with_memory_space_constraint(a, pltpu.HBM)


def _exchange_start(name, copies_fn, n_copies, srcs, lands, after=()):
    n_s, n_l, n_a = len(srcs), len(lands), len(after)

    def body(*refs):
        outs = refs[n_s + n_l + n_a:]
        for cp in copies_fn(refs[:n_s], refs[n_s:n_s + n_l], outs[0], outs[1]):
            cp.start()
        outs[-1][...] = jnp.zeros_like(outs[-1])

    res = pl.pallas_call(
        body, name=name,
        out_shape=[pltpu.SemaphoreType.DMA((n_copies,)), pltpu.SemaphoreType.DMA((n_copies,))]
                  + [pltpu.HBM(t.shape, t.dtype) for t in (*srcs, *lands)] + [jax.ShapeDtypeStruct((8, LANES), F32)],
        in_specs=[HBM] * (n_s + n_l) + [ANY] * n_a,
        out_specs=[SEM, SEM] + [HBM] * (n_s + n_l) + [pl.BlockSpec(memory_space=pltpu.VMEM)],
        input_output_aliases={i: 2 + i for i in range(n_s + n_l)},
        compiler_params=pltpu.CompilerParams(has_side_effects=pltpu.SideEffectType.DATAFLOW_SIDE_EFFECTING),
    )(*[_in_hbm(t) for t in (*srcs, *lands)], *after)
    return res[0], res[1], list(res[2:2 + n_s]), list(res[2 + n_s:2 + n_s + n_l]), res[-1]


def _exchange_wait(name, copies_fn, started, after):
    send_sems, recv_sems, srcs, lands, _ = started
    n_s, n_l = len(srcs), len(lands)

    def body(*refs):
        for cp in copies_fn(refs[:n_s], refs[n_s:n_s + n_l], refs[n_s + n_l], refs[n_s + n_l + 1]):
            cp.wait_send()
            cp.wait_recv()

    res = pl.pallas_call(
        body, name=name,
        out_shape=[pltpu.HBM(t.shape, t.dtype) for t in (*srcs, *lands)],
        in_specs=[HBM] * (n_s + n_l) + [SEM, SEM, ANY],
        out_specs=[HBM] * (n_s + n_l),
        input_output_aliases={i: i for i in range(n_s + n_l)},
        compiler_params=pltpu.CompilerParams(has_side_effects=pltpu.SideEffectType.DATAFLOW_SIDE_EFFECTING),
    )(*srcs, *lands, send_sems, recv_sems, after)
    return list(res[:n_s]), list(res[n_s:])


def _shard_block(shape):
    rows, cols = shape
    if rows % 256:
        return rows, 256
    return min(rows, 512 if cols <= LANES else 256), cols


def _chip_partial(ids, g_stack, recv1, name):
    shape = g_stack.shape[1:]
    br, bc = _shard_block(shape)

    def body(ids_ref, g_ref, r_ref, pb_ref, own_ref):
        s = g_ref[...] + r_ref[...]
        pb_ref[...] = s.astype(BF16)

        @pl.when(pl.program_id(2) == ids_ref[1])
        def _():
            own_ref[...] = s

    grid_spec = pltpu.PrefetchScalarGridSpec(
        num_scalar_prefetch=1, grid=(shape[0] // br, shape[1] // bc, 4),
        in_specs=[pl.BlockSpec((None, br, bc), lambda r, q, k, ids: (2 * k + ids[0], r, q)),
                  pl.BlockSpec((None, br, bc), lambda r, q, k, ids: (k, r, q))],
        out_specs=[pl.BlockSpec((None, br, bc), lambda r, q, k, ids: (k, r, q)),
                   pl.BlockSpec((br, bc), lambda r, q, k, ids: (r, q))])
    return pl.pallas_call(
        body, name=name, grid_spec=grid_spec,
        out_shape=[jax.ShapeDtypeStruct((4,) + shape, BF16), jax.ShapeDtypeStruct(shape, F32)],
        compiler_params=_cparams(("parallel", "parallel", "arbitrary")),
    )(ids, g_stack, recv1)


def _adamw(w, g, m, v):
    m = ADAM_B1 * m + (1.0 - ADAM_B1) * g
    v = ADAM_B2 * v + (1.0 - ADAM_B2) * (g * g)
    m_hat = m / (1.0 - ADAM_B1 ** ADAM_STEP)
    v_hat = v / (1.0 - ADAM_B2 ** ADAM_STEP)
    delta = -ADAM_LR * (m_hat / (jnp.sqrt(v_hat) + ADAM_EPS) + ADAM_WD * w)
    return delta, m, v


def _reduce_adamw(own, recv2, w, m, v, name, deps=()):
    shape = own.shape
    br, bc = _shard_block(shape)

    def body(own_ref, r_ref, w_ref, m_ref, v_ref, *refs):
        g_ref, d_ref, nm_ref, nv_ref = refs[len(deps):]
        g = own_ref[...]
        for j in range(3):
            g = g + r_ref[j].astype(F32)
        delta, nm, nv = _adamw(w_ref[...], g, m_ref[...], v_ref[...])
        g_ref[...] = g
        d_ref[...] = delta
        nm_ref[...] = nm
        nv_ref[...] = nv

    blk = pl.BlockSpec((br, bc), lambda r, q: (r, q))
    return pl.pallas_call(
        body, name=name, grid=(shape[0] // br, shape[1] // bc),
        in_specs=[blk, pl.BlockSpec((3, br, bc), lambda r, q: (0, r, q)), blk, blk, blk] + [ANY] * len(deps),
        out_specs=[blk] * 4, out_shape=[jax.ShapeDtypeStruct(shape, F32)] * 4,
        compiler_params=_cparams(("parallel", "parallel")),
    )(own, recv2, w, m, v, *deps)


def _small_allreduce_adamw(gvec, w, m, v):
    def body(g_ref, w_ref, m_ref, v_ref, go_ref, d_ref, nm_ref, nv_ref, buf, send_sems, recv_sems):
        x, y, c = _place()
        my_slot = 4 * x + 2 * y + c
        buf[my_slot] = g_ref[...]
        copies = []
        for k in range(1, N_DEV):
            px, py, pc = x ^ (k >> 2), y ^ ((k >> 1) & 1), c ^ (k & 1)
            copies.append(pltpu.make_async_remote_copy(
                src_ref=g_ref, dst_ref=buf.at[my_slot], send_sem=send_sems.at[k - 1], recv_sem=recv_sems.at[k - 1],
                device_id=(px, py, pc), device_id_type=MESH))
        for cp in copies:
            cp.start()
        for k in range(1, N_DEV):
            px, py, pc = x ^ (k >> 2), y ^ ((k >> 1) & 1), c ^ (k & 1)
            pltpu.make_async_remote_copy(
                src_ref=g_ref, dst_ref=buf.at[4 * px + 2 * py + pc], send_sem=send_sems.at[k - 1],
                recv_sem=recv_sems.at[k - 1], device_id=(px, py, pc), device_id_type=MESH).wait_recv()
        for cp in copies:
            cp.wait_send()
        g = buf[0]
        for s in range(1, N_DEV):
            g = g + buf[s]
        delta, nm, nv = _adamw(w_ref[...], g, m_ref[...], v_ref[...])
        go_ref[...] = g
        d_ref[...] = delta
        nm_ref[...] = nm
        nv_ref[...] = nv

    vm = pl.BlockSpec(memory_space=pltpu.VMEM)
    return pl.pallas_call(
        body, name="small_allreduce_adamw",
        in_specs=[vm] * 4, out_specs=[vm] * 4,
        out_shape=[jax.ShapeDtypeStruct((SMALL_R, LANES), F32)] * 4,
        scratch_shapes=[pltpu.VMEM((N_DEV, SMALL_R, LANES), F32),
                        pltpu.SemaphoreType.DMA((N_DEV - 1,)), pltpu.SemaphoreType.DMA((N_DEV - 1,))],
    )(gvec, w, m, v)


def _cols_to_whole(stack):
    rows = stack.shape[1]
    return stack.transpose(1, 0, 2).reshape(rows, -1)


def _whole_to_cols(t):
    rows = t.shape[0]
    return t.reshape(rows, N_DEV, -1).transpose(1, 0, 2)


W_IN_SEGMENTS = ((0, 3 * FOX_W, SEG_A), (3 * FOX_W, 3 * FOX_W + FOX_H, SEG_F),
                 (3 * FOX_W + FOX_H, 3 * FOX_W + FOX_H + 3 * DIL_W, SEG_B), (3 * FOX_W + FOX_H + 3 * DIL_W, D_IN, SEG_G))


def _shard_pieces(p):
    rows = D_IN // N_DEV
    lo, hi = p * rows, (p + 1) * rows
    return [(max(lo, a) - lo, min(hi, b) - lo, pad + max(lo, a) - a)
            for a, b, pad in W_IN_SEGMENTS if max(lo, a) < min(hi, b)]


def _unpad_dw_in_t(dwp):
    return jnp.stack([jnp.concatenate([dwp[pad:pad + b - a] for a, b, pad in _shard_pieces(p)], axis=0)
                      for p in range(N_DEV)])


LOSS_ROW = 25


def _pack_small(g_attn, b, g_mlp, g_final, loss_row=None):
    tail = jnp.pad(b, ((0, 7), (0, LANES - b.shape[1])))
    if loss_row is not None:
        tail = tail + jnp.pad(loss_row, ((LOSS_ROW - 24, 31 - LOSS_ROW), (0, 0)))
    return jnp.concatenate([g_attn.reshape(8, LANES), g_mlp.reshape(8, LANES), g_final.reshape(8, LANES), tail], axis=0)


def _unpack_small(p):
    return (p[0:8].reshape(1, D), p[24:25, :FOX_H], p[8:16].reshape(1, D), p[16:24].reshape(D))


class _StepComm:
    def __init__(self, ids, w_sh, m_sh, v_sh, lands, after_w_in):
        self.ids = ids
        self.w_sh, self.m_sh, self.v_sh = w_sh, m_sh, v_sh
        self.updates = None
        self.gather_mixer = _exchange_start("gather_mixer_start", _peer_copies, PEER_COPIES * 3, [], lands[:3],
                                            after=(after_w_in,))
        self.gather_mlp = _exchange_start("gather_mlp_start", _peer_copies, PEER_COPIES * 2, [], lands[3:],
                                          after=(self.gather_mixer[-1],))
        self.sibling = {}
        self.chips = {}
        self.own = {}
        self.names = {}

    def _reduce_start(self, group, names, grads, after=()):
        lands = [lax.empty((SIBLING_COPIES,) + t.shape[1:], F32) for t in grads]
        self.sibling[group] = _exchange_start("grad_%s_sibling_start" % group, _sibling_copies,
                                              SIBLING_COPIES * len(grads), grads, lands, after=after)
        self.names[group] = names
        return self.sibling[group][-1]

    def _reduce_mid(self, group, after):
        grads, recv1 = _exchange_wait("grad_%s_sibling_wait" % group, _sibling_copies, self.sibling[group], after)
        parts = [_chip_partial(self.ids, g, r, "grad_partial_" + n) for g, r, n in zip(grads, recv1, self.names[group])]
        self.own[group] = [p[1] for p in parts]
        srcs = [p[0] for p in parts]
        lands = [lax.empty((CHIP_COPIES,) + t.shape[1:], BF16) for t in srcs]
        self.chips[group] = _exchange_start("grad_%s_chips_start" % group, _chip_copies, CHIP_COPIES * len(srcs),
                                            srcs, lands)
        return self.chips[group][-1]

    def reduced(self, group, after):
        _, recv2 = _exchange_wait("grad_%s_chips_wait" % group, _chip_copies, self.chips[group], after)
        return list(zip(self.own[group], recv2))

    def first_deps(self):
        return [self.gather_mlp[-1]]

    def mixer_weights(self, after):
        _, (g_a, g_b, g_out) = _exchange_wait("gather_mixer_wait", _peer_copies, self.gather_mixer, after)
        return _cols_to_whole(g_a), _cols_to_whole(g_b), g_out.reshape(D, D)

    def mlp_weights(self, after):
        _, (g_up, g_down) = _exchange_wait("gather_mlp_wait", _peer_copies, self.gather_mlp, after)
        return g_up, g_down.reshape(DFF, D)

    def mlp_grads(self, dw_up_sh, dw_down):
        return [self._reduce_start("mlp", W_NAMES[4:], [dw_up_sh, dw_down.reshape((N_DEV,) + SHARD_SHAPES[5])])]

    def mixer_grads(self, dw_a, dw_b, dw_out):
        token = self._reduce_mid("mlp", dw_b)
        grads = [_whole_to_cols(dw_a), _whole_to_cols(dw_b), dw_out.reshape((N_DEV,) + SHARD_SHAPES[3])]
        return [self._reduce_start("mixer", W_NAMES[1:4], grads, after=(token,))]

    def mid_backward(self, after):
        return [self._reduce_mid("mixer", after)]

    def w_in_grad(self, dw_in_t):
        grads = [_unpad_dw_in_t(dw_in_t)]
        token = self._reduce_start("w_in", W_NAMES[:1], grads)
        reduced = self.reduced("mixer", token) + self.reduced("mlp", token)
        self.updates, last = [None] * len(reduced), token
        for i in (3, 4, 0, 1, 2):
            if i == 0:
                last = self._reduce_mid("w_in", last)
            self.updates[i] = _reduce_adamw(*reduced[i], self.w_sh[1 + i], self.m_sh[1 + i], self.v_sh[1 + i],
                                            "adamw_" + W_NAMES[1 + i], deps=[last])
            last = self.updates[i][0]
        return [last]

    def w_in_update(self, after):
        (reduced,) = self.reduced("w_in", after)
        return _reduce_adamw(*reduced, self.w_sh[0], self.m_sh[0], self.v_sh[0], "adamw_" + W_NAMES[0])


def kernel(x, norm_attn_g, w_in, b_forget, w_branch_a, w_branch_b, w_out, norm_mlp_g, w_up, w_down, norm_final_g, loss_target, m_norm_attn_g, m_w_in, m_b_forget, m_w_branch_a, m_w_branch_b, m_w_out, m_norm_mlp_g, m_w_up, m_w_down, m_norm_final_g, v_norm_attn_g, v_w_in, v_b_forget, v_w_branch_a, v_w_branch_b, v_w_out, v_norm_mlp_g, v_w_up, v_w_down, v_norm_final_g):
    cx, cy, cc = _place()
    ids = jnp.stack([cc, 2 * cx + cy]).astype(jnp.int32)

    w_sh = [w_in[0].T] + [t[0] for t in (w_branch_a, w_branch_b, w_out, w_up, w_down)]
    m_sh = [m_w_in[0].T] + [t[0] for t in (m_w_branch_a, m_w_branch_b, m_w_out, m_w_up, m_w_down)]
    v_sh = [v_w_in[0].T] + [t[0] for t in (v_w_branch_a, v_w_branch_b, v_w_out, v_w_up, v_w_down)]

    w_in_t, h1, *lands = _gather_w_in(jnp.pad(w_sh[0], ((0, W_SLAB - w_sh[0].shape[0]), (0, 0))), x[0], norm_attn_g,
                                      w_sh[1:])
    comm = _StepComm(ids, w_sh, m_sh, v_sh, lands, w_in_t)
    b_pad = jnp.pad(b_forget, ((0, 0), (0, LANES - FOX_H)))

    loss_row, dx, dsmall = _local_step(
        x[0], loss_target[0], h1, norm_attn_g, norm_mlp_g, norm_final_g.reshape(1, D), b_pad, w_in_t, comm)

    dg_attn, db, dg_mlp, dg_final = dsmall
    small = _small_allreduce_adamw(
        _pack_small(dg_attn, db, dg_mlp, dg_final, loss_row),
        _pack_small(norm_attn_g, b_forget, norm_mlp_g, norm_final_g.reshape(1, D)),
        _pack_small(m_norm_attn_g, m_b_forget, m_norm_mlp_g, m_norm_final_g.reshape(1, D)),
        _pack_small(v_norm_attn_g, v_b_forget, v_norm_mlp_g, v_norm_final_g.reshape(1, D)))
    big = [comm.w_in_update(small[0])] + comm.updates

    outs = [small[0][LOSS_ROW, 0], dx[None]]
    for q in range(4):
        s_attn, s_b, s_mlp, s_final = _unpack_small(small[q])
        b_in, b_a, b_b, b_out, b_up, b_down = [t[q][None] for t in big]
        b_in = jnp.swapaxes(b_in, 1, 2)
        outs += [s_attn, b_in, s_b, b_a, b_b, b_out, s_mlp, b_up, b_down, s_final]
    return tuple(outs)
```

```python
import functools

import jax
import jax.numpy as jnp
from jax import lax
from jax.experimental import pallas as pl
from jax.experimental.pallas import tpu as pltpu

F32 = jnp.float32
BF16 = jnp.bfloat16
MESH = pl.DeviceIdType.MESH

S = 2048
D = 1024
HD = 64
FOX_H = 8
FOX_W = FOX_H * HD
DIL_HG = 4
DIL_G = 3
DIL_W = DIL_G * DIL_HG * HD
DIL_OUT = DIL_HG * HD
DIL_BLK = 128
DIL_R = (1, 4, 16)
DFF = 4 * D
D_IN = 3 * FOX_W + FOX_H + 3 * DIL_W + 2 * D
EPS = 1e-6
NEG_INF = -1e30
SCALE = HD ** -0.5
ROPE_THETA = 500000.0
ROPE_DIM = HD // 4
N_DEV = 8

ADAM_LR = 0.001
ADAM_B1 = 0.9
ADAM_B2 = 0.999
ADAM_EPS = 1e-08
ADAM_WD = 0.01
ADAM_STEP = 10

LANES = 128
CB = 256
SEG_A = 0
SEG_B = 3 * FOX_W
SEG_Z = SEG_B + 3 * DIL_W
SEG_G = SEG_Z + CB
SEG_F = SEG_G + 2 * D
D_PAD = SEG_F + CB
DW_IN_TM = D_PAD // 10
SMALL_R = 32

VMEM_MB = 56


def _cparams(dims=None, vmem_mb=VMEM_MB, **kw):
    return pltpu.CompilerParams(dimension_semantics=dims, vmem_limit_bytes=vmem_mb << 20, **kw)


ANY = pl.BlockSpec(memory_space=pl.ANY)

_NN = (((1,), (0,)), ((), ()))
_NT = (((1,), (1,)), ((), ()))
_TN = (((0,), (0,)), ((), ()))


def _dot(a, b, dims):
    return lax.dot_general(a.astype(BF16), b.astype(BF16), dims, preferred_element_type=F32)


def _mm(a, b, *, mode, tm, tn, out_dtypes, name, n=None, k=None, a_off=0, b_off=0,
        b_sharded=False, out_sharded=False, epilogue=None, extras=(), deps=()):
    n_sh = b.shape[-1] if b_sharded else None
    if mode == "nn":
        m = a.shape[0]
        k = k or a.shape[1]
        a_spec = pl.BlockSpec((tm, k), lambda i, j: (i, a_off))
        if b_sharded:
            assert tn == n_sh
            n = N_DEV * n_sh
            b_spec = pl.BlockSpec((None, k, tn), lambda i, j: (j, 0, 0))
        else:
            n = n or b.shape[1]
            b_spec = pl.BlockSpec((k, tn), lambda i, j: (0, j + b_off))
        dims = _NN
    elif mode == "nt":
        m = a.shape[0]
        k = k or a.shape[1]
        a_spec = pl.BlockSpec((tm, k), lambda i, j: (i, a_off))
        if b_sharded:
            n = b.shape[1]
            b_spec = pl.BlockSpec((N_DEV, tn, n_sh), lambda i, j: (0, j, 0))
        else:
            n = n or b.shape[0]
            b_spec = pl.BlockSpec((tn, k), lambda i, j: (j + b_off, 0))
        dims = _NT
    else:
        k, m = a.shape
        n = n or b.shape[1]
        a_spec = pl.BlockSpec((k, tm), lambda i, j: (0, i))
        b_spec = pl.BlockSpec((k, tn), lambda i, j: (0, j + b_off))
        dims = _TN
    assert m % tm == 0 and n % tn == 0, (name, m, n, tm, tn)
    n_extra = len(extras)
    tile = pl.BlockSpec((tm, tn), lambda i, j: (i, j))
    if out_sharded:
        assert mode == "tn" and n // tn == N_DEV
        out_spec = pl.BlockSpec((None, tm, tn), lambda i, j: (j, i, 0))
        out_shape = (N_DEV, m, tn)
    else:
        out_spec, out_shape = tile, (m, n)

    def body(a_ref, b_ref, *refs):
        if mode == "nt" and b_sharded:
            acc = _dot(a_ref[:, 0:n_sh], b_ref[0], dims)
            for p in range(1, N_DEV):
                acc = acc + _dot(a_ref[:, p * n_sh:(p + 1) * n_sh], b_ref[p], dims)
        else:
            acc = _dot(a_ref[...], b_ref[...], dims)
        ex = [r[...] for r in refs[:n_extra]]
        outs = epilogue(acc, *ex) if epilogue is not None else (acc,)
        for o_ref, o in zip(refs[n_extra + len(deps):], outs):
            o_ref[...] = o.astype(o_ref.dtype)

    res = pl.pallas_call(
        body, name=name, grid=(m // tm, n // tn),
        in_specs=[a_spec, b_spec] + [tile] * n_extra + [ANY] * len(deps),
        out_specs=[out_spec] * len(out_dtypes),
        out_shape=[jax.ShapeDtypeStruct(out_shape, dt) for dt in out_dtypes],
        compiler_params=_cparams(("parallel", "parallel")),
    )(a, b, *extras, *deps)
    return res if len(out_dtypes) > 1 else res[0]


def _mm_rows(a, b, *, tm, name, epilogue, rows=(), vecs=(), row_out=(), vec_out=(), b_sharded=False, deps=()):
    m, k = a.shape
    n_rows, n_vecs, n_deps = len(rows), len(vecs), len(deps)
    n_sh = b.shape[-1] if b_sharded else None

    def body(a_ref, b_ref, *refs):
        row_refs, vec_refs = refs[:n_rows], refs[n_rows:n_rows + n_vecs]
        outs = refs[n_rows + n_vecs + n_deps:]
        if b_sharded:
            acc = _dot(a_ref[:, 0:n_sh], b_ref[0], _NT)
            for p in range(1, N_DEV):
                acc = acc + _dot(a_ref[:, p * n_sh:(p + 1) * n_sh], b_ref[p], _NT)
        else:
            acc = _dot(a_ref[...], b_ref[...], _NN)
        row_vals, vec_incs = epilogue(acc, [r[...] for r in row_refs], [v[...] for v in vec_refs])
        for o_ref, val in zip(outs[:len(row_out)], row_vals):
            o_ref[...] = val.astype(o_ref.dtype)

        @pl.when(pl.program_id(0) == 0)
        def _():
            for o_ref in outs[len(row_out):]:
                o_ref[...] = jnp.zeros_like(o_ref)

        for o_ref, inc in zip(outs[len(row_out):], vec_incs):
            o_ref[...] += inc

    tile = pl.BlockSpec((tm, D), lambda i: (i, 0))
    b_spec = pl.BlockSpec(b.shape, lambda i: (0,) * b.ndim)
    return pl.pallas_call(
        body, name=name, grid=(m // tm,),
        in_specs=[pl.BlockSpec((tm, k), lambda i: (i, 0)), b_spec] + [tile] * n_rows
                 + [pl.BlockSpec(v.shape, lambda i: (0, 0)) for v in vecs] + [ANY] * n_deps,
        out_specs=[tile] * len(row_out) + [pl.BlockSpec((1, w), lambda i: (0, 0)) for w in vec_out],
        out_shape=[jax.ShapeDtypeStruct((m, D), dt) for dt in row_out]
                  + [jax.ShapeDtypeStruct((1, w), F32) for w in vec_out],
        compiler_params=_cparams(("arbitrary",)),
    )(a, b, *rows, *vecs, *deps)


ROW_T = 256
ROWS_TM = 512


def _rms_rows(x, g):
    r = lax.rsqrt(jnp.mean(x * x, axis=-1, keepdims=True) + EPS)
    return (x * r) * g


def _rms_bwd_rows(dh, x, dres, g):
    r = lax.rsqrt(jnp.mean(x * x, axis=-1, keepdims=True) + EPS)
    xn = x * r
    dhn = dh * g
    dx = dres + r * (dhn - xn * jnp.mean(dhn * xn, axis=-1, keepdims=True))
    return dx, jnp.sum(dh * xn, axis=0, keepdims=True)


def _final_loss_rows(x, g, tgt):
    r = lax.rsqrt(jnp.mean(x * x, axis=-1, keepdims=True) + EPS)
    xn = x * r
    err = xn * g - tgt
    row_loss = jnp.mean(err * err, axis=-1, keepdims=True)
    loss = 0.5 * jnp.sum(row_loss, axis=0, keepdims=True) * jnp.ones((1, LANES), F32)
    dy = err * (1.0 / D)
    dyn = dy * g
    dx = r * (dyn - xn * jnp.mean(dyn * xn, axis=-1, keepdims=True))
    return dx, jnp.sum(dy * xn, axis=0, keepdims=True), loss


def _sigmoid(z):
    return 1.0 / (1.0 + jnp.exp(-z))


GATE_TR = 512


def _mixer_tail(oa, ob, w_a, w_b, w_out, proj_g, x, g_mlp):
    def body(oa_ref, ob_ref, wa_ref, wb_ref, wo_ref, g_ref, x_ref, gm_ref, ya_ref, yb_ref, mixed_ref, x2_ref, h2_ref):
        ya = _dot(oa_ref[...], wa_ref[...], _NN)
        yb = _dot(ob_ref[...], wb_ref[...], _NN)
        ya_ref[...] = ya
        yb_ref[...] = yb
        mixed = (_sigmoid(g_ref[:, :D]) * ya + _sigmoid(g_ref[:, D:]) * yb).astype(BF16)
        mixed_ref[...] = mixed
        x2 = x_ref[...] + _dot(mixed, wo_ref[...], _NN)
        x2_ref[...] = x2
        h2_ref[...] = _rms_rows(x2, gm_ref[...]).astype(BF16)

    def rows(width):
        return pl.BlockSpec((GATE_TR, width), lambda i: (i, 0))

    def whole(t):
        return pl.BlockSpec(t.shape, lambda i: (0, 0))

    return pl.pallas_call(
        body, name="mixer_tail", grid=(S // GATE_TR,),
        in_specs=[rows(FOX_W), rows(DIL_OUT), whole(w_a), whole(w_b), whole(w_out), rows(2 * D), rows(D), whole(g_mlp)],
        out_specs=[rows(D)] * 5,
        out_shape=[jax.ShapeDtypeStruct((S, D), dt) for dt in (F32, F32, BF16, F32, BF16)],
        compiler_params=_cparams(("parallel",)),
    )(oa, ob, w_a, w_b, w_out, proj_g, x, g_mlp)


def _out_proj_gate_bwd(dx2, w_out, proj_g, ya, yb):
    def body(dx_ref, w_ref, g_ref, ya_ref, yb_ref, dy_ref, dg_ref):
        dm = _dot(dx_ref[...], w_ref[...], _NT)
        for half, y_ref in enumerate((ya_ref, yb_ref)):
            cols = slice(half * D, (half + 1) * D)
            s = _sigmoid(g_ref[:, cols])
            dy_ref[:, cols] = (dm * s).astype(BF16)
            dg_ref[:, cols] = (dm * y_ref[...] * (s * (1.0 - s))).astype(BF16)

    row = pl.BlockSpec((GATE_TR, D), lambda i: (i, 0))
    wide = pl.BlockSpec((GATE_TR, 2 * D), lambda i: (i, 0))
    return pl.pallas_call(
        body, name="out_proj_gate_bwd", grid=(S // GATE_TR,),
        in_specs=[row, pl.BlockSpec((D, D), lambda i: (0, 0)), wide, row, row],
        out_specs=[wide, pl.BlockSpec((GATE_TR, 2 * D), lambda i: (i, SEG_G // (2 * D)))],
        out_shape=[jax.ShapeDtypeStruct((S, 2 * D), BF16), jax.ShapeDtypeStruct((S, D_PAD), BF16)],
        compiler_params=_cparams(("parallel",)),
    )(dx2, w_out, proj_g, ya, yb)


FOX_TQ = 256


def _scan_rows(x, reverse):
    n = x.shape[0]
    row = lax.broadcasted_iota(jnp.int32, x.shape, 0)
    k = 1
    while k < n:
        if reverse:
            x = x + jnp.where(row < n - k, pltpu.roll(x, n - k, 0), 0.0)
        else:
            x = x + jnp.where(row >= k, pltpu.roll(x, k, 0), 0.0)
        k *= 2
    return x


def _fox_dscan(dft, dfs, proj_f, b_pad, dproj):
    def body(dft_ref, dfs_ref, f_ref, b_ref, _, dfa_ref, db_ref):
        dfs_pad = jnp.concatenate([dfs_ref[...], jnp.zeros((LANES - FOX_H, S), F32)], axis=0)
        df = dfs_pad.T + dft_ref[0]
        for hp in range(1, dft_ref.shape[0]):
            df = df + pltpu.roll(dft_ref[hp], 2 * hp, 1)
        dlf = _scan_rows(df, reverse=True)
        z = f_ref[...] + b_ref[...]
        lane = lax.broadcasted_iota(jnp.int32, (S, LANES), 1)
        dfa = jnp.where(lane < FOX_H, dlf / (1.0 + jnp.exp(z)), 0.0)
        dfa_ref[:, :LANES] = dfa.astype(BF16)
        dfa_ref[:, LANES:] = jnp.zeros((S, CB - LANES), BF16)
        db_ref[...] = jnp.sum(dfa, axis=0, keepdims=True)

    return pl.pallas_call(
        body, name="fox_dscan", grid=(1,),
        in_specs=[pl.BlockSpec(dft.shape, lambda i: (0, 0, 0)), pl.BlockSpec((FOX_H, S), lambda i: (0, 0)),
                  pl.BlockSpec((S, LANES), lambda i: (0, 0)), pl.BlockSpec((1, LANES), lambda i: (0, 0)), ANY],
        out_specs=[pl.BlockSpec((S, CB), lambda i: (0, SEG_F // CB)), pl.BlockSpec((1, LANES), lambda i: (0, 0))],
        out_shape=[jax.ShapeDtypeStruct((S, D_PAD), BF16), jax.ShapeDtypeStruct((1, LANES), F32)],
        input_output_aliases={4: 0},
        compiler_params=_cparams(("arbitrary",)),
    )(dft, dfs, proj_f, b_pad, dproj)


FOX_NQ = S // FOX_TQ
FOX_HP = FOX_W // LANES
HEADS_PER_LB = LANES // HD
FOX_AUG = 2 * LANES


def _fox_prepare(proj_a, proj_f, b_pad):
    tr = 512

    def body(q_ref, k_ref, f_ref, b_ref, qa_ref, ka_ref, hi_s, mid_s, lo_s):
        i = pl.program_id(0)

        @pl.when(i == 0)
        def _():
            z = f_ref[...] + b_ref[...]
            lf = jnp.minimum(z, 0.0) - jnp.log1p(jnp.exp(-jnp.abs(z)))
            f = _scan_rows(lf, reverse=False)
            hi = f.astype(BF16).astype(F32)
            r1 = f - hi
            mid = r1.astype(BF16).astype(F32)
            hi_s[...] = hi
            mid_s[...] = mid
            lo_s[...] = (r1 - mid).astype(BF16).astype(F32)

        rows = pl.ds(pl.multiple_of(i * tr, tr), tr)
        hi, mid, lo = hi_s[rows, :], mid_s[rows, :], lo_s[rows, :]
        lane = lax.broadcasted_iota(jnp.int32, (tr, HD), 1)
        q_ones = jnp.where((lane >= 3) & (lane < 6), 1.0, 0.0)
        k_ones = jnp.where(lane < 3, 1.0, 0.0)
        for h in range(FOX_H):
            a1, a2, a3 = hi[:, h:h + 1], mid[:, h:h + 1], lo[:, h:h + 1]
            q_extra = jnp.where(lane == 0, a1, jnp.where(lane == 1, a2, jnp.where(lane == 2, a3, q_ones)))
            k_extra = jnp.where(lane == 3, -a1, jnp.where(lane == 4, -a2, jnp.where(lane == 5, -a3, k_ones)))
            base = h * LANES
            qa_ref[:, base:base + HD] = q_ref[:, h * HD:(h + 1) * HD] * jnp.asarray(SCALE, BF16)
            qa_ref[:, base + HD:base + LANES] = q_extra.astype(BF16)
            ka_ref[:, base:base + HD] = k_ref[:, h * HD:(h + 1) * HD]
            ka_ref[:, base + HD:base + LANES] = k_extra.astype(BF16)

    out_blk = pl.BlockSpec((tr, FOX_H * LANES), lambda i: (i, 0))
    return pl.pallas_call(
        body, name="fox_prepare", grid=(S // tr,),
        in_specs=[pl.BlockSpec((tr, FOX_W), lambda i: (i, 0)), pl.BlockSpec((tr, FOX_W), lambda i: (i, 1)),
                  pl.BlockSpec((S, LANES), lambda i: (0, 0)), pl.BlockSpec((1, LANES), lambda i: (0, 0))],
        out_specs=[out_blk, out_blk],
        out_shape=[jax.ShapeDtypeStruct((S, FOX_H * LANES), BF16)] * 2,
        scratch_shapes=[pltpu.VMEM((S, LANES), F32)] * 3,
        compiler_params=_cparams(("arbitrary",)),
    )(proj_a, proj_a, proj_f, b_pad)


def _fox_scores(qa, ka, w):
    s = _dot(qa, ka, _NT)
    row = lax.broadcasted_iota(jnp.int32, (FOX_TQ, FOX_TQ), 0)
    col = lax.broadcasted_iota(jnp.int32, (FOX_TQ, FOX_TQ), 1)
    diag = jnp.where(col <= row, s[:, w * FOX_TQ:], NEG_INF)
    return diag if w == 0 else jnp.concatenate([s[:, :w * FOX_TQ], diag], axis=1)


def _fox_fwd(q_aug, k_aug, proj_a):
    def body(qa_ref, ka_ref, v_ref, o_ref, lse_ref):
        qi = pl.program_id(1)
        lane = lax.broadcasted_iota(jnp.int32, (FOX_TQ, LANES), 1)
        for w in range(FOX_NQ):
            @pl.when(qi == w)
            def _(w=w):
                width = (w + 1) * FOX_TQ
                lse = jnp.zeros((FOX_TQ, LANES), F32)
                for hh in range(HEADS_PER_LB):
                    aug = slice(hh * LANES, (hh + 1) * LANES)
                    sl = slice(hh * HD, (hh + 1) * HD)
                    s = _fox_scores(qa_ref[:, aug], ka_ref[:width, aug], w)
                    m = jnp.max(s, axis=-1, keepdims=True)
                    p = jnp.exp(s - m)
                    l = jnp.sum(p, axis=-1, keepdims=True)
                    o_ref[:, sl] = _dot(p / l, v_ref[:width, sl], _NN)
                    lse = jnp.where(lane == hh, m + jnp.log(l), lse)
                lse_ref[...] = lse

    return pl.pallas_call(
        body, name="fox_fwd", grid=(FOX_HP, FOX_NQ),
        in_specs=[pl.BlockSpec((FOX_TQ, FOX_AUG), lambda h, i: (i, h)),
                  pl.BlockSpec((S, FOX_AUG), lambda h, i: (0, h)),
                  pl.BlockSpec((S, LANES), lambda h, i: (0, 2 * FOX_HP + h))],
        out_specs=[pl.BlockSpec((FOX_TQ, LANES), lambda h, i: (i, h)),
                   pl.BlockSpec((None, FOX_TQ, LANES), lambda h, i: (h, i, 0))],
        out_shape=[jax.ShapeDtypeStruct((S, FOX_W), F32), jax.ShapeDtypeStruct((FOX_HP, S, LANES), F32)],
        compiler_params=_cparams(("parallel", "parallel")),
    )(q_aug, k_aug, proj_a)


def _fox_bwd(q_aug, k_aug, proj_a, lse, do, dproj, deps=()):
    n_q = FOX_NQ

    def body(qa_ref, ka_ref, v_ref, lse_ref, do_ref, *refs):
        dproj_ref, dft_ref, dfs_ref, dk_acc, dv_acc, dq_buf, kv_buf, dq_sems, kv_sems = refs[1 + len(deps):]
        hp = pl.program_id(0)
        t = pl.program_id(1)
        slot = t % 2
        col = pl.multiple_of(hp * LANES, LANES)

        def dq_copy(step, buf_slot):
            rows = pl.ds(pl.multiple_of(step * FOX_TQ, FOX_TQ), FOX_TQ)
            return pltpu.make_async_copy(dq_buf.at[buf_slot], dproj_ref.at[rows, pl.ds(col, LANES)], dq_sems.at[buf_slot])

        @pl.when(t == 0)
        def _():
            dk_acc[...] = jnp.zeros_like(dk_acc)
            dv_acc[...] = jnp.zeros_like(dv_acc)

        @pl.when((t == 0) & (hp == 0))
        def _():
            dfs_ref[...] = jnp.zeros_like(dfs_ref)

        @pl.when(t >= 2)
        def _():
            dq_copy(t - 2, slot).wait()

        lane = lax.broadcasted_iota(jnp.int32, (FOX_TQ, LANES), 1)
        for w in range(n_q):
            @pl.when(t == w)
            def _(w=w):
                width = (w + 1) * FOX_TQ
                lse_all = lse_ref[...]
                sub = lax.broadcasted_iota(jnp.int32, (FOX_H, width), 0)
                dft = jnp.zeros((FOX_TQ, LANES), F32)
                for hh in range(HEADS_PER_LB):
                    aug = slice(hh * LANES, (hh + 1) * LANES)
                    sl = slice(hh * HD, (hh + 1) * HD)
                    head = hp * HEADS_PER_LB + hh
                    qa = qa_ref[:, aug]
                    ka = ka_ref[:width, aug]
                    do_h = do_ref[:, sl]
                    s = _fox_scores(qa, ka, w)
                    p = jnp.exp(s - lse_all[:, hh:hh + 1])
                    dp = _dot(do_h, v_ref[:width, sl], _NT)
                    ds = p * (dp - jnp.sum(dp * p, axis=-1, keepdims=True))
                    dft = jnp.where(lane == hh, jnp.sum(ds, axis=-1, keepdims=True), dft)
                    dfs_ref[:, :width] -= jnp.where(sub == head, jnp.sum(ds, axis=0, keepdims=True), 0.0)
                    dq_buf[slot, :, sl] = (_dot(ds, ka[:, :HD], _NN) * SCALE).astype(BF16)
                    dk_acc[:width, sl] += _dot(ds, qa[:, :HD], _TN)
                    dv_acc[:width, sl] += _dot(p, do_h, _TN)
                dft_ref[...] = dft

        dq_copy(t, slot).start()

        @pl.when(t == n_q - 1)
        def _():
            dq_copy(t - 1, 1 - slot).wait()
            dq_copy(t, slot).wait()
            kv_buf[0] = dk_acc[...].astype(BF16)
            kv_buf[1] = dv_acc[...].astype(BF16)
            copies = [pltpu.make_async_copy(
                kv_buf.at[j], dproj_ref.at[:, pl.ds(pl.multiple_of((j + 1) * FOX_W + hp * LANES, LANES), LANES)],
                kv_sems.at[j]) for j in range(2)]
            for cp in copies:
                cp.start()
            for cp in copies:
                cp.wait()

    qblk = pl.BlockSpec((FOX_TQ, FOX_AUG), lambda h, t: (t, h))
    lane_blk = pl.BlockSpec((None, FOX_TQ, LANES), lambda h, t: (h, t, 0))
    return pl.pallas_call(
        body, name="fox_bwd", grid=(FOX_HP, n_q),
        in_specs=[qblk, pl.BlockSpec((S, FOX_AUG), lambda h, t: (0, h)),
                  pl.BlockSpec((S, LANES), lambda h, t: (0, 2 * FOX_HP + h)),
                  lane_blk, pl.BlockSpec((FOX_TQ, LANES), lambda h, t: (t, h)), ANY] + [ANY] * len(deps),
        out_specs=[ANY, lane_blk, pl.BlockSpec((FOX_H, S), lambda h, t: (0, 0))],
        out_shape=[jax.ShapeDtypeStruct((S, D_PAD), BF16),
                   jax.ShapeDtypeStruct((FOX_HP, S, LANES), F32), jax.ShapeDtypeStruct((FOX_H, S), F32)],
        scratch_shapes=[pltpu.VMEM((S, LANES), F32), pltpu.VMEM((S, LANES), F32),
                        pltpu.VMEM((2, FOX_TQ, LANES), BF16), pltpu.VMEM((2, S, LANES), BF16),
                        pltpu.SemaphoreType.DMA((2,)), pltpu.SemaphoreType.DMA((2,))],
        input_output_aliases={5: 0},
        compiler_params=_cparams(("arbitrary", "arbitrary")),
    )(q_aug, k_aug, proj_a, lse, do, dproj, *deps)


def _rope_tables():
    half = ROPE_DIM // 2
    shape = (DIL_G, S, half)
    g = lax.broadcasted_iota(jnp.int32, shape, 0)
    row = lax.broadcasted_iota(jnp.int32, shape, 1)
    r = jnp.left_shift(1, 2 * g)
    per_class = S // r
    pos = (row % per_class) * r + row // per_class
    inv_freq = jnp.power(jnp.float32(ROPE_THETA), -lax.broadcasted_iota(F32, shape, 2) * 2.0 / ROPE_DIM)
    ang = pos.astype(F32) * inv_freq
    return jnp.concatenate([jnp.cos(ang), jnp.sin(ang)], axis=-1)


def _rope_factors(cs):
    half = ROPE_DIM // 2
    i = lax.broadcasted_iota(jnp.int32, (2 * half, 2 * LANES), 0)
    l = lax.broadcasted_iota(jnp.int32, (2 * half, 2 * LANES), 1)
    hit = ((l < LANES) & (i == l % half)) | ((l >= LANES) & (i - half == l % half))
    spread = jnp.where(hit, 1.0, 0.0).astype(BF16)
    hi = cs.astype(BF16)
    r1 = cs - hi.astype(F32)
    mid = r1.astype(BF16)
    lo = (r1 - mid.astype(F32)).astype(BF16)
    full = (_dot(hi, spread, _NN) + _dot(mid, spread, _NN)) + _dot(lo, spread, _NN)
    cos, sin = full[:, :LANES], full[:, LANES:]
    lane = lax.broadcasted_iota(jnp.int32, cos.shape, 1) % HD
    t_self = jnp.where(lane < ROPE_DIM, cos, 1.0)
    t_up = jnp.where(lane < half, -sin, 0.0)
    t_dn = jnp.where((lane >= half) & (lane < ROPE_DIM), sin, 0.0)
    return t_self, t_up, t_dn


def _residue_pieces(r):
    if r == 1:
        return [(slice(i, i + 512), slice(i, i + 512)) for i in range(0, S, 512)]
    n = S // r
    return [(pl.ds(j, n, stride=r), slice(j * n, (j + 1) * n)) for j in range(r)]


def _rope_apply(x, a, u, d, backward):
    half = ROPE_DIM // 2
    if backward:
        return x * a + pltpu.roll(x * u, half, 1) + pltpu.roll(x * d, LANES - half, 1)
    return x * a + pltpu.roll(x, LANES - half, 1) * u + pltpu.roll(x, half, 1) * d


def _rope_cache_factors(cs_ref, fac, first_step_of_group):
    @pl.when(first_step_of_group)
    def _():
        for lo in range(0, S, 512):
            for j, f in enumerate(_rope_factors(cs_ref[lo:lo + 512, :])):
                fac[j, lo:lo + 512, :] = f


LB_PER_CB = CB // LANES
ROPE_NB = 3 * DIL_G * LB_PER_CB


def _rope_step(s):
    per_group = 3 * LB_PER_CB
    return s // per_group, (s % per_group) // LB_PER_CB, s % LB_PER_CB


def _rope_split(proj_b, tables):
    def body(x_ref, cs_ref, o_ref, fac):
        g, t, hf = _rope_step(pl.program_id(0))
        _rope_cache_factors(cs_ref, fac, (t == 0) & (hf == 0))
        for gs in range(DIL_G):
            @pl.when(g == gs)
            def _(gs=gs):
                for tok, sub in _residue_pieces(DIL_R[gs]):
                    x = x_ref[tok, :]
                    y = _rope_apply(x, fac[0, sub, :], fac[1, sub, :], fac[2, sub, :], False)
                    o_ref[sub, :] = jnp.where(t < 2, y, x).astype(BF16)

    def in_index(s):
        g, t, hf = _rope_step(s)
        return 0, (t * DIL_G + g) * LB_PER_CB + hf

    def out_index(s):
        g, t, hf = _rope_step(s)
        return t * DIL_G + g, 0, hf

    tab = pl.BlockSpec((None, S, 2 * (ROPE_DIM // 2)), lambda s: (_rope_step(s)[0], 0, 0))
    return pl.pallas_call(
        body, name="rope_split", grid=(ROPE_NB,),
        in_specs=[pl.BlockSpec((S, LANES), in_index), tab],
        out_specs=pl.BlockSpec((None, S, LANES), out_index),
        out_shape=jax.ShapeDtypeStruct((3 * DIL_G, S, CB), BF16),
        scratch_shapes=[pltpu.VMEM((3, S, LANES), F32)],
        compiler_params=_cparams(("arbitrary",)),
    )(proj_b, tables)


def _rope_merge_bwd(dq3, dk3, dv3, tables, dproj):
    def body(dq_ref, dk_ref, dv_ref, cs_ref, _, o_ref, tmp, fac):
        s = pl.program_id(0)
        g, t, hf = _rope_step(s)
        _rope_cache_factors(cs_ref, fac, (t == 0) & (hf == 0) & (s < ROPE_NB))
        for gs in range(DIL_G):
            @pl.when((g == gs) & (s < ROPE_NB))
            def _(gs=gs):
                for tok, sub in _residue_pieces(DIL_R[gs]):
                    x = jnp.where(t == 0, dq_ref[sub, :], jnp.where(t == 1, dk_ref[sub, :], dv_ref[sub, :]))
                    y = _rope_apply(x, fac[0, sub, :], fac[1, sub, :], fac[2, sub, :], True)
                    tmp[tok, :] = jnp.where(t < 2, y, x)
                o_ref[...] = tmp[...].astype(BF16)

        @pl.when(s >= ROPE_NB)
        def _():
            o_ref[...] = jnp.zeros_like(o_ref)

    def src_spec(own):
        def index(s):
            g, t, hf = _rope_step(jnp.minimum(s, ROPE_NB - 1))
            return g, 0, jnp.where(t == own, hf, jnp.where(t > own, LB_PER_CB - 1, 0))
        return pl.BlockSpec((None, S, LANES), index)

    def out_index(s):
        g, t, hf = _rope_step(s)
        col = SEG_B // LANES + (t * DIL_G + g) * LB_PER_CB + hf
        return 0, jnp.where(s < ROPE_NB, col, SEG_Z // LANES + s - ROPE_NB)

    tab = pl.BlockSpec((None, S, 2 * (ROPE_DIM // 2)), lambda s: (_rope_step(jnp.minimum(s, ROPE_NB - 1))[0], 0, 0))
    return pl.pallas_call(
        body, name="rope_merge_bwd", grid=(ROPE_NB + LB_PER_CB,),
        in_specs=[src_spec(0), src_spec(1), src_spec(2), tab, ANY],
        out_specs=pl.BlockSpec((S, LANES), out_index),
        out_shape=jax.ShapeDtypeStruct((S, D_PAD), BF16),
        scratch_shapes=[pltpu.VMEM((S, LANES), F32), pltpu.VMEM((3, S, LANES), F32)],
        input_output_aliases={4: 0},
        compiler_params=_cparams(("arbitrary",)),
    )(dq3, dk3, dv3, tables, dproj)


DIL_NB = S // DIL_BLK


_BQK = (((2,), (2,)), ((0,), (0,)))
_BQD = (((2,), (1,)), ((0,), (0,)))
_BKD = (((1,), (1,)), ((0,), (0,)))


def _dil_mask(g):
    shape = (DIL_NB, DIL_BLK, 2 * DIL_BLK)
    blocks_per_seq = jnp.right_shift(DIL_NB, 2 * g)
    has_prev = jnp.bitwise_and(lax.broadcasted_iota(jnp.int32, shape, 0), blocks_per_seq - 1) != 0
    a = lax.broadcasted_iota(jnp.int32, shape, 1)
    kk = lax.broadcasted_iota(jnp.int32, shape, 2)
    diff = DIL_BLK + a - kk
    return (diff >= 0) & (diff <= DIL_BLK) & ((kk >= DIL_BLK) | has_prev)


def _blocks(t):
    return t.reshape(DIL_NB, DIL_BLK, t.shape[-1])


def _with_prev(t):
    prev = jnp.concatenate([jnp.zeros((DIL_BLK, t.shape[-1]), t.dtype), t[:-DIL_BLK]], axis=0)
    return jnp.concatenate([_blocks(prev), _blocks(t)], axis=1)


def _fold_prev(t2):
    n = t2.shape[-1]
    to_prev = t2[:, :DIL_BLK].reshape(S, n)
    own = t2[:, DIL_BLK:].reshape(S, n)
    return own + jnp.concatenate([to_prev[DIL_BLK:], jnp.zeros((DIL_BLK, n), t2.dtype)], axis=0)


def _dil_group_spec(t, width=DIL_OUT):
    return pl.BlockSpec((None, S, width), lambda g: (t * DIL_G + g, 0, 0))


def _dil_fwd(qkv3):
    def body(q_ref, k_ref, v_ref, o_ref, lse_ref):
        ok = _dil_mask(pl.program_id(0))
        lane = lax.broadcasted_iota(jnp.int32, (S, LANES), 1)
        lse = jnp.zeros((S, LANES), F32)
        for h in range(DIL_HG):
            sl = slice(h * HD, (h + 1) * HD)
            s = jnp.where(ok, _dot(_blocks(q_ref[:, sl]), _with_prev(k_ref[:, sl]), _BQK) * SCALE, NEG_INF)
            m = jnp.max(s, axis=-1, keepdims=True)
            p = jnp.exp(s - m)
            l = jnp.sum(p, axis=-1, keepdims=True)
            o_ref[:, sl] = _dot(p / l, _with_prev(v_ref[:, sl]), _BQD).reshape(S, HD)
            lse = jnp.where(lane == h, (m + jnp.log(l)).reshape(S, 1), lse)
        lse_ref[...] = lse

    return pl.pallas_call(
        body, name="dil_fwd", grid=(DIL_G,),
        in_specs=[_dil_group_spec(0), _dil_group_spec(1), _dil_group_spec(2)],
        out_specs=[_dil_group_spec(0), _dil_group_spec(0, LANES)],
        out_shape=[jax.ShapeDtypeStruct((DIL_G, S, DIL_OUT), F32), jax.ShapeDtypeStruct((DIL_G, S, LANES), F32)],
        compiler_params=_cparams(("parallel",)),
    )(qkv3, qkv3, qkv3)


def _dil_bwd(qkv3, lse3, do3, c3):
    def body(q_ref, k_ref, v_ref, lse_ref, do_ref, c_ref, dq_ref, dk_ref, dv_ref):
        ok = _dil_mask(pl.program_id(0))
        lse = lse_ref[...]
        c = c_ref[...]
        for h in range(DIL_HG):
            sl = slice(h * HD, (h + 1) * HD)
            q = _blocks(q_ref[:, sl])
            k2 = _with_prev(k_ref[:, sl])
            do_h = _blocks(do_ref[:, sl])
            s = jnp.where(ok, _dot(q, k2, _BQK) * SCALE, NEG_INF)
            p = jnp.exp(s - _blocks(lse[:, h:h + 1]))
            dp = _dot(do_h, _with_prev(v_ref[:, sl]), _BQK)
            ds = p * (dp - _blocks(c[:, h:h + 1]))
            dsb = (ds * SCALE).astype(BF16)
            dq_ref[:, sl] = _dot(dsb, k2, _BQD).reshape(S, HD)
            dk_ref[:, sl] = _fold_prev(_dot(dsb, q, _BKD))
            dv_ref[:, sl] = _fold_prev(_dot(p, do_h, _BKD))

    return pl.pallas_call(
        body, name="dil_bwd", grid=(DIL_G,),
        in_specs=[_dil_group_spec(0), _dil_group_spec(1), _dil_group_spec(2), _dil_group_spec(0, LANES),
                  _dil_group_spec(0), _dil_group_spec(0, LANES)],
        out_specs=[_dil_group_spec(0)] * 3,
        out_shape=[jax.ShapeDtypeStruct((DIL_G, S, DIL_OUT), F32)] * 3,
        compiler_params=_cparams(("parallel",)),
    )(qkv3, qkv3, qkv3, lse3, do3, c3)


COMB_T = 256


def _dil_combine(o3, lse3):
    heads_per_lb = LANES // HD

    def body(o_ref, lse_ref, ob_ref, al_ref, *scratch):
        o_tok = [scratch[LB_PER_CB * gg:LB_PER_CB * (gg + 1)] for gg in range(DIL_G)]
        lse_tok = scratch[LB_PER_CB * DIL_G:]
        g = pl.program_id(0)
        for gs in range(DIL_G):
            @pl.when(g == gs)
            def _(gs=gs):
                for tok, sub in _residue_pieces(DIL_R[gs]):
                    for hf in range(LB_PER_CB):
                        o_tok[gs][hf][tok, :] = o_ref[sub, hf * LANES:(hf + 1) * LANES]
                    lse_tok[gs][tok, :] = lse_ref[sub, :]

        @pl.when(g == DIL_G - 1)
        def _():
            def chunk(i, carry):
                rows = pl.ds(pl.multiple_of(i * COMB_T, COMB_T), COMB_T)
                lse = [lse_tok[gg][rows, :] for gg in range(DIL_G)]
                m = jnp.maximum(jnp.maximum(lse[0], lse[1]), lse[2])
                e = [jnp.exp(lse[gg] - m) for gg in range(DIL_G)]
                den = (e[0] + e[1]) + e[2]
                al = [e[gg] / den for gg in range(DIL_G)]
                for gg in range(DIL_G):
                    al_ref[gg, rows, :] = al[gg]
                for h in range(DIL_HG):
                    hf, sl = h // heads_per_lb, slice((h % heads_per_lb) * HD, (h % heads_per_lb + 1) * HD)
                    acc = al[0][:, h:h + 1] * o_tok[0][hf][rows, sl]
                    for gg in range(1, DIL_G):
                        acc = acc + al[gg][:, h:h + 1] * o_tok[gg][hf][rows, sl]
                    ob_ref[rows, h * HD:(h + 1) * HD] = acc
                return carry

            lax.fori_loop(0, S // COMB_T, chunk, 0)

    return pl.pallas_call(
        body, name="dil_combine", grid=(DIL_G,),
        in_specs=[pl.BlockSpec((None, S, DIL_OUT), lambda g: (g, 0, 0)),
                  pl.BlockSpec((None, S, LANES), lambda g: (g, 0, 0))],
        out_specs=[pl.BlockSpec((S, DIL_OUT), lambda g: (0, 0)),
                   pl.BlockSpec((DIL_G, S, LANES), lambda g: (0, 0, 0))],
        out_shape=[jax.ShapeDtypeStruct((S, DIL_OUT), F32), jax.ShapeDtypeStruct((DIL_G, S, LANES), F32)],
        scratch_shapes=[pltpu.VMEM((S, LANES), F32)] * (DIL_G * (LB_PER_CB + 1)),
        compiler_params=_cparams(("arbitrary",)),
    )(o3, lse3)


def _dil_combine_bwd(dob, ob, alpha, deps=()):
    heads_per_lb = LANES // HD

    def body(dob_ref, ob_ref, al_ref, *refs):
        do_ref, c_ref = refs[len(deps):]
        g = pl.program_id(0)
        hf = pl.program_id(1)
        for gs in range(DIL_G):
            @pl.when(g == gs)
            def _(gs=gs):
                for tok, sub in _residue_pieces(DIL_R[gs]):
                    dob = dob_ref[tok, :]
                    prod = dob * ob_ref[tok, :]
                    al = al_ref[tok, :]
                    lane = lax.broadcasted_iota(jnp.int32, al.shape, 1)
                    c = jnp.where(hf == 0, 0.0, c_ref[sub, :])
                    for hh in range(heads_per_lb):
                        sl = slice(hh * HD, (hh + 1) * HD)
                        head = hf * heads_per_lb + hh
                        a = jnp.sum(jnp.where(lane == head, al, 0.0), axis=-1, keepdims=True)
                        do_ref[sub, sl] = (a * dob[:, sl]).astype(BF16)
                        c = jnp.where(lane == head, a * jnp.sum(prod[:, sl], axis=-1, keepdims=True), c)
                    c_ref[sub, :] = c

    half = pl.BlockSpec((S, LANES), lambda g, hf: (0, hf))
    return pl.pallas_call(
        body, name="dil_combine_bwd", grid=(DIL_G, LB_PER_CB),
        in_specs=[half, half, pl.BlockSpec((None, S, LANES), lambda g, hf: (g, 0, 0))] + [ANY] * len(deps),
        out_specs=[pl.BlockSpec((None, S, LANES), lambda g, hf: (g, 0, hf)),
                   pl.BlockSpec((None, S, LANES), lambda g, hf: (g, 0, 0))],
        out_shape=[jax.ShapeDtypeStruct((DIL_G, S, DIL_OUT), BF16), jax.ShapeDtypeStruct((DIL_G, S, LANES), F32)],
        compiler_params=_cparams(("parallel", "arbitrary")),
    )(dob, ob, alpha, *deps)


def _local_step(x, tgt, h1, g_attn, g_mlp, g_final, b_pad, w_in_t, hooks):
    tables = _rope_tables()

    first = hooks.first_deps()
    proj_a = _mm(h1, w_in_t, mode="nt", tm=S, tn=512, n=3 * FOX_W, out_dtypes=[BF16], name="proj_a", deps=first)
    proj_b = _mm(h1, w_in_t, mode="nt", tm=S, tn=DIL_W, n=3 * DIL_W, b_off=SEG_B // DIL_W, out_dtypes=[F32], name="proj_b",
                deps=first)
    proj_g = _mm(h1, w_in_t, mode="nt", tm=S, tn=D, n=2 * D, b_off=SEG_G // D, out_dtypes=[F32], name="proj_g",
                deps=first)
    proj_f = _mm(h1, w_in_t, mode="nt", tm=S, tn=LANES, n=LANES, b_off=SEG_F // LANES, out_dtypes=[F32], name="proj_f",
                deps=first)

    q_aug, k_aug = _fox_prepare(proj_a, proj_f, b_pad)
    oa, lse_a = _fox_fwd(q_aug, k_aug, proj_a)

    qkv3 = _rope_split(proj_b, tables)
    o3, lse3 = _dil_fwd(qkv3)
    ob, alpha = _dil_combine(o3, lse3)

    w_a, w_b, w_out = hooks.mixer_weights(ob)
    ya, yb, mixed, x2, h2 = _mixer_tail(oa, ob, w_a, w_b, w_out, proj_g, x, g_mlp)
    w_up_sh, w_down = hooks.mlp_weights(h2)

    def up_epilogue(acc):
        r = jnp.maximum(acc, 0.0)
        return acc, r * r

    u, act = _mm(h2, w_up_sh, mode="nn", tm=S, tn=DFF // N_DEV, b_sharded=True, out_dtypes=[F32, BF16],
                 name="mlp_up", epilogue=up_epilogue)

    def loss_epilogue(acc, rows, vecs):
        dx3, dg, loss = _final_loss_rows(rows[0] + acc, vecs[0], rows[1])
        return (dx3,), (dg, loss)

    dx3, dg_final, loss = _mm_rows(act, w_down, tm=ROWS_TM, name="mlp_down_loss", epilogue=loss_epilogue,
                                   rows=(x2, tgt), vecs=(g_final,), row_out=(F32,), vec_out=(D, LANES))

    def rms_bwd_epilogue(acc, rows, vecs):
        dx, dg = _rms_bwd_rows(acc, rows[0], rows[1], vecs[0])
        return (dx,), (dg,)

    du = _mm(dx3, w_down, mode="nt", tm=S, tn=512, out_dtypes=[BF16], name="mlp_down_bwd",
             epilogue=lambda acc, u_t: (acc * (2.0 * jnp.maximum(u_t, 0.0)),), extras=(u,))
    dw_down = _mm(act, dx3, mode="tn", tm=512, tn=D, out_dtypes=[F32], name="dw_down")
    dw_up_sh = _mm(h2, du, mode="tn", tm=D, tn=DFF // N_DEV, out_sharded=True, out_dtypes=[F32], name="dw_up")
    dx2, dg_mlp = _mm_rows(du, w_up_sh, b_sharded=True, tm=ROWS_TM, name="mlp_up_bwd", epilogue=rms_bwd_epilogue,
                           rows=(x2, dx3), vecs=(g_mlp,), row_out=(F32,), vec_out=(D,),
                           deps=hooks.mlp_grads(dw_up_sh, dw_down))

    dw_out = _mm(mixed, dx2, mode="tn", tm=D, tn=D, out_dtypes=[F32], name="dw_out")

    dyab, dproj = _out_proj_gate_bwd(dx2, w_out, proj_g, ya, yb)
    doa = _mm(dyab, w_a, mode="nt", tm=S, tn=FOX_W, k=D, a_off=0, out_dtypes=[BF16], name="branch_a_bwd")
    dw_a = _mm(oa, dyab, mode="tn", tm=FOX_W, tn=D, n=D, b_off=0, out_dtypes=[F32], name="dw_branch_a")
    dob = _mm(dyab, w_b, mode="nt", tm=S, tn=CB, k=D, a_off=1, out_dtypes=[F32], name="branch_b_bwd")
    dw_b = _mm(ob, dyab, mode="tn", tm=DIL_OUT, tn=D, n=D, b_off=1, out_dtypes=[F32], name="dw_branch_b")

    dproj, dft, dfs = _fox_bwd(q_aug, k_aug, proj_a, lse_a, doa, dproj, deps=hooks.mixer_grads(dw_a, dw_b, dw_out))
    dproj, db = _fox_dscan(dft, dfs, proj_f, b_pad, dproj)

    do3, c3 = _dil_combine_bwd(dob, ob, alpha, deps=hooks.mid_backward(db))
    dq3, dk3, dv3 = _dil_bwd(qkv3, lse3, do3, c3)
    dproj = _rope_merge_bwd(dq3, dk3, dv3, tables, dproj)

    dw_in_t = _mm(dproj, h1, mode="tn", tm=DW_IN_TM, tn=D, out_dtypes=[F32], name="dw_in")
    dx, dg_attn = _mm_rows(dproj, w_in_t, tm=ROWS_TM // 2, name="proj_bwd", epilogue=rms_bwd_epilogue,
                           rows=(x, dx2), vecs=(g_attn,), row_out=(F32,), vec_out=(D,),
                           deps=hooks.w_in_grad(dw_in_t))

    return loss, dx, (dg_attn, db, dg_mlp, dg_final)


SHARD_SHAPES = ((D_IN // N_DEV, D), (FOX_W, D // N_DEV), (DIL_OUT, D // N_DEV), (D // N_DEV, D),
                (D, DFF // N_DEV), (DFF // N_DEV, D))
W_NAMES = ("w_in", "w_a", "w_b", "w_out", "w_up", "w_down")
AG_COPIES = 7
HBM = pl.BlockSpec(memory_space=pltpu.HBM)
SEM = pl.BlockSpec(memory_space=pltpu.SEMAPHORE)


def _place():
    return lax.axis_index("x"), lax.axis_index("y"), lax.axis_index("c")


W_SLAB = 752
W_WIN = 272
PAD_BLK = 256


def _pad_runs():
    runs = sorted((pad, pad + b - a, p, a) for p in range(N_DEV) for a, b, pad in _shard_pieces(p))
    blocks = []
    for k in range(D_PAD // PAD_BLK):
        lo, hi = k * PAD_BLK, (k + 1) * PAD_BLK
        blocks.append([(max(lo, r0) - lo, min(hi, r1) - lo, p, a + max(lo, r0) - r0)
                       for r0, r1, p, a in runs if max(lo, r0) < min(hi, r1)])
    return blocks


def _gather_w_in(shard, x_in, g_attn, late_shards):
    blocks = _pad_runs()

    norm_t = 512
    n_chunks = S // norm_t

    n_late = len(late_shards)
    late_shapes = [t.shape for t in late_shards]

    def body(x_ref, xin_ref, g_ref, *refs):
        late_in, (out_ref, h1_ref), late_out = refs[:n_late], refs[n_late:n_late + 2], refs[n_late + 2:2 * n_late + 2]
        scratch = refs[2 * n_late + 2:]
        stage, land, xbuf = scratch[:3]
        late_stage, late_cast = scratch[3:3 + n_late], scratch[3 + n_late:3 + 2 * n_late]
        send_sems, recv_sems, load_sem, x_sems, late_sems, store_sems = scratch[3 + 2 * n_late:]
        x, y, c = _place()
        me, sibling = (x, y, c), (x, y, 1 - c)
        chips = [(1 - x, y), (x, 1 - y), (1 - x, 1 - y)]

        def slot(px, py, pc):
            return land.at[4 * px + 2 * py + pc]

        def copy(k, block, to):
            return pltpu.make_async_remote_copy(
                src_ref=slot(*block), dst_ref=slot(*block), send_sem=send_sems.at[k], recv_sem=recv_sems.at[k],
                device_id=to, device_id_type=MESH)

        load = pltpu.make_async_copy(x_ref, stage, load_sem)
        load.start()
        load.wait()
        land[4 * x + 2 * y + c] = stage[...].astype(BF16)
        sent = [copy(0, me, sibling)] + [copy(1 + j, me, (*chip, c)) for j, chip in enumerate(chips)]
        for cp in sent:
            cp.start()

        def x_load(i):
            return pltpu.make_async_copy(xin_ref.at[pl.ds(i * norm_t, norm_t)], xbuf.at[i % 2], x_sems.at[i % 2])

        late_loads = [pltpu.make_async_copy(late_in[i], late_stage[i], late_sems.at[i]) for i in range(n_late)]
        for ld in late_loads:
            ld.start()
        x_load(0).start()
        for i in range(n_chunks):
            if i + 1 < n_chunks:
                x_load(i + 1).start()
            x_load(i).wait()
            h1_ref[i * norm_t:(i + 1) * norm_t, :] = _rms_rows(xbuf[i % 2], g_ref[...]).astype(BF16)
        stores = []
        for i in range(n_late):
            late_loads[i].wait()
            late_cast[i][...] = late_stage[i][...].astype(BF16)
            stores.append(pltpu.make_async_copy(late_cast[i], late_out[i].at[4 * x + 2 * y + c], store_sems.at[i]))
            stores[-1].start()

        for j, chip in enumerate(chips):
            copy(1 + j, (*chip, c), me).wait_recv()
            sent.append(copy(4 + j, (*chip, c), sibling))
            sent[-1].start()
        copy(0, sibling, me).wait_recv()
        for j, chip in enumerate(chips):
            copy(4 + j, (*chip, 1 - c), me).wait_recv()
        for cp in sent:
            cp.wait_send()
        for st in stores:
            st.wait()

        row = lax.broadcasted_iota(jnp.int32, (PAD_BLK, D), 0)
        for k, runs in enumerate(blocks):
            out = jnp.zeros((PAD_BLK, D), F32)
            for o_lo, o_hi, p, a in runs:
                start = min(a // 16 * 16, W_SLAB - W_WIN)
                win = land[p, start:start + W_WIN, :].astype(F32)
                moved = pltpu.roll(win, (o_lo - (a - start)) % W_WIN, 0)[:PAD_BLK]
                out = jnp.where((row >= o_lo) & (row < o_hi), moved, out)
            out_ref[k * PAD_BLK:(k + 1) * PAD_BLK, :] = out.astype(BF16)

    return pl.pallas_call(
        body, name="all_gather_w_in",
        out_shape=[jax.ShapeDtypeStruct((D_PAD, D), BF16), jax.ShapeDtypeStruct((S, D), BF16)]
                  + [jax.ShapeDtypeStruct((N_DEV,) + sh, BF16) for sh in late_shapes],
        in_specs=[ANY, ANY, pl.BlockSpec(memory_space=pltpu.VMEM)] + [ANY] * n_late,
        out_specs=[pl.BlockSpec(memory_space=pltpu.VMEM)] * 2 + [ANY] * n_late,
        scratch_shapes=[pltpu.VMEM((W_SLAB, D), F32), pltpu.VMEM((N_DEV, W_SLAB, D), BF16),
                        pltpu.VMEM((2, norm_t, D), F32)]
                       + [pltpu.VMEM(sh, F32) for sh in late_shapes] + [pltpu.VMEM(sh, BF16) for sh in late_shapes]
                       + [pltpu.SemaphoreType.DMA((AG_COPIES,)), pltpu.SemaphoreType.DMA((AG_COPIES,)),
                          pltpu.SemaphoreType.DMA(()), pltpu.SemaphoreType.DMA((2,)),
                          pltpu.SemaphoreType.DMA((n_late,)), pltpu.SemaphoreType.DMA((n_late,))],
        compiler_params=_cparams(None),
    )(shard, x_in, g_attn, *late_shards)


PEER_COPIES = N_DEV - 1
SIBLING_COPIES = 4
CHIP_COPIES = 3


def _peer_copies(_, land_refs, send_sems, recv_sems):
    x, y, c = _place()
    mine = 4 * x + 2 * y + c
    copies = []
    for i, ref in enumerate(land_refs):
        for k in range(1, N_DEV):
            peer = (x ^ (k >> 2), y ^ ((k >> 1) & 1), c ^ (k & 1))
            n = i * PEER_COPIES + k - 1
            copies.append(pltpu.make_async_remote_copy(
                src_ref=ref.at[mine], dst_ref=ref.at[mine], send_sem=send_sems.at[n], recv_sem=recv_sems.at[n],
                device_id=peer, device_id_type=MESH))
    return copies


def _sibling_copies(g_refs, r_refs, send_sems, recv_sems):
    x, y, c = _place()
    return [pltpu.make_async_remote_copy(
        src_ref=g_refs[i].at[2 * k + (1 - c)], dst_ref=r_refs[i].at[k],
        send_sem=send_sems.at[SIBLING_COPIES * i + k], recv_sem=recv_sems.at[SIBLING_COPIES * i + k],
        device_id=(x, y, 1 - c), device_id_type=MESH) for i in range(len(g_refs)) for k in range(SIBLING_COPIES)]


def _chip_copies(p_refs, r_refs, send_sems, recv_sems):
    x, y, c = _place()
    chips = [(1 - x, y), (x, 1 - y), (1 - x, 1 - y)]
    return [pltpu.make_async_remote_copy(
        src_ref=p_refs[i].at[2 * cx + cy], dst_ref=r_refs[i].at[j],
        send_sem=send_sems.at[CHIP_COPIES * i + j], recv_sem=recv_sems.at[CHIP_COPIES * i + j],
        device_id=(cx, cy, c), device_id_type=MESH) for i in range(len(p_refs)) for j, (cx, cy) in enumerate(chips)]


def _in_hbm(a):
    return pltpu.with_memory_space_constraint(a, pltpu.HBM)


def _exchange_start(name, copies_fn, n_copies, srcs, lands, after=()):
    n_s, n_l, n_a = len(srcs), len(lands), len(after)

    def body(*refs):
        outs = refs[n_s + n_l + n_a:]
        for cp in copies_fn(refs[:n_s], refs[n_s:n_s + n_l], outs[0], outs[1]):
            cp.start()
        outs[-1][...] = jnp.zeros_like(outs[-1])

    res = pl.pallas_call(
        body, name=name,
        out_shape=[pltpu.SemaphoreType.DMA((n_copies,)), pltpu.SemaphoreType.DMA((n_copies,))]
                  + [pltpu.HBM(t.shape, t.dtype) for t in (*srcs, *lands)] + [jax.ShapeDtypeStruct((8, LANES), F32)],
        in_specs=[HBM] * (n_s + n_l) + [ANY] * n_a,
        out_specs=[SEM, SEM] + [HBM] * (n_s + n_l) + [pl.BlockSpec(memory_space=pltpu.VMEM)],
        input_output_aliases={i: 2 + i for i in range(n_s + n_l)},
        compiler_params=pltpu.CompilerParams(has_side_effects=pltpu.SideEffectType.DATAFLOW_SIDE_EFFECTING),
    )(*[_in_hbm(t) for t in (*srcs, *lands)], *after)
    return res[0], res[1], list(res[2:2 + n_s]), list(res[2 + n_s:2 + n_s + n_l]), res[-1]


def _exchange_wait(name, copies_fn, started, after):
    send_sems, recv_sems, srcs, lands, _ = started
    n_s, n_l = len(srcs), len(lands)

    def body(*refs):
        for cp in copies_fn(refs[:n_s], refs[n_s:n_s + n_l], refs[n_s + n_l], refs[n_s + n_l + 1]):
            cp.wait_send()
            cp.wait_recv()

    res = pl.pallas_call(
        body, name=name,
        out_shape=[pltpu.HBM(t.shape, t.dtype) for t in (*srcs, *lands)],
        in_specs=[HBM] * (n_s + n_l) + [SEM, SEM, ANY],
        out_specs=[HBM] * (n_s + n_l),
        input_output_aliases={i: i for i in range(n_s + n_l)},
        compiler_params=pltpu.CompilerParams(has_side_effects=pltpu.SideEffectType.DATAFLOW_SIDE_EFFECTING),
    )(*srcs, *lands, send_sems, recv_sems, after)
    return list(res[:n_s]), list(res[n_s:])


def _shard_block(shape):
    rows, cols = shape
    if rows * cols <= 256 * 1024:
        return rows, cols
    if rows % 256:
        return rows, 256
    return 256, cols


def _chip_partial(ids, g_stack, recv1, name):
    shape = g_stack.shape[1:]
    br, bc = _shard_block(shape)

    def body(ids_ref, g_ref, r_ref, pb_ref, own_ref):
        s = g_ref[...] + r_ref[...]
        pb_ref[...] = s.astype(BF16)

        @pl.when(pl.program_id(2) == ids_ref[1])
        def _():
            own_ref[...] = s

    grid_spec = pltpu.PrefetchScalarGridSpec(
        num_scalar_prefetch=1, grid=(shape[0] // br, shape[1] // bc, 4),
        in_specs=[pl.BlockSpec((None, br, bc), lambda r, q, k, ids: (2 * k + ids[0], r, q)),
                  pl.BlockSpec((None, br, bc), lambda r, q, k, ids: (k, r, q))],
        out_specs=[pl.BlockSpec((None, br, bc), lambda r, q, k, ids: (k, r, q)),
                   pl.BlockSpec((br, bc), lambda r, q, k, ids: (r, q))])
    return pl.pallas_call(
        body, name=name, grid_spec=grid_spec,
        out_shape=[jax.ShapeDtypeStruct((4,) + shape, BF16), jax.ShapeDtypeStruct(shape, F32)],
        compiler_params=_cparams(("parallel", "parallel", "arbitrary")),
    )(ids, g_stack, recv1)


def _adamw(w, g, m, v):
    m = ADAM_B1 * m + (1.0 - ADAM_B1) * g
    v = ADAM_B2 * v + (1.0 - ADAM_B2) * (g * g)
    m_hat = m / (1.0 - ADAM_B1 ** ADAM_STEP)
    v_hat = v / (1.0 - ADAM_B2 ** ADAM_STEP)
    delta = -ADAM_LR * (m_hat / (jnp.sqrt(v_hat) + ADAM_EPS) + ADAM_WD * w)
    return delta, m, v


def _reduce_adamw(own, recv2, w, m, v, name, deps=()):
    shape = own.shape
    br, bc = _shard_block(shape)

    def body(own_ref, r_ref, w_ref, m_ref, v_ref, *refs):
        g_ref, d_ref, nm_ref, nv_ref = refs[len(deps):]
        g = own_ref[...]
        for j in range(3):
            g = g + r_ref[j].astype(F32)
        delta, nm, nv = _adamw(w_ref[...], g, m_ref[...], v_ref[...])
        g_ref[...] = g
        d_ref[...] = delta
        nm_ref[...] = nm
        nv_ref[...] = nv

    blk = pl.BlockSpec((br, bc), lambda r, q: (r, q))
    return pl.pallas_call(
        body, name=name, grid=(shape[0] // br, shape[1] // bc),
        in_specs=[blk, pl.BlockSpec((3, br, bc), lambda r, q: (0, r, q)), blk, blk, blk] + [ANY] * len(deps),
        out_specs=[blk] * 4, out_shape=[jax.ShapeDtypeStruct(shape, F32)] * 4,
        compiler_params=_cparams(("parallel", "parallel")),
    )(own, recv2, w, m, v, *deps)


def _small_allreduce_adamw(gvec, w, m, v):
    def body(g_ref, w_ref, m_ref, v_ref, go_ref, d_ref, nm_ref, nv_ref, buf, send_sems, recv_sems):
        x, y, c = _place()
        my_slot = 4 * x + 2 * y + c
        buf[my_slot] = g_ref[...]
        copies = []
        for k in range(1, N_DEV):
            px, py, pc = x ^ (k >> 2), y ^ ((k >> 1) & 1), c ^ (k & 1)
            copies.append(pltpu.make_async_remote_copy(
                src_ref=g_ref, dst_ref=buf.at[my_slot], send_sem=send_sems.at[k - 1], recv_sem=recv_sems.at[k - 1],
                device_id=(px, py, pc), device_id_type=MESH))
        for cp in copies:
            cp.start()
        for k in range(1, N_DEV):
            px, py, pc = x ^ (k >> 2), y ^ ((k >> 1) & 1), c ^ (k & 1)
            pltpu.make_async_remote_copy(
                src_ref=g_ref, dst_ref=buf.at[4 * px + 2 * py + pc], send_sem=send_sems.at[k - 1],
                recv_sem=recv_sems.at[k - 1], device_id=(px, py, pc), device_id_type=MESH).wait_recv()
        for cp in copies:
            cp.wait_send()
        g = buf[0]
        for s in range(1, N_DEV):
            g = g + buf[s]
        delta, nm, nv = _adamw(w_ref[...], g, m_ref[...], v_ref[...])
        go_ref[...] = g
        d_ref[...] = delta
        nm_ref[...] = nm
        nv_ref[...] = nv

    vm = pl.BlockSpec(memory_space=pltpu.VMEM)
    return pl.pallas_call(
        body, name="small_allreduce_adamw",
        in_specs=[vm] * 4, out_specs=[vm] * 4,
        out_shape=[jax.ShapeDtypeStruct((SMALL_R, LANES), F32)] * 4,
        scratch_shapes=[pltpu.VMEM((N_DEV, SMALL_R, LANES), F32),
                        pltpu.SemaphoreType.DMA((N_DEV - 1,)), pltpu.SemaphoreType.DMA((N_DEV - 1,))],
    )(gvec, w, m, v)


def _cols_to_whole(stack):
    rows = stack.shape[1]
    return stack.transpose(1, 0, 2).reshape(rows, -1)


def _whole_to_cols(t):
    rows = t.shape[0]
    return t.reshape(rows, N_DEV, -1).transpose(1, 0, 2)


W_IN_SEGMENTS = ((0, 3 * FOX_W, SEG_A), (3 * FOX_W, 3 * FOX_W + FOX_H, SEG_F),
                 (3 * FOX_W + FOX_H, 3 * FOX_W + FOX_H + 3 * DIL_W, SEG_B), (3 * FOX_W + FOX_H + 3 * DIL_W, D_IN, SEG_G))


def _shard_pieces(p):
    rows = D_IN // N_DEV
    lo, hi = p * rows, (p + 1) * rows
    return [(max(lo, a) - lo, min(hi, b) - lo, pad + max(lo, a) - a)
            for a, b, pad in W_IN_SEGMENTS if max(lo, a) < min(hi, b)]


def _unpad_dw_in_t(dwp):
    return jnp.stack([jnp.concatenate([dwp[pad:pad + b - a] for a, b, pad in _shard_pieces(p)], axis=0)
                      for p in range(N_DEV)])


LOSS_ROW = 25


def _pack_small(g_attn, b, g_mlp, g_final, loss_row=None):
    tail = jnp.pad(b, ((0, 7), (0, LANES - b.shape[1])))
    if loss_row is not None:
        tail = tail + jnp.pad(loss_row, ((LOSS_ROW - 24, 31 - LOSS_ROW), (0, 0)))
    return jnp.concatenate([g_attn.reshape(8, LANES), g_mlp.reshape(8, LANES), g_final.reshape(8, LANES), tail], axis=0)


def _unpack_small(p):
    return (p[0:8].reshape(1, D), p[24:25, :FOX_H], p[8:16].reshape(1, D), p[16:24].reshape(D))


class _StepComm:
    def __init__(self, ids, w_sh, m_sh, v_sh, lands, after_w_in):
        self.ids = ids
        self.w_sh, self.m_sh, self.v_sh = w_sh, m_sh, v_sh
        self.updates = None
        self.gather_mixer = _exchange_start("gather_mixer_start", _peer_copies, PEER_COPIES * 3, [], lands[:3],
                                            after=(after_w_in,))
        self.gather_mlp = _exchange_start("gather_mlp_start", _peer_copies, PEER_COPIES * 2, [], lands[3:],
                                          after=(self.gather_mixer[-1],))
        self.sibling = {}
        self.chips = {}
        self.own = {}
        self.names = {}

    def _reduce_start(self, group, names, grads, after=()):
        lands = [lax.empty((SIBLING_COPIES,) + t.shape[1:], F32) for t in grads]
        self.sibling[group] = _exchange_start("grad_%s_sibling_start" % group, _sibling_copies,
                                              SIBLING_COPIES * len(grads), grads, lands, after=after)
        self.names[group] = names
        return self.sibling[group][-1]

    def _reduce_mid(self, group, after):
        grads, recv1 = _exchange_wait("grad_%s_sibling_wait" % group, _sibling_copies, self.sibling[group], after)
        parts = [_chip_partial(self.ids, g, r, "grad_partial_" + n) for g, r, n in zip(grads, recv1, self.names[group])]
        self.own[group] = [p[1] for p in parts]
        srcs = [p[0] for p in parts]
        lands = [lax.empty((CHIP_COPIES,) + t.shape[1:], BF16) for t in srcs]
        self.chips[group] = _exchange_start("grad_%s_chips_start" % group, _chip_copies, CHIP_COPIES * len(srcs),
                                            srcs, lands)
        return self.chips[group][-1]

    def reduced(self, group, after):
        _, recv2 = _exchange_wait("grad_%s_chips_wait" % group, _chip_copies, self.chips[group], after)
        return list(zip(self.own[group], recv2))

    def first_deps(self):
        return [self.gather_mlp[-1]]

    def mixer_weights(self, after):
        _, (g_a, g_b, g_out) = _exchange_wait("gather_mixer_wait", _peer_copies, self.gather_mixer, after)
        return _cols_to_whole(g_a), _cols_to_whole(g_b), g_out.reshape(D, D)

    def mlp_weights(self, after):
        _, (g_up, g_down) = _exchange_wait("gather_mlp_wait", _peer_copies, self.gather_mlp, after)
        return g_up, g_down.reshape(DFF, D)

    def mlp_grads(self, dw_up_sh, dw_down):
        return [self._reduce_start("mlp", W_NAMES[4:], [dw_up_sh, dw_down.reshape((N_DEV,) + SHARD_SHAPES[5])])]

    def mixer_grads(self, dw_a, dw_b, dw_out):
        token = self._reduce_mid("mlp", dw_b)
        grads = [_whole_to_cols(dw_a), _whole_to_cols(dw_b), dw_out.reshape((N_DEV,) + SHARD_SHAPES[3])]
        return [self._reduce_start("mixer", W_NAMES[1:4], grads, after=(token,))]

    def mid_backward(self, after):
        return [self._reduce_mid("mixer", after)]

    def w_in_grad(self, dw_in_t):
        grads = [_unpad_dw_in_t(dw_in_t)]
        token = self._reduce_start("w_in", W_NAMES[:1], grads)
        reduced = self.reduced("mixer", token) + self.reduced("mlp", token)
        self.updates, last = [None] * len(reduced), token
        for i in (3, 4, 0, 1, 2):
            if i == 0:
                last = self._reduce_mid("w_in", last)
            self.updates[i] = _reduce_adamw(*reduced[i], self.w_sh[1 + i], self.m_sh[1 + i], self.v_sh[1 + i],
                                            "adamw_" + W_NAMES[1 + i], deps=[last])
            last = self.updates[i][0]
        return [last]

    def w_in_update(self, after):
        (reduced,) = self.reduced("w_in", after)
        return _reduce_adamw(*reduced, self.w_sh[0], self.m_sh[0], self.v_sh[0], "adamw_" + W_NAMES[0])


def kernel(x, norm_attn_g, w_in, b_forget, w_branch_a, w_branch_b, w_out, norm_mlp_g, w_up, w_down, norm_final_g, loss_target, m_norm_attn_g, m_w_in, m_b_forget, m_w_branch_a, m_w_branch_b, m_w_out, m_norm_mlp_g, m_w_up, m_w_down, m_norm_final_g, v_norm_attn_g, v_w_in, v_b_forget, v_w_branch_a, v_w_branch_b, v_w_out, v_norm_mlp_g, v_w_up, v_w_down, v_norm_final_g):
    cx, cy, cc = _place()
    ids = jnp.stack([cc, 2 * cx + cy]).astype(jnp.int32)

    w_sh = [w_in[0].T] + [t[0] for t in (w_branch_a, w_branch_b, w_out, w_up, w_down)]
    m_sh = [m_w_in[0].T] + [t[0] for t in (m_w_branch_a, m_w_branch_b, m_w_out, m_w_up, m_w_down)]
    v_sh = [v_w_in[0].T] + [t[0] for t in (v_w_branch_a, v_w_branch_b, v_w_out, v_w_up, v_w_down)]

    w_in_t, h1, *lands = _gather_w_in(jnp.pad(w_sh[0], ((0, W_SLAB - w_sh[0].shape[0]), (0, 0))), x[0], norm_attn_g,
                                      w_sh[1:])
    comm = _StepComm(ids, w_sh, m_sh, v_sh, lands, w_in_t)
    b_pad = jnp.pad(b_forget, ((0, 0), (0, LANES - FOX_H)))

    loss_row, dx, dsmall = _local_step(
        x[0], loss_target[0], h1, norm_attn_g, norm_mlp_g, norm_final_g.reshape(1, D), b_pad, w_in_t, comm)

    dg_attn, db, dg_mlp, dg_final = dsmall
    small = _small_allreduce_adamw(
        _pack_small(dg_attn, db, dg_mlp, dg_final, loss_row),
        _pack_small(norm_attn_g, b_forget, norm_mlp_g, norm_final_g.reshape(1, D)),
        _pack_small(m_norm_attn_g, m_b_forget, m_norm_mlp_g, m_norm_final_g.reshape(1, D)),
        _pack_small(v_norm_attn_g, v_b_forget, v_norm_mlp_g, v_norm_final_g.reshape(1, D)))
    big = [comm.w_in_update(small[0])] + comm.updates

    outs = [small[0][LOSS_ROW, 0], dx[None]]
    for q in range(4):
        s_attn, s_b, s_mlp, s_final = _unpack_small(small[q])
        b_in, b_a, b_b, b_out, b_up, b_down = [t[q][None] for t in big]
        b_in = jnp.swapaxes(b_in, 1, 2)
        outs += [s_attn, b_in, s_b, b_a, b_b, b_out, s_mlp, b_up, b_down, s_final]
    return tuple(outs)
```

```python
import functools

import jax
import jax.numpy as jnp
from jax import lax
from jax.experimental import pallas as pl
from jax.experimental.pallas import tpu as pltpu

F32 = jnp.float32
BF16 = jnp.bfloat16
MESH = pl.DeviceIdType.MESH

S = 2048
D = 1024
HD = 64
FOX_H = 8
FOX_W = FOX_H * HD
DIL_HG = 4
DIL_G = 3
DIL_W = DIL_G * DIL_HG * HD
DIL_OUT = DIL_HG * HD
DIL_BLK = 128
DIL_R = (1, 4, 16)
DFF = 4 * D
D_IN = 3 * FOX_W + FOX_H + 3 * DIL_W + 2 * D
EPS = 1e-6
NEG_INF = -1e30
SCALE = HD ** -0.5
ROPE_THETA = 500000.0
ROPE_DIM = HD // 4
N_DEV = 8

ADAM_LR = 0.001
ADAM_B1 = 0.9
ADAM_B2 = 0.999
ADAM_EPS = 1e-08
ADAM_WD = 0.01
ADAM_STEP = 10

LANES = 128
CB = 256
SEG_A = 0
SEG_B = 3 * FOX_W
SEG_Z = SEG_B + 3 * DIL_W
SEG_G = SEG_Z + CB
SEG_F = SEG_G + 2 * D
D_PAD = SEG_F + CB
DW_IN_TM = D_PAD // 5
SMALL_R = 32

VMEM_MB = 56


def _cparams(dims=None, vmem_mb=VMEM_MB, **kw):
    return pltpu.CompilerParams(dimension_semantics=dims, vmem_limit_bytes=vmem_mb << 20, **kw)


ANY = pl.BlockSpec(memory_space=pl.ANY)

_NN = (((1,), (0,)), ((), ()))
_NT = (((1,), (1,)), ((), ()))
_TN = (((0,), (0,)), ((), ()))


def _dot(a, b, dims):
    return lax.dot_general(a.astype(BF16), b.astype(BF16), dims, preferred_element_type=F32)


def _mm(a, b, *, mode, tm, tn, out_dtypes, name, n=None, k=None, a_off=0, b_off=0,
        b_sharded=False, out_sharded=False, epilogue=None, extras=(), deps=()):
    n_sh = b.shape[-1] if b_sharded else None
    if mode == "nn":
        m = a.shape[0]
        k = k or a.shape[1]
        a_spec = pl.BlockSpec((tm, k), lambda i, j: (i, a_off))
        if b_sharded:
            assert tn == n_sh
            n = N_DEV * n_sh
            b_spec = pl.BlockSpec((None, k, tn), lambda i, j: (j, 0, 0))
        else:
            n = n or b.shape[1]
            b_spec = pl.BlockSpec((k, tn), lambda i, j: (0, j + b_off))
        dims = _NN
    elif mode == "nt":
        m = a.shape[0]
        k = k or a.shape[1]
        a_spec = pl.BlockSpec((tm, k), lambda i, j: (i, a_off))
        if b_sharded:
            n = b.shape[1]
            b_spec = pl.BlockSpec((N_DEV, tn, n_sh), lambda i, j: (0, j, 0))
        else:
            n = n or b.shape[0]
            b_spec = pl.BlockSpec((tn, k), lambda i, j: (j + b_off, 0))
        dims = _NT
    else:
        k, m = a.shape
        n = n or b.shape[1]
        a_spec = pl.BlockSpec((k, tm), lambda i, j: (0, i))
        b_spec = pl.BlockSpec((k, tn), lambda i, j: (0, j + b_off))
        dims = _TN
    assert m % tm == 0 and n % tn == 0, (name, m, n, tm, tn)
    n_extra = len(extras)
    tile = pl.BlockSpec((tm, tn), lambda i, j: (i, j))
    if out_sharded:
        assert mode == "tn" and n // tn == N_DEV
        out_spec = pl.BlockSpec((None, tm, tn), lambda i, j: (j, i, 0))
        out_shape = (N_DEV, m, tn)
    else:
        out_spec, out_shape = tile, (m, n)

    def body(a_ref, b_ref, *refs):
        if mode == "nt" and b_sharded:
            acc = _dot(a_ref[:, 0:n_sh], b_ref[0], dims)
            for p in range(1, N_DEV):
                acc = acc + _dot(a_ref[:, p * n_sh:(p + 1) * n_sh], b_ref[p], dims)
        else:
            acc = _dot(a_ref[...], b_ref[...], dims)
        ex = [r[...] for r in refs[:n_extra]]
        outs = epilogue(acc, *ex) if epilogue is not None else (acc,)
        for o_ref, o in zip(refs[n_extra + len(deps):], outs):
            o_ref[...] = o.astype(o_ref.dtype)

    res = pl.pallas_call(
        body, name=name, grid=(m // tm, n // tn),
        in_specs=[a_spec, b_spec] + [tile] * n_extra + [ANY] * len(deps),
        out_specs=[out_spec] * len(out_dtypes),
        out_shape=[jax.ShapeDtypeStruct(out_shape, dt) for dt in out_dtypes],
        compiler_params=_cparams(("parallel", "parallel")),
    )(a, b, *extras, *deps)
    return res if len(out_dtypes) > 1 else res[0]


def _mm_rows(a, b, *, tm, name, epilogue, rows=(), vecs=(), row_out=(), vec_out=(), b_sharded=False, deps=()):
    m, k = a.shape
    n_rows, n_vecs, n_deps = len(rows), len(vecs), len(deps)
    n_sh = b.shape[-1] if b_sharded else None

    def body(a_ref, b_ref, *refs):
        row_refs, vec_refs = refs[:n_rows], refs[n_rows:n_rows + n_vecs]
        outs = refs[n_rows + n_vecs + n_deps:]
        if b_sharded:
            acc = _dot(a_ref[:, 0:n_sh], b_ref[0], _NT)
            for p in range(1, N_DEV):
                acc = acc + _dot(a_ref[:, p * n_sh:(p + 1) * n_sh], b_ref[p], _NT)
        else:
            acc = _dot(a_ref[...], b_ref[...], _NN)
        row_vals, vec_incs = epilogue(acc, [r[...] for r in row_refs], [v[...] for v in vec_refs])
        for o_ref, val in zip(outs[:len(row_out)], row_vals):
            o_ref[...] = val.astype(o_ref.dtype)

        @pl.when(pl.program_id(0) == 0)
        def _():
            for o_ref in outs[len(row_out):]:
                o_ref[...] = jnp.zeros_like(o_ref)

        for o_ref, inc in zip(outs[len(row_out):], vec_incs):
            o_ref[...] += inc

    tile = pl.BlockSpec((tm, D), lambda i: (i, 0))
    b_spec = pl.BlockSpec(b.shape, lambda i: (0,) * b.ndim)
    return pl.pallas_call(
        body, name=name, grid=(m // tm,),
        in_specs=[pl.BlockSpec((tm, k), lambda i: (i, 0)), b_spec] + [tile] * n_rows
                 + [pl.BlockSpec(v.shape, lambda i: (0, 0)) for v in vecs] + [ANY] * n_deps,
        out_specs=[tile] * len(row_out) + [pl.BlockSpec((1, w), lambda i: (0, 0)) for w in vec_out],
        out_shape=[jax.ShapeDtypeStruct((m, D), dt) for dt in row_out]
                  + [jax.ShapeDtypeStruct((1, w), F32) for w in vec_out],
        compiler_params=_cparams(("arbitrary",)),
    )(a, b, *rows, *vecs, *deps)


ROW_T = 256
ROWS_TM = 512


def _rms_rows(x, g):
    r = lax.rsqrt(jnp.mean(x * x, axis=-1, keepdims=True) + EPS)
    return (x * r) * g


def _rms_bwd_rows(dh, x, dres, g):
    r = lax.rsqrt(jnp.mean(x * x, axis=-1, keepdims=True) + EPS)
    xn = x * r
    dhn = dh * g
    dx = dres + r * (dhn - xn * jnp.mean(dhn * xn, axis=-1, keepdims=True))
    return dx, jnp.sum(dh * xn, axis=0, keepdims=True)


def _final_loss_rows(x, g, tgt):
    r = lax.rsqrt(jnp.mean(x * x, axis=-1, keepdims=True) + EPS)
    xn = x * r
    err = xn * g - tgt
    row_loss = jnp.mean(err * err, axis=-1, keepdims=True)
    loss = 0.5 * jnp.sum(row_loss, axis=0, keepdims=True) * jnp.ones((1, LANES), F32)
    dy = err * (1.0 / D)
    dyn = dy * g
    dx = r * (dyn - xn * jnp.mean(dyn * xn, axis=-1, keepdims=True))
    return dx, jnp.sum(dy * xn, axis=0, keepdims=True), loss


def _sigmoid(z):
    return 1.0 / (1.0 + jnp.exp(-z))


GATE_TR = 512


def _mixer_tail(oa, ob, w_a, w_b, w_out, proj_g, x, g_mlp):
    def body(oa_ref, ob_ref, wa_ref, wb_ref, wo_ref, g_ref, x_ref, gm_ref, ya_ref, yb_ref, mixed_ref, x2_ref, h2_ref):
        ya = _dot(oa_ref[...], wa_ref[...], _NN)
        yb = _dot(ob_ref[...], wb_ref[...], _NN)
        ya_ref[...] = ya
        yb_ref[...] = yb
        mixed = (_sigmoid(g_ref[:, :D]) * ya + _sigmoid(g_ref[:, D:]) * yb).astype(BF16)
        mixed_ref[...] = mixed
        x2 = x_ref[...] + _dot(mixed, wo_ref[...], _NN)
        x2_ref[...] = x2
        h2_ref[...] = _rms_rows(x2, gm_ref[...]).astype(BF16)

    def rows(width):
        return pl.BlockSpec((GATE_TR, width), lambda i: (i, 0))

    def whole(t):
        return pl.BlockSpec(t.shape, lambda i: (0, 0))

    return pl.pallas_call(
        body, name="mixer_tail", grid=(S // GATE_TR,),
        in_specs=[rows(FOX_W), rows(DIL_OUT), whole(w_a), whole(w_b), whole(w_out), rows(2 * D), rows(D), whole(g_mlp)],
        out_specs=[rows(D)] * 5,
        out_shape=[jax.ShapeDtypeStruct((S, D), dt) for dt in (F32, F32, BF16, F32, BF16)],
        compiler_params=_cparams(("parallel",)),
    )(oa, ob, w_a, w_b, w_out, proj_g, x, g_mlp)


def _out_proj_gate_bwd(dx2, w_out, proj_g, ya, yb):
    def body(dx_ref, w_ref, g_ref, ya_ref, yb_ref, dy_ref, dg_ref):
        dm = _dot(dx_ref[...], w_ref[...], _NT)
        for half, y_ref in enumerate((ya_ref, yb_ref)):
            cols = slice(half * D, (half + 1) * D)
            s = _sigmoid(g_ref[:, cols])
            dy_ref[:, cols] = (dm * s).astype(BF16)
            dg_ref[:, cols] = (dm * y_ref[...] * (s * (1.0 - s))).astype(BF16)

    row = pl.BlockSpec((GATE_TR, D), lambda i: (i, 0))
    wide = pl.BlockSpec((GATE_TR, 2 * D), lambda i: (i, 0))
    return pl.pallas_call(
        body, name="out_proj_gate_bwd", grid=(S // GATE_TR,),
        in_specs=[row, pl.BlockSpec((D, D), lambda i: (0, 0)), wide, row, row],
        out_specs=[wide, pl.BlockSpec((GATE_TR, 2 * D), lambda i: (i, SEG_G // (2 * D)))],
        out_shape=[jax.ShapeDtypeStruct((S, 2 * D), BF16), jax.ShapeDtypeStruct((S, D_PAD), BF16)],
        compiler_params=_cparams(("parallel",)),
    )(dx2, w_out, proj_g, ya, yb)


FOX_TQ = 256


def _scan_rows(x, reverse):
    n = x.shape[0]
    row = lax.broadcasted_iota(jnp.int32, x.shape, 0)
    k = 1
    while k < n:
        if reverse:
            x = x + jnp.where(row < n - k, pltpu.roll(x, n - k, 0), 0.0)
        else:
            x = x + jnp.where(row >= k, pltpu.roll(x, k, 0), 0.0)
        k *= 2
    return x


def _fox_dscan(dft, dfs, proj_f, b_pad, dproj):
    def body(dft_ref, dfs_ref, f_ref, b_ref, _, dfa_ref, db_ref):
        dfs_pad = jnp.concatenate([dfs_ref[...], jnp.zeros((LANES - FOX_H, S), F32)], axis=0)
        df = dfs_pad.T + dft_ref[0]
        for hp in range(1, dft_ref.shape[0]):
            df = df + pltpu.roll(dft_ref[hp], 2 * hp, 1)
        dlf = _scan_rows(df, reverse=True)
        z = f_ref[...] + b_ref[...]
        lane = lax.broadcasted_iota(jnp.int32, (S, LANES), 1)
        dfa = jnp.where(lane < FOX_H, dlf / (1.0 + jnp.exp(z)), 0.0)
        dfa_ref[:, :LANES] = dfa.astype(BF16)
        dfa_ref[:, LANES:] = jnp.zeros((S, CB - LANES), BF16)
        db_ref[...] = jnp.sum(dfa, axis=0, keepdims=True)

    return pl.pallas_call(
        body, name="fox_dscan", grid=(1,),
        in_specs=[pl.BlockSpec(dft.shape, lambda i: (0, 0, 0)), pl.BlockSpec((FOX_H, S), lambda i: (0, 0)),
                  pl.BlockSpec((S, LANES), lambda i: (0, 0)), pl.BlockSpec((1, LANES), lambda i: (0, 0)), ANY],
        out_specs=[pl.BlockSpec((S, CB), lambda i: (0, SEG_F // CB)), pl.BlockSpec((1, LANES), lambda i: (0, 0))],
        out_shape=[jax.ShapeDtypeStruct((S, D_PAD), BF16), jax.ShapeDtypeStruct((1, LANES), F32)],
        input_output_aliases={4: 0},
        compiler_params=_cparams(("arbitrary",)),
    )(dft, dfs, proj_f, b_pad, dproj)


FOX_NQ = S // FOX_TQ
FOX_HP = FOX_W // LANES
HEADS_PER_LB = LANES // HD
FOX_AUG = 2 * LANES


def _fox_prepare(proj_a, proj_f, b_pad):
    tr = 512

    def body(q_ref, k_ref, f_ref, b_ref, qa_ref, ka_ref, hi_s, mid_s, lo_s):
        i = pl.program_id(0)

        @pl.when(i == 0)
        def _():
            z = f_ref[...] + b_ref[...]
            lf = jnp.minimum(z, 0.0) - jnp.log1p(jnp.exp(-jnp.abs(z)))
            f = _scan_rows(lf, reverse=False)
            hi = f.astype(BF16).astype(F32)
            r1 = f - hi
            mid = r1.astype(BF16).astype(F32)
            hi_s[...] = hi
            mid_s[...] = mid
            lo_s[...] = (r1 - mid).astype(BF16).astype(F32)

        rows = pl.ds(pl.multiple_of(i * tr, tr), tr)
        hi, mid, lo = hi_s[rows, :], mid_s[rows, :], lo_s[rows, :]
        lane = lax.broadcasted_iota(jnp.int32, (tr, HD), 1)
        q_ones = jnp.where((lane >= 3) & (lane < 6), 1.0, 0.0)
        k_ones = jnp.where(lane < 3, 1.0, 0.0)
        for h in range(FOX_H):
            a1, a2, a3 = hi[:, h:h + 1], mid[:, h:h + 1], lo[:, h:h + 1]
            q_extra = jnp.where(lane == 0, a1, jnp.where(lane == 1, a2, jnp.where(lane == 2, a3, q_ones)))
            k_extra = jnp.where(lane == 3, -a1, jnp.where(lane == 4, -a2, jnp.where(lane == 5, -a3, k_ones)))
            base = h * LANES
            qa_ref[:, base:base + HD] = q_ref[:, h * HD:(h + 1) * HD] * jnp.asarray(SCALE, BF16)
            qa_ref[:, base + HD:base + LANES] = q_extra.astype(BF16)
            ka_ref[:, base:base + HD] = k_ref[:, h * HD:(h + 1) * HD]
            ka_ref[:, base + HD:base + LANES] = k_extra.astype(BF16)

    out_blk = pl.BlockSpec((tr, FOX_H * LANES), lambda i: (i, 0))
    return pl.pallas_call(
        body, name="fox_prepare", grid=(S // tr,),
        in_specs=[pl.BlockSpec((tr, FOX_W), lambda i: (i, 0)), pl.BlockSpec((tr, FOX_W), lambda i: (i, 1)),
                  pl.BlockSpec((S, LANES), lambda i: (0, 0)), pl.BlockSpec((1, LANES), lambda i: (0, 0))],
        out_specs=[out_blk, out_blk],
        out_shape=[jax.ShapeDtypeStruct((S, FOX_H * LANES), BF16)] * 2,
        scratch_shapes=[pltpu.VMEM((S, LANES), F32)] * 3,
        compiler_params=_cparams(("arbitrary",)),
    )(proj_a, proj_a, proj_f, b_pad)


def _fox_scores(qa, ka, w):
    s = _dot(qa, ka, _NT)
    row = lax.broadcasted_iota(jnp.int32, (FOX_TQ, FOX_TQ), 0)
    col = lax.broadcasted_iota(jnp.int32, (FOX_TQ, FOX_TQ), 1)
    diag = jnp.where(col <= row, s[:, w * FOX_TQ:], NEG_INF)
    return diag if w == 0 else jnp.concatenate([s[:, :w * FOX_TQ], diag], axis=1)


def _fox_fwd(q_aug, k_aug, proj_a):
    def body(qa_ref, ka_ref, v_ref, o_ref, lse_ref):
        qi = pl.program_id(1)
        lane = lax.broadcasted_iota(jnp.int32, (FOX_TQ, LANES), 1)
        for w in range(FOX_NQ):
            @pl.when(qi == w)
            def _(w=w):
                width = (w + 1) * FOX_TQ
                lse = jnp.zeros((FOX_TQ, LANES), F32)
                for hh in range(HEADS_PER_LB):
                    aug = slice(hh * LANES, (hh + 1) * LANES)
                    sl = slice(hh * HD, (hh + 1) * HD)
                    s = _fox_scores(qa_ref[:, aug], ka_ref[:width, aug], w)
                    m = jnp.max(s, axis=-1, keepdims=True)
                    p = jnp.exp(s - m)
                    l = jnp.sum(p, axis=-1, keepdims=True)
                    o_ref[:, sl] = _dot(p / l, v_ref[:width, sl], _NN)
                    lse = jnp.where(lane == hh, m + jnp.log(l), lse)
                lse_ref[...] = lse

    return pl.pallas_call(
        body, name="fox_fwd", grid=(FOX_HP, FOX_NQ),
        in_specs=[pl.BlockSpec((FOX_TQ, FOX_AUG), lambda h, i: (i, h)),
                  pl.BlockSpec((S, FOX_AUG), lambda h, i: (0, h)),
                  pl.BlockSpec((S, LANES), lambda h, i: (0, 2 * FOX_HP + h))],
        out_specs=[pl.BlockSpec((FOX_TQ, LANES), lambda h, i: (i, h)),
                   pl.BlockSpec((None, FOX_TQ, LANES), lambda h, i: (h, i, 0))],
        out_shape=[jax.ShapeDtypeStruct((S, FOX_W), F32), jax.ShapeDtypeStruct((FOX_HP, S, LANES), F32)],
        compiler_params=_cparams(("parallel", "parallel")),
    )(q_aug, k_aug, proj_a)


def _fox_bwd(q_aug, k_aug, proj_a, lse, do, dproj, deps=()):
    n_q = FOX_NQ

    def body(qa_ref, ka_ref, v_ref, lse_ref, do_ref, *refs):
        dproj_ref, dft_ref, dfs_ref, dk_acc, dv_acc, dq_buf, kv_buf, dq_sems, kv_sems = refs[1 + len(deps):]
        hp = pl.program_id(0)
        t = pl.program_id(1)
        slot = t % 2
        col = pl.multiple_of(hp * LANES, LANES)

        def dq_copy(step, buf_slot):
            rows = pl.ds(pl.multiple_of(step * FOX_TQ, FOX_TQ), FOX_TQ)
            return pltpu.make_async_copy(dq_buf.at[buf_slot], dproj_ref.at[rows, pl.ds(col, LANES)], dq_sems.at[buf_slot])

        @pl.when(t == 0)
        def _():
            dk_acc[...] = jnp.zeros_like(dk_acc)
            dv_acc[...] = jnp.zeros_like(dv_acc)

        @pl.when((t == 0) & (hp == 0))
        def _():
            dfs_ref[...] = jnp.zeros_like(dfs_ref)

        @pl.when(t >= 2)
        def _():
            dq_copy(t - 2, slot).wait()

        lane = lax.broadcasted_iota(jnp.int32, (FOX_TQ, LANES), 1)
        for w in range(n_q):
            @pl.when(t == w)
            def _(w=w):
                width = (w + 1) * FOX_TQ
                lse_all = lse_ref[...]
                sub = lax.broadcasted_iota(jnp.int32, (FOX_H, width), 0)
                dft = jnp.zeros((FOX_TQ, LANES), F32)
                for hh in range(HEADS_PER_LB):
                    aug = slice(hh * LANES, (hh + 1) * LANES)
                    sl = slice(hh * HD, (hh + 1) * HD)
                    head = hp * HEADS_PER_LB + hh
                    qa = qa_ref[:, aug]
                    ka = ka_ref[:width, aug]
                    do_h = do_ref[:, sl]
                    s = _fox_scores(qa, ka, w)
                    p = jnp.exp(s - lse_all[:, hh:hh + 1])
                    dp = _dot(do_h, v_ref[:width, sl], _NT)
                    ds = p * (dp - jnp.sum(dp * p, axis=-1, keepdims=True))
                    dft = jnp.where(lane == hh, jnp.sum(ds, axis=-1, keepdims=True), dft)
                    dfs_ref[:, :width] -= jnp.where(sub == head, jnp.sum(ds, axis=0, keepdims=True), 0.0)
                    dq_buf[slot, :, sl] = (_dot(ds, ka[:, :HD], _NN) * SCALE).astype(BF16)
                    dk_acc[:width, sl] += _dot(ds, qa[:, :HD], _TN)
                    dv_acc[:width, sl] += _dot(p, do_h, _TN)
                dft_ref[...] = dft

        dq_copy(t, slot).start()

        @pl.when(t == n_q - 1)
        def _():
            dq_copy(t - 1, 1 - slot).wait()
            dq_copy(t, slot).wait()
            kv_buf[0] = dk_acc[...].astype(BF16)
            kv_buf[1] = dv_acc[...].astype(BF16)
            copies = [pltpu.make_async_copy(
                kv_buf.at[j], dproj_ref.at[:, pl.ds(pl.multiple_of((j + 1) * FOX_W + hp * LANES, LANES), LANES)],
                kv_sems.at[j]) for j in range(2)]
            for cp in copies:
                cp.start()
            for cp in copies:
                cp.wait()

    qblk = pl.BlockSpec((FOX_TQ, FOX_AUG), lambda h, t: (t, h))
    lane_blk = pl.BlockSpec((None, FOX_TQ, LANES), lambda h, t: (h, t, 0))
    return pl.pallas_call(
        body, name="fox_bwd", grid=(FOX_HP, n_q),
        in_specs=[qblk, pl.BlockSpec((S, FOX_AUG), lambda h, t: (0, h)),
                  pl.BlockSpec((S, LANES), lambda h, t: (0, 2 * FOX_HP + h)),
                  lane_blk, pl.BlockSpec((FOX_TQ, LANES), lambda h, t: (t, h)), ANY] + [ANY] * len(deps),
        out_specs=[ANY, lane_blk, pl.BlockSpec((FOX_H, S), lambda h, t: (0, 0))],
        out_shape=[jax.ShapeDtypeStruct((S, D_PAD), BF16),
                   jax.ShapeDtypeStruct((FOX_HP, S, LANES), F32), jax.ShapeDtypeStruct((FOX_H, S), F32)],
        scratch_shapes=[pltpu.VMEM((S, LANES), F32), pltpu.VMEM((S, LANES), F32),
                        pltpu.VMEM((2, FOX_TQ, LANES), BF16), pltpu.VMEM((2, S, LANES), BF16),
                        pltpu.SemaphoreType.DMA((2,)), pltpu.SemaphoreType.DMA((2,))],
        input_output_aliases={5: 0},
        compiler_params=_cparams(("arbitrary", "arbitrary")),
    )(q_aug, k_aug, proj_a, lse, do, dproj, *deps)


def _rope_tables():
    half = ROPE_DIM // 2
    shape = (DIL_G, S, half)
    g = lax.broadcasted_iota(jnp.int32, shape, 0)
    row = lax.broadcasted_iota(jnp.int32, shape, 1)
    r = jnp.left_shift(1, 2 * g)
    per_class = S // r
    pos = (row % per_class) * r + row // per_class
    inv_freq = jnp.power(jnp.float32(ROPE_THETA), -lax.broadcasted_iota(F32, shape, 2) * 2.0 / ROPE_DIM)
    ang = pos.astype(F32) * inv_freq
    return jnp.concatenate([jnp.cos(ang), jnp.sin(ang)], axis=-1)


def _rope_factors(cs):
    half = ROPE_DIM // 2
    i = lax.broadcasted_iota(jnp.int32, (2 * half, 2 * LANES), 0)
    l = lax.broadcasted_iota(jnp.int32, (2 * half, 2 * LANES), 1)
    hit = ((l < LANES) & (i == l % half)) | ((l >= LANES) & (i - half == l % half))
    spread = jnp.where(hit, 1.0, 0.0).astype(BF16)
    hi = cs.astype(BF16)
    r1 = cs - hi.astype(F32)
    mid = r1.astype(BF16)
    lo = (r1 - mid.astype(F32)).astype(BF16)
    full = (_dot(hi, spread, _NN) + _dot(mid, spread, _NN)) + _dot(lo, spread, _NN)
    cos, sin = full[:, :LANES], full[:, LANES:]
    lane = lax.broadcasted_iota(jnp.int32, cos.shape, 1) % HD
    t_self = jnp.where(lane < ROPE_DIM, cos, 1.0)
    t_up = jnp.where(lane < half, -sin, 0.0)
    t_dn = jnp.where((lane >= half) & (lane < ROPE_DIM), sin, 0.0)
    return t_self, t_up, t_dn


def _residue_pieces(r):
    if r == 1:
        return [(slice(i, i + 512), slice(i, i + 512)) for i in range(0, S, 512)]
    n = S // r
    return [(pl.ds(j, n, stride=r), slice(j * n, (j + 1) * n)) for j in range(r)]


def _rope_apply(x, a, u, d, backward):
    half = ROPE_DIM // 2
    if backward:
        return x * a + pltpu.roll(x * u, half, 1) + pltpu.roll(x * d, LANES - half, 1)
    return x * a + pltpu.roll(x, LANES - half, 1) * u + pltpu.roll(x, half, 1) * d


def _rope_cache_factors(cs_ref, fac, first_step_of_group):
    @pl.when(first_step_of_group)
    def _():
        for lo in range(0, S, 512):
            for j, f in enumerate(_rope_factors(cs_ref[lo:lo + 512, :])):
                fac[j, lo:lo + 512, :] = f


LB_PER_CB = CB // LANES
ROPE_NB = 3 * DIL_G * LB_PER_CB


def _rope_step(s):
    per_group = 3 * LB_PER_CB
    return s // per_group, (s % per_group) // LB_PER_CB, s % LB_PER_CB


def _rope_split(proj_b, tables):
    def body(x_ref, cs_ref, o_ref, fac):
        g, t, hf = _rope_step(pl.program_id(0))
        _rope_cache_factors(cs_ref, fac, (t == 0) & (hf == 0))
        for gs in range(DIL_G):
            @pl.when(g == gs)
            def _(gs=gs):
                for tok, sub in _residue_pieces(DIL_R[gs]):
                    x = x_ref[tok, :]
                    y = _rope_apply(x, fac[0, sub, :], fac[1, sub, :], fac[2, sub, :], False)
                    o_ref[sub, :] = jnp.where(t < 2, y, x).astype(BF16)

    def in_index(s):
        g, t, hf = _rope_step(s)
        return 0, (t * DIL_G + g) * LB_PER_CB + hf

    def out_index(s):
        g, t, hf = _rope_step(s)
        return t * DIL_G + g, 0, hf

    tab = pl.BlockSpec((None, S, 2 * (ROPE_DIM // 2)), lambda s: (_rope_step(s)[0], 0, 0))
    return pl.pallas_call(
        body, name="rope_split", grid=(ROPE_NB,),
        in_specs=[pl.BlockSpec((S, LANES), in_index), tab],
        out_specs=pl.BlockSpec((None, S, LANES), out_index),
        out_shape=jax.ShapeDtypeStruct((3 * DIL_G, S, CB), BF16),
        scratch_shapes=[pltpu.VMEM((3, S, LANES), F32)],
        compiler_params=_cparams(("arbitrary",)),
    )(proj_b, tables)


def _rope_merge_bwd(dq3, dk3, dv3, tables, dproj):
    def body(dq_ref, dk_ref, dv_ref, cs_ref, _, o_ref, tmp, fac):
        s = pl.program_id(0)
        g, t, hf = _rope_step(s)
        _rope_cache_factors(cs_ref, fac, (t == 0) & (hf == 0) & (s < ROPE_NB))
        for gs in range(DIL_G):
            @pl.when((g == gs) & (s < ROPE_NB))
            def _(gs=gs):
                for tok, sub in _residue_pieces(DIL_R[gs]):
                    x = jnp.where(t == 0, dq_ref[sub, :], jnp.where(t == 1, dk_ref[sub, :], dv_ref[sub, :]))
                    y = _rope_apply(x, fac[0, sub, :], fac[1, sub, :], fac[2, sub, :], True)
                    tmp[tok, :] = jnp.where(t < 2, y, x)
                o_ref[...] = tmp[...].astype(BF16)

        @pl.when(s >= ROPE_NB)
        def _():
            o_ref[...] = jnp.zeros_like(o_ref)

    def src_spec(own):
        def index(s):
            g, t, hf = _rope_step(jnp.minimum(s, ROPE_NB - 1))
            return g, 0, jnp.where(t == own, hf, jnp.where(t > own, LB_PER_CB - 1, 0))
        return pl.BlockSpec((None, S, LANES), index)

    def out_index(s):
        g, t, hf = _rope_step(s)
        col = SEG_B // LANES + (t * DIL_G + g) * LB_PER_CB + hf
        return 0, jnp.where(s < ROPE_NB, col, SEG_Z // LANES + s - ROPE_NB)

    tab = pl.BlockSpec((None, S, 2 * (ROPE_DIM // 2)), lambda s: (_rope_step(jnp.minimum(s, ROPE_NB - 1))[0], 0, 0))
    return pl.pallas_call(
        body, name="rope_merge_bwd", grid=(ROPE_NB + LB_PER_CB,),
        in_specs=[src_spec(0), src_spec(1), src_spec(2), tab, ANY],
        out_specs=pl.BlockSpec((S, LANES), out_index),
        out_shape=jax.ShapeDtypeStruct((S, D_PAD), BF16),
        scratch_shapes=[pltpu.VMEM((S, LANES), F32), pltpu.VMEM((3, S, LANES), F32)],
        input_output_aliases={4: 0},
        compiler_params=_cparams(("arbitrary",)),
    )(dq3, dk3, dv3, tables, dproj)


DIL_NB = S // DIL_BLK


_BQK = (((2,), (2,)), ((0,), (0,)))
_BQD = (((2,), (1,)), ((0,), (0,)))
_BKD = (((1,), (1,)), ((0,), (0,)))


def _dil_mask(g):
    shape = (DIL_NB, DIL_BLK, 2 * DIL_BLK)
    blocks_per_seq = jnp.right_shift(DIL_NB, 2 * g)
    has_prev = jnp.bitwise_and(lax.broadcasted_iota(jnp.int32, shape, 0), blocks_per_seq - 1) != 0
    a = lax.broadcasted_iota(jnp.int32, shape, 1)
    kk = lax.broadcasted_iota(jnp.int32, shape, 2)
    diff = DIL_BLK + a - kk
    return (diff >= 0) & (diff <= DIL_BLK) & ((kk >= DIL_BLK) | has_prev)


def _blocks(t):
    return t.reshape(DIL_NB, DIL_BLK, t.shape[-1])


def _with_prev(t):
    prev = jnp.concatenate([jnp.zeros((DIL_BLK, t.shape[-1]), t.dtype), t[:-DIL_BLK]], axis=0)
    return jnp.concatenate([_blocks(prev), _blocks(t)], axis=1)


def _fold_prev(t2):
    n = t2.shape[-1]
    to_prev = t2[:, :DIL_BLK].reshape(S, n)
    own = t2[:, DIL_BLK:].reshape(S, n)
    return own + jnp.concatenate([to_prev[DIL_BLK:], jnp.zeros((DIL_BLK, n), t2.dtype)], axis=0)


def _dil_group_spec(t, width=DIL_OUT):
    return pl.BlockSpec((None, S, width), lambda g: (t * DIL_G + g, 0, 0))


def _dil_fwd(qkv3):
    def body(q_ref, k_ref, v_ref, o_ref, lse_ref):
        ok = _dil_mask(pl.program_id(0))
        lane = lax.broadcasted_iota(jnp.int32, (S, LANES), 1)
        lse = jnp.zeros((S, LANES), F32)
        for h in range(DIL_HG):
            sl = slice(h * HD, (h + 1) * HD)
            s = jnp.where(ok, _dot(_blocks(q_ref[:, sl]), _with_prev(k_ref[:, sl]), _BQK) * SCALE, NEG_INF)
            m = jnp.max(s, axis=-1, keepdims=True)
            p = jnp.exp(s - m)
            l = jnp.sum(p, axis=-1, keepdims=True)
            o_ref[:, sl] = _dot(p / l, _with_prev(v_ref[:, sl]), _BQD).reshape(S, HD)
            lse = jnp.where(lane == h, (m + jnp.log(l)).reshape(S, 1), lse)
        lse_ref[...] = lse

    return pl.pallas_call(
        body, name="dil_fwd", grid=(DIL_G,),
        in_specs=[_dil_group_spec(0), _dil_group_spec(1), _dil_group_spec(2)],
        out_specs=[_dil_group_spec(0), _dil_group_spec(0, LANES)],
        out_shape=[jax.ShapeDtypeStruct((DIL_G, S, DIL_OUT), F32), jax.ShapeDtypeStruct((DIL_G, S, LANES), F32)],
        compiler_params=_cparams(("parallel",)),
    )(qkv3, qkv3, qkv3)


def _dil_bwd(qkv3, lse3, do3, c3):
    def body(q_ref, k_ref, v_ref, lse_ref, do_ref, c_ref, dq_ref, dk_ref, dv_ref):
        ok = _dil_mask(pl.program_id(0))
        lse = lse_ref[...]
        c = c_ref[...]
        for h in range(DIL_HG):
            sl = slice(h * HD, (h + 1) * HD)
            q = _blocks(q_ref[:, sl])
            k2 = _with_prev(k_ref[:, sl])
            do_h = _blocks(do_ref[:, sl])
            s = jnp.where(ok, _dot(q, k2, _BQK) * SCALE, NEG_INF)
            p = jnp.exp(s - _blocks(lse[:, h:h + 1]))
            dp = _dot(do_h, _with_prev(v_ref[:, sl]), _BQK)
            ds = p * (dp - _blocks(c[:, h:h + 1]))
            dsb = (ds * SCALE).astype(BF16)
            dq_ref[:, sl] = _dot(dsb, k2, _BQD).reshape(S, HD)
            dk_ref[:, sl] = _fold_prev(_dot(dsb, q, _BKD))
            dv_ref[:, sl] = _fold_prev(_dot(p, do_h, _BKD))

    return pl.pallas_call(
        body, name="dil_bwd", grid=(DIL_G,),
        in_specs=[_dil_group_spec(0), _dil_group_spec(1), _dil_group_spec(2), _dil_group_spec(0, LANES),
                  _dil_group_spec(0), _dil_group_spec(0, LANES)],
        out_specs=[_dil_group_spec(0)] * 3,
        out_shape=[jax.ShapeDtypeStruct((DIL_G, S, DIL_OUT), F32)] * 3,
        compiler_params=_cparams(("parallel",)),
    )(qkv3, qkv3, qkv3, lse3, do3, c3)


COMB_T = 256


def _dil_combine(o3, lse3):
    heads_per_lb = LANES // HD

    def body(o_ref, lse_ref, ob_ref, al_ref, *scratch):
        o_tok = [scratch[LB_PER_CB * gg:LB_PER_CB * (gg + 1)] for gg in range(DIL_G)]
        lse_tok = scratch[LB_PER_CB * DIL_G:]
        g = pl.program_id(0)
        for gs in range(DIL_G):
            @pl.when(g == gs)
            def _(gs=gs):
                for tok, sub in _residue_pieces(DIL_R[gs]):
                    for hf in range(LB_PER_CB):
                        o_tok[gs][hf][tok, :] = o_ref[sub, hf * LANES:(hf + 1) * LANES]
                    lse_tok[gs][tok, :] = lse_ref[sub, :]

        @pl.when(g == DIL_G - 1)
        def _():
            def chunk(i, carry):
                rows = pl.ds(pl.multiple_of(i * COMB_T, COMB_T), COMB_T)
                lse = [lse_tok[gg][rows, :] for gg in range(DIL_G)]
                m = jnp.maximum(jnp.maximum(lse[0], lse[1]), lse[2])
                e = [jnp.exp(lse[gg] - m) for gg in range(DIL_G)]
                den = (e[0] + e[1]) + e[2]
                al = [e[gg] / den for gg in range(DIL_G)]
                for gg in range(DIL_G):
                    al_ref[gg, rows, :] = al[gg]
                for h in range(DIL_HG):
                    hf, sl = h // heads_per_lb, slice((h % heads_per_lb) * HD, (h % heads_per_lb + 1) * HD)
                    acc = al[0][:, h:h + 1] * o_tok[0][hf][rows, sl]
                    for gg in range(1, DIL_G):
                        acc = acc + al[gg][:, h:h + 1] * o_tok[gg][hf][rows, sl]
                    ob_ref[rows, h * HD:(h + 1) * HD] = acc
                return carry

            lax.fori_loop(0, S // COMB_T, chunk, 0)

    return pl.pallas_call(
        body, name="dil_combine", grid=(DIL_G,),
        in_specs=[pl.BlockSpec((None, S, DIL_OUT), lambda g: (g, 0, 0)),
                  pl.BlockSpec((None, S, LANES), lambda g: (g, 0, 0))],
        out_specs=[pl.BlockSpec((S, DIL_OUT), lambda g: (0, 0)),
                   pl.BlockSpec((DIL_G, S, LANES), lambda g: (0, 0, 0))],
        out_shape=[jax.ShapeDtypeStruct((S, DIL_OUT), F32), jax.ShapeDtypeStruct((DIL_G, S, LANES), F32)],
        scratch_shapes=[pltpu.VMEM((S, LANES), F32)] * (DIL_G * (LB_PER_CB + 1)),
        compiler_params=_cparams(("arbitrary",)),
    )(o3, lse3)


def _dil_combine_bwd(dob, ob, alpha, deps=()):
    heads_per_lb = LANES // HD

    def body(dob_ref, ob_ref, al_ref, *refs):
        do_ref, c_ref = refs[len(deps):]
        g = pl.program_id(0)
        hf = pl.program_id(1)
        for gs in range(DIL_G):
            @pl.when(g == gs)
            def _(gs=gs):
                for tok, sub in _residue_pieces(DIL_R[gs]):
                    dob = dob_ref[tok, :]
                    prod = dob * ob_ref[tok, :]
                    al = al_ref[tok, :]
                    lane = lax.broadcasted_iota(jnp.int32, al.shape, 1)
                    c = jnp.where(hf == 0, 0.0, c_ref[sub, :])
                    for hh in range(heads_per_lb):
                        sl = slice(hh * HD, (hh + 1) * HD)
                        head = hf * heads_per_lb + hh
                        a = jnp.sum(jnp.where(lane == head, al, 0.0), axis=-1, keepdims=True)
                        do_ref[sub, sl] = (a * dob[:, sl]).astype(BF16)
                        c = jnp.where(lane == head, a * jnp.sum(prod[:, sl], axis=-1, keepdims=True), c)
                    c_ref[sub, :] = c

    half = pl.BlockSpec((S, LANES), lambda g, hf: (0, hf))
    return pl.pallas_call(
        body, name="dil_combine_bwd", grid=(DIL_G, LB_PER_CB),
        in_specs=[half, half, pl.BlockSpec((None, S, LANES), lambda g, hf: (g, 0, 0))] + [ANY] * len(deps),
        out_specs=[pl.BlockSpec((None, S, LANES), lambda g, hf: (g, 0, hf)),
                   pl.BlockSpec((None, S, LANES), lambda g, hf: (g, 0, 0))],
        out_shape=[jax.ShapeDtypeStruct((DIL_G, S, DIL_OUT), BF16), jax.ShapeDtypeStruct((DIL_G, S, LANES), F32)],
        compiler_params=_cparams(("parallel", "arbitrary")),
    )(dob, ob, alpha, *deps)


def _local_step(x, tgt, h1, g_attn, g_mlp, g_final, b_pad, w_in_t, hooks):
    tables = _rope_tables()

    first = hooks.first_deps()
    proj_a = _mm(h1, w_in_t, mode="nt", tm=S, tn=512, n=3 * FOX_W, out_dtypes=[BF16], name="proj_a", deps=first)
    proj_b = _mm(h1, w_in_t, mode="nt", tm=S, tn=DIL_W, n=3 * DIL_W, b_off=SEG_B // DIL_W, out_dtypes=[F32], name="proj_b",
                deps=first)
    proj_g = _mm(h1, w_in_t, mode="nt", tm=S, tn=D, n=2 * D, b_off=SEG_G // D, out_dtypes=[F32], name="proj_g",
                deps=first)
    proj_f = _mm(h1, w_in_t, mode="nt", tm=S, tn=LANES, n=LANES, b_off=SEG_F // LANES, out_dtypes=[F32], name="proj_f",
                deps=first)

    q_aug, k_aug = _fox_prepare(proj_a, proj_f, b_pad)
    oa, lse_a = _fox_fwd(q_aug, k_aug, proj_a)

    qkv3 = _rope_split(proj_b, tables)
    o3, lse3 = _dil_fwd(qkv3)
    ob, alpha = _dil_combine(o3, lse3)

    w_a, w_b, w_out = hooks.mixer_weights(ob)
    ya, yb, mixed, x2, h2 = _mixer_tail(oa, ob, w_a, w_b, w_out, proj_g, x, g_mlp)
    w_up_sh, w_down = hooks.mlp_weights(h2)

    def up_epilogue(acc):
        r = jnp.maximum(acc, 0.0)
        return acc, r * r

    u, act = _mm(h2, w_up_sh, mode="nn", tm=S, tn=DFF // N_DEV, b_sharded=True, out_dtypes=[F32, BF16],
                 name="mlp_up", epilogue=up_epilogue)

    def loss_epilogue(acc, rows, vecs):
        dx3, dg, loss = _final_loss_rows(rows[0] + acc, vecs[0], rows[1])
        return (dx3,), (dg, loss)

    dx3, dg_final, loss = _mm_rows(act, w_down, tm=ROWS_TM, name="mlp_down_loss", epilogue=loss_epilogue,
                                   rows=(x2, tgt), vecs=(g_final,), row_out=(F32,), vec_out=(D, LANES))

    def rms_bwd_epilogue(acc, rows, vecs):
        dx, dg = _rms_bwd_rows(acc, rows[0], rows[1], vecs[0])
        return (dx,), (dg,)

    du = _mm(dx3, w_down, mode="nt", tm=S, tn=512, out_dtypes=[BF16], name="mlp_down_bwd",
             epilogue=lambda acc, u_t: (acc * (2.0 * jnp.maximum(u_t, 0.0)),), extras=(u,))
    dw_down = _mm(act, dx3, mode="tn", tm=D, tn=D, out_dtypes=[F32], name="dw_down")
    dw_up_sh = _mm(h2, du, mode="tn", tm=D, tn=DFF // N_DEV, out_sharded=True, out_dtypes=[F32], name="dw_up")
    dx2, dg_mlp = _mm_rows(du, w_up_sh, b_sharded=True, tm=ROWS_TM, name="mlp_up_bwd", epilogue=rms_bwd_epilogue,
                           rows=(x2, dx3), vecs=(g_mlp,), row_out=(F32,), vec_out=(D,),
                           deps=hooks.mlp_grads(dw_up_sh, dw_down))

    dw_out = _mm(mixed, dx2, mode="tn", tm=D, tn=D, out_dtypes=[F32], name="dw_out")

    dyab, dproj = _out_proj_gate_bwd(dx2, w_out, proj_g, ya, yb)
    doa = _mm(dyab, w_a, mode="nt", tm=S, tn=FOX_W, k=D, a_off=0, out_dtypes=[BF16], name="branch_a_bwd")
    dw_a = _mm(oa, dyab, mode="tn", tm=FOX_W, tn=D, n=D, b_off=0, out_dtypes=[F32], name="dw_branch_a")
    dob = _mm(dyab, w_b, mode="nt", tm=S, tn=CB, k=D, a_off=1, out_dtypes=[F32], name="branch_b_bwd")
    dw_b = _mm(ob, dyab, mode="tn", tm=DIL_OUT, tn=D, n=D, b_off=1, out_dtypes=[F32], name="dw_branch_b")

    dproj, dft, dfs = _fox_bwd(q_aug, k_aug, proj_a, lse_a, doa, dproj, deps=hooks.mixer_grads(dw_a, dw_b, dw_out))
    dproj, db = _fox_dscan(dft, dfs, proj_f, b_pad, dproj)

    do3, c3 = _dil_combine_bwd(dob, ob, alpha, deps=hooks.mid_backward(db))
    dq3, dk3, dv3 = _dil_bwd(qkv3, lse3, do3, c3)
    dproj = _rope_merge_bwd(dq3, dk3, dv3, tables, dproj)

    dw_in_t = _mm(dproj, h1, mode="tn", tm=DW_IN_TM, tn=D, out_dtypes=[F32], name="dw_in")
    dx, dg_attn = _mm_rows(dproj, w_in_t, tm=ROWS_TM // 2, name="proj_bwd", epilogue=rms_bwd_epilogue,
                           rows=(x, dx2), vecs=(g_attn,), row_out=(F32,), vec_out=(D,),
                           deps=hooks.w_in_grad(dw_in_t))

    return loss, dx, (dg_attn, db, dg_mlp, dg_final)


SHARD_SHAPES = ((D_IN // N_DEV, D), (FOX_W, D // N_DEV), (DIL_OUT, D // N_DEV), (D // N_DEV, D),
                (D, DFF // N_DEV), (DFF // N_DEV, D))
W_NAMES = ("w_in", "w_a", "w_b", "w_out", "w_up", "w_down")
AG_COPIES = 7
HBM = pl.BlockSpec(memory_space=pltpu.HBM)
SEM = pl.BlockSpec(memory_space=pltpu.SEMAPHORE)


def _place():
    return lax.axis_index("x"), lax.axis_index("y"), lax.axis_index("c")


W_SLAB = 752
W_WIN = 272
PAD_BLK = 256


def _pad_runs():
    runs = sorted((pad, pad + b - a, p, a) for p in range(N_DEV) for a, b, pad in _shard_pieces(p))
    blocks = []
    for k in range(D_PAD // PAD_BLK):
        lo, hi = k * PAD_BLK, (k + 1) * PAD_BLK
        blocks.append([(max(lo, r0) - lo, min(hi, r1) - lo, p, a + max(lo, r0) - r0)
                       for r0, r1, p, a in runs if max(lo, r0) < min(hi, r1)])
    return blocks


def _gather_w_in(shard, x_in, g_attn, late_shards):
    blocks = _pad_runs()

    norm_t = 512
    n_chunks = S // norm_t

    n_late = len(late_shards)
    late_shapes = [t.shape for t in late_shards]

    def body(x_ref, xin_ref, g_ref, *refs):
        late_in, (out_ref, h1_ref), late_out = refs[:n_late], refs[n_late:n_late + 2], refs[n_late + 2:2 * n_late + 2]
        scratch = refs[2 * n_late + 2:]
        stage, land, xbuf = scratch[:3]
        late_stage, late_cast = scratch[3:3 + n_late], scratch[3 + n_late:3 + 2 * n_late]
        send_sems, recv_sems, load_sem, x_sems, late_sems, store_sems = scratch[3 + 2 * n_late:]
        x, y, c = _place()
        me, sibling = (x, y, c), (x, y, 1 - c)
        chips = [(1 - x, y), (x, 1 - y), (1 - x, 1 - y)]

        def slot(px, py, pc):
            return land.at[4 * px + 2 * py + pc]

        def copy(k, block, to):
            return pltpu.make_async_remote_copy(
                src_ref=slot(*block), dst_ref=slot(*block), send_sem=send_sems.at[k], recv_sem=recv_sems.at[k],
                device_id=to, device_id_type=MESH)

        load = pltpu.make_async_copy(x_ref, stage, load_sem)
        load.start()
        load.wait()
        land[4 * x + 2 * y + c] = stage[...].astype(BF16)
        sent = [copy(0, me, sibling)] + [copy(1 + j, me, (*chip, c)) for j, chip in enumerate(chips)]
        for cp in sent:
            cp.start()

        def x_load(i):
            return pltpu.make_async_copy(xin_ref.at[pl.ds(i * norm_t, norm_t)], xbuf.at[i % 2], x_sems.at[i % 2])

        late_loads = [pltpu.make_async_copy(late_in[i], late_stage[i], late_sems.at[i]) for i in range(n_late)]
        for ld in late_loads:
            ld.start()
        x_load(0).start()
        for i in range(n_chunks):
            if i + 1 < n_chunks:
                x_load(i + 1).start()
            x_load(i).wait()
            h1_ref[i * norm_t:(i + 1) * norm_t, :] = _rms_rows(xbuf[i % 2], g_ref[...]).astype(BF16)
        stores = []
        for i in range(n_late):
            late_loads[i].wait()
            late_cast[i][...] = late_stage[i][...].astype(BF16)
            stores.append(pltpu.make_async_copy(late_cast[i], late_out[i].at[4 * x + 2 * y + c], store_sems.at[i]))
            stores[-1].start()

        for j, chip in enumerate(chips):
            copy(1 + j, (*chip, c), me).wait_recv()
            sent.append(copy(4 + j, (*chip, c), sibling))
            sent[-1].start()
        copy(0, sibling, me).wait_recv()
        for j, chip in enumerate(chips):
            copy(4 + j, (*chip, 1 - c), me).wait_recv()
        for cp in sent:
            cp.wait_send()
        for st in stores:
            st.wait()

        row = lax.broadcasted_iota(jnp.int32, (PAD_BLK, D), 0)
        for k, runs in enumerate(blocks):
            out = jnp.zeros((PAD_BLK, D), F32)
            for o_lo, o_hi, p, a in runs:
                start = min(a // 16 * 16, W_SLAB - W_WIN)
                win = land[p, start:start + W_WIN, :].astype(F32)
                moved = pltpu.roll(win, (o_lo - (a - start)) % W_WIN, 0)[:PAD_BLK]
                out = jnp.where((row >= o_lo) & (row < o_hi), moved, out)
            out_ref[k * PAD_BLK:(k + 1) * PAD_BLK, :] = out.astype(BF16)

    return pl.pallas_call(
        body, name="all_gather_w_in",
        out_shape=[jax.ShapeDtypeStruct((D_PAD, D), BF16), jax.ShapeDtypeStruct((S, D), BF16)]
                  + [jax.ShapeDtypeStruct((N_DEV,) + sh, BF16) for sh in late_shapes],
        in_specs=[ANY, ANY, pl.BlockSpec(memory_space=pltpu.VMEM)] + [ANY] * n_late,
        out_specs=[pl.BlockSpec(memory_space=pltpu.VMEM)] * 2 + [ANY] * n_late,
        scratch_shapes=[pltpu.VMEM((W_SLAB, D), F32), pltpu.VMEM((N_DEV, W_SLAB, D), BF16),
                        pltpu.VMEM((2, norm_t, D), F32)]
                       + [pltpu.VMEM(sh, F32) for sh in late_shapes] + [pltpu.VMEM(sh, BF16) for sh in late_shapes]
                       + [pltpu.SemaphoreType.DMA((AG_COPIES,)), pltpu.SemaphoreType.DMA((AG_COPIES,)),
                          pltpu.SemaphoreType.DMA(()), pltpu.SemaphoreType.DMA((2,)),
                          pltpu.SemaphoreType.DMA((n_late,)), pltpu.SemaphoreType.DMA((n_late,))],
        compiler_params=_cparams(None),
    )(shard, x_in, g_attn, *late_shards)


PEER_COPIES = N_DEV - 1
SIBLING_COPIES = 4
CHIP_COPIES = 3


def _peer_copies(_, land_refs, send_sems, recv_sems):
    x, y, c = _place()
    mine = 4 * x + 2 * y + c
    copies = []
    for i, ref in enumerate(land_refs):
        for k in range(1, N_DEV):
            peer = (x ^ (k >> 2), y ^ ((k >> 1) & 1), c ^ (k & 1))
            n = i * PEER_COPIES + k - 1
            copies.append(pltpu.make_async_remote_copy(
                src_ref=ref.at[mine], dst_ref=ref.at[mine], send_sem=send_sems.at[n], recv_sem=recv_sems.at[n],
                device_id=peer, device_id_type=MESH))
    return copies


def _sibling_copies(g_refs, r_refs, send_sems, recv_sems):
    x, y, c = _place()
    return [pltpu.make_async_remote_copy(
        src_ref=g_refs[i].at[2 * k + (1 - c)], dst_ref=r_refs[i].at[k],
        send_sem=send_sems.at[SIBLING_COPIES * i + k], recv_sem=recv_sems.at[SIBLING_COPIES * i + k],
        device_id=(x, y, 1 - c), device_id_type=MESH) for i in range(len(g_refs)) for k in range(SIBLING_COPIES)]


def _chip_copies(p_refs, r_refs, send_sems, recv_sems):
    x, y, c = _place()
    chips = [(1 - x, y), (x, 1 - y), (1 - x, 1 - y)]
    return [pltpu.make_async_remote_copy(
        src_ref=p_refs[i].at[2 * cx + cy], dst_ref=r_refs[i].at[j],
        send_sem=send_sems.at[CHIP_COPIES * i + j], recv_sem=recv_sems.at[CHIP_COPIES * i + j],
        device_id=(cx, cy, c), device_id_type=MESH) for i in range(len(p_refs)) for j, (cx, cy) in enumerate(chips)]


def _in_hbm(a):
    return pltpu.with_memory_space_constraint(a, pltpu.HBM)


def _exchange_start(name, copies_fn, n_copies, srcs, lands, after=()):
    n_s, n_l, n_a = len(srcs), len(lands), len(after)

    def body(*refs):
        outs = refs[n_s + n_l + n_a:]
        for cp in copies_fn(refs[:n_s], refs[n_s:n_s + n_l], outs[0], outs[1]):
            cp.start()
        outs[-1][...] = jnp.zeros_like(outs[-1])

    res = pl.pallas_call(
        body, name=name,
        out_shape=[pltpu.SemaphoreType.DMA((n_copies,)), pltpu.SemaphoreType.DMA((n_copies,))]
                  + [pltpu.HBM(t.shape, t.dtype) for t in (*srcs, *lands)] + [jax.ShapeDtypeStruct((8, LANES), F32)],
        in_specs=[HBM] * (n_s + n_l) + [ANY] * n_a,
        out_specs=[SEM, SEM] + [HBM] * (n_s + n_l) + [pl.BlockSpec(memory_space=pltpu.VMEM)],
        input_output_aliases={i: 2 + i for i in range(n_s + n_l)},
        compiler_params=pltpu.CompilerParams(has_side_effects=pltpu.SideEffectType.DATAFLOW_SIDE_EFFECTING),
    )(*[_in_hbm(t) for t in (*srcs, *lands)], *after)
    return res[0], res[1], list(res[2:2 + n_s]), list(res[2 + n_s:2 + n_s + n_l]), res[-1]


def _exchange_wait(name, copies_fn, started, after):
    send_sems, recv_sems, srcs, lands, _ = started
    n_s, n_l = len(srcs), len(lands)

    def body(*refs):
        for cp in copies_fn(refs[:n_s], refs[n_s:n_s + n_l], refs[n_s + n_l], refs[n_s + n_l + 1]):
            cp.wait_send()
            cp.wait_recv()

    res = pl.pallas_call(
        body, name=name,
        out_shape=[pltpu.HBM(t.shape, t.dtype) for t in (*srcs, *lands)],
        in_specs=[HBM] * (n_s + n_l) + [SEM, SEM, ANY],
        out_specs=[HBM] * (n_s + n_l),
        input_output_aliases={i: i for i in range(n_s + n_l)},
        compiler_params=pltpu.CompilerParams(has_side_effects=pltpu.SideEffectType.DATAFLOW_SIDE_EFFECTING),
    )(*srcs, *lands, send_sems, recv_sems, after)
    return list(res[:n_s]), list(res[n_s:])


def _shard_block(shape):
    rows, cols = shape
    if rows * cols <= 256 * 1024:
        return rows, cols
    if rows % 256:
        return rows, 256
    return 256, cols


def _chip_partial(ids, g_stack, recv1, name):
    shape = g_stack.shape[1:]
    br, bc = _shard_block(shape)

    def body(ids_ref, g_ref, r_ref, pb_ref, own_ref):
        s = g_ref[...] + r_ref[...]
        pb_ref[...] = s.astype(BF16)

        @pl.when(pl.program_id(2) == ids_ref[1])
        def _():
            own_ref[...] = s

    grid_spec = pltpu.PrefetchScalarGridSpec(
        num_scalar_prefetch=1, grid=(shape[0] // br, shape[1] // bc, 4),
        in_specs=[pl.BlockSpec((None, br, bc), lambda r, q, k, ids: (2 * k + ids[0], r, q)),
                  pl.BlockSpec((None, br, bc), lambda r, q, k, ids: (k, r, q))],
        out_specs=[pl.BlockSpec((None, br, bc), lambda r, q, k, ids: (k, r, q)),
                   pl.BlockSpec((br, bc), lambda r, q, k, ids: (r, q))])
    return pl.pallas_call(
        body, name=name, grid_spec=grid_spec,
        out_shape=[jax.ShapeDtypeStruct((4,) + shape, BF16), jax.ShapeDtypeStruct(shape, F32)],
        compiler_params=_cparams(("parallel", "parallel", "arbitrary")),
    )(ids, g_stack, recv1)


def _adamw(w, g, m, v):
    m = ADAM_B1 * m + (1.0 - ADAM_B1) * g
    v = ADAM_B2 * v + (1.0 - ADAM_B2) * (g * g)
    m_hat = m / (1.0 - ADAM_B1 ** ADAM_STEP)
    v_hat = v / (1.0 - ADAM_B2 ** ADAM_STEP)
    delta = -ADAM_LR * (m_hat / (jnp.sqrt(v_hat) + ADAM_EPS) + ADAM_WD * w)
    return delta, m, v


def _reduce_adamw(own, recv2, w, m, v, name, deps=()):
    shape = own.shape
    br, bc = _shard_block(shape)

    def body(own_ref, r_ref, w_ref, m_ref, v_ref, *refs):
        g_ref, d_ref, nm_ref, nv_ref = refs[len(deps):]
        g = own_ref[...]
        for j in range(3):
            g = g + r_ref[j].astype(F32)
        delta, nm, nv = _adamw(w_ref[...], g, m_ref[...], v_ref[...])
        g_ref[...] = g
        d_ref[...] = delta
        nm_ref[...] = nm
        nv_ref[...] = nv

    blk = pl.BlockSpec((br, bc), lambda r, q: (r, q))
    return pl.pallas_call(
        body, name=name, grid=(shape[0] // br, shape[1] // bc),
        in_specs=[blk, pl.BlockSpec((3, br, bc), lambda r, q: (0, r, q)), blk, blk, blk] + [ANY] * len(deps),
        out_specs=[blk] * 4, out_shape=[jax.ShapeDtypeStruct(shape, F32)] * 4,
        compiler_params=_cparams(("parallel", "parallel")),
    )(own, recv2, w, m, v, *deps)


def _small_allreduce_adamw(gvec, w, m, v):
    def body(g_ref, w_ref, m_ref, v_ref, go_ref, d_ref, nm_ref, nv_ref, buf, send_sems, recv_sems):
        x, y, c = _place()
        my_slot = 4 * x + 2 * y + c
        buf[my_slot] = g_ref[...]
        copies = []
        for k in range(1, N_DEV):
            px, py, pc = x ^ (k >> 2), y ^ ((k >> 1) & 1), c ^ (k & 1)
            copies.append(pltpu.make_async_remote_copy(
                src_ref=g_ref, dst_ref=buf.at[my_slot], send_sem=send_sems.at[k - 1], recv_sem=recv_sems.at[k - 1],
                device_id=(px, py, pc), device_id_type=MESH))
        for cp in copies:
            cp.start()
        for k in range(1, N_DEV):
            px, py, pc = x ^ (k >> 2), y ^ ((k >> 1) & 1), c ^ (k & 1)
            pltpu.make_async_remote_copy(
                src_ref=g_ref, dst_ref=buf.at[4 * px + 2 * py + pc], send_sem=send_sems.at[k - 1],
                recv_sem=recv_sems.at[k - 1], device_id=(px, py, pc), device_id_type=MESH).wait_recv()
        for cp in copies:
            cp.wait_send()
        g = buf[0]
        for s in range(1, N_DEV):
            g = g + buf[s]
        delta, nm, nv = _adamw(w_ref[...], g, m_ref[...], v_ref[...])
        go_ref[...] = g
        d_ref[...] = delta
        nm_ref[...] = nm
        nv_ref[...] = nv

    vm = pl.BlockSpec(memory_space=pltpu.VMEM)
    return pl.pallas_call(
        body, name="small_allreduce_adamw",
        in_specs=[vm] * 4, out_specs=[vm] * 4,
        out_shape=[jax.ShapeDtypeStruct((SMALL_R, LANES), F32)] * 4,
        scratch_shapes=[pltpu.VMEM((N_DEV, SMALL_R, LANES), F32),
                        pltpu.SemaphoreType.DMA((N_DEV - 1,)), pltpu.SemaphoreType.DMA((N_DEV - 1,))],
    )(gvec, w, m, v)


def _cols_to_whole(stack):
    rows = stack.shape[1]
    return stack.transpose(1, 0, 2).reshape(rows, -1)


def _whole_to_cols(t):
    rows = t.shape[0]
    return t.reshape(rows, N_DEV, -1).transpose(1, 0, 2)


W_IN_SEGMENTS = ((0, 3 * FOX_W, SEG_A), (3 * FOX_W, 3 * FOX_W + FOX_H, SEG_F),
                 (3 * FOX_W + FOX_H, 3 * FOX_W + FOX_H + 3 * DIL_W, SEG_B), (3 * FOX_W + FOX_H + 3 * DIL_W, D_IN, SEG_G))


def _shard_pieces(p):
    rows = D_IN // N_DEV
    lo, hi = p * rows, (p + 1) * rows
    return [(max(lo, a) - lo, min(hi, b) - lo, pad + max(lo, a) - a)
            for a, b, pad in W_IN_SEGMENTS if max(lo, a) < min(hi, b)]


def _unpad_dw_in_t(dwp):
    return jnp.stack([jnp.concatenate([dwp[pad:pad + b - a] for a, b, pad in _shard_pieces(p)], axis=0)
                      for p in range(N_DEV)])


LOSS_ROW = 25


def _pack_small(g_attn, b, g_mlp, g_final, loss_row=None):
    tail = jnp.pad(b, ((0, 7), (0, LANES - b.shape[1])))
    if loss_row is not None:
        tail = tail + jnp.pad(loss_row, ((LOSS_ROW - 24, 31 - LOSS_ROW), (0, 0)))
    return jnp.concatenate([g_attn.reshape(8, LANES), g_mlp.reshape(8, LANES), g_final.reshape(8, LANES), tail], axis=0)


def _unpack_small(p):
    return (p[0:8].reshape(1, D), p[24:25, :FOX_H], p[8:16].reshape(1, D), p[16:24].reshape(D))


class _StepComm:
    def __init__(self, ids, w_sh, m_sh, v_sh, lands, after_w_in):
        self.ids = ids
        self.w_sh, self.m_sh, self.v_sh = w_sh, m_sh, v_sh
        self.updates = None
        self.gather_mixer = _exchange_start("gather_mixer_start", _peer_copies, PEER_COPIES * 3, [], lands[:3],
                                            after=(after_w_in,))
        self.gather_mlp = _exchange_start("gather_mlp_start", _peer_copies, PEER_COPIES * 2, [], lands[3:],
                                          after=(self.gather_mixer[-1],))
        self.sibling = {}
        self.chips = {}
        self.own = {}
        self.names = {}

    def _reduce_start(self, group, names, grads, after=()):
        lands = [lax.empty((SIBLING_COPIES,) + t.shape[1:], F32) for t in grads]
        self.sibling[group] = _exchange_start("grad_%s_sibling_start" % group, _sibling_copies,
                                              SIBLING_COPIES * len(grads), grads, lands, after=after)
        self.names[group] = names
        return self.sibling[group][-1]

    def _reduce_mid(self, group, after):
        grads, recv1 = _exchange_wait("grad_%s_sibling_wait" % group, _sibling_copies, self.sibling[group], after)
        parts = [_chip_partial(self.ids, g, r, "grad_partial_" + n) for g, r, n in zip(grads, recv1, self.names[group])]
        self.own[group] = [p[1] for p in parts]
        srcs = [p[0] for p in parts]
        lands = [lax.empty((CHIP_COPIES,) + t.shape[1:], BF16) for t in srcs]
        self.chips[group] = _exchange_start("grad_%s_chips_start" % group, _chip_copies, CHIP_COPIES * len(srcs),
                                            srcs, lands)
        return self.chips[group][-1]

    def reduced(self, group, after):
        _, recv2 = _exchange_wait("grad_%s_chips_wait" % group, _chip_copies, self.chips[group], after)
        return list(zip(self.own[group], recv2))

    def first_deps(self):
        return [self.gather_mlp[-1]]

    def mixer_weights(self, after):
        _, (g_a, g_b, g_out) = _exchange_wait("gather_mixer_wait", _peer_copies, self.gather_mixer, after)
        return _cols_to_whole(g_a), _cols_to_whole(g_b), g_out.reshape(D, D)

    def mlp_weights(self, after):
        _, (g_up, g_down) = _exchange_wait("gather_mlp_wait", _peer_copies, self.gather_mlp, after)
        return g_up, g_down.reshape(DFF, D)

    def mlp_grads(self, dw_up_sh, dw_down):
        return [self._reduce_start("mlp", W_NAMES[4:], [dw_up_sh, dw_down.reshape((N_DEV,) + SHARD_SHAPES[5])])]

    def mixer_grads(self, dw_a, dw_b, dw_out):
        token = self._reduce_mid("mlp", dw_b)
        grads = [_whole_to_cols(dw_a), _whole_to_cols(dw_b), dw_out.reshape((N_DEV,) + SHARD_SHAPES[3])]
        return [self._reduce_start("mixer", W_NAMES[1:4], grads, after=(token,))]

    def mid_backward(self, after):
        return [self._reduce_mid("mixer", after)]

    def w_in_grad(self, dw_in_t):
        grads = [_unpad_dw_in_t(dw_in_t)]
        token = self._reduce_start("w_in", W_NAMES[:1], grads)
        reduced = self.reduced("mixer", token) + self.reduced("mlp", token)
        self.updates, last = [None] * len(reduced), token
        for i in (3, 4, 0, 1, 2):
            if i == 0:
                last = self._reduce_mid("w_in", last)
            self.updates[i] = _reduce_adamw(*reduced[i], self.w_sh[1 + i], self.m_sh[1 + i], self.v_sh[1 + i],
                                            "adamw_" + W_NAMES[1 + i], deps=[last])
            last = self.updates[i][0]
        return [last]

    def w_in_update(self, after):
        (reduced,) = self.reduced("w_in", after)
        return _reduce_adamw(*reduced, self.w_sh[0], self.m_sh[0], self.v_sh[0], "adamw_" + W_NAMES[0])


def kernel(x, norm_attn_g, w_in, b_forget, w_branch_a, w_branch_b, w_out, norm_mlp_g, w_up, w_down, norm_final_g, loss_target, m_norm_attn_g, m_w_in, m_b_forget, m_w_branch_a, m_w_branch_b, m_w_out, m_norm_mlp_g, m_w_up, m_w_down, m_norm_final_g, v_norm_attn_g, v_w_in, v_b_forget, v_w_branch_a, v_w_branch_b, v_w_out, v_norm_mlp_g, v_w_up, v_w_down, v_norm_final_g):
    cx, cy, cc = _place()
    ids = jnp.stack([cc, 2 * cx + cy]).astype(jnp.int32)

    w_sh = [w_in[0].T] + [t[0] for t in (w_branch_a, w_branch_b, w_out, w_up, w_down)]
    m_sh = [m_w_in[0].T] + [t[0] for t in (m_w_branch_a, m_w_branch_b, m_w_out, m_w_up, m_w_down)]
    v_sh = [v_w_in[0].T] + [t[0] for t in (v_w_branch_a, v_w_branch_b, v_w_out, v_w_up, v_w_down)]

    w_in_t, h1, *lands = _gather_w_in(jnp.pad(w_sh[0], ((0, W_SLAB - w_sh[0].shape[0]), (0, 0))), x[0], norm_attn_g,
                                      w_sh[1:])
    comm = _StepComm(ids, w_sh, m_sh, v_sh, lands, w_in_t)
    b_pad = jnp.pad(b_forget, ((0, 0), (0, LANES - FOX_H)))

    loss_row, dx, dsmall = _local_step(
        x[0], loss_target[0], h1, norm_attn_g, norm_mlp_g, norm_final_g.reshape(1, D), b_pad, w_in_t, comm)

    dg_attn, db, dg_mlp, dg_final = dsmall
    small = _small_allreduce_adamw(
        _pack_small(dg_attn, db, dg_mlp, dg_final, loss_row),
        _pack_small(norm_attn_g, b_forget, norm_mlp_g, norm_final_g.reshape(1, D)),
        _pack_small(m_norm_attn_g, m_b_forget, m_norm_mlp_g, m_norm_final_g.reshape(1, D)),
        _pack_small(v_norm_attn_g, v_b_forget, v_norm_mlp_g, v_norm_final_g.reshape(1, D)))
    big = [comm.w_in_update(small[0])] + comm.updates

    outs = [small[0][LOSS_ROW, 0], dx[None]]
    for q in range(4):
        s_attn, s_b, s_mlp, s_final = _unpack_small(small[q])
        b_in, b_a, b_b, b_out, b_up, b_down = [t[q][None] for t in big]
        b_in = jnp.swapaxes(b_in, 1, 2)
        outs += [s_attn, b_in, s_b, b_a, b_b, b_out, s_mlp, b_up, b_down, s_final]
    return tuple(outs)
```

```python
import functools

import jax
import jax.numpy as jnp
from jax import lax
from jax.experimental import pallas as pl
from jax.experimental.pallas import tpu as pltpu

F32 = jnp.float32
BF16 = jnp.bfloat16
MESH = pl.DeviceIdType.MESH

S = 2048
D = 1024
HD = 64
FOX_H = 8
FOX_W = FOX_H * HD
DIL_HG = 4
DIL_G = 3
DIL_W = DIL_G * DIL_HG * HD
DIL_OUT = DIL_HG * HD
DIL_BLK = 128
DIL_R = (1, 4, 16)
DFF = 4 * D
D_IN = 3 * FOX_W + FOX_H + 3 * DIL_W + 2 * D
EPS = 1e-6
NEG_INF = -1e30
SCALE = HD ** -0.5
ROPE_THETA = 500000.0
ROPE_DIM = HD // 4
N_DEV = 8

ADAM_LR = 0.001
ADAM_B1 = 0.9
ADAM_B2 = 0.999
ADAM_EPS = 1e-08
ADAM_WD = 0.01
ADAM_STEP = 10

LANES = 128
CB = 256
SEG_A = 0
SEG_B = 3 * FOX_W
SEG_Z = SEG_B + 3 * DIL_W
SEG_G = SEG_Z + CB
SEG_F = SEG_G + 2 * D
D_PAD = SEG_F + CB
DW_IN_TM = D_PAD // 10
SMALL_R = 32

VMEM_MB = 56


def _cparams(dims=None, vmem_mb=VMEM_MB, **kw):
    return pltpu.CompilerParams(dimension_semantics=dims, vmem_limit_bytes=vmem_mb << 20, **kw)


ANY = pl.BlockSpec(memory_space=pl.ANY)

_NN = (((1,), (0,)), ((), ()))
_NT = (((1,), (1,)), ((), ()))
_TN = (((0,), (0,)), ((), ()))


def _dot(a, b, dims):
    return lax.dot_general(a.astype(BF16), b.astype(BF16), dims, preferred_element_type=F32)


def _mm(a, b, *, mode, tm, tn, out_dtypes, name, n=None, k=None, a_off=0, b_off=0,
        b_sharded=False, out_sharded=False, epilogue=None, extras=(), deps=()):
    n_sh = b.shape[-1] if b_sharded else None
    if mode == "nn":
        m = a.shape[0]
        k = k or a.shape[1]
        a_spec = pl.BlockSpec((tm, k), lambda i, j: (i, a_off))
        if b_sharded:
            assert tn == n_sh
            n = N_DEV * n_sh
            b_spec = pl.BlockSpec((None, k, tn), lambda i, j: (j, 0, 0))
        else:
            n = n or b.shape[1]
            b_spec = pl.BlockSpec((k, tn), lambda i, j: (0, j + b_off))
        dims = _NN
    elif mode == "nt":
        m = a.shape[0]
        k = k or a.shape[1]
        a_spec = pl.BlockSpec((tm, k), lambda i, j: (i, a_off))
        if b_sharded:
            n = b.shape[1]
            b_spec = pl.BlockSpec((N_DEV, tn, n_sh), lambda i, j: (0, j, 0))
        else:
            n = n or b.shape[0]
            b_spec = pl.BlockSpec((tn, k), lambda i, j: (j + b_off, 0))
        dims = _NT
    else:
        k, m = a.shape
        n = n or b.shape[1]
        a_spec = pl.BlockSpec((k, tm), lambda i, j: (0, i))
        b_spec = pl.BlockSpec((k, tn), lambda i, j: (0, j + b_off))
        dims = _TN
    assert m % tm == 0 and n % tn == 0, (name, m, n, tm, tn)
    n_extra = len(extras)
    tile = pl.BlockSpec((tm, tn), lambda i, j: (i, j))
    if out_sharded:
        assert mode == "tn" and n // tn == N_DEV
        out_spec = pl.BlockSpec((None, tm, tn), lambda i, j: (j, i, 0))
        out_shape = (N_DEV, m, tn)
    else:
        out_spec, out_shape = tile, (m, n)

    def body(a_ref, b_ref, *refs):
        if mode == "nt" and b_sharded:
            acc = _dot(a_ref[:, 0:n_sh], b_ref[0], dims)
            for p in range(1, N_DEV):
                acc = acc + _dot(a_ref[:, p * n_sh:(p + 1) * n_sh], b_ref[p], dims)
        else:
            acc = _dot(a_ref[...], b_ref[...], dims)
        ex = [r[...] for r in refs[:n_extra]]
        outs = epilogue(acc, *ex) if epilogue is not None else (acc,)
        for o_ref, o in zip(refs[n_extra + len(deps):], outs):
            o_ref[...] = o.astype(o_ref.dtype)

    res = pl.pallas_call(
        body, name=name, grid=(m // tm, n // tn),
        in_specs=[a_spec, b_spec] + [tile] * n_extra + [ANY] * len(deps),
        out_specs=[out_spec] * len(out_dtypes),
        out_shape=[jax.ShapeDtypeStruct(out_shape, dt) for dt in out_dtypes],
        compiler_params=_cparams(("parallel", "parallel")),
    )(a, b, *extras, *deps)
    return res if len(out_dtypes) > 1 else res[0]


def _mm_rows(a, b, *, tm, name, epilogue, rows=(), vecs=(), row_out=(), vec_out=(), b_sharded=False, deps=()):
    m, k = a.shape
    n_rows, n_vecs, n_deps = len(rows), len(vecs), len(deps)
    n_sh = b.shape[-1] if b_sharded else None

    def body(a_ref, b_ref, *refs):
        row_refs, vec_refs = refs[:n_rows], refs[n_rows:n_rows + n_vecs]
        outs = refs[n_rows + n_vecs + n_deps:]
        if b_sharded:
            acc = _dot(a_ref[:, 0:n_sh], b_ref[0], _NT)
            for p in range(1, N_DEV):
                acc = acc + _dot(a_ref[:, p * n_sh:(p + 1) * n_sh], b_ref[p], _NT)
        else:
            acc = _dot(a_ref[...], b_ref[...], _NN)
        row_vals, vec_incs = epilogue(acc, [r[...] for r in row_refs], [v[...] for v in vec_refs])
        for o_ref, val in zip(outs[:len(row_out)], row_vals):
            o_ref[...] = val.astype(o_ref.dtype)

        @pl.when(pl.program_id(0) == 0)
        def _():
            for o_ref in outs[len(row_out):]:
                o_ref[...] = jnp.zeros_like(o_ref)

        for o_ref, inc in zip(outs[len(row_out):], vec_incs):
            o_ref[...] += inc

    tile = pl.BlockSpec((tm, D), lambda i: (i, 0))
    b_spec = pl.BlockSpec(b.shape, lambda i: (0,) * b.ndim)
    return pl.pallas_call(
        body, name=name, grid=(m // tm,),
        in_specs=[pl.BlockSpec((tm, k), lambda i: (i, 0)), b_spec] + [tile] * n_rows
                 + [pl.BlockSpec(v.shape, lambda i: (0, 0)) for v in vecs] + [ANY] * n_deps,
        out_specs=[tile] * len(row_out) + [pl.BlockSpec((1, w), lambda i: (0, 0)) for w in vec_out],
        out_shape=[jax.ShapeDtypeStruct((m, D), dt) for dt in row_out]
                  + [jax.ShapeDtypeStruct((1, w), F32) for w in vec_out],
        compiler_params=_cparams(("arbitrary",)),
    )(a, b, *rows, *vecs, *deps)


ROW_T = 256
ROWS_TM = 512


def _rms_rows(x, g):
    r = lax.rsqrt(jnp.mean(x * x, axis=-1, keepdims=True) + EPS)
    return (x * r) * g


def _rms_bwd_rows(dh, x, dres, g):
    r = lax.rsqrt(jnp.mean(x * x, axis=-1, keepdims=True) + EPS)
    xn = x * r
    dhn = dh * g
    dx = dres + r * (dhn - xn * jnp.mean(dhn * xn, axis=-1, keepdims=True))
    return dx, jnp.sum(dh * xn, axis=0, keepdims=True)


def _final_loss_rows(x, g, tgt):
    r = lax.rsqrt(jnp.mean(x * x, axis=-1, keepdims=True) + EPS)
    xn = x * r
    err = xn * g - tgt
    row_loss = jnp.mean(err * err, axis=-1, keepdims=True)
    loss = 0.5 * jnp.sum(row_loss, axis=0, keepdims=True) * jnp.ones((1, LANES), F32)
    dy = err * (1.0 / D)
    dyn = dy * g
    dx = r * (dyn - xn * jnp.mean(dyn * xn, axis=-1, keepdims=True))
    return dx, jnp.sum(dy * xn, axis=0, keepdims=True), loss


def _sigmoid(z):
    return 1.0 / (1.0 + jnp.exp(-z))


GATE_TR = 512


def _mixer_tail(oa, ob, w_a, w_b, w_out, proj_g, x, g_mlp):
    def body(oa_ref, ob_ref, wa_ref, wb_ref, wo_ref, g_ref, x_ref, gm_ref, ya_ref, yb_ref, mixed_ref, x2_ref, h2_ref):
        ya = _dot(oa_ref[...], wa_ref[...], _NN)
        yb = _dot(ob_ref[...], wb_ref[...], _NN)
        ya_ref[...] = ya
        yb_ref[...] = yb
        mixed = (_sigmoid(g_ref[:, :D]) * ya + _sigmoid(g_ref[:, D:]) * yb).astype(BF16)
        mixed_ref[...] = mixed
        x2 = x_ref[...] + _dot(mixed, wo_ref[...], _NN)
        x2_ref[...] = x2
        h2_ref[...] = _rms_rows(x2, gm_ref[...]).astype(BF16)

    def rows(width):
        return pl.BlockSpec((GATE_TR, width), lambda i: (i, 0))

    def whole(t):
        return pl.BlockSpec(t.shape, lambda i: (0, 0))

    return pl.pallas_call(
        body, name="mixer_tail", grid=(S // GATE_TR,),
        in_specs=[rows(FOX_W), rows(DIL_OUT), whole(w_a), whole(w_b), whole(w_out), rows(2 * D), rows(D), whole(g_mlp)],
        out_specs=[rows(D)] * 5,
        out_shape=[jax.ShapeDtypeStruct((S, D), dt) for dt in (F32, F32, BF16, F32, BF16)],
        compiler_params=_cparams(("parallel",)),
    )(oa, ob, w_a, w_b, w_out, proj_g, x, g_mlp)


def _out_proj_gate_bwd(dx2, w_out, proj_g, ya, yb):
    def body(dx_ref, w_ref, g_ref, ya_ref, yb_ref, dy_ref, dg_ref):
        dm = _dot(dx_ref[...], w_ref[...], _NT)
        for half, y_ref in enumerate((ya_ref, yb_ref)):
            cols = slice(half * D, (half + 1) * D)
            s = _sigmoid(g_ref[:, cols])
            dy_ref[:, cols] = (dm * s).astype(BF16)
            dg_ref[:, cols] = (dm * y_ref[...] * (s * (1.0 - s))).astype(BF16)

    row = pl.BlockSpec((GATE_TR, D), lambda i: (i, 0))
    wide = pl.BlockSpec((GATE_TR, 2 * D), lambda i: (i, 0))
    return pl.pallas_call(
        body, name="out_proj_gate_bwd", grid=(S // GATE_TR,),
        in_specs=[row, pl.BlockSpec((D, D), lambda i: (0, 0)), wide, row, row],
        out_specs=[wide, pl.BlockSpec((GATE_TR, 2 * D), lambda i: (i, SEG_G // (2 * D)))],
        out_shape=[jax.ShapeDtypeStruct((S, 2 * D), BF16), jax.ShapeDtypeStruct((S, D_PAD), BF16)],
        compiler_params=_cparams(("parallel",)),
    )(dx2, w_out, proj_g, ya, yb)


FOX_TQ = 512


def _scan_rows(x, reverse):
    n = x.shape[0]
    row = lax.broadcasted_iota(jnp.int32, x.shape, 0)
    k = 1
    while k < n:
        if reverse:
            x = x + jnp.where(row < n - k, pltpu.roll(x, n - k, 0), 0.0)
        else:
            x = x + jnp.where(row >= k, pltpu.roll(x, k, 0), 0.0)
        k *= 2
    return x


def _fox_dscan(dft, dfs, proj_f, b_pad, dproj):
    def body(dft_ref, dfs_ref, f_ref, b_ref, _, dfa_ref, db_ref):
        dfs_pad = jnp.concatenate([dfs_ref[...], jnp.zeros((LANES - FOX_H, S), F32)], axis=0)
        df = dfs_pad.T + dft_ref[0]
        for hp in range(1, dft_ref.shape[0]):
            df = df + pltpu.roll(dft_ref[hp], 2 * hp, 1)
        dlf = _scan_rows(df, reverse=True)
        z = f_ref[...] + b_ref[...]
        lane = lax.broadcasted_iota(jnp.int32, (S, LANES), 1)
        dfa = jnp.where(lane < FOX_H, dlf / (1.0 + jnp.exp(z)), 0.0)
        dfa_ref[:, :LANES] = dfa.astype(BF16)
        dfa_ref[:, LANES:] = jnp.zeros((S, CB - LANES), BF16)
        db_ref[...] = jnp.sum(dfa, axis=0, keepdims=True)

    return pl.pallas_call(
        body, name="fox_dscan", grid=(1,),
        in_specs=[pl.BlockSpec(dft.shape, lambda i: (0, 0, 0)), pl.BlockSpec((FOX_H, S), lambda i: (0, 0)),
                  pl.BlockSpec((S, LANES), lambda i: (0, 0)), pl.BlockSpec((1, LANES), lambda i: (0, 0)), ANY],
        out_specs=[pl.BlockSpec((S, CB), lambda i: (0, SEG_F // CB)), pl.BlockSpec((1, LANES), lambda i: (0, 0))],
        out_shape=[jax.ShapeDtypeStruct((S, D_PAD), BF16), jax.ShapeDtypeStruct((1, LANES), F32)],
        input_output_aliases={4: 0},
        compiler_params=_cparams(("arbitrary",)),
    )(dft, dfs, proj_f, b_pad, dproj)


FOX_NQ = S // FOX_TQ
FOX_HP = FOX_W // LANES
HEADS_PER_LB = LANES // HD
FOX_AUG = 2 * LANES


def _fox_prepare(proj_a, proj_f, b_pad):
    tr = 512

    def body(q_ref, k_ref, f_ref, b_ref, qa_ref, ka_ref, hi_s, mid_s, lo_s):
        i = pl.program_id(0)

        @pl.when(i == 0)
        def _():
            z = f_ref[...] + b_ref[...]
            lf = jnp.minimum(z, 0.0) - jnp.log1p(jnp.exp(-jnp.abs(z)))
            f = _scan_rows(lf, reverse=False)
            hi = f.astype(BF16).astype(F32)
            r1 = f - hi
            mid = r1.astype(BF16).astype(F32)
            hi_s[...] = hi
            mid_s[...] = mid
            lo_s[...] = (r1 - mid).astype(BF16).astype(F32)

        rows = pl.ds(pl.multiple_of(i * tr, tr), tr)
        hi, mid, lo = hi_s[rows, :], mid_s[rows, :], lo_s[rows, :]
        lane = lax.broadcasted_iota(jnp.int32, (tr, HD), 1)
        q_ones = jnp.where((lane >= 3) & (lane < 6), 1.0, 0.0)
        k_ones = jnp.where(lane < 3, 1.0, 0.0)
        for h in range(FOX_H):
            a1, a2, a3 = hi[:, h:h + 1], mid[:, h:h + 1], lo[:, h:h + 1]
            q_extra = jnp.where(lane == 0, a1, jnp.where(lane == 1, a2, jnp.where(lane == 2, a3, q_ones)))
            k_extra = jnp.where(lane == 3, -a1, jnp.where(lane == 4, -a2, jnp.where(lane == 5, -a3, k_ones)))
            base = h * LANES
            qa_ref[:, base:base + HD] = q_ref[:, h * HD:(h + 1) * HD] * jnp.asarray(SCALE, BF16)
            qa_ref[:, base + HD:base + LANES] = q_extra.astype(BF16)
            ka_ref[:, base:base + HD] = k_ref[:, h * HD:(h + 1) * HD]
            ka_ref[:, base + HD:base + LANES] = k_extra.astype(BF16)

    out_blk = pl.BlockSpec((tr, FOX_H * LANES), lambda i: (i, 0))
    return pl.pallas_call(
        body, name="fox_prepare", grid=(S // tr,),
        in_specs=[pl.BlockSpec((tr, FOX_W), lambda i: (i, 0)), pl.BlockSpec((tr, FOX_W), lambda i: (i, 1)),
                  pl.BlockSpec((S, LANES), lambda i: (0, 0)), pl.BlockSpec((1, LANES), lambda i: (0, 0))],
        out_specs=[out_blk, out_blk],
        out_shape=[jax.ShapeDtypeStruct((S, FOX_H * LANES), BF16)] * 2,
        scratch_shapes=[pltpu.VMEM((S, LANES), F32)] * 3,
        compiler_params=_cparams(("arbitrary",)),
    )(proj_a, proj_a, proj_f, b_pad)


def _fox_scores(qa, ka, w):
    s = _dot(qa, ka, _NT)
    row = lax.broadcasted_iota(jnp.int32, (FOX_TQ, FOX_TQ), 0)
    col = lax.broadcasted_iota(jnp.int32, (FOX_TQ, FOX_TQ), 1)
    diag = jnp.where(col <= row, s[:, w * FOX_TQ:], NEG_INF)
    return diag if w == 0 else jnp.concatenate([s[:, :w * FOX_TQ], diag], axis=1)


def _fox_fwd(q_aug, k_aug, proj_a):
    def body(qa_ref, ka_ref, v_ref, o_ref, lse_ref):
        qi = pl.program_id(1)
        lane = lax.broadcasted_iota(jnp.int32, (FOX_TQ, LANES), 1)
        for w in range(FOX_NQ):
            @pl.when(qi == w)
            def _(w=w):
                width = (w + 1) * FOX_TQ
                lse = jnp.zeros((FOX_TQ, LANES), F32)
                for hh in range(HEADS_PER_LB):
                    aug = slice(hh * LANES, (hh + 1) * LANES)
                    sl = slice(hh * HD, (hh + 1) * HD)
                    s = _fox_scores(qa_ref[:, aug], ka_ref[:width, aug], w)
                    m = jnp.max(s, axis=-1, keepdims=True)
                    p = jnp.exp(s - m)
                    l = jnp.sum(p, axis=-1, keepdims=True)
                    o_ref[:, sl] = _dot(p / l, v_ref[:width, sl], _NN)
                    lse = jnp.where(lane == hh, m + jnp.log(l), lse)
                lse_ref[...] = lse

    return pl.pallas_call(
        body, name="fox_fwd", grid=(FOX_HP, FOX_NQ),
        in_specs=[pl.BlockSpec((FOX_TQ, FOX_AUG), lambda h, i: (i, h)),
                  pl.BlockSpec((S, FOX_AUG), lambda h, i: (0, h)),
                  pl.BlockSpec((S, LANES), lambda h, i: (0, 2 * FOX_HP + h))],
        out_specs=[pl.BlockSpec((FOX_TQ, LANES), lambda h, i: (i, h)),
                   pl.BlockSpec((None, FOX_TQ, LANES), lambda h, i: (h, i, 0))],
        out_shape=[jax.ShapeDtypeStruct((S, FOX_W), F32), jax.ShapeDtypeStruct((FOX_HP, S, LANES), F32)],
        compiler_params=_cparams(("parallel", "parallel")),
    )(q_aug, k_aug, proj_a)


def _fox_bwd(q_aug, k_aug, proj_a, lse, do, dproj, deps=()):
    n_q = FOX_NQ

    def body(qa_ref, ka_ref, v_ref, lse_ref, do_ref, *refs):
        dproj_ref, dft_ref, dfs_ref, dk_acc, dv_acc, dq_buf, kv_buf, dq_sems, kv_sems = refs[1 + len(deps):]
        hp = pl.program_id(0)
        t = pl.program_id(1)
        slot = t % 2
        col = pl.multiple_of(hp * LANES, LANES)

        def dq_copy(step, buf_slot):
            rows = pl.ds(pl.multiple_of(step * FOX_TQ, FOX_TQ), FOX_TQ)
            return pltpu.make_async_copy(dq_buf.at[buf_slot], dproj_ref.at[rows, pl.ds(col, LANES)], dq_sems.at[buf_slot])

        @pl.when(t == 0)
        def _():
            dk_acc[...] = jnp.zeros_like(dk_acc)
            dv_acc[...] = jnp.zeros_like(dv_acc)

        @pl.when((t == 0) & (hp == 0))
        def _():
            dfs_ref[...] = jnp.zeros_like(dfs_ref)

        @pl.when(t >= 2)
        def _():
            dq_copy(t - 2, slot).wait()

        lane = lax.broadcasted_iota(jnp.int32, (FOX_TQ, LANES), 1)
        for w in range(n_q):
            @pl.when(t == w)
            def _(w=w):
                width = (w + 1) * FOX_TQ
                lse_all = lse_ref[...]
                sub = lax.broadcasted_iota(jnp.int32, (FOX_H, width), 0)
                dft = jnp.zeros((FOX_TQ, LANES), F32)
                for hh in range(HEADS_PER_LB):
                    aug = slice(hh * LANES, (hh + 1) * LANES)
                    sl = slice(hh * HD, (hh + 1) * HD)
                    head = hp * HEADS_PER_LB + hh
                    qa = qa_ref[:, aug]
                    ka = ka_ref[:width, aug]
                    do_h = do_ref[:, sl]
                    s = _fox_scores(qa, ka, w)
                    p = jnp.exp(s - lse_all[:, hh:hh + 1])
                    dp = _dot(do_h, v_ref[:width, sl], _NT)
                    ds = p * (dp - jnp.sum(dp * p, axis=-1, keepdims=True))
                    dft = jnp.where(lane == hh, jnp.sum(ds, axis=-1, keepdims=True), dft)
                    dfs_ref[:, :width] -= jnp.where(sub == head, jnp.sum(ds, axis=0, keepdims=True), 0.0)
                    dq_buf[slot, :, sl] = (_dot(ds, ka[:, :HD], _NN) * SCALE).astype(BF16)
                    dk_acc[:width, sl] += _dot(ds, qa[:, :HD], _TN)
                    dv_acc[:width, sl] += _dot(p, do_h, _TN)
                dft_ref[...] = dft

        dq_copy(t, slot).start()

        @pl.when(t == n_q - 1)
        def _():
            dq_copy(t - 1, 1 - slot).wait()
            dq_copy(t, slot).wait()
            kv_buf[0] = dk_acc[...].astype(BF16)
            kv_buf[1] = dv_acc[...].astype(BF16)
            copies = [pltpu.make_async_copy(
                kv_buf.at[j], dproj_ref.at[:, pl.ds(pl.multiple_of((j + 1) * FOX_W + hp * LANES, LANES), LANES)],
                kv_sems.at[j]) for j in range(2)]
            for cp in copies:
                cp.start()
            for cp in copies:
                cp.wait()

    qblk = pl.BlockSpec((FOX_TQ, FOX_AUG), lambda h, t: (t, h))
    lane_blk = pl.BlockSpec((None, FOX_TQ, LANES), lambda h, t: (h, t, 0))
    return pl.pallas_call(
        body, name="fox_bwd", grid=(FOX_HP, n_q),
        in_specs=[qblk, pl.BlockSpec((S, FOX_AUG), lambda h, t: (0, h)),
                  pl.BlockSpec((S, LANES), lambda h, t: (0, 2 * FOX_HP + h)),
                  lane_blk, pl.BlockSpec((FOX_TQ, LANES), lambda h, t: (t, h)), ANY] + [ANY] * len(deps),
        out_specs=[ANY, lane_blk, pl.BlockSpec((FOX_H, S), lambda h, t: (0, 0))],
        out_shape=[jax.ShapeDtypeStruct((S, D_PAD), BF16),
                   jax.ShapeDtypeStruct((FOX_HP, S, LANES), F32), jax.ShapeDtypeStruct((FOX_H, S), F32)],
        scratch_shapes=[pltpu.VMEM((S, LANES), F32), pltpu.VMEM((S, LANES), F32),
                        pltpu.VMEM((2, FOX_TQ, LANES), BF16), pltpu.VMEM((2, S, LANES), BF16),
                        pltpu.SemaphoreType.DMA((2,)), pltpu.SemaphoreType.DMA((2,))],
        input_output_aliases={5: 0},
        compiler_params=_cparams(("arbitrary", "arbitrary")),
    )(q_aug, k_aug, proj_a, lse, do, dproj, *deps)


def _rope_tables():
    half = ROPE_DIM // 2
    shape = (DIL_G, S, half)
    g = lax.broadcasted_iota(jnp.int32, shape, 0)
    row = lax.broadcasted_iota(jnp.int32, shape, 1)
    r = jnp.left_shift(1, 2 * g)
    per_class = S // r
    pos = (row % per_class) * r + row // per_class
    inv_freq = jnp.power(jnp.float32(ROPE_THETA), -lax.broadcasted_iota(F32, shape, 2) * 2.0 / ROPE_DIM)
    ang = pos.astype(F32) * inv_freq
    return jnp.concatenate([jnp.cos(ang), jnp.sin(ang)], axis=-1)


def _rope_factors(cs):
    half = ROPE_DIM // 2
    i = lax.broadcasted_iota(jnp.int32, (2 * half, 2 * LANES), 0)
    l = lax.broadcasted_iota(jnp.int32, (2 * half, 2 * LANES), 1)
    hit = ((l < LANES) & (i == l % half)) | ((l >= LANES) & (i - half == l % half))
    spread = jnp.where(hit, 1.0, 0.0).astype(BF16)
    hi = cs.astype(BF16)
    r1 = cs - hi.astype(F32)
    mid = r1.astype(BF16)
    lo = (r1 - mid.astype(F32)).astype(BF16)
    full = (_dot(hi, spread, _NN) + _dot(mid, spread, _NN)) + _dot(lo, spread, _NN)
    cos, sin = full[:, :LANES], full[:, LANES:]
    lane = lax.broadcasted_iota(jnp.int32, cos.shape, 1) % HD
    t_self = jnp.where(lane < ROPE_DIM, cos, 1.0)
    t_up = jnp.where(lane < half, -sin, 0.0)
    t_dn = jnp.where((lane >= half) & (lane < ROPE_DIM), sin, 0.0)
    return t_self, t_up, t_dn


def _residue_pieces(r):
    if r == 1:
        return [(slice(i, i + 512), slice(i, i + 512)) for i in range(0, S, 512)]
    n = S // r
    return [(pl.ds(j, n, stride=r), slice(j * n, (j + 1) * n)) for j in range(r)]


def _rope_apply(x, a, u, d, backward):
    half = ROPE_DIM // 2
    if backward:
        return x * a + pltpu.roll(x * u, half, 1) + pltpu.roll(x * d, LANES - half, 1)
    return x * a + pltpu.roll(x, LANES - half, 1) * u + pltpu.roll(x, half, 1) * d


def _rope_cache_factors(cs_ref, fac, first_step_of_group):
    @pl.when(first_step_of_group)
    def _():
        for lo in range(0, S, 512):
            for j, f in enumerate(_rope_factors(cs_ref[lo:lo + 512, :])):
                fac[j, lo:lo + 512, :] = f


LB_PER_CB = CB // LANES
ROPE_NB = 3 * DIL_G * LB_PER_CB


def _rope_step(s):
    per_group = 3 * LB_PER_CB
    return s // per_group, (s % per_group) // LB_PER_CB, s % LB_PER_CB


def _rope_split(proj_b, tables):
    def body(x_ref, cs_ref, o_ref, fac):
        g, t, hf = _rope_step(pl.program_id(0))
        _rope_cache_factors(cs_ref, fac, (t == 0) & (hf == 0))
        for gs in range(DIL_G):
            @pl.when(g == gs)
            def _(gs=gs):
                for tok, sub in _residue_pieces(DIL_R[gs]):
                    x = x_ref[tok, :]
                    y = _rope_apply(x, fac[0, sub, :], fac[1, sub, :], fac[2, sub, :], False)
                    o_ref[sub, :] = jnp.where(t < 2, y, x).astype(BF16)

    def in_index(s):
        g, t, hf = _rope_step(s)
        return 0, (t * DIL_G + g) * LB_PER_CB + hf

    def out_index(s):
        g, t, hf = _rope_step(s)
        return t * DIL_G + g, 0, hf

    tab = pl.BlockSpec((None, S, 2 * (ROPE_DIM // 2)), lambda s: (_rope_step(s)[0], 0, 0))
    return pl.pallas_call(
        body, name="rope_split", grid=(ROPE_NB,),
        in_specs=[pl.BlockSpec((S, LANES), in_index), tab],
        out_specs=pl.BlockSpec((None, S, LANES), out_index),
        out_shape=jax.ShapeDtypeStruct((3 * DIL_G, S, CB), BF16),
        scratch_shapes=[pltpu.VMEM((3, S, LANES), F32)],
        compiler_params=_cparams(("arbitrary",)),
    )(proj_b, tables)


def _rope_merge_bwd(dq3, dk3, dv3, tables, dproj):
    def body(dq_ref, dk_ref, dv_ref, cs_ref, _, o_ref, tmp, fac):
        s = pl.program_id(0)
        g, t, hf = _rope_step(s)
        _rope_cache_factors(cs_ref, fac, (t == 0) & (hf == 0) & (s < ROPE_NB))
        for gs in range(DIL_G):
            @pl.when((g == gs) & (s < ROPE_NB))
            def _(gs=gs):
                for tok, sub in _residue_pieces(DIL_R[gs]):
                    x = jnp.where(t == 0, dq_ref[sub, :], jnp.where(t == 1, dk_ref[sub, :], dv_ref[sub, :]))
                    y = _rope_apply(x, fac[0, sub, :], fac[1, sub, :], fac[2, sub, :], True)
                    tmp[tok, :] = jnp.where(t < 2, y, x)
                o_ref[...] = tmp[...].astype(BF16)

        @pl.when(s >= ROPE_NB)
        def _():
            o_ref[...] = jnp.zeros_like(o_ref)

    def src_spec(own):
        def index(s):
            g, t, hf = _rope_step(jnp.minimum(s, ROPE_NB - 1))
            return g, 0, jnp.where(t == own, hf, jnp.where(t > own, LB_PER_CB - 1, 0))
        return pl.BlockSpec((None, S, LANES), index)

    def out_index(s):
        g, t, hf = _rope_step(s)
        col = SEG_B // LANES + (t * DIL_G + g) * LB_PER_CB + hf
        return 0, jnp.where(s < ROPE_NB, col, SEG_Z // LANES + s - ROPE_NB)

    tab = pl.BlockSpec((None, S, 2 * (ROPE_DIM // 2)), lambda s: (_rope_step(jnp.minimum(s, ROPE_NB - 1))[0], 0, 0))
    return pl.pallas_call(
        body, name="rope_merge_bwd", grid=(ROPE_NB + LB_PER_CB,),
        in_specs=[src_spec(0), src_spec(1), src_spec(2), tab, ANY],
        out_specs=pl.BlockSpec((S, LANES), out_index),
        out_shape=jax.ShapeDtypeStruct((S, D_PAD), BF16),
        scratch_shapes=[pltpu.VMEM((S, LANES), F32), pltpu.VMEM((3, S, LANES), F32)],
        input_output_aliases={4: 0},
        compiler_params=_cparams(("arbitrary",)),
    )(dq3, dk3, dv3, tables, dproj)


DIL_NB = S // DIL_BLK


_BQK = (((2,), (2,)), ((0,), (0,)))
_BQD = (((2,), (1,)), ((0,), (0,)))
_BKD = (((1,), (1,)), ((0,), (0,)))


def _dil_mask(g):
    shape = (DIL_NB, DIL_BLK, 2 * DIL_BLK)
    blocks_per_seq = jnp.right_shift(DIL_NB, 2 * g)
    has_prev = jnp.bitwise_and(lax.broadcasted_iota(jnp.int32, shape, 0), blocks_per_seq - 1) != 0
    a = lax.broadcasted_iota(jnp.int32, shape, 1)
    kk = lax.broadcasted_iota(jnp.int32, shape, 2)
    diff = DIL_BLK + a - kk
    return (diff >= 0) & (diff <= DIL_BLK) & ((kk >= DIL_BLK) | has_prev)


def _blocks(t):
    return t.reshape(DIL_NB, DIL_BLK, t.shape[-1])


def _with_prev(t):
    prev = jnp.concatenate([jnp.zeros((DIL_BLK, t.shape[-1]), t.dtype), t[:-DIL_BLK]], axis=0)
    return jnp.concatenate([_blocks(prev), _blocks(t)], axis=1)


def _fold_prev(t2):
    n = t2.shape[-1]
    to_prev = t2[:, :DIL_BLK].reshape(S, n)
    own = t2[:, DIL_BLK:].reshape(S, n)
    return own + jnp.concatenate([to_prev[DIL_BLK:], jnp.zeros((DIL_BLK, n), t2.dtype)], axis=0)


def _dil_group_spec(t, width=DIL_OUT):
    return pl.BlockSpec((None, S, width), lambda g: (t * DIL_G + g, 0, 0))


def _dil_fwd(qkv3):
    def body(q_ref, k_ref, v_ref, o_ref, lse_ref):
        ok = _dil_mask(pl.program_id(0))
        lane = lax.broadcasted_iota(jnp.int32, (S, LANES), 1)
        lse = jnp.zeros((S, LANES), F32)
        for h in range(DIL_HG):
            sl = slice(h * HD, (h + 1) * HD)
            s = jnp.where(ok, _dot(_blocks(q_ref[:, sl]), _with_prev(k_ref[:, sl]), _BQK) * SCALE, NEG_INF)
            m = jnp.max(s, axis=-1, keepdims=True)
            p = jnp.exp(s - m)
            l = jnp.sum(p, axis=-1, keepdims=True)
            o_ref[:, sl] = _dot(p / l, _with_prev(v_ref[:, sl]), _BQD).reshape(S, HD)
            lse = jnp.where(lane == h, (m + jnp.log(l)).reshape(S, 1), lse)
        lse_ref[...] = lse

    return pl.pallas_call(
        body, name="dil_fwd", grid=(DIL_G,),
        in_specs=[_dil_group_spec(0), _dil_group_spec(1), _dil_group_spec(2)],
        out_specs=[_dil_group_spec(0), _dil_group_spec(0, LANES)],
        out_shape=[jax.ShapeDtypeStruct((DIL_G, S, DIL_OUT), F32), jax.ShapeDtypeStruct((DIL_G, S, LANES), F32)],
        compiler_params=_cparams(("parallel",)),
    )(qkv3, qkv3, qkv3)


def _dil_bwd(qkv3, lse3, do3, c3):
    def body(q_ref, k_ref, v_ref, lse_ref, do_ref, c_ref, dq_ref, dk_ref, dv_ref):
        ok = _dil_mask(pl.program_id(0))
        lse = lse_ref[...]
        c = c_ref[...]
        for h in range(DIL_HG):
            sl = slice(h * HD, (h + 1) * HD)
            q = _blocks(q_ref[:, sl])
            k2 = _with_prev(k_ref[:, sl])
            do_h = _blocks(do_ref[:, sl])
            s = jnp.where(ok, _dot(q, k2, _BQK) * SCALE, NEG_INF)
            p = jnp.exp(s - _blocks(lse[:, h:h + 1]))
            dp = _dot(do_h, _with_prev(v_ref[:, sl]), _BQK)
            ds = p * (dp - _blocks(c[:, h:h + 1]))
            dsb = (ds * SCALE).astype(BF16)
            dq_ref[:, sl] = _dot(dsb, k2, _BQD).reshape(S, HD)
            dk_ref[:, sl] = _fold_prev(_dot(dsb, q, _BKD))
            dv_ref[:, sl] = _fold_prev(_dot(p, do_h, _BKD))

    return pl.pallas_call(
        body, name="dil_bwd", grid=(DIL_G,),
        in_specs=[_dil_group_spec(0), _dil_group_spec(1), _dil_group_spec(2), _dil_group_spec(0, LANES),
                  _dil_group_spec(0), _dil_group_spec(0, LANES)],
        out_specs=[_dil_group_spec(0)] * 3,
        out_shape=[jax.ShapeDtypeStruct((DIL_G, S, DIL_OUT), F32)] * 3,
        compiler_params=_cparams(("parallel",)),
    )(qkv3, qkv3, qkv3, lse3, do3, c3)


COMB_T = 256


def _dil_combine(o3, lse3):
    heads_per_lb = LANES // HD

    def body(o_ref, lse_ref, ob_ref, al_ref, *scratch):
        o_tok = [scratch[LB_PER_CB * gg:LB_PER_CB * (gg + 1)] for gg in range(DIL_G)]
        lse_tok = scratch[LB_PER_CB * DIL_G:]
        g = pl.program_id(0)
        for gs in range(DIL_G):
            @pl.when(g == gs)
            def _(gs=gs):
                for tok, sub in _residue_pieces(DIL_R[gs]):
                    for hf in range(LB_PER_CB):
                        o_tok[gs][hf][tok, :] = o_ref[sub, hf * LANES:(hf + 1) * LANES]
                    lse_tok[gs][tok, :] = lse_ref[sub, :]

        @pl.when(g == DIL_G - 1)
        def _():
            def chunk(i, carry):
                rows = pl.ds(pl.multiple_of(i * COMB_T, COMB_T), COMB_T)
                lse = [lse_tok[gg][rows, :] for gg in range(DIL_G)]
                m = jnp.maximum(jnp.maximum(lse[0], lse[1]), lse[2])
                e = [jnp.exp(lse[gg] - m) for gg in range(DIL_G)]
                den = (e[0] + e[1]) + e[2]
                al = [e[gg] / den for gg in range(DIL_G)]
                for gg in range(DIL_G):
                    al_ref[gg, rows, :] = al[gg]
                for h in range(DIL_HG):
                    hf, sl = h // heads_per_lb, slice((h % heads_per_lb) * HD, (h % heads_per_lb + 1) * HD)
                    acc = al[0][:, h:h + 1] * o_tok[0][hf][rows, sl]
                    for gg in range(1, DIL_G):
                        acc = acc + al[gg][:, h:h + 1] * o_tok[gg][hf][rows, sl]
                    ob_ref[rows, h * HD:(h + 1) * HD] = acc
                return carry

            lax.fori_loop(0, S // COMB_T, chunk, 0)

    return pl.pallas_call(
        body, name="dil_combine", grid=(DIL_G,),
        in_specs=[pl.BlockSpec((None, S, DIL_OUT), lambda g: (g, 0, 0)),
                  pl.BlockSpec((None, S, LANES), lambda g: (g, 0, 0))],
        out_specs=[pl.BlockSpec((S, DIL_OUT), lambda g: (0, 0)),
                   pl.BlockSpec((DIL_G, S, LANES), lambda g: (0, 0, 0))],
        out_shape=[jax.ShapeDtypeStruct((S, DIL_OUT), F32), jax.ShapeDtypeStruct((DIL_G, S, LANES), F32)],
        scratch_shapes=[pltpu.VMEM((S, LANES), F32)] * (DIL_G * (LB_PER_CB + 1)),
        compiler_params=_cparams(("arbitrary",)),
    )(o3, lse3)


def _dil_combine_bwd(dob, ob, alpha, deps=()):
    heads_per_lb = LANES // HD

    def body(dob_ref, ob_ref, al_ref, *refs):
        do_ref, c_ref = refs[len(deps):]
        g = pl.program_id(0)
        hf = pl.program_id(1)
        for gs in range(DIL_G):
            @pl.when(g == gs)
            def _(gs=gs):
                for tok, sub in _residue_pieces(DIL_R[gs]):
                    dob = dob_ref[tok, :]
                    prod = dob * ob_ref[tok, :]
                    al = al_ref[tok, :]
                    lane = lax.broadcasted_iota(jnp.int32, al.shape, 1)
                    c = jnp.where(hf == 0, 0.0, c_ref[sub, :])
                    for hh in range(heads_per_lb):
                        sl = slice(hh * HD, (hh + 1) * HD)
                        head = hf * heads_per_lb + hh
                        a = jnp.sum(jnp.where(lane == head, al, 0.0), axis=-1, keepdims=True)
                        do_ref[sub, sl] = (a * dob[:, sl]).astype(BF16)
                        c = jnp.where(lane == head, a * jnp.sum(prod[:, sl], axis=-1, keepdims=True), c)
                    c_ref[sub, :] = c

    half = pl.BlockSpec((S, LANES), lambda g, hf: (0, hf))
    return pl.pallas_call(
        body, name="dil_combine_bwd", grid=(DIL_G, LB_PER_CB),
        in_specs=[half, half, pl.BlockSpec((None, S, LANES), lambda g, hf: (g, 0, 0))] + [ANY] * len(deps),
        out_specs=[pl.BlockSpec((None, S, LANES), lambda g, hf: (g, 0, hf)),
                   pl.BlockSpec((None, S, LANES), lambda g, hf: (g, 0, 0))],
        out_shape=[jax.ShapeDtypeStruct((DIL_G, S, DIL_OUT), BF16), jax.ShapeDtypeStruct((DIL_G, S, LANES), F32)],
        compiler_params=_cparams(("parallel", "arbitrary")),
    )(dob, ob, alpha, *deps)


def _local_step(x, tgt, h1, g_attn, g_mlp, g_final, b_pad, w_in_t, hooks):
    tables = _rope_tables()

    first = hooks.first_deps()
    proj_a = _mm(h1, w_in_t, mode="nt", tm=S, tn=512, n=3 * FOX_W, out_dtypes=[BF16], name="proj_a", deps=first)
    proj_b = _mm(h1, w_in_t, mode="nt", tm=S, tn=DIL_W, n=3 * DIL_W, b_off=SEG_B // DIL_W, out_dtypes=[F32], name="proj_b",
                deps=first)
    proj_g = _mm(h1, w_in_t, mode="nt", tm=S, tn=D, n=2 * D, b_off=SEG_G // D, out_dtypes=[F32], name="proj_g",
                deps=first)
    proj_f = _mm(h1, w_in_t, mode="nt", tm=S, tn=LANES, n=LANES, b_off=SEG_F // LANES, out_dtypes=[F32], name="proj_f",
                deps=first)

    q_aug, k_aug = _fox_prepare(proj_a, proj_f, b_pad)
    oa, lse_a = _fox_fwd(q_aug, k_aug, proj_a)

    qkv3 = _rope_split(proj_b, tables)
    o3, lse3 = _dil_fwd(qkv3)
    ob, alpha = _dil_combine(o3, lse3)

    w_a, w_b, w_out = hooks.mixer_weights(ob)
    ya, yb, mixed, x2, h2 = _mixer_tail(oa, ob, w_a, w_b, w_out, proj_g, x, g_mlp)
    w_up_sh, w_down = hooks.mlp_weights(h2)

    def up_epilogue(acc):
        r = jnp.maximum(acc, 0.0)
        return acc, r * r

    u, act = _mm(h2, w_up_sh, mode="nn", tm=S, tn=DFF // N_DEV, b_sharded=True, out_dtypes=[F32, BF16],
                 name="mlp_up", epilogue=up_epilogue)

    def loss_epilogue(acc, rows, vecs):
        dx3, dg, loss = _final_loss_rows(rows[0] + acc, vecs[0], rows[1])
        return (dx3,), (dg, loss)

    dx3, dg_final, loss = _mm_rows(act, w_down, tm=ROWS_TM, name="mlp_down_loss", epilogue=loss_epilogue,
                                   rows=(x2, tgt), vecs=(g_final,), row_out=(F32,), vec_out=(D, LANES))

    def rms_bwd_epilogue(acc, rows, vecs):
        dx, dg = _rms_bwd_rows(acc, rows[0], rows[1], vecs[0])
        return (dx,), (dg,)

    du = _mm(dx3, w_down, mode="nt", tm=S, tn=512, out_dtypes=[BF16], name="mlp_down_bwd",
             epilogue=lambda acc, u_t: (acc * (2.0 * jnp.maximum(u_t, 0.0)),), extras=(u,))
    dw_down = _mm(act, dx3, mode="tn", tm=512, tn=D, out_dtypes=[F32], name="dw_down")
    dw_up_sh = _mm(h2, du, mode="tn", tm=D, tn=DFF // N_DEV, out_sharded=True, out_dtypes=[F32], name="dw_up")
    dx2, dg_mlp = _mm_rows(du, w_up_sh, b_sharded=True, tm=ROWS_TM, name="mlp_up_bwd", epilogue=rms_bwd_epilogue,
                           rows=(x2, dx3), vecs=(g_mlp,), row_out=(F32,), vec_out=(D,),
                           deps=hooks.mlp_grads(dw_up_sh, dw_down))

    dw_out = _mm(mixed, dx2, mode="tn", tm=D, tn=D, out_dtypes=[F32], name="dw_out")

    dyab, dproj = _out_proj_gate_bwd(dx2, w_out, proj_g, ya, yb)
    doa = _mm(dyab, w_a, mode="nt", tm=S, tn=FOX_W, k=D, a_off=0, out_dtypes=[BF16], name="branch_a_bwd")
    dw_a = _mm(oa, dyab, mode="tn", tm=FOX_W, tn=D, n=D, b_off=0, out_dtypes=[F32], name="dw_branch_a")
    dob = _mm(dyab, w_b, mode="nt", tm=S, tn=CB, k=D, a_off=1, out_dtypes=[F32], name="branch_b_bwd")
    dw_b = _mm(ob, dyab, mode="tn", tm=DIL_OUT, tn=D, n=D, b_off=1, out_dtypes=[F32], name="dw_branch_b")

    dproj, dft, dfs = _fox_bwd(q_aug, k_aug, proj_a, lse_a, doa, dproj, deps=hooks.mixer_grads(dw_a, dw_b, dw_out))
    dproj, db = _fox_dscan(dft, dfs, proj_f, b_pad, dproj)

    do3, c3 = _dil_combine_bwd(dob, ob, alpha, deps=hooks.mid_backward(db))
    dq3, dk3, dv3 = _dil_bwd(qkv3, lse3, do3, c3)
    dproj = _rope_merge_bwd(dq3, dk3, dv3, tables, dproj)

    dw_in_t = _mm(dproj, h1, mode="tn", tm=DW_IN_TM, tn=D, out_dtypes=[F32], name="dw_in")
    dx, dg_attn = _mm_rows(dproj, w_in_t, tm=ROWS_TM // 2, name="proj_bwd", epilogue=rms_bwd_epilogue,
                           rows=(x, dx2), vecs=(g_attn,), row_out=(F32,), vec_out=(D,),
                           deps=hooks.w_in_grad(dw_in_t))

    return loss, dx, (dg_attn, db, dg_mlp, dg_final)


SHARD_SHAPES = ((D_IN // N_DEV, D), (FOX_W, D // N_DEV), (DIL_OUT, D // N_DEV), (D // N_DEV, D),
                (D, DFF // N_DEV), (DFF // N_DEV, D))
W_NAMES = ("w_in", "w_a", "w_b", "w_out", "w_up", "w_down")
AG_COPIES = 7
HBM = pl.BlockSpec(memory_space=pltpu.HBM)
SEM = pl.BlockSpec(memory_space=pltpu.SEMAPHORE)


def _place():
    return lax.axis_index("x"), lax.axis_index("y"), lax.axis_index("c")


W_SLAB = 752
W_WIN = 272
PAD_BLK = 256


def _pad_runs():
    runs = sorted((pad, pad + b - a, p, a) for p in range(N_DEV) for a, b, pad in _shard_pieces(p))
    blocks = []
    for k in range(D_PAD // PAD_BLK):
        lo, hi = k * PAD_BLK, (k + 1) * PAD_BLK
        blocks.append([(max(lo, r0) - lo, min(hi, r1) - lo, p, a + max(lo, r0) - r0)
                       for r0, r1, p, a in runs if max(lo, r0) < min(hi, r1)])
    return blocks


def _gather_w_in(shard, x_in, g_attn, late_shards):
    blocks = _pad_runs()

    norm_t = 512
    n_chunks = S // norm_t

    n_late = len(late_shards)
    late_shapes = [t.shape for t in late_shards]

    def body(x_ref, xin_ref, g_ref, *refs):
        late_in, (out_ref, h1_ref), late_out = refs[:n_late], refs[n_late:n_late + 2], refs[n_late + 2:2 * n_late + 2]
        scratch = refs[2 * n_late + 2:]
        stage, land, xbuf = scratch[:3]
        late_stage, late_cast = scratch[3:3 + n_late], scratch[3 + n_late:3 + 2 * n_late]
        send_sems, recv_sems, load_sem, x_sems, late_sems, store_sems = scratch[3 + 2 * n_late:]
        x, y, c = _place()
        me, sibling = (x, y, c), (x, y, 1 - c)
        chips = [(1 - x, y), (x, 1 - y), (1 - x, 1 - y)]

        def slot(px, py, pc):
            return land.at[4 * px + 2 * py + pc]

        def copy(k, block, to):
            return pltpu.make_async_remote_copy(
                src_ref=slot(*block), dst_ref=slot(*block), send_sem=send_sems.at[k], recv_sem=recv_sems.at[k],
                device_id=to, device_id_type=MESH)

        load = pltpu.make_async_copy(x_ref, stage, load_sem)
        load.start()
        load.wait()
        land[4 * x + 2 * y + c] = stage[...].astype(BF16)
        sent = [copy(0, me, sibling)] + [copy(1 + j, me, (*chip, c)) for j, chip in enumerate(chips)]
        for cp in sent:
            cp.start()

        def x_load(i):
            return pltpu.make_async_copy(xin_ref.at[pl.ds(i * norm_t, norm_t)], xbuf.at[i % 2], x_sems.at[i % 2])

        late_loads = [pltpu.make_async_copy(late_in[i], late_stage[i], late_sems.at[i]) for i in range(n_late)]
        for ld in late_loads:
            ld.start()
        x_load(0).start()
        for i in range(n_chunks):
            if i + 1 < n_chunks:
                x_load(i + 1).start()
            x_load(i).wait()
            h1_ref[i * norm_t:(i + 1) * norm_t, :] = _rms_rows(xbuf[i % 2], g_ref[...]).astype(BF16)
        stores = []
        for i in range(n_late):
            late_loads[i].wait()
            late_cast[i][...] = late_stage[i][...].astype(BF16)
            stores.append(pltpu.make_async_copy(late_cast[i], late_out[i].at[4 * x + 2 * y + c], store_sems.at[i]))
            stores[-1].start()

        for j, chip in enumerate(chips):
            copy(1 + j, (*chip, c), me).wait_recv()
            sent.append(copy(4 + j, (*chip, c), sibling))
            sent[-1].start()
        copy(0, sibling, me).wait_recv()
        for j, chip in enumerate(chips):
            copy(4 + j, (*chip, 1 - c), me).wait_recv()
        for cp in sent:
            cp.wait_send()
        for st in stores:
            st.wait()

        row = lax.broadcasted_iota(jnp.int32, (PAD_BLK, D), 0)
        for k, runs in enumerate(blocks):
            out = jnp.zeros((PAD_BLK, D), F32)
            for o_lo, o_hi, p, a in runs:
                start = min(a // 16 * 16, W_SLAB - W_WIN)
                win = land[p, start:start + W_WIN, :].astype(F32)
                moved = pltpu.roll(win, (o_lo - (a - start)) % W_WIN, 0)[:PAD_BLK]
                out = jnp.where((row >= o_lo) & (row < o_hi), moved, out)
            out_ref[k * PAD_BLK:(k + 1) * PAD_BLK, :] = out.astype(BF16)

    return pl.pallas_call(
        body, name="all_gather_w_in",
        out_shape=[jax.ShapeDtypeStruct((D_PAD, D), BF16), jax.ShapeDtypeStruct((S, D), BF16)]
                  + [jax.ShapeDtypeStruct((N_DEV,) + sh, BF16) for sh in late_shapes],
        in_specs=[ANY, ANY, pl.BlockSpec(memory_space=pltpu.VMEM)] + [ANY] * n_late,
        out_specs=[pl.BlockSpec(memory_space=pltpu.VMEM)] * 2 + [ANY] * n_late,
        scratch_shapes=[pltpu.VMEM((W_SLAB, D), F32), pltpu.VMEM((N_DEV, W_SLAB, D), BF16),
                        pltpu.VMEM((2, norm_t, D), F32)]
                       + [pltpu.VMEM(sh, F32) for sh in late_shapes] + [pltpu.VMEM(sh, BF16) for sh in late_shapes]
                       + [pltpu.SemaphoreType.DMA((AG_COPIES,)), pltpu.SemaphoreType.DMA((AG_COPIES,)),
                          pltpu.SemaphoreType.DMA(()), pltpu.SemaphoreType.DMA((2,)),
                          pltpu.SemaphoreType.DMA((n_late,)), pltpu.SemaphoreType.DMA((n_late,))],
        compiler_params=_cparams(None),
    )(shard, x_in, g_attn, *late_shards)


PEER_COPIES = N_DEV - 1
SIBLING_COPIES = 4
CHIP_COPIES = 3


def _peer_copies(_, land_refs, send_sems, recv_sems):
    x, y, c = _place()
    mine = 4 * x + 2 * y + c
    copies = []
    for i, ref in enumerate(land_refs):
        for k in range(1, N_DEV):
            peer = (x ^ (k >> 2), y ^ ((k >> 1) & 1), c ^ (k & 1))
            n = i * PEER_COPIES + k - 1
            copies.append(pltpu.make_async_remote_copy(
                src_ref=ref.at[mine], dst_ref=ref.at[mine], send_sem=send_sems.at[n], recv_sem=recv_sems.at[n],
                device_id=peer, device_id_type=MESH))
    return copies


def _sibling_copies(g_refs, r_refs, send_sems, recv_sems):
    x, y, c = _place()
    return [pltpu.make_async_remote_copy(
        src_ref=g_refs[i].at[2 * k + (1 - c)], dst_ref=r_refs[i].at[k],
        send_sem=send_sems.at[SIBLING_COPIES * i + k], recv_sem=recv_sems.at[SIBLING_COPIES * i + k],
        device_id=(x, y, 1 - c), device_id_type=MESH) for i in range(len(g_refs)) for k in range(SIBLING_COPIES)]


def _chip_copies(p_refs, r_refs, send_sems, recv_sems):
    x, y, c = _place()
    chips = [(1 - x, y), (x, 1 - y), (1 - x, 1 - y)]
    return [pltpu.make_async_remote_copy(
        src_ref=p_refs[i].at[2 * cx + cy], dst_ref=r_refs[i].at[j],
        send_sem=send_sems.at[CHIP_COPIES * i + j], recv_sem=recv_sems.at[CHIP_COPIES * i + j],
        device_id=(cx, cy, c), device_id_type=MESH) for i in range(len(p_refs)) for j, (cx, cy) in enumerate(chips)]


def _in_hbm(a):
    return pltpu.with_memory_space_constraint(a, pltpu.HBM)


def _exchange_start(name, copies_fn, n_copies, srcs, lands, after=()):
    n_s, n_l, n_a = len(srcs), len(lands), len(after)

    def body(*refs):
        outs = refs[n_s + n_l + n_a:]
        for cp in copies_fn(refs[:n_s], refs[n_s:n_s + n_l], outs[0], outs[1]):
            cp.start()
        outs[-1][...] = jnp.zeros_like(outs[-1])

    res = pl.pallas_call(
        body, name=name,
        out_shape=[pltpu.SemaphoreType.DMA((n_copies,)), pltpu.SemaphoreType.DMA((n_copies,))]
                  + [pltpu.HBM(t.shape, t.dtype) for t in (*srcs, *lands)] + [jax.ShapeDtypeStruct((8, LANES), F32)],
        in_specs=[HBM] * (n_s + n_l) + [ANY] * n_a,
        out_specs=[SEM, SEM] + [HBM] * (n_s + n_l) + [pl.BlockSpec(memory_space=pltpu.VMEM)],
        input_output_aliases={i: 2 + i for i in range(n_s + n_l)},
        compiler_params=pltpu.CompilerParams(has_side_effects=pltpu.SideEffectType.DATAFLOW_SIDE_EFFECTING),
    )(*[_in_hbm(t) for t in (*srcs, *lands)], *after)
    return res[0], res[1], list(res[2:2 + n_s]), list(res[2 + n_s:2 + n_s + n_l]), res[-1]


def _exchange_wait(name, copies_fn, started, after):
    send_sems, recv_sems, srcs, lands, _ = started
    n_s, n_l = len(srcs), len(lands)

    def body(*refs):
        for cp in copies_fn(refs[:n_s], refs[n_s:n_s + n_l], refs[n_s + n_l], refs[n_s + n_l + 1]):
            cp.wait_send()
            cp.wait_recv()

    res = pl.pallas_call(
        body, name=name,
        out_shape=[pltpu.HBM(t.shape, t.dtype) for t in (*srcs, *lands)],
        in_specs=[HBM] * (n_s + n_l) + [SEM, SEM, ANY],
        out_specs=[HBM] * (n_s + n_l),
        input_output_aliases={i: i for i in range(n_s + n_l)},
        compiler_params=pltpu.CompilerParams(has_side_effects=pltpu.SideEffectType.DATAFLOW_SIDE_EFFECTING),
    )(*srcs, *lands, send_sems, recv_sems, after)
    return list(res[:n_s]), list(res[n_s:])


def _shard_block(shape):
    rows, cols = shape
    if rows * cols <= 256 * 1024:
        return rows, cols
    if rows % 256:
        return rows, 256
    return 256, cols


def _chip_partial(ids, g_stack, recv1, name):
    shape = g_stack.shape[1:]
    br, bc = _shard_block(shape)

    def body(ids_ref, g_ref, r_ref, pb_ref, own_ref):
        s = g_ref[...] + r_ref[...]
        pb_ref[...] = s.astype(BF16)

        @pl.when(pl.program_id(2) == ids_ref[1])
        def _():
            own_ref[...] = s

    grid_spec = pltpu.PrefetchScalarGridSpec(
        num_scalar_prefetch=1, grid=(shape[0] // br, shape[1] // bc, 4),
        in_specs=[pl.BlockSpec((None, br, bc), lambda r, q, k, ids: (2 * k + ids[0], r, q)),
                  pl.BlockSpec((None, br, bc), lambda r, q, k, ids: (k, r, q))],
        out_specs=[pl.BlockSpec((None, br, bc), lambda r, q, k, ids: (k, r, q)),
                   pl.BlockSpec((br, bc), lambda r, q, k, ids: (r, q))])
    return pl.pallas_call(
        body, name=name, grid_spec=grid_spec,
        out_shape=[jax.ShapeDtypeStruct((4,) + shape, BF16), jax.ShapeDtypeStruct(shape, F32)],
        compiler_params=_cparams(("parallel", "parallel", "arbitrary")),
    )(ids, g_stack, recv1)


def _adamw(w, g, m, v):
    m = ADAM_B1 * m + (1.0 - ADAM_B1) * g
    v = ADAM_B2 * v + (1.0 - ADAM_B2) * (g * g)
    m_hat = m / (1.0 - ADAM_B1 ** ADAM_STEP)
    v_hat = v / (1.0 - ADAM_B2 ** ADAM_STEP)
    delta = -ADAM_LR * (m_hat / (jnp.sqrt(v_hat) + ADAM_EPS) + ADAM_WD * w)
    return delta, m, v


def _reduce_adamw(own, recv2, w, m, v, name, deps=()):
    shape = own.shape
    br, bc = _shard_block(shape)

    def body(own_ref, r_ref, w_ref, m_ref, v_ref, *refs):
        g_ref, d_ref, nm_ref, nv_ref = refs[len(deps):]
        g = own_ref[...]
        for j in range(3):
            g = g + r_ref[j].astype(F32)
        delta, nm, nv = _adamw(w_ref[...], g, m_ref[...], v_ref[...])
        g_ref[...] = g
        d_ref[...] = delta
        nm_ref[...] = nm
        nv_ref[...] = nv

    blk = pl.BlockSpec((br, bc), lambda r, q: (r, q))
    return pl.pallas_call(
        body, name=name, grid=(shape[0] // br, shape[1] // bc),
        in_specs=[blk, pl.BlockSpec((3, br, bc), lambda r, q: (0, r, q)), blk, blk, blk] + [ANY] * len(deps),
        out_specs=[blk] * 4, out_shape=[jax.ShapeDtypeStruct(shape, F32)] * 4,
        compiler_params=_cparams(("parallel", "parallel")),
    )(own, recv2, w, m, v, *deps)


def _small_allreduce_adamw(gvec, w, m, v):
    def body(g_ref, w_ref, m_ref, v_ref, go_ref, d_ref, nm_ref, nv_ref, buf, send_sems, recv_sems):
        x, y, c = _place()
        my_slot = 4 * x + 2 * y + c
        buf[my_slot] = g_ref[...]
        copies = []
        for k in range(1, N_DEV):
            px, py, pc = x ^ (k >> 2), y ^ ((k >> 1) & 1), c ^ (k & 1)
            copies.append(pltpu.make_async_remote_copy(
                src_ref=g_ref, dst_ref=buf.at[my_slot], send_sem=send_sems.at[k - 1], recv_sem=recv_sems.at[k - 1],
                device_id=(px, py, pc), device_id_type=MESH))
        for cp in copies:
            cp.start()
        for k in range(1, N_DEV):
            px, py, pc = x ^ (k >> 2), y ^ ((k >> 1) & 1), c ^ (k & 1)
            pltpu.make_async_remote_copy(
                src_ref=g_ref, dst_ref=buf.at[4 * px + 2 * py + pc], send_sem=send_sems.at[k - 1],
                recv_sem=recv_sems.at[k - 1], device_id=(px, py, pc), device_id_type=MESH).wait_recv()
        for cp in copies:
            cp.wait_send()
        g = buf[0]
        for s in range(1, N_DEV):
            g = g + buf[s]
        delta, nm, nv = _adamw(w_ref[...], g, m_ref[...], v_ref[...])
        go_ref[...] = g
        d_ref[...] = delta
        nm_ref[...] = nm
        nv_ref[...] = nv

    vm = pl.BlockSpec(memory_space=pltpu.VMEM)
    return pl.pallas_call(
        body, name="small_allreduce_adamw",
        in_specs=[vm] * 4, out_specs=[vm] * 4,
        out_shape=[jax.ShapeDtypeStruct((SMALL_R, LANES), F32)] * 4,
        scratch_shapes=[pltpu.VMEM((N_DEV, SMALL_R, LANES), F32),
                        pltpu.SemaphoreType.DMA((N_DEV - 1,)), pltpu.SemaphoreType.DMA((N_DEV - 1,))],
    )(gvec, w, m, v)


def _cols_to_whole(stack):
    rows = stack.shape[1]
    return stack.transpose(1, 0, 2).reshape(rows, -1)


def _whole_to_cols(t):
    rows = t.shape[0]
    return t.reshape(rows, N_DEV, -1).transpose(1, 0, 2)


W_IN_SEGMENTS = ((0, 3 * FOX_W, SEG_A), (3 * FOX_W, 3 * FOX_W + FOX_H, SEG_F),
                 (3 * FOX_W + FOX_H, 3 * FOX_W + FOX_H + 3 * DIL_W, SEG_B), (3 * FOX_W + FOX_H + 3 * DIL_W, D_IN, SEG_G))


def _shard_pieces(p):
    rows = D_IN // N_DEV
    lo, hi = p * rows, (p + 1) * rows
    return [(max(lo, a) - lo, min(hi, b) - lo, pad + max(lo, a) - a)
            for a, b, pad in W_IN_SEGMENTS if max(lo, a) < min(hi, b)]


def _unpad_dw_in_t(dwp):
    return jnp.stack([jnp.concatenate([dwp[pad:pad + b - a] for a, b, pad in _shard_pieces(p)], axis=0)
                      for p in range(N_DEV)])


LOSS_ROW = 25


def _pack_small(g_attn, b, g_mlp, g_final, loss_row=None):
    tail = jnp.pad(b, ((0, 7), (0, LANES - b.shape[1])))
    if loss_row is not None:
        tail = tail + jnp.pad(loss_row, ((LOSS_ROW - 24, 31 - LOSS_ROW), (0, 0)))
    return jnp.concatenate([g_attn.reshape(8, LANES), g_mlp.reshape(8, LANES), g_final.reshape(8, LANES), tail], axis=0)


def _unpack_small(p):
    return (p[0:8].reshape(1, D), p[24:25, :FOX_H], p[8:16].reshape(1, D), p[16:24].reshape(D))


class _StepComm:
    def __init__(self, ids, w_sh, m_sh, v_sh, lands, after_w_in):
        self.ids = ids
        self.w_sh, self.m_sh, self.v_sh = w_sh, m_sh, v_sh
        self.updates = None
        self.gather_mixer = _exchange_start("gather_mixer_start", _peer_copies, PEER_COPIES * 3, [], lands[:3],
                                            after=(after_w_in,))
        self.gather_mlp = _exchange_start("gather_mlp_start", _peer_copies, PEER_COPIES * 2, [], lands[3:],
                                          after=(self.gather_mixer[-1],))
        self.sibling = {}
        self.chips = {}
        self.own = {}
        self.names = {}

    def _reduce_start(self, group, names, grads, after=()):
        lands = [lax.empty((SIBLING_COPIES,) + t.shape[1:], F32) for t in grads]
        self.sibling[group] = _exchange_start("grad_%s_sibling_start" % group, _sibling_copies,
                                              SIBLING_COPIES * len(grads), grads, lands, after=after)
        self.names[group] = names
        return self.sibling[group][-1]

    def _reduce_mid(self, group, after):
        grads, recv1 = _exchange_wait("grad_%s_sibling_wait" % group, _sibling_copies, self.sibling[group], after)
        parts = [_chip_partial(self.ids, g, r, "grad_partial_" + n) for g, r, n in zip(grads, recv1, self.names[group])]
        self.own[group] = [p[1] for p in parts]
        srcs = [p[0] for p in parts]
        lands = [lax.empty((CHIP_COPIES,) + t.shape[1:], BF16) for t in srcs]
        self.chips[group] = _exchange_start("grad_%s_chips_start" % group, _chip_copies, CHIP_COPIES * len(srcs),
                                            srcs, lands)
        return self.chips[group][-1]

    def reduced(self, group, after):
        _, recv2 = _exchange_wait("grad_%s_chips_wait" % group, _chip_copies, self.chips[group], after)
        return list(zip(self.own[group], recv2))

    def first_deps(self):
        return [self.gather_mlp[-1]]

    def mixer_weights(self, after):
        _, (g_a, g_b, g_out) = _exchange_wait("gather_mixer_wait", _peer_copies, self.gather_mixer, after)
        return _cols_to_whole(g_a), _cols_to_whole(g_b), g_out.reshape(D, D)

    def mlp_weights(self, after):
        _, (g_up, g_down) = _exchange_wait("gather_mlp_wait", _peer_copies, self.gather_mlp, after)
        return g_up, g_down.reshape(DFF, D)

    def mlp_grads(self, dw_up_sh, dw_down):
        return [self._reduce_start("mlp", W_NAMES[4:], [dw_up_sh, dw_down.reshape((N_DEV,) + SHARD_SHAPES[5])])]

    def mixer_grads(self, dw_a, dw_b, dw_out):
        token = self._reduce_mid("mlp", dw_b)
        grads = [_whole_to_cols(dw_a), _whole_to_cols(dw_b), dw_out.reshape((N_DEV,) + SHARD_SHAPES[3])]
        return [self._reduce_start("mixer", W_NAMES[1:4], grads, after=(token,))]

    def mid_backward(self, after):
        return [self._reduce_mid("mixer", after)]

    def w_in_grad(self, dw_in_t):
        grads = [_unpad_dw_in_t(dw_in_t)]
        token = self._reduce_start("w_in", W_NAMES[:1], grads)
        reduced = self.reduced("mixer", token) + self.reduced("mlp", token)
        self.updates, last = [None] * len(reduced), token
        for i in (3, 4, 0, 1, 2):
            if i == 0:
                last = self._reduce_mid("w_in", last)
            self.updates[i] = _reduce_adamw(*reduced[i], self.w_sh[1 + i], self.m_sh[1 + i], self.v_sh[1 + i],
                                            "adamw_" + W_NAMES[1 + i], deps=[last])
            last = self.updates[i][0]
        return [last]

    def w_in_update(self, after):
        (reduced,) = self.reduced("w_in", after)
        return _reduce_adamw(*reduced, self.w_sh[0], self.m_sh[0], self.v_sh[0], "adamw_" + W_NAMES[0])


def kernel(x, norm_attn_g, w_in, b_forget, w_branch_a, w_branch_b, w_out, norm_mlp_g, w_up, w_down, norm_final_g, loss_target, m_norm_attn_g, m_w_in, m_b_forget, m_w_branch_a, m_w_branch_b, m_w_out, m_norm_mlp_g, m_w_up, m_w_down, m_norm_final_g, v_norm_attn_g, v_w_in, v_b_forget, v_w_branch_a, v_w_branch_b, v_w_out, v_norm_mlp_g, v_w_up, v_w_down, v_norm_final_g):
    cx, cy, cc = _place()
    ids = jnp.stack([cc, 2 * cx + cy]).astype(jnp.int32)

    w_sh = [w_in[0].T] + [t[0] for t in (w_branch_a, w_branch_b, w_out, w_up, w_down)]
    m_sh = [m_w_in[0].T] + [t[0] for t in (m_w_branch_a, m_w_branch_b, m_w_out, m_w_up, m_w_down)]
    v_sh = [v_w_in[0].T] + [t[0] for t in (v_w_branch_a, v_w_branch_b, v_w_out, v_w_up, v_w_down)]

    w_in_t, h1, *lands = _gather_w_in(jnp.pad(w_sh[0], ((0, W_SLAB - w_sh[0].shape[0]), (0, 0))), x[0], norm_attn_g,
                                      w_sh[1:])
    comm = _StepComm(ids, w_sh, m_sh, v_sh, lands, w_in_t)
    b_pad = jnp.pad(b_forget, ((0, 0), (0, LANES - FOX_H)))

    loss_row, dx, dsmall = _local_step(
        x[0], loss_target[0], h1, norm_attn_g, norm_mlp_g, norm_final_g.reshape(1, D), b_pad, w_in_t, comm)

    dg_attn, db, dg_mlp, dg_final = dsmall
    small = _small_allreduce_adamw(
        _pack_small(dg_attn, db, dg_mlp, dg_final, loss_row),
        _pack_small(norm_attn_g, b_forget, norm_mlp_g, norm_final_g.reshape(1, D)),
        _pack_small(m_norm_attn_g, m_b_forget, m_norm_mlp_g, m_norm_final_g.reshape(1, D)),
        _pack_small(v_norm_attn_g, v_b_forget, v_norm_mlp_g, v_norm_final_g.reshape(1, D)))
    big = [comm.w_in_update(small[0])] + comm.updates

    outs = [small[0][LOSS_ROW, 0], dx[None]]
    for q in range(4):
        s_attn, s_b, s_mlp, s_final = _unpack_small(small[q])
        b_in, b_a, b_b, b_out, b_up, b_down = [t[q][None] for t in big]
        b_in = jnp.swapaxes(b_in, 1, 2)
        outs += [s_attn, b_in, s_b, b_a, b_b, b_out, s_mlp, b_up, b_down, s_final]
    return tuple(outs)
```

```python
import functools

import jax
import jax.numpy as jnp
from jax import lax
from jax.experimental import pallas as pl
from jax.experimental.pallas import tpu as pltpu

F32 = jnp.float32
BF16 = jnp.bfloat16
MESH = pl.DeviceIdType.MESH

S = 2048
D = 1024
HD = 64
FOX_H = 8
FOX_W = FOX_H * HD
DIL_HG = 4
DIL_G = 3
DIL_W = DIL_G * DIL_HG * HD
DIL_OUT = DIL_HG * HD
DIL_BLK = 128
DIL_R = (1, 4, 16)
DFF = 4 * D
D_IN = 3 * FOX_W + FOX_H + 3 * DIL_W + 2 * D
EPS = 1e-6
NEG_INF = -1e30
SCALE = HD ** -0.5
ROPE_THETA = 500000.0
ROPE_DIM = HD // 4
N_DEV = 8

ADAM_LR = 0.001
ADAM_B1 = 0.9
ADAM_B2 = 0.999
ADAM_EPS = 1e-08
ADAM_WD = 0.01
ADAM_STEP = 10

LANES = 128
CB = 256
SEG_A = 0
SEG_B = 3 * FOX_W
SEG_Z = SEG_B + 3 * DIL_W
SEG_G = SEG_Z + CB
SEG_F = SEG_G + 2 * D
D_PAD = SEG_F + CB
DW_IN_TM = D_PAD // 10
SMALL_R = 32

VMEM_MB = 56


def _cparams(dims=None, vmem_mb=VMEM_MB, **kw):
    return pltpu.CompilerParams(dimension_semantics=dims, vmem_limit_bytes=vmem_mb << 20, **kw)


ANY = pl.BlockSpec(memory_space=pl.ANY)

_NN = (((1,), (0,)), ((), ()))
_NT = (((1,), (1,)), ((), ()))
_TN = (((0,), (0,)), ((), ()))


def _dot(a, b, dims):
    return lax.dot_general(a.astype(BF16), b.astype(BF16), dims, preferred_element_type=F32)


def _mm(a, b, *, mode, tm, tn, out_dtypes, name, n=None, k=None, a_off=0, b_off=0,
        b_sharded=False, out_sharded=False, epilogue=None, extras=(), deps=()):
    n_sh = b.shape[-1] if b_sharded else None
    if mode == "nn":
        m = a.shape[0]
        k = k or a.shape[1]
        a_spec = pl.BlockSpec((tm, k), lambda i, j: (i, a_off))
        if b_sharded:
            assert tn == n_sh
            n = N_DEV * n_sh
            b_spec = pl.BlockSpec((None, k, tn), lambda i, j: (j, 0, 0))
        else:
            n = n or b.shape[1]
            b_spec = pl.BlockSpec((k, tn), lambda i, j: (0, j + b_off))
        dims = _NN
    elif mode == "nt":
        m = a.shape[0]
        k = k or a.shape[1]
        a_spec = pl.BlockSpec((tm, k), lambda i, j: (i, a_off))
        if b_sharded:
            n = b.shape[1]
            b_spec = pl.BlockSpec((N_DEV, tn, n_sh), lambda i, j: (0, j, 0))
        else:
            n = n or b.shape[0]
            b_spec = pl.BlockSpec((tn, k), lambda i, j: (j + b_off, 0))
        dims = _NT
    else:
        k, m = a.shape
        n = n or b.shape[1]
        a_spec = pl.BlockSpec((k, tm), lambda i, j: (0, i))
        b_spec = pl.BlockSpec((k, tn), lambda i, j: (0, j + b_off))
        dims = _TN
    assert m % tm == 0 and n % tn == 0, (name, m, n, tm, tn)
    n_extra = len(extras)
    tile = pl.BlockSpec((tm, tn), lambda i, j: (i, j))
    if out_sharded:
        assert mode == "tn" and n // tn == N_DEV
        out_spec = pl.BlockSpec((None, tm, tn), lambda i, j: (j, i, 0))
        out_shape = (N_DEV, m, tn)
    else:
        out_spec, out_shape = tile, (m, n)

    def body(a_ref, b_ref, *refs):
        if mode == "nt" and b_sharded:
            acc = _dot(a_ref[:, 0:n_sh], b_ref[0], dims)
            for p in range(1, N_DEV):
                acc = acc + _dot(a_ref[:, p * n_sh:(p + 1) * n_sh], b_ref[p], dims)
        else:
            acc = _dot(a_ref[...], b_ref[...], dims)
        ex = [r[...] for r in refs[:n_extra]]
        outs = epilogue(acc, *ex) if epilogue is not None else (acc,)
        for o_ref, o in zip(refs[n_extra + len(deps):], outs):
            o_ref[...] = o.astype(o_ref.dtype)

    res = pl.pallas_call(
        body, name=name, grid=(m // tm, n // tn),
        in_specs=[a_spec, b_spec] + [tile] * n_extra + [ANY] * len(deps),
        out_specs=[out_spec] * len(out_dtypes),
        out_shape=[jax.ShapeDtypeStruct(out_shape, dt) for dt in out_dtypes],
        compiler_params=_cparams(("parallel", "parallel")),
    )(a, b, *extras, *deps)
    return res if len(out_dtypes) > 1 else res[0]


def _mm_rows(a, b, *, tm, name, epilogue, rows=(), vecs=(), row_out=(), vec_out=(), b_sharded=False, deps=()):
    m, k = a.shape
    n_rows, n_vecs, n_deps = len(rows), len(vecs), len(deps)
    n_sh = b.shape[-1] if b_sharded else None

    def body(a_ref, b_ref, *refs):
        row_refs, vec_refs = refs[:n_rows], refs[n_rows:n_rows + n_vecs]
        outs = refs[n_rows + n_vecs + n_deps:]
        if b_sharded:
            acc = _dot(a_ref[:, 0:n_sh], b_ref[0], _NT)
            for p in range(1, N_DEV):
                acc = acc + _dot(a_ref[:, p * n_sh:(p + 1) * n_sh], b_ref[p], _NT)
        else:
            acc = _dot(a_ref[...], b_ref[...], _NN)
        row_vals, vec_incs = epilogue(acc, [r[...] for r in row_refs], [v[...] for v in vec_refs])
        for o_ref, val in zip(outs[:len(row_out)], row_vals):
            o_ref[...] = val.astype(o_ref.dtype)

        @pl.when(pl.program_id(0) == 0)
        def _():
            for o_ref in outs[len(row_out):]:
                o_ref[...] = jnp.zeros_like(o_ref)

        for o_ref, inc in zip(outs[len(row_out):], vec_incs):
            o_ref[...] += inc

    tile = pl.BlockSpec((tm, D), lambda i: (i, 0))
    b_spec = pl.BlockSpec(b.shape, lambda i: (0,) * b.ndim)
    return pl.pallas_call(
        body, name=name, grid=(m // tm,),
        in_specs=[pl.BlockSpec((tm, k), lambda i: (i, 0)), b_spec] + [tile] * n_rows
                 + [pl.BlockSpec(v.shape, lambda i: (0, 0)) for v in vecs] + [ANY] * n_deps,
        out_specs=[tile] * len(row_out) + [pl.BlockSpec((1, w), lambda i: (0, 0)) for w in vec_out],
        out_shape=[jax.ShapeDtypeStruct((m, D), dt) for dt in row_out]
                  + [jax.ShapeDtypeStruct((1, w), F32) for w in vec_out],
        compiler_params=_cparams(("arbitrary",)),
    )(a, b, *rows, *vecs, *deps)


ROW_T = 256
ROWS_TM = 512


def _rms_rows(x, g):
    r = lax.rsqrt(jnp.mean(x * x, axis=-1, keepdims=True) + EPS)
    return (x * r) * g


def _rms_bwd_rows(dh, x, dres, g):
    r = lax.rsqrt(jnp.mean(x * x, axis=-1, keepdims=True) + EPS)
    xn = x * r
    dhn = dh * g
    dx = dres + r * (dhn - xn * jnp.mean(dhn * xn, axis=-1, keepdims=True))
    return dx, jnp.sum(dh * xn, axis=0, keepdims=True)


def _final_loss_rows(x, g, tgt):
    r = lax.rsqrt(jnp.mean(x * x, axis=-1, keepdims=True) + EPS)
    xn = x * r
    err = xn * g - tgt
    row_loss = jnp.mean(err * err, axis=-1, keepdims=True)
    loss = 0.5 * jnp.sum(row_loss, axis=0, keepdims=True) * jnp.ones((1, LANES), F32)
    dy = err * (1.0 / D)
    dyn = dy * g
    dx = r * (dyn - xn * jnp.mean(dyn * xn, axis=-1, keepdims=True))
    return dx, jnp.sum(dy * xn, axis=0, keepdims=True), loss


def _sigmoid(z):
    return 1.0 / (1.0 + jnp.exp(-z))


GATE_TR = 512


def _mixer_tail(oa, ob, w_a, w_b, w_out, proj_g, x, g_mlp):
    def body(oa_ref, ob_ref, wa_ref, wb_ref, wo_ref, g_ref, x_ref, gm_ref, ya_ref, yb_ref, mixed_ref, x2_ref, h2_ref):
        ya = _dot(oa_ref[...], wa_ref[...], _NN)
        yb = _dot(ob_ref[...], wb_ref[...], _NN)
        ya_ref[...] = ya
        yb_ref[...] = yb
        mixed = (_sigmoid(g_ref[:, :D]) * ya + _sigmoid(g_ref[:, D:]) * yb).astype(BF16)
        mixed_ref[...] = mixed
        x2 = x_ref[...] + _dot(mixed, wo_ref[...], _NN)
        x2_ref[...] = x2
        h2_ref[...] = _rms_rows(x2, gm_ref[...]).astype(BF16)

    def rows(width):
        return pl.BlockSpec((GATE_TR, width), lambda i: (i, 0))

    def whole(t):
        return pl.BlockSpec(t.shape, lambda i: (0, 0))

    return pl.pallas_call(
        body, name="mixer_tail", grid=(S // GATE_TR,),
        in_specs=[rows(FOX_W), rows(DIL_OUT), whole(w_a), whole(w_b), whole(w_out), rows(2 * D), rows(D), whole(g_mlp)],
        out_specs=[rows(D)] * 5,
        out_shape=[jax.ShapeDtypeStruct((S, D), dt) for dt in (F32, F32, BF16, F32, BF16)],
        compiler_params=_cparams(("parallel",)),
    )(oa, ob, w_a, w_b, w_out, proj_g, x, g_mlp)


def _out_proj_gate_bwd(dx2, w_out, proj_g, ya, yb):
    def body(dx_ref, w_ref, g_ref, ya_ref, yb_ref, dy_ref, dg_ref):
        dm = _dot(dx_ref[...], w_ref[...], _NT)
        for half, y_ref in enumerate((ya_ref, yb_ref)):
            cols = slice(half * D, (half + 1) * D)
            s = _sigmoid(g_ref[:, cols])
            dy_ref[:, cols] = (dm * s).astype(BF16)
            dg_ref[:, cols] = (dm * y_ref[...] * (s * (1.0 - s))).astype(BF16)

    row = pl.BlockSpec((GATE_TR, D), lambda i: (i, 0))
    wide = pl.BlockSpec((GATE_TR, 2 * D), lambda i: (i, 0))
    return pl.pallas_call(
        body, name="out_proj_gate_bwd", grid=(S // GATE_TR,),
        in_specs=[row, pl.BlockSpec((D, D), lambda i: (0, 0)), wide, row, row],
        out_specs=[wide, pl.BlockSpec((GATE_TR, 2 * D), lambda i: (i, SEG_G // (2 * D)))],
        out_shape=[jax.ShapeDtypeStruct((S, 2 * D), BF16), jax.ShapeDtypeStruct((S, D_PAD), BF16)],
        compiler_params=_cparams(("parallel",)),
    )(dx2, w_out, proj_g, ya, yb)


FOX_TQ = 512
FOX_TQ_FWD = 1024


def _scan_rows(x, reverse):
    n = x.shape[0]
    row = lax.broadcasted_iota(jnp.int32, x.shape, 0)
    k = 1
    while k < n:
        if reverse:
            x = x + jnp.where(row < n - k, pltpu.roll(x, n - k, 0), 0.0)
        else:
            x = x + jnp.where(row >= k, pltpu.roll(x, k, 0), 0.0)
        k *= 2
    return x


def _fox_dscan(dft, dfs, proj_f, b_pad, dproj):
    def body(dft_ref, dfs_ref, f_ref, b_ref, _, dfa_ref, db_ref):
        dfs_pad = jnp.concatenate([dfs_ref[...], jnp.zeros((LANES - FOX_H, S), F32)], axis=0)
        df = dfs_pad.T + dft_ref[0]
        for hp in range(1, dft_ref.shape[0]):
            df = df + pltpu.roll(dft_ref[hp], 2 * hp, 1)
        dlf = _scan_rows(df, reverse=True)
        z = f_ref[...] + b_ref[...]
        lane = lax.broadcasted_iota(jnp.int32, (S, LANES), 1)
        dfa = jnp.where(lane < FOX_H, dlf / (1.0 + jnp.exp(z)), 0.0)
        dfa_ref[:, :LANES] = dfa.astype(BF16)
        dfa_ref[:, LANES:] = jnp.zeros((S, CB - LANES), BF16)
        db_ref[...] = jnp.sum(dfa, axis=0, keepdims=True)

    return pl.pallas_call(
        body, name="fox_dscan", grid=(1,),
        in_specs=[pl.BlockSpec(dft.shape, lambda i: (0, 0, 0)), pl.BlockSpec((FOX_H, S), lambda i: (0, 0)),
                  pl.BlockSpec((S, LANES), lambda i: (0, 0)), pl.BlockSpec((1, LANES), lambda i: (0, 0)), ANY],
        out_specs=[pl.BlockSpec((S, CB), lambda i: (0, SEG_F // CB)), pl.BlockSpec((1, LANES), lambda i: (0, 0))],
        out_shape=[jax.ShapeDtypeStruct((S, D_PAD), BF16), jax.ShapeDtypeStruct((1, LANES), F32)],
        input_output_aliases={4: 0},
        compiler_params=_cparams(("arbitrary",)),
    )(dft, dfs, proj_f, b_pad, dproj)


FOX_NQ = S // FOX_TQ
FOX_HP = FOX_W // LANES
HEADS_PER_LB = LANES // HD
FOX_AUG = 2 * LANES


def _fox_prepare(proj_a, proj_f, b_pad):
    tr = 512

    def body(q_ref, k_ref, f_ref, b_ref, qa_ref, ka_ref, hi_s, mid_s, lo_s):
        i = pl.program_id(0)

        @pl.when(i == 0)
        def _():
            z = f_ref[...] + b_ref[...]
            lf = jnp.minimum(z, 0.0) - jnp.log1p(jnp.exp(-jnp.abs(z)))
            f = _scan_rows(lf, reverse=False)
            hi = f.astype(BF16).astype(F32)
            r1 = f - hi
            mid = r1.astype(BF16).astype(F32)
            hi_s[...] = hi
            mid_s[...] = mid
            lo_s[...] = (r1 - mid).astype(BF16).astype(F32)

        rows = pl.ds(pl.multiple_of(i * tr, tr), tr)
        hi, mid, lo = hi_s[rows, :], mid_s[rows, :], lo_s[rows, :]
        lane = lax.broadcasted_iota(jnp.int32, (tr, HD), 1)
        q_ones = jnp.where((lane >= 3) & (lane < 6), 1.0, 0.0)
        k_ones = jnp.where(lane < 3, 1.0, 0.0)
        for h in range(FOX_H):
            a1, a2, a3 = hi[:, h:h + 1], mid[:, h:h + 1], lo[:, h:h + 1]
            q_extra = jnp.where(lane == 0, a1, jnp.where(lane == 1, a2, jnp.where(lane == 2, a3, q_ones)))
            k_extra = jnp.where(lane == 3, -a1, jnp.where(lane == 4, -a2, jnp.where(lane == 5, -a3, k_ones)))
            base = h * LANES
            qa_ref[:, base:base + HD] = q_ref[:, h * HD:(h + 1) * HD] * jnp.asarray(SCALE, BF16)
            qa_ref[:, base + HD:base + LANES] = q_extra.astype(BF16)
            ka_ref[:, base:base + HD] = k_ref[:, h * HD:(h + 1) * HD]
            ka_ref[:, base + HD:base + LANES] = k_extra.astype(BF16)

    out_blk = pl.BlockSpec((tr, FOX_H * LANES), lambda i: (i, 0))
    return pl.pallas_call(
        body, name="fox_prepare", grid=(S // tr,),
        in_specs=[pl.BlockSpec((tr, FOX_W), lambda i: (i, 0)), pl.BlockSpec((tr, FOX_W), lambda i: (i, 1)),
                  pl.BlockSpec((S, LANES), lambda i: (0, 0)), pl.BlockSpec((1, LANES), lambda i: (0, 0))],
        out_specs=[out_blk, out_blk],
        out_shape=[jax.ShapeDtypeStruct((S, FOX_H * LANES), BF16)] * 2,
        scratch_shapes=[pltpu.VMEM((S, LANES), F32)] * 3,
        compiler_params=_cparams(("arbitrary",)),
    )(proj_a, proj_a, proj_f, b_pad)


def _fox_scores(qa, ka, w, tq):
    s = _dot(qa, ka, _NT)
    row = lax.broadcasted_iota(jnp.int32, (tq, tq), 0)
    col = lax.broadcasted_iota(jnp.int32, (tq, tq), 1)
    diag = jnp.where(col <= row, s[:, w * tq:], NEG_INF)
    return diag if w == 0 else jnp.concatenate([s[:, :w * tq], diag], axis=1)


def _fox_fwd(q_aug, k_aug, proj_a):
    tq = FOX_TQ_FWD

    def body(qa_ref, ka_ref, v_ref, o_ref, lse_ref):
        qi = pl.program_id(1)
        lane = lax.broadcasted_iota(jnp.int32, (tq, LANES), 1)
        for w in range(S // tq):
            @pl.when(qi == w)
            def _(w=w):
                width = (w + 1) * tq
                lse = jnp.zeros((tq, LANES), F32)
                for hh in range(HEADS_PER_LB):
                    aug = slice(hh * LANES, (hh + 1) * LANES)
                    sl = slice(hh * HD, (hh + 1) * HD)
                    s = _fox_scores(qa_ref[:, aug], ka_ref[:width, aug], w, tq)
                    m = jnp.max(s, axis=-1, keepdims=True)
                    p = jnp.exp(s - m)
                    l = jnp.sum(p, axis=-1, keepdims=True)
                    o_ref[:, sl] = _dot(p / l, v_ref[:width, sl], _NN)
                    lse = jnp.where(lane == hh, m + jnp.log(l), lse)
                lse_ref[...] = lse

    return pl.pallas_call(
        body, name="fox_fwd", grid=(FOX_HP, S // tq),
        in_specs=[pl.BlockSpec((tq, FOX_AUG), lambda h, i: (i, h)),
                  pl.BlockSpec((S, FOX_AUG), lambda h, i: (0, h)),
                  pl.BlockSpec((S, LANES), lambda h, i: (0, 2 * FOX_HP + h))],
        out_specs=[pl.BlockSpec((tq, LANES), lambda h, i: (i, h)),
                   pl.BlockSpec((None, tq, LANES), lambda h, i: (h, i, 0))],
        out_shape=[jax.ShapeDtypeStruct((S, FOX_W), F32), jax.ShapeDtypeStruct((FOX_HP, S, LANES), F32)],
        compiler_params=_cparams(("parallel", "parallel")),
    )(q_aug, k_aug, proj_a)


def _fox_bwd(q_aug, k_aug, proj_a, lse, do, dproj, deps=()):
    n_q = FOX_NQ

    def body(qa_ref, ka_ref, v_ref, lse_ref, do_ref, *refs):
        dproj_ref, dft_ref, dfs_ref, dk_acc, dv_acc, dq_buf, kv_buf, dq_sems, kv_sems = refs[1 + len(deps):]
        hp = pl.program_id(0)
        t = pl.program_id(1)
        slot = t % 2
        col = pl.multiple_of(hp * LANES, LANES)

        def dq_copy(step, buf_slot):
            rows = pl.ds(pl.multiple_of(step * FOX_TQ, FOX_TQ), FOX_TQ)
            return pltpu.make_async_copy(dq_buf.at[buf_slot], dproj_ref.at[rows, pl.ds(col, LANES)], dq_sems.at[buf_slot])

        @pl.when(t == 0)
        def _():
            dk_acc[...] = jnp.zeros_like(dk_acc)
            dv_acc[...] = jnp.zeros_like(dv_acc)

        @pl.when((t == 0) & (hp == 0))
        def _():
            dfs_ref[...] = jnp.zeros_like(dfs_ref)

        @pl.when(t >= 2)
        def _():
            dq_copy(t - 2, slot).wait()

        lane = lax.broadcasted_iota(jnp.int32, (FOX_TQ, LANES), 1)
        for w in range(n_q):
            @pl.when(t == w)
            def _(w=w):
                width = (w + 1) * FOX_TQ
                lse_all = lse_ref[...]
                sub = lax.broadcasted_iota(jnp.int32, (FOX_H, width), 0)
                dft = jnp.zeros((FOX_TQ, LANES), F32)
                for hh in range(HEADS_PER_LB):
                    aug = slice(hh * LANES, (hh + 1) * LANES)
                    sl = slice(hh * HD, (hh + 1) * HD)
                    head = hp * HEADS_PER_LB + hh
                    qa = qa_ref[:, aug]
                    ka = ka_ref[:width, aug]
                    do_h = do_ref[:, sl]
                    s = _fox_scores(qa, ka, w, FOX_TQ)
                    p = jnp.exp(s - lse_all[:, hh:hh + 1])
                    dp = _dot(do_h, v_ref[:width, sl], _NT)
                    ds = p * (dp - jnp.sum(dp * p, axis=-1, keepdims=True))
                    dft = jnp.where(lane == hh, jnp.sum(ds, axis=-1, keepdims=True), dft)
                    dfs_ref[:, :width] -= jnp.where(sub == head, jnp.sum(ds, axis=0, keepdims=True), 0.0)
                    dq_buf[slot, :, sl] = (_dot(ds, ka[:, :HD], _NN) * SCALE).astype(BF16)
                    dk_acc[:width, sl] += _dot(ds, qa[:, :HD], _TN)
                    dv_acc[:width, sl] += _dot(p, do_h, _TN)
                dft_ref[...] = dft

        dq_copy(t, slot).start()

        @pl.when(t == n_q - 1)
        def _():
            dq_copy(t - 1, 1 - slot).wait()
            dq_copy(t, slot).wait()
            kv_buf[0] = dk_acc[...].astype(BF16)
            kv_buf[1] = dv_acc[...].astype(BF16)
            copies = [pltpu.make_async_copy(
                kv_buf.at[j], dproj_ref.at[:, pl.ds(pl.multiple_of((j + 1) * FOX_W + hp * LANES, LANES), LANES)],
                kv_sems.at[j]) for j in range(2)]
            for cp in copies:
                cp.start()
            for cp in copies:
                cp.wait()

    qblk = pl.BlockSpec((FOX_TQ, FOX_AUG), lambda h, t: (t, h))
    lane_blk = pl.BlockSpec((None, FOX_TQ, LANES), lambda h, t: (h, t, 0))
    return pl.pallas_call(
        body, name="fox_bwd", grid=(FOX_HP, n_q),
        in_specs=[qblk, pl.BlockSpec((S, FOX_AUG), lambda h, t: (0, h)),
                  pl.BlockSpec((S, LANES), lambda h, t: (0, 2 * FOX_HP + h)),
                  lane_blk, pl.BlockSpec((FOX_TQ, LANES), lambda h, t: (t, h)), ANY] + [ANY] * len(deps),
        out_specs=[ANY, lane_blk, pl.BlockSpec((FOX_H, S), lambda h, t: (0, 0))],
        out_shape=[jax.ShapeDtypeStruct((S, D_PAD), BF16),
                   jax.ShapeDtypeStruct((FOX_HP, S, LANES), F32), jax.ShapeDtypeStruct((FOX_H, S), F32)],
        scratch_shapes=[pltpu.VMEM((S, LANES), F32), pltpu.VMEM((S, LANES), F32),
                        pltpu.VMEM((2, FOX_TQ, LANES), BF16), pltpu.VMEM((2, S, LANES), BF16),
                        pltpu.SemaphoreType.DMA((2,)), pltpu.SemaphoreType.DMA((2,))],
        input_output_aliases={5: 0},
        compiler_params=_cparams(("arbitrary", "arbitrary")),
    )(q_aug, k_aug, proj_a, lse, do, dproj, *deps)


def _rope_tables():
    half = ROPE_DIM // 2
    shape = (DIL_G, S, half)
    g = lax.broadcasted_iota(jnp.int32, shape, 0)
    row = lax.broadcasted_iota(jnp.int32, shape, 1)
    r = jnp.left_shift(1, 2 * g)
    per_class = S // r
    pos = (row % per_class) * r + row // per_class
    inv_freq = jnp.power(jnp.float32(ROPE_THETA), -lax.broadcasted_iota(F32, shape, 2) * 2.0 / ROPE_DIM)
    ang = pos.astype(F32) * inv_freq
    return jnp.concatenate([jnp.cos(ang), jnp.sin(ang)], axis=-1)


def _rope_factors(cs):
    half = ROPE_DIM // 2
    i = lax.broadcasted_iota(jnp.int32, (2 * half, 2 * LANES), 0)
    l = lax.broadcasted_iota(jnp.int32, (2 * half, 2 * LANES), 1)
    hit = ((l < LANES) & (i == l % half)) | ((l >= LANES) & (i - half == l % half))
    spread = jnp.where(hit, 1.0, 0.0).astype(BF16)
    hi = cs.astype(BF16)
    r1 = cs - hi.astype(F32)
    mid = r1.astype(BF16)
    lo = (r1 - mid.astype(F32)).astype(BF16)
    full = (_dot(hi, spread, _NN) + _dot(mid, spread, _NN)) + _dot(lo, spread, _NN)
    cos, sin = full[:, :LANES], full[:, LANES:]
    lane = lax.broadcasted_iota(jnp.int32, cos.shape, 1) % HD
    t_self = jnp.where(lane < ROPE_DIM, cos, 1.0)
    t_up = jnp.where(lane < half, -sin, 0.0)
    t_dn = jnp.where((lane >= half) & (lane < ROPE_DIM), sin, 0.0)
    return t_self, t_up, t_dn


def _residue_pieces(r):
    if r == 1:
        return [(slice(i, i + 512), slice(i, i + 512)) for i in range(0, S, 512)]
    n = S // r
    return [(pl.ds(j, n, stride=r), slice(j * n, (j + 1) * n)) for j in range(r)]


def _rope_apply(x, a, u, d, backward):
    half = ROPE_DIM // 2
    if backward:
        return x * a + pltpu.roll(x * u, half, 1) + pltpu.roll(x * d, LANES - half, 1)
    return x * a + pltpu.roll(x, LANES - half, 1) * u + pltpu.roll(x, half, 1) * d


def _rope_cache_factors(cs_ref, fac, first_step_of_group):
    @pl.when(first_step_of_group)
    def _():
        for lo in range(0, S, 512):
            for j, f in enumerate(_rope_factors(cs_ref[lo:lo + 512, :])):
                fac[j, lo:lo + 512, :] = f


LB_PER_CB = CB // LANES
ROPE_NB = 3 * DIL_G * LB_PER_CB


def _rope_step(s):
    per_group = 3 * LB_PER_CB
    return s // per_group, (s % per_group) // LB_PER_CB, s % LB_PER_CB


def _rope_split(proj_b, tables):
    def body(x_ref, cs_ref, o_ref, fac):
        g, t, hf = _rope_step(pl.program_id(0))
        _rope_cache_factors(cs_ref, fac, (t == 0) & (hf == 0))
        for gs in range(DIL_G):
            @pl.when(g == gs)
            def _(gs=gs):
                for tok, sub in _residue_pieces(DIL_R[gs]):
                    x = x_ref[tok, :]
                    y = _rope_apply(x, fac[0, sub, :], fac[1, sub, :], fac[2, sub, :], False)
                    o_ref[sub, :] = jnp.where(t < 2, y, x).astype(BF16)

    def in_index(s):
        g, t, hf = _rope_step(s)
        return 0, (t * DIL_G + g) * LB_PER_CB + hf

    def out_index(s):
        g, t, hf = _rope_step(s)
        return t * DIL_G + g, 0, hf

    tab = pl.BlockSpec((None, S, 2 * (ROPE_DIM // 2)), lambda s: (_rope_step(s)[0], 0, 0))
    return pl.pallas_call(
        body, name="rope_split", grid=(ROPE_NB,),
        in_specs=[pl.BlockSpec((S, LANES), in_index), tab],
        out_specs=pl.BlockSpec((None, S, LANES), out_index),
        out_shape=jax.ShapeDtypeStruct((3 * DIL_G, S, CB), BF16),
        scratch_shapes=[pltpu.VMEM((3, S, LANES), F32)],
        compiler_params=_cparams(("arbitrary",)),
    )(proj_b, tables)


def _rope_merge_bwd(dq3, dk3, dv3, tables, dproj):
    def body(dq_ref, dk_ref, dv_ref, cs_ref, _, o_ref, tmp, fac):
        s = pl.program_id(0)
        g, t, hf = _rope_step(s)
        _rope_cache_factors(cs_ref, fac, (t == 0) & (hf == 0) & (s < ROPE_NB))
        for gs in range(DIL_G):
            @pl.when((g == gs) & (s < ROPE_NB))
            def _(gs=gs):
                for tok, sub in _residue_pieces(DIL_R[gs]):
                    x = jnp.where(t == 0, dq_ref[sub, :], jnp.where(t == 1, dk_ref[sub, :], dv_ref[sub, :]))
                    y = _rope_apply(x, fac[0, sub, :], fac[1, sub, :], fac[2, sub, :], True)
                    tmp[tok, :] = jnp.where(t < 2, y, x)
                o_ref[...] = tmp[...].astype(BF16)

        @pl.when(s >= ROPE_NB)
        def _():
            o_ref[...] = jnp.zeros_like(o_ref)

    def src_spec(own):
        def index(s):
            g, t, hf = _rope_step(jnp.minimum(s, ROPE_NB - 1))
            return g, 0, jnp.where(t == own, hf, jnp.where(t > own, LB_PER_CB - 1, 0))
        return pl.BlockSpec((None, S, LANES), index)

    def out_index(s):
        g, t, hf = _rope_step(s)
        col = SEG_B // LANES + (t * DIL_G + g) * LB_PER_CB + hf
        return 0, jnp.where(s < ROPE_NB, col, SEG_Z // LANES + s - ROPE_NB)

    tab = pl.BlockSpec((None, S, 2 * (ROPE_DIM // 2)), lambda s: (_rope_step(jnp.minimum(s, ROPE_NB - 1))[0], 0, 0))
    return pl.pallas_call(
        body, name="rope_merge_bwd", grid=(ROPE_NB + LB_PER_CB,),
        in_specs=[src_spec(0), src_spec(1), src_spec(2), tab, ANY],
        out_specs=pl.BlockSpec((S, LANES), out_index),
        out_shape=jax.ShapeDtypeStruct((S, D_PAD), BF16),
        scratch_shapes=[pltpu.VMEM((S, LANES), F32), pltpu.VMEM((3, S, LANES), F32)],
        input_output_aliases={4: 0},
        compiler_params=_cparams(("arbitrary",)),
    )(dq3, dk3, dv3, tables, dproj)


DIL_NB = S // DIL_BLK


_BQK = (((2,), (2,)), ((0,), (0,)))
_BQD = (((2,), (1,)), ((0,), (0,)))
_BKD = (((1,), (1,)), ((0,), (0,)))


def _dil_mask(g):
    shape = (DIL_NB, DIL_BLK, 2 * DIL_BLK)
    blocks_per_seq = jnp.right_shift(DIL_NB, 2 * g)
    has_prev = jnp.bitwise_and(lax.broadcasted_iota(jnp.int32, shape, 0), blocks_per_seq - 1) != 0
    a = lax.broadcasted_iota(jnp.int32, shape, 1)
    kk = lax.broadcasted_iota(jnp.int32, shape, 2)
    diff = DIL_BLK + a - kk
    return (diff >= 0) & (diff <= DIL_BLK) & ((kk >= DIL_BLK) | has_prev)


def _blocks(t):
    return t.reshape(DIL_NB, DIL_BLK, t.shape[-1])


def _with_prev(t):
    prev = jnp.concatenate([jnp.zeros((DIL_BLK, t.shape[-1]), t.dtype), t[:-DIL_BLK]], axis=0)
    return jnp.concatenate([_blocks(prev), _blocks(t)], axis=1)


def _fold_prev(t2):
    n = t2.shape[-1]
    to_prev = t2[:, :DIL_BLK].reshape(S, n)
    own = t2[:, DIL_BLK:].reshape(S, n)
    return own + jnp.concatenate([to_prev[DIL_BLK:], jnp.zeros((DIL_BLK, n), t2.dtype)], axis=0)


def _dil_group_spec(t, width=DIL_OUT):
    return pl.BlockSpec((None, S, width), lambda g: (t * DIL_G + g, 0, 0))


def _dil_fwd(qkv3):
    def body(q_ref, k_ref, v_ref, o_ref, lse_ref):
        ok = _dil_mask(pl.program_id(0))
        lane = lax.broadcasted_iota(jnp.int32, (S, LANES), 1)
        lse = jnp.zeros((S, LANES), F32)
        for h in range(DIL_HG):
            sl = slice(h * HD, (h + 1) * HD)
            s = jnp.where(ok, _dot(_blocks(q_ref[:, sl]), _with_prev(k_ref[:, sl]), _BQK) * SCALE, NEG_INF)
            m = jnp.max(s, axis=-1, keepdims=True)
            p = jnp.exp(s - m)
            l = jnp.sum(p, axis=-1, keepdims=True)
            o_ref[:, sl] = _dot(p / l, _with_prev(v_ref[:, sl]), _BQD).reshape(S, HD)
            lse = jnp.where(lane == h, (m + jnp.log(l)).reshape(S, 1), lse)
        lse_ref[...] = lse

    return pl.pallas_call(
        body, name="dil_fwd", grid=(DIL_G,),
        in_specs=[_dil_group_spec(0), _dil_group_spec(1), _dil_group_spec(2)],
        out_specs=[_dil_group_spec(0), _dil_group_spec(0, LANES)],
        out_shape=[jax.ShapeDtypeStruct((DIL_G, S, DIL_OUT), F32), jax.ShapeDtypeStruct((DIL_G, S, LANES), F32)],
        compiler_params=_cparams(("parallel",)),
    )(qkv3, qkv3, qkv3)


def _dil_bwd(qkv3, lse3, do3, c3):
    def body(q_ref, k_ref, v_ref, lse_ref, do_ref, c_ref, dq_ref, dk_ref, dv_ref):
        ok = _dil_mask(pl.program_id(0))
        lse = lse_ref[...]
        c = c_ref[...]
        for h in range(DIL_HG):
            sl = slice(h * HD, (h + 1) * HD)
            q = _blocks(q_ref[:, sl])
            k2 = _with_prev(k_ref[:, sl])
            do_h = _blocks(do_ref[:, sl])
            s = jnp.where(ok, _dot(q, k2, _BQK) * SCALE, NEG_INF)
            p = jnp.exp(s - _blocks(lse[:, h:h + 1]))
            dp = _dot(do_h, _with_prev(v_ref[:, sl]), _BQK)
            ds = p * (dp - _blocks(c[:, h:h + 1]))
            dsb = (ds * SCALE).astype(BF16)
            dq_ref[:, sl] = _dot(dsb, k2, _BQD).reshape(S, HD)
            dk_ref[:, sl] = _fold_prev(_dot(dsb, q, _BKD))
            dv_ref[:, sl] = _fold_prev(_dot(p, do_h, _BKD))

    return pl.pallas_call(
        body, name="dil_bwd", grid=(DIL_G,),
        in_specs=[_dil_group_spec(0), _dil_group_spec(1), _dil_group_spec(2), _dil_group_spec(0, LANES),
                  _dil_group_spec(0), _dil_group_spec(0, LANES)],
        out_specs=[_dil_group_spec(0)] * 3,
        out_shape=[jax.ShapeDtypeStruct((DIL_G, S, DIL_OUT), F32)] * 3,
        compiler_params=_cparams(("parallel",)),
    )(qkv3, qkv3, qkv3, lse3, do3, c3)


COMB_T = 256


def _dil_combine(o3, lse3):
    heads_per_lb = LANES // HD

    def body(o_ref, lse_ref, ob_ref, al_ref, *scratch):
        o_tok = [scratch[LB_PER_CB * gg:LB_PER_CB * (gg + 1)] for gg in range(DIL_G)]
        lse_tok = scratch[LB_PER_CB * DIL_G:]
        g = pl.program_id(0)
        for gs in range(DIL_G):
            @pl.when(g == gs)
            def _(gs=gs):
                for tok, sub in _residue_pieces(DIL_R[gs]):
                    for hf in range(LB_PER_CB):
                        o_tok[gs][hf][tok, :] = o_ref[sub, hf * LANES:(hf + 1) * LANES]
                    lse_tok[gs][tok, :] = lse_ref[sub, :]

        @pl.when(g == DIL_G - 1)
        def _():
            def chunk(i, carry):
                rows = pl.ds(pl.multiple_of(i * COMB_T, COMB_T), COMB_T)
                lse = [lse_tok[gg][rows, :] for gg in range(DIL_G)]
                m = jnp.maximum(jnp.maximum(lse[0], lse[1]), lse[2])
                e = [jnp.exp(lse[gg] - m) for gg in range(DIL_G)]
                den = (e[0] + e[1]) + e[2]
                al = [e[gg] / den for gg in range(DIL_G)]
                for gg in range(DIL_G):
                    al_ref[gg, rows, :] = al[gg]
                for h in range(DIL_HG):
                    hf, sl = h // heads_per_lb, slice((h % heads_per_lb) * HD, (h % heads_per_lb + 1) * HD)
                    acc = al[0][:, h:h + 1] * o_tok[0][hf][rows, sl]
                    for gg in range(1, DIL_G):
                        acc = acc + al[gg][:, h:h + 1] * o_tok[gg][hf][rows, sl]
                    ob_ref[rows, h * HD:(h + 1) * HD] = acc
                return carry

            lax.fori_loop(0, S // COMB_T, chunk, 0)

    return pl.pallas_call(
        body, name="dil_combine", grid=(DIL_G,),
        in_specs=[pl.BlockSpec((None, S, DIL_OUT), lambda g: (g, 0, 0)),
                  pl.BlockSpec((None, S, LANES), lambda g: (g, 0, 0))],
        out_specs=[pl.BlockSpec((S, DIL_OUT), lambda g: (0, 0)),
                   pl.BlockSpec((DIL_G, S, LANES), lambda g: (0, 0, 0))],
        out_shape=[jax.ShapeDtypeStruct((S, DIL_OUT), F32), jax.ShapeDtypeStruct((DIL_G, S, LANES), F32)],
        scratch_shapes=[pltpu.VMEM((S, LANES), F32)] * (DIL_G * (LB_PER_CB + 1)),
        compiler_params=_cparams(("arbitrary",)),
    )(o3, lse3)


def _dil_combine_bwd(dob, ob, alpha, deps=()):
    heads_per_lb = LANES // HD

    def body(dob_ref, ob_ref, al_ref, *refs):
        do_ref, c_ref = refs[len(deps):]
        g = pl.program_id(0)
        hf = pl.program_id(1)
        for gs in range(DIL_G):
            @pl.when(g == gs)
            def _(gs=gs):
                for tok, sub in _residue_pieces(DIL_R[gs]):
                    dob = dob_ref[tok, :]
                    prod = dob * ob_ref[tok, :]
                    al = al_ref[tok, :]
                    lane = lax.broadcasted_iota(jnp.int32, al.shape, 1)
                    c = jnp.where(hf == 0, 0.0, c_ref[sub, :])
                    for hh in range(heads_per_lb):
                        sl = slice(hh * HD, (hh + 1) * HD)
                        head = hf * heads_per_lb + hh
                        a = jnp.sum(jnp.where(lane == head, al, 0.0), axis=-1, keepdims=True)
                        do_ref[sub, sl] = (a * dob[:, sl]).astype(BF16)
                        c = jnp.where(lane == head, a * jnp.sum(prod[:, sl], axis=-1, keepdims=True), c)
                    c_ref[sub, :] = c

    half = pl.BlockSpec((S, LANES), lambda g, hf: (0, hf))
    return pl.pallas_call(
        body, name="dil_combine_bwd", grid=(DIL_G, LB_PER_CB),
        in_specs=[half, half, pl.BlockSpec((None, S, LANES), lambda g, hf: (g, 0, 0))] + [ANY] * len(deps),
        out_specs=[pl.BlockSpec((None, S, LANES), lambda g, hf: (g, 0, hf)),
                   pl.BlockSpec((None, S, LANES), lambda g, hf: (g, 0, 0))],
        out_shape=[jax.ShapeDtypeStruct((DIL_G, S, DIL_OUT), BF16), jax.ShapeDtypeStruct((DIL_G, S, LANES), F32)],
        compiler_params=_cparams(("parallel", "arbitrary")),
    )(dob, ob, alpha, *deps)


def _local_step(x, tgt, h1, g_attn, g_mlp, g_final, b_pad, w_in_t, hooks):
    tables = _rope_tables()

    first = hooks.first_deps()
    proj_a = _mm(h1, w_in_t, mode="nt", tm=S, tn=512, n=3 * FOX_W, out_dtypes=[BF16], name="proj_a", deps=first)
    proj_b = _mm(h1, w_in_t, mode="nt", tm=S, tn=DIL_W, n=3 * DIL_W, b_off=SEG_B // DIL_W, out_dtypes=[F32], name="proj_b",
                deps=first)
    proj_g = _mm(h1, w_in_t, mode="nt", tm=S, tn=D, n=2 * D, b_off=SEG_G // D, out_dtypes=[F32], name="proj_g",
                deps=first)
    proj_f = _mm(h1, w_in_t, mode="nt", tm=S, tn=LANES, n=LANES, b_off=SEG_F // LANES, out_dtypes=[F32], name="proj_f",
                deps=first)

    q_aug, k_aug = _fox_prepare(proj_a, proj_f, b_pad)
    oa, lse_a = _fox_fwd(q_aug, k_aug, proj_a)

    qkv3 = _rope_split(proj_b, tables)
    o3, lse3 = _dil_fwd(qkv3)
    ob, alpha = _dil_combine(o3, lse3)

    w_a, w_b, w_out = hooks.mixer_weights(ob)
    ya, yb, mixed, x2, h2 = _mixer_tail(oa, ob, w_a, w_b, w_out, proj_g, x, g_mlp)
    w_up_sh, w_down = hooks.mlp_weights(h2)

    def up_epilogue(acc):
        r = jnp.maximum(acc, 0.0)
        return acc, r * r

    u, act = _mm(h2, w_up_sh, mode="nn", tm=S, tn=DFF // N_DEV, b_sharded=True, out_dtypes=[F32, BF16],
                 name="mlp_up", epilogue=up_epilogue)

    def loss_epilogue(acc, rows, vecs):
        dx3, dg, loss = _final_loss_rows(rows[0] + acc, vecs[0], rows[1])
        return (dx3,), (dg, loss)

    dx3, dg_final, loss = _mm_rows(act, w_down, tm=ROWS_TM, name="mlp_down_loss", epilogue=loss_epilogue,
                                   rows=(x2, tgt), vecs=(g_final,), row_out=(F32,), vec_out=(D, LANES))

    def rms_bwd_epilogue(acc, rows, vecs):
        dx, dg = _rms_bwd_rows(acc, rows[0], rows[1], vecs[0])
        return (dx,), (dg,)

    du = _mm(dx3, w_down, mode="nt", tm=S, tn=512, out_dtypes=[BF16], name="mlp_down_bwd",
             epilogue=lambda acc, u_t: (acc * (2.0 * jnp.maximum(u_t, 0.0)),), extras=(u,))
    dw_down = _mm(act, dx3, mode="tn", tm=512, tn=D, out_dtypes=[F32], name="dw_down")
    dw_up_sh = _mm(h2, du, mode="tn", tm=D, tn=DFF // N_DEV, out_sharded=True, out_dtypes=[F32], name="dw_up")
    dx2, dg_mlp = _mm_rows(du, w_up_sh, b_sharded=True, tm=ROWS_TM, name="mlp_up_bwd", epilogue=rms_bwd_epilogue,
                           rows=(x2, dx3), vecs=(g_mlp,), row_out=(F32,), vec_out=(D,),
                           deps=hooks.mlp_grads(dw_up_sh, dw_down))

    dw_out = _mm(mixed, dx2, mode="tn", tm=D, tn=D, out_dtypes=[F32], name="dw_out")

    dyab, dproj = _out_proj_gate_bwd(dx2, w_out, proj_g, ya, yb)
    doa = _mm(dyab, w_a, mode="nt", tm=S, tn=FOX_W, k=D, a_off=0, out_dtypes=[BF16], name="branch_a_bwd")
    dw_a = _mm(oa, dyab, mode="tn", tm=FOX_W, tn=D, n=D, b_off=0, out_dtypes=[F32], name="dw_branch_a")
    dob = _mm(dyab, w_b, mode="nt", tm=S, tn=CB, k=D, a_off=1, out_dtypes=[F32], name="branch_b_bwd")
    dw_b = _mm(ob, dyab, mode="tn", tm=DIL_OUT, tn=D, n=D, b_off=1, out_dtypes=[F32], name="dw_branch_b")

    dproj, dft, dfs = _fox_bwd(q_aug, k_aug, proj_a, lse_a, doa, dproj, deps=hooks.mixer_grads(dw_a, dw_b, dw_out))
    dproj, db = _fox_dscan(dft, dfs, proj_f, b_pad, dproj)

    do3, c3 = _dil_combine_bwd(dob, ob, alpha, deps=hooks.mid_backward(db))
    dq3, dk3, dv3 = _dil_bwd(qkv3, lse3, do3, c3)
    dproj = _rope_merge_bwd(dq3, dk3, dv3, tables, dproj)

    dw_in_t = _mm(dproj, h1, mode="tn", tm=DW_IN_TM, tn=D, out_dtypes=[F32], name="dw_in")
    dx, dg_attn = _mm_rows(dproj, w_in_t, tm=ROWS_TM // 2, name="proj_bwd", epilogue=rms_bwd_epilogue,
                           rows=(x, dx2), vecs=(g_attn,), row_out=(F32,), vec_out=(D,),
                           deps=hooks.w_in_grad(dw_in_t))

    return loss, dx, (dg_attn, db, dg_mlp, dg_final)


SHARD_SHAPES = ((D_IN // N_DEV, D), (FOX_W, D // N_DEV), (DIL_OUT, D // N_DEV), (D // N_DEV, D),
                (D, DFF // N_DEV), (DFF // N_DEV, D))
W_NAMES = ("w_in", "w_a", "w_b", "w_out", "w_up", "w_down")
AG_COPIES = 7
HBM = pl.BlockSpec(memory_space=pltpu.HBM)
SEM = pl.BlockSpec(memory_space=pltpu.SEMAPHORE)


def _place():
    return lax.axis_index("x"), lax.axis_index("y"), lax.axis_index("c")


W_SLAB = 752
W_WIN = 272
PAD_BLK = 256


def _pad_runs():
    runs = sorted((pad, pad + b - a, p, a) for p in range(N_DEV) for a, b, pad in _shard_pieces(p))
    blocks = []
    for k in range(D_PAD // PAD_BLK):
        lo, hi = k * PAD_BLK, (k + 1) * PAD_BLK
        blocks.append([(max(lo, r0) - lo, min(hi, r1) - lo, p, a + max(lo, r0) - r0)
                       for r0, r1, p, a in runs if max(lo, r0) < min(hi, r1)])
    return blocks


def _gather_w_in(shard, x_in, g_attn, late_shards):
    blocks = _pad_runs()

    norm_t = 512
    n_chunks = S // norm_t

    n_late = len(late_shards)
    late_shapes = [t.shape for t in late_shards]

    def body(x_ref, xin_ref, g_ref, *refs):
        late_in, (out_ref, h1_ref), late_out = refs[:n_late], refs[n_late:n_late + 2], refs[n_late + 2:2 * n_late + 2]
        scratch = refs[2 * n_late + 2:]
        stage, land, xbuf = scratch[:3]
        late_stage, late_cast = scratch[3:3 + n_late], scratch[3 + n_late:3 + 2 * n_late]
        send_sems, recv_sems, load_sem, x_sems, late_sems, store_sems = scratch[3 + 2 * n_late:]
        x, y, c = _place()
        me, sibling = (x, y, c), (x, y, 1 - c)
        chips = [(1 - x, y), (x, 1 - y), (1 - x, 1 - y)]

        def slot(px, py, pc):
            return land.at[4 * px + 2 * py + pc]

        def copy(k, block, to):
            return pltpu.make_async_remote_copy(
                src_ref=slot(*block), dst_ref=slot(*block), send_sem=send_sems.at[k], recv_sem=recv_sems.at[k],
                device_id=to, device_id_type=MESH)

        load = pltpu.make_async_copy(x_ref, stage, load_sem)
        load.start()
        load.wait()
        land[4 * x + 2 * y + c] = stage[...].astype(BF16)
        sent = [copy(0, me, sibling)] + [copy(1 + j, me, (*chip, c)) for j, chip in enumerate(chips)]
        for cp in sent:
            cp.start()

        def x_load(i):
            return pltpu.make_async_copy(xin_ref.at[pl.ds(i * norm_t, norm_t)], xbuf.at[i % 2], x_sems.at[i % 2])

        late_loads = [pltpu.make_async_copy(late_in[i], late_stage[i], late_sems.at[i]) for i in range(n_late)]
        for ld in late_loads:
            ld.start()
        x_load(0).start()
        for i in range(n_chunks):
            if i + 1 < n_chunks:
                x_load(i + 1).start()
            x_load(i).wait()
            h1_ref[i * norm_t:(i + 1) * norm_t, :] = _rms_rows(xbuf[i % 2], g_ref[...]).astype(BF16)
        stores = []
        for i in range(n_late):
            late_loads[i].wait()
            late_cast[i][...] = late_stage[i][...].astype(BF16)
            stores.append(pltpu.make_async_copy(late_cast[i], late_out[i].at[4 * x + 2 * y + c], store_sems.at[i]))
            stores[-1].start()

        for j, chip in enumerate(chips):
            copy(1 + j, (*chip, c), me).wait_recv()
            sent.append(copy(4 + j, (*chip, c), sibling))
            sent[-1].start()
        copy(0, sibling, me).wait_recv()
        for j, chip in enumerate(chips):
            copy(4 + j, (*chip, 1 - c), me).wait_recv()
        for cp in sent:
            cp.wait_send()
        for st in stores:
            st.wait()

        row = lax.broadcasted_iota(jnp.int32, (PAD_BLK, D), 0)
        for k, runs in enumerate(blocks):
            out = jnp.zeros((PAD_BLK, D), F32)
            for o_lo, o_hi, p, a in runs:
                start = min(a // 16 * 16, W_SLAB - W_WIN)
                win = land[p, start:start + W_WIN, :].astype(F32)
                moved = pltpu.roll(win, (o_lo - (a - start)) % W_WIN, 0)[:PAD_BLK]
                out = jnp.where((row >= o_lo) & (row < o_hi), moved, out)
            out_ref[k * PAD_BLK:(k + 1) * PAD_BLK, :] = out.astype(BF16)

    return pl.pallas_call(
        body, name="all_gather_w_in",
        out_shape=[jax.ShapeDtypeStruct((D_PAD, D), BF16), jax.ShapeDtypeStruct((S, D), BF16)]
                  + [jax.ShapeDtypeStruct((N_DEV,) + sh, BF16) for sh in late_shapes],
        in_specs=[ANY, ANY, pl.BlockSpec(memory_space=pltpu.VMEM)] + [ANY] * n_late,
        out_specs=[pl.BlockSpec(memory_space=pltpu.VMEM)] * 2 + [ANY] * n_late,
        scratch_shapes=[pltpu.VMEM((W_SLAB, D), F32), pltpu.VMEM((N_DEV, W_SLAB, D), BF16),
                        pltpu.VMEM((2, norm_t, D), F32)]
                       + [pltpu.VMEM(sh, F32) for sh in late_shapes] + [pltpu.VMEM(sh, BF16) for sh in late_shapes]
                       + [pltpu.SemaphoreType.DMA((AG_COPIES,)), pltpu.SemaphoreType.DMA((AG_COPIES,)),
                          pltpu.SemaphoreType.DMA(()), pltpu.SemaphoreType.DMA((2,)),
                          pltpu.SemaphoreType.DMA((n_late,)), pltpu.SemaphoreType.DMA((n_late,))],
        compiler_params=_cparams(None),
    )(shard, x_in, g_attn, *late_shards)


PEER_COPIES = N_DEV - 1
SIBLING_COPIES = 4
CHIP_COPIES = 3


def _peer_copies(_, land_refs, send_sems, recv_sems):
    x, y, c = _place()
    mine = 4 * x + 2 * y + c
    copies = []
    for i, ref in enumerate(land_refs):
        for k in range(1, N_DEV):
            peer = (x ^ (k >> 2), y ^ ((k >> 1) & 1), c ^ (k & 1))
            n = i * PEER_COPIES + k - 1
            copies.append(pltpu.make_async_remote_copy(
                src_ref=ref.at[mine], dst_ref=ref.at[mine], send_sem=send_sems.at[n], recv_sem=recv_sems.at[n],
                device_id=peer, device_id_type=MESH))
    return copies


def _sibling_copies(g_refs, r_refs, send_sems, recv_sems):
    x, y, c = _place()
    return [pltpu.make_async_remote_copy(
        src_ref=g_refs[i].at[2 * k + (1 - c)], dst_ref=r_refs[i].at[k],
        send_sem=send_sems.at[SIBLING_COPIES * i + k], recv_sem=recv_sems.at[SIBLING_COPIES * i + k],
        device_id=(x, y, 1 - c), device_id_type=MESH) for i in range(len(g_refs)) for k in range(SIBLING_COPIES)]


def _chip_copies(p_refs, r_refs, send_sems, recv_sems):
    x, y, c = _place()
    chips = [(1 - x, y), (x, 1 - y), (1 - x, 1 - y)]
    return [pltpu.make_async_remote_copy(
        src_ref=p_refs[i].at[2 * cx + cy], dst_ref=r_refs[i].at[j],
        send_sem=send_sems.at[CHIP_COPIES * i + j], recv_sem=recv_sems.at[CHIP_COPIES * i + j],
        device_id=(cx, cy, c), device_id_type=MESH) for i in range(len(p_refs)) for j, (cx, cy) in enumerate(chips)]


def _in_hbm(a):
    return pltpu.with_memory_space_constraint(a, pltpu.HBM)


def _exchange_start(name, copies_fn, n_copies, srcs, lands, after=()):
    n_s, n_l, n_a = len(srcs), len(lands), len(after)

    def body(*refs):
        outs = refs[n_s + n_l + n_a:]
        for cp in copies_fn(refs[:n_s], refs[n_s:n_s + n_l], outs[0], outs[1]):
            cp.start()
        outs[-1][...] = jnp.zeros_like(outs[-1])

    res = pl.pallas_call(
        body, name=name,
        out_shape=[pltpu.SemaphoreType.DMA((n_copies,)), pltpu.SemaphoreType.DMA((n_copies,))]
                  + [pltpu.HBM(t.shape, t.dtype) for t in (*srcs, *lands)] + [jax.ShapeDtypeStruct((8, LANES), F32)],
        in_specs=[HBM] * (n_s + n_l) + [ANY] * n_a,
        out_specs=[SEM, SEM] + [HBM] * (n_s + n_l) + [pl.BlockSpec(memory_space=pltpu.VMEM)],
        input_output_aliases={i: 2 + i for i in range(n_s + n_l)},
        compiler_params=pltpu.CompilerParams(has_side_effects=pltpu.SideEffectType.DATAFLOW_SIDE_EFFECTING),
    )(*[_in_hbm(t) for t in (*srcs, *lands)], *after)
    return res[0], res[1], list(res[2:2 + n_s]), list(res[2 + n_s:2 + n_s + n_l]), res[-1]


def _exchange_wait(name, copies_fn, started, after):
    send_sems, recv_sems, srcs, lands, _ = started
    n_s, n_l = len(srcs), len(lands)

    def body(*refs):
        for cp in copies_fn(refs[:n_s], refs[n_s:n_s + n_l], refs[n_s + n_l], refs[n_s + n_l + 1]):
            cp.wait_send()
            cp.wait_recv()

    res = pl.pallas_call(
        body, name=name,
        out_shape=[pltpu.HBM(t.shape, t.dtype) for t in (*srcs, *lands)],
        in_specs=[HBM] * (n_s + n_l) + [SEM, SEM, ANY],
        out_specs=[HBM] * (n_s + n_l),
        input_output_aliases={i: i for i in range(n_s + n_l)},
        compiler_params=pltpu.CompilerParams(has_side_effects=pltpu.SideEffectType.DATAFLOW_SIDE_EFFECTING),
    )(*srcs, *lands, send_sems, recv_sems, after)
    return list(res[:n_s]), list(res[n_s:])


def _shard_block(shape):
    rows, cols = shape
    if rows * cols <= 256 * 1024:
        return rows, cols
    if rows % 256:
        return rows, 256
    return 256, cols


def _chip_partial(ids, g_stack, recv1, name):
    shape = g_stack.shape[1:]
    br, bc = _shard_block(shape)

    def body(ids_ref, g_ref, r_ref, pb_ref, own_ref):
        s = g_ref[...] + r_ref[...]
        pb_ref[...] = s.astype(BF16)

        @pl.when(pl.program_id(2) == ids_ref[1])
        def _():
            own_ref[...] = s

    grid_spec = pltpu.PrefetchScalarGridSpec(
        num_scalar_prefetch=1, grid=(shape[0] // br, shape[1] // bc, 4),
        in_specs=[pl.BlockSpec((None, br, bc), lambda r, q, k, ids: (2 * k + ids[0], r, q)),
                  pl.BlockSpec((None, br, bc), lambda r, q, k, ids: (k, r, q))],
        out_specs=[pl.BlockSpec((None, br, bc), lambda r, q, k, ids: (k, r, q)),
                   pl.BlockSpec((br, bc), lambda r, q, k, ids: (r, q))])
    return pl.pallas_call(
        body, name=name, grid_spec=grid_spec,
        out_shape=[jax.ShapeDtypeStruct((4,) + shape, BF16), jax.ShapeDtypeStruct(shape, F32)],
        compiler_params=_cparams(("parallel", "parallel", "arbitrary")),
    )(ids, g_stack, recv1)


def _adamw(w, g, m, v):
    m = ADAM_B1 * m + (1.0 - ADAM_B1) * g
    v = ADAM_B2 * v + (1.0 - ADAM_B2) * (g * g)
    m_hat = m / (1.0 - ADAM_B1 ** ADAM_STEP)
    v_hat = v / (1.0 - ADAM_B2 ** ADAM_STEP)
    delta = -ADAM_LR * (m_hat / (jnp.sqrt(v_hat) + ADAM_EPS) + ADAM_WD * w)
    return delta, m, v


def _reduce_adamw(own, recv2, w, m, v, name, deps=()):
    shape = own.shape
    br, bc = _shard_block(shape)

    def body(own_ref, r_ref, w_ref, m_ref, v_ref, *refs):
        g_ref, d_ref, nm_ref, nv_ref = refs[len(deps):]
        g = own_ref[...]
        for j in range(3):
            g = g + r_ref[j].astype(F32)
        delta, nm, nv = _adamw(w_ref[...], g, m_ref[...], v_ref[...])
        g_ref[...] = g
        d_ref[...] = delta
        nm_ref[...] = nm
        nv_ref[...] = nv

    blk = pl.BlockSpec((br, bc), lambda r, q: (r, q))
    return pl.pallas_call(
        body, name=name, grid=(shape[0] // br, shape[1] // bc),
        in_specs=[blk, pl.BlockSpec((3, br, bc), lambda r, q: (0, r, q)), blk, blk, blk] + [ANY] * len(deps),
        out_specs=[blk] * 4, out_shape=[jax.ShapeDtypeStruct(shape, F32)] * 4,
        compiler_params=_cparams(("parallel", "parallel")),
    )(own, recv2, w, m, v, *deps)


def _small_allreduce_adamw(gvec, w, m, v):
    def body(g_ref, w_ref, m_ref, v_ref, go_ref, d_ref, nm_ref, nv_ref, buf, send_sems, recv_sems):
        x, y, c = _place()
        my_slot = 4 * x + 2 * y + c
        buf[my_slot] = g_ref[...]
        copies = []
        for k in range(1, N_DEV):
            px, py, pc = x ^ (k >> 2), y ^ ((k >> 1) & 1), c ^ (k & 1)
            copies.append(pltpu.make_async_remote_copy(
                src_ref=g_ref, dst_ref=buf.at[my_slot], send_sem=send_sems.at[k - 1], recv_sem=recv_sems.at[k - 1],
                device_id=(px, py, pc), device_id_type=MESH))
        for cp in copies:
            cp.start()
        for k in range(1, N_DEV):
            px, py, pc = x ^ (k >> 2), y ^ ((k >> 1) & 1), c ^ (k & 1)
            pltpu.make_async_remote_copy(
                src_ref=g_ref, dst_ref=buf.at[4 * px + 2 * py + pc], send_sem=send_sems.at[k - 1],
                recv_sem=recv_sems.at[k - 1], device_id=(px, py, pc), device_id_type=MESH).wait_recv()
        for cp in copies:
            cp.wait_send()
        g = buf[0]
        for s in range(1, N_DEV):
            g = g + buf[s]
        delta, nm, nv = _adamw(w_ref[...], g, m_ref[...], v_ref[...])
        go_ref[...] = g
        d_ref[...] = delta
        nm_ref[...] = nm
        nv_ref[...] = nv

    vm = pl.BlockSpec(memory_space=pltpu.VMEM)
    return pl.pallas_call(
        body, name="small_allreduce_adamw",
        in_specs=[vm] * 4, out_specs=[vm] * 4,
        out_shape=[jax.ShapeDtypeStruct((SMALL_R, LANES), F32)] * 4,
        scratch_shapes=[pltpu.VMEM((N_DEV, SMALL_R, LANES), F32),
                        pltpu.SemaphoreType.DMA((N_DEV - 1,)), pltpu.SemaphoreType.DMA((N_DEV - 1,))],
    )(gvec, w, m, v)


def _cols_to_whole(stack):
    rows = stack.shape[1]
    return stack.transpose(1, 0, 2).reshape(rows, -1)


def _whole_to_cols(t):
    rows = t.shape[0]
    return t.reshape(rows, N_DEV, -1).transpose(1, 0, 2)


W_IN_SEGMENTS = ((0, 3 * FOX_W, SEG_A), (3 * FOX_W, 3 * FOX_W + FOX_H, SEG_F),
                 (3 * FOX_W + FOX_H, 3 * FOX_W + FOX_H + 3 * DIL_W, SEG_B), (3 * FOX_W + FOX_H + 3 * DIL_W, D_IN, SEG_G))


def _shard_pieces(p):
    rows = D_IN // N_DEV
    lo, hi = p * rows, (p + 1) * rows
    return [(max(lo, a) - lo, min(hi, b) - lo, pad + max(lo, a) - a)
            for a, b, pad in W_IN_SEGMENTS if max(lo, a) < min(hi, b)]


def _unpad_dw_in_t(dwp):
    return jnp.stack([jnp.concatenate([dwp[pad:pad + b - a] for a, b, pad in _shard_pieces(p)], axis=0)
                      for p in range(N_DEV)])


LOSS_ROW = 25


def _pack_small(g_attn, b, g_mlp, g_final, loss_row=None):
    tail = jnp.pad(b, ((0, 7), (0, LANES - b.shape[1])))
    if loss_row is not None:
        tail = tail + jnp.pad(loss_row, ((LOSS_ROW - 24, 31 - LOSS_ROW), (0, 0)))
    return jnp.concatenate([g_attn.reshape(8, LANES), g_mlp.reshape(8, LANES), g_final.reshape(8, LANES), tail], axis=0)


def _unpack_small(p):
    return (p[0:8].reshape(1, D), p[24:25, :FOX_H], p[8:16].reshape(1, D), p[16:24].reshape(D))


class _StepComm:
    def __init__(self, ids, w_sh, m_sh, v_sh, lands, after_w_in):
        self.ids = ids
        self.w_sh, self.m_sh, self.v_sh = w_sh, m_sh, v_sh
        self.updates = None
        self.gather_mixer = _exchange_start("gather_mixer_start", _peer_copies, PEER_COPIES * 3, [], lands[:3],
                                            after=(after_w_in,))
        self.gather_mlp = _exchange_start("gather_mlp_start", _peer_copies, PEER_COPIES * 2, [], lands[3:],
                                          after=(self.gather_mixer[-1],))
        self.sibling = {}
        self.chips = {}
        self.own = {}
        self.names = {}

    def _reduce_start(self, group, names, grads, after=()):
        lands = [lax.empty((SIBLING_COPIES,) + t.shape[1:], F32) for t in grads]
        self.sibling[group] = _exchange_start("grad_%s_sibling_start" % group, _sibling_copies,
                                              SIBLING_COPIES * len(grads), grads, lands, after=after)
        self.names[group] = names
        return self.sibling[group][-1]

    def _reduce_mid(self, group, after):
        grads, recv1 = _exchange_wait("grad_%s_sibling_wait" % group, _sibling_copies, self.sibling[group], after)
        parts = [_chip_partial(self.ids, g, r, "grad_partial_" + n) for g, r, n in zip(grads, recv1, self.names[group])]
        self.own[group] = [p[1] for p in parts]
        srcs = [p[0] for p in parts]
        lands = [lax.empty((CHIP_COPIES,) + t.shape[1:], BF16) for t in srcs]
        self.chips[group] = _exchange_start("grad_%s_chips_start" % group, _chip_copies, CHIP_COPIES * len(srcs),
                                            srcs, lands)
        return self.chips[group][-1]

    def reduced(self, group, after):
        _, recv2 = _exchange_wait("grad_%s_chips_wait" % group, _chip_copies, self.chips[group], after)
        return list(zip(self.own[group], recv2))

    def first_deps(self):
        return [self.gather_mlp[-1]]

    def mixer_weights(self, after):
        _, (g_a, g_b, g_out) = _exchange_wait("gather_mixer_wait", _peer_copies, self.gather_mixer, after)
        return _cols_to_whole(g_a), _cols_to_whole(g_b), g_out.reshape(D, D)

    def mlp_weights(self, after):
        _, (g_up, g_down) = _exchange_wait("gather_mlp_wait", _peer_copies, self.gather_mlp, after)
        return g_up, g_down.reshape(DFF, D)

    def mlp_grads(self, dw_up_sh, dw_down):
        return [self._reduce_start("mlp", W_NAMES[4:], [dw_up_sh, dw_down.reshape((N_DEV,) + SHARD_SHAPES[5])])]

    def mixer_grads(self, dw_a, dw_b, dw_out):
        token = self._reduce_mid("mlp", dw_b)
        grads = [_whole_to_cols(dw_a), _whole_to_cols(dw_b), dw_out.reshape((N_DEV,) + SHARD_SHAPES[3])]
        return [self._reduce_start("mixer", W_NAMES[1:4], grads, after=(token,))]

    def mid_backward(self, after):
        return [self._reduce_mid("mixer", after)]

    def w_in_grad(self, dw_in_t):
        grads = [_unpad_dw_in_t(dw_in_t)]
        token = self._reduce_start("w_in", W_NAMES[:1], grads)
        reduced = self.reduced("mixer", token) + self.reduced("mlp", token)
        self.updates, last = [None] * len(reduced), token
        for i in (3, 4, 0, 1, 2):
            if i == 0:
                last = self._reduce_mid("w_in", last)
            self.updates[i] = _reduce_adamw(*reduced[i], self.w_sh[1 + i], self.m_sh[1 + i], self.v_sh[1 + i],
                                            "adamw_" + W_NAMES[1 + i], deps=[last])
            last = self.updates[i][0]
        return [last]

    def w_in_update(self, after):
        (reduced,) = self.reduced("w_in", after)
        return _reduce_adamw(*reduced, self.w_sh[0], self.m_sh[0], self.v_sh[0], "adamw_" + W_NAMES[0])


def kernel(x, norm_attn_g, w_in, b_forget, w_branch_a, w_branch_b, w_out, norm_mlp_g, w_up, w_down, norm_final_g, loss_target, m_norm_attn_g, m_w_in, m_b_forget, m_w_branch_a, m_w_branch_b, m_w_out, m_norm_mlp_g, m_w_up, m_w_down, m_norm_final_g, v_norm_attn_g, v_w_in, v_b_forget, v_w_branch_a, v_w_branch_b, v_w_out, v_norm_mlp_g, v_w_up, v_w_down, v_norm_final_g):
    cx, cy, cc = _place()
    ids = jnp.stack([cc, 2 * cx + cy]).astype(jnp.int32)

    w_sh = [w_in[0].T] + [t[0] for t in (w_branch_a, w_branch_b, w_out, w_up, w_down)]
    m_sh = [m_w_in[0].T] + [t[0] for t in (m_w_branch_a, m_w_branch_b, m_w_out, m_w_up, m_w_down)]
    v_sh = [v_w_in[0].T] + [t[0] for t in (v_w_branch_a, v_w_branch_b, v_w_out, v_w_up, v_w_down)]

    w_in_t, h1, *lands = _gather_w_in(jnp.pad(w_sh[0], ((0, W_SLAB - w_sh[0].shape[0]), (0, 0))), x[0], norm_attn_g,
                                      w_sh[1:])
    comm = _StepComm(ids, w_sh, m_sh, v_sh, lands, w_in_t)
    b_pad = jnp.pad(b_forget, ((0, 0), (0, LANES - FOX_H)))

    loss_row, dx, dsmall = _local_step(
        x[0], loss_target[0], h1, norm_attn_g, norm_mlp_g, norm_final_g.reshape(1, D), b_pad, w_in_t, comm)

    dg_attn, db, dg_mlp, dg_final = dsmall
    small = _small_allreduce_adamw(
        _pack_small(dg_attn, db, dg_mlp, dg_final, loss_row),
        _pack_small(norm_attn_g, b_forget, norm_mlp_g, norm_final_g.reshape(1, D)),
        _pack_small(m_norm_attn_g, m_b_forget, m_norm_mlp_g, m_norm_final_g.reshape(1, D)),
        _pack_small(v_norm_attn_g, v_b_forget, v_norm_mlp_g, v_norm_final_g.reshape(1, D)))
    big = [comm.w_in_update(small[0])] + comm.updates

    outs = [small[0][LOSS_ROW, 0], dx[None]]
    for q in range(4):
        s_attn, s_b, s_mlp, s_final = _unpack_small(small[q])
        b_in, b_a, b_b, b_out, b_up, b_down = [t[q][None] for t in big]
        b_in = jnp.swapaxes(b_in, 1, 2)
        outs += [s_attn, b_in, s_b, b_a, b_b, b_out, s_mlp, b_up, b_down, s_final]
    return tuple(outs)
```

```python
import functools

import jax
import jax.numpy as jnp
from jax import lax
from jax.experimental import pallas as pl
from jax.experimental.pallas import tpu as pltpu

F32 = jnp.float32
BF16 = jnp.bfloat16
MESH = pl.DeviceIdType.MESH

S = 2048
D = 1024
HD = 64
FOX_H = 8
FOX_W = FOX_H * HD
DIL_HG = 4
DIL_G = 3
DIL_W = DIL_G * DIL_HG * HD
DIL_OUT = DIL_HG * HD
DIL_BLK = 128
DIL_R = (1, 4, 16)
DFF = 4 * D
D_IN = 3 * FOX_W + FOX_H + 3 * DIL_W + 2 * D
EPS = 1e-6
NEG_INF = -1e30
SCALE = HD ** -0.5
ROPE_THETA = 500000.0
ROPE_DIM = HD // 4
N_DEV = 8

ADAM_LR = 0.001
ADAM_B1 = 0.9
ADAM_B2 = 0.999
ADAM_EPS = 1e-08
ADAM_WD = 0.01
ADAM_STEP = 10

LANES = 128
CB = 256
SEG_A = 0
SEG_B = 3 * FOX_W
SEG_Z = SEG_B + 3 * DIL_W
SEG_G = SEG_Z + CB
SEG_F = SEG_G + 2 * D
D_PAD = SEG_F + CB
DW_IN_TM = D_PAD // 10
SMALL_R = 32

VMEM_MB = 56


def _cparams(dims=None, vmem_mb=VMEM_MB, **kw):
    return pltpu.CompilerParams(dimension_semantics=dims, vmem_limit_bytes=vmem_mb << 20, **kw)


ANY = pl.BlockSpec(memory_space=pl.ANY)

_NN = (((1,), (0,)), ((), ()))
_NT = (((1,), (1,)), ((), ()))
_TN = (((0,), (0,)), ((), ()))


def _dot(a, b, dims):
    return lax.dot_general(a.astype(BF16), b.astype(BF16), dims, preferred_element_type=F32)


def _mm(a, b, *, mode, tm, tn, out_dtypes, name, n=None, k=None, a_off=0, b_off=0,
        b_sharded=False, out_sharded=False, epilogue=None, extras=(), deps=()):
    n_sh = b.shape[-1] if b_sharded else None
    if mode == "nn":
        m = a.shape[0]
        k = k or a.shape[1]
        a_spec = pl.BlockSpec((tm, k), lambda i, j: (i, a_off))
        if b_sharded:
            assert tn == n_sh
            n = N_DEV * n_sh
            b_spec = pl.BlockSpec((None, k, tn), lambda i, j: (j, 0, 0))
        else:
            n = n or b.shape[1]
            b_spec = pl.BlockSpec((k, tn), lambda i, j: (0, j + b_off))
        dims = _NN
    elif mode == "nt":
        m = a.shape[0]
        k = k or a.shape[1]
        a_spec = pl.BlockSpec((tm, k), lambda i, j: (i, a_off))
        if b_sharded:
            n = b.shape[1]
            b_spec = pl.BlockSpec((N_DEV, tn, n_sh), lambda i, j: (0, j, 0))
        else:
            n = n or b.shape[0]
            b_spec = pl.BlockSpec((tn, k), lambda i, j: (j + b_off, 0))
        dims = _NT
    else:
        k, m = a.shape
        n = n or b.shape[1]
        a_spec = pl.BlockSpec((k, tm), lambda i, j: (0, i))
        b_spec = pl.BlockSpec((k, tn), lambda i, j: (0, j + b_off))
        dims = _TN
    assert m % tm == 0 and n % tn == 0, (name, m, n, tm, tn)
    n_extra = len(extras)
    tile = pl.BlockSpec((tm, tn), lambda i, j: (i, j))
    if out_sharded:
        assert mode == "tn" and n // tn == N_DEV
        out_spec = pl.BlockSpec((None, tm, tn), lambda i, j: (j, i, 0))
        out_shape = (N_DEV, m, tn)
    else:
        out_spec, out_shape = tile, (m, n)

    def body(a_ref, b_ref, *refs):
        if mode == "nt" and b_sharded:
            acc = _dot(a_ref[:, 0:n_sh], b_ref[0], dims)
            for p in range(1, N_DEV):
                acc = acc + _dot(a_ref[:, p * n_sh:(p + 1) * n_sh], b_ref[p], dims)
        else:
            acc = _dot(a_ref[...], b_ref[...], dims)
        ex = [r[...] for r in refs[:n_extra]]
        outs = epilogue(acc, *ex) if epilogue is not None else (acc,)
        for o_ref, o in zip(refs[n_extra + len(deps):], outs):
            o_ref[...] = o.astype(o_ref.dtype)

    res = pl.pallas_call(
        body, name=name, grid=(m // tm, n // tn),
        in_specs=[a_spec, b_spec] + [tile] * n_extra + [ANY] * len(deps),
        out_specs=[out_spec] * len(out_dtypes),
        out_shape=[jax.ShapeDtypeStruct(out_shape, dt) for dt in out_dtypes],
        compiler_params=_cparams(("parallel", "parallel")),
    )(a, b, *extras, *deps)
    return res if len(out_dtypes) > 1 else res[0]


def _mm_rows(a, b, *, tm, name, epilogue, rows=(), vecs=(), row_out=(), vec_out=(), b_sharded=False, deps=()):
    m, k = a.shape
    n_rows, n_vecs, n_deps = len(rows), len(vecs), len(deps)
    n_sh = b.shape[-1] if b_sharded else None

    def body(a_ref, b_ref, *refs):
        row_refs, vec_refs = refs[:n_rows], refs[n_rows:n_rows + n_vecs]
        outs = refs[n_rows + n_vecs + n_deps:]
        if b_sharded:
            acc = _dot(a_ref[:, 0:n_sh], b_ref[0], _NT)
            for p in range(1, N_DEV):
                acc = acc + _dot(a_ref[:, p * n_sh:(p + 1) * n_sh], b_ref[p], _NT)
        else:
            acc = _dot(a_ref[...], b_ref[...], _NN)
        row_vals, vec_incs = epilogue(acc, [r[...] for r in row_refs], [v[...] for v in vec_refs])
        for o_ref, val in zip(outs[:len(row_out)], row_vals):
            o_ref[...] = val.astype(o_ref.dtype)

        @pl.when(pl.program_id(0) == 0)
        def _():
            for o_ref in outs[len(row_out):]:
                o_ref[...] = jnp.zeros_like(o_ref)

        for o_ref, inc in zip(outs[len(row_out):], vec_incs):
            o_ref[...] += inc

    tile = pl.BlockSpec((tm, D), lambda i: (i, 0))
    b_spec = pl.BlockSpec(b.shape, lambda i: (0,) * b.ndim)
    return pl.pallas_call(
        body, name=name, grid=(m // tm,),
        in_specs=[pl.BlockSpec((tm, k), lambda i: (i, 0)), b_spec] + [tile] * n_rows
                 + [pl.BlockSpec(v.shape, lambda i: (0, 0)) for v in vecs] + [ANY] * n_deps,
        out_specs=[tile] * len(row_out) + [pl.BlockSpec((1, w), lambda i: (0, 0)) for w in vec_out],
        out_shape=[jax.ShapeDtypeStruct((m, D), dt) for dt in row_out]
                  + [jax.ShapeDtypeStruct((1, w), F32) for w in vec_out],
        compiler_params=_cparams(("arbitrary",)),
    )(a, b, *rows, *vecs, *deps)


ROW_T = 256
ROWS_TM = 512


def _rms_rows(x, g):
    r = lax.rsqrt(jnp.mean(x * x, axis=-1, keepdims=True) + EPS)
    return (x * r) * g


def _rms_bwd_rows(dh, x, dres, g):
    r = lax.rsqrt(jnp.mean(x * x, axis=-1, keepdims=True) + EPS)
    xn = x * r
    dhn = dh * g
    dx = dres + r * (dhn - xn * jnp.mean(dhn * xn, axis=-1, keepdims=True))
    return dx, jnp.sum(dh * xn, axis=0, keepdims=True)


def _final_loss_rows(x, g, tgt):
    r = lax.rsqrt(jnp.mean(x * x, axis=-1, keepdims=True) + EPS)
    xn = x * r
    err = xn * g - tgt
    row_loss = jnp.mean(err * err, axis=-1, keepdims=True)
    loss = 0.5 * jnp.sum(row_loss, axis=0, keepdims=True) * jnp.ones((1, LANES), F32)
    dy = err * (1.0 / D)
    dyn = dy * g
    dx = r * (dyn - xn * jnp.mean(dyn * xn, axis=-1, keepdims=True))
    return dx, jnp.sum(dy * xn, axis=0, keepdims=True), loss


def _sigmoid(z):
    return 1.0 / (1.0 + jnp.exp(-z))


GATE_TR = 512


def _mixer_tail(oa, ob, w_a, w_b, w_out, proj_g, x, g_mlp):
    def body(oa_ref, ob_ref, wa_ref, wb_ref, wo_ref, g_ref, x_ref, gm_ref, ya_ref, yb_ref, mixed_ref, x2_ref, h2_ref):
        ya = _dot(oa_ref[...], wa_ref[...], _NN)
        yb = _dot(ob_ref[...], wb_ref[...], _NN)
        ya_ref[...] = ya
        yb_ref[...] = yb
        mixed = (_sigmoid(g_ref[:, :D]) * ya + _sigmoid(g_ref[:, D:]) * yb).astype(BF16)
        mixed_ref[...] = mixed
        x2 = x_ref[...] + _dot(mixed, wo_ref[...], _NN)
        x2_ref[...] = x2
        h2_ref[...] = _rms_rows(x2, gm_ref[...]).astype(BF16)

    def rows(width):
        return pl.BlockSpec((GATE_TR, width), lambda i: (i, 0))

    def whole(t):
        return pl.BlockSpec(t.shape, lambda i: (0, 0))

    return pl.pallas_call(
        body, name="mixer_tail", grid=(S // GATE_TR,),
        in_specs=[rows(FOX_W), rows(DIL_OUT), whole(w_a), whole(w_b), whole(w_out), rows(2 * D), rows(D), whole(g_mlp)],
        out_specs=[rows(D)] * 5,
        out_shape=[jax.ShapeDtypeStruct((S, D), dt) for dt in (F32, F32, BF16, F32, BF16)],
        compiler_params=_cparams(("parallel",)),
    )(oa, ob, w_a, w_b, w_out, proj_g, x, g_mlp)


def _out_proj_gate_bwd(dx2, w_out, proj_g, ya, yb):
    def body(dx_ref, w_ref, g_ref, ya_ref, yb_ref, dy_ref, dg_ref):
        dm = _dot(dx_ref[...], w_ref[...], _NT)
        for half, y_ref in enumerate((ya_ref, yb_ref)):
            cols = slice(half * D, (half + 1) * D)
            s = _sigmoid(g_ref[:, cols])
            dy_ref[:, cols] = (dm * s).astype(BF16)
            dg_ref[:, cols] = (dm * y_ref[...] * (s * (1.0 - s))).astype(BF16)

    row = pl.BlockSpec((GATE_TR, D), lambda i: (i, 0))
    wide = pl.BlockSpec((GATE_TR, 2 * D), lambda i: (i, 0))
    return pl.pallas_call(
        body, name="out_proj_gate_bwd", grid=(S // GATE_TR,),
        in_specs=[row, pl.BlockSpec((D, D), lambda i: (0, 0)), wide, row, row],
        out_specs=[wide, pl.BlockSpec((GATE_TR, 2 * D), lambda i: (i, SEG_G // (2 * D)))],
        out_shape=[jax.ShapeDtypeStruct((S, 2 * D), BF16), jax.ShapeDtypeStruct((S, D_PAD), BF16)],
        compiler_params=_cparams(("parallel",)),
    )(dx2, w_out, proj_g, ya, yb)


FOX_TQ = 512
FOX_TQ_FWD = 1024


def _scan_rows(x, reverse):
    n = x.shape[0]
    row = lax.broadcasted_iota(jnp.int32, x.shape, 0)
    k = 1
    while k < n:
        if reverse:
            x = x + jnp.where(row < n - k, pltpu.roll(x, n - k, 0), 0.0)
        else:
            x = x + jnp.where(row >= k, pltpu.roll(x, k, 0), 0.0)
        k *= 2
    return x


def _fox_dscan(dft, dfs, proj_f, b_pad, dproj):
    def body(dft_ref, dfs_ref, f_ref, b_ref, _, dfa_ref, db_ref):
        dfs_pad = jnp.concatenate([dfs_ref[...], jnp.zeros((LANES - FOX_H, S), F32)], axis=0)
        df = dfs_pad.T + dft_ref[0]
        for hp in range(1, dft_ref.shape[0]):
            df = df + pltpu.roll(dft_ref[hp], 2 * hp, 1)
        dlf = _scan_rows(df, reverse=True)
        z = f_ref[...] + b_ref[...]
        lane = lax.broadcasted_iota(jnp.int32, (S, LANES), 1)
        dfa = jnp.where(lane < FOX_H, dlf / (1.0 + jnp.exp(z)), 0.0)
        dfa_ref[:, :LANES] = dfa.astype(BF16)
        dfa_ref[:, LANES:] = jnp.zeros((S, CB - LANES), BF16)
        db_ref[...] = jnp.sum(dfa, axis=0, keepdims=True)

    return pl.pallas_call(
        body, name="fox_dscan", grid=(1,),
        in_specs=[pl.BlockSpec(dft.shape, lambda i: (0, 0, 0)), pl.BlockSpec((FOX_H, S), lambda i: (0, 0)),
                  pl.BlockSpec((S, LANES), lambda i: (0, 0)), pl.BlockSpec((1, LANES), lambda i: (0, 0)), ANY],
        out_specs=[pl.BlockSpec((S, CB), lambda i: (0, SEG_F // CB)), pl.BlockSpec((1, LANES), lambda i: (0, 0))],
        out_shape=[jax.ShapeDtypeStruct((S, D_PAD), BF16), jax.ShapeDtypeStruct((1, LANES), F32)],
        input_output_aliases={4: 0},
        compiler_params=_cparams(("arbitrary",)),
    )(dft, dfs, proj_f, b_pad, dproj)


FOX_NQ = S // FOX_TQ
FOX_HP = FOX_W // LANES
HEADS_PER_LB = LANES // HD
FOX_AUG = 2 * LANES


def _fox_prepare(proj_a, proj_f, b_pad):
    tr = 512

    def body(q_ref, k_ref, f_ref, b_ref, qa_ref, ka_ref, hi_s, mid_s, lo_s):
        i = pl.program_id(0)

        @pl.when(i == 0)
        def _():
            z = f_ref[...] + b_ref[...]
            lf = jnp.minimum(z, 0.0) - jnp.log1p(jnp.exp(-jnp.abs(z)))
            f = _scan_rows(lf, reverse=False)
            hi = f.astype(BF16).astype(F32)
            r1 = f - hi
            mid = r1.astype(BF16).astype(F32)
            hi_s[...] = hi
            mid_s[...] = mid
            lo_s[...] = (r1 - mid).astype(BF16).astype(F32)

        rows = pl.ds(pl.multiple_of(i * tr, tr), tr)
        hi, mid, lo = hi_s[rows, :], mid_s[rows, :], lo_s[rows, :]
        lane = lax.broadcasted_iota(jnp.int32, (tr, HD), 1)
        q_ones = jnp.where((lane >= 3) & (lane < 6), 1.0, 0.0)
        k_ones = jnp.where(lane < 3, 1.0, 0.0)
        for h in range(FOX_H):
            a1, a2, a3 = hi[:, h:h + 1], mid[:, h:h + 1], lo[:, h:h + 1]
            q_extra = jnp.where(lane == 0, a1, jnp.where(lane == 1, a2, jnp.where(lane == 2, a3, q_ones)))
            k_extra = jnp.where(lane == 3, -a1, jnp.where(lane == 4, -a2, jnp.where(lane == 5, -a3, k_ones)))
            base = h * LANES
            qa_ref[:, base:base + HD] = q_ref[:, h * HD:(h + 1) * HD] * jnp.asarray(SCALE, BF16)
            qa_ref[:, base + HD:base + LANES] = q_extra.astype(BF16)
            ka_ref[:, base:base + HD] = k_ref[:, h * HD:(h + 1) * HD]
            ka_ref[:, base + HD:base + LANES] = k_extra.astype(BF16)

    out_blk = pl.BlockSpec((tr, FOX_H * LANES), lambda i: (i, 0))
    return pl.pallas_call(
        body, name="fox_prepare", grid=(S // tr,),
        in_specs=[pl.BlockSpec((tr, FOX_W), lambda i: (i, 0)), pl.BlockSpec((tr, FOX_W), lambda i: (i, 1)),
                  pl.BlockSpec((S, LANES), lambda i: (0, 0)), pl.BlockSpec((1, LANES), lambda i: (0, 0))],
        out_specs=[out_blk, out_blk],
        out_shape=[jax.ShapeDtypeStruct((S, FOX_H * LANES), BF16)] * 2,
        scratch_shapes=[pltpu.VMEM((S, LANES), F32)] * 3,
        compiler_params=_cparams(("arbitrary",)),
    )(proj_a, proj_a, proj_f, b_pad)


def _fox_scores(qa, ka, w, tq):
    s = _dot(qa, ka, _NT)
    row = lax.broadcasted_iota(jnp.int32, (tq, tq), 0)
    col = lax.broadcasted_iota(jnp.int32, (tq, tq), 1)
    diag = jnp.where(col <= row, s[:, w * tq:], NEG_INF)
    return diag if w == 0 else jnp.concatenate([s[:, :w * tq], diag], axis=1)


def _fox_fwd(q_aug, k_aug, proj_a):
    tq = FOX_TQ_FWD

    def body(qa_ref, ka_ref, v_ref, o_ref, lse_ref):
        qi = pl.program_id(1)
        lane = lax.broadcasted_iota(jnp.int32, (tq, LANES), 1)
        for w in range(S // tq):
            @pl.when(qi == w)
            def _(w=w):
                width = (w + 1) * tq
                lse = jnp.zeros((tq, LANES), F32)
                for hh in range(HEADS_PER_LB):
                    aug = slice(hh * LANES, (hh + 1) * LANES)
                    sl = slice(hh * HD, (hh + 1) * HD)
                    s = _fox_scores(qa_ref[:, aug], ka_ref[:width, aug], w, tq)
                    m = jnp.max(s, axis=-1, keepdims=True)
                    p = jnp.exp(s - m)
                    l = jnp.sum(p, axis=-1, keepdims=True)
                    o_ref[:, sl] = _dot(p / l, v_ref[:width, sl], _NN)
                    lse = jnp.where(lane == hh, m + jnp.log(l), lse)
                lse_ref[...] = lse

    return pl.pallas_call(
        body, name="fox_fwd", grid=(FOX_HP, S // tq),
        in_specs=[pl.BlockSpec((tq, FOX_AUG), lambda h, i: (i, h)),
                  pl.BlockSpec((S, FOX_AUG), lambda h, i: (0, h)),
                  pl.BlockSpec((S, LANES), lambda h, i: (0, 2 * FOX_HP + h))],
        out_specs=[pl.BlockSpec((tq, LANES), lambda h, i: (i, h)),
                   pl.BlockSpec((None, tq, LANES), lambda h, i: (h, i, 0))],
        out_shape=[jax.ShapeDtypeStruct((S, FOX_W), F32), jax.ShapeDtypeStruct((FOX_HP, S, LANES), F32)],
        compiler_params=_cparams(("parallel", "parallel")),
    )(q_aug, k_aug, proj_a)


def _fox_bwd(q_aug, k_aug, proj_a, lse, do, dproj, deps=()):
    n_q = FOX_NQ

    def body(qa_ref, ka_ref, v_ref, lse_ref, do_ref, *refs):
        dproj_ref, dft_ref, dfs_ref, dk_acc, dv_acc, dq_buf, kv_buf, dq_sems, kv_sems = refs[1 + len(deps):]
        hp = pl.program_id(0)
        t = pl.program_id(1)
        slot = t % 2
        col = pl.multiple_of(hp * LANES, LANES)

        def dq_copy(step, buf_slot):
            rows = pl.ds(pl.multiple_of(step * FOX_TQ, FOX_TQ), FOX_TQ)
            return pltpu.make_async_copy(dq_buf.at[buf_slot], dproj_ref.at[rows, pl.ds(col, LANES)], dq_sems.at[buf_slot])

        @pl.when(t == 0)
        def _():
            dk_acc[...] = jnp.zeros_like(dk_acc)
            dv_acc[...] = jnp.zeros_like(dv_acc)

        @pl.when((t == 0) & (hp == 0))
        def _():
            dfs_ref[...] = jnp.zeros_like(dfs_ref)

        @pl.when(t >= 2)
        def _():
            dq_copy(t - 2, slot).wait()

        lane = lax.broadcasted_iota(jnp.int32, (FOX_TQ, LANES), 1)
        for w in range(n_q):
            @pl.when(t == w)
            def _(w=w):
                width = (w + 1) * FOX_TQ
                lse_all = lse_ref[...]
                sub = lax.broadcasted_iota(jnp.int32, (FOX_H, width), 0)
                dft = jnp.zeros((FOX_TQ, LANES), F32)
                for hh in range(HEADS_PER_LB):
                    aug = slice(hh * LANES, (hh + 1) * LANES)
                    sl = slice(hh * HD, (hh + 1) * HD)
                    head = hp * HEADS_PER_LB + hh
                    qa = qa_ref[:, aug]
                    ka = ka_ref[:width, aug]
                    do_h = do_ref[:, sl]
                    s = _fox_scores(qa, ka, w, FOX_TQ)
                    p = jnp.exp(s - lse_all[:, hh:hh + 1])
                    dp = _dot(do_h, v_ref[:width, sl], _NT)
                    ds = p * (dp - jnp.sum(dp * p, axis=-1, keepdims=True))
                    dft = jnp.where(lane == hh, jnp.sum(ds, axis=-1, keepdims=True), dft)
                    dfs_ref[:, :width] -= jnp.where(sub == head, jnp.sum(ds, axis=0, keepdims=True), 0.0)
                    dq_buf[slot, :, sl] = (_dot(ds, ka[:, :HD], _NN) * SCALE).astype(BF16)
                    dk_acc[:width, sl] += _dot(ds, qa[:, :HD], _TN)
                    dv_acc[:width, sl] += _dot(p, do_h, _TN)
                dft_ref[...] = dft

        dq_copy(t, slot).start()

        @pl.when(t == n_q - 1)
        def _():
            dq_copy(t - 1, 1 - slot).wait()
            dq_copy(t, slot).wait()
            kv_buf[0] = dk_acc[...].astype(BF16)
            kv_buf[1] = dv_acc[...].astype(BF16)
            copies = [pltpu.make_async_copy(
                kv_buf.at[j], dproj_ref.at[:, pl.ds(pl.multiple_of((j + 1) * FOX_W + hp * LANES, LANES), LANES)],
                kv_sems.at[j]) for j in range(2)]
            for cp in copies:
                cp.start()
            for cp in copies:
                cp.wait()

    qblk = pl.BlockSpec((FOX_TQ, FOX_AUG), lambda h, t: (t, h))
    lane_blk = pl.BlockSpec((None, FOX_TQ, LANES), lambda h, t: (h, t, 0))
    return pl.pallas_call(
        body, name="fox_bwd", grid=(FOX_HP, n_q),
        in_specs=[qblk, pl.BlockSpec((S, FOX_AUG), lambda h, t: (0, h)),
                  pl.BlockSpec((S, LANES), lambda h, t: (0, 2 * FOX_HP + h)),
                  lane_blk, pl.BlockSpec((FOX_TQ, LANES), lambda h, t: (t, h)), ANY] + [ANY] * len(deps),
        out_specs=[ANY, lane_blk, pl.BlockSpec((FOX_H, S), lambda h, t: (0, 0))],
        out_shape=[jax.ShapeDtypeStruct((S, D_PAD), BF16),
                   jax.ShapeDtypeStruct((FOX_HP, S, LANES), F32), jax.ShapeDtypeStruct((FOX_H, S), F32)],
        scratch_shapes=[pltpu.VMEM((S, LANES), F32), pltpu.VMEM((S, LANES), F32),
                        pltpu.VMEM((2, FOX_TQ, LANES), BF16), pltpu.VMEM((2, S, LANES), BF16),
                        pltpu.SemaphoreType.DMA((2,)), pltpu.SemaphoreType.DMA((2,))],
        input_output_aliases={5: 0},
        compiler_params=_cparams(("arbitrary", "arbitrary")),
    )(q_aug, k_aug, proj_a, lse, do, dproj, *deps)


def _rope_tables():
    half = ROPE_DIM // 2
    shape = (DIL_G, S, half)
    g = lax.broadcasted_iota(jnp.int32, shape, 0)
    row = lax.broadcasted_iota(jnp.int32, shape, 1)
    r = jnp.left_shift(1, 2 * g)
    per_class = S // r
    pos = (row % per_class) * r + row // per_class
    inv_freq = jnp.power(jnp.float32(ROPE_THETA), -lax.broadcasted_iota(F32, shape, 2) * 2.0 / ROPE_DIM)
    ang = pos.astype(F32) * inv_freq
    return jnp.concatenate([jnp.cos(ang), jnp.sin(ang)], axis=-1)


def _rope_factors(cs):
    half = ROPE_DIM // 2
    i = lax.broadcasted_iota(jnp.int32, (2 * half, 2 * LANES), 0)
    l = lax.broadcasted_iota(jnp.int32, (2 * half, 2 * LANES), 1)
    hit = ((l < LANES) & (i == l % half)) | ((l >= LANES) & (i - half == l % half))
    spread = jnp.where(hit, 1.0, 0.0).astype(BF16)
    hi = cs.astype(BF16)
    r1 = cs - hi.astype(F32)
    mid = r1.astype(BF16)
    lo = (r1 - mid.astype(F32)).astype(BF16)
    full = (_dot(hi, spread, _NN) + _dot(mid, spread, _NN)) + _dot(lo, spread, _NN)
    cos, sin = full[:, :LANES], full[:, LANES:]
    lane = lax.broadcasted_iota(jnp.int32, cos.shape, 1) % HD
    t_self = jnp.where(lane < ROPE_DIM, cos, 1.0)
    t_up = jnp.where(lane < half, -sin, 0.0)
    t_dn = jnp.where((lane >= half) & (lane < ROPE_DIM), sin, 0.0)
    return t_self, t_up, t_dn


def _residue_pieces(r):
    if r == 1:
        return [(slice(i, i + 512), slice(i, i + 512)) for i in range(0, S, 512)]
    n = S // r
    return [(pl.ds(j, n, stride=r), slice(j * n, (j + 1) * n)) for j in range(r)]


def _rope_apply(x, a, u, d, backward):
    half = ROPE_DIM // 2
    if backward:
        return x * a + pltpu.roll(x * u, half, 1) + pltpu.roll(x * d, LANES - half, 1)
    return x * a + pltpu.roll(x, LANES - half, 1) * u + pltpu.roll(x, half, 1) * d


def _rope_cache_factors(cs_ref, fac, first_step_of_group):
    @pl.when(first_step_of_group)
    def _():
        for lo in range(0, S, 512):
            for j, f in enumerate(_rope_factors(cs_ref[lo:lo + 512, :])):
                fac[j, lo:lo + 512, :] = f


LB_PER_CB = CB // LANES
ROPE_NB = 3 * DIL_G * LB_PER_CB


def _rope_step(s):
    per_group = 3 * LB_PER_CB
    return s // per_group, (s % per_group) // LB_PER_CB, s % LB_PER_CB


def _rope_split(proj_b, tables):
    def body(x_ref, cs_ref, o_ref, fac):
        g, t, hf = _rope_step(pl.program_id(0))
        _rope_cache_factors(cs_ref, fac, (t == 0) & (hf == 0))
        for gs in range(DIL_G):
            @pl.when(g == gs)
            def _(gs=gs):
                for tok, sub in _residue_pieces(DIL_R[gs]):
                    x = x_ref[tok, :]
                    y = _rope_apply(x, fac[0, sub, :], fac[1, sub, :], fac[2, sub, :], False)
                    o_ref[sub, :] = jnp.where(t < 2, y, x).astype(BF16)

    def in_index(s):
        g, t, hf = _rope_step(s)
        return 0, (t * DIL_G + g) * LB_PER_CB + hf

    def out_index(s):
        g, t, hf = _rope_step(s)
        return t * DIL_G + g, 0, hf

    tab = pl.BlockSpec((None, S, 2 * (ROPE_DIM // 2)), lambda s: (_rope_step(s)[0], 0, 0))
    return pl.pallas_call(
        body, name="rope_split", grid=(ROPE_NB,),
        in_specs=[pl.BlockSpec((S, LANES), in_index), tab],
        out_specs=pl.BlockSpec((None, S, LANES), out_index),
        out_shape=jax.ShapeDtypeStruct((3 * DIL_G, S, CB), BF16),
        scratch_shapes=[pltpu.VMEM((3, S, LANES), F32)],
        compiler_params=_cparams(("arbitrary",)),
    )(proj_b, tables)


def _rope_merge_bwd(dq3, dk3, dv3, tables, dproj):
    def body(dq_ref, dk_ref, dv_ref, cs_ref, _, o_ref, tmp, fac):
        s = pl.program_id(0)
        g, t, hf = _rope_step(s)
        _rope_cache_factors(cs_ref, fac, (t == 0) & (hf == 0) & (s < ROPE_NB))
        for gs in range(DIL_G):
            @pl.when((g == gs) & (s < ROPE_NB))
            def _(gs=gs):
                for tok, sub in _residue_pieces(DIL_R[gs]):
                    x = jnp.where(t == 0, dq_ref[sub, :], jnp.where(t == 1, dk_ref[sub, :], dv_ref[sub, :]))
                    y = _rope_apply(x, fac[0, sub, :], fac[1, sub, :], fac[2, sub, :], True)
                    tmp[tok, :] = jnp.where(t < 2, y, x)
                o_ref[...] = tmp[...].astype(BF16)

        @pl.when(s >= ROPE_NB)
        def _():
            o_ref[...] = jnp.zeros_like(o_ref)

    def src_spec(own):
        def index(s):
            g, t, hf = _rope_step(jnp.minimum(s, ROPE_NB - 1))
            return g, 0, jnp.where(t == own, hf, jnp.where(t > own, LB_PER_CB - 1, 0))
        return pl.BlockSpec((None, S, LANES), index)

    def out_index(s):
        g, t, hf = _rope_step(s)
        col = SEG_B // LANES + (t * DIL_G + g) * LB_PER_CB + hf
        return 0, jnp.where(s < ROPE_NB, col, SEG_Z // LANES + s - ROPE_NB)

    tab = pl.BlockSpec((None, S, 2 * (ROPE_DIM // 2)), lambda s: (_rope_step(jnp.minimum(s, ROPE_NB - 1))[0], 0, 0))
    return pl.pallas_call(
        body, name="rope_merge_bwd", grid=(ROPE_NB + LB_PER_CB,),
        in_specs=[src_spec(0), src_spec(1), src_spec(2), tab, ANY],
        out_specs=pl.BlockSpec((S, LANES), out_index),
        out_shape=jax.ShapeDtypeStruct((S, D_PAD), BF16),
        scratch_shapes=[pltpu.VMEM((S, LANES), F32), pltpu.VMEM((3, S, LANES), F32)],
        input_output_aliases={4: 0},
        compiler_params=_cparams(("arbitrary",)),
    )(dq3, dk3, dv3, tables, dproj)


DIL_NB = S // DIL_BLK


_BQK = (((2,), (2,)), ((0,), (0,)))
_BQD = (((2,), (1,)), ((0,), (0,)))
_BKD = (((1,), (1,)), ((0,), (0,)))


def _dil_mask(g):
    shape = (DIL_NB, DIL_BLK, 2 * DIL_BLK)
    blocks_per_seq = jnp.right_shift(DIL_NB, 2 * g)
    has_prev = jnp.bitwise_and(lax.broadcasted_iota(jnp.int32, shape, 0), blocks_per_seq - 1) != 0
    a = lax.broadcasted_iota(jnp.int32, shape, 1)
    kk = lax.broadcasted_iota(jnp.int32, shape, 2)
    diff = DIL_BLK + a - kk
    return (diff >= 0) & (diff <= DIL_BLK) & ((kk >= DIL_BLK) | has_prev)


def _blocks(t):
    return t.reshape(DIL_NB, DIL_BLK, t.shape[-1])


def _with_prev(t):
    prev = jnp.concatenate([jnp.zeros((DIL_BLK, t.shape[-1]), t.dtype), t[:-DIL_BLK]], axis=0)
    return jnp.concatenate([_blocks(prev), _blocks(t)], axis=1)


def _fold_prev(t2):
    n = t2.shape[-1]
    to_prev = t2[:, :DIL_BLK].reshape(S, n)
    own = t2[:, DIL_BLK:].reshape(S, n)
    return own + jnp.concatenate([to_prev[DIL_BLK:], jnp.zeros((DIL_BLK, n), t2.dtype)], axis=0)


def _dil_group_spec(t, width=DIL_OUT):
    return pl.BlockSpec((None, S, width), lambda g: (t * DIL_G + g, 0, 0))


def _dil_fwd(qkv3):
    def body(q_ref, k_ref, v_ref, o_ref, lse_ref):
        ok = _dil_mask(pl.program_id(0))
        lane = lax.broadcasted_iota(jnp.int32, (S, LANES), 1)
        lse = jnp.zeros((S, LANES), F32)
        for h in range(DIL_HG):
            sl = slice(h * HD, (h + 1) * HD)
            s = jnp.where(ok, _dot(_blocks(q_ref[:, sl]), _with_prev(k_ref[:, sl]), _BQK) * SCALE, NEG_INF)
            m = jnp.max(s, axis=-1, keepdims=True)
            p = jnp.exp(s - m)
            l = jnp.sum(p, axis=-1, keepdims=True)
            o_ref[:, sl] = _dot(p / l, _with_prev(v_ref[:, sl]), _BQD).reshape(S, HD)
            lse = jnp.where(lane == h, (m + jnp.log(l)).reshape(S, 1), lse)
        lse_ref[...] = lse

    return pl.pallas_call(
        body, name="dil_fwd", grid=(DIL_G,),
        in_specs=[_dil_group_spec(0), _dil_group_spec(1), _dil_group_spec(2)],
        out_specs=[_dil_group_spec(0), _dil_group_spec(0, LANES)],
        out_shape=[jax.ShapeDtypeStruct((DIL_G, S, DIL_OUT), F32), jax.ShapeDtypeStruct((DIL_G, S, LANES), F32)],
        compiler_params=_cparams(("parallel",)),
    )(qkv3, qkv3, qkv3)


def _dil_bwd(qkv3, lse3, do3, c3):
    def body(q_ref, k_ref, v_ref, lse_ref, do_ref, c_ref, dq_ref, dk_ref, dv_ref):
        ok = _dil_mask(pl.program_id(0))
        lse = lse_ref[...]
        c = c_ref[...]
        for h in range(DIL_HG):
            sl = slice(h * HD, (h + 1) * HD)
            q = _blocks(q_ref[:, sl])
            k2 = _with_prev(k_ref[:, sl])
            do_h = _blocks(do_ref[:, sl])
            s = jnp.where(ok, _dot(q, k2, _BQK) * SCALE, NEG_INF)
            p = jnp.exp(s - _blocks(lse[:, h:h + 1]))
            dp = _dot(do_h, _with_prev(v_ref[:, sl]), _BQK)
            ds = p * (dp - _blocks(c[:, h:h + 1]))
            dsb = (ds * SCALE).astype(BF16)
            dq_ref[:, sl] = _dot(dsb, k2, _BQD).reshape(S, HD)
            dk_ref[:, sl] = _fold_prev(_dot(dsb, q, _BKD))
            dv_ref[:, sl] = _fold_prev(_dot(p, do_h, _BKD))

    return pl.pallas_call(
        body, name="dil_bwd", grid=(DIL_G,),
        in_specs=[_dil_group_spec(0), _dil_group_spec(1), _dil_group_spec(2), _dil_group_spec(0, LANES),
                  _dil_group_spec(0), _dil_group_spec(0, LANES)],
        out_specs=[_dil_group_spec(0)] * 3,
        out_shape=[jax.ShapeDtypeStruct((DIL_G, S, DIL_OUT), F32)] * 3,
        compiler_params=_cparams(("parallel",)),
    )(qkv3, qkv3, qkv3, lse3, do3, c3)


COMB_T = 256


def _dil_combine(o3, lse3):
    heads_per_lb = LANES // HD

    def body(o_ref, lse_ref, ob_ref, al_ref, *scratch):
        o_tok = [scratch[LB_PER_CB * gg:LB_PER_CB * (gg + 1)] for gg in range(DIL_G)]
        lse_tok = scratch[LB_PER_CB * DIL_G:]
        g = pl.program_id(0)
        for gs in range(DIL_G):
            @pl.when(g == gs)
            def _(gs=gs):
                for tok, sub in _residue_pieces(DIL_R[gs]):
                    for hf in range(LB_PER_CB):
                        o_tok[gs][hf][tok, :] = o_ref[sub, hf * LANES:(hf + 1) * LANES]
                    lse_tok[gs][tok, :] = lse_ref[sub, :]

        @pl.when(g == DIL_G - 1)
        def _():
            def chunk(i, carry):
                rows = pl.ds(pl.multiple_of(i * COMB_T, COMB_T), COMB_T)
                lse = [lse_tok[gg][rows, :] for gg in range(DIL_G)]
                m = jnp.maximum(jnp.maximum(lse[0], lse[1]), lse[2])
                e = [jnp.exp(lse[gg] - m) for gg in range(DIL_G)]
                den = (e[0] + e[1]) + e[2]
                al = [e[gg] / den for gg in range(DIL_G)]
                for gg in range(DIL_G):
                    al_ref[gg, rows, :] = al[gg]
                for h in range(DIL_HG):
                    hf, sl = h // heads_per_lb, slice((h % heads_per_lb) * HD, (h % heads_per_lb + 1) * HD)
                    acc = al[0][:, h:h + 1] * o_tok[0][hf][rows, sl]
                    for gg in range(1, DIL_G):
                        acc = acc + al[gg][:, h:h + 1] * o_tok[gg][hf][rows, sl]
                    ob_ref[rows, h * HD:(h + 1) * HD] = acc
                return carry

            lax.fori_loop(0, S // COMB_T, chunk, 0)

    return pl.pallas_call(
        body, name="dil_combine", grid=(DIL_G,),
        in_specs=[pl.BlockSpec((None, S, DIL_OUT), lambda g: (g, 0, 0)),
                  pl.BlockSpec((None, S, LANES), lambda g: (g, 0, 0))],
        out_specs=[pl.BlockSpec((S, DIL_OUT), lambda g: (0, 0)),
                   pl.BlockSpec((DIL_G, S, LANES), lambda g: (0, 0, 0))],
        out_shape=[jax.ShapeDtypeStruct((S, DIL_OUT), F32), jax.ShapeDtypeStruct((DIL_G, S, LANES), F32)],
        scratch_shapes=[pltpu.VMEM((S, LANES), F32)] * (DIL_G * (LB_PER_CB + 1)),
        compiler_params=_cparams(("arbitrary",)),
    )(o3, lse3)


def _dil_combine_bwd(dob, ob, alpha, deps=()):
    heads_per_lb = LANES // HD

    def body(dob_ref, ob_ref, al_ref, *refs):
        do_ref, c_ref = refs[len(deps):]
        g = pl.program_id(0)
        hf = pl.program_id(1)
        for gs in range(DIL_G):
            @pl.when(g == gs)
            def _(gs=gs):
                for tok, sub in _residue_pieces(DIL_R[gs]):
                    dob = dob_ref[tok, :]
                    prod = dob * ob_ref[tok, :]
                    al = al_ref[tok, :]
                    lane = lax.broadcasted_iota(jnp.int32, al.shape, 1)
                    c = jnp.where(hf == 0, 0.0, c_ref[sub, :])
                    for hh in range(heads_per_lb):
                        sl = slice(hh * HD, (hh + 1) * HD)
                        head = hf * heads_per_lb + hh
                        a = jnp.sum(jnp.where(lane == head, al, 0.0), axis=-1, keepdims=True)
                        do_ref[sub, sl] = (a * dob[:, sl]).astype(BF16)
                        c = jnp.where(lane == head, a * jnp.sum(prod[:, sl], axis=-1, keepdims=True), c)
                    c_ref[sub, :] = c

    half = pl.BlockSpec((S, LANES), lambda g, hf: (0, hf))
    return pl.pallas_call(
        body, name="dil_combine_bwd", grid=(DIL_G, LB_PER_CB),
        in_specs=[half, half, pl.BlockSpec((None, S, LANES), lambda g, hf: (g, 0, 0))] + [ANY] * len(deps),
        out_specs=[pl.BlockSpec((None, S, LANES), lambda g, hf: (g, 0, hf)),
                   pl.BlockSpec((None, S, LANES), lambda g, hf: (g, 0, 0))],
        out_shape=[jax.ShapeDtypeStruct((DIL_G, S, DIL_OUT), BF16), jax.ShapeDtypeStruct((DIL_G, S, LANES), F32)],
        compiler_params=_cparams(("parallel", "arbitrary")),
    )(dob, ob, alpha, *deps)


def _local_step(x, tgt, h1, g_attn, g_mlp, g_final, b_pad, w_in_t, hooks):
    tables = _rope_tables()

    first = hooks.first_deps()
    proj_a = _mm(h1, w_in_t, mode="nt", tm=S, tn=512, n=3 * FOX_W, out_dtypes=[BF16], name="proj_a", deps=first)
    proj_b = _mm(h1, w_in_t, mode="nt", tm=S, tn=DIL_W, n=3 * DIL_W, b_off=SEG_B // DIL_W, out_dtypes=[F32], name="proj_b",
                deps=first)
    proj_g = _mm(h1, w_in_t, mode="nt", tm=S, tn=D, n=2 * D, b_off=SEG_G // D, out_dtypes=[F32], name="proj_g",
                deps=first)
    proj_f = _mm(h1, w_in_t, mode="nt", tm=S, tn=LANES, n=LANES, b_off=SEG_F // LANES, out_dtypes=[F32], name="proj_f",
                deps=first)

    q_aug, k_aug = _fox_prepare(proj_a, proj_f, b_pad)
    oa, lse_a = _fox_fwd(q_aug, k_aug, proj_a)

    qkv3 = _rope_split(proj_b, tables)
    o3, lse3 = _dil_fwd(qkv3)
    ob, alpha = _dil_combine(o3, lse3)

    w_a, w_b, w_out = hooks.mixer_weights(ob)
    ya, yb, mixed, x2, h2 = _mixer_tail(oa, ob, w_a, w_b, w_out, proj_g, x, g_mlp)
    w_up_sh, w_down = hooks.mlp_weights(h2)

    def up_epilogue(acc):
        r = jnp.maximum(acc, 0.0)
        return acc, r * r

    u, act = _mm(h2, w_up_sh, mode="nn", tm=S, tn=DFF // N_DEV, b_sharded=True, out_dtypes=[F32, BF16],
                 name="mlp_up", epilogue=up_epilogue)

    def loss_epilogue(acc, rows, vecs):
        dx3, dg, loss = _final_loss_rows(rows[0] + acc, vecs[0], rows[1])
        return (dx3,), (dg, loss)

    dx3, dg_final, loss = _mm_rows(act, w_down, tm=ROWS_TM, name="mlp_down_loss", epilogue=loss_epilogue,
                                   rows=(x2, tgt), vecs=(g_final,), row_out=(F32,), vec_out=(D, LANES))

    def rms_bwd_epilogue(acc, rows, vecs):
        dx, dg = _rms_bwd_rows(acc, rows[0], rows[1], vecs[0])
        return (dx,), (dg,)

    du = _mm(dx3, w_down, mode="nt", tm=S, tn=512, out_dtypes=[BF16], name="mlp_down_bwd",
             epilogue=lambda acc, u_t: (acc * (2.0 * jnp.maximum(u_t, 0.0)),), extras=(u,))
    dw_down = _mm(act, dx3, mode="tn", tm=512, tn=D, out_dtypes=[F32], name="dw_down")
    dw_up_sh = _mm(h2, du, mode="tn", tm=D, tn=DFF // N_DEV, out_sharded=True, out_dtypes=[F32], name="dw_up")
    dx2, dg_mlp = _mm_rows(du, w_up_sh, b_sharded=True, tm=ROWS_TM, name="mlp_up_bwd", epilogue=rms_bwd_epilogue,
                           rows=(x2, dx3), vecs=(g_mlp,), row_out=(F32,), vec_out=(D,),
                           deps=hooks.mlp_grads(dw_up_sh, dw_down))

    dw_out = _mm(mixed, dx2, mode="tn", tm=D, tn=D, out_dtypes=[F32], name="dw_out")

    dyab, dproj = _out_proj_gate_bwd(dx2, w_out, proj_g, ya, yb)
    doa = _mm(dyab, w_a, mode="nt", tm=S, tn=FOX_W, k=D, a_off=0, out_dtypes=[BF16], name="branch_a_bwd")
    dw_a = _mm(oa, dyab, mode="tn", tm=FOX_W, tn=D, n=D, b_off=0, out_dtypes=[F32], name="dw_branch_a")
    dob = _mm(dyab, w_b, mode="nt", tm=S, tn=CB, k=D, a_off=1, out_dtypes=[F32], name="branch_b_bwd")
    dw_b = _mm(ob, dyab, mode="tn", tm=DIL_OUT, tn=D, n=D, b_off=1, out_dtypes=[F32], name="dw_branch_b")

    dproj, dft, dfs = _fox_bwd(q_aug, k_aug, proj_a, lse_a, doa, dproj, deps=hooks.mixer_grads(dw_a, dw_b, dw_out))
    dproj, db = _fox_dscan(dft, dfs, proj_f, b_pad, dproj)

    do3, c3 = _dil_combine_bwd(dob, ob, alpha, deps=hooks.mid_backward(db))
    dq3, dk3, dv3 = _dil_bwd(qkv3, lse3, do3, c3)
    dproj = _rope_merge_bwd(dq3, dk3, dv3, tables, dproj)

    dw_in_t = _mm(dproj, h1, mode="tn", tm=DW_IN_TM, tn=D, out_dtypes=[F32], name="dw_in")
    dx, dg_attn = _mm_rows(dproj, w_in_t, tm=ROWS_TM // 2, name="proj_bwd", epilogue=rms_bwd_epilogue,
                           rows=(x, dx2), vecs=(g_attn,), row_out=(F32,), vec_out=(D,),
                           deps=hooks.w_in_grad(dw_in_t))

    return loss, dx, (dg_attn, db, dg_mlp, dg_final)


SHARD_SHAPES = ((D_IN // N_DEV, D), (FOX_W, D // N_DEV), (DIL_OUT, D // N_DEV), (D // N_DEV, D),
                (D, DFF // N_DEV), (DFF // N_DEV, D))
W_NAMES = ("w_in", "w_a", "w_b", "w_out", "w_up", "w_down")
AG_COPIES = 7
HBM = pl.BlockSpec(memory_space=pltpu.HBM)
SEM = pl.BlockSpec(memory_space=pltpu.SEMAPHORE)


def _place():
    return lax.axis_index("x"), lax.axis_index("y"), lax.axis_index("c")


W_SLAB = 752
W_WIN = 272
PAD_BLK = 256


def _pad_runs():
    runs = sorted((pad, pad + b - a, p, a) for p in range(N_DEV) for a, b, pad in _shard_pieces(p))
    blocks = []
    for k in range(D_PAD // PAD_BLK):
        lo, hi = k * PAD_BLK, (k + 1) * PAD_BLK
        blocks.append([(max(lo, r0) - lo, min(hi, r1) - lo, p, a + max(lo, r0) - r0)
                       for r0, r1, p, a in runs if max(lo, r0) < min(hi, r1)])
    return blocks


def _gather_w_in(shard, x_in, g_attn, late_shards):
    blocks = _pad_runs()

    norm_t = 512
    n_chunks = S // norm_t

    n_late = len(late_shards)
    late_shapes = [t.shape for t in late_shards]

    def body(x_ref, xin_ref, g_ref, *refs):
        late_in, (out_ref, h1_ref), late_out = refs[:n_late], refs[n_late:n_late + 2], refs[n_late + 2:2 * n_late + 2]
        scratch = refs[2 * n_late + 2:]
        stage, land, xbuf = scratch[:3]
        late_stage, late_cast = scratch[3:3 + n_late], scratch[3 + n_late:3 + 2 * n_late]
        send_sems, recv_sems, load_sem, x_sems, late_sems, store_sems = scratch[3 + 2 * n_late:]
        x, y, c = _place()
        me, sibling = (x, y, c), (x, y, 1 - c)
        chips = [(1 - x, y), (x, 1 - y), (1 - x, 1 - y)]

        def slot(px, py, pc):
            return land.at[4 * px + 2 * py + pc]

        def copy(k, block, to):
            return pltpu.make_async_remote_copy(
                src_ref=slot(*block), dst_ref=slot(*block), send_sem=send_sems.at[k], recv_sem=recv_sems.at[k],
                device_id=to, device_id_type=MESH)

        load = pltpu.make_async_copy(x_ref, stage, load_sem)
        load.start()
        load.wait()
        land[4 * x + 2 * y + c] = stage[...].astype(BF16)
        sent = [copy(0, me, sibling)] + [copy(1 + j, me, (*chip, c)) for j, chip in enumerate(chips)]
        for cp in sent:
            cp.start()

        def x_load(i):
            return pltpu.make_async_copy(xin_ref.at[pl.ds(i * norm_t, norm_t)], xbuf.at[i % 2], x_sems.at[i % 2])

        late_loads = [pltpu.make_async_copy(late_in[i], late_stage[i], late_sems.at[i]) for i in range(n_late)]
        for ld in late_loads:
            ld.start()
        x_load(0).start()
        for i in range(n_chunks):
            if i + 1 < n_chunks:
                x_load(i + 1).start()
            x_load(i).wait()
            h1_ref[i * norm_t:(i + 1) * norm_t, :] = _rms_rows(xbuf[i % 2], g_ref[...]).astype(BF16)
        stores = []
        for i in range(n_late):
            late_loads[i].wait()
            late_cast[i][...] = late_stage[i][...].astype(BF16)
            stores.append(pltpu.make_async_copy(late_cast[i], late_out[i].at[4 * x + 2 * y + c], store_sems.at[i]))
            stores[-1].start()

        for j, chip in enumerate(chips):
            copy(1 + j, (*chip, c), me).wait_recv()
            sent.append(copy(4 + j, (*chip, c), sibling))
            sent[-1].start()
        copy(0, sibling, me).wait_recv()
        for j, chip in enumerate(chips):
            copy(4 + j, (*chip, 1 - c), me).wait_recv()
        for cp in sent:
            cp.wait_send()
        for st in stores:
            st.wait()

        row = lax.broadcasted_iota(jnp.int32, (PAD_BLK, D), 0)
        for k, runs in enumerate(blocks):
            out = jnp.zeros((PAD_BLK, D), F32)
            for o_lo, o_hi, p, a in runs:
                start = min(a // 16 * 16, W_SLAB - W_WIN)
                win = land[p, start:start + W_WIN, :].astype(F32)
                moved = pltpu.roll(win, (o_lo - (a - start)) % W_WIN, 0)[:PAD_BLK]
                out = jnp.where((row >= o_lo) & (row < o_hi), moved, out)
            out_ref[k * PAD_BLK:(k + 1) * PAD_BLK, :] = out.astype(BF16)

    return pl.pallas_call(
        body, name="all_gather_w_in",
        out_shape=[jax.ShapeDtypeStruct((D_PAD, D), BF16), jax.ShapeDtypeStruct((S, D), BF16)]
                  + [jax.ShapeDtypeStruct((N_DEV,) + sh, BF16) for sh in late_shapes],
        in_specs=[ANY, ANY, pl.BlockSpec(memory_space=pltpu.VMEM)] + [ANY] * n_late,
        out_specs=[pl.BlockSpec(memory_space=pltpu.VMEM)] * 2 + [ANY] * n_late,
        scratch_shapes=[pltpu.VMEM((W_SLAB, D), F32), pltpu.VMEM((N_DEV, W_SLAB, D), BF16),
                        pltpu.VMEM((2, norm_t, D), F32)]
                       + [pltpu.VMEM(sh, F32) for sh in late_shapes] + [pltpu.VMEM(sh, BF16) for sh in late_shapes]
                       + [pltpu.SemaphoreType.DMA((AG_COPIES,)), pltpu.SemaphoreType.DMA((AG_COPIES,)),
                          pltpu.SemaphoreType.DMA(()), pltpu.SemaphoreType.DMA((2,)),
                          pltpu.SemaphoreType.DMA((n_late,)), pltpu.SemaphoreType.DMA((n_late,))],
        compiler_params=_cparams(None),
    )(shard, x_in, g_attn, *late_shards)


PEER_COPIES = N_DEV - 1
SIBLING_COPIES = 4
CHIP_COPIES = 3


def _peer_copies(_, land_refs, send_sems, recv_sems):
    x, y, c = _place()
    mine = 4 * x + 2 * y + c
    copies = []
    for i, ref in enumerate(land_refs):
        for k in range(1, N_DEV):
            peer = (x ^ (k >> 2), y ^ ((k >> 1) & 1), c ^ (k & 1))
            n = i * PEER_COPIES + k - 1
            copies.append(pltpu.make_async_remote_copy(
                src_ref=ref.at[mine], dst_ref=ref.at[mine], send_sem=send_sems.at[n], recv_sem=recv_sems.at[n],
                device_id=peer, device_id_type=MESH))
    return copies


def _sibling_copies(g_refs, r_refs, send_sems, recv_sems):
    x, y, c = _place()
    return [pltpu.make_async_remote_copy(
        src_ref=g_refs[i].at[2 * k + (1 - c)], dst_ref=r_refs[i].at[k],
        send_sem=send_sems.at[SIBLING_COPIES * i + k], recv_sem=recv_sems.at[SIBLING_COPIES * i + k],
        device_id=(x, y, 1 - c), device_id_type=MESH) for i in range(len(g_refs)) for k in range(SIBLING_COPIES)]


def _chip_copies(p_refs, r_refs, send_sems, recv_sems):
    x, y, c = _place()
    chips = [(1 - x, y), (x, 1 - y), (1 - x, 1 - y)]
    return [pltpu.make_async_remote_copy(
        src_ref=p_refs[i].at[2 * cx + cy], dst_ref=r_refs[i].at[j],
        send_sem=send_sems.at[CHIP_COPIES * i + j], recv_sem=recv_sems.at[CHIP_COPIES * i + j],
        device_id=(cx, cy, c), device_id_type=MESH) for i in range(len(p_refs)) for j, (cx, cy) in enumerate(chips)]


def _in_hbm(a):
    return pltpu.with_memory_space_constraint(a, pltpu.HBM)


def _exchange_start(name, copies_fn, n_copies, srcs, lands, after=()):
    n_s, n_l, n_a = len(srcs), len(lands), len(after)

    def body(*refs):
        outs = refs[n_s + n_l + n_a:]
        for cp in copies_fn(refs[:n_s], refs[n_s:n_s + n_l], outs[0], outs[1]):
            cp.start()
        outs[-1][...] = jnp.zeros_like(outs[-1])

    res = pl.pallas_call(
        body, name=name,
        out_shape=[pltpu.SemaphoreType.DMA((n_copies,)), pltpu.SemaphoreType.DMA((n_copies,))]
                  + [pltpu.HBM(t.shape, t.dtype) for t in (*srcs, *lands)] + [jax.ShapeDtypeStruct((8, LANES), F32)],
        in_specs=[HBM] * (n_s + n_l) + [ANY] * n_a,
        out_specs=[SEM, SEM] + [HBM] * (n_s + n_l) + [pl.BlockSpec(memory_space=pltpu.VMEM)],
        input_output_aliases={i: 2 + i for i in range(n_s + n_l)},
        compiler_params=pltpu.CompilerParams(has_side_effects=pltpu.SideEffectType.DATAFLOW_SIDE_EFFECTING),
    )(*[_in_hbm(t) for t in (*srcs, *lands)], *after)
    return res[0], res[1], list(res[2:2 + n_s]), list(res[2 + n_s:2 + n_s + n_l]), res[-1]


def _exchange_wait(name, copies_fn, started, after):
    send_sems, recv_sems, srcs, lands, _ = started
    n_s, n_l = len(srcs), len(lands)

    def body(*refs):
        for cp in copies_fn(refs[:n_s], refs[n_s:n_s + n_l], refs[n_s + n_l], refs[n_s + n_l + 1]):
            cp.wait_send()
            cp.wait_recv()

    res = pl.pallas_call(
        body, name=name,
        out_shape=[pltpu.HBM(t.shape, t.dtype) for t in (*srcs, *lands)],
        in_specs=[HBM] * (n_s + n_l) + [SEM, SEM, ANY],
        out_specs=[HBM] * (n_s + n_l),
        input_output_aliases={i: i for i in range(n_s + n_l)},
        compiler_params=pltpu.CompilerParams(has_side_effects=pltpu.SideEffectType.DATAFLOW_SIDE_EFFECTING),
    )(*srcs, *lands, send_sems, recv_sems, after)
    return list(res[:n_s]), list(res[n_s:])


def _shard_block(shape):
    rows, cols = shape
    if rows * cols <= 256 * 1024:
        return rows, cols
    if rows % 256:
        return rows, 256
    return 256, cols


def _chip_partial(ids, g_stack, recv1, name):
    shape = g_stack.shape[1:]
    br, bc = _shard_block(shape)

    def body(ids_ref, g_ref, r_ref, pb_ref, own_ref):
        s = g_ref[...] + r_ref[...]
        pb_ref[...] = s.astype(BF16)

        @pl.when(pl.program_id(2) == ids_ref[1])
        def _():
            own_ref[...] = s

    grid_spec = pltpu.PrefetchScalarGridSpec(
        num_scalar_prefetch=1, grid=(shape[0] // br, shape[1] // bc, 4),
        in_specs=[pl.BlockSpec((None, br, bc), lambda r, q, k, ids: (2 * k + ids[0], r, q)),
                  pl.BlockSpec((None, br, bc), lambda r, q, k, ids: (k, r, q))],
        out_specs=[pl.BlockSpec((None, br, bc), lambda r, q, k, ids: (k, r, q)),
                   pl.BlockSpec((br, bc), lambda r, q, k, ids: (r, q))])
    return pl.pallas_call(
        body, name=name, grid_spec=grid_spec,
        out_shape=[jax.ShapeDtypeStruct((4,) + shape, BF16), jax.ShapeDtypeStruct(shape, F32)],
        compiler_params=_cparams(("parallel", "parallel", "arbitrary")),
    )(ids, g_stack, recv1)


def _adamw(w, g, m, v):
    m = ADAM_B1 * m + (1.0 - ADAM_B1) * g
    v = ADAM_B2 * v + (1.0 - ADAM_B2) * (g * g)
    m_hat = m / (1.0 - ADAM_B1 ** ADAM_STEP)
    v_hat = v / (1.0 - ADAM_B2 ** ADAM_STEP)
    delta = -ADAM_LR * (m_hat / (jnp.sqrt(v_hat) + ADAM_EPS) + ADAM_WD * w)
    return delta, m, v


def _reduce_adamw(own, recv2, w, m, v, name, deps=()):
    shape = own.shape
    br, bc = _shard_block(shape)

    def body(own_ref, r_ref, w_ref, m_ref, v_ref, *refs):
        g_ref, d_ref, nm_ref, nv_ref = refs[len(deps):]
        g = own_ref[...]
        for j in range(3):
            g = g + r_ref[j].astype(F32)
        delta, nm, nv = _adamw(w_ref[...], g, m_ref[...], v_ref[...])
        g_ref[...] = g
        d_ref[...] = delta
        nm_ref[...] = nm
        nv_ref[...] = nv

    blk = pl.BlockSpec((br, bc), lambda r, q: (r, q))
    return pl.pallas_call(
        body, name=name, grid=(shape[0] // br, shape[1] // bc),
        in_specs=[blk, pl.BlockSpec((3, br, bc), lambda r, q: (0, r, q)), blk, blk, blk] + [ANY] * len(deps),
        out_specs=[blk] * 4, out_shape=[jax.ShapeDtypeStruct(shape, F32)] * 4,
        compiler_params=_cparams(("parallel", "parallel")),
    )(own, recv2, w, m, v, *deps)


def _small_allreduce_adamw(gvec, w, m, v):
    def body(g_ref, w_ref, m_ref, v_ref, go_ref, d_ref, nm_ref, nv_ref, buf, send_sems, recv_sems):
        x, y, c = _place()
        my_slot = 4 * x + 2 * y + c
        buf[my_slot] = g_ref[...]
        copies = []
        for k in range(1, N_DEV):
            px, py, pc = x ^ (k >> 2), y ^ ((k >> 1) & 1), c ^ (k & 1)
            copies.append(pltpu.make_async_remote_copy(
                src_ref=g_ref, dst_ref=buf.at[my_slot], send_sem=send_sems.at[k - 1], recv_sem=recv_sems.at[k - 1],
                device_id=(px, py, pc), device_id_type=MESH))
        for cp in copies:
            cp.start()
        for k in range(1, N_DEV):
            px, py, pc = x ^ (k >> 2), y ^ ((k >> 1) & 1), c ^ (k & 1)
            pltpu.make_async_remote_copy(
                src_ref=g_ref, dst_ref=buf.at[4 * px + 2 * py + pc], send_sem=send_sems.at[k - 1],
                recv_sem=recv_sems.at[k - 1], device_id=(px, py, pc), device_id_type=MESH).wait_recv()
        for cp in copies:
            cp.wait_send()
        g = buf[0]
        for s in range(1, N_DEV):
            g = g + buf[s]
        delta, nm, nv = _adamw(w_ref[...], g, m_ref[...], v_ref[...])
        go_ref[...] = g
        d_ref[...] = delta
        nm_ref[...] = nm
        nv_ref[...] = nv

    vm = pl.BlockSpec(memory_space=pltpu.VMEM)
    return pl.pallas_call(
        body, name="small_allreduce_adamw",
        in_specs=[vm] * 4, out_specs=[vm] * 4,
        out_shape=[jax.ShapeDtypeStruct((SMALL_R, LANES), F32)] * 4,
        scratch_shapes=[pltpu.VMEM((N_DEV, SMALL_R, LANES), F32),
                        pltpu.SemaphoreType.DMA((N_DEV - 1,)), pltpu.SemaphoreType.DMA((N_DEV - 1,))],
    )(gvec, w, m, v)


def _cols_to_whole(stack):
    rows = stack.shape[1]
    return stack.transpose(1, 0, 2).reshape(rows, -1)


def _whole_to_cols(t):
    rows = t.shape[0]
    return t.reshape(rows, N_DEV, -1).transpose(1, 0, 2)


W_IN_SEGMENTS = ((0, 3 * FOX_W, SEG_A), (3 * FOX_W, 3 * FOX_W + FOX_H, SEG_F),
                 (3 * FOX_W + FOX_H, 3 * FOX_W + FOX_H + 3 * DIL_W, SEG_B), (3 * FOX_W + FOX_H + 3 * DIL_W, D_IN, SEG_G))


def _shard_pieces(p):
    rows = D_IN // N_DEV
    lo, hi = p * rows, (p + 1) * rows
    return [(max(lo, a) - lo, min(hi, b) - lo, pad + max(lo, a) - a)
            for a, b, pad in W_IN_SEGMENTS if max(lo, a) < min(hi, b)]


def _unpad_runs(p):
    rows = D_IN // N_DEV
    blocks = []
    for lo in range(0, rows, PAD_BLK):
        hi = min(lo + PAD_BLK, rows)
        blocks.append([(max(lo, a) - lo, min(hi, b) - lo, pad + max(lo, a) - a)
                       for a, b, pad in _shard_pieces(p) if max(lo, a) < min(hi, b)])
    return blocks


def _unpad_dw_in_t(dwp):
    rows = D_IN // N_DEV
    plans = [_unpad_runs(p) for p in range(N_DEV)]
    windows = [[min(src // 8 * 8, D_PAD - W_WIN) for runs in plan for _, _, src in runs] for plan in plans]
    n_win = max(len(w) for w in windows)

    def body(src_ref, o_ref, buf, sems):
        p = pl.program_id(0)
        row = lax.broadcasted_iota(jnp.int32, (PAD_BLK, D), 0)

        def fetch(q):
            return [pltpu.make_async_copy(src_ref.at[pl.ds(start, W_WIN)], buf.at[q % 2, n], sems.at[q % 2, n])
                    for n, start in enumerate(windows[q])]

        for q in range(N_DEV):
            @pl.when(p == q)
            def _(q=q):
                for nxt in ([0, 1] if q == 0 else [q + 1]):
                    if nxt < N_DEV:
                        for cp in fetch(nxt):
                            cp.start()
                copies, n = fetch(q), 0
                for j, runs in enumerate(plans[q]):
                    out = None
                    for o_lo, o_hi, src in runs:
                        copies[n].wait()
                        moved = pltpu.roll(buf[q % 2, n], (o_lo - (src - windows[q][n])) % W_WIN, 0)[:PAD_BLK]
                        out = moved if out is None else jnp.where((row >= o_lo) & (row < o_hi), moved, out)
                        n += 1
                    size = min(PAD_BLK, rows - j * PAD_BLK)
                    o_ref[j * PAD_BLK:j * PAD_BLK + size, :] = out[:size]

    return pl.pallas_call(
        body, name="unpad_dw_in", grid=(N_DEV,),
        in_specs=[ANY], out_specs=pl.BlockSpec((None, rows, D), lambda p: (p, 0, 0)),
        out_shape=jax.ShapeDtypeStruct((N_DEV, rows, D), F32),
        scratch_shapes=[pltpu.VMEM((2, n_win, W_WIN, D), F32), pltpu.SemaphoreType.DMA((2, n_win))],
        compiler_params=_cparams(("arbitrary",)),
    )(dwp)


LOSS_ROW = 25


def _pack_small(g_attn, b, g_mlp, g_final, loss_row=None):
    tail = jnp.pad(b, ((0, 7), (0, LANES - b.shape[1])))
    if loss_row is not None:
        tail = tail + jnp.pad(loss_row, ((LOSS_ROW - 24, 31 - LOSS_ROW), (0, 0)))
    return jnp.concatenate([g_attn.reshape(8, LANES), g_mlp.reshape(8, LANES), g_final.reshape(8, LANES), tail], axis=0)


def _unpack_small(p):
    return (p[0:8].reshape(1, D), p[24:25, :FOX_H], p[8:16].reshape(1, D), p[16:24].reshape(D))


class _StepComm:
    def __init__(self, ids, w_sh, m_sh, v_sh, lands, after_w_in):
        self.ids = ids
        self.w_sh, self.m_sh, self.v_sh = w_sh, m_sh, v_sh
        self.updates = None
        self.gather_mixer = _exchange_start("gather_mixer_start", _peer_copies, PEER_COPIES * 3, [], lands[:3],
                                            after=(after_w_in,))
        self.gather_mlp = _exchange_start("gather_mlp_start", _peer_copies, PEER_COPIES * 2, [], lands[3:],
                                          after=(self.gather_mixer[-1],))
        self.sibling = {}
        self.chips = {}
        self.own = {}
        self.names = {}

    def _reduce_start(self, group, names, grads, after=()):
        lands = [lax.empty((SIBLING_COPIES,) + t.shape[1:], F32) for t in grads]
        self.sibling[group] = _exchange_start("grad_%s_sibling_start" % group, _sibling_copies,
                                              SIBLING_COPIES * len(grads), grads, lands, after=after)
        self.names[group] = names
        return self.sibling[group][-1]

    def _reduce_mid(self, group, after):
        grads, recv1 = _exchange_wait("grad_%s_sibling_wait" % group, _sibling_copies, self.sibling[group], after)
        parts = [_chip_partial(self.ids, g, r, "grad_partial_" + n) for g, r, n in zip(grads, recv1, self.names[group])]
        self.own[group] = [p[1] for p in parts]
        srcs = [p[0] for p in parts]
        lands = [lax.empty((CHIP_COPIES,) + t.shape[1:], BF16) for t in srcs]
        self.chips[group] = _exchange_start("grad_%s_chips_start" % group, _chip_copies, CHIP_COPIES * len(srcs),
                                            srcs, lands)
        return self.chips[group][-1]

    def reduced(self, group, after):
        _, recv2 = _exchange_wait("grad_%s_chips_wait" % group, _chip_copies, self.chips[group], after)
        return list(zip(self.own[group], recv2))

    def first_deps(self):
        return [self.gather_mlp[-1]]

    def mixer_weights(self, after):
        _, (g_a, g_b, g_out) = _exchange_wait("gather_mixer_wait", _peer_copies, self.gather_mixer, after)
        return _cols_to_whole(g_a), _cols_to_whole(g_b), g_out.reshape(D, D)

    def mlp_weights(self, after):
        _, (g_up, g_down) = _exchange_wait("gather_mlp_wait", _peer_copies, self.gather_mlp, after)
        return g_up, g_down.reshape(DFF, D)

    def mlp_grads(self, dw_up_sh, dw_down):
        return [self._reduce_start("mlp", W_NAMES[4:], [dw_up_sh, dw_down.reshape((N_DEV,) + SHARD_SHAPES[5])])]

    def mixer_grads(self, dw_a, dw_b, dw_out):
        token = self._reduce_mid("mlp", dw_b)
        grads = [_whole_to_cols(dw_a), _whole_to_cols(dw_b), dw_out.reshape((N_DEV,) + SHARD_SHAPES[3])]
        return [self._reduce_start("mixer", W_NAMES[1:4], grads, after=(token,))]

    def mid_backward(self, after):
        return [self._reduce_mid("mixer", after)]

    def w_in_grad(self, dw_in_t):
        grads = [_unpad_dw_in_t(dw_in_t)]
        token = self._reduce_start("w_in", W_NAMES[:1], grads)
        reduced = self.reduced("mixer", token) + self.reduced("mlp", token)
        self.updates, last = [None] * len(reduced), token
        for i in (3, 4, 0, 1, 2):
            if i == 0:
                last = self._reduce_mid("w_in", last)
            self.updates[i] = _reduce_adamw(*reduced[i], self.w_sh[1 + i], self.m_sh[1 + i], self.v_sh[1 + i],
                                            "adamw_" + W_NAMES[1 + i], deps=[last])
            last = self.updates[i][0]
        return [last]

    def w_in_update(self, after):
        (reduced,) = self.reduced("w_in", after)
        return _reduce_adamw(*reduced, self.w_sh[0], self.m_sh[0], self.v_sh[0], "adamw_" + W_NAMES[0])


def kernel(x, norm_attn_g, w_in, b_forget, w_branch_a, w_branch_b, w_out, norm_mlp_g, w_up, w_down, norm_final_g, loss_target, m_norm_attn_g, m_w_in, m_b_forget, m_w_branch_a, m_w_branch_b, m_w_out, m_norm_mlp_g, m_w_up, m_w_down, m_norm_final_g, v_norm_attn_g, v_w_in, v_b_forget, v_w_branch_a, v_w_branch_b, v_w_out, v_norm_mlp_g, v_w_up, v_w_down, v_norm_final_g):
    cx, cy, cc = _place()
    ids = jnp.stack([cc, 2 * cx + cy]).astype(jnp.int32)

    w_sh = [w_in[0].T] + [t[0] for t in (w_branch_a, w_branch_b, w_out, w_up, w_down)]
    m_sh = [m_w_in[0].T] + [t[0] for t in (m_w_branch_a, m_w_branch_b, m_w_out, m_w_up, m_w_down)]
    v_sh = [v_w_in[0].T] + [t[0] for t in (v_w_branch_a, v_w_branch_b, v_w_out, v_w_up, v_w_down)]

    w_in_t, h1, *lands = _gather_w_in(jnp.pad(w_sh[0], ((0, W_SLAB - w_sh[0].shape[0]), (0, 0))), x[0], norm_attn_g,
                                      w_sh[1:])
    comm = _StepComm(ids, w_sh, m_sh, v_sh, lands, w_in_t)
    b_pad = jnp.pad(b_forget, ((0, 0), (0, LANES - FOX_H)))

    loss_row, dx, dsmall = _local_step(
        x[0], loss_target[0], h1, norm_attn_g, norm_mlp_g, norm_final_g.reshape(1, D), b_pad, w_in_t, comm)

    dg_attn, db, dg_mlp, dg_final = dsmall
    small = _small_allreduce_adamw(
        _pack_small(dg_attn, db, dg_mlp, dg_final, loss_row),
        _pack_small(norm_attn_g, b_forget, norm_mlp_g, norm_final_g.reshape(1, D)),
        _pack_small(m_norm_attn_g, m_b_forget, m_norm_mlp_g, m_norm_final_g.reshape(1, D)),
        _pack_small(v_norm_attn_g, v_b_forget, v_norm_mlp_g, v_norm_final_g.reshape(1, D)))
    big = [comm.w_in_update(small[0])] + comm.updates

    outs = [small[0][LOSS_ROW, 0], dx[None]]
    for q in range(4):
        s_attn, s_b, s_mlp, s_final = _unpack_small(small[q])
        b_in, b_a, b_b, b_out, b_up, b_down = [t[q][None] for t in big]
        b_in = jnp.swapaxes(b_in, 1, 2)
        outs += [s_attn, b_in, s_b, b_a, b_b, b_out, s_mlp, b_up, b_down, s_final]
    return tuple(outs)
```

```python
import functools

import jax
import jax.numpy as jnp
from jax import lax
from jax.experimental import pallas as pl
from jax.experimental.pallas import tpu as pltpu

F32 = jnp.float32
BF16 = jnp.bfloat16
MESH = pl.DeviceIdType.MESH

S = 2048
D = 1024
HD = 64
FOX_H = 8
FOX_W = FOX_H * HD
DIL_HG = 4
DIL_G = 3
DIL_W = DIL_G * DIL_HG * HD
DIL_OUT = DIL_HG * HD
DIL_BLK = 128
DIL_R = (1, 4, 16)
DFF = 4 * D
D_IN = 3 * FOX_W + FOX_H + 3 * DIL_W + 2 * D
EPS = 1e-6
NEG_INF = -1e30
SCALE = HD ** -0.5
ROPE_THETA = 500000.0
ROPE_DIM = HD // 4
N_DEV = 8

ADAM_LR = 0.001
ADAM_B1 = 0.9
ADAM_B2 = 0.999
ADAM_EPS = 1e-08
ADAM_WD = 0.01
ADAM_STEP = 10

LANES = 128
CB = 256
SEG_A = 0
SEG_B = 3 * FOX_W
SEG_Z = SEG_B + 3 * DIL_W
SEG_G = SEG_Z + CB
SEG_F = SEG_G + 2 * D
D_PAD = SEG_F + CB
DW_IN_TM = D_PAD // 10
SMALL_R = 32

VMEM_MB = 56


def _cparams(dims=None, vmem_mb=VMEM_MB, **kw):
    return pltpu.CompilerParams(dimension_semantics=dims, vmem_limit_bytes=vmem_mb << 20, **kw)


ANY = pl.BlockSpec(memory_space=pl.ANY)

_NN = (((1,), (0,)), ((), ()))
_NT = (((1,), (1,)), ((), ()))
_TN = (((0,), (0,)), ((), ()))


def _dot(a, b, dims):
    return lax.dot_general(a.astype(BF16), b.astype(BF16), dims, preferred_element_type=F32)


def _mm(a, b, *, mode, tm, tn, out_dtypes, name, n=None, k=None, a_off=0, b_off=0,
        b_sharded=False, out_sharded=False, epilogue=None, extras=(), deps=()):
    n_sh = b.shape[-1] if b_sharded else None
    if mode == "nn":
        m = a.shape[0]
        k = k or a.shape[1]
        a_spec = pl.BlockSpec((tm, k), lambda i, j: (i, a_off))
        if b_sharded:
            assert tn == n_sh
            n = N_DEV * n_sh
            b_spec = pl.BlockSpec((None, k, tn), lambda i, j: (j, 0, 0))
        else:
            n = n or b.shape[1]
            b_spec = pl.BlockSpec((k, tn), lambda i, j: (0, j + b_off))
        dims = _NN
    elif mode == "nt":
        m = a.shape[0]
        k = k or a.shape[1]
        a_spec = pl.BlockSpec((tm, k), lambda i, j: (i, a_off))
        if b_sharded:
            n = b.shape[1]
            b_spec = pl.BlockSpec((N_DEV, tn, n_sh), lambda i, j: (0, j, 0))
        else:
            n = n or b.shape[0]
            b_spec = pl.BlockSpec((tn, k), lambda i, j: (j + b_off, 0))
        dims = _NT
    else:
        k, m = a.shape
        n = n or b.shape[1]
        a_spec = pl.BlockSpec((k, tm), lambda i, j: (0, i))
        b_spec = pl.BlockSpec((k, tn), lambda i, j: (0, j + b_off))
        dims = _TN
    assert m % tm == 0 and n % tn == 0, (name, m, n, tm, tn)
    n_extra = len(extras)
    tile = pl.BlockSpec((tm, tn), lambda i, j: (i, j))
    if out_sharded:
        assert mode == "tn" and n // tn == N_DEV
        out_spec = pl.BlockSpec((None, tm, tn), lambda i, j: (j, i, 0))
        out_shape = (N_DEV, m, tn)
    else:
        out_spec, out_shape = tile, (m, n)

    def body(a_ref, b_ref, *refs):
        if mode == "nt" and b_sharded:
            acc = _dot(a_ref[:, 0:n_sh], b_ref[0], dims)
            for p in range(1, N_DEV):
                acc = acc + _dot(a_ref[:, p * n_sh:(p + 1) * n_sh], b_ref[p], dims)
        else:
            acc = _dot(a_ref[...], b_ref[...], dims)
        ex = [r[...] for r in refs[:n_extra]]
        outs = epilogue(acc, *ex) if epilogue is not None else (acc,)
        for o_ref, o in zip(refs[n_extra + len(deps):], outs):
            o_ref[...] = o.astype(o_ref.dtype)

    res = pl.pallas_call(
        body, name=name, grid=(m // tm, n // tn),
        in_specs=[a_spec, b_spec] + [tile] * n_extra + [ANY] * len(deps),
        out_specs=[out_spec] * len(out_dtypes),
        out_shape=[jax.ShapeDtypeStruct(out_shape, dt) for dt in out_dtypes],
        compiler_params=_cparams(("parallel", "parallel")),
    )(a, b, *extras, *deps)
    return res if len(out_dtypes) > 1 else res[0]


def _mm_rows(a, b, *, tm, name, epilogue, rows=(), vecs=(), row_out=(), vec_out=(), b_sharded=False, deps=()):
    m, k = a.shape
    n_rows, n_vecs, n_deps = len(rows), len(vecs), len(deps)
    n_sh = b.shape[-1] if b_sharded else None

    def body(a_ref, b_ref, *refs):
        row_refs, vec_refs = refs[:n_rows], refs[n_rows:n_rows + n_vecs]
        outs = refs[n_rows + n_vecs + n_deps:]
        if b_sharded:
            acc = _dot(a_ref[:, 0:n_sh], b_ref[0], _NT)
            for p in range(1, N_DEV):
                acc = acc + _dot(a_ref[:, p * n_sh:(p + 1) * n_sh], b_ref[p], _NT)
        else:
            acc = _dot(a_ref[...], b_ref[...], _NN)
        row_vals, vec_incs = epilogue(acc, [r[...] for r in row_refs], [v[...] for v in vec_refs])
        for o_ref, val in zip(outs[:len(row_out)], row_vals):
            o_ref[...] = val.astype(o_ref.dtype)

        @pl.when(pl.program_id(0) == 0)
        def _():
            for o_ref in outs[len(row_out):]:
                o_ref[...] = jnp.zeros_like(o_ref)

        for o_ref, inc in zip(outs[len(row_out):], vec_incs):
            o_ref[...] += inc

    tile = pl.BlockSpec((tm, D), lambda i: (i, 0))
    b_spec = pl.BlockSpec(b.shape, lambda i: (0,) * b.ndim)
    return pl.pallas_call(
        body, name=name, grid=(m // tm,),
        in_specs=[pl.BlockSpec((tm, k), lambda i: (i, 0)), b_spec] + [tile] * n_rows
                 + [pl.BlockSpec(v.shape, lambda i: (0, 0)) for v in vecs] + [ANY] * n_deps,
        out_specs=[tile] * len(row_out) + [pl.BlockSpec((1, w), lambda i: (0, 0)) for w in vec_out],
        out_shape=[jax.ShapeDtypeStruct((m, D), dt) for dt in row_out]
                  + [jax.ShapeDtypeStruct((1, w), F32) for w in vec_out],
        compiler_params=_cparams(("arbitrary",)),
    )(a, b, *rows, *vecs, *deps)


ROW_T = 256
ROWS_TM = 512


def _rms_rows(x, g):
    r = lax.rsqrt(jnp.mean(x * x, axis=-1, keepdims=True) + EPS)
    return (x * r) * g


def _rms_bwd_rows(dh, x, dres, g):
    r = lax.rsqrt(jnp.mean(x * x, axis=-1, keepdims=True) + EPS)
    xn = x * r
    dhn = dh * g
    dx = dres + r * (dhn - xn * jnp.mean(dhn * xn, axis=-1, keepdims=True))
    return dx, jnp.sum(dh * xn, axis=0, keepdims=True)


def _final_loss_rows(x, g, tgt):
    r = lax.rsqrt(jnp.mean(x * x, axis=-1, keepdims=True) + EPS)
    xn = x * r
    err = xn * g - tgt
    row_loss = jnp.mean(err * err, axis=-1, keepdims=True)
    loss = 0.5 * jnp.sum(row_loss, axis=0, keepdims=True) * jnp.ones((1, LANES), F32)
    dy = err * (1.0 / D)
    dyn = dy * g
    dx = r * (dyn - xn * jnp.mean(dyn * xn, axis=-1, keepdims=True))
    return dx, jnp.sum(dy * xn, axis=0, keepdims=True), loss


def _sigmoid(z):
    return 1.0 / (1.0 + jnp.exp(-z))


GATE_TR = 512


def _mixer_tail(oa, ob, w_a, w_b, w_out, proj_g, x, g_mlp):
    def body(oa_ref, ob_ref, wa_ref, wb_ref, wo_ref, g_ref, x_ref, gm_ref, ya_ref, yb_ref, mixed_ref, x2_ref, h2_ref):
        ya = _dot(oa_ref[...], wa_ref[...], _NN)
        yb = _dot(ob_ref[...], wb_ref[...], _NN)
        ya_ref[...] = ya
        yb_ref[...] = yb
        mixed = (_sigmoid(g_ref[:, :D]) * ya + _sigmoid(g_ref[:, D:]) * yb).astype(BF16)
        mixed_ref[...] = mixed
        x2 = x_ref[...] + _dot(mixed, wo_ref[...], _NN)
        x2_ref[...] = x2
        h2_ref[...] = _rms_rows(x2, gm_ref[...]).astype(BF16)

    def rows(width):
        return pl.BlockSpec((GATE_TR, width), lambda i: (i, 0))

    def whole(t):
        return pl.BlockSpec(t.shape, lambda i: (0, 0))

    return pl.pallas_call(
        body, name="mixer_tail", grid=(S // GATE_TR,),
        in_specs=[rows(FOX_W), rows(DIL_OUT), whole(w_a), whole(w_b), whole(w_out), rows(2 * D), rows(D), whole(g_mlp)],
        out_specs=[rows(D)] * 5,
        out_shape=[jax.ShapeDtypeStruct((S, D), dt) for dt in (F32, F32, BF16, F32, BF16)],
        compiler_params=_cparams(("parallel",)),
    )(oa, ob, w_a, w_b, w_out, proj_g, x, g_mlp)


def _out_proj_gate_bwd(dx2, w_out, proj_g, ya, yb):
    def body(dx_ref, w_ref, g_ref, ya_ref, yb_ref, dy_ref, dg_ref):
        dm = _dot(dx_ref[...], w_ref[...], _NT)
        for half, y_ref in enumerate((ya_ref, yb_ref)):
            cols = slice(half * D, (half + 1) * D)
            s = _sigmoid(g_ref[:, cols])
            dy_ref[:, cols] = (dm * s).astype(BF16)
            dg_ref[:, cols] = (dm * y_ref[...] * (s * (1.0 - s))).astype(BF16)

    row = pl.BlockSpec((GATE_TR, D), lambda i: (i, 0))
    wide = pl.BlockSpec((GATE_TR, 2 * D), lambda i: (i, 0))
    return pl.pallas_call(
        body, name="out_proj_gate_bwd", grid=(S // GATE_TR,),
        in_specs=[row, pl.BlockSpec((D, D), lambda i: (0, 0)), wide, row, row],
        out_specs=[wide, pl.BlockSpec((GATE_TR, 2 * D), lambda i: (i, SEG_G // (2 * D)))],
        out_shape=[jax.ShapeDtypeStruct((S, 2 * D), BF16), jax.ShapeDtypeStruct((S, D_PAD), BF16)],
        compiler_params=_cparams(("parallel",)),
    )(dx2, w_out, proj_g, ya, yb)


FOX_TQ = 512
FOX_TQ_FWD = 1024


def _scan_rows(x, reverse):
    n = x.shape[0]
    row = lax.broadcasted_iota(jnp.int32, x.shape, 0)
    k = 1
    while k < n:
        if reverse:
            x = x + jnp.where(row < n - k, pltpu.roll(x, n - k, 0), 0.0)
        else:
            x = x + jnp.where(row >= k, pltpu.roll(x, k, 0), 0.0)
        k *= 2
    return x


def _fox_dscan(dft, dfs, proj_f, b_pad, dproj):
    def body(dft_ref, dfs_ref, f_ref, b_ref, _, dfa_ref, db_ref):
        dfs_pad = jnp.concatenate([dfs_ref[...], jnp.zeros((LANES - FOX_H, S), F32)], axis=0)
        df = dfs_pad.T + dft_ref[0]
        for hp in range(1, dft_ref.shape[0]):
            df = df + pltpu.roll(dft_ref[hp], 2 * hp, 1)
        dlf = _scan_rows(df, reverse=True)
        z = f_ref[...] + b_ref[...]
        lane = lax.broadcasted_iota(jnp.int32, (S, LANES), 1)
        dfa = jnp.where(lane < FOX_H, dlf / (1.0 + jnp.exp(z)), 0.0)
        dfa_ref[:, :LANES] = dfa.astype(BF16)
        dfa_ref[:, LANES:] = jnp.zeros((S, CB - LANES), BF16)
        db_ref[...] = jnp.sum(dfa, axis=0, keepdims=True)

    return pl.pallas_call(
        body, name="fox_dscan", grid=(1,),
        in_specs=[pl.BlockSpec(dft.shape, lambda i: (0, 0, 0)), pl.BlockSpec((FOX_H, S), lambda i: (0, 0)),
                  pl.BlockSpec((S, LANES), lambda i: (0, 0)), pl.BlockSpec((1, LANES), lambda i: (0, 0)), ANY],
        out_specs=[pl.BlockSpec((S, CB), lambda i: (0, SEG_F // CB)), pl.BlockSpec((1, LANES), lambda i: (0, 0))],
        out_shape=[jax.ShapeDtypeStruct((S, D_PAD), BF16), jax.ShapeDtypeStruct((1, LANES), F32)],
        input_output_aliases={4: 0},
        compiler_params=_cparams(("arbitrary",)),
    )(dft, dfs, proj_f, b_pad, dproj)


FOX_NQ = S // FOX_TQ
FOX_HP = FOX_W // LANES
HEADS_PER_LB = LANES // HD
FOX_AUG = 2 * LANES


def _fox_prepare(proj_a, proj_f, b_pad):
    tr = 512

    def body(q_ref, k_ref, f_ref, b_ref, qa_ref, ka_ref, hi_s, mid_s, lo_s):
        i = pl.program_id(0)

        @pl.when(i == 0)
        def _():
            z = f_ref[...] + b_ref[...]
            lf = jnp.minimum(z, 0.0) - jnp.log1p(jnp.exp(-jnp.abs(z)))
            f = _scan_rows(lf, reverse=False)
            hi = f.astype(BF16).astype(F32)
            r1 = f - hi
            mid = r1.astype(BF16).astype(F32)
            hi_s[...] = hi
            mid_s[...] = mid
            lo_s[...] = (r1 - mid).astype(BF16).astype(F32)

        rows = pl.ds(pl.multiple_of(i * tr, tr), tr)
        hi, mid, lo = hi_s[rows, :], mid_s[rows, :], lo_s[rows, :]
        lane = lax.broadcasted_iota(jnp.int32, (tr, HD), 1)
        q_ones = jnp.where((lane >= 3) & (lane < 6), 1.0, 0.0)
        k_ones = jnp.where(lane < 3, 1.0, 0.0)
        for h in range(FOX_H):
            a1, a2, a3 = hi[:, h:h + 1], mid[:, h:h + 1], lo[:, h:h + 1]
            q_extra = jnp.where(lane == 0, a1, jnp.where(lane == 1, a2, jnp.where(lane == 2, a3, q_ones)))
            k_extra = jnp.where(lane == 3, -a1, jnp.where(lane == 4, -a2, jnp.where(lane == 5, -a3, k_ones)))
            base = h * LANES
            qa_ref[:, base:base + HD] = q_ref[:, h * HD:(h + 1) * HD] * jnp.asarray(SCALE, BF16)
            qa_ref[:, base + HD:base + LANES] = q_extra.astype(BF16)
            ka_ref[:, base:base + HD] = k_ref[:, h * HD:(h + 1) * HD]
            ka_ref[:, base + HD:base + LANES] = k_extra.astype(BF16)

    out_blk = pl.BlockSpec((tr, FOX_H * LANES), lambda i: (i, 0))
    return pl.pallas_call(
        body, name="fox_prepare", grid=(S // tr,),
        in_specs=[pl.BlockSpec((tr, FOX_W), lambda i: (i, 0)), pl.BlockSpec((tr, FOX_W), lambda i: (i, 1)),
                  pl.BlockSpec((S, LANES), lambda i: (0, 0)), pl.BlockSpec((1, LANES), lambda i: (0, 0))],
        out_specs=[out_blk, out_blk],
        out_shape=[jax.ShapeDtypeStruct((S, FOX_H * LANES), BF16)] * 2,
        scratch_shapes=[pltpu.VMEM((S, LANES), F32)] * 3,
        compiler_params=_cparams(("arbitrary",)),
    )(proj_a, proj_a, proj_f, b_pad)


def _fox_scores(qa, ka, w, tq):
    s = _dot(qa, ka, _NT)
    row = lax.broadcasted_iota(jnp.int32, (tq, tq), 0)
    col = lax.broadcasted_iota(jnp.int32, (tq, tq), 1)
    diag = jnp.where(col <= row, s[:, w * tq:], NEG_INF)
    return diag if w == 0 else jnp.concatenate([s[:, :w * tq], diag], axis=1)


def _fox_fwd(q_aug, k_aug, proj_a):
    tq = FOX_TQ_FWD

    def body(qa_ref, ka_ref, v_ref, o_ref, lse_ref):
        qi = pl.program_id(1)
        lane = lax.broadcasted_iota(jnp.int32, (tq, LANES), 1)
        for w in range(S // tq):
            @pl.when(qi == w)
            def _(w=w):
                width = (w + 1) * tq
                lse = jnp.zeros((tq, LANES), F32)
                for hh in range(HEADS_PER_LB):
                    aug = slice(hh * LANES, (hh + 1) * LANES)
                    sl = slice(hh * HD, (hh + 1) * HD)
                    s = _fox_scores(qa_ref[:, aug], ka_ref[:width, aug], w, tq)
                    m = jnp.max(s, axis=-1, keepdims=True)
                    p = jnp.exp(s - m)
                    l = jnp.sum(p, axis=-1, keepdims=True)
                    o_ref[:, sl] = _dot(p / l, v_ref[:width, sl], _NN)
                    lse = jnp.where(lane == hh, m + jnp.log(l), lse)
                lse_ref[...] = lse

    return pl.pallas_call(
        body, name="fox_fwd", grid=(FOX_HP, S // tq),
        in_specs=[pl.BlockSpec((tq, FOX_AUG), lambda h, i: (i, h)),
                  pl.BlockSpec((S, FOX_AUG), lambda h, i: (0, h)),
                  pl.BlockSpec((S, LANES), lambda h, i: (0, 2 * FOX_HP + h))],
        out_specs=[pl.BlockSpec((tq, LANES), lambda h, i: (i, h)),
                   pl.BlockSpec((None, tq, LANES), lambda h, i: (h, i, 0))],
        out_shape=[jax.ShapeDtypeStruct((S, FOX_W), F32), jax.ShapeDtypeStruct((FOX_HP, S, LANES), F32)],
        compiler_params=_cparams(("parallel", "parallel")),
    )(q_aug, k_aug, proj_a)


def _fox_bwd(q_aug, k_aug, proj_a, lse, do, dproj, deps=()):
    n_q = FOX_NQ

    def body(qa_ref, ka_ref, v_ref, lse_ref, do_ref, *refs):
        dproj_ref, dft_ref, dfs_ref, dk_acc, dv_acc, dq_buf, kv_buf, dq_sems, kv_sems = refs[1 + len(deps):]
        hp = pl.program_id(0)
        t = pl.program_id(1)
        slot = t % 2
        col = pl.multiple_of(hp * LANES, LANES)

        def dq_copy(step, buf_slot):
            rows = pl.ds(pl.multiple_of(step * FOX_TQ, FOX_TQ), FOX_TQ)
            return pltpu.make_async_copy(dq_buf.at[buf_slot], dproj_ref.at[rows, pl.ds(col, LANES)], dq_sems.at[buf_slot])

        @pl.when(t == 0)
        def _():
            dk_acc[...] = jnp.zeros_like(dk_acc)
            dv_acc[...] = jnp.zeros_like(dv_acc)

        @pl.when((t == 0) & (hp == 0))
        def _():
            dfs_ref[...] = jnp.zeros_like(dfs_ref)

        @pl.when(t >= 2)
        def _():
            dq_copy(t - 2, slot).wait()

        lane = lax.broadcasted_iota(jnp.int32, (FOX_TQ, LANES), 1)
        for w in range(n_q):
            @pl.when(t == w)
            def _(w=w):
                width = (w + 1) * FOX_TQ
                lse_all = lse_ref[...]
                sub = lax.broadcasted_iota(jnp.int32, (FOX_H, width), 0)
                dft = jnp.zeros((FOX_TQ, LANES), F32)
                for hh in range(HEADS_PER_LB):
                    aug = slice(hh * LANES, (hh + 1) * LANES)
                    sl = slice(hh * HD, (hh + 1) * HD)
                    head = hp * HEADS_PER_LB + hh
                    qa = qa_ref[:, aug]
                    ka = ka_ref[:width, aug]
                    do_h = do_ref[:, sl]
                    s = _fox_scores(qa, ka, w, FOX_TQ)
                    p = jnp.exp(s - lse_all[:, hh:hh + 1])
                    dp = _dot(do_h, v_ref[:width, sl], _NT)
                    ds = p * (dp - jnp.sum(dp * p, axis=-1, keepdims=True))
                    dft = jnp.where(lane == hh, jnp.sum(ds, axis=-1, keepdims=True), dft)
                    dfs_ref[:, :width] -= jnp.where(sub == head, jnp.sum(ds, axis=0, keepdims=True), 0.0)
                    dq_buf[slot, :, sl] = (_dot(ds, ka[:, :HD], _NN) * SCALE).astype(BF16)
                    dk_acc[:width, sl] += _dot(ds, qa[:, :HD], _TN)
                    dv_acc[:width, sl] += _dot(p, do_h, _TN)
                dft_ref[...] = dft

        dq_copy(t, slot).start()

        @pl.when(t == n_q - 1)
        def _():
            dq_copy(t - 1, 1 - slot).wait()
            dq_copy(t, slot).wait()
            kv_buf[0] = dk_acc[...].astype(BF16)
            kv_buf[1] = dv_acc[...].astype(BF16)
            copies = [pltpu.make_async_copy(
                kv_buf.at[j], dproj_ref.at[:, pl.ds(pl.multiple_of((j + 1) * FOX_W + hp * LANES, LANES), LANES)],
                kv_sems.at[j]) for j in range(2)]
            for cp in copies:
                cp.start()
            for cp in copies:
                cp.wait()

    qblk = pl.BlockSpec((FOX_TQ, FOX_AUG), lambda h, t: (t, h))
    lane_blk = pl.BlockSpec((None, FOX_TQ, LANES), lambda h, t: (h, t, 0))
    return pl.pallas_call(
        body, name="fox_bwd", grid=(FOX_HP, n_q),
        in_specs=[qblk, pl.BlockSpec((S, FOX_AUG), lambda h, t: (0, h)),
                  pl.BlockSpec((S, LANES), lambda h, t: (0, 2 * FOX_HP + h)),
                  lane_blk, pl.BlockSpec((FOX_TQ, LANES), lambda h, t: (t, h)), ANY] + [ANY] * len(deps),
        out_specs=[ANY, lane_blk, pl.BlockSpec((FOX_H, S), lambda h, t: (0, 0))],
        out_shape=[jax.ShapeDtypeStruct((S, D_PAD), BF16),
                   jax.ShapeDtypeStruct((FOX_HP, S, LANES), F32), jax.ShapeDtypeStruct((FOX_H, S), F32)],
        scratch_shapes=[pltpu.VMEM((S, LANES), F32), pltpu.VMEM((S, LANES), F32),
                        pltpu.VMEM((2, FOX_TQ, LANES), BF16), pltpu.VMEM((2, S, LANES), BF16),
                        pltpu.SemaphoreType.DMA((2,)), pltpu.SemaphoreType.DMA((2,))],
        input_output_aliases={5: 0},
        compiler_params=_cparams(("arbitrary", "arbitrary")),
    )(q_aug, k_aug, proj_a, lse, do, dproj, *deps)


def _rope_tables():
    half = ROPE_DIM // 2
    shape = (DIL_G, S, half)
    g = lax.broadcasted_iota(jnp.int32, shape, 0)
    row = lax.broadcasted_iota(jnp.int32, shape, 1)
    r = jnp.left_shift(1, 2 * g)
    per_class = S // r
    pos = (row % per_class) * r + row // per_class
    inv_freq = jnp.power(jnp.float32(ROPE_THETA), -lax.broadcasted_iota(F32, shape, 2) * 2.0 / ROPE_DIM)
    ang = pos.astype(F32) * inv_freq
    return jnp.concatenate([jnp.cos(ang), jnp.sin(ang)], axis=-1)


def _rope_factors(cs):
    half = ROPE_DIM // 2
    i = lax.broadcasted_iota(jnp.int32, (2 * half, 2 * LANES), 0)
    l = lax.broadcasted_iota(jnp.int32, (2 * half, 2 * LANES), 1)
    hit = ((l < LANES) & (i == l % half)) | ((l >= LANES) & (i - half == l % half))
    spread = jnp.where(hit, 1.0, 0.0).astype(BF16)
    hi = cs.astype(BF16)
    r1 = cs - hi.astype(F32)
    mid = r1.astype(BF16)
    lo = (r1 - mid.astype(F32)).astype(BF16)
    full = (_dot(hi, spread, _NN) + _dot(mid, spread, _NN)) + _dot(lo, spread, _NN)
    cos, sin = full[:, :LANES], full[:, LANES:]
    lane = lax.broadcasted_iota(jnp.int32, cos.shape, 1) % HD
    t_self = jnp.where(lane < ROPE_DIM, cos, 1.0)
    t_up = jnp.where(lane < half, -sin, 0.0)
    t_dn = jnp.where((lane >= half) & (lane < ROPE_DIM), sin, 0.0)
    return t_self, t_up, t_dn


def _residue_pieces(r):
    if r == 1:
        return [(slice(i, i + 512), slice(i, i + 512)) for i in range(0, S, 512)]
    n = S // r
    return [(pl.ds(j, n, stride=r), slice(j * n, (j + 1) * n)) for j in range(r)]


def _rope_apply(x, a, u, d, backward):
    half = ROPE_DIM // 2
    if backward:
        return x * a + pltpu.roll(x * u, half, 1) + pltpu.roll(x * d, LANES - half, 1)
    return x * a + pltpu.roll(x, LANES - half, 1) * u + pltpu.roll(x, half, 1) * d


def _rope_cache_factors(cs_ref, fac, first_step_of_group):
    @pl.when(first_step_of_group)
    def _():
        for lo in range(0, S, 512):
            for j, f in enumerate(_rope_factors(cs_ref[lo:lo + 512, :])):
                fac[j, lo:lo + 512, :] = f


LB_PER_CB = CB // LANES
ROPE_NB = 3 * DIL_G * LB_PER_CB


def _rope_step(s):
    per_group = 3 * LB_PER_CB
    return s // per_group, (s % per_group) // LB_PER_CB, s % LB_PER_CB


def _rope_split(proj_b, tables):
    def body(x_ref, cs_ref, o_ref, fac):
        g, t, hf = _rope_step(pl.program_id(0))
        _rope_cache_factors(cs_ref, fac, (t == 0) & (hf == 0))
        for gs in range(DIL_G):
            @pl.when(g == gs)
            def _(gs=gs):
                for tok, sub in _residue_pieces(DIL_R[gs]):
                    x = x_ref[tok, :]
                    y = _rope_apply(x, fac[0, sub, :], fac[1, sub, :], fac[2, sub, :], False)
                    o_ref[sub, :] = jnp.where(t < 2, y, x).astype(BF16)

    def in_index(s):
        g, t, hf = _rope_step(s)
        return 0, (t * DIL_G + g) * LB_PER_CB + hf

    def out_index(s):
        g, t, hf = _rope_step(s)
        return t * DIL_G + g, 0, hf

    tab = pl.BlockSpec((None, S, 2 * (ROPE_DIM // 2)), lambda s: (_rope_step(s)[0], 0, 0))
    return pl.pallas_call(
        body, name="rope_split", grid=(ROPE_NB,),
        in_specs=[pl.BlockSpec((S, LANES), in_index), tab],
        out_specs=pl.BlockSpec((None, S, LANES), out_index),
        out_shape=jax.ShapeDtypeStruct((3 * DIL_G, S, CB), BF16),
        scratch_shapes=[pltpu.VMEM((3, S, LANES), F32)],
        compiler_params=_cparams(("arbitrary",)),
    )(proj_b, tables)


def _rope_merge_bwd(dq3, dk3, dv3, tables, dproj):
    def body(dq_ref, dk_ref, dv_ref, cs_ref, _, o_ref, tmp, fac):
        s = pl.program_id(0)
        g, t, hf = _rope_step(s)
        _rope_cache_factors(cs_ref, fac, (t == 0) & (hf == 0) & (s < ROPE_NB))
        for gs in range(DIL_G):
            @pl.when((g == gs) & (s < ROPE_NB))
            def _(gs=gs):
                for tok, sub in _residue_pieces(DIL_R[gs]):
                    x = jnp.where(t == 0, dq_ref[sub, :], jnp.where(t == 1, dk_ref[sub, :], dv_ref[sub, :]))
                    y = _rope_apply(x, fac[0, sub, :], fac[1, sub, :], fac[2, sub, :], True)
                    tmp[tok, :] = jnp.where(t < 2, y, x)
                o_ref[...] = tmp[...].astype(BF16)

        @pl.when(s >= ROPE_NB)
        def _():
            o_ref[...] = jnp.zeros_like(o_ref)

    def src_spec(own):
        def index(s):
            g, t, hf = _rope_step(jnp.minimum(s, ROPE_NB - 1))
            return g, 0, jnp.where(t == own, hf, jnp.where(t > own, LB_PER_CB - 1, 0))
        return pl.BlockSpec((None, S, LANES), index)

    def out_index(s):
        g, t, hf = _rope_step(s)
        col = SEG_B // LANES + (t * DIL_G + g) * LB_PER_CB + hf
        return 0, jnp.where(s < ROPE_NB, col, SEG_Z // LANES + s - ROPE_NB)

    tab = pl.BlockSpec((None, S, 2 * (ROPE_DIM // 2)), lambda s: (_rope_step(jnp.minimum(s, ROPE_NB - 1))[0], 0, 0))
    return pl.pallas_call(
        body, name="rope_merge_bwd", grid=(ROPE_NB + LB_PER_CB,),
        in_specs=[src_spec(0), src_spec(1), src_spec(2), tab, ANY],
        out_specs=pl.BlockSpec((S, LANES), out_index),
        out_shape=jax.ShapeDtypeStruct((S, D_PAD), BF16),
        scratch_shapes=[pltpu.VMEM((S, LANES), F32), pltpu.VMEM((3, S, LANES), F32)],
        input_output_aliases={4: 0},
        compiler_params=_cparams(("arbitrary",)),
    )(dq3, dk3, dv3, tables, dproj)


DIL_NB = S // DIL_BLK


_BQK = (((2,), (2,)), ((0,), (0,)))
_BQD = (((2,), (1,)), ((0,), (0,)))
_BKD = (((1,), (1,)), ((0,), (0,)))


def _dil_mask(g):
    shape = (DIL_NB, DIL_BLK, 2 * DIL_BLK)
    blocks_per_seq = jnp.right_shift(DIL_NB, 2 * g)
    has_prev = jnp.bitwise_and(lax.broadcasted_iota(jnp.int32, shape, 0), blocks_per_seq - 1) != 0
    a = lax.broadcasted_iota(jnp.int32, shape, 1)
    kk = lax.broadcasted_iota(jnp.int32, shape, 2)
    diff = DIL_BLK + a - kk
    return (diff >= 0) & (diff <= DIL_BLK) & ((kk >= DIL_BLK) | has_prev)


def _blocks(t):
    return t.reshape(DIL_NB, DIL_BLK, t.shape[-1])


def _with_prev(t):
    prev = jnp.concatenate([jnp.zeros((DIL_BLK, t.shape[-1]), t.dtype), t[:-DIL_BLK]], axis=0)
    return jnp.concatenate([_blocks(prev), _blocks(t)], axis=1)


def _fold_prev(t2):
    n = t2.shape[-1]
    to_prev = t2[:, :DIL_BLK].reshape(S, n)
    own = t2[:, DIL_BLK:].reshape(S, n)
    return own + jnp.concatenate([to_prev[DIL_BLK:], jnp.zeros((DIL_BLK, n), t2.dtype)], axis=0)


def _dil_group_spec(t, width=DIL_OUT):
    return pl.BlockSpec((None, S, width), lambda g: (t * DIL_G + g, 0, 0))


def _dil_fwd(qkv3):
    def body(q_ref, k_ref, v_ref, o_ref, lse_ref):
        ok = _dil_mask(pl.program_id(0))
        lane = lax.broadcasted_iota(jnp.int32, (S, LANES), 1)
        lse = jnp.zeros((S, LANES), F32)
        for h in range(DIL_HG):
            sl = slice(h * HD, (h + 1) * HD)
            s = jnp.where(ok, _dot(_blocks(q_ref[:, sl]), _with_prev(k_ref[:, sl]), _BQK) * SCALE, NEG_INF)
            m = jnp.max(s, axis=-1, keepdims=True)
            p = jnp.exp(s - m)
            l = jnp.sum(p, axis=-1, keepdims=True)
            o_ref[:, sl] = _dot(p / l, _with_prev(v_ref[:, sl]), _BQD).reshape(S, HD)
            lse = jnp.where(lane == h, (m + jnp.log(l)).reshape(S, 1), lse)
        lse_ref[...] = lse

    return pl.pallas_call(
        body, name="dil_fwd", grid=(DIL_G,),
        in_specs=[_dil_group_spec(0), _dil_group_spec(1), _dil_group_spec(2)],
        out_specs=[_dil_group_spec(0), _dil_group_spec(0, LANES)],
        out_shape=[jax.ShapeDtypeStruct((DIL_G, S, DIL_OUT), F32), jax.ShapeDtypeStruct((DIL_G, S, LANES), F32)],
        compiler_params=_cparams(("parallel",)),
    )(qkv3, qkv3, qkv3)


def _dil_bwd(qkv3, lse3, do3, c3):
    def body(q_ref, k_ref, v_ref, lse_ref, do_ref, c_ref, dq_ref, dk_ref, dv_ref):
        ok = _dil_mask(pl.program_id(0))
        lse = lse_ref[...]
        c = c_ref[...]
        for h in range(DIL_HG):
            sl = slice(h * HD, (h + 1) * HD)
            q = _blocks(q_ref[:, sl])
            k2 = _with_prev(k_ref[:, sl])
            do_h = _blocks(do_ref[:, sl])
            s = jnp.where(ok, _dot(q, k2, _BQK) * SCALE, NEG_INF)
            p = jnp.exp(s - _blocks(lse[:, h:h + 1]))
            dp = _dot(do_h, _with_prev(v_ref[:, sl]), _BQK)
            ds = p * (dp - _blocks(c[:, h:h + 1]))
            dsb = (ds * SCALE).astype(BF16)
            dq_ref[:, sl] = _dot(dsb, k2, _BQD).reshape(S, HD)
            dk_ref[:, sl] = _fold_prev(_dot(dsb, q, _BKD))
            dv_ref[:, sl] = _fold_prev(_dot(p, do_h, _BKD))

    return pl.pallas_call(
        body, name="dil_bwd", grid=(DIL_G,),
        in_specs=[_dil_group_spec(0), _dil_group_spec(1), _dil_group_spec(2), _dil_group_spec(0, LANES),
                  _dil_group_spec(0), _dil_group_spec(0, LANES)],
        out_specs=[_dil_group_spec(0)] * 3,
        out_shape=[jax.ShapeDtypeStruct((DIL_G, S, DIL_OUT), F32)] * 3,
        compiler_params=_cparams(("parallel",)),
    )(qkv3, qkv3, qkv3, lse3, do3, c3)


COMB_T = 256


def _dil_combine(o3, lse3):
    heads_per_lb = LANES // HD

    def body(o_ref, lse_ref, ob_ref, al_ref, *scratch):
        o_tok = [scratch[LB_PER_CB * gg:LB_PER_CB * (gg + 1)] for gg in range(DIL_G)]
        lse_tok = scratch[LB_PER_CB * DIL_G:]
        g = pl.program_id(0)
        for gs in range(DIL_G):
            @pl.when(g == gs)
            def _(gs=gs):
                for tok, sub in _residue_pieces(DIL_R[gs]):
                    for hf in range(LB_PER_CB):
                        o_tok[gs][hf][tok, :] = o_ref[sub, hf * LANES:(hf + 1) * LANES]
                    lse_tok[gs][tok, :] = lse_ref[sub, :]

        @pl.when(g == DIL_G - 1)
        def _():
            def chunk(i, carry):
                rows = pl.ds(pl.multiple_of(i * COMB_T, COMB_T), COMB_T)
                lse = [lse_tok[gg][rows, :] for gg in range(DIL_G)]
                m = jnp.maximum(jnp.maximum(lse[0], lse[1]), lse[2])
                e = [jnp.exp(lse[gg] - m) for gg in range(DIL_G)]
                den = (e[0] + e[1]) + e[2]
                al = [e[gg] / den for gg in range(DIL_G)]
                for gg in range(DIL_G):
                    al_ref[gg, rows, :] = al[gg]
                for h in range(DIL_HG):
                    hf, sl = h // heads_per_lb, slice((h % heads_per_lb) * HD, (h % heads_per_lb + 1) * HD)
                    acc = al[0][:, h:h + 1] * o_tok[0][hf][rows, sl]
                    for gg in range(1, DIL_G):
                        acc = acc + al[gg][:, h:h + 1] * o_tok[gg][hf][rows, sl]
                    ob_ref[rows, h * HD:(h + 1) * HD] = acc
                return carry

            lax.fori_loop(0, S // COMB_T, chunk, 0)

    return pl.pallas_call(
        body, name="dil_combine", grid=(DIL_G,),
        in_specs=[pl.BlockSpec((None, S, DIL_OUT), lambda g: (g, 0, 0)),
                  pl.BlockSpec((None, S, LANES), lambda g: (g, 0, 0))],
        out_specs=[pl.BlockSpec((S, DIL_OUT), lambda g: (0, 0)),
                   pl.BlockSpec((DIL_G, S, LANES), lambda g: (0, 0, 0))],
        out_shape=[jax.ShapeDtypeStruct((S, DIL_OUT), F32), jax.ShapeDtypeStruct((DIL_G, S, LANES), F32)],
        scratch_shapes=[pltpu.VMEM((S, LANES), F32)] * (DIL_G * (LB_PER_CB + 1)),
        compiler_params=_cparams(("arbitrary",)),
    )(o3, lse3)


def _dil_combine_bwd(dob, ob, alpha, deps=()):
    heads_per_lb = LANES // HD

    def body(dob_ref, ob_ref, al_ref, *refs):
        do_ref, c_ref = refs[len(deps):]
        g = pl.program_id(0)
        hf = pl.program_id(1)
        for gs in range(DIL_G):
            @pl.when(g == gs)
            def _(gs=gs):
                for tok, sub in _residue_pieces(DIL_R[gs]):
                    dob = dob_ref[tok, :]
                    prod = dob * ob_ref[tok, :]
                    al = al_ref[tok, :]
                    lane = lax.broadcasted_iota(jnp.int32, al.shape, 1)
                    c = jnp.where(hf == 0, 0.0, c_ref[sub, :])
                    for hh in range(heads_per_lb):
                        sl = slice(hh * HD, (hh + 1) * HD)
                        head = hf * heads_per_lb + hh
                        a = jnp.sum(jnp.where(lane == head, al, 0.0), axis=-1, keepdims=True)
                        do_ref[sub, sl] = (a * dob[:, sl]).astype(BF16)
                        c = jnp.where(lane == head, a * jnp.sum(prod[:, sl], axis=-1, keepdims=True), c)
                    c_ref[sub, :] = c

    half = pl.BlockSpec((S, LANES), lambda g, hf: (0, hf))
    return pl.pallas_call(
        body, name="dil_combine_bwd", grid=(DIL_G, LB_PER_CB),
        in_specs=[half, half, pl.BlockSpec((None, S, LANES), lambda g, hf: (g, 0, 0))] + [ANY] * len(deps),
        out_specs=[pl.BlockSpec((None, S, LANES), lambda g, hf: (g, 0, hf)),
                   pl.BlockSpec((None, S, LANES), lambda g, hf: (g, 0, 0))],
        out_shape=[jax.ShapeDtypeStruct((DIL_G, S, DIL_OUT), BF16), jax.ShapeDtypeStruct((DIL_G, S, LANES), F32)],
        compiler_params=_cparams(("parallel", "arbitrary")),
    )(dob, ob, alpha, *deps)


def _local_step(x, tgt, h1, g_attn, g_mlp, g_final, b_pad, w_in_t, hooks):
    tables = _rope_tables()

    first = hooks.first_deps()
    proj_a = _mm(h1, w_in_t, mode="nt", tm=S, tn=512, n=3 * FOX_W, out_dtypes=[BF16], name="proj_a", deps=first)
    proj_b = _mm(h1, w_in_t, mode="nt", tm=S, tn=DIL_W, n=3 * DIL_W, b_off=SEG_B // DIL_W, out_dtypes=[F32], name="proj_b",
                deps=first)
    proj_g = _mm(h1, w_in_t, mode="nt", tm=S, tn=D, n=2 * D, b_off=SEG_G // D, out_dtypes=[F32], name="proj_g",
                deps=first)
    proj_f = _mm(h1, w_in_t, mode="nt", tm=S, tn=LANES, n=LANES, b_off=SEG_F // LANES, out_dtypes=[F32], name="proj_f",
                deps=first)

    q_aug, k_aug = _fox_prepare(proj_a, proj_f, b_pad)
    oa, lse_a = _fox_fwd(q_aug, k_aug, proj_a)

    qkv3 = _rope_split(proj_b, tables)
    o3, lse3 = _dil_fwd(qkv3)
    ob, alpha = _dil_combine(o3, lse3)

    w_a, w_b, w_out = hooks.mixer_weights(ob)
    ya, yb, mixed, x2, h2 = _mixer_tail(oa, ob, w_a, w_b, w_out, proj_g, x, g_mlp)
    w_up_sh, w_down = hooks.mlp_weights(h2)

    def up_epilogue(acc):
        r = jnp.maximum(acc, 0.0)
        return acc, r * r

    u, act = _mm(h2, w_up_sh, mode="nn", tm=S, tn=DFF // N_DEV, b_sharded=True, out_dtypes=[F32, BF16],
                 name="mlp_up", epilogue=up_epilogue)

    def loss_epilogue(acc, rows, vecs):
        dx3, dg, loss = _final_loss_rows(rows[0] + acc, vecs[0], rows[1])
        return (dx3,), (dg, loss)

    dx3, dg_final, loss = _mm_rows(act, w_down, tm=ROWS_TM, name="mlp_down_loss", epilogue=loss_epilogue,
                                   rows=(x2, tgt), vecs=(g_final,), row_out=(F32,), vec_out=(D, LANES))

    def rms_bwd_epilogue(acc, rows, vecs):
        dx, dg = _rms_bwd_rows(acc, rows[0], rows[1], vecs[0])
        return (dx,), (dg,)

    du = _mm(dx3, w_down, mode="nt", tm=S, tn=512, out_dtypes=[BF16], name="mlp_down_bwd",
             epilogue=lambda acc, u_t: (acc * (2.0 * jnp.maximum(u_t, 0.0)),), extras=(u,))
    dw_down = _mm(act, dx3, mode="tn", tm=512, tn=D, out_dtypes=[F32], name="dw_down")
    dw_up_sh = _mm(h2, du, mode="tn", tm=D, tn=DFF // N_DEV, out_sharded=True, out_dtypes=[F32], name="dw_up")
    dx2, dg_mlp = _mm_rows(du, w_up_sh, b_sharded=True, tm=ROWS_TM, name="mlp_up_bwd", epilogue=rms_bwd_epilogue,
                           rows=(x2, dx3), vecs=(g_mlp,), row_out=(F32,), vec_out=(D,),
                           deps=hooks.mlp_grads(dw_up_sh, dw_down))

    dw_out = _mm(mixed, dx2, mode="tn", tm=D, tn=D, out_dtypes=[F32], name="dw_out")

    dyab, dproj = _out_proj_gate_bwd(dx2, w_out, proj_g, ya, yb)
    doa = _mm(dyab, w_a, mode="nt", tm=S, tn=FOX_W, k=D, a_off=0, out_dtypes=[BF16], name="branch_a_bwd")
    dw_a = _mm(oa, dyab, mode="tn", tm=FOX_W, tn=D, n=D, b_off=0, out_dtypes=[F32], name="dw_branch_a")
    dob = _mm(dyab, w_b, mode="nt", tm=S, tn=CB, k=D, a_off=1, out_dtypes=[F32], name="branch_b_bwd")
    dw_b = _mm(ob, dyab, mode="tn", tm=DIL_OUT, tn=D, n=D, b_off=1, out_dtypes=[F32], name="dw_branch_b")

    dproj, dft, dfs = _fox_bwd(q_aug, k_aug, proj_a, lse_a, doa, dproj, deps=hooks.mixer_grads(dw_a, dw_b, dw_out))
    dproj, db = _fox_dscan(dft, dfs, proj_f, b_pad, dproj)

    do3, c3 = _dil_combine_bwd(dob, ob, alpha, deps=hooks.mid_backward(db))
    dq3, dk3, dv3 = _dil_bwd(qkv3, lse3, do3, c3)
    dproj = _rope_merge_bwd(dq3, dk3, dv3, tables, dproj)

    dw_in_t = _mm(dproj, h1, mode="tn", tm=DW_IN_TM, tn=D, out_dtypes=[F32], name="dw_in")
    dx, dg_attn = _mm_rows(dproj, w_in_t, tm=ROWS_TM // 2, name="proj_bwd", epilogue=rms_bwd_epilogue,
                           rows=(x, dx2), vecs=(g_attn,), row_out=(F32,), vec_out=(D,),
                           deps=hooks.w_in_grad(dw_in_t))

    return loss, dx, (dg_attn, db, dg_mlp, dg_final)


SHARD_SHAPES = ((D_IN // N_DEV, D), (FOX_W, D // N_DEV), (DIL_OUT, D // N_DEV), (D // N_DEV, D),
                (D, DFF // N_DEV), (DFF // N_DEV, D))
W_NAMES = ("w_in", "w_a", "w_b", "w_out", "w_up", "w_down")
AG_COPIES = 7
HBM = pl.BlockSpec(memory_space=pltpu.HBM)
SEM = pl.BlockSpec(memory_space=pltpu.SEMAPHORE)


def _place():
    return lax.axis_index("x"), lax.axis_index("y"), lax.axis_index("c")


W_SLAB = 752
W_WIN = 272
PAD_BLK = 256


def _pad_runs():
    runs = sorted((pad, pad + b - a, p, a) for p in range(N_DEV) for a, b, pad in _shard_pieces(p))
    blocks = []
    for k in range(D_PAD // PAD_BLK):
        lo, hi = k * PAD_BLK, (k + 1) * PAD_BLK
        blocks.append([(max(lo, r0) - lo, min(hi, r1) - lo, p, a + max(lo, r0) - r0)
                       for r0, r1, p, a in runs if max(lo, r0) < min(hi, r1)])
    return blocks


def _gather_w_in(shard, x_in, g_attn, late_shards):
    blocks = _pad_runs()

    norm_t = 512
    n_chunks = S // norm_t

    n_late = len(late_shards)
    late_shapes = [t.shape for t in late_shards]

    def body(x_ref, xin_ref, g_ref, *refs):
        late_in, (out_ref, h1_ref), late_out = refs[:n_late], refs[n_late:n_late + 2], refs[n_late + 2:2 * n_late + 2]
        scratch = refs[2 * n_late + 2:]
        stage, land, xbuf = scratch[:3]
        late_stage, late_cast = scratch[3:3 + n_late], scratch[3 + n_late:3 + 2 * n_late]
        send_sems, recv_sems, load_sem, x_sems, late_sems, store_sems = scratch[3 + 2 * n_late:]
        x, y, c = _place()
        me, sibling = (x, y, c), (x, y, 1 - c)
        chips = [(1 - x, y), (x, 1 - y), (1 - x, 1 - y)]

        def slot(px, py, pc):
            return land.at[4 * px + 2 * py + pc]

        def copy(k, block, to):
            return pltpu.make_async_remote_copy(
                src_ref=slot(*block), dst_ref=slot(*block), send_sem=send_sems.at[k], recv_sem=recv_sems.at[k],
                device_id=to, device_id_type=MESH)

        load = pltpu.make_async_copy(x_ref, stage, load_sem)
        load.start()
        load.wait()
        land[4 * x + 2 * y + c] = stage[...].astype(BF16)
        sent = [copy(0, me, sibling)] + [copy(1 + j, me, (*chip, c)) for j, chip in enumerate(chips)]
        for cp in sent:
            cp.start()

        def x_load(i):
            return pltpu.make_async_copy(xin_ref.at[pl.ds(i * norm_t, norm_t)], xbuf.at[i % 2], x_sems.at[i % 2])

        late_loads = [pltpu.make_async_copy(late_in[i], late_stage[i], late_sems.at[i]) for i in range(n_late)]
        for ld in late_loads:
            ld.start()
        x_load(0).start()
        for i in range(n_chunks):
            if i + 1 < n_chunks:
                x_load(i + 1).start()
            x_load(i).wait()
            h1_ref[i * norm_t:(i + 1) * norm_t, :] = _rms_rows(xbuf[i % 2], g_ref[...]).astype(BF16)
        stores = []
        for i in range(n_late):
            late_loads[i].wait()
            late_cast[i][...] = late_stage[i][...].astype(BF16)
            stores.append(pltpu.make_async_copy(late_cast[i], late_out[i].at[4 * x + 2 * y + c], store_sems.at[i]))
            stores[-1].start()

        for j, chip in enumerate(chips):
            copy(1 + j, (*chip, c), me).wait_recv()
            sent.append(copy(4 + j, (*chip, c), sibling))
            sent[-1].start()
        copy(0, sibling, me).wait_recv()
        for j, chip in enumerate(chips):
            copy(4 + j, (*chip, 1 - c), me).wait_recv()
        for cp in sent:
            cp.wait_send()
        for st in stores:
            st.wait()

        row = lax.broadcasted_iota(jnp.int32, (PAD_BLK, D), 0)
        for k, runs in enumerate(blocks):
            out = jnp.zeros((PAD_BLK, D), F32)
            for o_lo, o_hi, p, a in runs:
                start = min(a // 16 * 16, W_SLAB - W_WIN)
                win = land[p, start:start + W_WIN, :].astype(F32)
                moved = pltpu.roll(win, (o_lo - (a - start)) % W_WIN, 0)[:PAD_BLK]
                out = jnp.where((row >= o_lo) & (row < o_hi), moved, out)
            out_ref[k * PAD_BLK:(k + 1) * PAD_BLK, :] = out.astype(BF16)

    return pl.pallas_call(
        body, name="all_gather_w_in",
        out_shape=[jax.ShapeDtypeStruct((D_PAD, D), BF16), jax.ShapeDtypeStruct((S, D), BF16)]
                  + [jax.ShapeDtypeStruct((N_DEV,) + sh, BF16) for sh in late_shapes],
        in_specs=[ANY, ANY, pl.BlockSpec(memory_space=pltpu.VMEM)] + [ANY] * n_late,
        out_specs=[pl.BlockSpec(memory_space=pltpu.VMEM)] * 2 + [ANY] * n_late,
        scratch_shapes=[pltpu.VMEM((W_SLAB, D), F32), pltpu.VMEM((N_DEV, W_SLAB, D), BF16),
                        pltpu.VMEM((2, norm_t, D), F32)]
                       + [pltpu.VMEM(sh, F32) for sh in late_shapes] + [pltpu.VMEM(sh, BF16) for sh in late_shapes]
                       + [pltpu.SemaphoreType.DMA((AG_COPIES,)), pltpu.SemaphoreType.DMA((AG_COPIES,)),
                          pltpu.SemaphoreType.DMA(()), pltpu.SemaphoreType.DMA((2,)),
                          pltpu.SemaphoreType.DMA((n_late,)), pltpu.SemaphoreType.DMA((n_late,))],
        compiler_params=_cparams(None),
    )(shard, x_in, g_attn, *late_shards)


PEER_COPIES = N_DEV - 1
SIBLING_COPIES = 4
CHIP_COPIES = 3


def _peer_copies(_, land_refs, send_sems, recv_sems):
    x, y, c = _place()
    mine = 4 * x + 2 * y + c
    copies = []
    for i, ref in enumerate(land_refs):
        for k in range(1, N_DEV):
            peer = (x ^ (k >> 2), y ^ ((k >> 1) & 1), c ^ (k & 1))
            n = i * PEER_COPIES + k - 1
            copies.append(pltpu.make_async_remote_copy(
                src_ref=ref.at[mine], dst_ref=ref.at[mine], send_sem=send_sems.at[n], recv_sem=recv_sems.at[n],
                device_id=peer, device_id_type=MESH))
    return copies


def _sibling_copies(g_refs, r_refs, send_sems, recv_sems):
    x, y, c = _place()
    return [pltpu.make_async_remote_copy(
        src_ref=g_refs[i].at[2 * k + (1 - c)], dst_ref=r_refs[i].at[k],
        send_sem=send_sems.at[SIBLING_COPIES * i + k], recv_sem=recv_sems.at[SIBLING_COPIES * i + k],
        device_id=(x, y, 1 - c), device_id_type=MESH) for i in range(len(g_refs)) for k in range(SIBLING_COPIES)]


def _chip_copies(p_refs, r_refs, send_sems, recv_sems):
    x, y, c = _place()
    chips = [(1 - x, y), (x, 1 - y), (1 - x, 1 - y)]
    return [pltpu.make_async_remote_copy(
        src_ref=p_refs[i].at[2 * cx + cy], dst_ref=r_refs[i].at[j],
        send_sem=send_sems.at[CHIP_COPIES * i + j], recv_sem=recv_sems.at[CHIP_COPIES * i + j],
        device_id=(cx, cy, c), device_id_type=MESH) for i in range(len(p_refs)) for j, (cx, cy) in enumerate(chips)]


def _in_hbm(a):
    return pltpu.with_memory_space_constraint(a, pltpu.HBM)


def _exchange_start(name, copies_fn, n_copies, srcs, lands, after=()):
    n_s, n_l, n_a = len(srcs), len(lands), len(after)

    def body(*refs):
        outs = refs[n_s + n_l + n_a:]
        for cp in copies_fn(refs[:n_s], refs[n_s:n_s + n_l], outs[0], outs[1]):
            cp.start()
        outs[-1][...] = jnp.zeros_like(outs[-1])

    res = pl.pallas_call(
        body, name=name,
        out_shape=[pltpu.SemaphoreType.DMA((n_copies,)), pltpu.SemaphoreType.DMA((n_copies,))]
                  + [pltpu.HBM(t.shape, t.dtype) for t in (*srcs, *lands)] + [jax.ShapeDtypeStruct((8, LANES), F32)],
        in_specs=[HBM] * (n_s + n_l) + [ANY] * n_a,
        out_specs=[SEM, SEM] + [HBM] * (n_s + n_l) + [pl.BlockSpec(memory_space=pltpu.VMEM)],
        input_output_aliases={i: 2 + i for i in range(n_s + n_l)},
        compiler_params=pltpu.CompilerParams(has_side_effects=pltpu.SideEffectType.DATAFLOW_SIDE_EFFECTING),
    )(*[_in_hbm(t) for t in (*srcs, *lands)], *after)
    return res[0], res[1], list(res[2:2 + n_s]), list(res[2 + n_s:2 + n_s + n_l]), res[-1]


def _exchange_wait(name, copies_fn, started, after):
    send_sems, recv_sems, srcs, lands, _ = started
    n_s, n_l = len(srcs), len(lands)

    def body(*refs):
        for cp in copies_fn(refs[:n_s], refs[n_s:n_s + n_l], refs[n_s + n_l], refs[n_s + n_l + 1]):
            cp.wait_send()
            cp.wait_recv()

    res = pl.pallas_call(
        body, name=name,
        out_shape=[pltpu.HBM(t.shape, t.dtype) for t in (*srcs, *lands)],
        in_specs=[HBM] * (n_s + n_l) + [SEM, SEM, ANY],
        out_specs=[HBM] * (n_s + n_l),
        input_output_aliases={i: i for i in range(n_s + n_l)},
        compiler_params=pltpu.CompilerParams(has_side_effects=pltpu.SideEffectType.DATAFLOW_SIDE_EFFECTING),
    )(*srcs, *lands, send_sems, recv_sems, after)
    return list(res[:n_s]), list(res[n_s:])


def _shard_block(shape):
    rows, cols = shape
    if rows * cols <= 256 * 1024:
        return rows, cols
    if rows % 256:
        return rows, 256
    return 256, cols


def _chip_partial(ids, g_stack, recv1, name):
    shape = g_stack.shape[1:]
    br, bc = _shard_block(shape)

    def body(ids_ref, g_ref, r_ref, pb_ref, own_ref):
        s = g_ref[...] + r_ref[...]
        pb_ref[...] = s.astype(BF16)

        @pl.when(pl.program_id(2) == ids_ref[1])
        def _():
            own_ref[...] = s

    grid_spec = pltpu.PrefetchScalarGridSpec(
        num_scalar_prefetch=1, grid=(shape[0] // br, shape[1] // bc, 4),
        in_specs=[pl.BlockSpec((None, br, bc), lambda r, q, k, ids: (2 * k + ids[0], r, q)),
                  pl.BlockSpec((None, br, bc), lambda r, q, k, ids: (k, r, q))],
        out_specs=[pl.BlockSpec((None, br, bc), lambda r, q, k, ids: (k, r, q)),
                   pl.BlockSpec((br, bc), lambda r, q, k, ids: (r, q))])
    return pl.pallas_call(
        body, name=name, grid_spec=grid_spec,
        out_shape=[jax.ShapeDtypeStruct((4,) + shape, BF16), jax.ShapeDtypeStruct(shape, F32)],
        compiler_params=_cparams(("parallel", "parallel", "arbitrary")),
    )(ids, g_stack, recv1)


def _adamw(w, g, m, v):
    m = ADAM_B1 * m + (1.0 - ADAM_B1) * g
    v = ADAM_B2 * v + (1.0 - ADAM_B2) * (g * g)
    m_hat = m / (1.0 - ADAM_B1 ** ADAM_STEP)
    v_hat = v / (1.0 - ADAM_B2 ** ADAM_STEP)
    delta = -ADAM_LR * (m_hat / (jnp.sqrt(v_hat) + ADAM_EPS) + ADAM_WD * w)
    return delta, m, v


def _reduce_adamw(own, recv2, w, m, v, name, deps=()):
    shape = own.shape
    br, bc = _shard_block(shape)

    def body(own_ref, r_ref, w_ref, m_ref, v_ref, *refs):
        g_ref, d_ref, nm_ref, nv_ref = refs[len(deps):]
        g = own_ref[...]
        for j in range(3):
            g = g + r_ref[j].astype(F32)
        delta, nm, nv = _adamw(w_ref[...], g, m_ref[...], v_ref[...])
        g_ref[...] = g
        d_ref[...] = delta
        nm_ref[...] = nm
        nv_ref[...] = nv

    blk = pl.BlockSpec((br, bc), lambda r, q: (r, q))
    return pl.pallas_call(
        body, name=name, grid=(shape[0] // br, shape[1] // bc),
        in_specs=[blk, pl.BlockSpec((3, br, bc), lambda r, q: (0, r, q)), blk, blk, blk] + [ANY] * len(deps),
        out_specs=[blk] * 4, out_shape=[jax.ShapeDtypeStruct(shape, F32)] * 4,
        compiler_params=_cparams(("parallel", "parallel")),
    )(own, recv2, w, m, v, *deps)


def _small_adamw(stack, w, m, v):
    def body(s_ref, w_ref, m_ref, v_ref, go_ref, d_ref, nm_ref, nv_ref):
        g = s_ref[0]
        for s in range(1, N_DEV):
            g = g + s_ref[s]
        delta, nm, nv = _adamw(w_ref[...], g, m_ref[...], v_ref[...])
        go_ref[...] = g
        d_ref[...] = delta
        nm_ref[...] = nm
        nv_ref[...] = nv

    vm = pl.BlockSpec(memory_space=pltpu.VMEM)
    return pl.pallas_call(
        body, name="small_adamw",
        in_specs=[vm] * 4, out_specs=[vm] * 4,
        out_shape=[jax.ShapeDtypeStruct((SMALL_R, LANES), F32)] * 4,
    )(stack, w, m, v)


def _cols_to_whole(stack):
    rows = stack.shape[1]
    return stack.transpose(1, 0, 2).reshape(rows, -1)


def _whole_to_cols(t):
    rows = t.shape[0]
    return t.reshape(rows, N_DEV, -1).transpose(1, 0, 2)


W_IN_SEGMENTS = ((0, 3 * FOX_W, SEG_A), (3 * FOX_W, 3 * FOX_W + FOX_H, SEG_F),
                 (3 * FOX_W + FOX_H, 3 * FOX_W + FOX_H + 3 * DIL_W, SEG_B), (3 * FOX_W + FOX_H + 3 * DIL_W, D_IN, SEG_G))


def _shard_pieces(p):
    rows = D_IN // N_DEV
    lo, hi = p * rows, (p + 1) * rows
    return [(max(lo, a) - lo, min(hi, b) - lo, pad + max(lo, a) - a)
            for a, b, pad in W_IN_SEGMENTS if max(lo, a) < min(hi, b)]


def _unpad_runs(p):
    rows = D_IN // N_DEV
    blocks = []
    for lo in range(0, rows, PAD_BLK):
        hi = min(lo + PAD_BLK, rows)
        blocks.append([(max(lo, a) - lo, min(hi, b) - lo, pad + max(lo, a) - a)
                       for a, b, pad in _shard_pieces(p) if max(lo, a) < min(hi, b)])
    return blocks


def _unpad_dw_in_t(dwp):
    rows = D_IN // N_DEV
    plans = [_unpad_runs(p) for p in range(N_DEV)]
    windows = [[min(src // 8 * 8, D_PAD - W_WIN) for runs in plan for _, _, src in runs] for plan in plans]
    n_win = max(len(w) for w in windows)

    def body(src_ref, o_ref, buf, sems):
        p = pl.program_id(0)
        row = lax.broadcasted_iota(jnp.int32, (PAD_BLK, D), 0)

        def fetch(q):
            return [pltpu.make_async_copy(src_ref.at[pl.ds(start, W_WIN)], buf.at[q % 2, n], sems.at[q % 2, n])
                    for n, start in enumerate(windows[q])]

        for q in range(N_DEV):
            @pl.when(p == q)
            def _(q=q):
                for nxt in ([0, 1] if q == 0 else [q + 1]):
                    if nxt < N_DEV:
                        for cp in fetch(nxt):
                            cp.start()
                copies, n = fetch(q), 0
                for j, runs in enumerate(plans[q]):
                    out = None
                    for o_lo, o_hi, src in runs:
                        copies[n].wait()
                        moved = pltpu.roll(buf[q % 2, n], (o_lo - (src - windows[q][n])) % W_WIN, 0)[:PAD_BLK]
                        out = moved if out is None else jnp.where((row >= o_lo) & (row < o_hi), moved, out)
                        n += 1
                    size = min(PAD_BLK, rows - j * PAD_BLK)
                    o_ref[j * PAD_BLK:j * PAD_BLK + size, :] = out[:size]

    return pl.pallas_call(
        body, name="unpad_dw_in", grid=(N_DEV,),
        in_specs=[ANY], out_specs=pl.BlockSpec((None, rows, D), lambda p: (p, 0, 0)),
        out_shape=jax.ShapeDtypeStruct((N_DEV, rows, D), F32),
        scratch_shapes=[pltpu.VMEM((2, n_win, W_WIN, D), F32), pltpu.SemaphoreType.DMA((2, n_win))],
        compiler_params=_cparams(("arbitrary",)),
    )(dwp)


LOSS_ROW = 25


def _pack_small(g_attn, b, g_mlp, g_final, loss_row=None):
    tail = jnp.pad(b, ((0, 7), (0, LANES - b.shape[1])))
    if loss_row is not None:
        tail = tail + jnp.pad(loss_row, ((LOSS_ROW - 24, 31 - LOSS_ROW), (0, 0)))
    return jnp.concatenate([g_attn.reshape(8, LANES), g_mlp.reshape(8, LANES), g_final.reshape(8, LANES), tail], axis=0)


def _unpack_small(p):
    return (p[0:8].reshape(1, D), p[24:25, :FOX_H], p[8:16].reshape(1, D), p[16:24].reshape(D))


class _StepComm:
    def __init__(self, ids, w_sh, m_sh, v_sh, lands, after_w_in):
        self.ids = ids
        self.w_sh, self.m_sh, self.v_sh = w_sh, m_sh, v_sh
        self.updates = None
        self.gather_mixer = _exchange_start("gather_mixer_start", _peer_copies, PEER_COPIES * 3, [], lands[:3],
                                            after=(after_w_in,))
        self.gather_mlp = _exchange_start("gather_mlp_start", _peer_copies, PEER_COPIES * 2, [], lands[3:],
                                          after=(self.gather_mixer[-1],))
        self.sibling = {}
        self.chips = {}
        self.own = {}
        self.names = {}

    def _reduce_start(self, group, names, grads, after=()):
        lands = [lax.empty((SIBLING_COPIES,) + t.shape[1:], F32) for t in grads]
        self.sibling[group] = _exchange_start("grad_%s_sibling_start" % group, _sibling_copies,
                                              SIBLING_COPIES * len(grads), grads, lands, after=after)
        self.names[group] = names
        return self.sibling[group][-1]

    def _reduce_mid(self, group, after):
        grads, recv1 = _exchange_wait("grad_%s_sibling_wait" % group, _sibling_copies, self.sibling[group], after)
        parts = [_chip_partial(self.ids, g, r, "grad_partial_" + n) for g, r, n in zip(grads, recv1, self.names[group])]
        self.own[group] = [p[1] for p in parts]
        srcs = [p[0] for p in parts]
        lands = [lax.empty((CHIP_COPIES,) + t.shape[1:], BF16) for t in srcs]
        self.chips[group] = _exchange_start("grad_%s_chips_start" % group, _chip_copies, CHIP_COPIES * len(srcs),
                                            srcs, lands)
        return self.chips[group][-1]

    def reduced(self, group, after):
        _, recv2 = _exchange_wait("grad_%s_chips_wait" % group, _chip_copies, self.chips[group], after)
        return list(zip(self.own[group], recv2))

    def first_deps(self):
        return [self.gather_mlp[-1]]

    def mixer_weights(self, after):
        _, (g_a, g_b, g_out) = _exchange_wait("gather_mixer_wait", _peer_copies, self.gather_mixer, after)
        return _cols_to_whole(g_a), _cols_to_whole(g_b), g_out.reshape(D, D)

    def mlp_weights(self, after):
        _, (g_up, g_down) = _exchange_wait("gather_mlp_wait", _peer_copies, self.gather_mlp, after)
        return g_up, g_down.reshape(DFF, D)

    def mlp_grads(self, dw_up_sh, dw_down):
        return [self._reduce_start("mlp", W_NAMES[4:], [dw_up_sh, dw_down.reshape((N_DEV,) + SHARD_SHAPES[5])])]

    def mixer_grads(self, dw_a, dw_b, dw_out):
        token = self._reduce_mid("mlp", dw_b)
        grads = [_whole_to_cols(dw_a), _whole_to_cols(dw_b), dw_out.reshape((N_DEV,) + SHARD_SHAPES[3])]
        return [self._reduce_start("mixer", W_NAMES[1:4], grads, after=(token,))]

    def mid_backward(self, after):
        return [self._reduce_mid("mixer", after)]

    def w_in_grad(self, dw_in_t):
        grads = [_unpad_dw_in_t(dw_in_t)]
        token = self._reduce_start("w_in", W_NAMES[:1], grads)
        reduced = self.reduced("mixer", token) + self.reduced("mlp", token)
        self.updates, last = [None] * len(reduced), token
        for i in (3, 4, 0, 1, 2):
            if i == 0:
                last = self._reduce_mid("w_in", last)
            self.updates[i] = _reduce_adamw(*reduced[i], self.w_sh[1 + i], self.m_sh[1 + i], self.v_sh[1 + i],
                                            "adamw_" + W_NAMES[1 + i], deps=[last])
            last = self.updates[i][0]
        return [last]

    def w_in_update(self, after):
        (reduced,) = self.reduced("w_in", after)
        return _reduce_adamw(*reduced, self.w_sh[0], self.m_sh[0], self.v_sh[0], "adamw_" + W_NAMES[0])


def kernel(x, norm_attn_g, w_in, b_forget, w_branch_a, w_branch_b, w_out, norm_mlp_g, w_up, w_down, norm_final_g, loss_target, m_norm_attn_g, m_w_in, m_b_forget, m_w_branch_a, m_w_branch_b, m_w_out, m_norm_mlp_g, m_w_up, m_w_down, m_norm_final_g, v_norm_attn_g, v_w_in, v_b_forget, v_w_branch_a, v_w_branch_b, v_w_out, v_norm_mlp_g, v_w_up, v_w_down, v_norm_final_g):
    cx, cy, cc = _place()
    ids = jnp.stack([cc, 2 * cx + cy]).astype(jnp.int32)

    w_sh = [w_in[0].T] + [t[0] for t in (w_branch_a, w_branch_b, w_out, w_up, w_down)]
    m_sh = [m_w_in[0].T] + [t[0] for t in (m_w_branch_a, m_w_branch_b, m_w_out, m_w_up, m_w_down)]
    v_sh = [v_w_in[0].T] + [t[0] for t in (v_w_branch_a, v_w_branch_b, v_w_out, v_w_up, v_w_down)]

    w_in_t, h1, *lands = _gather_w_in(jnp.pad(w_sh[0], ((0, W_SLAB - w_sh[0].shape[0]), (0, 0))), x[0], norm_attn_g,
                                      w_sh[1:])
    comm = _StepComm(ids, w_sh, m_sh, v_sh, lands, w_in_t)
    b_pad = jnp.pad(b_forget, ((0, 0), (0, LANES - FOX_H)))

    loss_row, dx, dsmall = _local_step(
        x[0], loss_target[0], h1, norm_attn_g, norm_mlp_g, norm_final_g.reshape(1, D), b_pad, w_in_t, comm)

    dg_attn, db, dg_mlp, dg_final = dsmall
    stack = lax.dynamic_update_slice(jnp.zeros((N_DEV, SMALL_R, LANES), F32),
                                     _pack_small(dg_attn, db, dg_mlp, dg_final, loss_row)[None],
                                     (4 * cx + 2 * cy + cc, 0, 0))
    small_gather = _exchange_start("small_gather_start", _peer_copies, PEER_COPIES, [], [stack])
    w_in_update = comm.w_in_update(small_gather[-1])
    _, (stack,) = _exchange_wait("small_gather_wait", _peer_copies, small_gather, w_in_update[0])
    small = _small_adamw(
        stack,
        _pack_small(norm_attn_g, b_forget, norm_mlp_g, norm_final_g.reshape(1, D)),
        _pack_small(m_norm_attn_g, m_b_forget, m_norm_mlp_g, m_norm_final_g.reshape(1, D)),
        _pack_small(v_norm_attn_g, v_b_forget, v_norm_mlp_g, v_norm_final_g.reshape(1, D)))
    big = [w_in_update] + comm.updates

    outs = [small[0][LOSS_ROW, 0], dx[None]]
    for q in range(4):
        s_attn, s_b, s_mlp, s_final = _unpack_small(small[q])
        b_in, b_a, b_b, b_out, b_up, b_down = [t[q][None] for t in big]
        b_in = jnp.swapaxes(b_in, 1, 2)
        outs += [s_attn, b_in, s_b, b_a, b_b, b_out, s_mlp, b_up, b_down, s_final]
    return tuple(outs)
```

```python
import functools

import jax
import jax.numpy as jnp
from jax import lax
from jax.experimental import pallas as pl
from jax.experimental.pallas import tpu as pltpu

F32 = jnp.float32
BF16 = jnp.bfloat16
MESH = pl.DeviceIdType.MESH

S = 2048
D = 1024
HD = 64
FOX_H = 8
FOX_W = FOX_H * HD
DIL_HG = 4
DIL_G = 3
DIL_W = DIL_G * DIL_HG * HD
DIL_OUT = DIL_HG * HD
DIL_BLK = 128
DIL_R = (1, 4, 16)
DFF = 4 * D
D_IN = 3 * FOX_W + FOX_H + 3 * DIL_W + 2 * D
EPS = 1e-6
NEG_INF = -1e30
SCALE = HD ** -0.5
ROPE_THETA = 500000.0
ROPE_DIM = HD // 4
N_DEV = 8

ADAM_LR = 0.001
ADAM_B1 = 0.9
ADAM_B2 = 0.999
ADAM_EPS = 1e-08
ADAM_WD = 0.01
ADAM_STEP = 10

LANES = 128
CB = 256
SEG_A = 0
SEG_B = 3 * FOX_W
SEG_Z = SEG_B + 3 * DIL_W
SEG_G = SEG_Z + CB
SEG_F = SEG_G + 2 * D
D_PAD = SEG_F + CB
DW_IN_TM = D_PAD // 10
SMALL_R = 32

VMEM_MB = 56


def _cparams(dims=None, vmem_mb=VMEM_MB, **kw):
    return pltpu.CompilerParams(dimension_semantics=dims, vmem_limit_bytes=vmem_mb << 20, **kw)


ANY = pl.BlockSpec(memory_space=pl.ANY)

_NN = (((1,), (0,)), ((), ()))
_NT = (((1,), (1,)), ((), ()))
_TN = (((0,), (0,)), ((), ()))


def _dot(a, b, dims):
    return lax.dot_general(a.astype(BF16), b.astype(BF16), dims, preferred_element_type=F32)


def _mm(a, b, *, mode, tm, tn, out_dtypes, name, n=None, k=None, a_off=0, b_off=0,
        b_sharded=False, out_sharded=False, epilogue=None, extras=(), deps=()):
    n_sh = b.shape[-1] if b_sharded else None
    if mode == "nn":
        m = a.shape[0]
        k = k or a.shape[1]
        a_spec = pl.BlockSpec((tm, k), lambda i, j: (i, a_off))
        if b_sharded:
            assert tn == n_sh
            n = N_DEV * n_sh
            b_spec = pl.BlockSpec((None, k, tn), lambda i, j: (j, 0, 0))
        else:
            n = n or b.shape[1]
            b_spec = pl.BlockSpec((k, tn), lambda i, j: (0, j + b_off))
        dims = _NN
    elif mode == "nt":
        m = a.shape[0]
        k = k or a.shape[1]
        a_spec = pl.BlockSpec((tm, k), lambda i, j: (i, a_off))
        if b_sharded:
            n = b.shape[1]
            b_spec = pl.BlockSpec((N_DEV, tn, n_sh), lambda i, j: (0, j, 0))
        else:
            n = n or b.shape[0]
            b_spec = pl.BlockSpec((tn, k), lambda i, j: (j + b_off, 0))
        dims = _NT
    else:
        k, m = a.shape
        n = n or b.shape[1]
        a_spec = pl.BlockSpec((k, tm), lambda i, j: (0, i))
        b_spec = pl.BlockSpec((k, tn), lambda i, j: (0, j + b_off))
        dims = _TN
    assert m % tm == 0 and n % tn == 0, (name, m, n, tm, tn)
    n_extra = len(extras)
    tile = pl.BlockSpec((tm, tn), lambda i, j: (i, j))
    if out_sharded:
        assert mode == "tn" and n // tn == N_DEV
        out_spec = pl.BlockSpec((None, tm, tn), lambda i, j: (j, i, 0))
        out_shape = (N_DEV, m, tn)
    else:
        out_spec, out_shape = tile, (m, n)

    def body(a_ref, b_ref, *refs):
        if mode == "nt" and b_sharded:
            acc = _dot(a_ref[:, 0:n_sh], b_ref[0], dims)
            for p in range(1, N_DEV):
                acc = acc + _dot(a_ref[:, p * n_sh:(p + 1) * n_sh], b_ref[p], dims)
        else:
            acc = _dot(a_ref[...], b_ref[...], dims)
        ex = [r[...] for r in refs[:n_extra]]
        outs = epilogue(acc, *ex) if epilogue is not None else (acc,)
        for o_ref, o in zip(refs[n_extra + len(deps):], outs):
            o_ref[...] = o.astype(o_ref.dtype)

    res = pl.pallas_call(
        body, name=name, grid=(m // tm, n // tn),
        in_specs=[a_spec, b_spec] + [tile] * n_extra + [ANY] * len(deps),
        out_specs=[out_spec] * len(out_dtypes),
        out_shape=[jax.ShapeDtypeStruct(out_shape, dt) for dt in out_dtypes],
        compiler_params=_cparams(("parallel", "parallel")),
    )(a, b, *extras, *deps)
    return res if len(out_dtypes) > 1 else res[0]


def _mm_rows(a, b, *, tm, name, epilogue, rows=(), vecs=(), row_out=(), vec_out=(), b_sharded=False, deps=()):
    m, k = a.shape
    n_rows, n_vecs, n_deps = len(rows), len(vecs), len(deps)
    n_sh = b.shape[-1] if b_sharded else None

    def body(a_ref, b_ref, *refs):
        row_refs, vec_refs = refs[:n_rows], refs[n_rows:n_rows + n_vecs]
        outs = refs[n_rows + n_vecs + n_deps:]
        if b_sharded:
            acc = _dot(a_ref[:, 0:n_sh], b_ref[0], _NT)
            for p in range(1, N_DEV):
                acc = acc + _dot(a_ref[:, p * n_sh:(p + 1) * n_sh], b_ref[p], _NT)
        else:
            acc = _dot(a_ref[...], b_ref[...], _NN)
        row_vals, vec_incs = epilogue(acc, [r[...] for r in row_refs], [v[...] for v in vec_refs])
        for o_ref, val in zip(outs[:len(row_out)], row_vals):
            o_ref[...] = val.astype(o_ref.dtype)

        @pl.when(pl.program_id(0) == 0)
        def _():
            for o_ref in outs[len(row_out):]:
                o_ref[...] = jnp.zeros_like(o_ref)

        for o_ref, inc in zip(outs[len(row_out):], vec_incs):
            o_ref[...] += inc

    tile = pl.BlockSpec((tm, D), lambda i: (i, 0))
    b_spec = pl.BlockSpec(b.shape, lambda i: (0,) * b.ndim)
    return pl.pallas_call(
        body, name=name, grid=(m // tm,),
        in_specs=[pl.BlockSpec((tm, k), lambda i: (i, 0)), b_spec] + [tile] * n_rows
                 + [pl.BlockSpec(v.shape, lambda i: (0, 0)) for v in vecs] + [ANY] * n_deps,
        out_specs=[tile] * len(row_out) + [pl.BlockSpec((1, w), lambda i: (0, 0)) for w in vec_out],
        out_shape=[jax.ShapeDtypeStruct((m, D), dt) for dt in row_out]
                  + [jax.ShapeDtypeStruct((1, w), F32) for w in vec_out],
        compiler_params=_cparams(("arbitrary",)),
    )(a, b, *rows, *vecs, *deps)


ROW_T = 256
ROWS_TM = 512


def _rms_rows(x, g):
    r = lax.rsqrt(jnp.mean(x * x, axis=-1, keepdims=True) + EPS)
    return (x * r) * g


def _rms_bwd_rows(dh, x, dres, g):
    r = lax.rsqrt(jnp.mean(x * x, axis=-1, keepdims=True) + EPS)
    xn = x * r
    dhn = dh * g
    dx = dres + r * (dhn - xn * jnp.mean(dhn * xn, axis=-1, keepdims=True))
    return dx, jnp.sum(dh * xn, axis=0, keepdims=True)


def _final_loss_rows(x, g, tgt):
    r = lax.rsqrt(jnp.mean(x * x, axis=-1, keepdims=True) + EPS)
    xn = x * r
    err = xn * g - tgt
    row_loss = jnp.mean(err * err, axis=-1, keepdims=True)
    loss = 0.5 * jnp.sum(row_loss, axis=0, keepdims=True) * jnp.ones((1, LANES), F32)
    dy = err * (1.0 / D)
    dyn = dy * g
    dx = r * (dyn - xn * jnp.mean(dyn * xn, axis=-1, keepdims=True))
    return dx, jnp.sum(dy * xn, axis=0, keepdims=True), loss


def _sigmoid(z):
    return 1.0 / (1.0 + jnp.exp(-z))


GATE_TR = 512


def _mixer_tail(oa, ob, w_a, w_b, w_out, proj_g, x, g_mlp):
    def body(oa_ref, ob_ref, wa_ref, wb_ref, wo_ref, g_ref, x_ref, gm_ref, ya_ref, yb_ref, mixed_ref, x2_ref, h2_ref):
        ya = _dot(oa_ref[...], wa_ref[...], _NN)
        yb = _dot(ob_ref[...], wb_ref[...], _NN)
        ya_ref[...] = ya
        yb_ref[...] = yb
        mixed = (_sigmoid(g_ref[:, :D]) * ya + _sigmoid(g_ref[:, D:]) * yb).astype(BF16)
        mixed_ref[...] = mixed
        x2 = x_ref[...] + _dot(mixed, wo_ref[...], _NN)
        x2_ref[...] = x2
        h2_ref[...] = _rms_rows(x2, gm_ref[...]).astype(BF16)

    def rows(width):
        return pl.BlockSpec((GATE_TR, width), lambda i: (i, 0))

    def whole(t):
        return pl.BlockSpec(t.shape, lambda i: (0, 0))

    return pl.pallas_call(
        body, name="mixer_tail", grid=(S // GATE_TR,),
        in_specs=[rows(FOX_W), rows(DIL_OUT), whole(w_a), whole(w_b), whole(w_out), rows(2 * D), rows(D), whole(g_mlp)],
        out_specs=[rows(D)] * 5,
        out_shape=[jax.ShapeDtypeStruct((S, D), dt) for dt in (F32, F32, BF16, F32, BF16)],
        compiler_params=_cparams(("parallel",)),
    )(oa, ob, w_a, w_b, w_out, proj_g, x, g_mlp)


def _out_proj_gate_bwd(dx2, w_out, proj_g, ya, yb):
    def body(dx_ref, w_ref, g_ref, ya_ref, yb_ref, dy_ref, dg_ref):
        dm = _dot(dx_ref[...], w_ref[...], _NT)
        for half, y_ref in enumerate((ya_ref, yb_ref)):
            cols = slice(half * D, (half + 1) * D)
            s = _sigmoid(g_ref[:, cols])
            dy_ref[:, cols] = (dm * s).astype(BF16)
            dg_ref[:, cols] = (dm * y_ref[...] * (s * (1.0 - s))).astype(BF16)

    row = pl.BlockSpec((GATE_TR, D), lambda i: (i, 0))
    wide = pl.BlockSpec((GATE_TR, 2 * D), lambda i: (i, 0))
    return pl.pallas_call(
        body, name="out_proj_gate_bwd", grid=(S // GATE_TR,),
        in_specs=[row, pl.BlockSpec((D, D), lambda i: (0, 0)), wide, row, row],
        out_specs=[wide, pl.BlockSpec((GATE_TR, 2 * D), lambda i: (i, SEG_G // (2 * D)))],
        out_shape=[jax.ShapeDtypeStruct((S, 2 * D), BF16), jax.ShapeDtypeStruct((S, D_PAD), BF16)],
        compiler_params=_cparams(("parallel",)),
    )(dx2, w_out, proj_g, ya, yb)


FOX_TQ = 512
FOX_TQ_FWD = 1024


def _scan_rows(x, reverse):
    n = x.shape[0]
    row = lax.broadcasted_iota(jnp.int32, x.shape, 0)
    k = 1
    while k < n:
        if reverse:
            x = x + jnp.where(row < n - k, pltpu.roll(x, n - k, 0), 0.0)
        else:
            x = x + jnp.where(row >= k, pltpu.roll(x, k, 0), 0.0)
        k *= 2
    return x


def _fox_dscan(dft, dfs, proj_f, b_pad, dproj):
    def body(dft_ref, dfs_ref, f_ref, b_ref, _, dfa_ref, db_ref):
        dfs_pad = jnp.concatenate([dfs_ref[...], jnp.zeros((LANES - FOX_H, S), F32)], axis=0)
        df = dfs_pad.T + dft_ref[0]
        for hp in range(1, dft_ref.shape[0]):
            df = df + pltpu.roll(dft_ref[hp], 2 * hp, 1)
        dlf = _scan_rows(df, reverse=True)
        z = f_ref[...] + b_ref[...]
        lane = lax.broadcasted_iota(jnp.int32, (S, LANES), 1)
        dfa = jnp.where(lane < FOX_H, dlf / (1.0 + jnp.exp(z)), 0.0)
        dfa_ref[:, :LANES] = dfa.astype(BF16)
        dfa_ref[:, LANES:] = jnp.zeros((S, CB - LANES), BF16)
        db_ref[...] = jnp.sum(dfa, axis=0, keepdims=True)

    return pl.pallas_call(
        body, name="fox_dscan", grid=(1,),
        in_specs=[pl.BlockSpec(dft.shape, lambda i: (0, 0, 0)), pl.BlockSpec((FOX_H, S), lambda i: (0, 0)),
                  pl.BlockSpec((S, LANES), lambda i: (0, 0)), pl.BlockSpec((1, LANES), lambda i: (0, 0)), ANY],
        out_specs=[pl.BlockSpec((S, CB), lambda i: (0, SEG_F // CB)), pl.BlockSpec((1, LANES), lambda i: (0, 0))],
        out_shape=[jax.ShapeDtypeStruct((S, D_PAD), BF16), jax.ShapeDtypeStruct((1, LANES), F32)],
        input_output_aliases={4: 0},
        compiler_params=_cparams(("arbitrary",)),
    )(dft, dfs, proj_f, b_pad, dproj)


FOX_NQ = S // FOX_TQ
FOX_HP = FOX_W // LANES
HEADS_PER_LB = LANES // HD
FOX_AUG = 2 * LANES


def _fox_prepare(proj_a, proj_f, b_pad):
    tr = 512

    def body(q_ref, k_ref, f_ref, b_ref, qa_ref, ka_ref, hi_s, mid_s, lo_s):
        i = pl.program_id(0)

        @pl.when(i == 0)
        def _():
            z = f_ref[...] + b_ref[...]
            lf = jnp.minimum(z, 0.0) - jnp.log1p(jnp.exp(-jnp.abs(z)))
            f = _scan_rows(lf, reverse=False)
            hi = f.astype(BF16).astype(F32)
            r1 = f - hi
            mid = r1.astype(BF16).astype(F32)
            hi_s[...] = hi
            mid_s[...] = mid
            lo_s[...] = (r1 - mid).astype(BF16).astype(F32)

        rows = pl.ds(pl.multiple_of(i * tr, tr), tr)
        hi, mid, lo = hi_s[rows, :], mid_s[rows, :], lo_s[rows, :]
        lane = lax.broadcasted_iota(jnp.int32, (tr, HD), 1)
        q_ones = jnp.where((lane >= 3) & (lane < 6), 1.0, 0.0)
        k_ones = jnp.where(lane < 3, 1.0, 0.0)
        for h in range(FOX_H):
            a1, a2, a3 = hi[:, h:h + 1], mid[:, h:h + 1], lo[:, h:h + 1]
            q_extra = jnp.where(lane == 0, a1, jnp.where(lane == 1, a2, jnp.where(lane == 2, a3, q_ones)))
            k_extra = jnp.where(lane == 3, -a1, jnp.where(lane == 4, -a2, jnp.where(lane == 5, -a3, k_ones)))
            base = h * LANES
            qa_ref[:, base:base + HD] = q_ref[:, h * HD:(h + 1) * HD] * jnp.asarray(SCALE, BF16)
            qa_ref[:, base + HD:base + LANES] = q_extra.astype(BF16)
            ka_ref[:, base:base + HD] = k_ref[:, h * HD:(h + 1) * HD]
            ka_ref[:, base + HD:base + LANES] = k_extra.astype(BF16)

    out_blk = pl.BlockSpec((tr, FOX_H * LANES), lambda i: (i, 0))
    return pl.pallas_call(
        body, name="fox_prepare", grid=(S // tr,),
        in_specs=[pl.BlockSpec((tr, FOX_W), lambda i: (i, 0)), pl.BlockSpec((tr, FOX_W), lambda i: (i, 1)),
                  pl.BlockSpec((S, LANES), lambda i: (0, 0)), pl.BlockSpec((1, LANES), lambda i: (0, 0))],
        out_specs=[out_blk, out_blk],
        out_shape=[jax.ShapeDtypeStruct((S, FOX_H * LANES), BF16)] * 2,
        scratch_shapes=[pltpu.VMEM((S, LANES), F32)] * 3,
        compiler_params=_cparams(("arbitrary",)),
    )(proj_a, proj_a, proj_f, b_pad)


def _fox_scores(qa, ka, w, tq):
    s = _dot(qa, ka, _NT)
    row = lax.broadcasted_iota(jnp.int32, (tq, tq), 0)
    col = lax.broadcasted_iota(jnp.int32, (tq, tq), 1)
    diag = jnp.where(col <= row, s[:, w * tq:], NEG_INF)
    return diag if w == 0 else jnp.concatenate([s[:, :w * tq], diag], axis=1)


def _fox_fwd(q_aug, k_aug, proj_a):
    tq = FOX_TQ_FWD

    def body(qa_ref, ka_ref, v_ref, o_ref, lse_ref):
        qi = pl.program_id(1)
        lane = lax.broadcasted_iota(jnp.int32, (tq, LANES), 1)
        for w in range(S // tq):
            @pl.when(qi == w)
            def _(w=w):
                width = (w + 1) * tq
                lse = jnp.zeros((tq, LANES), F32)
                for hh in range(HEADS_PER_LB):
                    aug = slice(hh * LANES, (hh + 1) * LANES)
                    sl = slice(hh * HD, (hh + 1) * HD)
                    s = _fox_scores(qa_ref[:, aug], ka_ref[:width, aug], w, tq)
                    m = jnp.max(s, axis=-1, keepdims=True)
                    p = jnp.exp(s - m)
                    l = jnp.sum(p, axis=-1, keepdims=True)
                    o_ref[:, sl] = _dot(p / l, v_ref[:width, sl], _NN)
                    lse = jnp.where(lane == hh, m + jnp.log(l), lse)
                lse_ref[...] = lse

    return pl.pallas_call(
        body, name="fox_fwd", grid=(FOX_HP, S // tq),
        in_specs=[pl.BlockSpec((tq, FOX_AUG), lambda h, i: (i, h)),
                  pl.BlockSpec((S, FOX_AUG), lambda h, i: (0, h)),
                  pl.BlockSpec((S, LANES), lambda h, i: (0, 2 * FOX_HP + h))],
        out_specs=[pl.BlockSpec((tq, LANES), lambda h, i: (i, h)),
                   pl.BlockSpec((None, tq, LANES), lambda h, i: (h, i, 0))],
        out_shape=[jax.ShapeDtypeStruct((S, FOX_W), F32), jax.ShapeDtypeStruct((FOX_HP, S, LANES), F32)],
        compiler_params=_cparams(("parallel", "parallel")),
    )(q_aug, k_aug, proj_a)


def _fox_bwd(q_aug, k_aug, proj_a, lse, do, dproj, deps=()):
    n_q = FOX_NQ

    def body(qa_ref, ka_ref, v_ref, lse_ref, do_ref, *refs):
        dproj_ref, dft_ref, dfs_ref, dk_acc, dv_acc, dq_buf, kv_buf, dq_sems, kv_sems = refs[1 + len(deps):]
        hp = pl.program_id(0)
        t = pl.program_id(1)
        slot = t % 2
        col = pl.multiple_of(hp * LANES, LANES)

        def dq_copy(step, buf_slot):
            rows = pl.ds(pl.multiple_of(step * FOX_TQ, FOX_TQ), FOX_TQ)
            return pltpu.make_async_copy(dq_buf.at[buf_slot], dproj_ref.at[rows, pl.ds(col, LANES)], dq_sems.at[buf_slot])

        @pl.when(t == 0)
        def _():
            dk_acc[...] = jnp.zeros_like(dk_acc)
            dv_acc[...] = jnp.zeros_like(dv_acc)

        @pl.when((t == 0) & (hp == 0))
        def _():
            dfs_ref[...] = jnp.zeros_like(dfs_ref)

        @pl.when(t >= 2)
        def _():
            dq_copy(t - 2, slot).wait()

        lane = lax.broadcasted_iota(jnp.int32, (FOX_TQ, LANES), 1)
        for w in range(n_q):
            @pl.when(t == w)
            def _(w=w):
                width = (w + 1) * FOX_TQ
                lse_all = lse_ref[...]
                sub = lax.broadcasted_iota(jnp.int32, (FOX_H, width), 0)
                dft = jnp.zeros((FOX_TQ, LANES), F32)
                for hh in range(HEADS_PER_LB):
                    aug = slice(hh * LANES, (hh + 1) * LANES)
                    sl = slice(hh * HD, (hh + 1) * HD)
                    head = hp * HEADS_PER_LB + hh
                    qa = qa_ref[:, aug]
                    ka = ka_ref[:width, aug]
                    do_h = do_ref[:, sl]
                    s = _fox_scores(qa, ka, w, FOX_TQ)
                    p = jnp.exp(s - lse_all[:, hh:hh + 1])
                    dp = _dot(do_h, v_ref[:width, sl], _NT)
                    ds = p * (dp - jnp.sum(dp * p, axis=-1, keepdims=True))
                    dft = jnp.where(lane == hh, jnp.sum(ds, axis=-1, keepdims=True), dft)
                    dfs_ref[:, :width] -= jnp.where(sub == head, jnp.sum(ds, axis=0, keepdims=True), 0.0)
                    dq_buf[slot, :, sl] = (_dot(ds, ka[:, :HD], _NN) * SCALE).astype(BF16)
                    dk_acc[:width, sl] += _dot(ds, qa[:, :HD], _TN)
                    dv_acc[:width, sl] += _dot(p, do_h, _TN)
                dft_ref[...] = dft

        dq_copy(t, slot).start()

        @pl.when(t == n_q - 1)
        def _():
            dq_copy(t - 1, 1 - slot).wait()
            dq_copy(t, slot).wait()
            kv_buf[0] = dk_acc[...].astype(BF16)
            kv_buf[1] = dv_acc[...].astype(BF16)
            copies = [pltpu.make_async_copy(
                kv_buf.at[j], dproj_ref.at[:, pl.ds(pl.multiple_of((j + 1) * FOX_W + hp * LANES, LANES), LANES)],
                kv_sems.at[j]) for j in range(2)]
            for cp in copies:
                cp.start()
            for cp in copies:
                cp.wait()

    qblk = pl.BlockSpec((FOX_TQ, FOX_AUG), lambda h, t: (t, h))
    lane_blk = pl.BlockSpec((None, FOX_TQ, LANES), lambda h, t: (h, t, 0))
    return pl.pallas_call(
        body, name="fox_bwd", grid=(FOX_HP, n_q),
        in_specs=[qblk, pl.BlockSpec((S, FOX_AUG), lambda h, t: (0, h)),
                  pl.BlockSpec((S, LANES), lambda h, t: (0, 2 * FOX_HP + h)),
                  lane_blk, pl.BlockSpec((FOX_TQ, LANES), lambda h, t: (t, h)), ANY] + [ANY] * len(deps),
        out_specs=[ANY, lane_blk, pl.BlockSpec((FOX_H, S), lambda h, t: (0, 0))],
        out_shape=[jax.ShapeDtypeStruct((S, D_PAD), BF16),
                   jax.ShapeDtypeStruct((FOX_HP, S, LANES), F32), jax.ShapeDtypeStruct((FOX_H, S), F32)],
        scratch_shapes=[pltpu.VMEM((S, LANES), F32), pltpu.VMEM((S, LANES), F32),
                        pltpu.VMEM((2, FOX_TQ, LANES), BF16), pltpu.VMEM((2, S, LANES), BF16),
                        pltpu.SemaphoreType.DMA((2,)), pltpu.SemaphoreType.DMA((2,))],
        input_output_aliases={5: 0},
        compiler_params=_cparams(("arbitrary", "arbitrary")),
    )(q_aug, k_aug, proj_a, lse, do, dproj, *deps)


def _rope_tables():
    half = ROPE_DIM // 2
    shape = (DIL_G, S, half)
    g = lax.broadcasted_iota(jnp.int32, shape, 0)
    row = lax.broadcasted_iota(jnp.int32, shape, 1)
    r = jnp.left_shift(1, 2 * g)
    per_class = S // r
    pos = (row % per_class) * r + row // per_class
    inv_freq = jnp.power(jnp.float32(ROPE_THETA), -lax.broadcasted_iota(F32, shape, 2) * 2.0 / ROPE_DIM)
    ang = pos.astype(F32) * inv_freq
    return jnp.concatenate([jnp.cos(ang), jnp.sin(ang)], axis=-1)


def _rope_factors(cs):
    half = ROPE_DIM // 2
    i = lax.broadcasted_iota(jnp.int32, (2 * half, 2 * LANES), 0)
    l = lax.broadcasted_iota(jnp.int32, (2 * half, 2 * LANES), 1)
    hit = ((l < LANES) & (i == l % half)) | ((l >= LANES) & (i - half == l % half))
    spread = jnp.where(hit, 1.0, 0.0).astype(BF16)
    hi = cs.astype(BF16)
    r1 = cs - hi.astype(F32)
    mid = r1.astype(BF16)
    lo = (r1 - mid.astype(F32)).astype(BF16)
    full = (_dot(hi, spread, _NN) + _dot(mid, spread, _NN)) + _dot(lo, spread, _NN)
    cos, sin = full[:, :LANES], full[:, LANES:]
    lane = lax.broadcasted_iota(jnp.int32, cos.shape, 1) % HD
    t_self = jnp.where(lane < ROPE_DIM, cos, 1.0)
    t_up = jnp.where(lane < half, -sin, 0.0)
    t_dn = jnp.where((lane >= half) & (lane < ROPE_DIM), sin, 0.0)
    return t_self, t_up, t_dn


def _residue_pieces(r):
    if r == 1:
        return [(slice(i, i + 512), slice(i, i + 512)) for i in range(0, S, 512)]
    n = S // r
    return [(pl.ds(j, n, stride=r), slice(j * n, (j + 1) * n)) for j in range(r)]


def _rope_apply(x, a, u, d, backward):
    half = ROPE_DIM // 2
    if backward:
        return x * a + pltpu.roll(x * u, half, 1) + pltpu.roll(x * d, LANES - half, 1)
    return x * a + pltpu.roll(x, LANES - half, 1) * u + pltpu.roll(x, half, 1) * d


def _rope_cache_factors(cs_ref, fac, first_step_of_group):
    @pl.when(first_step_of_group)
    def _():
        for lo in range(0, S, 512):
            for j, f in enumerate(_rope_factors(cs_ref[lo:lo + 512, :])):
                fac[j, lo:lo + 512, :] = f


LB_PER_CB = CB // LANES
ROPE_NB = 3 * DIL_G * LB_PER_CB


def _rope_step(s):
    per_group = 3 * LB_PER_CB
    return s // per_group, (s % per_group) // LB_PER_CB, s % LB_PER_CB


def _rope_split(proj_b, tables):
    def body(x_ref, cs_ref, o_ref, fac):
        g, t, hf = _rope_step(pl.program_id(0))
        _rope_cache_factors(cs_ref, fac, (t == 0) & (hf == 0))
        for gs in range(DIL_G):
            @pl.when(g == gs)
            def _(gs=gs):
                for tok, sub in _residue_pieces(DIL_R[gs]):
                    x = x_ref[tok, :]
                    y = _rope_apply(x, fac[0, sub, :], fac[1, sub, :], fac[2, sub, :], False)
                    o_ref[sub, :] = jnp.where(t < 2, y, x).astype(BF16)

    def in_index(s):
        g, t, hf = _rope_step(s)
        return 0, (t * DIL_G + g) * LB_PER_CB + hf

    def out_index(s):
        g, t, hf = _rope_step(s)
        return t * DIL_G + g, 0, hf

    tab = pl.BlockSpec((None, S, 2 * (ROPE_DIM // 2)), lambda s: (_rope_step(s)[0], 0, 0))
    return pl.pallas_call(
        body, name="rope_split", grid=(ROPE_NB,),
        in_specs=[pl.BlockSpec((S, LANES), in_index), tab],
        out_specs=pl.BlockSpec((None, S, LANES), out_index),
        out_shape=jax.ShapeDtypeStruct((3 * DIL_G, S, CB), BF16),
        scratch_shapes=[pltpu.VMEM((3, S, LANES), F32)],
        compiler_params=_cparams(("arbitrary",)),
    )(proj_b, tables)


def _rope_merge_bwd(dq3, dk3, dv3, tables, dproj):
    def body(dq_ref, dk_ref, dv_ref, cs_ref, _, o_ref, tmp, fac):
        s = pl.program_id(0)
        g, t, hf = _rope_step(s)
        _rope_cache_factors(cs_ref, fac, (t == 0) & (hf == 0) & (s < ROPE_NB))
        for gs in range(DIL_G):
            @pl.when((g == gs) & (s < ROPE_NB))
            def _(gs=gs):
                for tok, sub in _residue_pieces(DIL_R[gs]):
                    x = jnp.where(t == 0, dq_ref[sub, :], jnp.where(t == 1, dk_ref[sub, :], dv_ref[sub, :]))
                    y = _rope_apply(x, fac[0, sub, :], fac[1, sub, :], fac[2, sub, :], True)
                    tmp[tok, :] = jnp.where(t < 2, y, x)
                o_ref[...] = tmp[...].astype(BF16)

        @pl.when(s >= ROPE_NB)
        def _():
            o_ref[...] = jnp.zeros_like(o_ref)

    def src_spec(own):
        def index(s):
            g, t, hf = _rope_step(jnp.minimum(s, ROPE_NB - 1))
            return g, 0, jnp.where(t == own, hf, jnp.where(t > own, LB_PER_CB - 1, 0))
        return pl.BlockSpec((None, S, LANES), index)

    def out_index(s):
        g, t, hf = _rope_step(s)
        col = SEG_B // LANES + (t * DIL_G + g) * LB_PER_CB + hf
        return 0, jnp.where(s < ROPE_NB, col, SEG_Z // LANES + s - ROPE_NB)

    tab = pl.BlockSpec((None, S, 2 * (ROPE_DIM // 2)), lambda s: (_rope_step(jnp.minimum(s, ROPE_NB - 1))[0], 0, 0))
    return pl.pallas_call(
        body, name="rope_merge_bwd", grid=(ROPE_NB + LB_PER_CB,),
        in_specs=[src_spec(0), src_spec(1), src_spec(2), tab, ANY],
        out_specs=pl.BlockSpec((S, LANES), out_index),
        out_shape=jax.ShapeDtypeStruct((S, D_PAD), BF16),
        scratch_shapes=[pltpu.VMEM((S, LANES), F32), pltpu.VMEM((3, S, LANES), F32)],
        input_output_aliases={4: 0},
        compiler_params=_cparams(("arbitrary",)),
    )(dq3, dk3, dv3, tables, dproj)


DIL_NB = S // DIL_BLK


_BQK = (((2,), (2,)), ((0,), (0,)))
_BQD = (((2,), (1,)), ((0,), (0,)))
_BKD = (((1,), (1,)), ((0,), (0,)))


def _dil_mask(g):
    shape = (DIL_NB, DIL_BLK, 2 * DIL_BLK)
    blocks_per_seq = jnp.right_shift(DIL_NB, 2 * g)
    has_prev = jnp.bitwise_and(lax.broadcasted_iota(jnp.int32, shape, 0), blocks_per_seq - 1) != 0
    a = lax.broadcasted_iota(jnp.int32, shape, 1)
    kk = lax.broadcasted_iota(jnp.int32, shape, 2)
    diff = DIL_BLK + a - kk
    return (diff >= 0) & (diff <= DIL_BLK) & ((kk >= DIL_BLK) | has_prev)


def _blocks(t):
    return t.reshape(DIL_NB, DIL_BLK, t.shape[-1])


def _with_prev(t):
    prev = jnp.concatenate([jnp.zeros((DIL_BLK, t.shape[-1]), t.dtype), t[:-DIL_BLK]], axis=0)
    return jnp.concatenate([_blocks(prev), _blocks(t)], axis=1)


def _fold_prev(t2):
    n = t2.shape[-1]
    to_prev = t2[:, :DIL_BLK].reshape(S, n)
    own = t2[:, DIL_BLK:].reshape(S, n)
    return own + jnp.concatenate([to_prev[DIL_BLK:], jnp.zeros((DIL_BLK, n), t2.dtype)], axis=0)


def _dil_group_spec(t, width=DIL_OUT):
    return pl.BlockSpec((None, S, width), lambda g: (t * DIL_G + g, 0, 0))


def _dil_fwd(qkv3):
    def body(q_ref, k_ref, v_ref, o_ref, lse_ref):
        ok = _dil_mask(pl.program_id(0))
        lane = lax.broadcasted_iota(jnp.int32, (S, LANES), 1)
        lse = jnp.zeros((S, LANES), F32)
        for h in range(DIL_HG):
            sl = slice(h * HD, (h + 1) * HD)
            s = jnp.where(ok, _dot(_blocks(q_ref[:, sl]), _with_prev(k_ref[:, sl]), _BQK) * SCALE, NEG_INF)
            m = jnp.max(s, axis=-1, keepdims=True)
            p = jnp.exp(s - m)
            l = jnp.sum(p, axis=-1, keepdims=True)
            o_ref[:, sl] = _dot(p / l, _with_prev(v_ref[:, sl]), _BQD).reshape(S, HD)
            lse = jnp.where(lane == h, (m + jnp.log(l)).reshape(S, 1), lse)
        lse_ref[...] = lse

    return pl.pallas_call(
        body, name="dil_fwd", grid=(DIL_G,),
        in_specs=[_dil_group_spec(0), _dil_group_spec(1), _dil_group_spec(2)],
        out_specs=[_dil_group_spec(0), _dil_group_spec(0, LANES)],
        out_shape=[jax.ShapeDtypeStruct((DIL_G, S, DIL_OUT), F32), jax.ShapeDtypeStruct((DIL_G, S, LANES), F32)],
        compiler_params=_cparams(("parallel",)),
    )(qkv3, qkv3, qkv3)


def _dil_bwd(qkv3, lse3, do3, c3):
    def body(q_ref, k_ref, v_ref, lse_ref, do_ref, c_ref, dq_ref, dk_ref, dv_ref):
        ok = _dil_mask(pl.program_id(0))
        lse = lse_ref[...]
        c = c_ref[...]
        for h in range(DIL_HG):
            sl = slice(h * HD, (h + 1) * HD)
            q = _blocks(q_ref[:, sl])
            k2 = _with_prev(k_ref[:, sl])
            do_h = _blocks(do_ref[:, sl])
            s = jnp.where(ok, _dot(q, k2, _BQK) * SCALE, NEG_INF)
            p = jnp.exp(s - _blocks(lse[:, h:h + 1]))
            dp = _dot(do_h, _with_prev(v_ref[:, sl]), _BQK)
            ds = p * (dp - _blocks(c[:, h:h + 1]))
            dsb = (ds * SCALE).astype(BF16)
            dq_ref[:, sl] = _dot(dsb, k2, _BQD).reshape(S, HD)
            dk_ref[:, sl] = _fold_prev(_dot(dsb, q, _BKD))
            dv_ref[:, sl] = _fold_prev(_dot(p, do_h, _BKD))

    return pl.pallas_call(
        body, name="dil_bwd", grid=(DIL_G,),
        in_specs=[_dil_group_spec(0), _dil_group_spec(1), _dil_group_spec(2), _dil_group_spec(0, LANES),
                  _dil_group_spec(0), _dil_group_spec(0, LANES)],
        out_specs=[_dil_group_spec(0)] * 3,
        out_shape=[jax.ShapeDtypeStruct((DIL_G, S, DIL_OUT), F32)] * 3,
        compiler_params=_cparams(("parallel",)),
    )(qkv3, qkv3, qkv3, lse3, do3, c3)


COMB_T = 256


def _dil_combine(o3, lse3):
    heads_per_lb = LANES // HD

    def body(o_ref, lse_ref, ob_ref, al_ref, *scratch):
        o_tok = [scratch[LB_PER_CB * gg:LB_PER_CB * (gg + 1)] for gg in range(DIL_G)]
        lse_tok = scratch[LB_PER_CB * DIL_G:]
        g = pl.program_id(0)
        for gs in range(DIL_G):
            @pl.when(g == gs)
            def _(gs=gs):
                for tok, sub in _residue_pieces(DIL_R[gs]):
                    for hf in range(LB_PER_CB):
                        o_tok[gs][hf][tok, :] = o_ref[sub, hf * LANES:(hf + 1) * LANES]
                    lse_tok[gs][tok, :] = lse_ref[sub, :]

        @pl.when(g == DIL_G - 1)
        def _():
            def chunk(i, carry):
                rows = pl.ds(pl.multiple_of(i * COMB_T, COMB_T), COMB_T)
                lse = [lse_tok[gg][rows, :] for gg in range(DIL_G)]
                m = jnp.maximum(jnp.maximum(lse[0], lse[1]), lse[2])
                e = [jnp.exp(lse[gg] - m) for gg in range(DIL_G)]
                den = (e[0] + e[1]) + e[2]
                al = [e[gg] / den for gg in range(DIL_G)]
                for gg in range(DIL_G):
                    al_ref[gg, rows, :] = al[gg]
                for h in range(DIL_HG):
                    hf, sl = h // heads_per_lb, slice((h % heads_per_lb) * HD, (h % heads_per_lb + 1) * HD)
                    acc = al[0][:, h:h + 1] * o_tok[0][hf][rows, sl]
                    for gg in range(1, DIL_G):
                        acc = acc + al[gg][:, h:h + 1] * o_tok[gg][hf][rows, sl]
                    ob_ref[rows, h * HD:(h + 1) * HD] = acc
                return carry

            lax.fori_loop(0, S // COMB_T, chunk, 0)

    return pl.pallas_call(
        body, name="dil_combine", grid=(DIL_G,),
        in_specs=[pl.BlockSpec((None, S, DIL_OUT), lambda g: (g, 0, 0)),
                  pl.BlockSpec((None, S, LANES), lambda g: (g, 0, 0))],
        out_specs=[pl.BlockSpec((S, DIL_OUT), lambda g: (0, 0)),
                   pl.BlockSpec((DIL_G, S, LANES), lambda g: (0, 0, 0))],
        out_shape=[jax.ShapeDtypeStruct((S, DIL_OUT), F32), jax.ShapeDtypeStruct((DIL_G, S, LANES), F32)],
        scratch_shapes=[pltpu.VMEM((S, LANES), F32)] * (DIL_G * (LB_PER_CB + 1)),
        compiler_params=_cparams(("arbitrary",)),
    )(o3, lse3)


def _dil_combine_bwd(dob, ob, alpha, deps=()):
    heads_per_lb = LANES // HD

    def body(dob_ref, ob_ref, al_ref, *refs):
        do_ref, c_ref = refs[len(deps):]
        g = pl.program_id(0)
        hf = pl.program_id(1)
        for gs in range(DIL_G):
            @pl.when(g == gs)
            def _(gs=gs):
                for tok, sub in _residue_pieces(DIL_R[gs]):
                    dob = dob_ref[tok, :]
                    prod = dob * ob_ref[tok, :]
                    al = al_ref[tok, :]
                    lane = lax.broadcasted_iota(jnp.int32, al.shape, 1)
                    c = jnp.where(hf == 0, 0.0, c_ref[sub, :])
                    for hh in range(heads_per_lb):
                        sl = slice(hh * HD, (hh + 1) * HD)
                        head = hf * heads_per_lb + hh
                        a = jnp.sum(jnp.where(lane == head, al, 0.0), axis=-1, keepdims=True)
                        do_ref[sub, sl] = (a * dob[:, sl]).astype(BF16)
                        c = jnp.where(lane == head, a * jnp.sum(prod[:, sl], axis=-1, keepdims=True), c)
                    c_ref[sub, :] = c

    half = pl.BlockSpec((S, LANES), lambda g, hf: (0, hf))
    return pl.pallas_call(
        body, name="dil_combine_bwd", grid=(DIL_G, LB_PER_CB),
        in_specs=[half, half, pl.BlockSpec((None, S, LANES), lambda g, hf: (g, 0, 0))] + [ANY] * len(deps),
        out_specs=[pl.BlockSpec((None, S, LANES), lambda g, hf: (g, 0, hf)),
                   pl.BlockSpec((None, S, LANES), lambda g, hf: (g, 0, 0))],
        out_shape=[jax.ShapeDtypeStruct((DIL_G, S, DIL_OUT), BF16), jax.ShapeDtypeStruct((DIL_G, S, LANES), F32)],
        compiler_params=_cparams(("parallel", "arbitrary")),
    )(dob, ob, alpha, *deps)


def _local_step(x, tgt, h1, g_attn, g_mlp, g_final, b_pad, w_in_t, hooks):
    tables = _rope_tables()

    first = hooks.first_deps()
    proj_a = _mm(h1, w_in_t, mode="nt", tm=S, tn=512, n=3 * FOX_W, out_dtypes=[BF16], name="proj_a", deps=first)
    proj_b = _mm(h1, w_in_t, mode="nt", tm=S, tn=DIL_W, n=3 * DIL_W, b_off=SEG_B // DIL_W, out_dtypes=[F32], name="proj_b",
                deps=first)
    proj_g = _mm(h1, w_in_t, mode="nt", tm=S, tn=D, n=2 * D, b_off=SEG_G // D, out_dtypes=[F32], name="proj_g",
                deps=first)
    proj_f = _mm(h1, w_in_t, mode="nt", tm=S, tn=LANES, n=LANES, b_off=SEG_F // LANES, out_dtypes=[F32], name="proj_f",
                deps=first)

    q_aug, k_aug = _fox_prepare(proj_a, proj_f, b_pad)
    oa, lse_a = _fox_fwd(q_aug, k_aug, proj_a)

    qkv3 = _rope_split(proj_b, tables)
    o3, lse3 = _dil_fwd(qkv3)
    ob, alpha = _dil_combine(o3, lse3)

    w_a, w_b, w_out = hooks.mixer_weights(ob)
    ya, yb, mixed, x2, h2 = _mixer_tail(oa, ob, w_a, w_b, w_out, proj_g, x, g_mlp)
    w_up_sh, w_down = hooks.mlp_weights(h2)

    def up_epilogue(acc):
        r = jnp.maximum(acc, 0.0)
        return acc, r * r

    u, act = _mm(h2, w_up_sh, mode="nn", tm=S, tn=DFF // N_DEV, b_sharded=True, out_dtypes=[F32, BF16],
                 name="mlp_up", epilogue=up_epilogue)

    def loss_epilogue(acc, rows, vecs):
        dx3, dg, loss = _final_loss_rows(rows[0] + acc, vecs[0], rows[1])
        return (dx3,), (dg, loss)

    dx3, dg_final, loss = _mm_rows(act, w_down, tm=ROWS_TM, name="mlp_down_loss", epilogue=loss_epilogue,
                                   rows=(x2, tgt), vecs=(g_final,), row_out=(F32,), vec_out=(D, LANES))

    def rms_bwd_epilogue(acc, rows, vecs):
        dx, dg = _rms_bwd_rows(acc, rows[0], rows[1], vecs[0])
        return (dx,), (dg,)

    du = _mm(dx3, w_down, mode="nt", tm=S, tn=512, out_dtypes=[BF16], name="mlp_down_bwd",
             epilogue=lambda acc, u_t: (acc * (2.0 * jnp.maximum(u_t, 0.0)),), extras=(u,))
    dw_down = _mm(act, dx3, mode="tn", tm=512, tn=D, out_dtypes=[F32], name="dw_down")
    dw_up_sh = _mm(h2, du, mode="tn", tm=D, tn=DFF // N_DEV, out_sharded=True, out_dtypes=[F32], name="dw_up")
    dx2, dg_mlp = _mm_rows(du, w_up_sh, b_sharded=True, tm=ROWS_TM, name="mlp_up_bwd", epilogue=rms_bwd_epilogue,
                           rows=(x2, dx3), vecs=(g_mlp,), row_out=(F32,), vec_out=(D,),
                           deps=hooks.mlp_grads(dw_up_sh, dw_down))

    dw_out = _mm(mixed, dx2, mode="tn", tm=D, tn=D, out_dtypes=[F32], name="dw_out")

    dyab, dproj = _out_proj_gate_bwd(dx2, w_out, proj_g, ya, yb)
    doa = _mm(dyab, w_a, mode="nt", tm=S, tn=FOX_W, k=D, a_off=0, out_dtypes=[BF16], name="branch_a_bwd")
    dw_a = _mm(oa, dyab, mode="tn", tm=FOX_W, tn=D, n=D, b_off=0, out_dtypes=[F32], name="dw_branch_a")
    dob = _mm(dyab, w_b, mode="nt", tm=S, tn=CB, k=D, a_off=1, out_dtypes=[F32], name="branch_b_bwd")
    dw_b = _mm(ob, dyab, mode="tn", tm=DIL_OUT, tn=D, n=D, b_off=1, out_dtypes=[F32], name="dw_branch_b")

    dproj, dft, dfs = _fox_bwd(q_aug, k_aug, proj_a, lse_a, doa, dproj, deps=hooks.mixer_grads(dw_a, dw_b, dw_out))
    dproj, db = _fox_dscan(dft, dfs, proj_f, b_pad, dproj)

    do3, c3 = _dil_combine_bwd(dob, ob, alpha, deps=hooks.mid_backward(db))
    dq3, dk3, dv3 = _dil_bwd(qkv3, lse3, do3, c3)
    dproj = _rope_merge_bwd(dq3, dk3, dv3, tables, dproj)

    dw_in_t = _mm(dproj, h1, mode="tn", tm=DW_IN_TM, tn=D, out_dtypes=[F32], name="dw_in")
    dx, dg_attn = _mm_rows(dproj, w_in_t, tm=ROWS_TM // 2, name="proj_bwd", epilogue=rms_bwd_epilogue,
                           rows=(x, dx2), vecs=(g_attn,), row_out=(F32,), vec_out=(D,),
                           deps=hooks.w_in_grad(dw_in_t))

    return loss, dx, (dg_attn, db, dg_mlp, dg_final)


SHARD_SHAPES = ((D_IN // N_DEV, D), (FOX_W, D // N_DEV), (DIL_OUT, D // N_DEV), (D // N_DEV, D),
                (D, DFF // N_DEV), (DFF // N_DEV, D))
W_NAMES = ("w_in", "w_a", "w_b", "w_out", "w_up", "w_down")
AG_COPIES = 7
HBM = pl.BlockSpec(memory_space=pltpu.HBM)
SEM = pl.BlockSpec(memory_space=pltpu.SEMAPHORE)


def _place():
    return lax.axis_index("x"), lax.axis_index("y"), lax.axis_index("c")


W_SLAB = 752
W_WIN = 272
PAD_BLK = 256


def _pad_runs():
    runs = sorted((pad, pad + b - a, p, a) for p in range(N_DEV) for a, b, pad in _shard_pieces(p))
    blocks = []
    for k in range(D_PAD // PAD_BLK):
        lo, hi = k * PAD_BLK, (k + 1) * PAD_BLK
        blocks.append([(max(lo, r0) - lo, min(hi, r1) - lo, p, a + max(lo, r0) - r0)
                       for r0, r1, p, a in runs if max(lo, r0) < min(hi, r1)])
    return blocks


def _gather_w_in(shard, x_in, g_attn, late_shards):
    blocks = _pad_runs()

    norm_t = 512
    n_chunks = S // norm_t

    n_late = len(late_shards)
    late_shapes = [t.shape for t in late_shards]

    def body(x_ref, xin_ref, g_ref, *refs):
        late_in, (out_ref, h1_ref), late_out = refs[:n_late], refs[n_late:n_late + 2], refs[n_late + 2:2 * n_late + 2]
        scratch = refs[2 * n_late + 2:]
        stage, land, xbuf = scratch[:3]
        late_stage, late_cast = scratch[3:3 + n_late], scratch[3 + n_late:3 + 2 * n_late]
        send_sems, recv_sems, load_sem, x_sems, late_sems, store_sems = scratch[3 + 2 * n_late:]
        x, y, c = _place()
        me, sibling = (x, y, c), (x, y, 1 - c)
        chips = [(1 - x, y), (x, 1 - y), (1 - x, 1 - y)]

        def slot(px, py, pc):
            return land.at[4 * px + 2 * py + pc]

        def copy(k, block, to):
            return pltpu.make_async_remote_copy(
                src_ref=slot(*block), dst_ref=slot(*block), send_sem=send_sems.at[k], recv_sem=recv_sems.at[k],
                device_id=to, device_id_type=MESH)

        load = pltpu.make_async_copy(x_ref, stage, load_sem)
        load.start()
        load.wait()
        land[4 * x + 2 * y + c] = stage[...].astype(BF16)
        sent = [copy(0, me, sibling)] + [copy(1 + j, me, (*chip, c)) for j, chip in enumerate(chips)]
        for cp in sent:
            cp.start()

        def x_load(i):
            return pltpu.make_async_copy(xin_ref.at[pl.ds(i * norm_t, norm_t)], xbuf.at[i % 2], x_sems.at[i % 2])

        late_loads = [pltpu.make_async_copy(late_in[i], late_stage[i], late_sems.at[i]) for i in range(n_late)]
        for ld in late_loads:
            ld.start()
        x_load(0).start()
        for i in range(n_chunks):
            if i + 1 < n_chunks:
                x_load(i + 1).start()
            x_load(i).wait()
            h1_ref[i * norm_t:(i + 1) * norm_t, :] = _rms_rows(xbuf[i % 2], g_ref[...]).astype(BF16)
        stores = []
        for i in range(n_late):
            late_loads[i].wait()
            late_cast[i][...] = late_stage[i][...].astype(BF16)
            stores.append(pltpu.make_async_copy(late_cast[i], late_out[i].at[4 * x + 2 * y + c], store_sems.at[i]))
            stores[-1].start()

        for j, chip in enumerate(chips):
            copy(1 + j, (*chip, c), me).wait_recv()
            sent.append(copy(4 + j, (*chip, c), sibling))
            sent[-1].start()
        copy(0, sibling, me).wait_recv()
        for j, chip in enumerate(chips):
            copy(4 + j, (*chip, 1 - c), me).wait_recv()
        for cp in sent:
            cp.wait_send()
        for st in stores:
            st.wait()

        row = lax.broadcasted_iota(jnp.int32, (PAD_BLK, D), 0)
        for k, runs in enumerate(blocks):
            out = jnp.zeros((PAD_BLK, D), F32)
            for o_lo, o_hi, p, a in runs:
                start = min(a // 16 * 16, W_SLAB - W_WIN)
                win = land[p, start:start + W_WIN, :].astype(F32)
                moved = pltpu.roll(win, (o_lo - (a - start)) % W_WIN, 0)[:PAD_BLK]
                out = jnp.where((row >= o_lo) & (row < o_hi), moved, out)
            out_ref[k * PAD_BLK:(k + 1) * PAD_BLK, :] = out.astype(BF16)

    return pl.pallas_call(
        body, name="all_gather_w_in",
        out_shape=[jax.ShapeDtypeStruct((D_PAD, D), BF16), jax.ShapeDtypeStruct((S, D), BF16)]
                  + [jax.ShapeDtypeStruct((N_DEV,) + sh, BF16) for sh in late_shapes],
        in_specs=[ANY, ANY, pl.BlockSpec(memory_space=pltpu.VMEM)] + [ANY] * n_late,
        out_specs=[pl.BlockSpec(memory_space=pltpu.VMEM)] * 2 + [ANY] * n_late,
        scratch_shapes=[pltpu.VMEM((W_SLAB, D), F32), pltpu.VMEM((N_DEV, W_SLAB, D), BF16),
                        pltpu.VMEM((2, norm_t, D), F32)]
                       + [pltpu.VMEM(sh, F32) for sh in late_shapes] + [pltpu.VMEM(sh, BF16) for sh in late_shapes]
                       + [pltpu.SemaphoreType.DMA((AG_COPIES,)), pltpu.SemaphoreType.DMA((AG_COPIES,)),
                          pltpu.SemaphoreType.DMA(()), pltpu.SemaphoreType.DMA((2,)),
                          pltpu.SemaphoreType.DMA((n_late,)), pltpu.SemaphoreType.DMA((n_late,))],
        compiler_params=_cparams(None),
    )(shard, x_in, g_attn, *late_shards)


PEER_COPIES = N_DEV - 1
SIBLING_COPIES = 4
CHIP_COPIES = 3


def _peer_copies(_, land_refs, send_sems, recv_sems):
    x, y, c = _place()
    mine = 4 * x + 2 * y + c
    copies = []
    for i, ref in enumerate(land_refs):
        for k in range(1, N_DEV):
            peer = (x ^ (k >> 2), y ^ ((k >> 1) & 1), c ^ (k & 1))
            n = i * PEER_COPIES + k - 1
            copies.append(pltpu.make_async_remote_copy(
                src_ref=ref.at[mine], dst_ref=ref.at[mine], send_sem=send_sems.at[n], recv_sem=recv_sems.at[n],
                device_id=peer, device_id_type=MESH))
    return copies


def _sibling_copies(g_refs, r_refs, send_sems, recv_sems):
    x, y, c = _place()
    return [pltpu.make_async_remote_copy(
        src_ref=g_refs[i].at[2 * k + (1 - c)], dst_ref=r_refs[i].at[k],
        send_sem=send_sems.at[SIBLING_COPIES * i + k], recv_sem=recv_sems.at[SIBLING_COPIES * i + k],
        device_id=(x, y, 1 - c), device_id_type=MESH) for i in range(len(g_refs)) for k in range(SIBLING_COPIES)]


def _sibling_half_copies(g_refs, r_refs, send_sems, recv_sems):
    x, y, c = _place()
    return [pltpu.make_async_remote_copy(
        src_ref=g_refs[i].at[k], dst_ref=r_refs[i].at[k],
        send_sem=send_sems.at[SIBLING_COPIES * i + k], recv_sem=recv_sems.at[SIBLING_COPIES * i + k],
        device_id=(x, y, 1 - c), device_id_type=MESH) for i in range(len(g_refs)) for k in range(SIBLING_COPIES)]


def _chip_copies(p_refs, r_refs, send_sems, recv_sems):
    x, y, c = _place()
    chips = [(1 - x, y), (x, 1 - y), (1 - x, 1 - y)]
    return [pltpu.make_async_remote_copy(
        src_ref=p_refs[i].at[2 * cx + cy], dst_ref=r_refs[i].at[j],
        send_sem=send_sems.at[CHIP_COPIES * i + j], recv_sem=recv_sems.at[CHIP_COPIES * i + j],
        device_id=(cx, cy, c), device_id_type=MESH) for i in range(len(p_refs)) for j, (cx, cy) in enumerate(chips)]


def _in_hbm(a):
    return pltpu.with_memory_space_constraint(a, pltpu.HBM)


def _exchange_start(name, copies_fn, n_copies, srcs, lands, after=()):
    n_s, n_l, n_a = len(srcs), len(lands), len(after)

    def body(*refs):
        outs = refs[n_s + n_l + n_a:]
        for cp in copies_fn(refs[:n_s], refs[n_s:n_s + n_l], outs[0], outs[1]):
            cp.start()
        outs[-1][...] = jnp.zeros_like(outs[-1])

    res = pl.pallas_call(
        body, name=name,
        out_shape=[pltpu.SemaphoreType.DMA((n_copies,)), pltpu.SemaphoreType.DMA((n_copies,))]
                  + [pltpu.HBM(t.shape, t.dtype) for t in (*srcs, *lands)] + [jax.ShapeDtypeStruct((8, LANES), F32)],
        in_specs=[HBM] * (n_s + n_l) + [ANY] * n_a,
        out_specs=[SEM, SEM] + [HBM] * (n_s + n_l) + [pl.BlockSpec(memory_space=pltpu.VMEM)],
        input_output_aliases={i: 2 + i for i in range(n_s + n_l)},
        compiler_params=pltpu.CompilerParams(has_side_effects=pltpu.SideEffectType.DATAFLOW_SIDE_EFFECTING),
    )(*[_in_hbm(t) for t in (*srcs, *lands)], *after)
    return res[0], res[1], list(res[2:2 + n_s]), list(res[2 + n_s:2 + n_s + n_l]), res[-1]


def _exchange_wait(name, copies_fn, started, after):
    send_sems, recv_sems, srcs, lands, _ = started
    n_s, n_l = len(srcs), len(lands)

    def body(*refs):
        for cp in copies_fn(refs[:n_s], refs[n_s:n_s + n_l], refs[n_s + n_l], refs[n_s + n_l + 1]):
            cp.wait_send()
            cp.wait_recv()

    res = pl.pallas_call(
        body, name=name,
        out_shape=[pltpu.HBM(t.shape, t.dtype) for t in (*srcs, *lands)],
        in_specs=[HBM] * (n_s + n_l) + [SEM, SEM, ANY],
        out_specs=[HBM] * (n_s + n_l),
        input_output_aliases={i: i for i in range(n_s + n_l)},
        compiler_params=pltpu.CompilerParams(has_side_effects=pltpu.SideEffectType.DATAFLOW_SIDE_EFFECTING),
    )(*srcs, *lands, send_sems, recv_sems, after)
    return list(res[:n_s]), list(res[n_s:])


def _shard_block(shape):
    rows, cols = shape
    if rows * cols <= 256 * 1024:
        return rows, cols
    if rows % 256:
        return rows, 256
    return 256, cols


def _chip_partial(ids, g_stack, recv1, name):
    shape = g_stack.shape[1:]
    br, bc = _shard_block(shape)
    whole = g_stack.shape[0] == N_DEV

    def body(ids_ref, g_ref, r_ref, pb_ref, own_ref):
        s = g_ref[...] + r_ref[...]
        pb_ref[...] = s.astype(BF16)

        @pl.when(pl.program_id(2) == ids_ref[1])
        def _():
            own_ref[...] = s

    grid_spec = pltpu.PrefetchScalarGridSpec(
        num_scalar_prefetch=1, grid=(shape[0] // br, shape[1] // bc, 4),
        in_specs=[pl.BlockSpec((None, br, bc), lambda r, q, k, ids: (2 * k + ids[0] if whole else k, r, q)),
                  pl.BlockSpec((None, br, bc), lambda r, q, k, ids: (k, r, q))],
        out_specs=[pl.BlockSpec((None, br, bc), lambda r, q, k, ids: (k, r, q)),
                   pl.BlockSpec((br, bc), lambda r, q, k, ids: (r, q))])
    return pl.pallas_call(
        body, name=name, grid_spec=grid_spec,
        out_shape=[jax.ShapeDtypeStruct((4,) + shape, BF16), jax.ShapeDtypeStruct(shape, F32)],
        compiler_params=_cparams(("parallel", "parallel", "arbitrary")),
    )(ids, g_stack, recv1)


def _adamw(w, g, m, v):
    m = ADAM_B1 * m + (1.0 - ADAM_B1) * g
    v = ADAM_B2 * v + (1.0 - ADAM_B2) * (g * g)
    m_hat = m / (1.0 - ADAM_B1 ** ADAM_STEP)
    v_hat = v / (1.0 - ADAM_B2 ** ADAM_STEP)
    delta = -ADAM_LR * (m_hat / (jnp.sqrt(v_hat) + ADAM_EPS) + ADAM_WD * w)
    return delta, m, v


def _reduce_adamw(own, recv2, w, m, v, name, deps=()):
    shape = own.shape
    br, bc = _shard_block(shape)

    def body(own_ref, r_ref, w_ref, m_ref, v_ref, *refs):
        g_ref, d_ref, nm_ref, nv_ref = refs[len(deps):]
        g = own_ref[...]
        for j in range(3):
            g = g + r_ref[j].astype(F32)
        delta, nm, nv = _adamw(w_ref[...], g, m_ref[...], v_ref[...])
        g_ref[...] = g
        d_ref[...] = delta
        nm_ref[...] = nm
        nv_ref[...] = nv

    blk = pl.BlockSpec((br, bc), lambda r, q: (r, q))
    return pl.pallas_call(
        body, name=name, grid=(shape[0] // br, shape[1] // bc),
        in_specs=[blk, pl.BlockSpec((3, br, bc), lambda r, q: (0, r, q)), blk, blk, blk] + [ANY] * len(deps),
        out_specs=[blk] * 4, out_shape=[jax.ShapeDtypeStruct(shape, F32)] * 4,
        compiler_params=_cparams(("parallel", "parallel")),
    )(own, recv2, w, m, v, *deps)


def _small_adamw(stack, w, m, v):
    def body(s_ref, w_ref, m_ref, v_ref, go_ref, d_ref, nm_ref, nv_ref):
        g = s_ref[0]
        for s in range(1, N_DEV):
            g = g + s_ref[s]
        delta, nm, nv = _adamw(w_ref[...], g, m_ref[...], v_ref[...])
        go_ref[...] = g
        d_ref[...] = delta
        nm_ref[...] = nm
        nv_ref[...] = nv

    vm = pl.BlockSpec(memory_space=pltpu.VMEM)
    return pl.pallas_call(
        body, name="small_adamw",
        in_specs=[vm] * 4, out_specs=[vm] * 4,
        out_shape=[jax.ShapeDtypeStruct((SMALL_R, LANES), F32)] * 4,
    )(stack, w, m, v)


def _cols_to_whole(stack):
    rows = stack.shape[1]
    return stack.transpose(1, 0, 2).reshape(rows, -1)


def _whole_to_cols(t):
    rows = t.shape[0]
    return t.reshape(rows, N_DEV, -1).transpose(1, 0, 2)


W_IN_SEGMENTS = ((0, 3 * FOX_W, SEG_A), (3 * FOX_W, 3 * FOX_W + FOX_H, SEG_F),
                 (3 * FOX_W + FOX_H, 3 * FOX_W + FOX_H + 3 * DIL_W, SEG_B), (3 * FOX_W + FOX_H + 3 * DIL_W, D_IN, SEG_G))


def _shard_pieces(p):
    rows = D_IN // N_DEV
    lo, hi = p * rows, (p + 1) * rows
    return [(max(lo, a) - lo, min(hi, b) - lo, pad + max(lo, a) - a)
            for a, b, pad in W_IN_SEGMENTS if max(lo, a) < min(hi, b)]


def _unpad_runs(p):
    rows = D_IN // N_DEV
    blocks = []
    for lo in range(0, rows, PAD_BLK):
        hi = min(lo + PAD_BLK, rows)
        blocks.append([(max(lo, a) - lo, min(hi, b) - lo, pad + max(lo, a) - a)
                       for a, b, pad in _shard_pieces(p) if max(lo, a) < min(hi, b)])
    return blocks


def _unpad_dw_in_t(dwp, ids, kept, deps=()):
    rows = D_IN // N_DEV
    plans = [_unpad_runs(p) for p in range(N_DEV)]
    windows = [[min(src // 8 * 8, D_PAD - W_WIN) for runs in plan for _, _, src in runs] for plan in plans]
    n_win = max(len(w) for w in windows)

    def body(ids_ref, src_ref, *refs):
        o_ref, buf, sems = refs[len(deps):]
        side = ids_ref[0] if kept else 1 - ids_ref[0]
        p = 2 * pl.program_id(0) + side
        row = lax.broadcasted_iota(jnp.int32, (PAD_BLK, D), 0)

        def fetch(q):
            slot = (q // 2) % 2
            return [pltpu.make_async_copy(src_ref.at[pl.ds(start, W_WIN)], buf.at[slot, n], sems.at[slot, n])
                    for n, start in enumerate(windows[q])]

        for q in range(N_DEV):
            @pl.when(p == q)
            def _(q=q):
                for nxt in ([q, q + 2] if q < 2 else [q + 2]):
                    if nxt < N_DEV:
                        for cp in fetch(nxt):
                            cp.start()
                copies, n = fetch(q), 0
                for j, runs in enumerate(plans[q]):
                    out = None
                    for o_lo, o_hi, src in runs:
                        copies[n].wait()
                        moved = pltpu.roll(buf[(q // 2) % 2, n], (o_lo - (src - windows[q][n])) % W_WIN, 0)[:PAD_BLK]
                        out = moved if out is None else jnp.where((row >= o_lo) & (row < o_hi), moved, out)
                        n += 1
                    size = min(PAD_BLK, rows - j * PAD_BLK)
                    o_ref[j * PAD_BLK:j * PAD_BLK + size, :] = out[:size]

    grid_spec = pltpu.PrefetchScalarGridSpec(
        num_scalar_prefetch=1, grid=(N_DEV // 2,),
        in_specs=[ANY] * (1 + len(deps)), out_specs=pl.BlockSpec((None, rows, D), lambda k, ids: (k, 0, 0)),
        scratch_shapes=[pltpu.VMEM((2, n_win, W_WIN, D), F32), pltpu.SemaphoreType.DMA((2, n_win))])
    return pl.pallas_call(
        body, name="unpad_dw_in_kept" if kept else "unpad_dw_in_sent", grid_spec=grid_spec,
        out_shape=jax.ShapeDtypeStruct((N_DEV // 2, rows, D), F32),
        compiler_params=_cparams(("arbitrary",)),
    )(ids, dwp, *deps)


LOSS_ROW = 25


def _pack_small(g_attn, b, g_mlp, g_final, loss_row=None):
    tail = jnp.pad(b, ((0, 7), (0, LANES - b.shape[1])))
    if loss_row is not None:
        tail = tail + jnp.pad(loss_row, ((LOSS_ROW - 24, 31 - LOSS_ROW), (0, 0)))
    return jnp.concatenate([g_attn.reshape(8, LANES), g_mlp.reshape(8, LANES), g_final.reshape(8, LANES), tail], axis=0)


def _unpack_small(p):
    return (p[0:8].reshape(1, D), p[24:25, :FOX_H], p[8:16].reshape(1, D), p[16:24].reshape(D))


class _StepComm:
    def __init__(self, ids, w_sh, m_sh, v_sh, lands, after_w_in):
        self.ids = ids
        self.w_sh, self.m_sh, self.v_sh = w_sh, m_sh, v_sh
        self.updates = None
        self.gather_mixer = _exchange_start("gather_mixer_start", _peer_copies, PEER_COPIES * 3, [], lands[:3],
                                            after=(after_w_in,))
        self.gather_mlp = _exchange_start("gather_mlp_start", _peer_copies, PEER_COPIES * 2, [], lands[3:],
                                          after=(self.gather_mixer[-1],))
        self.sibling = {}
        self.chips = {}
        self.own = {}
        self.names = {}
        self.copies_fn = {}
        self.kept = {}

    def _reduce_start(self, group, names, grads, after=(), copies_fn=_sibling_copies):
        lands = [lax.empty((SIBLING_COPIES,) + t.shape[1:], F32) for t in grads]
        self.sibling[group] = _exchange_start("grad_%s_sibling_start" % group, copies_fn,
                                              SIBLING_COPIES * len(grads), grads, lands, after=after)
        self.names[group] = names
        self.copies_fn[group] = copies_fn
        return self.sibling[group][-1]

    def _reduce_mid(self, group, after):
        grads, recv1 = _exchange_wait("grad_%s_sibling_wait" % group, self.copies_fn[group], self.sibling[group],
                                      after)
        grads = self.kept.get(group, grads)
        parts = [_chip_partial(self.ids, g, r, "grad_partial_" + n) for g, r, n in zip(grads, recv1, self.names[group])]
        self.own[group] = [p[1] for p in parts]
        srcs = [p[0] for p in parts]
        lands = [lax.empty((CHIP_COPIES,) + t.shape[1:], BF16) for t in srcs]
        self.chips[group] = _exchange_start("grad_%s_chips_start" % group, _chip_copies, CHIP_COPIES * len(srcs),
                                            srcs, lands)
        return self.chips[group][-1]

    def reduced(self, group, after):
        _, recv2 = _exchange_wait("grad_%s_chips_wait" % group, _chip_copies, self.chips[group], after)
        return list(zip(self.own[group], recv2))

    def first_deps(self):
        return [self.gather_mlp[-1]]

    def mixer_weights(self, after):
        _, (g_a, g_b, g_out) = _exchange_wait("gather_mixer_wait", _peer_copies, self.gather_mixer, after)
        return _cols_to_whole(g_a), _cols_to_whole(g_b), g_out.reshape(D, D)

    def mlp_weights(self, after):
        _, (g_up, g_down) = _exchange_wait("gather_mlp_wait", _peer_copies, self.gather_mlp, after)
        return g_up, g_down.reshape(DFF, D)

    def mlp_grads(self, dw_up_sh, dw_down):
        return [self._reduce_start("mlp", W_NAMES[4:], [dw_up_sh, dw_down.reshape((N_DEV,) + SHARD_SHAPES[5])])]

    def mixer_grads(self, dw_a, dw_b, dw_out):
        token = self._reduce_mid("mlp", dw_b)
        grads = [_whole_to_cols(dw_a), _whole_to_cols(dw_b), dw_out.reshape((N_DEV,) + SHARD_SHAPES[3])]
        return [self._reduce_start("mixer", W_NAMES[1:4], grads, after=(token,))]

    def mid_backward(self, after):
        return [self._reduce_mid("mixer", after)]

    def w_in_grad(self, dw_in_t):
        token = self._reduce_start("w_in", W_NAMES[:1], [_unpad_dw_in_t(dw_in_t, self.ids, kept=False)],
                                   copies_fn=_sibling_half_copies)
        self.kept["w_in"] = [_unpad_dw_in_t(dw_in_t, self.ids, kept=True, deps=[token])]
        reduced = self.reduced("mixer", token) + self.reduced("mlp", token)
        self.updates, last = [None] * len(reduced), self.kept["w_in"][0]
        for i in (3, 4, 0, 1, 2):
            if i == 4:
                last = self._reduce_mid("w_in", last)
            self.updates[i] = _reduce_adamw(*reduced[i], self.w_sh[1 + i], self.m_sh[1 + i], self.v_sh[1 + i],
                                            "adamw_" + W_NAMES[1 + i], deps=[last])
            last = self.updates[i][0]
        return [last]

    def w_in_update(self, after):
        (reduced,) = self.reduced("w_in", after)
        return _reduce_adamw(*reduced, self.w_sh[0], self.m_sh[0], self.v_sh[0], "adamw_" + W_NAMES[0])


def kernel(x, norm_attn_g, w_in, b_forget, w_branch_a, w_branch_b, w_out, norm_mlp_g, w_up, w_down, norm_final_g, loss_target, m_norm_attn_g, m_w_in, m_b_forget, m_w_branch_a, m_w_branch_b, m_w_out, m_norm_mlp_g, m_w_up, m_w_down, m_norm_final_g, v_norm_attn_g, v_w_in, v_b_forget, v_w_branch_a, v_w_branch_b, v_w_out, v_norm_mlp_g, v_w_up, v_w_down, v_norm_final_g):
    cx, cy, cc = _place()
    ids = jnp.stack([cc, 2 * cx + cy]).astype(jnp.int32)

    w_sh = [w_in[0].T] + [t[0] for t in (w_branch_a, w_branch_b, w_out, w_up, w_down)]
    m_sh = [m_w_in[0].T] + [t[0] for t in (m_w_branch_a, m_w_branch_b, m_w_out, m_w_up, m_w_down)]
    v_sh = [v_w_in[0].T] + [t[0] for t in (v_w_branch_a, v_w_branch_b, v_w_out, v_w_up, v_w_down)]

    w_in_t, h1, *lands = _gather_w_in(jnp.pad(w_sh[0], ((0, W_SLAB - w_sh[0].shape[0]), (0, 0))), x[0], norm_attn_g,
                                      w_sh[1:])
    comm = _StepComm(ids, w_sh, m_sh, v_sh, lands, w_in_t)
    b_pad = jnp.pad(b_forget, ((0, 0), (0, LANES - FOX_H)))

    loss_row, dx, dsmall = _local_step(
        x[0], loss_target[0], h1, norm_attn_g, norm_mlp_g, norm_final_g.reshape(1, D), b_pad, w_in_t, comm)

    dg_attn, db, dg_mlp, dg_final = dsmall
    stack = lax.dynamic_update_slice(jnp.zeros((N_DEV, SMALL_R, LANES), F32),
                                     _pack_small(dg_attn, db, dg_mlp, dg_final, loss_row)[None],
                                     (4 * cx + 2 * cy + cc, 0, 0))
    small_gather = _exchange_start("small_gather_start", _peer_copies, PEER_COPIES, [], [stack])
    w_in_update = comm.w_in_update(small_gather[-1])
    _, (stack,) = _exchange_wait("small_gather_wait", _peer_copies, small_gather, w_in_update[0])
    small = _small_adamw(
        stack,
        _pack_small(norm_attn_g, b_forget, norm_mlp_g, norm_final_g.reshape(1, D)),
        _pack_small(m_norm_attn_g, m_b_forget, m_norm_mlp_g, m_norm_final_g.reshape(1, D)),
        _pack_small(v_norm_attn_g, v_b_forget, v_norm_mlp_g, v_norm_final_g.reshape(1, D)))
    big = [w_in_update] + comm.updates

    outs = [small[0][LOSS_ROW, 0], dx[None]]
    for q in range(4):
        s_attn, s_b, s_mlp, s_final = _unpack_small(small[q])
        b_in, b_a, b_b, b_out, b_up, b_down = [t[q][None] for t in big]
        b_in = jnp.swapaxes(b_in, 1, 2)
        outs += [s_attn, b_in, s_b, b_a, b_b, b_out, s_mlp, b_up, b_down, s_final]
    return tuple(outs)
```

```python
import functools

import jax
import jax.numpy as jnp
from jax import lax
from jax.experimental import pallas as pl
from jax.experimental.pallas import tpu as pltpu

F32 = jnp.float32
BF16 = jnp.bfloat16
MESH = pl.DeviceIdType.MESH

S = 2048
D = 1024
HD = 64
FOX_H = 8
FOX_W = FOX_H * HD
DIL_HG = 4
DIL_G = 3
DIL_W = DIL_G * DIL_HG * HD
DIL_OUT = DIL_HG * HD
DIL_BLK = 128
DIL_R = (1, 4, 16)
DFF = 4 * D
D_IN = 3 * FOX_W + FOX_H + 3 * DIL_W + 2 * D
EPS = 1e-6
NEG_INF = -1e30
SCALE = HD ** -0.5
ROPE_THETA = 500000.0
ROPE_DIM = HD // 4
N_DEV = 8

ADAM_LR = 0.001
ADAM_B1 = 0.9
ADAM_B2 = 0.999
ADAM_EPS = 1e-08
ADAM_WD = 0.01
ADAM_STEP = 10

LANES = 128
CB = 256
SEG_A = 0
SEG_B = 3 * FOX_W
SEG_Z = SEG_B + 3 * DIL_W
SEG_G = SEG_Z + CB
SEG_F = SEG_G + 2 * D
D_PAD = SEG_F + CB
DW_IN_TM = D_PAD // 10
SMALL_R = 32

VMEM_MB = 56


def _cparams(dims=None, vmem_mb=VMEM_MB, **kw):
    return pltpu.CompilerParams(dimension_semantics=dims, vmem_limit_bytes=vmem_mb << 20, **kw)


ANY = pl.BlockSpec(memory_space=pl.ANY)

_NN = (((1,), (0,)), ((), ()))
_NT = (((1,), (1,)), ((), ()))
_TN = (((0,), (0,)), ((), ()))


def _dot(a, b, dims):
    return lax.dot_general(a.astype(BF16), b.astype(BF16), dims, preferred_element_type=F32)


def _mm(a, b, *, mode, tm, tn, out_dtypes, name, n=None, k=None, a_off=0, b_off=0,
        b_sharded=False, out_sharded=False, epilogue=None, extras=(), deps=()):
    n_sh = b.shape[-1] if b_sharded else None
    if mode == "nn":
        m = a.shape[0]
        k = k or a.shape[1]
        a_spec = pl.BlockSpec((tm, k), lambda i, j: (i, a_off))
        if b_sharded:
            assert tn == n_sh
            n = N_DEV * n_sh
            b_spec = pl.BlockSpec((None, k, tn), lambda i, j: (j, 0, 0))
        else:
            n = n or b.shape[1]
            b_spec = pl.BlockSpec((k, tn), lambda i, j: (0, j + b_off))
        dims = _NN
    elif mode == "nt":
        m = a.shape[0]
        k = k or a.shape[1]
        a_spec = pl.BlockSpec((tm, k), lambda i, j: (i, a_off))
        if b_sharded:
            n = b.shape[1]
            b_spec = pl.BlockSpec((N_DEV, tn, n_sh), lambda i, j: (0, j, 0))
        else:
            n = n or b.shape[0]
            b_spec = pl.BlockSpec((tn, k), lambda i, j: (j + b_off, 0))
        dims = _NT
    else:
        k, m = a.shape
        n = n or b.shape[1]
        a_spec = pl.BlockSpec((k, tm), lambda i, j: (0, i))
        b_spec = pl.BlockSpec((k, tn), lambda i, j: (0, j + b_off))
        dims = _TN
    assert m % tm == 0 and n % tn == 0, (name, m, n, tm, tn)
    n_extra = len(extras)
    tile = pl.BlockSpec((tm, tn), lambda i, j: (i, j))
    if out_sharded:
        assert mode == "tn" and n // tn == N_DEV
        out_spec = pl.BlockSpec((None, tm, tn), lambda i, j: (j, i, 0))
        out_shape = (N_DEV, m, tn)
    else:
        out_spec, out_shape = tile, (m, n)

    def body(a_ref, b_ref, *refs):
        if mode == "nt" and b_sharded:
            acc = _dot(a_ref[:, 0:n_sh], b_ref[0], dims)
            for p in range(1, N_DEV):
                acc = acc + _dot(a_ref[:, p * n_sh:(p + 1) * n_sh], b_ref[p], dims)
        else:
            acc = _dot(a_ref[...], b_ref[...], dims)
        ex = [r[...] for r in refs[:n_extra]]
        outs = epilogue(acc, *ex) if epilogue is not None else (acc,)
        for o_ref, o in zip(refs[n_extra + len(deps):], outs):
            o_ref[...] = o.astype(o_ref.dtype)

    res = pl.pallas_call(
        body, name=name, grid=(m // tm, n // tn),
        in_specs=[a_spec, b_spec] + [tile] * n_extra + [ANY] * len(deps),
        out_specs=[out_spec] * len(out_dtypes),
        out_shape=[jax.ShapeDtypeStruct(out_shape, dt) for dt in out_dtypes],
        compiler_params=_cparams(("parallel", "parallel")),
    )(a, b, *extras, *deps)
    return res if len(out_dtypes) > 1 else res[0]


def _mm_rows(a, b, *, tm, name, epilogue, rows=(), vecs=(), row_out=(), vec_out=(), b_sharded=False, deps=()):
    m, k = a.shape
    n_rows, n_vecs, n_deps = len(rows), len(vecs), len(deps)
    n_sh = b.shape[-1] if b_sharded else None

    def body(a_ref, b_ref, *refs):
        row_refs, vec_refs = refs[:n_rows], refs[n_rows:n_rows + n_vecs]
        outs = refs[n_rows + n_vecs + n_deps:]
        if b_sharded:
            acc = _dot(a_ref[:, 0:n_sh], b_ref[0], _NT)
            for p in range(1, N_DEV):
                acc = acc + _dot(a_ref[:, p * n_sh:(p + 1) * n_sh], b_ref[p], _NT)
        else:
            acc = _dot(a_ref[...], b_ref[...], _NN)
        row_vals, vec_incs = epilogue(acc, [r[...] for r in row_refs], [v[...] for v in vec_refs])
        for o_ref, val in zip(outs[:len(row_out)], row_vals):
            o_ref[...] = val.astype(o_ref.dtype)

        @pl.when(pl.program_id(0) == 0)
        def _():
            for o_ref in outs[len(row_out):]:
                o_ref[...] = jnp.zeros_like(o_ref)

        for o_ref, inc in zip(outs[len(row_out):], vec_incs):
            o_ref[...] += inc

    tile = pl.BlockSpec((tm, D), lambda i: (i, 0))
    b_spec = pl.BlockSpec(b.shape, lambda i: (0,) * b.ndim)
    return pl.pallas_call(
        body, name=name, grid=(m // tm,),
        in_specs=[pl.BlockSpec((tm, k), lambda i: (i, 0)), b_spec] + [tile] * n_rows
                 + [pl.BlockSpec(v.shape, lambda i: (0, 0)) for v in vecs] + [ANY] * n_deps,
        out_specs=[tile] * len(row_out) + [pl.BlockSpec((1, w), lambda i: (0, 0)) for w in vec_out],
        out_shape=[jax.ShapeDtypeStruct((m, D), dt) for dt in row_out]
                  + [jax.ShapeDtypeStruct((1, w), F32) for w in vec_out],
        compiler_params=_cparams(("arbitrary",)),
    )(a, b, *rows, *vecs, *deps)


ROW_T = 256
ROWS_TM = 512


def _rms_rows(x, g):
    r = lax.rsqrt(jnp.mean(x * x, axis=-1, keepdims=True) + EPS)
    return (x * r) * g


def _rms_bwd_rows(dh, x, dres, g):
    r = lax.rsqrt(jnp.mean(x * x, axis=-1, keepdims=True) + EPS)
    xn = x * r
    dhn = dh * g
    dx = dres + r * (dhn - xn * jnp.mean(dhn * xn, axis=-1, keepdims=True))
    return dx, jnp.sum(dh * xn, axis=0, keepdims=True)


def _final_loss_rows(x, g, tgt):
    r = lax.rsqrt(jnp.mean(x * x, axis=-1, keepdims=True) + EPS)
    xn = x * r
    err = xn * g - tgt
    row_loss = jnp.mean(err * err, axis=-1, keepdims=True)
    loss = 0.5 * jnp.sum(row_loss, axis=0, keepdims=True) * jnp.ones((1, LANES), F32)
    dy = err * (1.0 / D)
    dyn = dy * g
    dx = r * (dyn - xn * jnp.mean(dyn * xn, axis=-1, keepdims=True))
    return dx, jnp.sum(dy * xn, axis=0, keepdims=True), loss


def _sigmoid(z):
    return 1.0 / (1.0 + jnp.exp(-z))


GATE_TR = 512


def _mixer_tail(oa, ob, w_a, w_b, w_out, proj_g, x, g_mlp):
    def body(oa_ref, ob_ref, wa_ref, wb_ref, wo_ref, g_ref, x_ref, gm_ref, ya_ref, yb_ref, mixed_ref, x2_ref, h2_ref):
        ya = _dot(oa_ref[...], wa_ref[...], _NN)
        yb = _dot(ob_ref[...], wb_ref[...], _NN)
        ya_ref[...] = ya
        yb_ref[...] = yb
        mixed = (_sigmoid(g_ref[:, :D]) * ya + _sigmoid(g_ref[:, D:]) * yb).astype(BF16)
        mixed_ref[...] = mixed
        x2 = x_ref[...] + _dot(mixed, wo_ref[...], _NN)
        x2_ref[...] = x2
        h2_ref[...] = _rms_rows(x2, gm_ref[...]).astype(BF16)

    def rows(width):
        return pl.BlockSpec((GATE_TR, width), lambda i: (i, 0))

    def whole(t):
        return pl.BlockSpec(t.shape, lambda i: (0, 0))

    return pl.pallas_call(
        body, name="mixer_tail", grid=(S // GATE_TR,),
        in_specs=[rows(FOX_W), rows(DIL_OUT), whole(w_a), whole(w_b), whole(w_out), rows(2 * D), rows(D), whole(g_mlp)],
        out_specs=[rows(D)] * 5,
        out_shape=[jax.ShapeDtypeStruct((S, D), dt) for dt in (F32, F32, BF16, F32, BF16)],
        compiler_params=_cparams(("parallel",)),
    )(oa, ob, w_a, w_b, w_out, proj_g, x, g_mlp)


def _out_proj_gate_bwd(dx2, w_out, proj_g, ya, yb):
    def body(dx_ref, w_ref, g_ref, ya_ref, yb_ref, dy_ref, dg_ref):
        dm = _dot(dx_ref[...], w_ref[...], _NT)
        for half, y_ref in enumerate((ya_ref, yb_ref)):
            cols = slice(half * D, (half + 1) * D)
            s = _sigmoid(g_ref[:, cols])
            dy_ref[:, cols] = (dm * s).astype(BF16)
            dg_ref[:, cols] = (dm * y_ref[...] * (s * (1.0 - s))).astype(BF16)

    row = pl.BlockSpec((GATE_TR, D), lambda i: (i, 0))
    wide = pl.BlockSpec((GATE_TR, 2 * D), lambda i: (i, 0))
    return pl.pallas_call(
        body, name="out_proj_gate_bwd", grid=(S // GATE_TR,),
        in_specs=[row, pl.BlockSpec((D, D), lambda i: (0, 0)), wide, row, row],
        out_specs=[wide, pl.BlockSpec((GATE_TR, 2 * D), lambda i: (i, SEG_G // (2 * D)))],
        out_shape=[jax.ShapeDtypeStruct((S, 2 * D), BF16), jax.ShapeDtypeStruct((S, D_PAD), BF16)],
        compiler_params=_cparams(("parallel",)),
    )(dx2, w_out, proj_g, ya, yb)


FOX_TQ = 512
FOX_TQ_FWD = 1024


def _scan_rows(x, reverse):
    n = x.shape[0]
    row = lax.broadcasted_iota(jnp.int32, x.shape, 0)
    k = 1
    while k < n:
        if reverse:
            x = x + jnp.where(row < n - k, pltpu.roll(x, n - k, 0), 0.0)
        else:
            x = x + jnp.where(row >= k, pltpu.roll(x, k, 0), 0.0)
        k *= 2
    return x


def _fox_dscan(dft, dfs, proj_f, b_pad, dproj):
    def body(dft_ref, dfs_ref, f_ref, b_ref, _, dfa_ref, db_ref):
        dfs_pad = jnp.concatenate([dfs_ref[...], jnp.zeros((LANES - FOX_H, S), F32)], axis=0)
        df = dfs_pad.T + dft_ref[0]
        for hp in range(1, dft_ref.shape[0]):
            df = df + pltpu.roll(dft_ref[hp], 2 * hp, 1)
        dlf = _scan_rows(df, reverse=True)
        z = f_ref[...] + b_ref[...]
        lane = lax.broadcasted_iota(jnp.int32, (S, LANES), 1)
        dfa = jnp.where(lane < FOX_H, dlf / (1.0 + jnp.exp(z)), 0.0)
        dfa_ref[:, :LANES] = dfa.astype(BF16)
        dfa_ref[:, LANES:] = jnp.zeros((S, CB - LANES), BF16)
        db_ref[...] = jnp.sum(dfa, axis=0, keepdims=True)

    return pl.pallas_call(
        body, name="fox_dscan", grid=(1,),
        in_specs=[pl.BlockSpec(dft.shape, lambda i: (0, 0, 0)), pl.BlockSpec((FOX_H, S), lambda i: (0, 0)),
                  pl.BlockSpec((S, LANES), lambda i: (0, 0)), pl.BlockSpec((1, LANES), lambda i: (0, 0)), ANY],
        out_specs=[pl.BlockSpec((S, CB), lambda i: (0, SEG_F // CB)), pl.BlockSpec((1, LANES), lambda i: (0, 0))],
        out_shape=[jax.ShapeDtypeStruct((S, D_PAD), BF16), jax.ShapeDtypeStruct((1, LANES), F32)],
        input_output_aliases={4: 0},
        compiler_params=_cparams(("arbitrary",)),
    )(dft, dfs, proj_f, b_pad, dproj)


FOX_NQ = S // FOX_TQ
FOX_HP = FOX_W // LANES
HEADS_PER_LB = LANES // HD
FOX_AUG = 2 * LANES


def _fox_prepare(proj_a, proj_f, b_pad):
    tr = 512

    def body(q_ref, k_ref, f_ref, b_ref, qa_ref, ka_ref, hi_s, mid_s, lo_s):
        i = pl.program_id(0)

        @pl.when(i == 0)
        def _():
            z = f_ref[...] + b_ref[...]
            lf = jnp.minimum(z, 0.0) - jnp.log1p(jnp.exp(-jnp.abs(z)))
            f = _scan_rows(lf, reverse=False)
            hi = f.astype(BF16).astype(F32)
            r1 = f - hi
            mid = r1.astype(BF16).astype(F32)
            hi_s[...] = hi
            mid_s[...] = mid
            lo_s[...] = (r1 - mid).astype(BF16).astype(F32)

        rows = pl.ds(pl.multiple_of(i * tr, tr), tr)
        hi, mid, lo = hi_s[rows, :], mid_s[rows, :], lo_s[rows, :]
        lane = lax.broadcasted_iota(jnp.int32, (tr, HD), 1)
        q_ones = jnp.where((lane >= 3) & (lane < 6), 1.0, 0.0)
        k_ones = jnp.where(lane < 3, 1.0, 0.0)
        for h in range(FOX_H):
            a1, a2, a3 = hi[:, h:h + 1], mid[:, h:h + 1], lo[:, h:h + 1]
            q_extra = jnp.where(lane == 0, a1, jnp.where(lane == 1, a2, jnp.where(lane == 2, a3, q_ones)))
            k_extra = jnp.where(lane == 3, -a1, jnp.where(lane == 4, -a2, jnp.where(lane == 5, -a3, k_ones)))
            base = h * LANES
            qa_ref[:, base:base + HD] = q_ref[:, h * HD:(h + 1) * HD] * jnp.asarray(SCALE, BF16)
            qa_ref[:, base + HD:base + LANES] = q_extra.astype(BF16)
            ka_ref[:, base:base + HD] = k_ref[:, h * HD:(h + 1) * HD]
            ka_ref[:, base + HD:base + LANES] = k_extra.astype(BF16)

    out_blk = pl.BlockSpec((tr, FOX_H * LANES), lambda i: (i, 0))
    return pl.pallas_call(
        body, name="fox_prepare", grid=(S // tr,),
        in_specs=[pl.BlockSpec((tr, FOX_W), lambda i: (i, 0)), pl.BlockSpec((tr, FOX_W), lambda i: (i, 1)),
                  pl.BlockSpec((S, LANES), lambda i: (0, 0)), pl.BlockSpec((1, LANES), lambda i: (0, 0))],
        out_specs=[out_blk, out_blk],
        out_shape=[jax.ShapeDtypeStruct((S, FOX_H * LANES), BF16)] * 2,
        scratch_shapes=[pltpu.VMEM((S, LANES), F32)] * 3,
        compiler_params=_cparams(("arbitrary",)),
    )(proj_a, proj_a, proj_f, b_pad)


def _fox_scores(qa, ka, w, tq):
    s = _dot(qa, ka, _NT)
    row = lax.broadcasted_iota(jnp.int32, (tq, tq), 0)
    col = lax.broadcasted_iota(jnp.int32, (tq, tq), 1)
    diag = jnp.where(col <= row, s[:, w * tq:], NEG_INF)
    return diag if w == 0 else jnp.concatenate([s[:, :w * tq], diag], axis=1)


def _fox_fwd(q_aug, k_aug, proj_a):
    tq = FOX_TQ_FWD

    def body(qa_ref, ka_ref, v_ref, o_ref, lse_ref):
        qi = pl.program_id(1)
        lane = lax.broadcasted_iota(jnp.int32, (tq, LANES), 1)
        for w in range(S // tq):
            @pl.when(qi == w)
            def _(w=w):
                width = (w + 1) * tq
                lse = jnp.zeros((tq, LANES), F32)
                for hh in range(HEADS_PER_LB):
                    aug = slice(hh * LANES, (hh + 1) * LANES)
                    sl = slice(hh * HD, (hh + 1) * HD)
                    s = _fox_scores(qa_ref[:, aug], ka_ref[:width, aug], w, tq)
                    m = jnp.max(s, axis=-1, keepdims=True)
                    p = jnp.exp(s - m)
                    l = jnp.sum(p, axis=-1, keepdims=True)
                    o_ref[:, sl] = _dot(p / l, v_ref[:width, sl], _NN)
                    lse = jnp.where(lane == hh, m + jnp.log(l), lse)
                lse_ref[...] = lse

    return pl.pallas_call(
        body, name="fox_fwd", grid=(FOX_HP, S // tq),
        in_specs=[pl.BlockSpec((tq, FOX_AUG), lambda h, i: (i, h)),
                  pl.BlockSpec((S, FOX_AUG), lambda h, i: (0, h)),
                  pl.BlockSpec((S, LANES), lambda h, i: (0, 2 * FOX_HP + h))],
        out_specs=[pl.BlockSpec((tq, LANES), lambda h, i: (i, h)),
                   pl.BlockSpec((None, tq, LANES), lambda h, i: (h, i, 0))],
        out_shape=[jax.ShapeDtypeStruct((S, FOX_W), F32), jax.ShapeDtypeStruct((FOX_HP, S, LANES), F32)],
        compiler_params=_cparams(("parallel", "parallel")),
    )(q_aug, k_aug, proj_a)


def _fox_bwd(q_aug, k_aug, proj_a, lse, do, dproj, deps=()):
    n_q = FOX_NQ

    def body(qa_ref, ka_ref, v_ref, lse_ref, do_ref, *refs):
        dproj_ref, dft_ref, dfs_ref, dk_acc, dv_acc, dq_buf, kv_buf, dq_sems, kv_sems = refs[1 + len(deps):]
        hp = pl.program_id(0)
        t = pl.program_id(1)
        slot = t % 2
        col = pl.multiple_of(hp * LANES, LANES)

        def dq_copy(step, buf_slot):
            rows = pl.ds(pl.multiple_of(step * FOX_TQ, FOX_TQ), FOX_TQ)
            return pltpu.make_async_copy(dq_buf.at[buf_slot], dproj_ref.at[rows, pl.ds(col, LANES)], dq_sems.at[buf_slot])

        @pl.when(t == 0)
        def _():
            dk_acc[...] = jnp.zeros_like(dk_acc)
            dv_acc[...] = jnp.zeros_like(dv_acc)

        @pl.when((t == 0) & (hp == 0))
        def _():
            dfs_ref[...] = jnp.zeros_like(dfs_ref)

        @pl.when(t >= 2)
        def _():
            dq_copy(t - 2, slot).wait()

        lane = lax.broadcasted_iota(jnp.int32, (FOX_TQ, LANES), 1)
        for w in range(n_q):
            @pl.when(t == w)
            def _(w=w):
                width = (w + 1) * FOX_TQ
                lse_all = lse_ref[...]
                sub = lax.broadcasted_iota(jnp.int32, (FOX_H, width), 0)
                dft = jnp.zeros((FOX_TQ, LANES), F32)
                for hh in range(HEADS_PER_LB):
                    aug = slice(hh * LANES, (hh + 1) * LANES)
                    sl = slice(hh * HD, (hh + 1) * HD)
                    head = hp * HEADS_PER_LB + hh
                    qa = qa_ref[:, aug]
                    ka = ka_ref[:width, aug]
                    do_h = do_ref[:, sl]
                    s = _fox_scores(qa, ka, w, FOX_TQ)
                    p = jnp.exp(s - lse_all[:, hh:hh + 1])
                    dp = _dot(do_h, v_ref[:width, sl], _NT)
                    ds = p * (dp - jnp.sum(dp * p, axis=-1, keepdims=True))
                    dft = jnp.where(lane == hh, jnp.sum(ds, axis=-1, keepdims=True), dft)
                    dfs_ref[:, :width] -= jnp.where(sub == head, jnp.sum(ds, axis=0, keepdims=True), 0.0)
                    dq_buf[slot, :, sl] = (_dot(ds, ka[:, :HD], _NN) * SCALE).astype(BF16)
                    dk_acc[:width, sl] += _dot(ds, qa[:, :HD], _TN)
                    dv_acc[:width, sl] += _dot(p, do_h, _TN)
                dft_ref[...] = dft

        dq_copy(t, slot).start()

        @pl.when(t == n_q - 1)
        def _():
            dq_copy(t - 1, 1 - slot).wait()
            dq_copy(t, slot).wait()
            kv_buf[0] = dk_acc[...].astype(BF16)
            kv_buf[1] = dv_acc[...].astype(BF16)
            copies = [pltpu.make_async_copy(
                kv_buf.at[j], dproj_ref.at[:, pl.ds(pl.multiple_of((j + 1) * FOX_W + hp * LANES, LANES), LANES)],
                kv_sems.at[j]) for j in range(2)]
            for cp in copies:
                cp.start()
            for cp in copies:
                cp.wait()

    qblk = pl.BlockSpec((FOX_TQ, FOX_AUG), lambda h, t: (t, h))
    lane_blk = pl.BlockSpec((None, FOX_TQ, LANES), lambda h, t: (h, t, 0))
    return pl.pallas_call(
        body, name="fox_bwd", grid=(FOX_HP, n_q),
        in_specs=[qblk, pl.BlockSpec((S, FOX_AUG), lambda h, t: (0, h)),
                  pl.BlockSpec((S, LANES), lambda h, t: (0, 2 * FOX_HP + h)),
                  lane_blk, pl.BlockSpec((FOX_TQ, LANES), lambda h, t: (t, h)), ANY] + [ANY] * len(deps),
        out_specs=[ANY, lane_blk, pl.BlockSpec((FOX_H, S), lambda h, t: (0, 0))],
        out_shape=[jax.ShapeDtypeStruct((S, D_PAD), BF16),
                   jax.ShapeDtypeStruct((FOX_HP, S, LANES), F32), jax.ShapeDtypeStruct((FOX_H, S), F32)],
        scratch_shapes=[pltpu.VMEM((S, LANES), F32), pltpu.VMEM((S, LANES), F32),
                        pltpu.VMEM((2, FOX_TQ, LANES), BF16), pltpu.VMEM((2, S, LANES), BF16),
                        pltpu.SemaphoreType.DMA((2,)), pltpu.SemaphoreType.DMA((2,))],
        input_output_aliases={5: 0},
        compiler_params=_cparams(("arbitrary", "arbitrary")),
    )(q_aug, k_aug, proj_a, lse, do, dproj, *deps)


def _rope_tables():
    half = ROPE_DIM // 2
    shape = (DIL_G, S, half)
    g = lax.broadcasted_iota(jnp.int32, shape, 0)
    row = lax.broadcasted_iota(jnp.int32, shape, 1)
    r = jnp.left_shift(1, 2 * g)
    per_class = S // r
    pos = (row % per_class) * r + row // per_class
    inv_freq = jnp.power(jnp.float32(ROPE_THETA), -lax.broadcasted_iota(F32, shape, 2) * 2.0 / ROPE_DIM)
    ang = pos.astype(F32) * inv_freq
    return jnp.concatenate([jnp.cos(ang), jnp.sin(ang)], axis=-1)


def _rope_factors(cs):
    half = ROPE_DIM // 2
    i = lax.broadcasted_iota(jnp.int32, (2 * half, 2 * LANES), 0)
    l = lax.broadcasted_iota(jnp.int32, (2 * half, 2 * LANES), 1)
    hit = ((l < LANES) & (i == l % half)) | ((l >= LANES) & (i - half == l % half))
    spread = jnp.where(hit, 1.0, 0.0).astype(BF16)
    hi = cs.astype(BF16)
    r1 = cs - hi.astype(F32)
    mid = r1.astype(BF16)
    lo = (r1 - mid.astype(F32)).astype(BF16)
    full = (_dot(hi, spread, _NN) + _dot(mid, spread, _NN)) + _dot(lo, spread, _NN)
    cos, sin = full[:, :LANES], full[:, LANES:]
    lane = lax.broadcasted_iota(jnp.int32, cos.shape, 1) % HD
    t_self = jnp.where(lane < ROPE_DIM, cos, 1.0)
    t_up = jnp.where(lane < half, -sin, 0.0)
    t_dn = jnp.where((lane >= half) & (lane < ROPE_DIM), sin, 0.0)
    return t_self, t_up, t_dn


def _residue_pieces(r):
    if r == 1:
        return [(slice(i, i + 512), slice(i, i + 512)) for i in range(0, S, 512)]
    n = S // r
    return [(pl.ds(j, n, stride=r), slice(j * n, (j + 1) * n)) for j in range(r)]


def _rope_apply(x, a, u, d, backward):
    half = ROPE_DIM // 2
    if backward:
        return x * a + pltpu.roll(x * u, half, 1) + pltpu.roll(x * d, LANES - half, 1)
    return x * a + pltpu.roll(x, LANES - half, 1) * u + pltpu.roll(x, half, 1) * d


def _rope_cache_factors(cs_ref, fac, first_step_of_group):
    @pl.when(first_step_of_group)
    def _():
        for lo in range(0, S, 512):
            for j, f in enumerate(_rope_factors(cs_ref[lo:lo + 512, :])):
                fac[j, lo:lo + 512, :] = f


LB_PER_CB = CB // LANES
ROPE_NB = 3 * DIL_G * LB_PER_CB


def _rope_step(s):
    per_group = 3 * LB_PER_CB
    return s // per_group, (s % per_group) // LB_PER_CB, s % LB_PER_CB


def _rope_split(proj_b, tables):
    def body(x_ref, cs_ref, o_ref, fac):
        g, t, hf = _rope_step(pl.program_id(0))
        _rope_cache_factors(cs_ref, fac, (t == 0) & (hf == 0))
        for gs in range(DIL_G):
            @pl.when(g == gs)
            def _(gs=gs):
                for tok, sub in _residue_pieces(DIL_R[gs]):
                    x = x_ref[tok, :]
                    y = _rope_apply(x, fac[0, sub, :], fac[1, sub, :], fac[2, sub, :], False)
                    o_ref[sub, :] = jnp.where(t < 2, y, x).astype(BF16)

    def in_index(s):
        g, t, hf = _rope_step(s)
        return 0, (t * DIL_G + g) * LB_PER_CB + hf

    def out_index(s):
        g, t, hf = _rope_step(s)
        return t * DIL_G + g, 0, hf

    tab = pl.BlockSpec((None, S, 2 * (ROPE_DIM // 2)), lambda s: (_rope_step(s)[0], 0, 0))
    return pl.pallas_call(
        body, name="rope_split", grid=(ROPE_NB,),
        in_specs=[pl.BlockSpec((S, LANES), in_index), tab],
        out_specs=pl.BlockSpec((None, S, LANES), out_index),
        out_shape=jax.ShapeDtypeStruct((3 * DIL_G, S, CB), BF16),
        scratch_shapes=[pltpu.VMEM((3, S, LANES), F32)],
        compiler_params=_cparams(("arbitrary",)),
    )(proj_b, tables)


def _rope_merge_bwd(dq3, dk3, dv3, tables, dproj):
    def body(dq_ref, dk_ref, dv_ref, cs_ref, _, o_ref, tmp, fac):
        s = pl.program_id(0)
        g, t, hf = _rope_step(s)
        _rope_cache_factors(cs_ref, fac, (t == 0) & (hf == 0) & (s < ROPE_NB))
        for gs in range(DIL_G):
            @pl.when((g == gs) & (s < ROPE_NB))
            def _(gs=gs):
                for tok, sub in _residue_pieces(DIL_R[gs]):
                    x = jnp.where(t == 0, dq_ref[sub, :], jnp.where(t == 1, dk_ref[sub, :], dv_ref[sub, :]))
                    y = _rope_apply(x, fac[0, sub, :], fac[1, sub, :], fac[2, sub, :], True)
                    tmp[tok, :] = jnp.where(t < 2, y, x)
                o_ref[...] = tmp[...].astype(BF16)

        @pl.when(s >= ROPE_NB)
        def _():
            o_ref[...] = jnp.zeros_like(o_ref)

    def src_spec(own):
        def index(s):
            g, t, hf = _rope_step(jnp.minimum(s, ROPE_NB - 1))
            return g, 0, jnp.where(t == own, hf, jnp.where(t > own, LB_PER_CB - 1, 0))
        return pl.BlockSpec((None, S, LANES), index)

    def out_index(s):
        g, t, hf = _rope_step(s)
        col = SEG_B // LANES + (t * DIL_G + g) * LB_PER_CB + hf
        return 0, jnp.where(s < ROPE_NB, col, SEG_Z // LANES + s - ROPE_NB)

    tab = pl.BlockSpec((None, S, 2 * (ROPE_DIM // 2)), lambda s: (_rope_step(jnp.minimum(s, ROPE_NB - 1))[0], 0, 0))
    return pl.pallas_call(
        body, name="rope_merge_bwd", grid=(ROPE_NB + LB_PER_CB,),
        in_specs=[src_spec(0), src_spec(1), src_spec(2), tab, ANY],
        out_specs=pl.BlockSpec((S, LANES), out_index),
        out_shape=jax.ShapeDtypeStruct((S, D_PAD), BF16),
        scratch_shapes=[pltpu.VMEM((S, LANES), F32), pltpu.VMEM((3, S, LANES), F32)],
        input_output_aliases={4: 0},
        compiler_params=_cparams(("arbitrary",)),
    )(dq3, dk3, dv3, tables, dproj)


DIL_NB = S // DIL_BLK


_BQK = (((2,), (2,)), ((0,), (0,)))
_BQD = (((2,), (1,)), ((0,), (0,)))
_BKD = (((1,), (1,)), ((0,), (0,)))


def _dil_mask(g):
    shape = (DIL_NB, DIL_BLK, 2 * DIL_BLK)
    blocks_per_seq = jnp.right_shift(DIL_NB, 2 * g)
    has_prev = jnp.bitwise_and(lax.broadcasted_iota(jnp.int32, shape, 0), blocks_per_seq - 1) != 0
    a = lax.broadcasted_iota(jnp.int32, shape, 1)
    kk = lax.broadcasted_iota(jnp.int32, shape, 2)
    diff = DIL_BLK + a - kk
    return (diff >= 0) & (diff <= DIL_BLK) & ((kk >= DIL_BLK) | has_prev)


def _blocks(t):
    return t.reshape(DIL_NB, DIL_BLK, t.shape[-1])


def _with_prev(t):
    prev = jnp.concatenate([jnp.zeros((DIL_BLK, t.shape[-1]), t.dtype), t[:-DIL_BLK]], axis=0)
    return jnp.concatenate([_blocks(prev), _blocks(t)], axis=1)


def _fold_prev(t2):
    n = t2.shape[-1]
    to_prev = t2[:, :DIL_BLK].reshape(S, n)
    own = t2[:, DIL_BLK:].reshape(S, n)
    return own + jnp.concatenate([to_prev[DIL_BLK:], jnp.zeros((DIL_BLK, n), t2.dtype)], axis=0)


def _dil_group_spec(t, width=DIL_OUT):
    return pl.BlockSpec((None, S, width), lambda g: (t * DIL_G + g, 0, 0))


def _dil_fwd(qkv3):
    def body(q_ref, k_ref, v_ref, o_ref, lse_ref):
        ok = _dil_mask(pl.program_id(0))
        lane = lax.broadcasted_iota(jnp.int32, (S, LANES), 1)
        lse = jnp.zeros((S, LANES), F32)
        for h in range(DIL_HG):
            sl = slice(h * HD, (h + 1) * HD)
            s = jnp.where(ok, _dot(_blocks(q_ref[:, sl]), _with_prev(k_ref[:, sl]), _BQK) * SCALE, NEG_INF)
            m = jnp.max(s, axis=-1, keepdims=True)
            p = jnp.exp(s - m)
            l = jnp.sum(p, axis=-1, keepdims=True)
            o_ref[:, sl] = _dot(p / l, _with_prev(v_ref[:, sl]), _BQD).reshape(S, HD)
            lse = jnp.where(lane == h, (m + jnp.log(l)).reshape(S, 1), lse)
        lse_ref[...] = lse

    return pl.pallas_call(
        body, name="dil_fwd", grid=(DIL_G,),
        in_specs=[_dil_group_spec(0), _dil_group_spec(1), _dil_group_spec(2)],
        out_specs=[_dil_group_spec(0), _dil_group_spec(0, LANES)],
        out_shape=[jax.ShapeDtypeStruct((DIL_G, S, DIL_OUT), F32), jax.ShapeDtypeStruct((DIL_G, S, LANES), F32)],
        compiler_params=_cparams(("parallel",)),
    )(qkv3, qkv3, qkv3)


def _dil_bwd(qkv3, lse3, do3, c3):
    def body(q_ref, k_ref, v_ref, lse_ref, do_ref, c_ref, dq_ref, dk_ref, dv_ref):
        ok = _dil_mask(pl.program_id(0))
        lse = lse_ref[...]
        c = c_ref[...]
        for h in range(DIL_HG):
            sl = slice(h * HD, (h + 1) * HD)
            q = _blocks(q_ref[:, sl])
            k2 = _with_prev(k_ref[:, sl])
            do_h = _blocks(do_ref[:, sl])
            s = jnp.where(ok, _dot(q, k2, _BQK) * SCALE, NEG_INF)
            p = jnp.exp(s - _blocks(lse[:, h:h + 1]))
            dp = _dot(do_h, _with_prev(v_ref[:, sl]), _BQK)
            ds = p * (dp - _blocks(c[:, h:h + 1]))
            dsb = (ds * SCALE).astype(BF16)
            dq_ref[:, sl] = _dot(dsb, k2, _BQD).reshape(S, HD)
            dk_ref[:, sl] = _fold_prev(_dot(dsb, q, _BKD))
            dv_ref[:, sl] = _fold_prev(_dot(p, do_h, _BKD))

    return pl.pallas_call(
        body, name="dil_bwd", grid=(DIL_G,),
        in_specs=[_dil_group_spec(0), _dil_group_spec(1), _dil_group_spec(2), _dil_group_spec(0, LANES),
                  _dil_group_spec(0), _dil_group_spec(0, LANES)],
        out_specs=[_dil_group_spec(0)] * 3,
        out_shape=[jax.ShapeDtypeStruct((DIL_G, S, DIL_OUT), F32)] * 3,
        compiler_params=_cparams(("parallel",)),
    )(qkv3, qkv3, qkv3, lse3, do3, c3)


COMB_T = 256


def _dil_combine(o3, lse3):
    heads_per_lb = LANES // HD

    def body(o_ref, lse_ref, ob_ref, al_ref, *scratch):
        o_tok = [scratch[LB_PER_CB * gg:LB_PER_CB * (gg + 1)] for gg in range(DIL_G)]
        lse_tok = scratch[LB_PER_CB * DIL_G:]
        g = pl.program_id(0)
        for gs in range(DIL_G):
            @pl.when(g == gs)
            def _(gs=gs):
                for tok, sub in _residue_pieces(DIL_R[gs]):
                    for hf in range(LB_PER_CB):
                        o_tok[gs][hf][tok, :] = o_ref[sub, hf * LANES:(hf + 1) * LANES]
                    lse_tok[gs][tok, :] = lse_ref[sub, :]

        @pl.when(g == DIL_G - 1)
        def _():
            def chunk(i, carry):
                rows = pl.ds(pl.multiple_of(i * COMB_T, COMB_T), COMB_T)
                lse = [lse_tok[gg][rows, :] for gg in range(DIL_G)]
                m = jnp.maximum(jnp.maximum(lse[0], lse[1]), lse[2])
                e = [jnp.exp(lse[gg] - m) for gg in range(DIL_G)]
                den = (e[0] + e[1]) + e[2]
                al = [e[gg] / den for gg in range(DIL_G)]
                for gg in range(DIL_G):
                    al_ref[gg, rows, :] = al[gg]
                for h in range(DIL_HG):
                    hf, sl = h // heads_per_lb, slice((h % heads_per_lb) * HD, (h % heads_per_lb + 1) * HD)
                    acc = al[0][:, h:h + 1] * o_tok[0][hf][rows, sl]
                    for gg in range(1, DIL_G):
                        acc = acc + al[gg][:, h:h + 1] * o_tok[gg][hf][rows, sl]
                    ob_ref[rows, h * HD:(h + 1) * HD] = acc
                return carry

            lax.fori_loop(0, S // COMB_T, chunk, 0)

    return pl.pallas_call(
        body, name="dil_combine", grid=(DIL_G,),
        in_specs=[pl.BlockSpec((None, S, DIL_OUT), lambda g: (g, 0, 0)),
                  pl.BlockSpec((None, S, LANES), lambda g: (g, 0, 0))],
        out_specs=[pl.BlockSpec((S, DIL_OUT), lambda g: (0, 0)),
                   pl.BlockSpec((DIL_G, S, LANES), lambda g: (0, 0, 0))],
        out_shape=[jax.ShapeDtypeStruct((S, DIL_OUT), F32), jax.ShapeDtypeStruct((DIL_G, S, LANES), F32)],
        scratch_shapes=[pltpu.VMEM((S, LANES), F32)] * (DIL_G * (LB_PER_CB + 1)),
        compiler_params=_cparams(("arbitrary",)),
    )(o3, lse3)


def _dil_combine_bwd(dob, ob, alpha, deps=()):
    heads_per_lb = LANES // HD

    def body(dob_ref, ob_ref, al_ref, *refs):
        do_ref, c_ref = refs[len(deps):]
        g = pl.program_id(0)
        hf = pl.program_id(1)
        for gs in range(DIL_G):
            @pl.when(g == gs)
            def _(gs=gs):
                for tok, sub in _residue_pieces(DIL_R[gs]):
                    dob = dob_ref[tok, :]
                    prod = dob * ob_ref[tok, :]
                    al = al_ref[tok, :]
                    lane = lax.broadcasted_iota(jnp.int32, al.shape, 1)
                    c = jnp.where(hf == 0, 0.0, c_ref[sub, :])
                    for hh in range(heads_per_lb):
                        sl = slice(hh * HD, (hh + 1) * HD)
                        head = hf * heads_per_lb + hh
                        a = jnp.sum(jnp.where(lane == head, al, 0.0), axis=-1, keepdims=True)
                        do_ref[sub, sl] = (a * dob[:, sl]).astype(BF16)
                        c = jnp.where(lane == head, a * jnp.sum(prod[:, sl], axis=-1, keepdims=True), c)
                    c_ref[sub, :] = c

    half = pl.BlockSpec((S, LANES), lambda g, hf: (0, hf))
    return pl.pallas_call(
        body, name="dil_combine_bwd", grid=(DIL_G, LB_PER_CB),
        in_specs=[half, half, pl.BlockSpec((None, S, LANES), lambda g, hf: (g, 0, 0))] + [ANY] * len(deps),
        out_specs=[pl.BlockSpec((None, S, LANES), lambda g, hf: (g, 0, hf)),
                   pl.BlockSpec((None, S, LANES), lambda g, hf: (g, 0, 0))],
        out_shape=[jax.ShapeDtypeStruct((DIL_G, S, DIL_OUT), BF16), jax.ShapeDtypeStruct((DIL_G, S, LANES), F32)],
        compiler_params=_cparams(("parallel", "arbitrary")),
    )(dob, ob, alpha, *deps)


def _local_step(x, tgt, h1, g_attn, g_mlp, g_final, b_pad, w_in_t, hooks):
    tables = _rope_tables()

    first = hooks.first_deps()
    proj_a = _mm(h1, w_in_t, mode="nt", tm=S, tn=512, n=3 * FOX_W, out_dtypes=[BF16], name="proj_a", deps=first)
    proj_b = _mm(h1, w_in_t, mode="nt", tm=S, tn=DIL_W, n=3 * DIL_W, b_off=SEG_B // DIL_W, out_dtypes=[F32], name="proj_b",
                deps=first)
    proj_g = _mm(h1, w_in_t, mode="nt", tm=S, tn=D, n=2 * D, b_off=SEG_G // D, out_dtypes=[F32], name="proj_g",
                deps=first)
    proj_f = _mm(h1, w_in_t, mode="nt", tm=S, tn=LANES, n=LANES, b_off=SEG_F // LANES, out_dtypes=[F32], name="proj_f",
                deps=first)

    q_aug, k_aug = _fox_prepare(proj_a, proj_f, b_pad)
    oa, lse_a = _fox_fwd(q_aug, k_aug, proj_a)

    qkv3 = _rope_split(proj_b, tables)
    o3, lse3 = _dil_fwd(qkv3)
    ob, alpha = _dil_combine(o3, lse3)

    w_a, w_b, w_out = hooks.mixer_weights(ob)
    ya, yb, mixed, x2, h2 = _mixer_tail(oa, ob, w_a, w_b, w_out, proj_g, x, g_mlp)
    w_up_sh, w_down = hooks.mlp_weights(h2)

    def up_epilogue(acc):
        r = jnp.maximum(acc, 0.0)
        return acc, r * r

    u, act = _mm(h2, w_up_sh, mode="nn", tm=S, tn=DFF // N_DEV, b_sharded=True, out_dtypes=[F32, BF16],
                 name="mlp_up", epilogue=up_epilogue)

    def loss_epilogue(acc, rows, vecs):
        dx3, dg, loss = _final_loss_rows(rows[0] + acc, vecs[0], rows[1])
        return (dx3,), (dg, loss)

    dx3, dg_final, loss = _mm_rows(act, w_down, tm=ROWS_TM, name="mlp_down_loss", epilogue=loss_epilogue,
                                   rows=(x2, tgt), vecs=(g_final,), row_out=(F32,), vec_out=(D, LANES))

    def rms_bwd_epilogue(acc, rows, vecs):
        dx, dg = _rms_bwd_rows(acc, rows[0], rows[1], vecs[0])
        return (dx,), (dg,)

    du = _mm(dx3, w_down, mode="nt", tm=S, tn=512, out_dtypes=[BF16], name="mlp_down_bwd",
             epilogue=lambda acc, u_t: (acc * (2.0 * jnp.maximum(u_t, 0.0)),), extras=(u,))
    dw_down = _mm(act, dx3, mode="tn", tm=512, tn=D, out_dtypes=[F32], name="dw_down")
    dw_up_sh = _mm(h2, du, mode="tn", tm=D, tn=DFF // N_DEV, out_sharded=True, out_dtypes=[F32], name="dw_up")
    dx2, dg_mlp = _mm_rows(du, w_up_sh, b_sharded=True, tm=ROWS_TM, name="mlp_up_bwd", epilogue=rms_bwd_epilogue,
                           rows=(x2, dx3), vecs=(g_mlp,), row_out=(F32,), vec_out=(D,),
                           deps=hooks.mlp_grads(dw_up_sh, dw_down))

    dw_out = _mm(mixed, dx2, mode="tn", tm=D, tn=D, out_dtypes=[F32], name="dw_out")

    dyab, dproj = _out_proj_gate_bwd(dx2, w_out, proj_g, ya, yb)
    doa = _mm(dyab, w_a, mode="nt", tm=S, tn=FOX_W, k=D, a_off=0, out_dtypes=[BF16], name="branch_a_bwd")
    dw_a = _mm(oa, dyab, mode="tn", tm=FOX_W, tn=D, n=D, b_off=0, out_dtypes=[F32], name="dw_branch_a")
    dob = _mm(dyab, w_b, mode="nt", tm=S, tn=CB, k=D, a_off=1, out_dtypes=[F32], name="branch_b_bwd")
    dw_b = _mm(ob, dyab, mode="tn", tm=DIL_OUT, tn=D, n=D, b_off=1, out_dtypes=[F32], name="dw_branch_b")

    dproj, dft, dfs = _fox_bwd(q_aug, k_aug, proj_a, lse_a, doa, dproj, deps=hooks.mixer_grads(dw_a, dw_b, dw_out))
    dproj, db = _fox_dscan(dft, dfs, proj_f, b_pad, dproj)

    do3, c3 = _dil_combine_bwd(dob, ob, alpha, deps=hooks.mid_backward(db))
    dq3, dk3, dv3 = _dil_bwd(qkv3, lse3, do3, c3)
    dproj = _rope_merge_bwd(dq3, dk3, dv3, tables, dproj)

    dw_in_t = _mm(dproj, h1, mode="tn", tm=DW_IN_TM, tn=D, out_dtypes=[F32], name="dw_in")
    dx, dg_attn = _mm_rows(dproj, w_in_t, tm=ROWS_TM // 2, name="proj_bwd", epilogue=rms_bwd_epilogue,
                           rows=(x, dx2), vecs=(g_attn,), row_out=(F32,), vec_out=(D,),
                           deps=hooks.w_in_grad(dw_in_t))

    return loss, dx, (dg_attn, db, dg_mlp, dg_final)


SHARD_SHAPES = ((D_IN // N_DEV, D), (FOX_W, D // N_DEV), (DIL_OUT, D // N_DEV), (D // N_DEV, D),
                (D, DFF // N_DEV), (DFF // N_DEV, D))
W_NAMES = ("w_in", "w_a", "w_b", "w_out", "w_up", "w_down")
AG_COPIES = 7
HBM = pl.BlockSpec(memory_space=pltpu.HBM)
SEM = pl.BlockSpec(memory_space=pltpu.SEMAPHORE)


def _place():
    return lax.axis_index("x"), lax.axis_index("y"), lax.axis_index("c")


W_SLAB = 752
W_WIN = 272
PAD_BLK = 256


def _pad_runs():
    runs = sorted((pad, pad + b - a, p, a) for p in range(N_DEV) for a, b, pad in _shard_pieces(p))
    blocks = []
    for k in range(D_PAD // PAD_BLK):
        lo, hi = k * PAD_BLK, (k + 1) * PAD_BLK
        blocks.append([(max(lo, r0) - lo, min(hi, r1) - lo, p, a + max(lo, r0) - r0)
                       for r0, r1, p, a in runs if max(lo, r0) < min(hi, r1)])
    return blocks


def _gather_w_in(shard, x_in, g_attn, late_shards):
    blocks = _pad_runs()

    norm_t = 512
    n_chunks = S // norm_t

    n_late = len(late_shards)
    late_shapes = [t.shape for t in late_shards]

    def body(x_ref, xin_ref, g_ref, *refs):
        late_in, (out_ref, h1_ref), late_out = refs[:n_late], refs[n_late:n_late + 2], refs[n_late + 2:2 * n_late + 2]
        scratch = refs[2 * n_late + 2:]
        stage, land, xbuf = scratch[:3]
        late_stage, late_cast = scratch[3:3 + n_late], scratch[3 + n_late:3 + 2 * n_late]
        send_sems, recv_sems, load_sem, x_sems, late_sems, store_sems = scratch[3 + 2 * n_late:]
        x, y, c = _place()
        me, sibling = (x, y, c), (x, y, 1 - c)
        chips = [(1 - x, y), (x, 1 - y), (1 - x, 1 - y)]

        def slot(px, py, pc):
            return land.at[4 * px + 2 * py + pc]

        def copy(k, block, to):
            return pltpu.make_async_remote_copy(
                src_ref=slot(*block), dst_ref=slot(*block), send_sem=send_sems.at[k], recv_sem=recv_sems.at[k],
                device_id=to, device_id_type=MESH)

        load = pltpu.make_async_copy(x_ref, stage, load_sem)
        load.start()
        load.wait()
        land[4 * x + 2 * y + c] = stage[...].astype(BF16)
        sent = [copy(0, me, sibling)] + [copy(1 + j, me, (*chip, c)) for j, chip in enumerate(chips)]
        for cp in sent:
            cp.start()

        def x_load(i):
            return pltpu.make_async_copy(xin_ref.at[pl.ds(i * norm_t, norm_t)], xbuf.at[i % 2], x_sems.at[i % 2])

        late_loads = [pltpu.make_async_copy(late_in[i], late_stage[i], late_sems.at[i]) for i in range(n_late)]
        for ld in late_loads:
            ld.start()
        x_load(0).start()
        for i in range(n_chunks):
            if i + 1 < n_chunks:
                x_load(i + 1).start()
            x_load(i).wait()
            h1_ref[i * norm_t:(i + 1) * norm_t, :] = _rms_rows(xbuf[i % 2], g_ref[...]).astype(BF16)
        stores = []
        for i in range(n_late):
            late_loads[i].wait()
            late_cast[i][...] = late_stage[i][...].astype(BF16)
            stores.append(pltpu.make_async_copy(late_cast[i], late_out[i].at[4 * x + 2 * y + c], store_sems.at[i]))
            stores[-1].start()

        for j, chip in enumerate(chips):
            copy(1 + j, (*chip, c), me).wait_recv()
            sent.append(copy(4 + j, (*chip, c), sibling))
            sent[-1].start()
        copy(0, sibling, me).wait_recv()
        for j, chip in enumerate(chips):
            copy(4 + j, (*chip, 1 - c), me).wait_recv()
        for cp in sent:
            cp.wait_send()
        for st in stores:
            st.wait()

        row = lax.broadcasted_iota(jnp.int32, (PAD_BLK, D), 0)
        for k, runs in enumerate(blocks):
            out = jnp.zeros((PAD_BLK, D), F32)
            for o_lo, o_hi, p, a in runs:
                start = min(a // 16 * 16, W_SLAB - W_WIN)
                win = land[p, start:start + W_WIN, :].astype(F32)
                moved = pltpu.roll(win, (o_lo - (a - start)) % W_WIN, 0)[:PAD_BLK]
                out = jnp.where((row >= o_lo) & (row < o_hi), moved, out)
            out_ref[k * PAD_BLK:(k + 1) * PAD_BLK, :] = out.astype(BF16)

    return pl.pallas_call(
        body, name="all_gather_w_in",
        out_shape=[jax.ShapeDtypeStruct((D_PAD, D), BF16), jax.ShapeDtypeStruct((S, D), BF16)]
                  + [jax.ShapeDtypeStruct((N_DEV,) + sh, BF16) for sh in late_shapes],
        in_specs=[ANY, ANY, pl.BlockSpec(memory_space=pltpu.VMEM)] + [ANY] * n_late,
        out_specs=[pl.BlockSpec(memory_space=pltpu.VMEM)] * 2 + [ANY] * n_late,
        scratch_shapes=[pltpu.VMEM((W_SLAB, D), F32), pltpu.VMEM((N_DEV, W_SLAB, D), BF16),
                        pltpu.VMEM((2, norm_t, D), F32)]
                       + [pltpu.VMEM(sh, F32) for sh in late_shapes] + [pltpu.VMEM(sh, BF16) for sh in late_shapes]
                       + [pltpu.SemaphoreType.DMA((AG_COPIES,)), pltpu.SemaphoreType.DMA((AG_COPIES,)),
                          pltpu.SemaphoreType.DMA(()), pltpu.SemaphoreType.DMA((2,)),
                          pltpu.SemaphoreType.DMA((n_late,)), pltpu.SemaphoreType.DMA((n_late,))],
        compiler_params=_cparams(None),
    )(shard, x_in, g_attn, *late_shards)


PEER_COPIES = N_DEV - 1
SIBLING_COPIES = 4
CHIP_COPIES = 3


def _peer_copies(_, land_refs, send_sems, recv_sems):
    x, y, c = _place()
    mine = 4 * x + 2 * y + c
    copies = []
    for i, ref in enumerate(land_refs):
        for k in range(1, N_DEV):
            peer = (x ^ (k >> 2), y ^ ((k >> 1) & 1), c ^ (k & 1))
            n = i * PEER_COPIES + k - 1
            copies.append(pltpu.make_async_remote_copy(
                src_ref=ref.at[mine], dst_ref=ref.at[mine], send_sem=send_sems.at[n], recv_sem=recv_sems.at[n],
                device_id=peer, device_id_type=MESH))
    return copies


def _sibling_copies(g_refs, r_refs, send_sems, recv_sems):
    x, y, c = _place()
    return [pltpu.make_async_remote_copy(
        src_ref=g_refs[i].at[2 * k + (1 - c)], dst_ref=r_refs[i].at[k],
        send_sem=send_sems.at[SIBLING_COPIES * i + k], recv_sem=recv_sems.at[SIBLING_COPIES * i + k],
        device_id=(x, y, 1 - c), device_id_type=MESH) for i in range(len(g_refs)) for k in range(SIBLING_COPIES)]


def _sibling_half_copies(g_refs, r_refs, send_sems, recv_sems):
    x, y, c = _place()
    return [pltpu.make_async_remote_copy(
        src_ref=g_refs[i].at[k], dst_ref=r_refs[i].at[k],
        send_sem=send_sems.at[SIBLING_COPIES * i + k], recv_sem=recv_sems.at[SIBLING_COPIES * i + k],
        device_id=(x, y, 1 - c), device_id_type=MESH) for i in range(len(g_refs)) for k in range(SIBLING_COPIES)]


def _chip_copies(p_refs, r_refs, send_sems, recv_sems):
    x, y, c = _place()
    chips = [(1 - x, y), (x, 1 - y), (1 - x, 1 - y)]
    return [pltpu.make_async_remote_copy(
        src_ref=p_refs[i].at[2 * cx + cy], dst_ref=r_refs[i].at[j],
        send_sem=send_sems.at[CHIP_COPIES * i + j], recv_sem=recv_sems.at[CHIP_COPIES * i + j],
        device_id=(cx, cy, c), device_id_type=MESH) for i in range(len(p_refs)) for j, (cx, cy) in enumerate(chips)]


def _in_hbm(a):
    return pltpu.with_memory_space_constraint(a, pltpu.HBM)


def _exchange_start(name, copies_fn, n_copies, srcs, lands, after=()):
    n_s, n_l, n_a = len(srcs), len(lands), len(after)

    def body(*refs):
        outs = refs[n_s + n_l + n_a:]
        for cp in copies_fn(refs[:n_s], refs[n_s:n_s + n_l], outs[0], outs[1]):
            cp.start()
        outs[-1][...] = jnp.zeros_like(outs[-1])

    res = pl.pallas_call(
        body, name=name,
        out_shape=[pltpu.SemaphoreType.DMA((n_copies,)), pltpu.SemaphoreType.DMA((n_copies,))]
                  + [pltpu.HBM(t.shape, t.dtype) for t in (*srcs, *lands)] + [jax.ShapeDtypeStruct((8, LANES), F32)],
        in_specs=[HBM] * (n_s + n_l) + [ANY] * n_a,
        out_specs=[SEM, SEM] + [HBM] * (n_s + n_l) + [pl.BlockSpec(memory_space=pltpu.VMEM)],
        input_output_aliases={i: 2 + i for i in range(n_s + n_l)},
        compiler_params=pltpu.CompilerParams(has_side_effects=pltpu.SideEffectType.DATAFLOW_SIDE_EFFECTING),
    )(*[_in_hbm(t) for t in (*srcs, *lands)], *after)
    return res[0], res[1], list(res[2:2 + n_s]), list(res[2 + n_s:2 + n_s + n_l]), res[-1]


def _exchange_wait(name, copies_fn, started, after):
    send_sems, recv_sems, srcs, lands, _ = started
    n_s, n_l = len(srcs), len(lands)

    def body(*refs):
        for cp in copies_fn(refs[:n_s], refs[n_s:n_s + n_l], refs[n_s + n_l], refs[n_s + n_l + 1]):
            cp.wait_send()
            cp.wait_recv()

    res = pl.pallas_call(
        body, name=name,
        out_shape=[pltpu.HBM(t.shape, t.dtype) for t in (*srcs, *lands)],
        in_specs=[HBM] * (n_s + n_l) + [SEM, SEM, ANY],
        out_specs=[HBM] * (n_s + n_l),
        input_output_aliases={i: i for i in range(n_s + n_l)},
        compiler_params=pltpu.CompilerParams(has_side_effects=pltpu.SideEffectType.DATAFLOW_SIDE_EFFECTING),
    )(*srcs, *lands, send_sems, recv_sems, after)
    return list(res[:n_s]), list(res[n_s:])


def _shard_block(shape):
    rows, cols = shape
    if rows * cols <= 256 * 1024:
        return rows, cols
    if rows % 256:
        return rows, 256
    return 256, cols


def _chip_partial(ids, g_stack, recv1, name):
    shape = g_stack.shape[1:]
    br, bc = _shard_block(shape)
    whole = g_stack.shape[0] == N_DEV

    def body(ids_ref, g_ref, r_ref, pb_ref, own_ref):
        s = g_ref[...] + r_ref[...]
        pb_ref[...] = s.astype(BF16)

        @pl.when(pl.program_id(2) == ids_ref[1])
        def _():
            own_ref[...] = s

    grid_spec = pltpu.PrefetchScalarGridSpec(
        num_scalar_prefetch=1, grid=(shape[0] // br, shape[1] // bc, 4),
        in_specs=[pl.BlockSpec((None, br, bc), lambda r, q, k, ids: (2 * k + ids[0] if whole else k, r, q)),
                  pl.BlockSpec((None, br, bc), lambda r, q, k, ids: (k, r, q))],
        out_specs=[pl.BlockSpec((None, br, bc), lambda r, q, k, ids: (k, r, q)),
                   pl.BlockSpec((br, bc), lambda r, q, k, ids: (r, q))])
    return pl.pallas_call(
        body, name=name, grid_spec=grid_spec,
        out_shape=[jax.ShapeDtypeStruct((4,) + shape, BF16), jax.ShapeDtypeStruct(shape, F32)],
        compiler_params=_cparams(("parallel", "parallel", "arbitrary")),
    )(ids, g_stack, recv1)


def _adamw(w, g, m, v):
    m = ADAM_B1 * m + (1.0 - ADAM_B1) * g
    v = ADAM_B2 * v + (1.0 - ADAM_B2) * (g * g)
    m_hat = m / (1.0 - ADAM_B1 ** ADAM_STEP)
    v_hat = v / (1.0 - ADAM_B2 ** ADAM_STEP)
    delta = -ADAM_LR * (m_hat / (jnp.sqrt(v_hat) + ADAM_EPS) + ADAM_WD * w)
    return delta, m, v


def _reduce_adamw(own, recv2, w, m, v, name, deps=()):
    shape = own.shape
    br, bc = _shard_block(shape)

    def body(own_ref, r_ref, w_ref, m_ref, v_ref, *refs):
        g_ref, d_ref, nm_ref, nv_ref = refs[len(deps):]
        g = own_ref[...]
        for j in range(3):
            g = g + r_ref[j].astype(F32)
        delta, nm, nv = _adamw(w_ref[...], g, m_ref[...], v_ref[...])
        g_ref[...] = g
        d_ref[...] = delta
        nm_ref[...] = nm
        nv_ref[...] = nv

    blk = pl.BlockSpec((br, bc), lambda r, q: (r, q))
    return pl.pallas_call(
        body, name=name, grid=(shape[0] // br, shape[1] // bc),
        in_specs=[blk, pl.BlockSpec((3, br, bc), lambda r, q: (0, r, q)), blk, blk, blk] + [ANY] * len(deps),
        out_specs=[blk] * 4, out_shape=[jax.ShapeDtypeStruct(shape, F32)] * 4,
        compiler_params=_cparams(("parallel", "parallel")),
    )(own, recv2, w, m, v, *deps)


def _small_adamw(stack, w, m, v):
    def body(s_ref, w_ref, m_ref, v_ref, g_ref, *o_refs):
        g = s_ref[0]
        for s in range(1, N_DEV):
            g = g + s_ref[s]
        delta, nm, nv = _adamw(w_ref[...], g, m_ref[...], v_ref[...])
        g_ref[...] = g
        for q, t in enumerate((g, delta, nm, nv)):
            attn_ref, b_ref, mlp_ref, final_ref = o_refs[4 * q:4 * q + 4]
            for i in range(8):
                cols = slice(i * LANES, (i + 1) * LANES)
                attn_ref[:, cols] = t[i:i + 1]
                mlp_ref[:, cols] = t[8 + i:9 + i]
                final_ref[:, cols] = t[16 + i:17 + i]
            b_ref[...] = t[24:25, :FOX_H]

    vm = pl.BlockSpec(memory_space=pltpu.VMEM)
    unpacked = [jax.ShapeDtypeStruct((1, n), F32) for n in (D, FOX_H, D, D)]
    return pl.pallas_call(
        body, name="small_adamw",
        in_specs=[vm] * 4, out_specs=[vm] * 17,
        out_shape=[jax.ShapeDtypeStruct((SMALL_R, LANES), F32)] + unpacked * 4,
    )(stack, w, m, v)


def _cols_to_whole(stack):
    rows = stack.shape[1]
    return stack.transpose(1, 0, 2).reshape(rows, -1)


def _whole_to_cols(t):
    rows = t.shape[0]
    return t.reshape(rows, N_DEV, -1).transpose(1, 0, 2)


W_IN_SEGMENTS = ((0, 3 * FOX_W, SEG_A), (3 * FOX_W, 3 * FOX_W + FOX_H, SEG_F),
                 (3 * FOX_W + FOX_H, 3 * FOX_W + FOX_H + 3 * DIL_W, SEG_B), (3 * FOX_W + FOX_H + 3 * DIL_W, D_IN, SEG_G))


def _shard_pieces(p):
    rows = D_IN // N_DEV
    lo, hi = p * rows, (p + 1) * rows
    return [(max(lo, a) - lo, min(hi, b) - lo, pad + max(lo, a) - a)
            for a, b, pad in W_IN_SEGMENTS if max(lo, a) < min(hi, b)]


def _unpad_runs(p):
    rows = D_IN // N_DEV
    blocks = []
    for lo in range(0, rows, PAD_BLK):
        hi = min(lo + PAD_BLK, rows)
        blocks.append([(max(lo, a) - lo, min(hi, b) - lo, pad + max(lo, a) - a)
                       for a, b, pad in _shard_pieces(p) if max(lo, a) < min(hi, b)])
    return blocks


def _unpad_dw_in_t(dwp, ids, kept, deps=()):
    rows = D_IN // N_DEV
    plans = [_unpad_runs(p) for p in range(N_DEV)]
    windows = [[min(src // 8 * 8, D_PAD - W_WIN) for runs in plan for _, _, src in runs] for plan in plans]
    n_win = max(len(w) for w in windows)

    def body(ids_ref, src_ref, *refs):
        o_ref, buf, sems = refs[len(deps):]
        side = ids_ref[0] if kept else 1 - ids_ref[0]
        p = 2 * pl.program_id(0) + side
        row = lax.broadcasted_iota(jnp.int32, (PAD_BLK, D), 0)

        def fetch(q):
            slot = (q // 2) % 2
            return [pltpu.make_async_copy(src_ref.at[pl.ds(start, W_WIN)], buf.at[slot, n], sems.at[slot, n])
                    for n, start in enumerate(windows[q])]

        for q in range(N_DEV):
            @pl.when(p == q)
            def _(q=q):
                for nxt in ([q, q + 2] if q < 2 else [q + 2]):
                    if nxt < N_DEV:
                        for cp in fetch(nxt):
                            cp.start()
                copies, n = fetch(q), 0
                for j, runs in enumerate(plans[q]):
                    out = None
                    for o_lo, o_hi, src in runs:
                        copies[n].wait()
                        moved = pltpu.roll(buf[(q // 2) % 2, n], (o_lo - (src - windows[q][n])) % W_WIN, 0)[:PAD_BLK]
                        out = moved if out is None else jnp.where((row >= o_lo) & (row < o_hi), moved, out)
                        n += 1
                    size = min(PAD_BLK, rows - j * PAD_BLK)
                    o_ref[j * PAD_BLK:j * PAD_BLK + size, :] = out[:size]

    grid_spec = pltpu.PrefetchScalarGridSpec(
        num_scalar_prefetch=1, grid=(N_DEV // 2,),
        in_specs=[ANY] * (1 + len(deps)), out_specs=pl.BlockSpec((None, rows, D), lambda k, ids: (k, 0, 0)),
        scratch_shapes=[pltpu.VMEM((2, n_win, W_WIN, D), F32), pltpu.SemaphoreType.DMA((2, n_win))])
    return pl.pallas_call(
        body, name="unpad_dw_in_kept" if kept else "unpad_dw_in_sent", grid_spec=grid_spec,
        out_shape=jax.ShapeDtypeStruct((N_DEV // 2, rows, D), F32),
        compiler_params=_cparams(("arbitrary",)),
    )(ids, dwp, *deps)


LOSS_ROW = 25


def _pack_small(g_attn, b, g_mlp, g_final, loss_row=None):
    tail = jnp.pad(b, ((0, 7), (0, LANES - b.shape[1])))
    if loss_row is not None:
        tail = tail + jnp.pad(loss_row, ((LOSS_ROW - 24, 31 - LOSS_ROW), (0, 0)))
    return jnp.concatenate([g_attn.reshape(8, LANES), g_mlp.reshape(8, LANES), g_final.reshape(8, LANES), tail], axis=0)


class _StepComm:
    def __init__(self, ids, w_sh, m_sh, v_sh, lands, after_w_in):
        self.ids = ids
        self.w_sh, self.m_sh, self.v_sh = w_sh, m_sh, v_sh
        self.updates = None
        self.gather_mixer = _exchange_start("gather_mixer_start", _peer_copies, PEER_COPIES * 3, [], lands[:3],
                                            after=(after_w_in,))
        self.gather_mlp = _exchange_start("gather_mlp_start", _peer_copies, PEER_COPIES * 2, [], lands[3:],
                                          after=(self.gather_mixer[-1],))
        self.sibling = {}
        self.chips = {}
        self.own = {}
        self.names = {}
        self.copies_fn = {}
        self.kept = {}

    def _reduce_start(self, group, names, grads, after=(), copies_fn=_sibling_copies):
        lands = [lax.empty((SIBLING_COPIES,) + t.shape[1:], F32) for t in grads]
        self.sibling[group] = _exchange_start("grad_%s_sibling_start" % group, copies_fn,
                                              SIBLING_COPIES * len(grads), grads, lands, after=after)
        self.names[group] = names
        self.copies_fn[group] = copies_fn
        return self.sibling[group][-1]

    def _reduce_mid(self, group, after):
        grads, recv1 = _exchange_wait("grad_%s_sibling_wait" % group, self.copies_fn[group], self.sibling[group],
                                      after)
        grads = self.kept.get(group, grads)
        parts = [_chip_partial(self.ids, g, r, "grad_partial_" + n) for g, r, n in zip(grads, recv1, self.names[group])]
        self.own[group] = [p[1] for p in parts]
        srcs = [p[0] for p in parts]
        lands = [lax.empty((CHIP_COPIES,) + t.shape[1:], BF16) for t in srcs]
        self.chips[group] = _exchange_start("grad_%s_chips_start" % group, _chip_copies, CHIP_COPIES * len(srcs),
                                            srcs, lands)
        return self.chips[group][-1]

    def reduced(self, group, after):
        _, recv2 = _exchange_wait("grad_%s_chips_wait" % group, _chip_copies, self.chips[group], after)
        return list(zip(self.own[group], recv2))

    def first_deps(self):
        return [self.gather_mlp[-1]]

    def mixer_weights(self, after):
        _, (g_a, g_b, g_out) = _exchange_wait("gather_mixer_wait", _peer_copies, self.gather_mixer, after)
        return _cols_to_whole(g_a), _cols_to_whole(g_b), g_out.reshape(D, D)

    def mlp_weights(self, after):
        _, (g_up, g_down) = _exchange_wait("gather_mlp_wait", _peer_copies, self.gather_mlp, after)
        return g_up, g_down.reshape(DFF, D)

    def mlp_grads(self, dw_up_sh, dw_down):
        return [self._reduce_start("mlp", W_NAMES[4:], [dw_up_sh, dw_down.reshape((N_DEV,) + SHARD_SHAPES[5])])]

    def mixer_grads(self, dw_a, dw_b, dw_out):
        token = self._reduce_mid("mlp", dw_b)
        grads = [_whole_to_cols(dw_a), _whole_to_cols(dw_b), dw_out.reshape((N_DEV,) + SHARD_SHAPES[3])]
        return [self._reduce_start("mixer", W_NAMES[1:4], grads, after=(token,))]

    def mid_backward(self, after):
        return [self._reduce_mid("mixer", after)]

    def w_in_grad(self, dw_in_t):
        token = self._reduce_start("w_in", W_NAMES[:1], [_unpad_dw_in_t(dw_in_t, self.ids, kept=False)],
                                   copies_fn=_sibling_half_copies)
        self.kept["w_in"] = [_unpad_dw_in_t(dw_in_t, self.ids, kept=True, deps=[token])]
        reduced = self.reduced("mixer", token) + self.reduced("mlp", token)
        self.updates, last = [None] * len(reduced), self.kept["w_in"][0]
        for i in (3, 4, 0, 1, 2):
            if i == 4:
                last = self._reduce_mid("w_in", last)
            self.updates[i] = _reduce_adamw(*reduced[i], self.w_sh[1 + i], self.m_sh[1 + i], self.v_sh[1 + i],
                                            "adamw_" + W_NAMES[1 + i], deps=[last])
            last = self.updates[i][0]
        return [last]

    def w_in_update(self, after):
        (reduced,) = self.reduced("w_in", after)
        return _reduce_adamw(*reduced, self.w_sh[0], self.m_sh[0], self.v_sh[0], "adamw_" + W_NAMES[0])


def kernel(x, norm_attn_g, w_in, b_forget, w_branch_a, w_branch_b, w_out, norm_mlp_g, w_up, w_down, norm_final_g, loss_target, m_norm_attn_g, m_w_in, m_b_forget, m_w_branch_a, m_w_branch_b, m_w_out, m_norm_mlp_g, m_w_up, m_w_down, m_norm_final_g, v_norm_attn_g, v_w_in, v_b_forget, v_w_branch_a, v_w_branch_b, v_w_out, v_norm_mlp_g, v_w_up, v_w_down, v_norm_final_g):
    cx, cy, cc = _place()
    ids = jnp.stack([cc, 2 * cx + cy]).astype(jnp.int32)

    w_sh = [w_in[0].T] + [t[0] for t in (w_branch_a, w_branch_b, w_out, w_up, w_down)]
    m_sh = [m_w_in[0].T] + [t[0] for t in (m_w_branch_a, m_w_branch_b, m_w_out, m_w_up, m_w_down)]
    v_sh = [v_w_in[0].T] + [t[0] for t in (v_w_branch_a, v_w_branch_b, v_w_out, v_w_up, v_w_down)]

    w_in_t, h1, *lands = _gather_w_in(jnp.pad(w_sh[0], ((0, W_SLAB - w_sh[0].shape[0]), (0, 0))), x[0], norm_attn_g,
                                      w_sh[1:])
    comm = _StepComm(ids, w_sh, m_sh, v_sh, lands, w_in_t)
    b_pad = jnp.pad(b_forget, ((0, 0), (0, LANES - FOX_H)))

    loss_row, dx, dsmall = _local_step(
        x[0], loss_target[0], h1, norm_attn_g, norm_mlp_g, norm_final_g.reshape(1, D), b_pad, w_in_t, comm)

    dg_attn, db, dg_mlp, dg_final = dsmall
    stack = lax.dynamic_update_slice(jnp.zeros((N_DEV, SMALL_R, LANES), F32),
                                     _pack_small(dg_attn, db, dg_mlp, dg_final, loss_row)[None],
                                     (4 * cx + 2 * cy + cc, 0, 0))
    small_gather = _exchange_start("small_gather_start", _peer_copies, PEER_COPIES, [], [stack])
    w_in_update = comm.w_in_update(small_gather[-1])
    _, (stack,) = _exchange_wait("small_gather_wait", _peer_copies, small_gather, w_in_update[0])
    pack, *small = _small_adamw(
        stack,
        _pack_small(norm_attn_g, b_forget, norm_mlp_g, norm_final_g.reshape(1, D)),
        _pack_small(m_norm_attn_g, m_b_forget, m_norm_mlp_g, m_norm_final_g.reshape(1, D)),
        _pack_small(v_norm_attn_g, v_b_forget, v_norm_mlp_g, v_norm_final_g.reshape(1, D)))
    big = [w_in_update] + comm.updates

    outs = [pack[LOSS_ROW, 0], dx[None]]
    for q in range(4):
        s_attn, s_b, s_mlp, s_final = small[4 * q:4 * q + 4]
        s_final = s_final.reshape(D)
        b_in, b_a, b_b, b_out, b_up, b_down = [t[q][None] for t in big]
        b_in = jnp.swapaxes(b_in, 1, 2)
        outs += [s_attn, b_in, s_b, b_a, b_b, b_out, s_mlp, b_up, b_down, s_final]
    return tuple(outs)
```

```python
import functools

import jax
import jax.numpy as jnp
from jax import lax
from jax.experimental import pallas as pl
from jax.experimental.pallas import tpu as pltpu

F32 = jnp.float32
BF16 = jnp.bfloat16
MESH = pl.DeviceIdType.MESH

S = 2048
D = 1024
HD = 64
FOX_H = 8
FOX_W = FOX_H * HD
DIL_HG = 4
DIL_G = 3
DIL_W = DIL_G * DIL_HG * HD
DIL_OUT = DIL_HG * HD
DIL_BLK = 128
DIL_R = (1, 4, 16)
DFF = 4 * D
D_IN = 3 * FOX_W + FOX_H + 3 * DIL_W + 2 * D
EPS = 1e-6
NEG_INF = -1e30
SCALE = HD ** -0.5
ROPE_THETA = 500000.0
ROPE_DIM = HD // 4
N_DEV = 8

ADAM_LR = 0.001
ADAM_B1 = 0.9
ADAM_B2 = 0.999
ADAM_EPS = 1e-08
ADAM_WD = 0.01
ADAM_STEP = 10

LANES = 128
CB = 256
SEG_A = 0
SEG_B = 3 * FOX_W
SEG_Z = SEG_B + 3 * DIL_W
SEG_G = SEG_Z + CB
SEG_F = SEG_G + 2 * D
D_PAD = SEG_F + CB
DW_IN_TM = D_PAD // 10
SMALL_R = 32

VMEM_MB = 56


def _cparams(dims=None, vmem_mb=VMEM_MB, **kw):
    return pltpu.CompilerParams(dimension_semantics=dims, vmem_limit_bytes=vmem_mb << 20, **kw)


ANY = pl.BlockSpec(memory_space=pl.ANY)

_NN = (((1,), (0,)), ((), ()))
_NT = (((1,), (1,)), ((), ()))
_TN = (((0,), (0,)), ((), ()))


def _dot(a, b, dims):
    return lax.dot_general(a.astype(BF16), b.astype(BF16), dims, preferred_element_type=F32)


def _mm(a, b, *, mode, tm, tn, out_dtypes, name, n=None, k=None, a_off=0, b_off=0,
        b_sharded=False, out_sharded=False, epilogue=None, extras=(), deps=()):
    n_sh = b.shape[-1] if b_sharded else None
    if mode == "nn":
        m = a.shape[0]
        k = k or a.shape[1]
        a_spec = pl.BlockSpec((tm, k), lambda i, j: (i, a_off))
        if b_sharded:
            assert tn == n_sh
            n = N_DEV * n_sh
            b_spec = pl.BlockSpec((None, k, tn), lambda i, j: (j, 0, 0))
        else:
            n = n or b.shape[1]
            b_spec = pl.BlockSpec((k, tn), lambda i, j: (0, j + b_off))
        dims = _NN
    elif mode == "nt":
        m = a.shape[0]
        k = k or a.shape[1]
        a_spec = pl.BlockSpec((tm, k), lambda i, j: (i, a_off))
        if b_sharded:
            n = b.shape[1]
            b_spec = pl.BlockSpec((N_DEV, tn, n_sh), lambda i, j: (0, j, 0))
        else:
            n = n or b.shape[0]
            b_spec = pl.BlockSpec((tn, k), lambda i, j: (j + b_off, 0))
        dims = _NT
    else:
        k, m = a.shape
        n = n or b.shape[1]
        a_spec = pl.BlockSpec((k, tm), lambda i, j: (0, i))
        b_spec = pl.BlockSpec((k, tn), lambda i, j: (0, j + b_off))
        dims = _TN
    assert m % tm == 0 and n % tn == 0, (name, m, n, tm, tn)
    n_extra = len(extras)
    tile = pl.BlockSpec((tm, tn), lambda i, j: (i, j))
    if out_sharded:
        assert mode == "tn" and n // tn == N_DEV
        out_spec = pl.BlockSpec((None, tm, tn), lambda i, j: (j, i, 0))
        out_shape = (N_DEV, m, tn)
    else:
        out_spec, out_shape = tile, (m, n)

    def body(a_ref, b_ref, *refs):
        if mode == "nt" and b_sharded:
            acc = _dot(a_ref[:, 0:n_sh], b_ref[0], dims)
            for p in range(1, N_DEV):
                acc = acc + _dot(a_ref[:, p * n_sh:(p + 1) * n_sh], b_ref[p], dims)
        else:
            acc = _dot(a_ref[...], b_ref[...], dims)
        ex = [r[...] for r in refs[:n_extra]]
        outs = epilogue(acc, *ex) if epilogue is not None else (acc,)
        for o_ref, o in zip(refs[n_extra + len(deps):], outs):
            o_ref[...] = o.astype(o_ref.dtype)

    res = pl.pallas_call(
        body, name=name, grid=(m // tm, n // tn),
        in_specs=[a_spec, b_spec] + [tile] * n_extra + [ANY] * len(deps),
        out_specs=[out_spec] * len(out_dtypes),
        out_shape=[jax.ShapeDtypeStruct(out_shape, dt) for dt in out_dtypes],
        compiler_params=_cparams(("parallel", "parallel")),
    )(a, b, *extras, *deps)
    return res if len(out_dtypes) > 1 else res[0]


def _mm_rows(a, b, *, tm, name, epilogue, rows=(), vecs=(), row_out=(), vec_out=(), b_sharded=False, deps=()):
    m, k = a.shape
    n_rows, n_vecs, n_deps = len(rows), len(vecs), len(deps)
    n_sh = b.shape[-1] if b_sharded else None

    def body(a_ref, b_ref, *refs):
        row_refs, vec_refs = refs[:n_rows], refs[n_rows:n_rows + n_vecs]
        outs = refs[n_rows + n_vecs + n_deps:]
        if b_sharded:
            acc = _dot(a_ref[:, 0:n_sh], b_ref[0], _NT)
            for p in range(1, N_DEV):
                acc = acc + _dot(a_ref[:, p * n_sh:(p + 1) * n_sh], b_ref[p], _NT)
        else:
            acc = _dot(a_ref[...], b_ref[...], _NN)
        row_vals, vec_incs = epilogue(acc, [r[...] for r in row_refs], [v[...] for v in vec_refs])
        for o_ref, val in zip(outs[:len(row_out)], row_vals):
            o_ref[...] = val.astype(o_ref.dtype)

        @pl.when(pl.program_id(0) == 0)
        def _():
            for o_ref in outs[len(row_out):]:
                o_ref[...] = jnp.zeros_like(o_ref)

        for o_ref, inc in zip(outs[len(row_out):], vec_incs):
            o_ref[...] += inc

    tile = pl.BlockSpec((tm, D), lambda i: (i, 0))
    b_spec = pl.BlockSpec(b.shape, lambda i: (0,) * b.ndim)
    return pl.pallas_call(
        body, name=name, grid=(m // tm,),
        in_specs=[pl.BlockSpec((tm, k), lambda i: (i, 0)), b_spec] + [tile] * n_rows
                 + [pl.BlockSpec(v.shape, lambda i: (0, 0)) for v in vecs] + [ANY] * n_deps,
        out_specs=[tile] * len(row_out) + [pl.BlockSpec((1, w), lambda i: (0, 0)) for w in vec_out],
        out_shape=[jax.ShapeDtypeStruct((m, D), dt) for dt in row_out]
                  + [jax.ShapeDtypeStruct((1, w), F32) for w in vec_out],
        compiler_params=_cparams(("arbitrary",)),
    )(a, b, *rows, *vecs, *deps)


ROW_T = 256
ROWS_TM = 512


def _rms_rows(x, g):
    r = lax.rsqrt(jnp.mean(x * x, axis=-1, keepdims=True) + EPS)
    return (x * r) * g


def _rms_bwd_rows(dh, x, dres, g):
    r = lax.rsqrt(jnp.mean(x * x, axis=-1, keepdims=True) + EPS)
    xn = x * r
    dhn = dh * g
    dx = dres + r * (dhn - xn * jnp.mean(dhn * xn, axis=-1, keepdims=True))
    return dx, jnp.sum(dh * xn, axis=0, keepdims=True)


def _final_loss_rows(x, g, tgt):
    r = lax.rsqrt(jnp.mean(x * x, axis=-1, keepdims=True) + EPS)
    xn = x * r
    err = xn * g - tgt
    row_loss = jnp.mean(err * err, axis=-1, keepdims=True)
    loss = 0.5 * jnp.sum(row_loss, axis=0, keepdims=True) * jnp.ones((1, LANES), F32)
    dy = err * (1.0 / D)
    dyn = dy * g
    dx = r * (dyn - xn * jnp.mean(dyn * xn, axis=-1, keepdims=True))
    return dx, jnp.sum(dy * xn, axis=0, keepdims=True), loss


def _sigmoid(z):
    return 1.0 / (1.0 + jnp.exp(-z))


GATE_TR = 512


def _mixer_tail(oa, ob, w_a, w_b, w_out, proj_g, x, g_mlp):
    def body(oa_ref, ob_ref, wa_ref, wb_ref, wo_ref, g_ref, x_ref, gm_ref, ya_ref, yb_ref, mixed_ref, x2_ref, h2_ref):
        ya = _dot(oa_ref[...], wa_ref[...], _NN)
        yb = _dot(ob_ref[...], wb_ref[...], _NN)
        ya_ref[...] = ya
        yb_ref[...] = yb
        mixed = (_sigmoid(g_ref[:, :D]) * ya + _sigmoid(g_ref[:, D:]) * yb).astype(BF16)
        mixed_ref[...] = mixed
        x2 = x_ref[...] + _dot(mixed, wo_ref[...], _NN)
        x2_ref[...] = x2
        h2_ref[...] = _rms_rows(x2, gm_ref[...]).astype(BF16)

    def rows(width):
        return pl.BlockSpec((GATE_TR, width), lambda i: (i, 0))

    def whole(t):
        return pl.BlockSpec(t.shape, lambda i: (0, 0))

    return pl.pallas_call(
        body, name="mixer_tail", grid=(S // GATE_TR,),
        in_specs=[rows(FOX_W), rows(DIL_OUT), whole(w_a), whole(w_b), whole(w_out), rows(2 * D), rows(D), whole(g_mlp)],
        out_specs=[rows(D)] * 5,
        out_shape=[jax.ShapeDtypeStruct((S, D), dt) for dt in (F32, F32, BF16, F32, BF16)],
        compiler_params=_cparams(("parallel",)),
    )(oa, ob, w_a, w_b, w_out, proj_g, x, g_mlp)


def _out_proj_gate_bwd(dx2, w_out, proj_g, ya, yb):
    def body(dx_ref, w_ref, g_ref, ya_ref, yb_ref, dy_ref, dg_ref):
        dm = _dot(dx_ref[...], w_ref[...], _NT)
        for half, y_ref in enumerate((ya_ref, yb_ref)):
            cols = slice(half * D, (half + 1) * D)
            s = _sigmoid(g_ref[:, cols])
            dy_ref[:, cols] = (dm * s).astype(BF16)
            dg_ref[:, cols] = (dm * y_ref[...] * (s * (1.0 - s))).astype(BF16)

    row = pl.BlockSpec((GATE_TR, D), lambda i: (i, 0))
    wide = pl.BlockSpec((GATE_TR, 2 * D), lambda i: (i, 0))
    return pl.pallas_call(
        body, name="out_proj_gate_bwd", grid=(S // GATE_TR,),
        in_specs=[row, pl.BlockSpec((D, D), lambda i: (0, 0)), wide, row, row],
        out_specs=[wide, pl.BlockSpec((GATE_TR, 2 * D), lambda i: (i, SEG_G // (2 * D)))],
        out_shape=[jax.ShapeDtypeStruct((S, 2 * D), BF16), jax.ShapeDtypeStruct((S, D_PAD), BF16)],
        compiler_params=_cparams(("parallel",)),
    )(dx2, w_out, proj_g, ya, yb)


FOX_TQ = 512
FOX_TQ_FWD = 1024


def _scan_rows(x, reverse):
    n = x.shape[0]
    row = lax.broadcasted_iota(jnp.int32, x.shape, 0)
    k = 1
    while k < n:
        if reverse:
            x = x + jnp.where(row < n - k, pltpu.roll(x, n - k, 0), 0.0)
        else:
            x = x + jnp.where(row >= k, pltpu.roll(x, k, 0), 0.0)
        k *= 2
    return x


def _fox_dscan(dft, dfs, proj_f, b_pad, dproj):
    def body(dft_ref, dfs_ref, f_ref, b_ref, _, dfa_ref, db_ref):
        dfs_pad = jnp.concatenate([dfs_ref[...], jnp.zeros((LANES - FOX_H, S), F32)], axis=0)
        df = dfs_pad.T + dft_ref[0]
        for hp in range(1, dft_ref.shape[0]):
            df = df + pltpu.roll(dft_ref[hp], 2 * hp, 1)
        dlf = _scan_rows(df, reverse=True)
        z = f_ref[...] + b_ref[...]
        lane = lax.broadcasted_iota(jnp.int32, (S, LANES), 1)
        dfa = jnp.where(lane < FOX_H, dlf / (1.0 + jnp.exp(z)), 0.0)
        dfa_ref[:, :LANES] = dfa.astype(BF16)
        dfa_ref[:, LANES:] = jnp.zeros((S, CB - LANES), BF16)
        db_ref[...] = jnp.sum(dfa, axis=0, keepdims=True)

    return pl.pallas_call(
        body, name="fox_dscan", grid=(1,),
        in_specs=[pl.BlockSpec(dft.shape, lambda i: (0, 0, 0)), pl.BlockSpec((FOX_H, S), lambda i: (0, 0)),
                  pl.BlockSpec((S, LANES), lambda i: (0, 0)), pl.BlockSpec((1, LANES), lambda i: (0, 0)), ANY],
        out_specs=[pl.BlockSpec((S, CB), lambda i: (0, SEG_F // CB)), pl.BlockSpec((1, LANES), lambda i: (0, 0))],
        out_shape=[jax.ShapeDtypeStruct((S, D_PAD), BF16), jax.ShapeDtypeStruct((1, LANES), F32)],
        input_output_aliases={4: 0},
        compiler_params=_cparams(("arbitrary",)),
    )(dft, dfs, proj_f, b_pad, dproj)


FOX_NQ = S // FOX_TQ
FOX_HP = FOX_W // LANES
HEADS_PER_LB = LANES // HD
FOX_AUG = 2 * LANES


def _fox_prepare(proj_a, proj_f, b_pad):
    tr = 512

    def body(q_ref, k_ref, f_ref, b_ref, qa_ref, ka_ref, hi_s, mid_s, lo_s):
        i = pl.program_id(0)

        @pl.when(i == 0)
        def _():
            z = f_ref[...] + b_ref[...]
            lf = jnp.minimum(z, 0.0) - jnp.log1p(jnp.exp(-jnp.abs(z)))
            f = _scan_rows(lf, reverse=False)
            hi = f.astype(BF16).astype(F32)
            r1 = f - hi
            mid = r1.astype(BF16).astype(F32)
            hi_s[...] = hi
            mid_s[...] = mid
            lo_s[...] = (r1 - mid).astype(BF16).astype(F32)

        rows = pl.ds(pl.multiple_of(i * tr, tr), tr)
        hi, mid, lo = hi_s[rows, :], mid_s[rows, :], lo_s[rows, :]
        lane = lax.broadcasted_iota(jnp.int32, (tr, HD), 1)
        q_ones = jnp.where((lane >= 3) & (lane < 6), 1.0, 0.0)
        k_ones = jnp.where(lane < 3, 1.0, 0.0)
        for h in range(FOX_H):
            a1, a2, a3 = hi[:, h:h + 1], mid[:, h:h + 1], lo[:, h:h + 1]
            q_extra = jnp.where(lane == 0, a1, jnp.where(lane == 1, a2, jnp.where(lane == 2, a3, q_ones)))
            k_extra = jnp.where(lane == 3, -a1, jnp.where(lane == 4, -a2, jnp.where(lane == 5, -a3, k_ones)))
            base = h * LANES
            qa_ref[:, base:base + HD] = q_ref[:, h * HD:(h + 1) * HD] * jnp.asarray(SCALE, BF16)
            qa_ref[:, base + HD:base + LANES] = q_extra.astype(BF16)
            ka_ref[:, base:base + HD] = k_ref[:, h * HD:(h + 1) * HD]
            ka_ref[:, base + HD:base + LANES] = k_extra.astype(BF16)

    out_blk = pl.BlockSpec((tr, FOX_H * LANES), lambda i: (i, 0))
    return pl.pallas_call(
        body, name="fox_prepare", grid=(S // tr,),
        in_specs=[pl.BlockSpec((tr, FOX_W), lambda i: (i, 0)), pl.BlockSpec((tr, FOX_W), lambda i: (i, 1)),
                  pl.BlockSpec((S, LANES), lambda i: (0, 0)), pl.BlockSpec((1, LANES), lambda i: (0, 0))],
        out_specs=[out_blk, out_blk],
        out_shape=[jax.ShapeDtypeStruct((S, FOX_H * LANES), BF16)] * 2,
        scratch_shapes=[pltpu.VMEM((S, LANES), F32)] * 3,
        compiler_params=_cparams(("arbitrary",)),
    )(proj_a, proj_a, proj_f, b_pad)


def _fox_scores(qa, ka, w, tq):
    s = _dot(qa, ka, _NT)
    row = lax.broadcasted_iota(jnp.int32, (tq, tq), 0)
    col = lax.broadcasted_iota(jnp.int32, (tq, tq), 1)
    diag = jnp.where(col <= row, s[:, w * tq:], NEG_INF)
    return diag if w == 0 else jnp.concatenate([s[:, :w * tq], diag], axis=1)


def _fox_fwd(q_aug, k_aug, proj_a):
    tq = FOX_TQ_FWD

    def body(qa_ref, ka_ref, v_ref, o_ref, lse_ref):
        qi = pl.program_id(1)
        lane = lax.broadcasted_iota(jnp.int32, (tq, LANES), 1)
        for w in range(S // tq):
            @pl.when(qi == w)
            def _(w=w):
                width = (w + 1) * tq
                lse = jnp.zeros((tq, LANES), F32)
                for hh in range(HEADS_PER_LB):
                    aug = slice(hh * LANES, (hh + 1) * LANES)
                    sl = slice(hh * HD, (hh + 1) * HD)
                    s = _fox_scores(qa_ref[:, aug], ka_ref[:width, aug], w, tq)
                    m = jnp.max(s, axis=-1, keepdims=True)
                    p = jnp.exp(s - m)
                    l = jnp.sum(p, axis=-1, keepdims=True)
                    o_ref[:, sl] = _dot(p / l, v_ref[:width, sl], _NN)
                    lse = jnp.where(lane == hh, m + jnp.log(l), lse)
                lse_ref[...] = lse

    return pl.pallas_call(
        body, name="fox_fwd", grid=(FOX_HP, S // tq),
        in_specs=[pl.BlockSpec((tq, FOX_AUG), lambda h, i: (i, h)),
                  pl.BlockSpec((S, FOX_AUG), lambda h, i: (0, h)),
                  pl.BlockSpec((S, LANES), lambda h, i: (0, 2 * FOX_HP + h))],
        out_specs=[pl.BlockSpec((tq, LANES), lambda h, i: (i, h)),
                   pl.BlockSpec((None, tq, LANES), lambda h, i: (h, i, 0))],
        out_shape=[jax.ShapeDtypeStruct((S, FOX_W), F32), jax.ShapeDtypeStruct((FOX_HP, S, LANES), F32)],
        compiler_params=_cparams(("parallel", "parallel")),
    )(q_aug, k_aug, proj_a)


def _fox_bwd(q_aug, k_aug, proj_a, lse, do, dproj, deps=()):
    n_q = FOX_NQ

    def body(qa_ref, ka_ref, v_ref, lse_ref, do_ref, *refs):
        dproj_ref, dft_ref, dfs_ref, dk_acc, dv_acc, dq_buf, kv_buf, dq_sems, kv_sems = refs[1 + len(deps):]
        hp = pl.program_id(0)
        t = pl.program_id(1)
        slot = t % 2
        col = pl.multiple_of(hp * LANES, LANES)

        def dq_copy(step, buf_slot):
            rows = pl.ds(pl.multiple_of(step * FOX_TQ, FOX_TQ), FOX_TQ)
            return pltpu.make_async_copy(dq_buf.at[buf_slot], dproj_ref.at[rows, pl.ds(col, LANES)], dq_sems.at[buf_slot])

        @pl.when(t == 0)
        def _():
            dk_acc[...] = jnp.zeros_like(dk_acc)
            dv_acc[...] = jnp.zeros_like(dv_acc)

        @pl.when((t == 0) & (hp == 0))
        def _():
            dfs_ref[...] = jnp.zeros_like(dfs_ref)

        @pl.when(t >= 2)
        def _():
            dq_copy(t - 2, slot).wait()

        lane = lax.broadcasted_iota(jnp.int32, (FOX_TQ, LANES), 1)
        for w in range(n_q):
            @pl.when(t == w)
            def _(w=w):
                width = (w + 1) * FOX_TQ
                lse_all = lse_ref[...]
                sub = lax.broadcasted_iota(jnp.int32, (FOX_H, width), 0)
                dft = jnp.zeros((FOX_TQ, LANES), F32)
                for hh in range(HEADS_PER_LB):
                    aug = slice(hh * LANES, (hh + 1) * LANES)
                    sl = slice(hh * HD, (hh + 1) * HD)
                    head = hp * HEADS_PER_LB + hh
                    qa = qa_ref[:, aug]
                    ka = ka_ref[:width, aug]
                    do_h = do_ref[:, sl]
                    s = _fox_scores(qa, ka, w, FOX_TQ)
                    p = jnp.exp(s - lse_all[:, hh:hh + 1])
                    dp = _dot(do_h, v_ref[:width, sl], _NT)
                    ds = p * (dp - jnp.sum(dp * p, axis=-1, keepdims=True))
                    dft = jnp.where(lane == hh, jnp.sum(ds, axis=-1, keepdims=True), dft)
                    dfs_ref[:, :width] -= jnp.where(sub == head, jnp.sum(ds, axis=0, keepdims=True), 0.0)
                    dq_buf[slot, :, sl] = (_dot(ds, ka[:, :HD], _NN) * SCALE).astype(BF16)
                    dk_acc[:width, sl] += _dot(ds, qa[:, :HD], _TN)
                    dv_acc[:width, sl] += _dot(p, do_h, _TN)
                dft_ref[...] = dft

        dq_copy(t, slot).start()

        @pl.when(t == n_q - 1)
        def _():
            dq_copy(t - 1, 1 - slot).wait()
            dq_copy(t, slot).wait()
            kv_buf[0] = dk_acc[...].astype(BF16)
            kv_buf[1] = dv_acc[...].astype(BF16)
            copies = [pltpu.make_async_copy(
                kv_buf.at[j], dproj_ref.at[:, pl.ds(pl.multiple_of((j + 1) * FOX_W + hp * LANES, LANES), LANES)],
                kv_sems.at[j]) for j in range(2)]
            for cp in copies:
                cp.start()
            for cp in copies:
                cp.wait()

    qblk = pl.BlockSpec((FOX_TQ, FOX_AUG), lambda h, t: (t, h))
    lane_blk = pl.BlockSpec((None, FOX_TQ, LANES), lambda h, t: (h, t, 0))
    return pl.pallas_call(
        body, name="fox_bwd", grid=(FOX_HP, n_q),
        in_specs=[qblk, pl.BlockSpec((S, FOX_AUG), lambda h, t: (0, h)),
                  pl.BlockSpec((S, LANES), lambda h, t: (0, 2 * FOX_HP + h)),
                  lane_blk, pl.BlockSpec((FOX_TQ, LANES), lambda h, t: (t, h)), ANY] + [ANY] * len(deps),
        out_specs=[ANY, lane_blk, pl.BlockSpec((FOX_H, S), lambda h, t: (0, 0))],
        out_shape=[jax.ShapeDtypeStruct((S, D_PAD), BF16),
                   jax.ShapeDtypeStruct((FOX_HP, S, LANES), F32), jax.ShapeDtypeStruct((FOX_H, S), F32)],
        scratch_shapes=[pltpu.VMEM((S, LANES), F32), pltpu.VMEM((S, LANES), F32),
                        pltpu.VMEM((2, FOX_TQ, LANES), BF16), pltpu.VMEM((2, S, LANES), BF16),
                        pltpu.SemaphoreType.DMA((2,)), pltpu.SemaphoreType.DMA((2,))],
        input_output_aliases={5: 0},
        compiler_params=_cparams(("arbitrary", "arbitrary")),
    )(q_aug, k_aug, proj_a, lse, do, dproj, *deps)


def _rope_tables():
    half = ROPE_DIM // 2
    shape = (DIL_G, half, S)
    g = lax.broadcasted_iota(jnp.int32, shape, 0)
    col = lax.broadcasted_iota(jnp.int32, shape, 2)
    r = jnp.left_shift(1, 2 * g)
    per_class = S // r
    pos = (col % per_class) * r + col // per_class
    inv_freq = jnp.power(jnp.float32(ROPE_THETA), -lax.broadcasted_iota(F32, shape, 1) * 2.0 / ROPE_DIM)
    ang = pos.astype(F32) * inv_freq
    return jnp.concatenate([jnp.cos(ang), jnp.sin(ang)], axis=1)


def _rope_factors(cs):
    half = ROPE_DIM // 2
    i = lax.broadcasted_iota(jnp.int32, (2 * half, 2 * LANES), 0)
    l = lax.broadcasted_iota(jnp.int32, (2 * half, 2 * LANES), 1)
    hit = ((l < LANES) & (i == l % half)) | ((l >= LANES) & (i - half == l % half))
    spread = jnp.where(hit, 1.0, 0.0).astype(BF16)
    hi = cs.astype(BF16)
    r1 = cs - hi.astype(F32)
    mid = r1.astype(BF16)
    lo = (r1 - mid.astype(F32)).astype(BF16)
    full = (_dot(hi, spread, _TN) + _dot(mid, spread, _TN)) + _dot(lo, spread, _TN)
    cos, sin = full[:, :LANES], full[:, LANES:]
    lane = lax.broadcasted_iota(jnp.int32, cos.shape, 1) % HD
    t_self = jnp.where(lane < ROPE_DIM, cos, 1.0)
    t_up = jnp.where(lane < half, -sin, 0.0)
    t_dn = jnp.where((lane >= half) & (lane < ROPE_DIM), sin, 0.0)
    return t_self, t_up, t_dn


def _residue_pieces(r):
    if r == 1:
        return [(slice(i, i + 512), slice(i, i + 512)) for i in range(0, S, 512)]
    n = S // r
    return [(pl.ds(j, n, stride=r), slice(j * n, (j + 1) * n)) for j in range(r)]


def _rope_apply(x, a, u, d, backward):
    half = ROPE_DIM // 2
    if backward:
        return x * a + pltpu.roll(x * u, half, 1) + pltpu.roll(x * d, LANES - half, 1)
    return x * a + pltpu.roll(x, LANES - half, 1) * u + pltpu.roll(x, half, 1) * d


def _rope_cache_factors(cs_ref, fac, first_step_of_group):
    @pl.when(first_step_of_group)
    def _():
        for lo in range(0, S, 512):
            for j, f in enumerate(_rope_factors(cs_ref[:, lo:lo + 512])):
                fac[j, lo:lo + 512, :] = f


LB_PER_CB = CB // LANES
ROPE_NB = 3 * DIL_G * LB_PER_CB


def _rope_step(s):
    per_group = 3 * LB_PER_CB
    return s // per_group, (s % per_group) // LB_PER_CB, s % LB_PER_CB


def _rope_split(proj_b, tables):
    def body(x_ref, cs_ref, o_ref, fac):
        g, t, hf = _rope_step(pl.program_id(0))
        _rope_cache_factors(cs_ref, fac, (t == 0) & (hf == 0))
        for gs in range(DIL_G):
            @pl.when(g == gs)
            def _(gs=gs):
                for tok, sub in _residue_pieces(DIL_R[gs]):
                    x = x_ref[tok, :]
                    y = _rope_apply(x, fac[0, sub, :], fac[1, sub, :], fac[2, sub, :], False)
                    o_ref[sub, :] = jnp.where(t < 2, y, x).astype(BF16)

    def in_index(s):
        g, t, hf = _rope_step(s)
        return 0, (t * DIL_G + g) * LB_PER_CB + hf

    def out_index(s):
        g, t, hf = _rope_step(s)
        return t * DIL_G + g, 0, hf

    tab = pl.BlockSpec((None, 2 * (ROPE_DIM // 2), S), lambda s: (_rope_step(s)[0], 0, 0))
    return pl.pallas_call(
        body, name="rope_split", grid=(ROPE_NB,),
        in_specs=[pl.BlockSpec((S, LANES), in_index), tab],
        out_specs=pl.BlockSpec((None, S, LANES), out_index),
        out_shape=jax.ShapeDtypeStruct((3 * DIL_G, S, CB), BF16),
        scratch_shapes=[pltpu.VMEM((3, S, LANES), F32)],
        compiler_params=_cparams(("arbitrary",)),
    )(proj_b, tables)


def _rope_merge_bwd(dq3, dk3, dv3, tables, dproj):
    def body(dq_ref, dk_ref, dv_ref, cs_ref, _, o_ref, tmp, fac):
        s = pl.program_id(0)
        g, t, hf = _rope_step(s)
        _rope_cache_factors(cs_ref, fac, (t == 0) & (hf == 0) & (s < ROPE_NB))
        for gs in range(DIL_G):
            @pl.when((g == gs) & (s < ROPE_NB))
            def _(gs=gs):
                for tok, sub in _residue_pieces(DIL_R[gs]):
                    x = jnp.where(t == 0, dq_ref[sub, :], jnp.where(t == 1, dk_ref[sub, :], dv_ref[sub, :]))
                    y = _rope_apply(x, fac[0, sub, :], fac[1, sub, :], fac[2, sub, :], True)
                    tmp[tok, :] = jnp.where(t < 2, y, x)
                o_ref[...] = tmp[...].astype(BF16)

        @pl.when(s >= ROPE_NB)
        def _():
            o_ref[...] = jnp.zeros_like(o_ref)

    def src_spec(own):
        def index(s):
            g, t, hf = _rope_step(jnp.minimum(s, ROPE_NB - 1))
            return g, 0, jnp.where(t == own, hf, jnp.where(t > own, LB_PER_CB - 1, 0))
        return pl.BlockSpec((None, S, LANES), index)

    def out_index(s):
        g, t, hf = _rope_step(s)
        col = SEG_B // LANES + (t * DIL_G + g) * LB_PER_CB + hf
        return 0, jnp.where(s < ROPE_NB, col, SEG_Z // LANES + s - ROPE_NB)

    tab = pl.BlockSpec((None, 2 * (ROPE_DIM // 2), S), lambda s: (_rope_step(jnp.minimum(s, ROPE_NB - 1))[0], 0, 0))
    return pl.pallas_call(
        body, name="rope_merge_bwd", grid=(ROPE_NB + LB_PER_CB,),
        in_specs=[src_spec(0), src_spec(1), src_spec(2), tab, ANY],
        out_specs=pl.BlockSpec((S, LANES), out_index),
        out_shape=jax.ShapeDtypeStruct((S, D_PAD), BF16),
        scratch_shapes=[pltpu.VMEM((S, LANES), F32), pltpu.VMEM((3, S, LANES), F32)],
        input_output_aliases={4: 0},
        compiler_params=_cparams(("arbitrary",)),
    )(dq3, dk3, dv3, tables, dproj)


DIL_NB = S // DIL_BLK


_BQK = (((2,), (2,)), ((0,), (0,)))
_BQD = (((2,), (1,)), ((0,), (0,)))
_BKD = (((1,), (1,)), ((0,), (0,)))


def _dil_mask(g):
    shape = (DIL_NB, DIL_BLK, 2 * DIL_BLK)
    blocks_per_seq = jnp.right_shift(DIL_NB, 2 * g)
    has_prev = jnp.bitwise_and(lax.broadcasted_iota(jnp.int32, shape, 0), blocks_per_seq - 1) != 0
    a = lax.broadcasted_iota(jnp.int32, shape, 1)
    kk = lax.broadcasted_iota(jnp.int32, shape, 2)
    diff = DIL_BLK + a - kk
    return (diff >= 0) & (diff <= DIL_BLK) & ((kk >= DIL_BLK) | has_prev)


def _blocks(t):
    return t.reshape(DIL_NB, DIL_BLK, t.shape[-1])


def _with_prev(t):
    prev = jnp.concatenate([jnp.zeros((DIL_BLK, t.shape[-1]), t.dtype), t[:-DIL_BLK]], axis=0)
    return jnp.concatenate([_blocks(prev), _blocks(t)], axis=1)


def _fold_prev(t2):
    n = t2.shape[-1]
    to_prev = t2[:, :DIL_BLK].reshape(S, n)
    own = t2[:, DIL_BLK:].reshape(S, n)
    return own + jnp.concatenate([to_prev[DIL_BLK:], jnp.zeros((DIL_BLK, n), t2.dtype)], axis=0)


def _dil_group_spec(t, width=DIL_OUT):
    return pl.BlockSpec((None, S, width), lambda g: (t * DIL_G + g, 0, 0))


def _dil_fwd(qkv3):
    def body(q_ref, k_ref, v_ref, o_ref, lse_ref):
        ok = _dil_mask(pl.program_id(0))
        lane = lax.broadcasted_iota(jnp.int32, (S, LANES), 1)
        lse = jnp.zeros((S, LANES), F32)
        for h in range(DIL_HG):
            sl = slice(h * HD, (h + 1) * HD)
            s = jnp.where(ok, _dot(_blocks(q_ref[:, sl]), _with_prev(k_ref[:, sl]), _BQK) * SCALE, NEG_INF)
            m = jnp.max(s, axis=-1, keepdims=True)
            p = jnp.exp(s - m)
            l = jnp.sum(p, axis=-1, keepdims=True)
            o_ref[:, sl] = _dot(p / l, _with_prev(v_ref[:, sl]), _BQD).reshape(S, HD)
            lse = jnp.where(lane == h, (m + jnp.log(l)).reshape(S, 1), lse)
        lse_ref[...] = lse

    return pl.pallas_call(
        body, name="dil_fwd", grid=(DIL_G,),
        in_specs=[_dil_group_spec(0), _dil_group_spec(1), _dil_group_spec(2)],
        out_specs=[_dil_group_spec(0), _dil_group_spec(0, LANES)],
        out_shape=[jax.ShapeDtypeStruct((DIL_G, S, DIL_OUT), F32), jax.ShapeDtypeStruct((DIL_G, S, LANES), F32)],
        compiler_params=_cparams(("parallel",)),
    )(qkv3, qkv3, qkv3)


def _dil_bwd(qkv3, lse3, do3, c3):
    def body(q_ref, k_ref, v_ref, lse_ref, do_ref, c_ref, dq_ref, dk_ref, dv_ref):
        ok = _dil_mask(pl.program_id(0))
        lse = lse_ref[...]
        c = c_ref[...]
        for h in range(DIL_HG):
            sl = slice(h * HD, (h + 1) * HD)
            q = _blocks(q_ref[:, sl])
            k2 = _with_prev(k_ref[:, sl])
            do_h = _blocks(do_ref[:, sl])
            s = jnp.where(ok, _dot(q, k2, _BQK) * SCALE, NEG_INF)
            p = jnp.exp(s - _blocks(lse[:, h:h + 1]))
            dp = _dot(do_h, _with_prev(v_ref[:, sl]), _BQK)
            ds = p * (dp - _blocks(c[:, h:h + 1]))
            dsb = (ds * SCALE).astype(BF16)
            dq_ref[:, sl] = _dot(dsb, k2, _BQD).reshape(S, HD)
            dk_ref[:, sl] = _fold_prev(_dot(dsb, q, _BKD))
            dv_ref[:, sl] = _fold_prev(_dot(p, do_h, _BKD))

    return pl.pallas_call(
        body, name="dil_bwd", grid=(DIL_G,),
        in_specs=[_dil_group_spec(0), _dil_group_spec(1), _dil_group_spec(2), _dil_group_spec(0, LANES),
                  _dil_group_spec(0), _dil_group_spec(0, LANES)],
        out_specs=[_dil_group_spec(0)] * 3,
        out_shape=[jax.ShapeDtypeStruct((DIL_G, S, DIL_OUT), F32)] * 3,
        compiler_params=_cparams(("parallel",)),
    )(qkv3, qkv3, qkv3, lse3, do3, c3)


COMB_T = 256


def _dil_combine(o3, lse3):
    heads_per_lb = LANES // HD

    def body(o_ref, lse_ref, ob_ref, al_ref, *scratch):
        o_tok = [scratch[LB_PER_CB * gg:LB_PER_CB * (gg + 1)] for gg in range(DIL_G)]
        lse_tok = scratch[LB_PER_CB * DIL_G:]
        g = pl.program_id(0)
        for gs in range(DIL_G):
            @pl.when(g == gs)
            def _(gs=gs):
                for tok, sub in _residue_pieces(DIL_R[gs]):
                    for hf in range(LB_PER_CB):
                        o_tok[gs][hf][tok, :] = o_ref[sub, hf * LANES:(hf + 1) * LANES]
                    lse_tok[gs][tok, :] = lse_ref[sub, :]

        @pl.when(g == DIL_G - 1)
        def _():
            def chunk(i, carry):
                rows = pl.ds(pl.multiple_of(i * COMB_T, COMB_T), COMB_T)
                lse = [lse_tok[gg][rows, :] for gg in range(DIL_G)]
                m = jnp.maximum(jnp.maximum(lse[0], lse[1]), lse[2])
                e = [jnp.exp(lse[gg] - m) for gg in range(DIL_G)]
                den = (e[0] + e[1]) + e[2]
                al = [e[gg] / den for gg in range(DIL_G)]
                for gg in range(DIL_G):
                    al_ref[gg, rows, :] = al[gg]
                for h in range(DIL_HG):
                    hf, sl = h // heads_per_lb, slice((h % heads_per_lb) * HD, (h % heads_per_lb + 1) * HD)
                    acc = al[0][:, h:h + 1] * o_tok[0][hf][rows, sl]
                    for gg in range(1, DIL_G):
                        acc = acc + al[gg][:, h:h + 1] * o_tok[gg][hf][rows, sl]
                    ob_ref[rows, h * HD:(h + 1) * HD] = acc
                return carry

            lax.fori_loop(0, S // COMB_T, chunk, 0)

    return pl.pallas_call(
        body, name="dil_combine", grid=(DIL_G,),
        in_specs=[pl.BlockSpec((None, S, DIL_OUT), lambda g: (g, 0, 0)),
                  pl.BlockSpec((None, S, LANES), lambda g: (g, 0, 0))],
        out_specs=[pl.BlockSpec((S, DIL_OUT), lambda g: (0, 0)),
                   pl.BlockSpec((DIL_G, S, LANES), lambda g: (0, 0, 0))],
        out_shape=[jax.ShapeDtypeStruct((S, DIL_OUT), F32), jax.ShapeDtypeStruct((DIL_G, S, LANES), F32)],
        scratch_shapes=[pltpu.VMEM((S, LANES), F32)] * (DIL_G * (LB_PER_CB + 1)),
        compiler_params=_cparams(("arbitrary",)),
    )(o3, lse3)


def _dil_combine_bwd(dob, ob, alpha, deps=()):
    heads_per_lb = LANES // HD

    def body(dob_ref, ob_ref, al_ref, *refs):
        do_ref, c_ref = refs[len(deps):]
        g = pl.program_id(0)
        hf = pl.program_id(1)
        for gs in range(DIL_G):
            @pl.when(g == gs)
            def _(gs=gs):
                for tok, sub in _residue_pieces(DIL_R[gs]):
                    dob = dob_ref[tok, :]
                    prod = dob * ob_ref[tok, :]
                    al = al_ref[tok, :]
                    lane = lax.broadcasted_iota(jnp.int32, al.shape, 1)
                    c = jnp.where(hf == 0, 0.0, c_ref[sub, :])
                    for hh in range(heads_per_lb):
                        sl = slice(hh * HD, (hh + 1) * HD)
                        head = hf * heads_per_lb + hh
                        a = jnp.sum(jnp.where(lane == head, al, 0.0), axis=-1, keepdims=True)
                        do_ref[sub, sl] = (a * dob[:, sl]).astype(BF16)
                        c = jnp.where(lane == head, a * jnp.sum(prod[:, sl], axis=-1, keepdims=True), c)
                    c_ref[sub, :] = c

    half = pl.BlockSpec((S, LANES), lambda g, hf: (0, hf))
    return pl.pallas_call(
        body, name="dil_combine_bwd", grid=(DIL_G, LB_PER_CB),
        in_specs=[half, half, pl.BlockSpec((None, S, LANES), lambda g, hf: (g, 0, 0))] + [ANY] * len(deps),
        out_specs=[pl.BlockSpec((None, S, LANES), lambda g, hf: (g, 0, hf)),
                   pl.BlockSpec((None, S, LANES), lambda g, hf: (g, 0, 0))],
        out_shape=[jax.ShapeDtypeStruct((DIL_G, S, DIL_OUT), BF16), jax.ShapeDtypeStruct((DIL_G, S, LANES), F32)],
        compiler_params=_cparams(("parallel", "arbitrary")),
    )(dob, ob, alpha, *deps)


def _local_step(x, tgt, h1, g_attn, g_mlp, g_final, b_pad, w_in_t, hooks):
    tables = _rope_tables()

    first = hooks.first_deps()
    proj_a = _mm(h1, w_in_t, mode="nt", tm=S, tn=512, n=3 * FOX_W, out_dtypes=[BF16], name="proj_a", deps=first)
    proj_b = _mm(h1, w_in_t, mode="nt", tm=S, tn=DIL_W, n=3 * DIL_W, b_off=SEG_B // DIL_W, out_dtypes=[F32], name="proj_b",
                deps=first)
    proj_g = _mm(h1, w_in_t, mode="nt", tm=S, tn=D, n=2 * D, b_off=SEG_G // D, out_dtypes=[F32], name="proj_g",
                deps=first)
    proj_f = _mm(h1, w_in_t, mode="nt", tm=S, tn=LANES, n=LANES, b_off=SEG_F // LANES, out_dtypes=[F32], name="proj_f",
                deps=first)

    q_aug, k_aug = _fox_prepare(proj_a, proj_f, b_pad)
    oa, lse_a = _fox_fwd(q_aug, k_aug, proj_a)

    qkv3 = _rope_split(proj_b, tables)
    o3, lse3 = _dil_fwd(qkv3)
    ob, alpha = _dil_combine(o3, lse3)

    w_a, w_b, w_out = hooks.mixer_weights(ob)
    ya, yb, mixed, x2, h2 = _mixer_tail(oa, ob, w_a, w_b, w_out, proj_g, x, g_mlp)
    w_up_sh, w_down = hooks.mlp_weights(h2)

    def up_epilogue(acc):
        r = jnp.maximum(acc, 0.0)
        return acc, r * r

    u, act = _mm(h2, w_up_sh, mode="nn", tm=S, tn=DFF // N_DEV, b_sharded=True, out_dtypes=[F32, BF16],
                 name="mlp_up", epilogue=up_epilogue)

    def loss_epilogue(acc, rows, vecs):
        dx3, dg, loss = _final_loss_rows(rows[0] + acc, vecs[0], rows[1])
        return (dx3,), (dg, loss)

    dx3, dg_final, loss = _mm_rows(act, w_down, tm=ROWS_TM, name="mlp_down_loss", epilogue=loss_epilogue,
                                   rows=(x2, tgt), vecs=(g_final,), row_out=(F32,), vec_out=(D, LANES))

    def rms_bwd_epilogue(acc, rows, vecs):
        dx, dg = _rms_bwd_rows(acc, rows[0], rows[1], vecs[0])
        return (dx,), (dg,)

    du = _mm(dx3, w_down, mode="nt", tm=S, tn=512, out_dtypes=[BF16], name="mlp_down_bwd",
             epilogue=lambda acc, u_t: (acc * (2.0 * jnp.maximum(u_t, 0.0)),), extras=(u,))
    dw_down = _mm(act, dx3, mode="tn", tm=512, tn=D, out_dtypes=[F32], name="dw_down")
    dw_up_sh = _mm(h2, du, mode="tn", tm=D, tn=DFF // N_DEV, out_sharded=True, out_dtypes=[F32], name="dw_up")
    dx2, dg_mlp = _mm_rows(du, w_up_sh, b_sharded=True, tm=ROWS_TM, name="mlp_up_bwd", epilogue=rms_bwd_epilogue,
                           rows=(x2, dx3), vecs=(g_mlp,), row_out=(F32,), vec_out=(D,),
                           deps=hooks.mlp_grads(dw_up_sh, dw_down))

    dw_out = _mm(mixed, dx2, mode="tn", tm=D, tn=D, out_dtypes=[F32], name="dw_out")

    dyab, dproj = _out_proj_gate_bwd(dx2, w_out, proj_g, ya, yb)
    doa = _mm(dyab, w_a, mode="nt", tm=S, tn=FOX_W, k=D, a_off=0, out_dtypes=[BF16], name="branch_a_bwd")
    dw_a = _mm(oa, dyab, mode="tn", tm=FOX_W, tn=D, n=D, b_off=0, out_dtypes=[F32], name="dw_branch_a")
    dob = _mm(dyab, w_b, mode="nt", tm=S, tn=CB, k=D, a_off=1, out_dtypes=[F32], name="branch_b_bwd")
    dw_b = _mm(ob, dyab, mode="tn", tm=DIL_OUT, tn=D, n=D, b_off=1, out_dtypes=[F32], name="dw_branch_b")

    dproj, dft, dfs = _fox_bwd(q_aug, k_aug, proj_a, lse_a, doa, dproj, deps=hooks.mixer_grads(dw_a, dw_b, dw_out))
    dproj, db = _fox_dscan(dft, dfs, proj_f, b_pad, dproj)

    do3, c3 = _dil_combine_bwd(dob, ob, alpha, deps=hooks.mid_backward(db))
    dq3, dk3, dv3 = _dil_bwd(qkv3, lse3, do3, c3)
    dproj = _rope_merge_bwd(dq3, dk3, dv3, tables, dproj)

    dw_in_t = _mm(dproj, h1, mode="tn", tm=DW_IN_TM, tn=D, out_dtypes=[F32], name="dw_in")
    dx, dg_attn = _mm_rows(dproj, w_in_t, tm=ROWS_TM // 2, name="proj_bwd", epilogue=rms_bwd_epilogue,
                           rows=(x, dx2), vecs=(g_attn,), row_out=(F32,), vec_out=(D,),
                           deps=hooks.w_in_grad(dw_in_t))

    return loss, dx, (dg_attn, db, dg_mlp, dg_final)


SHARD_SHAPES = ((D_IN // N_DEV, D), (FOX_W, D // N_DEV), (DIL_OUT, D // N_DEV), (D // N_DEV, D),
                (D, DFF // N_DEV), (DFF // N_DEV, D))
W_NAMES = ("w_in", "w_a", "w_b", "w_out", "w_up", "w_down")
AG_COPIES = 7
HBM = pl.BlockSpec(memory_space=pltpu.HBM)
SEM = pl.BlockSpec(memory_space=pltpu.SEMAPHORE)


def _place():
    return lax.axis_index("x"), lax.axis_index("y"), lax.axis_index("c")


W_SLAB = 752
W_WIN = 272
PAD_BLK = 256


def _pad_runs():
    runs = sorted((pad, pad + b - a, p, a) for p in range(N_DEV) for a, b, pad in _shard_pieces(p))
    blocks = []
    for k in range(D_PAD // PAD_BLK):
        lo, hi = k * PAD_BLK, (k + 1) * PAD_BLK
        blocks.append([(max(lo, r0) - lo, min(hi, r1) - lo, p, a + max(lo, r0) - r0)
                       for r0, r1, p, a in runs if max(lo, r0) < min(hi, r1)])
    return blocks


def _gather_w_in(shard, x_in, g_attn, late_shards):
    blocks = _pad_runs()

    norm_t = 512
    n_chunks = S // norm_t

    n_late = len(late_shards)
    late_shapes = [t.shape for t in late_shards]

    def body(x_ref, xin_ref, g_ref, *refs):
        late_in, (out_ref, h1_ref), late_out = refs[:n_late], refs[n_late:n_late + 2], refs[n_late + 2:2 * n_late + 2]
        scratch = refs[2 * n_late + 2:]
        stage, land, xbuf = scratch[:3]
        late_stage, late_cast = scratch[3:3 + n_late], scratch[3 + n_late:3 + 2 * n_late]
        send_sems, recv_sems, load_sem, x_sems, late_sems, store_sems = scratch[3 + 2 * n_late:]
        x, y, c = _place()
        me, sibling = (x, y, c), (x, y, 1 - c)
        chips = [(1 - x, y), (x, 1 - y), (1 - x, 1 - y)]

        def slot(px, py, pc):
            return land.at[4 * px + 2 * py + pc]

        def copy(k, block, to):
            return pltpu.make_async_remote_copy(
                src_ref=slot(*block), dst_ref=slot(*block), send_sem=send_sems.at[k], recv_sem=recv_sems.at[k],
                device_id=to, device_id_type=MESH)

        load = pltpu.make_async_copy(x_ref, stage, load_sem)
        load.start()
        load.wait()
        land[4 * x + 2 * y + c] = stage[...].astype(BF16)
        sent = [copy(0, me, sibling)] + [copy(1 + j, me, (*chip, c)) for j, chip in enumerate(chips)]
        for cp in sent:
            cp.start()

        def x_load(i):
            return pltpu.make_async_copy(xin_ref.at[pl.ds(i * norm_t, norm_t)], xbuf.at[i % 2], x_sems.at[i % 2])

        late_loads = [pltpu.make_async_copy(late_in[i], late_stage[i], late_sems.at[i]) for i in range(n_late)]
        for ld in late_loads:
            ld.start()
        x_load(0).start()
        for i in range(n_chunks):
            if i + 1 < n_chunks:
                x_load(i + 1).start()
            x_load(i).wait()
            h1_ref[i * norm_t:(i + 1) * norm_t, :] = _rms_rows(xbuf[i % 2], g_ref[...]).astype(BF16)
        stores = []
        for i in range(n_late):
            late_loads[i].wait()
            late_cast[i][...] = late_stage[i][...].astype(BF16)
            stores.append(pltpu.make_async_copy(late_cast[i], late_out[i].at[4 * x + 2 * y + c], store_sems.at[i]))
            stores[-1].start()

        for j, chip in enumerate(chips):
            copy(1 + j, (*chip, c), me).wait_recv()
            sent.append(copy(4 + j, (*chip, c), sibling))
            sent[-1].start()
        copy(0, sibling, me).wait_recv()
        for j, chip in enumerate(chips):
            copy(4 + j, (*chip, 1 - c), me).wait_recv()
        for cp in sent:
            cp.wait_send()
        for st in stores:
            st.wait()

        row = lax.broadcasted_iota(jnp.int32, (PAD_BLK, D), 0)
        for k, runs in enumerate(blocks):
            out = jnp.zeros((PAD_BLK, D), F32)
            for o_lo, o_hi, p, a in runs:
                start = min(a // 16 * 16, W_SLAB - W_WIN)
                win = land[p, start:start + W_WIN, :].astype(F32)
                moved = pltpu.roll(win, (o_lo - (a - start)) % W_WIN, 0)[:PAD_BLK]
                out = jnp.where((row >= o_lo) & (row < o_hi), moved, out)
            out_ref[k * PAD_BLK:(k + 1) * PAD_BLK, :] = out.astype(BF16)

    return pl.pallas_call(
        body, name="all_gather_w_in",
        out_shape=[jax.ShapeDtypeStruct((D_PAD, D), BF16), jax.ShapeDtypeStruct((S, D), BF16)]
                  + [jax.ShapeDtypeStruct((N_DEV,) + sh, BF16) for sh in late_shapes],
        in_specs=[ANY, ANY, pl.BlockSpec(memory_space=pltpu.VMEM)] + [ANY] * n_late,
        out_specs=[pl.BlockSpec(memory_space=pltpu.VMEM)] * 2 + [ANY] * n_late,
        scratch_shapes=[pltpu.VMEM((W_SLAB, D), F32), pltpu.VMEM((N_DEV, W_SLAB, D), BF16),
                        pltpu.VMEM((2, norm_t, D), F32)]
                       + [pltpu.VMEM(sh, F32) for sh in late_shapes] + [pltpu.VMEM(sh, BF16) for sh in late_shapes]
                       + [pltpu.SemaphoreType.DMA((AG_COPIES,)), pltpu.SemaphoreType.DMA((AG_COPIES,)),
                          pltpu.SemaphoreType.DMA(()), pltpu.SemaphoreType.DMA((2,)),
                          pltpu.SemaphoreType.DMA((n_late,)), pltpu.SemaphoreType.DMA((n_late,))],
        compiler_params=_cparams(None),
    )(shard, x_in, g_attn, *late_shards)


PEER_COPIES = N_DEV - 1
SIBLING_COPIES = 4
CHIP_COPIES = 3


def _peer_copies(_, land_refs, send_sems, recv_sems):
    x, y, c = _place()
    mine = 4 * x + 2 * y + c
    copies = []
    for i, ref in enumerate(land_refs):
        for k in range(1, N_DEV):
            peer = (x ^ (k >> 2), y ^ ((k >> 1) & 1), c ^ (k & 1))
            n = i * PEER_COPIES + k - 1
            copies.append(pltpu.make_async_remote_copy(
                src_ref=ref.at[mine], dst_ref=ref.at[mine], send_sem=send_sems.at[n], recv_sem=recv_sems.at[n],
                device_id=peer, device_id_type=MESH))
    return copies


def _sibling_copies(g_refs, r_refs, send_sems, recv_sems):
    x, y, c = _place()
    return [pltpu.make_async_remote_copy(
        src_ref=g_refs[i].at[2 * k + (1 - c)], dst_ref=r_refs[i].at[k],
        send_sem=send_sems.at[SIBLING_COPIES * i + k], recv_sem=recv_sems.at[SIBLING_COPIES * i + k],
        device_id=(x, y, 1 - c), device_id_type=MESH) for i in range(len(g_refs)) for k in range(SIBLING_COPIES)]


def _sibling_half_copies(g_refs, r_refs, send_sems, recv_sems):
    x, y, c = _place()
    return [pltpu.make_async_remote_copy(
        src_ref=g_refs[i].at[k], dst_ref=r_refs[i].at[k],
        send_sem=send_sems.at[SIBLING_COPIES * i + k], recv_sem=recv_sems.at[SIBLING_COPIES * i + k],
        device_id=(x, y, 1 - c), device_id_type=MESH) for i in range(len(g_refs)) for k in range(SIBLING_COPIES)]


def _chip_copies(p_refs, r_refs, send_sems, recv_sems):
    x, y, c = _place()
    chips = [(1 - x, y), (x, 1 - y), (1 - x, 1 - y)]
    return [pltpu.make_async_remote_copy(
        src_ref=p_refs[i].at[2 * cx + cy], dst_ref=r_refs[i].at[j],
        send_sem=send_sems.at[CHIP_COPIES * i + j], recv_sem=recv_sems.at[CHIP_COPIES * i + j],
        device_id=(cx, cy, c), device_id_type=MESH) for i in range(len(p_refs)) for j, (cx, cy) in enumerate(chips)]


def _in_hbm(a):
    return pltpu.with_memory_space_constraint(a, pltpu.HBM)


def _exchange_start(name, copies_fn, n_copies, srcs, lands, after=()):
    n_s, n_l, n_a = len(srcs), len(lands), len(after)

    def body(*refs):
        outs = refs[n_s + n_l + n_a:]
        for cp in copies_fn(refs[:n_s], refs[n_s:n_s + n_l], outs[0], outs[1]):
            cp.start()
        outs[-1][...] = jnp.zeros_like(outs[-1])

    res = pl.pallas_call(
        body, name=name,
        out_shape=[pltpu.SemaphoreType.DMA((n_copies,)), pltpu.SemaphoreType.DMA((n_copies,))]
                  + [pltpu.HBM(t.shape, t.dtype) for t in (*srcs, *lands)] + [jax.ShapeDtypeStruct((8, LANES), F32)],
        in_specs=[HBM] * (n_s + n_l) + [ANY] * n_a,
        out_specs=[SEM, SEM] + [HBM] * (n_s + n_l) + [pl.BlockSpec(memory_space=pltpu.VMEM)],
        input_output_aliases={i: 2 + i for i in range(n_s + n_l)},
        compiler_params=pltpu.CompilerParams(has_side_effects=pltpu.SideEffectType.DATAFLOW_SIDE_EFFECTING),
    )(*[_in_hbm(t) for t in (*srcs, *lands)], *after)
    return res[0], res[1], list(res[2:2 + n_s]), list(res[2 + n_s:2 + n_s + n_l]), res[-1]


def _exchange_wait(name, copies_fn, started, after):
    send_sems, recv_sems, srcs, lands, _ = started
    n_s, n_l = len(srcs), len(lands)

    def body(*refs):
        for cp in copies_fn(refs[:n_s], refs[n_s:n_s + n_l], refs[n_s + n_l], refs[n_s + n_l + 1]):
            cp.wait_send()
            cp.wait_recv()

    res = pl.pallas_call(
        body, name=name,
        out_shape=[pltpu.HBM(t.shape, t.dtype) for t in (*srcs, *lands)],
        in_specs=[HBM] * (n_s + n_l) + [SEM, SEM, ANY],
        out_specs=[HBM] * (n_s + n_l),
        input_output_aliases={i: i for i in range(n_s + n_l)},
        compiler_params=pltpu.CompilerParams(has_side_effects=pltpu.SideEffectType.DATAFLOW_SIDE_EFFECTING),
    )(*srcs, *lands, send_sems, recv_sems, after)
    return list(res[:n_s]), list(res[n_s:])


def _shard_block(shape):
    rows, cols = shape
    if rows * cols <= 256 * 1024:
        return rows, cols
    if rows % 256:
        return rows, 256
    return 256, cols


def _chip_partial(ids, g_stack, recv1, name):
    shape = g_stack.shape[1:]
    br, bc = _shard_block(shape)
    whole = g_stack.shape[0] == N_DEV

    def body(ids_ref, g_ref, r_ref, pb_ref, own_ref):
        s = g_ref[...] + r_ref[...]
        pb_ref[...] = s.astype(BF16)

        @pl.when(pl.program_id(2) == ids_ref[1])
        def _():
            own_ref[...] = s

    grid_spec = pltpu.PrefetchScalarGridSpec(
        num_scalar_prefetch=1, grid=(shape[0] // br, shape[1] // bc, 4),
        in_specs=[pl.BlockSpec((None, br, bc), lambda r, q, k, ids: (2 * k + ids[0] if whole else k, r, q)),
                  pl.BlockSpec((None, br, bc), lambda r, q, k, ids: (k, r, q))],
        out_specs=[pl.BlockSpec((None, br, bc), lambda r, q, k, ids: (k, r, q)),
                   pl.BlockSpec((br, bc), lambda r, q, k, ids: (r, q))])
    return pl.pallas_call(
        body, name=name, grid_spec=grid_spec,
        out_shape=[jax.ShapeDtypeStruct((4,) + shape, BF16), jax.ShapeDtypeStruct(shape, F32)],
        compiler_params=_cparams(("parallel", "parallel", "arbitrary")),
    )(ids, g_stack, recv1)


def _adamw(w, g, m, v):
    m = ADAM_B1 * m + (1.0 - ADAM_B1) * g
    v = ADAM_B2 * v + (1.0 - ADAM_B2) * (g * g)
    m_hat = m / (1.0 - ADAM_B1 ** ADAM_STEP)
    v_hat = v / (1.0 - ADAM_B2 ** ADAM_STEP)
    delta = -ADAM_LR * (m_hat / (jnp.sqrt(v_hat) + ADAM_EPS) + ADAM_WD * w)
    return delta, m, v


def _reduce_adamw(own, recv2, w, m, v, name, deps=()):
    shape = own.shape
    br, bc = _shard_block(shape)

    def body(own_ref, r_ref, w_ref, m_ref, v_ref, *refs):
        g_ref, d_ref, nm_ref, nv_ref = refs[len(deps):]
        g = own_ref[...]
        for j in range(3):
            g = g + r_ref[j].astype(F32)
        delta, nm, nv = _adamw(w_ref[...], g, m_ref[...], v_ref[...])
        g_ref[...] = g
        d_ref[...] = delta
        nm_ref[...] = nm
        nv_ref[...] = nv

    blk = pl.BlockSpec((br, bc), lambda r, q: (r, q))
    return pl.pallas_call(
        body, name=name, grid=(shape[0] // br, shape[1] // bc),
        in_specs=[blk, pl.BlockSpec((3, br, bc), lambda r, q: (0, r, q)), blk, blk, blk] + [ANY] * len(deps),
        out_specs=[blk] * 4, out_shape=[jax.ShapeDtypeStruct(shape, F32)] * 4,
        compiler_params=_cparams(("parallel", "parallel")),
    )(own, recv2, w, m, v, *deps)


def _small_adamw(stack, w, m, v):
    def body(s_ref, w_ref, m_ref, v_ref, g_ref, *o_refs):
        g = s_ref[0]
        for s in range(1, N_DEV):
            g = g + s_ref[s]
        delta, nm, nv = _adamw(w_ref[...], g, m_ref[...], v_ref[...])
        g_ref[...] = g
        for q, t in enumerate((g, delta, nm, nv)):
            attn_ref, b_ref, mlp_ref, final_ref = o_refs[4 * q:4 * q + 4]
            for i in range(8):
                cols = slice(i * LANES, (i + 1) * LANES)
                attn_ref[:, cols] = t[i:i + 1]
                mlp_ref[:, cols] = t[8 + i:9 + i]
                final_ref[:, cols] = t[16 + i:17 + i]
            b_ref[...] = t[24:25, :FOX_H]

    vm = pl.BlockSpec(memory_space=pltpu.VMEM)
    unpacked = [jax.ShapeDtypeStruct((1, n), F32) for n in (D, FOX_H, D, D)]
    return pl.pallas_call(
        body, name="small_adamw",
        in_specs=[vm] * 4, out_specs=[vm] * 17,
        out_shape=[jax.ShapeDtypeStruct((SMALL_R, LANES), F32)] + unpacked * 4,
    )(stack, w, m, v)


def _cols_to_whole(stack):
    rows = stack.shape[1]
    return stack.transpose(1, 0, 2).reshape(rows, -1)


def _whole_to_cols(t):
    rows = t.shape[0]
    return t.reshape(rows, N_DEV, -1).transpose(1, 0, 2)


W_IN_SEGMENTS = ((0, 3 * FOX_W, SEG_A), (3 * FOX_W, 3 * FOX_W + FOX_H, SEG_F),
                 (3 * FOX_W + FOX_H, 3 * FOX_W + FOX_H + 3 * DIL_W, SEG_B), (3 * FOX_W + FOX_H + 3 * DIL_W, D_IN, SEG_G))


def _shard_pieces(p):
    rows = D_IN // N_DEV
    lo, hi = p * rows, (p + 1) * rows
    return [(max(lo, a) - lo, min(hi, b) - lo, pad + max(lo, a) - a)
            for a, b, pad in W_IN_SEGMENTS if max(lo, a) < min(hi, b)]


def _unpad_runs(p):
    rows = D_IN // N_DEV
    blocks = []
    for lo in range(0, rows, PAD_BLK):
        hi = min(lo + PAD_BLK, rows)
        blocks.append([(max(lo, a) - lo, min(hi, b) - lo, pad + max(lo, a) - a)
                       for a, b, pad in _shard_pieces(p) if max(lo, a) < min(hi, b)])
    return blocks


def _unpad_dw_in_t(dwp, ids, kept, deps=()):
    rows = D_IN // N_DEV
    plans = [_unpad_runs(p) for p in range(N_DEV)]
    windows = [[min(src // 8 * 8, D_PAD - W_WIN) for runs in plan for _, _, src in runs] for plan in plans]
    n_win = max(len(w) for w in windows)

    def body(ids_ref, src_ref, *refs):
        o_ref, buf, sems = refs[len(deps):]
        side = ids_ref[0] if kept else 1 - ids_ref[0]
        p = 2 * pl.program_id(0) + side
        row = lax.broadcasted_iota(jnp.int32, (PAD_BLK, D), 0)

        def fetch(q):
            slot = (q // 2) % 2
            return [pltpu.make_async_copy(src_ref.at[pl.ds(start, W_WIN)], buf.at[slot, n], sems.at[slot, n])
                    for n, start in enumerate(windows[q])]

        for q in range(N_DEV):
            @pl.when(p == q)
            def _(q=q):
                for nxt in ([q, q + 2] if q < 2 else [q + 2]):
                    if nxt < N_DEV:
                        for cp in fetch(nxt):
                            cp.start()
                copies, n = fetch(q), 0
                for j, runs in enumerate(plans[q]):
                    out = None
                    for o_lo, o_hi, src in runs:
                        copies[n].wait()
                        moved = pltpu.roll(buf[(q // 2) % 2, n], (o_lo - (src - windows[q][n])) % W_WIN, 0)[:PAD_BLK]
                        out = moved if out is None else jnp.where((row >= o_lo) & (row < o_hi), moved, out)
                        n += 1
                    size = min(PAD_BLK, rows - j * PAD_BLK)
                    o_ref[j * PAD_BLK:j * PAD_BLK + size, :] = out[:size]

    grid_spec = pltpu.PrefetchScalarGridSpec(
        num_scalar_prefetch=1, grid=(N_DEV // 2,),
        in_specs=[ANY] * (1 + len(deps)), out_specs=pl.BlockSpec((None, rows, D), lambda k, ids: (k, 0, 0)),
        scratch_shapes=[pltpu.VMEM((2, n_win, W_WIN, D), F32), pltpu.SemaphoreType.DMA((2, n_win))])
    return pl.pallas_call(
        body, name="unpad_dw_in_kept" if kept else "unpad_dw_in_sent", grid_spec=grid_spec,
        out_shape=jax.ShapeDtypeStruct((N_DEV // 2, rows, D), F32),
        compiler_params=_cparams(("arbitrary",)),
    )(ids, dwp, *deps)


LOSS_ROW = 25


def _pack_small(g_attn, b, g_mlp, g_final, loss_row=None):
    tail = jnp.pad(b, ((0, 7), (0, LANES - b.shape[1])))
    if loss_row is not None:
        tail = tail + jnp.pad(loss_row, ((LOSS_ROW - 24, 31 - LOSS_ROW), (0, 0)))
    return jnp.concatenate([g_attn.reshape(8, LANES), g_mlp.reshape(8, LANES), g_final.reshape(8, LANES), tail], axis=0)


class _StepComm:
    def __init__(self, ids, w_sh, m_sh, v_sh, lands, after_w_in):
        self.ids = ids
        self.w_sh, self.m_sh, self.v_sh = w_sh, m_sh, v_sh
        self.updates = None
        self.gather_mixer = _exchange_start("gather_mixer_start", _peer_copies, PEER_COPIES * 3, [], lands[:3],
                                            after=(after_w_in,))
        self.gather_mlp = _exchange_start("gather_mlp_start", _peer_copies, PEER_COPIES * 2, [], lands[3:],
                                          after=(self.gather_mixer[-1],))
        self.sibling = {}
        self.chips = {}
        self.own = {}
        self.names = {}
        self.copies_fn = {}
        self.kept = {}

    def _reduce_start(self, group, names, grads, after=(), copies_fn=_sibling_copies):
        lands = [lax.empty((SIBLING_COPIES,) + t.shape[1:], F32) for t in grads]
        self.sibling[group] = _exchange_start("grad_%s_sibling_start" % group, copies_fn,
                                              SIBLING_COPIES * len(grads), grads, lands, after=after)
        self.names[group] = names
        self.copies_fn[group] = copies_fn
        return self.sibling[group][-1]

    def _reduce_mid(self, group, after):
        grads, recv1 = _exchange_wait("grad_%s_sibling_wait" % group, self.copies_fn[group], self.sibling[group],
                                      after)
        grads = self.kept.get(group, grads)
        parts = [_chip_partial(self.ids, g, r, "grad_partial_" + n) for g, r, n in zip(grads, recv1, self.names[group])]
        self.own[group] = [p[1] for p in parts]
        srcs = [p[0] for p in parts]
        lands = [lax.empty((CHIP_COPIES,) + t.shape[1:], BF16) for t in srcs]
        self.chips[group] = _exchange_start("grad_%s_chips_start" % group, _chip_copies, CHIP_COPIES * len(srcs),
                                            srcs, lands)
        return self.chips[group][-1]

    def reduced(self, group, after):
        _, recv2 = _exchange_wait("grad_%s_chips_wait" % group, _chip_copies, self.chips[group], after)
        return list(zip(self.own[group], recv2))

    def first_deps(self):
        return [self.gather_mlp[-1]]

    def mixer_weights(self, after):
        _, (g_a, g_b, g_out) = _exchange_wait("gather_mixer_wait", _peer_copies, self.gather_mixer, after)
        return _cols_to_whole(g_a), _cols_to_whole(g_b), g_out.reshape(D, D)

    def mlp_weights(self, after):
        _, (g_up, g_down) = _exchange_wait("gather_mlp_wait", _peer_copies, self.gather_mlp, after)
        return g_up, g_down.reshape(DFF, D)

    def mlp_grads(self, dw_up_sh, dw_down):
        return [self._reduce_start("mlp", W_NAMES[4:], [dw_up_sh, dw_down.reshape((N_DEV,) + SHARD_SHAPES[5])])]

    def mixer_grads(self, dw_a, dw_b, dw_out):
        token = self._reduce_mid("mlp", dw_b)
        grads = [_whole_to_cols(dw_a), _whole_to_cols(dw_b), dw_out.reshape((N_DEV,) + SHARD_SHAPES[3])]
        return [self._reduce_start("mixer", W_NAMES[1:4], grads, after=(token,))]

    def mid_backward(self, after):
        return [self._reduce_mid("mixer", after)]

    def w_in_grad(self, dw_in_t):
        token = self._reduce_start("w_in", W_NAMES[:1], [_unpad_dw_in_t(dw_in_t, self.ids, kept=False)],
                                   copies_fn=_sibling_half_copies)
        self.kept["w_in"] = [_unpad_dw_in_t(dw_in_t, self.ids, kept=True, deps=[token])]
        reduced = self.reduced("mixer", token) + self.reduced("mlp", token)
        self.updates, last = [None] * len(reduced), self.kept["w_in"][0]
        for i in (3, 4, 0, 1, 2):
            if i == 4:
                last = self._reduce_mid("w_in", last)
            self.updates[i] = _reduce_adamw(*reduced[i], self.w_sh[1 + i], self.m_sh[1 + i], self.v_sh[1 + i],
                                            "adamw_" + W_NAMES[1 + i], deps=[last])
            last = self.updates[i][0]
        return [last]

    def w_in_update(self, after):
        (reduced,) = self.reduced("w_in", after)
        return _reduce_adamw(*reduced, self.w_sh[0], self.m_sh[0], self.v_sh[0], "adamw_" + W_NAMES[0])


def kernel(x, norm_attn_g, w_in, b_forget, w_branch_a, w_branch_b, w_out, norm_mlp_g, w_up, w_down, norm_final_g, loss_target, m_norm_attn_g, m_w_in, m_b_forget, m_w_branch_a, m_w_branch_b, m_w_out, m_norm_mlp_g, m_w_up, m_w_down, m_norm_final_g, v_norm_attn_g, v_w_in, v_b_forget, v_w_branch_a, v_w_branch_b, v_w_out, v_norm_mlp_g, v_w_up, v_w_down, v_norm_final_g):
    cx, cy, cc = _place()
    ids = jnp.stack([cc, 2 * cx + cy]).astype(jnp.int32)

    w_sh = [w_in[0].T] + [t[0] for t in (w_branch_a, w_branch_b, w_out, w_up, w_down)]
    m_sh = [m_w_in[0].T] + [t[0] for t in (m_w_branch_a, m_w_branch_b, m_w_out, m_w_up, m_w_down)]
    v_sh = [v_w_in[0].T] + [t[0] for t in (v_w_branch_a, v_w_branch_b, v_w_out, v_w_up, v_w_down)]

    w_in_t, h1, *lands = _gather_w_in(jnp.pad(w_sh[0], ((0, W_SLAB - w_sh[0].shape[0]), (0, 0))), x[0], norm_attn_g,
                                      w_sh[1:])
    comm = _StepComm(ids, w_sh, m_sh, v_sh, lands, w_in_t)
    b_pad = jnp.pad(b_forget, ((0, 0), (0, LANES - FOX_H)))

    loss_row, dx, dsmall = _local_step(
        x[0], loss_target[0], h1, norm_attn_g, norm_mlp_g, norm_final_g.reshape(1, D), b_pad, w_in_t, comm)

    dg_attn, db, dg_mlp, dg_final = dsmall
    stack = lax.dynamic_update_slice(jnp.zeros((N_DEV, SMALL_R, LANES), F32),
                                     _pack_small(dg_attn, db, dg_mlp, dg_final, loss_row)[None],
                                     (4 * cx + 2 * cy + cc, 0, 0))
    small_gather = _exchange_start("small_gather_start", _peer_copies, PEER_COPIES, [], [stack])
    w_in_update = comm.w_in_update(small_gather[-1])
    _, (stack,) = _exchange_wait("small_gather_wait", _peer_copies, small_gather, w_in_update[0])
    pack, *small = _small_adamw(
        stack,
        _pack_small(norm_attn_g, b_forget, norm_mlp_g, norm_final_g.reshape(1, D)),
        _pack_small(m_norm_attn_g, m_b_forget, m_norm_mlp_g, m_norm_final_g.reshape(1, D)),
        _pack_small(v_norm_attn_g, v_b_forget, v_norm_mlp_g, v_norm_final_g.reshape(1, D)))
    big = [w_in_update] + comm.updates

    outs = [pack[LOSS_ROW, 0], dx[None]]
    for q in range(4):
        s_attn, s_b, s_mlp, s_final = small[4 * q:4 * q + 4]
        s_final = s_final.reshape(D)
        b_in, b_a, b_b, b_out, b_up, b_down = [t[q][None] for t in big]
        b_in = jnp.swapaxes(b_in, 1, 2)
        outs += [s_attn, b_in, s_b, b_a, b_b, b_out, s_mlp, b_up, b_down, s_final]
    return tuple(outs)
```

```python
import functools

import jax
import jax.numpy as jnp
from jax import lax
from jax.experimental import pallas as pl
from jax.experimental.pallas import tpu as pltpu

F32 = jnp.float32
BF16 = jnp.bfloat16
MESH = pl.DeviceIdType.MESH

S = 2048
D = 1024
HD = 64
FOX_H = 8
FOX_W = FOX_H * HD
DIL_HG = 4
DIL_G = 3
DIL_W = DIL_G * DIL_HG * HD
DIL_OUT = DIL_HG * HD
DIL_BLK = 128
DIL_R = (1, 4, 16)
DFF = 4 * D
D_IN = 3 * FOX_W + FOX_H + 3 * DIL_W + 2 * D
EPS = 1e-6
NEG_INF = -1e30
SCALE = HD ** -0.5
ROPE_THETA = 500000.0
ROPE_DIM = HD // 4
N_DEV = 8

ADAM_LR = 0.001
ADAM_B1 = 0.9
ADAM_B2 = 0.999
ADAM_EPS = 1e-08
ADAM_WD = 0.01
ADAM_STEP = 10

LANES = 128
CB = 256
SEG_A = 0
SEG_B = 3 * FOX_W
SEG_Z = SEG_B + 3 * DIL_W
SEG_G = SEG_Z + CB
SEG_F = SEG_G + 2 * D
D_PAD = SEG_F + CB
DW_IN_TM = D_PAD // 10
SMALL_R = 32

VMEM_MB = 56


def _cparams(dims=None, vmem_mb=VMEM_MB, **kw):
    return pltpu.CompilerParams(dimension_semantics=dims, vmem_limit_bytes=vmem_mb << 20, **kw)


ANY = pl.BlockSpec(memory_space=pl.ANY)

_NN = (((1,), (0,)), ((), ()))
_NT = (((1,), (1,)), ((), ()))
_TN = (((0,), (0,)), ((), ()))


def _dot(a, b, dims):
    return lax.dot_general(a.astype(BF16), b.astype(BF16), dims, preferred_element_type=F32)


def _mm(a, b, *, mode, tm, tn, out_dtypes, name, n=None, k=None, a_off=0, b_off=0,
        b_sharded=False, out_sharded=False, epilogue=None, extras=(), deps=()):
    n_sh = b.shape[-1] if b_sharded else None
    if mode == "nn":
        m = a.shape[0]
        k = k or a.shape[1]
        a_spec = pl.BlockSpec((tm, k), lambda i, j: (i, a_off))
        if b_sharded:
            assert tn == n_sh
            n = N_DEV * n_sh
            b_spec = pl.BlockSpec((None, k, tn), lambda i, j: (j, 0, 0))
        else:
            n = n or b.shape[1]
            b_spec = pl.BlockSpec((k, tn), lambda i, j: (0, j + b_off))
        dims = _NN
    elif mode == "nt":
        m = a.shape[0]
        k = k or a.shape[1]
        a_spec = pl.BlockSpec((tm, k), lambda i, j: (i, a_off))
        if b_sharded:
            n = b.shape[1]
            b_spec = pl.BlockSpec((N_DEV, tn, n_sh), lambda i, j: (0, j, 0))
        else:
            n = n or b.shape[0]
            b_spec = pl.BlockSpec((tn, k), lambda i, j: (j + b_off, 0))
        dims = _NT
    else:
        k, m = a.shape
        n = n or b.shape[1]
        a_spec = pl.BlockSpec((k, tm), lambda i, j: (0, i))
        b_spec = pl.BlockSpec((k, tn), lambda i, j: (0, j + b_off))
        dims = _TN
    assert m % tm == 0 and n % tn == 0, (name, m, n, tm, tn)
    n_extra = len(extras)
    tile = pl.BlockSpec((tm, tn), lambda i, j: (i, j))
    split_tile = out_sharded and n == tn
    if out_sharded:
        assert mode == "tn" and n // tn in (1, N_DEV)
        if split_tile:
            out_spec = pl.BlockSpec((N_DEV, tm, n // N_DEV), lambda i, j: (0, i, 0))
        else:
            out_spec = pl.BlockSpec((None, tm, tn), lambda i, j: (j, i, 0))
        out_shape = (N_DEV, m, n // N_DEV)
    else:
        out_spec, out_shape = tile, (m, n)

    def body(a_ref, b_ref, *refs):
        if mode == "nt" and b_sharded:
            acc = _dot(a_ref[...], jnp.concatenate([b_ref[p] for p in range(N_DEV)], axis=1), dims)
        else:
            acc = _dot(a_ref[...], b_ref[...], dims)
        ex = [r[...] for r in refs[:n_extra]]
        outs = epilogue(acc, *ex) if epilogue is not None else (acc,)
        for o_ref, o in zip(refs[n_extra + len(deps):], outs):
            if split_tile:
                w = n // N_DEV
                for p in range(N_DEV):
                    o_ref[p] = o[:, p * w:(p + 1) * w].astype(o_ref.dtype)
            else:
                o_ref[...] = o.astype(o_ref.dtype)

    res = pl.pallas_call(
        body, name=name, grid=(m // tm, n // tn),
        in_specs=[a_spec, b_spec] + [tile] * n_extra + [ANY] * len(deps),
        out_specs=[out_spec] * len(out_dtypes),
        out_shape=[jax.ShapeDtypeStruct(out_shape, dt) for dt in out_dtypes],
        compiler_params=_cparams(("parallel", "parallel")),
    )(a, b, *extras, *deps)
    return res if len(out_dtypes) > 1 else res[0]


def _mm_rows(a, b, *, tm, name, epilogue, rows=(), vecs=(), row_out=(), vec_out=(), b_sharded=False, deps=()):
    m, k = a.shape
    n_rows, n_vecs, n_deps = len(rows), len(vecs), len(deps)
    n_sh = b.shape[-1] if b_sharded else None

    def body(a_ref, b_ref, *refs):
        row_refs, vec_refs = refs[:n_rows], refs[n_rows:n_rows + n_vecs]
        outs = refs[n_rows + n_vecs + n_deps:]
        if b_sharded:
            acc = _dot(a_ref[:, 0:n_sh], b_ref[0], _NT)
            for p in range(1, N_DEV):
                acc = acc + _dot(a_ref[:, p * n_sh:(p + 1) * n_sh], b_ref[p], _NT)
        else:
            acc = _dot(a_ref[...], b_ref[...], _NN)
        row_vals, vec_incs = epilogue(acc, [r[...] for r in row_refs], [v[...] for v in vec_refs])
        for o_ref, val in zip(outs[:len(row_out)], row_vals):
            o_ref[...] = val.astype(o_ref.dtype)

        @pl.when(pl.program_id(0) == 0)
        def _():
            for o_ref in outs[len(row_out):]:
                o_ref[...] = jnp.zeros_like(o_ref)

        for o_ref, inc in zip(outs[len(row_out):], vec_incs):
            o_ref[...] += inc

    tile = pl.BlockSpec((tm, D), lambda i: (i, 0))
    b_spec = pl.BlockSpec(b.shape, lambda i: (0,) * b.ndim)
    return pl.pallas_call(
        body, name=name, grid=(m // tm,),
        in_specs=[pl.BlockSpec((tm, k), lambda i: (i, 0)), b_spec] + [tile] * n_rows
                 + [pl.BlockSpec(v.shape, lambda i: (0, 0)) for v in vecs] + [ANY] * n_deps,
        out_specs=[tile] * len(row_out) + [pl.BlockSpec((1, w), lambda i: (0, 0)) for w in vec_out],
        out_shape=[jax.ShapeDtypeStruct((m, D), dt) for dt in row_out]
                  + [jax.ShapeDtypeStruct((1, w), F32) for w in vec_out],
        compiler_params=_cparams(("arbitrary",)),
    )(a, b, *rows, *vecs, *deps)


ROW_T = 256
ROWS_TM = 512


def _rms_rows(x, g):
    r = lax.rsqrt(jnp.mean(x * x, axis=-1, keepdims=True) + EPS)
    return (x * r) * g


def _rms_bwd_rows(dh, x, dres, g):
    r = lax.rsqrt(jnp.mean(x * x, axis=-1, keepdims=True) + EPS)
    xn = x * r
    dhn = dh * g
    dx = dres + r * (dhn - xn * jnp.mean(dhn * xn, axis=-1, keepdims=True))
    return dx, jnp.sum(dh * xn, axis=0, keepdims=True)


def _final_loss_rows(x, g, tgt):
    r = lax.rsqrt(jnp.mean(x * x, axis=-1, keepdims=True) + EPS)
    xn = x * r
    err = xn * g - tgt
    row_loss = jnp.mean(err * err, axis=-1, keepdims=True)
    loss = 0.5 * jnp.sum(row_loss, axis=0, keepdims=True) * jnp.ones((1, LANES), F32)
    dy = err * (1.0 / D)
    dyn = dy * g
    dx = r * (dyn - xn * jnp.mean(dyn * xn, axis=-1, keepdims=True))
    return dx, jnp.sum(dy * xn, axis=0, keepdims=True), loss


def _sigmoid(z):
    return 1.0 / (1.0 + jnp.exp(-z))


GATE_TR = 512


def _mixer_tail(oa, ob, w_a, w_b, w_out, proj_g, x, g_mlp):
    def body(oa_ref, ob_ref, wa_ref, wb_ref, wo_ref, g_ref, x_ref, gm_ref, ya_ref, yb_ref, mixed_ref, x2_ref, h2_ref):
        ya = _dot(oa_ref[...], jnp.concatenate([wa_ref[p] for p in range(N_DEV)], axis=1), _NN)
        yb = _dot(ob_ref[...], jnp.concatenate([wb_ref[p] for p in range(N_DEV)], axis=1), _NN)
        ya_ref[...] = ya
        yb_ref[...] = yb
        mixed = (_sigmoid(g_ref[:, :D]) * ya + _sigmoid(g_ref[:, D:]) * yb).astype(BF16)
        mixed_ref[...] = mixed
        x2 = x_ref[...] + _dot(mixed, wo_ref[...], _NN)
        x2_ref[...] = x2
        h2_ref[...] = _rms_rows(x2, gm_ref[...]).astype(BF16)

    def rows(width):
        return pl.BlockSpec((GATE_TR, width), lambda i: (i, 0))

    def whole(t):
        return pl.BlockSpec(t.shape, lambda i: (0,) * t.ndim)

    return pl.pallas_call(
        body, name="mixer_tail", grid=(S // GATE_TR,),
        in_specs=[rows(FOX_W), rows(DIL_OUT), whole(w_a), whole(w_b), whole(w_out), rows(2 * D), rows(D), whole(g_mlp)],
        out_specs=[rows(D)] * 5,
        out_shape=[jax.ShapeDtypeStruct((S, D), dt) for dt in (F32, F32, BF16, F32, BF16)],
        compiler_params=_cparams(("parallel",)),
    )(oa, ob, w_a, w_b, w_out, proj_g, x, g_mlp)


def _out_proj_gate_bwd(dx2, w_out, proj_g, ya, yb):
    def body(dx_ref, w_ref, g_ref, ya_ref, yb_ref, dy_ref, dg_ref):
        dm = _dot(dx_ref[...], w_ref[...], _NT)
        for half, y_ref in enumerate((ya_ref, yb_ref)):
            cols = slice(half * D, (half + 1) * D)
            s = _sigmoid(g_ref[:, cols])
            dy_ref[:, cols] = (dm * s).astype(BF16)
            dg_ref[:, cols] = (dm * y_ref[...] * (s * (1.0 - s))).astype(BF16)

    row = pl.BlockSpec((GATE_TR, D), lambda i: (i, 0))
    wide = pl.BlockSpec((GATE_TR, 2 * D), lambda i: (i, 0))
    return pl.pallas_call(
        body, name="out_proj_gate_bwd", grid=(S // GATE_TR,),
        in_specs=[row, pl.BlockSpec((D, D), lambda i: (0, 0)), wide, row, row],
        out_specs=[wide, pl.BlockSpec((GATE_TR, 2 * D), lambda i: (i, SEG_G // (2 * D)))],
        out_shape=[jax.ShapeDtypeStruct((S, 2 * D), BF16), jax.ShapeDtypeStruct((S, D_PAD), BF16)],
        compiler_params=_cparams(("parallel",)),
    )(dx2, w_out, proj_g, ya, yb)


FOX_TQ = 512
FOX_TQ_FWD = 1024


def _scan_rows(x, reverse):
    n = x.shape[0]
    row = lax.broadcasted_iota(jnp.int32, x.shape, 0)
    k = 1
    while k < n:
        if reverse:
            x = x + jnp.where(row < n - k, pltpu.roll(x, n - k, 0), 0.0)
        else:
            x = x + jnp.where(row >= k, pltpu.roll(x, k, 0), 0.0)
        k *= 2
    return x


def _fox_dscan(dft, dfs, proj_f, b_pad, dproj):
    def body(dft_ref, dfs_ref, f_ref, b_ref, _, dfa_ref, db_ref):
        dfs_pad = jnp.concatenate([dfs_ref[...], jnp.zeros((LANES - FOX_H, S), F32)], axis=0)
        df = dfs_pad.T + dft_ref[0]
        for hp in range(1, dft_ref.shape[0]):
            df = df + pltpu.roll(dft_ref[hp], 2 * hp, 1)
        dlf = _scan_rows(df, reverse=True)
        z = f_ref[...] + b_ref[...]
        lane = lax.broadcasted_iota(jnp.int32, (S, LANES), 1)
        dfa = jnp.where(lane < FOX_H, dlf / (1.0 + jnp.exp(z)), 0.0)
        dfa_ref[:, :LANES] = dfa.astype(BF16)
        dfa_ref[:, LANES:] = jnp.zeros((S, CB - LANES), BF16)
        db_ref[...] = jnp.sum(dfa, axis=0, keepdims=True)

    return pl.pallas_call(
        body, name="fox_dscan", grid=(1,),
        in_specs=[pl.BlockSpec(dft.shape, lambda i: (0, 0, 0)), pl.BlockSpec((FOX_H, S), lambda i: (0, 0)),
                  pl.BlockSpec((S, LANES), lambda i: (0, 0)), pl.BlockSpec((1, LANES), lambda i: (0, 0)), ANY],
        out_specs=[pl.BlockSpec((S, CB), lambda i: (0, SEG_F // CB)), pl.BlockSpec((1, LANES), lambda i: (0, 0))],
        out_shape=[jax.ShapeDtypeStruct((S, D_PAD), BF16), jax.ShapeDtypeStruct((1, LANES), F32)],
        input_output_aliases={4: 0},
        compiler_params=_cparams(("arbitrary",)),
    )(dft, dfs, proj_f, b_pad, dproj)


FOX_NQ = S // FOX_TQ
FOX_HP = FOX_W // LANES
HEADS_PER_LB = LANES // HD
FOX_AUG = 2 * LANES


def _fox_prepare(proj_a, proj_f, b_pad):
    tr = 512

    def body(q_ref, k_ref, f_ref, b_ref, qa_ref, ka_ref, hi_s, mid_s, lo_s):
        i = pl.program_id(0)

        @pl.when(i == 0)
        def _():
            z = f_ref[...] + b_ref[...]
            lf = jnp.minimum(z, 0.0) - jnp.log1p(jnp.exp(-jnp.abs(z)))
            f = _scan_rows(lf, reverse=False)
            hi = f.astype(BF16).astype(F32)
            r1 = f - hi
            mid = r1.astype(BF16).astype(F32)
            hi_s[...] = hi
            mid_s[...] = mid
            lo_s[...] = (r1 - mid).astype(BF16).astype(F32)

        rows = pl.ds(pl.multiple_of(i * tr, tr), tr)
        hi, mid, lo = hi_s[rows, :], mid_s[rows, :], lo_s[rows, :]
        lane = lax.broadcasted_iota(jnp.int32, (tr, HD), 1)
        q_ones = jnp.where((lane >= 3) & (lane < 6), 1.0, 0.0)
        k_ones = jnp.where(lane < 3, 1.0, 0.0)
        for h in range(FOX_H):
            a1, a2, a3 = hi[:, h:h + 1], mid[:, h:h + 1], lo[:, h:h + 1]
            q_extra = jnp.where(lane == 0, a1, jnp.where(lane == 1, a2, jnp.where(lane == 2, a3, q_ones)))
            k_extra = jnp.where(lane == 3, -a1, jnp.where(lane == 4, -a2, jnp.where(lane == 5, -a3, k_ones)))
            base = h * LANES
            qa_ref[:, base:base + HD] = q_ref[:, h * HD:(h + 1) * HD] * jnp.asarray(SCALE, BF16)
            qa_ref[:, base + HD:base + LANES] = q_extra.astype(BF16)
            ka_ref[:, base:base + HD] = k_ref[:, h * HD:(h + 1) * HD]
            ka_ref[:, base + HD:base + LANES] = k_extra.astype(BF16)

    out_blk = pl.BlockSpec((tr, FOX_H * LANES), lambda i: (i, 0))
    return pl.pallas_call(
        body, name="fox_prepare", grid=(S // tr,),
        in_specs=[pl.BlockSpec((tr, FOX_W), lambda i: (i, 0)), pl.BlockSpec((tr, FOX_W), lambda i: (i, 1)),
                  pl.BlockSpec((S, LANES), lambda i: (0, 0)), pl.BlockSpec((1, LANES), lambda i: (0, 0))],
        out_specs=[out_blk, out_blk],
        out_shape=[jax.ShapeDtypeStruct((S, FOX_H * LANES), BF16)] * 2,
        scratch_shapes=[pltpu.VMEM((S, LANES), F32)] * 3,
        compiler_params=_cparams(("arbitrary",)),
    )(proj_a, proj_a, proj_f, b_pad)


def _fox_scores(qa, ka, w, tq):
    s = _dot(qa, ka, _NT)
    row = lax.broadcasted_iota(jnp.int32, (tq, tq), 0)
    col = lax.broadcasted_iota(jnp.int32, (tq, tq), 1)
    diag = jnp.where(col <= row, s[:, w * tq:], NEG_INF)
    return diag if w == 0 else jnp.concatenate([s[:, :w * tq], diag], axis=1)


def _fox_fwd(q_aug, k_aug, proj_a):
    tq = FOX_TQ_FWD

    def body(qa_ref, ka_ref, v_ref, o_ref, lse_ref):
        qi = pl.program_id(1)
        lane = lax.broadcasted_iota(jnp.int32, (tq, LANES), 1)
        for w in range(S // tq):
            @pl.when(qi == w)
            def _(w=w):
                width = (w + 1) * tq
                lse = jnp.zeros((tq, LANES), F32)
                for hh in range(HEADS_PER_LB):
                    aug = slice(hh * LANES, (hh + 1) * LANES)
                    sl = slice(hh * HD, (hh + 1) * HD)
                    s = _fox_scores(qa_ref[:, aug], ka_ref[:width, aug], w, tq)
                    m = jnp.max(s, axis=-1, keepdims=True)
                    p = jnp.exp(s - m)
                    l = jnp.sum(p, axis=-1, keepdims=True)
                    o_ref[:, sl] = _dot(p / l, v_ref[:width, sl], _NN)
                    lse = jnp.where(lane == hh, m + jnp.log(l), lse)
                lse_ref[...] = lse

    return pl.pallas_call(
        body, name="fox_fwd", grid=(FOX_HP, S // tq),
        in_specs=[pl.BlockSpec((tq, FOX_AUG), lambda h, i: (i, h)),
                  pl.BlockSpec((S, FOX_AUG), lambda h, i: (0, h)),
                  pl.BlockSpec((S, LANES), lambda h, i: (0, 2 * FOX_HP + h))],
        out_specs=[pl.BlockSpec((tq, LANES), lambda h, i: (i, h)),
                   pl.BlockSpec((None, tq, LANES), lambda h, i: (h, i, 0))],
        out_shape=[jax.ShapeDtypeStruct((S, FOX_W), F32), jax.ShapeDtypeStruct((FOX_HP, S, LANES), F32)],
        compiler_params=_cparams(("parallel", "parallel")),
    )(q_aug, k_aug, proj_a)


def _fox_bwd(q_aug, k_aug, proj_a, lse, do, dproj, deps=()):
    n_q = FOX_NQ

    def body(qa_ref, ka_ref, v_ref, lse_ref, do_ref, *refs):
        dproj_ref, dft_ref, dfs_ref, dk_acc, dv_acc, dq_buf, kv_buf, dq_sems, kv_sems = refs[1 + len(deps):]
        hp = pl.program_id(0)
        t = pl.program_id(1)
        slot = t % 2
        col = pl.multiple_of(hp * LANES, LANES)

        def dq_copy(step, buf_slot):
            rows = pl.ds(pl.multiple_of(step * FOX_TQ, FOX_TQ), FOX_TQ)
            return pltpu.make_async_copy(dq_buf.at[buf_slot], dproj_ref.at[rows, pl.ds(col, LANES)], dq_sems.at[buf_slot])

        @pl.when(t == 0)
        def _():
            dk_acc[...] = jnp.zeros_like(dk_acc)
            dv_acc[...] = jnp.zeros_like(dv_acc)

        @pl.when((t == 0) & (hp == 0))
        def _():
            dfs_ref[...] = jnp.zeros_like(dfs_ref)

        @pl.when(t >= 2)
        def _():
            dq_copy(t - 2, slot).wait()

        lane = lax.broadcasted_iota(jnp.int32, (FOX_TQ, LANES), 1)
        for w in range(n_q):
            @pl.when(t == w)
            def _(w=w):
                width = (w + 1) * FOX_TQ
                lse_all = lse_ref[...]
                sub = lax.broadcasted_iota(jnp.int32, (FOX_H, width), 0)
                dft = jnp.zeros((FOX_TQ, LANES), F32)
                for hh in range(HEADS_PER_LB):
                    aug = slice(hh * LANES, (hh + 1) * LANES)
                    sl = slice(hh * HD, (hh + 1) * HD)
                    head = hp * HEADS_PER_LB + hh
                    qa = qa_ref[:, aug]
                    ka = ka_ref[:width, aug]
                    do_h = do_ref[:, sl]
                    s = _fox_scores(qa, ka, w, FOX_TQ)
                    p = jnp.exp(s - lse_all[:, hh:hh + 1])
                    dp = _dot(do_h, v_ref[:width, sl], _NT)
                    ds = p * (dp - jnp.sum(dp * p, axis=-1, keepdims=True))
                    dft = jnp.where(lane == hh, jnp.sum(ds, axis=-1, keepdims=True), dft)
                    dfs_ref[:, :width] -= jnp.where(sub == head, jnp.sum(ds, axis=0, keepdims=True), 0.0)
                    dq_buf[slot, :, sl] = (_dot(ds, ka[:, :HD], _NN) * SCALE).astype(BF16)
                    dk_acc[:width, sl] += _dot(ds, qa[:, :HD], _TN)
                    dv_acc[:width, sl] += _dot(p, do_h, _TN)
                dft_ref[...] = dft

        dq_copy(t, slot).start()

        @pl.when(t == n_q - 1)
        def _():
            dq_copy(t - 1, 1 - slot).wait()
            dq_copy(t, slot).wait()
            kv_buf[0] = dk_acc[...].astype(BF16)
            kv_buf[1] = dv_acc[...].astype(BF16)
            copies = [pltpu.make_async_copy(
                kv_buf.at[j], dproj_ref.at[:, pl.ds(pl.multiple_of((j + 1) * FOX_W + hp * LANES, LANES), LANES)],
                kv_sems.at[j]) for j in range(2)]
            for cp in copies:
                cp.start()
            for cp in copies:
                cp.wait()

    qblk = pl.BlockSpec((FOX_TQ, FOX_AUG), lambda h, t: (t, h))
    lane_blk = pl.BlockSpec((None, FOX_TQ, LANES), lambda h, t: (h, t, 0))
    return pl.pallas_call(
        body, name="fox_bwd", grid=(FOX_HP, n_q),
        in_specs=[qblk, pl.BlockSpec((S, FOX_AUG), lambda h, t: (0, h)),
                  pl.BlockSpec((S, LANES), lambda h, t: (0, 2 * FOX_HP + h)),
                  lane_blk, pl.BlockSpec((FOX_TQ, LANES), lambda h, t: (t, h)), ANY] + [ANY] * len(deps),
        out_specs=[ANY, lane_blk, pl.BlockSpec((FOX_H, S), lambda h, t: (0, 0))],
        out_shape=[jax.ShapeDtypeStruct((S, D_PAD), BF16),
                   jax.ShapeDtypeStruct((FOX_HP, S, LANES), F32), jax.ShapeDtypeStruct((FOX_H, S), F32)],
        scratch_shapes=[pltpu.VMEM((S, LANES), F32), pltpu.VMEM((S, LANES), F32),
                        pltpu.VMEM((2, FOX_TQ, LANES), BF16), pltpu.VMEM((2, S, LANES), BF16),
                        pltpu.SemaphoreType.DMA((2,)), pltpu.SemaphoreType.DMA((2,))],
        input_output_aliases={5: 0},
        compiler_params=_cparams(("arbitrary", "arbitrary")),
    )(q_aug, k_aug, proj_a, lse, do, dproj, *deps)


def _rope_tables():
    half = ROPE_DIM // 2
    shape = (DIL_G, S, half)
    g = lax.broadcasted_iota(jnp.int32, shape, 0)
    row = lax.broadcasted_iota(jnp.int32, shape, 1)
    r = jnp.left_shift(1, 2 * g)
    per_class = S // r
    pos = (row % per_class) * r + row // per_class
    inv_freq = jnp.power(jnp.float32(ROPE_THETA), -lax.broadcasted_iota(F32, shape, 2) * 2.0 / ROPE_DIM)
    ang = pos.astype(F32) * inv_freq
    return jnp.concatenate([jnp.cos(ang), jnp.sin(ang)], axis=-1)


def _rope_factors(cs):
    half = ROPE_DIM // 2
    i = lax.broadcasted_iota(jnp.int32, (2 * half, 2 * LANES), 0)
    l = lax.broadcasted_iota(jnp.int32, (2 * half, 2 * LANES), 1)
    hit = ((l < LANES) & (i == l % half)) | ((l >= LANES) & (i - half == l % half))
    spread = jnp.where(hit, 1.0, 0.0).astype(BF16)
    hi = cs.astype(BF16)
    r1 = cs - hi.astype(F32)
    mid = r1.astype(BF16)
    lo = (r1 - mid.astype(F32)).astype(BF16)
    full = (_dot(hi, spread, _NN) + _dot(mid, spread, _NN)) + _dot(lo, spread, _NN)
    cos, sin = full[:, :LANES], full[:, LANES:]
    lane = lax.broadcasted_iota(jnp.int32, cos.shape, 1) % HD
    t_self = jnp.where(lane < ROPE_DIM, cos, 1.0)
    t_up = jnp.where(lane < half, -sin, 0.0)
    t_dn = jnp.where((lane >= half) & (lane < ROPE_DIM), sin, 0.0)
    return t_self, t_up, t_dn


def _residue_pieces(r):
    if r == 1:
        return [(slice(i, i + 512), slice(i, i + 512)) for i in range(0, S, 512)]
    n = S // r
    return [(pl.ds(j, n, stride=r), slice(j * n, (j + 1) * n)) for j in range(r)]


def _rope_apply(x, a, u, d, backward):
    half = ROPE_DIM // 2
    if backward:
        return x * a + pltpu.roll(x * u, half, 1) + pltpu.roll(x * d, LANES - half, 1)
    return x * a + pltpu.roll(x, LANES - half, 1) * u + pltpu.roll(x, half, 1) * d


def _rope_cache_factors(cs_ref, fac, first_step_of_group):
    @pl.when(first_step_of_group)
    def _():
        for lo in range(0, S, 512):
            for j, f in enumerate(_rope_factors(cs_ref[lo:lo + 512, :])):
                fac[j, lo:lo + 512, :] = f


LB_PER_CB = CB // LANES
ROPE_NB = 3 * DIL_G * LB_PER_CB


def _rope_step(s):
    per_group = 3 * LB_PER_CB
    return s // per_group, (s % per_group) // LB_PER_CB, s % LB_PER_CB


def _rope_split(proj_b, tables):
    def body(x_ref, cs_ref, o_ref, fac):
        g, t, hf = _rope_step(pl.program_id(0))
        _rope_cache_factors(cs_ref, fac, (t == 0) & (hf == 0))
        for gs in range(DIL_G):
            @pl.when(g == gs)
            def _(gs=gs):
                for tok, sub in _residue_pieces(DIL_R[gs]):
                    x = x_ref[tok, :]
                    y = _rope_apply(x, fac[0, sub, :], fac[1, sub, :], fac[2, sub, :], False)
                    o_ref[sub, :] = jnp.where(t < 2, y, x).astype(BF16)

    def in_index(s):
        g, t, hf = _rope_step(s)
        return 0, (t * DIL_G + g) * LB_PER_CB + hf

    def out_index(s):
        g, t, hf = _rope_step(s)
        return t * DIL_G + g, 0, hf

    tab = pl.BlockSpec((None, S, 2 * (ROPE_DIM // 2)), lambda s: (_rope_step(s)[0], 0, 0))
    return pl.pallas_call(
        body, name="rope_split", grid=(ROPE_NB,),
        in_specs=[pl.BlockSpec((S, LANES), in_index), tab],
        out_specs=pl.BlockSpec((None, S, LANES), out_index),
        out_shape=jax.ShapeDtypeStruct((3 * DIL_G, S, CB), BF16),
        scratch_shapes=[pltpu.VMEM((3, S, LANES), F32)],
        compiler_params=_cparams(("arbitrary",)),
    )(proj_b, tables)


def _rope_merge_bwd(dq3, dk3, dv3, tables, dproj):
    def body(dq_ref, dk_ref, dv_ref, cs_ref, _, o_ref, tmp, fac):
        s = pl.program_id(0)
        g, t, hf = _rope_step(s)
        _rope_cache_factors(cs_ref, fac, (t == 0) & (hf == 0) & (s < ROPE_NB))
        for gs in range(DIL_G):
            @pl.when((g == gs) & (s < ROPE_NB))
            def _(gs=gs):
                for tok, sub in _residue_pieces(DIL_R[gs]):
                    x = jnp.where(t == 0, dq_ref[sub, :], jnp.where(t == 1, dk_ref[sub, :], dv_ref[sub, :]))
                    y = _rope_apply(x, fac[0, sub, :], fac[1, sub, :], fac[2, sub, :], True)
                    tmp[tok, :] = jnp.where(t < 2, y, x)
                o_ref[...] = tmp[...].astype(BF16)

        @pl.when(s >= ROPE_NB)
        def _():
            o_ref[...] = jnp.zeros_like(o_ref)

    def src_spec(own):
        def index(s):
            g, t, hf = _rope_step(jnp.minimum(s, ROPE_NB - 1))
            return g, 0, jnp.where(t == own, hf, jnp.where(t > own, LB_PER_CB - 1, 0))
        return pl.BlockSpec((None, S, LANES), index)

    def out_index(s):
        g, t, hf = _rope_step(s)
        col = SEG_B // LANES + (t * DIL_G + g) * LB_PER_CB + hf
        return 0, jnp.where(s < ROPE_NB, col, SEG_Z // LANES + s - ROPE_NB)

    tab = pl.BlockSpec((None, S, 2 * (ROPE_DIM // 2)), lambda s: (_rope_step(jnp.minimum(s, ROPE_NB - 1))[0], 0, 0))
    return pl.pallas_call(
        body, name="rope_merge_bwd", grid=(ROPE_NB + LB_PER_CB,),
        in_specs=[src_spec(0), src_spec(1), src_spec(2), tab, ANY],
        out_specs=pl.BlockSpec((S, LANES), out_index),
        out_shape=jax.ShapeDtypeStruct((S, D_PAD), BF16),
        scratch_shapes=[pltpu.VMEM((S, LANES), F32), pltpu.VMEM((3, S, LANES), F32)],
        input_output_aliases={4: 0},
        compiler_params=_cparams(("arbitrary",)),
    )(dq3, dk3, dv3, tables, dproj)


DIL_NB = S // DIL_BLK


_BQK = (((2,), (2,)), ((0,), (0,)))
_BQD = (((2,), (1,)), ((0,), (0,)))
_BKD = (((1,), (1,)), ((0,), (0,)))


def _dil_mask(g):
    shape = (DIL_NB, DIL_BLK, 2 * DIL_BLK)
    blocks_per_seq = jnp.right_shift(DIL_NB, 2 * g)
    has_prev = jnp.bitwise_and(lax.broadcasted_iota(jnp.int32, shape, 0), blocks_per_seq - 1) != 0
    a = lax.broadcasted_iota(jnp.int32, shape, 1)
    kk = lax.broadcasted_iota(jnp.int32, shape, 2)
    diff = DIL_BLK + a - kk
    return (diff >= 0) & (diff <= DIL_BLK) & ((kk >= DIL_BLK) | has_prev)


def _blocks(t):
    return t.reshape(DIL_NB, DIL_BLK, t.shape[-1])


def _with_prev(t):
    prev = jnp.concatenate([jnp.zeros((DIL_BLK, t.shape[-1]), t.dtype), t[:-DIL_BLK]], axis=0)
    return jnp.concatenate([_blocks(prev), _blocks(t)], axis=1)


def _fold_prev(t2):
    n = t2.shape[-1]
    to_prev = t2[:, :DIL_BLK].reshape(S, n)
    own = t2[:, DIL_BLK:].reshape(S, n)
    return own + jnp.concatenate([to_prev[DIL_BLK:], jnp.zeros((DIL_BLK, n), t2.dtype)], axis=0)


def _dil_group_spec(t, width=DIL_OUT):
    return pl.BlockSpec((None, S, width), lambda g: (t * DIL_G + g, 0, 0))


def _dil_fwd(qkv3):
    def body(q_ref, k_ref, v_ref, o_ref, lse_ref):
        ok = _dil_mask(pl.program_id(0))
        lane = lax.broadcasted_iota(jnp.int32, (S, LANES), 1)
        lse = jnp.zeros((S, LANES), F32)
        for h in range(DIL_HG):
            sl = slice(h * HD, (h + 1) * HD)
            s = jnp.where(ok, _dot(_blocks(q_ref[:, sl]), _with_prev(k_ref[:, sl]), _BQK) * SCALE, NEG_INF)
            m = jnp.max(s, axis=-1, keepdims=True)
            p = jnp.exp(s - m)
            l = jnp.sum(p, axis=-1, keepdims=True)
            o_ref[:, sl] = _dot(p / l, _with_prev(v_ref[:, sl]), _BQD).reshape(S, HD)
            lse = jnp.where(lane == h, (m + jnp.log(l)).reshape(S, 1), lse)
        lse_ref[...] = lse

    return pl.pallas_call(
        body, name="dil_fwd", grid=(DIL_G,),
        in_specs=[_dil_group_spec(0), _dil_group_spec(1), _dil_group_spec(2)],
        out_specs=[_dil_group_spec(0), _dil_group_spec(0, LANES)],
        out_shape=[jax.ShapeDtypeStruct((DIL_G, S, DIL_OUT), F32), jax.ShapeDtypeStruct((DIL_G, S, LANES), F32)],
        compiler_params=_cparams(("parallel",)),
    )(qkv3, qkv3, qkv3)


def _dil_bwd(qkv3, lse3, do3, c3):
    def body(q_ref, k_ref, v_ref, lse_ref, do_ref, c_ref, dq_ref, dk_ref, dv_ref):
        ok = _dil_mask(pl.program_id(0))
        lse = lse_ref[...]
        c = c_ref[...]
        for h in range(DIL_HG):
            sl = slice(h * HD, (h + 1) * HD)
            q = _blocks(q_ref[:, sl])
            k2 = _with_prev(k_ref[:, sl])
            do_h = _blocks(do_ref[:, sl])
            s = jnp.where(ok, _dot(q, k2, _BQK) * SCALE, NEG_INF)
            p = jnp.exp(s - _blocks(lse[:, h:h + 1]))
            dp = _dot(do_h, _with_prev(v_ref[:, sl]), _BQK)
            ds = p * (dp - _blocks(c[:, h:h + 1]))
            dsb = (ds * SCALE).astype(BF16)
            dq_ref[:, sl] = _dot(dsb, k2, _BQD).reshape(S, HD)
            dk_ref[:, sl] = _fold_prev(_dot(dsb, q, _BKD))
            dv_ref[:, sl] = _fold_prev(_dot(p, do_h, _BKD))

    return pl.pallas_call(
        body, name="dil_bwd", grid=(DIL_G,),
        in_specs=[_dil_group_spec(0), _dil_group_spec(1), _dil_group_spec(2), _dil_group_spec(0, LANES),
                  _dil_group_spec(0), _dil_group_spec(0, LANES)],
        out_specs=[_dil_group_spec(0)] * 3,
        out_shape=[jax.ShapeDtypeStruct((DIL_G, S, DIL_OUT), F32)] * 3,
        compiler_params=_cparams(("parallel",)),
    )(qkv3, qkv3, qkv3, lse3, do3, c3)


COMB_T = 256


def _dil_combine(o3, lse3):
    heads_per_lb = LANES // HD

    def body(o_ref, lse_ref, ob_ref, al_ref, *scratch):
        o_tok = [scratch[LB_PER_CB * gg:LB_PER_CB * (gg + 1)] for gg in range(DIL_G)]
        lse_tok = scratch[LB_PER_CB * DIL_G:]
        g = pl.program_id(0)
        for gs in range(DIL_G):
            @pl.when(g == gs)
            def _(gs=gs):
                for tok, sub in _residue_pieces(DIL_R[gs]):
                    for hf in range(LB_PER_CB):
                        o_tok[gs][hf][tok, :] = o_ref[sub, hf * LANES:(hf + 1) * LANES]
                    lse_tok[gs][tok, :] = lse_ref[sub, :]

        @pl.when(g == DIL_G - 1)
        def _():
            def chunk(i, carry):
                rows = pl.ds(pl.multiple_of(i * COMB_T, COMB_T), COMB_T)
                lse = [lse_tok[gg][rows, :] for gg in range(DIL_G)]
                m = jnp.maximum(jnp.maximum(lse[0], lse[1]), lse[2])
                e = [jnp.exp(lse[gg] - m) for gg in range(DIL_G)]
                den = (e[0] + e[1]) + e[2]
                al = [e[gg] / den for gg in range(DIL_G)]
                for gg in range(DIL_G):
                    al_ref[gg, rows, :] = al[gg]
                for h in range(DIL_HG):
                    hf, sl = h // heads_per_lb, slice((h % heads_per_lb) * HD, (h % heads_per_lb + 1) * HD)
                    acc = al[0][:, h:h + 1] * o_tok[0][hf][rows, sl]
                    for gg in range(1, DIL_G):
                        acc = acc + al[gg][:, h:h + 1] * o_tok[gg][hf][rows, sl]
                    ob_ref[rows, h * HD:(h + 1) * HD] = acc
                return carry

            lax.fori_loop(0, S // COMB_T, chunk, 0)

    return pl.pallas_call(
        body, name="dil_combine", grid=(DIL_G,),
        in_specs=[pl.BlockSpec((None, S, DIL_OUT), lambda g: (g, 0, 0)),
                  pl.BlockSpec((None, S, LANES), lambda g: (g, 0, 0))],
        out_specs=[pl.BlockSpec((S, DIL_OUT), lambda g: (0, 0)),
                   pl.BlockSpec((DIL_G, S, LANES), lambda g: (0, 0, 0))],
        out_shape=[jax.ShapeDtypeStruct((S, DIL_OUT), F32), jax.ShapeDtypeStruct((DIL_G, S, LANES), F32)],
        scratch_shapes=[pltpu.VMEM((S, LANES), F32)] * (DIL_G * (LB_PER_CB + 1)),
        compiler_params=_cparams(("arbitrary",)),
    )(o3, lse3)


def _dil_combine_bwd(dob, ob, alpha, deps=()):
    heads_per_lb = LANES // HD

    def body(dob_ref, ob_ref, al_ref, *refs):
        do_ref, c_ref = refs[len(deps):]
        g = pl.program_id(0)
        hf = pl.program_id(1)
        for gs in range(DIL_G):
            @pl.when(g == gs)
            def _(gs=gs):
                for tok, sub in _residue_pieces(DIL_R[gs]):
                    dob = dob_ref[tok, :]
                    prod = dob * ob_ref[tok, :]
                    al = al_ref[tok, :]
                    lane = lax.broadcasted_iota(jnp.int32, al.shape, 1)
                    c = jnp.where(hf == 0, 0.0, c_ref[sub, :])
                    for hh in range(heads_per_lb):
                        sl = slice(hh * HD, (hh + 1) * HD)
                        head = hf * heads_per_lb + hh
                        a = jnp.sum(jnp.where(lane == head, al, 0.0), axis=-1, keepdims=True)
                        do_ref[sub, sl] = (a * dob[:, sl]).astype(BF16)
                        c = jnp.where(lane == head, a * jnp.sum(prod[:, sl], axis=-1, keepdims=True), c)
                    c_ref[sub, :] = c

    half = pl.BlockSpec((S, LANES), lambda g, hf: (0, hf))
    return pl.pallas_call(
        body, name="dil_combine_bwd", grid=(DIL_G, LB_PER_CB),
        in_specs=[half, half, pl.BlockSpec((None, S, LANES), lambda g, hf: (g, 0, 0))] + [ANY] * len(deps),
        out_specs=[pl.BlockSpec((None, S, LANES), lambda g, hf: (g, 0, hf)),
                   pl.BlockSpec((None, S, LANES), lambda g, hf: (g, 0, 0))],
        out_shape=[jax.ShapeDtypeStruct((DIL_G, S, DIL_OUT), BF16), jax.ShapeDtypeStruct((DIL_G, S, LANES), F32)],
        compiler_params=_cparams(("parallel", "arbitrary")),
    )(dob, ob, alpha, *deps)


def _local_step(x, tgt, h1, g_attn, g_mlp, g_final, b_pad, w_in_t, hooks):
    tables = _rope_tables()

    first = hooks.first_deps()
    proj_a = _mm(h1, w_in_t, mode="nt", tm=S, tn=512, n=3 * FOX_W, out_dtypes=[BF16], name="proj_a", deps=first)
    proj_b = _mm(h1, w_in_t, mode="nt", tm=S, tn=DIL_W, n=3 * DIL_W, b_off=SEG_B // DIL_W, out_dtypes=[F32], name="proj_b",
                deps=first)
    proj_g = _mm(h1, w_in_t, mode="nt", tm=S, tn=D, n=2 * D, b_off=SEG_G // D, out_dtypes=[F32], name="proj_g",
                deps=first)
    proj_f = _mm(h1, w_in_t, mode="nt", tm=S, tn=LANES, n=LANES, b_off=SEG_F // LANES, out_dtypes=[F32], name="proj_f",
                deps=first)

    q_aug, k_aug = _fox_prepare(proj_a, proj_f, b_pad)
    oa, lse_a = _fox_fwd(q_aug, k_aug, proj_a)

    qkv3 = _rope_split(proj_b, tables)
    o3, lse3 = _dil_fwd(qkv3)
    ob, alpha = _dil_combine(o3, lse3)

    w_a, w_b, w_out = hooks.mixer_weights(ob)
    ya, yb, mixed, x2, h2 = _mixer_tail(oa, ob, w_a, w_b, w_out, proj_g, x, g_mlp)
    w_up_sh, w_down = hooks.mlp_weights(h2)

    def up_epilogue(acc):
        r = jnp.maximum(acc, 0.0)
        return acc, r * r

    u, act = _mm(h2, w_up_sh, mode="nn", tm=S, tn=DFF // N_DEV, b_sharded=True, out_dtypes=[F32, BF16],
                 name="mlp_up", epilogue=up_epilogue)

    def loss_epilogue(acc, rows, vecs):
        dx3, dg, loss = _final_loss_rows(rows[0] + acc, vecs[0], rows[1])
        return (dx3,), (dg, loss)

    dx3, dg_final, loss = _mm_rows(act, w_down, tm=ROWS_TM, name="mlp_down_loss", epilogue=loss_epilogue,
                                   rows=(x2, tgt), vecs=(g_final,), row_out=(F32,), vec_out=(D, LANES))

    def rms_bwd_epilogue(acc, rows, vecs):
        dx, dg = _rms_bwd_rows(acc, rows[0], rows[1], vecs[0])
        return (dx,), (dg,)

    du = _mm(dx3, w_down, mode="nt", tm=S, tn=512, out_dtypes=[BF16], name="mlp_down_bwd",
             epilogue=lambda acc, u_t: (acc * (2.0 * jnp.maximum(u_t, 0.0)),), extras=(u,))
    dw_down = _mm(act, dx3, mode="tn", tm=512, tn=D, out_dtypes=[F32], name="dw_down")
    dw_up_sh = _mm(h2, du, mode="tn", tm=D, tn=DFF // N_DEV, out_sharded=True, out_dtypes=[F32], name="dw_up")
    dx2, dg_mlp = _mm_rows(du, w_up_sh, b_sharded=True, tm=ROWS_TM, name="mlp_up_bwd", epilogue=rms_bwd_epilogue,
                           rows=(x2, dx3), vecs=(g_mlp,), row_out=(F32,), vec_out=(D,),
                           deps=hooks.mlp_grads(dw_up_sh, dw_down))

    dw_out = _mm(mixed, dx2, mode="tn", tm=D, tn=D, out_dtypes=[F32], name="dw_out")

    dyab, dproj = _out_proj_gate_bwd(dx2, w_out, proj_g, ya, yb)
    doa = _mm(dyab, w_a, mode="nt", tm=S, tn=FOX_W, k=D, a_off=0, b_sharded=True, out_dtypes=[BF16],
              name="branch_a_bwd")
    dw_a = _mm(oa, dyab, mode="tn", tm=FOX_W, tn=D, n=D, b_off=0, out_sharded=True, out_dtypes=[F32],
               name="dw_branch_a")
    dob = _mm(dyab, w_b, mode="nt", tm=S, tn=CB, k=D, a_off=1, b_sharded=True, out_dtypes=[F32],
              name="branch_b_bwd")
    dw_b = _mm(ob, dyab, mode="tn", tm=DIL_OUT, tn=D, n=D, b_off=1, out_sharded=True, out_dtypes=[F32],
               name="dw_branch_b")

    dproj, dft, dfs = _fox_bwd(q_aug, k_aug, proj_a, lse_a, doa, dproj, deps=hooks.mixer_grads(dw_a, dw_b, dw_out))
    dproj, db = _fox_dscan(dft, dfs, proj_f, b_pad, dproj)

    do3, c3 = _dil_combine_bwd(dob, ob, alpha, deps=hooks.mid_backward(db))
    dq3, dk3, dv3 = _dil_bwd(qkv3, lse3, do3, c3)
    dproj = _rope_merge_bwd(dq3, dk3, dv3, tables, dproj)

    dw_in_t = _mm(dproj, h1, mode="tn", tm=DW_IN_TM, tn=D, out_dtypes=[F32], name="dw_in")
    dx, dg_attn = _mm_rows(dproj, w_in_t, tm=ROWS_TM // 2, name="proj_bwd", epilogue=rms_bwd_epilogue,
                           rows=(x, dx2), vecs=(g_attn,), row_out=(F32,), vec_out=(D,),
                           deps=hooks.w_in_grad(dw_in_t))

    return loss, dx, (dg_attn, db, dg_mlp, dg_final)


SHARD_SHAPES = ((D_IN // N_DEV, D), (FOX_W, D // N_DEV), (DIL_OUT, D // N_DEV), (D // N_DEV, D),
                (D, DFF // N_DEV), (DFF // N_DEV, D))
W_NAMES = ("w_in", "w_a", "w_b", "w_out", "w_up", "w_down")
AG_COPIES = 7
HBM = pl.BlockSpec(memory_space=pltpu.HBM)
SEM = pl.BlockSpec(memory_space=pltpu.SEMAPHORE)


def _place():
    return lax.axis_index("x"), lax.axis_index("y"), lax.axis_index("c")


W_SLAB = 752
W_WIN = 272
PAD_BLK = 256


def _pad_runs():
    runs = sorted((pad, pad + b - a, p, a) for p in range(N_DEV) for a, b, pad in _shard_pieces(p))
    blocks = []
    for k in range(D_PAD // PAD_BLK):
        lo, hi = k * PAD_BLK, (k + 1) * PAD_BLK
        blocks.append([(max(lo, r0) - lo, min(hi, r1) - lo, p, a + max(lo, r0) - r0)
                       for r0, r1, p, a in runs if max(lo, r0) < min(hi, r1)])
    return blocks


def _gather_w_in(shard, x_in, g_attn, late_shards):
    blocks = _pad_runs()

    norm_t = 512
    n_chunks = S // norm_t

    n_late = len(late_shards)
    late_shapes = [t.shape for t in late_shards]

    def body(x_ref, xin_ref, g_ref, *refs):
        late_in, (out_ref, h1_ref), late_out = refs[:n_late], refs[n_late:n_late + 2], refs[n_late + 2:2 * n_late + 2]
        scratch = refs[2 * n_late + 2:]
        stage, land, xbuf = scratch[:3]
        late_stage, late_cast = scratch[3:3 + n_late], scratch[3 + n_late:3 + 2 * n_late]
        send_sems, recv_sems, load_sem, x_sems, late_sems, store_sems = scratch[3 + 2 * n_late:]
        x, y, c = _place()
        me, sibling = (x, y, c), (x, y, 1 - c)
        chips = [(1 - x, y), (x, 1 - y), (1 - x, 1 - y)]

        def slot(px, py, pc):
            return land.at[4 * px + 2 * py + pc]

        def copy(k, block, to):
            return pltpu.make_async_remote_copy(
                src_ref=slot(*block), dst_ref=slot(*block), send_sem=send_sems.at[k], recv_sem=recv_sems.at[k],
                device_id=to, device_id_type=MESH)

        load = pltpu.make_async_copy(x_ref, stage, load_sem)
        load.start()
        load.wait()
        land[4 * x + 2 * y + c] = stage[...].astype(BF16)
        sent = [copy(0, me, sibling)] + [copy(1 + j, me, (*chip, c)) for j, chip in enumerate(chips)]
        for cp in sent:
            cp.start()

        def x_load(i):
            return pltpu.make_async_copy(xin_ref.at[pl.ds(i * norm_t, norm_t)], xbuf.at[i % 2], x_sems.at[i % 2])

        late_loads = [pltpu.make_async_copy(late_in[i], late_stage[i], late_sems.at[i]) for i in range(n_late)]
        for ld in late_loads:
            ld.start()
        x_load(0).start()
        for i in range(n_chunks):
            if i + 1 < n_chunks:
                x_load(i + 1).start()
            x_load(i).wait()
            h1_ref[i * norm_t:(i + 1) * norm_t, :] = _rms_rows(xbuf[i % 2], g_ref[...]).astype(BF16)
        stores = []
        for i in range(n_late):
            late_loads[i].wait()
            late_cast[i][...] = late_stage[i][...].astype(BF16)
            stores.append(pltpu.make_async_copy(late_cast[i], late_out[i].at[4 * x + 2 * y + c], store_sems.at[i]))
            stores[-1].start()

        for j, chip in enumerate(chips):
            copy(1 + j, (*chip, c), me).wait_recv()
            sent.append(copy(4 + j, (*chip, c), sibling))
            sent[-1].start()
        copy(0, sibling, me).wait_recv()
        for j, chip in enumerate(chips):
            copy(4 + j, (*chip, 1 - c), me).wait_recv()
        for cp in sent:
            cp.wait_send()
        for st in stores:
            st.wait()

        row = lax.broadcasted_iota(jnp.int32, (PAD_BLK, D), 0)
        for k, runs in enumerate(blocks):
            out = jnp.zeros((PAD_BLK, D), F32)
            for o_lo, o_hi, p, a in runs:
                start = min(a // 16 * 16, W_SLAB - W_WIN)
                win = land[p, start:start + W_WIN, :].astype(F32)
                moved = pltpu.roll(win, (o_lo - (a - start)) % W_WIN, 0)[:PAD_BLK]
                out = jnp.where((row >= o_lo) & (row < o_hi), moved, out)
            out_ref[k * PAD_BLK:(k + 1) * PAD_BLK, :] = out.astype(BF16)

    return pl.pallas_call(
        body, name="all_gather_w_in",
        out_shape=[jax.ShapeDtypeStruct((D_PAD, D), BF16), jax.ShapeDtypeStruct((S, D), BF16)]
                  + [jax.ShapeDtypeStruct((N_DEV,) + sh, BF16) for sh in late_shapes],
        in_specs=[ANY, ANY, pl.BlockSpec(memory_space=pltpu.VMEM)] + [ANY] * n_late,
        out_specs=[pl.BlockSpec(memory_space=pltpu.VMEM)] * 2 + [ANY] * n_late,
        scratch_shapes=[pltpu.VMEM((W_SLAB, D), F32), pltpu.VMEM((N_DEV, W_SLAB, D), BF16),
                        pltpu.VMEM((2, norm_t, D), F32)]
                       + [pltpu.VMEM(sh, F32) for sh in late_shapes] + [pltpu.VMEM(sh, BF16) for sh in late_shapes]
                       + [pltpu.SemaphoreType.DMA((AG_COPIES,)), pltpu.SemaphoreType.DMA((AG_COPIES,)),
                          pltpu.SemaphoreType.DMA(()), pltpu.SemaphoreType.DMA((2,)),
                          pltpu.SemaphoreType.DMA((n_late,)), pltpu.SemaphoreType.DMA((n_late,))],
        compiler_params=_cparams(None),
    )(shard, x_in, g_attn, *late_shards)


PEER_COPIES = N_DEV - 1
SIBLING_COPIES = 4
CHIP_COPIES = 3


def _peer_copies(_, land_refs, send_sems, recv_sems):
    x, y, c = _place()
    mine = 4 * x + 2 * y + c
    copies = []
    for i, ref in enumerate(land_refs):
        for k in range(1, N_DEV):
            peer = (x ^ (k >> 2), y ^ ((k >> 1) & 1), c ^ (k & 1))
            n = i * PEER_COPIES + k - 1
            copies.append(pltpu.make_async_remote_copy(
                src_ref=ref.at[mine], dst_ref=ref.at[mine], send_sem=send_sems.at[n], recv_sem=recv_sems.at[n],
                device_id=peer, device_id_type=MESH))
    return copies


def _sibling_copies(g_refs, r_refs, send_sems, recv_sems):
    x, y, c = _place()
    return [pltpu.make_async_remote_copy(
        src_ref=g_refs[i].at[2 * k + (1 - c)], dst_ref=r_refs[i].at[k],
        send_sem=send_sems.at[SIBLING_COPIES * i + k], recv_sem=recv_sems.at[SIBLING_COPIES * i + k],
        device_id=(x, y, 1 - c), device_id_type=MESH) for i in range(len(g_refs)) for k in range(SIBLING_COPIES)]


def _sibling_half_copies(g_refs, r_refs, send_sems, recv_sems):
    x, y, c = _place()
    return [pltpu.make_async_remote_copy(
        src_ref=g_refs[i].at[k], dst_ref=r_refs[i].at[k],
        send_sem=send_sems.at[SIBLING_COPIES * i + k], recv_sem=recv_sems.at[SIBLING_COPIES * i + k],
        device_id=(x, y, 1 - c), device_id_type=MESH) for i in range(len(g_refs)) for k in range(SIBLING_COPIES)]


def _chip_copies(p_refs, r_refs, send_sems, recv_sems):
    x, y, c = _place()
    chips = [(1 - x, y), (x, 1 - y), (1 - x, 1 - y)]
    return [pltpu.make_async_remote_copy(
        src_ref=p_refs[i].at[2 * cx + cy], dst_ref=r_refs[i].at[j],
        send_sem=send_sems.at[CHIP_COPIES * i + j], recv_sem=recv_sems.at[CHIP_COPIES * i + j],
        device_id=(cx, cy, c), device_id_type=MESH) for i in range(len(p_refs)) for j, (cx, cy) in enumerate(chips)]


def _in_hbm(a):
    return pltpu.with_memory_space_constraint(a, pltpu.HBM)


def _exchange_start(name, copies_fn, n_copies, srcs, lands, after=()):
    n_s, n_l, n_a = len(srcs), len(lands), len(after)

    def body(*refs):
        outs = refs[n_s + n_l + n_a:]
        for cp in copies_fn(refs[:n_s], refs[n_s:n_s + n_l], outs[0], outs[1]):
            cp.start()
        outs[-1][...] = jnp.zeros_like(outs[-1])

    res = pl.pallas_call(
        body, name=name,
        out_shape=[pltpu.SemaphoreType.DMA((n_copies,)), pltpu.SemaphoreType.DMA((n_copies,))]
                  + [pltpu.HBM(t.shape, t.dtype) for t in (*srcs, *lands)] + [jax.ShapeDtypeStruct((8, LANES), F32)],
        in_specs=[HBM] * (n_s + n_l) + [ANY] * n_a,
        out_specs=[SEM, SEM] + [HBM] * (n_s + n_l) + [pl.BlockSpec(memory_space=pltpu.VMEM)],
        input_output_aliases={i: 2 + i for i in range(n_s + n_l)},
        compiler_params=pltpu.CompilerParams(has_side_effects=pltpu.SideEffectType.DATAFLOW_SIDE_EFFECTING),
    )(*[_in_hbm(t) for t in (*srcs, *lands)], *after)
    return res[0], res[1], list(res[2:2 + n_s]), list(res[2 + n_s:2 + n_s + n_l]), res[-1]


def _exchange_wait(name, copies_fn, started, after):
    send_sems, recv_sems, srcs, lands, _ = started
    n_s, n_l = len(srcs), len(lands)

    def body(*refs):
        for cp in copies_fn(refs[:n_s], refs[n_s:n_s + n_l], refs[n_s + n_l], refs[n_s + n_l + 1]):
            cp.wait_send()
            cp.wait_recv()

    res = pl.pallas_call(
        body, name=name,
        out_shape=[pltpu.HBM(t.shape, t.dtype) for t in (*srcs, *lands)],
        in_specs=[HBM] * (n_s + n_l) + [SEM, SEM, ANY],
        out_specs=[HBM] * (n_s + n_l),
        input_output_aliases={i: i for i in range(n_s + n_l)},
        compiler_params=pltpu.CompilerParams(has_side_effects=pltpu.SideEffectType.DATAFLOW_SIDE_EFFECTING),
    )(*srcs, *lands, send_sems, recv_sems, after)
    return list(res[:n_s]), list(res[n_s:])


def _shard_block(shape):
    rows, cols = shape
    if rows * cols <= 256 * 1024:
        return rows, cols
    if rows % 256:
        return rows, 256
    return 256, cols


def _chip_partial(ids, g_stack, recv1, name):
    shape = g_stack.shape[1:]
    br, bc = _shard_block(shape)
    whole = g_stack.shape[0] == N_DEV

    def body(ids_ref, g_ref, r_ref, pb_ref, own_ref):
        s = g_ref[...] + r_ref[...]
        pb_ref[...] = s.astype(BF16)

        @pl.when(pl.program_id(2) == ids_ref[1])
        def _():
            own_ref[...] = s

    grid_spec = pltpu.PrefetchScalarGridSpec(
        num_scalar_prefetch=1, grid=(shape[0] // br, shape[1] // bc, 4),
        in_specs=[pl.BlockSpec((None, br, bc), lambda r, q, k, ids: (2 * k + ids[0] if whole else k, r, q)),
                  pl.BlockSpec((None, br, bc), lambda r, q, k, ids: (k, r, q))],
        out_specs=[pl.BlockSpec((None, br, bc), lambda r, q, k, ids: (k, r, q)),
                   pl.BlockSpec((br, bc), lambda r, q, k, ids: (r, q))])
    return pl.pallas_call(
        body, name=name, grid_spec=grid_spec,
        out_shape=[jax.ShapeDtypeStruct((4,) + shape, BF16), jax.ShapeDtypeStruct(shape, F32)],
        compiler_params=_cparams(("parallel", "parallel", "arbitrary")),
    )(ids, g_stack, recv1)


def _adamw(w, g, m, v):
    m = ADAM_B1 * m + (1.0 - ADAM_B1) * g
    v = ADAM_B2 * v + (1.0 - ADAM_B2) * (g * g)
    m_hat = m / (1.0 - ADAM_B1 ** ADAM_STEP)
    v_hat = v / (1.0 - ADAM_B2 ** ADAM_STEP)
    delta = -ADAM_LR * (m_hat / (jnp.sqrt(v_hat) + ADAM_EPS) + ADAM_WD * w)
    return delta, m, v


def _reduce_adamw(own, recv2, w, m, v, name, deps=()):
    shape = own.shape
    br, bc = _shard_block(shape)

    def body(own_ref, r_ref, w_ref, m_ref, v_ref, *refs):
        g_ref, d_ref, nm_ref, nv_ref = refs[len(deps):]
        g = own_ref[...]
        for j in range(3):
            g = g + r_ref[j].astype(F32)
        delta, nm, nv = _adamw(w_ref[...], g, m_ref[...], v_ref[...])
        g_ref[...] = g
        d_ref[...] = delta
        nm_ref[...] = nm
        nv_ref[...] = nv

    blk = pl.BlockSpec((br, bc), lambda r, q: (r, q))
    return pl.pallas_call(
        body, name=name, grid=(shape[0] // br, shape[1] // bc),
        in_specs=[blk, pl.BlockSpec((3, br, bc), lambda r, q: (0, r, q)), blk, blk, blk] + [ANY] * len(deps),
        out_specs=[blk] * 4, out_shape=[jax.ShapeDtypeStruct(shape, F32)] * 4,
        compiler_params=_cparams(("parallel", "parallel")),
    )(own, recv2, w, m, v, *deps)


def _small_adamw(stack, w, m, v):
    def body(s_ref, w_ref, m_ref, v_ref, g_ref, *o_refs):
        g = s_ref[0]
        for s in range(1, N_DEV):
            g = g + s_ref[s]
        delta, nm, nv = _adamw(w_ref[...], g, m_ref[...], v_ref[...])
        g_ref[...] = g
        for q, t in enumerate((g, delta, nm, nv)):
            attn_ref, b_ref, mlp_ref, final_ref = o_refs[4 * q:4 * q + 4]
            for i in range(8):
                cols = slice(i * LANES, (i + 1) * LANES)
                attn_ref[:, cols] = t[i:i + 1]
                mlp_ref[:, cols] = t[8 + i:9 + i]
                final_ref[:, cols] = t[16 + i:17 + i]
            b_ref[...] = t[24:25, :FOX_H]

    vm = pl.BlockSpec(memory_space=pltpu.VMEM)
    unpacked = [jax.ShapeDtypeStruct((1, n), F32) for n in (D, FOX_H, D, D)]
    return pl.pallas_call(
        body, name="small_adamw",
        in_specs=[vm] * 4, out_specs=[vm] * 17,
        out_shape=[jax.ShapeDtypeStruct((SMALL_R, LANES), F32)] + unpacked * 4,
    )(stack, w, m, v)


def _cols_to_whole(stack):
    rows = stack.shape[1]
    return stack.transpose(1, 0, 2).reshape(rows, -1)


def _whole_to_cols(t):
    rows = t.shape[0]
    return t.reshape(rows, N_DEV, -1).transpose(1, 0, 2)


W_IN_SEGMENTS = ((0, 3 * FOX_W, SEG_A), (3 * FOX_W, 3 * FOX_W + FOX_H, SEG_F),
                 (3 * FOX_W + FOX_H, 3 * FOX_W + FOX_H + 3 * DIL_W, SEG_B), (3 * FOX_W + FOX_H + 3 * DIL_W, D_IN, SEG_G))


def _shard_pieces(p):
    rows = D_IN // N_DEV
    lo, hi = p * rows, (p + 1) * rows
    return [(max(lo, a) - lo, min(hi, b) - lo, pad + max(lo, a) - a)
            for a, b, pad in W_IN_SEGMENTS if max(lo, a) < min(hi, b)]


def _unpad_runs(p):
    rows = D_IN // N_DEV
    blocks = []
    for lo in range(0, rows, PAD_BLK):
        hi = min(lo + PAD_BLK, rows)
        blocks.append([(max(lo, a) - lo, min(hi, b) - lo, pad + max(lo, a) - a)
                       for a, b, pad in _shard_pieces(p) if max(lo, a) < min(hi, b)])
    return blocks


def _unpad_dw_in_t(dwp, ids, kept, deps=()):
    rows = D_IN // N_DEV
    plans = [_unpad_runs(p) for p in range(N_DEV)]
    windows = [[min(src // 8 * 8, D_PAD - W_WIN) for runs in plan for _, _, src in runs] for plan in plans]
    n_win = max(len(w) for w in windows)

    def body(ids_ref, src_ref, *refs):
        o_ref, buf, sems = refs[len(deps):]
        side = ids_ref[0] if kept else 1 - ids_ref[0]
        p = 2 * pl.program_id(0) + side
        row = lax.broadcasted_iota(jnp.int32, (PAD_BLK, D), 0)

        def fetch(q):
            slot = (q // 2) % 2
            return [pltpu.make_async_copy(src_ref.at[pl.ds(start, W_WIN)], buf.at[slot, n], sems.at[slot, n])
                    for n, start in enumerate(windows[q])]

        for q in range(N_DEV):
            @pl.when(p == q)
            def _(q=q):
                for nxt in ([q, q + 2] if q < 2 else [q + 2]):
                    if nxt < N_DEV:
                        for cp in fetch(nxt):
                            cp.start()
                copies, n = fetch(q), 0
                for j, runs in enumerate(plans[q]):
                    out = None
                    for o_lo, o_hi, src in runs:
                        copies[n].wait()
                        moved = pltpu.roll(buf[(q // 2) % 2, n], (o_lo - (src - windows[q][n])) % W_WIN, 0)[:PAD_BLK]
                        out = moved if out is None else jnp.where((row >= o_lo) & (row < o_hi), moved, out)
                        n += 1
                    size = min(PAD_BLK, rows - j * PAD_BLK)
                    o_ref[j * PAD_BLK:j * PAD_BLK + size, :] = out[:size]

    grid_spec = pltpu.PrefetchScalarGridSpec(
        num_scalar_prefetch=1, grid=(N_DEV // 2,),
        in_specs=[ANY] * (1 + len(deps)), out_specs=pl.BlockSpec((None, rows, D), lambda k, ids: (k, 0, 0)),
        scratch_shapes=[pltpu.VMEM((2, n_win, W_WIN, D), F32), pltpu.SemaphoreType.DMA((2, n_win))])
    return pl.pallas_call(
        body, name="unpad_dw_in_kept" if kept else "unpad_dw_in_sent", grid_spec=grid_spec,
        out_shape=jax.ShapeDtypeStruct((N_DEV // 2, rows, D), F32),
        compiler_params=_cparams(("arbitrary",)),
    )(ids, dwp, *deps)


LOSS_ROW = 25


def _pack_small(g_attn, b, g_mlp, g_final, loss_row=None):
    tail = jnp.pad(b, ((0, 7), (0, LANES - b.shape[1])))
    if loss_row is not None:
        tail = tail + jnp.pad(loss_row, ((LOSS_ROW - 24, 31 - LOSS_ROW), (0, 0)))
    return jnp.concatenate([g_attn.reshape(8, LANES), g_mlp.reshape(8, LANES), g_final.reshape(8, LANES), tail], axis=0)


class _StepComm:
    def __init__(self, ids, w_sh, m_sh, v_sh, lands, after_w_in):
        self.ids = ids
        self.w_sh, self.m_sh, self.v_sh = w_sh, m_sh, v_sh
        self.updates = None
        self.gather_mixer = _exchange_start("gather_mixer_start", _peer_copies, PEER_COPIES * 3, [], lands[:3],
                                            after=(after_w_in,))
        self.gather_mlp = _exchange_start("gather_mlp_start", _peer_copies, PEER_COPIES * 2, [], lands[3:],
                                          after=(self.gather_mixer[-1],))
        self.sibling = {}
        self.chips = {}
        self.own = {}
        self.names = {}
        self.copies_fn = {}
        self.kept = {}

    def _reduce_start(self, group, names, grads, after=(), copies_fn=_sibling_copies):
        lands = [lax.empty((SIBLING_COPIES,) + t.shape[1:], F32) for t in grads]
        self.sibling[group] = _exchange_start("grad_%s_sibling_start" % group, copies_fn,
                                              SIBLING_COPIES * len(grads), grads, lands, after=after)
        self.names[group] = names
        self.copies_fn[group] = copies_fn
        return self.sibling[group][-1]

    def _reduce_mid(self, group, after):
        grads, recv1 = _exchange_wait("grad_%s_sibling_wait" % group, self.copies_fn[group], self.sibling[group],
                                      after)
        grads = self.kept.get(group, grads)
        parts = [_chip_partial(self.ids, g, r, "grad_partial_" + n) for g, r, n in zip(grads, recv1, self.names[group])]
        self.own[group] = [p[1] for p in parts]
        srcs = [p[0] for p in parts]
        lands = [lax.empty((CHIP_COPIES,) + t.shape[1:], BF16) for t in srcs]
        self.chips[group] = _exchange_start("grad_%s_chips_start" % group, _chip_copies, CHIP_COPIES * len(srcs),
                                            srcs, lands)
        return self.chips[group][-1]

    def reduced(self, group, after):
        _, recv2 = _exchange_wait("grad_%s_chips_wait" % group, _chip_copies, self.chips[group], after)
        return list(zip(self.own[group], recv2))

    def first_deps(self):
        return [self.gather_mlp[-1]]

    def mixer_weights(self, after):
        _, (g_a, g_b, g_out) = _exchange_wait("gather_mixer_wait", _peer_copies, self.gather_mixer, after)
        return g_a, g_b, g_out.reshape(D, D)

    def mlp_weights(self, after):
        _, (g_up, g_down) = _exchange_wait("gather_mlp_wait", _peer_copies, self.gather_mlp, after)
        return g_up, g_down.reshape(DFF, D)

    def mlp_grads(self, dw_up_sh, dw_down):
        return [self._reduce_start("mlp", W_NAMES[4:], [dw_up_sh, dw_down.reshape((N_DEV,) + SHARD_SHAPES[5])])]

    def mixer_grads(self, dw_a, dw_b, dw_out):
        token = self._reduce_mid("mlp", dw_b)
        grads = [dw_a, dw_b, dw_out.reshape((N_DEV,) + SHARD_SHAPES[3])]
        return [self._reduce_start("mixer", W_NAMES[1:4], grads, after=(token,))]

    def mid_backward(self, after):
        return [self._reduce_mid("mixer", after)]

    def w_in_grad(self, dw_in_t):
        token = self._reduce_start("w_in", W_NAMES[:1], [_unpad_dw_in_t(dw_in_t, self.ids, kept=False)],
                                   copies_fn=_sibling_half_copies)
        self.kept["w_in"] = [_unpad_dw_in_t(dw_in_t, self.ids, kept=True, deps=[token])]
        reduced = self.reduced("mixer", token) + self.reduced("mlp", token)
        self.updates, last = [None] * len(reduced), self.kept["w_in"][0]
        for i in (3, 4, 0, 1, 2):
            if i == 4:
                last = self._reduce_mid("w_in", last)
            self.updates[i] = _reduce_adamw(*reduced[i], self.w_sh[1 + i], self.m_sh[1 + i], self.v_sh[1 + i],
                                            "adamw_" + W_NAMES[1 + i], deps=[last])
            last = self.updates[i][0]
        return [last]

    def w_in_update(self, after):
        (reduced,) = self.reduced("w_in", after)
        return _reduce_adamw(*reduced, self.w_sh[0], self.m_sh[0], self.v_sh[0], "adamw_" + W_NAMES[0])


def kernel(x, norm_attn_g, w_in, b_forget, w_branch_a, w_branch_b, w_out, norm_mlp_g, w_up, w_down, norm_final_g, loss_target, m_norm_attn_g, m_w_in, m_b_forget, m_w_branch_a, m_w_branch_b, m_w_out, m_norm_mlp_g, m_w_up, m_w_down, m_norm_final_g, v_norm_attn_g, v_w_in, v_b_forget, v_w_branch_a, v_w_branch_b, v_w_out, v_norm_mlp_g, v_w_up, v_w_down, v_norm_final_g):
    cx, cy, cc = _place()
    ids = jnp.stack([cc, 2 * cx + cy]).astype(jnp.int32)

    w_sh = [w_in[0].T] + [t[0] for t in (w_branch_a, w_branch_b, w_out, w_up, w_down)]
    m_sh = [m_w_in[0].T] + [t[0] for t in (m_w_branch_a, m_w_branch_b, m_w_out, m_w_up, m_w_down)]
    v_sh = [v_w_in[0].T] + [t[0] for t in (v_w_branch_a, v_w_branch_b, v_w_out, v_w_up, v_w_down)]

    w_in_t, h1, *lands = _gather_w_in(jnp.pad(w_sh[0], ((0, W_SLAB - w_sh[0].shape[0]), (0, 0))), x[0], norm_attn_g,
                                      w_sh[1:])
    comm = _StepComm(ids, w_sh, m_sh, v_sh, lands, w_in_t)
    b_pad = jnp.pad(b_forget, ((0, 0), (0, LANES - FOX_H)))

    loss_row, dx, dsmall = _local_step(
        x[0], loss_target[0], h1, norm_attn_g, norm_mlp_g, norm_final_g.reshape(1, D), b_pad, w_in_t, comm)

    dg_attn, db, dg_mlp, dg_final = dsmall
    stack = lax.dynamic_update_slice(jnp.zeros((N_DEV, SMALL_R, LANES), F32),
                                     _pack_small(dg_attn, db, dg_mlp, dg_final, loss_row)[None],
                                     (4 * cx + 2 * cy + cc, 0, 0))
    small_gather = _exchange_start("small_gather_start", _peer_copies, PEER_COPIES, [], [stack])
    w_in_update = comm.w_in_update(small_gather[-1])
    _, (stack,) = _exchange_wait("small_gather_wait", _peer_copies, small_gather, w_in_update[0])
    pack, *small = _small_adamw(
        stack,
        _pack_small(norm_attn_g, b_forget, norm_mlp_g, norm_final_g.reshape(1, D)),
        _pack_small(m_norm_attn_g, m_b_forget, m_norm_mlp_g, m_norm_final_g.reshape(1, D)),
        _pack_small(v_norm_attn_g, v_b_forget, v_norm_mlp_g, v_norm_final_g.reshape(1, D)))
    big = [w_in_update] + comm.updates

    outs = [pack[LOSS_ROW, 0], dx[None]]
    for q in range(4):
        s_attn, s_b, s_mlp, s_final = small[4 * q:4 * q + 4]
        s_final = s_final.reshape(D)
        b_in, b_a, b_b, b_out, b_up, b_down = [t[q][None] for t in big]
        b_in = jnp.swapaxes(b_in, 1, 2)
        outs += [s_attn, b_in, s_b, b_a, b_b, b_out, s_mlp, b_up, b_down, s_final]
    return tuple(outs)
```

```python
import functools

import jax
import jax.numpy as jnp
from jax import lax
from jax.experimental import pallas as pl
from jax.experimental.pallas import tpu as pltpu

F32 = jnp.float32
BF16 = jnp.bfloat16
MESH = pl.DeviceIdType.MESH

S = 2048
D = 1024
HD = 64
FOX_H = 8
FOX_W = FOX_H * HD
DIL_HG = 4
DIL_G = 3
DIL_W = DIL_G * DIL_HG * HD
DIL_OUT = DIL_HG * HD
DIL_BLK = 128
DIL_R = (1, 4, 16)
DFF = 4 * D
D_IN = 3 * FOX_W + FOX_H + 3 * DIL_W + 2 * D
EPS = 1e-6
NEG_INF = -1e30
SCALE = HD ** -0.5
ROPE_THETA = 500000.0
ROPE_DIM = HD // 4
N_DEV = 8

ADAM_LR = 0.001
ADAM_B1 = 0.9
ADAM_B2 = 0.999
ADAM_EPS = 1e-08
ADAM_WD = 0.01
ADAM_STEP = 10

LANES = 128
CB = 256
SEG_A = 0
SEG_B = 3 * FOX_W
SEG_Z = SEG_B + 3 * DIL_W
SEG_G = SEG_Z + CB
SEG_F = SEG_G + 2 * D
D_PAD = SEG_F + CB
DW_IN_TM = D_PAD // 10
SMALL_R = 32

VMEM_MB = 56


def _cparams(dims=None, vmem_mb=VMEM_MB, **kw):
    return pltpu.CompilerParams(dimension_semantics=dims, vmem_limit_bytes=vmem_mb << 20, **kw)


ANY = pl.BlockSpec(memory_space=pl.ANY)

_NN = (((1,), (0,)), ((), ()))
_NT = (((1,), (1,)), ((), ()))
_TN = (((0,), (0,)), ((), ()))


def _dot(a, b, dims):
    return lax.dot_general(a.astype(BF16), b.astype(BF16), dims, preferred_element_type=F32)


def _mm(a, b, *, mode, tm, tn, out_dtypes, name, n=None, k=None, a_off=0, b_off=0,
        b_sharded=False, out_sharded=False, epilogue=None, extras=(), deps=()):
    n_sh = b.shape[-1] if b_sharded else None
    if mode == "nn":
        m = a.shape[0]
        k = k or a.shape[1]
        a_spec = pl.BlockSpec((tm, k), lambda i, j: (i, a_off))
        if b_sharded:
            assert tn == n_sh
            n = N_DEV * n_sh
            b_spec = pl.BlockSpec((None, k, tn), lambda i, j: (j, 0, 0))
        else:
            n = n or b.shape[1]
            b_spec = pl.BlockSpec((k, tn), lambda i, j: (0, j + b_off))
        dims = _NN
    elif mode == "nt":
        m = a.shape[0]
        k = k or a.shape[1]
        a_spec = pl.BlockSpec((tm, k), lambda i, j: (i, a_off))
        if b_sharded:
            n = b.shape[1]
            b_spec = pl.BlockSpec((N_DEV, tn, n_sh), lambda i, j: (0, j, 0))
        else:
            n = n or b.shape[0]
            b_spec = pl.BlockSpec((tn, k), lambda i, j: (j + b_off, 0))
        dims = _NT
    else:
        k, m = a.shape
        n = n or b.shape[1]
        a_spec = pl.BlockSpec((k, tm), lambda i, j: (0, i))
        b_spec = pl.BlockSpec((k, tn), lambda i, j: (0, j + b_off))
        dims = _TN
    assert m % tm == 0 and n % tn == 0, (name, m, n, tm, tn)
    n_extra = len(extras)
    tile = pl.BlockSpec((tm, tn), lambda i, j: (i, j))
    split_tile = out_sharded and n == tn
    if out_sharded:
        assert mode == "tn" and n // tn in (1, N_DEV)
        if split_tile:
            out_spec = pl.BlockSpec((N_DEV, tm, n // N_DEV), lambda i, j: (0, i, 0))
        else:
            out_spec = pl.BlockSpec((None, tm, tn), lambda i, j: (j, i, 0))
        out_shape = (N_DEV, m, n // N_DEV)
    else:
        out_spec, out_shape = tile, (m, n)

    def body(a_ref, b_ref, *refs):
        if mode == "nt" and b_sharded:
            acc = _dot(a_ref[...], jnp.concatenate([b_ref[p] for p in range(N_DEV)], axis=1), dims)
        else:
            acc = _dot(a_ref[...], b_ref[...], dims)
        ex = [r[...] for r in refs[:n_extra]]
        outs = epilogue(acc, *ex) if epilogue is not None else (acc,)
        for o_ref, o in zip(refs[n_extra + len(deps):], outs):
            if split_tile:
                w = n // N_DEV
                for p in range(N_DEV):
                    o_ref[p] = o[:, p * w:(p + 1) * w].astype(o_ref.dtype)
            else:
                o_ref[...] = o.astype(o_ref.dtype)

    res = pl.pallas_call(
        body, name=name, grid=(m // tm, n // tn),
        in_specs=[a_spec, b_spec] + [tile] * n_extra + [ANY] * len(deps),
        out_specs=[out_spec] * len(out_dtypes),
        out_shape=[jax.ShapeDtypeStruct(out_shape, dt) for dt in out_dtypes],
        compiler_params=_cparams(("parallel", "parallel")),
    )(a, b, *extras, *deps)
    return res if len(out_dtypes) > 1 else res[0]


def _mm_rows(a, b, *, tm, name, epilogue, rows=(), vecs=(), row_out=(), vec_out=(), b_sharded=False, deps=()):
    m, k = a.shape
    n_rows, n_vecs, n_deps = len(rows), len(vecs), len(deps)
    n_sh = b.shape[-1] if b_sharded else None

    def body(a_ref, b_ref, *refs):
        row_refs, vec_refs = refs[:n_rows], refs[n_rows:n_rows + n_vecs]
        outs = refs[n_rows + n_vecs + n_deps:]
        if b_sharded:
            acc = _dot(a_ref[:, 0:n_sh], b_ref[0], _NT)
            for p in range(1, N_DEV):
                acc = acc + _dot(a_ref[:, p * n_sh:(p + 1) * n_sh], b_ref[p], _NT)
        else:
            acc = _dot(a_ref[...], b_ref[...], _NN)
        row_vals, vec_incs = epilogue(acc, [r[...] for r in row_refs], [v[...] for v in vec_refs])
        for o_ref, val in zip(outs[:len(row_out)], row_vals):
            o_ref[...] = val.astype(o_ref.dtype)

        @pl.when(pl.program_id(0) == 0)
        def _():
            for o_ref in outs[len(row_out):]:
                o_ref[...] = jnp.zeros_like(o_ref)

        for o_ref, inc in zip(outs[len(row_out):], vec_incs):
            o_ref[...] += inc

    tile = pl.BlockSpec((tm, D), lambda i: (i, 0))
    b_spec = pl.BlockSpec(b.shape, lambda i: (0,) * b.ndim)
    return pl.pallas_call(
        body, name=name, grid=(m // tm,),
        in_specs=[pl.BlockSpec((tm, k), lambda i: (i, 0)), b_spec] + [tile] * n_rows
                 + [pl.BlockSpec(v.shape, lambda i: (0, 0)) for v in vecs] + [ANY] * n_deps,
        out_specs=[tile] * len(row_out) + [pl.BlockSpec((1, w), lambda i: (0, 0)) for w in vec_out],
        out_shape=[jax.ShapeDtypeStruct((m, D), dt) for dt in row_out]
                  + [jax.ShapeDtypeStruct((1, w), F32) for w in vec_out],
        compiler_params=_cparams(("arbitrary",)),
    )(a, b, *rows, *vecs, *deps)


ROW_T = 256
ROWS_TM = 512


def _rms_rows(x, g):
    r = lax.rsqrt(jnp.mean(x * x, axis=-1, keepdims=True) + EPS)
    return (x * r) * g


def _rms_bwd_rows(dh, x, dres, g):
    r = lax.rsqrt(jnp.mean(x * x, axis=-1, keepdims=True) + EPS)
    xn = x * r
    dhn = dh * g
    dx = dres + r * (dhn - xn * jnp.mean(dhn * xn, axis=-1, keepdims=True))
    return dx, jnp.sum(dh * xn, axis=0, keepdims=True)


def _final_loss_rows(x, g, tgt):
    r = lax.rsqrt(jnp.mean(x * x, axis=-1, keepdims=True) + EPS)
    xn = x * r
    err = xn * g - tgt
    row_loss = jnp.mean(err * err, axis=-1, keepdims=True)
    loss = 0.5 * jnp.sum(row_loss, axis=0, keepdims=True) * jnp.ones((1, LANES), F32)
    dy = err * (1.0 / D)
    dyn = dy * g
    dx = r * (dyn - xn * jnp.mean(dyn * xn, axis=-1, keepdims=True))
    return dx, jnp.sum(dy * xn, axis=0, keepdims=True), loss


def _sigmoid(z):
    return 1.0 / (1.0 + jnp.exp(-z))


GATE_TR = 512


def _mixer_tail(oa, ob, w_a, w_b, w_out, proj_g, x, g_mlp):
    def body(oa_ref, ob_ref, wa_ref, wb_ref, wo_ref, g_ref, x_ref, gm_ref, ya_ref, yb_ref, mixed_ref, x2_ref, h2_ref):
        ya = _dot(oa_ref[...], jnp.concatenate([wa_ref[p] for p in range(N_DEV)], axis=1), _NN)
        yb = _dot(ob_ref[...], jnp.concatenate([wb_ref[p] for p in range(N_DEV)], axis=1), _NN)
        ya_ref[...] = ya
        yb_ref[...] = yb
        mixed = (_sigmoid(g_ref[:, :D]) * ya + _sigmoid(g_ref[:, D:]) * yb).astype(BF16)
        mixed_ref[...] = mixed
        x2 = x_ref[...] + _dot(mixed, wo_ref[...], _NN)
        x2_ref[...] = x2
        h2_ref[...] = _rms_rows(x2, gm_ref[...]).astype(BF16)

    def rows(width):
        return pl.BlockSpec((GATE_TR, width), lambda i: (i, 0))

    def whole(t):
        return pl.BlockSpec(t.shape, lambda i: (0,) * t.ndim)

    return pl.pallas_call(
        body, name="mixer_tail", grid=(S // GATE_TR,),
        in_specs=[rows(FOX_W), rows(DIL_OUT), whole(w_a), whole(w_b), whole(w_out), rows(2 * D), rows(D), whole(g_mlp)],
        out_specs=[rows(D)] * 5,
        out_shape=[jax.ShapeDtypeStruct((S, D), dt) for dt in (F32, F32, BF16, F32, BF16)],
        compiler_params=_cparams(("parallel",)),
    )(oa, ob, w_a, w_b, w_out, proj_g, x, g_mlp)


def _out_proj_gate_bwd(dx2, w_out, proj_g, ya, yb):
    def body(dx_ref, w_ref, g_ref, ya_ref, yb_ref, dy_ref, dg_ref):
        dm = _dot(dx_ref[...], w_ref[...], _NT)
        for half, y_ref in enumerate((ya_ref, yb_ref)):
            cols = slice(half * D, (half + 1) * D)
            s = _sigmoid(g_ref[:, cols])
            dy_ref[:, cols] = (dm * s).astype(BF16)
            dg_ref[:, cols] = (dm * y_ref[...] * (s * (1.0 - s))).astype(BF16)

    row = pl.BlockSpec((GATE_TR, D), lambda i: (i, 0))
    wide = pl.BlockSpec((GATE_TR, 2 * D), lambda i: (i, 0))
    return pl.pallas_call(
        body, name="out_proj_gate_bwd", grid=(S // GATE_TR,),
        in_specs=[row, pl.BlockSpec((D, D), lambda i: (0, 0)), wide, row, row],
        out_specs=[wide, pl.BlockSpec((GATE_TR, 2 * D), lambda i: (i, SEG_G // (2 * D)))],
        out_shape=[jax.ShapeDtypeStruct((S, 2 * D), BF16), jax.ShapeDtypeStruct((S, D_PAD), BF16)],
        compiler_params=_cparams(("parallel",)),
    )(dx2, w_out, proj_g, ya, yb)


FOX_TQ = 512
FOX_TQ_FWD = 1024


def _scan_rows(x, reverse):
    n = x.shape[0]
    row = lax.broadcasted_iota(jnp.int32, x.shape, 0)
    k = 1
    while k < n:
        if reverse:
            x = x + jnp.where(row < n - k, pltpu.roll(x, n - k, 0), 0.0)
        else:
            x = x + jnp.where(row >= k, pltpu.roll(x, k, 0), 0.0)
        k *= 2
    return x


def _fox_dscan(dft, dfs, proj_f, b_pad, dproj):
    def body(dft_ref, dfs_ref, f_ref, b_ref, _, dfa_ref, db_ref):
        dfs_pad = jnp.concatenate([dfs_ref[...], jnp.zeros((LANES - FOX_H, S), F32)], axis=0)
        df = dfs_pad.T + dft_ref[0]
        for hp in range(1, dft_ref.shape[0]):
            df = df + pltpu.roll(dft_ref[hp], 2 * hp, 1)
        dlf = _scan_rows(df, reverse=True)
        z = f_ref[...] + b_ref[...]
        lane = lax.broadcasted_iota(jnp.int32, (S, LANES), 1)
        dfa = jnp.where(lane < FOX_H, dlf / (1.0 + jnp.exp(z)), 0.0)
        dfa_ref[:, :LANES] = dfa.astype(BF16)
        dfa_ref[:, LANES:] = jnp.zeros((S, CB - LANES), BF16)
        db_ref[...] = jnp.sum(dfa, axis=0, keepdims=True)

    return pl.pallas_call(
        body, name="fox_dscan", grid=(1,),
        in_specs=[pl.BlockSpec(dft.shape, lambda i: (0, 0, 0)), pl.BlockSpec((FOX_H, S), lambda i: (0, 0)),
                  pl.BlockSpec((S, LANES), lambda i: (0, 0)), pl.BlockSpec((1, LANES), lambda i: (0, 0)), ANY],
        out_specs=[pl.BlockSpec((S, CB), lambda i: (0, SEG_F // CB)), pl.BlockSpec((1, LANES), lambda i: (0, 0))],
        out_shape=[jax.ShapeDtypeStruct((S, D_PAD), BF16), jax.ShapeDtypeStruct((1, LANES), F32)],
        input_output_aliases={4: 0},
        compiler_params=_cparams(("arbitrary",)),
    )(dft, dfs, proj_f, b_pad, dproj)


FOX_NQ = S // FOX_TQ
FOX_HP = FOX_W // LANES
HEADS_PER_LB = LANES // HD
FOX_AUG = 2 * LANES


def _fox_prepare(proj_a, proj_f, b_pad):
    tr = 512

    def body(q_ref, k_ref, f_ref, b_ref, qa_ref, ka_ref, hi_s, mid_s, lo_s):
        i = pl.program_id(0)

        @pl.when(i == 0)
        def _():
            z = f_ref[...] + b_ref[...]
            lf = jnp.minimum(z, 0.0) - jnp.log1p(jnp.exp(-jnp.abs(z)))
            f = _scan_rows(lf, reverse=False)
            hi = f.astype(BF16).astype(F32)
            r1 = f - hi
            mid = r1.astype(BF16).astype(F32)
            hi_s[...] = hi
            mid_s[...] = mid
            lo_s[...] = (r1 - mid).astype(BF16).astype(F32)

        rows = pl.ds(pl.multiple_of(i * tr, tr), tr)
        hi, mid, lo = hi_s[rows, :], mid_s[rows, :], lo_s[rows, :]
        lane = lax.broadcasted_iota(jnp.int32, (tr, HD), 1)
        q_ones = jnp.where((lane >= 3) & (lane < 6), 1.0, 0.0)
        k_ones = jnp.where(lane < 3, 1.0, 0.0)
        for h in range(FOX_H):
            a1, a2, a3 = hi[:, h:h + 1], mid[:, h:h + 1], lo[:, h:h + 1]
            q_extra = jnp.where(lane == 0, a1, jnp.where(lane == 1, a2, jnp.where(lane == 2, a3, q_ones)))
            k_extra = jnp.where(lane == 3, -a1, jnp.where(lane == 4, -a2, jnp.where(lane == 5, -a3, k_ones)))
            base = h * LANES
            qa_ref[:, base:base + HD] = q_ref[:, h * HD:(h + 1) * HD] * jnp.asarray(SCALE, BF16)
            qa_ref[:, base + HD:base + LANES] = q_extra.astype(BF16)
            ka_ref[:, base:base + HD] = k_ref[:, h * HD:(h + 1) * HD]
            ka_ref[:, base + HD:base + LANES] = k_extra.astype(BF16)

    out_blk = pl.BlockSpec((tr, FOX_H * LANES), lambda i: (i, 0))
    return pl.pallas_call(
        body, name="fox_prepare", grid=(S // tr,),
        in_specs=[pl.BlockSpec((tr, FOX_W), lambda i: (i, 0)), pl.BlockSpec((tr, FOX_W), lambda i: (i, 1)),
                  pl.BlockSpec((S, LANES), lambda i: (0, 0)), pl.BlockSpec((1, LANES), lambda i: (0, 0))],
        out_specs=[out_blk, out_blk],
        out_shape=[jax.ShapeDtypeStruct((S, FOX_H * LANES), BF16)] * 2,
        scratch_shapes=[pltpu.VMEM((S, LANES), F32)] * 3,
        compiler_params=_cparams(("arbitrary",)),
    )(proj_a, proj_a, proj_f, b_pad)


def _fox_scores(qa, ka, w, tq):
    s = _dot(qa, ka, _NT)
    row = lax.broadcasted_iota(jnp.int32, (tq, tq), 0)
    col = lax.broadcasted_iota(jnp.int32, (tq, tq), 1)
    diag = jnp.where(col <= row, s[:, w * tq:], NEG_INF)
    return diag if w == 0 else jnp.concatenate([s[:, :w * tq], diag], axis=1)


def _fox_fwd(q_aug, k_aug, proj_a):
    tq = FOX_TQ_FWD

    def body(qa_ref, ka_ref, v_ref, o_ref, lse_ref):
        qi = pl.program_id(1)
        lane = lax.broadcasted_iota(jnp.int32, (tq, LANES), 1)
        for w in range(S // tq):
            @pl.when(qi == w)
            def _(w=w):
                width = (w + 1) * tq
                lse = jnp.zeros((tq, LANES), F32)
                for hh in range(HEADS_PER_LB):
                    aug = slice(hh * LANES, (hh + 1) * LANES)
                    sl = slice(hh * HD, (hh + 1) * HD)
                    s = _fox_scores(qa_ref[:, aug], ka_ref[:width, aug], w, tq)
                    m = jnp.max(s, axis=-1, keepdims=True)
                    p = jnp.exp(s - m)
                    l = jnp.sum(p, axis=-1, keepdims=True)
                    o_ref[:, sl] = _dot(p / l, v_ref[:width, sl], _NN)
                    lse = jnp.where(lane == hh, m + jnp.log(l), lse)
                lse_ref[...] = lse

    return pl.pallas_call(
        body, name="fox_fwd", grid=(FOX_HP, S // tq),
        in_specs=[pl.BlockSpec((tq, FOX_AUG), lambda h, i: (i, h)),
                  pl.BlockSpec((S, FOX_AUG), lambda h, i: (0, h)),
                  pl.BlockSpec((S, LANES), lambda h, i: (0, 2 * FOX_HP + h))],
        out_specs=[pl.BlockSpec((tq, LANES), lambda h, i: (i, h)),
                   pl.BlockSpec((None, tq, LANES), lambda h, i: (h, i, 0))],
        out_shape=[jax.ShapeDtypeStruct((S, FOX_W), F32), jax.ShapeDtypeStruct((FOX_HP, S, LANES), F32)],
        compiler_params=_cparams(("parallel", "parallel")),
    )(q_aug, k_aug, proj_a)


def _fox_bwd(q_aug, k_aug, proj_a, lse, do, dproj, deps=()):
    n_q = FOX_NQ

    def body(qa_ref, ka_ref, v_ref, lse_ref, do_ref, *refs):
        dproj_ref, dft_ref, dfs_ref, dk_acc, dv_acc, dq_buf, kv_buf, dq_sems, kv_sems = refs[1 + len(deps):]
        hp = pl.program_id(0)
        t = pl.program_id(1)
        slot = t % 2
        col = pl.multiple_of(hp * LANES, LANES)

        def dq_copy(step, buf_slot):
            rows = pl.ds(pl.multiple_of(step * FOX_TQ, FOX_TQ), FOX_TQ)
            return pltpu.make_async_copy(dq_buf.at[buf_slot], dproj_ref.at[rows, pl.ds(col, LANES)], dq_sems.at[buf_slot])

        @pl.when(t == 0)
        def _():
            dk_acc[...] = jnp.zeros_like(dk_acc)
            dv_acc[...] = jnp.zeros_like(dv_acc)

        @pl.when((t == 0) & (hp == 0))
        def _():
            dfs_ref[...] = jnp.zeros_like(dfs_ref)

        @pl.when(t >= 2)
        def _():
            dq_copy(t - 2, slot).wait()

        lane = lax.broadcasted_iota(jnp.int32, (FOX_TQ, LANES), 1)
        for w in range(n_q):
            @pl.when(t == w)
            def _(w=w):
                width = (w + 1) * FOX_TQ
                lse_all = lse_ref[...]
                sub = lax.broadcasted_iota(jnp.int32, (FOX_H, width), 0)
                dft = jnp.zeros((FOX_TQ, LANES), F32)
                for hh in range(HEADS_PER_LB):
                    aug = slice(hh * LANES, (hh + 1) * LANES)
                    sl = slice(hh * HD, (hh + 1) * HD)
                    head = hp * HEADS_PER_LB + hh
                    qa = qa_ref[:, aug]
                    ka = ka_ref[:width, aug]
                    do_h = do_ref[:, sl]
                    s = _fox_scores(qa, ka, w, FOX_TQ)
                    p = jnp.exp(s - lse_all[:, hh:hh + 1])
                    dp = _dot(do_h, v_ref[:width, sl], _NT)
                    ds = p * (dp - jnp.sum(dp * p, axis=-1, keepdims=True))
                    dft = jnp.where(lane == hh, jnp.sum(ds, axis=-1, keepdims=True), dft)
                    dfs_ref[:, :width] -= jnp.where(sub == head, jnp.sum(ds, axis=0, keepdims=True), 0.0)
                    dq_buf[slot, :, sl] = (_dot(ds, ka[:, :HD], _NN) * SCALE).astype(BF16)
                    dk_acc[:width, sl] += _dot(ds, qa[:, :HD], _TN)
                    dv_acc[:width, sl] += _dot(p, do_h, _TN)
                dft_ref[...] = dft

        dq_copy(t, slot).start()

        @pl.when(t == n_q - 1)
        def _():
            dq_copy(t - 1, 1 - slot).wait()
            dq_copy(t, slot).wait()
            kv_buf[0] = dk_acc[...].astype(BF16)
            kv_buf[1] = dv_acc[...].astype(BF16)
            copies = [pltpu.make_async_copy(
                kv_buf.at[j], dproj_ref.at[:, pl.ds(pl.multiple_of((j + 1) * FOX_W + hp * LANES, LANES), LANES)],
                kv_sems.at[j]) for j in range(2)]
            for cp in copies:
                cp.start()
            for cp in copies:
                cp.wait()

    qblk = pl.BlockSpec((FOX_TQ, FOX_AUG), lambda h, t: (t, h))
    lane_blk = pl.BlockSpec((None, FOX_TQ, LANES), lambda h, t: (h, t, 0))
    return pl.pallas_call(
        body, name="fox_bwd", grid=(FOX_HP, n_q),
        in_specs=[qblk, pl.BlockSpec((S, FOX_AUG), lambda h, t: (0, h)),
                  pl.BlockSpec((S, LANES), lambda h, t: (0, 2 * FOX_HP + h)),
                  lane_blk, pl.BlockSpec((FOX_TQ, LANES), lambda h, t: (t, h)), ANY] + [ANY] * len(deps),
        out_specs=[ANY, lane_blk, pl.BlockSpec((FOX_H, S), lambda h, t: (0, 0))],
        out_shape=[jax.ShapeDtypeStruct((S, D_PAD), BF16),
                   jax.ShapeDtypeStruct((FOX_HP, S, LANES), F32), jax.ShapeDtypeStruct((FOX_H, S), F32)],
        scratch_shapes=[pltpu.VMEM((S, LANES), F32), pltpu.VMEM((S, LANES), F32),
                        pltpu.VMEM((2, FOX_TQ, LANES), BF16), pltpu.VMEM((2, S, LANES), BF16),
                        pltpu.SemaphoreType.DMA((2,)), pltpu.SemaphoreType.DMA((2,))],
        input_output_aliases={5: 0},
        compiler_params=_cparams(("arbitrary", "arbitrary")),
    )(q_aug, k_aug, proj_a, lse, do, dproj, *deps)


def _rope_tables():
    half = ROPE_DIM // 2
    shape = (DIL_G, S, half)
    g = lax.broadcasted_iota(jnp.int32, shape, 0)
    row = lax.broadcasted_iota(jnp.int32, shape, 1)
    r = jnp.left_shift(1, 2 * g)
    per_class = S // r
    pos = (row % per_class) * r + row // per_class
    inv_freq = jnp.power(jnp.float32(ROPE_THETA), -lax.broadcasted_iota(F32, shape, 2) * 2.0 / ROPE_DIM)
    ang = pos.astype(F32) * inv_freq
    return jnp.concatenate([jnp.cos(ang), jnp.sin(ang)], axis=-1)


def _rope_factors(cs):
    half = ROPE_DIM // 2
    i = lax.broadcasted_iota(jnp.int32, (2 * half, 2 * LANES), 0)
    l = lax.broadcasted_iota(jnp.int32, (2 * half, 2 * LANES), 1)
    hit = ((l < LANES) & (i == l % half)) | ((l >= LANES) & (i - half == l % half))
    spread = jnp.where(hit, 1.0, 0.0).astype(BF16)
    hi = cs.astype(BF16)
    r1 = cs - hi.astype(F32)
    mid = r1.astype(BF16)
    lo = (r1 - mid.astype(F32)).astype(BF16)
    full = (_dot(hi, spread, _NN) + _dot(mid, spread, _NN)) + _dot(lo, spread, _NN)
    cos, sin = full[:, :LANES], full[:, LANES:]
    lane = lax.broadcasted_iota(jnp.int32, cos.shape, 1) % HD
    t_self = jnp.where(lane < ROPE_DIM, cos, 1.0)
    t_up = jnp.where(lane < half, -sin, 0.0)
    t_dn = jnp.where((lane >= half) & (lane < ROPE_DIM), sin, 0.0)
    return t_self, t_up, t_dn


def _residue_pieces(r):
    if r == 1:
        return [(slice(i, i + 512), slice(i, i + 512)) for i in range(0, S, 512)]
    n = S // r
    return [(pl.ds(j, n, stride=r), slice(j * n, (j + 1) * n)) for j in range(r)]


def _rope_apply(x, a, u, d, backward):
    half = ROPE_DIM // 2
    if backward:
        return x * a + pltpu.roll(x * u, half, 1) + pltpu.roll(x * d, LANES - half, 1)
    return x * a + pltpu.roll(x, LANES - half, 1) * u + pltpu.roll(x, half, 1) * d


def _rope_cache_factors(cs_ref, fac, first_step_of_group):
    @pl.when(first_step_of_group)
    def _():
        for lo in range(0, S, 512):
            for j, f in enumerate(_rope_factors(cs_ref[lo:lo + 512, :])):
                fac[j, lo:lo + 512, :] = f


LB_PER_CB = CB // LANES
ROPE_NB = 3 * DIL_G * LB_PER_CB


def _rope_step(s):
    per_group = 3 * LB_PER_CB
    return s // per_group, (s % per_group) // LB_PER_CB, s % LB_PER_CB


def _rope_split(proj_b, tables):
    def body(x_ref, cs_ref, o_ref, fac):
        g, t, hf = _rope_step(pl.program_id(0))
        _rope_cache_factors(cs_ref, fac, (t == 0) & (hf == 0))
        for gs in range(DIL_G):
            @pl.when(g == gs)
            def _(gs=gs):
                for tok, sub in _residue_pieces(DIL_R[gs]):
                    x = x_ref[tok, :]
                    y = _rope_apply(x, fac[0, sub, :], fac[1, sub, :], fac[2, sub, :], False)
                    o_ref[sub, :] = jnp.where(t < 2, y, x).astype(BF16)

    def in_index(s):
        g, t, hf = _rope_step(s)
        return 0, (t * DIL_G + g) * LB_PER_CB + hf

    def out_index(s):
        g, t, hf = _rope_step(s)
        return t * DIL_G + g, 0, hf

    tab = pl.BlockSpec((None, S, 2 * (ROPE_DIM // 2)), lambda s: (_rope_step(s)[0], 0, 0))
    return pl.pallas_call(
        body, name="rope_split", grid=(ROPE_NB,),
        in_specs=[pl.BlockSpec((S, LANES), in_index), tab],
        out_specs=pl.BlockSpec((None, S, LANES), out_index),
        out_shape=jax.ShapeDtypeStruct((3 * DIL_G, S, CB), BF16),
        scratch_shapes=[pltpu.VMEM((3, S, LANES), F32)],
        compiler_params=_cparams(("arbitrary",)),
    )(proj_b, tables)


def _rope_merge_bwd(dq3, dk3, dv3, tables, dproj):
    def body(dq_ref, dk_ref, dv_ref, cs_ref, _, o_ref, tmp, fac):
        s = pl.program_id(0)
        g, t, hf = _rope_step(s)
        _rope_cache_factors(cs_ref, fac, (t == 0) & (hf == 0) & (s < ROPE_NB))
        for gs in range(DIL_G):
            @pl.when((g == gs) & (s < ROPE_NB))
            def _(gs=gs):
                for tok, sub in _residue_pieces(DIL_R[gs]):
                    x = jnp.where(t == 0, dq_ref[sub, :], jnp.where(t == 1, dk_ref[sub, :], dv_ref[sub, :]))
                    y = _rope_apply(x, fac[0, sub, :], fac[1, sub, :], fac[2, sub, :], True)
                    tmp[tok, :] = jnp.where(t < 2, y, x)
                o_ref[...] = tmp[...].astype(BF16)

        @pl.when(s >= ROPE_NB)
        def _():
            o_ref[...] = jnp.zeros_like(o_ref)

    def src_spec(own):
        def index(s):
            g, t, hf = _rope_step(jnp.minimum(s, ROPE_NB - 1))
            return g, 0, jnp.where(t == own, hf, jnp.where(t > own, LB_PER_CB - 1, 0))
        return pl.BlockSpec((None, S, LANES), index)

    def out_index(s):
        g, t, hf = _rope_step(s)
        col = SEG_B // LANES + (t * DIL_G + g) * LB_PER_CB + hf
        return 0, jnp.where(s < ROPE_NB, col, SEG_Z // LANES + s - ROPE_NB)

    tab = pl.BlockSpec((None, S, 2 * (ROPE_DIM // 2)), lambda s: (_rope_step(jnp.minimum(s, ROPE_NB - 1))[0], 0, 0))
    return pl.pallas_call(
        body, name="rope_merge_bwd", grid=(ROPE_NB + LB_PER_CB,),
        in_specs=[src_spec(0), src_spec(1), src_spec(2), tab, ANY],
        out_specs=pl.BlockSpec((S, LANES), out_index),
        out_shape=jax.ShapeDtypeStruct((S, D_PAD), BF16),
        scratch_shapes=[pltpu.VMEM((S, LANES), F32), pltpu.VMEM((3, S, LANES), F32)],
        input_output_aliases={4: 0},
        compiler_params=_cparams(("arbitrary",)),
    )(dq3, dk3, dv3, tables, dproj)


DIL_NB = S // DIL_BLK


_BQK = (((2,), (2,)), ((0,), (0,)))
_BQD = (((2,), (1,)), ((0,), (0,)))
_BKD = (((1,), (1,)), ((0,), (0,)))


def _dil_mask(g):
    shape = (DIL_NB, DIL_BLK, 2 * DIL_BLK)
    blocks_per_seq = jnp.right_shift(DIL_NB, 2 * g)
    has_prev = jnp.bitwise_and(lax.broadcasted_iota(jnp.int32, shape, 0), blocks_per_seq - 1) != 0
    a = lax.broadcasted_iota(jnp.int32, shape, 1)
    kk = lax.broadcasted_iota(jnp.int32, shape, 2)
    diff = DIL_BLK + a - kk
    return (diff >= 0) & (diff <= DIL_BLK) & ((kk >= DIL_BLK) | has_prev)


def _blocks(t):
    return t.reshape(DIL_NB, DIL_BLK, t.shape[-1])


def _with_prev(t):
    prev = jnp.concatenate([jnp.zeros((DIL_BLK, t.shape[-1]), t.dtype), t[:-DIL_BLK]], axis=0)
    return jnp.concatenate([_blocks(prev), _blocks(t)], axis=1)


def _fold_prev(t2):
    n = t2.shape[-1]
    to_prev = t2[:, :DIL_BLK].reshape(S, n)
    own = t2[:, DIL_BLK:].reshape(S, n)
    return own + jnp.concatenate([to_prev[DIL_BLK:], jnp.zeros((DIL_BLK, n), t2.dtype)], axis=0)


def _dil_group_spec(t, width=DIL_OUT):
    return pl.BlockSpec((None, S, width), lambda g: (t * DIL_G + g, 0, 0))


def _dil_fwd(qkv3):
    def body(q_ref, k_ref, v_ref, o_ref, lse_ref):
        ok = _dil_mask(pl.program_id(0))
        lane = lax.broadcasted_iota(jnp.int32, (S, LANES), 1)
        lse = jnp.zeros((S, LANES), F32)
        for h in range(DIL_HG):
            sl = slice(h * HD, (h + 1) * HD)
            s = jnp.where(ok, _dot(_blocks(q_ref[:, sl]), _with_prev(k_ref[:, sl]), _BQK) * SCALE, NEG_INF)
            m = jnp.max(s, axis=-1, keepdims=True)
            p = jnp.exp(s - m)
            l = jnp.sum(p, axis=-1, keepdims=True)
            o_ref[:, sl] = _dot(p / l, _with_prev(v_ref[:, sl]), _BQD).reshape(S, HD)
            lse = jnp.where(lane == h, (m + jnp.log(l)).reshape(S, 1), lse)
        lse_ref[...] = lse

    return pl.pallas_call(
        body, name="dil_fwd", grid=(DIL_G,),
        in_specs=[_dil_group_spec(0), _dil_group_spec(1), _dil_group_spec(2)],
        out_specs=[_dil_group_spec(0), _dil_group_spec(0, LANES)],
        out_shape=[jax.ShapeDtypeStruct((DIL_G, S, DIL_OUT), F32), jax.ShapeDtypeStruct((DIL_G, S, LANES), F32)],
        compiler_params=_cparams(("parallel",)),
    )(qkv3, qkv3, qkv3)


def _dil_bwd(qkv3, lse3, do3, c3):
    def body(q_ref, k_ref, v_ref, lse_ref, do_ref, c_ref, dq_ref, dk_ref, dv_ref):
        ok = _dil_mask(pl.program_id(0))
        lse = lse_ref[...]
        c = c_ref[...]
        for h in range(DIL_HG):
            sl = slice(h * HD, (h + 1) * HD)
            q = _blocks(q_ref[:, sl])
            k2 = _with_prev(k_ref[:, sl])
            do_h = _blocks(do_ref[:, sl])
            s = jnp.where(ok, _dot(q, k2, _BQK) * SCALE, NEG_INF)
            p = jnp.exp(s - _blocks(lse[:, h:h + 1]))
            dp = _dot(do_h, _with_prev(v_ref[:, sl]), _BQK)
            ds = p * (dp - _blocks(c[:, h:h + 1]))
            dsb = (ds * SCALE).astype(BF16)
            dq_ref[:, sl] = _dot(dsb, k2, _BQD).reshape(S, HD)
            dk_ref[:, sl] = _fold_prev(_dot(dsb, q, _BKD))
            dv_ref[:, sl] = _fold_prev(_dot(p, do_h, _BKD))

    return pl.pallas_call(
        body, name="dil_bwd", grid=(DIL_G,),
        in_specs=[_dil_group_spec(0), _dil_group_spec(1), _dil_group_spec(2), _dil_group_spec(0, LANES),
                  _dil_group_spec(0), _dil_group_spec(0, LANES)],
        out_specs=[_dil_group_spec(0)] * 3,
        out_shape=[jax.ShapeDtypeStruct((DIL_G, S, DIL_OUT), F32)] * 3,
        compiler_params=_cparams(("parallel",)),
    )(qkv3, qkv3, qkv3, lse3, do3, c3)


COMB_T = 256


def _dil_combine(o3, lse3):
    heads_per_lb = LANES // HD

    def body(o_ref, lse_ref, ob_ref, al_ref, *scratch):
        o_tok = [scratch[LB_PER_CB * gg:LB_PER_CB * (gg + 1)] for gg in range(DIL_G)]
        lse_tok = scratch[LB_PER_CB * DIL_G:]
        g = pl.program_id(0)
        for gs in range(DIL_G):
            @pl.when(g == gs)
            def _(gs=gs):
                for tok, sub in _residue_pieces(DIL_R[gs]):
                    for hf in range(LB_PER_CB):
                        o_tok[gs][hf][tok, :] = o_ref[sub, hf * LANES:(hf + 1) * LANES]
                    lse_tok[gs][tok, :] = lse_ref[sub, :]

        @pl.when(g == DIL_G - 1)
        def _():
            def chunk(i, carry):
                rows = pl.ds(pl.multiple_of(i * COMB_T, COMB_T), COMB_T)
                lse = [lse_tok[gg][rows, :] for gg in range(DIL_G)]
                m = jnp.maximum(jnp.maximum(lse[0], lse[1]), lse[2])
                e = [jnp.exp(lse[gg] - m) for gg in range(DIL_G)]
                den = (e[0] + e[1]) + e[2]
                al = [e[gg] / den for gg in range(DIL_G)]
                for gg in range(DIL_G):
                    al_ref[gg, rows, :] = al[gg]
                for h in range(DIL_HG):
                    hf, sl = h // heads_per_lb, slice((h % heads_per_lb) * HD, (h % heads_per_lb + 1) * HD)
                    acc = al[0][:, h:h + 1] * o_tok[0][hf][rows, sl]
                    for gg in range(1, DIL_G):
                        acc = acc + al[gg][:, h:h + 1] * o_tok[gg][hf][rows, sl]
                    ob_ref[rows, h * HD:(h + 1) * HD] = acc
                return carry

            lax.fori_loop(0, S // COMB_T, chunk, 0)

    return pl.pallas_call(
        body, name="dil_combine", grid=(DIL_G,),
        in_specs=[pl.BlockSpec((None, S, DIL_OUT), lambda g: (g, 0, 0)),
                  pl.BlockSpec((None, S, LANES), lambda g: (g, 0, 0))],
        out_specs=[pl.BlockSpec((S, DIL_OUT), lambda g: (0, 0)),
                   pl.BlockSpec((DIL_G, S, LANES), lambda g: (0, 0, 0))],
        out_shape=[jax.ShapeDtypeStruct((S, DIL_OUT), F32), jax.ShapeDtypeStruct((DIL_G, S, LANES), F32)],
        scratch_shapes=[pltpu.VMEM((S, LANES), F32)] * (DIL_G * (LB_PER_CB + 1)),
        compiler_params=_cparams(("arbitrary",)),
    )(o3, lse3)


def _dil_combine_bwd(dob, ob, alpha, deps=()):
    heads_per_lb = LANES // HD

    def body(dob_ref, ob_ref, al_ref, *refs):
        do_ref, c_ref = refs[len(deps):]
        g = pl.program_id(0)
        hf = pl.program_id(1)
        for gs in range(DIL_G):
            @pl.when(g == gs)
            def _(gs=gs):
                for tok, sub in _residue_pieces(DIL_R[gs]):
                    dob = dob_ref[tok, :]
                    prod = dob * ob_ref[tok, :]
                    al = al_ref[tok, :]
                    lane = lax.broadcasted_iota(jnp.int32, al.shape, 1)
                    c = jnp.where(hf == 0, 0.0, c_ref[sub, :])
                    for hh in range(heads_per_lb):
                        sl = slice(hh * HD, (hh + 1) * HD)
                        head = hf * heads_per_lb + hh
                        a = jnp.sum(jnp.where(lane == head, al, 0.0), axis=-1, keepdims=True)
                        do_ref[sub, sl] = (a * dob[:, sl]).astype(BF16)
                        c = jnp.where(lane == head, a * jnp.sum(prod[:, sl], axis=-1, keepdims=True), c)
                    c_ref[sub, :] = c

    half = pl.BlockSpec((S, LANES), lambda g, hf: (0, hf))
    return pl.pallas_call(
        body, name="dil_combine_bwd", grid=(DIL_G, LB_PER_CB),
        in_specs=[half, half, pl.BlockSpec((None, S, LANES), lambda g, hf: (g, 0, 0))] + [ANY] * len(deps),
        out_specs=[pl.BlockSpec((None, S, LANES), lambda g, hf: (g, 0, hf)),
                   pl.BlockSpec((None, S, LANES), lambda g, hf: (g, 0, 0))],
        out_shape=[jax.ShapeDtypeStruct((DIL_G, S, DIL_OUT), BF16), jax.ShapeDtypeStruct((DIL_G, S, LANES), F32)],
        compiler_params=_cparams(("parallel", "arbitrary")),
    )(dob, ob, alpha, *deps)


def _local_step(x, tgt, h1, g_attn, g_mlp, g_final, b_pad, w_in_t, hooks):
    tables = _rope_tables()

    first = hooks.first_deps()
    proj_a = _mm(h1, w_in_t, mode="nt", tm=S, tn=512, n=3 * FOX_W, out_dtypes=[BF16], name="proj_a", deps=first)
    proj_b = _mm(h1, w_in_t, mode="nt", tm=S, tn=DIL_W, n=3 * DIL_W, b_off=SEG_B // DIL_W, out_dtypes=[F32], name="proj_b",
                deps=first)
    proj_g = _mm(h1, w_in_t, mode="nt", tm=S, tn=D, n=2 * D, b_off=SEG_G // D, out_dtypes=[F32], name="proj_g",
                deps=first)
    proj_f = _mm(h1, w_in_t, mode="nt", tm=S, tn=LANES, n=LANES, b_off=SEG_F // LANES, out_dtypes=[F32], name="proj_f",
                deps=first)

    q_aug, k_aug = _fox_prepare(proj_a, proj_f, b_pad)
    oa, lse_a = _fox_fwd(q_aug, k_aug, proj_a)

    qkv3 = _rope_split(proj_b, tables)
    o3, lse3 = _dil_fwd(qkv3)
    ob, alpha = _dil_combine(o3, lse3)

    w_a, w_b, w_out = hooks.mixer_weights(ob)
    ya, yb, mixed, x2, h2 = _mixer_tail(oa, ob, w_a, w_b, w_out, proj_g, x, g_mlp)
    w_up_sh, w_down = hooks.mlp_weights(h2)

    def up_epilogue(acc):
        r = jnp.maximum(acc, 0.0)
        return acc, r * r

    u, act = _mm(h2, w_up_sh, mode="nn", tm=S, tn=DFF // N_DEV, b_sharded=True, out_dtypes=[F32, BF16],
                 name="mlp_up", epilogue=up_epilogue)

    def loss_epilogue(acc, rows, vecs):
        dx3, dg, loss = _final_loss_rows(rows[0] + acc, vecs[0], rows[1])
        return (dx3,), (dg, loss)

    dx3, dg_final, loss = _mm_rows(act, w_down, tm=ROWS_TM, name="mlp_down_loss", epilogue=loss_epilogue,
                                   rows=(x2, tgt), vecs=(g_final,), row_out=(F32,), vec_out=(D, LANES))

    def rms_bwd_epilogue(acc, rows, vecs):
        dx, dg = _rms_bwd_rows(acc, rows[0], rows[1], vecs[0])
        return (dx,), (dg,)

    du = _mm(dx3, w_down, mode="nt", tm=S, tn=512, out_dtypes=[BF16], name="mlp_down_bwd",
             epilogue=lambda acc, u_t: (acc * (2.0 * jnp.maximum(u_t, 0.0)),), extras=(u,))
    dw_down = _mm(act, dx3, mode="tn", tm=512, tn=D, out_dtypes=[F32], name="dw_down")
    dw_up_sh = _mm(h2, du, mode="tn", tm=D, tn=DFF // N_DEV, out_sharded=True, out_dtypes=[F32], name="dw_up")
    dx2, dg_mlp = _mm_rows(du, w_up_sh, b_sharded=True, tm=ROWS_TM, name="mlp_up_bwd", epilogue=rms_bwd_epilogue,
                           rows=(x2, dx3), vecs=(g_mlp,), row_out=(F32,), vec_out=(D,),
                           deps=hooks.mlp_grads(dw_up_sh, dw_down))

    dw_out = _mm(mixed, dx2, mode="tn", tm=D, tn=D, out_dtypes=[F32], name="dw_out")

    dyab, dproj = _out_proj_gate_bwd(dx2, w_out, proj_g, ya, yb)
    doa = _mm(dyab, w_a, mode="nt", tm=S, tn=FOX_W, k=D, a_off=0, b_sharded=True, out_dtypes=[BF16],
              name="branch_a_bwd")
    dw_a = _mm(oa, dyab, mode="tn", tm=FOX_W, tn=D, n=D, b_off=0, out_sharded=True, out_dtypes=[F32],
               name="dw_branch_a")
    dob = _mm(dyab, w_b, mode="nt", tm=S, tn=CB, k=D, a_off=1, b_sharded=True, out_dtypes=[F32],
              name="branch_b_bwd")
    dw_b = _mm(ob, dyab, mode="tn", tm=DIL_OUT, tn=D, n=D, b_off=1, out_sharded=True, out_dtypes=[F32],
               name="dw_branch_b")

    dproj, dft, dfs = _fox_bwd(q_aug, k_aug, proj_a, lse_a, doa, dproj, deps=hooks.mixer_grads(dw_a, dw_b, dw_out))
    dproj, db = _fox_dscan(dft, dfs, proj_f, b_pad, dproj)

    do3, c3 = _dil_combine_bwd(dob, ob, alpha, deps=hooks.mid_backward(db))
    dq3, dk3, dv3 = _dil_bwd(qkv3, lse3, do3, c3)
    dproj = _rope_merge_bwd(dq3, dk3, dv3, tables, dproj)

    dw_in_t = _mm(dproj, h1, mode="tn", tm=DW_IN_TM, tn=D, out_dtypes=[F32], name="dw_in")
    dx, dg_attn = _mm_rows(dproj, w_in_t, tm=ROWS_TM // 2, name="proj_bwd", epilogue=rms_bwd_epilogue,
                           rows=(x, dx2), vecs=(g_attn,), row_out=(F32,), vec_out=(D,),
                           deps=hooks.w_in_grad(dw_in_t))

    return loss, dx, (dg_attn, db, dg_mlp, dg_final)


SHARD_SHAPES = ((D_IN // N_DEV, D), (FOX_W, D // N_DEV), (DIL_OUT, D // N_DEV), (D // N_DEV, D),
                (D, DFF // N_DEV), (DFF // N_DEV, D))
W_NAMES = ("w_in", "w_a", "w_b", "w_out", "w_up", "w_down")
AG_COPIES = 7
HBM = pl.BlockSpec(memory_space=pltpu.HBM)
SEM = pl.BlockSpec(memory_space=pltpu.SEMAPHORE)


def _place():
    return lax.axis_index("x"), lax.axis_index("y"), lax.axis_index("c")


W_SLAB = 752
W_WIN = 272
PAD_BLK = 256


def _pad_runs():
    runs = sorted((pad, pad + b - a, p, a) for p in range(N_DEV) for a, b, pad in _shard_pieces(p))
    blocks = []
    for k in range(D_PAD // PAD_BLK):
        lo, hi = k * PAD_BLK, (k + 1) * PAD_BLK
        blocks.append([(max(lo, r0) - lo, min(hi, r1) - lo, p, a + max(lo, r0) - r0)
                       for r0, r1, p, a in runs if max(lo, r0) < min(hi, r1)])
    return blocks


def _gather_w_in(shard, x_in, g_attn, late_shards):
    blocks = _pad_runs()

    norm_t = 512
    n_chunks = S // norm_t

    n_late = len(late_shards)
    late_shapes = [t.shape for t in late_shards]

    def body(x_ref, xin_ref, g_ref, *refs):
        late_in, (out_ref, h1_ref), late_out = refs[:n_late], refs[n_late:n_late + 2], refs[n_late + 2:2 * n_late + 2]
        scratch = refs[2 * n_late + 2:]
        stage, land, xbuf = scratch[:3]
        late_stage, late_cast = scratch[3:3 + n_late], scratch[3 + n_late:3 + 2 * n_late]
        send_sems, recv_sems, load_sem, x_sems, late_sems, store_sems = scratch[3 + 2 * n_late:]
        x, y, c = _place()
        me, sibling = (x, y, c), (x, y, 1 - c)
        chips = [(1 - x, y), (x, 1 - y), (1 - x, 1 - y)]

        def slot(px, py, pc):
            return land.at[4 * px + 2 * py + pc]

        def copy(k, block, to):
            return pltpu.make_async_remote_copy(
                src_ref=slot(*block), dst_ref=slot(*block), send_sem=send_sems.at[k], recv_sem=recv_sems.at[k],
                device_id=to, device_id_type=MESH)

        load = pltpu.make_async_copy(x_ref, stage, load_sem)
        load.start()
        load.wait()
        land[4 * x + 2 * y + c] = stage[...].astype(BF16)
        sent = [copy(0, me, sibling)] + [copy(1 + j, me, (*chip, c)) for j, chip in enumerate(chips)]
        for cp in sent:
            cp.start()

        def x_load(i):
            return pltpu.make_async_copy(xin_ref.at[pl.ds(i * norm_t, norm_t)], xbuf.at[i % 2], x_sems.at[i % 2])

        late_loads = [pltpu.make_async_copy(late_in[i], late_stage[i], late_sems.at[i]) for i in range(n_late)]
        for ld in late_loads:
            ld.start()
        x_load(0).start()
        for i in range(n_chunks):
            if i + 1 < n_chunks:
                x_load(i + 1).start()
            x_load(i).wait()
            h1_ref[i * norm_t:(i + 1) * norm_t, :] = _rms_rows(xbuf[i % 2], g_ref[...]).astype(BF16)
        stores = []
        for i in range(n_late):
            late_loads[i].wait()
            late_cast[i][...] = late_stage[i][...].astype(BF16)
            stores.append(pltpu.make_async_copy(late_cast[i], late_out[i].at[4 * x + 2 * y + c], store_sems.at[i]))
            stores[-1].start()

        for j, chip in enumerate(chips):
            copy(1 + j, (*chip, c), me).wait_recv()
            sent.append(copy(4 + j, (*chip, c), sibling))
            sent[-1].start()
        copy(0, sibling, me).wait_recv()
        for j, chip in enumerate(chips):
            copy(4 + j, (*chip, 1 - c), me).wait_recv()
        for cp in sent:
            cp.wait_send()
        for st in stores:
            st.wait()

        row = lax.broadcasted_iota(jnp.int32, (PAD_BLK, D), 0)
        for k, runs in enumerate(blocks):
            out = jnp.zeros((PAD_BLK, D), F32)
            for o_lo, o_hi, p, a in runs:
                start = min(a // 16 * 16, W_SLAB - W_WIN)
                win = land[p, start:start + W_WIN, :].astype(F32)
                moved = pltpu.roll(win, (o_lo - (a - start)) % W_WIN, 0)[:PAD_BLK]
                out = jnp.where((row >= o_lo) & (row < o_hi), moved, out)
            out_ref[k * PAD_BLK:(k + 1) * PAD_BLK, :] = out.astype(BF16)

    return pl.pallas_call(
        body, name="all_gather_w_in",
        out_shape=[jax.ShapeDtypeStruct((D_PAD, D), BF16), jax.ShapeDtypeStruct((S, D), BF16)]
                  + [jax.ShapeDtypeStruct((N_DEV,) + sh, BF16) for sh in late_shapes],
        in_specs=[ANY, ANY, pl.BlockSpec(memory_space=pltpu.VMEM)] + [ANY] * n_late,
        out_specs=[pl.BlockSpec(memory_space=pltpu.VMEM)] * 2 + [ANY] * n_late,
        scratch_shapes=[pltpu.VMEM((W_SLAB, D), F32), pltpu.VMEM((N_DEV, W_SLAB, D), BF16),
                        pltpu.VMEM((2, norm_t, D), F32)]
                       + [pltpu.VMEM(sh, F32) for sh in late_shapes] + [pltpu.VMEM(sh, BF16) for sh in late_shapes]
                       + [pltpu.SemaphoreType.DMA((AG_COPIES,)), pltpu.SemaphoreType.DMA((AG_COPIES,)),
                          pltpu.SemaphoreType.DMA(()), pltpu.SemaphoreType.DMA((2,)),
                          pltpu.SemaphoreType.DMA((n_late,)), pltpu.SemaphoreType.DMA((n_late,))],
        compiler_params=_cparams(None),
    )(shard, x_in, g_attn, *late_shards)


PEER_COPIES = N_DEV - 1
SIBLING_COPIES = 4
CHIP_COPIES = 3


def _peer_copies(_, land_refs, send_sems, recv_sems):
    x, y, c = _place()
    mine = 4 * x + 2 * y + c
    copies = []
    for i, ref in enumerate(land_refs):
        for k in range(1, N_DEV):
            peer = (x ^ (k >> 2), y ^ ((k >> 1) & 1), c ^ (k & 1))
            n = i * PEER_COPIES + k - 1
            copies.append(pltpu.make_async_remote_copy(
                src_ref=ref.at[mine], dst_ref=ref.at[mine], send_sem=send_sems.at[n], recv_sem=recv_sems.at[n],
                device_id=peer, device_id_type=MESH))
    return copies


def _sibling_copies(g_refs, r_refs, send_sems, recv_sems):
    x, y, c = _place()
    return [pltpu.make_async_remote_copy(
        src_ref=g_refs[i].at[2 * k + (1 - c)], dst_ref=r_refs[i].at[k],
        send_sem=send_sems.at[SIBLING_COPIES * i + k], recv_sem=recv_sems.at[SIBLING_COPIES * i + k],
        device_id=(x, y, 1 - c), device_id_type=MESH) for i in range(len(g_refs)) for k in range(SIBLING_COPIES)]


def _sibling_half_copies(g_refs, r_refs, send_sems, recv_sems):
    x, y, c = _place()
    return [pltpu.make_async_remote_copy(
        src_ref=g_refs[i].at[k], dst_ref=r_refs[i].at[k],
        send_sem=send_sems.at[SIBLING_COPIES * i + k], recv_sem=recv_sems.at[SIBLING_COPIES * i + k],
        device_id=(x, y, 1 - c), device_id_type=MESH) for i in range(len(g_refs)) for k in range(SIBLING_COPIES)]


def _chip_copies(p_refs, r_refs, send_sems, recv_sems):
    x, y, c = _place()
    chips = [(1 - x, y), (x, 1 - y), (1 - x, 1 - y)]
    return [pltpu.make_async_remote_copy(
        src_ref=p_refs[i].at[2 * cx + cy], dst_ref=r_refs[i].at[j],
        send_sem=send_sems.at[CHIP_COPIES * i + j], recv_sem=recv_sems.at[CHIP_COPIES * i + j],
        device_id=(cx, cy, c), device_id_type=MESH) for i in range(len(p_refs)) for j, (cx, cy) in enumerate(chips)]


def _in_hbm(a):
    return pltpu.with_memory_space_constraint(a, pltpu.HBM)


def _exchange_start(name, copies_fn, n_copies, srcs, lands, after=()):
    n_s, n_l, n_a = len(srcs), len(lands), len(after)

    def body(*refs):
        outs = refs[n_s + n_l + n_a:]
        for cp in copies_fn(refs[:n_s], refs[n_s:n_s + n_l], outs[0], outs[1]):
            cp.start()
        outs[-1][...] = jnp.zeros_like(outs[-1])

    res = pl.pallas_call(
        body, name=name,
        out_shape=[pltpu.SemaphoreType.DMA((n_copies,)), pltpu.SemaphoreType.DMA((n_copies,))]
                  + [pltpu.HBM(t.shape, t.dtype) for t in (*srcs, *lands)] + [jax.ShapeDtypeStruct((8, LANES), F32)],
        in_specs=[HBM] * (n_s + n_l) + [ANY] * n_a,
        out_specs=[SEM, SEM] + [HBM] * (n_s + n_l) + [pl.BlockSpec(memory_space=pltpu.VMEM)],
        input_output_aliases={i: 2 + i for i in range(n_s + n_l)},
        compiler_params=pltpu.CompilerParams(has_side_effects=pltpu.SideEffectType.DATAFLOW_SIDE_EFFECTING),
    )(*[_in_hbm(t) for t in (*srcs, *lands)], *after)
    return res[0], res[1], list(res[2:2 + n_s]), list(res[2 + n_s:2 + n_s + n_l]), res[-1]


def _exchange_wait(name, copies_fn, started, after):
    send_sems, recv_sems, srcs, lands, _ = started
    n_s, n_l = len(srcs), len(lands)

    def body(*refs):
        for cp in copies_fn(refs[:n_s], refs[n_s:n_s + n_l], refs[n_s + n_l], refs[n_s + n_l + 1]):
            cp.wait_send()
            cp.wait_recv()

    res = pl.pallas_call(
        body, name=name,
        out_shape=[pltpu.HBM(t.shape, t.dtype) for t in (*srcs, *lands)],
        in_specs=[HBM] * (n_s + n_l) + [SEM, SEM, ANY],
        out_specs=[HBM] * (n_s + n_l),
        input_output_aliases={i: i for i in range(n_s + n_l)},
        compiler_params=pltpu.CompilerParams(has_side_effects=pltpu.SideEffectType.DATAFLOW_SIDE_EFFECTING),
    )(*srcs, *lands, send_sems, recv_sems, after)
    return list(res[:n_s]), list(res[n_s:])


def _shard_block(shape):
    rows, cols = shape
    if rows * cols <= 256 * 1024:
        return rows, cols
    if rows % 256:
        return rows, 256
    return 256, cols


def _chip_partial(ids, g_stack, recv1, name):
    shape = g_stack.shape[1:]
    br, bc = _shard_block(shape)
    whole = g_stack.shape[0] == N_DEV

    def body(ids_ref, g_ref, r_ref, pb_ref, own_ref):
        s = g_ref[...] + r_ref[...]
        pb_ref[...] = s.astype(BF16)

        @pl.when(pl.program_id(2) == ids_ref[1])
        def _():
            own_ref[...] = s

    grid_spec = pltpu.PrefetchScalarGridSpec(
        num_scalar_prefetch=1, grid=(shape[0] // br, shape[1] // bc, 4),
        in_specs=[pl.BlockSpec((None, br, bc), lambda r, q, k, ids: (2 * k + ids[0] if whole else k, r, q)),
                  pl.BlockSpec((None, br, bc), lambda r, q, k, ids: (k, r, q))],
        out_specs=[pl.BlockSpec((None, br, bc), lambda r, q, k, ids: (k, r, q)),
                   pl.BlockSpec((br, bc), lambda r, q, k, ids: (r, q))])
    return pl.pallas_call(
        body, name=name, grid_spec=grid_spec,
        out_shape=[jax.ShapeDtypeStruct((4,) + shape, BF16), jax.ShapeDtypeStruct(shape, F32)],
        compiler_params=_cparams(("parallel", "parallel", "arbitrary")),
    )(ids, g_stack, recv1)


def _adamw(w, g, m, v):
    m = ADAM_B1 * m + (1.0 - ADAM_B1) * g
    v = ADAM_B2 * v + (1.0 - ADAM_B2) * (g * g)
    m_hat = m / (1.0 - ADAM_B1 ** ADAM_STEP)
    v_hat = v / (1.0 - ADAM_B2 ** ADAM_STEP)
    delta = -ADAM_LR * (m_hat / (jnp.sqrt(v_hat) + ADAM_EPS) + ADAM_WD * w)
    return delta, m, v


def _reduce_adamw(own, recv2, w, m, v, name, deps=()):
    shape = own.shape
    br, bc = _shard_block(shape)

    def body(own_ref, r_ref, w_ref, m_ref, v_ref, *refs):
        g_ref, d_ref, nm_ref, nv_ref = refs[len(deps):]
        g = own_ref[...]
        for j in range(3):
            g = g + r_ref[j].astype(F32)
        delta, nm, nv = _adamw(w_ref[...], g, m_ref[...], v_ref[...])
        g_ref[...] = g
        d_ref[...] = delta
        nm_ref[...] = nm
        nv_ref[...] = nv

    blk = pl.BlockSpec((br, bc), lambda r, q: (r, q))
    return pl.pallas_call(
        body, name=name, grid=(shape[0] // br, shape[1] // bc),
        in_specs=[blk, pl.BlockSpec((3, br, bc), lambda r, q: (0, r, q)), blk, blk, blk] + [ANY] * len(deps),
        out_specs=[blk] * 4, out_shape=[jax.ShapeDtypeStruct(shape, F32)] * 4,
        compiler_params=_cparams(("parallel", "parallel")),
    )(own, recv2, w, m, v, *deps)


def _small_adamw(stack, w, m, v):
    def body(s_ref, w_ref, m_ref, v_ref, loss_ref, *o_refs):
        g = s_ref[0]
        for s in range(1, N_DEV):
            g = g + s_ref[s]
        delta, nm, nv = _adamw(w_ref[...], g, m_ref[...], v_ref[...])
        loss_ref[...] = g[LOSS_ROW:LOSS_ROW + 1, 0:1]
        for q, t in enumerate((g, delta, nm, nv)):
            attn_ref, b_ref, mlp_ref, final_ref = o_refs[4 * q:4 * q + 4]
            for i in range(8):
                cols = slice(i * LANES, (i + 1) * LANES)
                attn_ref[:, cols] = t[i:i + 1]
                mlp_ref[:, cols] = t[8 + i:9 + i]
                final_ref[:, cols] = t[16 + i:17 + i]
            b_ref[...] = t[24:25, :FOX_H]

    vm = pl.BlockSpec(memory_space=pltpu.VMEM)
    unpacked = [jax.ShapeDtypeStruct((1, n), F32) for n in (D, FOX_H, D, D)]
    return pl.pallas_call(
        body, name="small_adamw",
        in_specs=[vm] * 4, out_specs=[vm] * 17,
        out_shape=[jax.ShapeDtypeStruct((1, 1), F32)] + unpacked * 4,
    )(stack, w, m, v)


W_IN_SEGMENTS = ((0, 3 * FOX_W, SEG_A), (3 * FOX_W, 3 * FOX_W + FOX_H, SEG_F),
                 (3 * FOX_W + FOX_H, 3 * FOX_W + FOX_H + 3 * DIL_W, SEG_B), (3 * FOX_W + FOX_H + 3 * DIL_W, D_IN, SEG_G))


def _shard_pieces(p):
    rows = D_IN // N_DEV
    lo, hi = p * rows, (p + 1) * rows
    return [(max(lo, a) - lo, min(hi, b) - lo, pad + max(lo, a) - a)
            for a, b, pad in W_IN_SEGMENTS if max(lo, a) < min(hi, b)]


def _unpad_runs(p):
    rows = D_IN // N_DEV
    blocks = []
    for lo in range(0, rows, PAD_BLK):
        hi = min(lo + PAD_BLK, rows)
        blocks.append([(max(lo, a) - lo, min(hi, b) - lo, pad + max(lo, a) - a)
                       for a, b, pad in _shard_pieces(p) if max(lo, a) < min(hi, b)])
    return blocks


def _unpad_dw_in_t(dwp, ids, kept, deps=()):
    rows = D_IN // N_DEV
    plans = [_unpad_runs(p) for p in range(N_DEV)]
    windows = [[min(src // 8 * 8, D_PAD - W_WIN) for runs in plan for _, _, src in runs] for plan in plans]
    n_win = max(len(w) for w in windows)

    def body(ids_ref, src_ref, *refs):
        o_ref, buf, sems = refs[len(deps):]
        side = ids_ref[0] if kept else 1 - ids_ref[0]
        p = 2 * pl.program_id(0) + side
        row = lax.broadcasted_iota(jnp.int32, (PAD_BLK, D), 0)

        def fetch(q):
            slot = (q // 2) % 2
            return [pltpu.make_async_copy(src_ref.at[pl.ds(start, W_WIN)], buf.at[slot, n], sems.at[slot, n])
                    for n, start in enumerate(windows[q])]

        for q in range(N_DEV):
            @pl.when(p == q)
            def _(q=q):
                for nxt in ([q, q + 2] if q < 2 else [q + 2]):
                    if nxt < N_DEV:
                        for cp in fetch(nxt):
                            cp.start()
                copies, n = fetch(q), 0
                for j, runs in enumerate(plans[q]):
                    out = None
                    for o_lo, o_hi, src in runs:
                        copies[n].wait()
                        moved = pltpu.roll(buf[(q // 2) % 2, n], (o_lo - (src - windows[q][n])) % W_WIN, 0)[:PAD_BLK]
                        out = moved if out is None else jnp.where((row >= o_lo) & (row < o_hi), moved, out)
                        n += 1
                    size = min(PAD_BLK, rows - j * PAD_BLK)
                    o_ref[j * PAD_BLK:j * PAD_BLK + size, :] = out[:size]

    grid_spec = pltpu.PrefetchScalarGridSpec(
        num_scalar_prefetch=1, grid=(N_DEV // 2,),
        in_specs=[ANY] * (1 + len(deps)), out_specs=pl.BlockSpec((None, rows, D), lambda k, ids: (k, 0, 0)),
        scratch_shapes=[pltpu.VMEM((2, n_win, W_WIN, D), F32), pltpu.SemaphoreType.DMA((2, n_win))])
    return pl.pallas_call(
        body, name="unpad_dw_in_kept" if kept else "unpad_dw_in_sent", grid_spec=grid_spec,
        out_shape=jax.ShapeDtypeStruct((N_DEV // 2, rows, D), F32),
        compiler_params=_cparams(("arbitrary",)),
    )(ids, dwp, *deps)


LOSS_ROW = 25


def _pack_small(g_attn, b, g_mlp, g_final, loss_row=None):
    tail = jnp.pad(b, ((0, 7), (0, LANES - b.shape[1])))
    if loss_row is not None:
        tail = tail + jnp.pad(loss_row, ((LOSS_ROW - 24, 31 - LOSS_ROW), (0, 0)))
    return jnp.concatenate([g_attn.reshape(8, LANES), g_mlp.reshape(8, LANES), g_final.reshape(8, LANES), tail], axis=0)


class _StepComm:
    def __init__(self, ids, w_sh, m_sh, v_sh, lands, after_w_in):
        self.ids = ids
        self.w_sh, self.m_sh, self.v_sh = w_sh, m_sh, v_sh
        self.updates = None
        self.gather_mixer = _exchange_start("gather_mixer_start", _peer_copies, PEER_COPIES * 3, [], lands[:3],
                                            after=(after_w_in,))
        self.gather_mlp = _exchange_start("gather_mlp_start", _peer_copies, PEER_COPIES * 2, [], lands[3:],
                                          after=(self.gather_mixer[-1],))
        self.sibling = {}
        self.chips = {}
        self.own = {}
        self.names = {}
        self.copies_fn = {}
        self.kept = {}

    def _reduce_start(self, group, names, grads, after=(), copies_fn=_sibling_copies):
        lands = [lax.empty((SIBLING_COPIES,) + t.shape[1:], F32) for t in grads]
        self.sibling[group] = _exchange_start("grad_%s_sibling_start" % group, copies_fn,
                                              SIBLING_COPIES * len(grads), grads, lands, after=after)
        self.names[group] = names
        self.copies_fn[group] = copies_fn
        return self.sibling[group][-1]

    def _reduce_mid(self, group, after):
        grads, recv1 = _exchange_wait("grad_%s_sibling_wait" % group, self.copies_fn[group], self.sibling[group],
                                      after)
        grads = self.kept.get(group, grads)
        parts = [_chip_partial(self.ids, g, r, "grad_partial_" + n) for g, r, n in zip(grads, recv1, self.names[group])]
        self.own[group] = [p[1] for p in parts]
        srcs = [p[0] for p in parts]
        lands = [lax.empty((CHIP_COPIES,) + t.shape[1:], BF16) for t in srcs]
        self.chips[group] = _exchange_start("grad_%s_chips_start" % group, _chip_copies, CHIP_COPIES * len(srcs),
                                            srcs, lands)
        return self.chips[group][-1]

    def reduced(self, group, after):
        _, recv2 = _exchange_wait("grad_%s_chips_wait" % group, _chip_copies, self.chips[group], after)
        return list(zip(self.own[group], recv2))

    def first_deps(self):
        return [self.gather_mlp[-1]]

    def mixer_weights(self, after):
        _, (g_a, g_b, g_out) = _exchange_wait("gather_mixer_wait", _peer_copies, self.gather_mixer, after)
        return g_a, g_b, g_out.reshape(D, D)

    def mlp_weights(self, after):
        _, (g_up, g_down) = _exchange_wait("gather_mlp_wait", _peer_copies, self.gather_mlp, after)
        return g_up, g_down.reshape(DFF, D)

    def mlp_grads(self, dw_up_sh, dw_down):
        return [self._reduce_start("mlp", W_NAMES[4:], [dw_up_sh, dw_down.reshape((N_DEV,) + SHARD_SHAPES[5])])]

    def mixer_grads(self, dw_a, dw_b, dw_out):
        token = self._reduce_mid("mlp", dw_b)
        grads = [dw_a, dw_b, dw_out.reshape((N_DEV,) + SHARD_SHAPES[3])]
        return [self._reduce_start("mixer", W_NAMES[1:4], grads, after=(token,))]

    def mid_backward(self, after):
        return [self._reduce_mid("mixer", after)]

    def w_in_grad(self, dw_in_t):
        token = self._reduce_start("w_in", W_NAMES[:1], [_unpad_dw_in_t(dw_in_t, self.ids, kept=False)],
                                   copies_fn=_sibling_half_copies)
        self.kept["w_in"] = [_unpad_dw_in_t(dw_in_t, self.ids, kept=True, deps=[token])]
        reduced = self.reduced("mixer", token) + self.reduced("mlp", token)
        self.updates, last = [None] * len(reduced), self.kept["w_in"][0]
        for i in (3, 4, 0, 1, 2):
            if i == 4:
                last = self._reduce_mid("w_in", last)
            self.updates[i] = _reduce_adamw(*reduced[i], self.w_sh[1 + i], self.m_sh[1 + i], self.v_sh[1 + i],
                                            "adamw_" + W_NAMES[1 + i], deps=[last])
            last = self.updates[i][0]
        return [last]

    def w_in_update(self, after):
        (reduced,) = self.reduced("w_in", after)
        return _reduce_adamw(*reduced, self.w_sh[0], self.m_sh[0], self.v_sh[0], "adamw_" + W_NAMES[0])


def kernel(x, norm_attn_g, w_in, b_forget, w_branch_a, w_branch_b, w_out, norm_mlp_g, w_up, w_down, norm_final_g, loss_target, m_norm_attn_g, m_w_in, m_b_forget, m_w_branch_a, m_w_branch_b, m_w_out, m_norm_mlp_g, m_w_up, m_w_down, m_norm_final_g, v_norm_attn_g, v_w_in, v_b_forget, v_w_branch_a, v_w_branch_b, v_w_out, v_norm_mlp_g, v_w_up, v_w_down, v_norm_final_g):
    cx, cy, cc = _place()
    ids = jnp.stack([cc, 2 * cx + cy]).astype(jnp.int32)

    w_sh = [w_in[0].T] + [t[0] for t in (w_branch_a, w_branch_b, w_out, w_up, w_down)]
    m_sh = [m_w_in[0].T] + [t[0] for t in (m_w_branch_a, m_w_branch_b, m_w_out, m_w_up, m_w_down)]
    v_sh = [v_w_in[0].T] + [t[0] for t in (v_w_branch_a, v_w_branch_b, v_w_out, v_w_up, v_w_down)]

    w_in_t, h1, *lands = _gather_w_in(jnp.pad(w_sh[0], ((0, W_SLAB - w_sh[0].shape[0]), (0, 0))), x[0], norm_attn_g,
                                      w_sh[1:])
    comm = _StepComm(ids, w_sh, m_sh, v_sh, lands, w_in_t)
    b_pad = jnp.pad(b_forget, ((0, 0), (0, LANES - FOX_H)))

    loss_row, dx, dsmall = _local_step(
        x[0], loss_target[0], h1, norm_attn_g, norm_mlp_g, norm_final_g.reshape(1, D), b_pad, w_in_t, comm)

    dg_attn, db, dg_mlp, dg_final = dsmall
    stack = lax.dynamic_update_slice(jnp.zeros((N_DEV, SMALL_R, LANES), F32),
                                     _pack_small(dg_attn, db, dg_mlp, dg_final, loss_row)[None],
                                     (4 * cx + 2 * cy + cc, 0, 0))
    small_gather = _exchange_start("small_gather_start", _peer_copies, PEER_COPIES, [], [stack])
    w_in_update = comm.w_in_update(small_gather[-1])
    _, (stack,) = _exchange_wait("small_gather_wait", _peer_copies, small_gather, w_in_update[0])
    loss, *small = _small_adamw(
        stack,
        _pack_small(norm_attn_g, b_forget, norm_mlp_g, norm_final_g.reshape(1, D)),
        _pack_small(m_norm_attn_g, m_b_forget, m_norm_mlp_g, m_norm_final_g.reshape(1, D)),
        _pack_small(v_norm_attn_g, v_b_forget, v_norm_mlp_g, v_norm_final_g.reshape(1, D)))
    big = [w_in_update] + comm.updates

    outs = [loss.reshape(()), dx[None]]
    for q in range(4):
        s_attn, s_b, s_mlp, s_final = small[4 * q:4 * q + 4]
        s_final = s_final.reshape(D)
        b_in, b_a, b_b, b_out, b_up, b_down = [t[q][None] for t in big]
        b_in = jnp.swapaxes(b_in, 1, 2)
        outs += [s_attn, b_in, s_b, b_a, b_b, b_out, s_mlp, b_up, b_down, s_final]
    return tuple(outs)
```

```python
import functools

import jax
import jax.numpy as jnp
from jax import lax
from jax.experimental import pallas as pl
from jax.experimental.pallas import tpu as pltpu

F32 = jnp.float32
BF16 = jnp.bfloat16
MESH = pl.DeviceIdType.MESH

S = 2048
D = 1024
HD = 64
FOX_H = 8
FOX_W = FOX_H * HD
DIL_HG = 4
DIL_G = 3
DIL_W = DIL_G * DIL_HG * HD
DIL_OUT = DIL_HG * HD
DIL_BLK = 128
DIL_R = (1, 4, 16)
DFF = 4 * D
D_IN = 3 * FOX_W + FOX_H + 3 * DIL_W + 2 * D
EPS = 1e-6
NEG_INF = -1e30
SCALE = HD ** -0.5
ROPE_THETA = 500000.0
ROPE_DIM = HD // 4
N_DEV = 8

ADAM_LR = 0.001
ADAM_B1 = 0.9
ADAM_B2 = 0.999
ADAM_EPS = 1e-08
ADAM_WD = 0.01
ADAM_STEP = 10

LANES = 128
CB = 256
SEG_A = 0
SEG_B = 3 * FOX_W
SEG_Z = SEG_B + 3 * DIL_W
SEG_G = SEG_Z + CB
SEG_F = SEG_G + 2 * D
D_PAD = SEG_F + CB
DW_IN_TM = D_PAD // 10
SMALL_R = 32

VMEM_MB = 56


def _cparams(dims=None, vmem_mb=VMEM_MB, **kw):
    return pltpu.CompilerParams(dimension_semantics=dims, vmem_limit_bytes=vmem_mb << 20, **kw)


ANY = pl.BlockSpec(memory_space=pl.ANY)

_NN = (((1,), (0,)), ((), ()))
_NT = (((1,), (1,)), ((), ()))
_TN = (((0,), (0,)), ((), ()))


def _dot(a, b, dims):
    return lax.dot_general(a.astype(BF16), b.astype(BF16), dims, preferred_element_type=F32)


def _mm(a, b, *, mode, tm, tn, out_dtypes, name, n=None, k=None, a_off=0, b_off=0,
        b_sharded=False, out_sharded=False, epilogue=None, extras=(), deps=()):
    n_sh = b.shape[-1] if b_sharded else None
    if mode == "nn":
        m = a.shape[0]
        k = k or a.shape[1]
        a_spec = pl.BlockSpec((tm, k), lambda i, j: (i, a_off))
        if b_sharded:
            assert tn == n_sh
            n = N_DEV * n_sh
            b_spec = pl.BlockSpec((None, k, tn), lambda i, j: (j, 0, 0))
        else:
            n = n or b.shape[1]
            b_spec = pl.BlockSpec((k, tn), lambda i, j: (0, j + b_off))
        dims = _NN
    elif mode == "nt":
        m = a.shape[0]
        k = k or a.shape[1]
        a_spec = pl.BlockSpec((tm, k), lambda i, j: (i, a_off))
        if b_sharded:
            n = b.shape[1]
            b_spec = pl.BlockSpec((N_DEV, tn, n_sh), lambda i, j: (0, j, 0))
        else:
            n = n or b.shape[0]
            b_spec = pl.BlockSpec((tn, k), lambda i, j: (j + b_off, 0))
        dims = _NT
    else:
        k, m = a.shape
        n = n or b.shape[1]
        a_spec = pl.BlockSpec((k, tm), lambda i, j: (0, i))
        b_spec = pl.BlockSpec((k, tn), lambda i, j: (0, j + b_off))
        dims = _TN
    assert m % tm == 0 and n % tn == 0, (name, m, n, tm, tn)
    n_extra = len(extras)
    tile = pl.BlockSpec((tm, tn), lambda i, j: (i, j))
    split_tile = out_sharded and n == tn
    if out_sharded:
        assert mode == "tn" and n // tn in (1, N_DEV)
        if split_tile:
            out_spec = pl.BlockSpec((N_DEV, tm, n // N_DEV), lambda i, j: (0, i, 0))
        else:
            out_spec = pl.BlockSpec((None, tm, tn), lambda i, j: (j, i, 0))
        out_shape = (N_DEV, m, n // N_DEV)
    else:
        out_spec, out_shape = tile, (m, n)

    def body(a_ref, b_ref, *refs):
        if mode == "nt" and b_sharded:
            acc = _dot(a_ref[...], jnp.concatenate([b_ref[p] for p in range(N_DEV)], axis=1), dims)
        else:
            acc = _dot(a_ref[...], b_ref[...], dims)
        ex = [r[...] for r in refs[:n_extra]]
        outs = epilogue(acc, *ex) if epilogue is not None else (acc,)
        for o_ref, o in zip(refs[n_extra + len(deps):], outs):
            if split_tile:
                w = n // N_DEV
                for p in range(N_DEV):
                    o_ref[p] = o[:, p * w:(p + 1) * w].astype(o_ref.dtype)
            else:
                o_ref[...] = o.astype(o_ref.dtype)

    res = pl.pallas_call(
        body, name=name, grid=(m // tm, n // tn),
        in_specs=[a_spec, b_spec] + [tile] * n_extra + [ANY] * len(deps),
        out_specs=[out_spec] * len(out_dtypes),
        out_shape=[jax.ShapeDtypeStruct(out_shape, dt) for dt in out_dtypes],
        compiler_params=_cparams(("parallel", "parallel")),
    )(a, b, *extras, *deps)
    return res if len(out_dtypes) > 1 else res[0]


def _mm_rows(a, b, *, tm, name, epilogue, rows=(), vecs=(), row_out=(), vec_out=(), b_sharded=False, deps=()):
    m, k = a.shape
    n_rows, n_vecs, n_deps = len(rows), len(vecs), len(deps)
    n_sh = b.shape[-1] if b_sharded else None

    def body(a_ref, b_ref, *refs):
        row_refs, vec_refs = refs[:n_rows], refs[n_rows:n_rows + n_vecs]
        outs = refs[n_rows + n_vecs + n_deps:]
        if b_sharded:
            acc = _dot(a_ref[:, 0:n_sh], b_ref[0], _NT)
            for p in range(1, N_DEV):
                acc = acc + _dot(a_ref[:, p * n_sh:(p + 1) * n_sh], b_ref[p], _NT)
        else:
            acc = _dot(a_ref[...], b_ref[...], _NN)
        row_vals, vec_incs = epilogue(acc, [r[...] for r in row_refs], [v[...] for v in vec_refs])
        for o_ref, val in zip(outs[:len(row_out)], row_vals):
            o_ref[...] = val.astype(o_ref.dtype)

        @pl.when(pl.program_id(0) == 0)
        def _():
            for o_ref in outs[len(row_out):]:
                o_ref[...] = jnp.zeros_like(o_ref)

        for o_ref, inc in zip(outs[len(row_out):], vec_incs):
            o_ref[...] += inc

    tile = pl.BlockSpec((tm, D), lambda i: (i, 0))
    b_spec = pl.BlockSpec(b.shape, lambda i: (0,) * b.ndim)
    return pl.pallas_call(
        body, name=name, grid=(m // tm,),
        in_specs=[pl.BlockSpec((tm, k), lambda i: (i, 0)), b_spec] + [tile] * n_rows
                 + [pl.BlockSpec(v.shape, lambda i: (0, 0)) for v in vecs] + [ANY] * n_deps,
        out_specs=[tile] * len(row_out) + [pl.BlockSpec((1, w), lambda i: (0, 0)) for w in vec_out],
        out_shape=[jax.ShapeDtypeStruct((m, D), dt) for dt in row_out]
                  + [jax.ShapeDtypeStruct((1, w), F32) for w in vec_out],
        compiler_params=_cparams(("arbitrary",)),
    )(a, b, *rows, *vecs, *deps)


ROW_T = 256
ROWS_TM = 512


def _rms_rows(x, g):
    r = lax.rsqrt(jnp.mean(x * x, axis=-1, keepdims=True) + EPS)
    return (x * r) * g


def _rms_bwd_rows(dh, x, dres, g):
    r = lax.rsqrt(jnp.mean(x * x, axis=-1, keepdims=True) + EPS)
    xn = x * r
    dhn = dh * g
    dx = dres + r * (dhn - xn * jnp.mean(dhn * xn, axis=-1, keepdims=True))
    return dx, jnp.sum(dh * xn, axis=0, keepdims=True)


def _final_loss_rows(x, g, tgt):
    r = lax.rsqrt(jnp.mean(x * x, axis=-1, keepdims=True) + EPS)
    xn = x * r
    err = xn * g - tgt
    row_loss = jnp.mean(err * err, axis=-1, keepdims=True)
    loss = 0.5 * jnp.sum(row_loss, axis=0, keepdims=True) * jnp.ones((1, LANES), F32)
    dy = err * (1.0 / D)
    dyn = dy * g
    dx = r * (dyn - xn * jnp.mean(dyn * xn, axis=-1, keepdims=True))
    return dx, jnp.sum(dy * xn, axis=0, keepdims=True), loss


def _sigmoid(z):
    return 1.0 / (1.0 + jnp.exp(-z))


GATE_TR = 512


def _mixer_tail(oa, ob, w_a, w_b, w_out, proj_g, x, g_mlp):
    def body(oa_ref, ob_ref, wa_ref, wb_ref, wo_ref, g_ref, x_ref, gm_ref, ya_ref, yb_ref, mixed_ref, x2_ref, h2_ref):
        ya = _dot(oa_ref[...], jnp.concatenate([wa_ref[p] for p in range(N_DEV)], axis=1), _NN)
        yb = _dot(ob_ref[...], jnp.concatenate([wb_ref[p] for p in range(N_DEV)], axis=1), _NN)
        ya_ref[...] = ya
        yb_ref[...] = yb
        mixed = (_sigmoid(g_ref[:, :D]) * ya + _sigmoid(g_ref[:, D:]) * yb).astype(BF16)
        mixed_ref[...] = mixed
        x2 = x_ref[...] + _dot(mixed, wo_ref[...], _NN)
        x2_ref[...] = x2
        h2_ref[...] = _rms_rows(x2, gm_ref[...]).astype(BF16)

    def rows(width):
        return pl.BlockSpec((GATE_TR, width), lambda i: (i, 0))

    def whole(t):
        return pl.BlockSpec(t.shape, lambda i: (0,) * t.ndim)

    return pl.pallas_call(
        body, name="mixer_tail", grid=(S // GATE_TR,),
        in_specs=[rows(FOX_W), rows(DIL_OUT), whole(w_a), whole(w_b), whole(w_out), rows(2 * D), rows(D), whole(g_mlp)],
        out_specs=[rows(D)] * 5,
        out_shape=[jax.ShapeDtypeStruct((S, D), dt) for dt in (F32, F32, BF16, F32, BF16)],
        compiler_params=_cparams(("parallel",)),
    )(oa, ob, w_a, w_b, w_out, proj_g, x, g_mlp)


def _out_proj_gate_bwd(dx2, w_out, proj_g, ya, yb):
    def body(dx_ref, w_ref, g_ref, ya_ref, yb_ref, dy_ref, dg_ref):
        dm = _dot(dx_ref[...], w_ref[...], _NT)
        for half, y_ref in enumerate((ya_ref, yb_ref)):
            cols = slice(half * D, (half + 1) * D)
            s = _sigmoid(g_ref[:, cols])
            dy_ref[:, cols] = (dm * s).astype(BF16)
            dg_ref[:, cols] = (dm * y_ref[...] * (s * (1.0 - s))).astype(BF16)

    row = pl.BlockSpec((GATE_TR, D), lambda i: (i, 0))
    wide = pl.BlockSpec((GATE_TR, 2 * D), lambda i: (i, 0))
    return pl.pallas_call(
        body, name="out_proj_gate_bwd", grid=(S // GATE_TR,),
        in_specs=[row, pl.BlockSpec((D, D), lambda i: (0, 0)), wide, row, row],
        out_specs=[wide, pl.BlockSpec((GATE_TR, 2 * D), lambda i: (i, SEG_G // (2 * D)))],
        out_shape=[jax.ShapeDtypeStruct((S, 2 * D), BF16), jax.ShapeDtypeStruct((S, D_PAD), BF16)],
        compiler_params=_cparams(("parallel",)),
    )(dx2, w_out, proj_g, ya, yb)


FOX_TQ = 512
FOX_TQ_FWD = 1024


def _scan_rows(x, reverse):
    n = x.shape[0]
    row = lax.broadcasted_iota(jnp.int32, x.shape, 0)
    k = 1
    while k < n:
        if reverse:
            x = x + jnp.where(row < n - k, pltpu.roll(x, n - k, 0), 0.0)
        else:
            x = x + jnp.where(row >= k, pltpu.roll(x, k, 0), 0.0)
        k *= 2
    return x


def _fox_dscan(dft, dfs, proj_f, b_pad, dproj):
    def body(dft_ref, dfs_ref, f_ref, b_ref, _, dfa_ref, db_ref):
        dfs_pad = jnp.concatenate([dfs_ref[...], jnp.zeros((LANES - FOX_H, S), F32)], axis=0)
        df = dfs_pad.T + dft_ref[0]
        for hp in range(1, dft_ref.shape[0]):
            df = df + pltpu.roll(dft_ref[hp], 2 * hp, 1)
        dlf = _scan_rows(df, reverse=True)
        z = f_ref[...] + b_ref[...]
        lane = lax.broadcasted_iota(jnp.int32, (S, LANES), 1)
        dfa = jnp.where(lane < FOX_H, dlf / (1.0 + jnp.exp(z)), 0.0)
        dfa_ref[:, :LANES] = dfa.astype(BF16)
        dfa_ref[:, LANES:] = jnp.zeros((S, CB - LANES), BF16)
        db_ref[...] = jnp.sum(dfa, axis=0, keepdims=True)

    return pl.pallas_call(
        body, name="fox_dscan", grid=(1,),
        in_specs=[pl.BlockSpec(dft.shape, lambda i: (0, 0, 0)), pl.BlockSpec((FOX_H, S), lambda i: (0, 0)),
                  pl.BlockSpec((S, LANES), lambda i: (0, 0)), pl.BlockSpec((1, LANES), lambda i: (0, 0)), ANY],
        out_specs=[pl.BlockSpec((S, CB), lambda i: (0, SEG_F // CB)), pl.BlockSpec((1, LANES), lambda i: (0, 0))],
        out_shape=[jax.ShapeDtypeStruct((S, D_PAD), BF16), jax.ShapeDtypeStruct((1, LANES), F32)],
        input_output_aliases={4: 0},
        compiler_params=_cparams(("arbitrary",)),
    )(dft, dfs, proj_f, b_pad, dproj)


FOX_NQ = S // FOX_TQ
FOX_HP = FOX_W // LANES
HEADS_PER_LB = LANES // HD
FOX_AUG = 2 * LANES


def _fox_prepare(proj_a, proj_f, b_pad):
    tr = 512

    def body(q_ref, k_ref, f_ref, b_ref, qa_ref, ka_ref, hi_s, mid_s, lo_s):
        i = pl.program_id(0)

        @pl.when(i == 0)
        def _():
            z = f_ref[...] + b_ref[...]
            lf = jnp.minimum(z, 0.0) - jnp.log1p(jnp.exp(-jnp.abs(z)))
            f = _scan_rows(lf, reverse=False)
            hi = f.astype(BF16).astype(F32)
            r1 = f - hi
            mid = r1.astype(BF16).astype(F32)
            hi_s[...] = hi
            mid_s[...] = mid
            lo_s[...] = (r1 - mid).astype(BF16).astype(F32)

        rows = pl.ds(pl.multiple_of(i * tr, tr), tr)
        hi, mid, lo = hi_s[rows, :], mid_s[rows, :], lo_s[rows, :]
        lane = lax.broadcasted_iota(jnp.int32, (tr, HD), 1)
        q_ones = jnp.where((lane >= 3) & (lane < 6), 1.0, 0.0)
        k_ones = jnp.where(lane < 3, 1.0, 0.0)
        for h in range(FOX_H):
            a1, a2, a3 = hi[:, h:h + 1], mid[:, h:h + 1], lo[:, h:h + 1]
            q_extra = jnp.where(lane == 0, a1, jnp.where(lane == 1, a2, jnp.where(lane == 2, a3, q_ones)))
            k_extra = jnp.where(lane == 3, -a1, jnp.where(lane == 4, -a2, jnp.where(lane == 5, -a3, k_ones)))
            base = h * LANES
            qa_ref[:, base:base + HD] = q_ref[:, h * HD:(h + 1) * HD] * jnp.asarray(SCALE, BF16)
            qa_ref[:, base + HD:base + LANES] = q_extra.astype(BF16)
            ka_ref[:, base:base + HD] = k_ref[:, h * HD:(h + 1) * HD]
            ka_ref[:, base + HD:base + LANES] = k_extra.astype(BF16)

    out_blk = pl.BlockSpec((tr, FOX_H * LANES), lambda i: (i, 0))
    return pl.pallas_call(
        body, name="fox_prepare", grid=(S // tr,),
        in_specs=[pl.BlockSpec((tr, FOX_W), lambda i: (i, 0)), pl.BlockSpec((tr, FOX_W), lambda i: (i, 1)),
                  pl.BlockSpec((S, LANES), lambda i: (0, 0)), pl.BlockSpec((1, LANES), lambda i: (0, 0))],
        out_specs=[out_blk, out_blk],
        out_shape=[jax.ShapeDtypeStruct((S, FOX_H * LANES), BF16)] * 2,
        scratch_shapes=[pltpu.VMEM((S, LANES), F32)] * 3,
        compiler_params=_cparams(("arbitrary",)),
    )(proj_a, proj_a, proj_f, b_pad)


def _fox_scores(qa, ka, w, tq):
    s = _dot(qa, ka, _NT)
    row = lax.broadcasted_iota(jnp.int32, (tq, tq), 0)
    col = lax.broadcasted_iota(jnp.int32, (tq, tq), 1)
    diag = jnp.where(col <= row, s[:, w * tq:], NEG_INF)
    return diag if w == 0 else jnp.concatenate([s[:, :w * tq], diag], axis=1)


def _fox_fwd(q_aug, k_aug, proj_a):
    tq = FOX_TQ_FWD

    def body(qa_ref, ka_ref, v_ref, o_ref, lse_ref):
        qi = pl.program_id(1)
        lane = lax.broadcasted_iota(jnp.int32, (tq, LANES), 1)
        for w in range(S // tq):
            @pl.when(qi == w)
            def _(w=w):
                width = (w + 1) * tq
                lse = jnp.zeros((tq, LANES), F32)
                for hh in range(HEADS_PER_LB):
                    aug = slice(hh * LANES, (hh + 1) * LANES)
                    sl = slice(hh * HD, (hh + 1) * HD)
                    s = _fox_scores(qa_ref[:, aug], ka_ref[:width, aug], w, tq)
                    m = jnp.max(s, axis=-1, keepdims=True)
                    p = jnp.exp(s - m)
                    l = jnp.sum(p, axis=-1, keepdims=True)
                    o_ref[:, sl] = _dot(p / l, v_ref[:width, sl], _NN)
                    lse = jnp.where(lane == hh, m + jnp.log(l), lse)
                lse_ref[...] = lse

    return pl.pallas_call(
        body, name="fox_fwd", grid=(FOX_HP, S // tq),
        in_specs=[pl.BlockSpec((tq, FOX_AUG), lambda h, i: (i, h)),
                  pl.BlockSpec((S, FOX_AUG), lambda h, i: (0, h)),
                  pl.BlockSpec((S, LANES), lambda h, i: (0, 2 * FOX_HP + h))],
        out_specs=[pl.BlockSpec((tq, LANES), lambda h, i: (i, h)),
                   pl.BlockSpec((None, tq, LANES), lambda h, i: (h, i, 0))],
        out_shape=[jax.ShapeDtypeStruct((S, FOX_W), F32), jax.ShapeDtypeStruct((FOX_HP, S, LANES), F32)],
        compiler_params=_cparams(("parallel", "parallel")),
    )(q_aug, k_aug, proj_a)


def _fox_bwd(q_aug, k_aug, proj_a, lse, do, dproj, deps=()):
    n_q = FOX_NQ

    def body(qa_ref, ka_ref, v_ref, lse_ref, do_ref, *refs):
        dproj_ref, dft_ref, dfs_ref, dk_acc, dv_acc, dq_buf, kv_buf, dq_sems, kv_sems = refs[1 + len(deps):]
        hp = pl.program_id(0)
        t = pl.program_id(1)
        slot = t % 2
        col = pl.multiple_of(hp * LANES, LANES)

        def dq_copy(step, buf_slot):
            rows = pl.ds(pl.multiple_of(step * FOX_TQ, FOX_TQ), FOX_TQ)
            return pltpu.make_async_copy(dq_buf.at[buf_slot], dproj_ref.at[rows, pl.ds(col, LANES)], dq_sems.at[buf_slot])

        @pl.when(t == 0)
        def _():
            dk_acc[...] = jnp.zeros_like(dk_acc)
            dv_acc[...] = jnp.zeros_like(dv_acc)

        @pl.when((t == 0) & (hp == 0))
        def _():
            dfs_ref[...] = jnp.zeros_like(dfs_ref)

        @pl.when(t >= 2)
        def _():
            dq_copy(t - 2, slot).wait()

        lane = lax.broadcasted_iota(jnp.int32, (FOX_TQ, LANES), 1)
        for w in range(n_q):
            @pl.when(t == w)
            def _(w=w):
                width = (w + 1) * FOX_TQ
                lse_all = lse_ref[...]
                sub = lax.broadcasted_iota(jnp.int32, (FOX_H, width), 0)
                dft = jnp.zeros((FOX_TQ, LANES), F32)
                for hh in range(HEADS_PER_LB):
                    aug = slice(hh * LANES, (hh + 1) * LANES)
                    sl = slice(hh * HD, (hh + 1) * HD)
                    head = hp * HEADS_PER_LB + hh
                    qa = qa_ref[:, aug]
                    ka = ka_ref[:width, aug]
                    do_h = do_ref[:, sl]
                    s = _fox_scores(qa, ka, w, FOX_TQ)
                    p = jnp.exp(s - lse_all[:, hh:hh + 1])
                    dp = _dot(do_h, v_ref[:width, sl], _NT)
                    ds = p * (dp - jnp.sum(dp * p, axis=-1, keepdims=True))
                    dft = jnp.where(lane == hh, jnp.sum(ds, axis=-1, keepdims=True), dft)
                    dfs_ref[:, :width] -= jnp.where(sub == head, jnp.sum(ds, axis=0, keepdims=True), 0.0)
                    dq_buf[slot, :, sl] = (_dot(ds, ka[:, :HD], _NN) * SCALE).astype(BF16)
                    dk_acc[:width, sl] += _dot(ds, qa[:, :HD], _TN)
                    dv_acc[:width, sl] += _dot(p, do_h, _TN)
                dft_ref[...] = dft

        dq_copy(t, slot).start()

        @pl.when(t == n_q - 1)
        def _():
            dq_copy(t - 1, 1 - slot).wait()
            dq_copy(t, slot).wait()
            kv_buf[0] = dk_acc[...].astype(BF16)
            kv_buf[1] = dv_acc[...].astype(BF16)
            copies = [pltpu.make_async_copy(
                kv_buf.at[j], dproj_ref.at[:, pl.ds(pl.multiple_of((j + 1) * FOX_W + hp * LANES, LANES), LANES)],
                kv_sems.at[j]) for j in range(2)]
            for cp in copies:
                cp.start()
            for cp in copies:
                cp.wait()

    qblk = pl.BlockSpec((FOX_TQ, FOX_AUG), lambda h, t: (t, h))
    lane_blk = pl.BlockSpec((None, FOX_TQ, LANES), lambda h, t: (h, t, 0))
    return pl.pallas_call(
        body, name="fox_bwd", grid=(FOX_HP, n_q),
        in_specs=[qblk, pl.BlockSpec((S, FOX_AUG), lambda h, t: (0, h)),
                  pl.BlockSpec((S, LANES), lambda h, t: (0, 2 * FOX_HP + h)),
                  lane_blk, pl.BlockSpec((FOX_TQ, LANES), lambda h, t: (t, h)), ANY] + [ANY] * len(deps),
        out_specs=[ANY, lane_blk, pl.BlockSpec((FOX_H, S), lambda h, t: (0, 0))],
        out_shape=[jax.ShapeDtypeStruct((S, D_PAD), BF16),
                   jax.ShapeDtypeStruct((FOX_HP, S, LANES), F32), jax.ShapeDtypeStruct((FOX_H, S), F32)],
        scratch_shapes=[pltpu.VMEM((S, LANES), F32), pltpu.VMEM((S, LANES), F32),
                        pltpu.VMEM((2, FOX_TQ, LANES), BF16), pltpu.VMEM((2, S, LANES), BF16),
                        pltpu.SemaphoreType.DMA((2,)), pltpu.SemaphoreType.DMA((2,))],
        input_output_aliases={5: 0},
        compiler_params=_cparams(("arbitrary", "arbitrary")),
    )(q_aug, k_aug, proj_a, lse, do, dproj, *deps)


def _rope_tables():
    half = ROPE_DIM // 2
    shape = (DIL_G, S, half)
    g = lax.broadcasted_iota(jnp.int32, shape, 0)
    row = lax.broadcasted_iota(jnp.int32, shape, 1)
    r = jnp.left_shift(1, 2 * g)
    per_class = S // r
    pos = (row % per_class) * r + row // per_class
    inv_freq = jnp.power(jnp.float32(ROPE_THETA), -lax.broadcasted_iota(F32, shape, 2) * 2.0 / ROPE_DIM)
    ang = pos.astype(F32) * inv_freq
    return jnp.concatenate([jnp.cos(ang), jnp.sin(ang)], axis=-1)


def _rope_factors(cs):
    half = ROPE_DIM // 2
    i = lax.broadcasted_iota(jnp.int32, (2 * half, 2 * LANES), 0)
    l = lax.broadcasted_iota(jnp.int32, (2 * half, 2 * LANES), 1)
    hit = ((l < LANES) & (i == l % half)) | ((l >= LANES) & (i - half == l % half))
    spread = jnp.where(hit, 1.0, 0.0).astype(BF16)
    hi = cs.astype(BF16)
    r1 = cs - hi.astype(F32)
    mid = r1.astype(BF16)
    lo = (r1 - mid.astype(F32)).astype(BF16)
    full = (_dot(hi, spread, _NN) + _dot(mid, spread, _NN)) + _dot(lo, spread, _NN)
    cos, sin = full[:, :LANES], full[:, LANES:]
    lane = lax.broadcasted_iota(jnp.int32, cos.shape, 1) % HD
    t_self = jnp.where(lane < ROPE_DIM, cos, 1.0)
    t_up = jnp.where(lane < half, -sin, 0.0)
    t_dn = jnp.where((lane >= half) & (lane < ROPE_DIM), sin, 0.0)
    return t_self, t_up, t_dn


def _residue_pieces(r):
    if r == 1:
        return [(slice(i, i + 512), slice(i, i + 512)) for i in range(0, S, 512)]
    n = S // r
    return [(pl.ds(j, n, stride=r), slice(j * n, (j + 1) * n)) for j in range(r)]


def _rope_apply(x, a, u, d, backward):
    half = ROPE_DIM // 2
    if backward:
        return x * a + pltpu.roll(x * u, half, 1) + pltpu.roll(x * d, LANES - half, 1)
    return x * a + pltpu.roll(x, LANES - half, 1) * u + pltpu.roll(x, half, 1) * d


def _rope_cache_factors(cs_ref, fac, first_step_of_group):
    @pl.when(first_step_of_group)
    def _():
        for lo in range(0, S, 512):
            for j, f in enumerate(_rope_factors(cs_ref[lo:lo + 512, :])):
                fac[j, lo:lo + 512, :] = f


LB_PER_CB = CB // LANES
ROPE_NB = 3 * DIL_G * LB_PER_CB


def _rope_step(s):
    per_group = 3 * LB_PER_CB
    return s // per_group, (s % per_group) // LB_PER_CB, s % LB_PER_CB


def _rope_split(proj_b, tables):
    def body(x_ref, cs_ref, o_ref, fac):
        g, t, hf = _rope_step(pl.program_id(0))
        _rope_cache_factors(cs_ref, fac, (t == 0) & (hf == 0))
        for gs in range(DIL_G):
            @pl.when(g == gs)
            def _(gs=gs):
                for tok, sub in _residue_pieces(DIL_R[gs]):
                    x = x_ref[tok, :]
                    y = _rope_apply(x, fac[0, sub, :], fac[1, sub, :], fac[2, sub, :], False)
                    o_ref[sub, :] = jnp.where(t < 2, y, x).astype(BF16)

    def in_index(s):
        g, t, hf = _rope_step(s)
        return 0, (t * DIL_G + g) * LB_PER_CB + hf

    def out_index(s):
        g, t, hf = _rope_step(s)
        return t * DIL_G + g, 0, hf

    tab = pl.BlockSpec((None, S, 2 * (ROPE_DIM // 2)), lambda s: (_rope_step(s)[0], 0, 0))
    return pl.pallas_call(
        body, name="rope_split", grid=(ROPE_NB,),
        in_specs=[pl.BlockSpec((S, LANES), in_index), tab],
        out_specs=pl.BlockSpec((None, S, LANES), out_index),
        out_shape=jax.ShapeDtypeStruct((3 * DIL_G, S, CB), BF16),
        scratch_shapes=[pltpu.VMEM((3, S, LANES), F32)],
        compiler_params=_cparams(("arbitrary",)),
    )(proj_b, tables)


def _rope_merge_bwd(dq3, dk3, dv3, tables, dproj):
    def body(dq_ref, dk_ref, dv_ref, cs_ref, _, o_ref, tmp, fac):
        s = pl.program_id(0)
        g, t, hf = _rope_step(s)
        _rope_cache_factors(cs_ref, fac, (t == 0) & (hf == 0) & (s < ROPE_NB))
        for gs in range(DIL_G):
            @pl.when((g == gs) & (s < ROPE_NB))
            def _(gs=gs):
                for tok, sub in _residue_pieces(DIL_R[gs]):
                    x = jnp.where(t == 0, dq_ref[sub, :], jnp.where(t == 1, dk_ref[sub, :], dv_ref[sub, :]))
                    y = _rope_apply(x, fac[0, sub, :], fac[1, sub, :], fac[2, sub, :], True)
                    tmp[tok, :] = jnp.where(t < 2, y, x)
                o_ref[...] = tmp[...].astype(BF16)

        @pl.when(s >= ROPE_NB)
        def _():
            o_ref[...] = jnp.zeros_like(o_ref)

    def src_spec(own):
        def index(s):
            g, t, hf = _rope_step(jnp.minimum(s, ROPE_NB - 1))
            return g, 0, jnp.where(t == own, hf, jnp.where(t > own, LB_PER_CB - 1, 0))
        return pl.BlockSpec((None, S, LANES), index)

    def out_index(s):
        g, t, hf = _rope_step(s)
        col = SEG_B // LANES + (t * DIL_G + g) * LB_PER_CB + hf
        return 0, jnp.where(s < ROPE_NB, col, SEG_Z // LANES + s - ROPE_NB)

    tab = pl.BlockSpec((None, S, 2 * (ROPE_DIM // 2)), lambda s: (_rope_step(jnp.minimum(s, ROPE_NB - 1))[0], 0, 0))
    return pl.pallas_call(
        body, name="rope_merge_bwd", grid=(ROPE_NB + LB_PER_CB,),
        in_specs=[src_spec(0), src_spec(1), src_spec(2), tab, ANY],
        out_specs=pl.BlockSpec((S, LANES), out_index),
        out_shape=jax.ShapeDtypeStruct((S, D_PAD), BF16),
        scratch_shapes=[pltpu.VMEM((S, LANES), F32), pltpu.VMEM((3, S, LANES), F32)],
        input_output_aliases={4: 0},
        compiler_params=_cparams(("arbitrary",)),
    )(dq3, dk3, dv3, tables, dproj)


DIL_NB = S // DIL_BLK


_BQK = (((2,), (2,)), ((0,), (0,)))
_BQD = (((2,), (1,)), ((0,), (0,)))
_BKD = (((1,), (1,)), ((0,), (0,)))


def _dil_mask(g):
    shape = (DIL_NB, DIL_BLK, 2 * DIL_BLK)
    blocks_per_seq = jnp.right_shift(DIL_NB, 2 * g)
    has_prev = jnp.bitwise_and(lax.broadcasted_iota(jnp.int32, shape, 0), blocks_per_seq - 1) != 0
    a = lax.broadcasted_iota(jnp.int32, shape, 1)
    kk = lax.broadcasted_iota(jnp.int32, shape, 2)
    diff = DIL_BLK + a - kk
    return (diff >= 0) & (diff <= DIL_BLK) & ((kk >= DIL_BLK) | has_prev)


def _blocks(t):
    return t.reshape(DIL_NB, DIL_BLK, t.shape[-1])


def _with_prev(t):
    prev = jnp.concatenate([jnp.zeros((DIL_BLK, t.shape[-1]), t.dtype), t[:-DIL_BLK]], axis=0)
    return jnp.concatenate([_blocks(prev), _blocks(t)], axis=1)


def _fold_prev(t2):
    n = t2.shape[-1]
    to_prev = t2[:, :DIL_BLK].reshape(S, n)
    own = t2[:, DIL_BLK:].reshape(S, n)
    return own + jnp.concatenate([to_prev[DIL_BLK:], jnp.zeros((DIL_BLK, n), t2.dtype)], axis=0)


def _dil_group_spec(t, width=DIL_OUT):
    return pl.BlockSpec((None, S, width), lambda g: (t * DIL_G + g, 0, 0))


def _dil_fwd(qkv3):
    def body(q_ref, k_ref, v_ref, o_ref, lse_ref):
        ok = _dil_mask(pl.program_id(0))
        lane = lax.broadcasted_iota(jnp.int32, (S, LANES), 1)
        lse = jnp.zeros((S, LANES), F32)
        for h in range(DIL_HG):
            sl = slice(h * HD, (h + 1) * HD)
            s = jnp.where(ok, _dot(_blocks(q_ref[:, sl]), _with_prev(k_ref[:, sl]), _BQK) * SCALE, NEG_INF)
            m = jnp.max(s, axis=-1, keepdims=True)
            p = jnp.exp(s - m)
            l = jnp.sum(p, axis=-1, keepdims=True)
            o_ref[:, sl] = _dot(p / l, _with_prev(v_ref[:, sl]), _BQD).reshape(S, HD)
            lse = jnp.where(lane == h, (m + jnp.log(l)).reshape(S, 1), lse)
        lse_ref[...] = lse

    return pl.pallas_call(
        body, name="dil_fwd", grid=(DIL_G,),
        in_specs=[_dil_group_spec(0), _dil_group_spec(1), _dil_group_spec(2)],
        out_specs=[_dil_group_spec(0), _dil_group_spec(0, LANES)],
        out_shape=[jax.ShapeDtypeStruct((DIL_G, S, DIL_OUT), F32), jax.ShapeDtypeStruct((DIL_G, S, LANES), F32)],
        compiler_params=_cparams(("parallel",)),
    )(qkv3, qkv3, qkv3)


def _dil_bwd(qkv3, lse3, do3, c3):
    def body(q_ref, k_ref, v_ref, lse_ref, do_ref, c_ref, dq_ref, dk_ref, dv_ref):
        ok = _dil_mask(pl.program_id(0))
        lse = lse_ref[...]
        c = c_ref[...]
        for h in range(DIL_HG):
            sl = slice(h * HD, (h + 1) * HD)
            q = _blocks(q_ref[:, sl])
            k2 = _with_prev(k_ref[:, sl])
            do_h = _blocks(do_ref[:, sl])
            s = jnp.where(ok, _dot(q, k2, _BQK) * SCALE, NEG_INF)
            p = jnp.exp(s - _blocks(lse[:, h:h + 1]))
            dp = _dot(do_h, _with_prev(v_ref[:, sl]), _BQK)
            ds = p * (dp - _blocks(c[:, h:h + 1]))
            dsb = (ds * SCALE).astype(BF16)
            dq_ref[:, sl] = _dot(dsb, k2, _BQD).reshape(S, HD)
            dk_ref[:, sl] = _fold_prev(_dot(dsb, q, _BKD))
            dv_ref[:, sl] = _fold_prev(_dot(p, do_h, _BKD))

    return pl.pallas_call(
        body, name="dil_bwd", grid=(DIL_G,),
        in_specs=[_dil_group_spec(0), _dil_group_spec(1), _dil_group_spec(2), _dil_group_spec(0, LANES),
                  _dil_group_spec(0), _dil_group_spec(0, LANES)],
        out_specs=[_dil_group_spec(0)] * 3,
        out_shape=[jax.ShapeDtypeStruct((DIL_G, S, DIL_OUT), F32)] * 3,
        compiler_params=_cparams(("parallel",)),
    )(qkv3, qkv3, qkv3, lse3, do3, c3)


COMB_T = 256


def _dil_combine(o3, lse3):
    heads_per_lb = LANES // HD

    def body(o_ref, lse_ref, ob_ref, al_ref, *scratch):
        o_tok = [scratch[LB_PER_CB * gg:LB_PER_CB * (gg + 1)] for gg in range(DIL_G)]
        lse_tok = scratch[LB_PER_CB * DIL_G:]
        g = pl.program_id(0)
        for gs in range(DIL_G):
            @pl.when(g == gs)
            def _(gs=gs):
                for tok, sub in _residue_pieces(DIL_R[gs]):
                    for hf in range(LB_PER_CB):
                        o_tok[gs][hf][tok, :] = o_ref[sub, hf * LANES:(hf + 1) * LANES]
                    lse_tok[gs][tok, :] = lse_ref[sub, :]

        @pl.when(g == DIL_G - 1)
        def _():
            def chunk(i, carry):
                rows = pl.ds(pl.multiple_of(i * COMB_T, COMB_T), COMB_T)
                lse = [lse_tok[gg][rows, :] for gg in range(DIL_G)]
                m = jnp.maximum(jnp.maximum(lse[0], lse[1]), lse[2])
                e = [jnp.exp(lse[gg] - m) for gg in range(DIL_G)]
                den = (e[0] + e[1]) + e[2]
                al = [e[gg] / den for gg in range(DIL_G)]
                for gg in range(DIL_G):
                    al_ref[gg, rows, :] = al[gg]
                for h in range(DIL_HG):
                    hf, sl = h // heads_per_lb, slice((h % heads_per_lb) * HD, (h % heads_per_lb + 1) * HD)
                    acc = al[0][:, h:h + 1] * o_tok[0][hf][rows, sl]
                    for gg in range(1, DIL_G):
                        acc = acc + al[gg][:, h:h + 1] * o_tok[gg][hf][rows, sl]
                    ob_ref[rows, h * HD:(h + 1) * HD] = acc
                return carry

            lax.fori_loop(0, S // COMB_T, chunk, 0)

    return pl.pallas_call(
        body, name="dil_combine", grid=(DIL_G,),
        in_specs=[pl.BlockSpec((None, S, DIL_OUT), lambda g: (g, 0, 0)),
                  pl.BlockSpec((None, S, LANES), lambda g: (g, 0, 0))],
        out_specs=[pl.BlockSpec((S, DIL_OUT), lambda g: (0, 0)),
                   pl.BlockSpec((DIL_G, S, LANES), lambda g: (0, 0, 0))],
        out_shape=[jax.ShapeDtypeStruct((S, DIL_OUT), F32), jax.ShapeDtypeStruct((DIL_G, S, LANES), F32)],
        scratch_shapes=[pltpu.VMEM((S, LANES), F32)] * (DIL_G * (LB_PER_CB + 1)),
        compiler_params=_cparams(("arbitrary",)),
    )(o3, lse3)


def _dil_combine_bwd(dob, ob, alpha, deps=()):
    heads_per_lb = LANES // HD

    def body(dob_ref, ob_ref, al_ref, *refs):
        do_ref, c_ref = refs[len(deps):]
        g = pl.program_id(0)
        hf = pl.program_id(1)
        for gs in range(DIL_G):
            @pl.when(g == gs)
            def _(gs=gs):
                for tok, sub in _residue_pieces(DIL_R[gs]):
                    dob = dob_ref[tok, :]
                    prod = dob * ob_ref[tok, :]
                    al = al_ref[tok, :]
                    lane = lax.broadcasted_iota(jnp.int32, al.shape, 1)
                    c = jnp.where(hf == 0, 0.0, c_ref[sub, :])
                    for hh in range(heads_per_lb):
                        sl = slice(hh * HD, (hh + 1) * HD)
                        head = hf * heads_per_lb + hh
                        a = jnp.sum(jnp.where(lane == head, al, 0.0), axis=-1, keepdims=True)
                        do_ref[sub, sl] = (a * dob[:, sl]).astype(BF16)
                        c = jnp.where(lane == head, a * jnp.sum(prod[:, sl], axis=-1, keepdims=True), c)
                    c_ref[sub, :] = c

    half = pl.BlockSpec((S, LANES), lambda g, hf: (0, hf))
    return pl.pallas_call(
        body, name="dil_combine_bwd", grid=(DIL_G, LB_PER_CB),
        in_specs=[half, half, pl.BlockSpec((None, S, LANES), lambda g, hf: (g, 0, 0))] + [ANY] * len(deps),
        out_specs=[pl.BlockSpec((None, S, LANES), lambda g, hf: (g, 0, hf)),
                   pl.BlockSpec((None, S, LANES), lambda g, hf: (g, 0, 0))],
        out_shape=[jax.ShapeDtypeStruct((DIL_G, S, DIL_OUT), BF16), jax.ShapeDtypeStruct((DIL_G, S, LANES), F32)],
        compiler_params=_cparams(("parallel", "arbitrary")),
    )(dob, ob, alpha, *deps)


def _local_step(x, tgt, h1, g_attn, g_mlp, g_final, b_pad, w_in_t, hooks):
    tables = _rope_tables()

    first = hooks.first_deps()
    proj_a = _mm(h1, w_in_t, mode="nt", tm=S, tn=512, n=3 * FOX_W, out_dtypes=[BF16], name="proj_a", deps=first)
    proj_b = _mm(h1, w_in_t, mode="nt", tm=S, tn=DIL_W, n=3 * DIL_W, b_off=SEG_B // DIL_W, out_dtypes=[F32], name="proj_b",
                deps=first)
    proj_g = _mm(h1, w_in_t, mode="nt", tm=S, tn=D, n=2 * D, b_off=SEG_G // D, out_dtypes=[F32], name="proj_g",
                deps=first)
    proj_f = _mm(h1, w_in_t, mode="nt", tm=S, tn=LANES, n=LANES, b_off=SEG_F // LANES, out_dtypes=[F32], name="proj_f",
                deps=first)

    q_aug, k_aug = _fox_prepare(proj_a, proj_f, b_pad)
    oa, lse_a = _fox_fwd(q_aug, k_aug, proj_a)

    qkv3 = _rope_split(proj_b, tables)
    o3, lse3 = _dil_fwd(qkv3)
    ob, alpha = _dil_combine(o3, lse3)

    w_a, w_b, w_out = hooks.mixer_weights(ob)
    ya, yb, mixed, x2, h2 = _mixer_tail(oa, ob, w_a, w_b, w_out, proj_g, x, g_mlp)
    w_up_sh, w_down = hooks.mlp_weights(h2)

    def up_epilogue(acc):
        r = jnp.maximum(acc, 0.0)
        return acc, r * r

    u, act = _mm(h2, w_up_sh, mode="nn", tm=S, tn=DFF // N_DEV, b_sharded=True, out_dtypes=[F32, BF16],
                 name="mlp_up", epilogue=up_epilogue)

    def loss_epilogue(acc, rows, vecs):
        dx3, dg, loss = _final_loss_rows(rows[0] + acc, vecs[0], rows[1])
        return (dx3,), (dg, loss)

    dx3, dg_final, loss = _mm_rows(act, w_down, tm=ROWS_TM, name="mlp_down_loss", epilogue=loss_epilogue,
                                   rows=(x2, tgt), vecs=(g_final,), row_out=(F32,), vec_out=(D, LANES))

    def rms_bwd_epilogue(acc, rows, vecs):
        dx, dg = _rms_bwd_rows(acc, rows[0], rows[1], vecs[0])
        return (dx,), (dg,)

    du = _mm(dx3, w_down, mode="nt", tm=S, tn=512, out_dtypes=[BF16], name="mlp_down_bwd",
             epilogue=lambda acc, u_t: (acc * (2.0 * jnp.maximum(u_t, 0.0)),), extras=(u,))
    dw_down = _mm(act, dx3, mode="tn", tm=512, tn=D, out_dtypes=[F32], name="dw_down")
    dw_up_sh = _mm(h2, du, mode="tn", tm=D, tn=DFF // N_DEV, out_sharded=True, out_dtypes=[F32], name="dw_up")
    dx2, dg_mlp = _mm_rows(du, w_up_sh, b_sharded=True, tm=ROWS_TM, name="mlp_up_bwd", epilogue=rms_bwd_epilogue,
                           rows=(x2, dx3), vecs=(g_mlp,), row_out=(F32,), vec_out=(D,),
                           deps=hooks.mlp_grads(dw_up_sh, dw_down))

    dw_out = _mm(mixed, dx2, mode="tn", tm=D, tn=D, out_dtypes=[F32], name="dw_out")

    dyab, dproj = _out_proj_gate_bwd(dx2, w_out, proj_g, ya, yb)
    doa = _mm(dyab, w_a, mode="nt", tm=S, tn=FOX_W, k=D, a_off=0, b_sharded=True, out_dtypes=[BF16],
              name="branch_a_bwd")
    dw_a = _mm(oa, dyab, mode="tn", tm=FOX_W, tn=D, n=D, b_off=0, out_sharded=True, out_dtypes=[F32],
               name="dw_branch_a")
    dob = _mm(dyab, w_b, mode="nt", tm=S, tn=CB, k=D, a_off=1, b_sharded=True, out_dtypes=[F32],
              name="branch_b_bwd")
    dw_b = _mm(ob, dyab, mode="tn", tm=DIL_OUT, tn=D, n=D, b_off=1, out_sharded=True, out_dtypes=[F32],
               name="dw_branch_b")

    dproj, dft, dfs = _fox_bwd(q_aug, k_aug, proj_a, lse_a, doa, dproj, deps=hooks.mixer_grads(dw_a, dw_b, dw_out))
    dproj, db = _fox_dscan(dft, dfs, proj_f, b_pad, dproj)

    do3, c3 = _dil_combine_bwd(dob, ob, alpha, deps=hooks.mid_backward(db))
    dq3, dk3, dv3 = _dil_bwd(qkv3, lse3, do3, c3)
    dproj = _rope_merge_bwd(dq3, dk3, dv3, tables, dproj)

    dw_in_t = _mm(dproj, h1, mode="tn", tm=DW_IN_TM, tn=D, out_dtypes=[F32], name="dw_in")
    dx, dg_attn = _mm_rows(dproj, w_in_t, tm=ROWS_TM // 2, name="proj_bwd", epilogue=rms_bwd_epilogue,
                           rows=(x, dx2), vecs=(g_attn,), row_out=(F32,), vec_out=(D,),
                           deps=hooks.w_in_grad(dw_in_t))

    return loss, dx, (dg_attn, db, dg_mlp, dg_final)


SHARD_SHAPES = ((D_IN // N_DEV, D), (FOX_W, D // N_DEV), (DIL_OUT, D // N_DEV), (D // N_DEV, D),
                (D, DFF // N_DEV), (DFF // N_DEV, D))
W_NAMES = ("w_in", "w_a", "w_b", "w_out", "w_up", "w_down")
AG_COPIES = 7
HBM = pl.BlockSpec(memory_space=pltpu.HBM)
SEM = pl.BlockSpec(memory_space=pltpu.SEMAPHORE)


def _place():
    return lax.axis_index("x"), lax.axis_index("y"), lax.axis_index("c")


W_SLAB = 752
W_WIN = 272
PAD_BLK = 256


def _pad_runs():
    runs = sorted((pad, pad + b - a, p, a) for p in range(N_DEV) for a, b, pad in _shard_pieces(p))
    blocks = []
    for k in range(D_PAD // PAD_BLK):
        lo, hi = k * PAD_BLK, (k + 1) * PAD_BLK
        blocks.append([(max(lo, r0) - lo, min(hi, r1) - lo, p, a + max(lo, r0) - r0)
                       for r0, r1, p, a in runs if max(lo, r0) < min(hi, r1)])
    return blocks


def _gather_w_in(shard, x_in, g_attn, late_shards):
    blocks = _pad_runs()

    norm_t = 512
    n_chunks = S // norm_t

    n_late = len(late_shards)
    late_shapes = [t.shape for t in late_shards]

    def body(x_ref, xin_ref, g_ref, *refs):
        late_in, (out_ref, h1_ref), late_out = refs[:n_late], refs[n_late:n_late + 2], refs[n_late + 2:2 * n_late + 2]
        scratch = refs[2 * n_late + 2:]
        stage, land, xbuf = scratch[:3]
        late_stage, late_cast = scratch[3:3 + n_late], scratch[3 + n_late:3 + 2 * n_late]
        send_sems, recv_sems, load_sem, x_sems, late_sems, store_sems = scratch[3 + 2 * n_late:]
        x, y, c = _place()
        me, sibling = (x, y, c), (x, y, 1 - c)
        chips = [(1 - x, y), (x, 1 - y), (1 - x, 1 - y)]

        def slot(px, py, pc):
            return land.at[4 * px + 2 * py + pc]

        def copy(k, block, to):
            return pltpu.make_async_remote_copy(
                src_ref=slot(*block), dst_ref=slot(*block), send_sem=send_sems.at[k], recv_sem=recv_sems.at[k],
                device_id=to, device_id_type=MESH)

        load = pltpu.make_async_copy(x_ref, stage, load_sem)
        load.start()
        load.wait()
        land[4 * x + 2 * y + c] = stage[...].astype(BF16)
        sent = [copy(0, me, sibling)] + [copy(1 + j, me, (*chip, c)) for j, chip in enumerate(chips)]
        for cp in sent:
            cp.start()

        def x_load(i):
            return pltpu.make_async_copy(xin_ref.at[pl.ds(i * norm_t, norm_t)], xbuf.at[i % 2], x_sems.at[i % 2])

        late_loads = [pltpu.make_async_copy(late_in[i], late_stage[i], late_sems.at[i]) for i in range(n_late)]
        for ld in late_loads:
            ld.start()
        x_load(0).start()
        for i in range(n_chunks):
            if i + 1 < n_chunks:
                x_load(i + 1).start()
            x_load(i).wait()
            h1_ref[i * norm_t:(i + 1) * norm_t, :] = _rms_rows(xbuf[i % 2], g_ref[...]).astype(BF16)
        stores = []
        for i in range(n_late):
            late_loads[i].wait()
            late_cast[i][...] = late_stage[i][...].astype(BF16)
            stores.append(pltpu.make_async_copy(late_cast[i], late_out[i].at[4 * x + 2 * y + c], store_sems.at[i]))
            stores[-1].start()

        for j, chip in enumerate(chips):
            copy(1 + j, (*chip, c), me).wait_recv()
            sent.append(copy(4 + j, (*chip, c), sibling))
            sent[-1].start()
        copy(0, sibling, me).wait_recv()
        for j, chip in enumerate(chips):
            copy(4 + j, (*chip, 1 - c), me).wait_recv()
        for cp in sent:
            cp.wait_send()
        for st in stores:
            st.wait()

        row = lax.broadcasted_iota(jnp.int32, (PAD_BLK, D), 0)
        for k, runs in enumerate(blocks):
            out = jnp.zeros((PAD_BLK, D), F32)
            for o_lo, o_hi, p, a in runs:
                start = min(a // 16 * 16, W_SLAB - W_WIN)
                win = land[p, start:start + W_WIN, :].astype(F32)
                moved = pltpu.roll(win, (o_lo - (a - start)) % W_WIN, 0)[:PAD_BLK]
                out = jnp.where((row >= o_lo) & (row < o_hi), moved, out)
            out_ref[k * PAD_BLK:(k + 1) * PAD_BLK, :] = out.astype(BF16)

    return pl.pallas_call(
        body, name="all_gather_w_in",
        out_shape=[jax.ShapeDtypeStruct((D_PAD, D), BF16), jax.ShapeDtypeStruct((S, D), BF16)]
                  + [jax.ShapeDtypeStruct((N_DEV,) + sh, BF16) for sh in late_shapes],
        in_specs=[ANY, ANY, pl.BlockSpec(memory_space=pltpu.VMEM)] + [ANY] * n_late,
        out_specs=[pl.BlockSpec(memory_space=pltpu.VMEM)] * 2 + [ANY] * n_late,
        scratch_shapes=[pltpu.VMEM((W_SLAB, D), F32), pltpu.VMEM((N_DEV, W_SLAB, D), BF16),
                        pltpu.VMEM((2, norm_t, D), F32)]
                       + [pltpu.VMEM(sh, F32) for sh in late_shapes] + [pltpu.VMEM(sh, BF16) for sh in late_shapes]
                       + [pltpu.SemaphoreType.DMA((AG_COPIES,)), pltpu.SemaphoreType.DMA((AG_COPIES,)),
                          pltpu.SemaphoreType.DMA(()), pltpu.SemaphoreType.DMA((2,)),
                          pltpu.SemaphoreType.DMA((n_late,)), pltpu.SemaphoreType.DMA((n_late,))],
        compiler_params=_cparams(None),
    )(shard, x_in, g_attn, *late_shards)


PEER_COPIES = N_DEV - 1
SIBLING_COPIES = 4
CHIP_COPIES = 3


def _peer_copies(_, land_refs, send_sems, recv_sems):
    x, y, c = _place()
    mine = 4 * x + 2 * y + c
    copies = []
    for i, ref in enumerate(land_refs):
        for k in range(1, N_DEV):
            peer = (x ^ (k >> 2), y ^ ((k >> 1) & 1), c ^ (k & 1))
            n = i * PEER_COPIES + k - 1
            copies.append(pltpu.make_async_remote_copy(
                src_ref=ref.at[mine], dst_ref=ref.at[mine], send_sem=send_sems.at[n], recv_sem=recv_sems.at[n],
                device_id=peer, device_id_type=MESH))
    return copies


def _sibling_copies(g_refs, r_refs, send_sems, recv_sems):
    x, y, c = _place()
    return [pltpu.make_async_remote_copy(
        src_ref=g_refs[i].at[2 * k + (1 - c)], dst_ref=r_refs[i].at[k],
        send_sem=send_sems.at[SIBLING_COPIES * i + k], recv_sem=recv_sems.at[SIBLING_COPIES * i + k],
        device_id=(x, y, 1 - c), device_id_type=MESH) for i in range(len(g_refs)) for k in range(SIBLING_COPIES)]


def _sibling_half_copies(g_refs, r_refs, send_sems, recv_sems):
    x, y, c = _place()
    return [pltpu.make_async_remote_copy(
        src_ref=g_refs[i].at[k], dst_ref=r_refs[i].at[k],
        send_sem=send_sems.at[SIBLING_COPIES * i + k], recv_sem=recv_sems.at[SIBLING_COPIES * i + k],
        device_id=(x, y, 1 - c), device_id_type=MESH) for i in range(len(g_refs)) for k in range(SIBLING_COPIES)]


def _chip_copies(p_refs, r_refs, send_sems, recv_sems):
    x, y, c = _place()
    chips = [(1 - x, y), (x, 1 - y), (1 - x, 1 - y)]
    return [pltpu.make_async_remote_copy(
        src_ref=p_refs[i].at[2 * cx + cy], dst_ref=r_refs[i].at[j],
        send_sem=send_sems.at[CHIP_COPIES * i + j], recv_sem=recv_sems.at[CHIP_COPIES * i + j],
        device_id=(cx, cy, c), device_id_type=MESH) for i in range(len(p_refs)) for j, (cx, cy) in enumerate(chips)]


def _in_hbm(a):
    return pltpu.with_memory_space_constraint(a, pltpu.HBM)


def _exchange_start(name, copies_fn, n_copies, srcs, lands, after=()):
    n_s, n_l, n_a = len(srcs), len(lands), len(after)

    def body(*refs):
        outs = refs[n_s + n_l + n_a:]
        for cp in copies_fn(refs[:n_s], refs[n_s:n_s + n_l], outs[0], outs[1]):
            cp.start()
        outs[-1][...] = jnp.zeros_like(outs[-1])

    res = pl.pallas_call(
        body, name=name,
        out_shape=[pltpu.SemaphoreType.DMA((n_copies,)), pltpu.SemaphoreType.DMA((n_copies,))]
                  + [pltpu.HBM(t.shape, t.dtype) for t in (*srcs, *lands)] + [jax.ShapeDtypeStruct((8, LANES), F32)],
        in_specs=[HBM] * (n_s + n_l) + [ANY] * n_a,
        out_specs=[SEM, SEM] + [HBM] * (n_s + n_l) + [pl.BlockSpec(memory_space=pltpu.VMEM)],
        input_output_aliases={i: 2 + i for i in range(n_s + n_l)},
        compiler_params=pltpu.CompilerParams(has_side_effects=pltpu.SideEffectType.DATAFLOW_SIDE_EFFECTING),
    )(*[_in_hbm(t) for t in (*srcs, *lands)], *after)
    return res[0], res[1], list(res[2:2 + n_s]), list(res[2 + n_s:2 + n_s + n_l]), res[-1]


def _exchange_wait(name, copies_fn, started, after):
    send_sems, recv_sems, srcs, lands, _ = started
    n_s, n_l = len(srcs), len(lands)

    def body(*refs):
        for cp in copies_fn(refs[:n_s], refs[n_s:n_s + n_l], refs[n_s + n_l], refs[n_s + n_l + 1]):
            cp.wait_send()
            cp.wait_recv()

    res = pl.pallas_call(
        body, name=name,
        out_shape=[pltpu.HBM(t.shape, t.dtype) for t in (*srcs, *lands)],
        in_specs=[HBM] * (n_s + n_l) + [SEM, SEM, ANY],
        out_specs=[HBM] * (n_s + n_l),
        input_output_aliases={i: i for i in range(n_s + n_l)},
        compiler_params=pltpu.CompilerParams(has_side_effects=pltpu.SideEffectType.DATAFLOW_SIDE_EFFECTING),
    )(*srcs, *lands, send_sems, recv_sems, after)
    return list(res[:n_s]), list(res[n_s:])


def _shard_block(shape):
    rows, cols = shape
    if rows * cols <= 256 * 1024:
        return rows, cols
    if rows % 256:
        return rows, 512
    return max(256, 256 * 1024 // cols), cols


def _chip_partial(ids, g_stack, recv1, name):
    shape = g_stack.shape[1:]
    br, bc = _shard_block(shape)
    whole = g_stack.shape[0] == N_DEV

    def body(ids_ref, g_ref, r_ref, pb_ref, own_ref):
        s = g_ref[...] + r_ref[...]
        pb_ref[...] = s.astype(BF16)

        @pl.when(pl.program_id(2) == ids_ref[1])
        def _():
            own_ref[...] = s

    grid_spec = pltpu.PrefetchScalarGridSpec(
        num_scalar_prefetch=1, grid=(shape[0] // br, shape[1] // bc, 4),
        in_specs=[pl.BlockSpec((None, br, bc), lambda r, q, k, ids: (2 * k + ids[0] if whole else k, r, q)),
                  pl.BlockSpec((None, br, bc), lambda r, q, k, ids: (k, r, q))],
        out_specs=[pl.BlockSpec((None, br, bc), lambda r, q, k, ids: (k, r, q)),
                   pl.BlockSpec((br, bc), lambda r, q, k, ids: (r, q))])
    return pl.pallas_call(
        body, name=name, grid_spec=grid_spec,
        out_shape=[jax.ShapeDtypeStruct((4,) + shape, BF16), jax.ShapeDtypeStruct(shape, F32)],
        compiler_params=_cparams(("parallel", "parallel", "arbitrary")),
    )(ids, g_stack, recv1)


def _adamw(w, g, m, v):
    m = ADAM_B1 * m + (1.0 - ADAM_B1) * g
    v = ADAM_B2 * v + (1.0 - ADAM_B2) * (g * g)
    m_hat = m / (1.0 - ADAM_B1 ** ADAM_STEP)
    v_hat = v / (1.0 - ADAM_B2 ** ADAM_STEP)
    delta = -ADAM_LR * (m_hat / (jnp.sqrt(v_hat) + ADAM_EPS) + ADAM_WD * w)
    return delta, m, v


def _reduce_adamw(own, recv2, w, m, v, name, deps=()):
    shape = own.shape
    br, bc = _shard_block(shape)

    def body(own_ref, r_ref, w_ref, m_ref, v_ref, *refs):
        g_ref, d_ref, nm_ref, nv_ref = refs[len(deps):]
        g = own_ref[...]
        for j in range(3):
            g = g + r_ref[j].astype(F32)
        delta, nm, nv = _adamw(w_ref[...], g, m_ref[...], v_ref[...])
        g_ref[...] = g
        d_ref[...] = delta
        nm_ref[...] = nm
        nv_ref[...] = nv

    blk = pl.BlockSpec((br, bc), lambda r, q: (r, q))
    return pl.pallas_call(
        body, name=name, grid=(shape[0] // br, shape[1] // bc),
        in_specs=[blk, pl.BlockSpec((3, br, bc), lambda r, q: (0, r, q)), blk, blk, blk] + [ANY] * len(deps),
        out_specs=[blk] * 4, out_shape=[jax.ShapeDtypeStruct(shape, F32)] * 4,
        compiler_params=_cparams(("parallel", "parallel")),
    )(own, recv2, w, m, v, *deps)


def _small_adamw(stack, w, m, v):
    def body(s_ref, w_ref, m_ref, v_ref, loss_ref, *o_refs):
        g = s_ref[0]
        for s in range(1, N_DEV):
            g = g + s_ref[s]
        delta, nm, nv = _adamw(w_ref[...], g, m_ref[...], v_ref[...])
        loss_ref[...] = g[LOSS_ROW:LOSS_ROW + 1, 0:1]
        for q, t in enumerate((g, delta, nm, nv)):
            attn_ref, b_ref, mlp_ref, final_ref = o_refs[4 * q:4 * q + 4]
            for i in range(8):
                cols = slice(i * LANES, (i + 1) * LANES)
                attn_ref[:, cols] = t[i:i + 1]
                mlp_ref[:, cols] = t[8 + i:9 + i]
                final_ref[:, cols] = t[16 + i:17 + i]
            b_ref[...] = t[24:25, :FOX_H]

    vm = pl.BlockSpec(memory_space=pltpu.VMEM)
    unpacked = [jax.ShapeDtypeStruct((1, n), F32) for n in (D, FOX_H, D, D)]
    return pl.pallas_call(
        body, name="small_adamw",
        in_specs=[vm] * 4, out_specs=[vm] * 17,
        out_shape=[jax.ShapeDtypeStruct((1, 1), F32)] + unpacked * 4,
    )(stack, w, m, v)


W_IN_SEGMENTS = ((0, 3 * FOX_W, SEG_A), (3 * FOX_W, 3 * FOX_W + FOX_H, SEG_F),
                 (3 * FOX_W + FOX_H, 3 * FOX_W + FOX_H + 3 * DIL_W, SEG_B), (3 * FOX_W + FOX_H + 3 * DIL_W, D_IN, SEG_G))


def _shard_pieces(p):
    rows = D_IN // N_DEV
    lo, hi = p * rows, (p + 1) * rows
    return [(max(lo, a) - lo, min(hi, b) - lo, pad + max(lo, a) - a)
            for a, b, pad in W_IN_SEGMENTS if max(lo, a) < min(hi, b)]


def _unpad_runs(p):
    rows = D_IN // N_DEV
    blocks = []
    for lo in range(0, rows, PAD_BLK):
        hi = min(lo + PAD_BLK, rows)
        blocks.append([(max(lo, a) - lo, min(hi, b) - lo, pad + max(lo, a) - a)
                       for a, b, pad in _shard_pieces(p) if max(lo, a) < min(hi, b)])
    return blocks


def _unpad_dw_in_t(dwp, ids, kept, deps=()):
    rows = D_IN // N_DEV
    plans = [_unpad_runs(p) for p in range(N_DEV)]
    windows = [[min(src // 8 * 8, D_PAD - W_WIN) for runs in plan for _, _, src in runs] for plan in plans]
    n_win = max(len(w) for w in windows)

    def body(ids_ref, src_ref, *refs):
        o_ref, buf, sems = refs[len(deps):]
        side = ids_ref[0] if kept else 1 - ids_ref[0]
        p = 2 * pl.program_id(0) + side
        row = lax.broadcasted_iota(jnp.int32, (PAD_BLK, D), 0)

        def fetch(q):
            slot = (q // 2) % 2
            return [pltpu.make_async_copy(src_ref.at[pl.ds(start, W_WIN)], buf.at[slot, n], sems.at[slot, n])
                    for n, start in enumerate(windows[q])]

        for q in range(N_DEV):
            @pl.when(p == q)
            def _(q=q):
                for nxt in ([q, q + 2] if q < 2 else [q + 2]):
                    if nxt < N_DEV:
                        for cp in fetch(nxt):
                            cp.start()
                copies, n = fetch(q), 0
                for j, runs in enumerate(plans[q]):
                    out = None
                    for o_lo, o_hi, src in runs:
                        copies[n].wait()
                        moved = pltpu.roll(buf[(q // 2) % 2, n], (o_lo - (src - windows[q][n])) % W_WIN, 0)[:PAD_BLK]
                        out = moved if out is None else jnp.where((row >= o_lo) & (row < o_hi), moved, out)
                        n += 1
                    size = min(PAD_BLK, rows - j * PAD_BLK)
                    o_ref[j * PAD_BLK:j * PAD_BLK + size, :] = out[:size]

    grid_spec = pltpu.PrefetchScalarGridSpec(
        num_scalar_prefetch=1, grid=(N_DEV // 2,),
        in_specs=[ANY] * (1 + len(deps)), out_specs=pl.BlockSpec((None, rows, D), lambda k, ids: (k, 0, 0)),
        scratch_shapes=[pltpu.VMEM((2, n_win, W_WIN, D), F32), pltpu.SemaphoreType.DMA((2, n_win))])
    return pl.pallas_call(
        body, name="unpad_dw_in_kept" if kept else "unpad_dw_in_sent", grid_spec=grid_spec,
        out_shape=jax.ShapeDtypeStruct((N_DEV // 2, rows, D), F32),
        compiler_params=_cparams(("arbitrary",)),
    )(ids, dwp, *deps)


LOSS_ROW = 25


def _pack_small(g_attn, b, g_mlp, g_final, loss_row=None):
    tail = jnp.pad(b, ((0, 7), (0, LANES - b.shape[1])))
    if loss_row is not None:
        tail = tail + jnp.pad(loss_row, ((LOSS_ROW - 24, 31 - LOSS_ROW), (0, 0)))
    return jnp.concatenate([g_attn.reshape(8, LANES), g_mlp.reshape(8, LANES), g_final.reshape(8, LANES), tail], axis=0)


class _StepComm:
    def __init__(self, ids, w_sh, m_sh, v_sh, lands, after_w_in):
        self.ids = ids
        self.w_sh, self.m_sh, self.v_sh = w_sh, m_sh, v_sh
        self.updates = None
        self.gather_mixer = _exchange_start("gather_mixer_start", _peer_copies, PEER_COPIES * 3, [], lands[:3],
                                            after=(after_w_in,))
        self.gather_mlp = _exchange_start("gather_mlp_start", _peer_copies, PEER_COPIES * 2, [], lands[3:],
                                          after=(self.gather_mixer[-1],))
        self.sibling = {}
        self.chips = {}
        self.own = {}
        self.names = {}
        self.copies_fn = {}
        self.kept = {}

    def _reduce_start(self, group, names, grads, after=(), copies_fn=_sibling_copies):
        lands = [lax.empty((SIBLING_COPIES,) + t.shape[1:], F32) for t in grads]
        self.sibling[group] = _exchange_start("grad_%s_sibling_start" % group, copies_fn,
                                              SIBLING_COPIES * len(grads), grads, lands, after=after)
        self.names[group] = names
        self.copies_fn[group] = copies_fn
        return self.sibling[group][-1]

    def _reduce_mid(self, group, after):
        grads, recv1 = _exchange_wait("grad_%s_sibling_wait" % group, self.copies_fn[group], self.sibling[group],
                                      after)
        grads = self.kept.get(group, grads)
        parts = [_chip_partial(self.ids, g, r, "grad_partial_" + n) for g, r, n in zip(grads, recv1, self.names[group])]
        self.own[group] = [p[1] for p in parts]
        srcs = [p[0] for p in parts]
        lands = [lax.empty((CHIP_COPIES,) + t.shape[1:], BF16) for t in srcs]
        self.chips[group] = _exchange_start("grad_%s_chips_start" % group, _chip_copies, CHIP_COPIES * len(srcs),
                                            srcs, lands)
        return self.chips[group][-1]

    def reduced(self, group, after):
        _, recv2 = _exchange_wait("grad_%s_chips_wait" % group, _chip_copies, self.chips[group], after)
        return list(zip(self.own[group], recv2))

    def first_deps(self):
        return [self.gather_mlp[-1]]

    def mixer_weights(self, after):
        _, (g_a, g_b, g_out) = _exchange_wait("gather_mixer_wait", _peer_copies, self.gather_mixer, after)
        return g_a, g_b, g_out.reshape(D, D)

    def mlp_weights(self, after):
        _, (g_up, g_down) = _exchange_wait("gather_mlp_wait", _peer_copies, self.gather_mlp, after)
        return g_up, g_down.reshape(DFF, D)

    def mlp_grads(self, dw_up_sh, dw_down):
        return [self._reduce_start("mlp", W_NAMES[4:], [dw_up_sh, dw_down.reshape((N_DEV,) + SHARD_SHAPES[5])])]

    def mixer_grads(self, dw_a, dw_b, dw_out):
        token = self._reduce_mid("mlp", dw_b)
        grads = [dw_a, dw_b, dw_out.reshape((N_DEV,) + SHARD_SHAPES[3])]
        return [self._reduce_start("mixer", W_NAMES[1:4], grads, after=(token,))]

    def mid_backward(self, after):
        return [self._reduce_mid("mixer", after)]

    def w_in_grad(self, dw_in_t):
        token = self._reduce_start("w_in", W_NAMES[:1], [_unpad_dw_in_t(dw_in_t, self.ids, kept=False)],
                                   copies_fn=_sibling_half_copies)
        self.kept["w_in"] = [_unpad_dw_in_t(dw_in_t, self.ids, kept=True, deps=[token])]
        reduced = self.reduced("mixer", token) + self.reduced("mlp", token)
        self.updates, last = [None] * len(reduced), self.kept["w_in"][0]
        for i in (3, 4, 0, 1, 2):
            if i == 4:
                last = self._reduce_mid("w_in", last)
            self.updates[i] = _reduce_adamw(*reduced[i], self.w_sh[1 + i], self.m_sh[1 + i], self.v_sh[1 + i],
                                            "adamw_" + W_NAMES[1 + i], deps=[last])
            last = self.updates[i][0]
        return [last]

    def w_in_update(self, after):
        (reduced,) = self.reduced("w_in", after)
        return _reduce_adamw(*reduced, self.w_sh[0], self.m_sh[0], self.v_sh[0], "adamw_" + W_NAMES[0])


def kernel(x, norm_attn_g, w_in, b_forget, w_branch_a, w_branch_b, w_out, norm_mlp_g, w_up, w_down, norm_final_g, loss_target, m_norm_attn_g, m_w_in, m_b_forget, m_w_branch_a, m_w_branch_b, m_w_out, m_norm_mlp_g, m_w_up, m_w_down, m_norm_final_g, v_norm_attn_g, v_w_in, v_b_forget, v_w_branch_a, v_w_branch_b, v_w_out, v_norm_mlp_g, v_w_up, v_w_down, v_norm_final_g):
    cx, cy, cc = _place()
    ids = jnp.stack([cc, 2 * cx + cy]).astype(jnp.int32)

    w_sh = [w_in[0].T] + [t[0] for t in (w_branch_a, w_branch_b, w_out, w_up, w_down)]
    m_sh = [m_w_in[0].T] + [t[0] for t in (m_w_branch_a, m_w_branch_b, m_w_out, m_w_up, m_w_down)]
    v_sh = [v_w_in[0].T] + [t[0] for t in (v_w_branch_a, v_w_branch_b, v_w_out, v_w_up, v_w_down)]

    w_in_t, h1, *lands = _gather_w_in(jnp.pad(w_sh[0], ((0, W_SLAB - w_sh[0].shape[0]), (0, 0))), x[0], norm_attn_g,
                                      w_sh[1:])
    comm = _StepComm(ids, w_sh, m_sh, v_sh, lands, w_in_t)
    b_pad = jnp.pad(b_forget, ((0, 0), (0, LANES - FOX_H)))

    loss_row, dx, dsmall = _local_step(
        x[0], loss_target[0], h1, norm_attn_g, norm_mlp_g, norm_final_g.reshape(1, D), b_pad, w_in_t, comm)

    dg_attn, db, dg_mlp, dg_final = dsmall
    stack = lax.dynamic_update_slice(jnp.zeros((N_DEV, SMALL_R, LANES), F32),
                                     _pack_small(dg_attn, db, dg_mlp, dg_final, loss_row)[None],
                                     (4 * cx + 2 * cy + cc, 0, 0))
    small_gather = _exchange_start("small_gather_start", _peer_copies, PEER_COPIES, [], [stack])
    w_in_update = comm.w_in_update(small_gather[-1])
    _, (stack,) = _exchange_wait("small_gather_wait", _peer_copies, small_gather, w_in_update[0])
    loss, *small = _small_adamw(
        stack,
        _pack_small(norm_attn_g, b_forget, norm_mlp_g, norm_final_g.reshape(1, D)),
        _pack_small(m_norm_attn_g, m_b_forget, m_norm_mlp_g, m_norm_final_g.reshape(1, D)),
        _pack_small(v_norm_attn_g, v_b_forget, v_norm_mlp_g, v_norm_final_g.reshape(1, D)))
    big = [w_in_update] + comm.updates

    outs = [loss.reshape(()), dx[None]]
    for q in range(4):
        s_attn, s_b, s_mlp, s_final = small[4 * q:4 * q + 4]
        s_final = s_final.reshape(D)
        b_in, b_a, b_b, b_out, b_up, b_down = [t[q][None] for t in big]
        b_in = jnp.swapaxes(b_in, 1, 2)
        outs += [s_attn, b_in, s_b, b_a, b_b, b_out, s_mlp, b_up, b_down, s_final]
    return tuple(outs)
```

```python
import functools

import jax
import jax.numpy as jnp
from jax import lax
from jax.experimental import pallas as pl
from jax.experimental.pallas import tpu as pltpu

F32 = jnp.float32
BF16 = jnp.bfloat16
MESH = pl.DeviceIdType.MESH

S = 2048
D = 1024
HD = 64
FOX_H = 8
FOX_W = FOX_H * HD
DIL_HG = 4
DIL_G = 3
DIL_W = DIL_G * DIL_HG * HD
DIL_OUT = DIL_HG * HD
DIL_BLK = 128
DIL_R = (1, 4, 16)
DFF = 4 * D
D_IN = 3 * FOX_W + FOX_H + 3 * DIL_W + 2 * D
EPS = 1e-6
NEG_INF = -1e30
SCALE = HD ** -0.5
ROPE_THETA = 500000.0
ROPE_DIM = HD // 4
N_DEV = 8

ADAM_LR = 0.001
ADAM_B1 = 0.9
ADAM_B2 = 0.999
ADAM_EPS = 1e-08
ADAM_WD = 0.01
ADAM_STEP = 10

LANES = 128
CB = 256
SEG_A = 0
SEG_B = 3 * FOX_W
SEG_Z = SEG_B + 3 * DIL_W
SEG_G = SEG_Z + CB
SEG_F = SEG_G + 2 * D
D_PAD = SEG_F + CB
DW_IN_TM = D_PAD // 10
SMALL_R = 32

VMEM_MB = 56


def _cparams(dims=None, vmem_mb=VMEM_MB, **kw):
    return pltpu.CompilerParams(dimension_semantics=dims, vmem_limit_bytes=vmem_mb << 20, **kw)


ANY = pl.BlockSpec(memory_space=pl.ANY)

_NN = (((1,), (0,)), ((), ()))
_NT = (((1,), (1,)), ((), ()))
_TN = (((0,), (0,)), ((), ()))


def _dot(a, b, dims):
    return lax.dot_general(a.astype(BF16), b.astype(BF16), dims, preferred_element_type=F32)


def _mm(a, b, *, mode, tm, tn, out_dtypes, name, n=None, k=None, a_off=0, b_off=0,
        b_sharded=False, out_sharded=False, epilogue=None, extras=(), deps=()):
    n_sh = b.shape[-1] if b_sharded else None
    if mode == "nn":
        m = a.shape[0]
        k = k or a.shape[1]
        a_spec = pl.BlockSpec((tm, k), lambda i, j: (i, a_off))
        if b_sharded:
            assert tn == n_sh
            n = N_DEV * n_sh
            b_spec = pl.BlockSpec((None, k, tn), lambda i, j: (j, 0, 0))
        else:
            n = n or b.shape[1]
            b_spec = pl.BlockSpec((k, tn), lambda i, j: (0, j + b_off))
        dims = _NN
    elif mode == "nt":
        m = a.shape[0]
        k = k or a.shape[1]
        a_spec = pl.BlockSpec((tm, k), lambda i, j: (i, a_off))
        if b_sharded:
            n = b.shape[1]
            b_spec = pl.BlockSpec((N_DEV, tn, n_sh), lambda i, j: (0, j, 0))
        else:
            n = n or b.shape[0]
            b_spec = pl.BlockSpec((tn, k), lambda i, j: (j + b_off, 0))
        dims = _NT
    else:
        k, m = a.shape
        n = n or b.shape[1]
        a_spec = pl.BlockSpec((k, tm), lambda i, j: (0, i))
        b_spec = pl.BlockSpec((k, tn), lambda i, j: (0, j + b_off))
        dims = _TN
    assert m % tm == 0 and n % tn == 0, (name, m, n, tm, tn)
    n_extra = len(extras)
    tile = pl.BlockSpec((tm, tn), lambda i, j: (i, j))
    split_tile = out_sharded and n == tn
    if out_sharded:
        assert mode == "tn" and n // tn in (1, N_DEV)
        if split_tile:
            out_spec = pl.BlockSpec((N_DEV, tm, n // N_DEV), lambda i, j: (0, i, 0))
        else:
            out_spec = pl.BlockSpec((None, tm, tn), lambda i, j: (j, i, 0))
        out_shape = (N_DEV, m, n // N_DEV)
    else:
        out_spec, out_shape = tile, (m, n)

    def body(a_ref, b_ref, *refs):
        if mode == "nt" and b_sharded:
            acc = _dot(a_ref[...], jnp.concatenate([b_ref[p] for p in range(N_DEV)], axis=1), dims)
        else:
            acc = _dot(a_ref[...], b_ref[...], dims)
        ex = [r[...] for r in refs[:n_extra]]
        outs = epilogue(acc, *ex) if epilogue is not None else (acc,)
        for o_ref, o in zip(refs[n_extra + len(deps):], outs):
            if split_tile:
                w = n // N_DEV
                for p in range(N_DEV):
                    o_ref[p] = o[:, p * w:(p + 1) * w].astype(o_ref.dtype)
            else:
                o_ref[...] = o.astype(o_ref.dtype)

    res = pl.pallas_call(
        body, name=name, grid=(m // tm, n // tn),
        in_specs=[a_spec, b_spec] + [tile] * n_extra + [ANY] * len(deps),
        out_specs=[out_spec] * len(out_dtypes),
        out_shape=[jax.ShapeDtypeStruct(out_shape, dt) for dt in out_dtypes],
        compiler_params=_cparams(("parallel", "parallel")),
    )(a, b, *extras, *deps)
    return res if len(out_dtypes) > 1 else res[0]


def _mm_rows(a, b, *, tm, name, epilogue, rows=(), vecs=(), row_out=(), vec_out=(), b_sharded=False, deps=()):
    m, k = a.shape
    n_rows, n_vecs, n_deps = len(rows), len(vecs), len(deps)
    n_sh = b.shape[-1] if b_sharded else None

    def body(a_ref, b_ref, *refs):
        row_refs, vec_refs = refs[:n_rows], refs[n_rows:n_rows + n_vecs]
        outs = refs[n_rows + n_vecs + n_deps:]
        if b_sharded:
            acc = _dot(a_ref[:, 0:n_sh], b_ref[0], _NT)
            for p in range(1, N_DEV):
                acc = acc + _dot(a_ref[:, p * n_sh:(p + 1) * n_sh], b_ref[p], _NT)
        else:
            acc = _dot(a_ref[...], b_ref[...], _NN)
        row_vals, vec_incs = epilogue(acc, [r[...] for r in row_refs], [v[...] for v in vec_refs])
        for o_ref, val in zip(outs[:len(row_out)], row_vals):
            o_ref[...] = val.astype(o_ref.dtype)

        @pl.when(pl.program_id(0) == 0)
        def _():
            for o_ref in outs[len(row_out):]:
                o_ref[...] = jnp.zeros_like(o_ref)

        for o_ref, inc in zip(outs[len(row_out):], vec_incs):
            o_ref[...] += inc

    tile = pl.BlockSpec((tm, D), lambda i: (i, 0))
    b_spec = pl.BlockSpec(b.shape, lambda i: (0,) * b.ndim)
    return pl.pallas_call(
        body, name=name, grid=(m // tm,),
        in_specs=[pl.BlockSpec((tm, k), lambda i: (i, 0)), b_spec] + [tile] * n_rows
                 + [pl.BlockSpec(v.shape, lambda i: (0, 0)) for v in vecs] + [ANY] * n_deps,
        out_specs=[tile] * len(row_out) + [pl.BlockSpec((1, w), lambda i: (0, 0)) for w in vec_out],
        out_shape=[jax.ShapeDtypeStruct((m, D), dt) for dt in row_out]
                  + [jax.ShapeDtypeStruct((1, w), F32) for w in vec_out],
        compiler_params=_cparams(("arbitrary",)),
    )(a, b, *rows, *vecs, *deps)


ROW_T = 256
ROWS_TM = 512


def _rms_rows(x, g):
    r = lax.rsqrt(jnp.mean(x * x, axis=-1, keepdims=True) + EPS)
    return (x * r) * g


def _rms_bwd_rows(dh, x, dres, g):
    r = lax.rsqrt(jnp.mean(x * x, axis=-1, keepdims=True) + EPS)
    xn = x * r
    dhn = dh * g
    dx = dres + r * (dhn - xn * jnp.mean(dhn * xn, axis=-1, keepdims=True))
    return dx, jnp.sum(dh * xn, axis=0, keepdims=True)


def _final_loss_rows(x, g, tgt):
    r = lax.rsqrt(jnp.mean(x * x, axis=-1, keepdims=True) + EPS)
    xn = x * r
    err = xn * g - tgt
    row_loss = jnp.mean(err * err, axis=-1, keepdims=True)
    loss = 0.5 * jnp.sum(row_loss, axis=0, keepdims=True) * jnp.ones((1, LANES), F32)
    dy = err * (1.0 / D)
    dyn = dy * g
    dx = r * (dyn - xn * jnp.mean(dyn * xn, axis=-1, keepdims=True))
    return dx, jnp.sum(dy * xn, axis=0, keepdims=True), loss


def _sigmoid(z):
    return 1.0 / (1.0 + jnp.exp(-z))


GATE_TR = 512


def _mixer_tail(oa, ob, w_a, w_b, w_out, proj_g, x, g_mlp):
    def body(oa_ref, ob_ref, wa_ref, wb_ref, wo_ref, g_ref, x_ref, gm_ref, ya_ref, yb_ref, mixed_ref, x2_ref, h2_ref):
        ya = _dot(oa_ref[...], jnp.concatenate([wa_ref[p] for p in range(N_DEV)], axis=1), _NN)
        yb = _dot(ob_ref[...], jnp.concatenate([wb_ref[p] for p in range(N_DEV)], axis=1), _NN)
        ya_ref[...] = ya
        yb_ref[...] = yb
        mixed = (_sigmoid(g_ref[:, :D]) * ya + _sigmoid(g_ref[:, D:]) * yb).astype(BF16)
        mixed_ref[...] = mixed
        x2 = x_ref[...] + _dot(mixed, wo_ref[...], _NN)
        x2_ref[...] = x2
        h2_ref[...] = _rms_rows(x2, gm_ref[...]).astype(BF16)

    def rows(width):
        return pl.BlockSpec((GATE_TR, width), lambda i: (i, 0))

    def whole(t):
        return pl.BlockSpec(t.shape, lambda i: (0,) * t.ndim)

    return pl.pallas_call(
        body, name="mixer_tail", grid=(S // GATE_TR,),
        in_specs=[rows(FOX_W), rows(DIL_OUT), whole(w_a), whole(w_b), whole(w_out), rows(2 * D), rows(D), whole(g_mlp)],
        out_specs=[rows(D)] * 5,
        out_shape=[jax.ShapeDtypeStruct((S, D), dt) for dt in (F32, F32, BF16, F32, BF16)],
        compiler_params=_cparams(("parallel",)),
    )(oa, ob, w_a, w_b, w_out, proj_g, x, g_mlp)


def _out_proj_gate_bwd(dx2, w_out, proj_g, ya, yb):
    def body(dx_ref, w_ref, g_ref, ya_ref, yb_ref, dy_ref, dg_ref):
        dm = _dot(dx_ref[...], w_ref[...], _NT)
        for half, y_ref in enumerate((ya_ref, yb_ref)):
            cols = slice(half * D, (half + 1) * D)
            s = _sigmoid(g_ref[:, cols])
            dy_ref[:, cols] = (dm * s).astype(BF16)
            dg_ref[:, cols] = (dm * y_ref[...] * (s * (1.0 - s))).astype(BF16)

    row = pl.BlockSpec((GATE_TR, D), lambda i: (i, 0))
    wide = pl.BlockSpec((GATE_TR, 2 * D), lambda i: (i, 0))
    return pl.pallas_call(
        body, name="out_proj_gate_bwd", grid=(S // GATE_TR,),
        in_specs=[row, pl.BlockSpec((D, D), lambda i: (0, 0)), wide, row, row],
        out_specs=[wide, pl.BlockSpec((GATE_TR, 2 * D), lambda i: (i, SEG_G // (2 * D)))],
        out_shape=[jax.ShapeDtypeStruct((S, 2 * D), BF16), jax.ShapeDtypeStruct((S, D_PAD), BF16)],
        compiler_params=_cparams(("parallel",)),
    )(dx2, w_out, proj_g, ya, yb)


FOX_TQ = 512
FOX_TQ_FWD = 1024


def _scan_rows(x, reverse):
    n = x.shape[0]
    row = lax.broadcasted_iota(jnp.int32, x.shape, 0)
    k = 1
    while k < n:
        if reverse:
            x = x + jnp.where(row < n - k, pltpu.roll(x, n - k, 0), 0.0)
        else:
            x = x + jnp.where(row >= k, pltpu.roll(x, k, 0), 0.0)
        k *= 2
    return x


def _fox_dscan(dft, dfs, proj_f, b_pad, dproj):
    def body(dft_ref, dfs_ref, f_ref, b_ref, _, dfa_ref, db_ref):
        dfs_pad = jnp.concatenate([dfs_ref[...], jnp.zeros((LANES - FOX_H, S), F32)], axis=0)
        df = dfs_pad.T + dft_ref[0]
        for hp in range(1, dft_ref.shape[0]):
            df = df + pltpu.roll(dft_ref[hp], 2 * hp, 1)
        dlf = _scan_rows(df, reverse=True)
        z = f_ref[...] + b_ref[...]
        lane = lax.broadcasted_iota(jnp.int32, (S, LANES), 1)
        dfa = jnp.where(lane < FOX_H, dlf / (1.0 + jnp.exp(z)), 0.0)
        dfa_ref[:, :LANES] = dfa.astype(BF16)
        dfa_ref[:, LANES:] = jnp.zeros((S, CB - LANES), BF16)
        db_ref[...] = jnp.sum(dfa, axis=0, keepdims=True)

    return pl.pallas_call(
        body, name="fox_dscan", grid=(1,),
        in_specs=[pl.BlockSpec(dft.shape, lambda i: (0, 0, 0)), pl.BlockSpec((FOX_H, S), lambda i: (0, 0)),
                  pl.BlockSpec((S, LANES), lambda i: (0, 0)), pl.BlockSpec((1, LANES), lambda i: (0, 0)), ANY],
        out_specs=[pl.BlockSpec((S, CB), lambda i: (0, SEG_F // CB)), pl.BlockSpec((1, LANES), lambda i: (0, 0))],
        out_shape=[jax.ShapeDtypeStruct((S, D_PAD), BF16), jax.ShapeDtypeStruct((1, LANES), F32)],
        input_output_aliases={4: 0},
        compiler_params=_cparams(("arbitrary",)),
    )(dft, dfs, proj_f, b_pad, dproj)


FOX_NQ = S // FOX_TQ
FOX_HP = FOX_W // LANES
HEADS_PER_LB = LANES // HD
FOX_AUG = 2 * LANES


def _fox_prepare(proj_a, proj_f, b_pad):
    tr = 512

    def body(q_ref, k_ref, f_ref, b_ref, qa_ref, ka_ref, hi_s, mid_s, lo_s):
        i = pl.program_id(0)

        @pl.when(i == 0)
        def _():
            z = f_ref[...] + b_ref[...]
            lf = jnp.minimum(z, 0.0) - jnp.log1p(jnp.exp(-jnp.abs(z)))
            f = _scan_rows(lf, reverse=False)
            hi = f.astype(BF16).astype(F32)
            r1 = f - hi
            mid = r1.astype(BF16).astype(F32)
            hi_s[...] = hi
            mid_s[...] = mid
            lo_s[...] = (r1 - mid).astype(BF16).astype(F32)

        rows = pl.ds(pl.multiple_of(i * tr, tr), tr)
        hi, mid, lo = hi_s[rows, :], mid_s[rows, :], lo_s[rows, :]
        lane = lax.broadcasted_iota(jnp.int32, (tr, HD), 1)
        q_ones = jnp.where((lane >= 3) & (lane < 6), 1.0, 0.0)
        k_ones = jnp.where(lane < 3, 1.0, 0.0)
        for h in range(FOX_H):
            a1, a2, a3 = hi[:, h:h + 1], mid[:, h:h + 1], lo[:, h:h + 1]
            q_extra = jnp.where(lane == 0, a1, jnp.where(lane == 1, a2, jnp.where(lane == 2, a3, q_ones)))
            k_extra = jnp.where(lane == 3, -a1, jnp.where(lane == 4, -a2, jnp.where(lane == 5, -a3, k_ones)))
            base = h * LANES
            qa_ref[:, base:base + HD] = q_ref[:, h * HD:(h + 1) * HD] * jnp.asarray(SCALE, BF16)
            qa_ref[:, base + HD:base + LANES] = q_extra.astype(BF16)
            ka_ref[:, base:base + HD] = k_ref[:, h * HD:(h + 1) * HD]
            ka_ref[:, base + HD:base + LANES] = k_extra.astype(BF16)

    out_blk = pl.BlockSpec((tr, FOX_H * LANES), lambda i: (i, 0))
    return pl.pallas_call(
        body, name="fox_prepare", grid=(S // tr,),
        in_specs=[pl.BlockSpec((tr, FOX_W), lambda i: (i, 0)), pl.BlockSpec((tr, FOX_W), lambda i: (i, 1)),
                  pl.BlockSpec((S, LANES), lambda i: (0, 0)), pl.BlockSpec((1, LANES), lambda i: (0, 0))],
        out_specs=[out_blk, out_blk],
        out_shape=[jax.ShapeDtypeStruct((S, FOX_H * LANES), BF16)] * 2,
        scratch_shapes=[pltpu.VMEM((S, LANES), F32)] * 3,
        compiler_params=_cparams(("arbitrary",)),
    )(proj_a, proj_a, proj_f, b_pad)


def _fox_scores(qa, ka, w, tq):
    s = _dot(qa, ka, _NT)
    row = lax.broadcasted_iota(jnp.int32, (tq, tq), 0)
    col = lax.broadcasted_iota(jnp.int32, (tq, tq), 1)
    diag = jnp.where(col <= row, s[:, w * tq:], NEG_INF)
    return diag if w == 0 else jnp.concatenate([s[:, :w * tq], diag], axis=1)


def _fox_fwd(q_aug, k_aug, proj_a):
    tq = FOX_TQ_FWD

    def body(qa_ref, ka_ref, v_ref, o_ref, lse_ref):
        qi = pl.program_id(1)
        lane = lax.broadcasted_iota(jnp.int32, (tq, LANES), 1)
        for w in range(S // tq):
            @pl.when(qi == w)
            def _(w=w):
                width = (w + 1) * tq
                lse = jnp.zeros((tq, LANES), F32)
                for hh in range(HEADS_PER_LB):
                    aug = slice(hh * LANES, (hh + 1) * LANES)
                    sl = slice(hh * HD, (hh + 1) * HD)
                    s = _fox_scores(qa_ref[:, aug], ka_ref[:width, aug], w, tq)
                    m = jnp.max(s, axis=-1, keepdims=True)
                    p = jnp.exp(s - m)
                    l = jnp.sum(p, axis=-1, keepdims=True)
                    o_ref[:, sl] = _dot(p / l, v_ref[:width, sl], _NN)
                    lse = jnp.where(lane == hh, m + jnp.log(l), lse)
                lse_ref[...] = lse

    return pl.pallas_call(
        body, name="fox_fwd", grid=(FOX_HP, S // tq),
        in_specs=[pl.BlockSpec((tq, FOX_AUG), lambda h, i: (i, h)),
                  pl.BlockSpec((S, FOX_AUG), lambda h, i: (0, h)),
                  pl.BlockSpec((S, LANES), lambda h, i: (0, 2 * FOX_HP + h))],
        out_specs=[pl.BlockSpec((tq, LANES), lambda h, i: (i, h)),
                   pl.BlockSpec((None, tq, LANES), lambda h, i: (h, i, 0))],
        out_shape=[jax.ShapeDtypeStruct((S, FOX_W), F32), jax.ShapeDtypeStruct((FOX_HP, S, LANES), F32)],
        compiler_params=_cparams(("parallel", "parallel")),
    )(q_aug, k_aug, proj_a)


def _fox_bwd(q_aug, k_aug, proj_a, lse, do, dproj, deps=()):
    n_q = FOX_NQ

    def body(qa_ref, ka_ref, v_ref, lse_ref, do_ref, *refs):
        dproj_ref, dft_ref, dfs_ref, dk_acc, dv_acc, dq_buf, kv_buf, dq_sems, kv_sems = refs[1 + len(deps):]
        hp = pl.program_id(0)
        t = pl.program_id(1)
        slot = t % 2
        col = pl.multiple_of(hp * LANES, LANES)

        def dq_copy(step, buf_slot):
            rows = pl.ds(pl.multiple_of(step * FOX_TQ, FOX_TQ), FOX_TQ)
            return pltpu.make_async_copy(dq_buf.at[buf_slot], dproj_ref.at[rows, pl.ds(col, LANES)], dq_sems.at[buf_slot])

        @pl.when(t == 0)
        def _():
            dk_acc[...] = jnp.zeros_like(dk_acc)
            dv_acc[...] = jnp.zeros_like(dv_acc)

        @pl.when((t == 0) & (hp == 0))
        def _():
            dfs_ref[...] = jnp.zeros_like(dfs_ref)

        @pl.when(t >= 2)
        def _():
            dq_copy(t - 2, slot).wait()

        lane = lax.broadcasted_iota(jnp.int32, (FOX_TQ, LANES), 1)
        for w in range(n_q):
            @pl.when(t == w)
            def _(w=w):
                width = (w + 1) * FOX_TQ
                lse_all = lse_ref[...]
                sub = lax.broadcasted_iota(jnp.int32, (FOX_H, width), 0)
                dft = jnp.zeros((FOX_TQ, LANES), F32)
                for hh in range(HEADS_PER_LB):
                    aug = slice(hh * LANES, (hh + 1) * LANES)
                    sl = slice(hh * HD, (hh + 1) * HD)
                    head = hp * HEADS_PER_LB + hh
                    qa = qa_ref[:, aug]
                    ka = ka_ref[:width, aug]
                    do_h = do_ref[:, sl]
                    s = _fox_scores(qa, ka, w, FOX_TQ)
                    p = jnp.exp(s - lse_all[:, hh:hh + 1])
                    dp = _dot(do_h, v_ref[:width, sl], _NT)
                    ds = p * (dp - jnp.sum(dp * p, axis=-1, keepdims=True))
                    dft = jnp.where(lane == hh, jnp.sum(ds, axis=-1, keepdims=True), dft)
                    dfs_ref[:, :width] -= jnp.where(sub == head, jnp.sum(ds, axis=0, keepdims=True), 0.0)
                    dq_buf[slot, :, sl] = (_dot(ds, ka[:, :HD], _NN) * SCALE).astype(BF16)
                    dk_acc[:width, sl] += _dot(ds, qa[:, :HD], _TN)
                    dv_acc[:width, sl] += _dot(p, do_h, _TN)
                dft_ref[...] = dft

        dq_copy(t, slot).start()

        @pl.when(t == n_q - 1)
        def _():
            dq_copy(t - 1, 1 - slot).wait()
            dq_copy(t, slot).wait()
            kv_buf[0] = dk_acc[...].astype(BF16)
            kv_buf[1] = dv_acc[...].astype(BF16)
            copies = [pltpu.make_async_copy(
                kv_buf.at[j], dproj_ref.at[:, pl.ds(pl.multiple_of((j + 1) * FOX_W + hp * LANES, LANES), LANES)],
                kv_sems.at[j]) for j in range(2)]
            for cp in copies:
                cp.start()
            for cp in copies:
                cp.wait()

    qblk = pl.BlockSpec((FOX_TQ, FOX_AUG), lambda h, t: (t, h))
    lane_blk = pl.BlockSpec((None, FOX_TQ, LANES), lambda h, t: (h, t, 0))
    return pl.pallas_call(
        body, name="fox_bwd", grid=(FOX_HP, n_q),
        in_specs=[qblk, pl.BlockSpec((S, FOX_AUG), lambda h, t: (0, h)),
                  pl.BlockSpec((S, LANES), lambda h, t: (0, 2 * FOX_HP + h)),
                  lane_blk, pl.BlockSpec((FOX_TQ, LANES), lambda h, t: (t, h)), ANY] + [ANY] * len(deps),
        out_specs=[ANY, lane_blk, pl.BlockSpec((FOX_H, S), lambda h, t: (0, 0))],
        out_shape=[jax.ShapeDtypeStruct((S, D_PAD), BF16),
                   jax.ShapeDtypeStruct((FOX_HP, S, LANES), F32), jax.ShapeDtypeStruct((FOX_H, S), F32)],
        scratch_shapes=[pltpu.VMEM((S, LANES), F32), pltpu.VMEM((S, LANES), F32),
                        pltpu.VMEM((2, FOX_TQ, LANES), BF16), pltpu.VMEM((2, S, LANES), BF16),
                        pltpu.SemaphoreType.DMA((2,)), pltpu.SemaphoreType.DMA((2,))],
        input_output_aliases={5: 0},
        compiler_params=_cparams(("arbitrary", "arbitrary")),
    )(q_aug, k_aug, proj_a, lse, do, dproj, *deps)


def _rope_tables():
    half = ROPE_DIM // 2
    shape = (DIL_G, S, half)
    g = lax.broadcasted_iota(jnp.int32, shape, 0)
    row = lax.broadcasted_iota(jnp.int32, shape, 1)
    r = jnp.left_shift(1, 2 * g)
    per_class = S // r
    pos = (row % per_class) * r + row // per_class
    inv_freq = jnp.power(jnp.float32(ROPE_THETA), -lax.broadcasted_iota(F32, shape, 2) * 2.0 / ROPE_DIM)
    ang = pos.astype(F32) * inv_freq
    return jnp.concatenate([jnp.cos(ang), jnp.sin(ang)], axis=-1)


def _rope_factors(cs):
    half = ROPE_DIM // 2
    i = lax.broadcasted_iota(jnp.int32, (2 * half, 2 * LANES), 0)
    l = lax.broadcasted_iota(jnp.int32, (2 * half, 2 * LANES), 1)
    hit = ((l < LANES) & (i == l % half)) | ((l >= LANES) & (i - half == l % half))
    spread = jnp.where(hit, 1.0, 0.0).astype(BF16)
    hi = cs.astype(BF16)
    r1 = cs - hi.astype(F32)
    mid = r1.astype(BF16)
    lo = (r1 - mid.astype(F32)).astype(BF16)
    full = (_dot(hi, spread, _NN) + _dot(mid, spread, _NN)) + _dot(lo, spread, _NN)
    cos, sin = full[:, :LANES], full[:, LANES:]
    lane = lax.broadcasted_iota(jnp.int32, cos.shape, 1) % HD
    t_self = jnp.where(lane < ROPE_DIM, cos, 1.0)
    t_up = jnp.where(lane < half, -sin, 0.0)
    t_dn = jnp.where((lane >= half) & (lane < ROPE_DIM), sin, 0.0)
    return t_self, t_up, t_dn


def _residue_pieces(r):
    if r == 1:
        return [(slice(i, i + 512), slice(i, i + 512)) for i in range(0, S, 512)]
    n = S // r
    return [(pl.ds(j, n, stride=r), slice(j * n, (j + 1) * n)) for j in range(r)]


def _rope_apply(x, a, u, d, backward):
    half = ROPE_DIM // 2
    if backward:
        return x * a + pltpu.roll(x * u, half, 1) + pltpu.roll(x * d, LANES - half, 1)
    return x * a + pltpu.roll(x, LANES - half, 1) * u + pltpu.roll(x, half, 1) * d


def _rope_cache_factors(cs_ref, fac, first_step_of_group):
    @pl.when(first_step_of_group)
    def _():
        for lo in range(0, S, 512):
            for j, f in enumerate(_rope_factors(cs_ref[lo:lo + 512, :])):
                fac[j, lo:lo + 512, :] = f


LB_PER_CB = CB // LANES
ROPE_NB = 3 * DIL_G * LB_PER_CB


def _rope_step(s):
    per_group = 3 * LB_PER_CB
    return s // per_group, (s % per_group) // LB_PER_CB, s % LB_PER_CB


def _rope_split(proj_b, tables):
    def body(x_ref, cs_ref, o_ref, fac):
        g, t, hf = _rope_step(pl.program_id(0))
        _rope_cache_factors(cs_ref, fac, (t == 0) & (hf == 0))
        for gs in range(DIL_G):
            @pl.when(g == gs)
            def _(gs=gs):
                for tok, sub in _residue_pieces(DIL_R[gs]):
                    x = x_ref[tok, :]
                    y = _rope_apply(x, fac[0, sub, :], fac[1, sub, :], fac[2, sub, :], False)
                    o_ref[sub, :] = jnp.where(t < 2, y, x).astype(BF16)

    def in_index(s):
        g, t, hf = _rope_step(s)
        return 0, (t * DIL_G + g) * LB_PER_CB + hf

    def out_index(s):
        g, t, hf = _rope_step(s)
        return t * DIL_G + g, 0, hf

    tab = pl.BlockSpec((None, S, 2 * (ROPE_DIM // 2)), lambda s: (_rope_step(s)[0], 0, 0))
    return pl.pallas_call(
        body, name="rope_split", grid=(ROPE_NB,),
        in_specs=[pl.BlockSpec((S, LANES), in_index), tab],
        out_specs=pl.BlockSpec((None, S, LANES), out_index),
        out_shape=jax.ShapeDtypeStruct((3 * DIL_G, S, CB), BF16),
        scratch_shapes=[pltpu.VMEM((3, S, LANES), F32)],
        compiler_params=_cparams(("arbitrary",)),
    )(proj_b, tables)


def _rope_merge_bwd(dq3, dk3, dv3, tables, dproj):
    def body(dq_ref, dk_ref, dv_ref, cs_ref, _, o_ref, tmp, fac):
        s = pl.program_id(0)
        g, t, hf = _rope_step(s)
        _rope_cache_factors(cs_ref, fac, (t == 0) & (hf == 0) & (s < ROPE_NB))
        for gs in range(DIL_G):
            @pl.when((g == gs) & (s < ROPE_NB))
            def _(gs=gs):
                for tok, sub in _residue_pieces(DIL_R[gs]):
                    x = jnp.where(t == 0, dq_ref[sub, :], jnp.where(t == 1, dk_ref[sub, :], dv_ref[sub, :]))
                    y = _rope_apply(x, fac[0, sub, :], fac[1, sub, :], fac[2, sub, :], True)
                    tmp[tok, :] = jnp.where(t < 2, y, x)
                o_ref[...] = tmp[...].astype(BF16)

        @pl.when(s >= ROPE_NB)
        def _():
            o_ref[...] = jnp.zeros_like(o_ref)

    def src_spec(own):
        def index(s):
            g, t, hf = _rope_step(jnp.minimum(s, ROPE_NB - 1))
            return g, 0, jnp.where(t == own, hf, jnp.where(t > own, LB_PER_CB - 1, 0))
        return pl.BlockSpec((None, S, LANES), index)

    def out_index(s):
        g, t, hf = _rope_step(s)
        col = SEG_B // LANES + (t * DIL_G + g) * LB_PER_CB + hf
        return 0, jnp.where(s < ROPE_NB, col, SEG_Z // LANES + s - ROPE_NB)

    tab = pl.BlockSpec((None, S, 2 * (ROPE_DIM // 2)), lambda s: (_rope_step(jnp.minimum(s, ROPE_NB - 1))[0], 0, 0))
    return pl.pallas_call(
        body, name="rope_merge_bwd", grid=(ROPE_NB + LB_PER_CB,),
        in_specs=[src_spec(0), src_spec(1), src_spec(2), tab, ANY],
        out_specs=pl.BlockSpec((S, LANES), out_index),
        out_shape=jax.ShapeDtypeStruct((S, D_PAD), BF16),
        scratch_shapes=[pltpu.VMEM((S, LANES), F32), pltpu.VMEM((3, S, LANES), F32)],
        input_output_aliases={4: 0},
        compiler_params=_cparams(("arbitrary",)),
    )(dq3, dk3, dv3, tables, dproj)


DIL_NB = S // DIL_BLK


_BQK = (((2,), (2,)), ((0,), (0,)))
_BQD = (((2,), (1,)), ((0,), (0,)))
_BKD = (((1,), (1,)), ((0,), (0,)))


def _dil_mask(g):
    shape = (DIL_NB, DIL_BLK, 2 * DIL_BLK)
    blocks_per_seq = jnp.right_shift(DIL_NB, 2 * g)
    has_prev = jnp.bitwise_and(lax.broadcasted_iota(jnp.int32, shape, 0), blocks_per_seq - 1) != 0
    a = lax.broadcasted_iota(jnp.int32, shape, 1)
    kk = lax.broadcasted_iota(jnp.int32, shape, 2)
    diff = DIL_BLK + a - kk
    return (diff >= 0) & (diff <= DIL_BLK) & ((kk >= DIL_BLK) | has_prev)


def _blocks(t):
    return t.reshape(DIL_NB, DIL_BLK, t.shape[-1])


def _with_prev(t):
    prev = jnp.concatenate([jnp.zeros((DIL_BLK, t.shape[-1]), t.dtype), t[:-DIL_BLK]], axis=0)
    return jnp.concatenate([_blocks(prev), _blocks(t)], axis=1)


def _fold_prev(t2):
    n = t2.shape[-1]
    to_prev = t2[:, :DIL_BLK].reshape(S, n)
    own = t2[:, DIL_BLK:].reshape(S, n)
    return own + jnp.concatenate([to_prev[DIL_BLK:], jnp.zeros((DIL_BLK, n), t2.dtype)], axis=0)


def _dil_group_spec(t, width=DIL_OUT):
    return pl.BlockSpec((None, S, width), lambda g: (t * DIL_G + g, 0, 0))


def _dil_fwd(qkv3):
    def body(q_ref, k_ref, v_ref, o_ref, lse_ref):
        ok = _dil_mask(pl.program_id(0))
        lane = lax.broadcasted_iota(jnp.int32, (S, LANES), 1)
        lse = jnp.zeros((S, LANES), F32)
        for h in range(DIL_HG):
            sl = slice(h * HD, (h + 1) * HD)
            s = jnp.where(ok, _dot(_blocks(q_ref[:, sl]), _with_prev(k_ref[:, sl]), _BQK) * SCALE, NEG_INF)
            m = jnp.max(s, axis=-1, keepdims=True)
            p = jnp.exp(s - m)
            l = jnp.sum(p, axis=-1, keepdims=True)
            o_ref[:, sl] = _dot(p / l, _with_prev(v_ref[:, sl]), _BQD).reshape(S, HD)
            lse = jnp.where(lane == h, (m + jnp.log(l)).reshape(S, 1), lse)
        lse_ref[...] = lse

    return pl.pallas_call(
        body, name="dil_fwd", grid=(DIL_G,),
        in_specs=[_dil_group_spec(0), _dil_group_spec(1), _dil_group_spec(2)],
        out_specs=[_dil_group_spec(0), _dil_group_spec(0, LANES)],
        out_shape=[jax.ShapeDtypeStruct((DIL_G, S, DIL_OUT), F32), jax.ShapeDtypeStruct((DIL_G, S, LANES), F32)],
        compiler_params=_cparams(("parallel",)),
    )(qkv3, qkv3, qkv3)


def _dil_bwd(qkv3, lse3, do3, c3):
    def body(q_ref, k_ref, v_ref, lse_ref, do_ref, c_ref, dq_ref, dk_ref, dv_ref):
        ok = _dil_mask(pl.program_id(0))
        lse = lse_ref[...]
        c = c_ref[...]
        for h in range(DIL_HG):
            sl = slice(h * HD, (h + 1) * HD)
            q = _blocks(q_ref[:, sl])
            k2 = _with_prev(k_ref[:, sl])
            do_h = _blocks(do_ref[:, sl])
            s = jnp.where(ok, _dot(q, k2, _BQK) * SCALE, NEG_INF)
            p = jnp.exp(s - _blocks(lse[:, h:h + 1]))
            dp = _dot(do_h, _with_prev(v_ref[:, sl]), _BQK)
            ds = p * (dp - _blocks(c[:, h:h + 1]))
            dsb = (ds * SCALE).astype(BF16)
            dq_ref[:, sl] = _dot(dsb, k2, _BQD).reshape(S, HD)
            dk_ref[:, sl] = _fold_prev(_dot(dsb, q, _BKD))
            dv_ref[:, sl] = _fold_prev(_dot(p, do_h, _BKD))

    return pl.pallas_call(
        body, name="dil_bwd", grid=(DIL_G,),
        in_specs=[_dil_group_spec(0), _dil_group_spec(1), _dil_group_spec(2), _dil_group_spec(0, LANES),
                  _dil_group_spec(0), _dil_group_spec(0, LANES)],
        out_specs=[_dil_group_spec(0)] * 3,
        out_shape=[jax.ShapeDtypeStruct((DIL_G, S, DIL_OUT), F32)] * 3,
        compiler_params=_cparams(("parallel",)),
    )(qkv3, qkv3, qkv3, lse3, do3, c3)


COMB_T = 256


def _dil_combine(o3, lse3):
    heads_per_lb = LANES // HD

    def body(o_ref, lse_ref, ob_ref, al_ref, *scratch):
        o_tok = [scratch[LB_PER_CB * gg:LB_PER_CB * (gg + 1)] for gg in range(DIL_G)]
        lse_tok = scratch[LB_PER_CB * DIL_G:]
        g = pl.program_id(0)
        for gs in range(DIL_G):
            @pl.when(g == gs)
            def _(gs=gs):
                for tok, sub in _residue_pieces(DIL_R[gs]):
                    for hf in range(LB_PER_CB):
                        o_tok[gs][hf][tok, :] = o_ref[sub, hf * LANES:(hf + 1) * LANES]
                    lse_tok[gs][tok, :] = lse_ref[sub, :]

        @pl.when(g == DIL_G - 1)
        def _():
            def chunk(i, carry):
                rows = pl.ds(pl.multiple_of(i * COMB_T, COMB_T), COMB_T)
                lse = [lse_tok[gg][rows, :] for gg in range(DIL_G)]
                m = jnp.maximum(jnp.maximum(lse[0], lse[1]), lse[2])
                e = [jnp.exp(lse[gg] - m) for gg in range(DIL_G)]
                den = (e[0] + e[1]) + e[2]
                al = [e[gg] / den for gg in range(DIL_G)]
                for gg in range(DIL_G):
                    al_ref[gg, rows, :] = al[gg]
                for h in range(DIL_HG):
                    hf, sl = h // heads_per_lb, slice((h % heads_per_lb) * HD, (h % heads_per_lb + 1) * HD)
                    acc = al[0][:, h:h + 1] * o_tok[0][hf][rows, sl]
                    for gg in range(1, DIL_G):
                        acc = acc + al[gg][:, h:h + 1] * o_tok[gg][hf][rows, sl]
                    ob_ref[rows, h * HD:(h + 1) * HD] = acc
                return carry

            lax.fori_loop(0, S // COMB_T, chunk, 0)

    return pl.pallas_call(
        body, name="dil_combine", grid=(DIL_G,),
        in_specs=[pl.BlockSpec((None, S, DIL_OUT), lambda g: (g, 0, 0)),
                  pl.BlockSpec((None, S, LANES), lambda g: (g, 0, 0))],
        out_specs=[pl.BlockSpec((S, DIL_OUT), lambda g: (0, 0)),
                   pl.BlockSpec((DIL_G, S, LANES), lambda g: (0, 0, 0))],
        out_shape=[jax.ShapeDtypeStruct((S, DIL_OUT), F32), jax.ShapeDtypeStruct((DIL_G, S, LANES), F32)],
        scratch_shapes=[pltpu.VMEM((S, LANES), F32)] * (DIL_G * (LB_PER_CB + 1)),
        compiler_params=_cparams(("arbitrary",)),
    )(o3, lse3)


def _dil_combine_bwd(dob, ob, alpha, deps=()):
    heads_per_lb = LANES // HD

    def body(dob_ref, ob_ref, al_ref, *refs):
        do_ref, c_ref = refs[len(deps):]
        g = pl.program_id(0)
        hf = pl.program_id(1)
        for gs in range(DIL_G):
            @pl.when(g == gs)
            def _(gs=gs):
                for tok, sub in _residue_pieces(DIL_R[gs]):
                    dob = dob_ref[tok, :]
                    prod = dob * ob_ref[tok, :]
                    al = al_ref[tok, :]
                    lane = lax.broadcasted_iota(jnp.int32, al.shape, 1)
                    c = jnp.where(hf == 0, 0.0, c_ref[sub, :])
                    for hh in range(heads_per_lb):
                        sl = slice(hh * HD, (hh + 1) * HD)
                        head = hf * heads_per_lb + hh
                        a = jnp.sum(jnp.where(lane == head, al, 0.0), axis=-1, keepdims=True)
                        do_ref[sub, sl] = (a * dob[:, sl]).astype(BF16)
                        c = jnp.where(lane == head, a * jnp.sum(prod[:, sl], axis=-1, keepdims=True), c)
                    c_ref[sub, :] = c

    half = pl.BlockSpec((S, LANES), lambda g, hf: (0, hf))
    return pl.pallas_call(
        body, name="dil_combine_bwd", grid=(DIL_G, LB_PER_CB),
        in_specs=[half, half, pl.BlockSpec((None, S, LANES), lambda g, hf: (g, 0, 0))] + [ANY] * len(deps),
        out_specs=[pl.BlockSpec((None, S, LANES), lambda g, hf: (g, 0, hf)),
                   pl.BlockSpec((None, S, LANES), lambda g, hf: (g, 0, 0))],
        out_shape=[jax.ShapeDtypeStruct((DIL_G, S, DIL_OUT), BF16), jax.ShapeDtypeStruct((DIL_G, S, LANES), F32)],
        compiler_params=_cparams(("parallel", "arbitrary")),
    )(dob, ob, alpha, *deps)


def _local_step(x, tgt, h1, g_attn, g_mlp, g_final, b_pad, w_in_t, hooks):
    tables = _rope_tables()

    first = hooks.first_deps()
    proj_a = _mm(h1, w_in_t, mode="nt", tm=S, tn=512, n=3 * FOX_W, out_dtypes=[BF16], name="proj_a", deps=first)
    proj_b = _mm(h1, w_in_t, mode="nt", tm=S, tn=DIL_W, n=3 * DIL_W, b_off=SEG_B // DIL_W, out_dtypes=[F32], name="proj_b",
                deps=first)
    proj_g = _mm(h1, w_in_t, mode="nt", tm=S, tn=D, n=2 * D, b_off=SEG_G // D, out_dtypes=[F32], name="proj_g",
                deps=first)
    proj_f = _mm(h1, w_in_t, mode="nt", tm=S, tn=LANES, n=LANES, b_off=SEG_F // LANES, out_dtypes=[F32], name="proj_f",
                deps=first)

    q_aug, k_aug = _fox_prepare(proj_a, proj_f, b_pad)
    oa, lse_a = _fox_fwd(q_aug, k_aug, proj_a)

    qkv3 = _rope_split(proj_b, tables)
    o3, lse3 = _dil_fwd(qkv3)
    ob, alpha = _dil_combine(o3, lse3)

    w_a, w_b, w_out = hooks.mixer_weights(ob)
    ya, yb, mixed, x2, h2 = _mixer_tail(oa, ob, w_a, w_b, w_out, proj_g, x, g_mlp)
    w_up_sh, w_down = hooks.mlp_weights(h2)

    def up_epilogue(acc):
        r = jnp.maximum(acc, 0.0)
        return acc, r * r

    u, act = _mm(h2, w_up_sh, mode="nn", tm=S, tn=DFF // N_DEV, b_sharded=True, out_dtypes=[F32, BF16],
                 name="mlp_up", epilogue=up_epilogue)

    def loss_epilogue(acc, rows, vecs):
        dx3, dg, loss = _final_loss_rows(rows[0] + acc, vecs[0], rows[1])
        return (dx3,), (dg, loss)

    dx3, dg_final, loss = _mm_rows(act, w_down, tm=ROWS_TM, name="mlp_down_loss", epilogue=loss_epilogue,
                                   rows=(x2, tgt), vecs=(g_final,), row_out=(F32,), vec_out=(D, LANES))

    def rms_bwd_epilogue(acc, rows, vecs):
        dx, dg = _rms_bwd_rows(acc, rows[0], rows[1], vecs[0])
        return (dx,), (dg,)

    du = _mm(dx3, w_down, mode="nt", tm=S, tn=512, out_dtypes=[BF16], name="mlp_down_bwd",
             epilogue=lambda acc, u_t: (acc * (2.0 * jnp.maximum(u_t, 0.0)),), extras=(u,))
    dw_down = _mm(act, dx3, mode="tn", tm=512, tn=D, out_dtypes=[F32], name="dw_down")
    dw_up_sh = _mm(h2, du, mode="tn", tm=D, tn=DFF // N_DEV, out_sharded=True, out_dtypes=[F32], name="dw_up")
    dx2, dg_mlp = _mm_rows(du, w_up_sh, b_sharded=True, tm=ROWS_TM, name="mlp_up_bwd", epilogue=rms_bwd_epilogue,
                           rows=(x2, dx3), vecs=(g_mlp,), row_out=(F32,), vec_out=(D,),
                           deps=hooks.mlp_grads(dw_up_sh, dw_down))

    dw_out = _mm(mixed, dx2, mode="tn", tm=D, tn=D, out_dtypes=[F32], name="dw_out")

    dyab, dproj = _out_proj_gate_bwd(dx2, w_out, proj_g, ya, yb)
    doa = _mm(dyab, w_a, mode="nt", tm=S, tn=FOX_W, k=D, a_off=0, b_sharded=True, out_dtypes=[BF16],
              name="branch_a_bwd")
    dw_a = _mm(oa, dyab, mode="tn", tm=FOX_W, tn=D, n=D, b_off=0, out_sharded=True, out_dtypes=[F32],
               name="dw_branch_a")
    dob = _mm(dyab, w_b, mode="nt", tm=S, tn=CB, k=D, a_off=1, b_sharded=True, out_dtypes=[F32],
              name="branch_b_bwd")
    dw_b = _mm(ob, dyab, mode="tn", tm=DIL_OUT, tn=D, n=D, b_off=1, out_sharded=True, out_dtypes=[F32],
               name="dw_branch_b")

    dproj, dft, dfs = _fox_bwd(q_aug, k_aug, proj_a, lse_a, doa, dproj, deps=hooks.mixer_grads(dw_a, dw_b, dw_out))
    dproj, db = _fox_dscan(dft, dfs, proj_f, b_pad, dproj)

    do3, c3 = _dil_combine_bwd(dob, ob, alpha, deps=hooks.mid_backward(db))
    dq3, dk3, dv3 = _dil_bwd(qkv3, lse3, do3, c3)
    dproj = _rope_merge_bwd(dq3, dk3, dv3, tables, dproj)

    dw_in_t = _mm(dproj, h1, mode="tn", tm=DW_IN_TM, tn=D, out_dtypes=[F32], name="dw_in")
    dx, dg_attn = _mm_rows(dproj, w_in_t, tm=ROWS_TM // 2, name="proj_bwd", epilogue=rms_bwd_epilogue,
                           rows=(x, dx2), vecs=(g_attn,), row_out=(F32,), vec_out=(D,),
                           deps=hooks.w_in_grad(dw_in_t))

    return loss, dx, (dg_attn, db, dg_mlp, dg_final)


SHARD_SHAPES = ((D_IN // N_DEV, D), (FOX_W, D // N_DEV), (DIL_OUT, D // N_DEV), (D // N_DEV, D),
                (D, DFF // N_DEV), (DFF // N_DEV, D))
W_NAMES = ("w_in", "w_a", "w_b", "w_out", "w_up", "w_down")
AG_COPIES = 7
HBM = pl.BlockSpec(memory_space=pltpu.HBM)
SEM = pl.BlockSpec(memory_space=pltpu.SEMAPHORE)


def _place():
    return lax.axis_index("x"), lax.axis_index("y"), lax.axis_index("c")


W_SLAB = 752
W_WIN = 272
PAD_BLK = 256


def _pad_runs():
    runs = sorted((pad, pad + b - a, p, a) for p in range(N_DEV) for a, b, pad in _shard_pieces(p))
    blocks = []
    for k in range(D_PAD // PAD_BLK):
        lo, hi = k * PAD_BLK, (k + 1) * PAD_BLK
        blocks.append([(max(lo, r0) - lo, min(hi, r1) - lo, p, a + max(lo, r0) - r0)
                       for r0, r1, p, a in runs if max(lo, r0) < min(hi, r1)])
    return blocks


def _gather_w_in(shard, x_in, g_attn, late_shards):
    blocks = _pad_runs()

    norm_t = 512
    n_chunks = S // norm_t

    n_late = len(late_shards)
    late_shapes = [t.shape for t in late_shards]

    def body(x_ref, xin_ref, g_ref, *refs):
        late_in, (out_ref, h1_ref), late_out = refs[:n_late], refs[n_late:n_late + 2], refs[n_late + 2:2 * n_late + 2]
        scratch = refs[2 * n_late + 2:]
        stage, land, xbuf = scratch[:3]
        late_stage, late_cast = scratch[3:3 + n_late], scratch[3 + n_late:3 + 2 * n_late]
        send_sems, recv_sems, load_sem, x_sems, late_sems, store_sems = scratch[3 + 2 * n_late:]
        x, y, c = _place()
        me, sibling = (x, y, c), (x, y, 1 - c)
        chips = [(1 - x, y), (x, 1 - y), (1 - x, 1 - y)]

        def slot(px, py, pc):
            return land.at[4 * px + 2 * py + pc]

        def copy(k, block, to):
            return pltpu.make_async_remote_copy(
                src_ref=slot(*block), dst_ref=slot(*block), send_sem=send_sems.at[k], recv_sem=recv_sems.at[k],
                device_id=to, device_id_type=MESH)

        load = pltpu.make_async_copy(x_ref, stage, load_sem)
        load.start()
        load.wait()
        land[4 * x + 2 * y + c] = stage[...].astype(BF16)
        sent = [copy(0, me, sibling)] + [copy(1 + j, me, (*chip, c)) for j, chip in enumerate(chips)]
        for cp in sent:
            cp.start()

        def x_load(i):
            return pltpu.make_async_copy(xin_ref.at[pl.ds(i * norm_t, norm_t)], xbuf.at[i % 2], x_sems.at[i % 2])

        late_loads = [pltpu.make_async_copy(late_in[i], late_stage[i], late_sems.at[i]) for i in range(n_late)]
        for ld in late_loads:
            ld.start()
        x_load(0).start()
        for i in range(n_chunks):
            if i + 1 < n_chunks:
                x_load(i + 1).start()
            x_load(i).wait()
            h1_ref[i * norm_t:(i + 1) * norm_t, :] = _rms_rows(xbuf[i % 2], g_ref[...]).astype(BF16)
        stores = []
        for i in range(n_late):
            late_loads[i].wait()
            late_cast[i][...] = late_stage[i][...].astype(BF16)
            stores.append(pltpu.make_async_copy(late_cast[i], late_out[i].at[4 * x + 2 * y + c], store_sems.at[i]))
            stores[-1].start()

        for j, chip in enumerate(chips):
            copy(1 + j, (*chip, c), me).wait_recv()
            sent.append(copy(4 + j, (*chip, c), sibling))
            sent[-1].start()
        copy(0, sibling, me).wait_recv()
        for j, chip in enumerate(chips):
            copy(4 + j, (*chip, 1 - c), me).wait_recv()
        for cp in sent:
            cp.wait_send()
        for st in stores:
            st.wait()

        row = lax.broadcasted_iota(jnp.int32, (PAD_BLK, D), 0)
        for k, runs in enumerate(blocks):
            out = jnp.zeros((PAD_BLK, D), F32)
            for o_lo, o_hi, p, a in runs:
                start = min(a // 16 * 16, W_SLAB - W_WIN)
                win = land[p, start:start + W_WIN, :].astype(F32)
                moved = pltpu.roll(win, (o_lo - (a - start)) % W_WIN, 0)[:PAD_BLK]
                out = jnp.where((row >= o_lo) & (row < o_hi), moved, out)
            out_ref[k * PAD_BLK:(k + 1) * PAD_BLK, :] = out.astype(BF16)

    return pl.pallas_call(
        body, name="all_gather_w_in",
        out_shape=[jax.ShapeDtypeStruct((D_PAD, D), BF16), jax.ShapeDtypeStruct((S, D), BF16)]
                  + [jax.ShapeDtypeStruct((N_DEV,) + sh, BF16) for sh in late_shapes],
        in_specs=[ANY, ANY, pl.BlockSpec(memory_space=pltpu.VMEM)] + [ANY] * n_late,
        out_specs=[pl.BlockSpec(memory_space=pltpu.VMEM)] * 2 + [ANY] * n_late,
        scratch_shapes=[pltpu.VMEM((W_SLAB, D), F32), pltpu.VMEM((N_DEV, W_SLAB, D), BF16),
                        pltpu.VMEM((2, norm_t, D), F32)]
                       + [pltpu.VMEM(sh, F32) for sh in late_shapes] + [pltpu.VMEM(sh, BF16) for sh in late_shapes]
                       + [pltpu.SemaphoreType.DMA((AG_COPIES,)), pltpu.SemaphoreType.DMA((AG_COPIES,)),
                          pltpu.SemaphoreType.DMA(()), pltpu.SemaphoreType.DMA((2,)),
                          pltpu.SemaphoreType.DMA((n_late,)), pltpu.SemaphoreType.DMA((n_late,))],
        compiler_params=_cparams(None),
    )(shard, x_in, g_attn, *late_shards)


PEER_COPIES = N_DEV - 1
SIBLING_COPIES = 4
CHIP_COPIES = 3


def _peer_copies(_, land_refs, send_sems, recv_sems):
    x, y, c = _place()
    mine = 4 * x + 2 * y + c
    copies = []
    for i, ref in enumerate(land_refs):
        for k in range(1, N_DEV):
            peer = (x ^ (k >> 2), y ^ ((k >> 1) & 1), c ^ (k & 1))
            n = i * PEER_COPIES + k - 1
            copies.append(pltpu.make_async_remote_copy(
                src_ref=ref.at[mine], dst_ref=ref.at[mine], send_sem=send_sems.at[n], recv_sem=recv_sems.at[n],
                device_id=peer, device_id_type=MESH))
    return copies


def _sibling_copies(g_refs, r_refs, send_sems, recv_sems):
    x, y, c = _place()
    return [pltpu.make_async_remote_copy(
        src_ref=g_refs[i].at[2 * k + (1 - c)], dst_ref=r_refs[i].at[k],
        send_sem=send_sems.at[SIBLING_COPIES * i + k], recv_sem=recv_sems.at[SIBLING_COPIES * i + k],
        device_id=(x, y, 1 - c), device_id_type=MESH) for i in range(len(g_refs)) for k in range(SIBLING_COPIES)]


def _sibling_half_copies(g_refs, r_refs, send_sems, recv_sems):
    x, y, c = _place()
    return [pltpu.make_async_remote_copy(
        src_ref=g_refs[i].at[k], dst_ref=r_refs[i].at[k],
        send_sem=send_sems.at[SIBLING_COPIES * i + k], recv_sem=recv_sems.at[SIBLING_COPIES * i + k],
        device_id=(x, y, 1 - c), device_id_type=MESH) for i in range(len(g_refs)) for k in range(SIBLING_COPIES)]


def _chip_copies(p_refs, r_refs, send_sems, recv_sems):
    x, y, c = _place()
    chips = [(1 - x, y), (x, 1 - y), (1 - x, 1 - y)]
    return [pltpu.make_async_remote_copy(
        src_ref=p_refs[i].at[2 * cx + cy], dst_ref=r_refs[i].at[j],
        send_sem=send_sems.at[CHIP_COPIES * i + j], recv_sem=recv_sems.at[CHIP_COPIES * i + j],
        device_id=(cx, cy, c), device_id_type=MESH) for i in range(len(p_refs)) for j, (cx, cy) in enumerate(chips)]


def _in_hbm(a):
    return pltpu.with_memory_space_constraint(a, pltpu.HBM)


def _exchange_start(name, copies_fn, n_copies, srcs, lands, after=()):
    n_s, n_l, n_a = len(srcs), len(lands), len(after)

    def body(*refs):
        outs = refs[n_s + n_l + n_a:]
        for cp in copies_fn(refs[:n_s], refs[n_s:n_s + n_l], outs[0], outs[1]):
            cp.start()
        outs[-1][...] = jnp.zeros_like(outs[-1])

    res = pl.pallas_call(
        body, name=name,
        out_shape=[pltpu.SemaphoreType.DMA((n_copies,)), pltpu.SemaphoreType.DMA((n_copies,))]
                  + [pltpu.HBM(t.shape, t.dtype) for t in (*srcs, *lands)] + [jax.ShapeDtypeStruct((8, LANES), F32)],
        in_specs=[HBM] * (n_s + n_l) + [ANY] * n_a,
        out_specs=[SEM, SEM] + [HBM] * (n_s + n_l) + [pl.BlockSpec(memory_space=pltpu.VMEM)],
        input_output_aliases={i: 2 + i for i in range(n_s + n_l)},
        compiler_params=pltpu.CompilerParams(has_side_effects=pltpu.SideEffectType.DATAFLOW_SIDE_EFFECTING),
    )(*[_in_hbm(t) for t in (*srcs, *lands)], *after)
    return res[0], res[1], list(res[2:2 + n_s]), list(res[2 + n_s:2 + n_s + n_l]), res[-1]


def _exchange_wait(name, copies_fn, started, after):
    send_sems, recv_sems, srcs, lands, _ = started
    n_s, n_l = len(srcs), len(lands)

    def body(*refs):
        for cp in copies_fn(refs[:n_s], refs[n_s:n_s + n_l], refs[n_s + n_l], refs[n_s + n_l + 1]):
            cp.wait_send()
            cp.wait_recv()

    res = pl.pallas_call(
        body, name=name,
        out_shape=[pltpu.HBM(t.shape, t.dtype) for t in (*srcs, *lands)],
        in_specs=[HBM] * (n_s + n_l) + [SEM, SEM, ANY],
        out_specs=[HBM] * (n_s + n_l),
        input_output_aliases={i: i for i in range(n_s + n_l)},
        compiler_params=pltpu.CompilerParams(has_side_effects=pltpu.SideEffectType.DATAFLOW_SIDE_EFFECTING),
    )(*srcs, *lands, send_sems, recv_sems, after)
    return list(res[:n_s]), list(res[n_s:])


def _shard_block(shape):
    rows, cols = shape
    if rows * cols <= 256 * 1024:
        return rows, cols
    if rows % 256:
        return rows, 512
    return max(256, 256 * 1024 // cols), cols


def _chip_partial(ids, g_stack, recv1, name):
    shape = g_stack.shape[1:]
    br, bc = shape
    whole = g_stack.shape[0] == N_DEV

    def body(ids_ref, g_ref, r_ref, pb_ref, own_ref):
        s = g_ref[...] + r_ref[...]
        pb_ref[...] = s.astype(BF16)

        @pl.when(pl.program_id(2) == ids_ref[1])
        def _():
            own_ref[...] = s

    grid_spec = pltpu.PrefetchScalarGridSpec(
        num_scalar_prefetch=1, grid=(shape[0] // br, shape[1] // bc, 4),
        in_specs=[pl.BlockSpec((None, br, bc), lambda r, q, k, ids: (2 * k + ids[0] if whole else k, r, q)),
                  pl.BlockSpec((None, br, bc), lambda r, q, k, ids: (k, r, q))],
        out_specs=[pl.BlockSpec((None, br, bc), lambda r, q, k, ids: (k, r, q)),
                   pl.BlockSpec((br, bc), lambda r, q, k, ids: (r, q))])
    return pl.pallas_call(
        body, name=name, grid_spec=grid_spec,
        out_shape=[jax.ShapeDtypeStruct((4,) + shape, BF16), jax.ShapeDtypeStruct(shape, F32)],
        compiler_params=_cparams(("parallel", "parallel", "arbitrary")),
    )(ids, g_stack, recv1)


def _adamw(w, g, m, v):
    m = ADAM_B1 * m + (1.0 - ADAM_B1) * g
    v = ADAM_B2 * v + (1.0 - ADAM_B2) * (g * g)
    m_hat = m / (1.0 - ADAM_B1 ** ADAM_STEP)
    v_hat = v / (1.0 - ADAM_B2 ** ADAM_STEP)
    delta = -ADAM_LR * (m_hat / (jnp.sqrt(v_hat) + ADAM_EPS) + ADAM_WD * w)
    return delta, m, v


def _reduce_adamw(own, recv2, w, m, v, name, deps=()):
    shape = own.shape
    br, bc = _shard_block(shape)

    def body(own_ref, r_ref, w_ref, m_ref, v_ref, *refs):
        g_ref, d_ref, nm_ref, nv_ref = refs[len(deps):]
        g = own_ref[...]
        for j in range(3):
            g = g + r_ref[j].astype(F32)
        delta, nm, nv = _adamw(w_ref[...], g, m_ref[...], v_ref[...])
        g_ref[...] = g
        d_ref[...] = delta
        nm_ref[...] = nm
        nv_ref[...] = nv

    blk = pl.BlockSpec((br, bc), lambda r, q: (r, q))
    return pl.pallas_call(
        body, name=name, grid=(shape[0] // br, shape[1] // bc),
        in_specs=[blk, pl.BlockSpec((3, br, bc), lambda r, q: (0, r, q)), blk, blk, blk] + [ANY] * len(deps),
        out_specs=[blk] * 4, out_shape=[jax.ShapeDtypeStruct(shape, F32)] * 4,
        compiler_params=_cparams(("parallel", "parallel")),
    )(own, recv2, w, m, v, *deps)


def _small_adamw(stack, w, m, v):
    def body(s_ref, w_ref, m_ref, v_ref, loss_ref, *o_refs):
        g = s_ref[0]
        for s in range(1, N_DEV):
            g = g + s_ref[s]
        delta, nm, nv = _adamw(w_ref[...], g, m_ref[...], v_ref[...])
        loss_ref[...] = g[LOSS_ROW:LOSS_ROW + 1, 0:1]
        for q, t in enumerate((g, delta, nm, nv)):
            attn_ref, b_ref, mlp_ref, final_ref = o_refs[4 * q:4 * q + 4]
            for i in range(8):
                cols = slice(i * LANES, (i + 1) * LANES)
                attn_ref[:, cols] = t[i:i + 1]
                mlp_ref[:, cols] = t[8 + i:9 + i]
                final_ref[:, cols] = t[16 + i:17 + i]
            b_ref[...] = t[24:25, :FOX_H]

    vm = pl.BlockSpec(memory_space=pltpu.VMEM)
    unpacked = [jax.ShapeDtypeStruct((1, n), F32) for n in (D, FOX_H, D, D)]
    return pl.pallas_call(
        body, name="small_adamw",
        in_specs=[vm] * 4, out_specs=[vm] * 17,
        out_shape=[jax.ShapeDtypeStruct((1, 1), F32)] + unpacked * 4,
    )(stack, w, m, v)


W_IN_SEGMENTS = ((0, 3 * FOX_W, SEG_A), (3 * FOX_W, 3 * FOX_W + FOX_H, SEG_F),
                 (3 * FOX_W + FOX_H, 3 * FOX_W + FOX_H + 3 * DIL_W, SEG_B), (3 * FOX_W + FOX_H + 3 * DIL_W, D_IN, SEG_G))


def _shard_pieces(p):
    rows = D_IN // N_DEV
    lo, hi = p * rows, (p + 1) * rows
    return [(max(lo, a) - lo, min(hi, b) - lo, pad + max(lo, a) - a)
            for a, b, pad in W_IN_SEGMENTS if max(lo, a) < min(hi, b)]


def _unpad_runs(p):
    rows = D_IN // N_DEV
    blocks = []
    for lo in range(0, rows, PAD_BLK):
        hi = min(lo + PAD_BLK, rows)
        blocks.append([(max(lo, a) - lo, min(hi, b) - lo, pad + max(lo, a) - a)
                       for a, b, pad in _shard_pieces(p) if max(lo, a) < min(hi, b)])
    return blocks


def _unpad_dw_in_t(dwp, ids, kept, deps=()):
    rows = D_IN // N_DEV
    plans = [_unpad_runs(p) for p in range(N_DEV)]
    windows = [[min(src // 8 * 8, D_PAD - W_WIN) for runs in plan for _, _, src in runs] for plan in plans]
    n_win = max(len(w) for w in windows)

    def body(ids_ref, src_ref, *refs):
        o_ref, buf, sems = refs[len(deps):]
        side = ids_ref[0] if kept else 1 - ids_ref[0]
        p = 2 * pl.program_id(0) + side
        row = lax.broadcasted_iota(jnp.int32, (PAD_BLK, D), 0)

        def fetch(q):
            slot = (q // 2) % 2
            return [pltpu.make_async_copy(src_ref.at[pl.ds(start, W_WIN)], buf.at[slot, n], sems.at[slot, n])
                    for n, start in enumerate(windows[q])]

        for q in range(N_DEV):
            @pl.when(p == q)
            def _(q=q):
                for nxt in ([q, q + 2] if q < 2 else [q + 2]):
                    if nxt < N_DEV:
                        for cp in fetch(nxt):
                            cp.start()
                copies, n = fetch(q), 0
                for j, runs in enumerate(plans[q]):
                    out = None
                    for o_lo, o_hi, src in runs:
                        copies[n].wait()
                        moved = pltpu.roll(buf[(q // 2) % 2, n], (o_lo - (src - windows[q][n])) % W_WIN, 0)[:PAD_BLK]
                        out = moved if out is None else jnp.where((row >= o_lo) & (row < o_hi), moved, out)
                        n += 1
                    size = min(PAD_BLK, rows - j * PAD_BLK)
                    o_ref[j * PAD_BLK:j * PAD_BLK + size, :] = out[:size]

    grid_spec = pltpu.PrefetchScalarGridSpec(
        num_scalar_prefetch=1, grid=(N_DEV // 2,),
        in_specs=[ANY] * (1 + len(deps)), out_specs=pl.BlockSpec((None, rows, D), lambda k, ids: (k, 0, 0)),
        scratch_shapes=[pltpu.VMEM((2, n_win, W_WIN, D), F32), pltpu.SemaphoreType.DMA((2, n_win))])
    return pl.pallas_call(
        body, name="unpad_dw_in_kept" if kept else "unpad_dw_in_sent", grid_spec=grid_spec,
        out_shape=jax.ShapeDtypeStruct((N_DEV // 2, rows, D), F32),
        compiler_params=_cparams(("arbitrary",)),
    )(ids, dwp, *deps)


LOSS_ROW = 25


def _pack_small(g_attn, b, g_mlp, g_final, loss_row=None):
    tail = jnp.pad(b, ((0, 7), (0, LANES - b.shape[1])))
    if loss_row is not None:
        tail = tail + jnp.pad(loss_row, ((LOSS_ROW - 24, 31 - LOSS_ROW), (0, 0)))
    return jnp.concatenate([g_attn.reshape(8, LANES), g_mlp.reshape(8, LANES), g_final.reshape(8, LANES), tail], axis=0)


class _StepComm:
    def __init__(self, ids, w_sh, m_sh, v_sh, lands, after_w_in):
        self.ids = ids
        self.w_sh, self.m_sh, self.v_sh = w_sh, m_sh, v_sh
        self.updates = None
        self.gather_mixer = _exchange_start("gather_mixer_start", _peer_copies, PEER_COPIES * 3, [], lands[:3],
                                            after=(after_w_in,))
        self.gather_mlp = _exchange_start("gather_mlp_start", _peer_copies, PEER_COPIES * 2, [], lands[3:],
                                          after=(self.gather_mixer[-1],))
        self.sibling = {}
        self.chips = {}
        self.own = {}
        self.names = {}
        self.copies_fn = {}
        self.kept = {}

    def _reduce_start(self, group, names, grads, after=(), copies_fn=_sibling_copies):
        lands = [lax.empty((SIBLING_COPIES,) + t.shape[1:], F32) for t in grads]
        self.sibling[group] = _exchange_start("grad_%s_sibling_start" % group, copies_fn,
                                              SIBLING_COPIES * len(grads), grads, lands, after=after)
        self.names[group] = names
        self.copies_fn[group] = copies_fn
        return self.sibling[group][-1]

    def _reduce_mid(self, group, after):
        grads, recv1 = _exchange_wait("grad_%s_sibling_wait" % group, self.copies_fn[group], self.sibling[group],
                                      after)
        grads = self.kept.get(group, grads)
        parts = [_chip_partial(self.ids, g, r, "grad_partial_" + n) for g, r, n in zip(grads, recv1, self.names[group])]
        self.own[group] = [p[1] for p in parts]
        srcs = [p[0] for p in parts]
        lands = [lax.empty((CHIP_COPIES,) + t.shape[1:], BF16) for t in srcs]
        self.chips[group] = _exchange_start("grad_%s_chips_start" % group, _chip_copies, CHIP_COPIES * len(srcs),
                                            srcs, lands)
        return self.chips[group][-1]

    def reduced(self, group, after):
        _, recv2 = _exchange_wait("grad_%s_chips_wait" % group, _chip_copies, self.chips[group], after)
        return list(zip(self.own[group], recv2))

    def first_deps(self):
        return [self.gather_mlp[-1]]

    def mixer_weights(self, after):
        _, (g_a, g_b, g_out) = _exchange_wait("gather_mixer_wait", _peer_copies, self.gather_mixer, after)
        return g_a, g_b, g_out.reshape(D, D)

    def mlp_weights(self, after):
        _, (g_up, g_down) = _exchange_wait("gather_mlp_wait", _peer_copies, self.gather_mlp, after)
        return g_up, g_down.reshape(DFF, D)

    def mlp_grads(self, dw_up_sh, dw_down):
        return [self._reduce_start("mlp", W_NAMES[4:], [dw_up_sh, dw_down.reshape((N_DEV,) + SHARD_SHAPES[5])])]

    def mixer_grads(self, dw_a, dw_b, dw_out):
        token = self._reduce_mid("mlp", dw_b)
        grads = [dw_a, dw_b, dw_out.reshape((N_DEV,) + SHARD_SHAPES[3])]
        return [self._reduce_start("mixer", W_NAMES[1:4], grads, after=(token,))]

    def mid_backward(self, after):
        return [self._reduce_mid("mixer", after)]

    def w_in_grad(self, dw_in_t):
        token = self._reduce_start("w_in", W_NAMES[:1], [_unpad_dw_in_t(dw_in_t, self.ids, kept=False)],
                                   copies_fn=_sibling_half_copies)
        self.kept["w_in"] = [_unpad_dw_in_t(dw_in_t, self.ids, kept=True, deps=[token])]
        reduced = self.reduced("mixer", token) + self.reduced("mlp", token)
        self.updates, last = [None] * len(reduced), self.kept["w_in"][0]
        for i in (3, 4, 0, 1, 2):
            if i == 4:
                last = self._reduce_mid("w_in", last)
            self.updates[i] = _reduce_adamw(*reduced[i], self.w_sh[1 + i], self.m_sh[1 + i], self.v_sh[1 + i],
                                            "adamw_" + W_NAMES[1 + i], deps=[last])
            last = self.updates[i][0]
        return [last]

    def w_in_update(self, after):
        (reduced,) = self.reduced("w_in", after)
        return _reduce_adamw(*reduced, self.w_sh[0], self.m_sh[0], self.v_sh[0], "adamw_" + W_NAMES[0])


def kernel(x, norm_attn_g, w_in, b_forget, w_branch_a, w_branch_b, w_out, norm_mlp_g, w_up, w_down, norm_final_g, loss_target, m_norm_attn_g, m_w_in, m_b_forget, m_w_branch_a, m_w_branch_b, m_w_out, m_norm_mlp_g, m_w_up, m_w_down, m_norm_final_g, v_norm_attn_g, v_w_in, v_b_forget, v_w_branch_a, v_w_branch_b, v_w_out, v_norm_mlp_g, v_w_up, v_w_down, v_norm_final_g):
    cx, cy, cc = _place()
    ids = jnp.stack([cc, 2 * cx + cy]).astype(jnp.int32)

    w_sh = [w_in[0].T] + [t[0] for t in (w_branch_a, w_branch_b, w_out, w_up, w_down)]
    m_sh = [m_w_in[0].T] + [t[0] for t in (m_w_branch_a, m_w_branch_b, m_w_out, m_w_up, m_w_down)]
    v_sh = [v_w_in[0].T] + [t[0] for t in (v_w_branch_a, v_w_branch_b, v_w_out, v_w_up, v_w_down)]

    w_in_t, h1, *lands = _gather_w_in(jnp.pad(w_sh[0], ((0, W_SLAB - w_sh[0].shape[0]), (0, 0))), x[0], norm_attn_g,
                                      w_sh[1:])
    comm = _StepComm(ids, w_sh, m_sh, v_sh, lands, w_in_t)
    b_pad = jnp.pad(b_forget, ((0, 0), (0, LANES - FOX_H)))

    loss_row, dx, dsmall = _local_step(
        x[0], loss_target[0], h1, norm_attn_g, norm_mlp_g, norm_final_g.reshape(1, D), b_pad, w_in_t, comm)

    dg_attn, db, dg_mlp, dg_final = dsmall
    stack = lax.dynamic_update_slice(jnp.zeros((N_DEV, SMALL_R, LANES), F32),
                                     _pack_small(dg_attn, db, dg_mlp, dg_final, loss_row)[None],
                                     (4 * cx + 2 * cy + cc, 0, 0))
    small_gather = _exchange_start("small_gather_start", _peer_copies, PEER_COPIES, [], [stack])
    w_in_update = comm.w_in_update(small_gather[-1])
    _, (stack,) = _exchange_wait("small_gather_wait", _peer_copies, small_gather, w_in_update[0])
    loss, *small = _small_adamw(
        stack,
        _pack_small(norm_attn_g, b_forget, norm_mlp_g, norm_final_g.reshape(1, D)),
        _pack_small(m_norm_attn_g, m_b_forget, m_norm_mlp_g, m_norm_final_g.reshape(1, D)),
        _pack_small(v_norm_attn_g, v_b_forget, v_norm_mlp_g, v_norm_final_g.reshape(1, D)))
    big = [w_in_update] + comm.updates

    outs = [loss.reshape(()), dx[None]]
    for q in range(4):
        s_attn, s_b, s_mlp, s_final = small[4 * q:4 * q + 4]
        s_final = s_final.reshape(D)
        b_in, b_a, b_b, b_out, b_up, b_down = [t[q][None] for t in big]
        b_in = jnp.swapaxes(b_in, 1, 2)
        outs += [s_attn, b_in, s_b, b_a, b_b, b_out, s_mlp, b_up, b_down, s_final]
    return tuple(outs)
```

```python
import functools

import jax
import jax.numpy as jnp
from jax import lax
from jax.experimental import pallas as pl
from jax.experimental.pallas import tpu as pltpu

F32 = jnp.float32
BF16 = jnp.bfloat16
MESH = pl.DeviceIdType.MESH

S = 2048
D = 1024
HD = 64
FOX_H = 8
FOX_W = FOX_H * HD
DIL_HG = 4
DIL_G = 3
DIL_W = DIL_G * DIL_HG * HD
DIL_OUT = DIL_HG * HD
DIL_BLK = 128
DIL_R = (1, 4, 16)
DFF = 4 * D
D_IN = 3 * FOX_W + FOX_H + 3 * DIL_W + 2 * D
EPS = 1e-6
NEG_INF = -1e30
SCALE = HD ** -0.5
ROPE_THETA = 500000.0
ROPE_DIM = HD // 4
N_DEV = 8

ADAM_LR = 0.001
ADAM_B1 = 0.9
ADAM_B2 = 0.999
ADAM_EPS = 1e-08
ADAM_WD = 0.01
ADAM_STEP = 10

LANES = 128
CB = 256
SEG_A = 0
SEG_B = 3 * FOX_W
SEG_Z = SEG_B + 3 * DIL_W
SEG_G = SEG_Z + CB
SEG_F = SEG_G + 2 * D
D_PAD = SEG_F + CB
DW_IN_TM = D_PAD // 10
SMALL_R = 32

VMEM_MB = 56


def _cparams(dims=None, vmem_mb=VMEM_MB, **kw):
    return pltpu.CompilerParams(dimension_semantics=dims, vmem_limit_bytes=vmem_mb << 20, **kw)


ANY = pl.BlockSpec(memory_space=pl.ANY)

_NN = (((1,), (0,)), ((), ()))
_NT = (((1,), (1,)), ((), ()))
_TN = (((0,), (0,)), ((), ()))


def _dot(a, b, dims):
    return lax.dot_general(a.astype(BF16), b.astype(BF16), dims, preferred_element_type=F32)


def _mm(a, b, *, mode, tm, tn, out_dtypes, name, n=None, k=None, a_off=0, b_off=0,
        b_sharded=False, out_sharded=False, epilogue=None, extras=(), deps=()):
    n_sh = b.shape[-1] if b_sharded else None
    if mode == "nn":
        m = a.shape[0]
        k = k or a.shape[1]
        a_spec = pl.BlockSpec((tm, k), lambda i, j: (i, a_off))
        if b_sharded:
            assert tn == n_sh
            n = N_DEV * n_sh
            b_spec = pl.BlockSpec((None, k, tn), lambda i, j: (j, 0, 0))
        else:
            n = n or b.shape[1]
            b_spec = pl.BlockSpec((k, tn), lambda i, j: (0, j + b_off))
        dims = _NN
    elif mode == "nt":
        m = a.shape[0]
        k = k or a.shape[1]
        a_spec = pl.BlockSpec((tm, k), lambda i, j: (i, a_off))
        if b_sharded:
            n = b.shape[1]
            b_spec = pl.BlockSpec((N_DEV, tn, n_sh), lambda i, j: (0, j, 0))
        else:
            n = n or b.shape[0]
            b_spec = pl.BlockSpec((tn, k), lambda i, j: (j + b_off, 0))
        dims = _NT
    else:
        k, m = a.shape
        n = n or b.shape[1]
        a_spec = pl.BlockSpec((k, tm), lambda i, j: (0, i))
        b_spec = pl.BlockSpec((k, tn), lambda i, j: (0, j + b_off))
        dims = _TN
    assert m % tm == 0 and n % tn == 0, (name, m, n, tm, tn)
    n_extra = len(extras)
    tile = pl.BlockSpec((tm, tn), lambda i, j: (i, j))
    split_tile = out_sharded and n == tn
    if out_sharded:
        assert mode == "tn" and n // tn in (1, N_DEV)
        if split_tile:
            out_spec = pl.BlockSpec((N_DEV, tm, n // N_DEV), lambda i, j: (0, i, 0))
        else:
            out_spec = pl.BlockSpec((None, tm, tn), lambda i, j: (j, i, 0))
        out_shape = (N_DEV, m, n // N_DEV)
    else:
        out_spec, out_shape = tile, (m, n)

    def body(a_ref, b_ref, *refs):
        if mode == "nt" and b_sharded:
            acc = _dot(a_ref[...], jnp.concatenate([b_ref[p] for p in range(N_DEV)], axis=1), dims)
        else:
            acc = _dot(a_ref[...], b_ref[...], dims)
        ex = [r[...] for r in refs[:n_extra]]
        outs = epilogue(acc, *ex) if epilogue is not None else (acc,)
        for o_ref, o in zip(refs[n_extra + len(deps):], outs):
            if split_tile:
                w = n // N_DEV
                for p in range(N_DEV):
                    o_ref[p] = o[:, p * w:(p + 1) * w].astype(o_ref.dtype)
            else:
                o_ref[...] = o.astype(o_ref.dtype)

    res = pl.pallas_call(
        body, name=name, grid=(m // tm, n // tn),
        in_specs=[a_spec, b_spec] + [tile] * n_extra + [ANY] * len(deps),
        out_specs=[out_spec] * len(out_dtypes),
        out_shape=[jax.ShapeDtypeStruct(out_shape, dt) for dt in out_dtypes],
        compiler_params=_cparams(("parallel", "parallel")),
    )(a, b, *extras, *deps)
    return res if len(out_dtypes) > 1 else res[0]


def _mm_rows(a, b, *, tm, name, epilogue, rows=(), vecs=(), row_out=(), vec_out=(), b_sharded=False, deps=()):
    m, k = a.shape
    n_rows, n_vecs, n_deps = len(rows), len(vecs), len(deps)
    n_sh = b.shape[-1] if b_sharded else None

    def body(a_ref, b_ref, *refs):
        row_refs, vec_refs = refs[:n_rows], refs[n_rows:n_rows + n_vecs]
        outs = refs[n_rows + n_vecs + n_deps:]
        if b_sharded:
            acc = _dot(a_ref[:, 0:n_sh], b_ref[0], _NT)
            for p in range(1, N_DEV):
                acc = acc + _dot(a_ref[:, p * n_sh:(p + 1) * n_sh], b_ref[p], _NT)
        else:
            acc = _dot(a_ref[...], b_ref[...], _NN)
        row_vals, vec_incs = epilogue(acc, [r[...] for r in row_refs], [v[...] for v in vec_refs])
        for o_ref, val in zip(outs[:len(row_out)], row_vals):
            o_ref[...] = val.astype(o_ref.dtype)

        @pl.when(pl.program_id(0) == 0)
        def _():
            for o_ref in outs[len(row_out):]:
                o_ref[...] = jnp.zeros_like(o_ref)

        for o_ref, inc in zip(outs[len(row_out):], vec_incs):
            o_ref[...] += inc

    tile = pl.BlockSpec((tm, D), lambda i: (i, 0))
    b_spec = pl.BlockSpec(b.shape, lambda i: (0,) * b.ndim)
    return pl.pallas_call(
        body, name=name, grid=(m // tm,),
        in_specs=[pl.BlockSpec((tm, k), lambda i: (i, 0)), b_spec] + [tile] * n_rows
                 + [pl.BlockSpec(v.shape, lambda i: (0, 0)) for v in vecs] + [ANY] * n_deps,
        out_specs=[tile] * len(row_out) + [pl.BlockSpec((1, w), lambda i: (0, 0)) for w in vec_out],
        out_shape=[jax.ShapeDtypeStruct((m, D), dt) for dt in row_out]
                  + [jax.ShapeDtypeStruct((1, w), F32) for w in vec_out],
        compiler_params=_cparams(("arbitrary",)),
    )(a, b, *rows, *vecs, *deps)


ROW_T = 256
ROWS_TM = 512


def _rms_rows(x, g):
    r = lax.rsqrt(jnp.mean(x * x, axis=-1, keepdims=True) + EPS)
    return (x * r) * g


def _rms_bwd_rows(dh, x, dres, g):
    r = lax.rsqrt(jnp.mean(x * x, axis=-1, keepdims=True) + EPS)
    xn = x * r
    dhn = dh * g
    dx = dres + r * (dhn - xn * jnp.mean(dhn * xn, axis=-1, keepdims=True))
    return dx, jnp.sum(dh * xn, axis=0, keepdims=True)


def _final_loss_rows(x, g, tgt):
    r = lax.rsqrt(jnp.mean(x * x, axis=-1, keepdims=True) + EPS)
    xn = x * r
    err = xn * g - tgt
    row_loss = jnp.mean(err * err, axis=-1, keepdims=True)
    loss = 0.5 * jnp.sum(row_loss, axis=0, keepdims=True) * jnp.ones((1, LANES), F32)
    dy = err * (1.0 / D)
    dyn = dy * g
    dx = r * (dyn - xn * jnp.mean(dyn * xn, axis=-1, keepdims=True))
    return dx, jnp.sum(dy * xn, axis=0, keepdims=True), loss


def _sigmoid(z):
    return 1.0 / (1.0 + jnp.exp(-z))


GATE_TR = 512


def _mixer_tail(oa, ob, w_a, w_b, w_out, proj_g, x, g_mlp):
    def body(oa_ref, ob_ref, wa_ref, wb_ref, wo_ref, g_ref, x_ref, gm_ref, ya_ref, yb_ref, mixed_ref, x2_ref, h2_ref):
        ya = _dot(oa_ref[...], jnp.concatenate([wa_ref[p] for p in range(N_DEV)], axis=1), _NN)
        yb = _dot(ob_ref[...], jnp.concatenate([wb_ref[p] for p in range(N_DEV)], axis=1), _NN)
        ya_ref[...] = ya
        yb_ref[...] = yb
        mixed = (_sigmoid(g_ref[:, :D]) * ya + _sigmoid(g_ref[:, D:]) * yb).astype(BF16)
        mixed_ref[...] = mixed
        x2 = x_ref[...] + _dot(mixed, wo_ref[...], _NN)
        x2_ref[...] = x2
        h2_ref[...] = _rms_rows(x2, gm_ref[...]).astype(BF16)

    def rows(width):
        return pl.BlockSpec((GATE_TR, width), lambda i: (i, 0))

    def whole(t):
        return pl.BlockSpec(t.shape, lambda i: (0,) * t.ndim)

    return pl.pallas_call(
        body, name="mixer_tail", grid=(S // GATE_TR,),
        in_specs=[rows(FOX_W), rows(DIL_OUT), whole(w_a), whole(w_b), whole(w_out), rows(2 * D), rows(D), whole(g_mlp)],
        out_specs=[rows(D)] * 5,
        out_shape=[jax.ShapeDtypeStruct((S, D), dt) for dt in (F32, F32, BF16, F32, BF16)],
        compiler_params=_cparams(("parallel",)),
    )(oa, ob, w_a, w_b, w_out, proj_g, x, g_mlp)


def _out_proj_gate_bwd(dx2, w_out, proj_g, ya, yb):
    def body(dx_ref, w_ref, g_ref, ya_ref, yb_ref, dy_ref, dg_ref):
        dm = _dot(dx_ref[...], w_ref[...], _NT)
        for half, y_ref in enumerate((ya_ref, yb_ref)):
            cols = slice(half * D, (half + 1) * D)
            s = _sigmoid(g_ref[:, cols])
            dy_ref[:, cols] = (dm * s).astype(BF16)
            dg_ref[:, cols] = (dm * y_ref[...] * (s * (1.0 - s))).astype(BF16)

    row = pl.BlockSpec((GATE_TR, D), lambda i: (i, 0))
    wide = pl.BlockSpec((GATE_TR, 2 * D), lambda i: (i, 0))
    return pl.pallas_call(
        body, name="out_proj_gate_bwd", grid=(S // GATE_TR,),
        in_specs=[row, pl.BlockSpec((D, D), lambda i: (0, 0)), wide, row, row],
        out_specs=[wide, pl.BlockSpec((GATE_TR, 2 * D), lambda i: (i, SEG_G // (2 * D)))],
        out_shape=[jax.ShapeDtypeStruct((S, 2 * D), BF16), jax.ShapeDtypeStruct((S, D_PAD), BF16)],
        compiler_params=_cparams(("parallel",)),
    )(dx2, w_out, proj_g, ya, yb)


FOX_TQ = 512
FOX_TQ_FWD = 1024


def _scan_rows(x, reverse):
    n = x.shape[0]
    row = lax.broadcasted_iota(jnp.int32, x.shape, 0)
    k = 1
    while k < n:
        if reverse:
            x = x + jnp.where(row < n - k, pltpu.roll(x, n - k, 0), 0.0)
        else:
            x = x + jnp.where(row >= k, pltpu.roll(x, k, 0), 0.0)
        k *= 2
    return x


def _fox_dscan(dft, dfs, proj_f, b_pad, dproj):
    def body(dft_ref, dfs_ref, f_ref, b_ref, _, dfa_ref, db_ref):
        dfs_pad = jnp.concatenate([dfs_ref[...], jnp.zeros((LANES - FOX_H, S), F32)], axis=0)
        df = dfs_pad.T + dft_ref[0]
        for hp in range(1, dft_ref.shape[0]):
            df = df + pltpu.roll(dft_ref[hp], 2 * hp, 1)
        dlf = _scan_rows(df, reverse=True)
        z = f_ref[...] + b_ref[...]
        lane = lax.broadcasted_iota(jnp.int32, (S, LANES), 1)
        dfa = jnp.where(lane < FOX_H, dlf / (1.0 + jnp.exp(z)), 0.0)
        dfa_ref[:, :LANES] = dfa.astype(BF16)
        dfa_ref[:, LANES:] = jnp.zeros((S, CB - LANES), BF16)
        db_ref[...] = jnp.sum(dfa, axis=0, keepdims=True)

    return pl.pallas_call(
        body, name="fox_dscan", grid=(1,),
        in_specs=[pl.BlockSpec(dft.shape, lambda i: (0, 0, 0)), pl.BlockSpec((FOX_H, S), lambda i: (0, 0)),
                  pl.BlockSpec((S, LANES), lambda i: (0, 0)), pl.BlockSpec((1, LANES), lambda i: (0, 0)), ANY],
        out_specs=[pl.BlockSpec((S, CB), lambda i: (0, SEG_F // CB)), pl.BlockSpec((1, LANES), lambda i: (0, 0))],
        out_shape=[jax.ShapeDtypeStruct((S, D_PAD), BF16), jax.ShapeDtypeStruct((1, LANES), F32)],
        input_output_aliases={4: 0},
        compiler_params=_cparams(("arbitrary",)),
    )(dft, dfs, proj_f, b_pad, dproj)


FOX_NQ = S // FOX_TQ
FOX_HP = FOX_W // LANES
HEADS_PER_LB = LANES // HD
FOX_AUG = 2 * LANES


def _fox_prepare(proj_a, proj_f, b_pad):
    tr = 512

    def body(q_ref, k_ref, f_ref, b_ref, qa_ref, ka_ref, hi_s, mid_s, lo_s):
        i = pl.program_id(0)

        @pl.when(i == 0)
        def _():
            z = f_ref[...] + b_ref[...]
            lf = jnp.minimum(z, 0.0) - jnp.log1p(jnp.exp(-jnp.abs(z)))
            f = _scan_rows(lf, reverse=False)
            hi = f.astype(BF16).astype(F32)
            r1 = f - hi
            mid = r1.astype(BF16).astype(F32)
            hi_s[...] = hi
            mid_s[...] = mid
            lo_s[...] = (r1 - mid).astype(BF16).astype(F32)

        rows = pl.ds(pl.multiple_of(i * tr, tr), tr)
        hi, mid, lo = hi_s[rows, :], mid_s[rows, :], lo_s[rows, :]
        lane = lax.broadcasted_iota(jnp.int32, (tr, HD), 1)
        q_ones = jnp.where((lane >= 3) & (lane < 6), 1.0, 0.0)
        k_ones = jnp.where(lane < 3, 1.0, 0.0)
        for h in range(FOX_H):
            a1, a2, a3 = hi[:, h:h + 1], mid[:, h:h + 1], lo[:, h:h + 1]
            q_extra = jnp.where(lane == 0, a1, jnp.where(lane == 1, a2, jnp.where(lane == 2, a3, q_ones)))
            k_extra = jnp.where(lane == 3, -a1, jnp.where(lane == 4, -a2, jnp.where(lane == 5, -a3, k_ones)))
            base = h * LANES
            qa_ref[:, base:base + HD] = q_ref[:, h * HD:(h + 1) * HD] * jnp.asarray(SCALE, BF16)
            qa_ref[:, base + HD:base + LANES] = q_extra.astype(BF16)
            ka_ref[:, base:base + HD] = k_ref[:, h * HD:(h + 1) * HD]
            ka_ref[:, base + HD:base + LANES] = k_extra.astype(BF16)

    out_blk = pl.BlockSpec((tr, FOX_H * LANES), lambda i: (i, 0))
    return pl.pallas_call(
        body, name="fox_prepare", grid=(S // tr,),
        in_specs=[pl.BlockSpec((tr, FOX_W), lambda i: (i, 0)), pl.BlockSpec((tr, FOX_W), lambda i: (i, 1)),
                  pl.BlockSpec((S, LANES), lambda i: (0, 0)), pl.BlockSpec((1, LANES), lambda i: (0, 0))],
        out_specs=[out_blk, out_blk],
        out_shape=[jax.ShapeDtypeStruct((S, FOX_H * LANES), BF16)] * 2,
        scratch_shapes=[pltpu.VMEM((S, LANES), F32)] * 3,
        compiler_params=_cparams(("arbitrary",)),
    )(proj_a, proj_a, proj_f, b_pad)


def _fox_scores(qa, ka, w, tq):
    s = _dot(qa, ka, _NT)
    row = lax.broadcasted_iota(jnp.int32, (tq, tq), 0)
    col = lax.broadcasted_iota(jnp.int32, (tq, tq), 1)
    diag = jnp.where(col <= row, s[:, w * tq:], NEG_INF)
    return diag if w == 0 else jnp.concatenate([s[:, :w * tq], diag], axis=1)


def _fox_fwd(q_aug, k_aug, proj_a):
    tq = FOX_TQ_FWD

    def body(qa_ref, ka_ref, v_ref, o_ref, lse_ref):
        qi = pl.program_id(1)
        lane = lax.broadcasted_iota(jnp.int32, (tq, LANES), 1)
        for w in range(S // tq):
            @pl.when(qi == w)
            def _(w=w):
                width = (w + 1) * tq
                lse = jnp.zeros((tq, LANES), F32)
                for hh in range(HEADS_PER_LB):
                    aug = slice(hh * LANES, (hh + 1) * LANES)
                    sl = slice(hh * HD, (hh + 1) * HD)
                    s = _fox_scores(qa_ref[:, aug], ka_ref[:width, aug], w, tq)
                    m = jnp.max(s, axis=-1, keepdims=True)
                    p = jnp.exp(s - m)
                    l = jnp.sum(p, axis=-1, keepdims=True)
                    o_ref[:, sl] = _dot(p / l, v_ref[:width, sl], _NN)
                    lse = jnp.where(lane == hh, m + jnp.log(l), lse)
                lse_ref[...] = lse

    return pl.pallas_call(
        body, name="fox_fwd", grid=(FOX_HP, S // tq),
        in_specs=[pl.BlockSpec((tq, FOX_AUG), lambda h, i: (i, h)),
                  pl.BlockSpec((S, FOX_AUG), lambda h, i: (0, h)),
                  pl.BlockSpec((S, LANES), lambda h, i: (0, 2 * FOX_HP + h))],
        out_specs=[pl.BlockSpec((tq, LANES), lambda h, i: (i, h)),
                   pl.BlockSpec((None, tq, LANES), lambda h, i: (h, i, 0))],
        out_shape=[jax.ShapeDtypeStruct((S, FOX_W), F32), jax.ShapeDtypeStruct((FOX_HP, S, LANES), F32)],
        compiler_params=_cparams(("parallel", "parallel")),
    )(q_aug, k_aug, proj_a)


def _fox_bwd(q_aug, k_aug, proj_a, lse, do, dproj, deps=()):
    n_q = FOX_NQ

    def body(qa_ref, ka_ref, v_ref, lse_ref, do_ref, *refs):
        dproj_ref, dft_ref, dfs_ref, dk_acc, dv_acc, dq_buf, kv_buf, dq_sems, kv_sems = refs[1 + len(deps):]
        hp = pl.program_id(0)
        t = pl.program_id(1)
        slot = t % 2
        col = pl.multiple_of(hp * LANES, LANES)

        def dq_copy(step, buf_slot):
            rows = pl.ds(pl.multiple_of(step * FOX_TQ, FOX_TQ), FOX_TQ)
            return pltpu.make_async_copy(dq_buf.at[buf_slot], dproj_ref.at[rows, pl.ds(col, LANES)], dq_sems.at[buf_slot])

        @pl.when(t == 0)
        def _():
            dk_acc[...] = jnp.zeros_like(dk_acc)
            dv_acc[...] = jnp.zeros_like(dv_acc)

        @pl.when((t == 0) & (hp == 0))
        def _():
            dfs_ref[...] = jnp.zeros_like(dfs_ref)

        @pl.when(t >= 2)
        def _():
            dq_copy(t - 2, slot).wait()

        lane = lax.broadcasted_iota(jnp.int32, (FOX_TQ, LANES), 1)
        for w in range(n_q):
            @pl.when(t == w)
            def _(w=w):
                width = (w + 1) * FOX_TQ
                lse_all = lse_ref[...]
                sub = lax.broadcasted_iota(jnp.int32, (FOX_H, width), 0)
                dft = jnp.zeros((FOX_TQ, LANES), F32)
                for hh in range(HEADS_PER_LB):
                    aug = slice(hh * LANES, (hh + 1) * LANES)
                    sl = slice(hh * HD, (hh + 1) * HD)
                    head = hp * HEADS_PER_LB + hh
                    qa = qa_ref[:, aug]
                    ka = ka_ref[:width, aug]
                    do_h = do_ref[:, sl]
                    s = _fox_scores(qa, ka, w, FOX_TQ)
                    p = jnp.exp(s - lse_all[:, hh:hh + 1])
                    dp = _dot(do_h, v_ref[:width, sl], _NT)
                    ds = p * (dp - jnp.sum(dp * p, axis=-1, keepdims=True))
                    dft = jnp.where(lane == hh, jnp.sum(ds, axis=-1, keepdims=True), dft)
                    dfs_ref[:, :width] -= jnp.where(sub == head, jnp.sum(ds, axis=0, keepdims=True), 0.0)
                    dq_buf[slot, :, sl] = (_dot(ds, ka[:, :HD], _NN) * SCALE).astype(BF16)
                    dk_acc[:width, sl] += _dot(ds, qa[:, :HD], _TN)
                    dv_acc[:width, sl] += _dot(p, do_h, _TN)
                dft_ref[...] = dft

        dq_copy(t, slot).start()

        @pl.when(t == n_q - 1)
        def _():
            dq_copy(t - 1, 1 - slot).wait()
            dq_copy(t, slot).wait()
            kv_buf[0] = dk_acc[...].astype(BF16)
            kv_buf[1] = dv_acc[...].astype(BF16)
            copies = [pltpu.make_async_copy(
                kv_buf.at[j], dproj_ref.at[:, pl.ds(pl.multiple_of((j + 1) * FOX_W + hp * LANES, LANES), LANES)],
                kv_sems.at[j]) for j in range(2)]
            for cp in copies:
                cp.start()
            for cp in copies:
                cp.wait()

    qblk = pl.BlockSpec((FOX_TQ, FOX_AUG), lambda h, t: (t, h))
    lane_blk = pl.BlockSpec((None, FOX_TQ, LANES), lambda h, t: (h, t, 0))
    return pl.pallas_call(
        body, name="fox_bwd", grid=(FOX_HP, n_q),
        in_specs=[qblk, pl.BlockSpec((S, FOX_AUG), lambda h, t: (0, h)),
                  pl.BlockSpec((S, LANES), lambda h, t: (0, 2 * FOX_HP + h)),
                  lane_blk, pl.BlockSpec((FOX_TQ, LANES), lambda h, t: (t, h)), ANY] + [ANY] * len(deps),
        out_specs=[ANY, lane_blk, pl.BlockSpec((FOX_H, S), lambda h, t: (0, 0))],
        out_shape=[jax.ShapeDtypeStruct((S, D_PAD), BF16),
                   jax.ShapeDtypeStruct((FOX_HP, S, LANES), F32), jax.ShapeDtypeStruct((FOX_H, S), F32)],
        scratch_shapes=[pltpu.VMEM((S, LANES), F32), pltpu.VMEM((S, LANES), F32),
                        pltpu.VMEM((2, FOX_TQ, LANES), BF16), pltpu.VMEM((2, S, LANES), BF16),
                        pltpu.SemaphoreType.DMA((2,)), pltpu.SemaphoreType.DMA((2,))],
        input_output_aliases={5: 0},
        compiler_params=_cparams(("arbitrary", "arbitrary")),
    )(q_aug, k_aug, proj_a, lse, do, dproj, *deps)


def _rope_tables():
    half = ROPE_DIM // 2
    shape = (DIL_G, S, half)
    g = lax.broadcasted_iota(jnp.int32, shape, 0)
    row = lax.broadcasted_iota(jnp.int32, shape, 1)
    r = jnp.left_shift(1, 2 * g)
    per_class = S // r
    pos = (row % per_class) * r + row // per_class
    inv_freq = jnp.power(jnp.float32(ROPE_THETA), -lax.broadcasted_iota(F32, shape, 2) * 2.0 / ROPE_DIM)
    ang = pos.astype(F32) * inv_freq
    return jnp.concatenate([jnp.cos(ang), jnp.sin(ang)], axis=-1)


def _rope_factors(cs):
    half = ROPE_DIM // 2
    i = lax.broadcasted_iota(jnp.int32, (2 * half, 2 * LANES), 0)
    l = lax.broadcasted_iota(jnp.int32, (2 * half, 2 * LANES), 1)
    hit = ((l < LANES) & (i == l % half)) | ((l >= LANES) & (i - half == l % half))
    spread = jnp.where(hit, 1.0, 0.0).astype(BF16)
    hi = cs.astype(BF16)
    r1 = cs - hi.astype(F32)
    mid = r1.astype(BF16)
    lo = (r1 - mid.astype(F32)).astype(BF16)
    full = (_dot(hi, spread, _NN) + _dot(mid, spread, _NN)) + _dot(lo, spread, _NN)
    cos, sin = full[:, :LANES], full[:, LANES:]
    lane = lax.broadcasted_iota(jnp.int32, cos.shape, 1) % HD
    t_self = jnp.where(lane < ROPE_DIM, cos, 1.0)
    t_up = jnp.where(lane < half, -sin, 0.0)
    t_dn = jnp.where((lane >= half) & (lane < ROPE_DIM), sin, 0.0)
    return t_self, t_up, t_dn


def _residue_pieces(r):
    if r == 1:
        return [(slice(i, i + 512), slice(i, i + 512)) for i in range(0, S, 512)]
    n = S // r
    return [(pl.ds(j, n, stride=r), slice(j * n, (j + 1) * n)) for j in range(r)]


def _rope_apply(x, a, u, d, backward):
    half = ROPE_DIM // 2
    if backward:
        return x * a + pltpu.roll(x * u, half, 1) + pltpu.roll(x * d, LANES - half, 1)
    return x * a + pltpu.roll(x, LANES - half, 1) * u + pltpu.roll(x, half, 1) * d


def _rope_cache_factors(cs_ref, fac, first_step_of_group):
    @pl.when(first_step_of_group)
    def _():
        for lo in range(0, S, 512):
            for j, f in enumerate(_rope_factors(cs_ref[lo:lo + 512, :])):
                fac[j, lo:lo + 512, :] = f


LB_PER_CB = CB // LANES
ROPE_NB = 3 * DIL_G * LB_PER_CB


def _rope_step(s):
    per_group = 3 * LB_PER_CB
    return s // per_group, (s % per_group) // LB_PER_CB, s % LB_PER_CB


def _rope_split(proj_b, tables):
    def body(x_ref, cs_ref, o_ref, fac):
        g, t, hf = _rope_step(pl.program_id(0))
        _rope_cache_factors(cs_ref, fac, (t == 0) & (hf == 0))
        for gs in range(DIL_G):
            @pl.when(g == gs)
            def _(gs=gs):
                for tok, sub in _residue_pieces(DIL_R[gs]):
                    x = x_ref[tok, :]
                    y = _rope_apply(x, fac[0, sub, :], fac[1, sub, :], fac[2, sub, :], False)
                    o_ref[sub, :] = jnp.where(t < 2, y, x).astype(BF16)

    def in_index(s):
        g, t, hf = _rope_step(s)
        return 0, (t * DIL_G + g) * LB_PER_CB + hf

    def out_index(s):
        g, t, hf = _rope_step(s)
        return t * DIL_G + g, 0, hf

    tab = pl.BlockSpec((None, S, 2 * (ROPE_DIM // 2)), lambda s: (_rope_step(s)[0], 0, 0))
    return pl.pallas_call(
        body, name="rope_split", grid=(ROPE_NB,),
        in_specs=[pl.BlockSpec((S, LANES), in_index), tab],
        out_specs=pl.BlockSpec((None, S, LANES), out_index),
        out_shape=jax.ShapeDtypeStruct((3 * DIL_G, S, CB), BF16),
        scratch_shapes=[pltpu.VMEM((3, S, LANES), F32)],
        compiler_params=_cparams(("arbitrary",)),
    )(proj_b, tables)


def _rope_merge_bwd(dq3, dk3, dv3, tables, dproj):
    def body(dq_ref, dk_ref, dv_ref, cs_ref, _, o_ref, tmp, fac):
        s = pl.program_id(0)
        g, t, hf = _rope_step(s)
        _rope_cache_factors(cs_ref, fac, (t == 0) & (hf == 0) & (s < ROPE_NB))
        for gs in range(DIL_G):
            @pl.when((g == gs) & (s < ROPE_NB))
            def _(gs=gs):
                for tok, sub in _residue_pieces(DIL_R[gs]):
                    x = jnp.where(t == 0, dq_ref[sub, :], jnp.where(t == 1, dk_ref[sub, :], dv_ref[sub, :]))
                    y = _rope_apply(x, fac[0, sub, :], fac[1, sub, :], fac[2, sub, :], True)
                    tmp[tok, :] = jnp.where(t < 2, y, x)
                o_ref[...] = tmp[...].astype(BF16)

        @pl.when(s >= ROPE_NB)
        def _():
            o_ref[...] = jnp.zeros_like(o_ref)

    def src_spec(own):
        def index(s):
            g, t, hf = _rope_step(jnp.minimum(s, ROPE_NB - 1))
            return g, 0, jnp.where(t == own, hf, jnp.where(t > own, LB_PER_CB - 1, 0))
        return pl.BlockSpec((None, S, LANES), index)

    def out_index(s):
        g, t, hf = _rope_step(s)
        col = SEG_B // LANES + (t * DIL_G + g) * LB_PER_CB + hf
        return 0, jnp.where(s < ROPE_NB, col, SEG_Z // LANES + s - ROPE_NB)

    tab = pl.BlockSpec((None, S, 2 * (ROPE_DIM // 2)), lambda s: (_rope_step(jnp.minimum(s, ROPE_NB - 1))[0], 0, 0))
    return pl.pallas_call(
        body, name="rope_merge_bwd", grid=(ROPE_NB + LB_PER_CB,),
        in_specs=[src_spec(0), src_spec(1), src_spec(2), tab, ANY],
        out_specs=pl.BlockSpec((S, LANES), out_index),
        out_shape=jax.ShapeDtypeStruct((S, D_PAD), BF16),
        scratch_shapes=[pltpu.VMEM((S, LANES), F32), pltpu.VMEM((3, S, LANES), F32)],
        input_output_aliases={4: 0},
        compiler_params=_cparams(("arbitrary",)),
    )(dq3, dk3, dv3, tables, dproj)


DIL_NB = S // DIL_BLK


_BQK = (((2,), (2,)), ((0,), (0,)))
_BQD = (((2,), (1,)), ((0,), (0,)))
_BKD = (((1,), (1,)), ((0,), (0,)))


def _dil_mask(g):
    shape = (DIL_NB, DIL_BLK, 2 * DIL_BLK)
    blocks_per_seq = jnp.right_shift(DIL_NB, 2 * g)
    has_prev = jnp.bitwise_and(lax.broadcasted_iota(jnp.int32, shape, 0), blocks_per_seq - 1) != 0
    a = lax.broadcasted_iota(jnp.int32, shape, 1)
    kk = lax.broadcasted_iota(jnp.int32, shape, 2)
    diff = DIL_BLK + a - kk
    return (diff >= 0) & (diff <= DIL_BLK) & ((kk >= DIL_BLK) | has_prev)


def _blocks(t):
    return t.reshape(DIL_NB, DIL_BLK, t.shape[-1])


def _with_prev(t):
    prev = jnp.concatenate([jnp.zeros((DIL_BLK, t.shape[-1]), t.dtype), t[:-DIL_BLK]], axis=0)
    return jnp.concatenate([_blocks(prev), _blocks(t)], axis=1)


def _fold_prev(t2):
    n = t2.shape[-1]
    to_prev = t2[:, :DIL_BLK].reshape(S, n)
    own = t2[:, DIL_BLK:].reshape(S, n)
    return own + jnp.concatenate([to_prev[DIL_BLK:], jnp.zeros((DIL_BLK, n), t2.dtype)], axis=0)


def _dil_group_spec(t, width=DIL_OUT):
    return pl.BlockSpec((None, S, width), lambda g: (t * DIL_G + g, 0, 0))


def _dil_fwd(qkv3):
    def body(q_ref, k_ref, v_ref, o_ref, lse_ref):
        ok = _dil_mask(pl.program_id(0))
        lane = lax.broadcasted_iota(jnp.int32, (S, LANES), 1)
        lse = jnp.zeros((S, LANES), F32)
        for h in range(DIL_HG):
            sl = slice(h * HD, (h + 1) * HD)
            s = jnp.where(ok, _dot(_blocks(q_ref[:, sl]), _with_prev(k_ref[:, sl]), _BQK) * SCALE, NEG_INF)
            m = jnp.max(s, axis=-1, keepdims=True)
            p = jnp.exp(s - m)
            l = jnp.sum(p, axis=-1, keepdims=True)
            o_ref[:, sl] = _dot(p / l, _with_prev(v_ref[:, sl]), _BQD).reshape(S, HD)
            lse = jnp.where(lane == h, (m + jnp.log(l)).reshape(S, 1), lse)
        lse_ref[...] = lse

    return pl.pallas_call(
        body, name="dil_fwd", grid=(DIL_G,),
        in_specs=[_dil_group_spec(0), _dil_group_spec(1), _dil_group_spec(2)],
        out_specs=[_dil_group_spec(0), _dil_group_spec(0, LANES)],
        out_shape=[jax.ShapeDtypeStruct((DIL_G, S, DIL_OUT), F32), jax.ShapeDtypeStruct((DIL_G, S, LANES), F32)],
        compiler_params=_cparams(("parallel",)),
    )(qkv3, qkv3, qkv3)


def _dil_bwd(qkv3, lse3, do3, c3):
    def body(q_ref, k_ref, v_ref, lse_ref, do_ref, c_ref, dq_ref, dk_ref, dv_ref):
        ok = _dil_mask(pl.program_id(0))
        lse = lse_ref[...]
        c = c_ref[...]
        for h in range(DIL_HG):
            sl = slice(h * HD, (h + 1) * HD)
            q = _blocks(q_ref[:, sl])
            k2 = _with_prev(k_ref[:, sl])
            do_h = _blocks(do_ref[:, sl])
            s = jnp.where(ok, _dot(q, k2, _BQK) * SCALE, NEG_INF)
            p = jnp.exp(s - _blocks(lse[:, h:h + 1]))
            dp = _dot(do_h, _with_prev(v_ref[:, sl]), _BQK)
            ds = p * (dp - _blocks(c[:, h:h + 1]))
            dsb = (ds * SCALE).astype(BF16)
            dq_ref[:, sl] = _dot(dsb, k2, _BQD).reshape(S, HD)
            dk_ref[:, sl] = _fold_prev(_dot(dsb, q, _BKD))
            dv_ref[:, sl] = _fold_prev(_dot(p, do_h, _BKD))

    return pl.pallas_call(
        body, name="dil_bwd", grid=(DIL_G,),
        in_specs=[_dil_group_spec(0), _dil_group_spec(1), _dil_group_spec(2), _dil_group_spec(0, LANES),
                  _dil_group_spec(0), _dil_group_spec(0, LANES)],
        out_specs=[_dil_group_spec(0)] * 3,
        out_shape=[jax.ShapeDtypeStruct((DIL_G, S, DIL_OUT), F32)] * 3,
        compiler_params=_cparams(("parallel",)),
    )(qkv3, qkv3, qkv3, lse3, do3, c3)


COMB_T = 256


def _dil_combine(o3, lse3):
    heads_per_lb = LANES // HD

    def body(o_ref, lse_ref, ob_ref, al_ref, *scratch):
        o_tok = [scratch[LB_PER_CB * gg:LB_PER_CB * (gg + 1)] for gg in range(DIL_G)]
        lse_tok = scratch[LB_PER_CB * DIL_G:]
        g = pl.program_id(0)
        for gs in range(DIL_G):
            @pl.when(g == gs)
            def _(gs=gs):
                for tok, sub in _residue_pieces(DIL_R[gs]):
                    for hf in range(LB_PER_CB):
                        o_tok[gs][hf][tok, :] = o_ref[sub, hf * LANES:(hf + 1) * LANES]
                    lse_tok[gs][tok, :] = lse_ref[sub, :]

        @pl.when(g == DIL_G - 1)
        def _():
            def chunk(i, carry):
                rows = pl.ds(pl.multiple_of(i * COMB_T, COMB_T), COMB_T)
                lse = [lse_tok[gg][rows, :] for gg in range(DIL_G)]
                m = jnp.maximum(jnp.maximum(lse[0], lse[1]), lse[2])
                e = [jnp.exp(lse[gg] - m) for gg in range(DIL_G)]
                den = (e[0] + e[1]) + e[2]
                al = [e[gg] / den for gg in range(DIL_G)]
                for gg in range(DIL_G):
                    al_ref[gg, rows, :] = al[gg]
                for h in range(DIL_HG):
                    hf, sl = h // heads_per_lb, slice((h % heads_per_lb) * HD, (h % heads_per_lb + 1) * HD)
                    acc = al[0][:, h:h + 1] * o_tok[0][hf][rows, sl]
                    for gg in range(1, DIL_G):
                        acc = acc + al[gg][:, h:h + 1] * o_tok[gg][hf][rows, sl]
                    ob_ref[rows, h * HD:(h + 1) * HD] = acc
                return carry

            lax.fori_loop(0, S // COMB_T, chunk, 0)

    return pl.pallas_call(
        body, name="dil_combine", grid=(DIL_G,),
        in_specs=[pl.BlockSpec((None, S, DIL_OUT), lambda g: (g, 0, 0)),
                  pl.BlockSpec((None, S, LANES), lambda g: (g, 0, 0))],
        out_specs=[pl.BlockSpec((S, DIL_OUT), lambda g: (0, 0)),
                   pl.BlockSpec((DIL_G, S, LANES), lambda g: (0, 0, 0))],
        out_shape=[jax.ShapeDtypeStruct((S, DIL_OUT), F32), jax.ShapeDtypeStruct((DIL_G, S, LANES), F32)],
        scratch_shapes=[pltpu.VMEM((S, LANES), F32)] * (DIL_G * (LB_PER_CB + 1)),
        compiler_params=_cparams(("arbitrary",)),
    )(o3, lse3)


def _dil_combine_bwd(dob, ob, alpha, deps=()):
    heads_per_lb = LANES // HD

    def body(dob_ref, ob_ref, al_ref, *refs):
        do_ref, c_ref = refs[len(deps):]
        g = pl.program_id(0)
        hf = pl.program_id(1)
        for gs in range(DIL_G):
            @pl.when(g == gs)
            def _(gs=gs):
                for tok, sub in _residue_pieces(DIL_R[gs]):
                    dob = dob_ref[tok, :]
                    prod = dob * ob_ref[tok, :]
                    al = al_ref[tok, :]
                    lane = lax.broadcasted_iota(jnp.int32, al.shape, 1)
                    c = jnp.where(hf == 0, 0.0, c_ref[sub, :])
                    for hh in range(heads_per_lb):
                        sl = slice(hh * HD, (hh + 1) * HD)
                        head = hf * heads_per_lb + hh
                        a = jnp.sum(jnp.where(lane == head, al, 0.0), axis=-1, keepdims=True)
                        do_ref[sub, sl] = (a * dob[:, sl]).astype(BF16)
                        c = jnp.where(lane == head, a * jnp.sum(prod[:, sl], axis=-1, keepdims=True), c)
                    c_ref[sub, :] = c

    half = pl.BlockSpec((S, LANES), lambda g, hf: (0, hf))
    return pl.pallas_call(
        body, name="dil_combine_bwd", grid=(DIL_G, LB_PER_CB),
        in_specs=[half, half, pl.BlockSpec((None, S, LANES), lambda g, hf: (g, 0, 0))] + [ANY] * len(deps),
        out_specs=[pl.BlockSpec((None, S, LANES), lambda g, hf: (g, 0, hf)),
                   pl.BlockSpec((None, S, LANES), lambda g, hf: (g, 0, 0))],
        out_shape=[jax.ShapeDtypeStruct((DIL_G, S, DIL_OUT), BF16), jax.ShapeDtypeStruct((DIL_G, S, LANES), F32)],
        compiler_params=_cparams(("parallel", "arbitrary")),
    )(dob, ob, alpha, *deps)


def _local_step(x, tgt, h1, g_attn, g_mlp, g_final, b_pad, w_in_t, hooks):
    tables = _rope_tables()

    first = hooks.first_deps()
    proj_a = _mm(h1, w_in_t, mode="nt", tm=S, tn=512, n=3 * FOX_W, out_dtypes=[BF16], name="proj_a", deps=first)
    proj_b = _mm(h1, w_in_t, mode="nt", tm=S, tn=DIL_W, n=3 * DIL_W, b_off=SEG_B // DIL_W, out_dtypes=[F32], name="proj_b",
                deps=first)
    proj_g = _mm(h1, w_in_t, mode="nt", tm=S, tn=D, n=2 * D, b_off=SEG_G // D, out_dtypes=[F32], name="proj_g",
                deps=first)
    proj_f = _mm(h1, w_in_t, mode="nt", tm=512, tn=LANES, n=LANES, b_off=SEG_F // LANES, out_dtypes=[F32],
                 name="proj_f", deps=first)

    q_aug, k_aug = _fox_prepare(proj_a, proj_f, b_pad)
    oa, lse_a = _fox_fwd(q_aug, k_aug, proj_a)

    qkv3 = _rope_split(proj_b, tables)
    o3, lse3 = _dil_fwd(qkv3)
    ob, alpha = _dil_combine(o3, lse3)

    w_a, w_b, w_out = hooks.mixer_weights(ob)
    ya, yb, mixed, x2, h2 = _mixer_tail(oa, ob, w_a, w_b, w_out, proj_g, x, g_mlp)
    w_up_sh, w_down = hooks.mlp_weights(h2)

    def up_epilogue(acc):
        r = jnp.maximum(acc, 0.0)
        return acc, r * r

    u, act = _mm(h2, w_up_sh, mode="nn", tm=S, tn=DFF // N_DEV, b_sharded=True, out_dtypes=[F32, BF16],
                 name="mlp_up", epilogue=up_epilogue)

    def loss_epilogue(acc, rows, vecs):
        dx3, dg, loss = _final_loss_rows(rows[0] + acc, vecs[0], rows[1])
        return (dx3,), (dg, loss)

    dx3, dg_final, loss = _mm_rows(act, w_down, tm=ROWS_TM, name="mlp_down_loss", epilogue=loss_epilogue,
                                   rows=(x2, tgt), vecs=(g_final,), row_out=(F32,), vec_out=(D, LANES))

    def rms_bwd_epilogue(acc, rows, vecs):
        dx, dg = _rms_bwd_rows(acc, rows[0], rows[1], vecs[0])
        return (dx,), (dg,)

    du = _mm(dx3, w_down, mode="nt", tm=S, tn=512, out_dtypes=[BF16], name="mlp_down_bwd",
             epilogue=lambda acc, u_t: (acc * (2.0 * jnp.maximum(u_t, 0.0)),), extras=(u,))
    dw_down = _mm(act, dx3, mode="tn", tm=512, tn=D, out_dtypes=[F32], name="dw_down")
    dw_up_sh = _mm(h2, du, mode="tn", tm=D, tn=DFF // N_DEV, out_sharded=True, out_dtypes=[F32], name="dw_up")
    dx2, dg_mlp = _mm_rows(du, w_up_sh, b_sharded=True, tm=ROWS_TM, name="mlp_up_bwd", epilogue=rms_bwd_epilogue,
                           rows=(x2, dx3), vecs=(g_mlp,), row_out=(F32,), vec_out=(D,),
                           deps=hooks.mlp_grads(dw_up_sh, dw_down))

    dw_out = _mm(mixed, dx2, mode="tn", tm=D, tn=D, out_dtypes=[F32], name="dw_out")

    dyab, dproj = _out_proj_gate_bwd(dx2, w_out, proj_g, ya, yb)
    doa = _mm(dyab, w_a, mode="nt", tm=512, tn=FOX_W, k=D, a_off=0, b_sharded=True, out_dtypes=[BF16],
              name="branch_a_bwd")
    dw_a = _mm(oa, dyab, mode="tn", tm=FOX_W, tn=D, n=D, b_off=0, out_sharded=True, out_dtypes=[F32],
               name="dw_branch_a")
    dob = _mm(dyab, w_b, mode="nt", tm=512, tn=CB, k=D, a_off=1, b_sharded=True, out_dtypes=[F32],
              name="branch_b_bwd")
    dw_b = _mm(ob, dyab, mode="tn", tm=DIL_OUT, tn=D, n=D, b_off=1, out_sharded=True, out_dtypes=[F32],
               name="dw_branch_b")

    dproj, dft, dfs = _fox_bwd(q_aug, k_aug, proj_a, lse_a, doa, dproj, deps=hooks.mixer_grads(dw_a, dw_b, dw_out))
    dproj, db = _fox_dscan(dft, dfs, proj_f, b_pad, dproj)

    do3, c3 = _dil_combine_bwd(dob, ob, alpha, deps=hooks.mid_backward(db))
    dq3, dk3, dv3 = _dil_bwd(qkv3, lse3, do3, c3)
    dproj = _rope_merge_bwd(dq3, dk3, dv3, tables, dproj)

    dw_in_t = _mm(dproj, h1, mode="tn", tm=DW_IN_TM, tn=D, out_dtypes=[F32], name="dw_in")
    dx, dg_attn = _mm_rows(dproj, w_in_t, tm=ROWS_TM // 2, name="proj_bwd", epilogue=rms_bwd_epilogue,
                           rows=(x, dx2), vecs=(g_attn,), row_out=(F32,), vec_out=(D,),
                           deps=hooks.w_in_grad(dw_in_t))

    return loss, dx, (dg_attn, db, dg_mlp, dg_final)


SHARD_SHAPES = ((D_IN // N_DEV, D), (FOX_W, D // N_DEV), (DIL_OUT, D // N_DEV), (D // N_DEV, D),
                (D, DFF // N_DEV), (DFF // N_DEV, D))
W_NAMES = ("w_in", "w_a", "w_b", "w_out", "w_up", "w_down")
AG_COPIES = 7
HBM = pl.BlockSpec(memory_space=pltpu.HBM)
SEM = pl.BlockSpec(memory_space=pltpu.SEMAPHORE)


def _place():
    return lax.axis_index("x"), lax.axis_index("y"), lax.axis_index("c")


W_SLAB = 752
W_WIN = 272
PAD_BLK = 256


def _pad_runs():
    runs = sorted((pad, pad + b - a, p, a) for p in range(N_DEV) for a, b, pad in _shard_pieces(p))
    blocks = []
    for k in range(D_PAD // PAD_BLK):
        lo, hi = k * PAD_BLK, (k + 1) * PAD_BLK
        blocks.append([(max(lo, r0) - lo, min(hi, r1) - lo, p, a + max(lo, r0) - r0)
                       for r0, r1, p, a in runs if max(lo, r0) < min(hi, r1)])
    return blocks


def _gather_w_in(shard, x_in, g_attn, late_shards):
    blocks = _pad_runs()

    norm_t = 512
    n_chunks = S // norm_t

    n_late = len(late_shards)
    late_shapes = [t.shape for t in late_shards]

    def body(x_ref, xin_ref, g_ref, *refs):
        late_in, (out_ref, h1_ref), late_out = refs[:n_late], refs[n_late:n_late + 2], refs[n_late + 2:2 * n_late + 2]
        scratch = refs[2 * n_late + 2:]
        stage, land, xbuf = scratch[:3]
        late_stage, late_cast = scratch[3:3 + n_late], scratch[3 + n_late:3 + 2 * n_late]
        send_sems, recv_sems, load_sem, x_sems, late_sems, store_sems = scratch[3 + 2 * n_late:]
        x, y, c = _place()
        me, sibling = (x, y, c), (x, y, 1 - c)
        chips = [(1 - x, y), (x, 1 - y), (1 - x, 1 - y)]

        def slot(px, py, pc):
            return land.at[4 * px + 2 * py + pc]

        def copy(k, block, to):
            return pltpu.make_async_remote_copy(
                src_ref=slot(*block), dst_ref=slot(*block), send_sem=send_sems.at[k], recv_sem=recv_sems.at[k],
                device_id=to, device_id_type=MESH)

        load = pltpu.make_async_copy(x_ref, stage, load_sem)
        load.start()
        load.wait()
        land[4 * x + 2 * y + c] = stage[...].astype(BF16)
        sent = [copy(0, me, sibling)] + [copy(1 + j, me, (*chip, c)) for j, chip in enumerate(chips)]
        for cp in sent:
            cp.start()

        def x_load(i):
            return pltpu.make_async_copy(xin_ref.at[pl.ds(i * norm_t, norm_t)], xbuf.at[i % 2], x_sems.at[i % 2])

        late_loads = [pltpu.make_async_copy(late_in[i], late_stage[i], late_sems.at[i]) for i in range(n_late)]
        for ld in late_loads:
            ld.start()
        x_load(0).start()
        for i in range(n_chunks):
            if i + 1 < n_chunks:
                x_load(i + 1).start()
            x_load(i).wait()
            h1_ref[i * norm_t:(i + 1) * norm_t, :] = _rms_rows(xbuf[i % 2], g_ref[...]).astype(BF16)
        stores = []
        for i in range(n_late):
            late_loads[i].wait()
            late_cast[i][...] = late_stage[i][...].astype(BF16)
            stores.append(pltpu.make_async_copy(late_cast[i], late_out[i].at[4 * x + 2 * y + c], store_sems.at[i]))
            stores[-1].start()

        for j, chip in enumerate(chips):
            copy(1 + j, (*chip, c), me).wait_recv()
            sent.append(copy(4 + j, (*chip, c), sibling))
            sent[-1].start()
        copy(0, sibling, me).wait_recv()
        for j, chip in enumerate(chips):
            copy(4 + j, (*chip, 1 - c), me).wait_recv()
        for cp in sent:
            cp.wait_send()
        for st in stores:
            st.wait()

        row = lax.broadcasted_iota(jnp.int32, (PAD_BLK, D), 0)
        for k, runs in enumerate(blocks):
            out = jnp.zeros((PAD_BLK, D), F32)
            for o_lo, o_hi, p, a in runs:
                start = min(a // 16 * 16, W_SLAB - W_WIN)
                win = land[p, start:start + W_WIN, :].astype(F32)
                moved = pltpu.roll(win, (o_lo - (a - start)) % W_WIN, 0)[:PAD_BLK]
                out = jnp.where((row >= o_lo) & (row < o_hi), moved, out)
            out_ref[k * PAD_BLK:(k + 1) * PAD_BLK, :] = out.astype(BF16)

    return pl.pallas_call(
        body, name="all_gather_w_in",
        out_shape=[jax.ShapeDtypeStruct((D_PAD, D), BF16), jax.ShapeDtypeStruct((S, D), BF16)]
                  + [jax.ShapeDtypeStruct((N_DEV,) + sh, BF16) for sh in late_shapes],
        in_specs=[ANY, ANY, pl.BlockSpec(memory_space=pltpu.VMEM)] + [ANY] * n_late,
        out_specs=[pl.BlockSpec(memory_space=pltpu.VMEM)] * 2 + [ANY] * n_late,
        scratch_shapes=[pltpu.VMEM((W_SLAB, D), F32), pltpu.VMEM((N_DEV, W_SLAB, D), BF16),
                        pltpu.VMEM((2, norm_t, D), F32)]
                       + [pltpu.VMEM(sh, F32) for sh in late_shapes] + [pltpu.VMEM(sh, BF16) for sh in late_shapes]
                       + [pltpu.SemaphoreType.DMA((AG_COPIES,)), pltpu.SemaphoreType.DMA((AG_COPIES,)),
                          pltpu.SemaphoreType.DMA(()), pltpu.SemaphoreType.DMA((2,)),
                          pltpu.SemaphoreType.DMA((n_late,)), pltpu.SemaphoreType.DMA((n_late,))],
        compiler_params=_cparams(None),
    )(shard, x_in, g_attn, *late_shards)


PEER_COPIES = N_DEV - 1
SIBLING_COPIES = 4
CHIP_COPIES = 3


def _peer_copies(_, land_refs, send_sems, recv_sems):
    x, y, c = _place()
    mine = 4 * x + 2 * y + c
    copies = []
    for i, ref in enumerate(land_refs):
        for k in range(1, N_DEV):
            peer = (x ^ (k >> 2), y ^ ((k >> 1) & 1), c ^ (k & 1))
            n = i * PEER_COPIES + k - 1
            copies.append(pltpu.make_async_remote_copy(
                src_ref=ref.at[mine], dst_ref=ref.at[mine], send_sem=send_sems.at[n], recv_sem=recv_sems.at[n],
                device_id=peer, device_id_type=MESH))
    return copies


def _sibling_copies(g_refs, r_refs, send_sems, recv_sems):
    x, y, c = _place()
    return [pltpu.make_async_remote_copy(
        src_ref=g_refs[i].at[2 * k + (1 - c)], dst_ref=r_refs[i].at[k],
        send_sem=send_sems.at[SIBLING_COPIES * i + k], recv_sem=recv_sems.at[SIBLING_COPIES * i + k],
        device_id=(x, y, 1 - c), device_id_type=MESH) for i in range(len(g_refs)) for k in range(SIBLING_COPIES)]


def _sibling_half_copies(g_refs, r_refs, send_sems, recv_sems):
    x, y, c = _place()
    return [pltpu.make_async_remote_copy(
        src_ref=g_refs[i].at[k], dst_ref=r_refs[i].at[k],
        send_sem=send_sems.at[SIBLING_COPIES * i + k], recv_sem=recv_sems.at[SIBLING_COPIES * i + k],
        device_id=(x, y, 1 - c), device_id_type=MESH) for i in range(len(g_refs)) for k in range(SIBLING_COPIES)]


def _chip_copies(p_refs, r_refs, send_sems, recv_sems):
    x, y, c = _place()
    chips = [(1 - x, y), (x, 1 - y), (1 - x, 1 - y)]
    return [pltpu.make_async_remote_copy(
        src_ref=p_refs[i].at[2 * cx + cy], dst_ref=r_refs[i].at[j],
        send_sem=send_sems.at[CHIP_COPIES * i + j], recv_sem=recv_sems.at[CHIP_COPIES * i + j],
        device_id=(cx, cy, c), device_id_type=MESH) for i in range(len(p_refs)) for j, (cx, cy) in enumerate(chips)]


def _in_hbm(a):
    return pltpu.with_memory_space_constraint(a, pltpu.HBM)


def _exchange_start(name, copies_fn, n_copies, srcs, lands, after=()):
    n_s, n_l, n_a = len(srcs), len(lands), len(after)

    def body(*refs):
        outs = refs[n_s + n_l + n_a:]
        for cp in copies_fn(refs[:n_s], refs[n_s:n_s + n_l], outs[0], outs[1]):
            cp.start()
        outs[-1][...] = jnp.zeros_like(outs[-1])

    res = pl.pallas_call(
        body, name=name,
        out_shape=[pltpu.SemaphoreType.DMA((n_copies,)), pltpu.SemaphoreType.DMA((n_copies,))]
                  + [pltpu.HBM(t.shape, t.dtype) for t in (*srcs, *lands)] + [jax.ShapeDtypeStruct((8, LANES), F32)],
        in_specs=[HBM] * (n_s + n_l) + [ANY] * n_a,
        out_specs=[SEM, SEM] + [HBM] * (n_s + n_l) + [pl.BlockSpec(memory_space=pltpu.VMEM)],
        input_output_aliases={i: 2 + i for i in range(n_s + n_l)},
        compiler_params=pltpu.CompilerParams(has_side_effects=pltpu.SideEffectType.DATAFLOW_SIDE_EFFECTING),
    )(*[_in_hbm(t) for t in (*srcs, *lands)], *after)
    return res[0], res[1], list(res[2:2 + n_s]), list(res[2 + n_s:2 + n_s + n_l]), res[-1]


def _exchange_wait(name, copies_fn, started, after):
    send_sems, recv_sems, srcs, lands, _ = started
    n_s, n_l = len(srcs), len(lands)

    def body(*refs):
        for cp in copies_fn(refs[:n_s], refs[n_s:n_s + n_l], refs[n_s + n_l], refs[n_s + n_l + 1]):
            cp.wait_send()
            cp.wait_recv()

    res = pl.pallas_call(
        body, name=name,
        out_shape=[pltpu.HBM(t.shape, t.dtype) for t in (*srcs, *lands)],
        in_specs=[HBM] * (n_s + n_l) + [SEM, SEM, ANY],
        out_specs=[HBM] * (n_s + n_l),
        input_output_aliases={i: i for i in range(n_s + n_l)},
        compiler_params=pltpu.CompilerParams(has_side_effects=pltpu.SideEffectType.DATAFLOW_SIDE_EFFECTING),
    )(*srcs, *lands, send_sems, recv_sems, after)
    return list(res[:n_s]), list(res[n_s:])


def _shard_block(shape):
    rows, cols = shape
    if rows * cols <= 256 * 1024:
        return rows, cols
    if rows % 256:
        return rows, 512
    return max(256, 256 * 1024 // cols), cols


def _chip_partial(ids, g_stack, recv1, name):
    shape = g_stack.shape[1:]
    br, bc = shape
    whole = g_stack.shape[0] == N_DEV

    def body(ids_ref, g_ref, r_ref, pb_ref, own_ref):
        s = g_ref[...] + r_ref[...]
        pb_ref[...] = s.astype(BF16)

        @pl.when(pl.program_id(2) == ids_ref[1])
        def _():
            own_ref[...] = s

    grid_spec = pltpu.PrefetchScalarGridSpec(
        num_scalar_prefetch=1, grid=(shape[0] // br, shape[1] // bc, 4),
        in_specs=[pl.BlockSpec((None, br, bc), lambda r, q, k, ids: (2 * k + ids[0] if whole else k, r, q)),
                  pl.BlockSpec((None, br, bc), lambda r, q, k, ids: (k, r, q))],
        out_specs=[pl.BlockSpec((None, br, bc), lambda r, q, k, ids: (k, r, q)),
                   pl.BlockSpec((br, bc), lambda r, q, k, ids: (r, q))])
    return pl.pallas_call(
        body, name=name, grid_spec=grid_spec,
        out_shape=[jax.ShapeDtypeStruct((4,) + shape, BF16), jax.ShapeDtypeStruct(shape, F32)],
        compiler_params=_cparams(("parallel", "parallel", "arbitrary")),
    )(ids, g_stack, recv1)


def _adamw(w, g, m, v):
    m = ADAM_B1 * m + (1.0 - ADAM_B1) * g
    v = ADAM_B2 * v + (1.0 - ADAM_B2) * (g * g)
    m_hat = m / (1.0 - ADAM_B1 ** ADAM_STEP)
    v_hat = v / (1.0 - ADAM_B2 ** ADAM_STEP)
    delta = -ADAM_LR * (m_hat / (jnp.sqrt(v_hat) + ADAM_EPS) + ADAM_WD * w)
    return delta, m, v


def _reduce_adamw(own, recv2, w, m, v, name, deps=()):
    shape = own.shape
    br, bc = _shard_block(shape)

    def body(own_ref, r_ref, w_ref, m_ref, v_ref, *refs):
        g_ref, d_ref, nm_ref, nv_ref = refs[len(deps):]
        g = own_ref[...]
        for j in range(3):
            g = g + r_ref[j].astype(F32)
        delta, nm, nv = _adamw(w_ref[...], g, m_ref[...], v_ref[...])
        g_ref[...] = g
        d_ref[...] = delta
        nm_ref[...] = nm
        nv_ref[...] = nv

    blk = pl.BlockSpec((br, bc), lambda r, q: (r, q))
    return pl.pallas_call(
        body, name=name, grid=(shape[0] // br, shape[1] // bc),
        in_specs=[blk, pl.BlockSpec((3, br, bc), lambda r, q: (0, r, q)), blk, blk, blk] + [ANY] * len(deps),
        out_specs=[blk] * 4, out_shape=[jax.ShapeDtypeStruct(shape, F32)] * 4,
        compiler_params=_cparams(("parallel", "parallel")),
    )(own, recv2, w, m, v, *deps)


def _small_adamw(stack, w, m, v):
    def body(s_ref, w_ref, m_ref, v_ref, loss_ref, *o_refs):
        g = s_ref[0]
        for s in range(1, N_DEV):
            g = g + s_ref[s]
        delta, nm, nv = _adamw(w_ref[...], g, m_ref[...], v_ref[...])
        loss_ref[...] = g[LOSS_ROW:LOSS_ROW + 1, 0:1]
        for q, t in enumerate((g, delta, nm, nv)):
            attn_ref, b_ref, mlp_ref, final_ref = o_refs[4 * q:4 * q + 4]
            for i in range(8):
                cols = slice(i * LANES, (i + 1) * LANES)
                attn_ref[:, cols] = t[i:i + 1]
                mlp_ref[:, cols] = t[8 + i:9 + i]
                final_ref[:, cols] = t[16 + i:17 + i]
            b_ref[...] = t[24:25, :FOX_H]

    vm = pl.BlockSpec(memory_space=pltpu.VMEM)
    unpacked = [jax.ShapeDtypeStruct((1, n), F32) for n in (D, FOX_H, D, D)]
    return pl.pallas_call(
        body, name="small_adamw",
        in_specs=[vm] * 4, out_specs=[vm] * 17,
        out_shape=[jax.ShapeDtypeStruct((1, 1), F32)] + unpacked * 4,
    )(stack, w, m, v)


W_IN_SEGMENTS = ((0, 3 * FOX_W, SEG_A), (3 * FOX_W, 3 * FOX_W + FOX_H, SEG_F),
                 (3 * FOX_W + FOX_H, 3 * FOX_W + FOX_H + 3 * DIL_W, SEG_B), (3 * FOX_W + FOX_H + 3 * DIL_W, D_IN, SEG_G))


def _shard_pieces(p):
    rows = D_IN // N_DEV
    lo, hi = p * rows, (p + 1) * rows
    return [(max(lo, a) - lo, min(hi, b) - lo, pad + max(lo, a) - a)
            for a, b, pad in W_IN_SEGMENTS if max(lo, a) < min(hi, b)]


def _unpad_runs(p):
    rows = D_IN // N_DEV
    blocks = []
    for lo in range(0, rows, PAD_BLK):
        hi = min(lo + PAD_BLK, rows)
        blocks.append([(max(lo, a) - lo, min(hi, b) - lo, pad + max(lo, a) - a)
                       for a, b, pad in _shard_pieces(p) if max(lo, a) < min(hi, b)])
    return blocks


def _unpad_dw_in_t(dwp, ids, kept, deps=()):
    rows = D_IN // N_DEV
    plans = [_unpad_runs(p) for p in range(N_DEV)]
    windows = [[min(src // 8 * 8, D_PAD - W_WIN) for runs in plan for _, _, src in runs] for plan in plans]
    n_win = max(len(w) for w in windows)

    def body(ids_ref, src_ref, *refs):
        o_ref, buf, sems = refs[len(deps):]
        side = ids_ref[0] if kept else 1 - ids_ref[0]
        p = 2 * pl.program_id(0) + side
        row = lax.broadcasted_iota(jnp.int32, (PAD_BLK, D), 0)

        def fetch(q):
            slot = (q // 2) % 2
            return [pltpu.make_async_copy(src_ref.at[pl.ds(start, W_WIN)], buf.at[slot, n], sems.at[slot, n])
                    for n, start in enumerate(windows[q])]

        for q in range(N_DEV):
            @pl.when(p == q)
            def _(q=q):
                for nxt in ([q, q + 2] if q < 2 else [q + 2]):
                    if nxt < N_DEV:
                        for cp in fetch(nxt):
                            cp.start()
                copies, n = fetch(q), 0
                for j, runs in enumerate(plans[q]):
                    out = None
                    for o_lo, o_hi, src in runs:
                        copies[n].wait()
                        moved = pltpu.roll(buf[(q // 2) % 2, n], (o_lo - (src - windows[q][n])) % W_WIN, 0)[:PAD_BLK]
                        out = moved if out is None else jnp.where((row >= o_lo) & (row < o_hi), moved, out)
                        n += 1
                    size = min(PAD_BLK, rows - j * PAD_BLK)
                    o_ref[j * PAD_BLK:j * PAD_BLK + size, :] = out[:size]

    grid_spec = pltpu.PrefetchScalarGridSpec(
        num_scalar_prefetch=1, grid=(N_DEV // 2,),
        in_specs=[ANY] * (1 + len(deps)), out_specs=pl.BlockSpec((None, rows, D), lambda k, ids: (k, 0, 0)),
        scratch_shapes=[pltpu.VMEM((2, n_win, W_WIN, D), F32), pltpu.SemaphoreType.DMA((2, n_win))])
    return pl.pallas_call(
        body, name="unpad_dw_in_kept" if kept else "unpad_dw_in_sent", grid_spec=grid_spec,
        out_shape=jax.ShapeDtypeStruct((N_DEV // 2, rows, D), F32),
        compiler_params=_cparams(("arbitrary",)),
    )(ids, dwp, *deps)


LOSS_ROW = 25


def _pack_small(g_attn, b, g_mlp, g_final, loss_row=None):
    tail = jnp.pad(b, ((0, 7), (0, LANES - b.shape[1])))
    if loss_row is not None:
        tail = tail + jnp.pad(loss_row, ((LOSS_ROW - 24, 31 - LOSS_ROW), (0, 0)))
    return jnp.concatenate([g_attn.reshape(8, LANES), g_mlp.reshape(8, LANES), g_final.reshape(8, LANES), tail], axis=0)


class _StepComm:
    def __init__(self, ids, w_sh, m_sh, v_sh, lands, after_w_in):
        self.ids = ids
        self.w_sh, self.m_sh, self.v_sh = w_sh, m_sh, v_sh
        self.updates = None
        self.gather_mixer = _exchange_start("gather_mixer_start", _peer_copies, PEER_COPIES * 3, [], lands[:3],
                                            after=(after_w_in,))
        self.gather_mlp = _exchange_start("gather_mlp_start", _peer_copies, PEER_COPIES * 2, [], lands[3:],
                                          after=(self.gather_mixer[-1],))
        self.sibling = {}
        self.chips = {}
        self.own = {}
        self.names = {}
        self.copies_fn = {}
        self.kept = {}

    def _reduce_start(self, group, names, grads, after=(), copies_fn=_sibling_copies):
        lands = [lax.empty((SIBLING_COPIES,) + t.shape[1:], F32) for t in grads]
        self.sibling[group] = _exchange_start("grad_%s_sibling_start" % group, copies_fn,
                                              SIBLING_COPIES * len(grads), grads, lands, after=after)
        self.names[group] = names
        self.copies_fn[group] = copies_fn
        return self.sibling[group][-1]

    def _reduce_mid(self, group, after):
        grads, recv1 = _exchange_wait("grad_%s_sibling_wait" % group, self.copies_fn[group], self.sibling[group],
                                      after)
        grads = self.kept.get(group, grads)
        parts = [_chip_partial(self.ids, g, r, "grad_partial_" + n) for g, r, n in zip(grads, recv1, self.names[group])]
        self.own[group] = [p[1] for p in parts]
        srcs = [p[0] for p in parts]
        lands = [lax.empty((CHIP_COPIES,) + t.shape[1:], BF16) for t in srcs]
        self.chips[group] = _exchange_start("grad_%s_chips_start" % group, _chip_copies, CHIP_COPIES * len(srcs),
                                            srcs, lands)
        return self.chips[group][-1]

    def reduced(self, group, after):
        _, recv2 = _exchange_wait("grad_%s_chips_wait" % group, _chip_copies, self.chips[group], after)
        return list(zip(self.own[group], recv2))

    def first_deps(self):
        return [self.gather_mlp[-1]]

    def mixer_weights(self, after):
        _, (g_a, g_b, g_out) = _exchange_wait("gather_mixer_wait", _peer_copies, self.gather_mixer, after)
        return g_a, g_b, g_out.reshape(D, D)

    def mlp_weights(self, after):
        _, (g_up, g_down) = _exchange_wait("gather_mlp_wait", _peer_copies, self.gather_mlp, after)
        return g_up, g_down.reshape(DFF, D)

    def mlp_grads(self, dw_up_sh, dw_down):
        return [self._reduce_start("mlp", W_NAMES[4:], [dw_up_sh, dw_down.reshape((N_DEV,) + SHARD_SHAPES[5])])]

    def mixer_grads(self, dw_a, dw_b, dw_out):
        token = self._reduce_mid("mlp", dw_b)
        grads = [dw_a, dw_b, dw_out.reshape((N_DEV,) + SHARD_SHAPES[3])]
        return [self._reduce_start("mixer", W_NAMES[1:4], grads, after=(token,))]

    def mid_backward(self, after):
        return [self._reduce_mid("mixer", after)]

    def w_in_grad(self, dw_in_t):
        token = self._reduce_start("w_in", W_NAMES[:1], [_unpad_dw_in_t(dw_in_t, self.ids, kept=False)],
                                   copies_fn=_sibling_half_copies)
        self.kept["w_in"] = [_unpad_dw_in_t(dw_in_t, self.ids, kept=True, deps=[token])]
        reduced = self.reduced("mixer", token) + self.reduced("mlp", token)
        self.updates, last = [None] * len(reduced), self.kept["w_in"][0]
        for i in (3, 4, 0, 1, 2):
            if i == 4:
                last = self._reduce_mid("w_in", last)
            self.updates[i] = _reduce_adamw(*reduced[i], self.w_sh[1 + i], self.m_sh[1 + i], self.v_sh[1 + i],
                                            "adamw_" + W_NAMES[1 + i], deps=[last])
            last = self.updates[i][0]
        return [last]

    def w_in_update(self, after):
        (reduced,) = self.reduced("w_in", after)
        return _reduce_adamw(*reduced, self.w_sh[0], self.m_sh[0], self.v_sh[0], "adamw_" + W_NAMES[0])


def kernel(x, norm_attn_g, w_in, b_forget, w_branch_a, w_branch_b, w_out, norm_mlp_g, w_up, w_down, norm_final_g, loss_target, m_norm_attn_g, m_w_in, m_b_forget, m_w_branch_a, m_w_branch_b, m_w_out, m_norm_mlp_g, m_w_up, m_w_down, m_norm_final_g, v_norm_attn_g, v_w_in, v_b_forget, v_w_branch_a, v_w_branch_b, v_w_out, v_norm_mlp_g, v_w_up, v_w_down, v_norm_final_g):
    cx, cy, cc = _place()
    ids = jnp.stack([cc, 2 * cx + cy]).astype(jnp.int32)

    w_sh = [w_in[0].T] + [t[0] for t in (w_branch_a, w_branch_b, w_out, w_up, w_down)]
    m_sh = [m_w_in[0].T] + [t[0] for t in (m_w_branch_a, m_w_branch_b, m_w_out, m_w_up, m_w_down)]
    v_sh = [v_w_in[0].T] + [t[0] for t in (v_w_branch_a, v_w_branch_b, v_w_out, v_w_up, v_w_down)]

    w_in_t, h1, *lands = _gather_w_in(jnp.pad(w_sh[0], ((0, W_SLAB - w_sh[0].shape[0]), (0, 0))), x[0], norm_attn_g,
                                      w_sh[1:])
    comm = _StepComm(ids, w_sh, m_sh, v_sh, lands, w_in_t)
    b_pad = jnp.pad(b_forget, ((0, 0), (0, LANES - FOX_H)))

    loss_row, dx, dsmall = _local_step(
        x[0], loss_target[0], h1, norm_attn_g, norm_mlp_g, norm_final_g.reshape(1, D), b_pad, w_in_t, comm)

    dg_attn, db, dg_mlp, dg_final = dsmall
    stack = lax.dynamic_update_slice(jnp.zeros((N_DEV, SMALL_R, LANES), F32),
                                     _pack_small(dg_attn, db, dg_mlp, dg_final, loss_row)[None],
                                     (4 * cx + 2 * cy + cc, 0, 0))
    small_gather = _exchange_start("small_gather_start", _peer_copies, PEER_COPIES, [], [stack])
    w_in_update = comm.w_in_update(small_gather[-1])
    _, (stack,) = _exchange_wait("small_gather_wait", _peer_copies, small_gather, w_in_update[0])
    loss, *small = _small_adamw(
        stack,
        _pack_small(norm_attn_g, b_forget, norm_mlp_g, norm_final_g.reshape(1, D)),
        _pack_small(m_norm_attn_g, m_b_forget, m_norm_mlp_g, m_norm_final_g.reshape(1, D)),
        _pack_small(v_norm_attn_g, v_b_forget, v_norm_mlp_g, v_norm_final_g.reshape(1, D)))
    big = [w_in_update] + comm.updates

    outs = [loss.reshape(()), dx[None]]
    for q in range(4):
        s_attn, s_b, s_mlp, s_final = small[4 * q:4 * q + 4]
        s_final = s_final.reshape(D)
        b_in, b_a, b_b, b_out, b_up, b_down = [t[q][None] for t in big]
        b_in = jnp.swapaxes(b_in, 1, 2)
        outs += [s_attn, b_in, s_b, b_a, b_b, b_out, s_mlp, b_up, b_down, s_final]
    return tuple(outs)
```

```python
import functools

import jax
import jax.numpy as jnp
from jax import lax
from jax.experimental import pallas as pl
from jax.experimental.pallas import tpu as pltpu

F32 = jnp.float32
BF16 = jnp.bfloat16
MESH = pl.DeviceIdType.MESH

S = 2048
D = 1024
HD = 64
FOX_H = 8
FOX_W = FOX_H * HD
DIL_HG = 4
DIL_G = 3
DIL_W = DIL_G * DIL_HG * HD
DIL_OUT = DIL_HG * HD
DIL_BLK = 128
DIL_R = (1, 4, 16)
DFF = 4 * D
D_IN = 3 * FOX_W + FOX_H + 3 * DIL_W + 2 * D
EPS = 1e-6
NEG_INF = -1e30
SCALE = HD ** -0.5
ROPE_THETA = 500000.0
ROPE_DIM = HD // 4
N_DEV = 8

ADAM_LR = 0.001
ADAM_B1 = 0.9
ADAM_B2 = 0.999
ADAM_EPS = 1e-08
ADAM_WD = 0.01
ADAM_STEP = 10

LANES = 128
CB = 256
SEG_A = 0
SEG_B = 3 * FOX_W
SEG_Z = SEG_B + 3 * DIL_W
SEG_G = SEG_Z + CB
SEG_F = SEG_G + 2 * D
D_PAD = SEG_F + CB
DW_IN_TM = D_PAD // 10
SMALL_R = 32

VMEM_MB = 56


def _cparams(dims=None, vmem_mb=VMEM_MB, **kw):
    return pltpu.CompilerParams(dimension_semantics=dims, vmem_limit_bytes=vmem_mb << 20, **kw)


ANY = pl.BlockSpec(memory_space=pl.ANY)

_NN = (((1,), (0,)), ((), ()))
_NT = (((1,), (1,)), ((), ()))
_TN = (((0,), (0,)), ((), ()))


def _dot(a, b, dims):
    return lax.dot_general(a.astype(BF16), b.astype(BF16), dims, preferred_element_type=F32)


def _mm(a, b, *, mode, tm, tn, out_dtypes, name, n=None, k=None, a_off=0, b_off=0,
        b_sharded=False, out_sharded=False, epilogue=None, extras=(), deps=()):
    n_sh = b.shape[-1] if b_sharded else None
    if mode == "nn":
        m = a.shape[0]
        k = k or a.shape[1]
        a_spec = pl.BlockSpec((tm, k), lambda i, j: (i, a_off))
        if b_sharded:
            assert tn == n_sh
            n = N_DEV * n_sh
            b_spec = pl.BlockSpec((None, k, tn), lambda i, j: (j, 0, 0))
        else:
            n = n or b.shape[1]
            b_spec = pl.BlockSpec((k, tn), lambda i, j: (0, j + b_off))
        dims = _NN
    elif mode == "nt":
        m = a.shape[0]
        k = k or a.shape[1]
        a_spec = pl.BlockSpec((tm, k), lambda i, j: (i, a_off))
        if b_sharded:
            n = b.shape[1]
            b_spec = pl.BlockSpec((N_DEV, tn, n_sh), lambda i, j: (0, j, 0))
        else:
            n = n or b.shape[0]
            b_spec = pl.BlockSpec((tn, k), lambda i, j: (j + b_off, 0))
        dims = _NT
    else:
        k, m = a.shape
        n = n or b.shape[1]
        a_spec = pl.BlockSpec((k, tm), lambda i, j: (0, i))
        b_spec = pl.BlockSpec((k, tn), lambda i, j: (0, j + b_off))
        dims = _TN
    assert m % tm == 0 and n % tn == 0, (name, m, n, tm, tn)
    n_extra = len(extras)
    tile = pl.BlockSpec((tm, tn), lambda i, j: (i, j))
    split_tile = out_sharded and n == tn
    if out_sharded:
        assert mode == "tn" and n // tn in (1, N_DEV)
        if split_tile:
            out_spec = pl.BlockSpec((N_DEV, tm, n // N_DEV), lambda i, j: (0, i, 0))
        else:
            out_spec = pl.BlockSpec((None, tm, tn), lambda i, j: (j, i, 0))
        out_shape = (N_DEV, m, n // N_DEV)
    else:
        out_spec, out_shape = tile, (m, n)

    def body(a_ref, b_ref, *refs):
        if mode == "nt" and b_sharded:
            acc = _dot(a_ref[...], jnp.concatenate([b_ref[p] for p in range(N_DEV)], axis=1), dims)
        else:
            acc = _dot(a_ref[...], b_ref[...], dims)
        ex = [r[...] for r in refs[:n_extra]]
        outs = epilogue(acc, *ex) if epilogue is not None else (acc,)
        for o_ref, o in zip(refs[n_extra + len(deps):], outs):
            if split_tile:
                w = n // N_DEV
                for p in range(N_DEV):
                    o_ref[p] = o[:, p * w:(p + 1) * w].astype(o_ref.dtype)
            else:
                o_ref[...] = o.astype(o_ref.dtype)

    res = pl.pallas_call(
        body, name=name, grid=(m // tm, n // tn),
        in_specs=[a_spec, b_spec] + [tile] * n_extra + [ANY] * len(deps),
        out_specs=[out_spec] * len(out_dtypes),
        out_shape=[jax.ShapeDtypeStruct(out_shape, dt) for dt in out_dtypes],
        compiler_params=_cparams(("parallel", "parallel")),
    )(a, b, *extras, *deps)
    return res if len(out_dtypes) > 1 else res[0]


def _mm_rows(a, b, *, tm, name, epilogue, rows=(), vecs=(), row_out=(), vec_out=(), b_sharded=False, deps=()):
    m, k = a.shape
    n_rows, n_vecs, n_deps = len(rows), len(vecs), len(deps)
    n_sh = b.shape[-1] if b_sharded else None

    def body(a_ref, b_ref, *refs):
        row_refs, vec_refs = refs[:n_rows], refs[n_rows:n_rows + n_vecs]
        outs = refs[n_rows + n_vecs + n_deps:]
        if b_sharded:
            acc = _dot(a_ref[:, 0:n_sh], b_ref[0], _NT)
            for p in range(1, N_DEV):
                acc = acc + _dot(a_ref[:, p * n_sh:(p + 1) * n_sh], b_ref[p], _NT)
        else:
            acc = _dot(a_ref[...], b_ref[...], _NN)
        row_vals, vec_incs = epilogue(acc, [r[...] for r in row_refs], [v[...] for v in vec_refs])
        for o_ref, val in zip(outs[:len(row_out)], row_vals):
            o_ref[...] = val.astype(o_ref.dtype)

        @pl.when(pl.program_id(0) == 0)
        def _():
            for o_ref in outs[len(row_out):]:
                o_ref[...] = jnp.zeros_like(o_ref)

        for o_ref, inc in zip(outs[len(row_out):], vec_incs):
            o_ref[...] += inc

    tile = pl.BlockSpec((tm, D), lambda i: (i, 0))
    b_spec = pl.BlockSpec(b.shape, lambda i: (0,) * b.ndim)
    return pl.pallas_call(
        body, name=name, grid=(m // tm,),
        in_specs=[pl.BlockSpec((tm, k), lambda i: (i, 0)), b_spec] + [tile] * n_rows
                 + [pl.BlockSpec(v.shape, lambda i: (0, 0)) for v in vecs] + [ANY] * n_deps,
        out_specs=[tile] * len(row_out) + [pl.BlockSpec((1, w), lambda i: (0, 0)) for w in vec_out],
        out_shape=[jax.ShapeDtypeStruct((m, D), dt) for dt in row_out]
                  + [jax.ShapeDtypeStruct((1, w), F32) for w in vec_out],
        compiler_params=_cparams(("arbitrary",)),
    )(a, b, *rows, *vecs, *deps)


ROW_T = 256
ROWS_TM = 512


def _rms_rows(x, g):
    r = lax.rsqrt(jnp.mean(x * x, axis=-1, keepdims=True) + EPS)
    return (x * r) * g


def _rms_bwd_rows(dh, x, dres, g):
    r = lax.rsqrt(jnp.mean(x * x, axis=-1, keepdims=True) + EPS)
    xn = x * r
    dhn = dh * g
    dx = dres + r * (dhn - xn * jnp.mean(dhn * xn, axis=-1, keepdims=True))
    return dx, jnp.sum(dh * xn, axis=0, keepdims=True)


def _final_loss_rows(x, g, tgt):
    r = lax.rsqrt(jnp.mean(x * x, axis=-1, keepdims=True) + EPS)
    xn = x * r
    err = xn * g - tgt
    row_loss = jnp.mean(err * err, axis=-1, keepdims=True)
    loss = 0.5 * jnp.sum(row_loss, axis=0, keepdims=True) * jnp.ones((1, LANES), F32)
    dy = err * (1.0 / D)
    dyn = dy * g
    dx = r * (dyn - xn * jnp.mean(dyn * xn, axis=-1, keepdims=True))
    return dx, jnp.sum(dy * xn, axis=0, keepdims=True), loss


def _sigmoid(z):
    return 1.0 / (1.0 + jnp.exp(-z))


GATE_TR = 512


def _mixer_tail(oa, ob, w_a, w_b, w_out, proj_g, x, g_mlp):
    def body(oa_ref, ob_ref, wa_ref, wb_ref, wo_ref, g_ref, x_ref, gm_ref, ya_ref, yb_ref, mixed_ref, x2_ref, h2_ref):
        ya = _dot(oa_ref[...], jnp.concatenate([wa_ref[p] for p in range(N_DEV)], axis=1), _NN)
        yb = _dot(ob_ref[...], jnp.concatenate([wb_ref[p] for p in range(N_DEV)], axis=1), _NN)
        ya_ref[...] = ya
        yb_ref[...] = yb
        mixed = (_sigmoid(g_ref[:, :D]) * ya + _sigmoid(g_ref[:, D:]) * yb).astype(BF16)
        mixed_ref[...] = mixed
        x2 = x_ref[...] + _dot(mixed, wo_ref[...], _NN)
        x2_ref[...] = x2
        h2_ref[...] = _rms_rows(x2, gm_ref[...]).astype(BF16)

    def rows(width):
        return pl.BlockSpec((GATE_TR, width), lambda i: (i, 0))

    def whole(t):
        return pl.BlockSpec(t.shape, lambda i: (0,) * t.ndim)

    return pl.pallas_call(
        body, name="mixer_tail", grid=(S // GATE_TR,),
        in_specs=[rows(FOX_W), rows(DIL_OUT), whole(w_a), whole(w_b), whole(w_out), rows(2 * D), rows(D), whole(g_mlp)],
        out_specs=[rows(D)] * 5,
        out_shape=[jax.ShapeDtypeStruct((S, D), dt) for dt in (F32, F32, BF16, F32, BF16)],
        compiler_params=_cparams(("parallel",)),
    )(oa, ob, w_a, w_b, w_out, proj_g, x, g_mlp)


def _out_proj_gate_bwd(dx2, w_out, proj_g, ya, yb):
    def body(dx_ref, w_ref, g_ref, ya_ref, yb_ref, dy_ref, dg_ref):
        dm = _dot(dx_ref[...], w_ref[...], _NT)
        for half, y_ref in enumerate((ya_ref, yb_ref)):
            cols = slice(half * D, (half + 1) * D)
            s = _sigmoid(g_ref[:, cols])
            dy_ref[:, cols] = (dm * s).astype(BF16)
            dg_ref[:, cols] = (dm * y_ref[...] * (s * (1.0 - s))).astype(BF16)

    row = pl.BlockSpec((GATE_TR, D), lambda i: (i, 0))
    wide = pl.BlockSpec((GATE_TR, 2 * D), lambda i: (i, 0))
    return pl.pallas_call(
        body, name="out_proj_gate_bwd", grid=(S // GATE_TR,),
        in_specs=[row, pl.BlockSpec((D, D), lambda i: (0, 0)), wide, row, row],
        out_specs=[wide, pl.BlockSpec((GATE_TR, 2 * D), lambda i: (i, SEG_G // (2 * D)))],
        out_shape=[jax.ShapeDtypeStruct((S, 2 * D), BF16), jax.ShapeDtypeStruct((S, D_PAD), BF16)],
        compiler_params=_cparams(("parallel",)),
    )(dx2, w_out, proj_g, ya, yb)


FOX_TQ = 512
FOX_TQ_FWD = 1024


def _scan_rows(x, reverse):
    n = x.shape[0]
    row = lax.broadcasted_iota(jnp.int32, x.shape, 0)
    k = 1
    while k < n:
        if reverse:
            x = x + jnp.where(row < n - k, pltpu.roll(x, n - k, 0), 0.0)
        else:
            x = x + jnp.where(row >= k, pltpu.roll(x, k, 0), 0.0)
        k *= 2
    return x


def _fox_dscan(dft, dfs, proj_f, b_pad, dproj):
    def body(dft_ref, dfs_ref, f_ref, b_ref, _, dfa_ref, db_ref):
        dfs_pad = jnp.concatenate([dfs_ref[...], jnp.zeros((LANES - FOX_H, S), F32)], axis=0)
        df = dfs_pad.T + dft_ref[0]
        for hp in range(1, dft_ref.shape[0]):
            df = df + pltpu.roll(dft_ref[hp], 2 * hp, 1)
        dlf = _scan_rows(df, reverse=True)
        z = f_ref[...] + b_ref[...]
        lane = lax.broadcasted_iota(jnp.int32, (S, LANES), 1)
        dfa = jnp.where(lane < FOX_H, dlf / (1.0 + jnp.exp(z)), 0.0)
        dfa_ref[:, :LANES] = dfa.astype(BF16)
        dfa_ref[:, LANES:] = jnp.zeros((S, CB - LANES), BF16)
        db_ref[...] = jnp.sum(dfa, axis=0, keepdims=True)

    return pl.pallas_call(
        body, name="fox_dscan", grid=(1,),
        in_specs=[pl.BlockSpec(dft.shape, lambda i: (0, 0, 0)), pl.BlockSpec((FOX_H, S), lambda i: (0, 0)),
                  pl.BlockSpec((S, LANES), lambda i: (0, 0)), pl.BlockSpec((1, LANES), lambda i: (0, 0)), ANY],
        out_specs=[pl.BlockSpec((S, CB), lambda i: (0, SEG_F // CB)), pl.BlockSpec((1, LANES), lambda i: (0, 0))],
        out_shape=[jax.ShapeDtypeStruct((S, D_PAD), BF16), jax.ShapeDtypeStruct((1, LANES), F32)],
        input_output_aliases={4: 0},
        compiler_params=_cparams(("arbitrary",)),
    )(dft, dfs, proj_f, b_pad, dproj)


FOX_NQ = S // FOX_TQ
FOX_HP = FOX_W // LANES
HEADS_PER_LB = LANES // HD
FOX_AUG = 2 * LANES


def _fox_prepare(proj_a, proj_f, b_pad):
    tr = 512

    def body(q_ref, k_ref, f_ref, b_ref, qa_ref, ka_ref, hi_s, mid_s, lo_s):
        i = pl.program_id(0)

        @pl.when(i == 0)
        def _():
            z = f_ref[...] + b_ref[...]
            lf = jnp.minimum(z, 0.0) - jnp.log1p(jnp.exp(-jnp.abs(z)))
            f = _scan_rows(lf, reverse=False)
            hi = f.astype(BF16).astype(F32)
            r1 = f - hi
            mid = r1.astype(BF16).astype(F32)
            hi_s[...] = hi
            mid_s[...] = mid
            lo_s[...] = (r1 - mid).astype(BF16).astype(F32)

        rows = pl.ds(pl.multiple_of(i * tr, tr), tr)
        hi, mid, lo = hi_s[rows, :], mid_s[rows, :], lo_s[rows, :]
        lane = lax.broadcasted_iota(jnp.int32, (tr, HD), 1)
        q_ones = jnp.where((lane >= 3) & (lane < 6), 1.0, 0.0)
        k_ones = jnp.where(lane < 3, 1.0, 0.0)
        for h in range(FOX_H):
            a1, a2, a3 = hi[:, h:h + 1], mid[:, h:h + 1], lo[:, h:h + 1]
            q_extra = jnp.where(lane == 0, a1, jnp.where(lane == 1, a2, jnp.where(lane == 2, a3, q_ones)))
            k_extra = jnp.where(lane == 3, -a1, jnp.where(lane == 4, -a2, jnp.where(lane == 5, -a3, k_ones)))
            base = h * LANES
            qa_ref[:, base:base + HD] = q_ref[:, h * HD:(h + 1) * HD] * jnp.asarray(SCALE, BF16)
            qa_ref[:, base + HD:base + LANES] = q_extra.astype(BF16)
            ka_ref[:, base:base + HD] = k_ref[:, h * HD:(h + 1) * HD]
            ka_ref[:, base + HD:base + LANES] = k_extra.astype(BF16)

    out_blk = pl.BlockSpec((tr, FOX_H * LANES), lambda i: (i, 0))
    return pl.pallas_call(
        body, name="fox_prepare", grid=(S // tr,),
        in_specs=[pl.BlockSpec((tr, FOX_W), lambda i: (i, 0)), pl.BlockSpec((tr, FOX_W), lambda i: (i, 1)),
                  pl.BlockSpec((S, LANES), lambda i: (0, 0)), pl.BlockSpec((1, LANES), lambda i: (0, 0))],
        out_specs=[out_blk, out_blk],
        out_shape=[jax.ShapeDtypeStruct((S, FOX_H * LANES), BF16)] * 2,
        scratch_shapes=[pltpu.VMEM((S, LANES), F32)] * 3,
        compiler_params=_cparams(("arbitrary",)),
    )(proj_a, proj_a, proj_f, b_pad)


def _fox_scores(qa, ka, w, tq):
    s = _dot(qa, ka, _NT)
    row = lax.broadcasted_iota(jnp.int32, (tq, tq), 0)
    col = lax.broadcasted_iota(jnp.int32, (tq, tq), 1)
    diag = jnp.where(col <= row, s[:, w * tq:], NEG_INF)
    return diag if w == 0 else jnp.concatenate([s[:, :w * tq], diag], axis=1)


def _fox_fwd(q_aug, k_aug, proj_a):
    tq = FOX_TQ_FWD

    def body(qa_ref, ka_ref, v_ref, o_ref, lse_ref):
        qi = pl.program_id(1)
        lane = lax.broadcasted_iota(jnp.int32, (tq, LANES), 1)
        for w in range(S // tq):
            @pl.when(qi == w)
            def _(w=w):
                width = (w + 1) * tq
                lse = jnp.zeros((tq, LANES), F32)
                for hh in range(HEADS_PER_LB):
                    aug = slice(hh * LANES, (hh + 1) * LANES)
                    sl = slice(hh * HD, (hh + 1) * HD)
                    s = _fox_scores(qa_ref[:, aug], ka_ref[:width, aug], w, tq)
                    m = jnp.max(s, axis=-1, keepdims=True)
                    p = jnp.exp(s - m)
                    l = jnp.sum(p, axis=-1, keepdims=True)
                    o_ref[:, sl] = _dot(p / l, v_ref[:width, sl], _NN)
                    lse = jnp.where(lane == hh, m + jnp.log(l), lse)
                lse_ref[...] = lse

    return pl.pallas_call(
        body, name="fox_fwd", grid=(FOX_HP, S // tq),
        in_specs=[pl.BlockSpec((tq, FOX_AUG), lambda h, i: (i, h)),
                  pl.BlockSpec((S, FOX_AUG), lambda h, i: (0, h)),
                  pl.BlockSpec((S, LANES), lambda h, i: (0, 2 * FOX_HP + h))],
        out_specs=[pl.BlockSpec((tq, LANES), lambda h, i: (i, h)),
                   pl.BlockSpec((None, tq, LANES), lambda h, i: (h, i, 0))],
        out_shape=[jax.ShapeDtypeStruct((S, FOX_W), F32), jax.ShapeDtypeStruct((FOX_HP, S, LANES), F32)],
        compiler_params=_cparams(("parallel", "parallel")),
    )(q_aug, k_aug, proj_a)


def _fox_bwd(q_aug, k_aug, proj_a, lse, do, dproj, deps=()):
    n_q = FOX_NQ

    def body(qa_ref, ka_ref, v_ref, lse_ref, do_ref, *refs):
        dproj_ref, dft_ref, dfs_ref, dk_acc, dv_acc, dq_buf, kv_buf, dq_sems, kv_sems = refs[1 + len(deps):]
        hp = pl.program_id(0)
        t = pl.program_id(1)
        slot = t % 2
        col = pl.multiple_of(hp * LANES, LANES)

        def dq_copy(step, buf_slot):
            rows = pl.ds(pl.multiple_of(step * FOX_TQ, FOX_TQ), FOX_TQ)
            return pltpu.make_async_copy(dq_buf.at[buf_slot], dproj_ref.at[rows, pl.ds(col, LANES)], dq_sems.at[buf_slot])

        @pl.when(t == 0)
        def _():
            dk_acc[...] = jnp.zeros_like(dk_acc)
            dv_acc[...] = jnp.zeros_like(dv_acc)

        @pl.when((t == 0) & (hp == 0))
        def _():
            dfs_ref[...] = jnp.zeros_like(dfs_ref)

        @pl.when(t >= 2)
        def _():
            dq_copy(t - 2, slot).wait()

        lane = lax.broadcasted_iota(jnp.int32, (FOX_TQ, LANES), 1)
        for w in range(n_q):
            @pl.when(t == w)
            def _(w=w):
                width = (w + 1) * FOX_TQ
                lse_all = lse_ref[...]
                sub = lax.broadcasted_iota(jnp.int32, (FOX_H, width), 0)
                dft = jnp.zeros((FOX_TQ, LANES), F32)
                for hh in range(HEADS_PER_LB):
                    aug = slice(hh * LANES, (hh + 1) * LANES)
                    sl = slice(hh * HD, (hh + 1) * HD)
                    head = hp * HEADS_PER_LB + hh
                    qa = qa_ref[:, aug]
                    ka = ka_ref[:width, aug]
                    do_h = do_ref[:, sl]
                    s = _fox_scores(qa, ka, w, FOX_TQ)
                    p = jnp.exp(s - lse_all[:, hh:hh + 1])
                    dp = _dot(do_h, v_ref[:width, sl], _NT)
                    ds = p * (dp - jnp.sum(dp * p, axis=-1, keepdims=True))
                    dft = jnp.where(lane == hh, jnp.sum(ds, axis=-1, keepdims=True), dft)
                    dfs_ref[:, :width] -= jnp.where(sub == head, jnp.sum(ds, axis=0, keepdims=True), 0.0)
                    dq_buf[slot, :, sl] = (_dot(ds, ka[:, :HD], _NN) * SCALE).astype(BF16)
                    dk_acc[:width, sl] += _dot(ds, qa[:, :HD], _TN)
                    dv_acc[:width, sl] += _dot(p, do_h, _TN)
                dft_ref[...] = dft

        dq_copy(t, slot).start()

        @pl.when(t == n_q - 1)
        def _():
            dq_copy(t - 1, 1 - slot).wait()
            dq_copy(t, slot).wait()
            kv_buf[0] = dk_acc[...].astype(BF16)
            kv_buf[1] = dv_acc[...].astype(BF16)
            copies = [pltpu.make_async_copy(
                kv_buf.at[j], dproj_ref.at[:, pl.ds(pl.multiple_of((j + 1) * FOX_W + hp * LANES, LANES), LANES)],
                kv_sems.at[j]) for j in range(2)]
            for cp in copies:
                cp.start()
            for cp in copies:
                cp.wait()

    qblk = pl.BlockSpec((FOX_TQ, FOX_AUG), lambda h, t: (t, h))
    lane_blk = pl.BlockSpec((None, FOX_TQ, LANES), lambda h, t: (h, t, 0))
    return pl.pallas_call(
        body, name="fox_bwd", grid=(FOX_HP, n_q),
        in_specs=[qblk, pl.BlockSpec((S, FOX_AUG), lambda h, t: (0, h)),
                  pl.BlockSpec((S, LANES), lambda h, t: (0, 2 * FOX_HP + h)),
                  lane_blk, pl.BlockSpec((FOX_TQ, LANES), lambda h, t: (t, h)), ANY] + [ANY] * len(deps),
        out_specs=[ANY, lane_blk, pl.BlockSpec((FOX_H, S), lambda h, t: (0, 0))],
        out_shape=[jax.ShapeDtypeStruct((S, D_PAD), BF16),
                   jax.ShapeDtypeStruct((FOX_HP, S, LANES), F32), jax.ShapeDtypeStruct((FOX_H, S), F32)],
        scratch_shapes=[pltpu.VMEM((S, LANES), F32), pltpu.VMEM((S, LANES), F32),
                        pltpu.VMEM((2, FOX_TQ, LANES), BF16), pltpu.VMEM((2, S, LANES), BF16),
                        pltpu.SemaphoreType.DMA((2,)), pltpu.SemaphoreType.DMA((2,))],
        input_output_aliases={5: 0},
        compiler_params=_cparams(("arbitrary", "arbitrary")),
    )(q_aug, k_aug, proj_a, lse, do, dproj, *deps)


def _rope_tables():
    half = ROPE_DIM // 2
    shape = (DIL_G, S, half)
    g = lax.broadcasted_iota(jnp.int32, shape, 0)
    row = lax.broadcasted_iota(jnp.int32, shape, 1)
    r = jnp.left_shift(1, 2 * g)
    per_class = S // r
    pos = (row % per_class) * r + row // per_class
    inv_freq = jnp.power(jnp.float32(ROPE_THETA), -lax.broadcasted_iota(F32, shape, 2) * 2.0 / ROPE_DIM)
    ang = pos.astype(F32) * inv_freq
    return jnp.concatenate([jnp.cos(ang), jnp.sin(ang)], axis=-1)


def _rope_factors(cs):
    half = ROPE_DIM // 2
    i = lax.broadcasted_iota(jnp.int32, (2 * half, 2 * LANES), 0)
    l = lax.broadcasted_iota(jnp.int32, (2 * half, 2 * LANES), 1)
    hit = ((l < LANES) & (i == l % half)) | ((l >= LANES) & (i - half == l % half))
    spread = jnp.where(hit, 1.0, 0.0).astype(BF16)
    hi = cs.astype(BF16)
    r1 = cs - hi.astype(F32)
    mid = r1.astype(BF16)
    lo = (r1 - mid.astype(F32)).astype(BF16)
    full = (_dot(hi, spread, _NN) + _dot(mid, spread, _NN)) + _dot(lo, spread, _NN)
    cos, sin = full[:, :LANES], full[:, LANES:]
    lane = lax.broadcasted_iota(jnp.int32, cos.shape, 1) % HD
    t_self = jnp.where(lane < ROPE_DIM, cos, 1.0)
    t_up = jnp.where(lane < half, -sin, 0.0)
    t_dn = jnp.where((lane >= half) & (lane < ROPE_DIM), sin, 0.0)
    return t_self, t_up, t_dn


def _residue_pieces(r):
    if r == 1:
        return [(slice(i, i + 512), slice(i, i + 512)) for i in range(0, S, 512)]
    n = S // r
    return [(pl.ds(j, n, stride=r), slice(j * n, (j + 1) * n)) for j in range(r)]


def _rope_apply(x, a, u, d, backward):
    half = ROPE_DIM // 2
    if backward:
        return x * a + pltpu.roll(x * u, half, 1) + pltpu.roll(x * d, LANES - half, 1)
    return x * a + pltpu.roll(x, LANES - half, 1) * u + pltpu.roll(x, half, 1) * d


def _rope_cache_factors(cs_ref, fac, first_step_of_group):
    @pl.when(first_step_of_group)
    def _():
        for lo in range(0, S, 512):
            for j, f in enumerate(_rope_factors(cs_ref[lo:lo + 512, :])):
                fac[j, lo:lo + 512, :] = f


LB_PER_CB = CB // LANES
ROPE_NB = 3 * DIL_G * LB_PER_CB


def _rope_step(s):
    per_group = 3 * LB_PER_CB
    return s // per_group, (s % per_group) // LB_PER_CB, s % LB_PER_CB


def _rope_split(proj_b, tables):
    def body(x_ref, cs_ref, o_ref, fac):
        g, t, hf = _rope_step(pl.program_id(0))
        _rope_cache_factors(cs_ref, fac, (t == 0) & (hf == 0))
        for gs in range(DIL_G):
            @pl.when(g == gs)
            def _(gs=gs):
                for tok, sub in _residue_pieces(DIL_R[gs]):
                    x = x_ref[tok, :]
                    y = _rope_apply(x, fac[0, sub, :], fac[1, sub, :], fac[2, sub, :], False)
                    o_ref[sub, :] = jnp.where(t < 2, y, x).astype(BF16)

    def in_index(s):
        g, t, hf = _rope_step(s)
        return 0, (t * DIL_G + g) * LB_PER_CB + hf

    def out_index(s):
        g, t, hf = _rope_step(s)
        return t * DIL_G + g, 0, hf

    tab = pl.BlockSpec((None, S, 2 * (ROPE_DIM // 2)), lambda s: (_rope_step(s)[0], 0, 0))
    return pl.pallas_call(
        body, name="rope_split", grid=(ROPE_NB,),
        in_specs=[pl.BlockSpec((S, LANES), in_index), tab],
        out_specs=pl.BlockSpec((None, S, LANES), out_index),
        out_shape=jax.ShapeDtypeStruct((3 * DIL_G, S, CB), BF16),
        scratch_shapes=[pltpu.VMEM((3, S, LANES), F32)],
        compiler_params=_cparams(("arbitrary",)),
    )(proj_b, tables)


def _rope_merge_bwd(dq3, dk3, dv3, tables, dproj):
    def body(dq_ref, dk_ref, dv_ref, cs_ref, _, o_ref, tmp, fac):
        s = pl.program_id(0)
        g, t, hf = _rope_step(s)
        _rope_cache_factors(cs_ref, fac, (t == 0) & (hf == 0) & (s < ROPE_NB))
        for gs in range(DIL_G):
            @pl.when((g == gs) & (s < ROPE_NB))
            def _(gs=gs):
                for tok, sub in _residue_pieces(DIL_R[gs]):
                    x = jnp.where(t == 0, dq_ref[sub, :], jnp.where(t == 1, dk_ref[sub, :], dv_ref[sub, :]))
                    y = _rope_apply(x, fac[0, sub, :], fac[1, sub, :], fac[2, sub, :], True)
                    tmp[tok, :] = jnp.where(t < 2, y, x)
                o_ref[...] = tmp[...].astype(BF16)

        @pl.when(s >= ROPE_NB)
        def _():
            o_ref[...] = jnp.zeros_like(o_ref)

    def src_spec(own):
        def index(s):
            g, t, hf = _rope_step(jnp.minimum(s, ROPE_NB - 1))
            return g, 0, jnp.where(t == own, hf, jnp.where(t > own, LB_PER_CB - 1, 0))
        return pl.BlockSpec((None, S, LANES), index)

    def out_index(s):
        g, t, hf = _rope_step(s)
        col = SEG_B // LANES + (t * DIL_G + g) * LB_PER_CB + hf
        return 0, jnp.where(s < ROPE_NB, col, SEG_Z // LANES + s - ROPE_NB)

    tab = pl.BlockSpec((None, S, 2 * (ROPE_DIM // 2)), lambda s: (_rope_step(jnp.minimum(s, ROPE_NB - 1))[0], 0, 0))
    return pl.pallas_call(
        body, name="rope_merge_bwd", grid=(ROPE_NB + LB_PER_CB,),
        in_specs=[src_spec(0), src_spec(1), src_spec(2), tab, ANY],
        out_specs=pl.BlockSpec((S, LANES), out_index),
        out_shape=jax.ShapeDtypeStruct((S, D_PAD), BF16),
        scratch_shapes=[pltpu.VMEM((S, LANES), F32), pltpu.VMEM((3, S, LANES), F32)],
        input_output_aliases={4: 0},
        compiler_params=_cparams(("arbitrary",)),
    )(dq3, dk3, dv3, tables, dproj)


DIL_NB = S // DIL_BLK


_BQK = (((2,), (2,)), ((0,), (0,)))
_BQD = (((2,), (1,)), ((0,), (0,)))
_BKD = (((1,), (1,)), ((0,), (0,)))


def _dil_mask(g):
    shape = (DIL_NB, DIL_BLK, 2 * DIL_BLK)
    blocks_per_seq = jnp.right_shift(DIL_NB, 2 * g)
    has_prev = jnp.bitwise_and(lax.broadcasted_iota(jnp.int32, shape, 0), blocks_per_seq - 1) != 0
    a = lax.broadcasted_iota(jnp.int32, shape, 1)
    kk = lax.broadcasted_iota(jnp.int32, shape, 2)
    diff = DIL_BLK + a - kk
    return (diff >= 0) & (diff <= DIL_BLK) & ((kk >= DIL_BLK) | has_prev)


def _blocks(t):
    return t.reshape(DIL_NB, DIL_BLK, t.shape[-1])


def _with_prev(t):
    prev = jnp.concatenate([jnp.zeros((DIL_BLK, t.shape[-1]), t.dtype), t[:-DIL_BLK]], axis=0)
    return jnp.concatenate([_blocks(prev), _blocks(t)], axis=1)


def _fold_prev(t2):
    n = t2.shape[-1]
    to_prev = t2[:, :DIL_BLK].reshape(S, n)
    own = t2[:, DIL_BLK:].reshape(S, n)
    return own + jnp.concatenate([to_prev[DIL_BLK:], jnp.zeros((DIL_BLK, n), t2.dtype)], axis=0)


def _dil_group_spec(t, width=DIL_OUT):
    return pl.BlockSpec((None, S, width), lambda g: (t * DIL_G + g, 0, 0))


def _dil_fwd(qkv3):
    def body(q_ref, k_ref, v_ref, o_ref, lse_ref):
        ok = _dil_mask(pl.program_id(0))
        lane = lax.broadcasted_iota(jnp.int32, (S, LANES), 1)
        lse = jnp.zeros((S, LANES), F32)
        for h in range(DIL_HG):
            sl = slice(h * HD, (h + 1) * HD)
            s = jnp.where(ok, _dot(_blocks(q_ref[:, sl]), _with_prev(k_ref[:, sl]), _BQK) * SCALE, NEG_INF)
            m = jnp.max(s, axis=-1, keepdims=True)
            p = jnp.exp(s - m)
            l = jnp.sum(p, axis=-1, keepdims=True)
            o_ref[:, sl] = _dot(p / l, _with_prev(v_ref[:, sl]), _BQD).reshape(S, HD)
            lse = jnp.where(lane == h, (m + jnp.log(l)).reshape(S, 1), lse)
        lse_ref[...] = lse

    return pl.pallas_call(
        body, name="dil_fwd", grid=(DIL_G,),
        in_specs=[_dil_group_spec(0), _dil_group_spec(1), _dil_group_spec(2)],
        out_specs=[_dil_group_spec(0), _dil_group_spec(0, LANES)],
        out_shape=[jax.ShapeDtypeStruct((DIL_G, S, DIL_OUT), F32), jax.ShapeDtypeStruct((DIL_G, S, LANES), F32)],
        compiler_params=_cparams(("parallel",)),
    )(qkv3, qkv3, qkv3)


def _dil_bwd(qkv3, lse3, do3, c3):
    def body(q_ref, k_ref, v_ref, lse_ref, do_ref, c_ref, dq_ref, dk_ref, dv_ref):
        ok = _dil_mask(pl.program_id(0))
        lse = lse_ref[...]
        c = c_ref[...]
        for h in range(DIL_HG):
            sl = slice(h * HD, (h + 1) * HD)
            q = _blocks(q_ref[:, sl])
            k2 = _with_prev(k_ref[:, sl])
            do_h = _blocks(do_ref[:, sl])
            s = jnp.where(ok, _dot(q, k2, _BQK) * SCALE, NEG_INF)
            p = jnp.exp(s - _blocks(lse[:, h:h + 1]))
            dp = _dot(do_h, _with_prev(v_ref[:, sl]), _BQK)
            ds = p * (dp - _blocks(c[:, h:h + 1]))
            dsb = (ds * SCALE).astype(BF16)
            dq_ref[:, sl] = _dot(dsb, k2, _BQD).reshape(S, HD)
            dk_ref[:, sl] = _fold_prev(_dot(dsb, q, _BKD))
            dv_ref[:, sl] = _fold_prev(_dot(p, do_h, _BKD))

    return pl.pallas_call(
        body, name="dil_bwd", grid=(DIL_G,),
        in_specs=[_dil_group_spec(0), _dil_group_spec(1), _dil_group_spec(2), _dil_group_spec(0, LANES),
                  _dil_group_spec(0), _dil_group_spec(0, LANES)],
        out_specs=[_dil_group_spec(0)] * 3,
        out_shape=[jax.ShapeDtypeStruct((DIL_G, S, DIL_OUT), F32)] * 3,
        compiler_params=_cparams(("parallel",)),
    )(qkv3, qkv3, qkv3, lse3, do3, c3)


COMB_T = 256


def _dil_combine(o3, lse3):
    heads_per_lb = LANES // HD

    def body(o_ref, lse_ref, ob_ref, al_ref, *scratch):
        o_tok = [scratch[LB_PER_CB * gg:LB_PER_CB * (gg + 1)] for gg in range(DIL_G)]
        lse_tok = scratch[LB_PER_CB * DIL_G:]
        g = pl.program_id(0)
        for gs in range(DIL_G):
            @pl.when(g == gs)
            def _(gs=gs):
                for tok, sub in _residue_pieces(DIL_R[gs]):
                    for hf in range(LB_PER_CB):
                        o_tok[gs][hf][tok, :] = o_ref[sub, hf * LANES:(hf + 1) * LANES]
                    lse_tok[gs][tok, :] = lse_ref[sub, :]

        @pl.when(g == DIL_G - 1)
        def _():
            def chunk(i, carry):
                rows = pl.ds(pl.multiple_of(i * COMB_T, COMB_T), COMB_T)
                lse = [lse_tok[gg][rows, :] for gg in range(DIL_G)]
                m = jnp.maximum(jnp.maximum(lse[0], lse[1]), lse[2])
                e = [jnp.exp(lse[gg] - m) for gg in range(DIL_G)]
                den = (e[0] + e[1]) + e[2]
                al = [e[gg] / den for gg in range(DIL_G)]
                for gg in range(DIL_G):
                    al_ref[gg, rows, :] = al[gg]
                for h in range(DIL_HG):
                    hf, sl = h // heads_per_lb, slice((h % heads_per_lb) * HD, (h % heads_per_lb + 1) * HD)
                    acc = al[0][:, h:h + 1] * o_tok[0][hf][rows, sl]
                    for gg in range(1, DIL_G):
                        acc = acc + al[gg][:, h:h + 1] * o_tok[gg][hf][rows, sl]
                    ob_ref[rows, h * HD:(h + 1) * HD] = acc
                return carry

            lax.fori_loop(0, S // COMB_T, chunk, 0)

    return pl.pallas_call(
        body, name="dil_combine", grid=(DIL_G,),
        in_specs=[pl.BlockSpec((None, S, DIL_OUT), lambda g: (g, 0, 0)),
                  pl.BlockSpec((None, S, LANES), lambda g: (g, 0, 0))],
        out_specs=[pl.BlockSpec((S, DIL_OUT), lambda g: (0, 0)),
                   pl.BlockSpec((DIL_G, S, LANES), lambda g: (0, 0, 0))],
        out_shape=[jax.ShapeDtypeStruct((S, DIL_OUT), F32), jax.ShapeDtypeStruct((DIL_G, S, LANES), F32)],
        scratch_shapes=[pltpu.VMEM((S, LANES), F32)] * (DIL_G * (LB_PER_CB + 1)),
        compiler_params=_cparams(("arbitrary",)),
    )(o3, lse3)


def _dil_combine_bwd(dob, ob, alpha, deps=()):
    heads_per_lb = LANES // HD

    def body(dob_ref, ob_ref, al_ref, *refs):
        do_ref, c_ref = refs[len(deps):]
        g = pl.program_id(0)
        hf = pl.program_id(1)
        for gs in range(DIL_G):
            @pl.when(g == gs)
            def _(gs=gs):
                for tok, sub in _residue_pieces(DIL_R[gs]):
                    dob = dob_ref[tok, :]
                    prod = dob * ob_ref[tok, :]
                    al = al_ref[tok, :]
                    lane = lax.broadcasted_iota(jnp.int32, al.shape, 1)
                    c = jnp.where(hf == 0, 0.0, c_ref[sub, :])
                    for hh in range(heads_per_lb):
                        sl = slice(hh * HD, (hh + 1) * HD)
                        head = hf * heads_per_lb + hh
                        a = jnp.sum(jnp.where(lane == head, al, 0.0), axis=-1, keepdims=True)
                        do_ref[sub, sl] = (a * dob[:, sl]).astype(BF16)
                        c = jnp.where(lane == head, a * jnp.sum(prod[:, sl], axis=-1, keepdims=True), c)
                    c_ref[sub, :] = c

    half = pl.BlockSpec((S, LANES), lambda g, hf: (0, hf))
    return pl.pallas_call(
        body, name="dil_combine_bwd", grid=(DIL_G, LB_PER_CB),
        in_specs=[half, half, pl.BlockSpec((None, S, LANES), lambda g, hf: (g, 0, 0))] + [ANY] * len(deps),
        out_specs=[pl.BlockSpec((None, S, LANES), lambda g, hf: (g, 0, hf)),
                   pl.BlockSpec((None, S, LANES), lambda g, hf: (g, 0, 0))],
        out_shape=[jax.ShapeDtypeStruct((DIL_G, S, DIL_OUT), BF16), jax.ShapeDtypeStruct((DIL_G, S, LANES), F32)],
        compiler_params=_cparams(("parallel", "arbitrary")),
    )(dob, ob, alpha, *deps)


DOWN_BWD_TN = 512
DOWN_BWD_RING = 3


def _mlp_down_bwd(dx3, w_down, u):
    tn, n_steps = DOWN_BWD_TN, DFF // DOWN_BWD_TN

    def body(a_ref, b_ref, u_hbm, o_ref, ubuf, sems):
        j = pl.program_id(0)

        def fetch(s):
            slot = s % DOWN_BWD_RING
            return pltpu.make_async_copy(u_hbm.at[:, pl.ds(pl.multiple_of(s * tn, tn), tn)], ubuf.at[slot],
                                         sems.at[slot])

        @pl.when(j == 0)
        def _():
            fetch(0).start()
            fetch(1).start()

        @pl.when(j + 2 < n_steps)
        def _():
            fetch(j + 2).start()

        acc = _dot(a_ref[...], b_ref[...], _NT)
        fetch(j).wait()
        o_ref[...] = (acc * (2.0 * jnp.maximum(ubuf[j % DOWN_BWD_RING], 0.0))).astype(BF16)

    return pl.pallas_call(
        body, name="mlp_down_bwd", grid=(n_steps,),
        in_specs=[pl.BlockSpec((S, D), lambda j: (0, 0)), pl.BlockSpec((tn, D), lambda j: (j, 0)), ANY],
        out_specs=pl.BlockSpec((S, tn), lambda j: (0, j)),
        out_shape=jax.ShapeDtypeStruct((S, DFF), BF16),
        scratch_shapes=[pltpu.VMEM((DOWN_BWD_RING, S, tn), F32), pltpu.SemaphoreType.DMA((DOWN_BWD_RING,))],
        compiler_params=_cparams(("arbitrary",)),
    )(dx3, w_down, u)


def _local_step(x, tgt, h1, g_attn, g_mlp, g_final, b_pad, w_in_t, hooks):
    tables = _rope_tables()

    first = hooks.first_deps()
    proj_a = _mm(h1, w_in_t, mode="nt", tm=S, tn=512, n=3 * FOX_W, out_dtypes=[BF16], name="proj_a", deps=first)
    proj_b = _mm(h1, w_in_t, mode="nt", tm=S, tn=DIL_W, n=3 * DIL_W, b_off=SEG_B // DIL_W, out_dtypes=[F32], name="proj_b",
                deps=first)
    proj_g = _mm(h1, w_in_t, mode="nt", tm=S, tn=D, n=2 * D, b_off=SEG_G // D, out_dtypes=[F32], name="proj_g",
                deps=first)
    proj_f = _mm(h1, w_in_t, mode="nt", tm=S, tn=LANES, n=LANES, b_off=SEG_F // LANES, out_dtypes=[F32], name="proj_f",
                deps=first)

    q_aug, k_aug = _fox_prepare(proj_a, proj_f, b_pad)
    oa, lse_a = _fox_fwd(q_aug, k_aug, proj_a)

    qkv3 = _rope_split(proj_b, tables)
    o3, lse3 = _dil_fwd(qkv3)
    ob, alpha = _dil_combine(o3, lse3)

    w_a, w_b, w_out = hooks.mixer_weights(ob)
    ya, yb, mixed, x2, h2 = _mixer_tail(oa, ob, w_a, w_b, w_out, proj_g, x, g_mlp)
    w_up_sh, w_down = hooks.mlp_weights(h2)

    def up_epilogue(acc):
        r = jnp.maximum(acc, 0.0)
        return acc, r * r

    u, act = _mm(h2, w_up_sh, mode="nn", tm=S, tn=DFF // N_DEV, b_sharded=True, out_dtypes=[F32, BF16],
                 name="mlp_up", epilogue=up_epilogue)

    def loss_epilogue(acc, rows, vecs):
        dx3, dg, loss = _final_loss_rows(rows[0] + acc, vecs[0], rows[1])
        return (dx3,), (dg, loss)

    dx3, dg_final, loss = _mm_rows(act, w_down, tm=ROWS_TM, name="mlp_down_loss", epilogue=loss_epilogue,
                                   rows=(x2, tgt), vecs=(g_final,), row_out=(F32,), vec_out=(D, LANES))

    def rms_bwd_epilogue(acc, rows, vecs):
        dx, dg = _rms_bwd_rows(acc, rows[0], rows[1], vecs[0])
        return (dx,), (dg,)

    du = _mlp_down_bwd(dx3, w_down, u)
    dw_down = _mm(act, dx3, mode="tn", tm=512, tn=D, out_dtypes=[F32], name="dw_down")
    dw_up_sh = _mm(h2, du, mode="tn", tm=D, tn=DFF // N_DEV, out_sharded=True, out_dtypes=[F32], name="dw_up")
    dx2, dg_mlp = _mm_rows(du, w_up_sh, b_sharded=True, tm=ROWS_TM, name="mlp_up_bwd", epilogue=rms_bwd_epilogue,
                           rows=(x2, dx3), vecs=(g_mlp,), row_out=(F32,), vec_out=(D,),
                           deps=hooks.mlp_grads(dw_up_sh, dw_down))

    dw_out = _mm(mixed, dx2, mode="tn", tm=D, tn=D, out_dtypes=[F32], name="dw_out")

    dyab, dproj = _out_proj_gate_bwd(dx2, w_out, proj_g, ya, yb)
    doa = _mm(dyab, w_a, mode="nt", tm=S, tn=FOX_W, k=D, a_off=0, b_sharded=True, out_dtypes=[BF16],
              name="branch_a_bwd")
    dw_a = _mm(oa, dyab, mode="tn", tm=FOX_W, tn=D, n=D, b_off=0, out_sharded=True, out_dtypes=[F32],
               name="dw_branch_a")
    dob = _mm(dyab, w_b, mode="nt", tm=S, tn=CB, k=D, a_off=1, b_sharded=True, out_dtypes=[F32],
              name="branch_b_bwd")
    dw_b = _mm(ob, dyab, mode="tn", tm=DIL_OUT, tn=D, n=D, b_off=1, out_sharded=True, out_dtypes=[F32],
               name="dw_branch_b")

    dproj, dft, dfs = _fox_bwd(q_aug, k_aug, proj_a, lse_a, doa, dproj, deps=hooks.mixer_grads(dw_a, dw_b, dw_out))
    dproj, db = _fox_dscan(dft, dfs, proj_f, b_pad, dproj)

    do3, c3 = _dil_combine_bwd(dob, ob, alpha, deps=hooks.mid_backward(db))
    dq3, dk3, dv3 = _dil_bwd(qkv3, lse3, do3, c3)
    dproj = _rope_merge_bwd(dq3, dk3, dv3, tables, dproj)

    dw_in_t = _mm(dproj, h1, mode="tn", tm=DW_IN_TM, tn=D, out_dtypes=[F32], name="dw_in")
    dx, dg_attn = _mm_rows(dproj, w_in_t, tm=ROWS_TM // 2, name="proj_bwd", epilogue=rms_bwd_epilogue,
                           rows=(x, dx2), vecs=(g_attn,), row_out=(F32,), vec_out=(D,),
                           deps=hooks.w_in_grad(dw_in_t))

    return loss, dx, (dg_attn, db, dg_mlp, dg_final)


SHARD_SHAPES = ((D_IN // N_DEV, D), (FOX_W, D // N_DEV), (DIL_OUT, D // N_DEV), (D // N_DEV, D),
                (D, DFF // N_DEV), (DFF // N_DEV, D))
W_NAMES = ("w_in", "w_a", "w_b", "w_out", "w_up", "w_down")
AG_COPIES = 7
HBM = pl.BlockSpec(memory_space=pltpu.HBM)
SEM = pl.BlockSpec(memory_space=pltpu.SEMAPHORE)


def _place():
    return lax.axis_index("x"), lax.axis_index("y"), lax.axis_index("c")


W_SLAB = 752
W_WIN = 272
PAD_BLK = 256


def _pad_runs():
    runs = sorted((pad, pad + b - a, p, a) for p in range(N_DEV) for a, b, pad in _shard_pieces(p))
    blocks = []
    for k in range(D_PAD // PAD_BLK):
        lo, hi = k * PAD_BLK, (k + 1) * PAD_BLK
        blocks.append([(max(lo, r0) - lo, min(hi, r1) - lo, p, a + max(lo, r0) - r0)
                       for r0, r1, p, a in runs if max(lo, r0) < min(hi, r1)])
    return blocks


def _gather_w_in(shard, x_in, g_attn, late_shards):
    blocks = _pad_runs()

    norm_t = 512
    n_chunks = S // norm_t

    n_late = len(late_shards)
    late_shapes = [t.shape for t in late_shards]

    def body(x_ref, xin_ref, g_ref, *refs):
        late_in, (out_ref, h1_ref), late_out = refs[:n_late], refs[n_late:n_late + 2], refs[n_late + 2:2 * n_late + 2]
        scratch = refs[2 * n_late + 2:]
        stage, land, xbuf = scratch[:3]
        late_stage, late_cast = scratch[3:3 + n_late], scratch[3 + n_late:3 + 2 * n_late]
        send_sems, recv_sems, load_sem, x_sems, late_sems, store_sems = scratch[3 + 2 * n_late:]
        x, y, c = _place()
        me, sibling = (x, y, c), (x, y, 1 - c)
        chips = [(1 - x, y), (x, 1 - y), (1 - x, 1 - y)]

        def slot(px, py, pc):
            return land.at[4 * px + 2 * py + pc]

        def copy(k, block, to):
            return pltpu.make_async_remote_copy(
                src_ref=slot(*block), dst_ref=slot(*block), send_sem=send_sems.at[k], recv_sem=recv_sems.at[k],
                device_id=to, device_id_type=MESH)

        load = pltpu.make_async_copy(x_ref, stage, load_sem)
        load.start()
        load.wait()
        land[4 * x + 2 * y + c] = stage[...].astype(BF16)
        sent = [copy(0, me, sibling)] + [copy(1 + j, me, (*chip, c)) for j, chip in enumerate(chips)]
        for cp in sent:
            cp.start()

        def x_load(i):
            return pltpu.make_async_copy(xin_ref.at[pl.ds(i * norm_t, norm_t)], xbuf.at[i % 2], x_sems.at[i % 2])

        late_loads = [pltpu.make_async_copy(late_in[i], late_stage[i], late_sems.at[i]) for i in range(n_late)]
        for ld in late_loads:
            ld.start()
        x_load(0).start()
        for i in range(n_chunks):
            if i + 1 < n_chunks:
                x_load(i + 1).start()
            x_load(i).wait()
            h1_ref[i * norm_t:(i + 1) * norm_t, :] = _rms_rows(xbuf[i % 2], g_ref[...]).astype(BF16)
        stores = []
        for i in range(n_late):
            late_loads[i].wait()
            late_cast[i][...] = late_stage[i][...].astype(BF16)
            stores.append(pltpu.make_async_copy(late_cast[i], late_out[i].at[4 * x + 2 * y + c], store_sems.at[i]))
            stores[-1].start()

        for j, chip in enumerate(chips):
            copy(1 + j, (*chip, c), me).wait_recv()
            sent.append(copy(4 + j, (*chip, c), sibling))
            sent[-1].start()
        copy(0, sibling, me).wait_recv()
        for j, chip in enumerate(chips):
            copy(4 + j, (*chip, 1 - c), me).wait_recv()
        for cp in sent:
            cp.wait_send()
        for st in stores:
            st.wait()

        row = lax.broadcasted_iota(jnp.int32, (PAD_BLK, D), 0)
        for k, runs in enumerate(blocks):
            out = jnp.zeros((PAD_BLK, D), F32)
            for o_lo, o_hi, p, a in runs:
                start = min(a // 16 * 16, W_SLAB - W_WIN)
                win = land[p, start:start + W_WIN, :].astype(F32)
                moved = pltpu.roll(win, (o_lo - (a - start)) % W_WIN, 0)[:PAD_BLK]
                out = jnp.where((row >= o_lo) & (row < o_hi), moved, out)
            out_ref[k * PAD_BLK:(k + 1) * PAD_BLK, :] = out.astype(BF16)

    return pl.pallas_call(
        body, name="all_gather_w_in",
        out_shape=[jax.ShapeDtypeStruct((D_PAD, D), BF16), jax.ShapeDtypeStruct((S, D), BF16)]
                  + [jax.ShapeDtypeStruct((N_DEV,) + sh, BF16) for sh in late_shapes],
        in_specs=[ANY, ANY, pl.BlockSpec(memory_space=pltpu.VMEM)] + [ANY] * n_late,
        out_specs=[pl.BlockSpec(memory_space=pltpu.VMEM)] * 2 + [ANY] * n_late,
        scratch_shapes=[pltpu.VMEM((W_SLAB, D), F32), pltpu.VMEM((N_DEV, W_SLAB, D), BF16),
                        pltpu.VMEM((2, norm_t, D), F32)]
                       + [pltpu.VMEM(sh, F32) for sh in late_shapes] + [pltpu.VMEM(sh, BF16) for sh in late_shapes]
                       + [pltpu.SemaphoreType.DMA((AG_COPIES,)), pltpu.SemaphoreType.DMA((AG_COPIES,)),
                          pltpu.SemaphoreType.DMA(()), pltpu.SemaphoreType.DMA((2,)),
                          pltpu.SemaphoreType.DMA((n_late,)), pltpu.SemaphoreType.DMA((n_late,))],
        compiler_params=_cparams(None),
    )(shard, x_in, g_attn, *late_shards)


PEER_COPIES = N_DEV - 1
SIBLING_COPIES = 4
CHIP_COPIES = 3


def _peer_copies(_, land_refs, send_sems, recv_sems):
    x, y, c = _place()
    mine = 4 * x + 2 * y + c
    copies = []
    for i, ref in enumerate(land_refs):
        for k in range(1, N_DEV):
            peer = (x ^ (k >> 2), y ^ ((k >> 1) & 1), c ^ (k & 1))
            n = i * PEER_COPIES + k - 1
            copies.append(pltpu.make_async_remote_copy(
                src_ref=ref.at[mine], dst_ref=ref.at[mine], send_sem=send_sems.at[n], recv_sem=recv_sems.at[n],
                device_id=peer, device_id_type=MESH))
    return copies


def _sibling_copies(g_refs, r_refs, send_sems, recv_sems):
    x, y, c = _place()
    return [pltpu.make_async_remote_copy(
        src_ref=g_refs[i].at[2 * k + (1 - c)], dst_ref=r_refs[i].at[k],
        send_sem=send_sems.at[SIBLING_COPIES * i + k], recv_sem=recv_sems.at[SIBLING_COPIES * i + k],
        device_id=(x, y, 1 - c), device_id_type=MESH) for i in range(len(g_refs)) for k in range(SIBLING_COPIES)]


def _sibling_half_copies(g_refs, r_refs, send_sems, recv_sems):
    x, y, c = _place()
    return [pltpu.make_async_remote_copy(
        src_ref=g_refs[i].at[k], dst_ref=r_refs[i].at[k],
        send_sem=send_sems.at[SIBLING_COPIES * i + k], recv_sem=recv_sems.at[SIBLING_COPIES * i + k],
        device_id=(x, y, 1 - c), device_id_type=MESH) for i in range(len(g_refs)) for k in range(SIBLING_COPIES)]


def _chip_copies(p_refs, r_refs, send_sems, recv_sems):
    x, y, c = _place()
    chips = [(1 - x, y), (x, 1 - y), (1 - x, 1 - y)]
    return [pltpu.make_async_remote_copy(
        src_ref=p_refs[i].at[2 * cx + cy], dst_ref=r_refs[i].at[j],
        send_sem=send_sems.at[CHIP_COPIES * i + j], recv_sem=recv_sems.at[CHIP_COPIES * i + j],
        device_id=(cx, cy, c), device_id_type=MESH) for i in range(len(p_refs)) for j, (cx, cy) in enumerate(chips)]


def _in_hbm(a):
    return pltpu.with_memory_space_constraint(a, pltpu.HBM)


def _exchange_start(name, copies_fn, n_copies, srcs, lands, after=()):
    n_s, n_l, n_a = len(srcs), len(lands), len(after)

    def body(*refs):
        outs = refs[n_s + n_l + n_a:]
        for cp in copies_fn(refs[:n_s], refs[n_s:n_s + n_l], outs[0], outs[1]):
            cp.start()
        outs[-1][...] = jnp.zeros_like(outs[-1])

    res = pl.pallas_call(
        body, name=name,
        out_shape=[pltpu.SemaphoreType.DMA((n_copies,)), pltpu.SemaphoreType.DMA((n_copies,))]
                  + [pltpu.HBM(t.shape, t.dtype) for t in (*srcs, *lands)] + [jax.ShapeDtypeStruct((8, LANES), F32)],
        in_specs=[HBM] * (n_s + n_l) + [ANY] * n_a,
        out_specs=[SEM, SEM] + [HBM] * (n_s + n_l) + [pl.BlockSpec(memory_space=pltpu.VMEM)],
        input_output_aliases={i: 2 + i for i in range(n_s + n_l)},
        compiler_params=pltpu.CompilerParams(has_side_effects=pltpu.SideEffectType.DATAFLOW_SIDE_EFFECTING),
    )(*[_in_hbm(t) for t in (*srcs, *lands)], *after)
    return res[0], res[1], list(res[2:2 + n_s]), list(res[2 + n_s:2 + n_s + n_l]), res[-1]


def _exchange_wait(name, copies_fn, started, after):
    send_sems, recv_sems, srcs, lands, _ = started
    n_s, n_l = len(srcs), len(lands)

    def body(*refs):
        for cp in copies_fn(refs[:n_s], refs[n_s:n_s + n_l], refs[n_s + n_l], refs[n_s + n_l + 1]):
            cp.wait_send()
            cp.wait_recv()

    res = pl.pallas_call(
        body, name=name,
        out_shape=[pltpu.HBM(t.shape, t.dtype) for t in (*srcs, *lands)],
        in_specs=[HBM] * (n_s + n_l) + [SEM, SEM, ANY],
        out_specs=[HBM] * (n_s + n_l),
        input_output_aliases={i: i for i in range(n_s + n_l)},
        compiler_params=pltpu.CompilerParams(has_side_effects=pltpu.SideEffectType.DATAFLOW_SIDE_EFFECTING),
    )(*srcs, *lands, send_sems, recv_sems, after)
    return list(res[:n_s]), list(res[n_s:])


def _shard_block(shape):
    rows, cols = shape
    if rows * cols <= 256 * 1024:
        return rows, cols
    if rows % 256:
        return rows, 512
    return max(256, 256 * 1024 // cols), cols


def _chip_partial(ids, g_stack, recv1, name):
    shape = g_stack.shape[1:]
    br, bc = shape
    whole = g_stack.shape[0] == N_DEV

    def body(ids_ref, g_ref, r_ref, pb_ref, own_ref):
        s = g_ref[...] + r_ref[...]
        pb_ref[...] = s.astype(BF16)

        @pl.when(pl.program_id(2) == ids_ref[1])
        def _():
            own_ref[...] = s

    grid_spec = pltpu.PrefetchScalarGridSpec(
        num_scalar_prefetch=1, grid=(shape[0] // br, shape[1] // bc, 4),
        in_specs=[pl.BlockSpec((None, br, bc), lambda r, q, k, ids: (2 * k + ids[0] if whole else k, r, q)),
                  pl.BlockSpec((None, br, bc), lambda r, q, k, ids: (k, r, q))],
        out_specs=[pl.BlockSpec((None, br, bc), lambda r, q, k, ids: (k, r, q)),
                   pl.BlockSpec((br, bc), lambda r, q, k, ids: (r, q))])
    return pl.pallas_call(
        body, name=name, grid_spec=grid_spec,
        out_shape=[jax.ShapeDtypeStruct((4,) + shape, BF16), jax.ShapeDtypeStruct(shape, F32)],
        compiler_params=_cparams(("parallel", "parallel", "arbitrary")),
    )(ids, g_stack, recv1)


def _adamw(w, g, m, v):
    m = ADAM_B1 * m + (1.0 - ADAM_B1) * g
    v = ADAM_B2 * v + (1.0 - ADAM_B2) * (g * g)
    m_hat = m / (1.0 - ADAM_B1 ** ADAM_STEP)
    v_hat = v / (1.0 - ADAM_B2 ** ADAM_STEP)
    delta = -ADAM_LR * (m_hat / (jnp.sqrt(v_hat) + ADAM_EPS) + ADAM_WD * w)
    return delta, m, v


def _reduce_adamw(own, recv2, w, m, v, name, deps=()):
    shape = own.shape
    br, bc = _shard_block(shape)

    def body(own_ref, r_ref, w_ref, m_ref, v_ref, *refs):
        g_ref, d_ref, nm_ref, nv_ref = refs[len(deps):]
        g = own_ref[...]
        for j in range(3):
            g = g + r_ref[j].astype(F32)
        delta, nm, nv = _adamw(w_ref[...], g, m_ref[...], v_ref[...])
        g_ref[...] = g
        d_ref[...] = delta
        nm_ref[...] = nm
        nv_ref[...] = nv

    blk = pl.BlockSpec((br, bc), lambda r, q: (r, q))
    return pl.pallas_call(
        body, name=name, grid=(shape[0] // br, shape[1] // bc),
        in_specs=[blk, pl.BlockSpec((3, br, bc), lambda r, q: (0, r, q)), blk, blk, blk] + [ANY] * len(deps),
        out_specs=[blk] * 4, out_shape=[jax.ShapeDtypeStruct(shape, F32)] * 4,
        compiler_params=_cparams(("parallel", "parallel")),
    )(own, recv2, w, m, v, *deps)


def _small_adamw(stack, w, m, v):
    def body(s_ref, w_ref, m_ref, v_ref, loss_ref, *o_refs):
        g = s_ref[0]
        for s in range(1, N_DEV):
            g = g + s_ref[s]
        delta, nm, nv = _adamw(w_ref[...], g, m_ref[...], v_ref[...])
        loss_ref[...] = g[LOSS_ROW:LOSS_ROW + 1, 0:1]
        for q, t in enumerate((g, delta, nm, nv)):
            attn_ref, b_ref, mlp_ref, final_ref = o_refs[4 * q:4 * q + 4]
            for i in range(8):
                cols = slice(i * LANES, (i + 1) * LANES)
                attn_ref[:, cols] = t[i:i + 1]
                mlp_ref[:, cols] = t[8 + i:9 + i]
                final_ref[:, cols] = t[16 + i:17 + i]
            b_ref[...] = t[24:25, :FOX_H]

    vm = pl.BlockSpec(memory_space=pltpu.VMEM)
    unpacked = [jax.ShapeDtypeStruct((1, n), F32) for n in (D, FOX_H, D, D)]
    return pl.pallas_call(
        body, name="small_adamw",
        in_specs=[vm] * 4, out_specs=[vm] * 17,
        out_shape=[jax.ShapeDtypeStruct((1, 1), F32)] + unpacked * 4,
    )(stack, w, m, v)


W_IN_SEGMENTS = ((0, 3 * FOX_W, SEG_A), (3 * FOX_W, 3 * FOX_W + FOX_H, SEG_F),
                 (3 * FOX_W + FOX_H, 3 * FOX_W + FOX_H + 3 * DIL_W, SEG_B), (3 * FOX_W + FOX_H + 3 * DIL_W, D_IN, SEG_G))


def _shard_pieces(p):
    rows = D_IN // N_DEV
    lo, hi = p * rows, (p + 1) * rows
    return [(max(lo, a) - lo, min(hi, b) - lo, pad + max(lo, a) - a)
            for a, b, pad in W_IN_SEGMENTS if max(lo, a) < min(hi, b)]


def _unpad_runs(p):
    rows = D_IN // N_DEV
    blocks = []
    for lo in range(0, rows, PAD_BLK):
        hi = min(lo + PAD_BLK, rows)
        blocks.append([(max(lo, a) - lo, min(hi, b) - lo, pad + max(lo, a) - a)
                       for a, b, pad in _shard_pieces(p) if max(lo, a) < min(hi, b)])
    return blocks


def _unpad_dw_in_t(dwp, ids, kept, deps=()):
    rows = D_IN // N_DEV
    plans = [_unpad_runs(p) for p in range(N_DEV)]
    windows = [[min(src // 8 * 8, D_PAD - W_WIN) for runs in plan for _, _, src in runs] for plan in plans]
    n_win = max(len(w) for w in windows)

    def body(ids_ref, src_ref, *refs):
        o_ref, buf, sems = refs[len(deps):]
        side = ids_ref[0] if kept else 1 - ids_ref[0]
        p = 2 * pl.program_id(0) + side
        row = lax.broadcasted_iota(jnp.int32, (PAD_BLK, D), 0)

        def fetch(q):
            slot = (q // 2) % 2
            return [pltpu.make_async_copy(src_ref.at[pl.ds(start, W_WIN)], buf.at[slot, n], sems.at[slot, n])
                    for n, start in enumerate(windows[q])]

        for q in range(N_DEV):
            @pl.when(p == q)
            def _(q=q):
                for nxt in ([q, q + 2] if q < 2 else [q + 2]):
                    if nxt < N_DEV:
                        for cp in fetch(nxt):
                            cp.start()
                copies, n = fetch(q), 0
                for j, runs in enumerate(plans[q]):
                    out = None
                    for o_lo, o_hi, src in runs:
                        copies[n].wait()
                        moved = pltpu.roll(buf[(q // 2) % 2, n], (o_lo - (src - windows[q][n])) % W_WIN, 0)[:PAD_BLK]
                        out = moved if out is None else jnp.where((row >= o_lo) & (row < o_hi), moved, out)
                        n += 1
                    size = min(PAD_BLK, rows - j * PAD_BLK)
                    o_ref[j * PAD_BLK:j * PAD_BLK + size, :] = out[:size]

    grid_spec = pltpu.PrefetchScalarGridSpec(
        num_scalar_prefetch=1, grid=(N_DEV // 2,),
        in_specs=[ANY] * (1 + len(deps)), out_specs=pl.BlockSpec((None, rows, D), lambda k, ids: (k, 0, 0)),
        scratch_shapes=[pltpu.VMEM((2, n_win, W_WIN, D), F32), pltpu.SemaphoreType.DMA((2, n_win))])
    return pl.pallas_call(
        body, name="unpad_dw_in_kept" if kept else "unpad_dw_in_sent", grid_spec=grid_spec,
        out_shape=jax.ShapeDtypeStruct((N_DEV // 2, rows, D), F32),
        compiler_params=_cparams(("arbitrary",)),
    )(ids, dwp, *deps)


LOSS_ROW = 25


def _pack_small(g_attn, b, g_mlp, g_final, loss_row=None):
    tail = jnp.pad(b, ((0, 7), (0, LANES - b.shape[1])))
    if loss_row is not None:
        tail = tail + jnp.pad(loss_row, ((LOSS_ROW - 24, 31 - LOSS_ROW), (0, 0)))
    return jnp.concatenate([g_attn.reshape(8, LANES), g_mlp.reshape(8, LANES), g_final.reshape(8, LANES), tail], axis=0)


class _StepComm:
    def __init__(self, ids, w_sh, m_sh, v_sh, lands, after_w_in):
        self.ids = ids
        self.w_sh, self.m_sh, self.v_sh = w_sh, m_sh, v_sh
        self.updates = None
        self.gather_mixer = _exchange_start("gather_mixer_start", _peer_copies, PEER_COPIES * 3, [], lands[:3],
                                            after=(after_w_in,))
        self.gather_mlp = _exchange_start("gather_mlp_start", _peer_copies, PEER_COPIES * 2, [], lands[3:],
                                          after=(self.gather_mixer[-1],))
        self.sibling = {}
        self.chips = {}
        self.own = {}
        self.names = {}
        self.copies_fn = {}
        self.kept = {}

    def _reduce_start(self, group, names, grads, after=(), copies_fn=_sibling_copies):
        lands = [lax.empty((SIBLING_COPIES,) + t.shape[1:], F32) for t in grads]
        self.sibling[group] = _exchange_start("grad_%s_sibling_start" % group, copies_fn,
                                              SIBLING_COPIES * len(grads), grads, lands, after=after)
        self.names[group] = names
        self.copies_fn[group] = copies_fn
        return self.sibling[group][-1]

    def _reduce_mid(self, group, after):
        grads, recv1 = _exchange_wait("grad_%s_sibling_wait" % group, self.copies_fn[group], self.sibling[group],
                                      after)
        grads = self.kept.get(group, grads)
        parts = [_chip_partial(self.ids, g, r, "grad_partial_" + n) for g, r, n in zip(grads, recv1, self.names[group])]
        self.own[group] = [p[1] for p in parts]
        srcs = [p[0] for p in parts]
        lands = [lax.empty((CHIP_COPIES,) + t.shape[1:], BF16) for t in srcs]
        self.chips[group] = _exchange_start("grad_%s_chips_start" % group, _chip_copies, CHIP_COPIES * len(srcs),
                                            srcs, lands)
        return self.chips[group][-1]

    def reduced(self, group, after):
        _, recv2 = _exchange_wait("grad_%s_chips_wait" % group, _chip_copies, self.chips[group], after)
        return list(zip(self.own[group], recv2))

    def first_deps(self):
        return [self.gather_mlp[-1]]

    def mixer_weights(self, after):
        _, (g_a, g_b, g_out) = _exchange_wait("gather_mixer_wait", _peer_copies, self.gather_mixer, after)
        return g_a, g_b, g_out.reshape(D, D)

    def mlp_weights(self, after):
        _, (g_up, g_down) = _exchange_wait("gather_mlp_wait", _peer_copies, self.gather_mlp, after)
        return g_up, g_down.reshape(DFF, D)

    def mlp_grads(self, dw_up_sh, dw_down):
        return [self._reduce_start("mlp", W_NAMES[4:], [dw_up_sh, dw_down.reshape((N_DEV,) + SHARD_SHAPES[5])])]

    def mixer_grads(self, dw_a, dw_b, dw_out):
        token = self._reduce_mid("mlp", dw_b)
        grads = [dw_a, dw_b, dw_out.reshape((N_DEV,) + SHARD_SHAPES[3])]
        return [self._reduce_start("mixer", W_NAMES[1:4], grads, after=(token,))]

    def mid_backward(self, after):
        return [self._reduce_mid("mixer", after)]

    def w_in_grad(self, dw_in_t):
        token = self._reduce_start("w_in", W_NAMES[:1], [_unpad_dw_in_t(dw_in_t, self.ids, kept=False)],
                                   copies_fn=_sibling_half_copies)
        self.kept["w_in"] = [_unpad_dw_in_t(dw_in_t, self.ids, kept=True, deps=[token])]
        reduced = self.reduced("mixer", token) + self.reduced("mlp", token)
        self.updates, last = [None] * len(reduced), self.kept["w_in"][0]
        for i in (3, 4, 0, 1, 2):
            if i == 4:
                last = self._reduce_mid("w_in", last)
            self.updates[i] = _reduce_adamw(*reduced[i], self.w_sh[1 + i], self.m_sh[1 + i], self.v_sh[1 + i],
                                            "adamw_" + W_NAMES[1 + i], deps=[last])
            last = self.updates[i][0]
        return [last]

    def w_in_update(self, after):
        (reduced,) = self.reduced("w_in", after)
        return _reduce_adamw(*reduced, self.w_sh[0], self.m_sh[0], self.v_sh[0], "adamw_" + W_NAMES[0])


def kernel(x, norm_attn_g, w_in, b_forget, w_branch_a, w_branch_b, w_out, norm_mlp_g, w_up, w_down, norm_final_g, loss_target, m_norm_attn_g, m_w_in, m_b_forget, m_w_branch_a, m_w_branch_b, m_w_out, m_norm_mlp_g, m_w_up, m_w_down, m_norm_final_g, v_norm_attn_g, v_w_in, v_b_forget, v_w_branch_a, v_w_branch_b, v_w_out, v_norm_mlp_g, v_w_up, v_w_down, v_norm_final_g):
    cx, cy, cc = _place()
    ids = jnp.stack([cc, 2 * cx + cy]).astype(jnp.int32)

    w_sh = [w_in[0].T] + [t[0] for t in (w_branch_a, w_branch_b, w_out, w_up, w_down)]
    m_sh = [m_w_in[0].T] + [t[0] for t in (m_w_branch_a, m_w_branch_b, m_w_out, m_w_up, m_w_down)]
    v_sh = [v_w_in[0].T] + [t[0] for t in (v_w_branch_a, v_w_branch_b, v_w_out, v_w_up, v_w_down)]

    w_in_t, h1, *lands = _gather_w_in(jnp.pad(w_sh[0], ((0, W_SLAB - w_sh[0].shape[0]), (0, 0))), x[0], norm_attn_g,
                                      w_sh[1:])
    comm = _StepComm(ids, w_sh, m_sh, v_sh, lands, w_in_t)
    b_pad = jnp.pad(b_forget, ((0, 0), (0, LANES - FOX_H)))

    loss_row, dx, dsmall = _local_step(
        x[0], loss_target[0], h1, norm_attn_g, norm_mlp_g, norm_final_g.reshape(1, D), b_pad, w_in_t, comm)

    dg_attn, db, dg_mlp, dg_final = dsmall
    stack = lax.dynamic_update_slice(jnp.zeros((N_DEV, SMALL_R, LANES), F32),
                                     _pack_small(dg_attn, db, dg_mlp, dg_final, loss_row)[None],
                                     (4 * cx + 2 * cy + cc, 0, 0))
    small_gather = _exchange_start("small_gather_start", _peer_copies, PEER_COPIES, [], [stack])
    w_in_update = comm.w_in_update(small_gather[-1])
    _, (stack,) = _exchange_wait("small_gather_wait", _peer_copies, small_gather, w_in_update[0])
    loss, *small = _small_adamw(
        stack,
        _pack_small(norm_attn_g, b_forget, norm_mlp_g, norm_final_g.reshape(1, D)),
        _pack_small(m_norm_attn_g, m_b_forget, m_norm_mlp_g, m_norm_final_g.reshape(1, D)),
        _pack_small(v_norm_attn_g, v_b_forget, v_norm_mlp_g, v_norm_final_g.reshape(1, D)))
    big = [w_in_update] + comm.updates

    outs = [loss.reshape(()), dx[None]]
    for q in range(4):
        s_attn, s_b, s_mlp, s_final = small[4 * q:4 * q + 4]
        s_final = s_final.reshape(D)
        b_in, b_a, b_b, b_out, b_up, b_down = [t[q][None] for t in big]
        b_in = jnp.swapaxes(b_in, 1, 2)
        outs += [s_attn, b_in, s_b, b_a, b_b, b_out, s_mlp, b_up, b_down, s_final]
    return tuple(outs)
```

```python
import functools

import jax
import jax.numpy as jnp
from jax import lax
from jax.experimental import pallas as pl
from jax.experimental.pallas import tpu as pltpu

F32 = jnp.float32
BF16 = jnp.bfloat16
MESH = pl.DeviceIdType.MESH

S = 2048
D = 1024
HD = 64
FOX_H = 8
FOX_W = FOX_H * HD
DIL_HG = 4
DIL_G = 3
DIL_W = DIL_G * DIL_HG * HD
DIL_OUT = DIL_HG * HD
DIL_BLK = 128
DIL_R = (1, 4, 16)
DFF = 4 * D
D_IN = 3 * FOX_W + FOX_H + 3 * DIL_W + 2 * D
EPS = 1e-6
NEG_INF = -1e30
SCALE = HD ** -0.5
ROPE_THETA = 500000.0
ROPE_DIM = HD // 4
N_DEV = 8

ADAM_LR = 0.001
ADAM_B1 = 0.9
ADAM_B2 = 0.999
ADAM_EPS = 1e-08
ADAM_WD = 0.01
ADAM_STEP = 10

LANES = 128
CB = 256
SEG_A = 0
SEG_B = 3 * FOX_W
SEG_Z = SEG_B + 3 * DIL_W
SEG_G = SEG_Z + CB
SEG_F = SEG_G + 2 * D
D_PAD = SEG_F + CB
DW_IN_TM = D_PAD // 10
SMALL_R = 32

VMEM_MB = 56


def _cparams(dims=None, vmem_mb=VMEM_MB, **kw):
    return pltpu.CompilerParams(dimension_semantics=dims, vmem_limit_bytes=vmem_mb << 20, **kw)


ANY = pl.BlockSpec(memory_space=pl.ANY)

_NN = (((1,), (0,)), ((), ()))
_NT = (((1,), (1,)), ((), ()))
_TN = (((0,), (0,)), ((), ()))


def _dot(a, b, dims):
    return lax.dot_general(a.astype(BF16), b.astype(BF16), dims, preferred_element_type=F32)


def _mm(a, b, *, mode, tm, tn, out_dtypes, name, n=None, k=None, a_off=0, b_off=0,
        b_sharded=False, out_sharded=False, epilogue=None, extras=(), deps=()):
    n_sh = b.shape[-1] if b_sharded else None
    if mode == "nn":
        m = a.shape[0]
        k = k or a.shape[1]
        a_spec = pl.BlockSpec((tm, k), lambda i, j: (i, a_off))
        if b_sharded:
            assert tn == n_sh
            n = N_DEV * n_sh
            b_spec = pl.BlockSpec((None, k, tn), lambda i, j: (j, 0, 0))
        else:
            n = n or b.shape[1]
            b_spec = pl.BlockSpec((k, tn), lambda i, j: (0, j + b_off))
        dims = _NN
    elif mode == "nt":
        m = a.shape[0]
        k = k or a.shape[1]
        a_spec = pl.BlockSpec((tm, k), lambda i, j: (i, a_off))
        if b_sharded:
            n = b.shape[1]
            b_spec = pl.BlockSpec((N_DEV, tn, n_sh), lambda i, j: (0, j, 0))
        else:
            n = n or b.shape[0]
            b_spec = pl.BlockSpec((tn, k), lambda i, j: (j + b_off, 0))
        dims = _NT
    else:
        k, m = a.shape
        n = n or b.shape[1]
        a_spec = pl.BlockSpec((k, tm), lambda i, j: (0, i))
        b_spec = pl.BlockSpec((k, tn), lambda i, j: (0, j + b_off))
        dims = _TN
    assert m % tm == 0 and n % tn == 0, (name, m, n, tm, tn)
    n_extra = len(extras)
    tile = pl.BlockSpec((tm, tn), lambda i, j: (i, j))
    split_tile = out_sharded and n == tn
    if out_sharded:
        assert mode == "tn" and n // tn in (1, N_DEV)
        if split_tile:
            out_spec = pl.BlockSpec((N_DEV, tm, n // N_DEV), lambda i, j: (0, i, 0))
        else:
            out_spec = pl.BlockSpec((None, tm, tn), lambda i, j: (j, i, 0))
        out_shape = (N_DEV, m, n // N_DEV)
    else:
        out_spec, out_shape = tile, (m, n)

    def body(a_ref, b_ref, *refs):
        if mode == "nt" and b_sharded:
            acc = _dot(a_ref[...], jnp.concatenate([b_ref[p] for p in range(N_DEV)], axis=1), dims)
        else:
            acc = _dot(a_ref[...], b_ref[...], dims)
        ex = [r[...] for r in refs[:n_extra]]
        outs = epilogue(acc, *ex) if epilogue is not None else (acc,)
        for o_ref, o in zip(refs[n_extra + len(deps):], outs):
            if split_tile:
                w = n // N_DEV
                for p in range(N_DEV):
                    o_ref[p] = o[:, p * w:(p + 1) * w].astype(o_ref.dtype)
            else:
                o_ref[...] = o.astype(o_ref.dtype)

    res = pl.pallas_call(
        body, name=name, grid=(m // tm, n // tn),
        in_specs=[a_spec, b_spec] + [tile] * n_extra + [ANY] * len(deps),
        out_specs=[out_spec] * len(out_dtypes),
        out_shape=[jax.ShapeDtypeStruct(out_shape, dt) for dt in out_dtypes],
        compiler_params=_cparams(("parallel", "parallel")),
    )(a, b, *extras, *deps)
    return res if len(out_dtypes) > 1 else res[0]


def _mm_rows(a, b, *, tm, name, epilogue, rows=(), vecs=(), row_out=(), vec_out=(), b_sharded=False, deps=()):
    m, k = a.shape
    n_rows, n_vecs, n_deps = len(rows), len(vecs), len(deps)
    n_sh = b.shape[-1] if b_sharded else None

    def body(a_ref, b_ref, *refs):
        row_refs, vec_refs = refs[:n_rows], refs[n_rows:n_rows + n_vecs]
        outs = refs[n_rows + n_vecs + n_deps:]
        if b_sharded:
            acc = _dot(a_ref[:, 0:n_sh], b_ref[0], _NT)
            for p in range(1, N_DEV):
                acc = acc + _dot(a_ref[:, p * n_sh:(p + 1) * n_sh], b_ref[p], _NT)
        else:
            acc = _dot(a_ref[...], b_ref[...], _NN)
        row_vals, vec_incs = epilogue(acc, [r[...] for r in row_refs], [v[...] for v in vec_refs])
        for o_ref, val in zip(outs[:len(row_out)], row_vals):
            o_ref[...] = val.astype(o_ref.dtype)

        @pl.when(pl.program_id(0) == 0)
        def _():
            for o_ref in outs[len(row_out):]:
                o_ref[...] = jnp.zeros_like(o_ref)

        for o_ref, inc in zip(outs[len(row_out):], vec_incs):
            o_ref[...] += inc

    tile = pl.BlockSpec((tm, D), lambda i: (i, 0))
    b_spec = pl.BlockSpec(b.shape, lambda i: (0,) * b.ndim)
    return pl.pallas_call(
        body, name=name, grid=(m // tm,),
        in_specs=[pl.BlockSpec((tm, k), lambda i: (i, 0)), b_spec] + [tile] * n_rows
                 + [pl.BlockSpec(v.shape, lambda i: (0, 0)) for v in vecs] + [ANY] * n_deps,
        out_specs=[tile] * len(row_out) + [pl.BlockSpec((1, w), lambda i: (0, 0)) for w in vec_out],
        out_shape=[jax.ShapeDtypeStruct((m, D), dt) for dt in row_out]
                  + [jax.ShapeDtypeStruct((1, w), F32) for w in vec_out],
        compiler_params=_cparams(("arbitrary",)),
    )(a, b, *rows, *vecs, *deps)


ROW_T = 256
ROWS_TM = 512


def _rms_rows(x, g):
    r = lax.rsqrt(jnp.mean(x * x, axis=-1, keepdims=True) + EPS)
    return (x * r) * g


def _rms_bwd_rows(dh, x, dres, g):
    r = lax.rsqrt(jnp.mean(x * x, axis=-1, keepdims=True) + EPS)
    xn = x * r
    dhn = dh * g
    dx = dres + r * (dhn - xn * jnp.mean(dhn * xn, axis=-1, keepdims=True))
    return dx, jnp.sum(dh * xn, axis=0, keepdims=True)


def _final_loss_rows(x, g, tgt):
    r = lax.rsqrt(jnp.mean(x * x, axis=-1, keepdims=True) + EPS)
    xn = x * r
    err = xn * g - tgt
    row_loss = jnp.mean(err * err, axis=-1, keepdims=True)
    loss = 0.5 * jnp.sum(row_loss, axis=0, keepdims=True) * jnp.ones((1, LANES), F32)
    dy = err * (1.0 / D)
    dyn = dy * g
    dx = r * (dyn - xn * jnp.mean(dyn * xn, axis=-1, keepdims=True))
    return dx, jnp.sum(dy * xn, axis=0, keepdims=True), loss


def _sigmoid(z):
    return 1.0 / (1.0 + jnp.exp(-z))


GATE_TR = 512


def _mixer_tail(oa, ob, w_a, w_b, w_out, proj_g, x, g_mlp):
    n_steps, ring = S // GATE_TR, 3

    def body(oa_ref, ob_ref, wa_ref, wb_ref, wo_ref, g_hbm, x_ref, gm_ref, ya_ref, yb_ref, mixed_ref, x2_ref, h2_ref,
             gbuf, sems):
        i = pl.program_id(0)

        def fetch(s):
            return pltpu.make_async_copy(g_hbm.at[pl.ds(pl.multiple_of(s * GATE_TR, GATE_TR), GATE_TR)],
                                         gbuf.at[s % ring], sems.at[s % ring])

        @pl.when(i == 0)
        def _():
            fetch(0).start()
            fetch(1).start()

        @pl.when(i + 2 < n_steps)
        def _():
            fetch(i + 2).start()

        ya = _dot(oa_ref[...], jnp.concatenate([wa_ref[p] for p in range(N_DEV)], axis=1), _NN)
        yb = _dot(ob_ref[...], jnp.concatenate([wb_ref[p] for p in range(N_DEV)], axis=1), _NN)
        ya_ref[...] = ya
        yb_ref[...] = yb
        fetch(i).wait()
        g_ref = gbuf.at[i % ring]
        mixed = (_sigmoid(g_ref[:, :D]) * ya + _sigmoid(g_ref[:, D:]) * yb).astype(BF16)
        mixed_ref[...] = mixed
        x2 = x_ref[...] + _dot(mixed, wo_ref[...], _NN)
        x2_ref[...] = x2
        h2_ref[...] = _rms_rows(x2, gm_ref[...]).astype(BF16)

    def rows(width):
        return pl.BlockSpec((GATE_TR, width), lambda i: (i, 0))

    def whole(t):
        return pl.BlockSpec(t.shape, lambda i: (0,) * t.ndim)

    return pl.pallas_call(
        body, name="mixer_tail", grid=(S // GATE_TR,),
        in_specs=[rows(FOX_W), rows(DIL_OUT), whole(w_a), whole(w_b), whole(w_out), ANY, rows(D), whole(g_mlp)],
        out_specs=[rows(D)] * 5,
        out_shape=[jax.ShapeDtypeStruct((S, D), dt) for dt in (F32, F32, BF16, F32, BF16)],
        scratch_shapes=[pltpu.VMEM((ring, GATE_TR, 2 * D), F32), pltpu.SemaphoreType.DMA((ring,))],
        compiler_params=_cparams(("arbitrary",)),
    )(oa, ob, w_a, w_b, w_out, proj_g, x, g_mlp)


def _out_proj_gate_bwd(dx2, w_out, proj_g, ya, yb):
    def body(dx_ref, w_ref, g_ref, ya_ref, yb_ref, dy_ref, dg_ref):
        dm = _dot(dx_ref[...], w_ref[...], _NT)
        for half, y_ref in enumerate((ya_ref, yb_ref)):
            cols = slice(half * D, (half + 1) * D)
            s = _sigmoid(g_ref[:, cols])
            dy_ref[:, cols] = (dm * s).astype(BF16)
            dg_ref[:, cols] = (dm * y_ref[...] * (s * (1.0 - s))).astype(BF16)

    row = pl.BlockSpec((GATE_TR, D), lambda i: (i, 0))
    wide = pl.BlockSpec((GATE_TR, 2 * D), lambda i: (i, 0))
    return pl.pallas_call(
        body, name="out_proj_gate_bwd", grid=(S // GATE_TR,),
        in_specs=[row, pl.BlockSpec((D, D), lambda i: (0, 0)), wide, row, row],
        out_specs=[wide, pl.BlockSpec((GATE_TR, 2 * D), lambda i: (i, SEG_G // (2 * D)))],
        out_shape=[jax.ShapeDtypeStruct((S, 2 * D), BF16), jax.ShapeDtypeStruct((S, D_PAD), BF16)],
        compiler_params=_cparams(("parallel",)),
    )(dx2, w_out, proj_g, ya, yb)


FOX_TQ = 512
FOX_TQ_FWD = 1024


def _scan_rows(x, reverse):
    n = x.shape[0]
    row = lax.broadcasted_iota(jnp.int32, x.shape, 0)
    k = 1
    while k < n:
        if reverse:
            x = x + jnp.where(row < n - k, pltpu.roll(x, n - k, 0), 0.0)
        else:
            x = x + jnp.where(row >= k, pltpu.roll(x, k, 0), 0.0)
        k *= 2
    return x


def _fox_dscan(dft, dfs, proj_f, b_pad, dproj):
    def body(dft_ref, dfs_ref, f_ref, b_ref, _, dfa_ref, db_ref):
        dfs_pad = jnp.concatenate([dfs_ref[...], jnp.zeros((LANES - FOX_H, S), F32)], axis=0)
        df = dfs_pad.T + dft_ref[0]
        for hp in range(1, dft_ref.shape[0]):
            df = df + pltpu.roll(dft_ref[hp], 2 * hp, 1)
        dlf = _scan_rows(df, reverse=True)
        z = f_ref[...] + b_ref[...]
        lane = lax.broadcasted_iota(jnp.int32, (S, LANES), 1)
        dfa = jnp.where(lane < FOX_H, dlf / (1.0 + jnp.exp(z)), 0.0)
        dfa_ref[:, :LANES] = dfa.astype(BF16)
        dfa_ref[:, LANES:] = jnp.zeros((S, CB - LANES), BF16)
        db_ref[...] = jnp.sum(dfa, axis=0, keepdims=True)

    return pl.pallas_call(
        body, name="fox_dscan", grid=(1,),
        in_specs=[pl.BlockSpec(dft.shape, lambda i: (0, 0, 0)), pl.BlockSpec((FOX_H, S), lambda i: (0, 0)),
                  pl.BlockSpec((S, LANES), lambda i: (0, 0)), pl.BlockSpec((1, LANES), lambda i: (0, 0)), ANY],
        out_specs=[pl.BlockSpec((S, CB), lambda i: (0, SEG_F // CB)), pl.BlockSpec((1, LANES), lambda i: (0, 0))],
        out_shape=[jax.ShapeDtypeStruct((S, D_PAD), BF16), jax.ShapeDtypeStruct((1, LANES), F32)],
        input_output_aliases={4: 0},
        compiler_params=_cparams(("arbitrary",)),
    )(dft, dfs, proj_f, b_pad, dproj)


FOX_NQ = S // FOX_TQ
FOX_HP = FOX_W // LANES
HEADS_PER_LB = LANES // HD
FOX_AUG = 2 * LANES


def _fox_prepare(proj_a, proj_f, b_pad):
    tr = 512

    def body(q_ref, k_ref, f_ref, b_ref, qa_ref, ka_ref, hi_s, mid_s, lo_s):
        i = pl.program_id(0)

        @pl.when(i == 0)
        def _():
            z = f_ref[...] + b_ref[...]
            lf = jnp.minimum(z, 0.0) - jnp.log1p(jnp.exp(-jnp.abs(z)))
            f = _scan_rows(lf, reverse=False)
            hi = f.astype(BF16).astype(F32)
            r1 = f - hi
            mid = r1.astype(BF16).astype(F32)
            hi_s[...] = hi
            mid_s[...] = mid
            lo_s[...] = (r1 - mid).astype(BF16).astype(F32)

        rows = pl.ds(pl.multiple_of(i * tr, tr), tr)
        hi, mid, lo = hi_s[rows, :], mid_s[rows, :], lo_s[rows, :]
        lane = lax.broadcasted_iota(jnp.int32, (tr, HD), 1)
        q_ones = jnp.where((lane >= 3) & (lane < 6), 1.0, 0.0)
        k_ones = jnp.where(lane < 3, 1.0, 0.0)
        for h in range(FOX_H):
            a1, a2, a3 = hi[:, h:h + 1], mid[:, h:h + 1], lo[:, h:h + 1]
            q_extra = jnp.where(lane == 0, a1, jnp.where(lane == 1, a2, jnp.where(lane == 2, a3, q_ones)))
            k_extra = jnp.where(lane == 3, -a1, jnp.where(lane == 4, -a2, jnp.where(lane == 5, -a3, k_ones)))
            base = h * LANES
            qa_ref[:, base:base + HD] = q_ref[:, h * HD:(h + 1) * HD] * jnp.asarray(SCALE, BF16)
            qa_ref[:, base + HD:base + LANES] = q_extra.astype(BF16)
            ka_ref[:, base:base + HD] = k_ref[:, h * HD:(h + 1) * HD]
            ka_ref[:, base + HD:base + LANES] = k_extra.astype(BF16)

    out_blk = pl.BlockSpec((tr, FOX_H * LANES), lambda i: (i, 0))
    return pl.pallas_call(
        body, name="fox_prepare", grid=(S // tr,),
        in_specs=[pl.BlockSpec((tr, FOX_W), lambda i: (i, 0)), pl.BlockSpec((tr, FOX_W), lambda i: (i, 1)),
                  pl.BlockSpec((S, LANES), lambda i: (0, 0)), pl.BlockSpec((1, LANES), lambda i: (0, 0))],
        out_specs=[out_blk, out_blk],
        out_shape=[jax.ShapeDtypeStruct((S, FOX_H * LANES), BF16)] * 2,
        scratch_shapes=[pltpu.VMEM((S, LANES), F32)] * 3,
        compiler_params=_cparams(("arbitrary",)),
    )(proj_a, proj_a, proj_f, b_pad)


def _fox_scores(qa, ka, w, tq):
    s = _dot(qa, ka, _NT)
    row = lax.broadcasted_iota(jnp.int32, (tq, tq), 0)
    col = lax.broadcasted_iota(jnp.int32, (tq, tq), 1)
    diag = jnp.where(col <= row, s[:, w * tq:], NEG_INF)
    return diag if w == 0 else jnp.concatenate([s[:, :w * tq], diag], axis=1)


def _fox_fwd(q_aug, k_aug, proj_a):
    tq = FOX_TQ_FWD

    def body(qa_ref, ka_ref, v_ref, o_ref, lse_ref):
        qi = pl.program_id(1)
        lane = lax.broadcasted_iota(jnp.int32, (tq, LANES), 1)
        for w in range(S // tq):
            @pl.when(qi == w)
            def _(w=w):
                width = (w + 1) * tq
                lse = jnp.zeros((tq, LANES), F32)
                for hh in range(HEADS_PER_LB):
                    aug = slice(hh * LANES, (hh + 1) * LANES)
                    sl = slice(hh * HD, (hh + 1) * HD)
                    s = _fox_scores(qa_ref[:, aug], ka_ref[:width, aug], w, tq)
                    m = jnp.max(s, axis=-1, keepdims=True)
                    p = jnp.exp(s - m)
                    l = jnp.sum(p, axis=-1, keepdims=True)
                    o_ref[:, sl] = _dot(p / l, v_ref[:width, sl], _NN)
                    lse = jnp.where(lane == hh, m + jnp.log(l), lse)
                lse_ref[...] = lse

    return pl.pallas_call(
        body, name="fox_fwd", grid=(FOX_HP, S // tq),
        in_specs=[pl.BlockSpec((tq, FOX_AUG), lambda h, i: (i, h)),
                  pl.BlockSpec((S, FOX_AUG), lambda h, i: (0, h)),
                  pl.BlockSpec((S, LANES), lambda h, i: (0, 2 * FOX_HP + h))],
        out_specs=[pl.BlockSpec((tq, LANES), lambda h, i: (i, h)),
                   pl.BlockSpec((None, tq, LANES), lambda h, i: (h, i, 0))],
        out_shape=[jax.ShapeDtypeStruct((S, FOX_W), F32), jax.ShapeDtypeStruct((FOX_HP, S, LANES), F32)],
        compiler_params=_cparams(("parallel", "parallel")),
    )(q_aug, k_aug, proj_a)


def _fox_bwd(q_aug, k_aug, proj_a, lse, do, dproj, deps=()):
    n_q = FOX_NQ

    def body(qa_ref, ka_ref, v_ref, lse_ref, do_ref, *refs):
        dproj_ref, dft_ref, dfs_ref, dk_acc, dv_acc, dq_buf, kv_buf, dq_sems, kv_sems = refs[1 + len(deps):]
        hp = pl.program_id(0)
        t = pl.program_id(1)
        slot = t % 2
        col = pl.multiple_of(hp * LANES, LANES)

        def dq_copy(step, buf_slot):
            rows = pl.ds(pl.multiple_of(step * FOX_TQ, FOX_TQ), FOX_TQ)
            return pltpu.make_async_copy(dq_buf.at[buf_slot], dproj_ref.at[rows, pl.ds(col, LANES)], dq_sems.at[buf_slot])

        @pl.when(t == 0)
        def _():
            dk_acc[...] = jnp.zeros_like(dk_acc)
            dv_acc[...] = jnp.zeros_like(dv_acc)

        @pl.when((t == 0) & (hp == 0))
        def _():
            dfs_ref[...] = jnp.zeros_like(dfs_ref)

        @pl.when(t >= 2)
        def _():
            dq_copy(t - 2, slot).wait()

        lane = lax.broadcasted_iota(jnp.int32, (FOX_TQ, LANES), 1)
        for w in range(n_q):
            @pl.when(t == w)
            def _(w=w):
                width = (w + 1) * FOX_TQ
                lse_all = lse_ref[...]
                sub = lax.broadcasted_iota(jnp.int32, (FOX_H, width), 0)
                dft = jnp.zeros((FOX_TQ, LANES), F32)
                for hh in range(HEADS_PER_LB):
                    aug = slice(hh * LANES, (hh + 1) * LANES)
                    sl = slice(hh * HD, (hh + 1) * HD)
                    head = hp * HEADS_PER_LB + hh
                    qa = qa_ref[:, aug]
                    ka = ka_ref[:width, aug]
                    do_h = do_ref[:, sl]
                    s = _fox_scores(qa, ka, w, FOX_TQ)
                    p = jnp.exp(s - lse_all[:, hh:hh + 1])
                    dp = _dot(do_h, v_ref[:width, sl], _NT)
                    ds = p * (dp - jnp.sum(dp * p, axis=-1, keepdims=True))
                    dft = jnp.where(lane == hh, jnp.sum(ds, axis=-1, keepdims=True), dft)
                    dfs_ref[:, :width] -= jnp.where(sub == head, jnp.sum(ds, axis=0, keepdims=True), 0.0)
                    dq_buf[slot, :, sl] = (_dot(ds, ka[:, :HD], _NN) * SCALE).astype(BF16)
                    dk_acc[:width, sl] += _dot(ds, qa[:, :HD], _TN)
                    dv_acc[:width, sl] += _dot(p, do_h, _TN)
                dft_ref[...] = dft

        dq_copy(t, slot).start()

        @pl.when(t == n_q - 1)
        def _():
            dq_copy(t - 1, 1 - slot).wait()
            dq_copy(t, slot).wait()
            kv_buf[0] = dk_acc[...].astype(BF16)
            kv_buf[1] = dv_acc[...].astype(BF16)
            copies = [pltpu.make_async_copy(
                kv_buf.at[j], dproj_ref.at[:, pl.ds(pl.multiple_of((j + 1) * FOX_W + hp * LANES, LANES), LANES)],
                kv_sems.at[j]) for j in range(2)]
            for cp in copies:
                cp.start()
            for cp in copies:
                cp.wait()

    qblk = pl.BlockSpec((FOX_TQ, FOX_AUG), lambda h, t: (t, h))
    lane_blk = pl.BlockSpec((None, FOX_TQ, LANES), lambda h, t: (h, t, 0))
    return pl.pallas_call(
        body, name="fox_bwd", grid=(FOX_HP, n_q),
        in_specs=[qblk, pl.BlockSpec((S, FOX_AUG), lambda h, t: (0, h)),
                  pl.BlockSpec((S, LANES), lambda h, t: (0, 2 * FOX_HP + h)),
                  lane_blk, pl.BlockSpec((FOX_TQ, LANES), lambda h, t: (t, h)), ANY] + [ANY] * len(deps),
        out_specs=[ANY, lane_blk, pl.BlockSpec((FOX_H, S), lambda h, t: (0, 0))],
        out_shape=[jax.ShapeDtypeStruct((S, D_PAD), BF16),
                   jax.ShapeDtypeStruct((FOX_HP, S, LANES), F32), jax.ShapeDtypeStruct((FOX_H, S), F32)],
        scratch_shapes=[pltpu.VMEM((S, LANES), F32), pltpu.VMEM((S, LANES), F32),
                        pltpu.VMEM((2, FOX_TQ, LANES), BF16), pltpu.VMEM((2, S, LANES), BF16),
                        pltpu.SemaphoreType.DMA((2,)), pltpu.SemaphoreType.DMA((2,))],
        input_output_aliases={5: 0},
        compiler_params=_cparams(("arbitrary", "arbitrary")),
    )(q_aug, k_aug, proj_a, lse, do, dproj, *deps)


def _rope_tables():
    half = ROPE_DIM // 2
    shape = (DIL_G, S, half)
    g = lax.broadcasted_iota(jnp.int32, shape, 0)
    row = lax.broadcasted_iota(jnp.int32, shape, 1)
    r = jnp.left_shift(1, 2 * g)
    per_class = S // r
    pos = (row % per_class) * r + row // per_class
    inv_freq = jnp.power(jnp.float32(ROPE_THETA), -lax.broadcasted_iota(F32, shape, 2) * 2.0 / ROPE_DIM)
    ang = pos.astype(F32) * inv_freq
    return jnp.concatenate([jnp.cos(ang), jnp.sin(ang)], axis=-1)


def _rope_factors(cs):
    half = ROPE_DIM // 2
    i = lax.broadcasted_iota(jnp.int32, (2 * half, 2 * LANES), 0)
    l = lax.broadcasted_iota(jnp.int32, (2 * half, 2 * LANES), 1)
    hit = ((l < LANES) & (i == l % half)) | ((l >= LANES) & (i - half == l % half))
    spread = jnp.where(hit, 1.0, 0.0).astype(BF16)
    hi = cs.astype(BF16)
    r1 = cs - hi.astype(F32)
    mid = r1.astype(BF16)
    lo = (r1 - mid.astype(F32)).astype(BF16)
    full = (_dot(hi, spread, _NN) + _dot(mid, spread, _NN)) + _dot(lo, spread, _NN)
    cos, sin = full[:, :LANES], full[:, LANES:]
    lane = lax.broadcasted_iota(jnp.int32, cos.shape, 1) % HD
    t_self = jnp.where(lane < ROPE_DIM, cos, 1.0)
    t_up = jnp.where(lane < half, -sin, 0.0)
    t_dn = jnp.where((lane >= half) & (lane < ROPE_DIM), sin, 0.0)
    return t_self, t_up, t_dn


def _residue_pieces(r):
    if r == 1:
        return [(slice(i, i + 512), slice(i, i + 512)) for i in range(0, S, 512)]
    n = S // r
    return [(pl.ds(j, n, stride=r), slice(j * n, (j + 1) * n)) for j in range(r)]


def _rope_apply(x, a, u, d, backward):
    half = ROPE_DIM // 2
    if backward:
        return x * a + pltpu.roll(x * u, half, 1) + pltpu.roll(x * d, LANES - half, 1)
    return x * a + pltpu.roll(x, LANES - half, 1) * u + pltpu.roll(x, half, 1) * d


def _rope_cache_factors(cs_ref, fac, first_step_of_group):
    @pl.when(first_step_of_group)
    def _():
        for lo in range(0, S, 512):
            for j, f in enumerate(_rope_factors(cs_ref[lo:lo + 512, :])):
                fac[j, lo:lo + 512, :] = f


LB_PER_CB = CB // LANES
ROPE_NB = 3 * DIL_G * LB_PER_CB


def _rope_step(s):
    per_group = 3 * LB_PER_CB
    return s // per_group, (s % per_group) // LB_PER_CB, s % LB_PER_CB


def _rope_split(proj_b, tables):
    def body(x_ref, cs_ref, o_ref, fac):
        g, t, hf = _rope_step(pl.program_id(0))
        _rope_cache_factors(cs_ref, fac, (t == 0) & (hf == 0))
        for gs in range(DIL_G):
            @pl.when(g == gs)
            def _(gs=gs):
                for tok, sub in _residue_pieces(DIL_R[gs]):
                    x = x_ref[tok, :]
                    y = _rope_apply(x, fac[0, sub, :], fac[1, sub, :], fac[2, sub, :], False)
                    o_ref[sub, :] = jnp.where(t < 2, y, x).astype(BF16)

    def in_index(s):
        g, t, hf = _rope_step(s)
        return 0, (t * DIL_G + g) * LB_PER_CB + hf

    def out_index(s):
        g, t, hf = _rope_step(s)
        return t * DIL_G + g, 0, hf

    tab = pl.BlockSpec((None, S, 2 * (ROPE_DIM // 2)), lambda s: (_rope_step(s)[0], 0, 0))
    return pl.pallas_call(
        body, name="rope_split", grid=(ROPE_NB,),
        in_specs=[pl.BlockSpec((S, LANES), in_index), tab],
        out_specs=pl.BlockSpec((None, S, LANES), out_index),
        out_shape=jax.ShapeDtypeStruct((3 * DIL_G, S, CB), BF16),
        scratch_shapes=[pltpu.VMEM((3, S, LANES), F32)],
        compiler_params=_cparams(("arbitrary",)),
    )(proj_b, tables)


def _rope_merge_bwd(dq3, dk3, dv3, tables, dproj):
    def body(dq_ref, dk_ref, dv_ref, cs_ref, _, o_ref, tmp, fac):
        s = pl.program_id(0)
        g, t, hf = _rope_step(s)
        _rope_cache_factors(cs_ref, fac, (t == 0) & (hf == 0) & (s < ROPE_NB))
        for gs in range(DIL_G):
            @pl.when((g == gs) & (s < ROPE_NB))
            def _(gs=gs):
                for tok, sub in _residue_pieces(DIL_R[gs]):
                    x = jnp.where(t == 0, dq_ref[sub, :], jnp.where(t == 1, dk_ref[sub, :], dv_ref[sub, :]))
                    y = _rope_apply(x, fac[0, sub, :], fac[1, sub, :], fac[2, sub, :], True)
                    tmp[tok, :] = jnp.where(t < 2, y, x)
                o_ref[...] = tmp[...].astype(BF16)

        @pl.when(s >= ROPE_NB)
        def _():
            o_ref[...] = jnp.zeros_like(o_ref)

    def src_spec(own):
        def index(s):
            g, t, hf = _rope_step(jnp.minimum(s, ROPE_NB - 1))
            return g, 0, jnp.where(t == own, hf, jnp.where(t > own, LB_PER_CB - 1, 0))
        return pl.BlockSpec((None, S, LANES), index)

    def out_index(s):
        g, t, hf = _rope_step(s)
        col = SEG_B // LANES + (t * DIL_G + g) * LB_PER_CB + hf
        return 0, jnp.where(s < ROPE_NB, col, SEG_Z // LANES + s - ROPE_NB)

    tab = pl.BlockSpec((None, S, 2 * (ROPE_DIM // 2)), lambda s: (_rope_step(jnp.minimum(s, ROPE_NB - 1))[0], 0, 0))
    return pl.pallas_call(
        body, name="rope_merge_bwd", grid=(ROPE_NB + LB_PER_CB,),
        in_specs=[src_spec(0), src_spec(1), src_spec(2), tab, ANY],
        out_specs=pl.BlockSpec((S, LANES), out_index),
        out_shape=jax.ShapeDtypeStruct((S, D_PAD), BF16),
        scratch_shapes=[pltpu.VMEM((S, LANES), F32), pltpu.VMEM((3, S, LANES), F32)],
        input_output_aliases={4: 0},
        compiler_params=_cparams(("arbitrary",)),
    )(dq3, dk3, dv3, tables, dproj)


DIL_NB = S // DIL_BLK


_BQK = (((2,), (2,)), ((0,), (0,)))
_BQD = (((2,), (1,)), ((0,), (0,)))
_BKD = (((1,), (1,)), ((0,), (0,)))


def _dil_mask(g):
    shape = (DIL_NB, DIL_BLK, 2 * DIL_BLK)
    blocks_per_seq = jnp.right_shift(DIL_NB, 2 * g)
    has_prev = jnp.bitwise_and(lax.broadcasted_iota(jnp.int32, shape, 0), blocks_per_seq - 1) != 0
    a = lax.broadcasted_iota(jnp.int32, shape, 1)
    kk = lax.broadcasted_iota(jnp.int32, shape, 2)
    diff = DIL_BLK + a - kk
    return (diff >= 0) & (diff <= DIL_BLK) & ((kk >= DIL_BLK) | has_prev)


def _blocks(t):
    return t.reshape(DIL_NB, DIL_BLK, t.shape[-1])


def _with_prev(t):
    prev = jnp.concatenate([jnp.zeros((DIL_BLK, t.shape[-1]), t.dtype), t[:-DIL_BLK]], axis=0)
    return jnp.concatenate([_blocks(prev), _blocks(t)], axis=1)


def _fold_prev(t2):
    n = t2.shape[-1]
    to_prev = t2[:, :DIL_BLK].reshape(S, n)
    own = t2[:, DIL_BLK:].reshape(S, n)
    return own + jnp.concatenate([to_prev[DIL_BLK:], jnp.zeros((DIL_BLK, n), t2.dtype)], axis=0)


def _dil_group_spec(t, width=DIL_OUT):
    return pl.BlockSpec((None, S, width), lambda g: (t * DIL_G + g, 0, 0))


def _dil_fwd(qkv3):
    def body(q_ref, k_ref, v_ref, o_ref, lse_ref):
        ok = _dil_mask(pl.program_id(0))
        lane = lax.broadcasted_iota(jnp.int32, (S, LANES), 1)
        lse = jnp.zeros((S, LANES), F32)
        for h in range(DIL_HG):
            sl = slice(h * HD, (h + 1) * HD)
            s = jnp.where(ok, _dot(_blocks(q_ref[:, sl]), _with_prev(k_ref[:, sl]), _BQK) * SCALE, NEG_INF)
            m = jnp.max(s, axis=-1, keepdims=True)
            p = jnp.exp(s - m)
            l = jnp.sum(p, axis=-1, keepdims=True)
            o_ref[:, sl] = _dot(p / l, _with_prev(v_ref[:, sl]), _BQD).reshape(S, HD)
            lse = jnp.where(lane == h, (m + jnp.log(l)).reshape(S, 1), lse)
        lse_ref[...] = lse

    return pl.pallas_call(
        body, name="dil_fwd", grid=(DIL_G,),
        in_specs=[_dil_group_spec(0), _dil_group_spec(1), _dil_group_spec(2)],
        out_specs=[_dil_group_spec(0), _dil_group_spec(0, LANES)],
        out_shape=[jax.ShapeDtypeStruct((DIL_G, S, DIL_OUT), F32), jax.ShapeDtypeStruct((DIL_G, S, LANES), F32)],
        compiler_params=_cparams(("parallel",)),
    )(qkv3, qkv3, qkv3)


def _dil_bwd(qkv3, lse3, do3, c3):
    def body(q_ref, k_ref, v_ref, lse_ref, do_ref, c_ref, dq_ref, dk_ref, dv_ref):
        ok = _dil_mask(pl.program_id(0))
        lse = lse_ref[...]
        c = c_ref[...]
        for h in range(DIL_HG):
            sl = slice(h * HD, (h + 1) * HD)
            q = _blocks(q_ref[:, sl])
            k2 = _with_prev(k_ref[:, sl])
            do_h = _blocks(do_ref[:, sl])
            s = jnp.where(ok, _dot(q, k2, _BQK) * SCALE, NEG_INF)
            p = jnp.exp(s - _blocks(lse[:, h:h + 1]))
            dp = _dot(do_h, _with_prev(v_ref[:, sl]), _BQK)
            ds = p * (dp - _blocks(c[:, h:h + 1]))
            dsb = (ds * SCALE).astype(BF16)
            dq_ref[:, sl] = _dot(dsb, k2, _BQD).reshape(S, HD)
            dk_ref[:, sl] = _fold_prev(_dot(dsb, q, _BKD))
            dv_ref[:, sl] = _fold_prev(_dot(p, do_h, _BKD))

    return pl.pallas_call(
        body, name="dil_bwd", grid=(DIL_G,),
        in_specs=[_dil_group_spec(0), _dil_group_spec(1), _dil_group_spec(2), _dil_group_spec(0, LANES),
                  _dil_group_spec(0), _dil_group_spec(0, LANES)],
        out_specs=[_dil_group_spec(0)] * 3,
        out_shape=[jax.ShapeDtypeStruct((DIL_G, S, DIL_OUT), F32)] * 3,
        compiler_params=_cparams(("parallel",)),
    )(qkv3, qkv3, qkv3, lse3, do3, c3)


COMB_T = 256


def _dil_combine(o3, lse3):
    heads_per_lb = LANES // HD

    def body(o_ref, lse_ref, ob_ref, al_ref, *scratch):
        o_tok = [scratch[LB_PER_CB * gg:LB_PER_CB * (gg + 1)] for gg in range(DIL_G)]
        lse_tok = scratch[LB_PER_CB * DIL_G:]
        g = pl.program_id(0)
        for gs in range(DIL_G):
            @pl.when(g == gs)
            def _(gs=gs):
                for tok, sub in _residue_pieces(DIL_R[gs]):
                    for hf in range(LB_PER_CB):
                        o_tok[gs][hf][tok, :] = o_ref[sub, hf * LANES:(hf + 1) * LANES]
                    lse_tok[gs][tok, :] = lse_ref[sub, :]

        @pl.when(g == DIL_G - 1)
        def _():
            def chunk(i, carry):
                rows = pl.ds(pl.multiple_of(i * COMB_T, COMB_T), COMB_T)
                lse = [lse_tok[gg][rows, :] for gg in range(DIL_G)]
                m = jnp.maximum(jnp.maximum(lse[0], lse[1]), lse[2])
                e = [jnp.exp(lse[gg] - m) for gg in range(DIL_G)]
                den = (e[0] + e[1]) + e[2]
                al = [e[gg] / den for gg in range(DIL_G)]
                for gg in range(DIL_G):
                    al_ref[gg, rows, :] = al[gg]
                for h in range(DIL_HG):
                    hf, sl = h // heads_per_lb, slice((h % heads_per_lb) * HD, (h % heads_per_lb + 1) * HD)
                    acc = al[0][:, h:h + 1] * o_tok[0][hf][rows, sl]
                    for gg in range(1, DIL_G):
                        acc = acc + al[gg][:, h:h + 1] * o_tok[gg][hf][rows, sl]
                    ob_ref[rows, h * HD:(h + 1) * HD] = acc
                return carry

            lax.fori_loop(0, S // COMB_T, chunk, 0)

    return pl.pallas_call(
        body, name="dil_combine", grid=(DIL_G,),
        in_specs=[pl.BlockSpec((None, S, DIL_OUT), lambda g: (g, 0, 0)),
                  pl.BlockSpec((None, S, LANES), lambda g: (g, 0, 0))],
        out_specs=[pl.BlockSpec((S, DIL_OUT), lambda g: (0, 0)),
                   pl.BlockSpec((DIL_G, S, LANES), lambda g: (0, 0, 0))],
        out_shape=[jax.ShapeDtypeStruct((S, DIL_OUT), F32), jax.ShapeDtypeStruct((DIL_G, S, LANES), F32)],
        scratch_shapes=[pltpu.VMEM((S, LANES), F32)] * (DIL_G * (LB_PER_CB + 1)),
        compiler_params=_cparams(("arbitrary",)),
    )(o3, lse3)


def _dil_combine_bwd(dob, ob, alpha, deps=()):
    heads_per_lb = LANES // HD

    def body(dob_ref, ob_ref, al_ref, *refs):
        do_ref, c_ref = refs[len(deps):]
        g = pl.program_id(0)
        hf = pl.program_id(1)
        for gs in range(DIL_G):
            @pl.when(g == gs)
            def _(gs=gs):
                for tok, sub in _residue_pieces(DIL_R[gs]):
                    dob = dob_ref[tok, :]
                    prod = dob * ob_ref[tok, :]
                    al = al_ref[tok, :]
                    lane = lax.broadcasted_iota(jnp.int32, al.shape, 1)
                    c = jnp.where(hf == 0, 0.0, c_ref[sub, :])
                    for hh in range(heads_per_lb):
                        sl = slice(hh * HD, (hh + 1) * HD)
                        head = hf * heads_per_lb + hh
                        a = jnp.sum(jnp.where(lane == head, al, 0.0), axis=-1, keepdims=True)
                        do_ref[sub, sl] = (a * dob[:, sl]).astype(BF16)
                        c = jnp.where(lane == head, a * jnp.sum(prod[:, sl], axis=-1, keepdims=True), c)
                    c_ref[sub, :] = c

    half = pl.BlockSpec((S, LANES), lambda g, hf: (0, hf))
    return pl.pallas_call(
        body, name="dil_combine_bwd", grid=(DIL_G, LB_PER_CB),
        in_specs=[half, half, pl.BlockSpec((None, S, LANES), lambda g, hf: (g, 0, 0))] + [ANY] * len(deps),
        out_specs=[pl.BlockSpec((None, S, LANES), lambda g, hf: (g, 0, hf)),
                   pl.BlockSpec((None, S, LANES), lambda g, hf: (g, 0, 0))],
        out_shape=[jax.ShapeDtypeStruct((DIL_G, S, DIL_OUT), BF16), jax.ShapeDtypeStruct((DIL_G, S, LANES), F32)],
        compiler_params=_cparams(("parallel", "arbitrary")),
    )(dob, ob, alpha, *deps)


DOWN_BWD_TN = 512
DOWN_BWD_RING = 3


def _mlp_down_bwd(dx3, w_down, u):
    tn, n_steps = DOWN_BWD_TN, DFF // DOWN_BWD_TN

    def body(a_ref, b_ref, u_hbm, o_ref, ubuf, sems):
        j = pl.program_id(0)

        def fetch(s):
            slot = s % DOWN_BWD_RING
            return pltpu.make_async_copy(u_hbm.at[:, pl.ds(pl.multiple_of(s * tn, tn), tn)], ubuf.at[slot],
                                         sems.at[slot])

        @pl.when(j == 0)
        def _():
            fetch(0).start()
            fetch(1).start()

        @pl.when(j + 2 < n_steps)
        def _():
            fetch(j + 2).start()

        acc = _dot(a_ref[...], b_ref[...], _NT)
        fetch(j).wait()
        o_ref[...] = (acc * (2.0 * jnp.maximum(ubuf[j % DOWN_BWD_RING], 0.0))).astype(BF16)

    return pl.pallas_call(
        body, name="mlp_down_bwd", grid=(n_steps,),
        in_specs=[pl.BlockSpec((S, D), lambda j: (0, 0)), pl.BlockSpec((tn, D), lambda j: (j, 0)), ANY],
        out_specs=pl.BlockSpec((S, tn), lambda j: (0, j)),
        out_shape=jax.ShapeDtypeStruct((S, DFF), BF16),
        scratch_shapes=[pltpu.VMEM((DOWN_BWD_RING, S, tn), F32), pltpu.SemaphoreType.DMA((DOWN_BWD_RING,))],
        compiler_params=_cparams(("arbitrary",)),
    )(dx3, w_down, u)


def _local_step(x, tgt, h1, g_attn, g_mlp, g_final, b_pad, w_in_t, hooks):
    tables = _rope_tables()

    first = hooks.first_deps()
    proj_a = _mm(h1, w_in_t, mode="nt", tm=S, tn=512, n=3 * FOX_W, out_dtypes=[BF16], name="proj_a", deps=first)
    proj_b = _mm(h1, w_in_t, mode="nt", tm=S, tn=DIL_W, n=3 * DIL_W, b_off=SEG_B // DIL_W, out_dtypes=[F32], name="proj_b",
                deps=first)
    proj_g = _mm(h1, w_in_t, mode="nt", tm=S, tn=D, n=2 * D, b_off=SEG_G // D, out_dtypes=[F32], name="proj_g",
                deps=first)
    proj_f = _mm(h1, w_in_t, mode="nt", tm=S, tn=LANES, n=LANES, b_off=SEG_F // LANES, out_dtypes=[F32], name="proj_f",
                deps=first)

    q_aug, k_aug = _fox_prepare(proj_a, proj_f, b_pad)
    oa, lse_a = _fox_fwd(q_aug, k_aug, proj_a)

    qkv3 = _rope_split(proj_b, tables)
    o3, lse3 = _dil_fwd(qkv3)
    ob, alpha = _dil_combine(o3, lse3)

    w_a, w_b, w_out = hooks.mixer_weights(ob)
    ya, yb, mixed, x2, h2 = _mixer_tail(oa, ob, w_a, w_b, w_out, proj_g, x, g_mlp)
    w_up_sh, w_down = hooks.mlp_weights(h2)

    def up_epilogue(acc):
        r = jnp.maximum(acc, 0.0)
        return acc, r * r

    u, act = _mm(h2, w_up_sh, mode="nn", tm=S, tn=DFF // N_DEV, b_sharded=True, out_dtypes=[F32, BF16],
                 name="mlp_up", epilogue=up_epilogue)

    def loss_epilogue(acc, rows, vecs):
        dx3, dg, loss = _final_loss_rows(rows[0] + acc, vecs[0], rows[1])
        return (dx3,), (dg, loss)

    dx3, dg_final, loss = _mm_rows(act, w_down, tm=ROWS_TM, name="mlp_down_loss", epilogue=loss_epilogue,
                                   rows=(x2, tgt), vecs=(g_final,), row_out=(F32,), vec_out=(D, LANES))

    def rms_bwd_epilogue(acc, rows, vecs):
        dx, dg = _rms_bwd_rows(acc, rows[0], rows[1], vecs[0])
        return (dx,), (dg,)

    du = _mlp_down_bwd(dx3, w_down, u)
    dw_down = _mm(act, dx3, mode="tn", tm=512, tn=D, out_dtypes=[F32], name="dw_down")
    dw_up_sh = _mm(h2, du, mode="tn", tm=D, tn=DFF // N_DEV, out_sharded=True, out_dtypes=[F32], name="dw_up")
    dx2, dg_mlp = _mm_rows(du, w_up_sh, b_sharded=True, tm=ROWS_TM, name="mlp_up_bwd", epilogue=rms_bwd_epilogue,
                           rows=(x2, dx3), vecs=(g_mlp,), row_out=(F32,), vec_out=(D,),
                           deps=hooks.mlp_grads(dw_up_sh, dw_down))

    dw_out = _mm(mixed, dx2, mode="tn", tm=D, tn=D, out_dtypes=[F32], name="dw_out")

    dyab, dproj = _out_proj_gate_bwd(dx2, w_out, proj_g, ya, yb)
    doa = _mm(dyab, w_a, mode="nt", tm=S, tn=FOX_W, k=D, a_off=0, b_sharded=True, out_dtypes=[BF16],
              name="branch_a_bwd")
    dw_a = _mm(oa, dyab, mode="tn", tm=FOX_W, tn=D, n=D, b_off=0, out_sharded=True, out_dtypes=[F32],
               name="dw_branch_a")
    dob = _mm(dyab, w_b, mode="nt", tm=S, tn=CB, k=D, a_off=1, b_sharded=True, out_dtypes=[F32],
              name="branch_b_bwd")
    dw_b = _mm(ob, dyab, mode="tn", tm=DIL_OUT, tn=D, n=D, b_off=1, out_sharded=True, out_dtypes=[F32],
               name="dw_branch_b")

    dproj, dft, dfs = _fox_bwd(q_aug, k_aug, proj_a, lse_a, doa, dproj, deps=hooks.mixer_grads(dw_a, dw_b, dw_out))
    dproj, db = _fox_dscan(dft, dfs, proj_f, b_pad, dproj)

    do3, c3 = _dil_combine_bwd(dob, ob, alpha, deps=hooks.mid_backward(db))
    dq3, dk3, dv3 = _dil_bwd(qkv3, lse3, do3, c3)
    dproj = _rope_merge_bwd(dq3, dk3, dv3, tables, dproj)

    dw_in_t = _mm(dproj, h1, mode="tn", tm=DW_IN_TM, tn=D, out_dtypes=[F32], name="dw_in")
    dx, dg_attn = _mm_rows(dproj, w_in_t, tm=ROWS_TM // 2, name="proj_bwd", epilogue=rms_bwd_epilogue,
                           rows=(x, dx2), vecs=(g_attn,), row_out=(F32,), vec_out=(D,),
                           deps=hooks.w_in_grad(dw_in_t))

    return loss, dx, (dg_attn, db, dg_mlp, dg_final)


SHARD_SHAPES = ((D_IN // N_DEV, D), (FOX_W, D // N_DEV), (DIL_OUT, D // N_DEV), (D // N_DEV, D),
                (D, DFF // N_DEV), (DFF // N_DEV, D))
W_NAMES = ("w_in", "w_a", "w_b", "w_out", "w_up", "w_down")
AG_COPIES = 7
HBM = pl.BlockSpec(memory_space=pltpu.HBM)
SEM = pl.BlockSpec(memory_space=pltpu.SEMAPHORE)


def _place():
    return lax.axis_index("x"), lax.axis_index("y"), lax.axis_index("c")


W_SLAB = 752
W_WIN = 272
PAD_BLK = 256


def _pad_runs():
    runs = sorted((pad, pad + b - a, p, a) for p in range(N_DEV) for a, b, pad in _shard_pieces(p))
    blocks = []
    for k in range(D_PAD // PAD_BLK):
        lo, hi = k * PAD_BLK, (k + 1) * PAD_BLK
        blocks.append([(max(lo, r0) - lo, min(hi, r1) - lo, p, a + max(lo, r0) - r0)
                       for r0, r1, p, a in runs if max(lo, r0) < min(hi, r1)])
    return blocks


def _gather_w_in(shard, x_in, g_attn, late_shards):
    blocks = _pad_runs()

    norm_t = 512
    n_chunks = S // norm_t

    n_late = len(late_shards)
    late_shapes = [t.shape for t in late_shards]

    def body(x_ref, xin_ref, g_ref, *refs):
        late_in, (out_ref, h1_ref), late_out = refs[:n_late], refs[n_late:n_late + 2], refs[n_late + 2:2 * n_late + 2]
        scratch = refs[2 * n_late + 2:]
        stage, land, xbuf = scratch[:3]
        late_stage, late_cast = scratch[3:3 + n_late], scratch[3 + n_late:3 + 2 * n_late]
        send_sems, recv_sems, load_sem, x_sems, late_sems, store_sems = scratch[3 + 2 * n_late:]
        x, y, c = _place()
        me, sibling = (x, y, c), (x, y, 1 - c)
        chips = [(1 - x, y), (x, 1 - y), (1 - x, 1 - y)]

        def slot(px, py, pc):
            return land.at[4 * px + 2 * py + pc]

        def copy(k, block, to):
            return pltpu.make_async_remote_copy(
                src_ref=slot(*block), dst_ref=slot(*block), send_sem=send_sems.at[k], recv_sem=recv_sems.at[k],
                device_id=to, device_id_type=MESH)

        load = pltpu.make_async_copy(x_ref, stage, load_sem)
        load.start()
        load.wait()
        land[4 * x + 2 * y + c] = stage[...].astype(BF16)
        sent = [copy(0, me, sibling)] + [copy(1 + j, me, (*chip, c)) for j, chip in enumerate(chips)]
        for cp in sent:
            cp.start()

        def x_load(i):
            return pltpu.make_async_copy(xin_ref.at[pl.ds(i * norm_t, norm_t)], xbuf.at[i % 2], x_sems.at[i % 2])

        late_loads = [pltpu.make_async_copy(late_in[i], late_stage[i], late_sems.at[i]) for i in range(n_late)]
        for ld in late_loads:
            ld.start()
        x_load(0).start()
        for i in range(n_chunks):
            if i + 1 < n_chunks:
                x_load(i + 1).start()
            x_load(i).wait()
            h1_ref[i * norm_t:(i + 1) * norm_t, :] = _rms_rows(xbuf[i % 2], g_ref[...]).astype(BF16)
        stores = []
        for i in range(n_late):
            late_loads[i].wait()
            late_cast[i][...] = late_stage[i][...].astype(BF16)
            stores.append(pltpu.make_async_copy(late_cast[i], late_out[i].at[4 * x + 2 * y + c], store_sems.at[i]))
            stores[-1].start()

        for j, chip in enumerate(chips):
            copy(1 + j, (*chip, c), me).wait_recv()
            sent.append(copy(4 + j, (*chip, c), sibling))
            sent[-1].start()
        copy(0, sibling, me).wait_recv()
        for j, chip in enumerate(chips):
            copy(4 + j, (*chip, 1 - c), me).wait_recv()
        for cp in sent:
            cp.wait_send()
        for st in stores:
            st.wait()

        row = lax.broadcasted_iota(jnp.int32, (PAD_BLK, D), 0)
        for k, runs in enumerate(blocks):
            out = jnp.zeros((PAD_BLK, D), F32)
            for o_lo, o_hi, p, a in runs:
                start = min(a // 16 * 16, W_SLAB - W_WIN)
                win = land[p, start:start + W_WIN, :].astype(F32)
                moved = pltpu.roll(win, (o_lo - (a - start)) % W_WIN, 0)[:PAD_BLK]
                out = jnp.where((row >= o_lo) & (row < o_hi), moved, out)
            out_ref[k * PAD_BLK:(k + 1) * PAD_BLK, :] = out.astype(BF16)

    return pl.pallas_call(
        body, name="all_gather_w_in",
        out_shape=[jax.ShapeDtypeStruct((D_PAD, D), BF16), jax.ShapeDtypeStruct((S, D), BF16)]
                  + [jax.ShapeDtypeStruct((N_DEV,) + sh, BF16) for sh in late_shapes],
        in_specs=[ANY, ANY, pl.BlockSpec(memory_space=pltpu.VMEM)] + [ANY] * n_late,
        out_specs=[pl.BlockSpec(memory_space=pltpu.VMEM)] * 2 + [ANY] * n_late,
        scratch_shapes=[pltpu.VMEM((W_SLAB, D), F32), pltpu.VMEM((N_DEV, W_SLAB, D), BF16),
                        pltpu.VMEM((2, norm_t, D), F32)]
                       + [pltpu.VMEM(sh, F32) for sh in late_shapes] + [pltpu.VMEM(sh, BF16) for sh in late_shapes]
                       + [pltpu.SemaphoreType.DMA((AG_COPIES,)), pltpu.SemaphoreType.DMA((AG_COPIES,)),
                          pltpu.SemaphoreType.DMA(()), pltpu.SemaphoreType.DMA((2,)),
                          pltpu.SemaphoreType.DMA((n_late,)), pltpu.SemaphoreType.DMA((n_late,))],
        compiler_params=_cparams(None),
    )(shard, x_in, g_attn, *late_shards)


PEER_COPIES = N_DEV - 1
SIBLING_COPIES = 4
CHIP_COPIES = 3


def _peer_copies(_, land_refs, send_sems, recv_sems):
    x, y, c = _place()
    mine = 4 * x + 2 * y + c
    copies = []
    for i, ref in enumerate(land_refs):
        for k in range(1, N_DEV):
            peer = (x ^ (k >> 2), y ^ ((k >> 1) & 1), c ^ (k & 1))
            n = i * PEER_COPIES + k - 1
            copies.append(pltpu.make_async_remote_copy(
                src_ref=ref.at[mine], dst_ref=ref.at[mine], send_sem=send_sems.at[n], recv_sem=recv_sems.at[n],
                device_id=peer, device_id_type=MESH))
    return copies


def _sibling_copies(g_refs, r_refs, send_sems, recv_sems):
    x, y, c = _place()
    return [pltpu.make_async_remote_copy(
        src_ref=g_refs[i].at[2 * k + (1 - c)], dst_ref=r_refs[i].at[k],
        send_sem=send_sems.at[SIBLING_COPIES * i + k], recv_sem=recv_sems.at[SIBLING_COPIES * i + k],
        device_id=(x, y, 1 - c), device_id_type=MESH) for i in range(len(g_refs)) for k in range(SIBLING_COPIES)]


def _sibling_half_copies(g_refs, r_refs, send_sems, recv_sems):
    x, y, c = _place()
    return [pltpu.make_async_remote_copy(
        src_ref=g_refs[i].at[k], dst_ref=r_refs[i].at[k],
        send_sem=send_sems.at[SIBLING_COPIES * i + k], recv_sem=recv_sems.at[SIBLING_COPIES * i + k],
        device_id=(x, y, 1 - c), device_id_type=MESH) for i in range(len(g_refs)) for k in range(SIBLING_COPIES)]


def _chip_copies(p_refs, r_refs, send_sems, recv_sems):
    x, y, c = _place()
    chips = [(1 - x, y), (x, 1 - y), (1 - x, 1 - y)]
    return [pltpu.make_async_remote_copy(
        src_ref=p_refs[i].at[2 * cx + cy], dst_ref=r_refs[i].at[j],
        send_sem=send_sems.at[CHIP_COPIES * i + j], recv_sem=recv_sems.at[CHIP_COPIES * i + j],
        device_id=(cx, cy, c), device_id_type=MESH) for i in range(len(p_refs)) for j, (cx, cy) in enumerate(chips)]


def _in_hbm(a):
    return pltpu.with_memory_space_constraint(a, pltpu.HBM)


def _exchange_start(name, copies_fn, n_copies, srcs, lands, after=()):
    n_s, n_l, n_a = len(srcs), len(lands), len(after)

    def body(*refs):
        outs = refs[n_s + n_l + n_a:]
        for cp in copies_fn(refs[:n_s], refs[n_s:n_s + n_l], outs[0], outs[1]):
            cp.start()
        outs[-1][...] = jnp.zeros_like(outs[-1])

    res = pl.pallas_call(
        body, name=name,
        out_shape=[pltpu.SemaphoreType.DMA((n_copies,)), pltpu.SemaphoreType.DMA((n_copies,))]
                  + [pltpu.HBM(t.shape, t.dtype) for t in (*srcs, *lands)] + [jax.ShapeDtypeStruct((8, LANES), F32)],
        in_specs=[HBM] * (n_s + n_l) + [ANY] * n_a,
        out_specs=[SEM, SEM] + [HBM] * (n_s + n_l) + [pl.BlockSpec(memory_space=pltpu.VMEM)],
        input_output_aliases={i: 2 + i for i in range(n_s + n_l)},
        compiler_params=pltpu.CompilerParams(has_side_effects=pltpu.SideEffectType.DATAFLOW_SIDE_EFFECTING),
    )(*[_in_hbm(t) for t in (*srcs, *lands)], *after)
    return res[0], res[1], list(res[2:2 + n_s]), list(res[2 + n_s:2 + n_s + n_l]), res[-1]


def _exchange_wait(name, copies_fn, started, after):
    send_sems, recv_sems, srcs, lands, _ = started
    n_s, n_l = len(srcs), len(lands)

    def body(*refs):
        for cp in copies_fn(refs[:n_s], refs[n_s:n_s + n_l], refs[n_s + n_l], refs[n_s + n_l + 1]):
            cp.wait_send()
            cp.wait_recv()

    res = pl.pallas_call(
        body, name=name,
        out_shape=[pltpu.HBM(t.shape, t.dtype) for t in (*srcs, *lands)],
        in_specs=[HBM] * (n_s + n_l) + [SEM, SEM, ANY],
        out_specs=[HBM] * (n_s + n_l),
        input_output_aliases={i: i for i in range(n_s + n_l)},
        compiler_params=pltpu.CompilerParams(has_side_effects=pltpu.SideEffectType.DATAFLOW_SIDE_EFFECTING),
    )(*srcs, *lands, send_sems, recv_sems, after)
    return list(res[:n_s]), list(res[n_s:])


def _shard_block(shape):
    rows, cols = shape
    if rows * cols <= 256 * 1024:
        return rows, cols
    if rows % 256:
        return rows, 512
    return max(256, 256 * 1024 // cols), cols


def _chip_partial(ids, g_stack, recv1, name):
    shape = g_stack.shape[1:]
    br, bc = shape
    whole = g_stack.shape[0] == N_DEV

    def body(ids_ref, g_ref, r_ref, pb_ref, own_ref):
        s = g_ref[...] + r_ref[...]
        pb_ref[...] = s.astype(BF16)

        @pl.when(pl.program_id(2) == ids_ref[1])
        def _():
            own_ref[...] = s

    grid_spec = pltpu.PrefetchScalarGridSpec(
        num_scalar_prefetch=1, grid=(shape[0] // br, shape[1] // bc, 4),
        in_specs=[pl.BlockSpec((None, br, bc), lambda r, q, k, ids: (2 * k + ids[0] if whole else k, r, q)),
                  pl.BlockSpec((None, br, bc), lambda r, q, k, ids: (k, r, q))],
        out_specs=[pl.BlockSpec((None, br, bc), lambda r, q, k, ids: (k, r, q)),
                   pl.BlockSpec((br, bc), lambda r, q, k, ids: (r, q))])
    return pl.pallas_call(
        body, name=name, grid_spec=grid_spec,
        out_shape=[jax.ShapeDtypeStruct((4,) + shape, BF16), jax.ShapeDtypeStruct(shape, F32)],
        compiler_params=_cparams(("parallel", "parallel", "arbitrary")),
    )(ids, g_stack, recv1)


def _adamw(w, g, m, v):
    m = ADAM_B1 * m + (1.0 - ADAM_B1) * g
    v = ADAM_B2 * v + (1.0 - ADAM_B2) * (g * g)
    m_hat = m / (1.0 - ADAM_B1 ** ADAM_STEP)
    v_hat = v / (1.0 - ADAM_B2 ** ADAM_STEP)
    delta = -ADAM_LR * (m_hat / (jnp.sqrt(v_hat) + ADAM_EPS) + ADAM_WD * w)
    return delta, m, v


def _reduce_adamw(own, recv2, w, m, v, name, deps=()):
    shape = own.shape
    br, bc = _shard_block(shape)

    def body(own_ref, r_ref, w_ref, m_ref, v_ref, *refs):
        g_ref, d_ref, nm_ref, nv_ref = refs[len(deps):]
        g = own_ref[...]
        for j in range(3):
            g = g + r_ref[j].astype(F32)
        delta, nm, nv = _adamw(w_ref[...], g, m_ref[...], v_ref[...])
        g_ref[...] = g
        d_ref[...] = delta
        nm_ref[...] = nm
        nv_ref[...] = nv

    blk = pl.BlockSpec((br, bc), lambda r, q: (r, q))
    return pl.pallas_call(
        body, name=name, grid=(shape[0] // br, shape[1] // bc),
        in_specs=[blk, pl.BlockSpec((3, br, bc), lambda r, q: (0, r, q)), blk, blk, blk] + [ANY] * len(deps),
        out_specs=[blk] * 4, out_shape=[jax.ShapeDtypeStruct(shape, F32)] * 4,
        compiler_params=_cparams(("parallel", "parallel")),
    )(own, recv2, w, m, v, *deps)


def _small_adamw(stack, w, m, v):
    def body(s_ref, w_ref, m_ref, v_ref, loss_ref, *o_refs):
        g = s_ref[0]
        for s in range(1, N_DEV):
            g = g + s_ref[s]
        delta, nm, nv = _adamw(w_ref[...], g, m_ref[...], v_ref[...])
        loss_ref[...] = g[LOSS_ROW:LOSS_ROW + 1, 0:1]
        for q, t in enumerate((g, delta, nm, nv)):
            attn_ref, b_ref, mlp_ref, final_ref = o_refs[4 * q:4 * q + 4]
            for i in range(8):
                cols = slice(i * LANES, (i + 1) * LANES)
                attn_ref[:, cols] = t[i:i + 1]
                mlp_ref[:, cols] = t[8 + i:9 + i]
                final_ref[:, cols] = t[16 + i:17 + i]
            b_ref[...] = t[24:25, :FOX_H]

    vm = pl.BlockSpec(memory_space=pltpu.VMEM)
    unpacked = [jax.ShapeDtypeStruct((1, n), F32) for n in (D, FOX_H, D, D)]
    return pl.pallas_call(
        body, name="small_adamw",
        in_specs=[vm] * 4, out_specs=[vm] * 17,
        out_shape=[jax.ShapeDtypeStruct((1, 1), F32)] + unpacked * 4,
    )(stack, w, m, v)


W_IN_SEGMENTS = ((0, 3 * FOX_W, SEG_A), (3 * FOX_W, 3 * FOX_W + FOX_H, SEG_F),
                 (3 * FOX_W + FOX_H, 3 * FOX_W + FOX_H + 3 * DIL_W, SEG_B), (3 * FOX_W + FOX_H + 3 * DIL_W, D_IN, SEG_G))


def _shard_pieces(p):
    rows = D_IN // N_DEV
    lo, hi = p * rows, (p + 1) * rows
    return [(max(lo, a) - lo, min(hi, b) - lo, pad + max(lo, a) - a)
            for a, b, pad in W_IN_SEGMENTS if max(lo, a) < min(hi, b)]


def _unpad_runs(p):
    rows = D_IN // N_DEV
    blocks = []
    for lo in range(0, rows, PAD_BLK):
        hi = min(lo + PAD_BLK, rows)
        blocks.append([(max(lo, a) - lo, min(hi, b) - lo, pad + max(lo, a) - a)
                       for a, b, pad in _shard_pieces(p) if max(lo, a) < min(hi, b)])
    return blocks


def _unpad_dw_in_t(dwp, ids, kept, deps=()):
    rows = D_IN // N_DEV
    plans = [_unpad_runs(p) for p in range(N_DEV)]
    windows = [[min(src // 8 * 8, D_PAD - W_WIN) for runs in plan for _, _, src in runs] for plan in plans]
    n_win = max(len(w) for w in windows)

    def body(ids_ref, src_ref, *refs):
        o_ref, buf, sems = refs[len(deps):]
        side = ids_ref[0] if kept else 1 - ids_ref[0]
        p = 2 * pl.program_id(0) + side
        row = lax.broadcasted_iota(jnp.int32, (PAD_BLK, D), 0)

        def fetch(q):
            slot = (q // 2) % 2
            return [pltpu.make_async_copy(src_ref.at[pl.ds(start, W_WIN)], buf.at[slot, n], sems.at[slot, n])
                    for n, start in enumerate(windows[q])]

        for q in range(N_DEV):
            @pl.when(p == q)
            def _(q=q):
                for nxt in ([q, q + 2] if q < 2 else [q + 2]):
                    if nxt < N_DEV:
                        for cp in fetch(nxt):
                            cp.start()
                copies, n = fetch(q), 0
                for j, runs in enumerate(plans[q]):
                    out = None
                    for o_lo, o_hi, src in runs:
                        copies[n].wait()
                        moved = pltpu.roll(buf[(q // 2) % 2, n], (o_lo - (src - windows[q][n])) % W_WIN, 0)[:PAD_BLK]
                        out = moved if out is None else jnp.where((row >= o_lo) & (row < o_hi), moved, out)
                        n += 1
                    size = min(PAD_BLK, rows - j * PAD_BLK)
                    o_ref[j * PAD_BLK:j * PAD_BLK + size, :] = out[:size]

    grid_spec = pltpu.PrefetchScalarGridSpec(
        num_scalar_prefetch=1, grid=(N_DEV // 2,),
        in_specs=[ANY] * (1 + len(deps)), out_specs=pl.BlockSpec((None, rows, D), lambda k, ids: (k, 0, 0)),
        scratch_shapes=[pltpu.VMEM((2, n_win, W_WIN, D), F32), pltpu.SemaphoreType.DMA((2, n_win))])
    return pl.pallas_call(
        body, name="unpad_dw_in_kept" if kept else "unpad_dw_in_sent", grid_spec=grid_spec,
        out_shape=jax.ShapeDtypeStruct((N_DEV // 2, rows, D), F32),
        compiler_params=_cparams(("arbitrary",)),
    )(ids, dwp, *deps)


LOSS_ROW = 25


def _pack_small(g_attn, b, g_mlp, g_final, loss_row=None):
    tail = jnp.pad(b, ((0, 7), (0, LANES - b.shape[1])))
    if loss_row is not None:
        tail = tail + jnp.pad(loss_row, ((LOSS_ROW - 24, 31 - LOSS_ROW), (0, 0)))
    return jnp.concatenate([g_attn.reshape(8, LANES), g_mlp.reshape(8, LANES), g_final.reshape(8, LANES), tail], axis=0)


class _StepComm:
    def __init__(self, ids, w_sh, m_sh, v_sh, lands, after_w_in):
        self.ids = ids
        self.w_sh, self.m_sh, self.v_sh = w_sh, m_sh, v_sh
        self.updates = None
        self.gather_mixer = _exchange_start("gather_mixer_start", _peer_copies, PEER_COPIES * 3, [], lands[:3],
                                            after=(after_w_in,))
        self.gather_mlp = _exchange_start("gather_mlp_start", _peer_copies, PEER_COPIES * 2, [], lands[3:],
                                          after=(self.gather_mixer[-1],))
        self.sibling = {}
        self.chips = {}
        self.own = {}
        self.names = {}
        self.copies_fn = {}
        self.kept = {}

    def _reduce_start(self, group, names, grads, after=(), copies_fn=_sibling_copies):
        lands = [lax.empty((SIBLING_COPIES,) + t.shape[1:], F32) for t in grads]
        self.sibling[group] = _exchange_start("grad_%s_sibling_start" % group, copies_fn,
                                              SIBLING_COPIES * len(grads), grads, lands, after=after)
        self.names[group] = names
        self.copies_fn[group] = copies_fn
        return self.sibling[group][-1]

    def _reduce_mid(self, group, after):
        grads, recv1 = _exchange_wait("grad_%s_sibling_wait" % group, self.copies_fn[group], self.sibling[group],
                                      after)
        grads = self.kept.get(group, grads)
        parts = [_chip_partial(self.ids, g, r, "grad_partial_" + n) for g, r, n in zip(grads, recv1, self.names[group])]
        self.own[group] = [p[1] for p in parts]
        srcs = [p[0] for p in parts]
        lands = [lax.empty((CHIP_COPIES,) + t.shape[1:], BF16) for t in srcs]
        self.chips[group] = _exchange_start("grad_%s_chips_start" % group, _chip_copies, CHIP_COPIES * len(srcs),
                                            srcs, lands)
        return self.chips[group][-1]

    def reduced(self, group, after):
        _, recv2 = _exchange_wait("grad_%s_chips_wait" % group, _chip_copies, self.chips[group], after)
        return list(zip(self.own[group], recv2))

    def first_deps(self):
        return [self.gather_mlp[-1]]

    def mixer_weights(self, after):
        _, (g_a, g_b, g_out) = _exchange_wait("gather_mixer_wait", _peer_copies, self.gather_mixer, after)
        return g_a, g_b, g_out.reshape(D, D)

    def mlp_weights(self, after):
        _, (g_up, g_down) = _exchange_wait("gather_mlp_wait", _peer_copies, self.gather_mlp, after)
        return g_up, g_down.reshape(DFF, D)

    def mlp_grads(self, dw_up_sh, dw_down):
        return [self._reduce_start("mlp", W_NAMES[4:], [dw_up_sh, dw_down.reshape((N_DEV,) + SHARD_SHAPES[5])])]

    def mixer_grads(self, dw_a, dw_b, dw_out):
        token = self._reduce_mid("mlp", dw_b)
        grads = [dw_a, dw_b, dw_out.reshape((N_DEV,) + SHARD_SHAPES[3])]
        return [self._reduce_start("mixer", W_NAMES[1:4], grads, after=(token,))]

    def mid_backward(self, after):
        return [self._reduce_mid("mixer", after)]

    def w_in_grad(self, dw_in_t):
        token = self._reduce_start("w_in", W_NAMES[:1], [_unpad_dw_in_t(dw_in_t, self.ids, kept=False)],
                                   copies_fn=_sibling_half_copies)
        self.kept["w_in"] = [_unpad_dw_in_t(dw_in_t, self.ids, kept=True, deps=[token])]
        reduced = self.reduced("mixer", token) + self.reduced("mlp", token)
        self.updates, last = [None] * len(reduced), self.kept["w_in"][0]
        for i in (3, 4, 0, 1, 2):
            if i == 4:
                last = self._reduce_mid("w_in", last)
            self.updates[i] = _reduce_adamw(*reduced[i], self.w_sh[1 + i], self.m_sh[1 + i], self.v_sh[1 + i],
                                            "adamw_" + W_NAMES[1 + i], deps=[last])
            last = self.updates[i][0]
        return [last]

    def w_in_update(self, after):
        (reduced,) = self.reduced("w_in", after)
        return _reduce_adamw(*reduced, self.w_sh[0], self.m_sh[0], self.v_sh[0], "adamw_" + W_NAMES[0])


def kernel(x, norm_attn_g, w_in, b_forget, w_branch_a, w_branch_b, w_out, norm_mlp_g, w_up, w_down, norm_final_g, loss_target, m_norm_attn_g, m_w_in, m_b_forget, m_w_branch_a, m_w_branch_b, m_w_out, m_norm_mlp_g, m_w_up, m_w_down, m_norm_final_g, v_norm_attn_g, v_w_in, v_b_forget, v_w_branch_a, v_w_branch_b, v_w_out, v_norm_mlp_g, v_w_up, v_w_down, v_norm_final_g):
    cx, cy, cc = _place()
    ids = jnp.stack([cc, 2 * cx + cy]).astype(jnp.int32)

    w_sh = [w_in[0].T] + [t[0] for t in (w_branch_a, w_branch_b, w_out, w_up, w_down)]
    m_sh = [m_w_in[0].T] + [t[0] for t in (m_w_branch_a, m_w_branch_b, m_w_out, m_w_up, m_w_down)]
    v_sh = [v_w_in[0].T] + [t[0] for t in (v_w_branch_a, v_w_branch_b, v_w_out, v_w_up, v_w_down)]

    w_in_t, h1, *lands = _gather_w_in(jnp.pad(w_sh[0], ((0, W_SLAB - w_sh[0].shape[0]), (0, 0))), x[0], norm_attn_g,
                                      w_sh[1:])
    comm = _StepComm(ids, w_sh, m_sh, v_sh, lands, w_in_t)
    b_pad = jnp.pad(b_forget, ((0, 0), (0, LANES - FOX_H)))

    loss_row, dx, dsmall = _local_step(
        x[0], loss_target[0], h1, norm_attn_g, norm_mlp_g, norm_final_g.reshape(1, D), b_pad, w_in_t, comm)

    dg_attn, db, dg_mlp, dg_final = dsmall
    stack = lax.dynamic_update_slice(jnp.zeros((N_DEV, SMALL_R, LANES), F32),
                                     _pack_small(dg_attn, db, dg_mlp, dg_final, loss_row)[None],
                                     (4 * cx + 2 * cy + cc, 0, 0))
    small_gather = _exchange_start("small_gather_start", _peer_copies, PEER_COPIES, [], [stack])
    w_in_update = comm.w_in_update(small_gather[-1])
    _, (stack,) = _exchange_wait("small_gather_wait", _peer_copies, small_gather, w_in_update[0])
    loss, *small = _small_adamw(
        stack,
        _pack_small(norm_attn_g, b_forget, norm_mlp_g, norm_final_g.reshape(1, D)),
        _pack_small(m_norm_attn_g, m_b_forget, m_norm_mlp_g, m_norm_final_g.reshape(1, D)),
        _pack_small(v_norm_attn_g, v_b_forget, v_norm_mlp_g, v_norm_final_g.reshape(1, D)))
    big = [w_in_update] + comm.updates

    outs = [loss.reshape(()), dx[None]]
    for q in range(4):
        s_attn, s_b, s_mlp, s_final = small[4 * q:4 * q + 4]
        s_final = s_final.reshape(D)
        b_in, b_a, b_b, b_out, b_up, b_down = [t[q][None] for t in big]
        b_in = jnp.swapaxes(b_in, 1, 2)
        outs += [s_attn, b_in, s_b, b_a, b_b, b_out, s_mlp, b_up, b_down, s_final]
    return tuple(outs)
```

```python
import functools

import jax
import jax.numpy as jnp
from jax import lax
from jax.experimental import pallas as pl
from jax.experimental.pallas import tpu as pltpu

F32 = jnp.float32
BF16 = jnp.bfloat16
MESH = pl.DeviceIdType.MESH

S = 2048
D = 1024
HD = 64
FOX_H = 8
FOX_W = FOX_H * HD
DIL_HG = 4
DIL_G = 3
DIL_W = DIL_G * DIL_HG * HD
DIL_OUT = DIL_HG * HD
DIL_BLK = 128
DIL_R = (1, 4, 16)
DFF = 4 * D
D_IN = 3 * FOX_W + FOX_H + 3 * DIL_W + 2 * D
EPS = 1e-6
NEG_INF = -1e30
SCALE = HD ** -0.5
ROPE_THETA = 500000.0
ROPE_DIM = HD // 4
N_DEV = 8

ADAM_LR = 0.001
ADAM_B1 = 0.9
ADAM_B2 = 0.999
ADAM_EPS = 1e-08
ADAM_WD = 0.01
ADAM_STEP = 10

LANES = 128
CB = 256
SEG_A = 0
SEG_B = 3 * FOX_W
SEG_Z = SEG_B + 3 * DIL_W
SEG_G = SEG_Z + CB
SEG_F = SEG_G + 2 * D
D_PAD = SEG_F + CB
DW_IN_TM = D_PAD // 10
SMALL_R = 32

VMEM_MB = 56


def _cparams(dims=None, vmem_mb=VMEM_MB, **kw):
    return pltpu.CompilerParams(dimension_semantics=dims, vmem_limit_bytes=vmem_mb << 20, **kw)


ANY = pl.BlockSpec(memory_space=pl.ANY)

_NN = (((1,), (0,)), ((), ()))
_NT = (((1,), (1,)), ((), ()))
_TN = (((0,), (0,)), ((), ()))


def _dot(a, b, dims):
    return lax.dot_general(a.astype(BF16), b.astype(BF16), dims, preferred_element_type=F32)


def _mm(a, b, *, mode, tm, tn, out_dtypes, name, n=None, k=None, a_off=0, b_off=0,
        b_sharded=False, out_sharded=False, epilogue=None, extras=(), deps=()):
    n_sh = b.shape[-1] if b_sharded else None
    if mode == "nn":
        m = a.shape[0]
        k = k or a.shape[1]
        a_spec = pl.BlockSpec((tm, k), lambda i, j: (i, a_off))
        if b_sharded:
            assert tn == n_sh
            n = N_DEV * n_sh
            b_spec = pl.BlockSpec((None, k, tn), lambda i, j: (j, 0, 0))
        else:
            n = n or b.shape[1]
            b_spec = pl.BlockSpec((k, tn), lambda i, j: (0, j + b_off))
        dims = _NN
    elif mode == "nt":
        m = a.shape[0]
        k = k or a.shape[1]
        a_spec = pl.BlockSpec((tm, k), lambda i, j: (i, a_off))
        if b_sharded:
            n = b.shape[1]
            b_spec = pl.BlockSpec((N_DEV, tn, n_sh), lambda i, j: (0, j, 0))
        else:
            n = n or b.shape[0]
            b_spec = pl.BlockSpec((tn, k), lambda i, j: (j + b_off, 0))
        dims = _NT
    else:
        k, m = a.shape
        n = n or b.shape[1]
        a_spec = pl.BlockSpec((k, tm), lambda i, j: (0, i))
        b_spec = pl.BlockSpec((k, tn), lambda i, j: (0, j + b_off))
        dims = _TN
    assert m % tm == 0 and n % tn == 0, (name, m, n, tm, tn)
    n_extra = len(extras)
    tile = pl.BlockSpec((tm, tn), lambda i, j: (i, j))
    split_tile = out_sharded and n == tn
    if out_sharded:
        assert mode == "tn" and n // tn in (1, N_DEV)
        if split_tile:
            out_spec = pl.BlockSpec((N_DEV, tm, n // N_DEV), lambda i, j: (0, i, 0))
        else:
            out_spec = pl.BlockSpec((None, tm, tn), lambda i, j: (j, i, 0))
        out_shape = (N_DEV, m, n // N_DEV)
    else:
        out_spec, out_shape = tile, (m, n)

    def body(a_ref, b_ref, *refs):
        if mode == "nt" and b_sharded:
            acc = _dot(a_ref[...], jnp.concatenate([b_ref[p] for p in range(N_DEV)], axis=1), dims)
        else:
            acc = _dot(a_ref[...], b_ref[...], dims)
        ex = [r[...] for r in refs[:n_extra]]
        outs = epilogue(acc, *ex) if epilogue is not None else (acc,)
        for o_ref, o in zip(refs[n_extra + len(deps):], outs):
            if split_tile:
                w = n // N_DEV
                for p in range(N_DEV):
                    o_ref[p] = o[:, p * w:(p + 1) * w].astype(o_ref.dtype)
            else:
                o_ref[...] = o.astype(o_ref.dtype)

    res = pl.pallas_call(
        body, name=name, grid=(m // tm, n // tn),
        in_specs=[a_spec, b_spec] + [tile] * n_extra + [ANY] * len(deps),
        out_specs=[out_spec] * len(out_dtypes),
        out_shape=[jax.ShapeDtypeStruct(out_shape, dt) for dt in out_dtypes],
        compiler_params=_cparams(("parallel", "parallel")),
    )(a, b, *extras, *deps)
    return res if len(out_dtypes) > 1 else res[0]


def _mm_rows(a, b, *, tm, name, epilogue, rows=(), vecs=(), row_out=(), vec_out=(), b_sharded=False, deps=()):
    m, k = a.shape
    n_rows, n_vecs, n_deps = len(rows), len(vecs), len(deps)
    n_sh = b.shape[-1] if b_sharded else None

    def body(a_ref, b_ref, *refs):
        row_refs, vec_refs = refs[:n_rows], refs[n_rows:n_rows + n_vecs]
        outs = refs[n_rows + n_vecs + n_deps:]
        if b_sharded:
            acc = _dot(a_ref[:, 0:n_sh], b_ref[0], _NT)
            for p in range(1, N_DEV):
                acc = acc + _dot(a_ref[:, p * n_sh:(p + 1) * n_sh], b_ref[p], _NT)
        else:
            acc = _dot(a_ref[...], b_ref[...], _NN)
        row_vals, vec_incs = epilogue(acc, [r[...] for r in row_refs], [v[...] for v in vec_refs])
        for o_ref, val in zip(outs[:len(row_out)], row_vals):
            o_ref[...] = val.astype(o_ref.dtype)

        @pl.when(pl.program_id(0) == 0)
        def _():
            for o_ref in outs[len(row_out):]:
                o_ref[...] = jnp.zeros_like(o_ref)

        for o_ref, inc in zip(outs[len(row_out):], vec_incs):
            o_ref[...] += inc

    tile = pl.BlockSpec((tm, D), lambda i: (i, 0))
    b_spec = pl.BlockSpec(b.shape, lambda i: (0,) * b.ndim)
    return pl.pallas_call(
        body, name=name, grid=(m // tm,),
        in_specs=[pl.BlockSpec((tm, k), lambda i: (i, 0)), b_spec] + [tile] * n_rows
                 + [pl.BlockSpec(v.shape, lambda i: (0, 0)) for v in vecs] + [ANY] * n_deps,
        out_specs=[tile] * len(row_out) + [pl.BlockSpec((1, w), lambda i: (0, 0)) for w in vec_out],
        out_shape=[jax.ShapeDtypeStruct((m, D), dt) for dt in row_out]
                  + [jax.ShapeDtypeStruct((1, w), F32) for w in vec_out],
        compiler_params=_cparams(("arbitrary",)),
    )(a, b, *rows, *vecs, *deps)


ROW_T = 256
ROWS_TM = 512


def _rms_rows(x, g):
    r = lax.rsqrt(jnp.mean(x * x, axis=-1, keepdims=True) + EPS)
    return (x * r) * g


def _rms_bwd_rows(dh, x, dres, g):
    r = lax.rsqrt(jnp.mean(x * x, axis=-1, keepdims=True) + EPS)
    xn = x * r
    dhn = dh * g
    dx = dres + r * (dhn - xn * jnp.mean(dhn * xn, axis=-1, keepdims=True))
    return dx, jnp.sum(dh * xn, axis=0, keepdims=True)


def _final_loss_rows(x, g, tgt):
    r = lax.rsqrt(jnp.mean(x * x, axis=-1, keepdims=True) + EPS)
    xn = x * r
    err = xn * g - tgt
    row_loss = jnp.mean(err * err, axis=-1, keepdims=True)
    loss = 0.5 * jnp.sum(row_loss, axis=0, keepdims=True) * jnp.ones((1, LANES), F32)
    dy = err * (1.0 / D)
    dyn = dy * g
    dx = r * (dyn - xn * jnp.mean(dyn * xn, axis=-1, keepdims=True))
    return dx, jnp.sum(dy * xn, axis=0, keepdims=True), loss


def _sigmoid(z):
    return 1.0 / (1.0 + jnp.exp(-z))


GATE_TR = 512


def _mixer_tail(oa, ob, w_a, w_b, w_out, proj_g, x, g_mlp):
    def body(oa_ref, ob_ref, wa_ref, wb_ref, wo_ref, g_ref, x_ref, gm_ref, ya_ref, yb_ref, mixed_ref, x2_ref, h2_ref):
        ya = _dot(oa_ref[...], jnp.concatenate([wa_ref[p] for p in range(N_DEV)], axis=1), _NN)
        yb = _dot(ob_ref[...], jnp.concatenate([wb_ref[p] for p in range(N_DEV)], axis=1), _NN)
        ya_ref[...] = ya
        yb_ref[...] = yb
        mixed = (_sigmoid(g_ref[:, :D]) * ya + _sigmoid(g_ref[:, D:]) * yb).astype(BF16)
        mixed_ref[...] = mixed
        x2 = x_ref[...] + _dot(mixed, wo_ref[...], _NN)
        x2_ref[...] = x2
        h2_ref[...] = _rms_rows(x2, gm_ref[...]).astype(BF16)

    def rows(width):
        return pl.BlockSpec((GATE_TR, width), lambda i: (i, 0))

    def whole(t):
        return pl.BlockSpec(t.shape, lambda i: (0,) * t.ndim)

    return pl.pallas_call(
        body, name="mixer_tail", grid=(S // GATE_TR,),
        in_specs=[rows(FOX_W), rows(DIL_OUT), whole(w_a), whole(w_b), whole(w_out), rows(2 * D), rows(D), whole(g_mlp)],
        out_specs=[rows(D)] * 5,
        out_shape=[jax.ShapeDtypeStruct((S, D), dt) for dt in (F32, F32, BF16, F32, BF16)],
        compiler_params=_cparams(("parallel",)),
    )(oa, ob, w_a, w_b, w_out, proj_g, x, g_mlp)


def _out_proj_gate_bwd(dx2, w_out, proj_g, ya, yb):
    def body(dx_ref, w_ref, g_ref, ya_ref, yb_ref, dy_ref, dg_ref):
        dm = _dot(dx_ref[...], w_ref[...], _NT)
        for half, y_ref in enumerate((ya_ref, yb_ref)):
            cols = slice(half * D, (half + 1) * D)
            s = _sigmoid(g_ref[:, cols])
            dy_ref[:, cols] = (dm * s).astype(BF16)
            dg_ref[:, cols] = (dm * y_ref[...] * (s * (1.0 - s))).astype(BF16)

    row = pl.BlockSpec((GATE_TR, D), lambda i: (i, 0))
    wide = pl.BlockSpec((GATE_TR, 2 * D), lambda i: (i, 0))
    return pl.pallas_call(
        body, name="out_proj_gate_bwd", grid=(S // GATE_TR,),
        in_specs=[row, pl.BlockSpec((D, D), lambda i: (0, 0)), wide, row, row],
        out_specs=[wide, pl.BlockSpec((GATE_TR, 2 * D), lambda i: (i, SEG_G // (2 * D)))],
        out_shape=[jax.ShapeDtypeStruct((S, 2 * D), BF16), jax.ShapeDtypeStruct((S, D_PAD), BF16)],
        compiler_params=_cparams(("parallel",)),
    )(dx2, w_out, proj_g, ya, yb)


FOX_TQ = 512
FOX_TQ_FWD = 1024


def _scan_rows(x, reverse):
    n = x.shape[0]
    row = lax.broadcasted_iota(jnp.int32, x.shape, 0)
    k = 1
    while k < n:
        if reverse:
            x = x + jnp.where(row < n - k, pltpu.roll(x, n - k, 0), 0.0)
        else:
            x = x + jnp.where(row >= k, pltpu.roll(x, k, 0), 0.0)
        k *= 2
    return x


def _fox_dscan(dft, dfs, proj_f, b_pad, dproj):
    def body(dft_ref, dfs_ref, f_ref, b_ref, _, dfa_ref, db_ref):
        dfs_pad = jnp.concatenate([dfs_ref[...], jnp.zeros((LANES - FOX_H, S), F32)], axis=0)
        df = dfs_pad.T + dft_ref[0]
        for hp in range(1, dft_ref.shape[0]):
            df = df + pltpu.roll(dft_ref[hp], 2 * hp, 1)
        dlf = _scan_rows(df, reverse=True)
        z = f_ref[...] + b_ref[...]
        lane = lax.broadcasted_iota(jnp.int32, (S, LANES), 1)
        dfa = jnp.where(lane < FOX_H, dlf / (1.0 + jnp.exp(z)), 0.0)
        dfa_ref[:, :LANES] = dfa.astype(BF16)
        dfa_ref[:, LANES:] = jnp.zeros((S, CB - LANES), BF16)
        db_ref[...] = jnp.sum(dfa, axis=0, keepdims=True)

    return pl.pallas_call(
        body, name="fox_dscan", grid=(1,),
        in_specs=[pl.BlockSpec(dft.shape, lambda i: (0, 0, 0)), pl.BlockSpec((FOX_H, S), lambda i: (0, 0)),
                  pl.BlockSpec((S, LANES), lambda i: (0, 0)), pl.BlockSpec((1, LANES), lambda i: (0, 0)), ANY],
        out_specs=[pl.BlockSpec((S, CB), lambda i: (0, SEG_F // CB)), pl.BlockSpec((1, LANES), lambda i: (0, 0))],
        out_shape=[jax.ShapeDtypeStruct((S, D_PAD), BF16), jax.ShapeDtypeStruct((1, LANES), F32)],
        input_output_aliases={4: 0},
        compiler_params=_cparams(("arbitrary",)),
    )(dft, dfs, proj_f, b_pad, dproj)


FOX_NQ = S // FOX_TQ
FOX_HP = FOX_W // LANES
HEADS_PER_LB = LANES // HD
FOX_AUG = 2 * LANES


def _fox_prepare(proj_a, proj_f, b_pad):
    tr = 512

    def body(q_ref, k_ref, f_ref, b_ref, qa_ref, ka_ref, hi_s, mid_s, lo_s):
        i = pl.program_id(0)

        @pl.when(i == 0)
        def _():
            z = f_ref[...] + b_ref[...]
            lf = jnp.minimum(z, 0.0) - jnp.log1p(jnp.exp(-jnp.abs(z)))
            f = _scan_rows(lf, reverse=False)
            hi = f.astype(BF16).astype(F32)
            r1 = f - hi
            mid = r1.astype(BF16).astype(F32)
            hi_s[...] = hi
            mid_s[...] = mid
            lo_s[...] = (r1 - mid).astype(BF16).astype(F32)

        rows = pl.ds(pl.multiple_of(i * tr, tr), tr)
        hi, mid, lo = hi_s[rows, :], mid_s[rows, :], lo_s[rows, :]
        lane = lax.broadcasted_iota(jnp.int32, (tr, HD), 1)
        q_ones = jnp.where((lane >= 3) & (lane < 6), 1.0, 0.0)
        k_ones = jnp.where(lane < 3, 1.0, 0.0)
        for h in range(FOX_H):
            a1, a2, a3 = hi[:, h:h + 1], mid[:, h:h + 1], lo[:, h:h + 1]
            q_extra = jnp.where(lane == 0, a1, jnp.where(lane == 1, a2, jnp.where(lane == 2, a3, q_ones)))
            k_extra = jnp.where(lane == 3, -a1, jnp.where(lane == 4, -a2, jnp.where(lane == 5, -a3, k_ones)))
            base = h * LANES
            qa_ref[:, base:base + HD] = q_ref[:, h * HD:(h + 1) * HD] * jnp.asarray(SCALE, BF16)
            qa_ref[:, base + HD:base + LANES] = q_extra.astype(BF16)
            ka_ref[:, base:base + HD] = k_ref[:, h * HD:(h + 1) * HD]
            ka_ref[:, base + HD:base + LANES] = k_extra.astype(BF16)

    out_blk = pl.BlockSpec((tr, FOX_H * LANES), lambda i: (i, 0))
    return pl.pallas_call(
        body, name="fox_prepare", grid=(S // tr,),
        in_specs=[pl.BlockSpec((tr, FOX_W), lambda i: (i, 0)), pl.BlockSpec((tr, FOX_W), lambda i: (i, 1)),
                  pl.BlockSpec((S, LANES), lambda i: (0, 0)), pl.BlockSpec((1, LANES), lambda i: (0, 0))],
        out_specs=[out_blk, out_blk],
        out_shape=[jax.ShapeDtypeStruct((S, FOX_H * LANES), BF16)] * 2,
        scratch_shapes=[pltpu.VMEM((S, LANES), F32)] * 3,
        compiler_params=_cparams(("arbitrary",)),
    )(proj_a, proj_a, proj_f, b_pad)


def _fox_scores(qa, ka, w, tq):
    s = _dot(qa, ka, _NT)
    row = lax.broadcasted_iota(jnp.int32, (tq, tq), 0)
    col = lax.broadcasted_iota(jnp.int32, (tq, tq), 1)
    diag = jnp.where(col <= row, s[:, w * tq:], NEG_INF)
    return diag if w == 0 else jnp.concatenate([s[:, :w * tq], diag], axis=1)


def _fox_fwd(q_aug, k_aug, proj_a):
    tq = FOX_TQ_FWD

    def body(qa_ref, ka_ref, v_ref, o_ref, lse_ref):
        qi = pl.program_id(1)
        lane = lax.broadcasted_iota(jnp.int32, (tq, LANES), 1)
        for w in range(S // tq):
            @pl.when(qi == w)
            def _(w=w):
                width = (w + 1) * tq
                lse = jnp.zeros((tq, LANES), F32)
                for hh in range(HEADS_PER_LB):
                    aug = slice(hh * LANES, (hh + 1) * LANES)
                    sl = slice(hh * HD, (hh + 1) * HD)
                    s = _fox_scores(qa_ref[:, aug], ka_ref[:width, aug], w, tq)
                    m = jnp.max(s, axis=-1, keepdims=True)
                    p = jnp.exp(s - m)
                    l = jnp.sum(p, axis=-1, keepdims=True)
                    o_ref[:, sl] = _dot(p / l, v_ref[:width, sl], _NN)
                    lse = jnp.where(lane == hh, m + jnp.log(l), lse)
                lse_ref[...] = lse

    return pl.pallas_call(
        body, name="fox_fwd", grid=(FOX_HP, S // tq),
        in_specs=[pl.BlockSpec((tq, FOX_AUG), lambda h, i: (i, h)),
                  pl.BlockSpec((S, FOX_AUG), lambda h, i: (0, h)),
                  pl.BlockSpec((S, LANES), lambda h, i: (0, 2 * FOX_HP + h))],
        out_specs=[pl.BlockSpec((tq, LANES), lambda h, i: (i, h)),
                   pl.BlockSpec((None, tq, LANES), lambda h, i: (h, i, 0))],
        out_shape=[jax.ShapeDtypeStruct((S, FOX_W), F32), jax.ShapeDtypeStruct((FOX_HP, S, LANES), F32)],
        compiler_params=_cparams(("parallel", "parallel")),
    )(q_aug, k_aug, proj_a)


def _fox_bwd(q_aug, k_aug, proj_a, lse, do, dproj, deps=()):
    n_q = FOX_NQ

    def body(qa_ref, ka_ref, v_ref, lse_ref, do_ref, *refs):
        dproj_ref, dft_ref, dfs_ref, dk_acc, dv_acc, dq_buf, kv_buf, dq_sems, kv_sems = refs[1 + len(deps):]
        hp = pl.program_id(0)
        t = pl.program_id(1)
        slot = t % 2
        col = pl.multiple_of(hp * LANES, LANES)

        def dq_copy(step, buf_slot):
            rows = pl.ds(pl.multiple_of(step * FOX_TQ, FOX_TQ), FOX_TQ)
            return pltpu.make_async_copy(dq_buf.at[buf_slot], dproj_ref.at[rows, pl.ds(col, LANES)], dq_sems.at[buf_slot])

        @pl.when(t == 0)
        def _():
            dk_acc[...] = jnp.zeros_like(dk_acc)
            dv_acc[...] = jnp.zeros_like(dv_acc)

        @pl.when((t == 0) & (hp == 0))
        def _():
            dfs_ref[...] = jnp.zeros_like(dfs_ref)

        @pl.when(t >= 2)
        def _():
            dq_copy(t - 2, slot).wait()

        lane = lax.broadcasted_iota(jnp.int32, (FOX_TQ, LANES), 1)
        for w in range(n_q):
            @pl.when(t == w)
            def _(w=w):
                width = (w + 1) * FOX_TQ
                lse_all = lse_ref[...]
                sub = lax.broadcasted_iota(jnp.int32, (FOX_H, width), 0)
                dft = jnp.zeros((FOX_TQ, LANES), F32)
                for hh in range(HEADS_PER_LB):
                    aug = slice(hh * LANES, (hh + 1) * LANES)
                    sl = slice(hh * HD, (hh + 1) * HD)
                    head = hp * HEADS_PER_LB + hh
                    qa = qa_ref[:, aug]
                    ka = ka_ref[:width, aug]
                    do_h = do_ref[:, sl]
                    s = _fox_scores(qa, ka, w, FOX_TQ)
                    p = jnp.exp(s - lse_all[:, hh:hh + 1])
                    dp = _dot(do_h, v_ref[:width, sl], _NT)
                    ds = p * (dp - jnp.sum(dp * p, axis=-1, keepdims=True))
                    dft = jnp.where(lane == hh, jnp.sum(ds, axis=-1, keepdims=True), dft)
                    dfs_ref[:, :width] -= jnp.where(sub == head, jnp.sum(ds, axis=0, keepdims=True), 0.0)
                    dq_buf[slot, :, sl] = (_dot(ds, ka[:, :HD], _NN) * SCALE).astype(BF16)
                    dk_acc[:width, sl] += _dot(ds, qa[:, :HD], _TN)
                    dv_acc[:width, sl] += _dot(p, do_h, _TN)
                dft_ref[...] = dft

        dq_copy(t, slot).start()

        @pl.when(t == n_q - 1)
        def _():
            dq_copy(t - 1, 1 - slot).wait()
            dq_copy(t, slot).wait()
            kv_buf[0] = dk_acc[...].astype(BF16)
            kv_buf[1] = dv_acc[...].astype(BF16)
            copies = [pltpu.make_async_copy(
                kv_buf.at[j], dproj_ref.at[:, pl.ds(pl.multiple_of((j + 1) * FOX_W + hp * LANES, LANES), LANES)],
                kv_sems.at[j]) for j in range(2)]
            for cp in copies:
                cp.start()
            for cp in copies:
                cp.wait()

    qblk = pl.BlockSpec((FOX_TQ, FOX_AUG), lambda h, t: (t, h))
    lane_blk = pl.BlockSpec((None, FOX_TQ, LANES), lambda h, t: (h, t, 0))
    return pl.pallas_call(
        body, name="fox_bwd", grid=(FOX_HP, n_q),
        in_specs=[qblk, pl.BlockSpec((S, FOX_AUG), lambda h, t: (0, h)),
                  pl.BlockSpec((S, LANES), lambda h, t: (0, 2 * FOX_HP + h)),
                  lane_blk, pl.BlockSpec((FOX_TQ, LANES), lambda h, t: (t, h)), ANY] + [ANY] * len(deps),
        out_specs=[ANY, lane_blk, pl.BlockSpec((FOX_H, S), lambda h, t: (0, 0))],
        out_shape=[jax.ShapeDtypeStruct((S, D_PAD), BF16),
                   jax.ShapeDtypeStruct((FOX_HP, S, LANES), F32), jax.ShapeDtypeStruct((FOX_H, S), F32)],
        scratch_shapes=[pltpu.VMEM((S, LANES), F32), pltpu.VMEM((S, LANES), F32),
                        pltpu.VMEM((2, FOX_TQ, LANES), BF16), pltpu.VMEM((2, S, LANES), BF16),
                        pltpu.SemaphoreType.DMA((2,)), pltpu.SemaphoreType.DMA((2,))],
        input_output_aliases={5: 0},
        compiler_params=_cparams(("arbitrary", "arbitrary")),
    )(q_aug, k_aug, proj_a, lse, do, dproj, *deps)


def _rope_tables():
    half = ROPE_DIM // 2
    shape = (DIL_G, S, half)
    g = lax.broadcasted_iota(jnp.int32, shape, 0)
    row = lax.broadcasted_iota(jnp.int32, shape, 1)
    r = jnp.left_shift(1, 2 * g)
    per_class = S // r
    pos = (row % per_class) * r + row // per_class
    inv_freq = jnp.power(jnp.float32(ROPE_THETA), -lax.broadcasted_iota(F32, shape, 2) * 2.0 / ROPE_DIM)
    ang = pos.astype(F32) * inv_freq
    return jnp.concatenate([jnp.cos(ang), jnp.sin(ang)], axis=-1)


def _rope_factors(cs):
    half = ROPE_DIM // 2
    i = lax.broadcasted_iota(jnp.int32, (2 * half, 2 * LANES), 0)
    l = lax.broadcasted_iota(jnp.int32, (2 * half, 2 * LANES), 1)
    hit = ((l < LANES) & (i == l % half)) | ((l >= LANES) & (i - half == l % half))
    spread = jnp.where(hit, 1.0, 0.0).astype(BF16)
    hi = cs.astype(BF16)
    r1 = cs - hi.astype(F32)
    mid = r1.astype(BF16)
    lo = (r1 - mid.astype(F32)).astype(BF16)
    full = (_dot(hi, spread, _NN) + _dot(mid, spread, _NN)) + _dot(lo, spread, _NN)
    cos, sin = full[:, :LANES], full[:, LANES:]
    lane = lax.broadcasted_iota(jnp.int32, cos.shape, 1) % HD
    t_self = jnp.where(lane < ROPE_DIM, cos, 1.0)
    t_up = jnp.where(lane < half, -sin, 0.0)
    t_dn = jnp.where((lane >= half) & (lane < ROPE_DIM), sin, 0.0)
    return t_self, t_up, t_dn


def _residue_pieces(r):
    if r == 1:
        return [(slice(i, i + 512), slice(i, i + 512)) for i in range(0, S, 512)]
    n = S // r
    return [(pl.ds(j, n, stride=r), slice(j * n, (j + 1) * n)) for j in range(r)]


def _rope_apply(x, a, u, d, backward):
    half = ROPE_DIM // 2
    if backward:
        return x * a + pltpu.roll(x * u, half, 1) + pltpu.roll(x * d, LANES - half, 1)
    return x * a + pltpu.roll(x, LANES - half, 1) * u + pltpu.roll(x, half, 1) * d


def _rope_cache_factors(cs_ref, fac, first_step_of_group):
    @pl.when(first_step_of_group)
    def _():
        for lo in range(0, S, 512):
            for j, f in enumerate(_rope_factors(cs_ref[lo:lo + 512, :])):
                fac[j, lo:lo + 512, :] = f


LB_PER_CB = CB // LANES
ROPE_NB = 3 * DIL_G * LB_PER_CB


def _rope_step(s):
    per_group = 3 * LB_PER_CB
    return s // per_group, (s % per_group) // LB_PER_CB, s % LB_PER_CB


def _rope_split(proj_b, tables):
    def body(x_ref, cs_ref, o_ref, fac):
        g, t, hf = _rope_step(pl.program_id(0))
        _rope_cache_factors(cs_ref, fac, (t == 0) & (hf == 0))
        for gs in range(DIL_G):
            @pl.when(g == gs)
            def _(gs=gs):
                for tok, sub in _residue_pieces(DIL_R[gs]):
                    x = x_ref[tok, :]
                    y = _rope_apply(x, fac[0, sub, :], fac[1, sub, :], fac[2, sub, :], False)
                    o_ref[sub, :] = jnp.where(t < 2, y, x).astype(BF16)

    def in_index(s):
        g, t, hf = _rope_step(s)
        return 0, (t * DIL_G + g) * LB_PER_CB + hf

    def out_index(s):
        g, t, hf = _rope_step(s)
        return t * DIL_G + g, 0, hf

    tab = pl.BlockSpec((None, S, 2 * (ROPE_DIM // 2)), lambda s: (_rope_step(s)[0], 0, 0))
    return pl.pallas_call(
        body, name="rope_split", grid=(ROPE_NB,),
        in_specs=[pl.BlockSpec((S, LANES), in_index), tab],
        out_specs=pl.BlockSpec((None, S, LANES), out_index),
        out_shape=jax.ShapeDtypeStruct((3 * DIL_G, S, CB), BF16),
        scratch_shapes=[pltpu.VMEM((3, S, LANES), F32)],
        compiler_params=_cparams(("arbitrary",)),
    )(proj_b, tables)


def _rope_merge_bwd(dq3, dk3, dv3, tables, dproj):
    def body(dq_ref, dk_ref, dv_ref, cs_ref, _, o_ref, tmp, fac):
        s = pl.program_id(0)
        g, t, hf = _rope_step(s)
        _rope_cache_factors(cs_ref, fac, (t == 0) & (hf == 0) & (s < ROPE_NB))
        for gs in range(DIL_G):
            @pl.when((g == gs) & (s < ROPE_NB))
            def _(gs=gs):
                for tok, sub in _residue_pieces(DIL_R[gs]):
                    x = jnp.where(t == 0, dq_ref[sub, :], jnp.where(t == 1, dk_ref[sub, :], dv_ref[sub, :]))
                    y = _rope_apply(x, fac[0, sub, :], fac[1, sub, :], fac[2, sub, :], True)
                    tmp[tok, :] = jnp.where(t < 2, y, x)
                o_ref[...] = tmp[...].astype(BF16)

        @pl.when(s >= ROPE_NB)
        def _():
            o_ref[...] = jnp.zeros_like(o_ref)

    def src_spec(own):
        def index(s):
            g, t, hf = _rope_step(jnp.minimum(s, ROPE_NB - 1))
            return g, 0, jnp.where(t == own, hf, jnp.where(t > own, LB_PER_CB - 1, 0))
        return pl.BlockSpec((None, S, LANES), index)

    def out_index(s):
        g, t, hf = _rope_step(s)
        col = SEG_B // LANES + (t * DIL_G + g) * LB_PER_CB + hf
        return 0, jnp.where(s < ROPE_NB, col, SEG_Z // LANES + s - ROPE_NB)

    tab = pl.BlockSpec((None, S, 2 * (ROPE_DIM // 2)), lambda s: (_rope_step(jnp.minimum(s, ROPE_NB - 1))[0], 0, 0))
    return pl.pallas_call(
        body, name="rope_merge_bwd", grid=(ROPE_NB + LB_PER_CB,),
        in_specs=[src_spec(0), src_spec(1), src_spec(2), tab, ANY],
        out_specs=pl.BlockSpec((S, LANES), out_index),
        out_shape=jax.ShapeDtypeStruct((S, D_PAD), BF16),
        scratch_shapes=[pltpu.VMEM((S, LANES), F32), pltpu.VMEM((3, S, LANES), F32)],
        input_output_aliases={4: 0},
        compiler_params=_cparams(("arbitrary",)),
    )(dq3, dk3, dv3, tables, dproj)


DIL_NB = S // DIL_BLK


_BQK = (((2,), (2,)), ((0,), (0,)))
_BQD = (((2,), (1,)), ((0,), (0,)))
_BKD = (((1,), (1,)), ((0,), (0,)))


def _dil_mask(g):
    shape = (DIL_NB, DIL_BLK, 2 * DIL_BLK)
    blocks_per_seq = jnp.right_shift(DIL_NB, 2 * g)
    has_prev = jnp.bitwise_and(lax.broadcasted_iota(jnp.int32, shape, 0), blocks_per_seq - 1) != 0
    a = lax.broadcasted_iota(jnp.int32, shape, 1)
    kk = lax.broadcasted_iota(jnp.int32, shape, 2)
    diff = DIL_BLK + a - kk
    return (diff >= 0) & (diff <= DIL_BLK) & ((kk >= DIL_BLK) | has_prev)


def _blocks(t):
    return t.reshape(DIL_NB, DIL_BLK, t.shape[-1])


def _with_prev(t):
    prev = jnp.concatenate([jnp.zeros((DIL_BLK, t.shape[-1]), t.dtype), t[:-DIL_BLK]], axis=0)
    return jnp.concatenate([_blocks(prev), _blocks(t)], axis=1)


def _fold_prev(t2):
    n = t2.shape[-1]
    to_prev = t2[:, :DIL_BLK].reshape(S, n)
    own = t2[:, DIL_BLK:].reshape(S, n)
    return own + jnp.concatenate([to_prev[DIL_BLK:], jnp.zeros((DIL_BLK, n), t2.dtype)], axis=0)


def _dil_group_spec(t, width=DIL_OUT):
    return pl.BlockSpec((None, S, width), lambda g: (t * DIL_G + g, 0, 0))


def _dil_fwd(qkv3):
    def body(q_ref, k_ref, v_ref, o_ref, lse_ref):
        ok = _dil_mask(pl.program_id(0))
        lane = lax.broadcasted_iota(jnp.int32, (S, LANES), 1)
        lse = jnp.zeros((S, LANES), F32)
        for h in range(DIL_HG):
            sl = slice(h * HD, (h + 1) * HD)
            s = jnp.where(ok, _dot(_blocks(q_ref[:, sl]), _with_prev(k_ref[:, sl]), _BQK) * SCALE, NEG_INF)
            m = jnp.max(s, axis=-1, keepdims=True)
            p = jnp.exp(s - m)
            l = jnp.sum(p, axis=-1, keepdims=True)
            o_ref[:, sl] = _dot(p / l, _with_prev(v_ref[:, sl]), _BQD).reshape(S, HD)
            lse = jnp.where(lane == h, (m + jnp.log(l)).reshape(S, 1), lse)
        lse_ref[...] = lse

    return pl.pallas_call(
        body, name="dil_fwd", grid=(DIL_G,),
        in_specs=[_dil_group_spec(0), _dil_group_spec(1), _dil_group_spec(2)],
        out_specs=[_dil_group_spec(0), _dil_group_spec(0, LANES)],
        out_shape=[jax.ShapeDtypeStruct((DIL_G, S, DIL_OUT), F32), jax.ShapeDtypeStruct((DIL_G, S, LANES), F32)],
        compiler_params=_cparams(("parallel",)),
    )(qkv3, qkv3, qkv3)


def _dil_bwd(qkv3, lse3, do3, c3):
    def body(q_ref, k_ref, v_ref, lse_ref, do_ref, c_ref, dq_ref, dk_ref, dv_ref):
        ok = _dil_mask(pl.program_id(0))
        lse = lse_ref[...]
        c = c_ref[...]
        for h in range(DIL_HG):
            sl = slice(h * HD, (h + 1) * HD)
            q = _blocks(q_ref[:, sl])
            k2 = _with_prev(k_ref[:, sl])
            do_h = _blocks(do_ref[:, sl])
            s = jnp.where(ok, _dot(q, k2, _BQK) * SCALE, NEG_INF)
            p = jnp.exp(s - _blocks(lse[:, h:h + 1]))
            dp = _dot(do_h, _with_prev(v_ref[:, sl]), _BQK)
            ds = p * (dp - _blocks(c[:, h:h + 1]))
            dsb = (ds * SCALE).astype(BF16)
            dq_ref[:, sl] = _dot(dsb, k2, _BQD).reshape(S, HD)
            dk_ref[:, sl] = _fold_prev(_dot(dsb, q, _BKD))
            dv_ref[:, sl] = _fold_prev(_dot(p, do_h, _BKD))

    return pl.pallas_call(
        body, name="dil_bwd", grid=(DIL_G,),
        in_specs=[_dil_group_spec(0), _dil_group_spec(1), _dil_group_spec(2), _dil_group_spec(0, LANES),
                  _dil_group_spec(0), _dil_group_spec(0, LANES)],
        out_specs=[_dil_group_spec(0)] * 3,
        out_shape=[jax.ShapeDtypeStruct((DIL_G, S, DIL_OUT), F32)] * 3,
        compiler_params=_cparams(("parallel",)),
    )(qkv3, qkv3, qkv3, lse3, do3, c3)


COMB_T = 256


def _dil_combine(o3, lse3):
    heads_per_lb = LANES // HD

    def body(o_ref, lse_ref, ob_ref, al_ref, *scratch):
        o_tok = [scratch[LB_PER_CB * gg:LB_PER_CB * (gg + 1)] for gg in range(DIL_G)]
        lse_tok = scratch[LB_PER_CB * DIL_G:]
        g = pl.program_id(0)
        for gs in range(DIL_G):
            @pl.when(g == gs)
            def _(gs=gs):
                for tok, sub in _residue_pieces(DIL_R[gs]):
                    for hf in range(LB_PER_CB):
                        o_tok[gs][hf][tok, :] = o_ref[sub, hf * LANES:(hf + 1) * LANES]
                    lse_tok[gs][tok, :] = lse_ref[sub, :]

        @pl.when(g == DIL_G - 1)
        def _():
            def chunk(i, carry):
                rows = pl.ds(pl.multiple_of(i * COMB_T, COMB_T), COMB_T)
                lse = [lse_tok[gg][rows, :] for gg in range(DIL_G)]
                m = jnp.maximum(jnp.maximum(lse[0], lse[1]), lse[2])
                e = [jnp.exp(lse[gg] - m) for gg in range(DIL_G)]
                den = (e[0] + e[1]) + e[2]
                al = [e[gg] / den for gg in range(DIL_G)]
                for gg in range(DIL_G):
                    al_ref[gg, rows, :] = al[gg]
                for h in range(DIL_HG):
                    hf, sl = h // heads_per_lb, slice((h % heads_per_lb) * HD, (h % heads_per_lb + 1) * HD)
                    acc = al[0][:, h:h + 1] * o_tok[0][hf][rows, sl]
                    for gg in range(1, DIL_G):
                        acc = acc + al[gg][:, h:h + 1] * o_tok[gg][hf][rows, sl]
                    ob_ref[rows, h * HD:(h + 1) * HD] = acc
                return carry

            lax.fori_loop(0, S // COMB_T, chunk, 0)

    return pl.pallas_call(
        body, name="dil_combine", grid=(DIL_G,),
        in_specs=[pl.BlockSpec((None, S, DIL_OUT), lambda g: (g, 0, 0)),
                  pl.BlockSpec((None, S, LANES), lambda g: (g, 0, 0))],
        out_specs=[pl.BlockSpec((S, DIL_OUT), lambda g: (0, 0)),
                   pl.BlockSpec((DIL_G, S, LANES), lambda g: (0, 0, 0))],
        out_shape=[jax.ShapeDtypeStruct((S, DIL_OUT), F32), jax.ShapeDtypeStruct((DIL_G, S, LANES), F32)],
        scratch_shapes=[pltpu.VMEM((S, LANES), F32)] * (DIL_G * (LB_PER_CB + 1)),
        compiler_params=_cparams(("arbitrary",)),
    )(o3, lse3)


def _dil_combine_bwd(dob, ob, alpha, deps=()):
    heads_per_lb = LANES // HD

    def body(dob_ref, ob_ref, al_ref, *refs):
        do_ref, c_ref = refs[len(deps):]
        g = pl.program_id(0)
        hf = pl.program_id(1)
        for gs in range(DIL_G):
            @pl.when(g == gs)
            def _(gs=gs):
                for tok, sub in _residue_pieces(DIL_R[gs]):
                    dob = dob_ref[tok, :]
                    prod = dob * ob_ref[tok, :]
                    al = al_ref[tok, :]
                    lane = lax.broadcasted_iota(jnp.int32, al.shape, 1)
                    c = jnp.where(hf == 0, 0.0, c_ref[sub, :])
                    for hh in range(heads_per_lb):
                        sl = slice(hh * HD, (hh + 1) * HD)
                        head = hf * heads_per_lb + hh
                        a = jnp.sum(jnp.where(lane == head, al, 0.0), axis=-1, keepdims=True)
                        do_ref[sub, sl] = (a * dob[:, sl]).astype(BF16)
                        c = jnp.where(lane == head, a * jnp.sum(prod[:, sl], axis=-1, keepdims=True), c)
                    c_ref[sub, :] = c

    half = pl.BlockSpec((S, LANES), lambda g, hf: (0, hf))
    return pl.pallas_call(
        body, name="dil_combine_bwd", grid=(DIL_G, LB_PER_CB),
        in_specs=[half, half, pl.BlockSpec((None, S, LANES), lambda g, hf: (g, 0, 0))] + [ANY] * len(deps),
        out_specs=[pl.BlockSpec((None, S, LANES), lambda g, hf: (g, 0, hf)),
                   pl.BlockSpec((None, S, LANES), lambda g, hf: (g, 0, 0))],
        out_shape=[jax.ShapeDtypeStruct((DIL_G, S, DIL_OUT), BF16), jax.ShapeDtypeStruct((DIL_G, S, LANES), F32)],
        compiler_params=_cparams(("parallel", "arbitrary")),
    )(dob, ob, alpha, *deps)


DOWN_BWD_TN = 512
DOWN_BWD_RING = 3


def _mlp_down_bwd(dx3, w_down, u):
    tn, n_steps = DOWN_BWD_TN, DFF // DOWN_BWD_TN

    def body(a_ref, b_ref, u_hbm, o_ref, ubuf, sems):
        j = pl.program_id(0)

        def fetch(s):
            slot = s % DOWN_BWD_RING
            return pltpu.make_async_copy(u_hbm.at[:, pl.ds(pl.multiple_of(s * tn, tn), tn)], ubuf.at[slot],
                                         sems.at[slot])

        @pl.when(j == 0)
        def _():
            fetch(0).start()
            fetch(1).start()

        acc = _dot(a_ref[...], b_ref[...], _NT)

        @pl.when(j + 2 < n_steps)
        def _():
            fetch(j + 2).start()

        fetch(j).wait()
        o_ref[...] = (acc * (2.0 * jnp.maximum(ubuf[j % DOWN_BWD_RING], 0.0))).astype(BF16)

    return pl.pallas_call(
        body, name="mlp_down_bwd", grid=(n_steps,),
        in_specs=[pl.BlockSpec((S, D), lambda j: (0, 0)), pl.BlockSpec((tn, D), lambda j: (j, 0)), ANY],
        out_specs=pl.BlockSpec((S, tn), lambda j: (0, j)),
        out_shape=jax.ShapeDtypeStruct((S, DFF), BF16),
        scratch_shapes=[pltpu.VMEM((DOWN_BWD_RING, S, tn), F32), pltpu.SemaphoreType.DMA((DOWN_BWD_RING,))],
        compiler_params=_cparams(("arbitrary",)),
    )(dx3, w_down, u)


def _local_step(x, tgt, h1, g_attn, g_mlp, g_final, b_pad, w_in_t, hooks):
    tables = _rope_tables()

    first = hooks.first_deps()
    proj_a = _mm(h1, w_in_t, mode="nt", tm=S, tn=512, n=3 * FOX_W, out_dtypes=[BF16], name="proj_a", deps=first)
    proj_b = _mm(h1, w_in_t, mode="nt", tm=S, tn=DIL_W, n=3 * DIL_W, b_off=SEG_B // DIL_W, out_dtypes=[F32], name="proj_b",
                deps=first)
    proj_g = _mm(h1, w_in_t, mode="nt", tm=S, tn=D, n=2 * D, b_off=SEG_G // D, out_dtypes=[F32], name="proj_g",
                deps=first)
    proj_f = _mm(h1, w_in_t, mode="nt", tm=S, tn=LANES, n=LANES, b_off=SEG_F // LANES, out_dtypes=[F32], name="proj_f",
                deps=first)

    q_aug, k_aug = _fox_prepare(proj_a, proj_f, b_pad)
    oa, lse_a = _fox_fwd(q_aug, k_aug, proj_a)

    qkv3 = _rope_split(proj_b, tables)
    o3, lse3 = _dil_fwd(qkv3)
    ob, alpha = _dil_combine(o3, lse3)

    w_a, w_b, w_out = hooks.mixer_weights(ob)
    ya, yb, mixed, x2, h2 = _mixer_tail(oa, ob, w_a, w_b, w_out, proj_g, x, g_mlp)
    w_up_sh, w_down = hooks.mlp_weights(h2)

    def up_epilogue(acc):
        r = jnp.maximum(acc, 0.0)
        return acc, r * r

    u, act = _mm(h2, w_up_sh, mode="nn", tm=S, tn=DFF // N_DEV, b_sharded=True, out_dtypes=[F32, BF16],
                 name="mlp_up", epilogue=up_epilogue)

    def loss_epilogue(acc, rows, vecs):
        dx3, dg, loss = _final_loss_rows(rows[0] + acc, vecs[0], rows[1])
        return (dx3,), (dg, loss)

    dx3, dg_final, loss = _mm_rows(act, w_down, tm=ROWS_TM, name="mlp_down_loss", epilogue=loss_epilogue,
                                   rows=(x2, tgt), vecs=(g_final,), row_out=(F32,), vec_out=(D, LANES))

    def rms_bwd_epilogue(acc, rows, vecs):
        dx, dg = _rms_bwd_rows(acc, rows[0], rows[1], vecs[0])
        return (dx,), (dg,)

    du = _mlp_down_bwd(dx3, w_down, u)
    dw_down = _mm(act, dx3, mode="tn", tm=512, tn=D, out_dtypes=[F32], name="dw_down")
    dw_up_sh = _mm(h2, du, mode="tn", tm=D, tn=DFF // N_DEV, out_sharded=True, out_dtypes=[F32], name="dw_up")
    dx2, dg_mlp = _mm_rows(du, w_up_sh, b_sharded=True, tm=ROWS_TM, name="mlp_up_bwd", epilogue=rms_bwd_epilogue,
                           rows=(x2, dx3), vecs=(g_mlp,), row_out=(F32,), vec_out=(D,),
                           deps=hooks.mlp_grads(dw_up_sh, dw_down))

    dw_out = _mm(mixed, dx2, mode="tn", tm=D, tn=D, out_dtypes=[F32], name="dw_out")

    dyab, dproj = _out_proj_gate_bwd(dx2, w_out, proj_g, ya, yb)
    doa = _mm(dyab, w_a, mode="nt", tm=S, tn=FOX_W, k=D, a_off=0, b_sharded=True, out_dtypes=[BF16],
              name="branch_a_bwd")
    dw_a = _mm(oa, dyab, mode="tn", tm=FOX_W, tn=D, n=D, b_off=0, out_sharded=True, out_dtypes=[F32],
               name="dw_branch_a")
    dob = _mm(dyab, w_b, mode="nt", tm=S, tn=CB, k=D, a_off=1, b_sharded=True, out_dtypes=[F32],
              name="branch_b_bwd")
    dw_b = _mm(ob, dyab, mode="tn", tm=DIL_OUT, tn=D, n=D, b_off=1, out_sharded=True, out_dtypes=[F32],
               name="dw_branch_b")

    dproj, dft, dfs = _fox_bwd(q_aug, k_aug, proj_a, lse_a, doa, dproj, deps=hooks.mixer_grads(dw_a, dw_b, dw_out))
    dproj, db = _fox_dscan(dft, dfs, proj_f, b_pad, dproj)

    do3, c3 = _dil_combine_bwd(dob, ob, alpha, deps=hooks.mid_backward(db))
    dq3, dk3, dv3 = _dil_bwd(qkv3, lse3, do3, c3)
    dproj = _rope_merge_bwd(dq3, dk3, dv3, tables, dproj)

    dw_in_t = _mm(dproj, h1, mode="tn", tm=DW_IN_TM, tn=D, out_dtypes=[F32], name="dw_in")
    dx, dg_attn = _mm_rows(dproj, w_in_t, tm=ROWS_TM // 2, name="proj_bwd", epilogue=rms_bwd_epilogue,
                           rows=(x, dx2), vecs=(g_attn,), row_out=(F32,), vec_out=(D,),
                           deps=hooks.w_in_grad(dw_in_t))

    return loss, dx, (dg_attn, db, dg_mlp, dg_final)


SHARD_SHAPES = ((D_IN // N_DEV, D), (FOX_W, D // N_DEV), (DIL_OUT, D // N_DEV), (D // N_DEV, D),
                (D, DFF // N_DEV), (DFF // N_DEV, D))
W_NAMES = ("w_in", "w_a", "w_b", "w_out", "w_up", "w_down")
AG_COPIES = 7
HBM = pl.BlockSpec(memory_space=pltpu.HBM)
SEM = pl.BlockSpec(memory_space=pltpu.SEMAPHORE)


def _place():
    return lax.axis_index("x"), lax.axis_index("y"), lax.axis_index("c")


W_SLAB = 752
W_WIN = 272
PAD_BLK = 256


def _pad_runs():
    runs = sorted((pad, pad + b - a, p, a) for p in range(N_DEV) for a, b, pad in _shard_pieces(p))
    blocks = []
    for k in range(D_PAD // PAD_BLK):
        lo, hi = k * PAD_BLK, (k + 1) * PAD_BLK
        blocks.append([(max(lo, r0) - lo, min(hi, r1) - lo, p, a + max(lo, r0) - r0)
                       for r0, r1, p, a in runs if max(lo, r0) < min(hi, r1)])
    return blocks


def _gather_w_in(shard, x_in, g_attn, late_shards):
    blocks = _pad_runs()

    norm_t = 512
    n_chunks = S // norm_t

    n_late = len(late_shards)
    late_shapes = [t.shape for t in late_shards]

    def body(x_ref, xin_ref, g_ref, *refs):
        late_in, (out_ref, h1_ref), late_out = refs[:n_late], refs[n_late:n_late + 2], refs[n_late + 2:2 * n_late + 2]
        scratch = refs[2 * n_late + 2:]
        stage, land, xbuf = scratch[:3]
        late_stage, late_cast = scratch[3:3 + n_late], scratch[3 + n_late:3 + 2 * n_late]
        send_sems, recv_sems, load_sem, x_sems, late_sems, store_sems = scratch[3 + 2 * n_late:]
        x, y, c = _place()
        me, sibling = (x, y, c), (x, y, 1 - c)
        chips = [(1 - x, y), (x, 1 - y), (1 - x, 1 - y)]

        def slot(px, py, pc):
            return land.at[4 * px + 2 * py + pc]

        def copy(k, block, to):
            return pltpu.make_async_remote_copy(
                src_ref=slot(*block), dst_ref=slot(*block), send_sem=send_sems.at[k], recv_sem=recv_sems.at[k],
                device_id=to, device_id_type=MESH)

        load = pltpu.make_async_copy(x_ref, stage, load_sem)
        load.start()
        load.wait()
        land[4 * x + 2 * y + c] = stage[...].astype(BF16)
        sent = [copy(0, me, sibling)] + [copy(1 + j, me, (*chip, c)) for j, chip in enumerate(chips)]
        for cp in sent:
            cp.start()

        def x_load(i):
            return pltpu.make_async_copy(xin_ref.at[pl.ds(i * norm_t, norm_t)], xbuf.at[i % 2], x_sems.at[i % 2])

        late_loads = [pltpu.make_async_copy(late_in[i], late_stage[i], late_sems.at[i]) for i in range(n_late)]
        for ld in late_loads:
            ld.start()
        x_load(0).start()
        for i in range(n_chunks):
            if i + 1 < n_chunks:
                x_load(i + 1).start()
            x_load(i).wait()
            h1_ref[i * norm_t:(i + 1) * norm_t, :] = _rms_rows(xbuf[i % 2], g_ref[...]).astype(BF16)
        stores = []
        for i in range(n_late):
            late_loads[i].wait()
            late_cast[i][...] = late_stage[i][...].astype(BF16)
            stores.append(pltpu.make_async_copy(late_cast[i], late_out[i].at[4 * x + 2 * y + c], store_sems.at[i]))
            stores[-1].start()

        for j, chip in enumerate(chips):
            copy(1 + j, (*chip, c), me).wait_recv()
            sent.append(copy(4 + j, (*chip, c), sibling))
            sent[-1].start()
        copy(0, sibling, me).wait_recv()
        for j, chip in enumerate(chips):
            copy(4 + j, (*chip, 1 - c), me).wait_recv()
        for cp in sent:
            cp.wait_send()
        for st in stores:
            st.wait()

        row = lax.broadcasted_iota(jnp.int32, (PAD_BLK, D), 0)
        for k, runs in enumerate(blocks):
            out = jnp.zeros((PAD_BLK, D), F32)
            for o_lo, o_hi, p, a in runs:
                start = min(a // 16 * 16, W_SLAB - W_WIN)
                win = land[p, start:start + W_WIN, :].astype(F32)
                moved = pltpu.roll(win, (o_lo - (a - start)) % W_WIN, 0)[:PAD_BLK]
                out = jnp.where((row >= o_lo) & (row < o_hi), moved, out)
            out_ref[k * PAD_BLK:(k + 1) * PAD_BLK, :] = out.astype(BF16)

    return pl.pallas_call(
        body, name="all_gather_w_in",
        out_shape=[jax.ShapeDtypeStruct((D_PAD, D), BF16), jax.ShapeDtypeStruct((S, D), BF16)]
                  + [jax.ShapeDtypeStruct((N_DEV,) + sh, BF16) for sh in late_shapes],
        in_specs=[ANY, ANY, pl.BlockSpec(memory_space=pltpu.VMEM)] + [ANY] * n_late,
        out_specs=[pl.BlockSpec(memory_space=pltpu.VMEM)] * 2 + [ANY] * n_late,
        scratch_shapes=[pltpu.VMEM((W_SLAB, D), F32), pltpu.VMEM((N_DEV, W_SLAB, D), BF16),
                        pltpu.VMEM((2, norm_t, D), F32)]
                       + [pltpu.VMEM(sh, F32) for sh in late_shapes] + [pltpu.VMEM(sh, BF16) for sh in late_shapes]
                       + [pltpu.SemaphoreType.DMA((AG_COPIES,)), pltpu.SemaphoreType.DMA((AG_COPIES,)),
                          pltpu.SemaphoreType.DMA(()), pltpu.SemaphoreType.DMA((2,)),
                          pltpu.SemaphoreType.DMA((n_late,)), pltpu.SemaphoreType.DMA((n_late,))],
        compiler_params=_cparams(None),
    )(shard, x_in, g_attn, *late_shards)


PEER_COPIES = N_DEV - 1
SIBLING_COPIES = 4
CHIP_COPIES = 3


def _peer_copies(_, land_refs, send_sems, recv_sems):
    x, y, c = _place()
    mine = 4 * x + 2 * y + c
    copies = []
    for i, ref in enumerate(land_refs):
        for k in range(1, N_DEV):
            peer = (x ^ (k >> 2), y ^ ((k >> 1) & 1), c ^ (k & 1))
            n = i * PEER_COPIES + k - 1
            copies.append(pltpu.make_async_remote_copy(
                src_ref=ref.at[mine], dst_ref=ref.at[mine], send_sem=send_sems.at[n], recv_sem=recv_sems.at[n],
                device_id=peer, device_id_type=MESH))
    return copies


def _sibling_copies(g_refs, r_refs, send_sems, recv_sems):
    x, y, c = _place()
    return [pltpu.make_async_remote_copy(
        src_ref=g_refs[i].at[2 * k + (1 - c)], dst_ref=r_refs[i].at[k],
        send_sem=send_sems.at[SIBLING_COPIES * i + k], recv_sem=recv_sems.at[SIBLING_COPIES * i + k],
        device_id=(x, y, 1 - c), device_id_type=MESH) for i in range(len(g_refs)) for k in range(SIBLING_COPIES)]


def _sibling_half_copies(g_refs, r_refs, send_sems, recv_sems):
    x, y, c = _place()
    return [pltpu.make_async_remote_copy(
        src_ref=g_refs[i].at[k], dst_ref=r_refs[i].at[k],
        send_sem=send_sems.at[SIBLING_COPIES * i + k], recv_sem=recv_sems.at[SIBLING_COPIES * i + k],
        device_id=(x, y, 1 - c), device_id_type=MESH) for i in range(len(g_refs)) for k in range(SIBLING_COPIES)]


def _chip_copies(p_refs, r_refs, send_sems, recv_sems):
    x, y, c = _place()
    chips = [(1 - x, y), (x, 1 - y), (1 - x, 1 - y)]
    return [pltpu.make_async_remote_copy(
        src_ref=p_refs[i].at[2 * cx + cy], dst_ref=r_refs[i].at[j],
        send_sem=send_sems.at[CHIP_COPIES * i + j], recv_sem=recv_sems.at[CHIP_COPIES * i + j],
        device_id=(cx, cy, c), device_id_type=MESH) for i in range(len(p_refs)) for j, (cx, cy) in enumerate(chips)]


def _in_hbm(a):
    return pltpu.with_memory_space_constraint(a, pltpu.HBM)


def _exchange_start(name, copies_fn, n_copies, srcs, lands, after=()):
    n_s, n_l, n_a = len(srcs), len(lands), len(after)

    def body(*refs):
        outs = refs[n_s + n_l + n_a:]
        for cp in copies_fn(refs[:n_s], refs[n_s:n_s + n_l], outs[0], outs[1]):
            cp.start()
        outs[-1][...] = jnp.zeros_like(outs[-1])

    res = pl.pallas_call(
        body, name=name,
        out_shape=[pltpu.SemaphoreType.DMA((n_copies,)), pltpu.SemaphoreType.DMA((n_copies,))]
                  + [pltpu.HBM(t.shape, t.dtype) for t in (*srcs, *lands)] + [jax.ShapeDtypeStruct((8, LANES), F32)],
        in_specs=[HBM] * (n_s + n_l) + [ANY] * n_a,
        out_specs=[SEM, SEM] + [HBM] * (n_s + n_l) + [pl.BlockSpec(memory_space=pltpu.VMEM)],
        input_output_aliases={i: 2 + i for i in range(n_s + n_l)},
        compiler_params=pltpu.CompilerParams(has_side_effects=pltpu.SideEffectType.DATAFLOW_SIDE_EFFECTING),
    )(*[_in_hbm(t) for t in (*srcs, *lands)], *after)
    return res[0], res[1], list(res[2:2 + n_s]), list(res[2 + n_s:2 + n_s + n_l]), res[-1]


def _exchange_wait(name, copies_fn, started, after):
    send_sems, recv_sems, srcs, lands, _ = started
    n_s, n_l = len(srcs), len(lands)

    def body(*refs):
        for cp in copies_fn(refs[:n_s], refs[n_s:n_s + n_l], refs[n_s + n_l], refs[n_s + n_l + 1]):
            cp.wait_send()
            cp.wait_recv()

    res = pl.pallas_call(
        body, name=name,
        out_shape=[pltpu.HBM(t.shape, t.dtype) for t in (*srcs, *lands)],
        in_specs=[HBM] * (n_s + n_l) + [SEM, SEM, ANY],
        out_specs=[HBM] * (n_s + n_l),
        input_output_aliases={i: i for i in range(n_s + n_l)},
        compiler_params=pltpu.CompilerParams(has_side_effects=pltpu.SideEffectType.DATAFLOW_SIDE_EFFECTING),
    )(*srcs, *lands, send_sems, recv_sems, after)
    return list(res[:n_s]), list(res[n_s:])


def _shard_block(shape):
    rows, cols = shape
    if rows * cols <= 256 * 1024:
        return rows, cols
    if rows % 256:
        return rows, 512
    return max(256, 256 * 1024 // cols), cols


def _chip_partial(ids, g_stack, recv1, name):
    shape = g_stack.shape[1:]
    br, bc = shape
    whole = g_stack.shape[0] == N_DEV

    def body(ids_ref, g_ref, r_ref, pb_ref, own_ref):
        s = g_ref[...] + r_ref[...]
        pb_ref[...] = s.astype(BF16)

        @pl.when(pl.program_id(2) == ids_ref[1])
        def _():
            own_ref[...] = s

    grid_spec = pltpu.PrefetchScalarGridSpec(
        num_scalar_prefetch=1, grid=(shape[0] // br, shape[1] // bc, 4),
        in_specs=[pl.BlockSpec((None, br, bc), lambda r, q, k, ids: (2 * k + ids[0] if whole else k, r, q)),
                  pl.BlockSpec((None, br, bc), lambda r, q, k, ids: (k, r, q))],
        out_specs=[pl.BlockSpec((None, br, bc), lambda r, q, k, ids: (k, r, q)),
                   pl.BlockSpec((br, bc), lambda r, q, k, ids: (r, q))])
    return pl.pallas_call(
        body, name=name, grid_spec=grid_spec,
        out_shape=[jax.ShapeDtypeStruct((4,) + shape, BF16), jax.ShapeDtypeStruct(shape, F32)],
        compiler_params=_cparams(("parallel", "parallel", "arbitrary")),
    )(ids, g_stack, recv1)


def _adamw(w, g, m, v):
    m = ADAM_B1 * m + (1.0 - ADAM_B1) * g
    v = ADAM_B2 * v + (1.0 - ADAM_B2) * (g * g)
    m_hat = m / (1.0 - ADAM_B1 ** ADAM_STEP)
    v_hat = v / (1.0 - ADAM_B2 ** ADAM_STEP)
    delta = -ADAM_LR * (m_hat / (jnp.sqrt(v_hat) + ADAM_EPS) + ADAM_WD * w)
    return delta, m, v


def _reduce_adamw(own, recv2, w, m, v, name, deps=()):
    shape = own.shape
    br, bc = _shard_block(shape)

    def body(own_ref, r_ref, w_ref, m_ref, v_ref, *refs):
        g_ref, d_ref, nm_ref, nv_ref = refs[len(deps):]
        g = own_ref[...]
        for j in range(3):
            g = g + r_ref[j].astype(F32)
        delta, nm, nv = _adamw(w_ref[...], g, m_ref[...], v_ref[...])
        g_ref[...] = g
        d_ref[...] = delta
        nm_ref[...] = nm
        nv_ref[...] = nv

    blk = pl.BlockSpec((br, bc), lambda r, q: (r, q))
    return pl.pallas_call(
        body, name=name, grid=(shape[0] // br, shape[1] // bc),
        in_specs=[blk, pl.BlockSpec((3, br, bc), lambda r, q: (0, r, q)), blk, blk, blk] + [ANY] * len(deps),
        out_specs=[blk] * 4, out_shape=[jax.ShapeDtypeStruct(shape, F32)] * 4,
        compiler_params=_cparams(("parallel", "parallel")),
    )(own, recv2, w, m, v, *deps)


def _small_adamw(stack, w, m, v):
    def body(s_ref, w_ref, m_ref, v_ref, loss_ref, *o_refs):
        g = s_ref[0]
        for s in range(1, N_DEV):
            g = g + s_ref[s]
        delta, nm, nv = _adamw(w_ref[...], g, m_ref[...], v_ref[...])
        loss_ref[...] = g[LOSS_ROW:LOSS_ROW + 1, 0:1]
        for q, t in enumerate((g, delta, nm, nv)):
            attn_ref, b_ref, mlp_ref, final_ref = o_refs[4 * q:4 * q + 4]
            for i in range(8):
                cols = slice(i * LANES, (i + 1) * LANES)
                attn_ref[:, cols] = t[i:i + 1]
                mlp_ref[:, cols] = t[8 + i:9 + i]
                final_ref[:, cols] = t[16 + i:17 + i]
            b_ref[...] = t[24:25, :FOX_H]

    vm = pl.BlockSpec(memory_space=pltpu.VMEM)
    unpacked = [jax.ShapeDtypeStruct((1, n), F32) for n in (D, FOX_H, D, D)]
    return pl.pallas_call(
        body, name="small_adamw",
        in_specs=[vm] * 4, out_specs=[vm] * 17,
        out_shape=[jax.ShapeDtypeStruct((1, 1), F32)] + unpacked * 4,
    )(stack, w, m, v)


W_IN_SEGMENTS = ((0, 3 * FOX_W, SEG_A), (3 * FOX_W, 3 * FOX_W + FOX_H, SEG_F),
                 (3 * FOX_W + FOX_H, 3 * FOX_W + FOX_H + 3 * DIL_W, SEG_B), (3 * FOX_W + FOX_H + 3 * DIL_W, D_IN, SEG_G))


def _shard_pieces(p):
    rows = D_IN // N_DEV
    lo, hi = p * rows, (p + 1) * rows
    return [(max(lo, a) - lo, min(hi, b) - lo, pad + max(lo, a) - a)
            for a, b, pad in W_IN_SEGMENTS if max(lo, a) < min(hi, b)]


def _unpad_runs(p):
    rows = D_IN // N_DEV
    blocks = []
    for lo in range(0, rows, PAD_BLK):
        hi = min(lo + PAD_BLK, rows)
        blocks.append([(max(lo, a) - lo, min(hi, b) - lo, pad + max(lo, a) - a)
                       for a, b, pad in _shard_pieces(p) if max(lo, a) < min(hi, b)])
    return blocks


def _unpad_dw_in_t(dwp, ids, kept, deps=()):
    rows = D_IN // N_DEV
    plans = [_unpad_runs(p) for p in range(N_DEV)]
    windows = [[min(src // 8 * 8, D_PAD - W_WIN) for runs in plan for _, _, src in runs] for plan in plans]
    n_win = max(len(w) for w in windows)

    def body(ids_ref, src_ref, *refs):
        o_ref, buf, sems = refs[len(deps):]
        side = ids_ref[0] if kept else 1 - ids_ref[0]
        p = 2 * pl.program_id(0) + side
        row = lax.broadcasted_iota(jnp.int32, (PAD_BLK, D), 0)

        def fetch(q):
            slot = (q // 2) % 2
            return [pltpu.make_async_copy(src_ref.at[pl.ds(start, W_WIN)], buf.at[slot, n], sems.at[slot, n])
                    for n, start in enumerate(windows[q])]

        for q in range(N_DEV):
            @pl.when(p == q)
            def _(q=q):
                for nxt in ([q, q + 2] if q < 2 else [q + 2]):
                    if nxt < N_DEV:
                        for cp in fetch(nxt):
                            cp.start()
                copies, n = fetch(q), 0
                for j, runs in enumerate(plans[q]):
                    out = None
                    for o_lo, o_hi, src in runs:
                        copies[n].wait()
                        moved = pltpu.roll(buf[(q // 2) % 2, n], (o_lo - (src - windows[q][n])) % W_WIN, 0)[:PAD_BLK]
                        out = moved if out is None else jnp.where((row >= o_lo) & (row < o_hi), moved, out)
                        n += 1
                    size = min(PAD_BLK, rows - j * PAD_BLK)
                    o_ref[j * PAD_BLK:j * PAD_BLK + size, :] = out[:size]

    grid_spec = pltpu.PrefetchScalarGridSpec(
        num_scalar_prefetch=1, grid=(N_DEV // 2,),
        in_specs=[ANY] * (1 + len(deps)), out_specs=pl.BlockSpec((None, rows, D), lambda k, ids: (k, 0, 0)),
        scratch_shapes=[pltpu.VMEM((2, n_win, W_WIN, D), F32), pltpu.SemaphoreType.DMA((2, n_win))])
    return pl.pallas_call(
        body, name="unpad_dw_in_kept" if kept else "unpad_dw_in_sent", grid_spec=grid_spec,
        out_shape=jax.ShapeDtypeStruct((N_DEV // 2, rows, D), F32),
        compiler_params=_cparams(("arbitrary",)),
    )(ids, dwp, *deps)


LOSS_ROW = 25


def _pack_small(g_attn, b, g_mlp, g_final, loss_row=None):
    tail = jnp.pad(b, ((0, 7), (0, LANES - b.shape[1])))
    if loss_row is not None:
        tail = tail + jnp.pad(loss_row, ((LOSS_ROW - 24, 31 - LOSS_ROW), (0, 0)))
    return jnp.concatenate([g_attn.reshape(8, LANES), g_mlp.reshape(8, LANES), g_final.reshape(8, LANES), tail], axis=0)


class _StepComm:
    def __init__(self, ids, w_sh, m_sh, v_sh, lands, after_w_in):
        self.ids = ids
        self.w_sh, self.m_sh, self.v_sh = w_sh, m_sh, v_sh
        self.updates = None
        self.gather_mixer = _exchange_start("gather_mixer_start", _peer_copies, PEER_COPIES * 3, [], lands[:3],
                                            after=(after_w_in,))
        self.gather_mlp = _exchange_start("gather_mlp_start", _peer_copies, PEER_COPIES * 2, [], lands[3:],
                                          after=(self.gather_mixer[-1],))
        self.sibling = {}
        self.chips = {}
        self.own = {}
        self.names = {}
        self.copies_fn = {}
        self.kept = {}

    def _reduce_start(self, group, names, grads, after=(), copies_fn=_sibling_copies):
        lands = [lax.empty((SIBLING_COPIES,) + t.shape[1:], F32) for t in grads]
        self.sibling[group] = _exchange_start("grad_%s_sibling_start" % group, copies_fn,
                                              SIBLING_COPIES * len(grads), grads, lands, after=after)
        self.names[group] = names
        self.copies_fn[group] = copies_fn
        return self.sibling[group][-1]

    def _reduce_mid(self, group, after):
        grads, recv1 = _exchange_wait("grad_%s_sibling_wait" % group, self.copies_fn[group], self.sibling[group],
                                      after)
        grads = self.kept.get(group, grads)
        parts = [_chip_partial(self.ids, g, r, "grad_partial_" + n) for g, r, n in zip(grads, recv1, self.names[group])]
        self.own[group] = [p[1] for p in parts]
        srcs = [p[0] for p in parts]
        lands = [lax.empty((CHIP_COPIES,) + t.shape[1:], BF16) for t in srcs]
        self.chips[group] = _exchange_start("grad_%s_chips_start" % group, _chip_copies, CHIP_COPIES * len(srcs),
                                            srcs, lands)
        return self.chips[group][-1]

    def reduced(self, group, after):
        _, recv2 = _exchange_wait("grad_%s_chips_wait" % group, _chip_copies, self.chips[group], after)
        return list(zip(self.own[group], recv2))

    def first_deps(self):
        return [self.gather_mlp[-1]]

    def mixer_weights(self, after):
        _, (g_a, g_b, g_out) = _exchange_wait("gather_mixer_wait", _peer_copies, self.gather_mixer, after)
        return g_a, g_b, g_out.reshape(D, D)

    def mlp_weights(self, after):
        _, (g_up, g_down) = _exchange_wait("gather_mlp_wait", _peer_copies, self.gather_mlp, after)
        return g_up, g_down.reshape(DFF, D)

    def mlp_grads(self, dw_up_sh, dw_down):
        return [self._reduce_start("mlp", W_NAMES[4:], [dw_up_sh, dw_down.reshape((N_DEV,) + SHARD_SHAPES[5])])]

    def mixer_grads(self, dw_a, dw_b, dw_out):
        token = self._reduce_mid("mlp", dw_b)
        grads = [dw_a, dw_b, dw_out.reshape((N_DEV,) + SHARD_SHAPES[3])]
        return [self._reduce_start("mixer", W_NAMES[1:4], grads, after=(token,))]

    def mid_backward(self, after):
        return [self._reduce_mid("mixer", after)]

    def w_in_grad(self, dw_in_t):
        token = self._reduce_start("w_in", W_NAMES[:1], [_unpad_dw_in_t(dw_in_t, self.ids, kept=False)],
                                   copies_fn=_sibling_half_copies)
        self.kept["w_in"] = [_unpad_dw_in_t(dw_in_t, self.ids, kept=True, deps=[token])]
        reduced = self.reduced("mixer", token) + self.reduced("mlp", token)
        self.updates, last = [None] * len(reduced), self.kept["w_in"][0]
        for i in (3, 4, 0, 1, 2):
            if i == 4:
                last = self._reduce_mid("w_in", last)
            self.updates[i] = _reduce_adamw(*reduced[i], self.w_sh[1 + i], self.m_sh[1 + i], self.v_sh[1 + i],
                                            "adamw_" + W_NAMES[1 + i], deps=[last])
            last = self.updates[i][0]
        return [last]

    def w_in_update(self, after):
        (reduced,) = self.reduced("w_in", after)
        return _reduce_adamw(*reduced, self.w_sh[0], self.m_sh[0], self.v_sh[0], "adamw_" + W_NAMES[0])


def kernel(x, norm_attn_g, w_in, b_forget, w_branch_a, w_branch_b, w_out, norm_mlp_g, w_up, w_down, norm_final_g, loss_target, m_norm_attn_g, m_w_in, m_b_forget, m_w_branch_a, m_w_branch_b, m_w_out, m_norm_mlp_g, m_w_up, m_w_down, m_norm_final_g, v_norm_attn_g, v_w_in, v_b_forget, v_w_branch_a, v_w_branch_b, v_w_out, v_norm_mlp_g, v_w_up, v_w_down, v_norm_final_g):
    cx, cy, cc = _place()
    ids = jnp.stack([cc, 2 * cx + cy]).astype(jnp.int32)

    w_sh = [w_in[0].T] + [t[0] for t in (w_branch_a, w_branch_b, w_out, w_up, w_down)]
    m_sh = [m_w_in[0].T] + [t[0] for t in (m_w_branch_a, m_w_branch_b, m_w_out, m_w_up, m_w_down)]
    v_sh = [v_w_in[0].T] + [t[0] for t in (v_w_branch_a, v_w_branch_b, v_w_out, v_w_up, v_w_down)]

    w_in_t, h1, *lands = _gather_w_in(jnp.pad(w_sh[0], ((0, W_SLAB - w_sh[0].shape[0]), (0, 0))), x[0], norm_attn_g,
                                      w_sh[1:])
    comm = _StepComm(ids, w_sh, m_sh, v_sh, lands, w_in_t)
    b_pad = jnp.pad(b_forget, ((0, 0), (0, LANES - FOX_H)))

    loss_row, dx, dsmall = _local_step(
        x[0], loss_target[0], h1, norm_attn_g, norm_mlp_g, norm_final_g.reshape(1, D), b_pad, w_in_t, comm)

    dg_attn, db, dg_mlp, dg_final = dsmall
    stack = lax.dynamic_update_slice(jnp.zeros((N_DEV, SMALL_R, LANES), F32),
                                     _pack_small(dg_attn, db, dg_mlp, dg_final, loss_row)[None],
                                     (4 * cx + 2 * cy + cc, 0, 0))
    small_gather = _exchange_start("small_gather_start", _peer_copies, PEER_COPIES, [], [stack])
    w_in_update = comm.w_in_update(small_gather[-1])
    _, (stack,) = _exchange_wait("small_gather_wait", _peer_copies, small_gather, w_in_update[0])
    loss, *small = _small_adamw(
        stack,
        _pack_small(norm_attn_g, b_forget, norm_mlp_g, norm_final_g.reshape(1, D)),
        _pack_small(m_norm_attn_g, m_b_forget, m_norm_mlp_g, m_norm_final_g.reshape(1, D)),
        _pack_small(v_norm_attn_g, v_b_forget, v_norm_mlp_g, v_norm_final_g.reshape(1, D)))
    big = [w_in_update] + comm.updates

    outs = [loss.reshape(()), dx[None]]
    for q in range(4):
        s_attn, s_b, s_mlp, s_final = small[4 * q:4 * q + 4]
        s_final = s_final.reshape(D)
        b_in, b_a, b_b, b_out, b_up, b_down = [t[q][None] for t in big]
        b_in = jnp.swapaxes(b_in, 1, 2)
        outs += [s_attn, b_in, s_b, b_a, b_b, b_out, s_mlp, b_up, b_down, s_final]
    return tuple(outs)
```
